```python
import jax, jax.numpy as jnp
from jax import lax
import numpy as np

D_MODEL = 1024
BATCH = 8
SEQ = 4096
DEPTH = 1

CHUNK = 64
N_PREV_CHUNKS = 8
BAND_CHUNKS = N_PREV_CHUNKS + 1
N_HEADS = 16
HEAD_DIM = 64
D_ATTN = N_HEADS * HEAD_DIM
D_CONV = D_MODEL
CONV_WIDTH = 3
MAX_REL = 256
D_FF = 4 * D_MODEL
N_BRANCHES = 2
EPS = 1e-6
NEG_INF = -1e30

kernel_name = "chunk_causal_hybrid_attn_shortconv_block"


def rms_norm(x, g):
    xf = x.astype(jnp.float32)
    y = xf * lax.rsqrt(jnp.mean(xf * xf, axis=-1, keepdims=True) + EPS)
    return (y * g.astype(jnp.float32)).astype(x.dtype)


def chunk_band(t):
    b, nc, c, h, dh = t.shape
    tp = jnp.pad(t, ((0, 0), (N_PREV_CHUNKS, 0), (0, 0), (0, 0), (0, 0)))
    band = jnp.stack([tp[:, o:o + nc] for o in range(BAND_CHUNKS)], axis=2)
    return band.reshape(b, nc, BAND_CHUNKS * c, h, dh)


def chunked_relpos_attention(q, k, v, q_norm_g, k_norm_g, rel_bias):
    b, s, _ = q.shape
    nc = s // CHUNK
    kw = BAND_CHUNKS * CHUNK
    q = rms_norm(q.reshape(b, nc, CHUNK, N_HEADS, HEAD_DIM), q_norm_g)
    k = rms_norm(k.reshape(b, nc, CHUNK, N_HEADS, HEAD_DIM), k_norm_g)
    v = v.reshape(b, nc, CHUNK, N_HEADS, HEAD_DIM)
    kb = chunk_band(k)
    vb = chunk_band(v)

    q_idx = jnp.arange(CHUNK)[:, None]
    k_idx = jnp.arange(kw)[None, :]
    dist = q_idx - k_idx + N_PREV_CHUNKS * CHUNK
    rel_idx = jnp.clip(dist, -MAX_REL, MAX_REL) + MAX_REL
    bias = rel_bias[:, rel_idx].astype(jnp.float32)

    key_chunk = jnp.arange(nc)[:, None] + (jnp.arange(kw) // CHUNK)[None, :] - N_PREV_CHUNKS
    valid = key_chunk >= 0

    scale = HEAD_DIM ** -0.5
    scores = jnp.einsum('bnqhd,bnkhd->bnhqk', q, kb).astype(jnp.float32) * scale
    scores = scores + bias[None, None]
    scores = jnp.where(valid[None, :, None, None, :], scores, NEG_INF)
    probs = jax.nn.softmax(scores, axis=-1).astype(vb.dtype)
    out = jnp.einsum('bnhqk,bnkhd->bnqhd', probs, vb)
    return out.reshape(b, s, D_ATTN)


def gated_short_conv(bg, cg, xc, conv_w, conv_b):
    s = xc.shape[1]
    u = cg * xc
    up = jnp.pad(u, ((0, 0), (CONV_WIDTH - 1, 0), (0, 0)))
    conv = conv_b + sum(conv_w[j] * up[:, j:j + s] for j in range(CONV_WIDTH))
    return bg * conv


def _fwd_setup_inputs(seed: int = 0) -> dict:
    key = jax.random.key(seed)
    ks = jax.random.split(key, 20)
    f32 = jnp.float32
    d_in = 3 * D_ATTN + 3 * D_CONV
    return {
        "x": jax.random.normal(ks[0], (BATCH, SEQ, D_MODEL), f32),
        "norm1_g": 1.0 + 0.05 * jax.random.normal(ks[1], (D_MODEL,), f32),
        "w_in": jax.random.normal(ks[2], (D_MODEL, d_in), f32) * D_MODEL ** -0.5,
        "q_norm_g": 1.0 + 0.05 * jax.random.normal(ks[3], (HEAD_DIM,), f32),
        "k_norm_g": 1.0 + 0.05 * jax.random.normal(ks[4], (HEAD_DIM,), f32),
        "rel_bias": 0.5 * jax.random.normal(ks[5], (N_HEADS, 2 * MAX_REL + 1), f32),
        "conv_w": jax.random.normal(ks[6], (CONV_WIDTH, D_CONV), f32) * CONV_WIDTH ** -0.5,
        "conv_b": 0.02 * jax.random.normal(ks[7], (D_CONV,), f32),
        "w_attn_proj": jax.random.normal(ks[8], (D_ATTN, D_MODEL), f32) * D_ATTN ** -0.5,
        "w_conv_proj": jax.random.normal(ks[9], (D_CONV, D_MODEL), f32) * D_CONV ** -0.5,
        "w_gate": jax.random.normal(ks[10], (D_MODEL, N_BRANCHES * D_MODEL), f32) * D_MODEL ** -0.5,
        "b_gate": 0.02 * jax.random.normal(ks[11], (N_BRANCHES * D_MODEL,), f32),
        "w_out": jax.random.normal(ks[12], (D_MODEL, D_MODEL), f32) * D_MODEL ** -0.5,
        "norm2_g": 1.0 + 0.05 * jax.random.normal(ks[13], (D_MODEL,), f32),
        "w_up": jax.random.normal(ks[14], (D_MODEL, D_FF), f32) * D_MODEL ** -0.5,
        "w_down": jax.random.normal(ks[15], (D_FF, D_MODEL), f32) * D_FF ** -0.5,
    }


def _fwd_reference(x, norm1_g, w_in, q_norm_g, k_norm_g, rel_bias, conv_w, conv_b,
              w_attn_proj, w_conv_proj, w_gate, b_gate, w_out, norm2_g, w_up, w_down):
    for _ in range(DEPTH):
        h = rms_norm(x, norm1_g)
        proj = jnp.einsum('bsd,de->bse', h, w_in)
        q, k, v, bg, cg, xc = jnp.split(
            proj,
            [D_ATTN, 2 * D_ATTN, 3 * D_ATTN,
             3 * D_ATTN + D_CONV, 3 * D_ATTN + 2 * D_CONV],
            axis=-1)

        y_attn = chunked_relpos_attention(q, k, v, q_norm_g, k_norm_g, rel_bias)
        y_conv = gated_short_conv(bg, cg, xc, conv_w, conv_b)

        y_attn = jnp.einsum('bse,ed->bsd', y_attn, w_attn_proj)
        y_conv = jnp.einsum('bse,ed->bsd', y_conv, w_conv_proj)

        gates = jax.nn.sigmoid(jnp.einsum('bsd,de->bse', h, w_gate) + b_gate)
        g_attn, g_conv = jnp.split(gates, 2, axis=-1)
        merged = g_attn * y_attn + g_conv * y_conv
        x = x + jnp.einsum('bsd,de->bse', merged, w_out)

        h2 = rms_norm(x, norm2_g)
        u = jnp.square(jax.nn.relu(jnp.einsum('bsd,df->bsf', h2, w_up)))
        x = x + jnp.einsum('bsf,fd->bsd', u, w_down)
    return x


import jax as _jax
import jax.numpy as _jnp

TWIN_FORMAT = 'train_step'
FWD_PARAMS = ['x', 'norm1_g', 'w_in', 'q_norm_g', 'k_norm_g', 'rel_bias', 'conv_w', 'conv_b', 'w_attn_proj', 'w_conv_proj', 'w_gate', 'b_gate', 'w_out', 'norm2_g', 'w_up', 'w_down']
TWIN_WEIGHTS = ['norm1_g', 'w_in', 'q_norm_g', 'k_norm_g', 'rel_bias', 'conv_w', 'conv_b', 'w_attn_proj', 'w_conv_proj', 'w_gate', 'b_gate', 'w_out', 'norm2_g', 'w_up', 'w_down']
TWIN_DIFF_INPUT = 'x'
TWIN_INPUTS = ['x', 'norm1_g', 'w_in', 'q_norm_g', 'k_norm_g', 'rel_bias', 'conv_w', 'conv_b', 'w_attn_proj', 'w_conv_proj', 'w_gate', 'b_gate', 'w_out', 'norm2_g', 'w_up', 'w_down', 'loss_target', 'm_norm1_g', 'm_w_in', 'm_q_norm_g', 'm_k_norm_g', 'm_rel_bias', 'm_conv_w', 'm_conv_b', 'm_w_attn_proj', 'm_w_conv_proj', 'm_w_gate', 'm_b_gate', 'm_w_out', 'm_norm2_g', 'm_w_up', 'm_w_down', 'v_norm1_g', 'v_w_in', 'v_q_norm_g', 'v_k_norm_g', 'v_rel_bias', 'v_conv_w', 'v_conv_b', 'v_w_attn_proj', 'v_w_conv_proj', 'v_w_gate', 'v_b_gate', 'v_w_out', 'v_norm2_g', 'v_w_up', 'v_w_down']
TWIN_OUTPUTS = ['loss', 'grad_x', 'grad_norm1_g', 'grad_w_in', 'grad_q_norm_g', 'grad_k_norm_g', 'grad_rel_bias', 'grad_conv_w', 'grad_conv_b', 'grad_w_attn_proj', 'grad_w_conv_proj', 'grad_w_gate', 'grad_b_gate', 'grad_w_out', 'grad_norm2_g', 'grad_w_up', 'grad_w_down', 'delta_norm1_g', 'delta_w_in', 'delta_q_norm_g', 'delta_k_norm_g', 'delta_rel_bias', 'delta_conv_w', 'delta_conv_b', 'delta_w_attn_proj', 'delta_w_conv_proj', 'delta_w_gate', 'delta_b_gate', 'delta_w_out', 'delta_norm2_g', 'delta_w_up', 'delta_w_down', 'new_m_norm1_g', 'new_m_w_in', 'new_m_q_norm_g', 'new_m_k_norm_g', 'new_m_rel_bias', 'new_m_conv_w', 'new_m_conv_b', 'new_m_w_attn_proj', 'new_m_w_conv_proj', 'new_m_w_gate', 'new_m_b_gate', 'new_m_w_out', 'new_m_norm2_g', 'new_m_w_up', 'new_m_w_down', 'new_v_norm1_g', 'new_v_w_in', 'new_v_q_norm_g', 'new_v_k_norm_g', 'new_v_rel_bias', 'new_v_conv_w', 'new_v_conv_b', 'new_v_w_attn_proj', 'new_v_w_conv_proj', 'new_v_w_gate', 'new_v_b_gate', 'new_v_w_out', 'new_v_norm2_g', 'new_v_w_up', 'new_v_w_down']
TWIN_LEAF_KINDS = {'loss': 'loss', 'grad_x': 'grad_x', 'grad_norm1_g': 'grad_w', 'grad_w_in': 'grad_w', 'grad_q_norm_g': 'grad_w', 'grad_k_norm_g': 'grad_w', 'grad_rel_bias': 'grad_w', 'grad_conv_w': 'grad_w', 'grad_conv_b': 'grad_w', 'grad_w_attn_proj': 'grad_w', 'grad_w_conv_proj': 'grad_w', 'grad_w_gate': 'grad_w', 'grad_b_gate': 'grad_w', 'grad_w_out': 'grad_w', 'grad_norm2_g': 'grad_w', 'grad_w_up': 'grad_w', 'grad_w_down': 'grad_w', 'delta_norm1_g': 'delta_w', 'delta_w_in': 'delta_w', 'delta_q_norm_g': 'delta_w', 'delta_k_norm_g': 'delta_w', 'delta_rel_bias': 'delta_w', 'delta_conv_w': 'delta_w', 'delta_conv_b': 'delta_w', 'delta_w_attn_proj': 'delta_w', 'delta_w_conv_proj': 'delta_w', 'delta_w_gate': 'delta_w', 'delta_b_gate': 'delta_w', 'delta_w_out': 'delta_w', 'delta_norm2_g': 'delta_w', 'delta_w_up': 'delta_w', 'delta_w_down': 'delta_w', 'new_m_norm1_g': 'new_m', 'new_m_w_in': 'new_m', 'new_m_q_norm_g': 'new_m', 'new_m_k_norm_g': 'new_m', 'new_m_rel_bias': 'new_m', 'new_m_conv_w': 'new_m', 'new_m_conv_b': 'new_m', 'new_m_w_attn_proj': 'new_m', 'new_m_w_conv_proj': 'new_m', 'new_m_w_gate': 'new_m', 'new_m_b_gate': 'new_m', 'new_m_w_out': 'new_m', 'new_m_norm2_g': 'new_m', 'new_m_w_up': 'new_m', 'new_m_w_down': 'new_m', 'new_v_norm1_g': 'new_v', 'new_v_w_in': 'new_v', 'new_v_q_norm_g': 'new_v', 'new_v_k_norm_g': 'new_v', 'new_v_rel_bias': 'new_v', 'new_v_conv_w': 'new_v', 'new_v_conv_b': 'new_v', 'new_v_w_attn_proj': 'new_v', 'new_v_w_conv_proj': 'new_v', 'new_v_w_gate': 'new_v', 'new_v_b_gate': 'new_v', 'new_v_w_out': 'new_v', 'new_v_norm2_g': 'new_v', 'new_v_w_up': 'new_v', 'new_v_w_down': 'new_v'}


def _forward(args):
    return _fwd_reference(*[args[k] for k in FWD_PARAMS])


def _output_shape():
    def fwd():
        inp = _fwd_setup_inputs(0)
        return _fwd_reference(*[inp[k] for k in FWD_PARAMS])
    out = _jax.eval_shape(fwd)
    return out.shape, out.dtype

N_MICROBATCH = 1
ADAM_LR = 0.001
ADAM_B1 = 0.9
ADAM_B2 = 0.999
ADAM_EPS = 1e-08
ADAM_WD = 0.01
ADAM_STEP = 10
PER_EXAMPLE_BATCH_AXIS = {'x': 0, 'loss_target': 0}
SHARED_INPUTS = []
_WEIGHT_DTYPES = {'norm1_g': _jnp.float32, 'w_in': _jnp.float32, 'q_norm_g': _jnp.float32, 'k_norm_g': _jnp.float32, 'rel_bias': _jnp.float32, 'conv_w': _jnp.float32, 'conv_b': _jnp.float32, 'w_attn_proj': _jnp.float32, 'w_conv_proj': _jnp.float32, 'w_gate': _jnp.float32, 'b_gate': _jnp.float32, 'w_out': _jnp.float32, 'norm2_g': _jnp.float32, 'w_up': _jnp.float32, 'w_down': _jnp.float32}
MOMENT_SCALE = {'norm1_g': 2.866104e+01, 'w_in': 3.504690e-01, 'q_norm_g': 7.870685e-01, 'k_norm_g': 7.874707e-01, 'rel_bias': 2.134449e-02, 'conv_w': 5.280174e+00, 'conv_b': 9.420745e-01, 'w_attn_proj': 1.191013e-01, 'w_conv_proj': 6.463983e-01, 'w_gate': 1.085391e-01, 'b_gate': 2.302163e+00, 'w_out': 7.357617e-01, 'norm2_g': 9.679582e+01, 'w_up': 8.916968e-01, 'w_down': 8.132049e+00}


def _to_microbatches(a, axis):
    t = _jnp.moveaxis(a, axis, 0)
    t = t.reshape((N_MICROBATCH, t.shape[0] // N_MICROBATCH) + t.shape[1:])
    return _jnp.moveaxis(t, 1, axis + 1)


def setup_inputs(seed: int = 0) -> dict:
    inp = _fwd_setup_inputs(seed)
    key = _jax.random.fold_in(_jax.random.key(seed), 7919)
    shape, _ = _output_shape()
    out = dict(inp)
    out["loss_target"] = _jax.random.normal(_jax.random.fold_in(key, 0), shape, _jnp.float32)
    for i, name in enumerate(TWIN_WEIGHTS):
        w = inp[name].astype(_jnp.float32)
        if MOMENT_SCALE is None:
            s = _jnp.sqrt(_jnp.mean(_jnp.square(w)) + 1e-30)
        else:
            s = MOMENT_SCALE[name]
        km, kv = _jax.random.split(_jax.random.fold_in(key, i + 1))
        out[name] = w
        out["m_" + name] = s * _jax.random.normal(km, w.shape, _jnp.float32)
        out["v_" + name] = (s * s) * _jax.random.uniform(kv, w.shape, _jnp.float32, 0.5, 1.5)
    if N_MICROBATCH > 1:
        for name, axis in PER_EXAMPLE_BATCH_AXIS.items():
            out[name] = _to_microbatches(out[name], axis)
    return {'x': out['x'], 'norm1_g': out['norm1_g'], 'w_in': out['w_in'], 'q_norm_g': out['q_norm_g'], 'k_norm_g': out['k_norm_g'], 'rel_bias': out['rel_bias'], 'conv_w': out['conv_w'], 'conv_b': out['conv_b'], 'w_attn_proj': out['w_attn_proj'], 'w_conv_proj': out['w_conv_proj'], 'w_gate': out['w_gate'], 'b_gate': out['b_gate'], 'w_out': out['w_out'], 'norm2_g': out['norm2_g'], 'w_up': out['w_up'], 'w_down': out['w_down'], 'loss_target': out['loss_target'], 'm_norm1_g': out['m_norm1_g'], 'm_w_in': out['m_w_in'], 'm_q_norm_g': out['m_q_norm_g'], 'm_k_norm_g': out['m_k_norm_g'], 'm_rel_bias': out['m_rel_bias'], 'm_conv_w': out['m_conv_w'], 'm_conv_b': out['m_conv_b'], 'm_w_attn_proj': out['m_w_attn_proj'], 'm_w_conv_proj': out['m_w_conv_proj'], 'm_w_gate': out['m_w_gate'], 'm_b_gate': out['m_b_gate'], 'm_w_out': out['m_w_out'], 'm_norm2_g': out['m_norm2_g'], 'm_w_up': out['m_w_up'], 'm_w_down': out['m_w_down'], 'v_norm1_g': out['v_norm1_g'], 'v_w_in': out['v_w_in'], 'v_q_norm_g': out['v_q_norm_g'], 'v_k_norm_g': out['v_k_norm_g'], 'v_rel_bias': out['v_rel_bias'], 'v_conv_w': out['v_conv_w'], 'v_conv_b': out['v_conv_b'], 'v_w_attn_proj': out['v_w_attn_proj'], 'v_w_conv_proj': out['v_w_conv_proj'], 'v_w_gate': out['v_w_gate'], 'v_b_gate': out['v_b_gate'], 'v_w_out': out['v_w_out'], 'v_norm2_g': out['v_norm2_g'], 'v_w_up': out['v_w_up'], 'v_w_down': out['v_w_down']}


def _loss(weights, diff, rest, loss_target):
    with _jax.named_scope("forward"):
        args = {**rest, TWIN_DIFF_INPUT: diff, **{k: w.astype(_WEIGHT_DTYPES[k]) for k, w in weights.items()}}
        y = _forward(args)
    with _jax.named_scope("loss_head"):
        err = _jnp.square(y.astype(_jnp.float32) - loss_target)
        return 0.5 * _jnp.sum(_jnp.mean(err, axis=-1)) if err.ndim else 0.5 * err


def _adamw(w, g, m, v):
    m = ADAM_B1 * m + (1.0 - ADAM_B1) * g
    v = ADAM_B2 * v + (1.0 - ADAM_B2) * _jnp.square(g)
    m_hat = m / (1.0 - ADAM_B1 ** ADAM_STEP)
    v_hat = v / (1.0 - ADAM_B2 ** ADAM_STEP)
    delta = -ADAM_LR * (m_hat / (_jnp.sqrt(v_hat) + ADAM_EPS) + ADAM_WD * w)
    return delta, m, v


def reference(x, norm1_g, w_in, q_norm_g, k_norm_g, rel_bias, conv_w, conv_b, w_attn_proj, w_conv_proj, w_gate, b_gate, w_out, norm2_g, w_up, w_down, loss_target, m_norm1_g, m_w_in, m_q_norm_g, m_k_norm_g, m_rel_bias, m_conv_w, m_conv_b, m_w_attn_proj, m_w_conv_proj, m_w_gate, m_b_gate, m_w_out, m_norm2_g, m_w_up, m_w_down, v_norm1_g, v_w_in, v_q_norm_g, v_k_norm_g, v_rel_bias, v_conv_w, v_conv_b, v_w_attn_proj, v_w_conv_proj, v_w_gate, v_b_gate, v_w_out, v_norm2_g, v_w_up, v_w_down):
    given = dict(x=x, norm1_g=norm1_g, w_in=w_in, q_norm_g=q_norm_g, k_norm_g=k_norm_g, rel_bias=rel_bias, conv_w=conv_w, conv_b=conv_b, w_attn_proj=w_attn_proj, w_conv_proj=w_conv_proj, w_gate=w_gate, b_gate=b_gate, w_out=w_out, norm2_g=norm2_g, w_up=w_up, w_down=w_down, loss_target=loss_target, m_norm1_g=m_norm1_g, m_w_in=m_w_in, m_q_norm_g=m_q_norm_g, m_k_norm_g=m_k_norm_g, m_rel_bias=m_rel_bias, m_conv_w=m_conv_w, m_conv_b=m_conv_b, m_w_attn_proj=m_w_attn_proj, m_w_conv_proj=m_w_conv_proj, m_w_gate=m_w_gate, m_b_gate=m_b_gate, m_w_out=m_w_out, m_norm2_g=m_norm2_g, m_w_up=m_w_up, m_w_down=m_w_down, v_norm1_g=v_norm1_g, v_w_in=v_w_in, v_q_norm_g=v_q_norm_g, v_k_norm_g=v_k_norm_g, v_rel_bias=v_rel_bias, v_conv_w=v_conv_w, v_conv_b=v_conv_b, v_w_attn_proj=v_w_attn_proj, v_w_conv_proj=v_w_conv_proj, v_w_gate=v_w_gate, v_b_gate=v_b_gate, v_w_out=v_w_out, v_norm2_g=v_norm2_g, v_w_up=v_w_up, v_w_down=v_w_down)
    weights = {n: given[n] for n in TWIN_WEIGHTS}
    shared = {n: given[n] for n in SHARED_INPUTS}
    per_example = {n: given[n] for n in ['x']}
    grad_fn = _jax.value_and_grad(_loss, argnums=(0, 1))

    def one_microbatch(ex, loss_target):
        ex = dict(ex)
        diff = ex.pop(TWIN_DIFF_INPUT)
        return grad_fn(weights, diff, {**shared, **ex}, loss_target)

    if N_MICROBATCH == 1:
        loss, (grad_w, grad_x) = one_microbatch(per_example, given["loss_target"])
    else:
        def body(carry, xs):
            loss_sum, grad_sum = carry
            l_k, (gw_k, gx_k) = one_microbatch(xs[0], xs[1])
            with _jax.named_scope("update"):
                return (loss_sum + l_k, _jax.tree.map(_jnp.add, grad_sum, gw_k)), gx_k

        init = (_jnp.zeros((), _jnp.float32), _jax.tree.map(_jnp.zeros_like, weights))
        (loss, grad_w), grad_x = _jax.lax.scan(body, init, (per_example, given["loss_target"]))
    with _jax.named_scope("update"):
        delta_w, new_m, new_v = {}, {}, {}
        for n in TWIN_WEIGHTS:
            delta_w[n], new_m[n], new_v[n] = _adamw(weights[n], grad_w[n], given["m_" + n], given["v_" + n])
    return (loss, grad_x, *[grad_w[n] for n in TWIN_WEIGHTS], *[delta_w[n] for n in TWIN_WEIGHTS],
            *[new_m[n] for n in TWIN_WEIGHTS], *[new_v[n] for n in TWIN_WEIGHTS])
```

```python
import functools

import jax
import jax.numpy as jnp
from jax import lax
from jax.experimental import pallas as pl
from jax.experimental.pallas import tpu as pltpu

F32 = jnp.float32
BF16 = jnp.bfloat16

D_MODEL = 1024
N_HEADS = 16
HEAD_DIM = 64
CHUNK = 64
N_PREV_CHUNKS = 8
MAX_REL = 256
D_FF = 4096
N_REL = 2 * MAX_REL + 1
REL_PAD = 640
EPS = 1e-6
NEG_INF = -1e30
QK_SCALE = HEAD_DIM ** -0.5

SUPER = 4 * CHUNK
BAND = SUPER + N_PREV_CHUNKS * CHUNK
SKEW_W = 1024
N_SHARD = 4
LANE = 128
VMEM_LIMIT = 48 * 1024 * 1024

ADAM_LR = 0.001
ADAM_B1 = 0.9
ADAM_B2 = 0.999
ADAM_EPS = 1e-08
ADAM_WD = 0.01
ADAM_STEP = 10

MESH = pl.DeviceIdType.MESH
NN = (((1,), (0,)), ((), ()))
NT = (((1,), (1,)), ((), ()))
TN = (((0,), (0,)), ((), ()))


def _params(*sem):
    return pltpu.CompilerParams(dimension_semantics=sem or None, vmem_limit_bytes=VMEM_LIMIT)


def _mm(name, dims, a, a_spec, b, b_spec, grid, tile, outs, epilogue=None, extras=()):
    nk, ne, no = grid[2], len(extras), len(outs)

    def body(a_ref, b_ref, *refs):
        e_refs, o_refs = refs[:ne], refs[ne:ne + no]
        part = lax.dot_general(a_ref[...], b_ref[...], dims, preferred_element_type=F32)

        def finish(acc):
            if epilogue is None:
                o_refs[0][...] = acc.astype(o_refs[0].dtype)
            else:
                epilogue(acc, [r[...] for r in e_refs], o_refs)

        if nk == 1:
            finish(part)
        else:
            acc_ref = refs[ne + no]
            k = pl.program_id(2)

            @pl.when(k == 0)
            def _():
                acc_ref[...] = part

            @pl.when(k > 0)
            def _():
                acc_ref[...] += part

            @pl.when(k == nk - 1)
            def _():
                finish(acc_ref[...])

    res = pl.pallas_call(
        body, name=name, grid=grid,
        in_specs=[a_spec, b_spec] + [s for _, s in extras],
        out_specs=[s for _, s in outs],
        out_shape=[s for s, _ in outs],
        scratch_shapes=[pltpu.VMEM(tile, F32)] if nk > 1 else [],
        compiler_params=_params("parallel", "parallel", "arbitrary"),
    )(a, b, *[e for e, _ in extras])
    return res[0] if no == 1 else res


def _tile_spec(tm, tn, col0=0):
    return pl.BlockSpec((tm, tn), lambda i, j, k: (i, j + col0))


def _out2d(m, n, dtype, tm, tn):
    return (jax.ShapeDtypeStruct((m, n), dtype), _tile_spec(tm, tn))


def _mm_fwd(name, a, w, tm, tn, tk, outs=None, epilogue=None, extras=()):
    m, kdim = a.shape
    if w.ndim == 3:
        per = w.shape[2] // tn
        n = N_SHARD * w.shape[2]
        w_spec = pl.BlockSpec((None, tk, tn), lambda i, j, k: (j // per, k, j % per))
    else:
        n = w.shape[1]
        w_spec = pl.BlockSpec((tk, tn), lambda i, j, k: (k, j))
    if outs is None:
        outs = [_out2d(m, n, F32, tm, tn)]
    return _mm(name, NN, a, pl.BlockSpec((tm, tk), lambda i, j, k: (i, k)), w, w_spec,
               (m // tm, n // tn, kdim // tk), (tm, tn), outs, epilogue, extras)


def _mm_bwd_x(name, g, g_spec, w, tm, tj, tc, m, n_contract, outs=None, epilogue=None, extras=()):
    if w.ndim == 3:
        per = w.shape[2] // tc
        kdim = w.shape[1]
        w_spec = pl.BlockSpec((None, tj, tc), lambda i, j, n: (n // per, j, n % per))
    else:
        kdim = w.shape[0]
        w_spec = pl.BlockSpec((tj, tc), lambda i, j, n: (j, n))
    if outs is None:
        outs = [_out2d(m, kdim, F32, tm, tj)]
    return _mm(name, NT, g, g_spec, w, w_spec, (m // tm, kdim // tj, n_contract // tc),
               (tm, tj), outs, epilogue, extras)


def _mm_bwd_w(name, a, g, g_spec, n, tk, tn, tm, sharded):
    m, kdim = a.shape
    if sharded:
        per = (n // N_SHARD) // tn
        out = (jax.ShapeDtypeStruct((N_SHARD, kdim, n // N_SHARD), F32),
               pl.BlockSpec((None, tk, tn), lambda i, j, mm: (j // per, i, j % per)))
    else:
        out = (jax.ShapeDtypeStruct((kdim, n), F32), pl.BlockSpec((tk, tn), lambda i, j, mm: (i, j)))
    return _mm(name, TN, a, pl.BlockSpec((tm, tk), lambda i, j, mm: (mm, i)), g, g_spec,
               (kdim // tk, n // tn, m // tm), (tk, tn), [out])


def _ew(name, fn, tiles, fulls, outs, sums=(), ts=512):
    tiles = [t if isinstance(t, tuple) else (t, pl.BlockSpec((ts, t.shape[1]), lambda i: (i, 0)))
             for t in tiles]
    s_rows = tiles[0][0].shape[-2]
    nt, nf, no = len(tiles), len(fulls), len(outs)

    def body(*refs):
        t_vals = [r[...] for r in refs[:nt]]
        f_vals = [r[...] for r in refs[nt:nt + nf]]
        o_refs, s_refs = refs[nt + nf:nt + nf + no], refs[nt + nf + no:]
        o_vals, s_vals = fn(t_vals, f_vals)
        for r, v in zip(o_refs, o_vals):
            r[...] = v.astype(r.dtype)
        for r, v in zip(s_refs, s_vals):
            part = jnp.sum(v, axis=0, keepdims=True)

            @pl.when(pl.program_id(0) == 0)
            def _():
                r[...] = part

            @pl.when(pl.program_id(0) > 0)
            def _():
                r[...] += part

    full_specs = [pl.BlockSpec(f.shape, lambda i, nd=f.ndim: (0,) * nd) for f in fulls]
    res = pl.pallas_call(
        body, name=name, grid=(s_rows // ts,),
        in_specs=[s for _, s in tiles] + full_specs,
        out_specs=[pl.BlockSpec((ts, c), lambda i: (i, 0)) for c, _ in outs]
        + [pl.BlockSpec((1, c), lambda i: (0, 0)) for c in sums],
        out_shape=[jax.ShapeDtypeStruct((s_rows, c), dt) for c, dt in outs]
        + [jax.ShapeDtypeStruct((1, c), F32) for c in sums],
        compiler_params=_params("arbitrary"),
    )(*[t for t, _ in tiles], *fulls)
    return res


def _rms_fwd(name, x, g):
    def fn(t, f):
        xv = t[0]
        r = lax.rsqrt(jnp.mean(xv * xv, axis=-1, keepdims=True) + EPS)
        return [xv * r * f[0]], []
    return _ew(name, fn, [x], [g], [(x.shape[1], BF16)])[0]


def _rms_bwd(name, dh, x, g, dres):
    def fn(t, f):
        dhv, xv, dr = t
        r = lax.rsqrt(jnp.mean(xv * xv, axis=-1, keepdims=True) + EPS)
        xhat = xv * r
        u = dhv * f[0]
        dx = r * (u - xhat * jnp.mean(u * xhat, axis=-1, keepdims=True)) + dr
        return [dx, dx], [dhv * xhat]
    c = x.shape[1]
    return _ew(name, fn, [dh, x, dres], [g], [(c, F32), (c, BF16)], sums=[c])


def _split3(x):
    x1 = x.astype(BF16)
    r1 = x - x1.astype(F32)
    x2 = r1.astype(BF16)
    x3 = (r1 - x2.astype(F32)).astype(BF16)
    return x1, x2, x3


def _rel_class(cp):
    far = (cp < MAX_REL) | (cp > BAND)
    return jnp.where(far, 2 * MAX_REL, BAND - cp)


def _skew_rows(x, sign):
    row = lax.broadcasted_iota(jnp.int32, x.shape, 0)
    for b in range(SUPER.bit_length() - 1):
        shift = (1 << b) if sign > 0 else SKEW_W - (1 << b)
        x = jnp.where((row >> b) & 1 == 1, pltpu.roll(x, shift, 1), x)
    return x


def _bias_expand(rel_bias):
    rel = jnp.pad(rel_bias, ((0, 0), (0, REL_PAD - N_REL))).reshape(N_HEADS, 1, REL_PAD)

    def body(rel_ref, o_ref):
        cls = lax.broadcasted_iota(jnp.int32, (REL_PAD, SKEW_W), 0)
        cp = lax.broadcasted_iota(jnp.int32, (REL_PAD, SKEW_W), 1)
        onehot = (cls == _rel_class(cp)).astype(BF16)
        rel8 = jnp.broadcast_to(rel_ref[...], (8, REL_PAD))
        trow = sum(jnp.dot(p, onehot, preferred_element_type=F32) for p in _split3(rel8))[0:1]
        full = _skew_rows(jnp.broadcast_to(trow, (SUPER, SKEW_W)), +1)[:, :BAND]
        qc = lax.broadcasted_iota(jnp.int32, (SUPER, BAND), 0) // CHUNK
        kc = lax.broadcasted_iota(jnp.int32, (SUPER, BAND), 1) // CHUNK
        on_band = (kc >= qc) & (kc <= qc + N_PREV_CHUNKS)
        o_ref[...] = jnp.where(on_band, full, NEG_INF)

    return pl.pallas_call(
        body, name="bias_expand", grid=(N_HEADS,),
        in_specs=[pl.BlockSpec((None, 1, REL_PAD), lambda h: (h, 0, 0))],
        out_specs=pl.BlockSpec((None, SUPER, BAND), lambda h: (h, 0, 0)),
        out_shape=jax.ShapeDtypeStruct((N_HEADS, SUPER, BAND), F32),
        compiler_params=_params("arbitrary"),
    )(rel)


def _bias_reduce(dbias):
    def body(d_ref, o_ref):
        x = jnp.concatenate([d_ref[...], jnp.zeros((SUPER, SKEW_W - BAND), F32)], axis=1)
        diag = jnp.sum(_skew_rows(x, -1), axis=0, keepdims=True)
        cp = lax.broadcasted_iota(jnp.int32, (SKEW_W, REL_PAD), 0)
        cls = lax.broadcasted_iota(jnp.int32, (SKEW_W, REL_PAD), 1)
        onehot = (cls == _rel_class(cp)).astype(BF16)
        diag8 = jnp.broadcast_to(diag, (8, SKEW_W))
        o_ref[...] = sum(jnp.dot(p, onehot, preferred_element_type=F32) for p in _split3(diag8))[0:1]

    out = pl.pallas_call(
        body, name="bias_reduce", grid=(N_HEADS,),
        in_specs=[pl.BlockSpec((None, SUPER, BAND), lambda h: (h, 0, 0))],
        out_specs=pl.BlockSpec((None, 1, REL_PAD), lambda h: (h, 0, 0)),
        out_shape=jax.ShapeDtypeStruct((N_HEADS, 1, REL_PAD), F32),
        compiler_params=_params("arbitrary"),
    )(dbias)
    return out.reshape(N_HEADS, REL_PAD)[:, :N_REL]


def _unit(x):
    r = lax.rsqrt(jnp.mean(x * x, axis=-1, keepdims=True) + EPS)
    return x * r, r


def _probs(qs, kn, bias, dead):
    s = lax.dot_general(qs, kn, NT, preferred_element_type=F32) + bias
    s = jnp.where(dead, NEG_INF, s)
    e = jnp.exp(s - jnp.max(s, axis=-1, keepdims=True))
    return e / jnp.sum(e, axis=-1, keepdims=True)


def _band_specs(nb, col0, clamp_hi):
    def spec(d):
        def index(hp, i):
            blk = jnp.maximum(i - d, 0)
            if clamp_hi:
                blk = jnp.minimum(blk, nb - 1)
            return (blk, col0 + hp)
        return pl.BlockSpec((SUPER, LANE), index)
    return [spec(2), spec(1), spec(0)]


def _dead_keys(i):
    col = lax.broadcasted_iota(jnp.int32, (SUPER, BAND), 1)
    return col < (2 - i) * SUPER


def _attn_fwd(proj, gq, gk, bias):
    s_len = proj.shape[0]
    nb = s_len // SUPER

    def body(q_ref, k0, k1, k2, v0, v1, v2, gq_ref, gk_ref, b_ref, o_ref):
        dead = _dead_keys(pl.program_id(1))
        outs = []
        for hh in range(2):
            sl = slice(HEAD_DIM * hh, HEAD_DIM * (hh + 1))
            qs = (_unit(q_ref[:, sl])[0] * (gq_ref[...] * QK_SCALE)).astype(BF16)
            kraw = jnp.concatenate([k0[:, sl], k1[:, sl], k2[:, sl]], axis=0)
            kn = (_unit(kraw)[0] * gk_ref[...]).astype(BF16)
            vv = jnp.concatenate([v0[:, sl], v1[:, sl], v2[:, sl]], axis=0).astype(BF16)
            p = _probs(qs, kn, b_ref[hh], dead).astype(BF16)
            outs.append(jnp.dot(p, vv, preferred_element_type=F32))
        o_ref[...] = jnp.concatenate(outs, axis=1).astype(o_ref.dtype)

    small = pl.BlockSpec((1, HEAD_DIM), lambda hp, i: (0, 0))
    return pl.pallas_call(
        body, name="attn_fwd", grid=(N_HEADS // 2, nb),
        in_specs=[pl.BlockSpec((SUPER, LANE), lambda hp, i: (i, hp))]
        + _band_specs(nb, 8, False) + _band_specs(nb, 16, False)
        + [small, small, pl.BlockSpec((2, SUPER, BAND), lambda hp, i: (hp, 0, 0))],
        out_specs=pl.BlockSpec((SUPER, LANE), lambda hp, i: (i, hp)),
        out_shape=jax.ShapeDtypeStruct((s_len, D_MODEL), BF16),
        compiler_params=_params("parallel", "arbitrary"),
    )(proj, proj, proj, proj, proj, proj, proj, gq, gk, bias)


def _attn_bwd(proj, d_out, gq, gk, bias):
    s_len = proj.shape[0]
    nb = s_len // SUPER

    def body(q_ref, k0, k1, k2, v0, v1, v2, do_ref, gq_ref, gk_ref, b_ref,
             dp_ref, db_ref, dgq_ref, dgk_ref, aq_ref, ak_ref, av_ref):
        i = pl.program_id(1)

        @pl.when(i == 0)
        def _():
            aq_ref[...] = jnp.zeros_like(aq_ref)
            ak_ref[...] = jnp.zeros_like(ak_ref)
            av_ref[...] = jnp.zeros_like(av_ref)
            db_ref[...] = jnp.zeros_like(db_ref)
            dgq_ref[...] = jnp.zeros_like(dgq_ref)
            dgk_ref[...] = jnp.zeros_like(dgk_ref)

        @pl.when(i < nb)
        def _():
            dead = _dead_keys(i)
            dq_pair, dk_pair, dv_pair, dgq_pair = [], [], [], []
            for hh in range(2):
                sl = slice(HEAD_DIM * hh, HEAD_DIM * (hh + 1))
                qhat, rq = _unit(q_ref[:, sl])
                qs = (qhat * (gq_ref[...] * QK_SCALE)).astype(BF16)
                kraw = jnp.concatenate([k0[:, sl], k1[:, sl], k2[:, sl]], axis=0)
                kn = (_unit(kraw)[0] * gk_ref[...]).astype(BF16)
                vv = jnp.concatenate([v0[:, sl], v1[:, sl], v2[:, sl]], axis=0).astype(BF16)
                do = do_ref[:, sl]
                p = _probs(qs, kn, b_ref[hh], dead)
                pb = p.astype(BF16)
                dv_pair.append(lax.dot_general(pb, do, TN, preferred_element_type=F32))
                dp = lax.dot_general(do, vv, NT, preferred_element_type=F32)
                ds = p * (dp - jnp.sum(p * dp, axis=-1, keepdims=True))
                db_ref[hh] += ds
                dsb = ds.astype(BF16)
                dqn = jnp.dot(dsb, kn, preferred_element_type=F32) * QK_SCALE
                dk_pair.append(lax.dot_general(dsb, qs, TN, preferred_element_type=F32))
                u = dqn * gq_ref[...]
                dq_pair.append(rq * (u - qhat * jnp.mean(u * qhat, axis=-1, keepdims=True)))
                dgq_pair.append(dqn * qhat)
            aq_ref[i % 3] = jnp.concatenate(dq_pair, axis=1)
            dgq_ref[...] += jnp.sum(jnp.concatenate(dgq_pair, axis=1), axis=0, keepdims=True)
            dk_band = jnp.concatenate(dk_pair, axis=1)
            dv_band = jnp.concatenate(dv_pair, axis=1)
            for j in range(3):
                slot = (i + 1 + j) % 3
                rows = slice(SUPER * j, SUPER * (j + 1))
                if j < 2:
                    ak_ref[slot] += dk_band[rows]
                    av_ref[slot] += dv_band[rows]
                else:
                    ak_ref[slot] = dk_band[rows]
                    av_ref[slot] = dv_band[rows]

        slot = (i + 1) % 3
        dkn = ak_ref[slot]
        dk_out, dgk_out = [], []
        for hh in range(2):
            sl = slice(HEAD_DIM * hh, HEAD_DIM * (hh + 1))
            khat, rk = _unit(k0[:, sl])
            u = dkn[:, sl] * gk_ref[...]
            dk_out.append(rk * (u - khat * jnp.mean(u * khat, axis=-1, keepdims=True)))
            dgk_out.append(dkn[:, sl] * khat)
        dp_ref[0] = aq_ref[slot].astype(dp_ref.dtype)
        dp_ref[1] = jnp.concatenate(dk_out, axis=1).astype(dp_ref.dtype)
        dp_ref[2] = av_ref[slot].astype(dp_ref.dtype)

        @pl.when(i >= 2)
        def _():
            dgk_ref[...] += jnp.sum(jnp.concatenate(dgk_out, axis=1), axis=0, keepdims=True)

    def qrow(hp, i):
        return (jnp.minimum(i, nb - 1), hp)

    small = pl.BlockSpec((1, HEAD_DIM), lambda hp, i: (0, 0))
    part = pl.BlockSpec((None, 1, LANE), lambda hp, i: (hp, 0, 0))
    return pl.pallas_call(
        body, name="attn_bwd", grid=(N_HEADS // 2, nb + 2),
        in_specs=[pl.BlockSpec((SUPER, LANE), qrow)]
        + _band_specs(nb, 8, True) + _band_specs(nb, 16, True)
        + [pl.BlockSpec((SUPER, LANE), qrow), small, small,
           pl.BlockSpec((2, SUPER, BAND), lambda hp, i: (hp, 0, 0))],
        out_specs=[pl.BlockSpec((3, SUPER, LANE), lambda hp, i: (0, jnp.maximum(i - 2, 0), hp)),
                   pl.BlockSpec((2, SUPER, BAND), lambda hp, i: (hp, 0, 0)), part, part],
        out_shape=[jax.ShapeDtypeStruct((6, s_len, D_MODEL), BF16),
                   jax.ShapeDtypeStruct((N_HEADS, SUPER, BAND), F32),
                   jax.ShapeDtypeStruct((N_HEADS // 2, 1, LANE), F32),
                   jax.ShapeDtypeStruct((N_HEADS // 2, 1, LANE), F32)],
        scratch_shapes=[pltpu.VMEM((3, SUPER, LANE), F32)] * 3,
        compiler_params=_params("parallel", "arbitrary"),
    )(proj, proj, proj, proj, proj, proj, proj, d_out, gq, gk, bias)


CONV_ROWS = 512
HALO = 8


def _rows_with_halo(ref, r0, n, front, s_len):
    zeros = jnp.zeros((HALO, ref.shape[1]), F32)
    if front:
        return (jnp.concatenate([zeros, ref[0:n, :]], axis=0) if r0 == 0
                else ref[r0 - HALO:r0 + n, :])
    return (jnp.concatenate([ref[r0:r0 + n, :], zeros], axis=0) if r0 + n == s_len
            else ref[r0:r0 + n + HALO, :])


def _earlier(ext, k):
    return pltpu.roll(ext, k, 0)[HALO:]


def _later(ext, k):
    n = ext.shape[0]
    return pltpu.roll(ext, n - k, 0)[:n - HALO]


def _conv_cols(col0):
    return lambda s_len: pl.BlockSpec((s_len, LANE), lambda j: (0, col0 + j))


def _conv_fwd(proj, conv_w, conv_b):
    s_len = proj.shape[0]

    def body(bg_ref, cg_ref, xc_ref, w_ref, b_ref, o_ref):
        w = [w_ref[t:t + 1, :] for t in range(3)]
        for r0 in range(0, s_len, CONV_ROWS):
            u = _rows_with_halo(cg_ref, r0, CONV_ROWS, True, s_len) * \
                _rows_with_halo(xc_ref, r0, CONV_ROWS, True, s_len)
            conv = b_ref[...] + w[0] * _earlier(u, 2) + w[1] * _earlier(u, 1) + w[2] * u[HALO:]
            o_ref[r0:r0 + CONV_ROWS, :] = (bg_ref[r0:r0 + CONV_ROWS, :] * conv).astype(o_ref.dtype)

    return pl.pallas_call(
        body, name="conv_fwd", grid=(D_MODEL // LANE,),
        in_specs=[_conv_cols(24)(s_len), _conv_cols(32)(s_len), _conv_cols(40)(s_len),
                  pl.BlockSpec((3, LANE), lambda j: (0, j)), pl.BlockSpec((1, LANE), lambda j: (0, j))],
        out_specs=pl.BlockSpec((s_len, LANE), lambda j: (0, j)),
        out_shape=jax.ShapeDtypeStruct((s_len, D_MODEL), BF16),
        compiler_params=_params("parallel"),
    )(proj, proj, proj, conv_w, conv_b)


def _conv_bwd(dproj6, dy, proj, conv_w, conv_b):
    s_len = proj.shape[0]

    def body(dy_ref, bg_ref, cg_ref, xc_ref, w_ref, b_ref, _, dp_ref, dw_ref):
        w = [w_ref[t:t + 1, :] for t in range(3)]
        acc = [jnp.zeros((1, LANE), F32) for _ in range(4)]
        for r0 in range(0, s_len, CONV_ROWS):
            rows = slice(r0, r0 + CONV_ROWS)
            u = _rows_with_halo(cg_ref, r0, CONV_ROWS, True, s_len) * \
                _rows_with_halo(xc_ref, r0, CONV_ROWS, True, s_len)
            u2, u1, u0 = _earlier(u, 2), _earlier(u, 1), u[HALO:]
            conv = b_ref[...] + w[0] * u2 + w[1] * u1 + w[2] * u0
            dyv = dy_ref[rows, :]
            dp_ref[0, rows, :] = (dyv * conv).astype(dp_ref.dtype)
            dconv_ext = _rows_with_halo(dy_ref, r0, CONV_ROWS, False, s_len) * \
                _rows_with_halo(bg_ref, r0, CONV_ROWS, False, s_len)
            dconv = dconv_ext[:CONV_ROWS]
            for t, term in enumerate([dconv * u2, dconv * u1, dconv * u0, dconv]):
                acc[t] = acc[t] + jnp.sum(term, axis=0, keepdims=True)
            du = w[2] * dconv + w[1] * _later(dconv_ext, 1) + w[0] * _later(dconv_ext, 2)
            dp_ref[1, rows, :] = (du * xc_ref[rows, :]).astype(dp_ref.dtype)
            dp_ref[2, rows, :] = (du * cg_ref[rows, :]).astype(dp_ref.dtype)
        dw_ref[...] = jnp.zeros_like(dw_ref)
        for t in range(4):
            dw_ref[t:t + 1, :] = acc[t]

    return pl.pallas_call(
        body, name="conv_bwd", grid=(D_MODEL // LANE,),
        in_specs=[pl.BlockSpec((s_len, LANE), lambda j: (0, j)),
                  _conv_cols(24)(s_len), _conv_cols(32)(s_len), _conv_cols(40)(s_len),
                  pl.BlockSpec((3, LANE), lambda j: (0, j)), pl.BlockSpec((1, LANE), lambda j: (0, j)),
                  pl.BlockSpec(memory_space=pl.ANY)],
        out_specs=[pl.BlockSpec((3, s_len, LANE), lambda j: (1, 0, j)),
                   pl.BlockSpec((8, LANE), lambda j: (0, j))],
        out_shape=[jax.ShapeDtypeStruct(dproj6.shape, dproj6.dtype),
                   jax.ShapeDtypeStruct((8, D_MODEL), F32)],
        input_output_aliases={6: 0},
        compiler_params=_params("parallel"),
    )(dy, proj, proj, proj, conv_w, conv_b, dproj6)


def _local_grads(x, target, norm1_g, w_in3, q_norm_g, k_norm_g, rel_bias, conv_w, conv_b,
                 w_ap, w_cp, w_gate3, b_gate, w_out, norm2_g, w_up3, w_down):
    s_len = x.shape[0]
    tm = min(1024, s_len)
    row = lambda v: v.reshape(1, -1)

    h = _rms_fwd("norm1", x, row(norm1_g))
    proj = _mm_fwd("proj", h, w_in3, tm, 512, D_MODEL)

    def gate_epi(acc, e, o):
        o[0][...] = jax.nn.sigmoid(acc + e[0])
    gates = _mm_fwd("gates", h, w_gate3, tm, 512, D_MODEL, epilogue=gate_epi,
                    extras=[(row(b_gate), pl.BlockSpec((1, 512), lambda i, j, k: (0, j)))])

    bias = _bias_expand(rel_bias)
    gq, gk = row(q_norm_g), row(k_norm_g)
    attn = _attn_fwd(proj, gq, gk, bias)
    yconv = _conv_fwd(proj, conv_w, row(conv_b))
    ya = _mm_fwd("attn_proj", attn, w_ap, tm, 512, D_MODEL)

    def merge_epi(acc, e, o):
        ya_v, ga, gc = e
        o[0][...] = acc
        o[1][...] = (ga * ya_v + gc * acc).astype(BF16)
    gate_a, gate_c = (gates, _tile_spec(tm, 512, 0)), (gates, _tile_spec(tm, 512, 2))
    yc, merged = _mm_fwd("conv_proj", yconv, w_cp, tm, 512, D_MODEL, epilogue=merge_epi,
                         outs=[_out2d(s_len, D_MODEL, F32, tm, 512), _out2d(s_len, D_MODEL, BF16, tm, 512)],
                         extras=[(ya, _tile_spec(tm, 512)), gate_a, gate_c])

    def res_epi(acc, e, o):
        o[0][...] = e[0] + acc
    x1 = _mm_fwd("out_proj", merged, w_out, tm, 512, D_MODEL, epilogue=res_epi,
                 extras=[(x, _tile_spec(tm, 512))])
    h2 = _rms_fwd("norm2", x1, row(norm2_g))

    def up_epi(acc, e, o):
        o[0][...] = acc.astype(BF16)
        o[1][...] = jnp.square(jnp.maximum(acc, 0.0)).astype(BF16)
    up, act = _mm_fwd("mlp_up", h2, w_up3, tm, 512, D_MODEL, epilogue=up_epi,
                      outs=[_out2d(s_len, D_FF, BF16, tm, 512), _out2d(s_len, D_FF, BF16, tm, 512)])

    def loss_epi(acc, e, o):
        err = e[0] + acc - e[1]
        dy = err * (1.0 / D_MODEL)
        o[0][...] = dy
        o[1][...] = dy.astype(BF16)
        sq = err * err
        part = sq[:, 0:LANE]
        for c0 in range(LANE, D_MODEL, LANE):
            part = part + sq[:, c0:c0 + LANE]
        o[2][...] = jnp.sum(part.reshape(tl // 8, 8, LANE), axis=0)
    tl = 512
    dy, dy_b, loss_part = _mm_fwd(
        "mlp_down", act, w_down, tl, D_MODEL, 1024, epilogue=loss_epi,
        outs=[_out2d(s_len, D_MODEL, F32, tl, D_MODEL), _out2d(s_len, D_MODEL, BF16, tl, D_MODEL),
              (jax.ShapeDtypeStruct((8 * (s_len // tl), LANE), F32), pl.BlockSpec((8, LANE), lambda i, j, k: (i, 0)))],
        extras=[(x1, _tile_spec(tl, D_MODEL)), (target, _tile_spec(tl, D_MODEL))])

    def dup_epi(acc, e, o):
        o[0][...] = (acc * (2.0 * jnp.maximum(e[0].astype(F32), 0.0))).astype(BF16)
    dup = _mm_bwd_x("d_act", dy_b, pl.BlockSpec((tm, D_MODEL), lambda i, j, n: (i, n)), w_down,
                    tm, 512, D_MODEL, s_len, D_MODEL, epilogue=dup_epi,
                    outs=[_out2d(s_len, D_FF, BF16, tm, 512)], extras=[(up, _tile_spec(tm, 512))])
    g_down = _mm_bwd_w("g_down", act, dy_b, pl.BlockSpec((tm, 512), lambda i, j, m: (m, j)),
                       D_MODEL, 1024, 512, tm, False)
    g_up = _mm_bwd_w("g_up", h2, dup, pl.BlockSpec((tm, 512), lambda i, j, m: (m, j)),
                     D_FF, 1024, 512, tm, True)
    dh2 = _mm_bwd_x("d_h2", dup, pl.BlockSpec((tm, 1024), lambda i, j, n: (i, n)), w_up3,
                    tm, 512, 1024, s_len, D_FF)
    dx1, dx1_b, dg2 = _rms_bwd("norm2_bwd", dh2, x1, row(norm2_g), dy)

    def dmerge_epi(acc, e, o):
        ya_v, yc_v, ga, gc = e
        o[0][...] = (acc * ga).astype(BF16)
        o[1][...] = (acc * gc).astype(BF16)
        o[2][0] = (acc * ya_v * ga * (1.0 - ga)).astype(BF16)
        o[2][1] = (acc * yc_v * gc * (1.0 - gc)).astype(BF16)
    dya, dyc, dgp2 = _mm_bwd_x(
        "d_merged", dx1_b, pl.BlockSpec((tm, D_MODEL), lambda i, j, n: (i, n)), w_out,
        tm, 512, D_MODEL, s_len, D_MODEL, epilogue=dmerge_epi,
        outs=[_out2d(s_len, D_MODEL, BF16, tm, 512), _out2d(s_len, D_MODEL, BF16, tm, 512),
              (jax.ShapeDtypeStruct((2, s_len, D_MODEL), BF16), pl.BlockSpec((2, tm, 512), lambda i, j, n: (0, i, j)))],
        extras=[(ya, _tile_spec(tm, 512)), (yc, _tile_spec(tm, 512)), gate_a, gate_c])
    g_out = _mm_bwd_w("g_out", merged, dx1_b, pl.BlockSpec((tm, 512), lambda i, j, m: (m, j)),
                      D_MODEL, 1024, 512, tm, False)
    d_attn = _mm_bwd_x("d_attn", dya, pl.BlockSpec((tm, D_MODEL), lambda i, j, n: (i, n)), w_ap,
                       tm, 512, D_MODEL, s_len, D_MODEL, outs=[_out2d(s_len, D_MODEL, BF16, tm, 512)])
    g_ap = _mm_bwd_w("g_attn_proj", attn, dya, pl.BlockSpec((tm, 512), lambda i, j, m: (m, j)),
                     D_MODEL, 1024, 512, tm, False)
    d_yconv = _mm_bwd_x("d_yconv", dyc, pl.BlockSpec((tm, D_MODEL), lambda i, j, n: (i, n)), w_cp,
                        tm, 512, D_MODEL, s_len, D_MODEL)
    g_cp = _mm_bwd_w("g_conv_proj", yconv, dyc, pl.BlockSpec((tm, 512), lambda i, j, m: (m, j)),
                     D_MODEL, 1024, 512, tm, False)

    dproj6, dbias, dgq_p, dgk_p = _attn_bwd(proj, d_attn, gq, gk, bias)
    dproj6, dconv_wb = _conv_bwd(dproj6, d_yconv, proj, conv_w, row(conv_b))
    d_rel = _bias_reduce(dbias)

    piece = lambda width: (lambda blk: (blk * width) // D_MODEL, lambda blk: (blk * width % D_MODEL) // width)
    pc, cb = piece(512)
    g_in = _mm_bwd_w("g_in", h, dproj6, pl.BlockSpec((None, tm, 512), lambda i, j, m: (pc(j), m, cb(j))),
                     6 * D_MODEL, 1024, 512, tm, True)
    g_gate = _mm_bwd_w("g_gate", h, dgp2, pl.BlockSpec((None, tm, 512), lambda i, j, m: (pc(j), m, cb(j))),
                       2 * D_MODEL, 1024, 512, tm, True)
    dh_a = _mm_bwd_x("d_h_proj", dproj6, pl.BlockSpec((None, tm, 512), lambda i, j, n: (pc(n), i, cb(n))),
                     w_in3, tm, 512, 512, s_len, 6 * D_MODEL)

    def add_epi(acc, e, o):
        o[0][...] = acc + e[0]
    dh = _mm_bwd_x("d_h_gate", dgp2, pl.BlockSpec((None, tm, 512), lambda i, j, n: (pc(n), i, cb(n))),
                   w_gate3, tm, 512, 512, s_len, 2 * D_MODEL, epilogue=add_epi,
                   extras=[(dh_a, _tile_spec(tm, 512))])
    grad_x, _, dg1 = _rms_bwd("norm1_bwd", dh, x, row(norm1_g), dx1)

    def bsum(t, f):
        return [], [t[0].astype(F32), t[1].astype(F32)]
    ts = 512
    db_a, db_c = _ew("b_gate_sum", bsum,
                     [(dgp2, pl.BlockSpec((None, ts, D_MODEL), lambda i: (0, i, 0))),
                      (dgp2, pl.BlockSpec((None, ts, D_MODEL), lambda i: (1, i, 0)))],
                     [], [], sums=[D_MODEL, D_MODEL], ts=ts)

    big = dict(w_in=g_in, w_attn_proj=g_ap, w_conv_proj=g_cp, w_gate=g_gate, w_out=g_out,
               w_up=g_up, w_down=g_down)
    small = dict(norm1_g=dg1, norm2_g=dg2, conv_wb=dconv_wb, b_gate=(db_a, db_c),
                 q_norm_g=dgq_p.reshape(N_HEADS // 2, LANE), k_norm_g=dgk_p.reshape(N_HEADS // 2, LANE),
                 rel_bias=d_rel)
    return loss_part, grad_x, big, small


def _finish_small(loss_part, dgq_p, dgk_p):
    def body(l_ref, q_ref, k_ref, lo_ref, qo_ref, ko_ref):
        total = jnp.sum(jnp.sum(l_ref[...], axis=0, keepdims=True), axis=1, keepdims=True)
        lo_ref[...] = jnp.broadcast_to(total * (0.5 / D_MODEL), lo_ref.shape)
        for src, dst in ((q_ref, qo_ref), (k_ref, ko_ref)):
            s = jnp.sum(src[...], axis=0, keepdims=True)
            dst[...] = s[:, :HEAD_DIM] + s[:, HEAD_DIM:]

    return pl.pallas_call(
        body, name="finish_small",
        out_shape=[jax.ShapeDtypeStruct((8, LANE), F32), jax.ShapeDtypeStruct((1, HEAD_DIM), F32),
                   jax.ShapeDtypeStruct((1, HEAD_DIM), F32)],
    )(loss_part, dgq_p, dgk_p)


HBM_SPEC = pl.BlockSpec(memory_space=pl.ANY)


def _place():
    return lax.axis_index("x"), lax.axis_index("y"), lax.axis_index("c")


def _other_chips(x, y):
    return [(1 - x, y), (x, 1 - y), (1 - x, 1 - y)]


def _all_gather(shards, small):
    n = len(shards)

    def body(*refs):
        ins, small_in = refs[:n], refs[n]
        outs, small_out = refs[n + 1:2 * n + 1], refs[2 * n + 1]
        send1, recv1, send2, recv2, send3, recv3, local = refs[2 * n + 2:]
        x, y, c = _place()
        me = 2 * x + y
        chips = _other_chips(x, y)
        sibling = (x, y, 1 - c)

        def half(ref, w, shard, which):
            hr = shards[w].shape[0] // 2
            return ref.at[shard, pl.ds(which * hr, hr)]

        def remote(src, dst, send, recv, k, to):
            return pltpu.make_async_remote_copy(src_ref=src, dst_ref=dst, send_sem=send.at[k],
                                                recv_sem=recv.at[k], device_id=to, device_id_type=MESH)

        own = [pltpu.make_async_copy(ins[w], outs[w].at[me], local.at[w]) for w in range(n)]
        own.append(pltpu.make_async_copy(small_in, small_out.at[me], local.at[n]))
        for cp in own:
            cp.start()
        sent = []
        for w in range(n):
            hr = shards[w].shape[0] // 2
            for j, chip in enumerate(chips):
                sent.append(remote(ins[w].at[pl.ds(c * hr, hr)], half(outs[w], w, me, c),
                                   send1, recv1, 3 * w + j, (*chip, c)))
        for j, chip in enumerate(chips):
            sent.append(remote(small_in, small_out.at[me], send3, recv3, j, (*chip, c)))
        for cp in sent:
            cp.start()
        for w in range(n):
            for j, chip in enumerate(chips):
                landed = half(outs[w], w, 2 * chip[0] + chip[1], c)
                remote(landed, landed, send1, recv1, 3 * w + j, (*chip, c)).wait_recv()
                fwd = remote(landed, landed, send2, recv2, 3 * w + j, sibling)
                fwd.start()
                sent.append(fwd)
        for w in range(n):
            for j, chip in enumerate(chips):
                other = half(outs[w], w, 2 * chip[0] + chip[1], 1 - c)
                remote(other, other, send2, recv2, 3 * w + j, sibling).wait_recv()
        for j, chip in enumerate(chips):
            dst = small_out.at[2 * chip[0] + chip[1]]
            remote(dst, dst, send3, recv3, j, (*chip, c)).wait_recv()
        for cp in sent:
            cp.wait_send()
        for cp in own:
            cp.wait()

    return pl.pallas_call(
        body, name="gather_weights",
        in_specs=[HBM_SPEC] * (n + 1), out_specs=[HBM_SPEC] * (n + 1),
        out_shape=[jax.ShapeDtypeStruct((N_SHARD,) + s.shape, s.dtype) for s in shards]
        + [jax.ShapeDtypeStruct((N_SHARD,) + small.shape, small.dtype)],
        scratch_shapes=[pltpu.SemaphoreType.DMA((3 * n,))] * 4 + [pltpu.SemaphoreType.DMA((3,))] * 2
        + [pltpu.SemaphoreType.DMA((n + 1,))],
    )(*shards, small)


def _pair_exchange(grads):
    n = len(grads)

    def body(*refs):
        ins, outs = refs[:n], refs[n:2 * n]
        send, recv = refs[2 * n:]
        x, y, c = _place()
        copies = []
        for w in range(n):
            hr = grads[w].shape[1] // 2
            copies.append(pltpu.make_async_remote_copy(
                src_ref=ins[w].at[:, pl.ds((1 - c) * hr, hr)], dst_ref=outs[w],
                send_sem=send.at[w], recv_sem=recv.at[w], device_id=(x, y, 1 - c), device_id_type=MESH))
        for cp in copies:
            cp.start()
        for cp in copies:
            cp.wait()

    return pl.pallas_call(
        body, name="pair_exchange",
        in_specs=[HBM_SPEC] * n, out_specs=[HBM_SPEC] * n,
        out_shape=[jax.ShapeDtypeStruct((N_SHARD, g.shape[1] // 2, g.shape[2]), g.dtype) for g in grads],
        scratch_shapes=[pltpu.SemaphoreType.DMA((n,))] * 2,
    )(*grads)


def _row_tile(hr):
    return min(hr, 256)


def _pair_add(name, where, grad, got):
    _, hr, cols = got.shape
    tr = _row_tile(hr)
    nblk = hr // tr

    def body(w_ref, g_ref, r_ref, o_ref):
        o_ref[...] = (g_ref[...] + r_ref[...]).astype(o_ref.dtype)

    return pl.pallas_call(
        body, name=name,
        grid_spec=pltpu.PrefetchScalarGridSpec(
            num_scalar_prefetch=1, grid=(N_SHARD, nblk),
            in_specs=[pl.BlockSpec((None, tr, cols), lambda s, i, w: (s, w[1] * nblk + i, 0)),
                      pl.BlockSpec((None, tr, cols), lambda s, i, w: (s, i, 0))],
            out_specs=pl.BlockSpec((None, tr, cols), lambda s, i, w: (s, i, 0))),
        out_shape=jax.ShapeDtypeStruct(got.shape, BF16),
        compiler_params=_params("parallel", "parallel"),
    )(where, grad, got)


def _chip_exchange(partials):
    n = len(partials)

    def body(*refs):
        ins, outs = refs[:n], refs[n:2 * n]
        send, recv = refs[2 * n:]
        x, y, c = _place()
        copies = []
        for w in range(n):
            for j, chip in enumerate(_other_chips(x, y)):
                copies.append(pltpu.make_async_remote_copy(
                    src_ref=ins[w].at[2 * chip[0] + chip[1]], dst_ref=outs[w].at[j],
                    send_sem=send.at[3 * w + j], recv_sem=recv.at[3 * w + j],
                    device_id=(*chip, c), device_id_type=MESH))
        for cp in copies:
            cp.start()
        for cp in copies:
            cp.wait()

    return pl.pallas_call(
        body, name="chip_exchange",
        in_specs=[HBM_SPEC] * n, out_specs=[HBM_SPEC] * n,
        out_shape=[jax.ShapeDtypeStruct((3,) + p.shape[1:], p.dtype) for p in partials],
        scratch_shapes=[pltpu.SemaphoreType.DMA((3 * n,))] * 2,
    )(*partials)


def _final_add(name, where, grad, got, arrived):
    _, hr, cols = got.shape
    tr = _row_tile(hr)
    nblk = hr // tr

    def body(w_ref, g_ref, r_ref, a_ref, o_ref):
        acc = g_ref[...] + r_ref[...]
        for j in range(3):
            acc = acc + a_ref[j].astype(F32)
        o_ref[...] = acc

    return pl.pallas_call(
        body, name=name,
        grid_spec=pltpu.PrefetchScalarGridSpec(
            num_scalar_prefetch=1, grid=(nblk,),
            in_specs=[pl.BlockSpec((None, tr, cols), lambda i, w: (w[0], w[1] * nblk + i, 0)),
                      pl.BlockSpec((None, tr, cols), lambda i, w: (w[0], i, 0)),
                      pl.BlockSpec((3, tr, cols), lambda i, w: (0, i, 0))],
            out_specs=pl.BlockSpec((tr, cols), lambda i, w: (i, 0))),
        out_shape=jax.ShapeDtypeStruct((hr, cols), F32),
        compiler_params=_params("parallel"),
    )(where, grad, got, arrived)


def _pair_share(halves):
    n = len(halves)

    def body(*refs):
        ins, outs = refs[:n], refs[n:2 * n]
        send, recv, local = refs[2 * n:]
        x, y, c = _place()
        copies, own = [], []
        for w in range(n):
            hr = halves[w].shape[0]
            mine = outs[w].at[pl.ds(c * hr, hr)]
            own.append(pltpu.make_async_copy(ins[w], mine, local.at[w]))
            copies.append(pltpu.make_async_remote_copy(
                src_ref=ins[w], dst_ref=mine, send_sem=send.at[w], recv_sem=recv.at[w],
                device_id=(x, y, 1 - c), device_id_type=MESH))
        for cp in own + copies:
            cp.start()
        for w in range(n):
            hr = halves[w].shape[0]
            theirs = outs[w].at[pl.ds((1 - c) * hr, hr)]
            pltpu.make_async_remote_copy(
                src_ref=theirs, dst_ref=theirs, send_sem=send.at[w], recv_sem=recv.at[w],
                device_id=(x, y, 1 - c), device_id_type=MESH).wait_recv()
        for cp in copies:
            cp.wait_send()
        for cp in own:
            cp.wait()

    return pl.pallas_call(
        body, name="pair_share",
        in_specs=[HBM_SPEC] * n, out_specs=[HBM_SPEC] * n,
        out_shape=[jax.ShapeDtypeStruct((2 * h.shape[0], h.shape[1]), h.dtype) for h in halves],
        scratch_shapes=[pltpu.SemaphoreType.DMA((n,))] * 3,
    )(*halves)


SMALL_ROWS = 32
N_DEV = 8


def _all_reduce_small(pack):
    def body(p_ref, o_ref, buf, send, recv):
        x, y, c = _place()
        buf[4 * x + 2 * y + c] = p_ref[...]
        copies, waits = [], []
        for k in range(1, N_DEV):
            px = 1 - x if k & 4 else x
            py = 1 - y if k & 2 else y
            pc = 1 - c if k & 1 else c
            copies.append(pltpu.make_async_remote_copy(
                src_ref=p_ref, dst_ref=buf.at[4 * x + 2 * y + c], send_sem=send.at[k - 1],
                recv_sem=recv.at[k - 1], device_id=(px, py, pc), device_id_type=MESH))
            waits.append(pltpu.make_async_remote_copy(
                src_ref=p_ref, dst_ref=buf.at[4 * px + 2 * py + pc], send_sem=send.at[k - 1],
                recv_sem=recv.at[k - 1], device_id=(px, py, pc), device_id_type=MESH))
        for cp in copies:
            cp.start()
        for cp in waits:
            cp.wait_recv()
        acc = buf[0]
        for d in range(1, N_DEV):
            acc = acc + buf[d]
        o_ref[...] = acc
        for cp in copies:
            cp.wait_send()

    return pl.pallas_call(
        body, name="all_reduce_small",
        out_shape=jax.ShapeDtypeStruct(pack.shape, F32),
        scratch_shapes=[pltpu.VMEM((N_DEV,) + pack.shape, F32),
                        pltpu.SemaphoreType.DMA((N_DEV - 1,)), pltpu.SemaphoreType.DMA((N_DEV - 1,))],
    )(pack)


def _adamw(name, w, g, m, v):
    c1 = 1.0 - ADAM_B1 ** ADAM_STEP
    c2 = 1.0 - ADAM_B2 ** ADAM_STEP

    def fn(t, f):
        wv, gv, mv, vv = t
        m2 = ADAM_B1 * mv + (1.0 - ADAM_B1) * gv
        v2 = ADAM_B2 * vv + (1.0 - ADAM_B2) * (gv * gv)
        delta = -ADAM_LR * ((m2 / c1) / (jnp.sqrt(v2 / c2) + ADAM_EPS) + ADAM_WD * wv)
        return [delta, m2, v2], []

    cols = w.shape[1]
    return _ew(name, fn, [w, g, m, v], [], [(cols, F32)] * 3, ts=min(w.shape[0], 256))


def _cast(name, w):
    return _ew(name, lambda t, f: ([t[0]], []), [w], [], [(w.shape[1], BF16)], ts=256)[0]


def _pack_small(norm1_g, norm2_g, conv_b, b_gate, conv_w, q_norm_g, k_norm_g, rel_bias):
    pack = jnp.zeros((SMALL_ROWS, D_MODEL), F32)
    for r0, v in ((0, norm1_g), (1, norm2_g), (2, conv_b), (3, b_gate.reshape(2, D_MODEL)), (5, conv_w),
                  (8, q_norm_g), (9, k_norm_g), (10, rel_bias)):
        v = v.reshape(-1, v.shape[-1]).astype(F32)
        pack = pack.at[r0:r0 + v.shape[0], :v.shape[1]].set(v)
    return pack


def _unpack_small(pack, conv_cols):
    return dict(norm1_g=pack[0], norm2_g=pack[1], conv_b=pack[2], b_gate=pack[3:5].reshape(2 * D_MODEL),
                conv_w=pack[5:8, :conv_cols], q_norm_g=pack[8, :HEAD_DIM], k_norm_g=pack[9, :HEAD_DIM],
                rel_bias=pack[10:10 + N_HEADS, :N_REL])


BIG = ["w_in", "w_attn_proj", "w_conv_proj", "w_gate", "w_out", "w_up", "w_down"]
WEIGHTS = ["norm1_g", "w_in", "q_norm_g", "k_norm_g", "rel_bias", "conv_w", "conv_b", "w_attn_proj",
           "w_conv_proj", "w_gate", "b_gate", "w_out", "norm2_g", "w_up", "w_down"]


def kernel(x, norm1_g, w_in, q_norm_g, k_norm_g, rel_bias, conv_w, conv_b, w_attn_proj, w_conv_proj, w_gate, b_gate, w_out, norm2_g, w_up, w_down, loss_target, m_norm1_g, m_w_in, m_q_norm_g, m_k_norm_g, m_rel_bias, m_conv_w, m_conv_b, m_w_attn_proj, m_w_conv_proj, m_w_gate, m_b_gate, m_w_out, m_norm2_g, m_w_up, m_w_down, v_norm1_g, v_w_in, v_q_norm_g, v_k_norm_g, v_rel_bias, v_conv_w, v_conv_b, v_w_attn_proj, v_w_conv_proj, v_w_gate, v_b_gate, v_w_out, v_norm2_g, v_w_up, v_w_down):
    given = dict(locals())
    w = {n: given[n] for n in WEIGHTS}
    m = {n: given["m_" + n] for n in WEIGHTS}
    v = {n: given["v_" + n] for n in WEIGHTS}
    s_len = x.shape[1]
    shard = 2 * lax.axis_index("x") + lax.axis_index("y")
    where = jnp.stack([shard, lax.axis_index("c")]).astype(jnp.int32)
    conv_cols = conv_w.shape[1]

    small_in = jnp.zeros((8, conv_cols), F32).at[:3].set(conv_w)
    *gathered, conv_w4 = _all_gather([_cast("cast_" + n, w[n]) for n in BIG], small_in)
    wg = dict(zip(BIG, gathered))
    conv_w_full = conv_w4[:, :3, :].transpose(1, 0, 2).reshape(3, N_SHARD * conv_cols)
    rows = lambda a: a.reshape(a.shape[0] * a.shape[1], a.shape[2])

    loss_part, grad_x, big, small = _local_grads(
        x.reshape(s_len, D_MODEL), loss_target.reshape(s_len, D_MODEL), norm1_g, wg["w_in"], q_norm_g,
        k_norm_g, rel_bias, conv_w_full, conv_b, rows(wg["w_attn_proj"]), rows(wg["w_conv_proj"]),
        wg["w_gate"], b_gate, rows(wg["w_out"]), norm2_g, wg["w_up"], rows(wg["w_down"]))

    g4 = [big[n] if big[n].ndim == 3 else big[n].reshape(N_SHARD, -1, big[n].shape[1]) for n in BIG]
    got = _pair_exchange(g4)
    partial = [_pair_add("pair_add_" + n, where, g, r) for n, g, r in zip(BIG, g4, got)]
    arrived = _chip_exchange(partial)
    halves = [_final_add("final_add_" + n, where, g, r, a) for n, g, r, a in zip(BIG, g4, got, arrived)]
    grad = dict(zip(BIG, _pair_share(halves)))

    loss_local, dgq, dgk = _finish_small(loss_part, small["q_norm_g"], small["k_norm_g"])
    pack = _pack_small(small["norm1_g"], small["norm2_g"], small["conv_wb"][3], jnp.concatenate(small["b_gate"], axis=1),
                       small["conv_wb"][0:3], dgq, dgk, small["rel_bias"])
    total = _all_reduce_small(pack)
    g_small = _unpack_small(total, D_MODEL)
    g_small["conv_w"] = lax.dynamic_slice(g_small["conv_w"], (0, shard * conv_cols), (3, conv_cols))
    grad.update(g_small)

    delta, new_m, new_v = {}, {}, {}
    for n in BIG:
        delta[n], new_m[n], new_v[n] = _adamw("adamw_" + n, w[n], grad[n], m[n], v[n])
    small_names = [n for n in WEIGHTS if n not in BIG]
    packs = [_pack_small(**{n: src[n] for n in small_names}) for src in (w, grad, m, v)]
    for out, packed in zip((delta, new_m, new_v), _adamw("adamw_small", *packs)):
        out.update({n: a.reshape(w[n].shape) for n, a in _unpack_small(packed, conv_cols).items()})

    loss = lax.psum(loss_local[0, 0], ("x", "y", "c"))
    outs = [loss, grad_x.reshape(x.shape)]
    for group in (grad, delta, new_m, new_v):
        outs += [group[n].reshape(w[n].shape) for n in WEIGHTS]
    return tuple(outs)
```

```python
import functools

import jax
import jax.numpy as jnp
from jax import lax
from jax.experimental import pallas as pl
from jax.experimental.pallas import tpu as pltpu

F32 = jnp.float32
BF16 = jnp.bfloat16

D_MODEL = 1024
N_HEADS = 16
HEAD_DIM = 64
CHUNK = 64
N_PREV_CHUNKS = 8
MAX_REL = 256
D_FF = 4096
N_REL = 2 * MAX_REL + 1
REL_PAD = 640
EPS = 1e-6
NEG_INF = -1e30
QK_SCALE = HEAD_DIM ** -0.5

SUPER = 4 * CHUNK
BAND = SUPER + N_PREV_CHUNKS * CHUNK
SKEW_W = 1024
N_SHARD = 4
LANE = 128
VMEM_LIMIT = 48 * 1024 * 1024

ADAM_LR = 0.001
ADAM_B1 = 0.9
ADAM_B2 = 0.999
ADAM_EPS = 1e-08
ADAM_WD = 0.01
ADAM_STEP = 10

MESH = pl.DeviceIdType.MESH
NN = (((1,), (0,)), ((), ()))
NT = (((1,), (1,)), ((), ()))
TN = (((0,), (0,)), ((), ()))


def _params(*sem):
    return pltpu.CompilerParams(dimension_semantics=sem or None, vmem_limit_bytes=VMEM_LIMIT)


def _mm(name, dims, a, a_spec, b, b_spec, grid, tile, outs, epilogue=None, extras=()):
    nk, ne, no = grid[2], len(extras), len(outs)

    def body(a_ref, b_ref, *refs):
        e_refs, o_refs = refs[:ne], refs[ne:ne + no]
        part = lax.dot_general(a_ref[...], b_ref[...], dims, preferred_element_type=F32)

        def finish(acc):
            if epilogue is None:
                o_refs[0][...] = acc.astype(o_refs[0].dtype)
            else:
                epilogue(acc, [r[...] for r in e_refs], o_refs)

        if nk == 1:
            finish(part)
        else:
            acc_ref = refs[ne + no]
            k = pl.program_id(2)

            @pl.when(k == 0)
            def _():
                acc_ref[...] = part

            @pl.when(k > 0)
            def _():
                acc_ref[...] += part

            @pl.when(k == nk - 1)
            def _():
                finish(acc_ref[...])

    res = pl.pallas_call(
        body, name=name, grid=grid,
        in_specs=[a_spec, b_spec] + [s for _, s in extras],
        out_specs=[s for _, s in outs],
        out_shape=[s for s, _ in outs],
        scratch_shapes=[pltpu.VMEM(tile, F32)] if nk > 1 else [],
        compiler_params=_params("parallel", "parallel", "arbitrary"),
    )(a, b, *[e for e, _ in extras])
    return res[0] if no == 1 else res


def _tile_spec(tm, tn, col0=0):
    return pl.BlockSpec((tm, tn), lambda i, j, k: (i, j + col0))


def _out2d(m, n, dtype, tm, tn):
    return (jax.ShapeDtypeStruct((m, n), dtype), _tile_spec(tm, tn))


def _mm_fwd(name, a, w, tm, tn, tk, outs=None, epilogue=None, extras=()):
    m, kdim = a.shape
    if w.ndim == 3:
        per = w.shape[2] // tn
        n = N_SHARD * w.shape[2]
        w_spec = pl.BlockSpec((None, tk, tn), lambda i, j, k: (j // per, k, j % per))
    else:
        n = w.shape[1]
        w_spec = pl.BlockSpec((tk, tn), lambda i, j, k: (k, j))
    if outs is None:
        outs = [_out2d(m, n, F32, tm, tn)]
    return _mm(name, NN, a, pl.BlockSpec((tm, tk), lambda i, j, k: (i, k)), w, w_spec,
               (m // tm, n // tn, kdim // tk), (tm, tn), outs, epilogue, extras)


def _mm_bwd_x(name, g, g_spec, w, tm, tj, tc, m, n_contract, outs=None, epilogue=None, extras=()):
    if w.ndim == 3:
        per = w.shape[2] // tc
        kdim = w.shape[1]
        w_spec = pl.BlockSpec((None, tj, tc), lambda i, j, n: (n // per, j, n % per))
    else:
        kdim = w.shape[0]
        w_spec = pl.BlockSpec((tj, tc), lambda i, j, n: (j, n))
    if outs is None:
        outs = [_out2d(m, kdim, F32, tm, tj)]
    return _mm(name, NT, g, g_spec, w, w_spec, (m // tm, kdim // tj, n_contract // tc),
               (tm, tj), outs, epilogue, extras)


def _mm_bwd_w(name, a, g, g_spec, n, tk, tn, tm, sharded):
    m, kdim = a.shape
    if sharded:
        per = (n // N_SHARD) // tn
        out = (jax.ShapeDtypeStruct((N_SHARD, kdim, n // N_SHARD), F32),
               pl.BlockSpec((None, tk, tn), lambda i, j, mm: (j // per, i, j % per)))
    else:
        out = (jax.ShapeDtypeStruct((kdim, n), F32), pl.BlockSpec((tk, tn), lambda i, j, mm: (i, j)))
    return _mm(name, TN, a, pl.BlockSpec((tm, tk), lambda i, j, mm: (mm, i)), g, g_spec,
               (kdim // tk, n // tn, m // tm), (tk, tn), [out])


def _ew(name, fn, tiles, fulls, outs, sums=(), ts=512):
    tiles = [t if isinstance(t, tuple) else (t, pl.BlockSpec((ts, t.shape[1]), lambda i: (i, 0)))
             for t in tiles]
    s_rows = tiles[0][0].shape[-2]
    nt, nf, no = len(tiles), len(fulls), len(outs)

    def body(*refs):
        t_vals = [r[...] for r in refs[:nt]]
        f_vals = [r[...] for r in refs[nt:nt + nf]]
        o_refs, s_refs = refs[nt + nf:nt + nf + no], refs[nt + nf + no:]
        o_vals, s_vals = fn(t_vals, f_vals)
        for r, v in zip(o_refs, o_vals):
            r[...] = v.astype(r.dtype)
        for r, v in zip(s_refs, s_vals):
            part = jnp.sum(v, axis=0, keepdims=True)

            @pl.when(pl.program_id(0) == 0)
            def _():
                r[...] = part

            @pl.when(pl.program_id(0) > 0)
            def _():
                r[...] += part

    full_specs = [pl.BlockSpec(f.shape, lambda i, nd=f.ndim: (0,) * nd) for f in fulls]
    res = pl.pallas_call(
        body, name=name, grid=(s_rows // ts,),
        in_specs=[s for _, s in tiles] + full_specs,
        out_specs=[pl.BlockSpec((ts, c), lambda i: (i, 0)) for c, _ in outs]
        + [pl.BlockSpec((1, c), lambda i: (0, 0)) for c in sums],
        out_shape=[jax.ShapeDtypeStruct((s_rows, c), dt) for c, dt in outs]
        + [jax.ShapeDtypeStruct((1, c), F32) for c in sums],
        compiler_params=_params("arbitrary"),
    )(*[t for t, _ in tiles], *fulls)
    return res


def _rms_fwd(name, x, g):
    def fn(t, f):
        xv = t[0]
        r = lax.rsqrt(jnp.mean(xv * xv, axis=-1, keepdims=True) + EPS)
        return [xv * r * f[0]], []
    return _ew(name, fn, [x], [g], [(x.shape[1], BF16)])[0]


def _rms_bwd(name, dh, x, g, dres):
    def fn(t, f):
        dhv, xv, dr = t
        r = lax.rsqrt(jnp.mean(xv * xv, axis=-1, keepdims=True) + EPS)
        xhat = xv * r
        u = dhv * f[0]
        dx = r * (u - xhat * jnp.mean(u * xhat, axis=-1, keepdims=True)) + dr
        return [dx, dx], [dhv * xhat]
    c = x.shape[1]
    return _ew(name, fn, [dh, x, dres], [g], [(c, F32), (c, BF16)], sums=[c])


def _split3(x):
    x1 = x.astype(BF16)
    r1 = x - x1.astype(F32)
    x2 = r1.astype(BF16)
    x3 = (r1 - x2.astype(F32)).astype(BF16)
    return x1, x2, x3


def _rel_class(cp):
    far = (cp < MAX_REL) | (cp > BAND)
    return jnp.where(far, 2 * MAX_REL, BAND - cp)


def _skew_rows(x, sign):
    row = lax.broadcasted_iota(jnp.int32, x.shape, 0)
    for b in range(SUPER.bit_length() - 1):
        shift = (1 << b) if sign > 0 else SKEW_W - (1 << b)
        x = jnp.where((row >> b) & 1 == 1, pltpu.roll(x, shift, 1), x)
    return x


def _bias_expand(rel_bias):
    rel = jnp.pad(rel_bias, ((0, 0), (0, REL_PAD - N_REL))).reshape(N_HEADS, 1, REL_PAD)

    def body(rel_ref, o_ref):
        cls = lax.broadcasted_iota(jnp.int32, (REL_PAD, SKEW_W), 0)
        cp = lax.broadcasted_iota(jnp.int32, (REL_PAD, SKEW_W), 1)
        onehot = (cls == _rel_class(cp)).astype(BF16)
        rel8 = jnp.broadcast_to(rel_ref[...], (8, REL_PAD))
        trow = sum(jnp.dot(p, onehot, preferred_element_type=F32) for p in _split3(rel8))[0:1]
        full = _skew_rows(jnp.broadcast_to(trow, (SUPER, SKEW_W)), +1)[:, :BAND]
        qc = lax.broadcasted_iota(jnp.int32, (SUPER, BAND), 0) // CHUNK
        kc = lax.broadcasted_iota(jnp.int32, (SUPER, BAND), 1) // CHUNK
        on_band = (kc >= qc) & (kc <= qc + N_PREV_CHUNKS)
        o_ref[...] = jnp.where(on_band, full, NEG_INF)

    return pl.pallas_call(
        body, name="bias_expand", grid=(N_HEADS,),
        in_specs=[pl.BlockSpec((None, 1, REL_PAD), lambda h: (h, 0, 0))],
        out_specs=pl.BlockSpec((None, SUPER, BAND), lambda h: (h, 0, 0)),
        out_shape=jax.ShapeDtypeStruct((N_HEADS, SUPER, BAND), F32),
        compiler_params=_params("arbitrary"),
    )(rel)


def _bias_reduce(dbias):
    def body(d_ref, o_ref):
        x = jnp.concatenate([d_ref[...], jnp.zeros((SUPER, SKEW_W - BAND), F32)], axis=1)
        diag = jnp.sum(_skew_rows(x, -1), axis=0, keepdims=True)
        cp = lax.broadcasted_iota(jnp.int32, (SKEW_W, REL_PAD), 0)
        cls = lax.broadcasted_iota(jnp.int32, (SKEW_W, REL_PAD), 1)
        onehot = (cls == _rel_class(cp)).astype(BF16)
        diag8 = jnp.broadcast_to(diag, (8, SKEW_W))
        o_ref[...] = sum(jnp.dot(p, onehot, preferred_element_type=F32) for p in _split3(diag8))[0:1]

    out = pl.pallas_call(
        body, name="bias_reduce", grid=(N_HEADS,),
        in_specs=[pl.BlockSpec((None, SUPER, BAND), lambda h: (h, 0, 0))],
        out_specs=pl.BlockSpec((None, 1, REL_PAD), lambda h: (h, 0, 0)),
        out_shape=jax.ShapeDtypeStruct((N_HEADS, 1, REL_PAD), F32),
        compiler_params=_params("arbitrary"),
    )(dbias)
    return out.reshape(N_HEADS, REL_PAD)[:, :N_REL]


def _unit(x):
    r = lax.rsqrt(jnp.mean(x * x, axis=-1, keepdims=True) + EPS)
    return x * r, r


def _probs(qs, kn, bias, dead):
    s = lax.dot_general(qs, kn, NT, preferred_element_type=F32) + bias
    s = jnp.where(dead, NEG_INF, s)
    e = jnp.exp(s - jnp.max(s, axis=-1, keepdims=True))
    return e / jnp.sum(e, axis=-1, keepdims=True)


def _band_specs(nb, col0, clamp_hi):
    def spec(d):
        def index(hp, i):
            blk = jnp.maximum(i - d, 0)
            if clamp_hi:
                blk = jnp.minimum(blk, nb - 1)
            return (blk, col0 + hp)
        return pl.BlockSpec((SUPER, LANE), index)
    return [spec(2), spec(1), spec(0)]


def _dead_keys(i):
    col = lax.broadcasted_iota(jnp.int32, (SUPER, BAND), 1)
    return col < (2 - i) * SUPER


def _attn_fwd(proj, gq, gk, bias):
    s_len = proj.shape[0]
    nb = s_len // SUPER

    def body(q_ref, k0, k1, k2, v0, v1, v2, gq_ref, gk_ref, b_ref, o_ref):
        dead = _dead_keys(pl.program_id(1))
        outs = []
        for hh in range(2):
            sl = slice(HEAD_DIM * hh, HEAD_DIM * (hh + 1))
            qs = (_unit(q_ref[:, sl])[0] * (gq_ref[...] * QK_SCALE)).astype(BF16)
            kraw = jnp.concatenate([k0[:, sl], k1[:, sl], k2[:, sl]], axis=0)
            kn = (_unit(kraw)[0] * gk_ref[...]).astype(BF16)
            vv = jnp.concatenate([v0[:, sl], v1[:, sl], v2[:, sl]], axis=0).astype(BF16)
            p = _probs(qs, kn, b_ref[hh], dead).astype(BF16)
            outs.append(jnp.dot(p, vv, preferred_element_type=F32))
        o_ref[...] = jnp.concatenate(outs, axis=1).astype(o_ref.dtype)

    small = pl.BlockSpec((1, HEAD_DIM), lambda hp, i: (0, 0))
    return pl.pallas_call(
        body, name="attn_fwd", grid=(N_HEADS // 2, nb),
        in_specs=[pl.BlockSpec((SUPER, LANE), lambda hp, i: (i, hp))]
        + _band_specs(nb, 8, False) + _band_specs(nb, 16, False)
        + [small, small, pl.BlockSpec((2, SUPER, BAND), lambda hp, i: (hp, 0, 0))],
        out_specs=pl.BlockSpec((SUPER, LANE), lambda hp, i: (i, hp)),
        out_shape=jax.ShapeDtypeStruct((s_len, D_MODEL), BF16),
        compiler_params=_params("parallel", "arbitrary"),
    )(proj, proj, proj, proj, proj, proj, proj, gq, gk, bias)


def _attn_bwd(proj, d_out, gq, gk, bias):
    s_len = proj.shape[0]
    nb = s_len // SUPER

    def body(q_ref, k0, k1, k2, v0, v1, v2, do_ref, gq_ref, gk_ref, b_ref,
             dp_ref, db_ref, dgq_ref, dgk_ref, aq_ref, ak_ref, av_ref):
        i = pl.program_id(1)

        @pl.when(i == 0)
        def _():
            aq_ref[...] = jnp.zeros_like(aq_ref)
            ak_ref[...] = jnp.zeros_like(ak_ref)
            av_ref[...] = jnp.zeros_like(av_ref)
            db_ref[...] = jnp.zeros_like(db_ref)
            dgq_ref[...] = jnp.zeros_like(dgq_ref)
            dgk_ref[...] = jnp.zeros_like(dgk_ref)

        @pl.when(i < nb)
        def _():
            dead = _dead_keys(i)
            dq_pair, dk_pair, dv_pair, dgq_pair = [], [], [], []
            for hh in range(2):
                sl = slice(HEAD_DIM * hh, HEAD_DIM * (hh + 1))
                qhat, rq = _unit(q_ref[:, sl])
                qs = (qhat * (gq_ref[...] * QK_SCALE)).astype(BF16)
                kraw = jnp.concatenate([k0[:, sl], k1[:, sl], k2[:, sl]], axis=0)
                kn = (_unit(kraw)[0] * gk_ref[...]).astype(BF16)
                vv = jnp.concatenate([v0[:, sl], v1[:, sl], v2[:, sl]], axis=0).astype(BF16)
                do = do_ref[:, sl]
                p = _probs(qs, kn, b_ref[hh], dead)
                pb = p.astype(BF16)
                dv_pair.append(lax.dot_general(pb, do, TN, preferred_element_type=F32))
                dp = lax.dot_general(do, vv, NT, preferred_element_type=F32)
                ds = p * (dp - jnp.sum(p * dp, axis=-1, keepdims=True))
                db_ref[hh] += ds
                dsb = ds.astype(BF16)
                dqn = jnp.dot(dsb, kn, preferred_element_type=F32) * QK_SCALE
                dk_pair.append(lax.dot_general(dsb, qs, TN, preferred_element_type=F32))
                u = dqn * gq_ref[...]
                dq_pair.append(rq * (u - qhat * jnp.mean(u * qhat, axis=-1, keepdims=True)))
                dgq_pair.append(dqn * qhat)
            aq_ref[i % 3] = jnp.concatenate(dq_pair, axis=1)
            dgq_ref[...] += jnp.sum(jnp.concatenate(dgq_pair, axis=1), axis=0, keepdims=True)
            dk_band = jnp.concatenate(dk_pair, axis=1)
            dv_band = jnp.concatenate(dv_pair, axis=1)
            for j in range(3):
                slot = (i + 1 + j) % 3
                rows = slice(SUPER * j, SUPER * (j + 1))
                if j < 2:
                    ak_ref[slot] += dk_band[rows]
                    av_ref[slot] += dv_band[rows]
                else:
                    ak_ref[slot] = dk_band[rows]
                    av_ref[slot] = dv_band[rows]

        slot = (i + 1) % 3
        dkn = ak_ref[slot]
        dk_out, dgk_out = [], []
        for hh in range(2):
            sl = slice(HEAD_DIM * hh, HEAD_DIM * (hh + 1))
            khat, rk = _unit(k0[:, sl])
            u = dkn[:, sl] * gk_ref[...]
            dk_out.append(rk * (u - khat * jnp.mean(u * khat, axis=-1, keepdims=True)))
            dgk_out.append(dkn[:, sl] * khat)
        dp_ref[0] = aq_ref[slot].astype(dp_ref.dtype)
        dp_ref[1] = jnp.concatenate(dk_out, axis=1).astype(dp_ref.dtype)
        dp_ref[2] = av_ref[slot].astype(dp_ref.dtype)

        @pl.when(i >= 2)
        def _():
            dgk_ref[...] += jnp.sum(jnp.concatenate(dgk_out, axis=1), axis=0, keepdims=True)

    def qrow(hp, i):
        return (jnp.minimum(i, nb - 1), hp)

    small = pl.BlockSpec((1, HEAD_DIM), lambda hp, i: (0, 0))
    part = pl.BlockSpec((None, 1, LANE), lambda hp, i: (hp, 0, 0))
    return pl.pallas_call(
        body, name="attn_bwd", grid=(N_HEADS // 2, nb + 2),
        in_specs=[pl.BlockSpec((SUPER, LANE), qrow)]
        + _band_specs(nb, 8, True) + _band_specs(nb, 16, True)
        + [pl.BlockSpec((SUPER, LANE), qrow), small, small,
           pl.BlockSpec((2, SUPER, BAND), lambda hp, i: (hp, 0, 0))],
        out_specs=[pl.BlockSpec((3, SUPER, LANE), lambda hp, i: (0, jnp.maximum(i - 2, 0), hp)),
                   pl.BlockSpec((2, SUPER, BAND), lambda hp, i: (hp, 0, 0)), part, part],
        out_shape=[jax.ShapeDtypeStruct((6, s_len, D_MODEL), BF16),
                   jax.ShapeDtypeStruct((N_HEADS, SUPER, BAND), F32),
                   jax.ShapeDtypeStruct((N_HEADS // 2, 1, LANE), F32),
                   jax.ShapeDtypeStruct((N_HEADS // 2, 1, LANE), F32)],
        scratch_shapes=[pltpu.VMEM((3, SUPER, LANE), F32)] * 3,
        compiler_params=_params("parallel", "arbitrary"),
    )(proj, proj, proj, proj, proj, proj, proj, d_out, gq, gk, bias)


CONV_ROWS = 512
HALO = 8


def _rows_with_halo(ref, r0, n, front, s_len):
    zeros = jnp.zeros((HALO, ref.shape[1]), F32)
    if front:
        return (jnp.concatenate([zeros, ref[0:n, :]], axis=0) if r0 == 0
                else ref[r0 - HALO:r0 + n, :])
    return (jnp.concatenate([ref[r0:r0 + n, :], zeros], axis=0) if r0 + n == s_len
            else ref[r0:r0 + n + HALO, :])


def _earlier(ext, k):
    return pltpu.roll(ext, k, 0)[HALO:]


def _later(ext, k):
    n = ext.shape[0]
    return pltpu.roll(ext, n - k, 0)[:n - HALO]


def _conv_cols(col0):
    return lambda s_len: pl.BlockSpec((s_len, LANE), lambda j: (0, col0 + j))


def _conv_fwd(proj, conv_w, conv_b):
    s_len = proj.shape[0]

    def body(bg_ref, cg_ref, xc_ref, w_ref, b_ref, o_ref):
        w = [w_ref[t:t + 1, :] for t in range(3)]
        for r0 in range(0, s_len, CONV_ROWS):
            u = _rows_with_halo(cg_ref, r0, CONV_ROWS, True, s_len) * \
                _rows_with_halo(xc_ref, r0, CONV_ROWS, True, s_len)
            conv = b_ref[...] + w[0] * _earlier(u, 2) + w[1] * _earlier(u, 1) + w[2] * u[HALO:]
            o_ref[r0:r0 + CONV_ROWS, :] = (bg_ref[r0:r0 + CONV_ROWS, :] * conv).astype(o_ref.dtype)

    return pl.pallas_call(
        body, name="conv_fwd", grid=(D_MODEL // LANE,),
        in_specs=[_conv_cols(24)(s_len), _conv_cols(32)(s_len), _conv_cols(40)(s_len),
                  pl.BlockSpec((3, LANE), lambda j: (0, j)), pl.BlockSpec((1, LANE), lambda j: (0, j))],
        out_specs=pl.BlockSpec((s_len, LANE), lambda j: (0, j)),
        out_shape=jax.ShapeDtypeStruct((s_len, D_MODEL), BF16),
        compiler_params=_params("parallel"),
    )(proj, proj, proj, conv_w, conv_b)


def _conv_bwd(dproj6, dy, proj, conv_w, conv_b):
    s_len = proj.shape[0]

    def body(dy_ref, bg_ref, cg_ref, xc_ref, w_ref, b_ref, _, dp_ref, dw_ref):
        w = [w_ref[t:t + 1, :] for t in range(3)]
        acc = [jnp.zeros((1, LANE), F32) for _ in range(4)]
        for r0 in range(0, s_len, CONV_ROWS):
            rows = slice(r0, r0 + CONV_ROWS)
            u = _rows_with_halo(cg_ref, r0, CONV_ROWS, True, s_len) * \
                _rows_with_halo(xc_ref, r0, CONV_ROWS, True, s_len)
            u2, u1, u0 = _earlier(u, 2), _earlier(u, 1), u[HALO:]
            conv = b_ref[...] + w[0] * u2 + w[1] * u1 + w[2] * u0
            dyv = dy_ref[rows, :]
            dp_ref[0, rows, :] = (dyv * conv).astype(dp_ref.dtype)
            dconv_ext = _rows_with_halo(dy_ref, r0, CONV_ROWS, False, s_len) * \
                _rows_with_halo(bg_ref, r0, CONV_ROWS, False, s_len)
            dconv = dconv_ext[:CONV_ROWS]
            for t, term in enumerate([dconv * u2, dconv * u1, dconv * u0, dconv]):
                acc[t] = acc[t] + jnp.sum(term, axis=0, keepdims=True)
            du = w[2] * dconv + w[1] * _later(dconv_ext, 1) + w[0] * _later(dconv_ext, 2)
            dp_ref[1, rows, :] = (du * xc_ref[rows, :]).astype(dp_ref.dtype)
            dp_ref[2, rows, :] = (du * cg_ref[rows, :]).astype(dp_ref.dtype)
        dw_ref[...] = jnp.zeros_like(dw_ref)
        for t in range(4):
            dw_ref[t:t + 1, :] = acc[t]

    return pl.pallas_call(
        body, name="conv_bwd", grid=(D_MODEL // LANE,),
        in_specs=[pl.BlockSpec((s_len, LANE), lambda j: (0, j)),
                  _conv_cols(24)(s_len), _conv_cols(32)(s_len), _conv_cols(40)(s_len),
                  pl.BlockSpec((3, LANE), lambda j: (0, j)), pl.BlockSpec((1, LANE), lambda j: (0, j)),
                  pl.BlockSpec(memory_space=pl.ANY)],
        out_specs=[pl.BlockSpec((3, s_len, LANE), lambda j: (1, 0, j)),
                   pl.BlockSpec((8, LANE), lambda j: (0, j))],
        out_shape=[jax.ShapeDtypeStruct(dproj6.shape, dproj6.dtype),
                   jax.ShapeDtypeStruct((8, D_MODEL), F32)],
        input_output_aliases={6: 0},
        compiler_params=_params("parallel"),
    )(dy, proj, proj, proj, conv_w, conv_b, dproj6)


def _local_grads(x, target, norm1_g, w_in3, q_norm_g, k_norm_g, rel_bias, conv_w, conv_b,
                 w_ap, w_cp, w_gate3, b_gate, w_out, norm2_g, w_up3, w_down):
    s_len = x.shape[0]
    tm = min(1024, s_len)
    row = lambda v: v.reshape(1, -1)

    h = _rms_fwd("norm1", x, row(norm1_g))
    proj = _mm_fwd("proj", h, w_in3, tm, 512, D_MODEL)

    def gate_epi(acc, e, o):
        o[0][...] = jax.nn.sigmoid(acc + e[0])
    gates = _mm_fwd("gates", h, w_gate3, tm, 512, D_MODEL, epilogue=gate_epi,
                    extras=[(row(b_gate), pl.BlockSpec((1, 512), lambda i, j, k: (0, j)))])

    bias = _bias_expand(rel_bias)
    gq, gk = row(q_norm_g), row(k_norm_g)
    attn = _attn_fwd(proj, gq, gk, bias)
    yconv = _conv_fwd(proj, conv_w, row(conv_b))
    ya = _mm_fwd("attn_proj", attn, w_ap, tm, 512, D_MODEL)

    def merge_epi(acc, e, o):
        ya_v, ga, gc = e
        o[0][...] = acc
        o[1][...] = (ga * ya_v + gc * acc).astype(BF16)
    gate_a, gate_c = (gates, _tile_spec(tm, 512, 0)), (gates, _tile_spec(tm, 512, 2))
    yc, merged = _mm_fwd("conv_proj", yconv, w_cp, tm, 512, D_MODEL, epilogue=merge_epi,
                         outs=[_out2d(s_len, D_MODEL, F32, tm, 512), _out2d(s_len, D_MODEL, BF16, tm, 512)],
                         extras=[(ya, _tile_spec(tm, 512)), gate_a, gate_c])

    def res_epi(acc, e, o):
        o[0][...] = e[0] + acc
    x1 = _mm_fwd("out_proj", merged, w_out, tm, 512, D_MODEL, epilogue=res_epi,
                 extras=[(x, _tile_spec(tm, 512))])
    h2 = _rms_fwd("norm2", x1, row(norm2_g))

    def up_epi(acc, e, o):
        o[0][...] = acc.astype(BF16)
        o[1][...] = jnp.square(jnp.maximum(acc, 0.0)).astype(BF16)
    up, act = _mm_fwd("mlp_up", h2, w_up3, tm, 512, D_MODEL, epilogue=up_epi,
                      outs=[_out2d(s_len, D_FF, BF16, tm, 512), _out2d(s_len, D_FF, BF16, tm, 512)])

    def loss_epi(acc, e, o):
        err = e[0] + acc - e[1]
        dy = err * (1.0 / D_MODEL)
        o[0][...] = dy
        o[1][...] = dy.astype(BF16)
        sq = err * err
        part = sq[:, 0:LANE]
        for c0 in range(LANE, D_MODEL, LANE):
            part = part + sq[:, c0:c0 + LANE]
        o[2][...] = jnp.sum(part.reshape(tl // 8, 8, LANE), axis=0)
    tl = 512
    dy, dy_b, loss_part = _mm_fwd(
        "mlp_down", act, w_down, tl, D_MODEL, 1024, epilogue=loss_epi,
        outs=[_out2d(s_len, D_MODEL, F32, tl, D_MODEL), _out2d(s_len, D_MODEL, BF16, tl, D_MODEL),
              (jax.ShapeDtypeStruct((8 * (s_len // tl), LANE), F32), pl.BlockSpec((8, LANE), lambda i, j, k: (i, 0)))],
        extras=[(x1, _tile_spec(tl, D_MODEL)), (target, _tile_spec(tl, D_MODEL))])

    def dup_epi(acc, e, o):
        o[0][...] = (acc * (2.0 * jnp.maximum(e[0].astype(F32), 0.0))).astype(BF16)
    dup = _mm_bwd_x("d_act", dy_b, pl.BlockSpec((tm, D_MODEL), lambda i, j, n: (i, n)), w_down,
                    tm, 512, D_MODEL, s_len, D_MODEL, epilogue=dup_epi,
                    outs=[_out2d(s_len, D_FF, BF16, tm, 512)], extras=[(up, _tile_spec(tm, 512))])
    g_down = _mm_bwd_w("g_down", act, dy_b, pl.BlockSpec((tm, 512), lambda i, j, m: (m, j)),
                       D_MODEL, 1024, 512, tm, False)
    g_up = _mm_bwd_w("g_up", h2, dup, pl.BlockSpec((tm, 512), lambda i, j, m: (m, j)),
                     D_FF, 1024, 512, tm, True)
    dh2 = _mm_bwd_x("d_h2", dup, pl.BlockSpec((tm, 1024), lambda i, j, n: (i, n)), w_up3,
                    tm, 512, 1024, s_len, D_FF)
    dx1, dx1_b, dg2 = _rms_bwd("norm2_bwd", dh2, x1, row(norm2_g), dy)

    def dmerge_epi(acc, e, o):
        ya_v, yc_v, ga, gc = e
        o[0][...] = (acc * ga).astype(BF16)
        o[1][...] = (acc * gc).astype(BF16)
        o[2][0] = (acc * ya_v * ga * (1.0 - ga)).astype(BF16)
        o[2][1] = (acc * yc_v * gc * (1.0 - gc)).astype(BF16)
    dya, dyc, dgp2 = _mm_bwd_x(
        "d_merged", dx1_b, pl.BlockSpec((tm, D_MODEL), lambda i, j, n: (i, n)), w_out,
        tm, 512, D_MODEL, s_len, D_MODEL, epilogue=dmerge_epi,
        outs=[_out2d(s_len, D_MODEL, BF16, tm, 512), _out2d(s_len, D_MODEL, BF16, tm, 512),
              (jax.ShapeDtypeStruct((2, s_len, D_MODEL), BF16), pl.BlockSpec((2, tm, 512), lambda i, j, n: (0, i, j)))],
        extras=[(ya, _tile_spec(tm, 512)), (yc, _tile_spec(tm, 512)), gate_a, gate_c])
    g_out = _mm_bwd_w("g_out", merged, dx1_b, pl.BlockSpec((tm, 512), lambda i, j, m: (m, j)),
                      D_MODEL, 1024, 512, tm, False)
    d_attn = _mm_bwd_x("d_attn", dya, pl.BlockSpec((tm, D_MODEL), lambda i, j, n: (i, n)), w_ap,
                       tm, 512, D_MODEL, s_len, D_MODEL, outs=[_out2d(s_len, D_MODEL, BF16, tm, 512)])
    g_ap = _mm_bwd_w("g_attn_proj", attn, dya, pl.BlockSpec((tm, 512), lambda i, j, m: (m, j)),
                     D_MODEL, 1024, 512, tm, False)
    d_yconv = _mm_bwd_x("d_yconv", dyc, pl.BlockSpec((tm, D_MODEL), lambda i, j, n: (i, n)), w_cp,
                        tm, 512, D_MODEL, s_len, D_MODEL)
    g_cp = _mm_bwd_w("g_conv_proj", yconv, dyc, pl.BlockSpec((tm, 512), lambda i, j, m: (m, j)),
                     D_MODEL, 1024, 512, tm, False)

    dproj6, dbias, dgq_p, dgk_p = _attn_bwd(proj, d_attn, gq, gk, bias)
    dproj6, dconv_wb = _conv_bwd(dproj6, d_yconv, proj, conv_w, row(conv_b))
    d_rel = _bias_reduce(dbias)

    piece = lambda width: (lambda blk: (blk * width) // D_MODEL, lambda blk: (blk * width % D_MODEL) // width)
    pc, cb = piece(512)
    g_in = _mm_bwd_w("g_in", h, dproj6, pl.BlockSpec((None, tm, 512), lambda i, j, m: (pc(j), m, cb(j))),
                     6 * D_MODEL, 1024, 512, tm, True)
    g_gate = _mm_bwd_w("g_gate", h, dgp2, pl.BlockSpec((None, tm, 512), lambda i, j, m: (pc(j), m, cb(j))),
                       2 * D_MODEL, 1024, 512, tm, True)
    dh_a = _mm_bwd_x("d_h_proj", dproj6, pl.BlockSpec((None, tm, 512), lambda i, j, n: (pc(n), i, cb(n))),
                     w_in3, tm, 512, 512, s_len, 6 * D_MODEL)

    def add_epi(acc, e, o):
        o[0][...] = acc + e[0]
    dh = _mm_bwd_x("d_h_gate", dgp2, pl.BlockSpec((None, tm, 512), lambda i, j, n: (pc(n), i, cb(n))),
                   w_gate3, tm, 512, 512, s_len, 2 * D_MODEL, epilogue=add_epi,
                   extras=[(dh_a, _tile_spec(tm, 512))])
    grad_x, _, dg1 = _rms_bwd("norm1_bwd", dh, x, row(norm1_g), dx1)

    def bsum(t, f):
        return [], [t[0].astype(F32), t[1].astype(F32)]
    ts = 512
    db_a, db_c = _ew("b_gate_sum", bsum,
                     [(dgp2, pl.BlockSpec((None, ts, D_MODEL), lambda i: (0, i, 0))),
                      (dgp2, pl.BlockSpec((None, ts, D_MODEL), lambda i: (1, i, 0)))],
                     [], [], sums=[D_MODEL, D_MODEL], ts=ts)

    big = dict(w_in=g_in, w_attn_proj=g_ap, w_conv_proj=g_cp, w_gate=g_gate, w_out=g_out,
               w_up=g_up, w_down=g_down)
    small = dict(norm1_g=dg1, norm2_g=dg2, conv_wb=dconv_wb, b_gate=(db_a, db_c),
                 q_norm_g=dgq_p.reshape(N_HEADS // 2, LANE), k_norm_g=dgk_p.reshape(N_HEADS // 2, LANE),
                 rel_bias=d_rel)
    return loss_part, grad_x, big, small


def _finish_small(loss_part, dgq_p, dgk_p):
    def body(l_ref, q_ref, k_ref, lo_ref, qo_ref, ko_ref):
        total = jnp.sum(jnp.sum(l_ref[...], axis=0, keepdims=True), axis=1, keepdims=True)
        lo_ref[...] = jnp.broadcast_to(total * (0.5 / D_MODEL), lo_ref.shape)
        for src, dst in ((q_ref, qo_ref), (k_ref, ko_ref)):
            s = jnp.sum(src[...], axis=0, keepdims=True)
            dst[...] = s[:, :HEAD_DIM] + s[:, HEAD_DIM:]

    return pl.pallas_call(
        body, name="finish_small",
        out_shape=[jax.ShapeDtypeStruct((8, LANE), F32), jax.ShapeDtypeStruct((1, HEAD_DIM), F32),
                   jax.ShapeDtypeStruct((1, HEAD_DIM), F32)],
    )(loss_part, dgq_p, dgk_p)


HBM_SPEC = pl.BlockSpec(memory_space=pl.ANY)


def _place():
    return lax.axis_index("x"), lax.axis_index("y"), lax.axis_index("c")


def _other_chips(x, y):
    return [(1 - x, y), (x, 1 - y), (1 - x, 1 - y)]


def _cast_into_slot(name, where, w):
    r, cols = w.shape
    ts = 256

    def body(w_ref, x_ref, o_ref):
        o_ref[...] = x_ref[...].astype(o_ref.dtype)

    return pl.pallas_call(
        body, name=name,
        grid_spec=pltpu.PrefetchScalarGridSpec(
            num_scalar_prefetch=1, grid=(r // ts,),
            in_specs=[pl.BlockSpec((ts, cols), lambda i, w: (i, 0))],
            out_specs=pl.BlockSpec((None, ts, cols), lambda i, w: (w[0], i, 0))),
        out_shape=jax.ShapeDtypeStruct((N_SHARD, r, cols), BF16),
        compiler_params=_params("parallel"),
    )(where, w)


def _all_gather(slots, small):
    n = len(slots)

    def body(*refs):
        outs, small_out = refs[n + 1:2 * n + 1], refs[2 * n + 1]
        send1, recv1, send2, recv2, send3, recv3 = refs[2 * n + 2:]
        x, y, c = _place()
        me = 2 * x + y
        chips = _other_chips(x, y)
        sibling = (x, y, 1 - c)

        def half(w, shard, which):
            hr = slots[w].shape[1] // 2
            return outs[w].at[shard, pl.ds(which * hr, hr)]

        def remote(ref, send, recv, k, to):
            return pltpu.make_async_remote_copy(src_ref=ref, dst_ref=ref, send_sem=send.at[k],
                                                recv_sem=recv.at[k], device_id=to, device_id_type=MESH)

        sent = [remote(half(w, me, c), send1, recv1, 3 * w + j, (*chip, c))
                for w in range(n) for j, chip in enumerate(chips)]
        sent += [remote(small_out.at[me], send3, recv3, j, (*chip, c)) for j, chip in enumerate(chips)]
        for cp in sent:
            cp.start()
        for w in range(n):
            for j, chip in enumerate(chips):
                landed = half(w, 2 * chip[0] + chip[1], c)
                remote(landed, send1, recv1, 3 * w + j, (*chip, c)).wait_recv()
                fwd = remote(landed, send2, recv2, 3 * w + j, sibling)
                fwd.start()
                sent.append(fwd)
        for w in range(n):
            for j, chip in enumerate(chips):
                remote(half(w, 2 * chip[0] + chip[1], 1 - c), send2, recv2, 3 * w + j, sibling).wait_recv()
        for j, chip in enumerate(chips):
            remote(small_out.at[2 * chip[0] + chip[1]], send3, recv3, j, (*chip, c)).wait_recv()
        for cp in sent:
            cp.wait_send()

    return pl.pallas_call(
        body, name="gather_weights",
        in_specs=[HBM_SPEC] * (n + 1), out_specs=[HBM_SPEC] * (n + 1),
        out_shape=[jax.ShapeDtypeStruct(s.shape, s.dtype) for s in slots]
        + [jax.ShapeDtypeStruct(small.shape, small.dtype)],
        input_output_aliases={k: k for k in range(n + 1)},
        scratch_shapes=[pltpu.SemaphoreType.DMA((3 * n,))] * 4 + [pltpu.SemaphoreType.DMA((3,))] * 2,
    )(*slots, small)


def _pair_exchange(grads):
    n = len(grads)

    def body(*refs):
        ins, outs = refs[:n], refs[n:2 * n]
        send, recv = refs[2 * n:]
        x, y, c = _place()
        copies = []
        for w in range(n):
            hr = grads[w].shape[1] // 2
            copies.append(pltpu.make_async_remote_copy(
                src_ref=ins[w].at[:, pl.ds((1 - c) * hr, hr)], dst_ref=outs[w],
                send_sem=send.at[w], recv_sem=recv.at[w], device_id=(x, y, 1 - c), device_id_type=MESH))
        for cp in copies:
            cp.start()
        for cp in copies:
            cp.wait()

    return pl.pallas_call(
        body, name="pair_exchange",
        in_specs=[HBM_SPEC] * n, out_specs=[HBM_SPEC] * n,
        out_shape=[jax.ShapeDtypeStruct((N_SHARD, g.shape[1] // 2, g.shape[2]), g.dtype) for g in grads],
        scratch_shapes=[pltpu.SemaphoreType.DMA((n,))] * 2,
    )(*grads)


def _row_tile(hr):
    return min(hr, 256)


def _pair_add(name, where, grad, got):
    _, hr, cols = got.shape
    tr = _row_tile(hr)
    nblk = hr // tr

    def body(w_ref, g_ref, r_ref, o_ref):
        o_ref[...] = (g_ref[...] + r_ref[...]).astype(o_ref.dtype)

    return pl.pallas_call(
        body, name=name,
        grid_spec=pltpu.PrefetchScalarGridSpec(
            num_scalar_prefetch=1, grid=(N_SHARD, nblk),
            in_specs=[pl.BlockSpec((None, tr, cols), lambda s, i, w: (s, w[1] * nblk + i, 0)),
                      pl.BlockSpec((None, tr, cols), lambda s, i, w: (s, i, 0))],
            out_specs=pl.BlockSpec((None, tr, cols), lambda s, i, w: (s, i, 0))),
        out_shape=jax.ShapeDtypeStruct(got.shape, BF16),
        compiler_params=_params("parallel", "parallel"),
    )(where, grad, got)


def _chip_exchange(partials):
    n = len(partials)

    def body(*refs):
        ins, outs = refs[:n], refs[n:2 * n]
        send, recv = refs[2 * n:]
        x, y, c = _place()
        copies = []
        for w in range(n):
            for j, chip in enumerate(_other_chips(x, y)):
                copies.append(pltpu.make_async_remote_copy(
                    src_ref=ins[w].at[2 * chip[0] + chip[1]], dst_ref=outs[w].at[j],
                    send_sem=send.at[3 * w + j], recv_sem=recv.at[3 * w + j],
                    device_id=(*chip, c), device_id_type=MESH))
        for cp in copies:
            cp.start()
        for cp in copies:
            cp.wait()

    return pl.pallas_call(
        body, name="chip_exchange",
        in_specs=[HBM_SPEC] * n, out_specs=[HBM_SPEC] * n,
        out_shape=[jax.ShapeDtypeStruct((3,) + p.shape[1:], p.dtype) for p in partials],
        scratch_shapes=[pltpu.SemaphoreType.DMA((3 * n,))] * 2,
    )(*partials)


def _final_add(name, where, grad, got, arrived):
    _, hr, cols = got.shape
    tr = _row_tile(hr)
    nblk = hr // tr

    def body(w_ref, g_ref, r_ref, a_ref, o_ref):
        acc = g_ref[...] + r_ref[...]
        for j in range(3):
            acc = acc + a_ref[j].astype(F32)
        o_ref[...] = acc

    return pl.pallas_call(
        body, name=name,
        grid_spec=pltpu.PrefetchScalarGridSpec(
            num_scalar_prefetch=1, grid=(nblk,),
            in_specs=[pl.BlockSpec((None, tr, cols), lambda i, w: (w[0], w[1] * nblk + i, 0)),
                      pl.BlockSpec((None, tr, cols), lambda i, w: (w[0], i, 0)),
                      pl.BlockSpec((3, tr, cols), lambda i, w: (0, i, 0))],
            out_specs=pl.BlockSpec((tr, cols), lambda i, w: (w[1] * nblk + i, 0))),
        out_shape=jax.ShapeDtypeStruct((2 * hr, cols), F32),
        compiler_params=_params("parallel"),
    )(where, grad, got, arrived)


def _pair_share(shards):
    n = len(shards)

    def body(*refs):
        outs = refs[n:2 * n]
        send, recv = refs[2 * n:]
        x, y, c = _place()

        def half(w, which):
            hr = shards[w].shape[0] // 2
            return outs[w].at[pl.ds(which * hr, hr)]

        def remote(ref, w):
            return pltpu.make_async_remote_copy(src_ref=ref, dst_ref=ref, send_sem=send.at[w], recv_sem=recv.at[w],
                                                device_id=(x, y, 1 - c), device_id_type=MESH)

        copies = [remote(half(w, c), w) for w in range(n)]
        for cp in copies:
            cp.start()
        for w in range(n):
            remote(half(w, 1 - c), w).wait_recv()
        for cp in copies:
            cp.wait_send()

    return pl.pallas_call(
        body, name="pair_share",
        in_specs=[HBM_SPEC] * n, out_specs=[HBM_SPEC] * n,
        out_shape=[jax.ShapeDtypeStruct(s.shape, s.dtype) for s in shards],
        input_output_aliases={k: k for k in range(n)},
        scratch_shapes=[pltpu.SemaphoreType.DMA((n,))] * 2,
    )(*shards)


SMALL_ROWS = 32
N_DEV = 8


def _all_reduce_small(pack):
    def body(p_ref, o_ref, buf, send, recv):
        x, y, c = _place()
        buf[4 * x + 2 * y + c] = p_ref[...]
        copies, waits = [], []
        for k in range(1, N_DEV):
            px = 1 - x if k & 4 else x
            py = 1 - y if k & 2 else y
            pc = 1 - c if k & 1 else c
            copies.append(pltpu.make_async_remote_copy(
                src_ref=p_ref, dst_ref=buf.at[4 * x + 2 * y + c], send_sem=send.at[k - 1],
                recv_sem=recv.at[k - 1], device_id=(px, py, pc), device_id_type=MESH))
            waits.append(pltpu.make_async_remote_copy(
                src_ref=p_ref, dst_ref=buf.at[4 * px + 2 * py + pc], send_sem=send.at[k - 1],
                recv_sem=recv.at[k - 1], device_id=(px, py, pc), device_id_type=MESH))
        for cp in copies:
            cp.start()
        for cp in waits:
            cp.wait_recv()
        acc = buf[0]
        for d in range(1, N_DEV):
            acc = acc + buf[d]
        o_ref[...] = acc
        for cp in copies:
            cp.wait_send()

    return pl.pallas_call(
        body, name="all_reduce_small",
        out_shape=jax.ShapeDtypeStruct(pack.shape, F32),
        scratch_shapes=[pltpu.VMEM((N_DEV,) + pack.shape, F32),
                        pltpu.SemaphoreType.DMA((N_DEV - 1,)), pltpu.SemaphoreType.DMA((N_DEV - 1,))],
    )(pack)


def _adamw(name, w, g, m, v):
    c1 = 1.0 - ADAM_B1 ** ADAM_STEP
    c2 = 1.0 - ADAM_B2 ** ADAM_STEP

    def fn(t, f):
        wv, gv, mv, vv = t
        m2 = ADAM_B1 * mv + (1.0 - ADAM_B1) * gv
        v2 = ADAM_B2 * vv + (1.0 - ADAM_B2) * (gv * gv)
        delta = -ADAM_LR * ((m2 / c1) / (jnp.sqrt(v2 / c2) + ADAM_EPS) + ADAM_WD * wv)
        return [delta, m2, v2], []

    cols = w.shape[1]
    return _ew(name, fn, [w, g, m, v], [], [(cols, F32)] * 3, ts=min(w.shape[0], 256))


def _pack_small(norm1_g, norm2_g, conv_b, b_gate, conv_w, q_norm_g, k_norm_g, rel_bias):
    pack = jnp.zeros((SMALL_ROWS, D_MODEL), F32)
    for r0, v in ((0, norm1_g), (1, norm2_g), (2, conv_b), (3, b_gate.reshape(2, D_MODEL)), (5, conv_w),
                  (8, q_norm_g), (9, k_norm_g), (10, rel_bias)):
        v = v.reshape(-1, v.shape[-1]).astype(F32)
        pack = pack.at[r0:r0 + v.shape[0], :v.shape[1]].set(v)
    return pack


def _unpack_small(pack, conv_cols):
    return dict(norm1_g=pack[0], norm2_g=pack[1], conv_b=pack[2], b_gate=pack[3:5].reshape(2 * D_MODEL),
                conv_w=pack[5:8, :conv_cols], q_norm_g=pack[8, :HEAD_DIM], k_norm_g=pack[9, :HEAD_DIM],
                rel_bias=pack[10:10 + N_HEADS, :N_REL])


BIG = ["w_in", "w_attn_proj", "w_conv_proj", "w_gate", "w_out", "w_up", "w_down"]
WEIGHTS = ["norm1_g", "w_in", "q_norm_g", "k_norm_g", "rel_bias", "conv_w", "conv_b", "w_attn_proj",
           "w_conv_proj", "w_gate", "b_gate", "w_out", "norm2_g", "w_up", "w_down"]


def kernel(x, norm1_g, w_in, q_norm_g, k_norm_g, rel_bias, conv_w, conv_b, w_attn_proj, w_conv_proj, w_gate, b_gate, w_out, norm2_g, w_up, w_down, loss_target, m_norm1_g, m_w_in, m_q_norm_g, m_k_norm_g, m_rel_bias, m_conv_w, m_conv_b, m_w_attn_proj, m_w_conv_proj, m_w_gate, m_b_gate, m_w_out, m_norm2_g, m_w_up, m_w_down, v_norm1_g, v_w_in, v_q_norm_g, v_k_norm_g, v_rel_bias, v_conv_w, v_conv_b, v_w_attn_proj, v_w_conv_proj, v_w_gate, v_b_gate, v_w_out, v_norm2_g, v_w_up, v_w_down):
    given = dict(locals())
    w = {n: given[n] for n in WEIGHTS}
    m = {n: given["m_" + n] for n in WEIGHTS}
    v = {n: given["v_" + n] for n in WEIGHTS}
    s_len = x.shape[1]
    shard = 2 * lax.axis_index("x") + lax.axis_index("y")
    where = jnp.stack([shard, lax.axis_index("c")]).astype(jnp.int32)
    conv_cols = conv_w.shape[1]

    small_in = lax.dynamic_update_slice(jnp.zeros((N_SHARD, 8, conv_cols), F32), conv_w[None], (shard, 0, 0))
    *gathered, conv_w4 = _all_gather([_cast_into_slot("cast_" + n, where, w[n]) for n in BIG], small_in)
    wg = dict(zip(BIG, gathered))
    conv_w_full = conv_w4[:, :3, :].transpose(1, 0, 2).reshape(3, N_SHARD * conv_cols)
    rows = lambda a: a.reshape(a.shape[0] * a.shape[1], a.shape[2])

    loss_part, grad_x, big, small = _local_grads(
        x.reshape(s_len, D_MODEL), loss_target.reshape(s_len, D_MODEL), norm1_g, wg["w_in"], q_norm_g,
        k_norm_g, rel_bias, conv_w_full, conv_b, rows(wg["w_attn_proj"]), rows(wg["w_conv_proj"]),
        wg["w_gate"], b_gate, rows(wg["w_out"]), norm2_g, wg["w_up"], rows(wg["w_down"]))

    g4 = [big[n] if big[n].ndim == 3 else big[n].reshape(N_SHARD, -1, big[n].shape[1]) for n in BIG]
    got = _pair_exchange(g4)
    partial = [_pair_add("pair_add_" + n, where, g, r) for n, g, r in zip(BIG, g4, got)]
    arrived = _chip_exchange(partial)
    halves = [_final_add("final_add_" + n, where, g, r, a) for n, g, r, a in zip(BIG, g4, got, arrived)]
    grad = dict(zip(BIG, _pair_share(halves)))

    loss_local, dgq, dgk = _finish_small(loss_part, small["q_norm_g"], small["k_norm_g"])
    pack = _pack_small(small["norm1_g"], small["norm2_g"], small["conv_wb"][3], jnp.concatenate(small["b_gate"], axis=1),
                       small["conv_wb"][0:3], dgq, dgk, small["rel_bias"])
    total = _all_reduce_small(pack)
    g_small = _unpack_small(total, D_MODEL)
    g_small["conv_w"] = lax.dynamic_slice(g_small["conv_w"], (0, shard * conv_cols), (3, conv_cols))
    grad.update(g_small)

    delta, new_m, new_v = {}, {}, {}
    for n in BIG:
        delta[n], new_m[n], new_v[n] = _adamw("adamw_" + n, w[n], grad[n], m[n], v[n])
    small_names = [n for n in WEIGHTS if n not in BIG]
    packs = [_pack_small(**{n: src[n] for n in small_names}) for src in (w, grad, m, v)]
    for out, packed in zip((delta, new_m, new_v), _adamw("adamw_small", *packs)):
        out.update({n: a.reshape(w[n].shape) for n, a in _unpack_small(packed, conv_cols).items()})

    loss = lax.psum(loss_local[0, 0], ("x", "y", "c"))
    outs = [loss, grad_x.reshape(x.shape)]
    for group in (grad, delta, new_m, new_v):
        outs += [group[n].reshape(w[n].shape) for n in WEIGHTS]
    return tuple(outs)
```

```python
import functools

import jax
import jax.numpy as jnp
from jax import lax
from jax.experimental import pallas as pl
from jax.experimental.pallas import tpu as pltpu

F32 = jnp.float32
BF16 = jnp.bfloat16

D_MODEL = 1024
N_HEADS = 16
HEAD_DIM = 64
CHUNK = 64
N_PREV_CHUNKS = 8
MAX_REL = 256
D_FF = 4096
N_REL = 2 * MAX_REL + 1
REL_PAD = 640
EPS = 1e-6
NEG_INF = -1e30
QK_SCALE = HEAD_DIM ** -0.5

SUPER = 4 * CHUNK
BAND = SUPER + N_PREV_CHUNKS * CHUNK
SKEW_W = 1024
N_SHARD = 4
LANE = 128
VMEM_LIMIT = 48 * 1024 * 1024

ADAM_LR = 0.001
ADAM_B1 = 0.9
ADAM_B2 = 0.999
ADAM_EPS = 1e-08
ADAM_WD = 0.01
ADAM_STEP = 10

MESH = pl.DeviceIdType.MESH
NN = (((1,), (0,)), ((), ()))
NT = (((1,), (1,)), ((), ()))
TN = (((0,), (0,)), ((), ()))


def _params(*sem):
    return pltpu.CompilerParams(dimension_semantics=sem or None, vmem_limit_bytes=VMEM_LIMIT)


def _mm(name, dims, a, a_spec, b, b_spec, grid, tile, outs, epilogue=None, extras=()):
    nk, ne, no = grid[2], len(extras), len(outs)

    def body(a_ref, b_ref, *refs):
        e_refs, o_refs = refs[:ne], refs[ne:ne + no]
        part = lax.dot_general(a_ref[...], b_ref[...], dims, preferred_element_type=F32)

        def finish(acc):
            if epilogue is None:
                o_refs[0][...] = acc.astype(o_refs[0].dtype)
            else:
                epilogue(acc, [r[...] for r in e_refs], o_refs)

        if nk == 1:
            finish(part)
        else:
            acc_ref = refs[ne + no]
            k = pl.program_id(2)

            @pl.when(k == 0)
            def _():
                acc_ref[...] = part

            @pl.when(k > 0)
            def _():
                acc_ref[...] += part

            @pl.when(k == nk - 1)
            def _():
                finish(acc_ref[...])

    res = pl.pallas_call(
        body, name=name, grid=grid,
        in_specs=[a_spec, b_spec] + [s for _, s in extras],
        out_specs=[s for _, s in outs],
        out_shape=[s for s, _ in outs],
        scratch_shapes=[pltpu.VMEM(tile, F32)] if nk > 1 else [],
        compiler_params=_params("parallel", "parallel", "arbitrary"),
    )(a, b, *[e for e, _ in extras])
    return res[0] if no == 1 else res


def _tile_spec(tm, tn, col0=0):
    return pl.BlockSpec((tm, tn), lambda i, j, k: (i, j + col0))


def _out2d(m, n, dtype, tm, tn):
    return (jax.ShapeDtypeStruct((m, n), dtype), _tile_spec(tm, tn))


def _mm_fwd(name, a, w, tm, tn, tk, outs=None, epilogue=None, extras=(), col0=0, ncols=None):
    m, kdim = a.shape
    if w.ndim == 3:
        per = w.shape[2] // tn
        n = ncols or N_SHARD * w.shape[2]
        w_spec = pl.BlockSpec((None, tk, tn), lambda i, j, k: ((j + col0) // per, k, (j + col0) % per))
    else:
        n = ncols or w.shape[1]
        w_spec = pl.BlockSpec((tk, tn), lambda i, j, k: (k, j + col0))
    if outs is None:
        outs = [_out2d(m, n, F32, tm, tn)]
    return _mm(name, NN, a, pl.BlockSpec((tm, tk), lambda i, j, k: (i, k)), w, w_spec,
               (m // tm, n // tn, kdim // tk), (tm, tn), outs, epilogue, extras)


def _mm_bwd_x(name, g, g_spec, w, tm, tj, tc, m, n_contract, outs=None, epilogue=None, extras=()):
    if w.ndim == 3:
        per = w.shape[2] // tc
        kdim = w.shape[1]
        w_spec = pl.BlockSpec((None, tj, tc), lambda i, j, n: (n // per, j, n % per))
    else:
        kdim = w.shape[0]
        w_spec = pl.BlockSpec((tj, tc), lambda i, j, n: (j, n))
    if outs is None:
        outs = [_out2d(m, kdim, F32, tm, tj)]
    return _mm(name, NT, g, g_spec, w, w_spec, (m // tm, kdim // tj, n_contract // tc),
               (tm, tj), outs, epilogue, extras)


def _mm_bwd_w(name, a, g, g_spec, n, tk, tn, tm, sharded):
    m, kdim = a.shape
    if sharded:
        per = (n // N_SHARD) // tn
        out = (jax.ShapeDtypeStruct((N_SHARD, kdim, n // N_SHARD), F32),
               pl.BlockSpec((None, tk, tn), lambda i, j, mm: (j // per, i, j % per)))
    else:
        out = (jax.ShapeDtypeStruct((kdim, n), F32), pl.BlockSpec((tk, tn), lambda i, j, mm: (i, j)))
    return _mm(name, TN, a, pl.BlockSpec((tm, tk), lambda i, j, mm: (mm, i)), g, g_spec,
               (kdim // tk, n // tn, m // tm), (tk, tn), [out])


def _ew(name, fn, tiles, fulls, outs, sums=(), ts=512):
    tiles = [t if isinstance(t, tuple) else (t, pl.BlockSpec((ts, t.shape[1]), lambda i: (i, 0)))
             for t in tiles]
    s_rows = tiles[0][0].shape[-2]
    nt, nf, no = len(tiles), len(fulls), len(outs)

    def body(*refs):
        t_vals = [r[...] for r in refs[:nt]]
        f_vals = [r[...] for r in refs[nt:nt + nf]]
        o_refs, s_refs = refs[nt + nf:nt + nf + no], refs[nt + nf + no:]
        o_vals, s_vals = fn(t_vals, f_vals)
        for r, v in zip(o_refs, o_vals):
            r[...] = v.astype(r.dtype)
        for r, v in zip(s_refs, s_vals):
            part = jnp.sum(v, axis=0, keepdims=True)

            @pl.when(pl.program_id(0) == 0)
            def _():
                r[...] = part

            @pl.when(pl.program_id(0) > 0)
            def _():
                r[...] += part

    full_specs = [pl.BlockSpec(f.shape, lambda i, nd=f.ndim: (0,) * nd) for f in fulls]
    res = pl.pallas_call(
        body, name=name, grid=(s_rows // ts,),
        in_specs=[s for _, s in tiles] + full_specs,
        out_specs=[pl.BlockSpec((ts, c), lambda i: (i, 0)) for c, _ in outs]
        + [pl.BlockSpec((1, c), lambda i: (0, 0)) for c in sums],
        out_shape=[jax.ShapeDtypeStruct((s_rows, c), dt) for c, dt in outs]
        + [jax.ShapeDtypeStruct((1, c), F32) for c in sums],
        compiler_params=_params("arbitrary"),
    )(*[t for t, _ in tiles], *fulls)
    return res


def _rms_fwd(name, x, g):
    def fn(t, f):
        xv = t[0]
        r = lax.rsqrt(jnp.mean(xv * xv, axis=-1, keepdims=True) + EPS)
        return [xv * r * f[0]], []
    return _ew(name, fn, [x], [g], [(x.shape[1], BF16)])[0]


def _rms_bwd(name, dh, x, g, dres):
    def fn(t, f):
        dhv, xv, dr = t
        r = lax.rsqrt(jnp.mean(xv * xv, axis=-1, keepdims=True) + EPS)
        xhat = xv * r
        u = dhv * f[0]
        dx = r * (u - xhat * jnp.mean(u * xhat, axis=-1, keepdims=True)) + dr
        return [dx, dx], [dhv * xhat]
    c = x.shape[1]
    return _ew(name, fn, [dh, x, dres], [g], [(c, F32), (c, BF16)], sums=[c])


def _split3(x):
    x1 = x.astype(BF16)
    r1 = x - x1.astype(F32)
    x2 = r1.astype(BF16)
    x3 = (r1 - x2.astype(F32)).astype(BF16)
    return x1, x2, x3


def _rel_class(cp):
    far = (cp < MAX_REL) | (cp > BAND)
    return jnp.where(far, 2 * MAX_REL, BAND - cp)


def _skew_rows(x, sign):
    row = lax.broadcasted_iota(jnp.int32, x.shape, 0)
    for b in range(SUPER.bit_length() - 1):
        shift = (1 << b) if sign > 0 else SKEW_W - (1 << b)
        x = jnp.where((row >> b) & 1 == 1, pltpu.roll(x, shift, 1), x)
    return x


def _bias_expand(rel_bias):
    rel = jnp.pad(rel_bias, ((0, 0), (0, REL_PAD - N_REL))).reshape(N_HEADS, 1, REL_PAD)

    def body(rel_ref, o_ref):
        cls = lax.broadcasted_iota(jnp.int32, (REL_PAD, SKEW_W), 0)
        cp = lax.broadcasted_iota(jnp.int32, (REL_PAD, SKEW_W), 1)
        onehot = (cls == _rel_class(cp)).astype(BF16)
        rel8 = jnp.broadcast_to(rel_ref[...], (8, REL_PAD))
        trow = sum(jnp.dot(p, onehot, preferred_element_type=F32) for p in _split3(rel8))[0:1]
        full = _skew_rows(jnp.broadcast_to(trow, (SUPER, SKEW_W)), +1)[:, :BAND]
        qc = lax.broadcasted_iota(jnp.int32, (SUPER, BAND), 0) // CHUNK
        kc = lax.broadcasted_iota(jnp.int32, (SUPER, BAND), 1) // CHUNK
        on_band = (kc >= qc) & (kc <= qc + N_PREV_CHUNKS)
        o_ref[...] = jnp.where(on_band, full, NEG_INF)

    return pl.pallas_call(
        body, name="bias_expand", grid=(N_HEADS,),
        in_specs=[pl.BlockSpec((None, 1, REL_PAD), lambda h: (h, 0, 0))],
        out_specs=pl.BlockSpec((None, SUPER, BAND), lambda h: (h, 0, 0)),
        out_shape=jax.ShapeDtypeStruct((N_HEADS, SUPER, BAND), F32),
        compiler_params=_params("arbitrary"),
    )(rel)


def _bias_reduce(dbias):
    def body(d_ref, o_ref):
        x = jnp.concatenate([d_ref[...], jnp.zeros((SUPER, SKEW_W - BAND), F32)], axis=1)
        diag = jnp.sum(_skew_rows(x, -1), axis=0, keepdims=True)
        cp = lax.broadcasted_iota(jnp.int32, (SKEW_W, REL_PAD), 0)
        cls = lax.broadcasted_iota(jnp.int32, (SKEW_W, REL_PAD), 1)
        onehot = (cls == _rel_class(cp)).astype(BF16)
        diag8 = jnp.broadcast_to(diag, (8, SKEW_W))
        o_ref[...] = sum(jnp.dot(p, onehot, preferred_element_type=F32) for p in _split3(diag8))[0:1]

    out = pl.pallas_call(
        body, name="bias_reduce", grid=(N_HEADS,),
        in_specs=[pl.BlockSpec((None, SUPER, BAND), lambda h: (h, 0, 0))],
        out_specs=pl.BlockSpec((None, 1, REL_PAD), lambda h: (h, 0, 0)),
        out_shape=jax.ShapeDtypeStruct((N_HEADS, 1, REL_PAD), F32),
        compiler_params=_params("arbitrary"),
    )(dbias)
    return out.reshape(N_HEADS, REL_PAD)[:, :N_REL]


HEADS_PER_STEP = 4
HEAD_COLS = HEADS_PER_STEP * HEAD_DIM
N_HEAD_GROUPS = N_HEADS // HEADS_PER_STEP


def _unit(x):
    r = lax.rsqrt(jnp.mean(x * x, axis=-1, keepdims=True) + EPS)
    return x * r, r


def _probs(qs, kn, bias, dead):
    s = jnp.concatenate([lax.dot_general(qs, k, NT, preferred_element_type=F32) for k in kn], axis=1) + bias
    s = jnp.where(dead, NEG_INF, s)
    e = jnp.exp(s - jnp.max(s, axis=-1, keepdims=True))
    return e / jnp.sum(e, axis=-1, keepdims=True)


def _band_specs(nb, col0, clamp_hi):
    def spec(d):
        def index(hg, i):
            blk = jnp.maximum(i - d, 0)
            if clamp_hi:
                blk = jnp.minimum(blk, nb - 1)
            return (blk, col0 + hg)
        return pl.BlockSpec((SUPER, HEAD_COLS), index)
    return [spec(2), spec(1), spec(0)]


def _head(hh):
    return slice(HEAD_DIM * hh, HEAD_DIM * (hh + 1))


def _key_block(j):
    return slice(SUPER * j, SUPER * (j + 1))


def _dead_keys(i):
    col = lax.broadcasted_iota(jnp.int32, (SUPER, BAND), 1)
    return col < (2 - i) * SUPER


def _attn_fwd(qkn, v, bias):
    s_len = qkn.shape[0]
    nb = s_len // SUPER

    def body(q_ref, k0, k1, k2, v0, v1, v2, b_ref, o_ref):
        dead = _dead_keys(pl.program_id(1))
        outs = []
        for hh in range(HEADS_PER_STEP):
            sl = _head(hh)
            p = _probs(q_ref[:, sl], [k0[:, sl], k1[:, sl], k2[:, sl]], b_ref[hh], dead).astype(BF16)
            outs.append(sum(jnp.dot(p[:, _key_block(j)], vj[:, sl], preferred_element_type=F32)
                            for j, vj in enumerate((v0, v1, v2))))
        o_ref[...] = jnp.concatenate(outs, axis=1).astype(o_ref.dtype)

    return pl.pallas_call(
        body, name="attn_fwd", grid=(N_HEAD_GROUPS, nb),
        in_specs=[pl.BlockSpec((SUPER, HEAD_COLS), lambda hg, i: (i, hg))]
        + _band_specs(nb, N_HEAD_GROUPS, False) + _band_specs(nb, 0, False)
        + [pl.BlockSpec((HEADS_PER_STEP, SUPER, BAND), lambda hg, i: (hg, 0, 0))],
        out_specs=pl.BlockSpec((SUPER, HEAD_COLS), lambda hg, i: (i, hg)),
        out_shape=jax.ShapeDtypeStruct((s_len, D_MODEL), BF16),
        compiler_params=_params("parallel", "arbitrary"),
    )(qkn, qkn, qkn, qkn, v, v, v, bias)


def _attn_bwd(qkn, v, d_out, bias):
    s_len = qkn.shape[0]
    nb = s_len // SUPER

    def body(q_ref, k0, k1, k2, v0, v1, v2, do_ref, b_ref, dp_ref, db_ref, aq_ref, ak_ref, av_ref):
        i = pl.program_id(1)

        @pl.when(i == 0)
        def _():
            aq_ref[...] = jnp.zeros_like(aq_ref)
            ak_ref[...] = jnp.zeros_like(ak_ref)
            av_ref[...] = jnp.zeros_like(av_ref)
            db_ref[...] = jnp.zeros_like(db_ref)

        @pl.when(i < nb)
        def _():
            dead = _dead_keys(i)
            dq, dk, dv = [], [[], [], []], [[], [], []]
            for hh in range(HEADS_PER_STEP):
                sl = _head(hh)
                qs, do = q_ref[:, sl], do_ref[:, sl]
                kn = [k0[:, sl], k1[:, sl], k2[:, sl]]
                vv = [v0[:, sl], v1[:, sl], v2[:, sl]]
                p = _probs(qs, kn, b_ref[hh], dead)
                pb = p.astype(BF16)
                dp = jnp.concatenate([lax.dot_general(do, vj, NT, preferred_element_type=F32) for vj in vv], axis=1)
                ds = p * (dp - jnp.sum(p * dp, axis=-1, keepdims=True))
                db_ref[hh] += ds
                dsb = ds.astype(BF16)
                dq.append(sum(jnp.dot(dsb[:, _key_block(j)], kn[j], preferred_element_type=F32) for j in range(3)))
                for j in range(3):
                    dv[j].append(lax.dot_general(pb[:, _key_block(j)], do, TN, preferred_element_type=F32))
                    dk[j].append(lax.dot_general(dsb[:, _key_block(j)], qs, TN, preferred_element_type=F32))
            aq_ref[i % 3] = jnp.concatenate(dq, axis=1)
            for j in range(3):
                slot = (i + 1 + j) % 3
                if j < 2:
                    ak_ref[slot] += jnp.concatenate(dk[j], axis=1)
                    av_ref[slot] += jnp.concatenate(dv[j], axis=1)
                else:
                    ak_ref[slot] = jnp.concatenate(dk[j], axis=1)
                    av_ref[slot] = jnp.concatenate(dv[j], axis=1)

        slot = (i + 1) % 3
        dp_ref[0] = aq_ref[slot].astype(dp_ref.dtype)
        dp_ref[1] = ak_ref[slot].astype(dp_ref.dtype)
        dp_ref[2] = av_ref[slot].astype(dp_ref.dtype)

    def qrow(hg, i):
        return (jnp.minimum(i, nb - 1), hg)

    return pl.pallas_call(
        body, name="attn_bwd", grid=(N_HEAD_GROUPS, nb + 2),
        in_specs=[pl.BlockSpec((SUPER, HEAD_COLS), qrow)]
        + _band_specs(nb, N_HEAD_GROUPS, True) + _band_specs(nb, 0, True)
        + [pl.BlockSpec((SUPER, HEAD_COLS), qrow),
           pl.BlockSpec((HEADS_PER_STEP, SUPER, BAND), lambda hg, i: (hg, 0, 0))],
        out_specs=[pl.BlockSpec((3, SUPER, HEAD_COLS), lambda hg, i: (0, jnp.maximum(i - 2, 0), hg)),
                   pl.BlockSpec((HEADS_PER_STEP, SUPER, BAND), lambda hg, i: (hg, 0, 0))],
        out_shape=[jax.ShapeDtypeStruct((6, s_len, D_MODEL), BF16),
                   jax.ShapeDtypeStruct((N_HEADS, SUPER, BAND), F32)],
        scratch_shapes=[pltpu.VMEM((3, SUPER, HEAD_COLS), F32)] * 3,
        compiler_params=_params("parallel", "arbitrary"),
    )(qkn, qkn, qkn, qkn, v, v, v, d_out, bias)


def _qk_norm_bwd(dproj6, qk_raw, gq, gk):
    s_len = qk_raw.shape[0]
    ts = 256

    def body(d_ref, raw_ref, gq_ref, gk_ref, o_ref, dgq_ref, dgk_ref):
        @pl.when(pl.program_id(0) == 0)
        def _():
            dgq_ref[...] = jnp.zeros_like(dgq_ref)
            dgk_ref[...] = jnp.zeros_like(dgk_ref)

        for piece, (g_ref, dg_ref, scale) in enumerate(((gq_ref, dgq_ref, QK_SCALE), (gk_ref, dgk_ref, 1.0))):
            outs, dg = [], jnp.zeros((1, HEAD_DIM), F32)
            for hh in range(N_HEADS):
                xhat, r = _unit(raw_ref[:, piece * D_MODEL + hh * HEAD_DIM:piece * D_MODEL + (hh + 1) * HEAD_DIM].astype(F32))
                dn = d_ref[piece, :, _head(hh)].astype(F32) * scale
                u = dn * g_ref[...]
                outs.append(r * (u - xhat * jnp.mean(u * xhat, axis=-1, keepdims=True)))
                dg = dg + jnp.sum(dn * xhat, axis=0, keepdims=True)
            o_ref[piece] = jnp.concatenate(outs, axis=1).astype(o_ref.dtype)
            dg_ref[...] += dg

    small = pl.BlockSpec((1, HEAD_DIM), lambda i: (0, 0))
    return pl.pallas_call(
        body, name="qk_norm_bwd", grid=(s_len // ts,),
        in_specs=[pl.BlockSpec((2, ts, D_MODEL), lambda i: (0, i, 0)),
                  pl.BlockSpec((ts, 2 * D_MODEL), lambda i: (i, 0)), small, small],
        out_specs=[pl.BlockSpec((2, ts, D_MODEL), lambda i: (0, i, 0)), small, small],
        out_shape=[jax.ShapeDtypeStruct(dproj6.shape, dproj6.dtype),
                   jax.ShapeDtypeStruct((1, HEAD_DIM), F32), jax.ShapeDtypeStruct((1, HEAD_DIM), F32)],
        input_output_aliases={0: 0},
        compiler_params=_params("arbitrary"),
    )(dproj6, qk_raw, gq, gk)


CONV_ROWS = 512
HALO = 16


def _rows_with_halo(ref, r0, n, front, s_len):
    zeros = jnp.zeros((HALO, ref.shape[1]), F32)
    if front:
        return (jnp.concatenate([zeros, ref[0:n, :].astype(F32)], axis=0) if r0 == 0
                else ref[r0 - HALO:r0 + n, :].astype(F32))
    return (jnp.concatenate([ref[r0:r0 + n, :].astype(F32), zeros], axis=0) if r0 + n == s_len
            else ref[r0:r0 + n + HALO, :].astype(F32))


def _earlier(ext, k):
    return pltpu.roll(ext, k, 0)[HALO:]


def _later(ext, k):
    n = ext.shape[0]
    return pltpu.roll(ext, n - k, 0)[:n - HALO]


def _conv_cols(col0):
    return lambda s_len: pl.BlockSpec((s_len, LANE), lambda j: (0, col0 + j))


def _conv_fwd(proj, conv_w, conv_b):
    s_len = proj.shape[0]

    def body(bg_ref, cg_ref, xc_ref, w_ref, b_ref, o_ref):
        w = [w_ref[t:t + 1, :] for t in range(3)]
        for r0 in range(0, s_len, CONV_ROWS):
            u = _rows_with_halo(cg_ref, r0, CONV_ROWS, True, s_len) * \
                _rows_with_halo(xc_ref, r0, CONV_ROWS, True, s_len)
            conv = b_ref[...] + w[0] * _earlier(u, 2) + w[1] * _earlier(u, 1) + w[2] * u[HALO:]
            o_ref[r0:r0 + CONV_ROWS, :] = (bg_ref[r0:r0 + CONV_ROWS, :].astype(F32) * conv).astype(o_ref.dtype)

    return pl.pallas_call(
        body, name="conv_fwd", grid=(D_MODEL // LANE,),
        in_specs=[_conv_cols(0)(s_len), _conv_cols(8)(s_len), _conv_cols(16)(s_len),
                  pl.BlockSpec((3, LANE), lambda j: (0, j)), pl.BlockSpec((1, LANE), lambda j: (0, j))],
        out_specs=pl.BlockSpec((s_len, LANE), lambda j: (0, j)),
        out_shape=jax.ShapeDtypeStruct((s_len, D_MODEL), BF16),
        compiler_params=_params("parallel"),
    )(proj, proj, proj, conv_w, conv_b)


def _conv_bwd(dproj6, dy, proj, conv_w, conv_b):
    s_len = proj.shape[0]

    def body(dy_ref, bg_ref, cg_ref, xc_ref, w_ref, b_ref, _, dp_ref, dw_ref):
        w = [w_ref[t:t + 1, :] for t in range(3)]
        acc = [jnp.zeros((1, LANE), F32) for _ in range(4)]
        for r0 in range(0, s_len, CONV_ROWS):
            rows = slice(r0, r0 + CONV_ROWS)
            u = _rows_with_halo(cg_ref, r0, CONV_ROWS, True, s_len) * \
                _rows_with_halo(xc_ref, r0, CONV_ROWS, True, s_len)
            u2, u1, u0 = _earlier(u, 2), _earlier(u, 1), u[HALO:]
            conv = b_ref[...] + w[0] * u2 + w[1] * u1 + w[2] * u0
            dp_ref[0, rows, :] = (dy_ref[rows, :].astype(F32) * conv).astype(dp_ref.dtype)
            dconv_ext = _rows_with_halo(dy_ref, r0, CONV_ROWS, False, s_len) * \
                _rows_with_halo(bg_ref, r0, CONV_ROWS, False, s_len)
            dconv = dconv_ext[:CONV_ROWS]
            for t, term in enumerate([dconv * u2, dconv * u1, dconv * u0, dconv]):
                acc[t] = acc[t] + jnp.sum(term, axis=0, keepdims=True)
            du = w[2] * dconv + w[1] * _later(dconv_ext, 1) + w[0] * _later(dconv_ext, 2)
            dp_ref[1, rows, :] = (du * xc_ref[rows, :].astype(F32)).astype(dp_ref.dtype)
            dp_ref[2, rows, :] = (du * cg_ref[rows, :].astype(F32)).astype(dp_ref.dtype)
        dw_ref[...] = jnp.zeros_like(dw_ref)
        for t in range(4):
            dw_ref[t:t + 1, :] = acc[t]

    return pl.pallas_call(
        body, name="conv_bwd", grid=(D_MODEL // LANE,),
        in_specs=[pl.BlockSpec((s_len, LANE), lambda j: (0, j)),
                  _conv_cols(0)(s_len), _conv_cols(8)(s_len), _conv_cols(16)(s_len),
                  pl.BlockSpec((3, LANE), lambda j: (0, j)), pl.BlockSpec((1, LANE), lambda j: (0, j)),
                  pl.BlockSpec(memory_space=pl.ANY)],
        out_specs=[pl.BlockSpec((3, s_len, LANE), lambda j: (1, 0, j)),
                   pl.BlockSpec((8, LANE), lambda j: (0, j))],
        out_shape=[jax.ShapeDtypeStruct(dproj6.shape, dproj6.dtype),
                   jax.ShapeDtypeStruct((8, D_MODEL), F32)],
        input_output_aliases={6: 0},
        compiler_params=_params("parallel"),
    )(dy, proj, proj, proj, conv_w, conv_b, dproj6)


def _local_grads(x, target, norm1_g, w_in3, q_norm_g, k_norm_g, rel_bias, conv_w, conv_b,
                 w_ap, w_cp, w_gate3, b_gate, w_out, norm2_g, w_up3, w_down):
    s_len = x.shape[0]
    tm = min(1024, s_len)
    row = lambda v: v.reshape(1, -1)

    h = _rms_fwd("norm1", x, row(norm1_g))
    gq, gk = row(q_norm_g), row(k_norm_g)

    def qk_epi(acc, e, o):
        o[0][...] = acc.astype(BF16)
        gain = jnp.where(pl.program_id(1) < 2, e[0] * QK_SCALE, e[1])
        o[1][...] = jnp.concatenate([_unit(acc[:, _head(hh)])[0] * gain for hh in range(512 // HEAD_DIM)],
                                    axis=1).astype(BF16)
    small = pl.BlockSpec((1, HEAD_DIM), lambda i, j, k: (0, 0))
    qk_raw, qkn = _mm_fwd("proj_qk", h, w_in3, tm, 512, D_MODEL, ncols=2 * D_MODEL, epilogue=qk_epi,
                          outs=[_out2d(s_len, 2 * D_MODEL, BF16, tm, 512)] * 2, extras=[(gq, small), (gk, small)])
    v = _mm_fwd("proj_v", h, w_in3, tm, 512, D_MODEL, col0=4, ncols=D_MODEL,
                outs=[_out2d(s_len, D_MODEL, BF16, tm, 512)])
    conv_in = _mm_fwd("proj_conv", h, w_in3, tm, 1536, D_MODEL, col0=2, ncols=3 * D_MODEL,
                      outs=[_out2d(s_len, 3 * D_MODEL, BF16, tm, 1536)])

    def gate_epi(acc, e, o):
        o[0][...] = jax.nn.sigmoid(acc + e[0]).astype(BF16)
    gates = _mm_fwd("gates", h, w_gate3, tm, 512, D_MODEL, epilogue=gate_epi,
                    outs=[_out2d(s_len, 2 * D_MODEL, BF16, tm, 512)],
                    extras=[(row(b_gate), pl.BlockSpec((1, 512), lambda i, j, k: (0, j)))])

    bias = _bias_expand(rel_bias)
    attn = _attn_fwd(qkn, v, bias)
    yconv = _conv_fwd(conv_in, conv_w, row(conv_b))
    tw = 1024
    ya = _mm_fwd("attn_proj", attn, w_ap, tm, tw, D_MODEL, outs=[_out2d(s_len, D_MODEL, BF16, tm, tw)])

    def merge_epi(acc, e, o):
        ya_v, ga, gc = [t.astype(F32) for t in e]
        o[0][...] = acc.astype(BF16)
        o[1][...] = (ga * ya_v + gc * acc).astype(BF16)
    gate_a, gate_c = (gates, _tile_spec(tm, tw, 0)), (gates, _tile_spec(tm, tw, 1))
    yc, merged = _mm_fwd("conv_proj", yconv, w_cp, tm, tw, D_MODEL, epilogue=merge_epi,
                         outs=[_out2d(s_len, D_MODEL, BF16, tm, tw), _out2d(s_len, D_MODEL, BF16, tm, tw)],
                         extras=[(ya, _tile_spec(tm, tw)), gate_a, gate_c])

    def res_epi(acc, e, o):
        o[0][...] = e[0] + acc
    x1 = _mm_fwd("out_proj", merged, w_out, tm, tw, D_MODEL, epilogue=res_epi,
                 extras=[(x, _tile_spec(tm, tw))])
    h2 = _rms_fwd("norm2", x1, row(norm2_g))

    def up_epi(acc, e, o):
        o[0][...] = acc.astype(BF16)
        o[1][...] = jnp.square(jnp.maximum(acc, 0.0)).astype(BF16)
    up, act = _mm_fwd("mlp_up", h2, w_up3, tm, tw, D_MODEL, epilogue=up_epi,
                      outs=[_out2d(s_len, D_FF, BF16, tm, tw), _out2d(s_len, D_FF, BF16, tm, tw)])

    def loss_epi(acc, e, o):
        err = e[0] + acc - e[1]
        dy = err * (1.0 / D_MODEL)
        o[0][...] = dy
        o[1][...] = dy.astype(BF16)
        sq = err * err
        part = sq[:, 0:LANE]
        for c0 in range(LANE, D_MODEL, LANE):
            part = part + sq[:, c0:c0 + LANE]
        o[2][...] = jnp.sum(part.reshape(tl // 8, 8, LANE), axis=0)
    tl = 512
    dy, dy_b, loss_part = _mm_fwd(
        "mlp_down", act, w_down, tl, D_MODEL, D_FF, epilogue=loss_epi,
        outs=[_out2d(s_len, D_MODEL, F32, tl, D_MODEL), _out2d(s_len, D_MODEL, BF16, tl, D_MODEL),
              (jax.ShapeDtypeStruct((8 * (s_len // tl), LANE), F32), pl.BlockSpec((8, LANE), lambda i, j, k: (i, 0)))],
        extras=[(x1, _tile_spec(tl, D_MODEL)), (target, _tile_spec(tl, D_MODEL))])

    def dup_epi(acc, e, o):
        o[0][...] = (acc * (2.0 * jnp.maximum(e[0].astype(F32), 0.0))).astype(BF16)
    full = lambda cols: pl.BlockSpec((tm, cols), lambda i, j, n: (i, n))
    tokens = lambda cols: pl.BlockSpec((s_len, cols), lambda i, j, m: (m, j))
    dup = _mm_bwd_x("d_act", dy_b, full(D_MODEL), w_down, tm, tw, D_MODEL, s_len, D_MODEL, epilogue=dup_epi,
                    outs=[_out2d(s_len, D_FF, BF16, tm, tw)], extras=[(up, _tile_spec(tm, tw))])
    g_down = _mm_bwd_w("g_down", act, dy_b, tokens(D_MODEL), D_MODEL, 512, D_MODEL, s_len, False)
    g_up = _mm_bwd_w("g_up", h2, dup, tokens(512), D_FF, D_MODEL, 512, s_len, True)
    dh2 = _mm_bwd_x("d_h2", dup, full(1024), w_up3, tm, tw, 1024, s_len, D_FF)
    dx1, dx1_b, dg2 = _rms_bwd("norm2_bwd", dh2, x1, row(norm2_g), dy)

    def dmerge_epi(acc, e, o):
        ya_v, yc_v, ga, gc = [t.astype(F32) for t in e]
        o[0][...] = (acc * ga).astype(BF16)
        o[1][...] = (acc * gc).astype(BF16)
        o[2][0] = (acc * ya_v * ga * (1.0 - ga)).astype(BF16)
        o[2][1] = (acc * yc_v * gc * (1.0 - gc)).astype(BF16)
    dya, dyc, dgp2 = _mm_bwd_x(
        "d_merged", dx1_b, full(D_MODEL), w_out, tm, tw, D_MODEL, s_len, D_MODEL, epilogue=dmerge_epi,
        outs=[_out2d(s_len, D_MODEL, BF16, tm, tw), _out2d(s_len, D_MODEL, BF16, tm, tw),
              (jax.ShapeDtypeStruct((2, s_len, D_MODEL), BF16), pl.BlockSpec((2, tm, tw), lambda i, j, n: (0, i, j)))],
        extras=[(ya, _tile_spec(tm, tw)), (yc, _tile_spec(tm, tw)), gate_a, gate_c])
    g_out = _mm_bwd_w("g_out", merged, dx1_b, tokens(512), D_MODEL, D_MODEL, 512, s_len, False)
    d_attn = _mm_bwd_x("d_attn", dya, full(D_MODEL), w_ap, tm, tw, D_MODEL, s_len, D_MODEL,
                       outs=[_out2d(s_len, D_MODEL, BF16, tm, tw)])
    g_ap = _mm_bwd_w("g_attn_proj", attn, dya, tokens(512), D_MODEL, D_MODEL, 512, s_len, False)
    d_yconv = _mm_bwd_x("d_yconv", dyc, full(D_MODEL), w_cp, tm, tw, D_MODEL, s_len, D_MODEL,
                        outs=[_out2d(s_len, D_MODEL, BF16, tm, tw)])
    g_cp = _mm_bwd_w("g_conv_proj", yconv, dyc, tokens(512), D_MODEL, D_MODEL, 512, s_len, False)

    dproj6, dbias = _attn_bwd(qkn, v, d_attn, bias)
    dproj6, dgq, dgk = _qk_norm_bwd(dproj6, qk_raw, gq, gk)
    dproj6, dconv_wb = _conv_bwd(dproj6, d_yconv, conv_in, conv_w, row(conv_b))
    d_rel = _bias_reduce(dbias)

    piece = lambda width: (lambda blk: (blk * width) // D_MODEL, lambda blk: (blk * width % D_MODEL) // width)
    pc, cb = piece(512)
    pieces = pl.BlockSpec((None, s_len, 512), lambda i, j, m: (pc(j), m, cb(j)))
    g_in = _mm_bwd_w("g_in", h, dproj6, pieces, 6 * D_MODEL, D_MODEL, 512, s_len, True)
    g_gate = _mm_bwd_w("g_gate", h, dgp2, pieces, 2 * D_MODEL, D_MODEL, 512, s_len, True)
    dh_a = _mm_bwd_x("d_h_proj", dproj6, pl.BlockSpec((None, tm, 512), lambda i, j, n: (pc(n), i, cb(n))),
                     w_in3, tm, tw, 512, s_len, 6 * D_MODEL)

    def add_epi(acc, e, o):
        o[0][...] = acc + e[0]
    dh = _mm_bwd_x("d_h_gate", dgp2, pl.BlockSpec((None, tm, 512), lambda i, j, n: (pc(n), i, cb(n))),
                   w_gate3, tm, tw, 512, s_len, 2 * D_MODEL, epilogue=add_epi,
                   extras=[(dh_a, _tile_spec(tm, tw))])
    grad_x, _, dg1 = _rms_bwd("norm1_bwd", dh, x, row(norm1_g), dx1)

    def bsum(t, f):
        return [], [t[0].astype(F32), t[1].astype(F32)]
    ts = 512
    db_a, db_c = _ew("b_gate_sum", bsum,
                     [(dgp2, pl.BlockSpec((None, ts, D_MODEL), lambda i: (0, i, 0))),
                      (dgp2, pl.BlockSpec((None, ts, D_MODEL), lambda i: (1, i, 0)))],
                     [], [], sums=[D_MODEL, D_MODEL], ts=ts)

    big = dict(w_in=g_in, w_attn_proj=g_ap, w_conv_proj=g_cp, w_gate=g_gate, w_out=g_out,
               w_up=g_up, w_down=g_down)
    small = dict(norm1_g=dg1, norm2_g=dg2, conv_wb=dconv_wb, b_gate=(db_a, db_c),
                 q_norm_g=dgq, k_norm_g=dgk, rel_bias=d_rel)
    return loss_part, grad_x, big, small


def _finish_loss(loss_part):
    def body(l_ref, lo_ref):
        total = jnp.sum(jnp.sum(l_ref[...], axis=0, keepdims=True), axis=1, keepdims=True)
        lo_ref[...] = jnp.broadcast_to(total * (0.5 / D_MODEL), lo_ref.shape)

    return pl.pallas_call(body, name="finish_loss", out_shape=jax.ShapeDtypeStruct((8, LANE), F32))(loss_part)


HBM_SPEC = pl.BlockSpec(memory_space=pl.ANY)


def _place():
    return lax.axis_index("x"), lax.axis_index("y"), lax.axis_index("c")


def _other_chips(x, y):
    return [(1 - x, y), (x, 1 - y), (1 - x, 1 - y)]


def _cast_into_slot(name, where, w):
    r, cols = w.shape
    ts = 256

    def body(w_ref, x_ref, o_ref):
        o_ref[...] = x_ref[...].astype(o_ref.dtype)

    return pl.pallas_call(
        body, name=name,
        grid_spec=pltpu.PrefetchScalarGridSpec(
            num_scalar_prefetch=1, grid=(r // ts,),
            in_specs=[pl.BlockSpec((ts, cols), lambda i, w: (i, 0))],
            out_specs=pl.BlockSpec((None, ts, cols), lambda i, w: (w[0], i, 0))),
        out_shape=jax.ShapeDtypeStruct((N_SHARD, r, cols), BF16),
        compiler_params=_params("parallel"),
    )(where, w)


def _all_gather(slots, small):
    n = len(slots)

    def body(*refs):
        outs, small_out = refs[n + 1:2 * n + 1], refs[2 * n + 1]
        send1, recv1, send2, recv2, send3, recv3 = refs[2 * n + 2:]
        x, y, c = _place()
        me = 2 * x + y
        chips = _other_chips(x, y)
        sibling = (x, y, 1 - c)

        def half(w, shard, which):
            hr = slots[w].shape[1] // 2
            return outs[w].at[shard, pl.ds(which * hr, hr)]

        def remote(ref, send, recv, k, to):
            return pltpu.make_async_remote_copy(src_ref=ref, dst_ref=ref, send_sem=send.at[k],
                                                recv_sem=recv.at[k], device_id=to, device_id_type=MESH)

        sent = [remote(half(w, me, c), send1, recv1, 3 * w + j, (*chip, c))
                for w in range(n) for j, chip in enumerate(chips)]
        sent += [remote(small_out.at[me], send3, recv3, j, (*chip, c)) for j, chip in enumerate(chips)]
        for cp in sent:
            cp.start()
        for w in range(n):
            for j, chip in enumerate(chips):
                landed = half(w, 2 * chip[0] + chip[1], c)
                remote(landed, send1, recv1, 3 * w + j, (*chip, c)).wait_recv()
                fwd = remote(landed, send2, recv2, 3 * w + j, sibling)
                fwd.start()
                sent.append(fwd)
        for w in range(n):
            for j, chip in enumerate(chips):
                remote(half(w, 2 * chip[0] + chip[1], 1 - c), send2, recv2, 3 * w + j, sibling).wait_recv()
        for j, chip in enumerate(chips):
            remote(small_out.at[2 * chip[0] + chip[1]], send3, recv3, j, (*chip, c)).wait_recv()
        for cp in sent:
            cp.wait_send()

    return pl.pallas_call(
        body, name="gather_weights",
        in_specs=[HBM_SPEC] * (n + 1), out_specs=[HBM_SPEC] * (n + 1),
        out_shape=[jax.ShapeDtypeStruct(s.shape, s.dtype) for s in slots]
        + [jax.ShapeDtypeStruct(small.shape, small.dtype)],
        input_output_aliases={k: k for k in range(n + 1)},
        scratch_shapes=[pltpu.SemaphoreType.DMA((3 * n,))] * 4 + [pltpu.SemaphoreType.DMA((3,))] * 2,
    )(*slots, small)


def _pair_exchange(grads):
    n = len(grads)

    def body(*refs):
        ins, outs = refs[:n], refs[n:2 * n]
        send, recv = refs[2 * n:]
        x, y, c = _place()
        copies = []
        for w in range(n):
            hr = grads[w].shape[1] // 2
            copies.append(pltpu.make_async_remote_copy(
                src_ref=ins[w].at[:, pl.ds((1 - c) * hr, hr)], dst_ref=outs[w],
                send_sem=send.at[w], recv_sem=recv.at[w], device_id=(x, y, 1 - c), device_id_type=MESH))
        for cp in copies:
            cp.start()
        for cp in copies:
            cp.wait()

    return pl.pallas_call(
        body, name="pair_exchange",
        in_specs=[HBM_SPEC] * n, out_specs=[HBM_SPEC] * n,
        out_shape=[jax.ShapeDtypeStruct((N_SHARD, g.shape[1] // 2, g.shape[2]), g.dtype) for g in grads],
        scratch_shapes=[pltpu.SemaphoreType.DMA((n,))] * 2,
    )(*grads)


def _row_tile(hr):
    return min(hr, 256)


def _pair_add(name, where, grad, got):
    _, hr, cols = got.shape
    tr = _row_tile(hr)
    nblk = hr // tr

    def body(w_ref, g_ref, r_ref, o_ref):
        o_ref[...] = (g_ref[...] + r_ref[...]).astype(o_ref.dtype)

    return pl.pallas_call(
        body, name=name,
        grid_spec=pltpu.PrefetchScalarGridSpec(
            num_scalar_prefetch=1, grid=(N_SHARD, nblk),
            in_specs=[pl.BlockSpec((None, tr, cols), lambda s, i, w: (s, w[1] * nblk + i, 0)),
                      pl.BlockSpec((None, tr, cols), lambda s, i, w: (s, i, 0))],
            out_specs=pl.BlockSpec((None, tr, cols), lambda s, i, w: (s, i, 0))),
        out_shape=jax.ShapeDtypeStruct(got.shape, BF16),
        compiler_params=_params("parallel", "parallel"),
    )(where, grad, got)


def _chip_exchange(partials):
    n = len(partials)

    def body(*refs):
        ins, outs = refs[:n], refs[n:2 * n]
        send, recv = refs[2 * n:]
        x, y, c = _place()
        copies = []
        for w in range(n):
            for j, chip in enumerate(_other_chips(x, y)):
                copies.append(pltpu.make_async_remote_copy(
                    src_ref=ins[w].at[2 * chip[0] + chip[1]], dst_ref=outs[w].at[j],
                    send_sem=send.at[3 * w + j], recv_sem=recv.at[3 * w + j],
                    device_id=(*chip, c), device_id_type=MESH))
        for cp in copies:
            cp.start()
        for cp in copies:
            cp.wait()

    return pl.pallas_call(
        body, name="chip_exchange",
        in_specs=[HBM_SPEC] * n, out_specs=[HBM_SPEC] * n,
        out_shape=[jax.ShapeDtypeStruct((3,) + p.shape[1:], p.dtype) for p in partials],
        scratch_shapes=[pltpu.SemaphoreType.DMA((3 * n,))] * 2,
    )(*partials)


def _final_add(name, where, grad, got, arrived):
    _, hr, cols = got.shape
    tr = _row_tile(hr)
    nblk = hr // tr

    def body(w_ref, g_ref, r_ref, a_ref, o_ref):
        acc = g_ref[...] + r_ref[...]
        for j in range(3):
            acc = acc + a_ref[j].astype(F32)
        o_ref[...] = acc

    return pl.pallas_call(
        body, name=name,
        grid_spec=pltpu.PrefetchScalarGridSpec(
            num_scalar_prefetch=1, grid=(nblk,),
            in_specs=[pl.BlockSpec((None, tr, cols), lambda i, w: (w[0], w[1] * nblk + i, 0)),
                      pl.BlockSpec((None, tr, cols), lambda i, w: (w[0], i, 0)),
                      pl.BlockSpec((3, tr, cols), lambda i, w: (0, i, 0))],
            out_specs=pl.BlockSpec((tr, cols), lambda i, w: (w[1] * nblk + i, 0))),
        out_shape=jax.ShapeDtypeStruct((2 * hr, cols), F32),
        compiler_params=_params("parallel"),
    )(where, grad, got, arrived)


def _pair_share(shards):
    n = len(shards)

    def body(*refs):
        outs = refs[n:2 * n]
        send, recv = refs[2 * n:]
        x, y, c = _place()

        def half(w, which):
            hr = shards[w].shape[0] // 2
            return outs[w].at[pl.ds(which * hr, hr)]

        def remote(ref, w):
            return pltpu.make_async_remote_copy(src_ref=ref, dst_ref=ref, send_sem=send.at[w], recv_sem=recv.at[w],
                                                device_id=(x, y, 1 - c), device_id_type=MESH)

        copies = [remote(half(w, c), w) for w in range(n)]
        for cp in copies:
            cp.start()
        for w in range(n):
            remote(half(w, 1 - c), w).wait_recv()
        for cp in copies:
            cp.wait_send()

    return pl.pallas_call(
        body, name="pair_share",
        in_specs=[HBM_SPEC] * n, out_specs=[HBM_SPEC] * n,
        out_shape=[jax.ShapeDtypeStruct(s.shape, s.dtype) for s in shards],
        input_output_aliases={k: k for k in range(n)},
        scratch_shapes=[pltpu.SemaphoreType.DMA((n,))] * 2,
    )(*shards)


SMALL_ROWS = 32
N_DEV = 8


def _all_reduce_small(pack):
    def body(p_ref, o_ref, buf, send, recv):
        x, y, c = _place()
        buf[4 * x + 2 * y + c] = p_ref[...]
        copies, waits = [], []
        for k in range(1, N_DEV):
            px = 1 - x if k & 4 else x
            py = 1 - y if k & 2 else y
            pc = 1 - c if k & 1 else c
            copies.append(pltpu.make_async_remote_copy(
                src_ref=p_ref, dst_ref=buf.at[4 * x + 2 * y + c], send_sem=send.at[k - 1],
                recv_sem=recv.at[k - 1], device_id=(px, py, pc), device_id_type=MESH))
            waits.append(pltpu.make_async_remote_copy(
                src_ref=p_ref, dst_ref=buf.at[4 * px + 2 * py + pc], send_sem=send.at[k - 1],
                recv_sem=recv.at[k - 1], device_id=(px, py, pc), device_id_type=MESH))
        for cp in copies:
            cp.start()
        for cp in waits:
            cp.wait_recv()
        acc = buf[0]
        for d in range(1, N_DEV):
            acc = acc + buf[d]
        o_ref[...] = acc
        for cp in copies:
            cp.wait_send()

    return pl.pallas_call(
        body, name="all_reduce_small",
        out_shape=jax.ShapeDtypeStruct(pack.shape, F32),
        scratch_shapes=[pltpu.VMEM((N_DEV,) + pack.shape, F32),
                        pltpu.SemaphoreType.DMA((N_DEV - 1,)), pltpu.SemaphoreType.DMA((N_DEV - 1,))],
    )(pack)


def _adamw(name, w, g, m, v):
    c1 = 1.0 - ADAM_B1 ** ADAM_STEP
    c2 = 1.0 - ADAM_B2 ** ADAM_STEP

    def fn(t, f):
        wv, gv, mv, vv = t
        m2 = ADAM_B1 * mv + (1.0 - ADAM_B1) * gv
        v2 = ADAM_B2 * vv + (1.0 - ADAM_B2) * (gv * gv)
        delta = -ADAM_LR * ((m2 / c1) / (jnp.sqrt(v2 / c2) + ADAM_EPS) + ADAM_WD * wv)
        return [delta, m2, v2], []

    cols = w.shape[1]
    return _ew(name, fn, [w, g, m, v], [], [(cols, F32)] * 3, ts=min(w.shape[0], 256))


def _pack_small(norm1_g, norm2_g, conv_b, b_gate, conv_w, q_norm_g, k_norm_g, rel_bias):
    pack = jnp.zeros((SMALL_ROWS, D_MODEL), F32)
    for r0, v in ((0, norm1_g), (1, norm2_g), (2, conv_b), (3, b_gate.reshape(2, D_MODEL)), (5, conv_w),
                  (8, q_norm_g), (9, k_norm_g), (10, rel_bias)):
        v = v.reshape(-1, v.shape[-1]).astype(F32)
        pack = pack.at[r0:r0 + v.shape[0], :v.shape[1]].set(v)
    return pack


def _unpack_small(pack, conv_cols):
    return dict(norm1_g=pack[0], norm2_g=pack[1], conv_b=pack[2], b_gate=pack[3:5].reshape(2 * D_MODEL),
                conv_w=pack[5:8, :conv_cols], q_norm_g=pack[8, :HEAD_DIM], k_norm_g=pack[9, :HEAD_DIM],
                rel_bias=pack[10:10 + N_HEADS, :N_REL])


BIG = ["w_in", "w_attn_proj", "w_conv_proj", "w_gate", "w_out", "w_up", "w_down"]
WEIGHTS = ["norm1_g", "w_in", "q_norm_g", "k_norm_g", "rel_bias", "conv_w", "conv_b", "w_attn_proj",
           "w_conv_proj", "w_gate", "b_gate", "w_out", "norm2_g", "w_up", "w_down"]


def kernel(x, norm1_g, w_in, q_norm_g, k_norm_g, rel_bias, conv_w, conv_b, w_attn_proj, w_conv_proj, w_gate, b_gate, w_out, norm2_g, w_up, w_down, loss_target, m_norm1_g, m_w_in, m_q_norm_g, m_k_norm_g, m_rel_bias, m_conv_w, m_conv_b, m_w_attn_proj, m_w_conv_proj, m_w_gate, m_b_gate, m_w_out, m_norm2_g, m_w_up, m_w_down, v_norm1_g, v_w_in, v_q_norm_g, v_k_norm_g, v_rel_bias, v_conv_w, v_conv_b, v_w_attn_proj, v_w_conv_proj, v_w_gate, v_b_gate, v_w_out, v_norm2_g, v_w_up, v_w_down):
    given = dict(locals())
    w = {n: given[n] for n in WEIGHTS}
    m = {n: given["m_" + n] for n in WEIGHTS}
    v = {n: given["v_" + n] for n in WEIGHTS}
    s_len = x.shape[1]
    shard = 2 * lax.axis_index("x") + lax.axis_index("y")
    where = jnp.stack([shard, lax.axis_index("c")]).astype(jnp.int32)
    conv_cols = conv_w.shape[1]

    small_in = lax.dynamic_update_slice(jnp.zeros((N_SHARD, 8, conv_cols), F32), conv_w[None], (shard, 0, 0))
    *gathered, conv_w4 = _all_gather([_cast_into_slot("cast_" + n, where, w[n]) for n in BIG], small_in)
    wg = dict(zip(BIG, gathered))
    conv_w_full = conv_w4[:, :3, :].transpose(1, 0, 2).reshape(3, N_SHARD * conv_cols)
    rows = lambda a: a.reshape(a.shape[0] * a.shape[1], a.shape[2])

    loss_part, grad_x, big, small = _local_grads(
        x.reshape(s_len, D_MODEL), loss_target.reshape(s_len, D_MODEL), norm1_g, wg["w_in"], q_norm_g,
        k_norm_g, rel_bias, conv_w_full, conv_b, rows(wg["w_attn_proj"]), rows(wg["w_conv_proj"]),
        wg["w_gate"], b_gate, rows(wg["w_out"]), norm2_g, wg["w_up"], rows(wg["w_down"]))

    g4 = [big[n] if big[n].ndim == 3 else big[n].reshape(N_SHARD, -1, big[n].shape[1]) for n in BIG]
    got = _pair_exchange(g4)
    partial = [_pair_add("pair_add_" + n, where, g, r) for n, g, r in zip(BIG, g4, got)]
    arrived = _chip_exchange(partial)
    halves = [_final_add("final_add_" + n, where, g, r, a) for n, g, r, a in zip(BIG, g4, got, arrived)]
    grad = dict(zip(BIG, _pair_share(halves)))

    loss_local = _finish_loss(loss_part)
    pack = _pack_small(small["norm1_g"], small["norm2_g"], small["conv_wb"][3], jnp.concatenate(small["b_gate"], axis=1),
                       small["conv_wb"][0:3], small["q_norm_g"], small["k_norm_g"], small["rel_bias"])
    total = _all_reduce_small(pack)
    g_small = _unpack_small(total, D_MODEL)
    g_small["conv_w"] = lax.dynamic_slice(g_small["conv_w"], (0, shard * conv_cols), (3, conv_cols))
    grad.update(g_small)

    delta, new_m, new_v = {}, {}, {}
    for n in BIG:
        delta[n], new_m[n], new_v[n] = _adamw("adamw_" + n, w[n], grad[n], m[n], v[n])
    small_names = [n for n in WEIGHTS if n not in BIG]
    packs = [_pack_small(**{n: src[n] for n in small_names}) for src in (w, grad, m, v)]
    for out, packed in zip((delta, new_m, new_v), _adamw("adamw_small", *packs)):
        out.update({n: a.reshape(w[n].shape) for n, a in _unpack_small(packed, conv_cols).items()})

    loss = lax.psum(loss_local[0, 0], ("x", "y", "c"))
    outs = [loss, grad_x.reshape(x.shape)]
    for group in (grad, delta, new_m, new_v):
        outs += [group[n].reshape(w[n].shape) for n in WEIGHTS]
    return tuple(outs)
```

```python
import functools

import jax
import jax.numpy as jnp
from jax import lax
from jax.experimental import pallas as pl
from jax.experimental.pallas import tpu as pltpu

F32 = jnp.float32
BF16 = jnp.bfloat16

D_MODEL = 1024
N_HEADS = 16
HEAD_DIM = 64
CHUNK = 64
N_PREV_CHUNKS = 8
MAX_REL = 256
D_FF = 4096
N_REL = 2 * MAX_REL + 1
REL_PAD = 640
EPS = 1e-6
NEG_INF = -1e30
QK_SCALE = HEAD_DIM ** -0.5

SUPER = 4 * CHUNK
BAND = SUPER + N_PREV_CHUNKS * CHUNK
SKEW_W = 1024
N_SHARD = 4
LANE = 128
VMEM_LIMIT = 48 * 1024 * 1024

ADAM_LR = 0.001
ADAM_B1 = 0.9
ADAM_B2 = 0.999
ADAM_EPS = 1e-08
ADAM_WD = 0.01
ADAM_STEP = 10

MESH = pl.DeviceIdType.MESH
NN = (((1,), (0,)), ((), ()))
NT = (((1,), (1,)), ((), ()))
TN = (((0,), (0,)), ((), ()))


def _params(*sem):
    return pltpu.CompilerParams(dimension_semantics=sem or None, vmem_limit_bytes=VMEM_LIMIT)


def _mm(name, dims, a, a_spec, b, b_spec, grid, tile, outs, epilogue=None, extras=()):
    nk, ne, no = grid[2], len(extras), len(outs)

    def body(a_ref, b_ref, *refs):
        e_refs, o_refs = refs[:ne], refs[ne:ne + no]
        part = lax.dot_general(a_ref[...], b_ref[...], dims, preferred_element_type=F32)

        def finish(acc):
            if epilogue is None:
                o_refs[0][...] = acc.astype(o_refs[0].dtype)
            else:
                epilogue(acc, [r[...] for r in e_refs], o_refs)

        if nk == 1:
            finish(part)
        else:
            acc_ref = refs[ne + no]
            k = pl.program_id(2)

            @pl.when(k == 0)
            def _():
                acc_ref[...] = part

            @pl.when(k > 0)
            def _():
                acc_ref[...] += part

            @pl.when(k == nk - 1)
            def _():
                finish(acc_ref[...])

    res = pl.pallas_call(
        body, name=name, grid=grid,
        in_specs=[a_spec, b_spec] + [s for _, s in extras],
        out_specs=[s for _, s in outs],
        out_shape=[s for s, _ in outs],
        scratch_shapes=[pltpu.VMEM(tile, F32)] if nk > 1 else [],
        compiler_params=_params("parallel", "parallel", "arbitrary"),
    )(a, b, *[e for e, _ in extras])
    return res[0] if no == 1 else res


def _tile_spec(tm, tn, col0=0):
    return pl.BlockSpec((tm, tn), lambda i, j, k: (i, j + col0))


def _out2d(m, n, dtype, tm, tn):
    return (jax.ShapeDtypeStruct((m, n), dtype), _tile_spec(tm, tn))


def _mm_fwd(name, a, w, tm, tn, tk, outs=None, epilogue=None, extras=(), col0=0, ncols=None):
    m, kdim = a.shape
    if w.ndim == 3:
        per = w.shape[2] // tn
        n = ncols or N_SHARD * w.shape[2]
        w_spec = pl.BlockSpec((None, tk, tn), lambda i, j, k: ((j + col0) // per, k, (j + col0) % per))
    else:
        n = ncols or w.shape[1]
        w_spec = pl.BlockSpec((tk, tn), lambda i, j, k: (k, j + col0))
    if outs is None:
        outs = [_out2d(m, n, F32, tm, tn)]
    return _mm(name, NN, a, pl.BlockSpec((tm, tk), lambda i, j, k: (i, k)), w, w_spec,
               (m // tm, n // tn, kdim // tk), (tm, tn), outs, epilogue, extras)


def _mm_bwd_x(name, g, g_spec, w, tm, tj, tc, m, n_contract, outs=None, epilogue=None, extras=()):
    if w.ndim == 3:
        per = w.shape[2] // tc
        kdim = w.shape[1]
        w_spec = pl.BlockSpec((None, tj, tc), lambda i, j, n: (n // per, j, n % per))
    else:
        kdim = w.shape[0]
        w_spec = pl.BlockSpec((tj, tc), lambda i, j, n: (j, n))
    if outs is None:
        outs = [_out2d(m, kdim, F32, tm, tj)]
    return _mm(name, NT, g, g_spec, w, w_spec, (m // tm, kdim // tj, n_contract // tc),
               (tm, tj), outs, epilogue, extras)


def _mm_bwd_w(name, a, g, g_spec, n, tk, tn, tm, sharded):
    m, kdim = a.shape
    if sharded:
        per = (n // N_SHARD) // tn
        out = (jax.ShapeDtypeStruct((N_SHARD, kdim, n // N_SHARD), F32),
               pl.BlockSpec((None, tk, tn), lambda i, j, mm: (j // per, i, j % per)))
    else:
        out = (jax.ShapeDtypeStruct((kdim, n), F32), pl.BlockSpec((tk, tn), lambda i, j, mm: (i, j)))
    return _mm(name, TN, a, pl.BlockSpec((tm, tk), lambda i, j, mm: (mm, i)), g, g_spec,
               (kdim // tk, n // tn, m // tm), (tk, tn), [out])


def _ew(name, fn, tiles, fulls, outs, sums=(), ts=512):
    tiles = [t if isinstance(t, tuple) else (t, pl.BlockSpec((ts, t.shape[1]), lambda i: (i, 0)))
             for t in tiles]
    s_rows = tiles[0][0].shape[-2]
    nt, nf, no = len(tiles), len(fulls), len(outs)

    def body(*refs):
        t_vals = [r[...] for r in refs[:nt]]
        f_vals = [r[...] for r in refs[nt:nt + nf]]
        o_refs, s_refs = refs[nt + nf:nt + nf + no], refs[nt + nf + no:]
        o_vals, s_vals = fn(t_vals, f_vals)
        for r, v in zip(o_refs, o_vals):
            r[...] = v.astype(r.dtype)
        for r, v in zip(s_refs, s_vals):
            part = jnp.sum(v, axis=0, keepdims=True)

            @pl.when(pl.program_id(0) == 0)
            def _():
                r[...] = part

            @pl.when(pl.program_id(0) > 0)
            def _():
                r[...] += part

    full_specs = [pl.BlockSpec(f.shape, lambda i, nd=f.ndim: (0,) * nd) for f in fulls]
    res = pl.pallas_call(
        body, name=name, grid=(s_rows // ts,),
        in_specs=[s for _, s in tiles] + full_specs,
        out_specs=[pl.BlockSpec((ts, c), lambda i: (i, 0)) for c, _ in outs]
        + [pl.BlockSpec((1, c), lambda i: (0, 0)) for c in sums],
        out_shape=[jax.ShapeDtypeStruct((s_rows, c), dt) for c, dt in outs]
        + [jax.ShapeDtypeStruct((1, c), F32) for c in sums],
        compiler_params=_params("arbitrary"),
    )(*[t for t, _ in tiles], *fulls)
    return res


def _rms_fwd(name, x, g):
    def fn(t, f):
        xv = t[0]
        r = lax.rsqrt(jnp.mean(xv * xv, axis=-1, keepdims=True) + EPS)
        return [xv * r * f[0]], []
    return _ew(name, fn, [x], [g], [(x.shape[1], BF16)])[0]


def _rms_bwd(name, dh, x, g, dres):
    def fn(t, f):
        dhv, xv, dr = t
        r = lax.rsqrt(jnp.mean(xv * xv, axis=-1, keepdims=True) + EPS)
        xhat = xv * r
        u = dhv * f[0]
        dx = r * (u - xhat * jnp.mean(u * xhat, axis=-1, keepdims=True)) + dr
        return [dx, dx], [dhv * xhat]
    c = x.shape[1]
    return _ew(name, fn, [dh, x, dres], [g], [(c, F32), (c, BF16)], sums=[c])


def _split3(x):
    x1 = x.astype(BF16)
    r1 = x - x1.astype(F32)
    x2 = r1.astype(BF16)
    x3 = (r1 - x2.astype(F32)).astype(BF16)
    return x1, x2, x3


def _rel_class(cp):
    far = (cp < MAX_REL) | (cp > BAND)
    return jnp.where(far, 2 * MAX_REL, BAND - cp)


def _skew_rows(x, sign):
    row = lax.broadcasted_iota(jnp.int32, x.shape, 0)
    for b in range(SUPER.bit_length() - 1):
        shift = (1 << b) if sign > 0 else SKEW_W - (1 << b)
        x = jnp.where((row >> b) & 1 == 1, pltpu.roll(x, shift, 1), x)
    return x


def _bias_expand(rel_bias):
    rel = jnp.pad(rel_bias, ((0, 0), (0, REL_PAD - N_REL))).reshape(N_HEADS, 1, REL_PAD)

    def body(rel_ref, o_ref):
        cls = lax.broadcasted_iota(jnp.int32, (REL_PAD, SKEW_W), 0)
        cp = lax.broadcasted_iota(jnp.int32, (REL_PAD, SKEW_W), 1)
        onehot = (cls == _rel_class(cp)).astype(BF16)
        rel8 = jnp.broadcast_to(rel_ref[...], (8, REL_PAD))
        trow = sum(jnp.dot(p, onehot, preferred_element_type=F32) for p in _split3(rel8))[0:1]
        full = _skew_rows(jnp.broadcast_to(trow, (SUPER, SKEW_W)), +1)[:, :BAND]
        qc = lax.broadcasted_iota(jnp.int32, (SUPER, BAND), 0) // CHUNK
        kc = lax.broadcasted_iota(jnp.int32, (SUPER, BAND), 1) // CHUNK
        on_band = (kc >= qc) & (kc <= qc + N_PREV_CHUNKS)
        o_ref[...] = jnp.where(on_band, full, NEG_INF)

    return pl.pallas_call(
        body, name="bias_expand", grid=(N_HEADS,),
        in_specs=[pl.BlockSpec((None, 1, REL_PAD), lambda h: (h, 0, 0))],
        out_specs=pl.BlockSpec((None, SUPER, BAND), lambda h: (h, 0, 0)),
        out_shape=jax.ShapeDtypeStruct((N_HEADS, SUPER, BAND), F32),
        compiler_params=_params("arbitrary"),
    )(rel)


def _bias_reduce(dbias):
    def body(d_ref, o_ref):
        x = jnp.concatenate([d_ref[...], jnp.zeros((SUPER, SKEW_W - BAND), F32)], axis=1)
        diag = jnp.sum(_skew_rows(x, -1), axis=0, keepdims=True)
        cp = lax.broadcasted_iota(jnp.int32, (SKEW_W, REL_PAD), 0)
        cls = lax.broadcasted_iota(jnp.int32, (SKEW_W, REL_PAD), 1)
        onehot = (cls == _rel_class(cp)).astype(BF16)
        diag8 = jnp.broadcast_to(diag, (8, SKEW_W))
        o_ref[...] = sum(jnp.dot(p, onehot, preferred_element_type=F32) for p in _split3(diag8))[0:1]

    out = pl.pallas_call(
        body, name="bias_reduce", grid=(N_HEADS,),
        in_specs=[pl.BlockSpec((None, SUPER, BAND), lambda h: (h, 0, 0))],
        out_specs=pl.BlockSpec((None, 1, REL_PAD), lambda h: (h, 0, 0)),
        out_shape=jax.ShapeDtypeStruct((N_HEADS, 1, REL_PAD), F32),
        compiler_params=_params("arbitrary"),
    )(dbias)
    return out.reshape(N_HEADS, REL_PAD)[:, :N_REL]


HEADS_PER_STEP = 4
HEAD_COLS = HEADS_PER_STEP * HEAD_DIM
N_HEAD_GROUPS = N_HEADS // HEADS_PER_STEP


def _unit(x):
    r = lax.rsqrt(jnp.mean(x * x, axis=-1, keepdims=True) + EPS)
    return x * r, r


def _probs(qs, kn, bias, dead):
    s = jnp.concatenate([lax.dot_general(qs, k, NT, preferred_element_type=F32) for k in kn], axis=1) + bias
    s = jnp.where(dead, NEG_INF, s)
    e = jnp.exp(s - jnp.max(s, axis=-1, keepdims=True))
    return e / jnp.sum(e, axis=-1, keepdims=True)


def _band_specs(nb, col0, clamp_hi):
    def spec(d):
        def index(hg, i):
            blk = jnp.maximum(i - d, 0)
            if clamp_hi:
                blk = jnp.minimum(blk, nb - 1)
            return (blk, col0 + hg)
        return pl.BlockSpec((SUPER, HEAD_COLS), index)
    return [spec(2), spec(1), spec(0)]


def _head_sums(y):
    cols = y.shape[1]
    same_head = (lax.broadcasted_iota(jnp.int32, (cols, cols), 0) // HEAD_DIM
                 == lax.broadcasted_iota(jnp.int32, (cols, cols), 1) // HEAD_DIM).astype(BF16)
    hi = y.astype(BF16)
    lo = (y - hi.astype(F32)).astype(BF16)
    return (jnp.dot(hi, same_head, preferred_element_type=F32)
            + jnp.dot(lo, same_head, preferred_element_type=F32))


def _head_unit(x):
    r = lax.rsqrt(_head_sums(x * x) * (1.0 / HEAD_DIM) + EPS)
    return x * r, r


def _head(hh):
    return slice(HEAD_DIM * hh, HEAD_DIM * (hh + 1))


def _key_block(j):
    return slice(SUPER * j, SUPER * (j + 1))


def _dead_keys(i):
    col = lax.broadcasted_iota(jnp.int32, (SUPER, BAND), 1)
    return col < (2 - i) * SUPER


def _attn_fwd(qkn, v, bias):
    s_len = qkn.shape[0]
    nb = s_len // SUPER

    def body(q_ref, k0, k1, k2, v0, v1, v2, b_ref, o_ref):
        dead = _dead_keys(pl.program_id(1))
        outs = []
        for hh in range(HEADS_PER_STEP):
            sl = _head(hh)
            p = _probs(q_ref[:, sl], [k0[:, sl], k1[:, sl], k2[:, sl]], b_ref[hh], dead).astype(BF16)
            outs.append(sum(jnp.dot(p[:, _key_block(j)], vj[:, sl], preferred_element_type=F32)
                            for j, vj in enumerate((v0, v1, v2))))
        o_ref[...] = jnp.concatenate(outs, axis=1).astype(o_ref.dtype)

    return pl.pallas_call(
        body, name="attn_fwd", grid=(N_HEAD_GROUPS, nb),
        in_specs=[pl.BlockSpec((SUPER, HEAD_COLS), lambda hg, i: (i, hg))]
        + _band_specs(nb, N_HEAD_GROUPS, False) + _band_specs(nb, 0, False)
        + [pl.BlockSpec((HEADS_PER_STEP, SUPER, BAND), lambda hg, i: (hg, 0, 0))],
        out_specs=pl.BlockSpec((SUPER, HEAD_COLS), lambda hg, i: (i, hg)),
        out_shape=jax.ShapeDtypeStruct((s_len, D_MODEL), BF16),
        compiler_params=_params("parallel", "arbitrary"),
    )(qkn, qkn, qkn, qkn, v, v, v, bias)


def _attn_bwd(qkn, v, d_out, bias):
    s_len = qkn.shape[0]
    nb = s_len // SUPER

    def body(q_ref, k0, k1, k2, v0, v1, v2, do_ref, b_ref, dp_ref, db_ref, aq_ref, ak_ref, av_ref):
        i = pl.program_id(1)

        @pl.when(i == 0)
        def _():
            aq_ref[...] = jnp.zeros_like(aq_ref)
            ak_ref[...] = jnp.zeros_like(ak_ref)
            av_ref[...] = jnp.zeros_like(av_ref)
            db_ref[...] = jnp.zeros_like(db_ref)

        @pl.when(i < nb)
        def _():
            dead = _dead_keys(i)
            dq, dk, dv = [], [[], [], []], [[], [], []]
            for hh in range(HEADS_PER_STEP):
                sl = _head(hh)
                qs, do = q_ref[:, sl], do_ref[:, sl]
                kn = [k0[:, sl], k1[:, sl], k2[:, sl]]
                vv = [v0[:, sl], v1[:, sl], v2[:, sl]]
                p = _probs(qs, kn, b_ref[hh], dead)
                pb = p.astype(BF16)
                dp = jnp.concatenate([lax.dot_general(do, vj, NT, preferred_element_type=F32) for vj in vv], axis=1)
                ds = p * (dp - jnp.sum(p * dp, axis=-1, keepdims=True))
                db_ref[hh] += ds
                dsb = ds.astype(BF16)
                dq.append(sum(jnp.dot(dsb[:, _key_block(j)], kn[j], preferred_element_type=F32) for j in range(3)))
                for j in range(3):
                    dv[j].append(lax.dot_general(pb[:, _key_block(j)], do, TN, preferred_element_type=F32))
                    dk[j].append(lax.dot_general(dsb[:, _key_block(j)], qs, TN, preferred_element_type=F32))
            aq_ref[i % 3] = jnp.concatenate(dq, axis=1)
            for j in range(3):
                slot = (i + 1 + j) % 3
                if j < 2:
                    ak_ref[slot] += jnp.concatenate(dk[j], axis=1)
                    av_ref[slot] += jnp.concatenate(dv[j], axis=1)
                else:
                    ak_ref[slot] = jnp.concatenate(dk[j], axis=1)
                    av_ref[slot] = jnp.concatenate(dv[j], axis=1)

        slot = (i + 1) % 3
        dp_ref[0] = aq_ref[slot].astype(dp_ref.dtype)
        dp_ref[1] = ak_ref[slot].astype(dp_ref.dtype)
        dp_ref[2] = av_ref[slot].astype(dp_ref.dtype)

    def qrow(hg, i):
        return (jnp.minimum(i, nb - 1), hg)

    return pl.pallas_call(
        body, name="attn_bwd", grid=(N_HEAD_GROUPS, nb + 2),
        in_specs=[pl.BlockSpec((SUPER, HEAD_COLS), qrow)]
        + _band_specs(nb, N_HEAD_GROUPS, True) + _band_specs(nb, 0, True)
        + [pl.BlockSpec((SUPER, HEAD_COLS), qrow),
           pl.BlockSpec((HEADS_PER_STEP, SUPER, BAND), lambda hg, i: (hg, 0, 0))],
        out_specs=[pl.BlockSpec((3, SUPER, HEAD_COLS), lambda hg, i: (0, jnp.maximum(i - 2, 0), hg)),
                   pl.BlockSpec((HEADS_PER_STEP, SUPER, BAND), lambda hg, i: (hg, 0, 0))],
        out_shape=[jax.ShapeDtypeStruct((6, s_len, D_MODEL), BF16),
                   jax.ShapeDtypeStruct((N_HEADS, SUPER, BAND), F32)],
        scratch_shapes=[pltpu.VMEM((3, SUPER, HEAD_COLS), F32)] * 3,
        compiler_params=_params("parallel", "arbitrary"),
    )(qkn, qkn, qkn, qkn, v, v, v, d_out, bias)


def _qk_norm_bwd(dproj6, qk_raw, gq, gk):
    s_len = qk_raw.shape[0]
    ts = 256

    nsteps = s_len // ts
    half = D_MODEL // 2

    def body(d_ref, raw_ref, gq_ref, gk_ref, o_ref, dgq_ref, dgk_ref, acc_ref):
        step = pl.program_id(0)

        @pl.when(step == 0)
        def _():
            acc_ref[...] = jnp.zeros_like(acc_ref)

        for piece, (g_ref, scale) in enumerate(((gq_ref, QK_SCALE), (gk_ref, 1.0))):
            for c0 in (0, half):
                xhat, r = _head_unit(raw_ref[:, piece * D_MODEL + c0:piece * D_MODEL + c0 + half].astype(F32))
                dn = d_ref[piece, :, c0:c0 + half].astype(F32) * scale
                u = dn * g_ref[...]
                dx = r * (u - xhat * (_head_sums(u * xhat) * (1.0 / HEAD_DIM)))
                o_ref[piece, :, c0:c0 + half] = dx.astype(o_ref.dtype)
                acc_ref[piece:piece + 1, c0:c0 + half] += jnp.sum(dn * xhat, axis=0, keepdims=True)

        @pl.when(step == nsteps - 1)
        def _():
            lane = lax.broadcasted_iota(jnp.int32, (D_MODEL, LANE), 0) % HEAD_DIM
            fold = (lane == lax.broadcasted_iota(jnp.int32, (D_MODEL, LANE), 1)).astype(BF16)
            tot = sum(jnp.dot(p, fold, preferred_element_type=F32) for p in _split3(acc_ref[...]))
            dgq_ref[...] = tot[0:1, :HEAD_DIM]
            dgk_ref[...] = tot[1:2, :HEAD_DIM]

    gain = pl.BlockSpec((1, half), lambda i: (0, 0))
    small = pl.BlockSpec((1, HEAD_DIM), lambda i: (0, 0))
    per_head = lambda g: jnp.tile(g, (1, half // HEAD_DIM))
    return pl.pallas_call(
        body, name="qk_norm_bwd", grid=(nsteps,),
        in_specs=[pl.BlockSpec((2, ts, D_MODEL), lambda i: (0, i, 0)),
                  pl.BlockSpec((ts, 2 * D_MODEL), lambda i: (i, 0)), gain, gain],
        out_specs=[pl.BlockSpec((2, ts, D_MODEL), lambda i: (0, i, 0)), small, small],
        out_shape=[jax.ShapeDtypeStruct(dproj6.shape, dproj6.dtype),
                   jax.ShapeDtypeStruct((1, HEAD_DIM), F32), jax.ShapeDtypeStruct((1, HEAD_DIM), F32)],
        scratch_shapes=[pltpu.VMEM((8, D_MODEL), F32)],
        input_output_aliases={0: 0},
        compiler_params=_params("arbitrary"),
    )(dproj6, qk_raw, per_head(gq), per_head(gk))


CONV_ROWS = 512
HALO = 16


def _rows_with_halo(ref, r0, n, front, s_len):
    zeros = jnp.zeros((HALO, ref.shape[1]), F32)
    if front:
        return (jnp.concatenate([zeros, ref[0:n, :].astype(F32)], axis=0) if r0 == 0
                else ref[r0 - HALO:r0 + n, :].astype(F32))
    return (jnp.concatenate([ref[r0:r0 + n, :].astype(F32), zeros], axis=0) if r0 + n == s_len
            else ref[r0:r0 + n + HALO, :].astype(F32))


def _earlier(ext, k):
    return pltpu.roll(ext, k, 0)[HALO:]


def _later(ext, k):
    n = ext.shape[0]
    return pltpu.roll(ext, n - k, 0)[:n - HALO]


def _conv_cols(col0):
    return lambda s_len: pl.BlockSpec((s_len, LANE), lambda j: (0, col0 + j))


def _conv_fwd(proj, conv_w, conv_b):
    s_len = proj.shape[0]

    def body(bg_ref, cg_ref, xc_ref, w_ref, b_ref, o_ref):
        w = [w_ref[t:t + 1, :] for t in range(3)]
        for r0 in range(0, s_len, CONV_ROWS):
            u = _rows_with_halo(cg_ref, r0, CONV_ROWS, True, s_len) * \
                _rows_with_halo(xc_ref, r0, CONV_ROWS, True, s_len)
            conv = b_ref[...] + w[0] * _earlier(u, 2) + w[1] * _earlier(u, 1) + w[2] * u[HALO:]
            o_ref[r0:r0 + CONV_ROWS, :] = (bg_ref[r0:r0 + CONV_ROWS, :].astype(F32) * conv).astype(o_ref.dtype)

    return pl.pallas_call(
        body, name="conv_fwd", grid=(D_MODEL // LANE,),
        in_specs=[_conv_cols(0)(s_len), _conv_cols(8)(s_len), _conv_cols(16)(s_len),
                  pl.BlockSpec((3, LANE), lambda j: (0, j)), pl.BlockSpec((1, LANE), lambda j: (0, j))],
        out_specs=pl.BlockSpec((s_len, LANE), lambda j: (0, j)),
        out_shape=jax.ShapeDtypeStruct((s_len, D_MODEL), BF16),
        compiler_params=_params("parallel"),
    )(proj, proj, proj, conv_w, conv_b)


def _conv_bwd(dproj6, dy, proj, conv_w, conv_b):
    s_len = proj.shape[0]

    def body(dy_ref, bg_ref, cg_ref, xc_ref, w_ref, b_ref, _, dp_ref, dw_ref):
        w = [w_ref[t:t + 1, :] for t in range(3)]
        acc = [jnp.zeros((1, LANE), F32) for _ in range(4)]
        for r0 in range(0, s_len, CONV_ROWS):
            rows = slice(r0, r0 + CONV_ROWS)
            u = _rows_with_halo(cg_ref, r0, CONV_ROWS, True, s_len) * \
                _rows_with_halo(xc_ref, r0, CONV_ROWS, True, s_len)
            u2, u1, u0 = _earlier(u, 2), _earlier(u, 1), u[HALO:]
            conv = b_ref[...] + w[0] * u2 + w[1] * u1 + w[2] * u0
            dp_ref[0, rows, :] = (dy_ref[rows, :].astype(F32) * conv).astype(dp_ref.dtype)
            dconv_ext = _rows_with_halo(dy_ref, r0, CONV_ROWS, False, s_len) * \
                _rows_with_halo(bg_ref, r0, CONV_ROWS, False, s_len)
            dconv = dconv_ext[:CONV_ROWS]
            for t, term in enumerate([dconv * u2, dconv * u1, dconv * u0, dconv]):
                acc[t] = acc[t] + jnp.sum(term, axis=0, keepdims=True)
            du = w[2] * dconv + w[1] * _later(dconv_ext, 1) + w[0] * _later(dconv_ext, 2)
            dp_ref[1, rows, :] = (du * xc_ref[rows, :].astype(F32)).astype(dp_ref.dtype)
            dp_ref[2, rows, :] = (du * cg_ref[rows, :].astype(F32)).astype(dp_ref.dtype)
        dw_ref[...] = jnp.zeros_like(dw_ref)
        for t in range(4):
            dw_ref[t:t + 1, :] = acc[t]

    return pl.pallas_call(
        body, name="conv_bwd", grid=(D_MODEL // LANE,),
        in_specs=[pl.BlockSpec((s_len, LANE), lambda j: (0, j)),
                  _conv_cols(0)(s_len), _conv_cols(8)(s_len), _conv_cols(16)(s_len),
                  pl.BlockSpec((3, LANE), lambda j: (0, j)), pl.BlockSpec((1, LANE), lambda j: (0, j)),
                  pl.BlockSpec(memory_space=pl.ANY)],
        out_specs=[pl.BlockSpec((3, s_len, LANE), lambda j: (1, 0, j)),
                   pl.BlockSpec((8, LANE), lambda j: (0, j))],
        out_shape=[jax.ShapeDtypeStruct(dproj6.shape, dproj6.dtype),
                   jax.ShapeDtypeStruct((8, D_MODEL), F32)],
        input_output_aliases={6: 0},
        compiler_params=_params("parallel"),
    )(dy, proj, proj, proj, conv_w, conv_b, dproj6)


def _local_grads(x, target, norm1_g, w_in3, q_norm_g, k_norm_g, rel_bias, conv_w, conv_b,
                 w_ap, w_cp, w_gate3, b_gate, w_out, norm2_g, w_up3, w_down):
    s_len = x.shape[0]
    tm = min(1024, s_len)
    row = lambda v: v.reshape(1, -1)

    h = _rms_fwd("norm1", x, row(norm1_g))
    gq, gk = row(q_norm_g), row(k_norm_g)

    def qk_epi(acc, e, o):
        o[0][...] = acc.astype(BF16)
        gain = jnp.where(pl.program_id(1) < 2, e[0] * QK_SCALE, e[1])
        o[1][...] = (_head_unit(acc)[0] * gain).astype(BF16)
    small = pl.BlockSpec((1, 512), lambda i, j, k: (0, 0))
    per_head = lambda g: jnp.tile(g, (1, 512 // HEAD_DIM))
    qk_raw, qkn = _mm_fwd("proj_qk", h, w_in3, tm, 512, D_MODEL, ncols=2 * D_MODEL, epilogue=qk_epi,
                          outs=[_out2d(s_len, 2 * D_MODEL, BF16, tm, 512)] * 2,
                          extras=[(per_head(gq), small), (per_head(gk), small)])
    v = _mm_fwd("proj_v", h, w_in3, tm, 512, D_MODEL, col0=4, ncols=D_MODEL,
                outs=[_out2d(s_len, D_MODEL, BF16, tm, 512)])
    conv_in = _mm_fwd("proj_conv", h, w_in3, tm, 1536, D_MODEL, col0=2, ncols=3 * D_MODEL,
                      outs=[_out2d(s_len, 3 * D_MODEL, BF16, tm, 1536)])

    def gate_epi(acc, e, o):
        o[0][...] = jax.nn.sigmoid(acc + e[0]).astype(BF16)
    gates = _mm_fwd("gates", h, w_gate3, tm, 512, D_MODEL, epilogue=gate_epi,
                    outs=[_out2d(s_len, 2 * D_MODEL, BF16, tm, 512)],
                    extras=[(row(b_gate), pl.BlockSpec((1, 512), lambda i, j, k: (0, j)))])

    bias = _bias_expand(rel_bias)
    attn = _attn_fwd(qkn, v, bias)
    yconv = _conv_fwd(conv_in, conv_w, row(conv_b))
    tw = 1024
    ya = _mm_fwd("attn_proj", attn, w_ap, tm, tw, D_MODEL, outs=[_out2d(s_len, D_MODEL, BF16, tm, tw)])

    def merge_epi(acc, e, o):
        ya_v, ga, gc = [t.astype(F32) for t in e]
        o[0][...] = acc.astype(BF16)
        o[1][...] = (ga * ya_v + gc * acc).astype(BF16)
    gate_a, gate_c = (gates, _tile_spec(tm, tw, 0)), (gates, _tile_spec(tm, tw, 1))
    yc, merged = _mm_fwd("conv_proj", yconv, w_cp, tm, tw, D_MODEL, epilogue=merge_epi,
                         outs=[_out2d(s_len, D_MODEL, BF16, tm, tw), _out2d(s_len, D_MODEL, BF16, tm, tw)],
                         extras=[(ya, _tile_spec(tm, tw)), gate_a, gate_c])

    def res_epi(acc, e, o):
        o[0][...] = e[0] + acc
    x1 = _mm_fwd("out_proj", merged, w_out, tm, tw, D_MODEL, epilogue=res_epi,
                 extras=[(x, _tile_spec(tm, tw))])
    h2 = _rms_fwd("norm2", x1, row(norm2_g))

    def up_epi(acc, e, o):
        o[0][...] = acc.astype(BF16)
        o[1][...] = jnp.square(jnp.maximum(acc, 0.0)).astype(BF16)
    up, act = _mm_fwd("mlp_up", h2, w_up3, tm, tw, D_MODEL, epilogue=up_epi,
                      outs=[_out2d(s_len, D_FF, BF16, tm, tw), _out2d(s_len, D_FF, BF16, tm, tw)])

    def loss_epi(acc, e, o):
        err = e[0] + acc - e[1]
        dy = err * (1.0 / D_MODEL)
        o[0][...] = dy
        o[1][...] = dy.astype(BF16)
        sq = err * err
        part = sq[:, 0:LANE]
        for c0 in range(LANE, D_MODEL, LANE):
            part = part + sq[:, c0:c0 + LANE]
        o[2][...] = jnp.sum(part.reshape(tl // 8, 8, LANE), axis=0)
    tl = 512
    dy, dy_b, loss_part = _mm_fwd(
        "mlp_down", act, w_down, tl, D_MODEL, D_FF, epilogue=loss_epi,
        outs=[_out2d(s_len, D_MODEL, F32, tl, D_MODEL), _out2d(s_len, D_MODEL, BF16, tl, D_MODEL),
              (jax.ShapeDtypeStruct((8 * (s_len // tl), LANE), F32), pl.BlockSpec((8, LANE), lambda i, j, k: (i, 0)))],
        extras=[(x1, _tile_spec(tl, D_MODEL)), (target, _tile_spec(tl, D_MODEL))])

    def dup_epi(acc, e, o):
        o[0][...] = (acc * (2.0 * jnp.maximum(e[0].astype(F32), 0.0))).astype(BF16)
    full = lambda cols: pl.BlockSpec((tm, cols), lambda i, j, n: (i, n))
    tokens = lambda cols: pl.BlockSpec((s_len, cols), lambda i, j, m: (m, j))
    dup = _mm_bwd_x("d_act", dy_b, full(D_MODEL), w_down, tm, tw, D_MODEL, s_len, D_MODEL, epilogue=dup_epi,
                    outs=[_out2d(s_len, D_FF, BF16, tm, tw)], extras=[(up, _tile_spec(tm, tw))])
    g_down = _mm_bwd_w("g_down", act, dy_b, tokens(D_MODEL), D_MODEL, 512, D_MODEL, s_len, False)
    g_up = _mm_bwd_w("g_up", h2, dup, tokens(512), D_FF, D_MODEL, 512, s_len, True)
    dh2 = _mm_bwd_x("d_h2", dup, full(1024), w_up3, tm, tw, 1024, s_len, D_FF)
    dx1, dx1_b, dg2 = _rms_bwd("norm2_bwd", dh2, x1, row(norm2_g), dy)

    def dmerge_epi(acc, e, o):
        ya_v, yc_v, ga, gc = [t.astype(F32) for t in e]
        o[0][...] = (acc * ga).astype(BF16)
        o[1][...] = (acc * gc).astype(BF16)
        o[2][0] = (acc * ya_v * ga * (1.0 - ga)).astype(BF16)
        o[2][1] = (acc * yc_v * gc * (1.0 - gc)).astype(BF16)
    dya, dyc, dgp2 = _mm_bwd_x(
        "d_merged", dx1_b, full(D_MODEL), w_out, tm, tw, D_MODEL, s_len, D_MODEL, epilogue=dmerge_epi,
        outs=[_out2d(s_len, D_MODEL, BF16, tm, tw), _out2d(s_len, D_MODEL, BF16, tm, tw),
              (jax.ShapeDtypeStruct((2, s_len, D_MODEL), BF16), pl.BlockSpec((2, tm, tw), lambda i, j, n: (0, i, j)))],
        extras=[(ya, _tile_spec(tm, tw)), (yc, _tile_spec(tm, tw)), gate_a, gate_c])
    g_out = _mm_bwd_w("g_out", merged, dx1_b, tokens(512), D_MODEL, D_MODEL, 512, s_len, False)
    d_attn = _mm_bwd_x("d_attn", dya, full(D_MODEL), w_ap, tm, tw, D_MODEL, s_len, D_MODEL,
                       outs=[_out2d(s_len, D_MODEL, BF16, tm, tw)])
    g_ap = _mm_bwd_w("g_attn_proj", attn, dya, tokens(512), D_MODEL, D_MODEL, 512, s_len, False)
    d_yconv = _mm_bwd_x("d_yconv", dyc, full(D_MODEL), w_cp, tm, tw, D_MODEL, s_len, D_MODEL,
                        outs=[_out2d(s_len, D_MODEL, BF16, tm, tw)])
    g_cp = _mm_bwd_w("g_conv_proj", yconv, dyc, tokens(512), D_MODEL, D_MODEL, 512, s_len, False)

    dproj6, dbias = _attn_bwd(qkn, v, d_attn, bias)
    dproj6, dgq, dgk = _qk_norm_bwd(dproj6, qk_raw, gq, gk)
    dproj6, dconv_wb = _conv_bwd(dproj6, d_yconv, conv_in, conv_w, row(conv_b))
    d_rel = _bias_reduce(dbias)

    piece = lambda width: (lambda blk: (blk * width) // D_MODEL, lambda blk: (blk * width % D_MODEL) // width)
    pc, cb = piece(512)
    pieces = pl.BlockSpec((None, s_len, 512), lambda i, j, m: (pc(j), m, cb(j)))
    g_in = _mm_bwd_w("g_in", h, dproj6, pieces, 6 * D_MODEL, D_MODEL, 512, s_len, True)
    g_gate = _mm_bwd_w("g_gate", h, dgp2, pieces, 2 * D_MODEL, D_MODEL, 512, s_len, True)
    dh_a = _mm_bwd_x("d_h_proj", dproj6, pl.BlockSpec((None, tm, 512), lambda i, j, n: (pc(n), i, cb(n))),
                     w_in3, tm, tw, 512, s_len, 6 * D_MODEL)

    def add_epi(acc, e, o):
        o[0][...] = acc + e[0]
    dh = _mm_bwd_x("d_h_gate", dgp2, pl.BlockSpec((None, tm, 512), lambda i, j, n: (pc(n), i, cb(n))),
                   w_gate3, tm, tw, 512, s_len, 2 * D_MODEL, epilogue=add_epi,
                   extras=[(dh_a, _tile_spec(tm, tw))])
    grad_x, _, dg1 = _rms_bwd("norm1_bwd", dh, x, row(norm1_g), dx1)

    def bsum(t, f):
        return [], [t[0].astype(F32), t[1].astype(F32)]
    ts = 512
    db_a, db_c = _ew("b_gate_sum", bsum,
                     [(dgp2, pl.BlockSpec((None, ts, D_MODEL), lambda i: (0, i, 0))),
                      (dgp2, pl.BlockSpec((None, ts, D_MODEL), lambda i: (1, i, 0)))],
                     [], [], sums=[D_MODEL, D_MODEL], ts=ts)

    big = dict(w_in=g_in, w_attn_proj=g_ap, w_conv_proj=g_cp, w_gate=g_gate, w_out=g_out,
               w_up=g_up, w_down=g_down)
    small = dict(norm1_g=dg1, norm2_g=dg2, conv_wb=dconv_wb, b_gate=(db_a, db_c),
                 q_norm_g=dgq, k_norm_g=dgk, rel_bias=d_rel)
    return loss_part, grad_x, big, small


def _finish_loss(loss_part):
    def body(l_ref, lo_ref):
        total = jnp.sum(jnp.sum(l_ref[...], axis=0, keepdims=True), axis=1, keepdims=True)
        lo_ref[...] = jnp.broadcast_to(total * (0.5 / D_MODEL), lo_ref.shape)

    return pl.pallas_call(body, name="finish_loss", out_shape=jax.ShapeDtypeStruct((8, LANE), F32))(loss_part)


HBM_SPEC = pl.BlockSpec(memory_space=pl.ANY)


def _place():
    return lax.axis_index("x"), lax.axis_index("y"), lax.axis_index("c")


def _other_chips(x, y):
    return [(1 - x, y), (x, 1 - y), (1 - x, 1 - y)]


def _cast_into_slot(name, where, w):
    r, cols = w.shape
    ts = 256

    def body(w_ref, x_ref, o_ref):
        o_ref[...] = x_ref[...].astype(o_ref.dtype)

    return pl.pallas_call(
        body, name=name,
        grid_spec=pltpu.PrefetchScalarGridSpec(
            num_scalar_prefetch=1, grid=(r // ts,),
            in_specs=[pl.BlockSpec((ts, cols), lambda i, w: (i, 0))],
            out_specs=pl.BlockSpec((None, ts, cols), lambda i, w: (w[0], i, 0))),
        out_shape=jax.ShapeDtypeStruct((N_SHARD, r, cols), BF16),
        compiler_params=_params("parallel"),
    )(where, w)


def _all_gather(slots, small):
    n = len(slots)

    def body(*refs):
        outs, small_out = refs[n + 1:2 * n + 1], refs[2 * n + 1]
        send1, recv1, send2, recv2, send3, recv3 = refs[2 * n + 2:]
        x, y, c = _place()
        me = 2 * x + y
        chips = _other_chips(x, y)
        sibling = (x, y, 1 - c)

        def half(w, shard, which):
            hr = slots[w].shape[1] // 2
            return outs[w].at[shard, pl.ds(which * hr, hr)]

        def remote(ref, send, recv, k, to):
            return pltpu.make_async_remote_copy(src_ref=ref, dst_ref=ref, send_sem=send.at[k],
                                                recv_sem=recv.at[k], device_id=to, device_id_type=MESH)

        sent = [remote(half(w, me, c), send1, recv1, 3 * w + j, (*chip, c))
                for w in range(n) for j, chip in enumerate(chips)]
        sent += [remote(small_out.at[me], send3, recv3, j, (*chip, c)) for j, chip in enumerate(chips)]
        for cp in sent:
            cp.start()
        for w in range(n):
            for j, chip in enumerate(chips):
                landed = half(w, 2 * chip[0] + chip[1], c)
                remote(landed, send1, recv1, 3 * w + j, (*chip, c)).wait_recv()
                fwd = remote(landed, send2, recv2, 3 * w + j, sibling)
                fwd.start()
                sent.append(fwd)
        for w in range(n):
            for j, chip in enumerate(chips):
                remote(half(w, 2 * chip[0] + chip[1], 1 - c), send2, recv2, 3 * w + j, sibling).wait_recv()
        for j, chip in enumerate(chips):
            remote(small_out.at[2 * chip[0] + chip[1]], send3, recv3, j, (*chip, c)).wait_recv()
        for cp in sent:
            cp.wait_send()

    return pl.pallas_call(
        body, name="gather_weights",
        in_specs=[HBM_SPEC] * (n + 1), out_specs=[HBM_SPEC] * (n + 1),
        out_shape=[jax.ShapeDtypeStruct(s.shape, s.dtype) for s in slots]
        + [jax.ShapeDtypeStruct(small.shape, small.dtype)],
        input_output_aliases={k: k for k in range(n + 1)},
        scratch_shapes=[pltpu.SemaphoreType.DMA((3 * n,))] * 4 + [pltpu.SemaphoreType.DMA((3,))] * 2,
    )(*slots, small)


def _pair_exchange(grads):
    n = len(grads)

    def body(*refs):
        ins, outs = refs[:n], refs[n:2 * n]
        send, recv = refs[2 * n:]
        x, y, c = _place()
        copies = []
        for w in range(n):
            hr = grads[w].shape[1] // 2
            copies.append(pltpu.make_async_remote_copy(
                src_ref=ins[w].at[:, pl.ds((1 - c) * hr, hr)], dst_ref=outs[w],
                send_sem=send.at[w], recv_sem=recv.at[w], device_id=(x, y, 1 - c), device_id_type=MESH))
        for cp in copies:
            cp.start()
        for cp in copies:
            cp.wait()

    return pl.pallas_call(
        body, name="pair_exchange",
        in_specs=[HBM_SPEC] * n, out_specs=[HBM_SPEC] * n,
        out_shape=[jax.ShapeDtypeStruct((N_SHARD, g.shape[1] // 2, g.shape[2]), g.dtype) for g in grads],
        scratch_shapes=[pltpu.SemaphoreType.DMA((n,))] * 2,
    )(*grads)


def _row_tile(hr):
    return min(hr, 256)


def _pair_add(name, where, grad, got):
    _, hr, cols = got.shape
    tr = _row_tile(hr)
    nblk = hr // tr

    def body(w_ref, g_ref, r_ref, o_ref):
        o_ref[...] = (g_ref[...] + r_ref[...]).astype(o_ref.dtype)

    return pl.pallas_call(
        body, name=name,
        grid_spec=pltpu.PrefetchScalarGridSpec(
            num_scalar_prefetch=1, grid=(N_SHARD, nblk),
            in_specs=[pl.BlockSpec((None, tr, cols), lambda s, i, w: (s, w[1] * nblk + i, 0)),
                      pl.BlockSpec((None, tr, cols), lambda s, i, w: (s, i, 0))],
            out_specs=pl.BlockSpec((None, tr, cols), lambda s, i, w: (s, i, 0))),
        out_shape=jax.ShapeDtypeStruct(got.shape, BF16),
        compiler_params=_params("parallel", "parallel"),
    )(where, grad, got)


def _chip_exchange(partials):
    n = len(partials)

    def body(*refs):
        ins, outs = refs[:n], refs[n:2 * n]
        send, recv = refs[2 * n:]
        x, y, c = _place()
        copies = []
        for w in range(n):
            for j, chip in enumerate(_other_chips(x, y)):
                copies.append(pltpu.make_async_remote_copy(
                    src_ref=ins[w].at[2 * chip[0] + chip[1]], dst_ref=outs[w].at[j],
                    send_sem=send.at[3 * w + j], recv_sem=recv.at[3 * w + j],
                    device_id=(*chip, c), device_id_type=MESH))
        for cp in copies:
            cp.start()
        for cp in copies:
            cp.wait()

    return pl.pallas_call(
        body, name="chip_exchange",
        in_specs=[HBM_SPEC] * n, out_specs=[HBM_SPEC] * n,
        out_shape=[jax.ShapeDtypeStruct((3,) + p.shape[1:], p.dtype) for p in partials],
        scratch_shapes=[pltpu.SemaphoreType.DMA((3 * n,))] * 2,
    )(*partials)


def _final_add(name, where, grad, got, arrived):
    _, hr, cols = got.shape
    tr = _row_tile(hr)
    nblk = hr // tr

    def body(w_ref, g_ref, r_ref, a_ref, o_ref):
        acc = g_ref[...] + r_ref[...]
        for j in range(3):
            acc = acc + a_ref[j].astype(F32)
        o_ref[...] = acc

    return pl.pallas_call(
        body, name=name,
        grid_spec=pltpu.PrefetchScalarGridSpec(
            num_scalar_prefetch=1, grid=(nblk,),
            in_specs=[pl.BlockSpec((None, tr, cols), lambda i, w: (w[0], w[1] * nblk + i, 0)),
                      pl.BlockSpec((None, tr, cols), lambda i, w: (w[0], i, 0)),
                      pl.BlockSpec((3, tr, cols), lambda i, w: (0, i, 0))],
            out_specs=pl.BlockSpec((tr, cols), lambda i, w: (w[1] * nblk + i, 0))),
        out_shape=jax.ShapeDtypeStruct((2 * hr, cols), F32),
        compiler_params=_params("parallel"),
    )(where, grad, got, arrived)


def _pair_share(shards):
    n = len(shards)

    def body(*refs):
        outs = refs[n:2 * n]
        send, recv = refs[2 * n:]
        x, y, c = _place()

        def half(w, which):
            hr = shards[w].shape[0] // 2
            return outs[w].at[pl.ds(which * hr, hr)]

        def remote(ref, w):
            return pltpu.make_async_remote_copy(src_ref=ref, dst_ref=ref, send_sem=send.at[w], recv_sem=recv.at[w],
                                                device_id=(x, y, 1 - c), device_id_type=MESH)

        copies = [remote(half(w, c), w) for w in range(n)]
        for cp in copies:
            cp.start()
        for w in range(n):
            remote(half(w, 1 - c), w).wait_recv()
        for cp in copies:
            cp.wait_send()

    return pl.pallas_call(
        body, name="pair_share",
        in_specs=[HBM_SPEC] * n, out_specs=[HBM_SPEC] * n,
        out_shape=[jax.ShapeDtypeStruct(s.shape, s.dtype) for s in shards],
        input_output_aliases={k: k for k in range(n)},
        scratch_shapes=[pltpu.SemaphoreType.DMA((n,))] * 2,
    )(*shards)


SMALL_ROWS = 32
N_DEV = 8


def _all_reduce_small(pack):
    def body(p_ref, o_ref, buf, send, recv):
        x, y, c = _place()
        buf[4 * x + 2 * y + c] = p_ref[...]
        copies, waits = [], []
        for k in range(1, N_DEV):
            px = 1 - x if k & 4 else x
            py = 1 - y if k & 2 else y
            pc = 1 - c if k & 1 else c
            copies.append(pltpu.make_async_remote_copy(
                src_ref=p_ref, dst_ref=buf.at[4 * x + 2 * y + c], send_sem=send.at[k - 1],
                recv_sem=recv.at[k - 1], device_id=(px, py, pc), device_id_type=MESH))
            waits.append(pltpu.make_async_remote_copy(
                src_ref=p_ref, dst_ref=buf.at[4 * px + 2 * py + pc], send_sem=send.at[k - 1],
                recv_sem=recv.at[k - 1], device_id=(px, py, pc), device_id_type=MESH))
        for cp in copies:
            cp.start()
        for cp in waits:
            cp.wait_recv()
        acc = buf[0]
        for d in range(1, N_DEV):
            acc = acc + buf[d]
        o_ref[...] = acc
        for cp in copies:
            cp.wait_send()

    return pl.pallas_call(
        body, name="all_reduce_small",
        out_shape=jax.ShapeDtypeStruct(pack.shape, F32),
        scratch_shapes=[pltpu.VMEM((N_DEV,) + pack.shape, F32),
                        pltpu.SemaphoreType.DMA((N_DEV - 1,)), pltpu.SemaphoreType.DMA((N_DEV - 1,))],
    )(pack)


def _adamw(name, w, g, m, v):
    c1 = 1.0 - ADAM_B1 ** ADAM_STEP
    c2 = 1.0 - ADAM_B2 ** ADAM_STEP

    def fn(t, f):
        wv, gv, mv, vv = t
        m2 = ADAM_B1 * mv + (1.0 - ADAM_B1) * gv
        v2 = ADAM_B2 * vv + (1.0 - ADAM_B2) * (gv * gv)
        delta = -ADAM_LR * ((m2 / c1) / (jnp.sqrt(v2 / c2) + ADAM_EPS) + ADAM_WD * wv)
        return [delta, m2, v2], []

    cols = w.shape[1]
    return _ew(name, fn, [w, g, m, v], [], [(cols, F32)] * 3, ts=min(w.shape[0], 256))


def _pack_small(norm1_g, norm2_g, conv_b, b_gate, conv_w, q_norm_g, k_norm_g, rel_bias):
    pack = jnp.zeros((SMALL_ROWS, D_MODEL), F32)
    for r0, v in ((0, norm1_g), (1, norm2_g), (2, conv_b), (3, b_gate.reshape(2, D_MODEL)), (5, conv_w),
                  (8, q_norm_g), (9, k_norm_g), (10, rel_bias)):
        v = v.reshape(-1, v.shape[-1]).astype(F32)
        pack = pack.at[r0:r0 + v.shape[0], :v.shape[1]].set(v)
    return pack


def _unpack_small(pack, conv_cols):
    return dict(norm1_g=pack[0], norm2_g=pack[1], conv_b=pack[2], b_gate=pack[3:5].reshape(2 * D_MODEL),
                conv_w=pack[5:8, :conv_cols], q_norm_g=pack[8, :HEAD_DIM], k_norm_g=pack[9, :HEAD_DIM],
                rel_bias=pack[10:10 + N_HEADS, :N_REL])


BIG = ["w_in", "w_attn_proj", "w_conv_proj", "w_gate", "w_out", "w_up", "w_down"]
WEIGHTS = ["norm1_g", "w_in", "q_norm_g", "k_norm_g", "rel_bias", "conv_w", "conv_b", "w_attn_proj",
           "w_conv_proj", "w_gate", "b_gate", "w_out", "norm2_g", "w_up", "w_down"]


def kernel(x, norm1_g, w_in, q_norm_g, k_norm_g, rel_bias, conv_w, conv_b, w_attn_proj, w_conv_proj, w_gate, b_gate, w_out, norm2_g, w_up, w_down, loss_target, m_norm1_g, m_w_in, m_q_norm_g, m_k_norm_g, m_rel_bias, m_conv_w, m_conv_b, m_w_attn_proj, m_w_conv_proj, m_w_gate, m_b_gate, m_w_out, m_norm2_g, m_w_up, m_w_down, v_norm1_g, v_w_in, v_q_norm_g, v_k_norm_g, v_rel_bias, v_conv_w, v_conv_b, v_w_attn_proj, v_w_conv_proj, v_w_gate, v_b_gate, v_w_out, v_norm2_g, v_w_up, v_w_down):
    given = dict(locals())
    w = {n: given[n] for n in WEIGHTS}
    m = {n: given["m_" + n] for n in WEIGHTS}
    v = {n: given["v_" + n] for n in WEIGHTS}
    s_len = x.shape[1]
    shard = 2 * lax.axis_index("x") + lax.axis_index("y")
    where = jnp.stack([shard, lax.axis_index("c")]).astype(jnp.int32)
    conv_cols = conv_w.shape[1]

    small_in = lax.dynamic_update_slice(jnp.zeros((N_SHARD, 8, conv_cols), F32), conv_w[None], (shard, 0, 0))
    *gathered, conv_w4 = _all_gather([_cast_into_slot("cast_" + n, where, w[n]) for n in BIG], small_in)
    wg = dict(zip(BIG, gathered))
    conv_w_full = conv_w4[:, :3, :].transpose(1, 0, 2).reshape(3, N_SHARD * conv_cols)
    rows = lambda a: a.reshape(a.shape[0] * a.shape[1], a.shape[2])

    loss_part, grad_x, big, small = _local_grads(
        x.reshape(s_len, D_MODEL), loss_target.reshape(s_len, D_MODEL), norm1_g, wg["w_in"], q_norm_g,
        k_norm_g, rel_bias, conv_w_full, conv_b, rows(wg["w_attn_proj"]), rows(wg["w_conv_proj"]),
        wg["w_gate"], b_gate, rows(wg["w_out"]), norm2_g, wg["w_up"], rows(wg["w_down"]))

    g4 = [big[n] if big[n].ndim == 3 else big[n].reshape(N_SHARD, -1, big[n].shape[1]) for n in BIG]
    got = _pair_exchange(g4)
    partial = [_pair_add("pair_add_" + n, where, g, r) for n, g, r in zip(BIG, g4, got)]
    arrived = _chip_exchange(partial)
    halves = [_final_add("final_add_" + n, where, g, r, a) for n, g, r, a in zip(BIG, g4, got, arrived)]
    grad = dict(zip(BIG, _pair_share(halves)))

    loss_local = _finish_loss(loss_part)
    pack = _pack_small(small["norm1_g"], small["norm2_g"], small["conv_wb"][3], jnp.concatenate(small["b_gate"], axis=1),
                       small["conv_wb"][0:3], small["q_norm_g"], small["k_norm_g"], small["rel_bias"])
    total = _all_reduce_small(pack)
    g_small = _unpack_small(total, D_MODEL)
    g_small["conv_w"] = lax.dynamic_slice(g_small["conv_w"], (0, shard * conv_cols), (3, conv_cols))
    grad.update(g_small)

    delta, new_m, new_v = {}, {}, {}
    for n in BIG:
        delta[n], new_m[n], new_v[n] = _adamw("adamw_" + n, w[n], grad[n], m[n], v[n])
    small_names = [n for n in WEIGHTS if n not in BIG]
    packs = [_pack_small(**{n: src[n] for n in small_names}) for src in (w, grad, m, v)]
    for out, packed in zip((delta, new_m, new_v), _adamw("adamw_small", *packs)):
        out.update({n: a.reshape(w[n].shape) for n, a in _unpack_small(packed, conv_cols).items()})

    loss = lax.psum(loss_local[0, 0], ("x", "y", "c"))
    outs = [loss, grad_x.reshape(x.shape)]
    for group in (grad, delta, new_m, new_v):
        outs += [group[n].reshape(w[n].shape) for n in WEIGHTS]
    return tuple(outs)
```

```python
import functools

import jax
import jax.numpy as jnp
from jax import lax
from jax.experimental import pallas as pl
from jax.experimental.pallas import tpu as pltpu

F32 = jnp.float32
BF16 = jnp.bfloat16

D_MODEL = 1024
N_HEADS = 16
HEAD_DIM = 64
CHUNK = 64
N_PREV_CHUNKS = 8
MAX_REL = 256
D_FF = 4096
N_REL = 2 * MAX_REL + 1
REL_PAD = 640
EPS = 1e-6
NEG_INF = -1e30
QK_SCALE = HEAD_DIM ** -0.5

SUPER = 4 * CHUNK
BAND = SUPER + N_PREV_CHUNKS * CHUNK
SKEW_W = 1024
N_SHARD = 4
LANE = 128
VMEM_LIMIT = 48 * 1024 * 1024

ADAM_LR = 0.001
ADAM_B1 = 0.9
ADAM_B2 = 0.999
ADAM_EPS = 1e-08
ADAM_WD = 0.01
ADAM_STEP = 10

MESH = pl.DeviceIdType.MESH
NN = (((1,), (0,)), ((), ()))
NT = (((1,), (1,)), ((), ()))
TN = (((0,), (0,)), ((), ()))


def _params(*sem):
    return pltpu.CompilerParams(dimension_semantics=sem or None, vmem_limit_bytes=VMEM_LIMIT)


HBM_SPEC = pl.BlockSpec(memory_space=pl.ANY)


class _Rider:
    def __init__(self, sources, arrays, n_sem, start, finish):
        self.sources, self.arrays, self.n_sem, self.start, self.finish = sources, arrays, n_sem, start, finish


def _carry(riders, body, *, name, out_shape, grid=(), in_specs=None, out_specs=None, scratch_shapes=(),
           semantics=(), input_output_aliases=None):
    aliases = dict(input_output_aliases or {})
    if not riders:
        kw = {} if in_specs is None else dict(in_specs=in_specs, out_specs=out_specs)
        return pl.pallas_call(body, name=name, grid=grid, out_shape=out_shape, scratch_shapes=scratch_shapes,
                              input_output_aliases=aliases, compiler_params=_params(*semantics), **kw)
    single = not isinstance(out_shape, (list, tuple))
    shapes = [out_shape] if single else list(out_shape)
    n_out, n_scr = len(shapes), len(scratch_shapes)
    srcs = [a for r in riders for a in r.sources]
    arrs = [a for r in riders for a in r.arrays]
    vmem = pl.BlockSpec(memory_space=pltpu.VMEM)

    def run(*args):
        n_in = len(args)

        def wrapped(*refs):
            pos = n_in
            src_refs = refs[pos:pos + len(srcs)]
            pos += len(srcs) + len(arrs)
            outs = refs[pos:pos + n_out]
            pos += n_out
            arr_refs = refs[pos:pos + len(arrs)]
            pos += len(arrs)
            scratch = refs[pos:pos + n_scr]
            sems = refs[pos + n_scr:]
            first, last = True, True
            for d, size in enumerate(grid):
                first = jnp.logical_and(first, pl.program_id(d) == 0)
                last = jnp.logical_and(last, pl.program_id(d) == size - 1)

            def each(method):
                s0 = a0 = 0
                for k, r in enumerate(riders):
                    getattr(r, method)(src_refs[s0:s0 + len(r.sources)], arr_refs[a0:a0 + len(r.arrays)],
                                       sems[2 * k], sems[2 * k + 1])
                    s0, a0 = s0 + len(r.sources), a0 + len(r.arrays)

            pl.when(first)(lambda: each("start"))
            body(*refs[:n_in], *outs, *scratch)
            pl.when(last)(lambda: each("finish"))

        ins = [vmem] * n_in if in_specs is None else list(in_specs)
        if out_specs is None:
            o_specs = [vmem] * n_out
        else:
            o_specs = [out_specs] if single else list(out_specs)
        for k in range(len(arrs)):
            aliases[n_in + len(srcs) + k] = n_out + k
        res = pl.pallas_call(
            wrapped, name=name, grid=grid,
            in_specs=ins + [HBM_SPEC] * (len(srcs) + len(arrs)),
            out_specs=o_specs + [HBM_SPEC] * len(arrs),
            out_shape=shapes + [jax.ShapeDtypeStruct(a.shape, a.dtype) for a in arrs],
            scratch_shapes=list(scratch_shapes) + [pltpu.SemaphoreType.DMA((r.n_sem,)) for r in riders for _ in range(2)],
            input_output_aliases=aliases,
            compiler_params=_params(*["arbitrary"] * len(grid)),
        )(*args, *srcs, *arrs)
        core, rest = res[:n_out], list(res[n_out:])
        carried, a0 = [], 0
        for r in riders:
            carried.append(rest[a0:a0 + len(r.arrays)])
            a0 += len(r.arrays)
        return (core[0] if single else core), carried

    return run


def _mm(name, dims, a, a_spec, b, b_spec, grid, tile, outs, epilogue=None, extras=(), riders=()):
    nk, ne, no = grid[2], len(extras), len(outs)

    def body(a_ref, b_ref, *refs):
        e_refs, o_refs = refs[:ne], refs[ne:ne + no]
        part = lax.dot_general(a_ref[...], b_ref[...], dims, preferred_element_type=F32)

        def finish(acc):
            if epilogue is None:
                o_refs[0][...] = acc.astype(o_refs[0].dtype)
            else:
                epilogue(acc, [r[...] for r in e_refs], o_refs)

        if nk == 1:
            finish(part)
        else:
            acc_ref = refs[ne + no]
            k = pl.program_id(2)

            @pl.when(k == 0)
            def _():
                acc_ref[...] = part

            @pl.when(k > 0)
            def _():
                acc_ref[...] += part

            @pl.when(k == nk - 1)
            def _():
                finish(acc_ref[...])

    res = _carry(
        riders, body, name=name, grid=grid,
        in_specs=[a_spec, b_spec] + [s for _, s in extras],
        out_specs=[s for _, s in outs],
        out_shape=[s for s, _ in outs],
        scratch_shapes=[pltpu.VMEM(tile, F32)] if nk > 1 else [],
        semantics=("parallel", "parallel", "arbitrary"),
    )(a, b, *[e for e, _ in extras])
    res, carried = res if riders else (res, None)
    res = res[0] if no == 1 else res
    return (res, carried) if riders else res


def _tile_spec(tm, tn, col0=0):
    return pl.BlockSpec((tm, tn), lambda i, j, k: (i, j + col0))


def _out2d(m, n, dtype, tm, tn):
    return (jax.ShapeDtypeStruct((m, n), dtype), _tile_spec(tm, tn))


def _mm_fwd(name, a, w, tm, tn, tk, outs=None, epilogue=None, extras=(), col0=0, ncols=None, riders=()):
    m, kdim = a.shape
    if w.ndim == 3:
        per = w.shape[2] // tn
        n = ncols or N_SHARD * w.shape[2]
        w_spec = pl.BlockSpec((None, tk, tn), lambda i, j, k: ((j + col0) // per, k, (j + col0) % per))
    else:
        n = ncols or w.shape[1]
        w_spec = pl.BlockSpec((tk, tn), lambda i, j, k: (k, j + col0))
    if outs is None:
        outs = [_out2d(m, n, F32, tm, tn)]
    return _mm(name, NN, a, pl.BlockSpec((tm, tk), lambda i, j, k: (i, k)), w, w_spec,
               (m // tm, n // tn, kdim // tk), (tm, tn), outs, epilogue, extras, riders)


def _mm_bwd_x(name, g, g_spec, w, tm, tj, tc, m, n_contract, outs=None, epilogue=None, extras=(), riders=()):
    if w.ndim == 3:
        per = w.shape[2] // tc
        kdim = w.shape[1]
        w_spec = pl.BlockSpec((None, tj, tc), lambda i, j, n: (n // per, j, n % per))
    else:
        kdim = w.shape[0]
        w_spec = pl.BlockSpec((tj, tc), lambda i, j, n: (j, n))
    if outs is None:
        outs = [_out2d(m, kdim, F32, tm, tj)]
    return _mm(name, NT, g, g_spec, w, w_spec, (m // tm, kdim // tj, n_contract // tc),
               (tm, tj), outs, epilogue, extras, riders)


def _mm_bwd_w(name, a, g, g_spec, n, tk, tn, tm, sharded, riders=()):
    m, kdim = a.shape
    if sharded:
        per = (n // N_SHARD) // tn
        out = (jax.ShapeDtypeStruct((N_SHARD, kdim, n // N_SHARD), F32),
               pl.BlockSpec((None, tk, tn), lambda i, j, mm: (j // per, i, j % per)))
    else:
        out = (jax.ShapeDtypeStruct((kdim, n), F32), pl.BlockSpec((tk, tn), lambda i, j, mm: (i, j)))
    return _mm(name, TN, a, pl.BlockSpec((tm, tk), lambda i, j, mm: (mm, i)), g, g_spec,
               (kdim // tk, n // tn, m // tm), (tk, tn), [out], riders=riders)


def _ew(name, fn, tiles, fulls, outs, sums=(), ts=512, riders=()):
    tiles = [t if isinstance(t, tuple) else (t, pl.BlockSpec((ts, t.shape[1]), lambda i: (i, 0)))
             for t in tiles]
    s_rows = tiles[0][0].shape[-2]
    nt, nf, no = len(tiles), len(fulls), len(outs)

    def body(*refs):
        t_vals = [r[...] for r in refs[:nt]]
        f_vals = [r[...] for r in refs[nt:nt + nf]]
        o_refs, s_refs = refs[nt + nf:nt + nf + no], refs[nt + nf + no:]
        o_vals, s_vals = fn(t_vals, f_vals)
        for r, v in zip(o_refs, o_vals):
            r[...] = v.astype(r.dtype)
        for r, v in zip(s_refs, s_vals):
            part = jnp.sum(v, axis=0, keepdims=True)

            @pl.when(pl.program_id(0) == 0)
            def _():
                r[...] = part

            @pl.when(pl.program_id(0) > 0)
            def _():
                r[...] += part

    full_specs = [pl.BlockSpec(f.shape, lambda i, nd=f.ndim: (0,) * nd) for f in fulls]
    return _carry(
        riders, body, name=name, grid=(s_rows // ts,),
        in_specs=[s for _, s in tiles] + full_specs,
        out_specs=[pl.BlockSpec((ts, c), lambda i: (i, 0)) for c, _ in outs]
        + [pl.BlockSpec((1, c), lambda i: (0, 0)) for c in sums],
        out_shape=[jax.ShapeDtypeStruct((s_rows, c), dt) for c, dt in outs]
        + [jax.ShapeDtypeStruct((1, c), F32) for c in sums],
        semantics=("arbitrary",),
    )(*[t for t, _ in tiles], *fulls)


def _rms_fwd(name, x, g):
    def fn(t, f):
        xv = t[0]
        r = lax.rsqrt(jnp.mean(xv * xv, axis=-1, keepdims=True) + EPS)
        return [xv * r * f[0]], []
    return _ew(name, fn, [x], [g], [(x.shape[1], BF16)])[0]


def _rms_bwd(name, dh, x, g, dres, riders=()):
    def fn(t, f):
        dhv, xv, dr = t
        r = lax.rsqrt(jnp.mean(xv * xv, axis=-1, keepdims=True) + EPS)
        xhat = xv * r
        u = dhv * f[0]
        dx = r * (u - xhat * jnp.mean(u * xhat, axis=-1, keepdims=True)) + dr
        return [dx, dx], [dhv * xhat]
    c = x.shape[1]
    return _ew(name, fn, [dh, x, dres], [g], [(c, F32), (c, BF16)], sums=[c], riders=riders)


def _split3(x):
    x1 = x.astype(BF16)
    r1 = x - x1.astype(F32)
    x2 = r1.astype(BF16)
    x3 = (r1 - x2.astype(F32)).astype(BF16)
    return x1, x2, x3


def _rel_class(cp):
    far = (cp < MAX_REL) | (cp > BAND)
    return jnp.where(far, 2 * MAX_REL, BAND - cp)


def _skew_rows(x, sign):
    row = lax.broadcasted_iota(jnp.int32, x.shape, 0)
    for b in range(SUPER.bit_length() - 1):
        shift = (1 << b) if sign > 0 else SKEW_W - (1 << b)
        x = jnp.where((row >> b) & 1 == 1, pltpu.roll(x, shift, 1), x)
    return x


def _bias_expand(rel_bias):
    rel = jnp.pad(rel_bias, ((0, 0), (0, REL_PAD - N_REL))).reshape(N_HEADS, 1, REL_PAD)

    def body(rel_ref, o_ref):
        cls = lax.broadcasted_iota(jnp.int32, (REL_PAD, SKEW_W), 0)
        cp = lax.broadcasted_iota(jnp.int32, (REL_PAD, SKEW_W), 1)
        onehot = (cls == _rel_class(cp)).astype(BF16)
        rel8 = jnp.broadcast_to(rel_ref[...], (8, REL_PAD))
        trow = sum(jnp.dot(p, onehot, preferred_element_type=F32) for p in _split3(rel8))[0:1]
        full = _skew_rows(jnp.broadcast_to(trow, (SUPER, SKEW_W)), +1)[:, :BAND]
        qc = lax.broadcasted_iota(jnp.int32, (SUPER, BAND), 0) // CHUNK
        kc = lax.broadcasted_iota(jnp.int32, (SUPER, BAND), 1) // CHUNK
        on_band = (kc >= qc) & (kc <= qc + N_PREV_CHUNKS)
        o_ref[...] = jnp.where(on_band, full, NEG_INF)

    return pl.pallas_call(
        body, name="bias_expand", grid=(N_HEADS,),
        in_specs=[pl.BlockSpec((None, 1, REL_PAD), lambda h: (h, 0, 0))],
        out_specs=pl.BlockSpec((None, SUPER, BAND), lambda h: (h, 0, 0)),
        out_shape=jax.ShapeDtypeStruct((N_HEADS, SUPER, BAND), F32),
        compiler_params=_params("arbitrary"),
    )(rel)


def _bias_reduce(dbias):
    def body(d_ref, o_ref):
        x = jnp.concatenate([d_ref[...], jnp.zeros((SUPER, SKEW_W - BAND), F32)], axis=1)
        diag = jnp.sum(_skew_rows(x, -1), axis=0, keepdims=True)
        cp = lax.broadcasted_iota(jnp.int32, (SKEW_W, REL_PAD), 0)
        cls = lax.broadcasted_iota(jnp.int32, (SKEW_W, REL_PAD), 1)
        onehot = (cls == _rel_class(cp)).astype(BF16)
        diag8 = jnp.broadcast_to(diag, (8, SKEW_W))
        o_ref[...] = sum(jnp.dot(p, onehot, preferred_element_type=F32) for p in _split3(diag8))[0:1]

    out = pl.pallas_call(
        body, name="bias_reduce", grid=(N_HEADS,),
        in_specs=[pl.BlockSpec((None, SUPER, BAND), lambda h: (h, 0, 0))],
        out_specs=pl.BlockSpec((None, 1, REL_PAD), lambda h: (h, 0, 0)),
        out_shape=jax.ShapeDtypeStruct((N_HEADS, 1, REL_PAD), F32),
        compiler_params=_params("arbitrary"),
    )(dbias)
    return out.reshape(N_HEADS, REL_PAD)[:, :N_REL]


HEADS_PER_STEP = 4
HEAD_COLS = HEADS_PER_STEP * HEAD_DIM
N_HEAD_GROUPS = N_HEADS // HEADS_PER_STEP


def _unit(x):
    r = lax.rsqrt(jnp.mean(x * x, axis=-1, keepdims=True) + EPS)
    return x * r, r


def _probs(qs, kn, bias, dead):
    s = jnp.concatenate([lax.dot_general(qs, k, NT, preferred_element_type=F32) for k in kn], axis=1) + bias
    s = jnp.where(dead, NEG_INF, s)
    e = jnp.exp(s - jnp.max(s, axis=-1, keepdims=True))
    return e / jnp.sum(e, axis=-1, keepdims=True)


def _band_specs(nb, col0, clamp_hi):
    def spec(d):
        def index(hg, i):
            blk = jnp.maximum(i - d, 0)
            if clamp_hi:
                blk = jnp.minimum(blk, nb - 1)
            return (blk, col0 + hg)
        return pl.BlockSpec((SUPER, HEAD_COLS), index)
    return [spec(2), spec(1), spec(0)]


def _head_sums(y):
    cols = y.shape[1]
    same_head = (lax.broadcasted_iota(jnp.int32, (cols, cols), 0) // HEAD_DIM
                 == lax.broadcasted_iota(jnp.int32, (cols, cols), 1) // HEAD_DIM).astype(BF16)
    hi = y.astype(BF16)
    lo = (y - hi.astype(F32)).astype(BF16)
    return (jnp.dot(hi, same_head, preferred_element_type=F32)
            + jnp.dot(lo, same_head, preferred_element_type=F32))


def _head_unit(x):
    r = lax.rsqrt(_head_sums(x * x) * (1.0 / HEAD_DIM) + EPS)
    return x * r, r


def _head(hh):
    return slice(HEAD_DIM * hh, HEAD_DIM * (hh + 1))


def _key_block(j):
    return slice(SUPER * j, SUPER * (j + 1))


def _dead_keys(i):
    col = lax.broadcasted_iota(jnp.int32, (SUPER, BAND), 1)
    return col < (2 - i) * SUPER


def _attn_fwd(qkn, v, bias, riders=()):
    s_len = qkn.shape[0]
    nb = s_len // SUPER

    def body(q_ref, k0, k1, k2, v0, v1, v2, b_ref, o_ref):
        dead = _dead_keys(pl.program_id(1))
        outs = []
        for hh in range(HEADS_PER_STEP):
            sl = _head(hh)
            p = _probs(q_ref[:, sl], [k0[:, sl], k1[:, sl], k2[:, sl]], b_ref[hh], dead).astype(BF16)
            outs.append(sum(jnp.dot(p[:, _key_block(j)], vj[:, sl], preferred_element_type=F32)
                            for j, vj in enumerate((v0, v1, v2))))
        o_ref[...] = jnp.concatenate(outs, axis=1).astype(o_ref.dtype)

    return _carry(
        riders, body, name="attn_fwd", grid=(N_HEAD_GROUPS, nb),
        in_specs=[pl.BlockSpec((SUPER, HEAD_COLS), lambda hg, i: (i, hg))]
        + _band_specs(nb, N_HEAD_GROUPS, False) + _band_specs(nb, 0, False)
        + [pl.BlockSpec((HEADS_PER_STEP, SUPER, BAND), lambda hg, i: (hg, 0, 0))],
        out_specs=pl.BlockSpec((SUPER, HEAD_COLS), lambda hg, i: (i, hg)),
        out_shape=jax.ShapeDtypeStruct((s_len, D_MODEL), BF16),
        semantics=("parallel", "arbitrary"),
    )(qkn, qkn, qkn, qkn, v, v, v, bias)


def _attn_bwd(qkn, v, d_out, bias, riders=()):
    s_len = qkn.shape[0]
    nb = s_len // SUPER

    def body(q_ref, k0, k1, k2, v0, v1, v2, do_ref, b_ref, dp_ref, db_ref, aq_ref, ak_ref, av_ref):
        i = pl.program_id(1)

        @pl.when(i == 0)
        def _():
            aq_ref[...] = jnp.zeros_like(aq_ref)
            ak_ref[...] = jnp.zeros_like(ak_ref)
            av_ref[...] = jnp.zeros_like(av_ref)
            db_ref[...] = jnp.zeros_like(db_ref)

        @pl.when(i < nb)
        def _():
            dead = _dead_keys(i)
            dq, dk, dv = [], [[], [], []], [[], [], []]
            for hh in range(HEADS_PER_STEP):
                sl = _head(hh)
                qs, do = q_ref[:, sl], do_ref[:, sl]
                kn = [k0[:, sl], k1[:, sl], k2[:, sl]]
                vv = [v0[:, sl], v1[:, sl], v2[:, sl]]
                p = _probs(qs, kn, b_ref[hh], dead)
                pb = p.astype(BF16)
                dp = jnp.concatenate([lax.dot_general(do, vj, NT, preferred_element_type=F32) for vj in vv], axis=1)
                ds = p * (dp - jnp.sum(p * dp, axis=-1, keepdims=True))
                db_ref[hh] += ds
                dsb = ds.astype(BF16)
                dq.append(sum(jnp.dot(dsb[:, _key_block(j)], kn[j], preferred_element_type=F32) for j in range(3)))
                for j in range(3):
                    dv[j].append(lax.dot_general(pb[:, _key_block(j)], do, TN, preferred_element_type=F32))
                    dk[j].append(lax.dot_general(dsb[:, _key_block(j)], qs, TN, preferred_element_type=F32))
            aq_ref[i % 3] = jnp.concatenate(dq, axis=1)
            for j in range(3):
                slot = (i + 1 + j) % 3
                if j < 2:
                    ak_ref[slot] += jnp.concatenate(dk[j], axis=1)
                    av_ref[slot] += jnp.concatenate(dv[j], axis=1)
                else:
                    ak_ref[slot] = jnp.concatenate(dk[j], axis=1)
                    av_ref[slot] = jnp.concatenate(dv[j], axis=1)

        slot = (i + 1) % 3
        dp_ref[0] = aq_ref[slot].astype(dp_ref.dtype)
        dp_ref[1] = ak_ref[slot].astype(dp_ref.dtype)
        dp_ref[2] = av_ref[slot].astype(dp_ref.dtype)

    def qrow(hg, i):
        return (jnp.minimum(i, nb - 1), hg)

    return _carry(
        riders, body, name="attn_bwd", grid=(N_HEAD_GROUPS, nb + 2),
        in_specs=[pl.BlockSpec((SUPER, HEAD_COLS), qrow)]
        + _band_specs(nb, N_HEAD_GROUPS, True) + _band_specs(nb, 0, True)
        + [pl.BlockSpec((SUPER, HEAD_COLS), qrow),
           pl.BlockSpec((HEADS_PER_STEP, SUPER, BAND), lambda hg, i: (hg, 0, 0))],
        out_specs=[pl.BlockSpec((3, SUPER, HEAD_COLS), lambda hg, i: (0, jnp.maximum(i - 2, 0), hg)),
                   pl.BlockSpec((HEADS_PER_STEP, SUPER, BAND), lambda hg, i: (hg, 0, 0))],
        out_shape=[jax.ShapeDtypeStruct((6, s_len, D_MODEL), BF16),
                   jax.ShapeDtypeStruct((N_HEADS, SUPER, BAND), F32)],
        scratch_shapes=[pltpu.VMEM((3, SUPER, HEAD_COLS), F32)] * 3,
        semantics=("parallel", "arbitrary"),
    )(qkn, qkn, qkn, qkn, v, v, v, d_out, bias)


def _qk_norm_bwd(dproj6, qk_raw, gq, gk):
    s_len = qk_raw.shape[0]
    ts = 256

    nsteps = s_len // ts
    half = D_MODEL // 2

    def body(d_ref, raw_ref, gq_ref, gk_ref, o_ref, dgq_ref, dgk_ref, acc_ref):
        step = pl.program_id(0)

        @pl.when(step == 0)
        def _():
            acc_ref[...] = jnp.zeros_like(acc_ref)

        for piece, (g_ref, scale) in enumerate(((gq_ref, QK_SCALE), (gk_ref, 1.0))):
            for c0 in (0, half):
                xhat, r = _head_unit(raw_ref[:, piece * D_MODEL + c0:piece * D_MODEL + c0 + half].astype(F32))
                dn = d_ref[piece, :, c0:c0 + half].astype(F32) * scale
                u = dn * g_ref[...]
                dx = r * (u - xhat * (_head_sums(u * xhat) * (1.0 / HEAD_DIM)))
                o_ref[piece, :, c0:c0 + half] = dx.astype(o_ref.dtype)
                acc_ref[piece:piece + 1, c0:c0 + half] += jnp.sum(dn * xhat, axis=0, keepdims=True)

        @pl.when(step == nsteps - 1)
        def _():
            lane = lax.broadcasted_iota(jnp.int32, (D_MODEL, LANE), 0) % HEAD_DIM
            fold = (lane == lax.broadcasted_iota(jnp.int32, (D_MODEL, LANE), 1)).astype(BF16)
            tot = sum(jnp.dot(p, fold, preferred_element_type=F32) for p in _split3(acc_ref[...]))
            dgq_ref[...] = tot[0:1, :HEAD_DIM]
            dgk_ref[...] = tot[1:2, :HEAD_DIM]

    gain = pl.BlockSpec((1, half), lambda i: (0, 0))
    small = pl.BlockSpec((1, HEAD_DIM), lambda i: (0, 0))
    per_head = lambda g: jnp.tile(g, (1, half // HEAD_DIM))
    return pl.pallas_call(
        body, name="qk_norm_bwd", grid=(nsteps,),
        in_specs=[pl.BlockSpec((2, ts, D_MODEL), lambda i: (0, i, 0)),
                  pl.BlockSpec((ts, 2 * D_MODEL), lambda i: (i, 0)), gain, gain],
        out_specs=[pl.BlockSpec((2, ts, D_MODEL), lambda i: (0, i, 0)), small, small],
        out_shape=[jax.ShapeDtypeStruct(dproj6.shape, dproj6.dtype),
                   jax.ShapeDtypeStruct((1, HEAD_DIM), F32), jax.ShapeDtypeStruct((1, HEAD_DIM), F32)],
        scratch_shapes=[pltpu.VMEM((8, D_MODEL), F32)],
        input_output_aliases={0: 0},
        compiler_params=_params("arbitrary"),
    )(dproj6, qk_raw, per_head(gq), per_head(gk))


CONV_ROWS = 512
HALO = 16


def _rows_with_halo(ref, r0, n, front, s_len):
    zeros = jnp.zeros((HALO, ref.shape[1]), F32)
    if front:
        return (jnp.concatenate([zeros, ref[0:n, :].astype(F32)], axis=0) if r0 == 0
                else ref[r0 - HALO:r0 + n, :].astype(F32))
    return (jnp.concatenate([ref[r0:r0 + n, :].astype(F32), zeros], axis=0) if r0 + n == s_len
            else ref[r0:r0 + n + HALO, :].astype(F32))


def _earlier(ext, k):
    return pltpu.roll(ext, k, 0)[HALO:]


def _later(ext, k):
    n = ext.shape[0]
    return pltpu.roll(ext, n - k, 0)[:n - HALO]


def _conv_cols(col0):
    return lambda s_len: pl.BlockSpec((s_len, LANE), lambda j: (0, col0 + j))


def _conv_fwd(proj, conv_w, conv_b, riders=()):
    s_len = proj.shape[0]

    def body(bg_ref, cg_ref, xc_ref, w_ref, b_ref, o_ref):
        w = [w_ref[t:t + 1, :] for t in range(3)]
        for r0 in range(0, s_len, CONV_ROWS):
            u = _rows_with_halo(cg_ref, r0, CONV_ROWS, True, s_len) * \
                _rows_with_halo(xc_ref, r0, CONV_ROWS, True, s_len)
            conv = b_ref[...] + w[0] * _earlier(u, 2) + w[1] * _earlier(u, 1) + w[2] * u[HALO:]
            o_ref[r0:r0 + CONV_ROWS, :] = (bg_ref[r0:r0 + CONV_ROWS, :].astype(F32) * conv).astype(o_ref.dtype)

    return _carry(
        riders, body, name="conv_fwd", grid=(D_MODEL // LANE,),
        in_specs=[_conv_cols(0)(s_len), _conv_cols(8)(s_len), _conv_cols(16)(s_len),
                  pl.BlockSpec((3, LANE), lambda j: (0, j)), pl.BlockSpec((1, LANE), lambda j: (0, j))],
        out_specs=pl.BlockSpec((s_len, LANE), lambda j: (0, j)),
        out_shape=jax.ShapeDtypeStruct((s_len, D_MODEL), BF16),
        semantics=("parallel",),
    )(proj, proj, proj, conv_w, conv_b)


def _conv_bwd(dproj6, dy, proj, conv_w, conv_b, riders=()):
    s_len = proj.shape[0]

    def body(dy_ref, bg_ref, cg_ref, xc_ref, w_ref, b_ref, _, dp_ref, dw_ref):
        w = [w_ref[t:t + 1, :] for t in range(3)]
        acc = [jnp.zeros((1, LANE), F32) for _ in range(4)]
        for r0 in range(0, s_len, CONV_ROWS):
            rows = slice(r0, r0 + CONV_ROWS)
            u = _rows_with_halo(cg_ref, r0, CONV_ROWS, True, s_len) * \
                _rows_with_halo(xc_ref, r0, CONV_ROWS, True, s_len)
            u2, u1, u0 = _earlier(u, 2), _earlier(u, 1), u[HALO:]
            conv = b_ref[...] + w[0] * u2 + w[1] * u1 + w[2] * u0
            dp_ref[0, rows, :] = (dy_ref[rows, :].astype(F32) * conv).astype(dp_ref.dtype)
            dconv_ext = _rows_with_halo(dy_ref, r0, CONV_ROWS, False, s_len) * \
                _rows_with_halo(bg_ref, r0, CONV_ROWS, False, s_len)
            dconv = dconv_ext[:CONV_ROWS]
            for t, term in enumerate([dconv * u2, dconv * u1, dconv * u0, dconv]):
                acc[t] = acc[t] + jnp.sum(term, axis=0, keepdims=True)
            du = w[2] * dconv + w[1] * _later(dconv_ext, 1) + w[0] * _later(dconv_ext, 2)
            dp_ref[1, rows, :] = (du * xc_ref[rows, :].astype(F32)).astype(dp_ref.dtype)
            dp_ref[2, rows, :] = (du * cg_ref[rows, :].astype(F32)).astype(dp_ref.dtype)
        dw_ref[...] = jnp.zeros_like(dw_ref)
        for t in range(4):
            dw_ref[t:t + 1, :] = acc[t]

    return _carry(
        riders, body, name="conv_bwd", grid=(D_MODEL // LANE,),
        in_specs=[pl.BlockSpec((s_len, LANE), lambda j: (0, j)),
                  _conv_cols(0)(s_len), _conv_cols(8)(s_len), _conv_cols(16)(s_len),
                  pl.BlockSpec((3, LANE), lambda j: (0, j)), pl.BlockSpec((1, LANE), lambda j: (0, j)),
                  pl.BlockSpec(memory_space=pl.ANY)],
        out_specs=[pl.BlockSpec((3, s_len, LANE), lambda j: (1, 0, j)),
                   pl.BlockSpec((8, LANE), lambda j: (0, j))],
        out_shape=[jax.ShapeDtypeStruct(dproj6.shape, dproj6.dtype),
                   jax.ShapeDtypeStruct((8, D_MODEL), F32)],
        input_output_aliases={6: 0},
        semantics=("parallel",),
    )(dy, proj, proj, proj, conv_w, conv_b, dproj6)


class _NoComm:
    def __init__(self, late):
        self.late = late

    def riders(self, at):
        return ()

    def done(self, at, carried):
        pass

    def late_weights(self):
        return self.late

    def grads_ready(self, group, grads):
        pass


def _local_grads(x, target, norm1_g, w_in3, q_norm_g, k_norm_g, rel_bias, conv_w, conv_b,
                 w_gate3, b_gate, norm2_g, comm):
    s_len = x.shape[0]
    tm = min(1024, s_len)
    row = lambda v: v.reshape(1, -1)

    def carrying(at, fn, *args, **kw):
        riders = comm.riders(at)
        out = fn(*args, riders=riders, **kw)
        if riders:
            out, carried = out
            comm.done(at, carried)
        return out

    h = _rms_fwd("norm1", x, row(norm1_g))
    gq, gk = row(q_norm_g), row(k_norm_g)

    def qk_epi(acc, e, o):
        o[0][...] = acc.astype(BF16)
        gain = jnp.where(pl.program_id(1) < 2, e[0] * QK_SCALE, e[1])
        o[1][...] = (_head_unit(acc)[0] * gain).astype(BF16)
    small = pl.BlockSpec((1, 512), lambda i, j, k: (0, 0))
    per_head = lambda g: jnp.tile(g, (1, 512 // HEAD_DIM))
    qk_raw, qkn = _mm_fwd("proj_qk", h, w_in3, tm, 512, D_MODEL, ncols=2 * D_MODEL, epilogue=qk_epi,
                          outs=[_out2d(s_len, 2 * D_MODEL, BF16, tm, 512)] * 2,
                          extras=[(per_head(gq), small), (per_head(gk), small)])
    v = _mm_fwd("proj_v", h, w_in3, tm, 512, D_MODEL, col0=4, ncols=D_MODEL,
                outs=[_out2d(s_len, D_MODEL, BF16, tm, 512)])
    conv_in = _mm_fwd("proj_conv", h, w_in3, tm, 1536, D_MODEL, col0=2, ncols=3 * D_MODEL,
                      outs=[_out2d(s_len, 3 * D_MODEL, BF16, tm, 1536)])

    def gate_epi(acc, e, o):
        o[0][...] = jax.nn.sigmoid(acc + e[0]).astype(BF16)
    gates = _mm_fwd("gates", h, w_gate3, tm, 512, D_MODEL, epilogue=gate_epi,
                    outs=[_out2d(s_len, 2 * D_MODEL, BF16, tm, 512)],
                    extras=[(row(b_gate), pl.BlockSpec((1, 512), lambda i, j, k: (0, j)))])

    bias = _bias_expand(rel_bias)
    attn = carrying("attn_fwd", _attn_fwd, qkn, v, bias)
    yconv = carrying("conv_fwd", _conv_fwd, conv_in, conv_w, row(conv_b))
    w_ap, w_cp, w_out, w_up3, w_down = comm.late_weights()
    tw = 1024
    ya = _mm_fwd("attn_proj", attn, w_ap, tm, tw, D_MODEL, outs=[_out2d(s_len, D_MODEL, BF16, tm, tw)])

    def merge_epi(acc, e, o):
        ya_v, ga, gc = [t.astype(F32) for t in e]
        o[0][...] = acc.astype(BF16)
        o[1][...] = (ga * ya_v + gc * acc).astype(BF16)
    gate_a, gate_c = (gates, _tile_spec(tm, tw, 0)), (gates, _tile_spec(tm, tw, 1))
    yc, merged = _mm_fwd("conv_proj", yconv, w_cp, tm, tw, D_MODEL, epilogue=merge_epi,
                         outs=[_out2d(s_len, D_MODEL, BF16, tm, tw), _out2d(s_len, D_MODEL, BF16, tm, tw)],
                         extras=[(ya, _tile_spec(tm, tw)), gate_a, gate_c])

    def res_epi(acc, e, o):
        o[0][...] = e[0] + acc
    x1 = _mm_fwd("out_proj", merged, w_out, tm, tw, D_MODEL, epilogue=res_epi,
                 extras=[(x, _tile_spec(tm, tw))])
    h2 = _rms_fwd("norm2", x1, row(norm2_g))

    def up_epi(acc, e, o):
        o[0][...] = acc.astype(BF16)
        o[1][...] = jnp.square(jnp.maximum(acc, 0.0)).astype(BF16)
    up, act = _mm_fwd("mlp_up", h2, w_up3, tm, tw, D_MODEL, epilogue=up_epi,
                      outs=[_out2d(s_len, D_FF, BF16, tm, tw), _out2d(s_len, D_FF, BF16, tm, tw)])

    def loss_epi(acc, e, o):
        err = e[0] + acc - e[1]
        dy = err * (1.0 / D_MODEL)
        o[0][...] = dy
        o[1][...] = dy.astype(BF16)
        sq = err * err
        part = sq[:, 0:LANE]
        for c0 in range(LANE, D_MODEL, LANE):
            part = part + sq[:, c0:c0 + LANE]
        o[2][...] = jnp.sum(part.reshape(tl // 8, 8, LANE), axis=0)
    tl = 512
    dy, dy_b, loss_part = _mm_fwd(
        "mlp_down", act, w_down, tl, D_MODEL, D_FF, epilogue=loss_epi,
        outs=[_out2d(s_len, D_MODEL, F32, tl, D_MODEL), _out2d(s_len, D_MODEL, BF16, tl, D_MODEL),
              (jax.ShapeDtypeStruct((8 * (s_len // tl), LANE), F32), pl.BlockSpec((8, LANE), lambda i, j, k: (i, 0)))],
        extras=[(x1, _tile_spec(tl, D_MODEL)), (target, _tile_spec(tl, D_MODEL))])

    def dup_epi(acc, e, o):
        o[0][...] = (acc * (2.0 * jnp.maximum(e[0].astype(F32), 0.0))).astype(BF16)
    full = lambda cols: pl.BlockSpec((tm, cols), lambda i, j, n: (i, n))
    tokens = lambda cols: pl.BlockSpec((s_len, cols), lambda i, j, m: (m, j))
    dup = _mm_bwd_x("d_act", dy_b, full(D_MODEL), w_down, tm, tw, D_MODEL, s_len, D_MODEL, epilogue=dup_epi,
                    outs=[_out2d(s_len, D_FF, BF16, tm, tw)], extras=[(up, _tile_spec(tm, tw))])
    g_down = _mm_bwd_w("g_down", act, dy_b, tokens(D_MODEL), D_MODEL, 512, D_MODEL, s_len, False)
    g_up = _mm_bwd_w("g_up", h2, dup, tokens(512), D_FF, D_MODEL, 512, s_len, True)
    comm.grads_ready("mlp", dict(w_down=g_down, w_up=g_up))
    dh2 = carrying("d_h2", _mm_bwd_x, "d_h2", dup, full(1024), w_up3, tm, tw, 1024, s_len, D_FF)
    dx1, dx1_b, dg2 = _rms_bwd("norm2_bwd", dh2, x1, row(norm2_g), dy)

    def dmerge_epi(acc, e, o):
        ya_v, yc_v, ga, gc = [t.astype(F32) for t in e]
        o[0][...] = (acc * ga).astype(BF16)
        o[1][...] = (acc * gc).astype(BF16)
        o[2][0] = (acc * ya_v * ga * (1.0 - ga)).astype(BF16)
        o[2][1] = (acc * yc_v * gc * (1.0 - gc)).astype(BF16)
    dya, dyc, dgp2 = _mm_bwd_x(
        "d_merged", dx1_b, full(D_MODEL), w_out, tm, tw, D_MODEL, s_len, D_MODEL, epilogue=dmerge_epi,
        outs=[_out2d(s_len, D_MODEL, BF16, tm, tw), _out2d(s_len, D_MODEL, BF16, tm, tw),
              (jax.ShapeDtypeStruct((2, s_len, D_MODEL), BF16), pl.BlockSpec((2, tm, tw), lambda i, j, n: (0, i, j)))],
        extras=[(ya, _tile_spec(tm, tw)), (yc, _tile_spec(tm, tw)), gate_a, gate_c])
    g_out = _mm_bwd_w("g_out", merged, dx1_b, tokens(512), D_MODEL, D_MODEL, 512, s_len, False)
    d_attn = _mm_bwd_x("d_attn", dya, full(D_MODEL), w_ap, tm, tw, D_MODEL, s_len, D_MODEL,
                       outs=[_out2d(s_len, D_MODEL, BF16, tm, tw)])
    g_ap = _mm_bwd_w("g_attn_proj", attn, dya, tokens(512), D_MODEL, D_MODEL, 512, s_len, False)
    d_yconv = _mm_bwd_x("d_yconv", dyc, full(D_MODEL), w_cp, tm, tw, D_MODEL, s_len, D_MODEL,
                        outs=[_out2d(s_len, D_MODEL, BF16, tm, tw)])
    g_cp = _mm_bwd_w("g_conv_proj", yconv, dyc, tokens(512), D_MODEL, D_MODEL, 512, s_len, False)
    comm.grads_ready("proj", dict(w_out=g_out, w_attn_proj=g_ap, w_conv_proj=g_cp))

    dproj6, dbias = carrying("attn_bwd", _attn_bwd, qkn, v, d_attn, bias)
    dproj6, dgq, dgk = _qk_norm_bwd(dproj6, qk_raw, gq, gk)
    dproj6, dconv_wb = carrying("conv_bwd", _conv_bwd, dproj6, d_yconv, conv_in, conv_w, row(conv_b))
    d_rel = _bias_reduce(dbias)

    piece = lambda width: (lambda blk: (blk * width) // D_MODEL, lambda blk: (blk * width % D_MODEL) // width)
    pc, cb = piece(512)
    pieces = pl.BlockSpec((None, s_len, 512), lambda i, j, m: (pc(j), m, cb(j)))
    g_in = carrying("g_in", _mm_bwd_w, "g_in", h, dproj6, pieces, 6 * D_MODEL, D_MODEL, 512, s_len, True)
    g_gate = carrying("g_gate", _mm_bwd_w, "g_gate", h, dgp2, pieces, 2 * D_MODEL, D_MODEL, 512, s_len, True)
    comm.grads_ready("in", dict(w_in=g_in, w_gate=g_gate))
    by_piece = pl.BlockSpec((None, tm, 512), lambda i, j, n: (pc(n), i, cb(n)))
    dh_g = carrying("d_h_gate", _mm_bwd_x, "d_h_gate", dgp2, by_piece, w_gate3, tm, tw, 512, s_len, 2 * D_MODEL)

    def add_epi(acc, e, o):
        o[0][...] = acc + e[0]
    dh = carrying("d_h_proj", _mm_bwd_x, "d_h_proj", dproj6, by_piece, w_in3, tm, tw, 512, s_len, 6 * D_MODEL,
                  epilogue=add_epi, extras=[(dh_g, _tile_spec(tm, tw))])
    grad_x, _, dg1 = carrying("norm1_bwd", _rms_bwd, "norm1_bwd", dh, x, row(norm1_g), dx1)

    def bsum(t, f):
        return [], [t[0].astype(F32), t[1].astype(F32)]
    ts = 512
    db_a, db_c = _ew("b_gate_sum", bsum,
                     [(dgp2, pl.BlockSpec((None, ts, D_MODEL), lambda i: (0, i, 0))),
                      (dgp2, pl.BlockSpec((None, ts, D_MODEL), lambda i: (1, i, 0)))],
                     [], [], sums=[D_MODEL, D_MODEL], ts=ts)

    big = dict(w_in=g_in, w_attn_proj=g_ap, w_conv_proj=g_cp, w_gate=g_gate, w_out=g_out,
               w_up=g_up, w_down=g_down)
    small = dict(norm1_g=dg1, norm2_g=dg2, conv_wb=dconv_wb, b_gate=(db_a, db_c),
                 q_norm_g=dgq, k_norm_g=dgk, rel_bias=d_rel)
    return loss_part, grad_x, big, small


def _finish_loss(loss_part):
    def body(l_ref, lo_ref):
        total = jnp.sum(jnp.sum(l_ref[...], axis=0, keepdims=True), axis=1, keepdims=True)
        lo_ref[...] = jnp.broadcast_to(total * (0.5 / D_MODEL), lo_ref.shape)

    return pl.pallas_call(body, name="finish_loss", out_shape=jax.ShapeDtypeStruct((8, LANE), F32))(loss_part)


HBM_SPEC = pl.BlockSpec(memory_space=pl.ANY)


def _place():
    return lax.axis_index("x"), lax.axis_index("y"), lax.axis_index("c")


def _other_chips(x, y):
    return [(1 - x, y), (x, 1 - y), (1 - x, 1 - y)]


def _cast_into_slot(name, where, w):
    r, cols = w.shape
    ts = 256

    def body(w_ref, x_ref, o_ref):
        o_ref[...] = x_ref[...].astype(o_ref.dtype)

    return pl.pallas_call(
        body, name=name,
        grid_spec=pltpu.PrefetchScalarGridSpec(
            num_scalar_prefetch=1, grid=(r // ts,),
            in_specs=[pl.BlockSpec((ts, cols), lambda i, w: (i, 0))],
            out_specs=pl.BlockSpec((None, ts, cols), lambda i, w: (w[0], i, 0))),
        out_shape=jax.ShapeDtypeStruct((N_SHARD, r, cols), BF16),
        compiler_params=_params("parallel"),
    )(where, w)


def _all_gather(slots, small):
    n = len(slots)

    def body(*refs):
        outs, small_out = refs[n + 1:2 * n + 1], refs[2 * n + 1]
        send1, recv1, send2, recv2, send3, recv3 = refs[2 * n + 2:]
        x, y, c = _place()
        me = 2 * x + y
        chips = _other_chips(x, y)
        sibling = (x, y, 1 - c)

        def half(w, shard, which):
            hr = slots[w].shape[1] // 2
            return outs[w].at[shard, pl.ds(which * hr, hr)]

        def remote(ref, send, recv, k, to):
            return pltpu.make_async_remote_copy(src_ref=ref, dst_ref=ref, send_sem=send.at[k],
                                                recv_sem=recv.at[k], device_id=to, device_id_type=MESH)

        sent = [remote(half(w, me, c), send1, recv1, 3 * w + j, (*chip, c))
                for w in range(n) for j, chip in enumerate(chips)]
        sent += [remote(small_out.at[me], send3, recv3, j, (*chip, c)) for j, chip in enumerate(chips)]
        for cp in sent:
            cp.start()
        for w in range(n):
            for j, chip in enumerate(chips):
                landed = half(w, 2 * chip[0] + chip[1], c)
                remote(landed, send1, recv1, 3 * w + j, (*chip, c)).wait_recv()
                fwd = remote(landed, send2, recv2, 3 * w + j, sibling)
                fwd.start()
                sent.append(fwd)
        for w in range(n):
            for j, chip in enumerate(chips):
                remote(half(w, 2 * chip[0] + chip[1], 1 - c), send2, recv2, 3 * w + j, sibling).wait_recv()
        for j, chip in enumerate(chips):
            remote(small_out.at[2 * chip[0] + chip[1]], send3, recv3, j, (*chip, c)).wait_recv()
        for cp in sent:
            cp.wait_send()

    return pl.pallas_call(
        body, name="gather_weights",
        in_specs=[HBM_SPEC] * (n + 1), out_specs=[HBM_SPEC] * (n + 1),
        out_shape=[jax.ShapeDtypeStruct(s.shape, s.dtype) for s in slots]
        + [jax.ShapeDtypeStruct(small.shape, small.dtype)],
        input_output_aliases={k: k for k in range(n + 1)},
        scratch_shapes=[pltpu.SemaphoreType.DMA((3 * n,))] * 4 + [pltpu.SemaphoreType.DMA((3,))] * 2,
    )(*slots, small)


def _remote(src, dst, send, recv, k, to):
    return pltpu.make_async_remote_copy(src_ref=src, dst_ref=dst, send_sem=send.at[k], recv_sem=recv.at[k],
                                        device_id=to, device_id_type=MESH)


def _gather_riders(slots):
    n = len(slots)

    def half(refs, w, shard, which):
        hr = slots[w].shape[1] // 2
        return refs[w].at[shard, pl.ds(which * hr, hr)]

    def each(fn):
        x, y, c = _place()
        for w in range(n):
            for j, chip in enumerate(_other_chips(x, y)):
                fn(w, 3 * w + j, 2 * x + y, 2 * chip[0] + chip[1], c, (*chip, c), (x, y, 1 - c))

    def chips_start(_, refs, send, recv):
        each(lambda w, k, me, them, c, peer, sib: _remote(half(refs, w, me, c), half(refs, w, me, c), send, recv, k, peer).start())

    def chips_finish(_, refs, send, recv):
        each(lambda w, k, me, them, c, peer, sib: _remote(half(refs, w, them, c), half(refs, w, them, c), send, recv, k, peer).wait_recv())
        each(lambda w, k, me, them, c, peer, sib: _remote(half(refs, w, me, c), half(refs, w, me, c), send, recv, k, peer).wait_send())

    def sibling_start(_, refs, send, recv):
        each(lambda w, k, me, them, c, peer, sib: _remote(half(refs, w, them, c), half(refs, w, them, c), send, recv, k, sib).start())

    def sibling_finish(_, refs, send, recv):
        each(lambda w, k, me, them, c, peer, sib: _remote(half(refs, w, them, 1 - c), half(refs, w, them, 1 - c), send, recv, k, sib).wait_recv())
        each(lambda w, k, me, them, c, peer, sib: _remote(half(refs, w, them, c), half(refs, w, them, c), send, recv, k, sib).wait_send())

    return (lambda s: _Rider([], s, 3 * n, chips_start, chips_finish),
            lambda s: _Rider([], s, 3 * n, sibling_start, sibling_finish))


def _pair_exchange_rider(grads, landing):
    n = len(grads)

    def copies(srcs, dsts, send, recv):
        x, y, c = _place()
        out = []
        for w in range(n):
            hr = grads[w].shape[1] // 2
            out.append(_remote(srcs[w].at[:, pl.ds((1 - c) * hr, hr)], dsts[w], send, recv, w, (x, y, 1 - c)))
        return out

    def start(srcs, dsts, send, recv):
        for cp in copies(srcs, dsts, send, recv):
            cp.start()

    def finish(srcs, dsts, send, recv):
        for cp in copies(srcs, dsts, send, recv):
            cp.wait()

    return _Rider(grads, landing, n, start, finish)


def _row_tile(hr):
    return min(hr, 256)


def _pair_add(name, where, grad, got):
    _, hr, cols = got.shape
    tr = _row_tile(hr)
    nblk = hr // tr

    def body(w_ref, g_ref, r_ref, o_ref):
        o_ref[...] = (g_ref[...] + r_ref[...]).astype(o_ref.dtype)

    return pl.pallas_call(
        body, name=name,
        grid_spec=pltpu.PrefetchScalarGridSpec(
            num_scalar_prefetch=1, grid=(N_SHARD, nblk),
            in_specs=[pl.BlockSpec((None, tr, cols), lambda s, i, w: (s, w[1] * nblk + i, 0)),
                      pl.BlockSpec((None, tr, cols), lambda s, i, w: (s, i, 0))],
            out_specs=pl.BlockSpec((None, tr, cols), lambda s, i, w: (s, i, 0))),
        out_shape=jax.ShapeDtypeStruct(got.shape, BF16),
        compiler_params=_params("parallel", "parallel"),
    )(where, grad, got)


def _chip_exchange_rider(partials, landing):
    n = len(partials)

    def copies(srcs, dsts, send, recv):
        x, y, c = _place()
        return [_remote(srcs[w].at[2 * chip[0] + chip[1]], dsts[w].at[j], send, recv, 3 * w + j, (*chip, c))
                for w in range(n) for j, chip in enumerate(_other_chips(x, y))]

    def start(srcs, dsts, send, recv):
        for cp in copies(srcs, dsts, send, recv):
            cp.start()

    def finish(srcs, dsts, send, recv):
        for cp in copies(srcs, dsts, send, recv):
            cp.wait()

    return _Rider(partials, landing, 3 * n, start, finish)


def _final_add(name, where, grad, got, arrived):
    _, hr, cols = got.shape
    tr = _row_tile(hr)
    nblk = hr // tr

    def body(w_ref, g_ref, r_ref, a_ref, o_ref):
        acc = g_ref[...] + r_ref[...]
        for j in range(3):
            acc = acc + a_ref[j].astype(F32)
        o_ref[...] = acc

    return pl.pallas_call(
        body, name=name,
        grid_spec=pltpu.PrefetchScalarGridSpec(
            num_scalar_prefetch=1, grid=(nblk,),
            in_specs=[pl.BlockSpec((None, tr, cols), lambda i, w: (w[0], w[1] * nblk + i, 0)),
                      pl.BlockSpec((None, tr, cols), lambda i, w: (w[0], i, 0)),
                      pl.BlockSpec((3, tr, cols), lambda i, w: (0, i, 0))],
            out_specs=pl.BlockSpec((tr, cols), lambda i, w: (w[1] * nblk + i, 0))),
        out_shape=jax.ShapeDtypeStruct((2 * hr, cols), F32),
        compiler_params=_params("parallel"),
    )(where, grad, got, arrived)


def _pair_share_rider(shards):
    n = len(shards)

    def half(refs, w, which):
        hr = shards[w].shape[0] // 2
        return refs[w].at[pl.ds(which * hr, hr)]

    def start(_, refs, send, recv):
        x, y, c = _place()
        for w in range(n):
            _remote(half(refs, w, c), half(refs, w, c), send, recv, w, (x, y, 1 - c)).start()

    def finish(_, refs, send, recv):
        x, y, c = _place()
        for w in range(n):
            _remote(half(refs, w, 1 - c), half(refs, w, 1 - c), send, recv, w, (x, y, 1 - c)).wait_recv()
        for w in range(n):
            _remote(half(refs, w, c), half(refs, w, c), send, recv, w, (x, y, 1 - c)).wait_send()

    return _Rider([], shards, n, start, finish)


class _Exchange:
    PLAN = {"attn_fwd": [("late", "gather")], "conv_fwd": [("late", "forward")],
            "d_h2": [("mlp", "pair")], "attn_bwd": [("mlp", "chips"), ("proj", "pair")], "conv_bwd": [("mlp", "share")],
            "g_in": [("proj", "chips")], "g_gate": [("proj", "share")],
            "d_h_gate": [("in", "pair")], "d_h_proj": [("in", "chips")], "norm1_bwd": [("in", "share")]}

    def __init__(self, where, late_slots):
        self.where = where
        self.late = late_slots
        self.gather, self.forward = _gather_riders(late_slots)
        self.groups, self.reduced, self.pending = {}, {}, []

    def late_weights(self):
        rows = lambda a: a.reshape(a.shape[0] * a.shape[1], a.shape[2])
        w_ap, w_cp, w_out, w_up3, w_down = self.late
        return rows(w_ap), rows(w_cp), rows(w_out), w_up3, rows(w_down)

    def grads_ready(self, group, grads):
        names = list(grads)
        g4 = [g if g.ndim == 3 else g.reshape(N_SHARD, -1, g.shape[1]) for g in grads.values()]
        self.groups[group] = dict(names=names, g4=g4)

    def riders(self, at):
        self.pending = self.PLAN.get(at, [])
        out = []
        for group, stage in self.pending:
            if group == "late":
                out.append(self.gather(self.late) if stage == "gather" else self.forward(self.late))
                continue
            st = self.groups[group]
            if stage == "pair":
                landing = [lax.empty((N_SHARD, g.shape[1] // 2, g.shape[2]), F32) for g in st["g4"]]
                out.append(_pair_exchange_rider(st["g4"], landing))
            elif stage == "chips":
                landing = [lax.empty((3,) + p.shape[1:], p.dtype) for p in st["partial"]]
                out.append(_chip_exchange_rider(st["partial"], landing))
            else:
                out.append(_pair_share_rider(st["halves"]))
        return out

    def done(self, at, carried):
        for (group, stage), arrays in zip(self.pending, carried):
            if group == "late":
                self.late = arrays
                continue
            st = self.groups[group]
            tag = lambda what, n: what + "_" + n
            if stage == "pair":
                st["got"] = arrays
                st["partial"] = [_pair_add(tag("pair_add", n), self.where, g, r)
                                 for n, g, r in zip(st["names"], st["g4"], arrays)]
            elif stage == "chips":
                st["halves"] = [_final_add(tag("final_add", n), self.where, g, r, a)
                                for n, g, r, a in zip(st["names"], st["g4"], st["got"], arrays)]
            else:
                self.reduced.update(zip(st["names"], arrays))


SMALL_ROWS = 32
N_DEV = 8


def _all_reduce_small(pack):
    def body(p_ref, o_ref, buf, send, recv):
        x, y, c = _place()
        buf[4 * x + 2 * y + c] = p_ref[...]
        copies, waits = [], []
        for k in range(1, N_DEV):
            px = 1 - x if k & 4 else x
            py = 1 - y if k & 2 else y
            pc = 1 - c if k & 1 else c
            copies.append(pltpu.make_async_remote_copy(
                src_ref=p_ref, dst_ref=buf.at[4 * x + 2 * y + c], send_sem=send.at[k - 1],
                recv_sem=recv.at[k - 1], device_id=(px, py, pc), device_id_type=MESH))
            waits.append(pltpu.make_async_remote_copy(
                src_ref=p_ref, dst_ref=buf.at[4 * px + 2 * py + pc], send_sem=send.at[k - 1],
                recv_sem=recv.at[k - 1], device_id=(px, py, pc), device_id_type=MESH))
        for cp in copies:
            cp.start()
        for cp in waits:
            cp.wait_recv()
        acc = buf[0]
        for d in range(1, N_DEV):
            acc = acc + buf[d]
        o_ref[...] = acc
        for cp in copies:
            cp.wait_send()

    return pl.pallas_call(
        body, name="all_reduce_small",
        out_shape=jax.ShapeDtypeStruct(pack.shape, F32),
        scratch_shapes=[pltpu.VMEM((N_DEV,) + pack.shape, F32),
                        pltpu.SemaphoreType.DMA((N_DEV - 1,)), pltpu.SemaphoreType.DMA((N_DEV - 1,))],
    )(pack)


def _adamw(name, w, g, m, v):
    c1 = 1.0 - ADAM_B1 ** ADAM_STEP
    c2 = 1.0 - ADAM_B2 ** ADAM_STEP

    def fn(t, f):
        wv, gv, mv, vv = t
        m2 = ADAM_B1 * mv + (1.0 - ADAM_B1) * gv
        v2 = ADAM_B2 * vv + (1.0 - ADAM_B2) * (gv * gv)
        delta = -ADAM_LR * ((m2 / c1) / (jnp.sqrt(v2 / c2) + ADAM_EPS) + ADAM_WD * wv)
        return [delta, m2, v2], []

    cols = w.shape[1]
    return _ew(name, fn, [w, g, m, v], [], [(cols, F32)] * 3, ts=min(w.shape[0], 256))


def _pack_small(norm1_g, norm2_g, conv_b, b_gate, conv_w, q_norm_g, k_norm_g, rel_bias):
    pack = jnp.zeros((SMALL_ROWS, D_MODEL), F32)
    for r0, v in ((0, norm1_g), (1, norm2_g), (2, conv_b), (3, b_gate.reshape(2, D_MODEL)), (5, conv_w),
                  (8, q_norm_g), (9, k_norm_g), (10, rel_bias)):
        v = v.reshape(-1, v.shape[-1]).astype(F32)
        pack = pack.at[r0:r0 + v.shape[0], :v.shape[1]].set(v)
    return pack


def _unpack_small(pack, conv_cols):
    return dict(norm1_g=pack[0], norm2_g=pack[1], conv_b=pack[2], b_gate=pack[3:5].reshape(2 * D_MODEL),
                conv_w=pack[5:8, :conv_cols], q_norm_g=pack[8, :HEAD_DIM], k_norm_g=pack[9, :HEAD_DIM],
                rel_bias=pack[10:10 + N_HEADS, :N_REL])


BIG = ["w_in", "w_attn_proj", "w_conv_proj", "w_gate", "w_out", "w_up", "w_down"]
LATE = ["w_attn_proj", "w_conv_proj", "w_out", "w_up", "w_down"]
WEIGHTS = ["norm1_g", "w_in", "q_norm_g", "k_norm_g", "rel_bias", "conv_w", "conv_b", "w_attn_proj",
           "w_conv_proj", "w_gate", "b_gate", "w_out", "norm2_g", "w_up", "w_down"]


def kernel(x, norm1_g, w_in, q_norm_g, k_norm_g, rel_bias, conv_w, conv_b, w_attn_proj, w_conv_proj, w_gate, b_gate, w_out, norm2_g, w_up, w_down, loss_target, m_norm1_g, m_w_in, m_q_norm_g, m_k_norm_g, m_rel_bias, m_conv_w, m_conv_b, m_w_attn_proj, m_w_conv_proj, m_w_gate, m_b_gate, m_w_out, m_norm2_g, m_w_up, m_w_down, v_norm1_g, v_w_in, v_q_norm_g, v_k_norm_g, v_rel_bias, v_conv_w, v_conv_b, v_w_attn_proj, v_w_conv_proj, v_w_gate, v_b_gate, v_w_out, v_norm2_g, v_w_up, v_w_down):
    given = dict(locals())
    w = {n: given[n] for n in WEIGHTS}
    m = {n: given["m_" + n] for n in WEIGHTS}
    v = {n: given["v_" + n] for n in WEIGHTS}
    s_len = x.shape[1]
    shard = 2 * lax.axis_index("x") + lax.axis_index("y")
    where = jnp.stack([shard, lax.axis_index("c")]).astype(jnp.int32)
    conv_cols = conv_w.shape[1]

    small_in = lax.dynamic_update_slice(jnp.zeros((N_SHARD, 8, conv_cols), F32), conv_w[None], (shard, 0, 0))
    slot = {n: _cast_into_slot("cast_" + n, where, w[n]) for n in BIG}
    w_in3, w_gate3, conv_w4 = _all_gather([slot["w_in"], slot["w_gate"]], small_in)
    conv_w_full = conv_w4[:, :3, :].transpose(1, 0, 2).reshape(3, N_SHARD * conv_cols)
    comm = _Exchange(where, [slot[n] for n in LATE])

    loss_part, grad_x, _, small = _local_grads(
        x.reshape(s_len, D_MODEL), loss_target.reshape(s_len, D_MODEL), norm1_g, w_in3, q_norm_g,
        k_norm_g, rel_bias, conv_w_full, conv_b, w_gate3, b_gate, norm2_g, comm)
    grad = dict(comm.reduced)

    loss_local = _finish_loss(loss_part)
    pack = _pack_small(small["norm1_g"], small["norm2_g"], small["conv_wb"][3], jnp.concatenate(small["b_gate"], axis=1),
                       small["conv_wb"][0:3], small["q_norm_g"], small["k_norm_g"], small["rel_bias"])
    total = _all_reduce_small(pack)
    g_small = _unpack_small(total, D_MODEL)
    g_small["conv_w"] = lax.dynamic_slice(g_small["conv_w"], (0, shard * conv_cols), (3, conv_cols))
    grad.update(g_small)

    delta, new_m, new_v = {}, {}, {}
    for n in BIG:
        delta[n], new_m[n], new_v[n] = _adamw("adamw_" + n, w[n], grad[n], m[n], v[n])
    small_names = [n for n in WEIGHTS if n not in BIG]
    packs = [_pack_small(**{n: src[n] for n in small_names}) for src in (w, grad, m, v)]
    for out, packed in zip((delta, new_m, new_v), _adamw("adamw_small", *packs)):
        out.update({n: a.reshape(w[n].shape) for n, a in _unpack_small(packed, conv_cols).items()})

    loss = lax.psum(loss_local[0, 0], ("x", "y", "c"))
    outs = [loss, grad_x.reshape(x.shape)]
    for group in (grad, delta, new_m, new_v):
        outs += [group[n].reshape(w[n].shape) for n in WEIGHTS]
    return tuple(outs)
```

```python
import functools

import jax
import jax.numpy as jnp
from jax import lax
from jax.experimental import pallas as pl
from jax.experimental.pallas import tpu as pltpu

F32 = jnp.float32
BF16 = jnp.bfloat16

D_MODEL = 1024
N_HEADS = 16
HEAD_DIM = 64
CHUNK = 64
N_PREV_CHUNKS = 8
MAX_REL = 256
D_FF = 4096
N_REL = 2 * MAX_REL + 1
REL_PAD = 640
EPS = 1e-6
NEG_INF = -1e30
QK_SCALE = HEAD_DIM ** -0.5

SUPER = 4 * CHUNK
BAND = SUPER + N_PREV_CHUNKS * CHUNK
SKEW_W = 1024
N_SHARD = 4
LANE = 128
VMEM_LIMIT = 48 * 1024 * 1024

ADAM_LR = 0.001
ADAM_B1 = 0.9
ADAM_B2 = 0.999
ADAM_EPS = 1e-08
ADAM_WD = 0.01
ADAM_STEP = 10

MESH = pl.DeviceIdType.MESH
NN = (((1,), (0,)), ((), ()))
NT = (((1,), (1,)), ((), ()))
TN = (((0,), (0,)), ((), ()))


def _params(*sem):
    return pltpu.CompilerParams(dimension_semantics=sem or None, vmem_limit_bytes=VMEM_LIMIT)


HBM_SPEC = pl.BlockSpec(memory_space=pl.ANY)


class _Rider:
    def __init__(self, sources, arrays, n_sem, start, finish):
        self.sources, self.arrays, self.n_sem, self.start, self.finish = sources, arrays, n_sem, start, finish


def _carry(riders, body, *, name, out_shape, grid=(), in_specs=None, out_specs=None, scratch_shapes=(),
           semantics=(), input_output_aliases=None):
    aliases = dict(input_output_aliases or {})
    if not riders:
        kw = {} if in_specs is None else dict(in_specs=in_specs, out_specs=out_specs)
        return pl.pallas_call(body, name=name, grid=grid, out_shape=out_shape, scratch_shapes=scratch_shapes,
                              input_output_aliases=aliases, compiler_params=_params(*semantics), **kw)
    single = not isinstance(out_shape, (list, tuple))
    shapes = [out_shape] if single else list(out_shape)
    n_out, n_scr = len(shapes), len(scratch_shapes)
    srcs = [a for r in riders for a in r.sources]
    arrs = [a for r in riders for a in r.arrays]
    vmem = pl.BlockSpec(memory_space=pltpu.VMEM)

    def run(*args):
        n_in = len(args)

        def wrapped(*refs):
            pos = n_in
            src_refs = refs[pos:pos + len(srcs)]
            pos += len(srcs) + len(arrs)
            outs = refs[pos:pos + n_out]
            pos += n_out
            arr_refs = refs[pos:pos + len(arrs)]
            pos += len(arrs)
            scratch = refs[pos:pos + n_scr]
            sems = refs[pos + n_scr:]
            first, last = True, True
            for d, size in enumerate(grid):
                first = jnp.logical_and(first, pl.program_id(d) == 0)
                last = jnp.logical_and(last, pl.program_id(d) == size - 1)

            def each(method):
                s0 = a0 = 0
                for k, r in enumerate(riders):
                    getattr(r, method)(src_refs[s0:s0 + len(r.sources)], arr_refs[a0:a0 + len(r.arrays)],
                                       sems[2 * k], sems[2 * k + 1])
                    s0, a0 = s0 + len(r.sources), a0 + len(r.arrays)

            pl.when(first)(lambda: each("start"))
            body(*refs[:n_in], *outs, *scratch)
            pl.when(last)(lambda: each("finish"))

        ins = [vmem] * n_in if in_specs is None else list(in_specs)
        if out_specs is None:
            o_specs = [vmem] * n_out
        else:
            o_specs = [out_specs] if single else list(out_specs)
        for k in range(len(arrs)):
            aliases[n_in + len(srcs) + k] = n_out + k
        res = pl.pallas_call(
            wrapped, name=name, grid=grid,
            in_specs=ins + [HBM_SPEC] * (len(srcs) + len(arrs)),
            out_specs=o_specs + [HBM_SPEC] * len(arrs),
            out_shape=shapes + [jax.ShapeDtypeStruct(a.shape, a.dtype) for a in arrs],
            scratch_shapes=list(scratch_shapes) + [pltpu.SemaphoreType.DMA((r.n_sem,)) for r in riders for _ in range(2)],
            input_output_aliases=aliases,
            compiler_params=_params(*["arbitrary"] * len(grid)),
        )(*args, *srcs, *arrs)
        core, rest = res[:n_out], list(res[n_out:])
        carried, a0 = [], 0
        for r in riders:
            carried.append(rest[a0:a0 + len(r.arrays)])
            a0 += len(r.arrays)
        return (core[0] if single else core), carried

    return run


def _mm(name, dims, a, a_spec, b, b_spec, grid, tile, outs, epilogue=None, extras=(), riders=()):
    nk, ne, no = grid[2], len(extras), len(outs)

    def body(a_ref, b_ref, *refs):
        e_refs, o_refs = refs[:ne], refs[ne:ne + no]
        part = lax.dot_general(a_ref[...], b_ref[...], dims, preferred_element_type=F32)

        def finish(acc):
            if epilogue is None:
                o_refs[0][...] = acc.astype(o_refs[0].dtype)
            else:
                epilogue(acc, [r[...] for r in e_refs], o_refs)

        if nk == 1:
            finish(part)
        else:
            acc_ref = refs[ne + no]
            k = pl.program_id(2)

            @pl.when(k == 0)
            def _():
                acc_ref[...] = part

            @pl.when(k > 0)
            def _():
                acc_ref[...] += part

            @pl.when(k == nk - 1)
            def _():
                finish(acc_ref[...])

    res = _carry(
        riders, body, name=name, grid=grid,
        in_specs=[a_spec, b_spec] + [s for _, s in extras],
        out_specs=[s for _, s in outs],
        out_shape=[s for s, _ in outs],
        scratch_shapes=[pltpu.VMEM(tile, F32)] if nk > 1 else [],
        semantics=("parallel", "parallel", "arbitrary"),
    )(a, b, *[e for e, _ in extras])
    res, carried = res if riders else (res, None)
    res = res[0] if no == 1 else res
    return (res, carried) if riders else res


def _tile_spec(tm, tn, col0=0):
    return pl.BlockSpec((tm, tn), lambda i, j, k: (i, j + col0))


def _out2d(m, n, dtype, tm, tn):
    return (jax.ShapeDtypeStruct((m, n), dtype), _tile_spec(tm, tn))


def _mm_fwd(name, a, w, tm, tn, tk, outs=None, epilogue=None, extras=(), col0=0, ncols=None, riders=()):
    m, kdim = a.shape
    if w.ndim == 3:
        per = w.shape[2] // tn
        n = ncols or N_SHARD * w.shape[2]
        w_spec = pl.BlockSpec((None, tk, tn), lambda i, j, k: ((j + col0) // per, k, (j + col0) % per))
    else:
        n = ncols or w.shape[1]
        w_spec = pl.BlockSpec((tk, tn), lambda i, j, k: (k, j + col0))
    if outs is None:
        outs = [_out2d(m, n, F32, tm, tn)]
    return _mm(name, NN, a, pl.BlockSpec((tm, tk), lambda i, j, k: (i, k)), w, w_spec,
               (m // tm, n // tn, kdim // tk), (tm, tn), outs, epilogue, extras, riders)


def _mm_bwd_x(name, g, g_spec, w, tm, tj, tc, m, n_contract, outs=None, epilogue=None, extras=(), riders=()):
    if w.ndim == 3:
        per = w.shape[2] // tc
        kdim = w.shape[1]
        w_spec = pl.BlockSpec((None, tj, tc), lambda i, j, n: (n // per, j, n % per))
    else:
        kdim = w.shape[0]
        w_spec = pl.BlockSpec((tj, tc), lambda i, j, n: (j, n))
    if outs is None:
        outs = [_out2d(m, kdim, F32, tm, tj)]
    return _mm(name, NT, g, g_spec, w, w_spec, (m // tm, kdim // tj, n_contract // tc),
               (tm, tj), outs, epilogue, extras, riders)


def _mm_bwd_w(name, a, g, g_spec, n, tk, tn, tm, sharded, riders=()):
    m, kdim = a.shape
    if sharded:
        per = (n // N_SHARD) // tn
        out = (jax.ShapeDtypeStruct((N_SHARD, kdim, n // N_SHARD), F32),
               pl.BlockSpec((None, tk, tn), lambda i, j, mm: (j // per, i, j % per)))
    else:
        out = (jax.ShapeDtypeStruct((kdim, n), F32), pl.BlockSpec((tk, tn), lambda i, j, mm: (i, j)))
    return _mm(name, TN, a, pl.BlockSpec((tm, tk), lambda i, j, mm: (mm, i)), g, g_spec,
               (kdim // tk, n // tn, m // tm), (tk, tn), [out], riders=riders)


def _ew(name, fn, tiles, fulls, outs, sums=(), ts=512, riders=()):
    tiles = [t if isinstance(t, tuple) else (t, pl.BlockSpec((ts, t.shape[1]), lambda i: (i, 0)))
             for t in tiles]
    s_rows = tiles[0][0].shape[-2]
    nt, nf, no = len(tiles), len(fulls), len(outs)

    def body(*refs):
        t_vals = [r[...] for r in refs[:nt]]
        f_vals = [r[...] for r in refs[nt:nt + nf]]
        o_refs, s_refs = refs[nt + nf:nt + nf + no], refs[nt + nf + no:]
        o_vals, s_vals = fn(t_vals, f_vals)
        for r, v in zip(o_refs, o_vals):
            r[...] = v.astype(r.dtype)
        for r, v in zip(s_refs, s_vals):
            part = jnp.sum(v, axis=0, keepdims=True)

            @pl.when(pl.program_id(0) == 0)
            def _():
                r[...] = part

            @pl.when(pl.program_id(0) > 0)
            def _():
                r[...] += part

    full_specs = [pl.BlockSpec(f.shape, lambda i, nd=f.ndim: (0,) * nd) for f in fulls]
    return _carry(
        riders, body, name=name, grid=(s_rows // ts,),
        in_specs=[s for _, s in tiles] + full_specs,
        out_specs=[pl.BlockSpec((ts, c), lambda i: (i, 0)) for c, _ in outs]
        + [pl.BlockSpec((1, c), lambda i: (0, 0)) for c in sums],
        out_shape=[jax.ShapeDtypeStruct((s_rows, c), dt) for c, dt in outs]
        + [jax.ShapeDtypeStruct((1, c), F32) for c in sums],
        semantics=("arbitrary",),
    )(*[t for t, _ in tiles], *fulls)


def _rms_fwd(name, x, g, riders=()):
    def fn(t, f):
        xv = t[0]
        r = lax.rsqrt(jnp.mean(xv * xv, axis=-1, keepdims=True) + EPS)
        return [xv * r * f[0]], []
    out = _ew(name, fn, [x], [g], [(x.shape[1], BF16)], riders=riders)
    return (out[0][0], out[1]) if riders else out[0]


def _rms_bwd(name, dh, x, g, dres, riders=()):
    def fn(t, f):
        dhv, xv, dr = t
        r = lax.rsqrt(jnp.mean(xv * xv, axis=-1, keepdims=True) + EPS)
        xhat = xv * r
        u = dhv * f[0]
        dx = r * (u - xhat * jnp.mean(u * xhat, axis=-1, keepdims=True)) + dr
        return [dx, dx], [dhv * xhat]
    c = x.shape[1]
    return _ew(name, fn, [dh, x, dres], [g], [(c, F32), (c, BF16)], sums=[c], riders=riders)


def _split3(x):
    x1 = x.astype(BF16)
    r1 = x - x1.astype(F32)
    x2 = r1.astype(BF16)
    x3 = (r1 - x2.astype(F32)).astype(BF16)
    return x1, x2, x3


def _rel_class(cp):
    far = (cp < MAX_REL) | (cp > BAND)
    return jnp.where(far, 2 * MAX_REL, BAND - cp)


def _skew_rows(x, sign):
    row = lax.broadcasted_iota(jnp.int32, x.shape, 0)
    for b in range(SUPER.bit_length() - 1):
        shift = (1 << b) if sign > 0 else SKEW_W - (1 << b)
        x = jnp.where((row >> b) & 1 == 1, pltpu.roll(x, shift, 1), x)
    return x


def _bias_expand(rel_bias, riders=()):
    rel = jnp.pad(rel_bias, ((0, 0), (0, REL_PAD - N_REL))).reshape(N_HEADS, 1, REL_PAD)

    def body(rel_ref, o_ref):
        cls = lax.broadcasted_iota(jnp.int32, (REL_PAD, SKEW_W), 0)
        cp = lax.broadcasted_iota(jnp.int32, (REL_PAD, SKEW_W), 1)
        onehot = (cls == _rel_class(cp)).astype(BF16)
        rel8 = jnp.broadcast_to(rel_ref[...], (8, REL_PAD))
        trow = sum(jnp.dot(p, onehot, preferred_element_type=F32) for p in _split3(rel8))[0:1]
        full = _skew_rows(jnp.broadcast_to(trow, (SUPER, SKEW_W)), +1)[:, :BAND]
        qc = lax.broadcasted_iota(jnp.int32, (SUPER, BAND), 0) // CHUNK
        kc = lax.broadcasted_iota(jnp.int32, (SUPER, BAND), 1) // CHUNK
        on_band = (kc >= qc) & (kc <= qc + N_PREV_CHUNKS)
        o_ref[...] = jnp.where(on_band, full, NEG_INF)

    return _carry(
        riders, body, name="bias_expand", grid=(N_HEADS,),
        in_specs=[pl.BlockSpec((None, 1, REL_PAD), lambda h: (h, 0, 0))],
        out_specs=pl.BlockSpec((None, SUPER, BAND), lambda h: (h, 0, 0)),
        out_shape=jax.ShapeDtypeStruct((N_HEADS, SUPER, BAND), F32),
        semantics=("arbitrary",),
    )(rel)


def _bias_reduce(dbias):
    def body(d_ref, o_ref):
        x = jnp.concatenate([d_ref[...], jnp.zeros((SUPER, SKEW_W - BAND), F32)], axis=1)
        diag = jnp.sum(_skew_rows(x, -1), axis=0, keepdims=True)
        cp = lax.broadcasted_iota(jnp.int32, (SKEW_W, REL_PAD), 0)
        cls = lax.broadcasted_iota(jnp.int32, (SKEW_W, REL_PAD), 1)
        onehot = (cls == _rel_class(cp)).astype(BF16)
        diag8 = jnp.broadcast_to(diag, (8, SKEW_W))
        o_ref[...] = sum(jnp.dot(p, onehot, preferred_element_type=F32) for p in _split3(diag8))[0:1]

    out = pl.pallas_call(
        body, name="bias_reduce", grid=(N_HEADS,),
        in_specs=[pl.BlockSpec((None, SUPER, BAND), lambda h: (h, 0, 0))],
        out_specs=pl.BlockSpec((None, 1, REL_PAD), lambda h: (h, 0, 0)),
        out_shape=jax.ShapeDtypeStruct((N_HEADS, 1, REL_PAD), F32),
        compiler_params=_params("arbitrary"),
    )(dbias)
    return out.reshape(N_HEADS, REL_PAD)[:, :N_REL]


HEADS_PER_STEP = 4
HEAD_COLS = HEADS_PER_STEP * HEAD_DIM
N_HEAD_GROUPS = N_HEADS // HEADS_PER_STEP


def _unit(x):
    r = lax.rsqrt(jnp.mean(x * x, axis=-1, keepdims=True) + EPS)
    return x * r, r


def _probs(qs, kn, bias, dead):
    s = jnp.concatenate([lax.dot_general(qs, k, NT, preferred_element_type=F32) for k in kn], axis=1) + bias
    s = jnp.where(dead, NEG_INF, s)
    e = jnp.exp(s - jnp.max(s, axis=-1, keepdims=True))
    return e / jnp.sum(e, axis=-1, keepdims=True)


def _band_specs(nb, col0, clamp_hi):
    def spec(d):
        def index(hg, i):
            blk = jnp.maximum(i - d, 0)
            if clamp_hi:
                blk = jnp.minimum(blk, nb - 1)
            return (blk, col0 + hg)
        return pl.BlockSpec((SUPER, HEAD_COLS), index)
    return [spec(2), spec(1), spec(0)]


def _head_sums(y):
    cols = y.shape[1]
    same_head = (lax.broadcasted_iota(jnp.int32, (cols, cols), 0) // HEAD_DIM
                 == lax.broadcasted_iota(jnp.int32, (cols, cols), 1) // HEAD_DIM).astype(BF16)
    hi = y.astype(BF16)
    lo = (y - hi.astype(F32)).astype(BF16)
    return (jnp.dot(hi, same_head, preferred_element_type=F32)
            + jnp.dot(lo, same_head, preferred_element_type=F32))


def _head_unit(x):
    r = lax.rsqrt(_head_sums(x * x) * (1.0 / HEAD_DIM) + EPS)
    return x * r, r


def _head(hh):
    return slice(HEAD_DIM * hh, HEAD_DIM * (hh + 1))


def _key_block(j):
    return slice(SUPER * j, SUPER * (j + 1))


def _dead_keys(i):
    col = lax.broadcasted_iota(jnp.int32, (SUPER, BAND), 1)
    return col < (2 - i) * SUPER


def _attn_fwd(qkn, v, bias, riders=()):
    s_len = qkn.shape[0]
    nb = s_len // SUPER

    def body(q_ref, k0, k1, k2, v0, v1, v2, b_ref, o_ref):
        dead = _dead_keys(pl.program_id(1))
        outs = []
        for hh in range(HEADS_PER_STEP):
            sl = _head(hh)
            p = _probs(q_ref[:, sl], [k0[:, sl], k1[:, sl], k2[:, sl]], b_ref[hh], dead).astype(BF16)
            outs.append(sum(jnp.dot(p[:, _key_block(j)], vj[:, sl], preferred_element_type=F32)
                            for j, vj in enumerate((v0, v1, v2))))
        o_ref[...] = jnp.concatenate(outs, axis=1).astype(o_ref.dtype)

    return _carry(
        riders, body, name="attn_fwd", grid=(N_HEAD_GROUPS, nb),
        in_specs=[pl.BlockSpec((SUPER, HEAD_COLS), lambda hg, i: (i, hg))]
        + _band_specs(nb, N_HEAD_GROUPS, False) + _band_specs(nb, 0, False)
        + [pl.BlockSpec((HEADS_PER_STEP, SUPER, BAND), lambda hg, i: (hg, 0, 0))],
        out_specs=pl.BlockSpec((SUPER, HEAD_COLS), lambda hg, i: (i, hg)),
        out_shape=jax.ShapeDtypeStruct((s_len, D_MODEL), BF16),
        semantics=("parallel", "arbitrary"),
    )(qkn, qkn, qkn, qkn, v, v, v, bias)


def _attn_bwd(qkn, v, d_out, bias, riders=()):
    s_len = qkn.shape[0]
    nb = s_len // SUPER

    def body(q_ref, k0, k1, k2, v0, v1, v2, do_ref, b_ref, dp_ref, db_ref, aq_ref, ak_ref, av_ref):
        i = pl.program_id(1)

        @pl.when(i == 0)
        def _():
            aq_ref[...] = jnp.zeros_like(aq_ref)
            ak_ref[...] = jnp.zeros_like(ak_ref)
            av_ref[...] = jnp.zeros_like(av_ref)
            db_ref[...] = jnp.zeros_like(db_ref)

        @pl.when(i < nb)
        def _():
            dead = _dead_keys(i)
            dq, dk, dv = [], [[], [], []], [[], [], []]
            for hh in range(HEADS_PER_STEP):
                sl = _head(hh)
                qs, do = q_ref[:, sl], do_ref[:, sl]
                kn = [k0[:, sl], k1[:, sl], k2[:, sl]]
                vv = [v0[:, sl], v1[:, sl], v2[:, sl]]
                p = _probs(qs, kn, b_ref[hh], dead)
                pb = p.astype(BF16)
                dp = jnp.concatenate([lax.dot_general(do, vj, NT, preferred_element_type=F32) for vj in vv], axis=1)
                ds = p * (dp - jnp.sum(p * dp, axis=-1, keepdims=True))
                db_ref[hh] += ds
                dsb = ds.astype(BF16)
                dq.append(sum(jnp.dot(dsb[:, _key_block(j)], kn[j], preferred_element_type=F32) for j in range(3)))
                for j in range(3):
                    dv[j].append(lax.dot_general(pb[:, _key_block(j)], do, TN, preferred_element_type=F32))
                    dk[j].append(lax.dot_general(dsb[:, _key_block(j)], qs, TN, preferred_element_type=F32))
            aq_ref[i % 3] = jnp.concatenate(dq, axis=1)
            for j in range(3):
                slot = (i + 1 + j) % 3
                if j < 2:
                    ak_ref[slot] += jnp.concatenate(dk[j], axis=1)
                    av_ref[slot] += jnp.concatenate(dv[j], axis=1)
                else:
                    ak_ref[slot] = jnp.concatenate(dk[j], axis=1)
                    av_ref[slot] = jnp.concatenate(dv[j], axis=1)

        slot = (i + 1) % 3
        dp_ref[0] = aq_ref[slot].astype(dp_ref.dtype)
        dp_ref[1] = ak_ref[slot].astype(dp_ref.dtype)
        dp_ref[2] = av_ref[slot].astype(dp_ref.dtype)

    def qrow(hg, i):
        return (jnp.minimum(i, nb - 1), hg)

    return _carry(
        riders, body, name="attn_bwd", grid=(N_HEAD_GROUPS, nb + 2),
        in_specs=[pl.BlockSpec((SUPER, HEAD_COLS), qrow)]
        + _band_specs(nb, N_HEAD_GROUPS, True) + _band_specs(nb, 0, True)
        + [pl.BlockSpec((SUPER, HEAD_COLS), qrow),
           pl.BlockSpec((HEADS_PER_STEP, SUPER, BAND), lambda hg, i: (hg, 0, 0))],
        out_specs=[pl.BlockSpec((3, SUPER, HEAD_COLS), lambda hg, i: (0, jnp.maximum(i - 2, 0), hg)),
                   pl.BlockSpec((HEADS_PER_STEP, SUPER, BAND), lambda hg, i: (hg, 0, 0))],
        out_shape=[jax.ShapeDtypeStruct((6, s_len, D_MODEL), BF16),
                   jax.ShapeDtypeStruct((N_HEADS, SUPER, BAND), F32)],
        scratch_shapes=[pltpu.VMEM((3, SUPER, HEAD_COLS), F32)] * 3,
        semantics=("parallel", "arbitrary"),
    )(qkn, qkn, qkn, qkn, v, v, v, d_out, bias)


def _qk_norm_bwd(dproj6, qk_raw, gq, gk):
    s_len = qk_raw.shape[0]
    ts = 256

    nsteps = s_len // ts
    half = D_MODEL // 2

    def body(d_ref, raw_ref, gq_ref, gk_ref, o_ref, dgq_ref, dgk_ref, acc_ref):
        step = pl.program_id(0)

        @pl.when(step == 0)
        def _():
            acc_ref[...] = jnp.zeros_like(acc_ref)

        for piece, (g_ref, scale) in enumerate(((gq_ref, QK_SCALE), (gk_ref, 1.0))):
            for c0 in (0, half):
                xhat, r = _head_unit(raw_ref[:, piece * D_MODEL + c0:piece * D_MODEL + c0 + half].astype(F32))
                dn = d_ref[piece, :, c0:c0 + half].astype(F32) * scale
                u = dn * g_ref[...]
                dx = r * (u - xhat * (_head_sums(u * xhat) * (1.0 / HEAD_DIM)))
                o_ref[piece, :, c0:c0 + half] = dx.astype(o_ref.dtype)
                acc_ref[piece:piece + 1, c0:c0 + half] += jnp.sum(dn * xhat, axis=0, keepdims=True)

        @pl.when(step == nsteps - 1)
        def _():
            lane = lax.broadcasted_iota(jnp.int32, (D_MODEL, LANE), 0) % HEAD_DIM
            fold = (lane == lax.broadcasted_iota(jnp.int32, (D_MODEL, LANE), 1)).astype(BF16)
            tot = sum(jnp.dot(p, fold, preferred_element_type=F32) for p in _split3(acc_ref[...]))
            dgq_ref[...] = tot[0:1, :HEAD_DIM]
            dgk_ref[...] = tot[1:2, :HEAD_DIM]

    gain = pl.BlockSpec((1, half), lambda i: (0, 0))
    small = pl.BlockSpec((1, HEAD_DIM), lambda i: (0, 0))
    per_head = lambda g: jnp.tile(g, (1, half // HEAD_DIM))
    return pl.pallas_call(
        body, name="qk_norm_bwd", grid=(nsteps,),
        in_specs=[pl.BlockSpec((2, ts, D_MODEL), lambda i: (0, i, 0)),
                  pl.BlockSpec((ts, 2 * D_MODEL), lambda i: (i, 0)), gain, gain],
        out_specs=[pl.BlockSpec((2, ts, D_MODEL), lambda i: (0, i, 0)), small, small],
        out_shape=[jax.ShapeDtypeStruct(dproj6.shape, dproj6.dtype),
                   jax.ShapeDtypeStruct((1, HEAD_DIM), F32), jax.ShapeDtypeStruct((1, HEAD_DIM), F32)],
        scratch_shapes=[pltpu.VMEM((8, D_MODEL), F32)],
        input_output_aliases={0: 0},
        compiler_params=_params("arbitrary"),
    )(dproj6, qk_raw, per_head(gq), per_head(gk))


CONV_ROWS = 512
HALO = 16


def _rows_with_halo(ref, r0, n, front, s_len):
    zeros = jnp.zeros((HALO, ref.shape[1]), F32)
    if front:
        return (jnp.concatenate([zeros, ref[0:n, :].astype(F32)], axis=0) if r0 == 0
                else ref[r0 - HALO:r0 + n, :].astype(F32))
    return (jnp.concatenate([ref[r0:r0 + n, :].astype(F32), zeros], axis=0) if r0 + n == s_len
            else ref[r0:r0 + n + HALO, :].astype(F32))


def _earlier(ext, k):
    return pltpu.roll(ext, k, 0)[HALO:]


def _later(ext, k):
    n = ext.shape[0]
    return pltpu.roll(ext, n - k, 0)[:n - HALO]


def _conv_cols(col0):
    return lambda s_len: pl.BlockSpec((s_len, LANE), lambda j: (0, col0 + j))


def _conv_fwd(proj, conv_w, conv_b, riders=()):
    s_len = proj.shape[0]

    def body(bg_ref, cg_ref, xc_ref, w_ref, b_ref, o_ref):
        w = [w_ref[t:t + 1, :] for t in range(3)]
        for r0 in range(0, s_len, CONV_ROWS):
            u = _rows_with_halo(cg_ref, r0, CONV_ROWS, True, s_len) * \
                _rows_with_halo(xc_ref, r0, CONV_ROWS, True, s_len)
            conv = b_ref[...] + w[0] * _earlier(u, 2) + w[1] * _earlier(u, 1) + w[2] * u[HALO:]
            o_ref[r0:r0 + CONV_ROWS, :] = (bg_ref[r0:r0 + CONV_ROWS, :].astype(F32) * conv).astype(o_ref.dtype)

    return _carry(
        riders, body, name="conv_fwd", grid=(D_MODEL // LANE,),
        in_specs=[_conv_cols(0)(s_len), _conv_cols(8)(s_len), _conv_cols(16)(s_len),
                  pl.BlockSpec((3, LANE), lambda j: (0, j)), pl.BlockSpec((1, LANE), lambda j: (0, j))],
        out_specs=pl.BlockSpec((s_len, LANE), lambda j: (0, j)),
        out_shape=jax.ShapeDtypeStruct((s_len, D_MODEL), BF16),
        semantics=("parallel",),
    )(proj, proj, proj, conv_w, conv_b)


def _conv_bwd(dproj6, dy, proj, conv_w, conv_b, riders=()):
    s_len = proj.shape[0]

    def body(dy_ref, bg_ref, cg_ref, xc_ref, w_ref, b_ref, _, dp_ref, dw_ref):
        w = [w_ref[t:t + 1, :] for t in range(3)]
        acc = [jnp.zeros((1, LANE), F32) for _ in range(4)]
        for r0 in range(0, s_len, CONV_ROWS):
            rows = slice(r0, r0 + CONV_ROWS)
            u = _rows_with_halo(cg_ref, r0, CONV_ROWS, True, s_len) * \
                _rows_with_halo(xc_ref, r0, CONV_ROWS, True, s_len)
            u2, u1, u0 = _earlier(u, 2), _earlier(u, 1), u[HALO:]
            conv = b_ref[...] + w[0] * u2 + w[1] * u1 + w[2] * u0
            dp_ref[0, rows, :] = (dy_ref[rows, :].astype(F32) * conv).astype(dp_ref.dtype)
            dconv_ext = _rows_with_halo(dy_ref, r0, CONV_ROWS, False, s_len) * \
                _rows_with_halo(bg_ref, r0, CONV_ROWS, False, s_len)
            dconv = dconv_ext[:CONV_ROWS]
            for t, term in enumerate([dconv * u2, dconv * u1, dconv * u0, dconv]):
                acc[t] = acc[t] + jnp.sum(term, axis=0, keepdims=True)
            du = w[2] * dconv + w[1] * _later(dconv_ext, 1) + w[0] * _later(dconv_ext, 2)
            dp_ref[1, rows, :] = (du * xc_ref[rows, :].astype(F32)).astype(dp_ref.dtype)
            dp_ref[2, rows, :] = (du * cg_ref[rows, :].astype(F32)).astype(dp_ref.dtype)
        dw_ref[...] = jnp.zeros_like(dw_ref)
        for t in range(4):
            dw_ref[t:t + 1, :] = acc[t]

    return _carry(
        riders, body, name="conv_bwd", grid=(D_MODEL // LANE,),
        in_specs=[pl.BlockSpec((s_len, LANE), lambda j: (0, j)),
                  _conv_cols(0)(s_len), _conv_cols(8)(s_len), _conv_cols(16)(s_len),
                  pl.BlockSpec((3, LANE), lambda j: (0, j)), pl.BlockSpec((1, LANE), lambda j: (0, j)),
                  pl.BlockSpec(memory_space=pl.ANY)],
        out_specs=[pl.BlockSpec((3, s_len, LANE), lambda j: (1, 0, j)),
                   pl.BlockSpec((8, LANE), lambda j: (0, j))],
        out_shape=[jax.ShapeDtypeStruct(dproj6.shape, dproj6.dtype),
                   jax.ShapeDtypeStruct((8, D_MODEL), F32)],
        input_output_aliases={6: 0},
        semantics=("parallel",),
    )(dy, proj, proj, proj, conv_w, conv_b, dproj6)


class _NoComm:
    def __init__(self, early, late):
        self.early, self.late = early, late

    def riders(self, at):
        return ()

    def done(self, at, carried):
        pass

    def early_weights(self):
        return self.early

    def late_weights(self):
        return self.late

    def grads_ready(self, group, grads):
        pass


def _d_input(name, pieces, w3, extra=None, riders=()):
    n_piece, s_len, width = pieces.shape
    _, kdim, per_shard = w3.shape
    tm, chunk = 256, 512

    def body(p_ref, w_ref, *refs):
        o_ref = refs[-1]
        acc = refs[0][...] if extra is not None else jnp.zeros(o_ref.shape, F32)
        for c0 in range(0, n_piece * width, chunk):
            acc = acc + lax.dot_general(p_ref[c0 // width, :, c0 % width:c0 % width + chunk],
                                        w_ref[c0 // per_shard, :, c0 % per_shard:c0 % per_shard + chunk],
                                        NT, preferred_element_type=F32)
        o_ref[...] = acc

    tile = pl.BlockSpec((tm, kdim), lambda i: (i, 0))
    return _carry(
        riders, body, name=name, grid=(s_len // tm,),
        in_specs=[pl.BlockSpec((n_piece, tm, width), lambda i: (0, i, 0)),
                  pl.BlockSpec(w3.shape, lambda i: (0, 0, 0))] + ([tile] if extra is not None else []),
        out_specs=tile, out_shape=jax.ShapeDtypeStruct((s_len, kdim), F32),
        semantics=("parallel",),
    )(pieces, w3, *([extra] if extra is not None else []))


def _local_grads(x, target, norm1_g, q_norm_g, k_norm_g, rel_bias, conv_b, b_gate, norm2_g, comm):
    s_len = x.shape[0]
    tm = min(1024, s_len)
    row = lambda v: v.reshape(1, -1)

    def carrying(at, fn, *args, **kw):
        riders = comm.riders(at)
        out = fn(*args, riders=riders, **kw)
        if riders:
            out, carried = out
            comm.done(at, carried)
        return out

    bias = carrying("bias_expand", _bias_expand, rel_bias)
    h = carrying("norm1", _rms_fwd, "norm1", x, row(norm1_g))
    w_in3, w_gate3, conv_w = comm.early_weights()
    gq, gk = row(q_norm_g), row(k_norm_g)

    def qk_epi(acc, e, o):
        o[0][...] = acc.astype(BF16)
        gain = jnp.where(pl.program_id(1) < 2, e[0] * QK_SCALE, e[1])
        o[1][...] = (_head_unit(acc)[0] * gain).astype(BF16)
    small = pl.BlockSpec((1, 512), lambda i, j, k: (0, 0))
    per_head = lambda g: jnp.tile(g, (1, 512 // HEAD_DIM))
    qk_raw, qkn = _mm_fwd("proj_qk", h, w_in3, tm, 512, D_MODEL, ncols=2 * D_MODEL, epilogue=qk_epi,
                          outs=[_out2d(s_len, 2 * D_MODEL, BF16, tm, 512)] * 2,
                          extras=[(per_head(gq), small), (per_head(gk), small)])
    v = _mm_fwd("proj_v", h, w_in3, tm, 512, D_MODEL, col0=4, ncols=D_MODEL,
                outs=[_out2d(s_len, D_MODEL, BF16, tm, 512)])
    conv_in = _mm_fwd("proj_conv", h, w_in3, tm, 1536, D_MODEL, col0=2, ncols=3 * D_MODEL,
                      outs=[_out2d(s_len, 3 * D_MODEL, BF16, tm, 1536)])

    def gate_epi(acc, e, o):
        o[0][...] = jax.nn.sigmoid(acc + e[0]).astype(BF16)
    gates = _mm_fwd("gates", h, w_gate3, tm, 512, D_MODEL, epilogue=gate_epi,
                    outs=[_out2d(s_len, 2 * D_MODEL, BF16, tm, 512)],
                    extras=[(row(b_gate), pl.BlockSpec((1, 512), lambda i, j, k: (0, j)))])

    attn = carrying("attn_fwd", _attn_fwd, qkn, v, bias)
    yconv = carrying("conv_fwd", _conv_fwd, conv_in, conv_w, row(conv_b))
    w_ap, w_cp, w_out, w_up3, w_down = comm.late_weights()
    tw = 1024
    ya = _mm_fwd("attn_proj", attn, w_ap, tm, tw, D_MODEL, outs=[_out2d(s_len, D_MODEL, BF16, tm, tw)])

    def merge_epi(acc, e, o):
        ya_v, ga, gc = [t.astype(F32) for t in e]
        o[0][...] = acc.astype(BF16)
        o[1][...] = (ga * ya_v + gc * acc).astype(BF16)
    gate_a, gate_c = (gates, _tile_spec(tm, tw, 0)), (gates, _tile_spec(tm, tw, 1))
    yc, merged = _mm_fwd("conv_proj", yconv, w_cp, tm, tw, D_MODEL, epilogue=merge_epi,
                         outs=[_out2d(s_len, D_MODEL, BF16, tm, tw), _out2d(s_len, D_MODEL, BF16, tm, tw)],
                         extras=[(ya, _tile_spec(tm, tw)), gate_a, gate_c])

    def res_epi(acc, e, o):
        o[0][...] = e[0] + acc
    x1 = _mm_fwd("out_proj", merged, w_out, tm, tw, D_MODEL, epilogue=res_epi,
                 extras=[(x, _tile_spec(tm, tw))])
    h2 = _rms_fwd("norm2", x1, row(norm2_g))

    def up_epi(acc, e, o):
        o[0][...] = acc.astype(BF16)
        o[1][...] = jnp.square(jnp.maximum(acc, 0.0)).astype(BF16)
    up, act = _mm_fwd("mlp_up", h2, w_up3, tm, tw, D_MODEL, epilogue=up_epi,
                      outs=[_out2d(s_len, D_FF, BF16, tm, tw), _out2d(s_len, D_FF, BF16, tm, tw)])

    def loss_epi(acc, e, o):
        err = e[0] + acc - e[1]
        dy = err * (1.0 / D_MODEL)
        o[0][...] = dy
        o[1][...] = dy.astype(BF16)
        sq = err * err
        part = sq[:, 0:LANE]
        for c0 in range(LANE, D_MODEL, LANE):
            part = part + sq[:, c0:c0 + LANE]
        o[2][...] = jnp.sum(part.reshape(tl // 8, 8, LANE), axis=0)
    tl = 512
    dy, dy_b, loss_part = _mm_fwd(
        "mlp_down", act, w_down, tl, D_MODEL, D_FF, epilogue=loss_epi,
        outs=[_out2d(s_len, D_MODEL, F32, tl, D_MODEL), _out2d(s_len, D_MODEL, BF16, tl, D_MODEL),
              (jax.ShapeDtypeStruct((8 * (s_len // tl), LANE), F32), pl.BlockSpec((8, LANE), lambda i, j, k: (i, 0)))],
        extras=[(x1, _tile_spec(tl, D_MODEL)), (target, _tile_spec(tl, D_MODEL))])

    def dup_epi(acc, e, o):
        o[0][...] = (acc * (2.0 * jnp.maximum(e[0].astype(F32), 0.0))).astype(BF16)
    full = lambda cols: pl.BlockSpec((tm, cols), lambda i, j, n: (i, n))
    tokens = lambda cols: pl.BlockSpec((s_len, cols), lambda i, j, m: (m, j))
    dup = _mm_bwd_x("d_act", dy_b, full(D_MODEL), w_down, tm, tw, D_MODEL, s_len, D_MODEL, epilogue=dup_epi,
                    outs=[_out2d(s_len, D_FF, BF16, tm, tw)], extras=[(up, _tile_spec(tm, tw))])
    g_down = _mm_bwd_w("g_down", act, dy_b, tokens(D_MODEL), D_MODEL, 512, D_MODEL, s_len, False)
    g_up = _mm_bwd_w("g_up", h2, dup, tokens(512), D_FF, D_MODEL, 512, s_len, True)
    comm.grads_ready("mlp", dict(w_down=g_down, w_up=g_up))
    dh2 = carrying("d_h2", _mm_bwd_x, "d_h2", dup, full(1024), w_up3, tm, tw, 1024, s_len, D_FF)
    dx1, dx1_b, dg2 = _rms_bwd("norm2_bwd", dh2, x1, row(norm2_g), dy)

    def dmerge_epi(acc, e, o):
        ya_v, yc_v, ga, gc = [t.astype(F32) for t in e]
        o[0][...] = (acc * ga).astype(BF16)
        o[1][...] = (acc * gc).astype(BF16)
        o[2][0] = (acc * ya_v * ga * (1.0 - ga)).astype(BF16)
        o[2][1] = (acc * yc_v * gc * (1.0 - gc)).astype(BF16)
    dya, dyc, dgp2 = _mm_bwd_x(
        "d_merged", dx1_b, full(D_MODEL), w_out, tm, tw, D_MODEL, s_len, D_MODEL, epilogue=dmerge_epi,
        outs=[_out2d(s_len, D_MODEL, BF16, tm, tw), _out2d(s_len, D_MODEL, BF16, tm, tw),
              (jax.ShapeDtypeStruct((2, s_len, D_MODEL), BF16), pl.BlockSpec((2, tm, tw), lambda i, j, n: (0, i, j)))],
        extras=[(ya, _tile_spec(tm, tw)), (yc, _tile_spec(tm, tw)), gate_a, gate_c])
    g_out = _mm_bwd_w("g_out", merged, dx1_b, tokens(512), D_MODEL, D_MODEL, 512, s_len, False)
    d_attn = _mm_bwd_x("d_attn", dya, full(D_MODEL), w_ap, tm, tw, D_MODEL, s_len, D_MODEL,
                       outs=[_out2d(s_len, D_MODEL, BF16, tm, tw)])
    g_ap = _mm_bwd_w("g_attn_proj", attn, dya, tokens(512), D_MODEL, D_MODEL, 512, s_len, False)
    d_yconv = _mm_bwd_x("d_yconv", dyc, full(D_MODEL), w_cp, tm, tw, D_MODEL, s_len, D_MODEL,
                        outs=[_out2d(s_len, D_MODEL, BF16, tm, tw)])
    g_cp = _mm_bwd_w("g_conv_proj", yconv, dyc, tokens(512), D_MODEL, D_MODEL, 512, s_len, False)
    comm.grads_ready("proj", dict(w_out=g_out, w_attn_proj=g_ap, w_conv_proj=g_cp))

    dproj6, dbias = carrying("attn_bwd", _attn_bwd, qkn, v, d_attn, bias)
    dproj6, dgq, dgk = _qk_norm_bwd(dproj6, qk_raw, gq, gk)
    dproj6, dconv_wb = carrying("conv_bwd", _conv_bwd, dproj6, d_yconv, conv_in, conv_w, row(conv_b))
    d_rel = _bias_reduce(dbias)

    piece = lambda width: (lambda blk: (blk * width) // D_MODEL, lambda blk: (blk * width % D_MODEL) // width)
    pc, cb = piece(512)
    pieces = pl.BlockSpec((None, s_len, 512), lambda i, j, m: (pc(j), m, cb(j)))
    g_in = carrying("g_in", _mm_bwd_w, "g_in", h, dproj6, pieces, 6 * D_MODEL, D_MODEL, 512, s_len, True)
    g_gate = carrying("g_gate", _mm_bwd_w, "g_gate", h, dgp2, pieces, 2 * D_MODEL, D_MODEL, 512, s_len, True)
    comm.grads_ready("in", dict(w_in=g_in, w_gate=g_gate))
    dh_g = carrying("d_h_gate", _d_input, "d_h_gate", dgp2, w_gate3)
    dh = carrying("d_h_proj", _d_input, "d_h_proj", dproj6, w_in3, extra=dh_g)
    grad_x, _, dg1 = carrying("norm1_bwd", _rms_bwd, "norm1_bwd", dh, x, row(norm1_g), dx1)

    def bsum(t, f):
        return [], [t[0].astype(F32), t[1].astype(F32)]
    ts = 512
    db_a, db_c = _ew("b_gate_sum", bsum,
                     [(dgp2, pl.BlockSpec((None, ts, D_MODEL), lambda i: (0, i, 0))),
                      (dgp2, pl.BlockSpec((None, ts, D_MODEL), lambda i: (1, i, 0)))],
                     [], [], sums=[D_MODEL, D_MODEL], ts=ts)

    big = dict(w_in=g_in, w_attn_proj=g_ap, w_conv_proj=g_cp, w_gate=g_gate, w_out=g_out,
               w_up=g_up, w_down=g_down)
    small = dict(norm1_g=dg1, norm2_g=dg2, conv_wb=dconv_wb, b_gate=(db_a, db_c),
                 q_norm_g=dgq, k_norm_g=dgk, rel_bias=d_rel)
    return loss_part, grad_x, big, small


def _finish_loss(loss_part):
    def body(l_ref, lo_ref):
        total = jnp.sum(jnp.sum(l_ref[...], axis=0, keepdims=True), axis=1, keepdims=True)
        lo_ref[...] = jnp.broadcast_to(total * (0.5 / D_MODEL), lo_ref.shape)

    return pl.pallas_call(body, name="finish_loss", out_shape=jax.ShapeDtypeStruct((8, LANE), F32))(loss_part)


HBM_SPEC = pl.BlockSpec(memory_space=pl.ANY)


def _place():
    return lax.axis_index("x"), lax.axis_index("y"), lax.axis_index("c")


def _other_chips(x, y):
    return [(1 - x, y), (x, 1 - y), (1 - x, 1 - y)]


def _cast_into_slot(name, where, w):
    r, cols = w.shape
    ts = 256

    def body(w_ref, x_ref, o_ref):
        o_ref[...] = x_ref[...].astype(o_ref.dtype)

    return pl.pallas_call(
        body, name=name,
        grid_spec=pltpu.PrefetchScalarGridSpec(
            num_scalar_prefetch=1, grid=(r // ts,),
            in_specs=[pl.BlockSpec((ts, cols), lambda i, w: (i, 0))],
            out_specs=pl.BlockSpec((None, ts, cols), lambda i, w: (w[0], i, 0))),
        out_shape=jax.ShapeDtypeStruct((N_SHARD, r, cols), BF16),
        compiler_params=_params("parallel"),
    )(where, w)


def _remote(src, dst, send, recv, k, to):
    return pltpu.make_async_remote_copy(src_ref=src, dst_ref=dst, send_sem=send.at[k], recv_sem=recv.at[k],
                                        device_id=to, device_id_type=MESH)


def _gather_riders(slots):
    n = len(slots)

    def half(refs, w, shard, which):
        hr = slots[w].shape[1] // 2
        return refs[w].at[shard, pl.ds(which * hr, hr)]

    def each(fn):
        x, y, c = _place()
        for w in range(n):
            for j, chip in enumerate(_other_chips(x, y)):
                fn(w, 3 * w + j, 2 * x + y, 2 * chip[0] + chip[1], c, (*chip, c), (x, y, 1 - c))

    def chips_start(_, refs, send, recv):
        each(lambda w, k, me, them, c, peer, sib: _remote(half(refs, w, me, c), half(refs, w, me, c), send, recv, k, peer).start())

    def chips_finish(_, refs, send, recv):
        each(lambda w, k, me, them, c, peer, sib: _remote(half(refs, w, them, c), half(refs, w, them, c), send, recv, k, peer).wait_recv())
        each(lambda w, k, me, them, c, peer, sib: _remote(half(refs, w, me, c), half(refs, w, me, c), send, recv, k, peer).wait_send())

    def sibling_start(_, refs, send, recv):
        each(lambda w, k, me, them, c, peer, sib: _remote(half(refs, w, them, c), half(refs, w, them, c), send, recv, k, sib).start())

    def sibling_finish(_, refs, send, recv):
        each(lambda w, k, me, them, c, peer, sib: _remote(half(refs, w, them, 1 - c), half(refs, w, them, 1 - c), send, recv, k, sib).wait_recv())
        each(lambda w, k, me, them, c, peer, sib: _remote(half(refs, w, them, c), half(refs, w, them, c), send, recv, k, sib).wait_send())

    return (lambda s: _Rider([], s, 3 * n, chips_start, chips_finish),
            lambda s: _Rider([], s, 3 * n, sibling_start, sibling_finish))


def _pair_exchange_rider(grads, landing):
    n = len(grads)

    def copies(srcs, dsts, send, recv):
        x, y, c = _place()
        out = []
        for w in range(n):
            hr = grads[w].shape[1] // 2
            out.append(_remote(srcs[w].at[:, pl.ds((1 - c) * hr, hr)], dsts[w], send, recv, w, (x, y, 1 - c)))
        return out

    def start(srcs, dsts, send, recv):
        for cp in copies(srcs, dsts, send, recv):
            cp.start()

    def finish(srcs, dsts, send, recv):
        for cp in copies(srcs, dsts, send, recv):
            cp.wait()

    return _Rider(grads, landing, n, start, finish)


def _row_tile(hr):
    return min(hr, 256)


def _pair_add(name, where, grad, got):
    _, hr, cols = got.shape
    tr = _row_tile(hr)
    nblk = hr // tr

    def body(w_ref, g_ref, r_ref, o_ref):
        o_ref[...] = (g_ref[...] + r_ref[...]).astype(o_ref.dtype)

    return pl.pallas_call(
        body, name=name,
        grid_spec=pltpu.PrefetchScalarGridSpec(
            num_scalar_prefetch=1, grid=(N_SHARD, nblk),
            in_specs=[pl.BlockSpec((None, tr, cols), lambda s, i, w: (s, w[1] * nblk + i, 0)),
                      pl.BlockSpec((None, tr, cols), lambda s, i, w: (s, i, 0))],
            out_specs=pl.BlockSpec((None, tr, cols), lambda s, i, w: (s, i, 0))),
        out_shape=jax.ShapeDtypeStruct(got.shape, BF16),
        compiler_params=_params("parallel", "parallel"),
    )(where, grad, got)


def _chip_exchange_rider(partials, landing):
    n = len(partials)

    def copies(srcs, dsts, send, recv):
        x, y, c = _place()
        return [_remote(srcs[w].at[2 * chip[0] + chip[1]], dsts[w].at[j], send, recv, 3 * w + j, (*chip, c))
                for w in range(n) for j, chip in enumerate(_other_chips(x, y))]

    def start(srcs, dsts, send, recv):
        for cp in copies(srcs, dsts, send, recv):
            cp.start()

    def finish(srcs, dsts, send, recv):
        for cp in copies(srcs, dsts, send, recv):
            cp.wait()

    return _Rider(partials, landing, 3 * n, start, finish)


def _final_add(name, where, grad, got, arrived):
    _, hr, cols = got.shape
    tr = _row_tile(hr)
    nblk = hr // tr

    def body(w_ref, g_ref, r_ref, a_ref, o_ref):
        acc = g_ref[...] + r_ref[...]
        for j in range(3):
            acc = acc + a_ref[j].astype(F32)
        o_ref[...] = acc

    return pl.pallas_call(
        body, name=name,
        grid_spec=pltpu.PrefetchScalarGridSpec(
            num_scalar_prefetch=1, grid=(nblk,),
            in_specs=[pl.BlockSpec((None, tr, cols), lambda i, w: (w[0], w[1] * nblk + i, 0)),
                      pl.BlockSpec((None, tr, cols), lambda i, w: (w[0], i, 0)),
                      pl.BlockSpec((3, tr, cols), lambda i, w: (0, i, 0))],
            out_specs=pl.BlockSpec((tr, cols), lambda i, w: (w[1] * nblk + i, 0))),
        out_shape=jax.ShapeDtypeStruct((2 * hr, cols), F32),
        compiler_params=_params("parallel"),
    )(where, grad, got, arrived)


def _pair_share_rider(shards):
    n = len(shards)

    def half(refs, w, which):
        hr = shards[w].shape[0] // 2
        return refs[w].at[pl.ds(which * hr, hr)]

    def start(_, refs, send, recv):
        x, y, c = _place()
        for w in range(n):
            _remote(half(refs, w, c), half(refs, w, c), send, recv, w, (x, y, 1 - c)).start()

    def finish(_, refs, send, recv):
        x, y, c = _place()
        for w in range(n):
            _remote(half(refs, w, 1 - c), half(refs, w, 1 - c), send, recv, w, (x, y, 1 - c)).wait_recv()
        for w in range(n):
            _remote(half(refs, w, c), half(refs, w, c), send, recv, w, (x, y, 1 - c)).wait_send()

    return _Rider([], shards, n, start, finish)


class _Exchange:
    PLAN = {"bias_expand": [("early", "gather")], "norm1": [("early", "forward")],
            "attn_fwd": [("late", "gather")], "conv_fwd": [("late", "forward")],
            "d_h2": [("mlp", "pair")], "attn_bwd": [("mlp", "chips"), ("proj", "pair")], "conv_bwd": [("mlp", "share")],
            "g_in": [("proj", "chips")], "g_gate": [("proj", "share")],
            "d_h_gate": [("in", "pair")], "d_h_proj": [("in", "chips")], "norm1_bwd": [("in", "share")]}

    def __init__(self, where, early_slots, late_slots):
        self.where = where
        self.slots = dict(early=early_slots, late=late_slots)
        self.stage = {g: dict(zip(("gather", "forward"), _gather_riders(s))) for g, s in self.slots.items()}
        self.groups, self.reduced, self.pending = {}, {}, []

    def early_weights(self):
        w_in3, w_gate3, small = self.slots["early"]
        conv_w = small[:, :3, :].transpose(1, 0, 2).reshape(3, N_SHARD * small.shape[2])
        return w_in3, w_gate3, conv_w

    def late_weights(self):
        rows = lambda a: a.reshape(a.shape[0] * a.shape[1], a.shape[2])
        w_ap, w_cp, w_out, w_up3, w_down = self.slots["late"]
        return rows(w_ap), rows(w_cp), rows(w_out), w_up3, rows(w_down)

    def grads_ready(self, group, grads):
        names = list(grads)
        g4 = [g if g.ndim == 3 else g.reshape(N_SHARD, -1, g.shape[1]) for g in grads.values()]
        self.groups[group] = dict(names=names, g4=g4)

    def riders(self, at):
        self.pending = self.PLAN.get(at, [])
        out = []
        for group, stage in self.pending:
            if group in self.slots:
                out.append(self.stage[group][stage](self.slots[group]))
                continue
            st = self.groups[group]
            if stage == "pair":
                landing = [lax.empty((N_SHARD, g.shape[1] // 2, g.shape[2]), F32) for g in st["g4"]]
                out.append(_pair_exchange_rider(st["g4"], landing))
            elif stage == "chips":
                landing = [lax.empty((3,) + p.shape[1:], p.dtype) for p in st["partial"]]
                out.append(_chip_exchange_rider(st["partial"], landing))
            else:
                out.append(_pair_share_rider(st["halves"]))
        return out

    def done(self, at, carried):
        for (group, stage), arrays in zip(self.pending, carried):
            if group in self.slots:
                self.slots[group] = arrays
                continue
            st = self.groups[group]
            tag = lambda what, n: what + "_" + n
            if stage == "pair":
                st["got"] = arrays
                st["partial"] = [_pair_add(tag("pair_add", n), self.where, g, r)
                                 for n, g, r in zip(st["names"], st["g4"], arrays)]
            elif stage == "chips":
                st["halves"] = [_final_add(tag("final_add", n), self.where, g, r, a)
                                for n, g, r, a in zip(st["names"], st["g4"], st["got"], arrays)]
            else:
                self.reduced.update(zip(st["names"], arrays))


SMALL_ROWS = 32
N_DEV = 8


def _all_reduce_small(pack):
    def body(p_ref, o_ref, buf, send, recv):
        x, y, c = _place()
        buf[4 * x + 2 * y + c] = p_ref[...]
        copies, waits = [], []
        for k in range(1, N_DEV):
            px = 1 - x if k & 4 else x
            py = 1 - y if k & 2 else y
            pc = 1 - c if k & 1 else c
            copies.append(pltpu.make_async_remote_copy(
                src_ref=p_ref, dst_ref=buf.at[4 * x + 2 * y + c], send_sem=send.at[k - 1],
                recv_sem=recv.at[k - 1], device_id=(px, py, pc), device_id_type=MESH))
            waits.append(pltpu.make_async_remote_copy(
                src_ref=p_ref, dst_ref=buf.at[4 * px + 2 * py + pc], send_sem=send.at[k - 1],
                recv_sem=recv.at[k - 1], device_id=(px, py, pc), device_id_type=MESH))
        for cp in copies:
            cp.start()
        for cp in waits:
            cp.wait_recv()
        acc = buf[0]
        for d in range(1, N_DEV):
            acc = acc + buf[d]
        o_ref[...] = acc
        for cp in copies:
            cp.wait_send()

    return pl.pallas_call(
        body, name="all_reduce_small",
        out_shape=jax.ShapeDtypeStruct(pack.shape, F32),
        scratch_shapes=[pltpu.VMEM((N_DEV,) + pack.shape, F32),
                        pltpu.SemaphoreType.DMA((N_DEV - 1,)), pltpu.SemaphoreType.DMA((N_DEV - 1,))],
    )(pack)


def _adamw(name, w, g, m, v):
    c1 = 1.0 - ADAM_B1 ** ADAM_STEP
    c2 = 1.0 - ADAM_B2 ** ADAM_STEP

    def fn(t, f):
        wv, gv, mv, vv = t
        m2 = ADAM_B1 * mv + (1.0 - ADAM_B1) * gv
        v2 = ADAM_B2 * vv + (1.0 - ADAM_B2) * (gv * gv)
        delta = -ADAM_LR * ((m2 / c1) / (jnp.sqrt(v2 / c2) + ADAM_EPS) + ADAM_WD * wv)
        return [delta, m2, v2], []

    cols = w.shape[1]
    return _ew(name, fn, [w, g, m, v], [], [(cols, F32)] * 3, ts=min(w.shape[0], 256))


def _pack_small(norm1_g, norm2_g, conv_b, b_gate, conv_w, q_norm_g, k_norm_g, rel_bias):
    pack = jnp.zeros((SMALL_ROWS, D_MODEL), F32)
    for r0, v in ((0, norm1_g), (1, norm2_g), (2, conv_b), (3, b_gate.reshape(2, D_MODEL)), (5, conv_w),
                  (8, q_norm_g), (9, k_norm_g), (10, rel_bias)):
        v = v.reshape(-1, v.shape[-1]).astype(F32)
        pack = pack.at[r0:r0 + v.shape[0], :v.shape[1]].set(v)
    return pack


def _unpack_small(pack, conv_cols):
    return dict(norm1_g=pack[0], norm2_g=pack[1], conv_b=pack[2], b_gate=pack[3:5].reshape(2 * D_MODEL),
                conv_w=pack[5:8, :conv_cols], q_norm_g=pack[8, :HEAD_DIM], k_norm_g=pack[9, :HEAD_DIM],
                rel_bias=pack[10:10 + N_HEADS, :N_REL])


BIG = ["w_in", "w_attn_proj", "w_conv_proj", "w_gate", "w_out", "w_up", "w_down"]
LATE = ["w_attn_proj", "w_conv_proj", "w_out", "w_up", "w_down"]
WEIGHTS = ["norm1_g", "w_in", "q_norm_g", "k_norm_g", "rel_bias", "conv_w", "conv_b", "w_attn_proj",
           "w_conv_proj", "w_gate", "b_gate", "w_out", "norm2_g", "w_up", "w_down"]


def kernel(x, norm1_g, w_in, q_norm_g, k_norm_g, rel_bias, conv_w, conv_b, w_attn_proj, w_conv_proj, w_gate, b_gate, w_out, norm2_g, w_up, w_down, loss_target, m_norm1_g, m_w_in, m_q_norm_g, m_k_norm_g, m_rel_bias, m_conv_w, m_conv_b, m_w_attn_proj, m_w_conv_proj, m_w_gate, m_b_gate, m_w_out, m_norm2_g, m_w_up, m_w_down, v_norm1_g, v_w_in, v_q_norm_g, v_k_norm_g, v_rel_bias, v_conv_w, v_conv_b, v_w_attn_proj, v_w_conv_proj, v_w_gate, v_b_gate, v_w_out, v_norm2_g, v_w_up, v_w_down):
    given = dict(locals())
    w = {n: given[n] for n in WEIGHTS}
    m = {n: given["m_" + n] for n in WEIGHTS}
    v = {n: given["v_" + n] for n in WEIGHTS}
    s_len = x.shape[1]
    shard = 2 * lax.axis_index("x") + lax.axis_index("y")
    where = jnp.stack([shard, lax.axis_index("c")]).astype(jnp.int32)
    conv_cols = conv_w.shape[1]

    small_in = lax.dynamic_update_slice(jnp.zeros((N_SHARD, 16, conv_cols), F32), conv_w[None], (shard, 0, 0))
    slot = {n: _cast_into_slot("cast_" + n, where, w[n]) for n in BIG}
    comm = _Exchange(where, [slot["w_in"], slot["w_gate"], small_in], [slot[n] for n in LATE])

    loss_part, grad_x, _, small = _local_grads(
        x.reshape(s_len, D_MODEL), loss_target.reshape(s_len, D_MODEL), norm1_g, q_norm_g, k_norm_g,
        rel_bias, conv_b, b_gate, norm2_g, comm)
    grad = dict(comm.reduced)

    loss_local = _finish_loss(loss_part)
    pack = _pack_small(small["norm1_g"], small["norm2_g"], small["conv_wb"][3], jnp.concatenate(small["b_gate"], axis=1),
                       small["conv_wb"][0:3], small["q_norm_g"], small["k_norm_g"], small["rel_bias"])
    total = _all_reduce_small(pack)
    g_small = _unpack_small(total, D_MODEL)
    g_small["conv_w"] = lax.dynamic_slice(g_small["conv_w"], (0, shard * conv_cols), (3, conv_cols))
    grad.update(g_small)

    delta, new_m, new_v = {}, {}, {}
    for n in BIG:
        delta[n], new_m[n], new_v[n] = _adamw("adamw_" + n, w[n], grad[n], m[n], v[n])
    small_names = [n for n in WEIGHTS if n not in BIG]
    packs = [_pack_small(**{n: src[n] for n in small_names}) for src in (w, grad, m, v)]
    for out, packed in zip((delta, new_m, new_v), _adamw("adamw_small", *packs)):
        out.update({n: a.reshape(w[n].shape) for n, a in _unpack_small(packed, conv_cols).items()})

    loss = lax.psum(loss_local[0, 0], ("x", "y", "c"))
    outs = [loss, grad_x.reshape(x.shape)]
    for group in (grad, delta, new_m, new_v):
        outs += [group[n].reshape(w[n].shape) for n in WEIGHTS]
    return tuple(outs)
```

```python
import functools

import jax
import jax.numpy as jnp
from jax import lax
from jax.experimental import pallas as pl
from jax.experimental.pallas import tpu as pltpu

F32 = jnp.float32
BF16 = jnp.bfloat16

D_MODEL = 1024
N_HEADS = 16
HEAD_DIM = 64
CHUNK = 64
N_PREV_CHUNKS = 8
MAX_REL = 256
D_FF = 4096
N_REL = 2 * MAX_REL + 1
REL_PAD = 640
EPS = 1e-6
NEG_INF = -1e30
QK_SCALE = HEAD_DIM ** -0.5

SUPER = 4 * CHUNK
BAND = SUPER + N_PREV_CHUNKS * CHUNK
SKEW_W = 1024
N_SHARD = 4
LANE = 128
VMEM_LIMIT = 48 * 1024 * 1024

ADAM_LR = 0.001
ADAM_B1 = 0.9
ADAM_B2 = 0.999
ADAM_EPS = 1e-08
ADAM_WD = 0.01
ADAM_STEP = 10

MESH = pl.DeviceIdType.MESH
NN = (((1,), (0,)), ((), ()))
NT = (((1,), (1,)), ((), ()))
TN = (((0,), (0,)), ((), ()))


def _params(*sem):
    return pltpu.CompilerParams(dimension_semantics=sem or None, vmem_limit_bytes=VMEM_LIMIT)


HBM_SPEC = pl.BlockSpec(memory_space=pl.ANY)


class _Rider:
    def __init__(self, sources, arrays, n_sem, start, finish):
        self.sources, self.arrays, self.n_sem, self.start, self.finish = sources, arrays, n_sem, start, finish


def _carry(riders, body, *, name, out_shape, grid=(), in_specs=None, out_specs=None, scratch_shapes=(),
           semantics=(), input_output_aliases=None):
    aliases = dict(input_output_aliases or {})
    if not riders:
        kw = {} if in_specs is None else dict(in_specs=in_specs, out_specs=out_specs)
        return pl.pallas_call(body, name=name, grid=grid, out_shape=out_shape, scratch_shapes=scratch_shapes,
                              input_output_aliases=aliases, compiler_params=_params(*semantics), **kw)
    single = not isinstance(out_shape, (list, tuple))
    shapes = [out_shape] if single else list(out_shape)
    n_out, n_scr = len(shapes), len(scratch_shapes)
    srcs = [a for r in riders for a in r.sources]
    arrs = [a for r in riders for a in r.arrays]
    vmem = pl.BlockSpec(memory_space=pltpu.VMEM)

    def run(*args):
        n_in = len(args)

        def wrapped(*refs):
            pos = n_in
            src_refs = refs[pos:pos + len(srcs)]
            pos += len(srcs) + len(arrs)
            outs = refs[pos:pos + n_out]
            pos += n_out
            arr_refs = refs[pos:pos + len(arrs)]
            pos += len(arrs)
            scratch = refs[pos:pos + n_scr]
            sems = refs[pos + n_scr:]
            first, last = True, True
            for d, size in enumerate(grid):
                first = jnp.logical_and(first, pl.program_id(d) == 0)
                last = jnp.logical_and(last, pl.program_id(d) == size - 1)

            def each(method):
                s0 = a0 = 0
                for k, r in enumerate(riders):
                    getattr(r, method)(src_refs[s0:s0 + len(r.sources)], arr_refs[a0:a0 + len(r.arrays)],
                                       sems[2 * k], sems[2 * k + 1])
                    s0, a0 = s0 + len(r.sources), a0 + len(r.arrays)

            pl.when(first)(lambda: each("start"))
            body(*refs[:n_in], *outs, *scratch)
            pl.when(last)(lambda: each("finish"))

        ins = [vmem] * n_in if in_specs is None else list(in_specs)
        if out_specs is None:
            o_specs = [vmem] * n_out
        else:
            o_specs = [out_specs] if single else list(out_specs)
        for k in range(len(arrs)):
            aliases[n_in + len(srcs) + k] = n_out + k
        res = pl.pallas_call(
            wrapped, name=name, grid=grid,
            in_specs=ins + [HBM_SPEC] * (len(srcs) + len(arrs)),
            out_specs=o_specs + [HBM_SPEC] * len(arrs),
            out_shape=shapes + [jax.ShapeDtypeStruct(a.shape, a.dtype) for a in arrs],
            scratch_shapes=list(scratch_shapes) + [pltpu.SemaphoreType.DMA((r.n_sem,)) for r in riders for _ in range(2)],
            input_output_aliases=aliases,
            compiler_params=_params(*["arbitrary"] * len(grid)),
        )(*args, *srcs, *arrs)
        core, rest = res[:n_out], list(res[n_out:])
        carried, a0 = [], 0
        for r in riders:
            carried.append(rest[a0:a0 + len(r.arrays)])
            a0 += len(r.arrays)
        return (core[0] if single else core), carried

    return run


def _mm(name, dims, a, a_spec, b, b_spec, grid, tile, outs, epilogue=None, extras=(), riders=()):
    nk, ne, no = grid[2], len(extras), len(outs)

    def body(a_ref, b_ref, *refs):
        e_refs, o_refs = refs[:ne], refs[ne:ne + no]
        part = lax.dot_general(a_ref[...], b_ref[...], dims, preferred_element_type=F32)

        def finish(acc):
            if epilogue is None:
                o_refs[0][...] = acc.astype(o_refs[0].dtype)
            else:
                epilogue(acc, [r[...] for r in e_refs], o_refs)

        if nk == 1:
            finish(part)
        else:
            acc_ref = refs[ne + no]
            k = pl.program_id(2)

            @pl.when(k == 0)
            def _():
                acc_ref[...] = part

            @pl.when(k > 0)
            def _():
                acc_ref[...] += part

            @pl.when(k == nk - 1)
            def _():
                finish(acc_ref[...])

    res = _carry(
        riders, body, name=name, grid=grid,
        in_specs=[a_spec, b_spec] + [s for _, s in extras],
        out_specs=[s for _, s in outs],
        out_shape=[s for s, _ in outs],
        scratch_shapes=[pltpu.VMEM(tile, F32)] if nk > 1 else [],
        semantics=("parallel", "parallel", "arbitrary"),
    )(a, b, *[e for e, _ in extras])
    res, carried = res if riders else (res, None)
    res = res[0] if no == 1 else res
    return (res, carried) if riders else res


def _tile_spec(tm, tn, col0=0):
    return pl.BlockSpec((tm, tn), lambda i, j, k: (i, j + col0))


def _out2d(m, n, dtype, tm, tn):
    return (jax.ShapeDtypeStruct((m, n), dtype), _tile_spec(tm, tn))


def _mm_fwd(name, a, w, tm, tn, tk, outs=None, epilogue=None, extras=(), col0=0, ncols=None, riders=()):
    m, kdim = a.shape
    if w.ndim == 3:
        per = w.shape[2] // tn
        n = ncols or N_SHARD * w.shape[2]
        w_spec = pl.BlockSpec((None, tk, tn), lambda i, j, k: ((j + col0) // per, k, (j + col0) % per))
    else:
        n = ncols or w.shape[1]
        w_spec = pl.BlockSpec((tk, tn), lambda i, j, k: (k, j + col0))
    if outs is None:
        outs = [_out2d(m, n, F32, tm, tn)]
    return _mm(name, NN, a, pl.BlockSpec((tm, tk), lambda i, j, k: (i, k)), w, w_spec,
               (m // tm, n // tn, kdim // tk), (tm, tn), outs, epilogue, extras, riders)


def _mm_bwd_x(name, g, g_spec, w, tm, tj, tc, m, n_contract, outs=None, epilogue=None, extras=(), riders=()):
    if w.ndim == 3:
        per = w.shape[2] // tc
        kdim = w.shape[1]
        w_spec = pl.BlockSpec((None, tj, tc), lambda i, j, n: (n // per, j, n % per))
    else:
        kdim = w.shape[0]
        w_spec = pl.BlockSpec((tj, tc), lambda i, j, n: (j, n))
    if outs is None:
        outs = [_out2d(m, kdim, F32, tm, tj)]
    return _mm(name, NT, g, g_spec, w, w_spec, (m // tm, kdim // tj, n_contract // tc),
               (tm, tj), outs, epilogue, extras, riders)


def _mm_bwd_w(name, a, g, g_spec, n, tk, tn, tm, sharded, riders=()):
    m, kdim = a.shape
    if sharded:
        per = (n // N_SHARD) // tn
        out = (jax.ShapeDtypeStruct((N_SHARD, kdim, n // N_SHARD), F32),
               pl.BlockSpec((None, tk, tn), lambda i, j, mm: (j // per, i, j % per)))
    else:
        out = (jax.ShapeDtypeStruct((kdim, n), F32), pl.BlockSpec((tk, tn), lambda i, j, mm: (i, j)))
    return _mm(name, TN, a, pl.BlockSpec((tm, tk), lambda i, j, mm: (mm, i)), g, g_spec,
               (kdim // tk, n // tn, m // tm), (tk, tn), [out], riders=riders)


def _ew(name, fn, tiles, fulls, outs, sums=(), ts=512, riders=()):
    tiles = [t if isinstance(t, tuple) else (t, pl.BlockSpec((ts, t.shape[1]), lambda i: (i, 0)))
             for t in tiles]
    s_rows = tiles[0][0].shape[-2]
    nt, nf, no = len(tiles), len(fulls), len(outs)

    def body(*refs):
        t_vals = [r[...] for r in refs[:nt]]
        f_vals = [r[...] for r in refs[nt:nt + nf]]
        o_refs, s_refs = refs[nt + nf:nt + nf + no], refs[nt + nf + no:]
        o_vals, s_vals = fn(t_vals, f_vals)
        for r, v in zip(o_refs, o_vals):
            r[...] = v.astype(r.dtype)
        for r, v in zip(s_refs, s_vals):
            part = jnp.sum(v, axis=0, keepdims=True)

            @pl.when(pl.program_id(0) == 0)
            def _():
                r[...] = part

            @pl.when(pl.program_id(0) > 0)
            def _():
                r[...] += part

    full_specs = [pl.BlockSpec(f.shape, lambda i, nd=f.ndim: (0,) * nd) for f in fulls]
    return _carry(
        riders, body, name=name, grid=(s_rows // ts,),
        in_specs=[s for _, s in tiles] + full_specs,
        out_specs=[pl.BlockSpec((ts, c), lambda i: (i, 0)) for c, _ in outs]
        + [pl.BlockSpec((1, c), lambda i: (0, 0)) for c in sums],
        out_shape=[jax.ShapeDtypeStruct((s_rows, c), dt) for c, dt in outs]
        + [jax.ShapeDtypeStruct((1, c), F32) for c in sums],
        semantics=("arbitrary",),
    )(*[t for t, _ in tiles], *fulls)


def _rms_fwd(name, x, g, riders=()):
    def fn(t, f):
        xv = t[0]
        r = lax.rsqrt(jnp.mean(xv * xv, axis=-1, keepdims=True) + EPS)
        return [xv * r * f[0]], []
    out = _ew(name, fn, [x], [g], [(x.shape[1], BF16)], riders=riders)
    return (out[0][0], out[1]) if riders else out[0]


def _rms_bwd(name, dh, x, g, dres, riders=()):
    def fn(t, f):
        dhv, xv, dr = t
        r = lax.rsqrt(jnp.mean(xv * xv, axis=-1, keepdims=True) + EPS)
        xhat = xv * r
        u = dhv * f[0]
        dx = r * (u - xhat * jnp.mean(u * xhat, axis=-1, keepdims=True)) + dr
        return [dx, dx], [dhv * xhat]
    c = x.shape[1]
    return _ew(name, fn, [dh, x, dres], [g], [(c, F32), (c, BF16)], sums=[c], riders=riders)


def _split3(x):
    x1 = x.astype(BF16)
    r1 = x - x1.astype(F32)
    x2 = r1.astype(BF16)
    x3 = (r1 - x2.astype(F32)).astype(BF16)
    return x1, x2, x3


def _rel_class(cp):
    far = (cp < MAX_REL) | (cp > BAND)
    return jnp.where(far, 2 * MAX_REL, BAND - cp)


def _skew_rows(x, sign):
    row = lax.broadcasted_iota(jnp.int32, x.shape, 0)
    for b in range(SUPER.bit_length() - 1):
        shift = (1 << b) if sign > 0 else SKEW_W - (1 << b)
        x = jnp.where((row >> b) & 1 == 1, pltpu.roll(x, shift, 1), x)
    return x


def _bias_expand(rel_bias, riders=()):
    rel = jnp.pad(rel_bias, ((0, 0), (0, REL_PAD - N_REL))).reshape(N_HEADS, 1, REL_PAD)

    def body(rel_ref, o_ref):
        cls = lax.broadcasted_iota(jnp.int32, (REL_PAD, SKEW_W), 0)
        cp = lax.broadcasted_iota(jnp.int32, (REL_PAD, SKEW_W), 1)
        onehot = (cls == _rel_class(cp)).astype(BF16)
        rel8 = jnp.broadcast_to(rel_ref[...], (8, REL_PAD))
        trow = sum(jnp.dot(p, onehot, preferred_element_type=F32) for p in _split3(rel8))[0:1]
        full = _skew_rows(jnp.broadcast_to(trow, (SUPER, SKEW_W)), +1)[:, :BAND]
        qc = lax.broadcasted_iota(jnp.int32, (SUPER, BAND), 0) // CHUNK
        kc = lax.broadcasted_iota(jnp.int32, (SUPER, BAND), 1) // CHUNK
        on_band = (kc >= qc) & (kc <= qc + N_PREV_CHUNKS)
        o_ref[...] = jnp.where(on_band, full, NEG_INF).T

    return _carry(
        riders, body, name="bias_expand", grid=(N_HEADS,),
        in_specs=[pl.BlockSpec((None, 1, REL_PAD), lambda h: (h, 0, 0))],
        out_specs=pl.BlockSpec((None, BAND, SUPER), lambda h: (h, 0, 0)),
        out_shape=jax.ShapeDtypeStruct((N_HEADS, BAND, SUPER), F32),
        semantics=("arbitrary",),
    )(rel)


def _bias_reduce(dbias):
    def body(d_ref, o_ref):
        x = jnp.concatenate([d_ref[...].T, jnp.zeros((SUPER, SKEW_W - BAND), F32)], axis=1)
        diag = jnp.sum(_skew_rows(x, -1), axis=0, keepdims=True)
        cp = lax.broadcasted_iota(jnp.int32, (SKEW_W, REL_PAD), 0)
        cls = lax.broadcasted_iota(jnp.int32, (SKEW_W, REL_PAD), 1)
        onehot = (cls == _rel_class(cp)).astype(BF16)
        diag8 = jnp.broadcast_to(diag, (8, SKEW_W))
        o_ref[...] = sum(jnp.dot(p, onehot, preferred_element_type=F32) for p in _split3(diag8))[0:1]

    out = pl.pallas_call(
        body, name="bias_reduce", grid=(N_HEADS,),
        in_specs=[pl.BlockSpec((None, BAND, SUPER), lambda h: (h, 0, 0))],
        out_specs=pl.BlockSpec((None, 1, REL_PAD), lambda h: (h, 0, 0)),
        out_shape=jax.ShapeDtypeStruct((N_HEADS, 1, REL_PAD), F32),
        compiler_params=_params("arbitrary"),
    )(dbias)
    return out.reshape(N_HEADS, REL_PAD)[:, :N_REL]


HEADS_PER_STEP = 4
HEAD_COLS = HEADS_PER_STEP * HEAD_DIM
N_HEAD_GROUPS = N_HEADS // HEADS_PER_STEP


def _unit(x):
    r = lax.rsqrt(jnp.mean(x * x, axis=-1, keepdims=True) + EPS)
    return x * r, r


def _scores_t(qs, kn, bias_t, dead):
    s = jnp.concatenate([lax.dot_general(k, qs, NT, preferred_element_type=F32) for k in kn], axis=0) + bias_t
    return jnp.where(dead, NEG_INF, s)


def _band_specs(nb, col0, clamp_hi):
    def spec(d):
        def index(hg, i):
            blk = jnp.maximum(i - d, 0)
            if clamp_hi:
                blk = jnp.minimum(blk, nb - 1)
            return (blk, col0 + hg)
        return pl.BlockSpec((SUPER, HEAD_COLS), index)
    return [spec(2), spec(1), spec(0)]


def _head_sums(y):
    cols = y.shape[1]
    same_head = (lax.broadcasted_iota(jnp.int32, (cols, cols), 0) // HEAD_DIM
                 == lax.broadcasted_iota(jnp.int32, (cols, cols), 1) // HEAD_DIM).astype(BF16)
    hi = y.astype(BF16)
    lo = (y - hi.astype(F32)).astype(BF16)
    return (jnp.dot(hi, same_head, preferred_element_type=F32)
            + jnp.dot(lo, same_head, preferred_element_type=F32))


def _head_unit(x):
    r = lax.rsqrt(_head_sums(x * x) * (1.0 / HEAD_DIM) + EPS)
    return x * r, r


def _head(hh):
    return slice(HEAD_DIM * hh, HEAD_DIM * (hh + 1))


def _key_block(j):
    return slice(SUPER * j, SUPER * (j + 1))


def _dead_keys(i):
    key = lax.broadcasted_iota(jnp.int32, (BAND, SUPER), 0)
    return key < (2 - i) * SUPER


LSE_ROWS = 8


def _attn_fwd(qkn, v, bias, riders=()):
    s_len = qkn.shape[0]
    nb = s_len // SUPER

    def body(q_ref, k0, k1, k2, v0, v1, v2, b_ref, o_ref, lse_ref):
        dead = _dead_keys(pl.program_id(1))
        outs = []

        def probabilities(hh):
            sl = _head(hh)
            s = _scores_t(q_ref[:, sl], [k0[:, sl], k1[:, sl], k2[:, sl]], b_ref[hh], dead)
            m = jnp.max(s, axis=0, keepdims=True)
            e = jnp.exp(s - m)
            l = jnp.sum(e, axis=0, keepdims=True)
            lse_ref[hh:hh + 1, :] = m + jnp.log(l)
            return (e * (1.0 / l)).astype(BF16), sl

        def weighted_values(p, sl):
            outs.append(sum(lax.dot_general(p[_key_block(j), :], vj[:, sl], TN, preferred_element_type=F32)
                            for j, vj in enumerate((v0, v1, v2))))

        ready = probabilities(0)
        for hh in range(1, HEADS_PER_STEP):
            following = probabilities(hh)
            weighted_values(*ready)
            ready = following
        weighted_values(*ready)
        o_ref[...] = jnp.concatenate(outs, axis=1).astype(o_ref.dtype)

    return _carry(
        riders, body, name="attn_fwd", grid=(N_HEAD_GROUPS, nb),
        in_specs=[pl.BlockSpec((SUPER, HEAD_COLS), lambda hg, i: (i, hg))]
        + _band_specs(nb, N_HEAD_GROUPS, False) + _band_specs(nb, 0, False)
        + [pl.BlockSpec((HEADS_PER_STEP, BAND, SUPER), lambda hg, i: (hg, 0, 0))],
        out_specs=[pl.BlockSpec((SUPER, HEAD_COLS), lambda hg, i: (i, hg)),
                   pl.BlockSpec((None, LSE_ROWS, SUPER), lambda hg, i: (hg, 0, i))],
        out_shape=[jax.ShapeDtypeStruct((s_len, D_MODEL), BF16),
                   jax.ShapeDtypeStruct((N_HEAD_GROUPS, LSE_ROWS, s_len), F32)],
        semantics=("parallel", "arbitrary"),
    )(qkn, qkn, qkn, qkn, v, v, v, bias)


def _attn_bwd(qkn, v, d_out, bias, lse, riders=()):
    s_len = qkn.shape[0]
    nb = s_len // SUPER

    def body(q_ref, k0, k1, k2, v0, v1, v2, do_ref, b_ref, lse_ref, dp_ref, db_ref, aq_ref, ak_ref, av_ref):
        i = pl.program_id(1)

        @pl.when(i == 0)
        def _():
            aq_ref[...] = jnp.zeros_like(aq_ref)
            ak_ref[...] = jnp.zeros_like(ak_ref)
            av_ref[...] = jnp.zeros_like(av_ref)
            db_ref[...] = jnp.zeros_like(db_ref)

        @pl.when(i < nb)
        def _():
            dead = _dead_keys(i)
            dq, dk, dv = [], [[], [], []], [[], [], []]

            def softmax_grad(hh):
                sl = _head(hh)
                qs, do = q_ref[:, sl], do_ref[:, sl]
                kn = [k0[:, sl], k1[:, sl], k2[:, sl]]
                vv = [v0[:, sl], v1[:, sl], v2[:, sl]]
                p = jnp.exp(_scores_t(qs, kn, b_ref[hh], dead) - lse_ref[hh:hh + 1, :])
                dp = jnp.concatenate([lax.dot_general(vj, do, NT, preferred_element_type=F32) for vj in vv], axis=0)
                ds = p * (dp - jnp.sum(p * dp, axis=0, keepdims=True))
                db_ref[hh] += ds
                return p.astype(BF16), ds.astype(BF16), qs, do, kn

            def operand_grads(pb, dsb, qs, do, kn):
                dq.append(sum(lax.dot_general(dsb[_key_block(j), :], kn[j], TN, preferred_element_type=F32)
                              for j in range(3)))
                for j in range(3):
                    dv[j].append(jnp.dot(pb[_key_block(j), :], do, preferred_element_type=F32))
                    dk[j].append(jnp.dot(dsb[_key_block(j), :], qs, preferred_element_type=F32))

            ready = softmax_grad(0)
            for hh in range(1, HEADS_PER_STEP):
                following = softmax_grad(hh)
                operand_grads(*ready)
                ready = following
            operand_grads(*ready)
            aq_ref[i % 3] = jnp.concatenate(dq, axis=1)
            for j in range(3):
                slot = (i + 1 + j) % 3
                if j < 2:
                    ak_ref[slot] += jnp.concatenate(dk[j], axis=1)
                    av_ref[slot] += jnp.concatenate(dv[j], axis=1)
                else:
                    ak_ref[slot] = jnp.concatenate(dk[j], axis=1)
                    av_ref[slot] = jnp.concatenate(dv[j], axis=1)

        slot = (i + 1) % 3
        dp_ref[0] = aq_ref[slot].astype(dp_ref.dtype)
        dp_ref[1] = ak_ref[slot].astype(dp_ref.dtype)
        dp_ref[2] = av_ref[slot].astype(dp_ref.dtype)

    def qrow(hg, i):
        return (jnp.minimum(i, nb - 1), hg)

    return _carry(
        riders, body, name="attn_bwd", grid=(N_HEAD_GROUPS, nb + 2),
        in_specs=[pl.BlockSpec((SUPER, HEAD_COLS), qrow)]
        + _band_specs(nb, N_HEAD_GROUPS, True) + _band_specs(nb, 0, True)
        + [pl.BlockSpec((SUPER, HEAD_COLS), qrow),
           pl.BlockSpec((HEADS_PER_STEP, BAND, SUPER), lambda hg, i: (hg, 0, 0)),
           pl.BlockSpec((None, LSE_ROWS, SUPER), lambda hg, i: (hg, 0, jnp.minimum(i, nb - 1)))],
        out_specs=[pl.BlockSpec((3, SUPER, HEAD_COLS), lambda hg, i: (0, jnp.maximum(i - 2, 0), hg)),
                   pl.BlockSpec((HEADS_PER_STEP, BAND, SUPER), lambda hg, i: (hg, 0, 0))],
        out_shape=[jax.ShapeDtypeStruct((6, s_len, D_MODEL), BF16),
                   jax.ShapeDtypeStruct((N_HEADS, BAND, SUPER), F32)],
        scratch_shapes=[pltpu.VMEM((3, SUPER, HEAD_COLS), F32)] * 3,
        semantics=("parallel", "arbitrary"),
    )(qkn, qkn, qkn, qkn, v, v, v, d_out, bias, lse)


def _qk_norm_bwd(dproj6, qk_raw, gq, gk):
    s_len = qk_raw.shape[0]
    ts = 256

    nsteps = s_len // ts
    half = D_MODEL // 2

    def body(d_ref, raw_ref, gq_ref, gk_ref, o_ref, dgq_ref, dgk_ref, acc_ref):
        step = pl.program_id(0)

        @pl.when(step == 0)
        def _():
            acc_ref[...] = jnp.zeros_like(acc_ref)

        for piece, (g_ref, scale) in enumerate(((gq_ref, QK_SCALE), (gk_ref, 1.0))):
            for c0 in (0, half):
                xhat, r = _head_unit(raw_ref[:, piece * D_MODEL + c0:piece * D_MODEL + c0 + half].astype(F32))
                dn = d_ref[piece, :, c0:c0 + half].astype(F32) * scale
                u = dn * g_ref[...]
                dx = r * (u - xhat * (_head_sums(u * xhat) * (1.0 / HEAD_DIM)))
                o_ref[piece, :, c0:c0 + half] = dx.astype(o_ref.dtype)
                acc_ref[piece:piece + 1, c0:c0 + half] += jnp.sum(dn * xhat, axis=0, keepdims=True)

        @pl.when(step == nsteps - 1)
        def _():
            lane = lax.broadcasted_iota(jnp.int32, (D_MODEL, LANE), 0) % HEAD_DIM
            fold = (lane == lax.broadcasted_iota(jnp.int32, (D_MODEL, LANE), 1)).astype(BF16)
            tot = sum(jnp.dot(p, fold, preferred_element_type=F32) for p in _split3(acc_ref[...]))
            dgq_ref[...] = tot[0:1, :HEAD_DIM]
            dgk_ref[...] = tot[1:2, :HEAD_DIM]

    gain = pl.BlockSpec((1, half), lambda i: (0, 0))
    small = pl.BlockSpec((1, HEAD_DIM), lambda i: (0, 0))
    per_head = lambda g: jnp.tile(g, (1, half // HEAD_DIM))
    return pl.pallas_call(
        body, name="qk_norm_bwd", grid=(nsteps,),
        in_specs=[pl.BlockSpec((2, ts, D_MODEL), lambda i: (0, i, 0)),
                  pl.BlockSpec((ts, 2 * D_MODEL), lambda i: (i, 0)), gain, gain],
        out_specs=[pl.BlockSpec((2, ts, D_MODEL), lambda i: (0, i, 0)), small, small],
        out_shape=[jax.ShapeDtypeStruct(dproj6.shape, dproj6.dtype),
                   jax.ShapeDtypeStruct((1, HEAD_DIM), F32), jax.ShapeDtypeStruct((1, HEAD_DIM), F32)],
        scratch_shapes=[pltpu.VMEM((8, D_MODEL), F32)],
        input_output_aliases={0: 0},
        compiler_params=_params("arbitrary"),
    )(dproj6, qk_raw, per_head(gq), per_head(gk))


CONV_ROWS = 512
HALO = 16


def _rows_with_halo(ref, r0, n, front, s_len):
    zeros = jnp.zeros((HALO, ref.shape[1]), F32)
    if front:
        return (jnp.concatenate([zeros, ref[0:n, :].astype(F32)], axis=0) if r0 == 0
                else ref[r0 - HALO:r0 + n, :].astype(F32))
    return (jnp.concatenate([ref[r0:r0 + n, :].astype(F32), zeros], axis=0) if r0 + n == s_len
            else ref[r0:r0 + n + HALO, :].astype(F32))


def _earlier(ext, k):
    return pltpu.roll(ext, k, 0)[HALO:]


def _later(ext, k):
    n = ext.shape[0]
    return pltpu.roll(ext, n - k, 0)[:n - HALO]


def _conv_cols(col0):
    return lambda s_len: pl.BlockSpec((s_len, LANE), lambda j: (0, col0 + j))


def _conv_fwd(proj, conv_w, conv_b, riders=()):
    s_len = proj.shape[0]

    def body(bg_ref, cg_ref, xc_ref, w_ref, b_ref, o_ref):
        w = [w_ref[t:t + 1, :] for t in range(3)]
        for r0 in range(0, s_len, CONV_ROWS):
            u = _rows_with_halo(cg_ref, r0, CONV_ROWS, True, s_len) * \
                _rows_with_halo(xc_ref, r0, CONV_ROWS, True, s_len)
            conv = b_ref[...] + w[0] * _earlier(u, 2) + w[1] * _earlier(u, 1) + w[2] * u[HALO:]
            o_ref[r0:r0 + CONV_ROWS, :] = (bg_ref[r0:r0 + CONV_ROWS, :].astype(F32) * conv).astype(o_ref.dtype)

    return _carry(
        riders, body, name="conv_fwd", grid=(D_MODEL // LANE,),
        in_specs=[_conv_cols(0)(s_len), _conv_cols(8)(s_len), _conv_cols(16)(s_len),
                  pl.BlockSpec((3, LANE), lambda j: (0, j)), pl.BlockSpec((1, LANE), lambda j: (0, j))],
        out_specs=pl.BlockSpec((s_len, LANE), lambda j: (0, j)),
        out_shape=jax.ShapeDtypeStruct((s_len, D_MODEL), BF16),
        semantics=("parallel",),
    )(proj, proj, proj, conv_w, conv_b)


def _conv_bwd(dproj6, dy, proj, conv_w, conv_b, riders=()):
    s_len = proj.shape[0]

    def body(dy_ref, bg_ref, cg_ref, xc_ref, w_ref, b_ref, _, dp_ref, dw_ref):
        w = [w_ref[t:t + 1, :] for t in range(3)]
        acc = [jnp.zeros((1, LANE), F32) for _ in range(4)]
        for r0 in range(0, s_len, CONV_ROWS):
            rows = slice(r0, r0 + CONV_ROWS)
            u = _rows_with_halo(cg_ref, r0, CONV_ROWS, True, s_len) * \
                _rows_with_halo(xc_ref, r0, CONV_ROWS, True, s_len)
            u2, u1, u0 = _earlier(u, 2), _earlier(u, 1), u[HALO:]
            conv = b_ref[...] + w[0] * u2 + w[1] * u1 + w[2] * u0
            dp_ref[0, rows, :] = (dy_ref[rows, :].astype(F32) * conv).astype(dp_ref.dtype)
            dconv_ext = _rows_with_halo(dy_ref, r0, CONV_ROWS, False, s_len) * \
                _rows_with_halo(bg_ref, r0, CONV_ROWS, False, s_len)
            dconv = dconv_ext[:CONV_ROWS]
            for t, term in enumerate([dconv * u2, dconv * u1, dconv * u0, dconv]):
                acc[t] = acc[t] + jnp.sum(term, axis=0, keepdims=True)
            du = w[2] * dconv + w[1] * _later(dconv_ext, 1) + w[0] * _later(dconv_ext, 2)
            dp_ref[1, rows, :] = (du * xc_ref[rows, :].astype(F32)).astype(dp_ref.dtype)
            dp_ref[2, rows, :] = (du * cg_ref[rows, :].astype(F32)).astype(dp_ref.dtype)
        dw_ref[...] = jnp.zeros_like(dw_ref)
        for t in range(4):
            dw_ref[t:t + 1, :] = acc[t]

    return _carry(
        riders, body, name="conv_bwd", grid=(D_MODEL // LANE,),
        in_specs=[pl.BlockSpec((s_len, LANE), lambda j: (0, j)),
                  _conv_cols(0)(s_len), _conv_cols(8)(s_len), _conv_cols(16)(s_len),
                  pl.BlockSpec((3, LANE), lambda j: (0, j)), pl.BlockSpec((1, LANE), lambda j: (0, j)),
                  pl.BlockSpec(memory_space=pl.ANY)],
        out_specs=[pl.BlockSpec((3, s_len, LANE), lambda j: (1, 0, j)),
                   pl.BlockSpec((8, LANE), lambda j: (0, j))],
        out_shape=[jax.ShapeDtypeStruct(dproj6.shape, dproj6.dtype),
                   jax.ShapeDtypeStruct((8, D_MODEL), F32)],
        input_output_aliases={6: 0},
        semantics=("parallel",),
    )(dy, proj, proj, proj, conv_w, conv_b, dproj6)


class _NoComm:
    def __init__(self, early, late):
        self.early, self.late = early, late

    def riders(self, at):
        return ()

    def done(self, at, carried):
        pass

    def early_weights(self):
        return self.early

    def late_weights(self):
        return self.late

    def grads_ready(self, group, grads):
        pass


def _d_input(name, pieces, w3, extra=None, riders=()):
    n_piece, s_len, width = pieces.shape
    _, kdim, per_shard = w3.shape
    tm, chunk = 256, 512

    def body(p_ref, w_ref, *refs):
        o_ref = refs[-1]
        acc = refs[0][...] if extra is not None else jnp.zeros(o_ref.shape, F32)
        for c0 in range(0, n_piece * width, chunk):
            acc = acc + lax.dot_general(p_ref[c0 // width, :, c0 % width:c0 % width + chunk],
                                        w_ref[c0 // per_shard, :, c0 % per_shard:c0 % per_shard + chunk],
                                        NT, preferred_element_type=F32)
        o_ref[...] = acc

    tile = pl.BlockSpec((tm, kdim), lambda i: (i, 0))
    return _carry(
        riders, body, name=name, grid=(s_len // tm,),
        in_specs=[pl.BlockSpec((n_piece, tm, width), lambda i: (0, i, 0)),
                  pl.BlockSpec(w3.shape, lambda i: (0, 0, 0))] + ([tile] if extra is not None else []),
        out_specs=tile, out_shape=jax.ShapeDtypeStruct((s_len, kdim), F32),
        semantics=("parallel",),
    )(pieces, w3, *([extra] if extra is not None else []))


def _local_grads(x, target, norm1_g, q_norm_g, k_norm_g, rel_bias, conv_b, b_gate, norm2_g, comm):
    s_len = x.shape[0]
    tm = min(1024, s_len)
    row = lambda v: v.reshape(1, -1)

    def carrying(at, fn, *args, **kw):
        riders = comm.riders(at)
        out = fn(*args, riders=riders, **kw)
        if riders:
            out, carried = out
            comm.done(at, carried)
        return out

    bias = carrying("bias_expand", _bias_expand, rel_bias)
    h = carrying("norm1", _rms_fwd, "norm1", x, row(norm1_g))
    w_in3, w_gate3, conv_w = comm.early_weights()
    gq, gk = row(q_norm_g), row(k_norm_g)

    def qk_epi(acc, e, o):
        o[0][...] = acc.astype(BF16)
        gain = jnp.where(pl.program_id(1) < 2, e[0] * QK_SCALE, e[1])
        o[1][...] = (_head_unit(acc)[0] * gain).astype(BF16)
    small = pl.BlockSpec((1, 512), lambda i, j, k: (0, 0))
    per_head = lambda g: jnp.tile(g, (1, 512 // HEAD_DIM))
    qk_raw, qkn = _mm_fwd("proj_qk", h, w_in3, tm, 512, D_MODEL, ncols=2 * D_MODEL, epilogue=qk_epi,
                          outs=[_out2d(s_len, 2 * D_MODEL, BF16, tm, 512)] * 2,
                          extras=[(per_head(gq), small), (per_head(gk), small)])
    v = _mm_fwd("proj_v", h, w_in3, tm, 512, D_MODEL, col0=4, ncols=D_MODEL,
                outs=[_out2d(s_len, D_MODEL, BF16, tm, 512)])
    conv_in = _mm_fwd("proj_conv", h, w_in3, tm, 1536, D_MODEL, col0=2, ncols=3 * D_MODEL,
                      outs=[_out2d(s_len, 3 * D_MODEL, BF16, tm, 1536)])

    def gate_epi(acc, e, o):
        o[0][...] = jax.nn.sigmoid(acc + e[0]).astype(BF16)
    gates = _mm_fwd("gates", h, w_gate3, tm, 512, D_MODEL, epilogue=gate_epi,
                    outs=[_out2d(s_len, 2 * D_MODEL, BF16, tm, 512)],
                    extras=[(row(b_gate), pl.BlockSpec((1, 512), lambda i, j, k: (0, j)))])

    attn, lse = carrying("attn_fwd", _attn_fwd, qkn, v, bias)
    yconv = carrying("conv_fwd", _conv_fwd, conv_in, conv_w, row(conv_b))
    w_ap, w_cp, w_out, w_up3, w_down = comm.late_weights()
    tw = 1024
    ya = _mm_fwd("attn_proj", attn, w_ap, tm, tw, D_MODEL, outs=[_out2d(s_len, D_MODEL, BF16, tm, tw)])

    def merge_epi(acc, e, o):
        ya_v, ga, gc = [t.astype(F32) for t in e]
        o[0][...] = acc.astype(BF16)
        o[1][...] = (ga * ya_v + gc * acc).astype(BF16)
    gate_a, gate_c = (gates, _tile_spec(tm, tw, 0)), (gates, _tile_spec(tm, tw, 1))
    yc, merged = _mm_fwd("conv_proj", yconv, w_cp, tm, tw, D_MODEL, epilogue=merge_epi,
                         outs=[_out2d(s_len, D_MODEL, BF16, tm, tw), _out2d(s_len, D_MODEL, BF16, tm, tw)],
                         extras=[(ya, _tile_spec(tm, tw)), gate_a, gate_c])

    def res_epi(acc, e, o):
        o[0][...] = e[0] + acc
    x1 = _mm_fwd("out_proj", merged, w_out, tm, tw, D_MODEL, epilogue=res_epi,
                 extras=[(x, _tile_spec(tm, tw))])
    h2 = _rms_fwd("norm2", x1, row(norm2_g))

    def up_epi(acc, e, o):
        o[0][...] = acc.astype(BF16)
        o[1][...] = jnp.square(jnp.maximum(acc, 0.0)).astype(BF16)
    up, act = _mm_fwd("mlp_up", h2, w_up3, tm, tw, D_MODEL, epilogue=up_epi,
                      outs=[_out2d(s_len, D_FF, BF16, tm, tw), _out2d(s_len, D_FF, BF16, tm, tw)])

    def loss_epi(acc, e, o):
        err = e[0] + acc - e[1]
        dy = err * (1.0 / D_MODEL)
        o[0][...] = dy
        o[1][...] = dy.astype(BF16)
        sq = err * err
        part = sq[:, 0:LANE]
        for c0 in range(LANE, D_MODEL, LANE):
            part = part + sq[:, c0:c0 + LANE]
        o[2][...] = jnp.sum(part.reshape(tl // 8, 8, LANE), axis=0)
    tl = 512
    dy, dy_b, loss_part = _mm_fwd(
        "mlp_down", act, w_down, tl, D_MODEL, D_FF, epilogue=loss_epi,
        outs=[_out2d(s_len, D_MODEL, F32, tl, D_MODEL), _out2d(s_len, D_MODEL, BF16, tl, D_MODEL),
              (jax.ShapeDtypeStruct((8 * (s_len // tl), LANE), F32), pl.BlockSpec((8, LANE), lambda i, j, k: (i, 0)))],
        extras=[(x1, _tile_spec(tl, D_MODEL)), (target, _tile_spec(tl, D_MODEL))])

    def dup_epi(acc, e, o):
        o[0][...] = (acc * (2.0 * jnp.maximum(e[0].astype(F32), 0.0))).astype(BF16)
    full = lambda cols: pl.BlockSpec((tm, cols), lambda i, j, n: (i, n))
    tokens = lambda cols: pl.BlockSpec((s_len, cols), lambda i, j, m: (m, j))
    dup = _mm_bwd_x("d_act", dy_b, full(D_MODEL), w_down, tm, tw, D_MODEL, s_len, D_MODEL, epilogue=dup_epi,
                    outs=[_out2d(s_len, D_FF, BF16, tm, tw)], extras=[(up, _tile_spec(tm, tw))])
    g_down = _mm_bwd_w("g_down", act, dy_b, tokens(D_MODEL), D_MODEL, 512, D_MODEL, s_len, False)
    g_up = _mm_bwd_w("g_up", h2, dup, tokens(512), D_FF, D_MODEL, 512, s_len, True)
    comm.grads_ready("mlp", dict(w_down=g_down, w_up=g_up))
    dh2 = carrying("d_h2", _mm_bwd_x, "d_h2", dup, full(1024), w_up3, tm, tw, 1024, s_len, D_FF)
    dx1, dx1_b, dg2 = _rms_bwd("norm2_bwd", dh2, x1, row(norm2_g), dy)

    def dmerge_epi(acc, e, o):
        ya_v, yc_v, ga, gc = [t.astype(F32) for t in e]
        o[0][...] = (acc * ga).astype(BF16)
        o[1][...] = (acc * gc).astype(BF16)
        o[2][0] = (acc * ya_v * ga * (1.0 - ga)).astype(BF16)
        o[2][1] = (acc * yc_v * gc * (1.0 - gc)).astype(BF16)
    dya, dyc, dgp2 = _mm_bwd_x(
        "d_merged", dx1_b, full(D_MODEL), w_out, tm, tw, D_MODEL, s_len, D_MODEL, epilogue=dmerge_epi,
        outs=[_out2d(s_len, D_MODEL, BF16, tm, tw), _out2d(s_len, D_MODEL, BF16, tm, tw),
              (jax.ShapeDtypeStruct((2, s_len, D_MODEL), BF16), pl.BlockSpec((2, tm, tw), lambda i, j, n: (0, i, j)))],
        extras=[(ya, _tile_spec(tm, tw)), (yc, _tile_spec(tm, tw)), gate_a, gate_c])
    g_out = _mm_bwd_w("g_out", merged, dx1_b, tokens(512), D_MODEL, D_MODEL, 512, s_len, False)
    d_attn = _mm_bwd_x("d_attn", dya, full(D_MODEL), w_ap, tm, tw, D_MODEL, s_len, D_MODEL,
                       outs=[_out2d(s_len, D_MODEL, BF16, tm, tw)])
    g_ap = _mm_bwd_w("g_attn_proj", attn, dya, tokens(512), D_MODEL, D_MODEL, 512, s_len, False)
    d_yconv = _mm_bwd_x("d_yconv", dyc, full(D_MODEL), w_cp, tm, tw, D_MODEL, s_len, D_MODEL,
                        outs=[_out2d(s_len, D_MODEL, BF16, tm, tw)])
    g_cp = _mm_bwd_w("g_conv_proj", yconv, dyc, tokens(512), D_MODEL, D_MODEL, 512, s_len, False)
    comm.grads_ready("proj", dict(w_out=g_out, w_attn_proj=g_ap, w_conv_proj=g_cp))

    dproj6, dbias = carrying("attn_bwd", _attn_bwd, qkn, v, d_attn, bias, lse)
    dproj6, dgq, dgk = _qk_norm_bwd(dproj6, qk_raw, gq, gk)
    dproj6, dconv_wb = carrying("conv_bwd", _conv_bwd, dproj6, d_yconv, conv_in, conv_w, row(conv_b))
    d_rel = _bias_reduce(dbias)

    piece = lambda width: (lambda blk: (blk * width) // D_MODEL, lambda blk: (blk * width % D_MODEL) // width)
    pc, cb = piece(512)
    pieces = pl.BlockSpec((None, s_len, 512), lambda i, j, m: (pc(j), m, cb(j)))
    g_in = carrying("g_in", _mm_bwd_w, "g_in", h, dproj6, pieces, 6 * D_MODEL, D_MODEL, 512, s_len, True)
    g_gate = carrying("g_gate", _mm_bwd_w, "g_gate", h, dgp2, pieces, 2 * D_MODEL, D_MODEL, 512, s_len, True)
    comm.grads_ready("in", dict(w_in=g_in, w_gate=g_gate))
    dh_g = carrying("d_h_gate", _d_input, "d_h_gate", dgp2, w_gate3)
    dh = carrying("d_h_proj", _d_input, "d_h_proj", dproj6, w_in3, extra=dh_g)
    grad_x, _, dg1 = carrying("norm1_bwd", _rms_bwd, "norm1_bwd", dh, x, row(norm1_g), dx1)

    def bsum(t, f):
        return [], [t[0].astype(F32), t[1].astype(F32)]
    ts = 512
    db_a, db_c = _ew("b_gate_sum", bsum,
                     [(dgp2, pl.BlockSpec((None, ts, D_MODEL), lambda i: (0, i, 0))),
                      (dgp2, pl.BlockSpec((None, ts, D_MODEL), lambda i: (1, i, 0)))],
                     [], [], sums=[D_MODEL, D_MODEL], ts=ts)

    big = dict(w_in=g_in, w_attn_proj=g_ap, w_conv_proj=g_cp, w_gate=g_gate, w_out=g_out,
               w_up=g_up, w_down=g_down)
    small = dict(norm1_g=dg1, norm2_g=dg2, conv_wb=dconv_wb, b_gate=(db_a, db_c),
                 q_norm_g=dgq, k_norm_g=dgk, rel_bias=d_rel)
    return loss_part, grad_x, big, small


def _finish_loss(loss_part):
    def body(l_ref, lo_ref):
        total = jnp.sum(jnp.sum(l_ref[...], axis=0, keepdims=True), axis=1, keepdims=True)
        lo_ref[...] = jnp.broadcast_to(total * (0.5 / D_MODEL), lo_ref.shape)

    return pl.pallas_call(body, name="finish_loss", out_shape=jax.ShapeDtypeStruct((8, LANE), F32))(loss_part)


HBM_SPEC = pl.BlockSpec(memory_space=pl.ANY)


def _place():
    return lax.axis_index("x"), lax.axis_index("y"), lax.axis_index("c")


def _other_chips(x, y):
    return [(1 - x, y), (x, 1 - y), (1 - x, 1 - y)]


def _cast_into_slot(name, where, w):
    r, cols = w.shape
    ts = 256

    def body(w_ref, x_ref, o_ref):
        o_ref[...] = x_ref[...].astype(o_ref.dtype)

    return pl.pallas_call(
        body, name=name,
        grid_spec=pltpu.PrefetchScalarGridSpec(
            num_scalar_prefetch=1, grid=(r // ts,),
            in_specs=[pl.BlockSpec((ts, cols), lambda i, w: (i, 0))],
            out_specs=pl.BlockSpec((None, ts, cols), lambda i, w: (w[0], i, 0))),
        out_shape=jax.ShapeDtypeStruct((N_SHARD, r, cols), BF16),
        compiler_params=_params("parallel"),
    )(where, w)


def _remote(src, dst, send, recv, k, to):
    return pltpu.make_async_remote_copy(src_ref=src, dst_ref=dst, send_sem=send.at[k], recv_sem=recv.at[k],
                                        device_id=to, device_id_type=MESH)


def _gather_riders(slots):
    n = len(slots)

    def half(refs, w, shard, which):
        hr = slots[w].shape[1] // 2
        return refs[w].at[shard, pl.ds(which * hr, hr)]

    def each(fn):
        x, y, c = _place()
        for w in range(n):
            for j, chip in enumerate(_other_chips(x, y)):
                fn(w, 3 * w + j, 2 * x + y, 2 * chip[0] + chip[1], c, (*chip, c), (x, y, 1 - c))

    def chips_start(_, refs, send, recv):
        each(lambda w, k, me, them, c, peer, sib: _remote(half(refs, w, me, c), half(refs, w, me, c), send, recv, k, peer).start())

    def chips_finish(_, refs, send, recv):
        each(lambda w, k, me, them, c, peer, sib: _remote(half(refs, w, them, c), half(refs, w, them, c), send, recv, k, peer).wait_recv())
        each(lambda w, k, me, them, c, peer, sib: _remote(half(refs, w, me, c), half(refs, w, me, c), send, recv, k, peer).wait_send())

    def sibling_start(_, refs, send, recv):
        each(lambda w, k, me, them, c, peer, sib: _remote(half(refs, w, them, c), half(refs, w, them, c), send, recv, k, sib).start())

    def sibling_finish(_, refs, send, recv):
        each(lambda w, k, me, them, c, peer, sib: _remote(half(refs, w, them, 1 - c), half(refs, w, them, 1 - c), send, recv, k, sib).wait_recv())
        each(lambda w, k, me, them, c, peer, sib: _remote(half(refs, w, them, c), half(refs, w, them, c), send, recv, k, sib).wait_send())

    return (lambda s: _Rider([], s, 3 * n, chips_start, chips_finish),
            lambda s: _Rider([], s, 3 * n, sibling_start, sibling_finish))


def _pair_exchange_rider(grads, landing):
    n = len(grads)

    def copies(srcs, dsts, send, recv):
        x, y, c = _place()
        out = []
        for w in range(n):
            hr = grads[w].shape[1] // 2
            out.append(_remote(srcs[w].at[:, pl.ds((1 - c) * hr, hr)], dsts[w], send, recv, w, (x, y, 1 - c)))
        return out

    def start(srcs, dsts, send, recv):
        for cp in copies(srcs, dsts, send, recv):
            cp.start()

    def finish(srcs, dsts, send, recv):
        for cp in copies(srcs, dsts, send, recv):
            cp.wait()

    return _Rider(grads, landing, n, start, finish)


def _row_tile(hr):
    return min(hr, 256)


def _pair_add(name, where, grad, got):
    _, hr, cols = got.shape
    tr = _row_tile(hr)
    nblk = hr // tr

    def body(w_ref, g_ref, r_ref, o_ref):
        o_ref[...] = (g_ref[...] + r_ref[...]).astype(o_ref.dtype)

    return pl.pallas_call(
        body, name=name,
        grid_spec=pltpu.PrefetchScalarGridSpec(
            num_scalar_prefetch=1, grid=(N_SHARD, nblk),
            in_specs=[pl.BlockSpec((None, tr, cols), lambda s, i, w: (s, w[1] * nblk + i, 0)),
                      pl.BlockSpec((None, tr, cols), lambda s, i, w: (s, i, 0))],
            out_specs=pl.BlockSpec((None, tr, cols), lambda s, i, w: (s, i, 0))),
        out_shape=jax.ShapeDtypeStruct(got.shape, BF16),
        compiler_params=_params("parallel", "parallel"),
    )(where, grad, got)


def _chip_exchange_rider(partials, landing):
    n = len(partials)

    def copies(srcs, dsts, send, recv):
        x, y, c = _place()
        return [_remote(srcs[w].at[2 * chip[0] + chip[1]], dsts[w].at[j], send, recv, 3 * w + j, (*chip, c))
                for w in range(n) for j, chip in enumerate(_other_chips(x, y))]

    def start(srcs, dsts, send, recv):
        for cp in copies(srcs, dsts, send, recv):
            cp.start()

    def finish(srcs, dsts, send, recv):
        for cp in copies(srcs, dsts, send, recv):
            cp.wait()

    return _Rider(partials, landing, 3 * n, start, finish)


def _final_add(name, where, grad, got, arrived):
    _, hr, cols = got.shape
    tr = _row_tile(hr)
    nblk = hr // tr

    def body(w_ref, g_ref, r_ref, a_ref, o_ref):
        acc = g_ref[...] + r_ref[...]
        for j in range(3):
            acc = acc + a_ref[j].astype(F32)
        o_ref[...] = acc

    return pl.pallas_call(
        body, name=name,
        grid_spec=pltpu.PrefetchScalarGridSpec(
            num_scalar_prefetch=1, grid=(nblk,),
            in_specs=[pl.BlockSpec((None, tr, cols), lambda i, w: (w[0], w[1] * nblk + i, 0)),
                      pl.BlockSpec((None, tr, cols), lambda i, w: (w[0], i, 0)),
                      pl.BlockSpec((3, tr, cols), lambda i, w: (0, i, 0))],
            out_specs=pl.BlockSpec((tr, cols), lambda i, w: (w[1] * nblk + i, 0))),
        out_shape=jax.ShapeDtypeStruct((2 * hr, cols), F32),
        compiler_params=_params("parallel"),
    )(where, grad, got, arrived)


def _pair_share_rider(shards):
    n = len(shards)

    def half(refs, w, which):
        hr = shards[w].shape[0] // 2
        return refs[w].at[pl.ds(which * hr, hr)]

    def start(_, refs, send, recv):
        x, y, c = _place()
        for w in range(n):
            _remote(half(refs, w, c), half(refs, w, c), send, recv, w, (x, y, 1 - c)).start()

    def finish(_, refs, send, recv):
        x, y, c = _place()
        for w in range(n):
            _remote(half(refs, w, 1 - c), half(refs, w, 1 - c), send, recv, w, (x, y, 1 - c)).wait_recv()
        for w in range(n):
            _remote(half(refs, w, c), half(refs, w, c), send, recv, w, (x, y, 1 - c)).wait_send()

    return _Rider([], shards, n, start, finish)


class _Exchange:
    PLAN = {"bias_expand": [("early", "gather")], "norm1": [("early", "forward")],
            "attn_fwd": [("late", "gather")], "conv_fwd": [("late", "forward")],
            "d_h2": [("mlp", "pair")], "attn_bwd": [("mlp", "chips"), ("proj", "pair")], "conv_bwd": [("mlp", "share")],
            "g_in": [("proj", "chips")], "g_gate": [("proj", "share")],
            "d_h_gate": [("in", "pair")], "d_h_proj": [("in", "chips")], "norm1_bwd": [("in", "share")]}

    def __init__(self, where, early_slots, late_slots):
        self.where = where
        self.slots = dict(early=early_slots, late=late_slots)
        self.stage = {g: dict(zip(("gather", "forward"), _gather_riders(s))) for g, s in self.slots.items()}
        self.groups, self.reduced, self.pending = {}, {}, []

    def early_weights(self):
        w_in3, w_gate3, small = self.slots["early"]
        conv_w = small[:, :3, :].transpose(1, 0, 2).reshape(3, N_SHARD * small.shape[2])
        return w_in3, w_gate3, conv_w

    def late_weights(self):
        rows = lambda a: a.reshape(a.shape[0] * a.shape[1], a.shape[2])
        w_ap, w_cp, w_out, w_up3, w_down = self.slots["late"]
        return rows(w_ap), rows(w_cp), rows(w_out), w_up3, rows(w_down)

    def grads_ready(self, group, grads):
        names = list(grads)
        g4 = [g if g.ndim == 3 else g.reshape(N_SHARD, -1, g.shape[1]) for g in grads.values()]
        self.groups[group] = dict(names=names, g4=g4)

    def riders(self, at):
        self.pending = self.PLAN.get(at, [])
        out = []
        for group, stage in self.pending:
            if group in self.slots:
                out.append(self.stage[group][stage](self.slots[group]))
                continue
            st = self.groups[group]
            if stage == "pair":
                landing = [lax.empty((N_SHARD, g.shape[1] // 2, g.shape[2]), F32) for g in st["g4"]]
                out.append(_pair_exchange_rider(st["g4"], landing))
            elif stage == "chips":
                landing = [lax.empty((3,) + p.shape[1:], p.dtype) for p in st["partial"]]
                out.append(_chip_exchange_rider(st["partial"], landing))
            else:
                out.append(_pair_share_rider(st["halves"]))
        return out

    def done(self, at, carried):
        for (group, stage), arrays in zip(self.pending, carried):
            if group in self.slots:
                self.slots[group] = arrays
                continue
            st = self.groups[group]
            tag = lambda what, n: what + "_" + n
            if stage == "pair":
                st["got"] = arrays
                st["partial"] = [_pair_add(tag("pair_add", n), self.where, g, r)
                                 for n, g, r in zip(st["names"], st["g4"], arrays)]
            elif stage == "chips":
                st["halves"] = [_final_add(tag("final_add", n), self.where, g, r, a)
                                for n, g, r, a in zip(st["names"], st["g4"], st["got"], arrays)]
            else:
                self.reduced.update(zip(st["names"], arrays))


SMALL_ROWS = 32
N_DEV = 8


def _all_reduce_small(pack):
    def body(p_ref, o_ref, buf, send, recv):
        x, y, c = _place()
        buf[4 * x + 2 * y + c] = p_ref[...]
        copies, waits = [], []
        for k in range(1, N_DEV):
            px = 1 - x if k & 4 else x
            py = 1 - y if k & 2 else y
            pc = 1 - c if k & 1 else c
            copies.append(pltpu.make_async_remote_copy(
                src_ref=p_ref, dst_ref=buf.at[4 * x + 2 * y + c], send_sem=send.at[k - 1],
                recv_sem=recv.at[k - 1], device_id=(px, py, pc), device_id_type=MESH))
            waits.append(pltpu.make_async_remote_copy(
                src_ref=p_ref, dst_ref=buf.at[4 * px + 2 * py + pc], send_sem=send.at[k - 1],
                recv_sem=recv.at[k - 1], device_id=(px, py, pc), device_id_type=MESH))
        for cp in copies:
            cp.start()
        for cp in waits:
            cp.wait_recv()
        acc = buf[0]
        for d in range(1, N_DEV):
            acc = acc + buf[d]
        o_ref[...] = acc
        for cp in copies:
            cp.wait_send()

    return pl.pallas_call(
        body, name="all_reduce_small",
        out_shape=jax.ShapeDtypeStruct(pack.shape, F32),
        scratch_shapes=[pltpu.VMEM((N_DEV,) + pack.shape, F32),
                        pltpu.SemaphoreType.DMA((N_DEV - 1,)), pltpu.SemaphoreType.DMA((N_DEV - 1,))],
    )(pack)


def _adamw(name, w, g, m, v):
    c1 = 1.0 - ADAM_B1 ** ADAM_STEP
    c2 = 1.0 - ADAM_B2 ** ADAM_STEP

    def fn(t, f):
        wv, gv, mv, vv = t
        m2 = ADAM_B1 * mv + (1.0 - ADAM_B1) * gv
        v2 = ADAM_B2 * vv + (1.0 - ADAM_B2) * (gv * gv)
        delta = -ADAM_LR * ((m2 / c1) / (jnp.sqrt(v2 / c2) + ADAM_EPS) + ADAM_WD * wv)
        return [delta, m2, v2], []

    cols = w.shape[1]
    return _ew(name, fn, [w, g, m, v], [], [(cols, F32)] * 3, ts=min(w.shape[0], 256))


def _pack_small(norm1_g, norm2_g, conv_b, b_gate, conv_w, q_norm_g, k_norm_g, rel_bias):
    pack = jnp.zeros((SMALL_ROWS, D_MODEL), F32)
    for r0, v in ((0, norm1_g), (1, norm2_g), (2, conv_b), (3, b_gate.reshape(2, D_MODEL)), (5, conv_w),
                  (8, q_norm_g), (9, k_norm_g), (10, rel_bias)):
        v = v.reshape(-1, v.shape[-1]).astype(F32)
        pack = pack.at[r0:r0 + v.shape[0], :v.shape[1]].set(v)
    return pack


def _unpack_small(pack, conv_cols):
    return dict(norm1_g=pack[0], norm2_g=pack[1], conv_b=pack[2], b_gate=pack[3:5].reshape(2 * D_MODEL),
                conv_w=pack[5:8, :conv_cols], q_norm_g=pack[8, :HEAD_DIM], k_norm_g=pack[9, :HEAD_DIM],
                rel_bias=pack[10:10 + N_HEADS, :N_REL])


BIG = ["w_in", "w_attn_proj", "w_conv_proj", "w_gate", "w_out", "w_up", "w_down"]
LATE = ["w_attn_proj", "w_conv_proj", "w_out", "w_up", "w_down"]
WEIGHTS = ["norm1_g", "w_in", "q_norm_g", "k_norm_g", "rel_bias", "conv_w", "conv_b", "w_attn_proj",
           "w_conv_proj", "w_gate", "b_gate", "w_out", "norm2_g", "w_up", "w_down"]


def kernel(x, norm1_g, w_in, q_norm_g, k_norm_g, rel_bias, conv_w, conv_b, w_attn_proj, w_conv_proj, w_gate, b_gate, w_out, norm2_g, w_up, w_down, loss_target, m_norm1_g, m_w_in, m_q_norm_g, m_k_norm_g, m_rel_bias, m_conv_w, m_conv_b, m_w_attn_proj, m_w_conv_proj, m_w_gate, m_b_gate, m_w_out, m_norm2_g, m_w_up, m_w_down, v_norm1_g, v_w_in, v_q_norm_g, v_k_norm_g, v_rel_bias, v_conv_w, v_conv_b, v_w_attn_proj, v_w_conv_proj, v_w_gate, v_b_gate, v_w_out, v_norm2_g, v_w_up, v_w_down):
    given = dict(locals())
    w = {n: given[n] for n in WEIGHTS}
    m = {n: given["m_" + n] for n in WEIGHTS}
    v = {n: given["v_" + n] for n in WEIGHTS}
    s_len = x.shape[1]
    shard = 2 * lax.axis_index("x") + lax.axis_index("y")
    where = jnp.stack([shard, lax.axis_index("c")]).astype(jnp.int32)
    conv_cols = conv_w.shape[1]

    small_in = lax.dynamic_update_slice(jnp.zeros((N_SHARD, 16, conv_cols), F32), conv_w[None], (shard, 0, 0))
    slot = {n: _cast_into_slot("cast_" + n, where, w[n]) for n in BIG}
    comm = _Exchange(where, [slot["w_in"], slot["w_gate"], small_in], [slot[n] for n in LATE])

    loss_part, grad_x, _, small = _local_grads(
        x.reshape(s_len, D_MODEL), loss_target.reshape(s_len, D_MODEL), norm1_g, q_norm_g, k_norm_g,
        rel_bias, conv_b, b_gate, norm2_g, comm)
    grad = dict(comm.reduced)

    loss_local = _finish_loss(loss_part)
    pack = _pack_small(small["norm1_g"], small["norm2_g"], small["conv_wb"][3], jnp.concatenate(small["b_gate"], axis=1),
                       small["conv_wb"][0:3], small["q_norm_g"], small["k_norm_g"], small["rel_bias"])
    total = _all_reduce_small(pack)
    g_small = _unpack_small(total, D_MODEL)
    g_small["conv_w"] = lax.dynamic_slice(g_small["conv_w"], (0, shard * conv_cols), (3, conv_cols))
    grad.update(g_small)

    delta, new_m, new_v = {}, {}, {}
    for n in BIG:
        delta[n], new_m[n], new_v[n] = _adamw("adamw_" + n, w[n], grad[n], m[n], v[n])
    small_names = [n for n in WEIGHTS if n not in BIG]
    packs = [_pack_small(**{n: src[n] for n in small_names}) for src in (w, grad, m, v)]
    for out, packed in zip((delta, new_m, new_v), _adamw("adamw_small", *packs)):
        out.update({n: a.reshape(w[n].shape) for n, a in _unpack_small(packed, conv_cols).items()})

    loss = lax.psum(loss_local[0, 0], ("x", "y", "c"))
    outs = [loss, grad_x.reshape(x.shape)]
    for group in (grad, delta, new_m, new_v):
        outs += [group[n].reshape(w[n].shape) for n in WEIGHTS]
    return tuple(outs)
```

```python
import functools

import jax
import jax.numpy as jnp
from jax import lax
from jax.experimental import pallas as pl
from jax.experimental.pallas import tpu as pltpu

F32 = jnp.float32
BF16 = jnp.bfloat16

D_MODEL = 1024
N_HEADS = 16
HEAD_DIM = 64
CHUNK = 64
N_PREV_CHUNKS = 8
MAX_REL = 256
D_FF = 4096
N_REL = 2 * MAX_REL + 1
REL_PAD = 640
EPS = 1e-6
NEG_INF = -1e30
QK_SCALE = HEAD_DIM ** -0.5

SUPER = 4 * CHUNK
BAND = SUPER + N_PREV_CHUNKS * CHUNK
SKEW_W = 1024
N_SHARD = 4
LANE = 128
VMEM_LIMIT = 48 * 1024 * 1024

ADAM_LR = 0.001
ADAM_B1 = 0.9
ADAM_B2 = 0.999
ADAM_EPS = 1e-08
ADAM_WD = 0.01
ADAM_STEP = 10

MESH = pl.DeviceIdType.MESH
NN = (((1,), (0,)), ((), ()))
NT = (((1,), (1,)), ((), ()))
TN = (((0,), (0,)), ((), ()))


def _params(*sem):
    return pltpu.CompilerParams(dimension_semantics=sem or None, vmem_limit_bytes=VMEM_LIMIT)


HBM_SPEC = pl.BlockSpec(memory_space=pl.ANY)


class _Rider:
    def __init__(self, sources, arrays, n_sem, start, finish):
        self.sources, self.arrays, self.n_sem, self.start, self.finish = sources, arrays, n_sem, start, finish


def _carry(riders, body, *, name, out_shape, grid=(), in_specs=None, out_specs=None, scratch_shapes=(),
           semantics=(), input_output_aliases=None):
    aliases = dict(input_output_aliases or {})
    if not riders:
        kw = {} if in_specs is None else dict(in_specs=in_specs, out_specs=out_specs)
        return pl.pallas_call(body, name=name, grid=grid, out_shape=out_shape, scratch_shapes=scratch_shapes,
                              input_output_aliases=aliases, compiler_params=_params(*semantics), **kw)
    single = not isinstance(out_shape, (list, tuple))
    shapes = [out_shape] if single else list(out_shape)
    n_out, n_scr = len(shapes), len(scratch_shapes)
    srcs = [a for r in riders for a in r.sources]
    arrs = [a for r in riders for a in r.arrays]
    vmem = pl.BlockSpec(memory_space=pltpu.VMEM)

    def run(*args):
        n_in = len(args)

        def wrapped(*refs):
            pos = n_in
            src_refs = refs[pos:pos + len(srcs)]
            pos += len(srcs) + len(arrs)
            outs = refs[pos:pos + n_out]
            pos += n_out
            arr_refs = refs[pos:pos + len(arrs)]
            pos += len(arrs)
            scratch = refs[pos:pos + n_scr]
            sems = refs[pos + n_scr:]
            first, last = True, True
            for d, size in enumerate(grid):
                first = jnp.logical_and(first, pl.program_id(d) == 0)
                last = jnp.logical_and(last, pl.program_id(d) == size - 1)

            def each(method):
                s0 = a0 = 0
                for k, r in enumerate(riders):
                    getattr(r, method)(src_refs[s0:s0 + len(r.sources)], arr_refs[a0:a0 + len(r.arrays)],
                                       sems[2 * k], sems[2 * k + 1])
                    s0, a0 = s0 + len(r.sources), a0 + len(r.arrays)

            pl.when(first)(lambda: each("start"))
            body(*refs[:n_in], *outs, *scratch)
            pl.when(last)(lambda: each("finish"))

        ins = [vmem] * n_in if in_specs is None else list(in_specs)
        if out_specs is None:
            o_specs = [vmem] * n_out
        else:
            o_specs = [out_specs] if single else list(out_specs)
        for k in range(len(arrs)):
            aliases[n_in + len(srcs) + k] = n_out + k
        res = pl.pallas_call(
            wrapped, name=name, grid=grid,
            in_specs=ins + [HBM_SPEC] * (len(srcs) + len(arrs)),
            out_specs=o_specs + [HBM_SPEC] * len(arrs),
            out_shape=shapes + [jax.ShapeDtypeStruct(a.shape, a.dtype) for a in arrs],
            scratch_shapes=list(scratch_shapes) + [pltpu.SemaphoreType.DMA((r.n_sem,)) for r in riders for _ in range(2)],
            input_output_aliases=aliases,
            compiler_params=_params(*["arbitrary"] * len(grid)),
        )(*args, *srcs, *arrs)
        core, rest = res[:n_out], list(res[n_out:])
        carried, a0 = [], 0
        for r in riders:
            carried.append(rest[a0:a0 + len(r.arrays)])
            a0 += len(r.arrays)
        return (core[0] if single else core), carried

    return run


def _mm(name, dims, a, a_spec, b, b_spec, grid, tile, outs, epilogue=None, extras=(), riders=()):
    nk, ne, no = grid[2], len(extras), len(outs)

    def body(a_ref, b_ref, *refs):
        e_refs, o_refs = refs[:ne], refs[ne:ne + no]
        part = lax.dot_general(a_ref[...], b_ref[...], dims, preferred_element_type=F32)

        def finish(acc):
            if epilogue is None:
                o_refs[0][...] = acc.astype(o_refs[0].dtype)
            else:
                epilogue(acc, [r[...] for r in e_refs], o_refs)

        if nk == 1:
            finish(part)
        else:
            acc_ref = refs[ne + no]
            k = pl.program_id(2)

            @pl.when(k == 0)
            def _():
                acc_ref[...] = part

            @pl.when(k > 0)
            def _():
                acc_ref[...] += part

            @pl.when(k == nk - 1)
            def _():
                finish(acc_ref[...])

    res = _carry(
        riders, body, name=name, grid=grid,
        in_specs=[a_spec, b_spec] + [s for _, s in extras],
        out_specs=[s for _, s in outs],
        out_shape=[s for s, _ in outs],
        scratch_shapes=[pltpu.VMEM(tile, F32)] if nk > 1 else [],
        semantics=("parallel", "parallel", "arbitrary"),
    )(a, b, *[e for e, _ in extras])
    res, carried = res if riders else (res, None)
    res = res[0] if no == 1 else res
    return (res, carried) if riders else res


def _tile_spec(tm, tn, col0=0):
    return pl.BlockSpec((tm, tn), lambda i, j, k: (i, j + col0))


def _out2d(m, n, dtype, tm, tn):
    return (jax.ShapeDtypeStruct((m, n), dtype), _tile_spec(tm, tn))


def _mm_fwd(name, a, w, tm, tn, tk, outs=None, epilogue=None, extras=(), col0=0, ncols=None, riders=()):
    m, kdim = a.shape
    if w.ndim == 3:
        per = w.shape[2] // tn
        n = ncols or N_SHARD * w.shape[2]
        w_spec = pl.BlockSpec((None, tk, tn), lambda i, j, k: ((j + col0) // per, k, (j + col0) % per))
    else:
        n = ncols or w.shape[1]
        w_spec = pl.BlockSpec((tk, tn), lambda i, j, k: (k, j + col0))
    if outs is None:
        outs = [_out2d(m, n, F32, tm, tn)]
    return _mm(name, NN, a, pl.BlockSpec((tm, tk), lambda i, j, k: (i, k)), w, w_spec,
               (m // tm, n // tn, kdim // tk), (tm, tn), outs, epilogue, extras, riders)


def _mm_bwd_x(name, g, g_spec, w, tm, tj, tc, m, n_contract, outs=None, epilogue=None, extras=(), riders=()):
    if w.ndim == 3:
        per = w.shape[2] // tc
        kdim = w.shape[1]
        w_spec = pl.BlockSpec((None, tj, tc), lambda i, j, n: (n // per, j, n % per))
    else:
        kdim = w.shape[0]
        w_spec = pl.BlockSpec((tj, tc), lambda i, j, n: (j, n))
    if outs is None:
        outs = [_out2d(m, kdim, F32, tm, tj)]
    return _mm(name, NT, g, g_spec, w, w_spec, (m // tm, kdim // tj, n_contract // tc),
               (tm, tj), outs, epilogue, extras, riders)


def _mm_bwd_w(name, a, g, g_spec, n, tk, tn, tm, sharded, riders=()):
    m, kdim = a.shape
    if sharded:
        per = (n // N_SHARD) // tn
        out = (jax.ShapeDtypeStruct((N_SHARD, kdim, n // N_SHARD), F32),
               pl.BlockSpec((None, tk, tn), lambda i, j, mm: (j // per, i, j % per)))
    else:
        out = (jax.ShapeDtypeStruct((kdim, n), F32), pl.BlockSpec((tk, tn), lambda i, j, mm: (i, j)))
    return _mm(name, TN, a, pl.BlockSpec((tm, tk), lambda i, j, mm: (mm, i)), g, g_spec,
               (kdim // tk, n // tn, m // tm), (tk, tn), [out], riders=riders)


def _ew(name, fn, tiles, fulls, outs, sums=(), ts=512, riders=()):
    tiles = [t if isinstance(t, tuple) else (t, pl.BlockSpec((ts, t.shape[1]), lambda i: (i, 0)))
             for t in tiles]
    s_rows = tiles[0][0].shape[-2]
    nt, nf, no = len(tiles), len(fulls), len(outs)

    def body(*refs):
        t_vals = [r[...] for r in refs[:nt]]
        f_vals = [r[...] for r in refs[nt:nt + nf]]
        o_refs, s_refs = refs[nt + nf:nt + nf + no], refs[nt + nf + no:]
        o_vals, s_vals = fn(t_vals, f_vals)
        for r, v in zip(o_refs, o_vals):
            r[...] = v.astype(r.dtype)
        for r, v in zip(s_refs, s_vals):
            part = jnp.sum(v, axis=0, keepdims=True)

            @pl.when(pl.program_id(0) == 0)
            def _():
                r[...] = part

            @pl.when(pl.program_id(0) > 0)
            def _():
                r[...] += part

    full_specs = [pl.BlockSpec(f.shape, lambda i, nd=f.ndim: (0,) * nd) for f in fulls]
    return _carry(
        riders, body, name=name, grid=(s_rows // ts,),
        in_specs=[s for _, s in tiles] + full_specs,
        out_specs=[pl.BlockSpec((ts, c), lambda i: (i, 0)) for c, _ in outs]
        + [pl.BlockSpec((1, c), lambda i: (0, 0)) for c in sums],
        out_shape=[jax.ShapeDtypeStruct((s_rows, c), dt) for c, dt in outs]
        + [jax.ShapeDtypeStruct((1, c), F32) for c in sums],
        semantics=("arbitrary",),
    )(*[t for t, _ in tiles], *fulls)


def _rms_fwd(name, x, g, riders=()):
    def fn(t, f):
        xv = t[0]
        r = lax.rsqrt(jnp.mean(xv * xv, axis=-1, keepdims=True) + EPS)
        return [xv * r * f[0]], []
    out = _ew(name, fn, [x], [g], [(x.shape[1], BF16)], riders=riders)
    return (out[0][0], out[1]) if riders else out[0]


def _rms_bwd(name, dh, x, g, dres, riders=()):
    def fn(t, f):
        dhv, xv, dr = t
        r = lax.rsqrt(jnp.mean(xv * xv, axis=-1, keepdims=True) + EPS)
        xhat = xv * r
        u = dhv * f[0]
        dx = r * (u - xhat * jnp.mean(u * xhat, axis=-1, keepdims=True)) + dr
        return [dx, dx], [dhv * xhat]
    c = x.shape[1]
    return _ew(name, fn, [dh, x, dres], [g], [(c, F32), (c, BF16)], sums=[c], riders=riders)


def _split3(x):
    x1 = x.astype(BF16)
    r1 = x - x1.astype(F32)
    x2 = r1.astype(BF16)
    x3 = (r1 - x2.astype(F32)).astype(BF16)
    return x1, x2, x3


def _rel_class(cp):
    far = (cp < MAX_REL) | (cp > BAND)
    return jnp.where(far, 2 * MAX_REL, BAND - cp)


def _skew_rows(x, sign):
    row = lax.broadcasted_iota(jnp.int32, x.shape, 0)
    for b in range(CHUNK.bit_length() - 1):
        shift = (1 << b) if sign > 0 else SKEW_W - (1 << b)
        x = jnp.where((row >> b) & 1 == 1, pltpu.roll(x, shift, 1), x)
    return x


def _roll_lanes(x, shift):
    return x if shift % SKEW_W == 0 else pltpu.roll(x, shift % SKEW_W, 1)


def _bias_expand(rel_bias, riders=()):
    rel = jnp.pad(rel_bias, ((0, 0), (0, REL_PAD - N_REL))).reshape(N_HEADS, 1, REL_PAD)

    def body(rel_ref, o_ref):
        cls = lax.broadcasted_iota(jnp.int32, (REL_PAD, SKEW_W), 0)
        cp = lax.broadcasted_iota(jnp.int32, (REL_PAD, SKEW_W), 1)
        onehot = (cls == _rel_class(cp)).astype(BF16)
        rel8 = jnp.broadcast_to(rel_ref[...], (8, REL_PAD))
        trow = sum(jnp.dot(p, onehot, preferred_element_type=F32) for p in _split3(rel8))[0:1]
        first = _skew_rows(jnp.broadcast_to(trow, (CHUNK, SKEW_W)), +1)
        full = jnp.concatenate([_roll_lanes(first, CHUNK * g) for g in range(SUPER // CHUNK)], axis=0)[:, :BAND]
        qc = lax.broadcasted_iota(jnp.int32, (SUPER, BAND), 0) // CHUNK
        kc = lax.broadcasted_iota(jnp.int32, (SUPER, BAND), 1) // CHUNK
        on_band = (kc >= qc) & (kc <= qc + N_PREV_CHUNKS)
        o_ref[...] = jnp.where(on_band, full, NEG_INF).T

    return _carry(
        riders, body, name="bias_expand", grid=(N_HEADS,),
        in_specs=[pl.BlockSpec((None, 1, REL_PAD), lambda h: (h, 0, 0))],
        out_specs=pl.BlockSpec((None, BAND, SUPER), lambda h: (h, 0, 0)),
        out_shape=jax.ShapeDtypeStruct((N_HEADS, BAND, SUPER), F32),
        semantics=("arbitrary",),
    )(rel)


def _bias_reduce(dbias):
    def body(d_ref, o_ref):
        x = jnp.concatenate([d_ref[...].T, jnp.zeros((SUPER, SKEW_W - BAND), F32)], axis=1)
        folded = sum(_roll_lanes(x[CHUNK * g:CHUNK * (g + 1)], -CHUNK * g) for g in range(SUPER // CHUNK))
        diag = jnp.sum(_skew_rows(folded, -1), axis=0, keepdims=True)
        cp = lax.broadcasted_iota(jnp.int32, (SKEW_W, REL_PAD), 0)
        cls = lax.broadcasted_iota(jnp.int32, (SKEW_W, REL_PAD), 1)
        onehot = (cls == _rel_class(cp)).astype(BF16)
        diag8 = jnp.broadcast_to(diag, (8, SKEW_W))
        o_ref[...] = sum(jnp.dot(p, onehot, preferred_element_type=F32) for p in _split3(diag8))[0:1]

    out = pl.pallas_call(
        body, name="bias_reduce", grid=(N_HEADS,),
        in_specs=[pl.BlockSpec((None, BAND, SUPER), lambda h: (h, 0, 0))],
        out_specs=pl.BlockSpec((None, 1, REL_PAD), lambda h: (h, 0, 0)),
        out_shape=jax.ShapeDtypeStruct((N_HEADS, 1, REL_PAD), F32),
        compiler_params=_params("arbitrary"),
    )(dbias)
    return out.reshape(N_HEADS, REL_PAD)[:, :N_REL]


HEADS_PER_STEP = 4
HEAD_COLS = HEADS_PER_STEP * HEAD_DIM
N_HEAD_GROUPS = N_HEADS // HEADS_PER_STEP


def _unit(x):
    r = lax.rsqrt(jnp.mean(x * x, axis=-1, keepdims=True) + EPS)
    return x * r, r


def _scores_t(qs, kn, bias_t, dead):
    s = jnp.concatenate([lax.dot_general(k, qs, NT, preferred_element_type=F32) for k in kn], axis=0) + bias_t
    return jnp.where(dead, NEG_INF, s)


def _band_specs(nb, col0, clamp_hi):
    def spec(d):
        def index(hg, i):
            blk = jnp.maximum(i - d, 0)
            if clamp_hi:
                blk = jnp.minimum(blk, nb - 1)
            return (blk, col0 + hg)
        return pl.BlockSpec((SUPER, HEAD_COLS), index)
    return [spec(2), spec(1), spec(0)]


def _head_sums(y):
    cols = y.shape[1]
    same_head = (lax.broadcasted_iota(jnp.int32, (cols, cols), 0) // HEAD_DIM
                 == lax.broadcasted_iota(jnp.int32, (cols, cols), 1) // HEAD_DIM).astype(BF16)
    hi = y.astype(BF16)
    lo = (y - hi.astype(F32)).astype(BF16)
    return (jnp.dot(hi, same_head, preferred_element_type=F32)
            + jnp.dot(lo, same_head, preferred_element_type=F32))


def _head_unit(x):
    r = lax.rsqrt(_head_sums(x * x) * (1.0 / HEAD_DIM) + EPS)
    return x * r, r


def _head(hh):
    return slice(HEAD_DIM * hh, HEAD_DIM * (hh + 1))


def _key_block(j):
    return slice(SUPER * j, SUPER * (j + 1))


def _dead_keys(i):
    key = lax.broadcasted_iota(jnp.int32, (BAND, SUPER), 0)
    return key < (2 - i) * SUPER


LSE_ROWS = 8


def _attn_fwd(qkn, v, bias, riders=()):
    s_len = qkn.shape[0]
    nb = s_len // SUPER

    def body(q_ref, k0, k1, k2, v0, v1, v2, b_ref, o_ref, lse_ref):
        dead = _dead_keys(pl.program_id(1))
        outs = []

        def probabilities(hh):
            sl = _head(hh)
            s = _scores_t(q_ref[:, sl], [k0[:, sl], k1[:, sl], k2[:, sl]], b_ref[hh], dead)
            m = jnp.max(s, axis=0, keepdims=True)
            e = jnp.exp(s - m)
            l = jnp.sum(e, axis=0, keepdims=True)
            lse_ref[hh:hh + 1, :] = m + jnp.log(l)
            return (e * (1.0 / l)).astype(BF16), sl

        def weighted_values(p, sl):
            outs.append(sum(lax.dot_general(p[_key_block(j), :], vj[:, sl], TN, preferred_element_type=F32)
                            for j, vj in enumerate((v0, v1, v2))))

        ready = probabilities(0)
        for hh in range(1, HEADS_PER_STEP):
            following = probabilities(hh)
            weighted_values(*ready)
            ready = following
        weighted_values(*ready)
        o_ref[...] = jnp.concatenate(outs, axis=1).astype(o_ref.dtype)

    return _carry(
        riders, body, name="attn_fwd", grid=(N_HEAD_GROUPS, nb),
        in_specs=[pl.BlockSpec((SUPER, HEAD_COLS), lambda hg, i: (i, hg))]
        + _band_specs(nb, N_HEAD_GROUPS, False) + _band_specs(nb, 0, False)
        + [pl.BlockSpec((HEADS_PER_STEP, BAND, SUPER), lambda hg, i: (hg, 0, 0))],
        out_specs=[pl.BlockSpec((SUPER, HEAD_COLS), lambda hg, i: (i, hg)),
                   pl.BlockSpec((None, LSE_ROWS, SUPER), lambda hg, i: (hg, 0, i))],
        out_shape=[jax.ShapeDtypeStruct((s_len, D_MODEL), BF16),
                   jax.ShapeDtypeStruct((N_HEAD_GROUPS, LSE_ROWS, s_len), F32)],
        semantics=("parallel", "arbitrary"),
    )(qkn, qkn, qkn, qkn, v, v, v, bias)


def _attn_bwd(qkn, v, d_out, bias, lse, riders=()):
    s_len = qkn.shape[0]
    nb = s_len // SUPER

    def body(q_ref, k0, k1, k2, v0, v1, v2, do_ref, b_ref, lse_ref, dp_ref, db_ref, aq_ref, ak_ref, av_ref):
        i = pl.program_id(1)

        @pl.when(i == 0)
        def _():
            aq_ref[...] = jnp.zeros_like(aq_ref)
            ak_ref[...] = jnp.zeros_like(ak_ref)
            av_ref[...] = jnp.zeros_like(av_ref)
            db_ref[...] = jnp.zeros_like(db_ref)

        @pl.when(i < nb)
        def _():
            dead = _dead_keys(i)
            dq, dk, dv = [], [[], [], []], [[], [], []]

            def softmax_grad(hh):
                sl = _head(hh)
                qs, do = q_ref[:, sl], do_ref[:, sl]
                kn = [k0[:, sl], k1[:, sl], k2[:, sl]]
                vv = [v0[:, sl], v1[:, sl], v2[:, sl]]
                p = jnp.exp(_scores_t(qs, kn, b_ref[hh], dead) - lse_ref[hh:hh + 1, :])
                dp = jnp.concatenate([lax.dot_general(vj, do, NT, preferred_element_type=F32) for vj in vv], axis=0)
                ds = p * (dp - jnp.sum(p * dp, axis=0, keepdims=True))
                db_ref[hh] += ds
                return p.astype(BF16), ds.astype(BF16), qs, do, kn

            def operand_grads(pb, dsb, qs, do, kn):
                dq.append(sum(lax.dot_general(dsb[_key_block(j), :], kn[j], TN, preferred_element_type=F32)
                              for j in range(3)))
                for j in range(3):
                    dv[j].append(jnp.dot(pb[_key_block(j), :], do, preferred_element_type=F32))
                    dk[j].append(jnp.dot(dsb[_key_block(j), :], qs, preferred_element_type=F32))

            ready = softmax_grad(0)
            for hh in range(1, HEADS_PER_STEP):
                following = softmax_grad(hh)
                operand_grads(*ready)
                ready = following
            operand_grads(*ready)
            aq_ref[i % 3] = jnp.concatenate(dq, axis=1)
            for j in range(3):
                slot = (i + 1 + j) % 3
                if j < 2:
                    ak_ref[slot] += jnp.concatenate(dk[j], axis=1)
                    av_ref[slot] += jnp.concatenate(dv[j], axis=1)
                else:
                    ak_ref[slot] = jnp.concatenate(dk[j], axis=1)
                    av_ref[slot] = jnp.concatenate(dv[j], axis=1)

        slot = (i + 1) % 3
        dp_ref[0] = aq_ref[slot].astype(dp_ref.dtype)
        dp_ref[1] = ak_ref[slot].astype(dp_ref.dtype)
        dp_ref[2] = av_ref[slot].astype(dp_ref.dtype)

    def qrow(hg, i):
        return (jnp.minimum(i, nb - 1), hg)

    return _carry(
        riders, body, name="attn_bwd", grid=(N_HEAD_GROUPS, nb + 2),
        in_specs=[pl.BlockSpec((SUPER, HEAD_COLS), qrow)]
        + _band_specs(nb, N_HEAD_GROUPS, True) + _band_specs(nb, 0, True)
        + [pl.BlockSpec((SUPER, HEAD_COLS), qrow),
           pl.BlockSpec((HEADS_PER_STEP, BAND, SUPER), lambda hg, i: (hg, 0, 0)),
           pl.BlockSpec((None, LSE_ROWS, SUPER), lambda hg, i: (hg, 0, jnp.minimum(i, nb - 1)))],
        out_specs=[pl.BlockSpec((3, SUPER, HEAD_COLS), lambda hg, i: (0, jnp.maximum(i - 2, 0), hg)),
                   pl.BlockSpec((HEADS_PER_STEP, BAND, SUPER), lambda hg, i: (hg, 0, 0))],
        out_shape=[jax.ShapeDtypeStruct((6, s_len, D_MODEL), BF16),
                   jax.ShapeDtypeStruct((N_HEADS, BAND, SUPER), F32)],
        scratch_shapes=[pltpu.VMEM((3, SUPER, HEAD_COLS), F32)] * 3,
        semantics=("parallel", "arbitrary"),
    )(qkn, qkn, qkn, qkn, v, v, v, d_out, bias, lse)


def _qk_norm_bwd(dproj6, qk_raw, gq, gk):
    s_len = qk_raw.shape[0]
    ts = min(1024, s_len)

    nsteps = s_len // ts
    half = D_MODEL // 2

    def body(d_ref, raw_ref, gq_ref, gk_ref, o_ref, dgq_ref, dgk_ref, acc_ref):
        step = pl.program_id(0)

        @pl.when(step == 0)
        def _():
            acc_ref[...] = jnp.zeros_like(acc_ref)

        for piece, (g_ref, scale) in enumerate(((gq_ref, QK_SCALE), (gk_ref, 1.0))):
            for c0 in (0, half):
                xhat, r = _head_unit(raw_ref[:, piece * D_MODEL + c0:piece * D_MODEL + c0 + half].astype(F32))
                dn = d_ref[piece, :, c0:c0 + half].astype(F32) * scale
                u = dn * g_ref[...]
                dx = r * (u - xhat * (_head_sums(u * xhat) * (1.0 / HEAD_DIM)))
                o_ref[piece, :, c0:c0 + half] = dx.astype(o_ref.dtype)
                acc_ref[piece:piece + 1, c0:c0 + half] += jnp.sum(dn * xhat, axis=0, keepdims=True)

        @pl.when(step == nsteps - 1)
        def _():
            lane = lax.broadcasted_iota(jnp.int32, (D_MODEL, LANE), 0) % HEAD_DIM
            fold = (lane == lax.broadcasted_iota(jnp.int32, (D_MODEL, LANE), 1)).astype(BF16)
            tot = sum(jnp.dot(p, fold, preferred_element_type=F32) for p in _split3(acc_ref[...]))
            dgq_ref[...] = tot[0:1, :HEAD_DIM]
            dgk_ref[...] = tot[1:2, :HEAD_DIM]

    gain = pl.BlockSpec((1, half), lambda i: (0, 0))
    small = pl.BlockSpec((1, HEAD_DIM), lambda i: (0, 0))
    per_head = lambda g: jnp.tile(g, (1, half // HEAD_DIM))
    return pl.pallas_call(
        body, name="qk_norm_bwd", grid=(nsteps,),
        in_specs=[pl.BlockSpec((2, ts, D_MODEL), lambda i: (0, i, 0)),
                  pl.BlockSpec((ts, 2 * D_MODEL), lambda i: (i, 0)), gain, gain],
        out_specs=[pl.BlockSpec((2, ts, D_MODEL), lambda i: (0, i, 0)), small, small],
        out_shape=[jax.ShapeDtypeStruct(dproj6.shape, dproj6.dtype),
                   jax.ShapeDtypeStruct((1, HEAD_DIM), F32), jax.ShapeDtypeStruct((1, HEAD_DIM), F32)],
        scratch_shapes=[pltpu.VMEM((8, D_MODEL), F32)],
        input_output_aliases={0: 0},
        compiler_params=_params("arbitrary"),
    )(dproj6, qk_raw, per_head(gq), per_head(gk))


CONV_ROWS = 512
HALO = 16


def _rows_with_halo(ref, r0, n, front, s_len):
    zeros = jnp.zeros((HALO, ref.shape[1]), F32)
    if front:
        return (jnp.concatenate([zeros, ref[0:n, :].astype(F32)], axis=0) if r0 == 0
                else ref[r0 - HALO:r0 + n, :].astype(F32))
    return (jnp.concatenate([ref[r0:r0 + n, :].astype(F32), zeros], axis=0) if r0 + n == s_len
            else ref[r0:r0 + n + HALO, :].astype(F32))


def _earlier(ext, k):
    return pltpu.roll(ext, k, 0)[HALO:]


def _later(ext, k):
    n = ext.shape[0]
    return pltpu.roll(ext, n - k, 0)[:n - HALO]


def _conv_cols(col0):
    return lambda s_len: pl.BlockSpec((s_len, LANE), lambda j: (0, col0 + j))


def _conv_fwd(proj, conv_w, conv_b, riders=()):
    s_len = proj.shape[0]

    def body(bg_ref, cg_ref, xc_ref, w_ref, b_ref, o_ref):
        w = [w_ref[t:t + 1, :] for t in range(3)]
        for r0 in range(0, s_len, CONV_ROWS):
            u = _rows_with_halo(cg_ref, r0, CONV_ROWS, True, s_len) * \
                _rows_with_halo(xc_ref, r0, CONV_ROWS, True, s_len)
            conv = b_ref[...] + w[0] * _earlier(u, 2) + w[1] * _earlier(u, 1) + w[2] * u[HALO:]
            o_ref[r0:r0 + CONV_ROWS, :] = (bg_ref[r0:r0 + CONV_ROWS, :].astype(F32) * conv).astype(o_ref.dtype)

    return _carry(
        riders, body, name="conv_fwd", grid=(D_MODEL // LANE,),
        in_specs=[_conv_cols(0)(s_len), _conv_cols(8)(s_len), _conv_cols(16)(s_len),
                  pl.BlockSpec((3, LANE), lambda j: (0, j)), pl.BlockSpec((1, LANE), lambda j: (0, j))],
        out_specs=pl.BlockSpec((s_len, LANE), lambda j: (0, j)),
        out_shape=jax.ShapeDtypeStruct((s_len, D_MODEL), BF16),
        semantics=("parallel",),
    )(proj, proj, proj, conv_w, conv_b)


def _conv_bwd(dproj6, dy, proj, conv_w, conv_b, riders=()):
    s_len = proj.shape[0]

    def body(dy_ref, bg_ref, cg_ref, xc_ref, w_ref, b_ref, _, dp_ref, dw_ref):
        w = [w_ref[t:t + 1, :] for t in range(3)]
        acc = [jnp.zeros((1, LANE), F32) for _ in range(4)]
        for r0 in range(0, s_len, CONV_ROWS):
            rows = slice(r0, r0 + CONV_ROWS)
            u = _rows_with_halo(cg_ref, r0, CONV_ROWS, True, s_len) * \
                _rows_with_halo(xc_ref, r0, CONV_ROWS, True, s_len)
            u2, u1, u0 = _earlier(u, 2), _earlier(u, 1), u[HALO:]
            conv = b_ref[...] + w[0] * u2 + w[1] * u1 + w[2] * u0
            dp_ref[0, rows, :] = (dy_ref[rows, :].astype(F32) * conv).astype(dp_ref.dtype)
            dconv_ext = _rows_with_halo(dy_ref, r0, CONV_ROWS, False, s_len) * \
                _rows_with_halo(bg_ref, r0, CONV_ROWS, False, s_len)
            dconv = dconv_ext[:CONV_ROWS]
            for t, term in enumerate([dconv * u2, dconv * u1, dconv * u0, dconv]):
                acc[t] = acc[t] + jnp.sum(term, axis=0, keepdims=True)
            du = w[2] * dconv + w[1] * _later(dconv_ext, 1) + w[0] * _later(dconv_ext, 2)
            dp_ref[1, rows, :] = (du * xc_ref[rows, :].astype(F32)).astype(dp_ref.dtype)
            dp_ref[2, rows, :] = (du * cg_ref[rows, :].astype(F32)).astype(dp_ref.dtype)
        dw_ref[...] = jnp.zeros_like(dw_ref)
        for t in range(4):
            dw_ref[t:t + 1, :] = acc[t]

    return _carry(
        riders, body, name="conv_bwd", grid=(D_MODEL // LANE,),
        in_specs=[pl.BlockSpec((s_len, LANE), lambda j: (0, j)),
                  _conv_cols(0)(s_len), _conv_cols(8)(s_len), _conv_cols(16)(s_len),
                  pl.BlockSpec((3, LANE), lambda j: (0, j)), pl.BlockSpec((1, LANE), lambda j: (0, j)),
                  pl.BlockSpec(memory_space=pl.ANY)],
        out_specs=[pl.BlockSpec((3, s_len, LANE), lambda j: (1, 0, j)),
                   pl.BlockSpec((8, LANE), lambda j: (0, j))],
        out_shape=[jax.ShapeDtypeStruct(dproj6.shape, dproj6.dtype),
                   jax.ShapeDtypeStruct((8, D_MODEL), F32)],
        input_output_aliases={6: 0},
        semantics=("parallel",),
    )(dy, proj, proj, proj, conv_w, conv_b, dproj6)


class _NoComm:
    def __init__(self, early, late):
        self.early, self.late = early, late

    def riders(self, at):
        return ()

    def done(self, at, carried):
        pass

    def early_weights(self):
        return self.early

    def late_weights(self):
        return self.late

    def grads_ready(self, group, grads):
        pass


def _d_input(name, pieces, w3, extra=None, riders=()):
    n_piece, s_len, width = pieces.shape
    _, kdim, per_shard = w3.shape
    tm, chunk = 256, 512

    def body(p_ref, w_ref, *refs):
        o_ref = refs[-1]
        acc = refs[0][...] if extra is not None else jnp.zeros(o_ref.shape, F32)
        for c0 in range(0, n_piece * width, chunk):
            acc = acc + lax.dot_general(p_ref[c0 // width, :, c0 % width:c0 % width + chunk],
                                        w_ref[c0 // per_shard, :, c0 % per_shard:c0 % per_shard + chunk],
                                        NT, preferred_element_type=F32)
        o_ref[...] = acc

    tile = pl.BlockSpec((tm, kdim), lambda i: (i, 0))
    return _carry(
        riders, body, name=name, grid=(s_len // tm,),
        in_specs=[pl.BlockSpec((n_piece, tm, width), lambda i: (0, i, 0)),
                  pl.BlockSpec(w3.shape, lambda i: (0, 0, 0))] + ([tile] if extra is not None else []),
        out_specs=tile, out_shape=jax.ShapeDtypeStruct((s_len, kdim), F32),
        semantics=("parallel",),
    )(pieces, w3, *([extra] if extra is not None else []))


def _local_grads(x, target, norm1_g, q_norm_g, k_norm_g, rel_bias, conv_b, b_gate, norm2_g, comm):
    s_len = x.shape[0]
    tm = min(1024, s_len)
    row = lambda v: v.reshape(1, -1)

    def carrying(at, fn, *args, **kw):
        riders = comm.riders(at)
        out = fn(*args, riders=riders, **kw)
        if riders:
            out, carried = out
            comm.done(at, carried)
        return out

    bias = carrying("bias_expand", _bias_expand, rel_bias)
    h = carrying("norm1", _rms_fwd, "norm1", x, row(norm1_g))
    w_in3, w_gate3, conv_w = comm.early_weights()
    gq, gk = row(q_norm_g), row(k_norm_g)

    def qk_epi(acc, e, o):
        o[0][...] = acc.astype(BF16)
        gain = jnp.where(pl.program_id(1) < 2, e[0] * QK_SCALE, e[1])
        o[1][...] = (_head_unit(acc)[0] * gain).astype(BF16)
    small = pl.BlockSpec((1, 512), lambda i, j, k: (0, 0))
    per_head = lambda g: jnp.tile(g, (1, 512 // HEAD_DIM))
    qk_raw, qkn = _mm_fwd("proj_qk", h, w_in3, tm, 512, D_MODEL, ncols=2 * D_MODEL, epilogue=qk_epi,
                          outs=[_out2d(s_len, 2 * D_MODEL, BF16, tm, 512)] * 2,
                          extras=[(per_head(gq), small), (per_head(gk), small)])
    v = _mm_fwd("proj_v", h, w_in3, tm, 512, D_MODEL, col0=4, ncols=D_MODEL,
                outs=[_out2d(s_len, D_MODEL, BF16, tm, 512)])
    conv_in = _mm_fwd("proj_conv", h, w_in3, tm, 1536, D_MODEL, col0=2, ncols=3 * D_MODEL,
                      outs=[_out2d(s_len, 3 * D_MODEL, BF16, tm, 1536)])

    def gate_epi(acc, e, o):
        o[0][...] = jax.nn.sigmoid(acc + e[0]).astype(BF16)
    gates = _mm_fwd("gates", h, w_gate3, tm, 512, D_MODEL, epilogue=gate_epi,
                    outs=[_out2d(s_len, 2 * D_MODEL, BF16, tm, 512)],
                    extras=[(row(b_gate), pl.BlockSpec((1, 512), lambda i, j, k: (0, j)))])

    attn, lse = carrying("attn_fwd", _attn_fwd, qkn, v, bias)
    yconv = carrying("conv_fwd", _conv_fwd, conv_in, conv_w, row(conv_b))
    w_ap, w_cp, w_out, w_up3, w_down = comm.late_weights()
    tw = 1024
    ya = _mm_fwd("attn_proj", attn, w_ap, tm, tw, D_MODEL, outs=[_out2d(s_len, D_MODEL, BF16, tm, tw)])

    def merge_epi(acc, e, o):
        ya_v, ga, gc = [t.astype(F32) for t in e]
        o[0][...] = acc.astype(BF16)
        o[1][...] = (ga * ya_v + gc * acc).astype(BF16)
    gate_a, gate_c = (gates, _tile_spec(tm, tw, 0)), (gates, _tile_spec(tm, tw, 1))
    yc, merged = _mm_fwd("conv_proj", yconv, w_cp, tm, tw, D_MODEL, epilogue=merge_epi,
                         outs=[_out2d(s_len, D_MODEL, BF16, tm, tw), _out2d(s_len, D_MODEL, BF16, tm, tw)],
                         extras=[(ya, _tile_spec(tm, tw)), gate_a, gate_c])

    def res_epi(acc, e, o):
        o[0][...] = e[0] + acc
    x1 = _mm_fwd("out_proj", merged, w_out, tm, tw, D_MODEL, epilogue=res_epi,
                 extras=[(x, _tile_spec(tm, tw))])
    h2 = _rms_fwd("norm2", x1, row(norm2_g))

    def up_epi(acc, e, o):
        o[0][...] = acc.astype(BF16)
        o[1][...] = jnp.square(jnp.maximum(acc, 0.0)).astype(BF16)
    up, act = _mm_fwd("mlp_up", h2, w_up3, tm, tw, D_MODEL, epilogue=up_epi,
                      outs=[_out2d(s_len, D_FF, BF16, tm, tw), _out2d(s_len, D_FF, BF16, tm, tw)])

    def loss_epi(acc, e, o):
        err = e[0] + acc - e[1]
        dy = err * (1.0 / D_MODEL)
        o[0][...] = dy
        o[1][...] = dy.astype(BF16)
        sq = err * err
        part = sq[:, 0:LANE]
        for c0 in range(LANE, D_MODEL, LANE):
            part = part + sq[:, c0:c0 + LANE]
        o[2][...] = jnp.sum(part.reshape(tl // 8, 8, LANE), axis=0)
    tl = 512
    dy, dy_b, loss_part = _mm_fwd(
        "mlp_down", act, w_down, tl, D_MODEL, D_FF, epilogue=loss_epi,
        outs=[_out2d(s_len, D_MODEL, F32, tl, D_MODEL), _out2d(s_len, D_MODEL, BF16, tl, D_MODEL),
              (jax.ShapeDtypeStruct((8 * (s_len // tl), LANE), F32), pl.BlockSpec((8, LANE), lambda i, j, k: (i, 0)))],
        extras=[(x1, _tile_spec(tl, D_MODEL)), (target, _tile_spec(tl, D_MODEL))])

    def dup_epi(acc, e, o):
        o[0][...] = (acc * (2.0 * jnp.maximum(e[0].astype(F32), 0.0))).astype(BF16)
    full = lambda cols: pl.BlockSpec((tm, cols), lambda i, j, n: (i, n))
    tokens = lambda cols: pl.BlockSpec((s_len, cols), lambda i, j, m: (m, j))
    dup = _mm_bwd_x("d_act", dy_b, full(D_MODEL), w_down, tm, tw, D_MODEL, s_len, D_MODEL, epilogue=dup_epi,
                    outs=[_out2d(s_len, D_FF, BF16, tm, tw)], extras=[(up, _tile_spec(tm, tw))])
    g_down = _mm_bwd_w("g_down", act, dy_b, tokens(D_MODEL), D_MODEL, 512, D_MODEL, s_len, False)
    g_up = _mm_bwd_w("g_up", h2, dup, tokens(512), D_FF, D_MODEL, 512, s_len, True)
    comm.grads_ready("mlp", dict(w_down=g_down, w_up=g_up))
    dh2 = carrying("d_h2", _mm_bwd_x, "d_h2", dup, full(1024), w_up3, tm, tw, 1024, s_len, D_FF)
    dx1, dx1_b, dg2 = _rms_bwd("norm2_bwd", dh2, x1, row(norm2_g), dy)

    def dmerge_epi(acc, e, o):
        ya_v, yc_v, ga, gc = [t.astype(F32) for t in e]
        o[0][...] = (acc * ga).astype(BF16)
        o[1][...] = (acc * gc).astype(BF16)
        o[2][0] = (acc * ya_v * ga * (1.0 - ga)).astype(BF16)
        o[2][1] = (acc * yc_v * gc * (1.0 - gc)).astype(BF16)
    dya, dyc, dgp2 = _mm_bwd_x(
        "d_merged", dx1_b, full(D_MODEL), w_out, tm, tw, D_MODEL, s_len, D_MODEL, epilogue=dmerge_epi,
        outs=[_out2d(s_len, D_MODEL, BF16, tm, tw), _out2d(s_len, D_MODEL, BF16, tm, tw),
              (jax.ShapeDtypeStruct((2, s_len, D_MODEL), BF16), pl.BlockSpec((2, tm, tw), lambda i, j, n: (0, i, j)))],
        extras=[(ya, _tile_spec(tm, tw)), (yc, _tile_spec(tm, tw)), gate_a, gate_c])
    g_out = _mm_bwd_w("g_out", merged, dx1_b, tokens(512), D_MODEL, D_MODEL, 512, s_len, False)
    d_attn = _mm_bwd_x("d_attn", dya, full(D_MODEL), w_ap, tm, tw, D_MODEL, s_len, D_MODEL,
                       outs=[_out2d(s_len, D_MODEL, BF16, tm, tw)])
    g_ap = _mm_bwd_w("g_attn_proj", attn, dya, tokens(512), D_MODEL, D_MODEL, 512, s_len, False)
    d_yconv = _mm_bwd_x("d_yconv", dyc, full(D_MODEL), w_cp, tm, tw, D_MODEL, s_len, D_MODEL,
                        outs=[_out2d(s_len, D_MODEL, BF16, tm, tw)])
    g_cp = _mm_bwd_w("g_conv_proj", yconv, dyc, tokens(512), D_MODEL, D_MODEL, 512, s_len, False)
    comm.grads_ready("proj", dict(w_out=g_out, w_attn_proj=g_ap, w_conv_proj=g_cp))

    dproj6, dbias = carrying("attn_bwd", _attn_bwd, qkn, v, d_attn, bias, lse)
    dproj6, dgq, dgk = _qk_norm_bwd(dproj6, qk_raw, gq, gk)
    dproj6, dconv_wb = carrying("conv_bwd", _conv_bwd, dproj6, d_yconv, conv_in, conv_w, row(conv_b))
    d_rel = _bias_reduce(dbias)

    piece = lambda width: (lambda blk: (blk * width) // D_MODEL, lambda blk: (blk * width % D_MODEL) // width)
    pc, cb = piece(512)
    pieces = pl.BlockSpec((None, s_len, 512), lambda i, j, m: (pc(j), m, cb(j)))
    g_in = carrying("g_in", _mm_bwd_w, "g_in", h, dproj6, pieces, 6 * D_MODEL, D_MODEL, 512, s_len, True)
    g_gate = carrying("g_gate", _mm_bwd_w, "g_gate", h, dgp2, pieces, 2 * D_MODEL, D_MODEL, 512, s_len, True)
    comm.grads_ready("in", dict(w_in=g_in, w_gate=g_gate))
    dh_g = carrying("d_h_gate", _d_input, "d_h_gate", dgp2, w_gate3)
    dh = carrying("d_h_proj", _d_input, "d_h_proj", dproj6, w_in3, extra=dh_g)
    grad_x, _, dg1 = carrying("norm1_bwd", _rms_bwd, "norm1_bwd", dh, x, row(norm1_g), dx1)

    def bsum(t, f):
        return [], [t[0].astype(F32), t[1].astype(F32)]
    ts = 512
    db_a, db_c = carrying("b_gate_sum", _ew, "b_gate_sum", bsum,
                          [(dgp2, pl.BlockSpec((None, ts, D_MODEL), lambda i: (0, i, 0))),
                           (dgp2, pl.BlockSpec((None, ts, D_MODEL), lambda i: (1, i, 0)))],
                          [], [], sums=[D_MODEL, D_MODEL], ts=ts)

    big = dict(w_in=g_in, w_attn_proj=g_ap, w_conv_proj=g_cp, w_gate=g_gate, w_out=g_out,
               w_up=g_up, w_down=g_down)
    small = dict(norm1_g=dg1, norm2_g=dg2, conv_wb=dconv_wb, b_gate=(db_a, db_c),
                 q_norm_g=dgq, k_norm_g=dgk, rel_bias=d_rel)
    return loss_part, grad_x, big, small


def _finish_loss(loss_part):
    def body(l_ref, lo_ref):
        total = jnp.sum(jnp.sum(l_ref[...], axis=0, keepdims=True), axis=1, keepdims=True)
        lo_ref[...] = jnp.broadcast_to(total * (0.5 / D_MODEL), lo_ref.shape)

    return pl.pallas_call(body, name="finish_loss", out_shape=jax.ShapeDtypeStruct((8, LANE), F32))(loss_part)


HBM_SPEC = pl.BlockSpec(memory_space=pl.ANY)


def _place():
    return lax.axis_index("x"), lax.axis_index("y"), lax.axis_index("c")


def _other_chips(x, y):
    return [(1 - x, y), (x, 1 - y), (1 - x, 1 - y)]


def _cast_into_slot(name, where, w):
    r, cols = w.shape
    ts = 256

    def body(w_ref, x_ref, o_ref):
        o_ref[...] = x_ref[...].astype(o_ref.dtype)

    return pl.pallas_call(
        body, name=name,
        grid_spec=pltpu.PrefetchScalarGridSpec(
            num_scalar_prefetch=1, grid=(r // ts,),
            in_specs=[pl.BlockSpec((ts, cols), lambda i, w: (i, 0))],
            out_specs=pl.BlockSpec((None, ts, cols), lambda i, w: (w[0], i, 0))),
        out_shape=jax.ShapeDtypeStruct((N_SHARD, r, cols), BF16),
        compiler_params=_params("parallel"),
    )(where, w)


def _remote(src, dst, send, recv, k, to):
    return pltpu.make_async_remote_copy(src_ref=src, dst_ref=dst, send_sem=send.at[k], recv_sem=recv.at[k],
                                        device_id=to, device_id_type=MESH)


def _gather_riders(slots):
    n = len(slots)

    def half(refs, w, shard, which):
        hr = slots[w].shape[1] // 2
        return refs[w].at[shard, pl.ds(which * hr, hr)]

    def each(fn):
        x, y, c = _place()
        for w in range(n):
            for j, chip in enumerate(_other_chips(x, y)):
                fn(w, 3 * w + j, 2 * x + y, 2 * chip[0] + chip[1], c, (*chip, c), (x, y, 1 - c))

    def chips_start(_, refs, send, recv):
        each(lambda w, k, me, them, c, peer, sib: _remote(half(refs, w, me, c), half(refs, w, me, c), send, recv, k, peer).start())

    def chips_finish(_, refs, send, recv):
        each(lambda w, k, me, them, c, peer, sib: _remote(half(refs, w, them, c), half(refs, w, them, c), send, recv, k, peer).wait_recv())
        each(lambda w, k, me, them, c, peer, sib: _remote(half(refs, w, me, c), half(refs, w, me, c), send, recv, k, peer).wait_send())

    def sibling_start(_, refs, send, recv):
        each(lambda w, k, me, them, c, peer, sib: _remote(half(refs, w, them, c), half(refs, w, them, c), send, recv, k, sib).start())

    def sibling_finish(_, refs, send, recv):
        each(lambda w, k, me, them, c, peer, sib: _remote(half(refs, w, them, 1 - c), half(refs, w, them, 1 - c), send, recv, k, sib).wait_recv())
        each(lambda w, k, me, them, c, peer, sib: _remote(half(refs, w, them, c), half(refs, w, them, c), send, recv, k, sib).wait_send())

    return (lambda s: _Rider([], s, 3 * n, chips_start, chips_finish),
            lambda s: _Rider([], s, 3 * n, sibling_start, sibling_finish))


def _pair_exchange_rider(grads, landing):
    n = len(grads)

    def copies(srcs, dsts, send, recv):
        x, y, c = _place()
        out = []
        for w in range(n):
            hr = grads[w].shape[1] // 2
            out.append(_remote(srcs[w].at[:, pl.ds((1 - c) * hr, hr)], dsts[w], send, recv, w, (x, y, 1 - c)))
        return out

    def start(srcs, dsts, send, recv):
        for cp in copies(srcs, dsts, send, recv):
            cp.start()

    def finish(srcs, dsts, send, recv):
        for cp in copies(srcs, dsts, send, recv):
            cp.wait()

    return _Rider(grads, landing, n, start, finish)


def _row_tile(hr):
    return min(hr, 256)


def _pair_add(name, where, grad, got):
    _, hr, cols = got.shape
    tr = _row_tile(hr)
    nblk = hr // tr

    def body(w_ref, g_ref, r_ref, o_ref):
        o_ref[...] = (g_ref[...] + r_ref[...]).astype(o_ref.dtype)

    return pl.pallas_call(
        body, name=name,
        grid_spec=pltpu.PrefetchScalarGridSpec(
            num_scalar_prefetch=1, grid=(N_SHARD, nblk),
            in_specs=[pl.BlockSpec((None, tr, cols), lambda s, i, w: (s, w[1] * nblk + i, 0)),
                      pl.BlockSpec((None, tr, cols), lambda s, i, w: (s, i, 0))],
            out_specs=pl.BlockSpec((None, tr, cols), lambda s, i, w: (s, i, 0))),
        out_shape=jax.ShapeDtypeStruct(got.shape, BF16),
        compiler_params=_params("parallel", "parallel"),
    )(where, grad, got)


def _chip_exchange_rider(partials, landing):
    n = len(partials)

    def copies(srcs, dsts, send, recv):
        x, y, c = _place()
        return [_remote(srcs[w].at[2 * chip[0] + chip[1]], dsts[w].at[j], send, recv, 3 * w + j, (*chip, c))
                for w in range(n) for j, chip in enumerate(_other_chips(x, y))]

    def start(srcs, dsts, send, recv):
        for cp in copies(srcs, dsts, send, recv):
            cp.start()

    def finish(srcs, dsts, send, recv):
        for cp in copies(srcs, dsts, send, recv):
            cp.wait()

    return _Rider(partials, landing, 3 * n, start, finish)


def _final_add(name, where, grad, got, arrived):
    _, hr, cols = got.shape
    tr = _row_tile(hr)
    nblk = hr // tr

    def body(w_ref, g_ref, r_ref, a_ref, o_ref):
        acc = g_ref[...] + r_ref[...]
        for j in range(3):
            acc = acc + a_ref[j].astype(F32)
        o_ref[...] = acc

    return pl.pallas_call(
        body, name=name,
        grid_spec=pltpu.PrefetchScalarGridSpec(
            num_scalar_prefetch=1, grid=(nblk,),
            in_specs=[pl.BlockSpec((None, tr, cols), lambda i, w: (w[0], w[1] * nblk + i, 0)),
                      pl.BlockSpec((None, tr, cols), lambda i, w: (w[0], i, 0)),
                      pl.BlockSpec((3, tr, cols), lambda i, w: (0, i, 0))],
            out_specs=pl.BlockSpec((tr, cols), lambda i, w: (w[1] * nblk + i, 0))),
        out_shape=jax.ShapeDtypeStruct((2 * hr, cols), F32),
        compiler_params=_params("parallel"),
    )(where, grad, got, arrived)


def _pair_share_rider(shards):
    n = len(shards)

    def half(refs, w, which):
        hr = shards[w].shape[0] // 2
        return refs[w].at[pl.ds(which * hr, hr)]

    def start(_, refs, send, recv):
        x, y, c = _place()
        for w in range(n):
            _remote(half(refs, w, c), half(refs, w, c), send, recv, w, (x, y, 1 - c)).start()

    def finish(_, refs, send, recv):
        x, y, c = _place()
        for w in range(n):
            _remote(half(refs, w, 1 - c), half(refs, w, 1 - c), send, recv, w, (x, y, 1 - c)).wait_recv()
        for w in range(n):
            _remote(half(refs, w, c), half(refs, w, c), send, recv, w, (x, y, 1 - c)).wait_send()

    return _Rider([], shards, n, start, finish)


class _Exchange:
    PLAN = {"bias_expand": [("early", "gather")], "norm1": [("early", "forward")],
            "attn_fwd": [("late", "gather")], "conv_fwd": [("late", "forward")],
            "d_h2": [("mlp", "pair")], "attn_bwd": [("mlp", "chips"), ("proj", "pair")], "conv_bwd": [("mlp", "share")],
            "g_in": [("proj", "chips")], "g_gate": [("proj", "share")],
            "d_h_gate": [("in", "pair")], "d_h_proj": [("in", "chips")], "b_gate_sum": [("in", "share")]}

    def __init__(self, where, early_slots, late_slots):
        self.where = where
        self.slots = dict(early=early_slots, late=late_slots)
        self.stage = {g: dict(zip(("gather", "forward"), _gather_riders(s))) for g, s in self.slots.items()}
        self.groups, self.reduced, self.pending = {}, {}, []

    def early_weights(self):
        w_in3, w_gate3, small = self.slots["early"]
        conv_w = small[:, :3, :].transpose(1, 0, 2).reshape(3, N_SHARD * small.shape[2])
        return w_in3, w_gate3, conv_w

    def late_weights(self):
        rows = lambda a: a.reshape(a.shape[0] * a.shape[1], a.shape[2])
        w_ap, w_cp, w_out, w_up3, w_down = self.slots["late"]
        return rows(w_ap), rows(w_cp), rows(w_out), w_up3, rows(w_down)

    def grads_ready(self, group, grads):
        names = list(grads)
        g4 = [g if g.ndim == 3 else g.reshape(N_SHARD, -1, g.shape[1]) for g in grads.values()]
        self.groups[group] = dict(names=names, g4=g4)

    def riders(self, at):
        self.pending = self.PLAN.get(at, [])
        out = []
        for group, stage in self.pending:
            if group in self.slots:
                out.append(self.stage[group][stage](self.slots[group]))
                continue
            st = self.groups[group]
            if stage == "pair":
                landing = [lax.empty((N_SHARD, g.shape[1] // 2, g.shape[2]), F32) for g in st["g4"]]
                out.append(_pair_exchange_rider(st["g4"], landing))
            elif stage == "chips":
                landing = [lax.empty((3,) + p.shape[1:], p.dtype) for p in st["partial"]]
                out.append(_chip_exchange_rider(st["partial"], landing))
            else:
                out.append(_pair_share_rider(st["halves"]))
        return out

    def done(self, at, carried):
        for (group, stage), arrays in zip(self.pending, carried):
            if group in self.slots:
                self.slots[group] = arrays
                continue
            st = self.groups[group]
            tag = lambda what, n: what + "_" + n
            if stage == "pair":
                st["got"] = arrays
                st["partial"] = [_pair_add(tag("pair_add", n), self.where, g, r)
                                 for n, g, r in zip(st["names"], st["g4"], arrays)]
            elif stage == "chips":
                st["halves"] = [_final_add(tag("final_add", n), self.where, g, r, a)
                                for n, g, r, a in zip(st["names"], st["g4"], st["got"], arrays)]
            else:
                self.reduced.update(zip(st["names"], arrays))


SMALL_ROWS = 32
N_DEV = 8


def _all_reduce_small(pack):
    def body(p_ref, o_ref, buf, send, recv):
        x, y, c = _place()
        buf[4 * x + 2 * y + c] = p_ref[...]
        copies, waits = [], []
        for k in range(1, N_DEV):
            px = 1 - x if k & 4 else x
            py = 1 - y if k & 2 else y
            pc = 1 - c if k & 1 else c
            copies.append(pltpu.make_async_remote_copy(
                src_ref=p_ref, dst_ref=buf.at[4 * x + 2 * y + c], send_sem=send.at[k - 1],
                recv_sem=recv.at[k - 1], device_id=(px, py, pc), device_id_type=MESH))
            waits.append(pltpu.make_async_remote_copy(
                src_ref=p_ref, dst_ref=buf.at[4 * px + 2 * py + pc], send_sem=send.at[k - 1],
                recv_sem=recv.at[k - 1], device_id=(px, py, pc), device_id_type=MESH))
        for cp in copies:
            cp.start()
        for cp in waits:
            cp.wait_recv()
        acc = buf[0]
        for d in range(1, N_DEV):
            acc = acc + buf[d]
        o_ref[...] = acc
        for cp in copies:
            cp.wait_send()

    return pl.pallas_call(
        body, name="all_reduce_small",
        out_shape=jax.ShapeDtypeStruct(pack.shape, F32),
        scratch_shapes=[pltpu.VMEM((N_DEV,) + pack.shape, F32),
                        pltpu.SemaphoreType.DMA((N_DEV - 1,)), pltpu.SemaphoreType.DMA((N_DEV - 1,))],
    )(pack)


def _adamw(name, w, g, m, v):
    c1 = 1.0 - ADAM_B1 ** ADAM_STEP
    c2 = 1.0 - ADAM_B2 ** ADAM_STEP

    def fn(t, f):
        wv, gv, mv, vv = t
        m2 = ADAM_B1 * mv + (1.0 - ADAM_B1) * gv
        v2 = ADAM_B2 * vv + (1.0 - ADAM_B2) * (gv * gv)
        delta = -ADAM_LR * ((m2 / c1) / (jnp.sqrt(v2 / c2) + ADAM_EPS) + ADAM_WD * wv)
        return [delta, m2, v2], []

    cols = w.shape[1]
    return _ew(name, fn, [w, g, m, v], [], [(cols, F32)] * 3, ts=min(w.shape[0], 256))


LOSS_ROW = 26


def _pack_small(norm1_g, norm2_g, conv_b, b_gate, conv_w, q_norm_g, k_norm_g, rel_bias, loss=None):
    pack = jnp.zeros((SMALL_ROWS, D_MODEL), F32)
    for r0, v in ((0, norm1_g), (1, norm2_g), (2, conv_b), (3, b_gate.reshape(2, D_MODEL)), (5, conv_w),
                  (8, q_norm_g), (9, k_norm_g), (10, rel_bias)) + (((LOSS_ROW, loss),) if loss is not None else ()):
        v = v.reshape(-1, v.shape[-1]).astype(F32)
        pack = pack.at[r0:r0 + v.shape[0], :v.shape[1]].set(v)
    return pack


def _unpack_small(pack, conv_cols):
    return dict(norm1_g=pack[0], norm2_g=pack[1], conv_b=pack[2], b_gate=pack[3:5].reshape(2 * D_MODEL),
                conv_w=pack[5:8, :conv_cols], q_norm_g=pack[8, :HEAD_DIM], k_norm_g=pack[9, :HEAD_DIM],
                rel_bias=pack[10:10 + N_HEADS, :N_REL])


BIG = ["w_in", "w_attn_proj", "w_conv_proj", "w_gate", "w_out", "w_up", "w_down"]
LATE = ["w_attn_proj", "w_conv_proj", "w_out", "w_up", "w_down"]
WEIGHTS = ["norm1_g", "w_in", "q_norm_g", "k_norm_g", "rel_bias", "conv_w", "conv_b", "w_attn_proj",
           "w_conv_proj", "w_gate", "b_gate", "w_out", "norm2_g", "w_up", "w_down"]


def kernel(x, norm1_g, w_in, q_norm_g, k_norm_g, rel_bias, conv_w, conv_b, w_attn_proj, w_conv_proj, w_gate, b_gate, w_out, norm2_g, w_up, w_down, loss_target, m_norm1_g, m_w_in, m_q_norm_g, m_k_norm_g, m_rel_bias, m_conv_w, m_conv_b, m_w_attn_proj, m_w_conv_proj, m_w_gate, m_b_gate, m_w_out, m_norm2_g, m_w_up, m_w_down, v_norm1_g, v_w_in, v_q_norm_g, v_k_norm_g, v_rel_bias, v_conv_w, v_conv_b, v_w_attn_proj, v_w_conv_proj, v_w_gate, v_b_gate, v_w_out, v_norm2_g, v_w_up, v_w_down):
    given = dict(locals())
    w = {n: given[n] for n in WEIGHTS}
    m = {n: given["m_" + n] for n in WEIGHTS}
    v = {n: given["v_" + n] for n in WEIGHTS}
    s_len = x.shape[1]
    shard = 2 * lax.axis_index("x") + lax.axis_index("y")
    where = jnp.stack([shard, lax.axis_index("c")]).astype(jnp.int32)
    conv_cols = conv_w.shape[1]

    small_in = lax.dynamic_update_slice(jnp.zeros((N_SHARD, 16, conv_cols), F32), conv_w[None], (shard, 0, 0))
    slot = {n: _cast_into_slot("cast_" + n, where, w[n]) for n in BIG}
    comm = _Exchange(where, [slot["w_in"], slot["w_gate"], small_in], [slot[n] for n in LATE])

    loss_part, grad_x, _, small = _local_grads(
        x.reshape(s_len, D_MODEL), loss_target.reshape(s_len, D_MODEL), norm1_g, q_norm_g, k_norm_g,
        rel_bias, conv_b, b_gate, norm2_g, comm)
    grad = dict(comm.reduced)

    loss_local = _finish_loss(loss_part)
    pack = _pack_small(small["norm1_g"], small["norm2_g"], small["conv_wb"][3], jnp.concatenate(small["b_gate"], axis=1),
                       small["conv_wb"][0:3], small["q_norm_g"], small["k_norm_g"], small["rel_bias"],
                       loss=loss_local[0:1, :])
    total = _all_reduce_small(pack)
    g_small = _unpack_small(total, D_MODEL)
    g_small["conv_w"] = lax.dynamic_slice(g_small["conv_w"], (0, shard * conv_cols), (3, conv_cols))
    grad.update(g_small)

    delta, new_m, new_v = {}, {}, {}
    for n in BIG:
        delta[n], new_m[n], new_v[n] = _adamw("adamw_" + n, w[n], grad[n], m[n], v[n])
    small_names = [n for n in WEIGHTS if n not in BIG]
    packs = [_pack_small(**{n: src[n] for n in small_names}) for src in (w, grad, m, v)]
    for out, packed in zip((delta, new_m, new_v), _adamw("adamw_small", *packs)):
        out.update({n: a.reshape(w[n].shape) for n, a in _unpack_small(packed, conv_cols).items()})

    outs = [total[LOSS_ROW, 0], grad_x.reshape(x.shape)]
    for group in (grad, delta, new_m, new_v):
        outs += [group[n].reshape(w[n].shape) for n in WEIGHTS]
    return tuple(outs)
```

```python
import functools

import jax
import jax.numpy as jnp
from jax import lax
from jax.experimental import pallas as pl
from jax.experimental.pallas import tpu as pltpu

F32 = jnp.float32
BF16 = jnp.bfloat16

D_MODEL = 1024
N_HEADS = 16
HEAD_DIM = 64
CHUNK = 64
N_PREV_CHUNKS = 8
MAX_REL = 256
D_FF = 4096
N_REL = 2 * MAX_REL + 1
REL_PAD = 640
EPS = 1e-6
NEG_INF = -1e30
QK_SCALE = HEAD_DIM ** -0.5

SUPER = 4 * CHUNK
BAND = SUPER + N_PREV_CHUNKS * CHUNK
SKEW_W = 1024
N_SHARD = 4
LANE = 128
VMEM_LIMIT = 48 * 1024 * 1024

ADAM_LR = 0.001
ADAM_B1 = 0.9
ADAM_B2 = 0.999
ADAM_EPS = 1e-08
ADAM_WD = 0.01
ADAM_STEP = 10

MESH = pl.DeviceIdType.MESH
NN = (((1,), (0,)), ((), ()))
NT = (((1,), (1,)), ((), ()))
TN = (((0,), (0,)), ((), ()))


def _params(*sem):
    return pltpu.CompilerParams(dimension_semantics=sem or None, vmem_limit_bytes=VMEM_LIMIT)


HBM_SPEC = pl.BlockSpec(memory_space=pl.ANY)


class _Rider:
    def __init__(self, sources, arrays, n_sem, start, finish):
        self.sources, self.arrays, self.n_sem, self.start, self.finish = sources, arrays, n_sem, start, finish


def _carry(riders, body, *, name, out_shape, grid=(), in_specs=None, out_specs=None, scratch_shapes=(),
           semantics=(), input_output_aliases=None):
    aliases = dict(input_output_aliases or {})
    if not riders:
        kw = {} if in_specs is None else dict(in_specs=in_specs, out_specs=out_specs)
        return pl.pallas_call(body, name=name, grid=grid, out_shape=out_shape, scratch_shapes=scratch_shapes,
                              input_output_aliases=aliases, compiler_params=_params(*semantics), **kw)
    single = not isinstance(out_shape, (list, tuple))
    shapes = [out_shape] if single else list(out_shape)
    n_out, n_scr = len(shapes), len(scratch_shapes)
    srcs = [a for r in riders for a in r.sources]
    arrs = [a for r in riders for a in r.arrays]
    vmem = pl.BlockSpec(memory_space=pltpu.VMEM)

    def run(*args):
        n_in = len(args)

        def wrapped(*refs):
            pos = n_in
            src_refs = refs[pos:pos + len(srcs)]
            pos += len(srcs) + len(arrs)
            outs = refs[pos:pos + n_out]
            pos += n_out
            arr_refs = refs[pos:pos + len(arrs)]
            pos += len(arrs)
            scratch = refs[pos:pos + n_scr]
            sems = refs[pos + n_scr:]
            first, last = True, True
            for d, size in enumerate(grid):
                first = jnp.logical_and(first, pl.program_id(d) == 0)
                last = jnp.logical_and(last, pl.program_id(d) == size - 1)

            def each(method):
                s0 = a0 = 0
                for k, r in enumerate(riders):
                    getattr(r, method)(src_refs[s0:s0 + len(r.sources)], arr_refs[a0:a0 + len(r.arrays)],
                                       sems[2 * k], sems[2 * k + 1])
                    s0, a0 = s0 + len(r.sources), a0 + len(r.arrays)

            pl.when(first)(lambda: each("start"))
            body(*refs[:n_in], *outs, *scratch)
            pl.when(last)(lambda: each("finish"))

        ins = [vmem] * n_in if in_specs is None else list(in_specs)
        if out_specs is None:
            o_specs = [vmem] * n_out
        else:
            o_specs = [out_specs] if single else list(out_specs)
        for k in range(len(arrs)):
            aliases[n_in + len(srcs) + k] = n_out + k
        res = pl.pallas_call(
            wrapped, name=name, grid=grid,
            in_specs=ins + [HBM_SPEC] * (len(srcs) + len(arrs)),
            out_specs=o_specs + [HBM_SPEC] * len(arrs),
            out_shape=shapes + [jax.ShapeDtypeStruct(a.shape, a.dtype) for a in arrs],
            scratch_shapes=list(scratch_shapes) + [pltpu.SemaphoreType.DMA((r.n_sem,)) for r in riders for _ in range(2)],
            input_output_aliases=aliases,
            compiler_params=_params(*["arbitrary"] * len(grid)),
        )(*args, *srcs, *arrs)
        core, rest = res[:n_out], list(res[n_out:])
        carried, a0 = [], 0
        for r in riders:
            carried.append(rest[a0:a0 + len(r.arrays)])
            a0 += len(r.arrays)
        return (core[0] if single else core), carried

    return run


def _mm(name, dims, a, a_spec, b, b_spec, grid, tile, outs, epilogue=None, extras=(), riders=()):
    nk, ne, no = grid[2], len(extras), len(outs)

    def body(a_ref, b_ref, *refs):
        e_refs, o_refs = refs[:ne], refs[ne:ne + no]
        part = lax.dot_general(a_ref[...], b_ref[...], dims, preferred_element_type=F32)

        def finish(acc):
            if epilogue is None:
                o_refs[0][...] = acc.astype(o_refs[0].dtype)
            else:
                epilogue(acc, [r[...] for r in e_refs], o_refs)

        if nk == 1:
            finish(part)
        else:
            acc_ref = refs[ne + no]
            k = pl.program_id(2)

            @pl.when(k == 0)
            def _():
                acc_ref[...] = part

            @pl.when(k > 0)
            def _():
                acc_ref[...] += part

            @pl.when(k == nk - 1)
            def _():
                finish(acc_ref[...])

    res = _carry(
        riders, body, name=name, grid=grid,
        in_specs=[a_spec, b_spec] + [s for _, s in extras],
        out_specs=[s for _, s in outs],
        out_shape=[s for s, _ in outs],
        scratch_shapes=[pltpu.VMEM(tile, F32)] if nk > 1 else [],
        semantics=("parallel", "parallel", "arbitrary"),
    )(a, b, *[e for e, _ in extras])
    res, carried = res if riders else (res, None)
    res = res[0] if no == 1 else res
    return (res, carried) if riders else res


def _tile_spec(tm, tn, col0=0):
    return pl.BlockSpec((tm, tn), lambda i, j, k: (i, j + col0))


def _out2d(m, n, dtype, tm, tn):
    return (jax.ShapeDtypeStruct((m, n), dtype), _tile_spec(tm, tn))


def _mm_fwd(name, a, w, tm, tn, tk, outs=None, epilogue=None, extras=(), col0=0, ncols=None, riders=()):
    m, kdim = a.shape
    if w.ndim == 3:
        per = w.shape[2] // tn
        n = ncols or N_SHARD * w.shape[2]
        w_spec = pl.BlockSpec((None, tk, tn), lambda i, j, k: ((j + col0) // per, k, (j + col0) % per))
    else:
        n = ncols or w.shape[1]
        w_spec = pl.BlockSpec((tk, tn), lambda i, j, k: (k, j + col0))
    if outs is None:
        outs = [_out2d(m, n, F32, tm, tn)]
    return _mm(name, NN, a, pl.BlockSpec((tm, tk), lambda i, j, k: (i, k)), w, w_spec,
               (m // tm, n // tn, kdim // tk), (tm, tn), outs, epilogue, extras, riders)


def _mm_bwd_x(name, g, g_spec, w, tm, tj, tc, m, n_contract, outs=None, epilogue=None, extras=(), riders=()):
    if w.ndim == 3:
        per = w.shape[2] // tc
        kdim = w.shape[1]
        w_spec = pl.BlockSpec((None, tj, tc), lambda i, j, n: (n // per, j, n % per))
    else:
        kdim = w.shape[0]
        w_spec = pl.BlockSpec((tj, tc), lambda i, j, n: (j, n))
    if outs is None:
        outs = [_out2d(m, kdim, F32, tm, tj)]
    return _mm(name, NT, g, g_spec, w, w_spec, (m // tm, kdim // tj, n_contract // tc),
               (tm, tj), outs, epilogue, extras, riders)


def _mm_bwd_w(name, a, g, g_spec, n, tk, tn, tm, sharded, riders=()):
    m, kdim = a.shape
    if sharded:
        per = (n // N_SHARD) // tn
        out = (jax.ShapeDtypeStruct((N_SHARD, kdim, n // N_SHARD), F32),
               pl.BlockSpec((None, tk, tn), lambda i, j, mm: (j // per, i, j % per)))
    else:
        out = (jax.ShapeDtypeStruct((kdim, n), F32), pl.BlockSpec((tk, tn), lambda i, j, mm: (i, j)))
    return _mm(name, TN, a, pl.BlockSpec((tm, tk), lambda i, j, mm: (mm, i)), g, g_spec,
               (kdim // tk, n // tn, m // tm), (tk, tn), [out], riders=riders)


def _ew(name, fn, tiles, fulls, outs, sums=(), ts=512, riders=()):
    tiles = [t if isinstance(t, tuple) else (t, pl.BlockSpec((ts, t.shape[1]), lambda i: (i, 0)))
             for t in tiles]
    s_rows = tiles[0][0].shape[-2]
    nt, nf, no = len(tiles), len(fulls), len(outs)

    def body(*refs):
        t_vals = [r[...] for r in refs[:nt]]
        f_vals = [r[...] for r in refs[nt:nt + nf]]
        o_refs, s_refs = refs[nt + nf:nt + nf + no], refs[nt + nf + no:]
        o_vals, s_vals = fn(t_vals, f_vals)
        for r, v in zip(o_refs, o_vals):
            r[...] = v.astype(r.dtype)
        for r, v in zip(s_refs, s_vals):
            part = jnp.sum(v, axis=0, keepdims=True)

            @pl.when(pl.program_id(0) == 0)
            def _():
                r[...] = part

            @pl.when(pl.program_id(0) > 0)
            def _():
                r[...] += part

    full_specs = [pl.BlockSpec(f.shape, lambda i, nd=f.ndim: (0,) * nd) for f in fulls]
    return _carry(
        riders, body, name=name, grid=(s_rows // ts,),
        in_specs=[s for _, s in tiles] + full_specs,
        out_specs=[pl.BlockSpec((ts, c), lambda i: (i, 0)) for c, _ in outs]
        + [pl.BlockSpec((1, c), lambda i: (0, 0)) for c in sums],
        out_shape=[jax.ShapeDtypeStruct((s_rows, c), dt) for c, dt in outs]
        + [jax.ShapeDtypeStruct((1, c), F32) for c in sums],
        semantics=("arbitrary",),
    )(*[t for t, _ in tiles], *fulls)


def _rms_fwd(name, x, g, riders=()):
    def fn(t, f):
        xv = t[0]
        r = lax.rsqrt(jnp.mean(xv * xv, axis=-1, keepdims=True) + EPS)
        return [xv * r * f[0]], []
    out = _ew(name, fn, [x], [g], [(x.shape[1], BF16)], riders=riders)
    return (out[0][0], out[1]) if riders else out[0]


def _rms_bwd(name, dh, x, g, dres, out_dtype=BF16, riders=()):
    def fn(t, f):
        dhv, xv, dr = [v.astype(F32) for v in t]
        r = lax.rsqrt(jnp.mean(xv * xv, axis=-1, keepdims=True) + EPS)
        xhat = xv * r
        u = dhv * f[0]
        dx = r * (u - xhat * jnp.mean(u * xhat, axis=-1, keepdims=True)) + dr
        return [dx], [dhv * xhat]
    c = x.shape[1]
    return _ew(name, fn, [dh, x, dres], [g], [(c, out_dtype)], sums=[c], riders=riders)


def _split3(x):
    x1 = x.astype(BF16)
    r1 = x - x1.astype(F32)
    x2 = r1.astype(BF16)
    x3 = (r1 - x2.astype(F32)).astype(BF16)
    return x1, x2, x3


def _rel_class(cp):
    far = (cp < MAX_REL) | (cp > BAND)
    return jnp.where(far, 2 * MAX_REL, BAND - cp)


def _skew_rows(x, sign):
    row = lax.broadcasted_iota(jnp.int32, x.shape, 0)
    for b in range(CHUNK.bit_length() - 1):
        shift = (1 << b) if sign > 0 else SKEW_W - (1 << b)
        x = jnp.where((row >> b) & 1 == 1, pltpu.roll(x, shift, 1), x)
    return x


def _roll_lanes(x, shift):
    return x if shift % SKEW_W == 0 else pltpu.roll(x, shift % SKEW_W, 1)


def _bias_expand(rel_bias, riders=()):
    rel = jnp.pad(rel_bias, ((0, 0), (0, REL_PAD - N_REL))).reshape(N_HEADS, 1, REL_PAD)

    def body(rel_ref, o_ref):
        cls = lax.broadcasted_iota(jnp.int32, (REL_PAD, SKEW_W), 0)
        cp = lax.broadcasted_iota(jnp.int32, (REL_PAD, SKEW_W), 1)
        onehot = (cls == _rel_class(cp)).astype(BF16)
        rel8 = jnp.broadcast_to(rel_ref[...], (8, REL_PAD))
        trow = sum(jnp.dot(p, onehot, preferred_element_type=F32) for p in _split3(rel8))[0:1]
        first = _skew_rows(jnp.broadcast_to(trow, (CHUNK, SKEW_W)), +1)
        full = jnp.concatenate([_roll_lanes(first, CHUNK * g) for g in range(SUPER // CHUNK)], axis=0)[:, :BAND]
        qc = lax.broadcasted_iota(jnp.int32, (SUPER, BAND), 0) // CHUNK
        kc = lax.broadcasted_iota(jnp.int32, (SUPER, BAND), 1) // CHUNK
        on_band = (kc >= qc) & (kc <= qc + N_PREV_CHUNKS)
        o_ref[...] = jnp.where(on_band, full, NEG_INF).T

    return _carry(
        riders, body, name="bias_expand", grid=(N_HEADS,),
        in_specs=[pl.BlockSpec((None, 1, REL_PAD), lambda h: (h, 0, 0))],
        out_specs=pl.BlockSpec((None, BAND, SUPER), lambda h: (h, 0, 0)),
        out_shape=jax.ShapeDtypeStruct((N_HEADS, BAND, SUPER), F32),
        semantics=("arbitrary",),
    )(rel)


def _bias_reduce(dbias):
    def body(d_ref, o_ref):
        x = jnp.concatenate([d_ref[...].T, jnp.zeros((SUPER, SKEW_W - BAND), F32)], axis=1)
        folded = sum(_roll_lanes(x[CHUNK * g:CHUNK * (g + 1)], -CHUNK * g) for g in range(SUPER // CHUNK))
        diag = jnp.sum(_skew_rows(folded, -1), axis=0, keepdims=True)
        cp = lax.broadcasted_iota(jnp.int32, (SKEW_W, REL_PAD), 0)
        cls = lax.broadcasted_iota(jnp.int32, (SKEW_W, REL_PAD), 1)
        onehot = (cls == _rel_class(cp)).astype(BF16)
        diag8 = jnp.broadcast_to(diag, (8, SKEW_W))
        o_ref[...] = sum(jnp.dot(p, onehot, preferred_element_type=F32) for p in _split3(diag8))[0:1]

    out = pl.pallas_call(
        body, name="bias_reduce", grid=(N_HEADS,),
        in_specs=[pl.BlockSpec((None, BAND, SUPER), lambda h: (h, 0, 0))],
        out_specs=pl.BlockSpec((None, 1, REL_PAD), lambda h: (h, 0, 0)),
        out_shape=jax.ShapeDtypeStruct((N_HEADS, 1, REL_PAD), F32),
        compiler_params=_params("arbitrary"),
    )(dbias)
    return out.reshape(N_HEADS, REL_PAD)[:, :N_REL]


HEADS_PER_STEP = 4
HEAD_COLS = HEADS_PER_STEP * HEAD_DIM
N_HEAD_GROUPS = N_HEADS // HEADS_PER_STEP


def _unit(x):
    r = lax.rsqrt(jnp.mean(x * x, axis=-1, keepdims=True) + EPS)
    return x * r, r


def _scores_t(qs, kn, bias_t, dead):
    s = jnp.concatenate([lax.dot_general(k, qs, NT, preferred_element_type=F32) for k in kn], axis=0) + bias_t
    return jnp.where(dead, NEG_INF, s)


def _band_specs(nb, col0, clamp_hi):
    def spec(d):
        def index(hg, i):
            blk = jnp.maximum(i - d, 0)
            if clamp_hi:
                blk = jnp.minimum(blk, nb - 1)
            return (blk, col0 + hg)
        return pl.BlockSpec((SUPER, HEAD_COLS), index)
    return [spec(2), spec(1), spec(0)]


def _head_sums(y):
    cols = y.shape[1]
    same_head = (lax.broadcasted_iota(jnp.int32, (cols, cols), 0) // HEAD_DIM
                 == lax.broadcasted_iota(jnp.int32, (cols, cols), 1) // HEAD_DIM).astype(BF16)
    hi = y.astype(BF16)
    lo = (y - hi.astype(F32)).astype(BF16)
    return (jnp.dot(hi, same_head, preferred_element_type=F32)
            + jnp.dot(lo, same_head, preferred_element_type=F32))


def _head_unit(x):
    r = lax.rsqrt(_head_sums(x * x) * (1.0 / HEAD_DIM) + EPS)
    return x * r, r


def _head(hh):
    return slice(HEAD_DIM * hh, HEAD_DIM * (hh + 1))


def _key_block(j):
    return slice(SUPER * j, SUPER * (j + 1))


def _dead_keys(i):
    key = lax.broadcasted_iota(jnp.int32, (BAND, SUPER), 0)
    return key < (2 - i) * SUPER


LSE_ROWS = 8


def _attn_fwd(qkn, v, bias, riders=()):
    s_len = qkn.shape[0]
    nb = s_len // SUPER

    def body(q_ref, k0, k1, k2, v0, v1, v2, b_ref, o_ref, lse_ref):
        dead = _dead_keys(pl.program_id(1))
        outs = []

        def probabilities(hh):
            sl = _head(hh)
            s = _scores_t(q_ref[:, sl], [k0[:, sl], k1[:, sl], k2[:, sl]], b_ref[hh], dead)
            m = jnp.max(s, axis=0, keepdims=True)
            e = jnp.exp(s - m)
            l = jnp.sum(e, axis=0, keepdims=True)
            lse_ref[hh:hh + 1, :] = m + jnp.log(l)
            return (e * (1.0 / l)).astype(BF16), sl

        def weighted_values(p, sl):
            outs.append(sum(lax.dot_general(p[_key_block(j), :], vj[:, sl], TN, preferred_element_type=F32)
                            for j, vj in enumerate((v0, v1, v2))))

        ready = probabilities(0)
        for hh in range(1, HEADS_PER_STEP):
            following = probabilities(hh)
            weighted_values(*ready)
            ready = following
        weighted_values(*ready)
        o_ref[...] = jnp.concatenate(outs, axis=1).astype(o_ref.dtype)

    return _carry(
        riders, body, name="attn_fwd", grid=(N_HEAD_GROUPS, nb),
        in_specs=[pl.BlockSpec((SUPER, HEAD_COLS), lambda hg, i: (i, hg))]
        + _band_specs(nb, N_HEAD_GROUPS, False) + _band_specs(nb, 0, False)
        + [pl.BlockSpec((HEADS_PER_STEP, BAND, SUPER), lambda hg, i: (hg, 0, 0))],
        out_specs=[pl.BlockSpec((SUPER, HEAD_COLS), lambda hg, i: (i, hg)),
                   pl.BlockSpec((None, LSE_ROWS, SUPER), lambda hg, i: (hg, 0, i))],
        out_shape=[jax.ShapeDtypeStruct((s_len, D_MODEL), BF16),
                   jax.ShapeDtypeStruct((N_HEAD_GROUPS, LSE_ROWS, s_len), F32)],
        semantics=("parallel", "arbitrary"),
    )(qkn, qkn, qkn, qkn, v, v, v, bias)


def _attn_bwd(qkn, v, d_out, bias, lse, riders=()):
    s_len = qkn.shape[0]
    nb = s_len // SUPER

    def body(q_ref, k0, k1, k2, v0, v1, v2, do_ref, b_ref, lse_ref, dp_ref, db_ref, aq_ref, ak_ref, av_ref):
        i = pl.program_id(1)

        @pl.when(i == 0)
        def _():
            aq_ref[...] = jnp.zeros_like(aq_ref)
            ak_ref[...] = jnp.zeros_like(ak_ref)
            av_ref[...] = jnp.zeros_like(av_ref)
            db_ref[...] = jnp.zeros_like(db_ref)

        @pl.when(i < nb)
        def _():
            dead = _dead_keys(i)
            dq, dk, dv = [], [[], [], []], [[], [], []]

            def softmax_grad(hh):
                sl = _head(hh)
                qs, do = q_ref[:, sl], do_ref[:, sl]
                kn = [k0[:, sl], k1[:, sl], k2[:, sl]]
                vv = [v0[:, sl], v1[:, sl], v2[:, sl]]
                p = jnp.exp(_scores_t(qs, kn, b_ref[hh], dead) - lse_ref[hh:hh + 1, :])
                dp = jnp.concatenate([lax.dot_general(vj, do, NT, preferred_element_type=F32) for vj in vv], axis=0)
                ds = p * (dp - jnp.sum(p * dp, axis=0, keepdims=True))
                db_ref[hh] += ds
                return p.astype(BF16), ds.astype(BF16), qs, do, kn

            def operand_grads(pb, dsb, qs, do, kn):
                dq.append(sum(lax.dot_general(dsb[_key_block(j), :], kn[j], TN, preferred_element_type=F32)
                              for j in range(3)))
                for j in range(3):
                    dv[j].append(jnp.dot(pb[_key_block(j), :], do, preferred_element_type=F32))
                    dk[j].append(jnp.dot(dsb[_key_block(j), :], qs, preferred_element_type=F32))

            ready = softmax_grad(0)
            for hh in range(1, HEADS_PER_STEP):
                following = softmax_grad(hh)
                operand_grads(*ready)
                ready = following
            operand_grads(*ready)
            aq_ref[i % 3] = jnp.concatenate(dq, axis=1)
            for j in range(3):
                slot = (i + 1 + j) % 3
                if j < 2:
                    ak_ref[slot] += jnp.concatenate(dk[j], axis=1)
                    av_ref[slot] += jnp.concatenate(dv[j], axis=1)
                else:
                    ak_ref[slot] = jnp.concatenate(dk[j], axis=1)
                    av_ref[slot] = jnp.concatenate(dv[j], axis=1)

        slot = (i + 1) % 3
        dp_ref[0] = aq_ref[slot].astype(dp_ref.dtype)
        dp_ref[1] = ak_ref[slot].astype(dp_ref.dtype)
        dp_ref[2] = av_ref[slot].astype(dp_ref.dtype)

    def qrow(hg, i):
        return (jnp.minimum(i, nb - 1), hg)

    return _carry(
        riders, body, name="attn_bwd", grid=(N_HEAD_GROUPS, nb + 2),
        in_specs=[pl.BlockSpec((SUPER, HEAD_COLS), qrow)]
        + _band_specs(nb, N_HEAD_GROUPS, True) + _band_specs(nb, 0, True)
        + [pl.BlockSpec((SUPER, HEAD_COLS), qrow),
           pl.BlockSpec((HEADS_PER_STEP, BAND, SUPER), lambda hg, i: (hg, 0, 0)),
           pl.BlockSpec((None, LSE_ROWS, SUPER), lambda hg, i: (hg, 0, jnp.minimum(i, nb - 1)))],
        out_specs=[pl.BlockSpec((3, SUPER, HEAD_COLS), lambda hg, i: (0, jnp.maximum(i - 2, 0), hg)),
                   pl.BlockSpec((HEADS_PER_STEP, BAND, SUPER), lambda hg, i: (hg, 0, 0))],
        out_shape=[jax.ShapeDtypeStruct((6, s_len, D_MODEL), BF16),
                   jax.ShapeDtypeStruct((N_HEADS, BAND, SUPER), F32)],
        scratch_shapes=[pltpu.VMEM((3, SUPER, HEAD_COLS), F32)] * 3,
        semantics=("parallel", "arbitrary"),
    )(qkn, qkn, qkn, qkn, v, v, v, d_out, bias, lse)


def _qk_norm_bwd(dproj6, qk_raw, gq, gk):
    s_len = qk_raw.shape[0]
    ts = min(1024, s_len)

    nsteps = s_len // ts
    half = D_MODEL // 2

    def body(d_ref, raw_ref, gq_ref, gk_ref, o_ref, dgq_ref, dgk_ref, acc_ref):
        step = pl.program_id(0)

        @pl.when(step == 0)
        def _():
            acc_ref[...] = jnp.zeros_like(acc_ref)

        for piece, (g_ref, scale) in enumerate(((gq_ref, QK_SCALE), (gk_ref, 1.0))):
            for c0 in (0, half):
                xhat, r = _head_unit(raw_ref[:, piece * D_MODEL + c0:piece * D_MODEL + c0 + half].astype(F32))
                dn = d_ref[piece, :, c0:c0 + half].astype(F32) * scale
                u = dn * g_ref[...]
                dx = r * (u - xhat * (_head_sums(u * xhat) * (1.0 / HEAD_DIM)))
                o_ref[piece, :, c0:c0 + half] = dx.astype(o_ref.dtype)
                acc_ref[piece:piece + 1, c0:c0 + half] += jnp.sum(dn * xhat, axis=0, keepdims=True)

        @pl.when(step == nsteps - 1)
        def _():
            lane = lax.broadcasted_iota(jnp.int32, (D_MODEL, LANE), 0) % HEAD_DIM
            fold = (lane == lax.broadcasted_iota(jnp.int32, (D_MODEL, LANE), 1)).astype(BF16)
            tot = sum(jnp.dot(p, fold, preferred_element_type=F32) for p in _split3(acc_ref[...]))
            dgq_ref[...] = tot[0:1, :HEAD_DIM]
            dgk_ref[...] = tot[1:2, :HEAD_DIM]

    gain = pl.BlockSpec((1, half), lambda i: (0, 0))
    small = pl.BlockSpec((1, HEAD_DIM), lambda i: (0, 0))
    per_head = lambda g: jnp.tile(g, (1, half // HEAD_DIM))
    return pl.pallas_call(
        body, name="qk_norm_bwd", grid=(nsteps,),
        in_specs=[pl.BlockSpec((2, ts, D_MODEL), lambda i: (0, i, 0)),
                  pl.BlockSpec((ts, 2 * D_MODEL), lambda i: (i, 0)), gain, gain],
        out_specs=[pl.BlockSpec((2, ts, D_MODEL), lambda i: (0, i, 0)), small, small],
        out_shape=[jax.ShapeDtypeStruct(dproj6.shape, dproj6.dtype),
                   jax.ShapeDtypeStruct((1, HEAD_DIM), F32), jax.ShapeDtypeStruct((1, HEAD_DIM), F32)],
        scratch_shapes=[pltpu.VMEM((8, D_MODEL), F32)],
        input_output_aliases={0: 0},
        compiler_params=_params("arbitrary"),
    )(dproj6, qk_raw, per_head(gq), per_head(gk))


CONV_ROWS = 512
HALO = 16


def _rows_with_halo(ref, r0, n, front, s_len):
    zeros = jnp.zeros((HALO, ref.shape[1]), F32)
    if front:
        return (jnp.concatenate([zeros, ref[0:n, :].astype(F32)], axis=0) if r0 == 0
                else ref[r0 - HALO:r0 + n, :].astype(F32))
    return (jnp.concatenate([ref[r0:r0 + n, :].astype(F32), zeros], axis=0) if r0 + n == s_len
            else ref[r0:r0 + n + HALO, :].astype(F32))


def _earlier(ext, k):
    return pltpu.roll(ext, k, 0)[HALO:]


def _later(ext, k):
    n = ext.shape[0]
    return pltpu.roll(ext, n - k, 0)[:n - HALO]


def _conv_cols(col0):
    return lambda s_len: pl.BlockSpec((s_len, LANE), lambda j: (0, col0 + j))


def _conv_fwd(proj, conv_w, conv_b, riders=()):
    s_len = proj.shape[0]

    def body(bg_ref, cg_ref, xc_ref, w_ref, b_ref, o_ref):
        w = [w_ref[t:t + 1, :] for t in range(3)]
        for r0 in range(0, s_len, CONV_ROWS):
            u = _rows_with_halo(cg_ref, r0, CONV_ROWS, True, s_len) * \
                _rows_with_halo(xc_ref, r0, CONV_ROWS, True, s_len)
            conv = b_ref[...] + w[0] * _earlier(u, 2) + w[1] * _earlier(u, 1) + w[2] * u[HALO:]
            o_ref[r0:r0 + CONV_ROWS, :] = (bg_ref[r0:r0 + CONV_ROWS, :].astype(F32) * conv).astype(o_ref.dtype)

    return _carry(
        riders, body, name="conv_fwd", grid=(D_MODEL // LANE,),
        in_specs=[_conv_cols(0)(s_len), _conv_cols(8)(s_len), _conv_cols(16)(s_len),
                  pl.BlockSpec((3, LANE), lambda j: (0, j)), pl.BlockSpec((1, LANE), lambda j: (0, j))],
        out_specs=pl.BlockSpec((s_len, LANE), lambda j: (0, j)),
        out_shape=jax.ShapeDtypeStruct((s_len, D_MODEL), BF16),
        semantics=("parallel",),
    )(proj, proj, proj, conv_w, conv_b)


def _conv_bwd(dproj6, dy, proj, conv_w, conv_b, riders=()):
    s_len = proj.shape[0]

    def body(dy_ref, bg_ref, cg_ref, xc_ref, w_ref, b_ref, _, dp_ref, dw_ref):
        w = [w_ref[t:t + 1, :] for t in range(3)]
        acc = [jnp.zeros((1, LANE), F32) for _ in range(4)]
        for r0 in range(0, s_len, CONV_ROWS):
            rows = slice(r0, r0 + CONV_ROWS)
            u = _rows_with_halo(cg_ref, r0, CONV_ROWS, True, s_len) * \
                _rows_with_halo(xc_ref, r0, CONV_ROWS, True, s_len)
            u2, u1, u0 = _earlier(u, 2), _earlier(u, 1), u[HALO:]
            conv = b_ref[...] + w[0] * u2 + w[1] * u1 + w[2] * u0
            dp_ref[0, rows, :] = (dy_ref[rows, :].astype(F32) * conv).astype(dp_ref.dtype)
            dconv_ext = _rows_with_halo(dy_ref, r0, CONV_ROWS, False, s_len) * \
                _rows_with_halo(bg_ref, r0, CONV_ROWS, False, s_len)
            dconv = dconv_ext[:CONV_ROWS]
            for t, term in enumerate([dconv * u2, dconv * u1, dconv * u0, dconv]):
                acc[t] = acc[t] + jnp.sum(term, axis=0, keepdims=True)
            du = w[2] * dconv + w[1] * _later(dconv_ext, 1) + w[0] * _later(dconv_ext, 2)
            dp_ref[1, rows, :] = (du * xc_ref[rows, :].astype(F32)).astype(dp_ref.dtype)
            dp_ref[2, rows, :] = (du * cg_ref[rows, :].astype(F32)).astype(dp_ref.dtype)
        dw_ref[...] = jnp.zeros_like(dw_ref)
        for t in range(4):
            dw_ref[t:t + 1, :] = acc[t]

    return _carry(
        riders, body, name="conv_bwd", grid=(D_MODEL // LANE,),
        in_specs=[pl.BlockSpec((s_len, LANE), lambda j: (0, j)),
                  _conv_cols(0)(s_len), _conv_cols(8)(s_len), _conv_cols(16)(s_len),
                  pl.BlockSpec((3, LANE), lambda j: (0, j)), pl.BlockSpec((1, LANE), lambda j: (0, j)),
                  pl.BlockSpec(memory_space=pl.ANY)],
        out_specs=[pl.BlockSpec((3, s_len, LANE), lambda j: (1, 0, j)),
                   pl.BlockSpec((8, LANE), lambda j: (0, j))],
        out_shape=[jax.ShapeDtypeStruct(dproj6.shape, dproj6.dtype),
                   jax.ShapeDtypeStruct((8, D_MODEL), F32)],
        input_output_aliases={6: 0},
        semantics=("parallel",),
    )(dy, proj, proj, proj, conv_w, conv_b, dproj6)


class _NoComm:
    def __init__(self, early, late):
        self.early, self.late = early, late

    def riders(self, at):
        return ()

    def done(self, at, carried):
        pass

    def early_weights(self):
        return self.early

    def late_weights(self):
        return self.late

    def grads_ready(self, group, grads):
        pass


def _d_input(name, pieces, w3, extra=None, riders=()):
    n_piece, s_len, width = pieces.shape
    _, kdim, per_shard = w3.shape
    tm, chunk = 512, 512

    def body(p_ref, w_ref, *refs):
        o_ref = refs[-1]
        acc = refs[0][...].astype(F32) if extra is not None else jnp.zeros(o_ref.shape, F32)
        for c0 in range(0, n_piece * width, chunk):
            acc = acc + lax.dot_general(p_ref[c0 // width, :, c0 % width:c0 % width + chunk],
                                        w_ref[c0 // per_shard, :, c0 % per_shard:c0 % per_shard + chunk],
                                        NT, preferred_element_type=F32)
        o_ref[...] = acc.astype(o_ref.dtype)

    tile = pl.BlockSpec((tm, kdim), lambda i: (i, 0))
    return _carry(
        riders, body, name=name, grid=(s_len // tm,),
        in_specs=[pl.BlockSpec((n_piece, tm, width), lambda i: (0, i, 0)),
                  pl.BlockSpec(w3.shape, lambda i: (0, 0, 0))] + ([tile] if extra is not None else []),
        out_specs=tile, out_shape=jax.ShapeDtypeStruct((s_len, kdim), BF16),
        semantics=("parallel",),
    )(pieces, w3, *([extra] if extra is not None else []))


def _local_grads(x, target, norm1_g, q_norm_g, k_norm_g, rel_bias, conv_b, b_gate, norm2_g, comm):
    s_len = x.shape[0]
    tm = min(1024, s_len)
    row = lambda v: v.reshape(1, -1)

    def carrying(at, fn, *args, **kw):
        riders = comm.riders(at)
        out = fn(*args, riders=riders, **kw)
        if riders:
            out, carried = out
            comm.done(at, carried)
        return out

    bias = carrying("bias_expand", _bias_expand, rel_bias)
    h = carrying("norm1", _rms_fwd, "norm1", x, row(norm1_g))
    w_in3, w_gate3, conv_w = comm.early_weights()
    gq, gk = row(q_norm_g), row(k_norm_g)

    def qk_epi(acc, e, o):
        o[0][...] = acc.astype(BF16)
        gain = jnp.where(pl.program_id(1) < 2, e[0] * QK_SCALE, e[1])
        o[1][...] = (_head_unit(acc)[0] * gain).astype(BF16)
    small = pl.BlockSpec((1, 512), lambda i, j, k: (0, 0))
    per_head = lambda g: jnp.tile(g, (1, 512 // HEAD_DIM))
    qk_raw, qkn = _mm_fwd("proj_qk", h, w_in3, tm, 512, D_MODEL, ncols=2 * D_MODEL, epilogue=qk_epi,
                          outs=[_out2d(s_len, 2 * D_MODEL, BF16, tm, 512)] * 2,
                          extras=[(per_head(gq), small), (per_head(gk), small)])
    v = _mm_fwd("proj_v", h, w_in3, tm, 512, D_MODEL, col0=4, ncols=D_MODEL,
                outs=[_out2d(s_len, D_MODEL, BF16, tm, 512)])
    conv_in = _mm_fwd("proj_conv", h, w_in3, tm, 1536, D_MODEL, col0=2, ncols=3 * D_MODEL,
                      outs=[_out2d(s_len, 3 * D_MODEL, BF16, tm, 1536)])

    def gate_epi(acc, e, o):
        o[0][...] = jax.nn.sigmoid(acc + e[0]).astype(BF16)
    gates = _mm_fwd("gates", h, w_gate3, tm, 512, D_MODEL, epilogue=gate_epi,
                    outs=[_out2d(s_len, 2 * D_MODEL, BF16, tm, 512)],
                    extras=[(row(b_gate), pl.BlockSpec((1, 512), lambda i, j, k: (0, j)))])

    attn, lse = carrying("attn_fwd", _attn_fwd, qkn, v, bias)
    yconv = carrying("conv_fwd", _conv_fwd, conv_in, conv_w, row(conv_b))
    w_ap, w_cp, w_out, w_up3, w_down = comm.late_weights()
    tw = 1024
    ya = _mm_fwd("attn_proj", attn, w_ap, tm, tw, D_MODEL, outs=[_out2d(s_len, D_MODEL, BF16, tm, tw)])

    def merge_epi(acc, e, o):
        ya_v, ga, gc = [t.astype(F32) for t in e]
        o[0][...] = acc.astype(BF16)
        o[1][...] = (ga * ya_v + gc * acc).astype(BF16)
    gate_a, gate_c = (gates, _tile_spec(tm, tw, 0)), (gates, _tile_spec(tm, tw, 1))
    yc, merged = _mm_fwd("conv_proj", yconv, w_cp, tm, tw, D_MODEL, epilogue=merge_epi,
                         outs=[_out2d(s_len, D_MODEL, BF16, tm, tw), _out2d(s_len, D_MODEL, BF16, tm, tw)],
                         extras=[(ya, _tile_spec(tm, tw)), gate_a, gate_c])

    def res_epi(acc, e, o):
        o[0][...] = e[0] + acc
    x1 = _mm_fwd("out_proj", merged, w_out, tm, tw, D_MODEL, epilogue=res_epi,
                 extras=[(x, _tile_spec(tm, tw))])
    h2 = _rms_fwd("norm2", x1, row(norm2_g))

    def up_epi(acc, e, o):
        o[0][...] = jnp.square(jnp.maximum(acc, 0.0)).astype(BF16)
    act = _mm_fwd("mlp_up", h2, w_up3, tm, tw, D_MODEL, epilogue=up_epi, outs=[_out2d(s_len, D_FF, BF16, tm, tw)])

    def loss_epi(acc, e, o):
        err = e[0] + acc - e[1]
        o[0][...] = (err * (1.0 / D_MODEL)).astype(BF16)
        sq = err * err
        part = sq[:, 0:LANE]
        for c0 in range(LANE, D_MODEL, LANE):
            part = part + sq[:, c0:c0 + LANE]
        o[1][...] = jnp.sum(part.reshape(tl // 8, 8, LANE), axis=0)
    tl = 512
    dy_b, loss_part = _mm_fwd(
        "mlp_down", act, w_down, tl, D_MODEL, D_FF, epilogue=loss_epi,
        outs=[_out2d(s_len, D_MODEL, BF16, tl, D_MODEL),
              (jax.ShapeDtypeStruct((8 * (s_len // tl), LANE), F32), pl.BlockSpec((8, LANE), lambda i, j, k: (i, 0)))],
        extras=[(x1, _tile_spec(tl, D_MODEL)), (target, _tile_spec(tl, D_MODEL))])

    def dup_epi(acc, e, o):
        o[0][...] = (acc * (2.0 * jnp.sqrt(e[0].astype(F32)))).astype(BF16)
    full = lambda cols: pl.BlockSpec((tm, cols), lambda i, j, n: (i, n))
    tokens = lambda cols: pl.BlockSpec((s_len, cols), lambda i, j, m: (m, j))
    dup = _mm_bwd_x("d_act", dy_b, full(D_MODEL), w_down, tm, tw, D_MODEL, s_len, D_MODEL, epilogue=dup_epi,
                    outs=[_out2d(s_len, D_FF, BF16, tm, tw)], extras=[(act, _tile_spec(tm, tw))])
    g_down = _mm_bwd_w("g_down", act, dy_b, tokens(D_MODEL), D_MODEL, 512, D_MODEL, s_len, False)
    g_up = _mm_bwd_w("g_up", h2, dup, tokens(512), D_FF, D_MODEL, 512, s_len, True)
    comm.grads_ready("mlp", dict(w_down=g_down, w_up=g_up))
    dh2 = carrying("d_h2", _mm_bwd_x, "d_h2", dup, full(1024), w_up3, tm, tw, 1024, s_len, D_FF,
                   outs=[_out2d(s_len, D_MODEL, BF16, tm, tw)])
    dx1_b, dg2 = _rms_bwd("norm2_bwd", dh2, x1, row(norm2_g), dy_b)

    def dmerge_epi(acc, e, o):
        ya_v, yc_v, ga, gc = [t.astype(F32) for t in e]
        o[0][...] = (acc * ga).astype(BF16)
        o[1][...] = (acc * gc).astype(BF16)
        o[2][0] = (acc * ya_v * ga * (1.0 - ga)).astype(BF16)
        o[2][1] = (acc * yc_v * gc * (1.0 - gc)).astype(BF16)
    dya, dyc, dgp2 = _mm_bwd_x(
        "d_merged", dx1_b, full(D_MODEL), w_out, tm, tw, D_MODEL, s_len, D_MODEL, epilogue=dmerge_epi,
        outs=[_out2d(s_len, D_MODEL, BF16, tm, tw), _out2d(s_len, D_MODEL, BF16, tm, tw),
              (jax.ShapeDtypeStruct((2, s_len, D_MODEL), BF16), pl.BlockSpec((2, tm, tw), lambda i, j, n: (0, i, j)))],
        extras=[(ya, _tile_spec(tm, tw)), (yc, _tile_spec(tm, tw)), gate_a, gate_c])
    g_out = _mm_bwd_w("g_out", merged, dx1_b, tokens(512), D_MODEL, D_MODEL, 512, s_len, False)
    d_attn = _mm_bwd_x("d_attn", dya, full(D_MODEL), w_ap, tm, tw, D_MODEL, s_len, D_MODEL,
                       outs=[_out2d(s_len, D_MODEL, BF16, tm, tw)])
    g_ap = _mm_bwd_w("g_attn_proj", attn, dya, tokens(512), D_MODEL, D_MODEL, 512, s_len, False)
    d_yconv = _mm_bwd_x("d_yconv", dyc, full(D_MODEL), w_cp, tm, tw, D_MODEL, s_len, D_MODEL,
                        outs=[_out2d(s_len, D_MODEL, BF16, tm, tw)])
    g_cp = _mm_bwd_w("g_conv_proj", yconv, dyc, tokens(512), D_MODEL, D_MODEL, 512, s_len, False)
    comm.grads_ready("proj", dict(w_out=g_out, w_attn_proj=g_ap, w_conv_proj=g_cp))

    dproj6, dbias = carrying("attn_bwd", _attn_bwd, qkn, v, d_attn, bias, lse)
    dproj6, dgq, dgk = _qk_norm_bwd(dproj6, qk_raw, gq, gk)
    dproj6, dconv_wb = carrying("conv_bwd", _conv_bwd, dproj6, d_yconv, conv_in, conv_w, row(conv_b))
    d_rel = _bias_reduce(dbias)

    piece = lambda width: (lambda blk: (blk * width) // D_MODEL, lambda blk: (blk * width % D_MODEL) // width)
    pc, cb = piece(512)
    pieces = pl.BlockSpec((None, s_len, 512), lambda i, j, m: (pc(j), m, cb(j)))
    g_in = carrying("g_in", _mm_bwd_w, "g_in", h, dproj6, pieces, 6 * D_MODEL, D_MODEL, 512, s_len, True)
    g_gate = carrying("g_gate", _mm_bwd_w, "g_gate", h, dgp2, pieces, 2 * D_MODEL, D_MODEL, 512, s_len, True)
    comm.grads_ready("in", dict(w_in=g_in, w_gate=g_gate))
    dh_g = carrying("d_h_gate", _d_input, "d_h_gate", dgp2, w_gate3)
    dh = carrying("d_h_proj", _d_input, "d_h_proj", dproj6, w_in3, extra=dh_g)
    grad_x, dg1 = carrying("norm1_bwd", _rms_bwd, "norm1_bwd", dh, x, row(norm1_g), dx1_b, out_dtype=F32)

    def bsum(t, f):
        return [], [t[0].astype(F32), t[1].astype(F32)]
    ts = 512
    db_a, db_c = carrying("b_gate_sum", _ew, "b_gate_sum", bsum,
                          [(dgp2, pl.BlockSpec((None, ts, D_MODEL), lambda i: (0, i, 0))),
                           (dgp2, pl.BlockSpec((None, ts, D_MODEL), lambda i: (1, i, 0)))],
                          [], [], sums=[D_MODEL, D_MODEL], ts=ts)

    big = dict(w_in=g_in, w_attn_proj=g_ap, w_conv_proj=g_cp, w_gate=g_gate, w_out=g_out,
               w_up=g_up, w_down=g_down)
    small = dict(norm1_g=dg1, norm2_g=dg2, conv_wb=dconv_wb, b_gate=(db_a, db_c),
                 q_norm_g=dgq, k_norm_g=dgk, rel_bias=d_rel)
    return loss_part, grad_x, big, small


def _finish_loss(loss_part):
    def body(l_ref, lo_ref):
        total = jnp.sum(jnp.sum(l_ref[...], axis=0, keepdims=True), axis=1, keepdims=True)
        lo_ref[...] = jnp.broadcast_to(total * (0.5 / D_MODEL), lo_ref.shape)

    return pl.pallas_call(body, name="finish_loss", out_shape=jax.ShapeDtypeStruct((8, LANE), F32))(loss_part)


HBM_SPEC = pl.BlockSpec(memory_space=pl.ANY)


def _place():
    return lax.axis_index("x"), lax.axis_index("y"), lax.axis_index("c")


def _other_chips(x, y):
    return [(1 - x, y), (x, 1 - y), (1 - x, 1 - y)]


def _cast_into_slot(name, where, w):
    r, cols = w.shape
    ts = 256

    def body(w_ref, x_ref, o_ref):
        o_ref[...] = x_ref[...].astype(o_ref.dtype)

    return pl.pallas_call(
        body, name=name,
        grid_spec=pltpu.PrefetchScalarGridSpec(
            num_scalar_prefetch=1, grid=(r // ts,),
            in_specs=[pl.BlockSpec((ts, cols), lambda i, w: (i, 0))],
            out_specs=pl.BlockSpec((None, ts, cols), lambda i, w: (w[0], i, 0))),
        out_shape=jax.ShapeDtypeStruct((N_SHARD, r, cols), BF16),
        compiler_params=_params("parallel"),
    )(where, w)


def _remote(src, dst, send, recv, k, to):
    return pltpu.make_async_remote_copy(src_ref=src, dst_ref=dst, send_sem=send.at[k], recv_sem=recv.at[k],
                                        device_id=to, device_id_type=MESH)


def _gather_riders(slots):
    n = len(slots)

    def half(refs, w, shard, which):
        hr = slots[w].shape[1] // 2
        return refs[w].at[shard, pl.ds(which * hr, hr)]

    def each(fn):
        x, y, c = _place()
        for w in range(n):
            for j, chip in enumerate(_other_chips(x, y)):
                fn(w, 3 * w + j, 2 * x + y, 2 * chip[0] + chip[1], c, (*chip, c), (x, y, 1 - c))

    def chips_start(_, refs, send, recv):
        each(lambda w, k, me, them, c, peer, sib: _remote(half(refs, w, me, c), half(refs, w, me, c), send, recv, k, peer).start())

    def chips_finish(_, refs, send, recv):
        each(lambda w, k, me, them, c, peer, sib: _remote(half(refs, w, them, c), half(refs, w, them, c), send, recv, k, peer).wait_recv())
        each(lambda w, k, me, them, c, peer, sib: _remote(half(refs, w, me, c), half(refs, w, me, c), send, recv, k, peer).wait_send())

    def sibling_start(_, refs, send, recv):
        each(lambda w, k, me, them, c, peer, sib: _remote(half(refs, w, them, c), half(refs, w, them, c), send, recv, k, sib).start())

    def sibling_finish(_, refs, send, recv):
        each(lambda w, k, me, them, c, peer, sib: _remote(half(refs, w, them, 1 - c), half(refs, w, them, 1 - c), send, recv, k, sib).wait_recv())
        each(lambda w, k, me, them, c, peer, sib: _remote(half(refs, w, them, c), half(refs, w, them, c), send, recv, k, sib).wait_send())

    return (lambda s: _Rider([], s, 3 * n, chips_start, chips_finish),
            lambda s: _Rider([], s, 3 * n, sibling_start, sibling_finish))


def _pair_exchange_rider(grads, landing):
    n = len(grads)

    def copies(srcs, dsts, send, recv):
        x, y, c = _place()
        out = []
        for w in range(n):
            hr = grads[w].shape[1] // 2
            out.append(_remote(srcs[w].at[:, pl.ds((1 - c) * hr, hr)], dsts[w], send, recv, w, (x, y, 1 - c)))
        return out

    def start(srcs, dsts, send, recv):
        for cp in copies(srcs, dsts, send, recv):
            cp.start()

    def finish(srcs, dsts, send, recv):
        for cp in copies(srcs, dsts, send, recv):
            cp.wait()

    return _Rider(grads, landing, n, start, finish)


def _row_tile(hr):
    return min(hr, 256)


def _pair_add(name, where, grad, got):
    _, hr, cols = got.shape
    tr = _row_tile(hr)
    nblk = hr // tr

    def body(w_ref, g_ref, r_ref, o_ref):
        o_ref[...] = (g_ref[...] + r_ref[...]).astype(o_ref.dtype)

    return pl.pallas_call(
        body, name=name,
        grid_spec=pltpu.PrefetchScalarGridSpec(
            num_scalar_prefetch=1, grid=(N_SHARD, nblk),
            in_specs=[pl.BlockSpec((None, tr, cols), lambda s, i, w: (s, w[1] * nblk + i, 0)),
                      pl.BlockSpec((None, tr, cols), lambda s, i, w: (s, i, 0))],
            out_specs=pl.BlockSpec((None, tr, cols), lambda s, i, w: (s, i, 0))),
        out_shape=jax.ShapeDtypeStruct(got.shape, BF16),
        compiler_params=_params("parallel", "parallel"),
    )(where, grad, got)


def _chip_exchange_rider(partials, landing):
    n = len(partials)

    def copies(srcs, dsts, send, recv):
        x, y, c = _place()
        return [_remote(srcs[w].at[2 * chip[0] + chip[1]], dsts[w].at[j], send, recv, 3 * w + j, (*chip, c))
                for w in range(n) for j, chip in enumerate(_other_chips(x, y))]

    def start(srcs, dsts, send, recv):
        for cp in copies(srcs, dsts, send, recv):
            cp.start()

    def finish(srcs, dsts, send, recv):
        for cp in copies(srcs, dsts, send, recv):
            cp.wait()

    return _Rider(partials, landing, 3 * n, start, finish)


def _final_add(name, where, grad, got, arrived):
    _, hr, cols = got.shape
    tr = _row_tile(hr)
    nblk = hr // tr

    def body(w_ref, g_ref, r_ref, a_ref, o_ref):
        acc = g_ref[...] + r_ref[...]
        for j in range(3):
            acc = acc + a_ref[j].astype(F32)
        o_ref[...] = acc

    return pl.pallas_call(
        body, name=name,
        grid_spec=pltpu.PrefetchScalarGridSpec(
            num_scalar_prefetch=1, grid=(nblk,),
            in_specs=[pl.BlockSpec((None, tr, cols), lambda i, w: (w[0], w[1] * nblk + i, 0)),
                      pl.BlockSpec((None, tr, cols), lambda i, w: (w[0], i, 0)),
                      pl.BlockSpec((3, tr, cols), lambda i, w: (0, i, 0))],
            out_specs=pl.BlockSpec((tr, cols), lambda i, w: (w[1] * nblk + i, 0))),
        out_shape=jax.ShapeDtypeStruct((2 * hr, cols), F32),
        compiler_params=_params("parallel"),
    )(where, grad, got, arrived)


def _pair_share_rider(shards):
    n = len(shards)

    def half(refs, w, which):
        hr = shards[w].shape[0] // 2
        return refs[w].at[pl.ds(which * hr, hr)]

    def start(_, refs, send, recv):
        x, y, c = _place()
        for w in range(n):
            _remote(half(refs, w, c), half(refs, w, c), send, recv, w, (x, y, 1 - c)).start()

    def finish(_, refs, send, recv):
        x, y, c = _place()
        for w in range(n):
            _remote(half(refs, w, 1 - c), half(refs, w, 1 - c), send, recv, w, (x, y, 1 - c)).wait_recv()
        for w in range(n):
            _remote(half(refs, w, c), half(refs, w, c), send, recv, w, (x, y, 1 - c)).wait_send()

    return _Rider([], shards, n, start, finish)


class _Exchange:
    PLAN = {"bias_expand": [("early", "gather")], "norm1": [("early", "forward")],
            "attn_fwd": [("late", "gather")], "conv_fwd": [("late", "forward")],
            "d_h2": [("mlp", "pair")], "attn_bwd": [("mlp", "chips"), ("proj", "pair")], "conv_bwd": [("mlp", "share")],
            "g_in": [("proj", "chips")], "g_gate": [("proj", "share")],
            "d_h_gate": [("in", "pair")], "d_h_proj": [("in", "chips")], "b_gate_sum": [("in", "share")]}

    def __init__(self, where, early_slots, late_slots):
        self.where = where
        self.slots = dict(early=early_slots, late=late_slots)
        self.stage = {g: dict(zip(("gather", "forward"), _gather_riders(s))) for g, s in self.slots.items()}
        self.groups, self.reduced, self.pending = {}, {}, []

    def early_weights(self):
        w_in3, w_gate3, small = self.slots["early"]
        conv_w = small[:, :3, :].transpose(1, 0, 2).reshape(3, N_SHARD * small.shape[2])
        return w_in3, w_gate3, conv_w

    def late_weights(self):
        rows = lambda a: a.reshape(a.shape[0] * a.shape[1], a.shape[2])
        w_ap, w_cp, w_out, w_up3, w_down = self.slots["late"]
        return rows(w_ap), rows(w_cp), rows(w_out), w_up3, rows(w_down)

    def grads_ready(self, group, grads):
        names = list(grads)
        g4 = [g if g.ndim == 3 else g.reshape(N_SHARD, -1, g.shape[1]) for g in grads.values()]
        self.groups[group] = dict(names=names, g4=g4)

    def riders(self, at):
        self.pending = self.PLAN.get(at, [])
        out = []
        for group, stage in self.pending:
            if group in self.slots:
                out.append(self.stage[group][stage](self.slots[group]))
                continue
            st = self.groups[group]
            if stage == "pair":
                landing = [lax.empty((N_SHARD, g.shape[1] // 2, g.shape[2]), F32) for g in st["g4"]]
                out.append(_pair_exchange_rider(st["g4"], landing))
            elif stage == "chips":
                landing = [lax.empty((3,) + p.shape[1:], p.dtype) for p in st["partial"]]
                out.append(_chip_exchange_rider(st["partial"], landing))
            else:
                out.append(_pair_share_rider(st["halves"]))
        return out

    def done(self, at, carried):
        for (group, stage), arrays in zip(self.pending, carried):
            if group in self.slots:
                self.slots[group] = arrays
                continue
            st = self.groups[group]
            tag = lambda what, n: what + "_" + n
            if stage == "pair":
                st["got"] = arrays
                st["partial"] = [_pair_add(tag("pair_add", n), self.where, g, r)
                                 for n, g, r in zip(st["names"], st["g4"], arrays)]
            elif stage == "chips":
                st["halves"] = [_final_add(tag("final_add", n), self.where, g, r, a)
                                for n, g, r, a in zip(st["names"], st["g4"], st["got"], arrays)]
            else:
                self.reduced.update(zip(st["names"], arrays))


SMALL_ROWS = 32
N_DEV = 8


def _all_reduce_small(pack):
    def body(p_ref, o_ref, buf, send, recv):
        x, y, c = _place()
        buf[4 * x + 2 * y + c] = p_ref[...]
        copies, waits = [], []
        for k in range(1, N_DEV):
            px = 1 - x if k & 4 else x
            py = 1 - y if k & 2 else y
            pc = 1 - c if k & 1 else c
            copies.append(pltpu.make_async_remote_copy(
                src_ref=p_ref, dst_ref=buf.at[4 * x + 2 * y + c], send_sem=send.at[k - 1],
                recv_sem=recv.at[k - 1], device_id=(px, py, pc), device_id_type=MESH))
            waits.append(pltpu.make_async_remote_copy(
                src_ref=p_ref, dst_ref=buf.at[4 * px + 2 * py + pc], send_sem=send.at[k - 1],
                recv_sem=recv.at[k - 1], device_id=(px, py, pc), device_id_type=MESH))
        for cp in copies:
            cp.start()
        for cp in waits:
            cp.wait_recv()
        acc = buf[0]
        for d in range(1, N_DEV):
            acc = acc + buf[d]
        o_ref[...] = acc
        for cp in copies:
            cp.wait_send()

    return pl.pallas_call(
        body, name="all_reduce_small",
        out_shape=jax.ShapeDtypeStruct(pack.shape, F32),
        scratch_shapes=[pltpu.VMEM((N_DEV,) + pack.shape, F32),
                        pltpu.SemaphoreType.DMA((N_DEV - 1,)), pltpu.SemaphoreType.DMA((N_DEV - 1,))],
    )(pack)


def _adamw(name, w, g, m, v):
    c1 = 1.0 - ADAM_B1 ** ADAM_STEP
    c2 = 1.0 - ADAM_B2 ** ADAM_STEP

    def fn(t, f):
        wv, gv, mv, vv = t
        m2 = ADAM_B1 * mv + (1.0 - ADAM_B1) * gv
        v2 = ADAM_B2 * vv + (1.0 - ADAM_B2) * (gv * gv)
        delta = -ADAM_LR * ((m2 / c1) / (jnp.sqrt(v2 / c2) + ADAM_EPS) + ADAM_WD * wv)
        return [delta, m2, v2], []

    cols = w.shape[1]
    return _ew(name, fn, [w, g, m, v], [], [(cols, F32)] * 3, ts=min(w.shape[0], 256))


LOSS_ROW = 26


def _pack_small(norm1_g, norm2_g, conv_b, b_gate, conv_w, q_norm_g, k_norm_g, rel_bias, loss=None):
    pack = jnp.zeros((SMALL_ROWS, D_MODEL), F32)
    for r0, v in ((0, norm1_g), (1, norm2_g), (2, conv_b), (3, b_gate.reshape(2, D_MODEL)), (5, conv_w),
                  (8, q_norm_g), (9, k_norm_g), (10, rel_bias)) + (((LOSS_ROW, loss),) if loss is not None else ()):
        v = v.reshape(-1, v.shape[-1]).astype(F32)
        pack = pack.at[r0:r0 + v.shape[0], :v.shape[1]].set(v)
    return pack


def _unpack_small(pack, conv_cols):
    return dict(norm1_g=pack[0], norm2_g=pack[1], conv_b=pack[2], b_gate=pack[3:5].reshape(2 * D_MODEL),
                conv_w=pack[5:8, :conv_cols], q_norm_g=pack[8, :HEAD_DIM], k_norm_g=pack[9, :HEAD_DIM],
                rel_bias=pack[10:10 + N_HEADS, :N_REL])


BIG = ["w_in", "w_attn_proj", "w_conv_proj", "w_gate", "w_out", "w_up", "w_down"]
LATE = ["w_attn_proj", "w_conv_proj", "w_out", "w_up", "w_down"]
WEIGHTS = ["norm1_g", "w_in", "q_norm_g", "k_norm_g", "rel_bias", "conv_w", "conv_b", "w_attn_proj",
           "w_conv_proj", "w_gate", "b_gate", "w_out", "norm2_g", "w_up", "w_down"]


def kernel(x, norm1_g, w_in, q_norm_g, k_norm_g, rel_bias, conv_w, conv_b, w_attn_proj, w_conv_proj, w_gate, b_gate, w_out, norm2_g, w_up, w_down, loss_target, m_norm1_g, m_w_in, m_q_norm_g, m_k_norm_g, m_rel_bias, m_conv_w, m_conv_b, m_w_attn_proj, m_w_conv_proj, m_w_gate, m_b_gate, m_w_out, m_norm2_g, m_w_up, m_w_down, v_norm1_g, v_w_in, v_q_norm_g, v_k_norm_g, v_rel_bias, v_conv_w, v_conv_b, v_w_attn_proj, v_w_conv_proj, v_w_gate, v_b_gate, v_w_out, v_norm2_g, v_w_up, v_w_down):
    given = dict(locals())
    w = {n: given[n] for n in WEIGHTS}
    m = {n: given["m_" + n] for n in WEIGHTS}
    v = {n: given["v_" + n] for n in WEIGHTS}
    s_len = x.shape[1]
    shard = 2 * lax.axis_index("x") + lax.axis_index("y")
    where = jnp.stack([shard, lax.axis_index("c")]).astype(jnp.int32)
    conv_cols = conv_w.shape[1]

    small_in = lax.dynamic_update_slice(jnp.zeros((N_SHARD, 16, conv_cols), F32), conv_w[None], (shard, 0, 0))
    slot = {n: _cast_into_slot("cast_" + n, where, w[n]) for n in BIG}
    comm = _Exchange(where, [slot["w_in"], slot["w_gate"], small_in], [slot[n] for n in LATE])

    loss_part, grad_x, _, small = _local_grads(
        x.reshape(s_len, D_MODEL), loss_target.reshape(s_len, D_MODEL), norm1_g, q_norm_g, k_norm_g,
        rel_bias, conv_b, b_gate, norm2_g, comm)
    grad = dict(comm.reduced)

    loss_local = _finish_loss(loss_part)
    pack = _pack_small(small["norm1_g"], small["norm2_g"], small["conv_wb"][3], jnp.concatenate(small["b_gate"], axis=1),
                       small["conv_wb"][0:3], small["q_norm_g"], small["k_norm_g"], small["rel_bias"],
                       loss=loss_local[0:1, :])
    total = _all_reduce_small(pack)
    g_small = _unpack_small(total, D_MODEL)
    g_small["conv_w"] = lax.dynamic_slice(g_small["conv_w"], (0, shard * conv_cols), (3, conv_cols))
    grad.update(g_small)

    delta, new_m, new_v = {}, {}, {}
    for n in BIG:
        delta[n], new_m[n], new_v[n] = _adamw("adamw_" + n, w[n], grad[n], m[n], v[n])
    small_names = [n for n in WEIGHTS if n not in BIG]
    packs = [_pack_small(**{n: src[n] for n in small_names}) for src in (w, grad, m, v)]
    for out, packed in zip((delta, new_m, new_v), _adamw("adamw_small", *packs)):
        out.update({n: a.reshape(w[n].shape) for n, a in _unpack_small(packed, conv_cols).items()})

    outs = [total[LOSS_ROW, 0], grad_x.reshape(x.shape)]
    for group in (grad, delta, new_m, new_v):
        outs += [group[n].reshape(w[n].shape) for n in WEIGHTS]
    return tuple(outs)
```

```python
import functools

import jax
import jax.numpy as jnp
from jax import lax
from jax.experimental import pallas as pl
from jax.experimental.pallas import tpu as pltpu

F32 = jnp.float32
BF16 = jnp.bfloat16

D_MODEL = 1024
N_HEADS = 16
HEAD_DIM = 64
CHUNK = 64
N_PREV_CHUNKS = 8
MAX_REL = 256
D_FF = 4096
N_REL = 2 * MAX_REL + 1
REL_PAD = 640
EPS = 1e-6
NEG_INF = -1e30
QK_SCALE = HEAD_DIM ** -0.5

SUPER = 4 * CHUNK
BAND = SUPER + N_PREV_CHUNKS * CHUNK
SKEW_W = 1024
N_SHARD = 4
LANE = 128
VMEM_LIMIT = 48 * 1024 * 1024

ADAM_LR = 0.001
ADAM_B1 = 0.9
ADAM_B2 = 0.999
ADAM_EPS = 1e-08
ADAM_WD = 0.01
ADAM_STEP = 10

MESH = pl.DeviceIdType.MESH
NN = (((1,), (0,)), ((), ()))
NT = (((1,), (1,)), ((), ()))
TN = (((0,), (0,)), ((), ()))


def _params(*sem):
    return pltpu.CompilerParams(dimension_semantics=sem or None, vmem_limit_bytes=VMEM_LIMIT)


HBM_SPEC = pl.BlockSpec(memory_space=pl.ANY)


class _Rider:
    def __init__(self, sources, arrays, n_sem, start, finish):
        self.sources, self.arrays, self.n_sem, self.start, self.finish = sources, arrays, n_sem, start, finish


def _carry(riders, body, *, name, out_shape, grid=(), in_specs=None, out_specs=None, scratch_shapes=(),
           semantics=(), input_output_aliases=None):
    aliases = dict(input_output_aliases or {})
    if not riders:
        kw = {} if in_specs is None else dict(in_specs=in_specs, out_specs=out_specs)
        return pl.pallas_call(body, name=name, grid=grid, out_shape=out_shape, scratch_shapes=scratch_shapes,
                              input_output_aliases=aliases, compiler_params=_params(*semantics), **kw)
    single = not isinstance(out_shape, (list, tuple))
    shapes = [out_shape] if single else list(out_shape)
    n_out, n_scr = len(shapes), len(scratch_shapes)
    srcs = [a for r in riders for a in r.sources]
    arrs = [a for r in riders for a in r.arrays]
    vmem = pl.BlockSpec(memory_space=pltpu.VMEM)

    def run(*args):
        n_in = len(args)

        def wrapped(*refs):
            pos = n_in
            src_refs = refs[pos:pos + len(srcs)]
            pos += len(srcs) + len(arrs)
            outs = refs[pos:pos + n_out]
            pos += n_out
            arr_refs = refs[pos:pos + len(arrs)]
            pos += len(arrs)
            scratch = refs[pos:pos + n_scr]
            sems = refs[pos + n_scr:]
            first, last = True, True
            for d, size in enumerate(grid):
                first = jnp.logical_and(first, pl.program_id(d) == 0)
                last = jnp.logical_and(last, pl.program_id(d) == size - 1)

            def each(method):
                s0 = a0 = 0
                for k, r in enumerate(riders):
                    getattr(r, method)(src_refs[s0:s0 + len(r.sources)], arr_refs[a0:a0 + len(r.arrays)],
                                       sems[2 * k], sems[2 * k + 1])
                    s0, a0 = s0 + len(r.sources), a0 + len(r.arrays)

            pl.when(first)(lambda: each("start"))
            body(*refs[:n_in], *outs, *scratch)
            pl.when(last)(lambda: each("finish"))

        ins = [vmem] * n_in if in_specs is None else list(in_specs)
        if out_specs is None:
            o_specs = [vmem] * n_out
        else:
            o_specs = [out_specs] if single else list(out_specs)
        for k in range(len(arrs)):
            aliases[n_in + len(srcs) + k] = n_out + k
        res = pl.pallas_call(
            wrapped, name=name, grid=grid,
            in_specs=ins + [HBM_SPEC] * (len(srcs) + len(arrs)),
            out_specs=o_specs + [HBM_SPEC] * len(arrs),
            out_shape=shapes + [jax.ShapeDtypeStruct(a.shape, a.dtype) for a in arrs],
            scratch_shapes=list(scratch_shapes) + [pltpu.SemaphoreType.DMA((r.n_sem,)) for r in riders for _ in range(2)],
            input_output_aliases=aliases,
            compiler_params=_params(*["arbitrary"] * len(grid)),
        )(*args, *srcs, *arrs)
        core, rest = res[:n_out], list(res[n_out:])
        carried, a0 = [], 0
        for r in riders:
            carried.append(rest[a0:a0 + len(r.arrays)])
            a0 += len(r.arrays)
        return (core[0] if single else core), carried

    return run


def _mm(name, dims, a, a_spec, b, b_spec, grid, tile, outs, epilogue=None, extras=(), riders=()):
    nk, ne, no = grid[2], len(extras), len(outs)

    def body(a_ref, b_ref, *refs):
        e_refs, o_refs = refs[:ne], refs[ne:ne + no]
        part = lax.dot_general(a_ref[...], b_ref[...], dims, preferred_element_type=F32)

        def finish(acc):
            if epilogue is None:
                o_refs[0][...] = acc.astype(o_refs[0].dtype)
            else:
                epilogue(acc, [r[...] for r in e_refs], o_refs)

        if nk == 1:
            finish(part)
        else:
            acc_ref = refs[ne + no]
            k = pl.program_id(2)

            @pl.when(k == 0)
            def _():
                acc_ref[...] = part

            @pl.when(k > 0)
            def _():
                acc_ref[...] += part

            @pl.when(k == nk - 1)
            def _():
                finish(acc_ref[...])

    res = _carry(
        riders, body, name=name, grid=grid,
        in_specs=[a_spec, b_spec] + [s for _, s in extras],
        out_specs=[s for _, s in outs],
        out_shape=[s for s, _ in outs],
        scratch_shapes=[pltpu.VMEM(tile, F32)] if nk > 1 else [],
        semantics=("parallel", "parallel", "arbitrary"),
    )(a, b, *[e for e, _ in extras])
    res, carried = res if riders else (res, None)
    res = res[0] if no == 1 else res
    return (res, carried) if riders else res


def _tile_spec(tm, tn, col0=0):
    return pl.BlockSpec((tm, tn), lambda i, j, k: (i, j + col0))


def _out2d(m, n, dtype, tm, tn):
    return (jax.ShapeDtypeStruct((m, n), dtype), _tile_spec(tm, tn))


def _mm_fwd(name, a, w, tm, tn, tk, outs=None, epilogue=None, extras=(), col0=0, ncols=None, riders=()):
    m, kdim = a.shape
    if w.ndim == 3:
        per = w.shape[2] // tn
        n = ncols or N_SHARD * w.shape[2]
        w_spec = pl.BlockSpec((None, tk, tn), lambda i, j, k: ((j + col0) // per, k, (j + col0) % per))
    else:
        n = ncols or w.shape[1]
        w_spec = pl.BlockSpec((tk, tn), lambda i, j, k: (k, j + col0))
    if outs is None:
        outs = [_out2d(m, n, F32, tm, tn)]
    return _mm(name, NN, a, pl.BlockSpec((tm, tk), lambda i, j, k: (i, k)), w, w_spec,
               (m // tm, n // tn, kdim // tk), (tm, tn), outs, epilogue, extras, riders)


def _mm_bwd_x(name, g, g_spec, w, tm, tj, tc, m, n_contract, outs=None, epilogue=None, extras=(), riders=()):
    if w.ndim == 3:
        per = w.shape[2] // tc
        kdim = w.shape[1]
        w_spec = pl.BlockSpec((None, tj, tc), lambda i, j, n: (n // per, j, n % per))
    else:
        kdim = w.shape[0]
        w_spec = pl.BlockSpec((tj, tc), lambda i, j, n: (j, n))
    if outs is None:
        outs = [_out2d(m, kdim, F32, tm, tj)]
    return _mm(name, NT, g, g_spec, w, w_spec, (m // tm, kdim // tj, n_contract // tc),
               (tm, tj), outs, epilogue, extras, riders)


def _mm_bwd_w(name, a, g, g_spec, n, tk, tn, tm, sharded, riders=()):
    m, kdim = a.shape
    if sharded:
        per = (n // N_SHARD) // tn
        out = (jax.ShapeDtypeStruct((N_SHARD, kdim, n // N_SHARD), F32),
               pl.BlockSpec((None, tk, tn), lambda i, j, mm: (j // per, i, j % per)))
    else:
        out = (jax.ShapeDtypeStruct((kdim, n), F32), pl.BlockSpec((tk, tn), lambda i, j, mm: (i, j)))
    return _mm(name, TN, a, pl.BlockSpec((tm, tk), lambda i, j, mm: (mm, i)), g, g_spec,
               (kdim // tk, n // tn, m // tm), (tk, tn), [out], riders=riders)


def _ew(name, fn, tiles, fulls, outs, sums=(), ts=512, riders=()):
    tiles = [t if isinstance(t, tuple) else (t, pl.BlockSpec((ts, t.shape[1]), lambda i: (i, 0)))
             for t in tiles]
    s_rows = tiles[0][0].shape[-2]
    nt, nf, no = len(tiles), len(fulls), len(outs)

    def body(*refs):
        t_vals = [r[...] for r in refs[:nt]]
        f_vals = [r[...] for r in refs[nt:nt + nf]]
        o_refs, s_refs = refs[nt + nf:nt + nf + no], refs[nt + nf + no:]
        o_vals, s_vals = fn(t_vals, f_vals)
        for r, v in zip(o_refs, o_vals):
            r[...] = v.astype(r.dtype)
        for r, v in zip(s_refs, s_vals):
            part = jnp.sum(v, axis=0, keepdims=True)

            @pl.when(pl.program_id(0) == 0)
            def _():
                r[...] = part

            @pl.when(pl.program_id(0) > 0)
            def _():
                r[...] += part

    full_specs = [pl.BlockSpec(f.shape, lambda i, nd=f.ndim: (0,) * nd) for f in fulls]
    return _carry(
        riders, body, name=name, grid=(s_rows // ts,),
        in_specs=[s for _, s in tiles] + full_specs,
        out_specs=[pl.BlockSpec((ts, c), lambda i: (i, 0)) for c, _ in outs]
        + [pl.BlockSpec((1, c), lambda i: (0, 0)) for c in sums],
        out_shape=[jax.ShapeDtypeStruct((s_rows, c), dt) for c, dt in outs]
        + [jax.ShapeDtypeStruct((1, c), F32) for c in sums],
        semantics=("arbitrary",),
    )(*[t for t, _ in tiles], *fulls)


def _rms_fwd(name, x, g, riders=()):
    def fn(t, f):
        xv = t[0]
        r = lax.rsqrt(jnp.mean(xv * xv, axis=-1, keepdims=True) + EPS)
        return [xv * r * f[0]], []
    out = _ew(name, fn, [x], [g], [(x.shape[1], BF16)], riders=riders)
    return (out[0][0], out[1]) if riders else out[0]


def _rms_bwd(name, dh, x, g, dres, out_dtype=BF16, riders=()):
    def fn(t, f):
        dhv, xv, dr = [v.astype(F32) for v in t]
        r = lax.rsqrt(jnp.mean(xv * xv, axis=-1, keepdims=True) + EPS)
        xhat = xv * r
        u = dhv * f[0]
        dx = r * (u - xhat * jnp.mean(u * xhat, axis=-1, keepdims=True)) + dr
        return [dx], [dhv * xhat]
    c = x.shape[1]
    return _ew(name, fn, [dh, x, dres], [g], [(c, out_dtype)], sums=[c], riders=riders)


def _split3(x):
    x1 = x.astype(BF16)
    r1 = x - x1.astype(F32)
    x2 = r1.astype(BF16)
    x3 = (r1 - x2.astype(F32)).astype(BF16)
    return x1, x2, x3


def _rel_class(cp):
    far = (cp < MAX_REL) | (cp > BAND)
    return jnp.where(far, 2 * MAX_REL, BAND - cp)


def _skew_rows(x, sign):
    row = lax.broadcasted_iota(jnp.int32, x.shape, 0)
    for b in range(CHUNK.bit_length() - 1):
        shift = (1 << b) if sign > 0 else SKEW_W - (1 << b)
        x = jnp.where((row >> b) & 1 == 1, pltpu.roll(x, shift, 1), x)
    return x


def _roll_lanes(x, shift):
    return x if shift % SKEW_W == 0 else pltpu.roll(x, shift % SKEW_W, 1)


def _bias_expand(rel_bias, riders=()):
    rel = jnp.pad(rel_bias, ((0, 0), (0, REL_PAD - N_REL))).reshape(N_HEADS, 1, REL_PAD)

    def body(rel_ref, o_ref):
        cls = lax.broadcasted_iota(jnp.int32, (REL_PAD, SKEW_W), 0)
        cp = lax.broadcasted_iota(jnp.int32, (REL_PAD, SKEW_W), 1)
        onehot = (cls == _rel_class(cp)).astype(BF16)
        rel8 = jnp.broadcast_to(rel_ref[...], (8, REL_PAD))
        trow = sum(jnp.dot(p, onehot, preferred_element_type=F32) for p in _split3(rel8))[0:1]
        first = _skew_rows(jnp.broadcast_to(trow, (CHUNK, SKEW_W)), +1)
        full = jnp.concatenate([_roll_lanes(first, CHUNK * g) for g in range(SUPER // CHUNK)], axis=0)[:, :BAND]
        qc = lax.broadcasted_iota(jnp.int32, (SUPER, BAND), 0) // CHUNK
        kc = lax.broadcasted_iota(jnp.int32, (SUPER, BAND), 1) // CHUNK
        on_band = (kc >= qc) & (kc <= qc + N_PREV_CHUNKS)
        o_ref[...] = jnp.where(on_band, full, NEG_INF).T

    return _carry(
        riders, body, name="bias_expand", grid=(N_HEADS,),
        in_specs=[pl.BlockSpec((None, 1, REL_PAD), lambda h: (h, 0, 0))],
        out_specs=pl.BlockSpec((None, BAND, SUPER), lambda h: (h, 0, 0)),
        out_shape=jax.ShapeDtypeStruct((N_HEADS, BAND, SUPER), F32),
        semantics=("arbitrary",),
    )(rel)


def _bias_reduce(dbias):
    def body(d_ref, o_ref):
        x = jnp.concatenate([d_ref[...].T, jnp.zeros((SUPER, SKEW_W - BAND), F32)], axis=1)
        folded = sum(_roll_lanes(x[CHUNK * g:CHUNK * (g + 1)], -CHUNK * g) for g in range(SUPER // CHUNK))
        diag = jnp.sum(_skew_rows(folded, -1), axis=0, keepdims=True)
        cp = lax.broadcasted_iota(jnp.int32, (SKEW_W, REL_PAD), 0)
        cls = lax.broadcasted_iota(jnp.int32, (SKEW_W, REL_PAD), 1)
        onehot = (cls == _rel_class(cp)).astype(BF16)
        diag8 = jnp.broadcast_to(diag, (8, SKEW_W))
        o_ref[...] = sum(jnp.dot(p, onehot, preferred_element_type=F32) for p in _split3(diag8))[0:1]

    out = pl.pallas_call(
        body, name="bias_reduce", grid=(N_HEADS,),
        in_specs=[pl.BlockSpec((None, BAND, SUPER), lambda h: (h, 0, 0))],
        out_specs=pl.BlockSpec((None, 1, REL_PAD), lambda h: (h, 0, 0)),
        out_shape=jax.ShapeDtypeStruct((N_HEADS, 1, REL_PAD), F32),
        compiler_params=_params("arbitrary"),
    )(dbias)
    return out.reshape(N_HEADS, REL_PAD)[:, :N_REL]


HEADS_PER_STEP = 8
HEAD_COLS = HEADS_PER_STEP * HEAD_DIM
N_HEAD_GROUPS = N_HEADS // HEADS_PER_STEP


def _unit(x):
    r = lax.rsqrt(jnp.mean(x * x, axis=-1, keepdims=True) + EPS)
    return x * r, r


def _scores_t(qs, kn, bias_t, dead):
    s = jnp.concatenate([lax.dot_general(k, qs, NT, preferred_element_type=F32) for k in kn], axis=0) + bias_t
    return jnp.where(dead, NEG_INF, s)


def _band_specs(nb, col0, clamp_hi):
    def spec(d):
        def index(hg, i):
            blk = jnp.maximum(i - d, 0)
            if clamp_hi:
                blk = jnp.minimum(blk, nb - 1)
            return (blk, col0 + hg)
        return pl.BlockSpec((SUPER, HEAD_COLS), index)
    return [spec(2), spec(1), spec(0)]


def _head_sums(y):
    cols = y.shape[1]
    same_head = (lax.broadcasted_iota(jnp.int32, (cols, cols), 0) // HEAD_DIM
                 == lax.broadcasted_iota(jnp.int32, (cols, cols), 1) // HEAD_DIM).astype(BF16)
    hi = y.astype(BF16)
    lo = (y - hi.astype(F32)).astype(BF16)
    return (jnp.dot(hi, same_head, preferred_element_type=F32)
            + jnp.dot(lo, same_head, preferred_element_type=F32))


def _head_unit(x):
    r = lax.rsqrt(_head_sums(x * x) * (1.0 / HEAD_DIM) + EPS)
    return x * r, r


def _head(hh):
    return slice(HEAD_DIM * hh, HEAD_DIM * (hh + 1))


def _key_block(j):
    return slice(SUPER * j, SUPER * (j + 1))


def _dead_keys(i):
    key = lax.broadcasted_iota(jnp.int32, (BAND, SUPER), 0)
    return key < (2 - i) * SUPER


LSE_ROWS = 8


def _attn_fwd(qkn, v, bias, riders=()):
    s_len = qkn.shape[0]
    nb = s_len // SUPER

    def body(q_ref, k0, k1, k2, v0, v1, v2, b_ref, o_ref, lse_ref):
        dead = _dead_keys(pl.program_id(1))
        outs = []

        def probabilities(hh):
            sl = _head(hh)
            s = _scores_t(q_ref[:, sl], [k0[:, sl], k1[:, sl], k2[:, sl]], b_ref[hh], dead)
            m = jnp.max(s, axis=0, keepdims=True)
            e = jnp.exp(s - m)
            l = jnp.sum(e, axis=0, keepdims=True)
            lse_ref[hh:hh + 1, :] = m + jnp.log(l)
            return (e * (1.0 / l)).astype(BF16), sl

        def weighted_values(p, sl):
            outs.append(sum(lax.dot_general(p[_key_block(j), :], vj[:, sl], TN, preferred_element_type=F32)
                            for j, vj in enumerate((v0, v1, v2))))

        ready = probabilities(0)
        for hh in range(1, HEADS_PER_STEP):
            following = probabilities(hh)
            weighted_values(*ready)
            ready = following
        weighted_values(*ready)
        o_ref[...] = jnp.concatenate(outs, axis=1).astype(o_ref.dtype)

    return _carry(
        riders, body, name="attn_fwd", grid=(N_HEAD_GROUPS, nb),
        in_specs=[pl.BlockSpec((SUPER, HEAD_COLS), lambda hg, i: (i, hg))]
        + _band_specs(nb, N_HEAD_GROUPS, False) + _band_specs(nb, 0, False)
        + [pl.BlockSpec((HEADS_PER_STEP, BAND, SUPER), lambda hg, i: (hg, 0, 0))],
        out_specs=[pl.BlockSpec((SUPER, HEAD_COLS), lambda hg, i: (i, hg)),
                   pl.BlockSpec((None, LSE_ROWS, SUPER), lambda hg, i: (hg, 0, i))],
        out_shape=[jax.ShapeDtypeStruct((s_len, D_MODEL), BF16),
                   jax.ShapeDtypeStruct((N_HEAD_GROUPS, LSE_ROWS, s_len), F32)],
        semantics=("parallel", "arbitrary"),
    )(qkn, qkn, qkn, qkn, v, v, v, bias)


def _attn_bwd(qkn, v, d_out, bias, lse, riders=()):
    s_len = qkn.shape[0]
    nb = s_len // SUPER

    def body(q_ref, k0, k1, k2, v0, v1, v2, do_ref, b_ref, lse_ref, dp_ref, db_ref, aq_ref, ak_ref, av_ref):
        i = pl.program_id(1)

        @pl.when(i == 0)
        def _():
            aq_ref[...] = jnp.zeros_like(aq_ref)
            ak_ref[...] = jnp.zeros_like(ak_ref)
            av_ref[...] = jnp.zeros_like(av_ref)
            db_ref[...] = jnp.zeros_like(db_ref)

        @pl.when(i < nb)
        def _():
            dead = _dead_keys(i)
            dq, dk, dv = [], [[], [], []], [[], [], []]

            def softmax_grad(hh):
                sl = _head(hh)
                qs, do = q_ref[:, sl], do_ref[:, sl]
                kn = [k0[:, sl], k1[:, sl], k2[:, sl]]
                vv = [v0[:, sl], v1[:, sl], v2[:, sl]]
                p = jnp.exp(_scores_t(qs, kn, b_ref[hh], dead) - lse_ref[hh:hh + 1, :])
                dp = jnp.concatenate([lax.dot_general(vj, do, NT, preferred_element_type=F32) for vj in vv], axis=0)
                ds = p * (dp - jnp.sum(p * dp, axis=0, keepdims=True))
                db_ref[hh] += ds
                return p.astype(BF16), ds.astype(BF16), qs, do, kn

            def operand_grads(pb, dsb, qs, do, kn):
                dq.append(sum(lax.dot_general(dsb[_key_block(j), :], kn[j], TN, preferred_element_type=F32)
                              for j in range(3)))
                for j in range(3):
                    dv[j].append(jnp.dot(pb[_key_block(j), :], do, preferred_element_type=F32))
                    dk[j].append(jnp.dot(dsb[_key_block(j), :], qs, preferred_element_type=F32))

            ready = softmax_grad(0)
            for hh in range(1, HEADS_PER_STEP):
                following = softmax_grad(hh)
                operand_grads(*ready)
                ready = following
            operand_grads(*ready)
            aq_ref[i % 3] = jnp.concatenate(dq, axis=1)
            for j in range(3):
                slot = (i + 1 + j) % 3
                if j < 2:
                    ak_ref[slot] += jnp.concatenate(dk[j], axis=1)
                    av_ref[slot] += jnp.concatenate(dv[j], axis=1)
                else:
                    ak_ref[slot] = jnp.concatenate(dk[j], axis=1)
                    av_ref[slot] = jnp.concatenate(dv[j], axis=1)

        slot = (i + 1) % 3
        dp_ref[0] = aq_ref[slot].astype(dp_ref.dtype)
        dp_ref[1] = ak_ref[slot].astype(dp_ref.dtype)
        dp_ref[2] = av_ref[slot].astype(dp_ref.dtype)

    def qrow(hg, i):
        return (jnp.minimum(i, nb - 1), hg)

    return _carry(
        riders, body, name="attn_bwd", grid=(N_HEAD_GROUPS, nb + 2),
        in_specs=[pl.BlockSpec((SUPER, HEAD_COLS), qrow)]
        + _band_specs(nb, N_HEAD_GROUPS, True) + _band_specs(nb, 0, True)
        + [pl.BlockSpec((SUPER, HEAD_COLS), qrow),
           pl.BlockSpec((HEADS_PER_STEP, BAND, SUPER), lambda hg, i: (hg, 0, 0)),
           pl.BlockSpec((None, LSE_ROWS, SUPER), lambda hg, i: (hg, 0, jnp.minimum(i, nb - 1)))],
        out_specs=[pl.BlockSpec((3, SUPER, HEAD_COLS), lambda hg, i: (0, jnp.maximum(i - 2, 0), hg)),
                   pl.BlockSpec((HEADS_PER_STEP, BAND, SUPER), lambda hg, i: (hg, 0, 0))],
        out_shape=[jax.ShapeDtypeStruct((6, s_len, D_MODEL), BF16),
                   jax.ShapeDtypeStruct((N_HEADS, BAND, SUPER), F32)],
        scratch_shapes=[pltpu.VMEM((3, SUPER, HEAD_COLS), F32)] * 3,
        semantics=("parallel", "arbitrary"),
    )(qkn, qkn, qkn, qkn, v, v, v, d_out, bias, lse)


def _qk_norm_bwd(dproj6, qk_raw, gq, gk):
    s_len = qk_raw.shape[0]
    ts = min(1024, s_len)

    nsteps = s_len // ts
    half = D_MODEL // 2

    def body(d_ref, raw_ref, gq_ref, gk_ref, o_ref, dgq_ref, dgk_ref, acc_ref):
        step = pl.program_id(0)

        @pl.when(step == 0)
        def _():
            acc_ref[...] = jnp.zeros_like(acc_ref)

        for piece, (g_ref, scale) in enumerate(((gq_ref, QK_SCALE), (gk_ref, 1.0))):
            for c0 in (0, half):
                xhat, r = _head_unit(raw_ref[:, piece * D_MODEL + c0:piece * D_MODEL + c0 + half].astype(F32))
                dn = d_ref[piece, :, c0:c0 + half].astype(F32) * scale
                u = dn * g_ref[...]
                dx = r * (u - xhat * (_head_sums(u * xhat) * (1.0 / HEAD_DIM)))
                o_ref[piece, :, c0:c0 + half] = dx.astype(o_ref.dtype)
                acc_ref[piece:piece + 1, c0:c0 + half] += jnp.sum(dn * xhat, axis=0, keepdims=True)

        @pl.when(step == nsteps - 1)
        def _():
            lane = lax.broadcasted_iota(jnp.int32, (D_MODEL, LANE), 0) % HEAD_DIM
            fold = (lane == lax.broadcasted_iota(jnp.int32, (D_MODEL, LANE), 1)).astype(BF16)
            tot = sum(jnp.dot(p, fold, preferred_element_type=F32) for p in _split3(acc_ref[...]))
            dgq_ref[...] = tot[0:1, :HEAD_DIM]
            dgk_ref[...] = tot[1:2, :HEAD_DIM]

    gain = pl.BlockSpec((1, half), lambda i: (0, 0))
    small = pl.BlockSpec((1, HEAD_DIM), lambda i: (0, 0))
    per_head = lambda g: jnp.tile(g, (1, half // HEAD_DIM))
    return pl.pallas_call(
        body, name="qk_norm_bwd", grid=(nsteps,),
        in_specs=[pl.BlockSpec((2, ts, D_MODEL), lambda i: (0, i, 0)),
                  pl.BlockSpec((ts, 2 * D_MODEL), lambda i: (i, 0)), gain, gain],
        out_specs=[pl.BlockSpec((2, ts, D_MODEL), lambda i: (0, i, 0)), small, small],
        out_shape=[jax.ShapeDtypeStruct(dproj6.shape, dproj6.dtype),
                   jax.ShapeDtypeStruct((1, HEAD_DIM), F32), jax.ShapeDtypeStruct((1, HEAD_DIM), F32)],
        scratch_shapes=[pltpu.VMEM((8, D_MODEL), F32)],
        input_output_aliases={0: 0},
        compiler_params=_params("arbitrary"),
    )(dproj6, qk_raw, per_head(gq), per_head(gk))


CONV_ROWS = 512
HALO = 16


def _rows_with_halo(ref, r0, n, front, s_len):
    zeros = jnp.zeros((HALO, ref.shape[1]), F32)
    if front:
        return (jnp.concatenate([zeros, ref[0:n, :].astype(F32)], axis=0) if r0 == 0
                else ref[r0 - HALO:r0 + n, :].astype(F32))
    return (jnp.concatenate([ref[r0:r0 + n, :].astype(F32), zeros], axis=0) if r0 + n == s_len
            else ref[r0:r0 + n + HALO, :].astype(F32))


def _earlier(ext, k):
    return pltpu.roll(ext, k, 0)[HALO:]


def _later(ext, k):
    n = ext.shape[0]
    return pltpu.roll(ext, n - k, 0)[:n - HALO]


def _conv_cols(col0):
    return lambda s_len: pl.BlockSpec((s_len, LANE), lambda j: (0, col0 + j))


def _conv_fwd(proj, conv_w, conv_b, riders=()):
    s_len = proj.shape[0]

    def body(bg_ref, cg_ref, xc_ref, w_ref, b_ref, o_ref):
        w = [w_ref[t:t + 1, :] for t in range(3)]
        for r0 in range(0, s_len, CONV_ROWS):
            u = _rows_with_halo(cg_ref, r0, CONV_ROWS, True, s_len) * \
                _rows_with_halo(xc_ref, r0, CONV_ROWS, True, s_len)
            conv = b_ref[...] + w[0] * _earlier(u, 2) + w[1] * _earlier(u, 1) + w[2] * u[HALO:]
            o_ref[r0:r0 + CONV_ROWS, :] = (bg_ref[r0:r0 + CONV_ROWS, :].astype(F32) * conv).astype(o_ref.dtype)

    return _carry(
        riders, body, name="conv_fwd", grid=(D_MODEL // LANE,),
        in_specs=[_conv_cols(0)(s_len), _conv_cols(8)(s_len), _conv_cols(16)(s_len),
                  pl.BlockSpec((3, LANE), lambda j: (0, j)), pl.BlockSpec((1, LANE), lambda j: (0, j))],
        out_specs=pl.BlockSpec((s_len, LANE), lambda j: (0, j)),
        out_shape=jax.ShapeDtypeStruct((s_len, D_MODEL), BF16),
        semantics=("parallel",),
    )(proj, proj, proj, conv_w, conv_b)


def _conv_bwd(dproj6, dy, proj, conv_w, conv_b, riders=()):
    s_len = proj.shape[0]

    def body(dy_ref, bg_ref, cg_ref, xc_ref, w_ref, b_ref, _, dp_ref, dw_ref):
        w = [w_ref[t:t + 1, :] for t in range(3)]
        acc = [jnp.zeros((1, LANE), F32) for _ in range(4)]
        for r0 in range(0, s_len, CONV_ROWS):
            rows = slice(r0, r0 + CONV_ROWS)
            u = _rows_with_halo(cg_ref, r0, CONV_ROWS, True, s_len) * \
                _rows_with_halo(xc_ref, r0, CONV_ROWS, True, s_len)
            u2, u1, u0 = _earlier(u, 2), _earlier(u, 1), u[HALO:]
            conv = b_ref[...] + w[0] * u2 + w[1] * u1 + w[2] * u0
            dp_ref[0, rows, :] = (dy_ref[rows, :].astype(F32) * conv).astype(dp_ref.dtype)
            dconv_ext = _rows_with_halo(dy_ref, r0, CONV_ROWS, False, s_len) * \
                _rows_with_halo(bg_ref, r0, CONV_ROWS, False, s_len)
            dconv = dconv_ext[:CONV_ROWS]
            for t, term in enumerate([dconv * u2, dconv * u1, dconv * u0, dconv]):
                acc[t] = acc[t] + jnp.sum(term, axis=0, keepdims=True)
            du = w[2] * dconv + w[1] * _later(dconv_ext, 1) + w[0] * _later(dconv_ext, 2)
            dp_ref[1, rows, :] = (du * xc_ref[rows, :].astype(F32)).astype(dp_ref.dtype)
            dp_ref[2, rows, :] = (du * cg_ref[rows, :].astype(F32)).astype(dp_ref.dtype)
        dw_ref[...] = jnp.zeros_like(dw_ref)
        for t in range(4):
            dw_ref[t:t + 1, :] = acc[t]

    return _carry(
        riders, body, name="conv_bwd", grid=(D_MODEL // LANE,),
        in_specs=[pl.BlockSpec((s_len, LANE), lambda j: (0, j)),
                  _conv_cols(0)(s_len), _conv_cols(8)(s_len), _conv_cols(16)(s_len),
                  pl.BlockSpec((3, LANE), lambda j: (0, j)), pl.BlockSpec((1, LANE), lambda j: (0, j)),
                  pl.BlockSpec(memory_space=pl.ANY)],
        out_specs=[pl.BlockSpec((3, s_len, LANE), lambda j: (1, 0, j)),
                   pl.BlockSpec((8, LANE), lambda j: (0, j))],
        out_shape=[jax.ShapeDtypeStruct(dproj6.shape, dproj6.dtype),
                   jax.ShapeDtypeStruct((8, D_MODEL), F32)],
        input_output_aliases={6: 0},
        semantics=("parallel",),
    )(dy, proj, proj, proj, conv_w, conv_b, dproj6)


class _NoComm:
    def __init__(self, early, late):
        self.early, self.late = early, late

    def riders(self, at):
        return ()

    def done(self, at, carried):
        pass

    def early_weights(self):
        return self.early

    def late_weights(self):
        return self.late

    def grads_ready(self, group, grads):
        pass


def _d_input(name, pieces, w3, extra=None, riders=()):
    n_piece, s_len, width = pieces.shape
    _, kdim, per_shard = w3.shape
    tm, chunk = 512, 512

    def body(p_ref, w_ref, *refs):
        o_ref = refs[-1]
        acc = refs[0][...].astype(F32) if extra is not None else jnp.zeros(o_ref.shape, F32)
        for c0 in range(0, n_piece * width, chunk):
            acc = acc + lax.dot_general(p_ref[c0 // width, :, c0 % width:c0 % width + chunk],
                                        w_ref[c0 // per_shard, :, c0 % per_shard:c0 % per_shard + chunk],
                                        NT, preferred_element_type=F32)
        o_ref[...] = acc.astype(o_ref.dtype)

    tile = pl.BlockSpec((tm, kdim), lambda i: (i, 0))
    return _carry(
        riders, body, name=name, grid=(s_len // tm,),
        in_specs=[pl.BlockSpec((n_piece, tm, width), lambda i: (0, i, 0)),
                  pl.BlockSpec(w3.shape, lambda i: (0, 0, 0))] + ([tile] if extra is not None else []),
        out_specs=tile, out_shape=jax.ShapeDtypeStruct((s_len, kdim), BF16),
        semantics=("parallel",),
    )(pieces, w3, *([extra] if extra is not None else []))


def _local_grads(x, target, norm1_g, q_norm_g, k_norm_g, rel_bias, conv_b, b_gate, norm2_g, comm):
    s_len = x.shape[0]
    tm = min(1024, s_len)
    row = lambda v: v.reshape(1, -1)

    def carrying(at, fn, *args, **kw):
        riders = comm.riders(at)
        out = fn(*args, riders=riders, **kw)
        if riders:
            out, carried = out
            comm.done(at, carried)
        return out

    bias = carrying("bias_expand", _bias_expand, rel_bias)
    h = carrying("norm1", _rms_fwd, "norm1", x, row(norm1_g))
    w_in3, w_gate3, conv_w = comm.early_weights()
    gq, gk = row(q_norm_g), row(k_norm_g)

    def qk_epi(acc, e, o):
        o[0][...] = acc.astype(BF16)
        gain = jnp.where(pl.program_id(1) < 2, e[0] * QK_SCALE, e[1])
        o[1][...] = (_head_unit(acc)[0] * gain).astype(BF16)
    small = pl.BlockSpec((1, 512), lambda i, j, k: (0, 0))
    per_head = lambda g: jnp.tile(g, (1, 512 // HEAD_DIM))
    qk_raw, qkn = _mm_fwd("proj_qk", h, w_in3, tm, 512, D_MODEL, ncols=2 * D_MODEL, epilogue=qk_epi,
                          outs=[_out2d(s_len, 2 * D_MODEL, BF16, tm, 512)] * 2,
                          extras=[(per_head(gq), small), (per_head(gk), small)])
    v = _mm_fwd("proj_v", h, w_in3, tm, 512, D_MODEL, col0=4, ncols=D_MODEL,
                outs=[_out2d(s_len, D_MODEL, BF16, tm, 512)])
    conv_in = _mm_fwd("proj_conv", h, w_in3, tm, 1536, D_MODEL, col0=2, ncols=3 * D_MODEL,
                      outs=[_out2d(s_len, 3 * D_MODEL, BF16, tm, 1536)])

    def gate_epi(acc, e, o):
        o[0][...] = jax.nn.sigmoid(acc + e[0]).astype(BF16)
    gates = _mm_fwd("gates", h, w_gate3, tm, 512, D_MODEL, epilogue=gate_epi,
                    outs=[_out2d(s_len, 2 * D_MODEL, BF16, tm, 512)],
                    extras=[(row(b_gate), pl.BlockSpec((1, 512), lambda i, j, k: (0, j)))])

    attn, lse = carrying("attn_fwd", _attn_fwd, qkn, v, bias)
    yconv = carrying("conv_fwd", _conv_fwd, conv_in, conv_w, row(conv_b))
    w_ap, w_cp, w_out, w_up3, w_down = comm.late_weights()
    tw = 1024
    ya = _mm_fwd("attn_proj", attn, w_ap, tm, tw, D_MODEL, outs=[_out2d(s_len, D_MODEL, BF16, tm, tw)])

    def merge_epi(acc, e, o):
        ya_v, ga, gc = [t.astype(F32) for t in e]
        o[0][...] = acc.astype(BF16)
        o[1][...] = (ga * ya_v + gc * acc).astype(BF16)
    gate_a, gate_c = (gates, _tile_spec(tm, tw, 0)), (gates, _tile_spec(tm, tw, 1))
    yc, merged = _mm_fwd("conv_proj", yconv, w_cp, tm, tw, D_MODEL, epilogue=merge_epi,
                         outs=[_out2d(s_len, D_MODEL, BF16, tm, tw), _out2d(s_len, D_MODEL, BF16, tm, tw)],
                         extras=[(ya, _tile_spec(tm, tw)), gate_a, gate_c])

    def res_epi(acc, e, o):
        o[0][...] = e[0] + acc
    x1 = _mm_fwd("out_proj", merged, w_out, tm, tw, D_MODEL, epilogue=res_epi,
                 extras=[(x, _tile_spec(tm, tw))])
    h2 = _rms_fwd("norm2", x1, row(norm2_g))

    def up_epi(acc, e, o):
        o[0][...] = jnp.square(jnp.maximum(acc, 0.0)).astype(BF16)
    act = _mm_fwd("mlp_up", h2, w_up3, tm, tw, D_MODEL, epilogue=up_epi, outs=[_out2d(s_len, D_FF, BF16, tm, tw)])

    def loss_epi(acc, e, o):
        err = e[0] + acc - e[1]
        o[0][...] = (err * (1.0 / D_MODEL)).astype(BF16)
        sq = err * err
        part = sq[:, 0:LANE]
        for c0 in range(LANE, D_MODEL, LANE):
            part = part + sq[:, c0:c0 + LANE]
        o[1][...] = jnp.sum(part.reshape(tl // 8, 8, LANE), axis=0)
    tl = 512
    dy_b, loss_part = _mm_fwd(
        "mlp_down", act, w_down, tl, D_MODEL, D_FF, epilogue=loss_epi,
        outs=[_out2d(s_len, D_MODEL, BF16, tl, D_MODEL),
              (jax.ShapeDtypeStruct((8 * (s_len // tl), LANE), F32), pl.BlockSpec((8, LANE), lambda i, j, k: (i, 0)))],
        extras=[(x1, _tile_spec(tl, D_MODEL)), (target, _tile_spec(tl, D_MODEL))])

    def dup_epi(acc, e, o):
        o[0][...] = (acc * (2.0 * jnp.sqrt(e[0].astype(F32)))).astype(BF16)
    full = lambda cols: pl.BlockSpec((tm, cols), lambda i, j, n: (i, n))
    tokens = lambda cols: pl.BlockSpec((s_len, cols), lambda i, j, m: (m, j))
    dup = _mm_bwd_x("d_act", dy_b, full(D_MODEL), w_down, tm, tw, D_MODEL, s_len, D_MODEL, epilogue=dup_epi,
                    outs=[_out2d(s_len, D_FF, BF16, tm, tw)], extras=[(act, _tile_spec(tm, tw))])
    g_down = _mm_bwd_w("g_down", act, dy_b, tokens(D_MODEL), D_MODEL, 512, D_MODEL, s_len, False)
    g_up = _mm_bwd_w("g_up", h2, dup, tokens(512), D_FF, D_MODEL, 512, s_len, True)
    comm.grads_ready("mlp", dict(w_down=g_down, w_up=g_up))
    dh2 = carrying("d_h2", _mm_bwd_x, "d_h2", dup, full(1024), w_up3, tm, tw, 1024, s_len, D_FF,
                   outs=[_out2d(s_len, D_MODEL, BF16, tm, tw)])
    dx1_b, dg2 = _rms_bwd("norm2_bwd", dh2, x1, row(norm2_g), dy_b)

    def dmerge_epi(acc, e, o):
        ya_v, yc_v, ga, gc = [t.astype(F32) for t in e]
        o[0][...] = (acc * ga).astype(BF16)
        o[1][...] = (acc * gc).astype(BF16)
        o[2][0] = (acc * ya_v * ga * (1.0 - ga)).astype(BF16)
        o[2][1] = (acc * yc_v * gc * (1.0 - gc)).astype(BF16)
    dya, dyc, dgp2 = _mm_bwd_x(
        "d_merged", dx1_b, full(D_MODEL), w_out, tm, tw, D_MODEL, s_len, D_MODEL, epilogue=dmerge_epi,
        outs=[_out2d(s_len, D_MODEL, BF16, tm, tw), _out2d(s_len, D_MODEL, BF16, tm, tw),
              (jax.ShapeDtypeStruct((2, s_len, D_MODEL), BF16), pl.BlockSpec((2, tm, tw), lambda i, j, n: (0, i, j)))],
        extras=[(ya, _tile_spec(tm, tw)), (yc, _tile_spec(tm, tw)), gate_a, gate_c])
    g_out = _mm_bwd_w("g_out", merged, dx1_b, tokens(512), D_MODEL, D_MODEL, 512, s_len, False)
    d_attn = _mm_bwd_x("d_attn", dya, full(D_MODEL), w_ap, tm, tw, D_MODEL, s_len, D_MODEL,
                       outs=[_out2d(s_len, D_MODEL, BF16, tm, tw)])
    g_ap = _mm_bwd_w("g_attn_proj", attn, dya, tokens(512), D_MODEL, D_MODEL, 512, s_len, False)
    d_yconv = _mm_bwd_x("d_yconv", dyc, full(D_MODEL), w_cp, tm, tw, D_MODEL, s_len, D_MODEL,
                        outs=[_out2d(s_len, D_MODEL, BF16, tm, tw)])
    g_cp = _mm_bwd_w("g_conv_proj", yconv, dyc, tokens(512), D_MODEL, D_MODEL, 512, s_len, False)
    comm.grads_ready("proj", dict(w_out=g_out, w_attn_proj=g_ap, w_conv_proj=g_cp))

    dproj6, dbias = carrying("attn_bwd", _attn_bwd, qkn, v, d_attn, bias, lse)
    dproj6, dgq, dgk = _qk_norm_bwd(dproj6, qk_raw, gq, gk)
    dproj6, dconv_wb = carrying("conv_bwd", _conv_bwd, dproj6, d_yconv, conv_in, conv_w, row(conv_b))
    d_rel = _bias_reduce(dbias)

    piece = lambda width: (lambda blk: (blk * width) // D_MODEL, lambda blk: (blk * width % D_MODEL) // width)
    pc, cb = piece(512)
    pieces = pl.BlockSpec((None, s_len, 512), lambda i, j, m: (pc(j), m, cb(j)))
    g_in = carrying("g_in", _mm_bwd_w, "g_in", h, dproj6, pieces, 6 * D_MODEL, D_MODEL, 512, s_len, True)
    g_gate = carrying("g_gate", _mm_bwd_w, "g_gate", h, dgp2, pieces, 2 * D_MODEL, D_MODEL, 512, s_len, True)
    comm.grads_ready("in", dict(w_in=g_in, w_gate=g_gate))
    dh_g = carrying("d_h_gate", _d_input, "d_h_gate", dgp2, w_gate3)
    dh = carrying("d_h_proj", _d_input, "d_h_proj", dproj6, w_in3, extra=dh_g)
    grad_x, dg1 = carrying("norm1_bwd", _rms_bwd, "norm1_bwd", dh, x, row(norm1_g), dx1_b, out_dtype=F32)

    def bsum(t, f):
        return [], [t[0].astype(F32), t[1].astype(F32)]
    ts = 512
    db_a, db_c = carrying("b_gate_sum", _ew, "b_gate_sum", bsum,
                          [(dgp2, pl.BlockSpec((None, ts, D_MODEL), lambda i: (0, i, 0))),
                           (dgp2, pl.BlockSpec((None, ts, D_MODEL), lambda i: (1, i, 0)))],
                          [], [], sums=[D_MODEL, D_MODEL], ts=ts)

    big = dict(w_in=g_in, w_attn_proj=g_ap, w_conv_proj=g_cp, w_gate=g_gate, w_out=g_out,
               w_up=g_up, w_down=g_down)
    small = dict(norm1_g=dg1, norm2_g=dg2, conv_wb=dconv_wb, b_gate=(db_a, db_c),
                 q_norm_g=dgq, k_norm_g=dgk, rel_bias=d_rel)
    return loss_part, grad_x, big, small


def _finish_loss(loss_part):
    def body(l_ref, lo_ref):
        total = jnp.sum(jnp.sum(l_ref[...], axis=0, keepdims=True), axis=1, keepdims=True)
        lo_ref[...] = jnp.broadcast_to(total * (0.5 / D_MODEL), lo_ref.shape)

    return pl.pallas_call(body, name="finish_loss", out_shape=jax.ShapeDtypeStruct((8, LANE), F32))(loss_part)


HBM_SPEC = pl.BlockSpec(memory_space=pl.ANY)


def _place():
    return lax.axis_index("x"), lax.axis_index("y"), lax.axis_index("c")


def _other_chips(x, y):
    return [(1 - x, y), (x, 1 - y), (1 - x, 1 - y)]


def _cast_into_slot(name, where, w):
    r, cols = w.shape
    ts = 256

    def body(w_ref, x_ref, o_ref):
        o_ref[...] = x_ref[...].astype(o_ref.dtype)

    return pl.pallas_call(
        body, name=name,
        grid_spec=pltpu.PrefetchScalarGridSpec(
            num_scalar_prefetch=1, grid=(r // ts,),
            in_specs=[pl.BlockSpec((ts, cols), lambda i, w: (i, 0))],
            out_specs=pl.BlockSpec((None, ts, cols), lambda i, w: (w[0], i, 0))),
        out_shape=jax.ShapeDtypeStruct((N_SHARD, r, cols), BF16),
        compiler_params=_params("parallel"),
    )(where, w)


def _remote(src, dst, send, recv, k, to):
    return pltpu.make_async_remote_copy(src_ref=src, dst_ref=dst, send_sem=send.at[k], recv_sem=recv.at[k],
                                        device_id=to, device_id_type=MESH)


def _gather_riders(slots):
    n = len(slots)

    def half(refs, w, shard, which):
        hr = slots[w].shape[1] // 2
        return refs[w].at[shard, pl.ds(which * hr, hr)]

    def each(fn):
        x, y, c = _place()
        for w in range(n):
            for j, chip in enumerate(_other_chips(x, y)):
                fn(w, 3 * w + j, 2 * x + y, 2 * chip[0] + chip[1], c, (*chip, c), (x, y, 1 - c))

    def chips_start(_, refs, send, recv):
        each(lambda w, k, me, them, c, peer, sib: _remote(half(refs, w, me, c), half(refs, w, me, c), send, recv, k, peer).start())

    def chips_finish(_, refs, send, recv):
        each(lambda w, k, me, them, c, peer, sib: _remote(half(refs, w, them, c), half(refs, w, them, c), send, recv, k, peer).wait_recv())
        each(lambda w, k, me, them, c, peer, sib: _remote(half(refs, w, me, c), half(refs, w, me, c), send, recv, k, peer).wait_send())

    def sibling_start(_, refs, send, recv):
        each(lambda w, k, me, them, c, peer, sib: _remote(half(refs, w, them, c), half(refs, w, them, c), send, recv, k, sib).start())

    def sibling_finish(_, refs, send, recv):
        each(lambda w, k, me, them, c, peer, sib: _remote(half(refs, w, them, 1 - c), half(refs, w, them, 1 - c), send, recv, k, sib).wait_recv())
        each(lambda w, k, me, them, c, peer, sib: _remote(half(refs, w, them, c), half(refs, w, them, c), send, recv, k, sib).wait_send())

    return (lambda s: _Rider([], s, 3 * n, chips_start, chips_finish),
            lambda s: _Rider([], s, 3 * n, sibling_start, sibling_finish))


def _pair_exchange_rider(grads, landing):
    n = len(grads)

    def copies(srcs, dsts, send, recv):
        x, y, c = _place()
        out = []
        for w in range(n):
            hr = grads[w].shape[1] // 2
            out.append(_remote(srcs[w].at[:, pl.ds((1 - c) * hr, hr)], dsts[w], send, recv, w, (x, y, 1 - c)))
        return out

    def start(srcs, dsts, send, recv):
        for cp in copies(srcs, dsts, send, recv):
            cp.start()

    def finish(srcs, dsts, send, recv):
        for cp in copies(srcs, dsts, send, recv):
            cp.wait()

    return _Rider(grads, landing, n, start, finish)


def _row_tile(hr):
    return min(hr, 256)


def _pair_add(name, where, grad, got):
    _, hr, cols = got.shape
    tr = _row_tile(hr)
    nblk = hr // tr

    def body(w_ref, g_ref, r_ref, o_ref):
        o_ref[...] = (g_ref[...] + r_ref[...]).astype(o_ref.dtype)

    return pl.pallas_call(
        body, name=name,
        grid_spec=pltpu.PrefetchScalarGridSpec(
            num_scalar_prefetch=1, grid=(N_SHARD, nblk),
            in_specs=[pl.BlockSpec((None, tr, cols), lambda s, i, w: (s, w[1] * nblk + i, 0)),
                      pl.BlockSpec((None, tr, cols), lambda s, i, w: (s, i, 0))],
            out_specs=pl.BlockSpec((None, tr, cols), lambda s, i, w: (s, i, 0))),
        out_shape=jax.ShapeDtypeStruct(got.shape, BF16),
        compiler_params=_params("parallel", "parallel"),
    )(where, grad, got)


def _chip_exchange_rider(partials, landing):
    n = len(partials)

    def copies(srcs, dsts, send, recv):
        x, y, c = _place()
        return [_remote(srcs[w].at[2 * chip[0] + chip[1]], dsts[w].at[j], send, recv, 3 * w + j, (*chip, c))
                for w in range(n) for j, chip in enumerate(_other_chips(x, y))]

    def start(srcs, dsts, send, recv):
        for cp in copies(srcs, dsts, send, recv):
            cp.start()

    def finish(srcs, dsts, send, recv):
        for cp in copies(srcs, dsts, send, recv):
            cp.wait()

    return _Rider(partials, landing, 3 * n, start, finish)


def _final_add(name, where, grad, got, arrived):
    _, hr, cols = got.shape
    tr = _row_tile(hr)
    nblk = hr // tr

    def body(w_ref, g_ref, r_ref, a_ref, o_ref):
        acc = g_ref[...] + r_ref[...]
        for j in range(3):
            acc = acc + a_ref[j].astype(F32)
        o_ref[...] = acc

    return pl.pallas_call(
        body, name=name,
        grid_spec=pltpu.PrefetchScalarGridSpec(
            num_scalar_prefetch=1, grid=(nblk,),
            in_specs=[pl.BlockSpec((None, tr, cols), lambda i, w: (w[0], w[1] * nblk + i, 0)),
                      pl.BlockSpec((None, tr, cols), lambda i, w: (w[0], i, 0)),
                      pl.BlockSpec((3, tr, cols), lambda i, w: (0, i, 0))],
            out_specs=pl.BlockSpec((tr, cols), lambda i, w: (w[1] * nblk + i, 0))),
        out_shape=jax.ShapeDtypeStruct((2 * hr, cols), F32),
        compiler_params=_params("parallel"),
    )(where, grad, got, arrived)


def _pair_share_rider(shards):
    n = len(shards)

    def half(refs, w, which):
        hr = shards[w].shape[0] // 2
        return refs[w].at[pl.ds(which * hr, hr)]

    def start(_, refs, send, recv):
        x, y, c = _place()
        for w in range(n):
            _remote(half(refs, w, c), half(refs, w, c), send, recv, w, (x, y, 1 - c)).start()

    def finish(_, refs, send, recv):
        x, y, c = _place()
        for w in range(n):
            _remote(half(refs, w, 1 - c), half(refs, w, 1 - c), send, recv, w, (x, y, 1 - c)).wait_recv()
        for w in range(n):
            _remote(half(refs, w, c), half(refs, w, c), send, recv, w, (x, y, 1 - c)).wait_send()

    return _Rider([], shards, n, start, finish)


class _Exchange:
    PLAN = {"bias_expand": [("early", "gather")], "norm1": [("early", "forward")],
            "attn_fwd": [("late", "gather")], "conv_fwd": [("late", "forward")],
            "d_h2": [("mlp", "pair")], "attn_bwd": [("mlp", "chips"), ("proj", "pair")], "conv_bwd": [("mlp", "share")],
            "g_in": [("proj", "chips")], "g_gate": [("proj", "share")],
            "d_h_gate": [("in", "pair")], "d_h_proj": [("in", "chips")], "b_gate_sum": [("in", "share")]}

    def __init__(self, where, early_slots, late_slots):
        self.where = where
        self.slots = dict(early=early_slots, late=late_slots)
        self.stage = {g: dict(zip(("gather", "forward"), _gather_riders(s))) for g, s in self.slots.items()}
        self.groups, self.reduced, self.pending = {}, {}, []

    def early_weights(self):
        w_in3, w_gate3, small = self.slots["early"]
        conv_w = small[:, :3, :].transpose(1, 0, 2).reshape(3, N_SHARD * small.shape[2])
        return w_in3, w_gate3, conv_w

    def late_weights(self):
        rows = lambda a: a.reshape(a.shape[0] * a.shape[1], a.shape[2])
        w_ap, w_cp, w_out, w_up3, w_down = self.slots["late"]
        return rows(w_ap), rows(w_cp), rows(w_out), w_up3, rows(w_down)

    def grads_ready(self, group, grads):
        names = list(grads)
        g4 = [g if g.ndim == 3 else g.reshape(N_SHARD, -1, g.shape[1]) for g in grads.values()]
        self.groups[group] = dict(names=names, g4=g4)

    def riders(self, at):
        self.pending = self.PLAN.get(at, [])
        out = []
        for group, stage in self.pending:
            if group in self.slots:
                out.append(self.stage[group][stage](self.slots[group]))
                continue
            st = self.groups[group]
            if stage == "pair":
                landing = [lax.empty((N_SHARD, g.shape[1] // 2, g.shape[2]), F32) for g in st["g4"]]
                out.append(_pair_exchange_rider(st["g4"], landing))
            elif stage == "chips":
                landing = [lax.empty((3,) + p.shape[1:], p.dtype) for p in st["partial"]]
                out.append(_chip_exchange_rider(st["partial"], landing))
            else:
                out.append(_pair_share_rider(st["halves"]))
        return out

    def done(self, at, carried):
        for (group, stage), arrays in zip(self.pending, carried):
            if group in self.slots:
                self.slots[group] = arrays
                continue
            st = self.groups[group]
            tag = lambda what, n: what + "_" + n
            if stage == "pair":
                st["got"] = arrays
                st["partial"] = [_pair_add(tag("pair_add", n), self.where, g, r)
                                 for n, g, r in zip(st["names"], st["g4"], arrays)]
            elif stage == "chips":
                st["halves"] = [_final_add(tag("final_add", n), self.where, g, r, a)
                                for n, g, r, a in zip(st["names"], st["g4"], st["got"], arrays)]
            else:
                self.reduced.update(zip(st["names"], arrays))


SMALL_ROWS = 32
N_DEV = 8


def _all_reduce_small(pack):
    def body(p_ref, o_ref, buf, send, recv):
        x, y, c = _place()
        buf[4 * x + 2 * y + c] = p_ref[...]
        copies, waits = [], []
        for k in range(1, N_DEV):
            px = 1 - x if k & 4 else x
            py = 1 - y if k & 2 else y
            pc = 1 - c if k & 1 else c
            copies.append(pltpu.make_async_remote_copy(
                src_ref=p_ref, dst_ref=buf.at[4 * x + 2 * y + c], send_sem=send.at[k - 1],
                recv_sem=recv.at[k - 1], device_id=(px, py, pc), device_id_type=MESH))
            waits.append(pltpu.make_async_remote_copy(
                src_ref=p_ref, dst_ref=buf.at[4 * px + 2 * py + pc], send_sem=send.at[k - 1],
                recv_sem=recv.at[k - 1], device_id=(px, py, pc), device_id_type=MESH))
        for cp in copies:
            cp.start()
        for cp in waits:
            cp.wait_recv()
        acc = buf[0]
        for d in range(1, N_DEV):
            acc = acc + buf[d]
        o_ref[...] = acc
        for cp in copies:
            cp.wait_send()

    return pl.pallas_call(
        body, name="all_reduce_small",
        out_shape=jax.ShapeDtypeStruct(pack.shape, F32),
        scratch_shapes=[pltpu.VMEM((N_DEV,) + pack.shape, F32),
                        pltpu.SemaphoreType.DMA((N_DEV - 1,)), pltpu.SemaphoreType.DMA((N_DEV - 1,))],
    )(pack)


def _adamw(name, w, g, m, v):
    c1 = 1.0 - ADAM_B1 ** ADAM_STEP
    c2 = 1.0 - ADAM_B2 ** ADAM_STEP

    def fn(t, f):
        wv, gv, mv, vv = t
        m2 = ADAM_B1 * mv + (1.0 - ADAM_B1) * gv
        v2 = ADAM_B2 * vv + (1.0 - ADAM_B2) * (gv * gv)
        delta = -ADAM_LR * ((m2 / c1) / (jnp.sqrt(v2 / c2) + ADAM_EPS) + ADAM_WD * wv)
        return [delta, m2, v2], []

    cols = w.shape[1]
    return _ew(name, fn, [w, g, m, v], [], [(cols, F32)] * 3, ts=min(w.shape[0], 256))


LOSS_ROW = 26


def _pack_small(norm1_g, norm2_g, conv_b, b_gate, conv_w, q_norm_g, k_norm_g, rel_bias, loss=None):
    pack = jnp.zeros((SMALL_ROWS, D_MODEL), F32)
    for r0, v in ((0, norm1_g), (1, norm2_g), (2, conv_b), (3, b_gate.reshape(2, D_MODEL)), (5, conv_w),
                  (8, q_norm_g), (9, k_norm_g), (10, rel_bias)) + (((LOSS_ROW, loss),) if loss is not None else ()):
        v = v.reshape(-1, v.shape[-1]).astype(F32)
        pack = pack.at[r0:r0 + v.shape[0], :v.shape[1]].set(v)
    return pack


def _unpack_small(pack, conv_cols):
    return dict(norm1_g=pack[0], norm2_g=pack[1], conv_b=pack[2], b_gate=pack[3:5].reshape(2 * D_MODEL),
                conv_w=pack[5:8, :conv_cols], q_norm_g=pack[8, :HEAD_DIM], k_norm_g=pack[9, :HEAD_DIM],
                rel_bias=pack[10:10 + N_HEADS, :N_REL])


BIG = ["w_in", "w_attn_proj", "w_conv_proj", "w_gate", "w_out", "w_up", "w_down"]
LATE = ["w_attn_proj", "w_conv_proj", "w_out", "w_up", "w_down"]
WEIGHTS = ["norm1_g", "w_in", "q_norm_g", "k_norm_g", "rel_bias", "conv_w", "conv_b", "w_attn_proj",
           "w_conv_proj", "w_gate", "b_gate", "w_out", "norm2_g", "w_up", "w_down"]


def kernel(x, norm1_g, w_in, q_norm_g, k_norm_g, rel_bias, conv_w, conv_b, w_attn_proj, w_conv_proj, w_gate, b_gate, w_out, norm2_g, w_up, w_down, loss_target, m_norm1_g, m_w_in, m_q_norm_g, m_k_norm_g, m_rel_bias, m_conv_w, m_conv_b, m_w_attn_proj, m_w_conv_proj, m_w_gate, m_b_gate, m_w_out, m_norm2_g, m_w_up, m_w_down, v_norm1_g, v_w_in, v_q_norm_g, v_k_norm_g, v_rel_bias, v_conv_w, v_conv_b, v_w_attn_proj, v_w_conv_proj, v_w_gate, v_b_gate, v_w_out, v_norm2_g, v_w_up, v_w_down):
    given = dict(locals())
    w = {n: given[n] for n in WEIGHTS}
    m = {n: given["m_" + n] for n in WEIGHTS}
    v = {n: given["v_" + n] for n in WEIGHTS}
    s_len = x.shape[1]
    shard = 2 * lax.axis_index("x") + lax.axis_index("y")
    where = jnp.stack([shard, lax.axis_index("c")]).astype(jnp.int32)
    conv_cols = conv_w.shape[1]

    small_in = lax.dynamic_update_slice(jnp.zeros((N_SHARD, 16, conv_cols), F32), conv_w[None], (shard, 0, 0))
    slot = {n: _cast_into_slot("cast_" + n, where, w[n]) for n in BIG}
    comm = _Exchange(where, [slot["w_in"], slot["w_gate"], small_in], [slot[n] for n in LATE])

    loss_part, grad_x, _, small = _local_grads(
        x.reshape(s_len, D_MODEL), loss_target.reshape(s_len, D_MODEL), norm1_g, q_norm_g, k_norm_g,
        rel_bias, conv_b, b_gate, norm2_g, comm)
    grad = dict(comm.reduced)

    loss_local = _finish_loss(loss_part)
    pack = _pack_small(small["norm1_g"], small["norm2_g"], small["conv_wb"][3], jnp.concatenate(small["b_gate"], axis=1),
                       small["conv_wb"][0:3], small["q_norm_g"], small["k_norm_g"], small["rel_bias"],
                       loss=loss_local[0:1, :])
    total = _all_reduce_small(pack)
    g_small = _unpack_small(total, D_MODEL)
    g_small["conv_w"] = lax.dynamic_slice(g_small["conv_w"], (0, shard * conv_cols), (3, conv_cols))
    grad.update(g_small)

    delta, new_m, new_v = {}, {}, {}
    for n in BIG:
        delta[n], new_m[n], new_v[n] = _adamw("adamw_" + n, w[n], grad[n], m[n], v[n])
    small_names = [n for n in WEIGHTS if n not in BIG]
    packs = [_pack_small(**{n: src[n] for n in small_names}) for src in (w, grad, m, v)]
    for out, packed in zip((delta, new_m, new_v), _adamw("adamw_small", *packs)):
        out.update({n: a.reshape(w[n].shape) for n, a in _unpack_small(packed, conv_cols).items()})

    outs = [total[LOSS_ROW, 0], grad_x.reshape(x.shape)]
    for group in (grad, delta, new_m, new_v):
        outs += [group[n].reshape(w[n].shape) for n in WEIGHTS]
    return tuple(outs)
```

```python
import functools

import jax
import jax.numpy as jnp
from jax import lax
from jax.experimental import pallas as pl
from jax.experimental.pallas import tpu as pltpu

F32 = jnp.float32
BF16 = jnp.bfloat16

D_MODEL = 1024
N_HEADS = 16
HEAD_DIM = 64
CHUNK = 64
N_PREV_CHUNKS = 8
MAX_REL = 256
D_FF = 4096
N_REL = 2 * MAX_REL + 1
REL_PAD = 640
EPS = 1e-6
NEG_INF = -1e30
QK_SCALE = HEAD_DIM ** -0.5

SUPER = 4 * CHUNK
BAND = SUPER + N_PREV_CHUNKS * CHUNK
SKEW_W = 1024
N_SHARD = 4
LANE = 128
VMEM_LIMIT = 48 * 1024 * 1024

ADAM_LR = 0.001
ADAM_B1 = 0.9
ADAM_B2 = 0.999
ADAM_EPS = 1e-08
ADAM_WD = 0.01
ADAM_STEP = 10

MESH = pl.DeviceIdType.MESH
NN = (((1,), (0,)), ((), ()))
NT = (((1,), (1,)), ((), ()))
TN = (((0,), (0,)), ((), ()))


def _params(*sem):
    return pltpu.CompilerParams(dimension_semantics=sem or None, vmem_limit_bytes=VMEM_LIMIT)


HBM_SPEC = pl.BlockSpec(memory_space=pl.ANY)


class _Rider:
    def __init__(self, sources, arrays, n_sem, start, finish):
        self.sources, self.arrays, self.n_sem, self.start, self.finish = sources, arrays, n_sem, start, finish


def _carry(riders, body, *, name, out_shape, grid=(), in_specs=None, out_specs=None, scratch_shapes=(),
           semantics=(), input_output_aliases=None):
    aliases = dict(input_output_aliases or {})
    if not riders:
        kw = {} if in_specs is None else dict(in_specs=in_specs, out_specs=out_specs)
        return pl.pallas_call(body, name=name, grid=grid, out_shape=out_shape, scratch_shapes=scratch_shapes,
                              input_output_aliases=aliases, compiler_params=_params(*semantics), **kw)
    single = not isinstance(out_shape, (list, tuple))
    shapes = [out_shape] if single else list(out_shape)
    n_out, n_scr = len(shapes), len(scratch_shapes)
    srcs = [a for r in riders for a in r.sources]
    arrs = [a for r in riders for a in r.arrays]
    vmem = pl.BlockSpec(memory_space=pltpu.VMEM)

    def run(*args):
        n_in = len(args)

        def wrapped(*refs):
            pos = n_in
            src_refs = refs[pos:pos + len(srcs)]
            pos += len(srcs) + len(arrs)
            outs = refs[pos:pos + n_out]
            pos += n_out
            arr_refs = refs[pos:pos + len(arrs)]
            pos += len(arrs)
            scratch = refs[pos:pos + n_scr]
            sems = refs[pos + n_scr:]
            first, last = True, True
            for d, size in enumerate(grid):
                first = jnp.logical_and(first, pl.program_id(d) == 0)
                last = jnp.logical_and(last, pl.program_id(d) == size - 1)

            def each(method):
                s0 = a0 = 0
                for k, r in enumerate(riders):
                    getattr(r, method)(src_refs[s0:s0 + len(r.sources)], arr_refs[a0:a0 + len(r.arrays)],
                                       sems[2 * k], sems[2 * k + 1])
                    s0, a0 = s0 + len(r.sources), a0 + len(r.arrays)

            pl.when(first)(lambda: each("start"))
            body(*refs[:n_in], *outs, *scratch)
            pl.when(last)(lambda: each("finish"))

        ins = [vmem] * n_in if in_specs is None else list(in_specs)
        if out_specs is None:
            o_specs = [vmem] * n_out
        else:
            o_specs = [out_specs] if single else list(out_specs)
        for k in range(len(arrs)):
            aliases[n_in + len(srcs) + k] = n_out + k
        res = pl.pallas_call(
            wrapped, name=name, grid=grid,
            in_specs=ins + [HBM_SPEC] * (len(srcs) + len(arrs)),
            out_specs=o_specs + [HBM_SPEC] * len(arrs),
            out_shape=shapes + [jax.ShapeDtypeStruct(a.shape, a.dtype) for a in arrs],
            scratch_shapes=list(scratch_shapes) + [pltpu.SemaphoreType.DMA((r.n_sem,)) for r in riders for _ in range(2)],
            input_output_aliases=aliases,
            compiler_params=_params(*["arbitrary"] * len(grid)),
        )(*args, *srcs, *arrs)
        core, rest = res[:n_out], list(res[n_out:])
        carried, a0 = [], 0
        for r in riders:
            carried.append(rest[a0:a0 + len(r.arrays)])
            a0 += len(r.arrays)
        return (core[0] if single else core), carried

    return run


def _mm(name, dims, a, a_spec, b, b_spec, grid, tile, outs, epilogue=None, extras=(), riders=()):
    nk, ne, no = grid[2], len(extras), len(outs)

    def body(a_ref, b_ref, *refs):
        e_refs, o_refs = refs[:ne], refs[ne:ne + no]
        part = lax.dot_general(a_ref[...], b_ref[...], dims, preferred_element_type=F32)

        def finish(acc):
            if epilogue is None:
                o_refs[0][...] = acc.astype(o_refs[0].dtype)
            else:
                epilogue(acc, [r[...] for r in e_refs], o_refs)

        if nk == 1:
            finish(part)
        else:
            acc_ref = refs[ne + no]
            k = pl.program_id(2)

            @pl.when(k == 0)
            def _():
                acc_ref[...] = part

            @pl.when(k > 0)
            def _():
                acc_ref[...] += part

            @pl.when(k == nk - 1)
            def _():
                finish(acc_ref[...])

    res = _carry(
        riders, body, name=name, grid=grid,
        in_specs=[a_spec, b_spec] + [s for _, s in extras],
        out_specs=[s for _, s in outs],
        out_shape=[s for s, _ in outs],
        scratch_shapes=[pltpu.VMEM(tile, F32)] if nk > 1 else [],
        semantics=("parallel", "parallel", "arbitrary"),
    )(a, b, *[e for e, _ in extras])
    res, carried = res if riders else (res, None)
    res = res[0] if no == 1 else res
    return (res, carried) if riders else res


def _tile_spec(tm, tn, col0=0):
    return pl.BlockSpec((tm, tn), lambda i, j, k: (i, j + col0))


def _out2d(m, n, dtype, tm, tn):
    return (jax.ShapeDtypeStruct((m, n), dtype), _tile_spec(tm, tn))


def _mm_fwd(name, a, w, tm, tn, tk, outs=None, epilogue=None, extras=(), col0=0, ncols=None, riders=()):
    m, kdim = a.shape
    if w.ndim == 3:
        per = w.shape[2] // tn
        n = ncols or N_SHARD * w.shape[2]
        w_spec = pl.BlockSpec((None, tk, tn), lambda i, j, k: ((j + col0) // per, k, (j + col0) % per))
    else:
        n = ncols or w.shape[1]
        w_spec = pl.BlockSpec((tk, tn), lambda i, j, k: (k, j + col0))
    if outs is None:
        outs = [_out2d(m, n, F32, tm, tn)]
    return _mm(name, NN, a, pl.BlockSpec((tm, tk), lambda i, j, k: (i, k)), w, w_spec,
               (m // tm, n // tn, kdim // tk), (tm, tn), outs, epilogue, extras, riders)


def _mm_bwd_x(name, g, g_spec, w, tm, tj, tc, m, n_contract, outs=None, epilogue=None, extras=(), riders=()):
    if w.ndim == 3:
        per = w.shape[2] // tc
        kdim = w.shape[1]
        w_spec = pl.BlockSpec((None, tj, tc), lambda i, j, n: (n // per, j, n % per))
    else:
        kdim = w.shape[0]
        w_spec = pl.BlockSpec((tj, tc), lambda i, j, n: (j, n))
    if outs is None:
        outs = [_out2d(m, kdim, F32, tm, tj)]
    return _mm(name, NT, g, g_spec, w, w_spec, (m // tm, kdim // tj, n_contract // tc),
               (tm, tj), outs, epilogue, extras, riders)


def _mm_bwd_w(name, a, g, g_spec, n, tk, tn, tm, sharded, riders=()):
    m, kdim = a.shape
    if sharded:
        per = (n // N_SHARD) // tn
        out = (jax.ShapeDtypeStruct((N_SHARD, kdim, n // N_SHARD), F32),
               pl.BlockSpec((None, tk, tn), lambda i, j, mm: (j // per, i, j % per)))
    else:
        out = (jax.ShapeDtypeStruct((kdim, n), F32), pl.BlockSpec((tk, tn), lambda i, j, mm: (i, j)))
    return _mm(name, TN, a, pl.BlockSpec((tm, tk), lambda i, j, mm: (mm, i)), g, g_spec,
               (kdim // tk, n // tn, m // tm), (tk, tn), [out], riders=riders)


def _ew(name, fn, tiles, fulls, outs, sums=(), ts=512, riders=()):
    tiles = [t if isinstance(t, tuple) else (t, pl.BlockSpec((ts, t.shape[1]), lambda i: (i, 0)))
             for t in tiles]
    s_rows = tiles[0][0].shape[-2]
    nt, nf, no = len(tiles), len(fulls), len(outs)

    def body(*refs):
        t_vals = [r[...] for r in refs[:nt]]
        f_vals = [r[...] for r in refs[nt:nt + nf]]
        o_refs, s_refs = refs[nt + nf:nt + nf + no], refs[nt + nf + no:]
        o_vals, s_vals = fn(t_vals, f_vals)
        for r, v in zip(o_refs, o_vals):
            r[...] = v.astype(r.dtype)
        for r, v in zip(s_refs, s_vals):
            part = jnp.sum(v, axis=0, keepdims=True)

            @pl.when(pl.program_id(0) == 0)
            def _():
                r[...] = part

            @pl.when(pl.program_id(0) > 0)
            def _():
                r[...] += part

    full_specs = [pl.BlockSpec(f.shape, lambda i, nd=f.ndim: (0,) * nd) for f in fulls]
    return _carry(
        riders, body, name=name, grid=(s_rows // ts,),
        in_specs=[s for _, s in tiles] + full_specs,
        out_specs=[pl.BlockSpec((ts, c), lambda i: (i, 0)) for c, _ in outs]
        + [pl.BlockSpec((1, c), lambda i: (0, 0)) for c in sums],
        out_shape=[jax.ShapeDtypeStruct((s_rows, c), dt) for c, dt in outs]
        + [jax.ShapeDtypeStruct((1, c), F32) for c in sums],
        semantics=("arbitrary",),
    )(*[t for t, _ in tiles], *fulls)


def _rms_fwd(name, x, g, riders=()):
    def fn(t, f):
        xv = t[0]
        r = lax.rsqrt(jnp.mean(xv * xv, axis=-1, keepdims=True) + EPS)
        return [xv * r * f[0]], []
    out = _ew(name, fn, [x], [g], [(x.shape[1], BF16)], riders=riders)
    return (out[0][0], out[1]) if riders else out[0]


def _rms_bwd(name, dh, x, g, dres, out_dtype=BF16, riders=()):
    def fn(t, f):
        dhv, xv, dr = [v.astype(F32) for v in t]
        r = lax.rsqrt(jnp.mean(xv * xv, axis=-1, keepdims=True) + EPS)
        xhat = xv * r
        u = dhv * f[0]
        dx = r * (u - xhat * jnp.mean(u * xhat, axis=-1, keepdims=True)) + dr
        return [dx], [dhv * xhat]
    c = x.shape[1]
    return _ew(name, fn, [dh, x, dres], [g], [(c, out_dtype)], sums=[c], riders=riders)


def _split3(x):
    x1 = x.astype(BF16)
    r1 = x - x1.astype(F32)
    x2 = r1.astype(BF16)
    x3 = (r1 - x2.astype(F32)).astype(BF16)
    return x1, x2, x3


def _rel_class(cp):
    far = (cp < MAX_REL) | (cp > BAND)
    return jnp.where(far, 2 * MAX_REL, BAND - cp)


def _skew_rows(x, sign):
    row = lax.broadcasted_iota(jnp.int32, x.shape, 0)
    for b in range(CHUNK.bit_length() - 1):
        shift = (1 << b) if sign > 0 else SKEW_W - (1 << b)
        x = jnp.where((row >> b) & 1 == 1, pltpu.roll(x, shift, 1), x)
    return x


def _roll_lanes(x, shift):
    return x if shift % SKEW_W == 0 else pltpu.roll(x, shift % SKEW_W, 1)


def _bias_expand(rel_bias, riders=()):
    rel = jnp.pad(rel_bias, ((0, 0), (0, REL_PAD - N_REL))).reshape(N_HEADS, 1, REL_PAD)

    def body(rel_ref, o_ref):
        cls = lax.broadcasted_iota(jnp.int32, (REL_PAD, SKEW_W), 0)
        cp = lax.broadcasted_iota(jnp.int32, (REL_PAD, SKEW_W), 1)
        onehot = (cls == _rel_class(cp)).astype(BF16)
        rel8 = jnp.broadcast_to(rel_ref[...], (8, REL_PAD))
        trow = sum(jnp.dot(p, onehot, preferred_element_type=F32) for p in _split3(rel8))[0:1]
        first = _skew_rows(jnp.broadcast_to(trow, (CHUNK, SKEW_W)), +1)
        full = jnp.concatenate([_roll_lanes(first, CHUNK * g) for g in range(SUPER // CHUNK)], axis=0)[:, :BAND]
        qc = lax.broadcasted_iota(jnp.int32, (SUPER, BAND), 0) // CHUNK
        kc = lax.broadcasted_iota(jnp.int32, (SUPER, BAND), 1) // CHUNK
        on_band = (kc >= qc) & (kc <= qc + N_PREV_CHUNKS)
        o_ref[...] = jnp.where(on_band, full, NEG_INF).T

    return _carry(
        riders, body, name="bias_expand", grid=(N_HEADS,),
        in_specs=[pl.BlockSpec((None, 1, REL_PAD), lambda h: (h, 0, 0))],
        out_specs=pl.BlockSpec((None, BAND, SUPER), lambda h: (h, 0, 0)),
        out_shape=jax.ShapeDtypeStruct((N_HEADS, BAND, SUPER), F32),
        semantics=("arbitrary",),
    )(rel)


def _bias_reduce(dbias, riders=()):
    def body(d_ref, o_ref):
        x = jnp.concatenate([d_ref[...].T, jnp.zeros((SUPER, SKEW_W - BAND), F32)], axis=1)
        folded = sum(_roll_lanes(x[CHUNK * g:CHUNK * (g + 1)], -CHUNK * g) for g in range(SUPER // CHUNK))
        diag = jnp.sum(_skew_rows(folded, -1), axis=0, keepdims=True)
        cp = lax.broadcasted_iota(jnp.int32, (SKEW_W, REL_PAD), 0)
        cls = lax.broadcasted_iota(jnp.int32, (SKEW_W, REL_PAD), 1)
        onehot = (cls == _rel_class(cp)).astype(BF16)
        diag8 = jnp.broadcast_to(diag, (8, SKEW_W))
        o_ref[...] = sum(jnp.dot(p, onehot, preferred_element_type=F32) for p in _split3(diag8))[0:1]

    out = _carry(
        riders, body, name="bias_reduce", grid=(N_HEADS,),
        in_specs=[pl.BlockSpec((None, BAND, SUPER), lambda h: (h, 0, 0))],
        out_specs=pl.BlockSpec((None, 1, REL_PAD), lambda h: (h, 0, 0)),
        out_shape=jax.ShapeDtypeStruct((N_HEADS, 1, REL_PAD), F32),
        semantics=("arbitrary",),
    )(dbias)
    out, carried = out if riders else (out, None)
    out = out.reshape(N_HEADS, REL_PAD)[:, :N_REL]
    return (out, carried) if riders else out


HEADS_PER_STEP = 8
HEAD_COLS = HEADS_PER_STEP * HEAD_DIM
N_HEAD_GROUPS = N_HEADS // HEADS_PER_STEP


def _unit(x):
    r = lax.rsqrt(jnp.mean(x * x, axis=-1, keepdims=True) + EPS)
    return x * r, r


def _scores_t(qs, kn, bias_t, dead):
    s = jnp.concatenate([lax.dot_general(k, qs, NT, preferred_element_type=F32) for k in kn], axis=0) + bias_t
    return jnp.where(dead, NEG_INF, s)


def _band_specs(nb, col0, clamp_hi):
    def spec(d):
        def index(hg, i):
            blk = jnp.maximum(i - d, 0)
            if clamp_hi:
                blk = jnp.minimum(blk, nb - 1)
            return (blk, col0 + hg)
        return pl.BlockSpec((SUPER, HEAD_COLS), index)
    return [spec(2), spec(1), spec(0)]


def _head_sums(y):
    cols = y.shape[1]
    same_head = (lax.broadcasted_iota(jnp.int32, (cols, cols), 0) // HEAD_DIM
                 == lax.broadcasted_iota(jnp.int32, (cols, cols), 1) // HEAD_DIM).astype(BF16)
    hi = y.astype(BF16)
    lo = (y - hi.astype(F32)).astype(BF16)
    return (jnp.dot(hi, same_head, preferred_element_type=F32)
            + jnp.dot(lo, same_head, preferred_element_type=F32))


def _head_unit(x):
    r = lax.rsqrt(_head_sums(x * x) * (1.0 / HEAD_DIM) + EPS)
    return x * r, r


def _head(hh):
    return slice(HEAD_DIM * hh, HEAD_DIM * (hh + 1))


def _key_block(j):
    return slice(SUPER * j, SUPER * (j + 1))


def _dead_keys(i):
    key = lax.broadcasted_iota(jnp.int32, (BAND, SUPER), 0)
    return key < (2 - i) * SUPER


LSE_ROWS = 8


def _attn_fwd(qkn, v, bias, riders=()):
    s_len = qkn.shape[0]
    nb = s_len // SUPER

    def body(q_ref, k0, k1, k2, v0, v1, v2, b_ref, o_ref, lse_ref):
        dead = _dead_keys(pl.program_id(1))
        outs = []

        def probabilities(hh):
            sl = _head(hh)
            s = _scores_t(q_ref[:, sl], [k0[:, sl], k1[:, sl], k2[:, sl]], b_ref[hh], dead)
            m = jnp.max(s, axis=0, keepdims=True)
            e = jnp.exp(s - m)
            l = jnp.sum(e, axis=0, keepdims=True)
            lse_ref[hh:hh + 1, :] = m + jnp.log(l)
            return (e * (1.0 / l)).astype(BF16), sl

        def weighted_values(p, sl):
            outs.append(sum(lax.dot_general(p[_key_block(j), :], vj[:, sl], TN, preferred_element_type=F32)
                            for j, vj in enumerate((v0, v1, v2))))

        ready = probabilities(0)
        for hh in range(1, HEADS_PER_STEP):
            following = probabilities(hh)
            weighted_values(*ready)
            ready = following
        weighted_values(*ready)
        o_ref[...] = jnp.concatenate(outs, axis=1).astype(o_ref.dtype)

    return _carry(
        riders, body, name="attn_fwd", grid=(N_HEAD_GROUPS, nb),
        in_specs=[pl.BlockSpec((SUPER, HEAD_COLS), lambda hg, i: (i, hg))]
        + _band_specs(nb, N_HEAD_GROUPS, False) + _band_specs(nb, 0, False)
        + [pl.BlockSpec((HEADS_PER_STEP, BAND, SUPER), lambda hg, i: (hg, 0, 0))],
        out_specs=[pl.BlockSpec((SUPER, HEAD_COLS), lambda hg, i: (i, hg)),
                   pl.BlockSpec((None, LSE_ROWS, SUPER), lambda hg, i: (hg, 0, i))],
        out_shape=[jax.ShapeDtypeStruct((s_len, D_MODEL), BF16),
                   jax.ShapeDtypeStruct((N_HEAD_GROUPS, LSE_ROWS, s_len), F32)],
        semantics=("parallel", "arbitrary"),
    )(qkn, qkn, qkn, qkn, v, v, v, bias)


def _attn_bwd(qkn, v, d_out, bias, lse, riders=()):
    s_len = qkn.shape[0]
    nb = s_len // SUPER

    def body(q_ref, k0, k1, k2, v0, v1, v2, do_ref, b_ref, lse_ref, dp_ref, db_ref, aq_ref, ak_ref, av_ref):
        i = pl.program_id(1)

        @pl.when(i == 0)
        def _():
            aq_ref[...] = jnp.zeros_like(aq_ref)
            ak_ref[...] = jnp.zeros_like(ak_ref)
            av_ref[...] = jnp.zeros_like(av_ref)
            db_ref[...] = jnp.zeros_like(db_ref)

        @pl.when(i < nb)
        def _():
            dead = _dead_keys(i)
            dq, dk, dv = [], [[], [], []], [[], [], []]

            def softmax_grad(hh):
                sl = _head(hh)
                qs, do = q_ref[:, sl], do_ref[:, sl]
                kn = [k0[:, sl], k1[:, sl], k2[:, sl]]
                vv = [v0[:, sl], v1[:, sl], v2[:, sl]]
                p = jnp.exp(_scores_t(qs, kn, b_ref[hh], dead) - lse_ref[hh:hh + 1, :])
                dp = jnp.concatenate([lax.dot_general(vj, do, NT, preferred_element_type=F32) for vj in vv], axis=0)
                ds = p * (dp - jnp.sum(p * dp, axis=0, keepdims=True))
                db_ref[hh] += ds
                return p.astype(BF16), ds.astype(BF16), qs, do, kn

            def operand_grads(pb, dsb, qs, do, kn):
                dq.append(sum(lax.dot_general(dsb[_key_block(j), :], kn[j], TN, preferred_element_type=F32)
                              for j in range(3)))
                for j in range(3):
                    dv[j].append(jnp.dot(pb[_key_block(j), :], do, preferred_element_type=F32))
                    dk[j].append(jnp.dot(dsb[_key_block(j), :], qs, preferred_element_type=F32))

            ready = softmax_grad(0)
            for hh in range(1, HEADS_PER_STEP):
                following = softmax_grad(hh)
                operand_grads(*ready)
                ready = following
            operand_grads(*ready)
            aq_ref[i % 3] = jnp.concatenate(dq, axis=1)
            for j in range(3):
                slot = (i + 1 + j) % 3
                if j < 2:
                    ak_ref[slot] += jnp.concatenate(dk[j], axis=1)
                    av_ref[slot] += jnp.concatenate(dv[j], axis=1)
                else:
                    ak_ref[slot] = jnp.concatenate(dk[j], axis=1)
                    av_ref[slot] = jnp.concatenate(dv[j], axis=1)

        slot = (i + 1) % 3
        dp_ref[0] = aq_ref[slot].astype(dp_ref.dtype)
        dp_ref[1] = ak_ref[slot].astype(dp_ref.dtype)
        dp_ref[2] = av_ref[slot].astype(dp_ref.dtype)

    def qrow(hg, i):
        return (jnp.minimum(i, nb - 1), hg)

    return _carry(
        riders, body, name="attn_bwd", grid=(N_HEAD_GROUPS, nb + 2),
        in_specs=[pl.BlockSpec((SUPER, HEAD_COLS), qrow)]
        + _band_specs(nb, N_HEAD_GROUPS, True) + _band_specs(nb, 0, True)
        + [pl.BlockSpec((SUPER, HEAD_COLS), qrow),
           pl.BlockSpec((HEADS_PER_STEP, BAND, SUPER), lambda hg, i: (hg, 0, 0)),
           pl.BlockSpec((None, LSE_ROWS, SUPER), lambda hg, i: (hg, 0, jnp.minimum(i, nb - 1)))],
        out_specs=[pl.BlockSpec((3, SUPER, HEAD_COLS), lambda hg, i: (0, jnp.maximum(i - 2, 0), hg)),
                   pl.BlockSpec((HEADS_PER_STEP, BAND, SUPER), lambda hg, i: (hg, 0, 0))],
        out_shape=[jax.ShapeDtypeStruct((6, s_len, D_MODEL), BF16),
                   jax.ShapeDtypeStruct((N_HEADS, BAND, SUPER), F32)],
        scratch_shapes=[pltpu.VMEM((3, SUPER, HEAD_COLS), F32)] * 3,
        semantics=("parallel", "arbitrary"),
    )(qkn, qkn, qkn, qkn, v, v, v, d_out, bias, lse)


def _qk_norm_bwd(dproj6, qk_raw, gq, gk):
    s_len = qk_raw.shape[0]
    ts = min(1024, s_len)

    nsteps = s_len // ts
    half = D_MODEL // 2

    def body(d_ref, raw_ref, gq_ref, gk_ref, o_ref, dgq_ref, dgk_ref, acc_ref):
        step = pl.program_id(0)

        @pl.when(step == 0)
        def _():
            acc_ref[...] = jnp.zeros_like(acc_ref)

        for piece, (g_ref, scale) in enumerate(((gq_ref, QK_SCALE), (gk_ref, 1.0))):
            for c0 in (0, half):
                xhat, r = _head_unit(raw_ref[:, piece * D_MODEL + c0:piece * D_MODEL + c0 + half].astype(F32))
                dn = d_ref[piece, :, c0:c0 + half].astype(F32) * scale
                u = dn * g_ref[...]
                dx = r * (u - xhat * (_head_sums(u * xhat) * (1.0 / HEAD_DIM)))
                o_ref[piece, :, c0:c0 + half] = dx.astype(o_ref.dtype)
                acc_ref[piece:piece + 1, c0:c0 + half] += jnp.sum(dn * xhat, axis=0, keepdims=True)

        @pl.when(step == nsteps - 1)
        def _():
            lane = lax.broadcasted_iota(jnp.int32, (D_MODEL, LANE), 0) % HEAD_DIM
            fold = (lane == lax.broadcasted_iota(jnp.int32, (D_MODEL, LANE), 1)).astype(BF16)
            tot = sum(jnp.dot(p, fold, preferred_element_type=F32) for p in _split3(acc_ref[...]))
            dgq_ref[...] = tot[0:1, :HEAD_DIM]
            dgk_ref[...] = tot[1:2, :HEAD_DIM]

    gain = pl.BlockSpec((1, half), lambda i: (0, 0))
    small = pl.BlockSpec((1, HEAD_DIM), lambda i: (0, 0))
    per_head = lambda g: jnp.tile(g, (1, half // HEAD_DIM))
    return pl.pallas_call(
        body, name="qk_norm_bwd", grid=(nsteps,),
        in_specs=[pl.BlockSpec((2, ts, D_MODEL), lambda i: (0, i, 0)),
                  pl.BlockSpec((ts, 2 * D_MODEL), lambda i: (i, 0)), gain, gain],
        out_specs=[pl.BlockSpec((2, ts, D_MODEL), lambda i: (0, i, 0)), small, small],
        out_shape=[jax.ShapeDtypeStruct(dproj6.shape, dproj6.dtype),
                   jax.ShapeDtypeStruct((1, HEAD_DIM), F32), jax.ShapeDtypeStruct((1, HEAD_DIM), F32)],
        scratch_shapes=[pltpu.VMEM((8, D_MODEL), F32)],
        input_output_aliases={0: 0},
        compiler_params=_params("arbitrary"),
    )(dproj6, qk_raw, per_head(gq), per_head(gk))


CONV_ROWS = 512
HALO = 16


def _rows_with_halo(ref, r0, n, front, s_len):
    zeros = jnp.zeros((HALO, ref.shape[1]), F32)
    if front:
        return (jnp.concatenate([zeros, ref[0:n, :].astype(F32)], axis=0) if r0 == 0
                else ref[r0 - HALO:r0 + n, :].astype(F32))
    return (jnp.concatenate([ref[r0:r0 + n, :].astype(F32), zeros], axis=0) if r0 + n == s_len
            else ref[r0:r0 + n + HALO, :].astype(F32))


def _earlier(ext, k):
    return pltpu.roll(ext, k, 0)[HALO:]


def _later(ext, k):
    n = ext.shape[0]
    return pltpu.roll(ext, n - k, 0)[:n - HALO]


def _conv_cols(col0):
    return lambda s_len: pl.BlockSpec((s_len, LANE), lambda j: (0, col0 + j))


def _conv_fwd(proj, conv_w, conv_b, riders=()):
    s_len = proj.shape[0]

    def body(bg_ref, cg_ref, xc_ref, w_ref, b_ref, o_ref):
        w = [w_ref[t:t + 1, :] for t in range(3)]
        for r0 in range(0, s_len, CONV_ROWS):
            u = _rows_with_halo(cg_ref, r0, CONV_ROWS, True, s_len) * \
                _rows_with_halo(xc_ref, r0, CONV_ROWS, True, s_len)
            conv = b_ref[...] + w[0] * _earlier(u, 2) + w[1] * _earlier(u, 1) + w[2] * u[HALO:]
            o_ref[r0:r0 + CONV_ROWS, :] = (bg_ref[r0:r0 + CONV_ROWS, :].astype(F32) * conv).astype(o_ref.dtype)

    return _carry(
        riders, body, name="conv_fwd", grid=(D_MODEL // LANE,),
        in_specs=[_conv_cols(0)(s_len), _conv_cols(8)(s_len), _conv_cols(16)(s_len),
                  pl.BlockSpec((3, LANE), lambda j: (0, j)), pl.BlockSpec((1, LANE), lambda j: (0, j))],
        out_specs=pl.BlockSpec((s_len, LANE), lambda j: (0, j)),
        out_shape=jax.ShapeDtypeStruct((s_len, D_MODEL), BF16),
        semantics=("parallel",),
    )(proj, proj, proj, conv_w, conv_b)


def _conv_bwd(dproj6, dy, proj, conv_w, conv_b, riders=()):
    s_len = proj.shape[0]

    def body(dy_ref, bg_ref, cg_ref, xc_ref, w_ref, b_ref, _, dp_ref, dw_ref):
        w = [w_ref[t:t + 1, :] for t in range(3)]
        acc = [jnp.zeros((1, LANE), F32) for _ in range(4)]
        for r0 in range(0, s_len, CONV_ROWS):
            rows = slice(r0, r0 + CONV_ROWS)
            u = _rows_with_halo(cg_ref, r0, CONV_ROWS, True, s_len) * \
                _rows_with_halo(xc_ref, r0, CONV_ROWS, True, s_len)
            u2, u1, u0 = _earlier(u, 2), _earlier(u, 1), u[HALO:]
            conv = b_ref[...] + w[0] * u2 + w[1] * u1 + w[2] * u0
            dp_ref[0, rows, :] = (dy_ref[rows, :].astype(F32) * conv).astype(dp_ref.dtype)
            dconv_ext = _rows_with_halo(dy_ref, r0, CONV_ROWS, False, s_len) * \
                _rows_with_halo(bg_ref, r0, CONV_ROWS, False, s_len)
            dconv = dconv_ext[:CONV_ROWS]
            for t, term in enumerate([dconv * u2, dconv * u1, dconv * u0, dconv]):
                acc[t] = acc[t] + jnp.sum(term, axis=0, keepdims=True)
            du = w[2] * dconv + w[1] * _later(dconv_ext, 1) + w[0] * _later(dconv_ext, 2)
            dp_ref[1, rows, :] = (du * xc_ref[rows, :].astype(F32)).astype(dp_ref.dtype)
            dp_ref[2, rows, :] = (du * cg_ref[rows, :].astype(F32)).astype(dp_ref.dtype)
        dw_ref[...] = jnp.zeros_like(dw_ref)
        for t in range(4):
            dw_ref[t:t + 1, :] = acc[t]

    return _carry(
        riders, body, name="conv_bwd", grid=(D_MODEL // LANE,),
        in_specs=[pl.BlockSpec((s_len, LANE), lambda j: (0, j)),
                  _conv_cols(0)(s_len), _conv_cols(8)(s_len), _conv_cols(16)(s_len),
                  pl.BlockSpec((3, LANE), lambda j: (0, j)), pl.BlockSpec((1, LANE), lambda j: (0, j)),
                  pl.BlockSpec(memory_space=pl.ANY)],
        out_specs=[pl.BlockSpec((3, s_len, LANE), lambda j: (1, 0, j)),
                   pl.BlockSpec((8, LANE), lambda j: (0, j))],
        out_shape=[jax.ShapeDtypeStruct(dproj6.shape, dproj6.dtype),
                   jax.ShapeDtypeStruct((8, D_MODEL), F32)],
        input_output_aliases={6: 0},
        semantics=("parallel",),
    )(dy, proj, proj, proj, conv_w, conv_b, dproj6)


class _NoComm:
    def __init__(self, early, late):
        self.early, self.late = early, late

    def riders(self, at):
        return ()

    def done(self, at, carried):
        pass

    def early_weights(self):
        return self.early

    def late_weights(self):
        return self.late

    def grads_ready(self, group, grads):
        pass


def _d_input(name, terms, riders=()):
    s_len, kdim = terms[0][0].shape[1], terms[0][1].shape[1]
    tm, chunk = 512, 512
    n = len(terms)

    def body(*refs):
        o_ref = refs[-1]
        acc = jnp.zeros(o_ref.shape, F32)
        for p_ref, w_ref in zip(refs[:n], refs[n:2 * n]):
            n_piece, _, width = p_ref.shape
            per_shard = w_ref.shape[2]
            for c0 in range(0, n_piece * width, chunk):
                acc = acc + lax.dot_general(p_ref[c0 // width, :, c0 % width:c0 % width + chunk],
                                            w_ref[c0 // per_shard, :, c0 % per_shard:c0 % per_shard + chunk],
                                            NT, preferred_element_type=F32)
        o_ref[...] = acc.astype(o_ref.dtype)

    tile = pl.BlockSpec((tm, kdim), lambda i: (i, 0))
    return _carry(
        riders, body, name=name, grid=(s_len // tm,),
        in_specs=[pl.BlockSpec((p.shape[0], tm, p.shape[2]), lambda i: (0, i, 0)) for p, _ in terms]
        + [pl.BlockSpec(w.shape, lambda i: (0, 0, 0), pipeline_mode=pl.Buffered(1)) for _, w in terms],
        out_specs=tile, out_shape=jax.ShapeDtypeStruct((s_len, kdim), BF16),
        semantics=("parallel",),
    )(*[p for p, _ in terms], *[w for _, w in terms])


def _local_grads(x, target, norm1_g, q_norm_g, k_norm_g, rel_bias, conv_b, b_gate, norm2_g, comm):
    s_len = x.shape[0]
    tm = min(1024, s_len)
    row = lambda v: v.reshape(1, -1)

    def carrying(at, fn, *args, **kw):
        riders = comm.riders(at)
        out = fn(*args, riders=riders, **kw)
        if riders:
            out, carried = out
            comm.done(at, carried)
        return out

    bias = carrying("bias_expand", _bias_expand, rel_bias)
    h = carrying("norm1", _rms_fwd, "norm1", x, row(norm1_g))
    w_in3, w_gate3, conv_w = comm.early_weights()
    gq, gk = row(q_norm_g), row(k_norm_g)

    def qk_epi(acc, e, o):
        o[0][...] = acc.astype(BF16)
        gain = jnp.where(pl.program_id(1) < 2, e[0] * QK_SCALE, e[1])
        o[1][...] = (_head_unit(acc)[0] * gain).astype(BF16)
    small = pl.BlockSpec((1, 512), lambda i, j, k: (0, 0))
    per_head = lambda g: jnp.tile(g, (1, 512 // HEAD_DIM))
    qk_raw, qkn = _mm_fwd("proj_qk", h, w_in3, tm, 512, D_MODEL, ncols=2 * D_MODEL, epilogue=qk_epi,
                          outs=[_out2d(s_len, 2 * D_MODEL, BF16, tm, 512)] * 2,
                          extras=[(per_head(gq), small), (per_head(gk), small)])
    v = _mm_fwd("proj_v", h, w_in3, tm, 512, D_MODEL, col0=4, ncols=D_MODEL,
                outs=[_out2d(s_len, D_MODEL, BF16, tm, 512)])
    conv_in = _mm_fwd("proj_conv", h, w_in3, tm, 1536, D_MODEL, col0=2, ncols=3 * D_MODEL,
                      outs=[_out2d(s_len, 3 * D_MODEL, BF16, tm, 1536)])

    def gate_epi(acc, e, o):
        o[0][...] = jax.nn.sigmoid(acc + e[0]).astype(BF16)
    gates = _mm_fwd("gates", h, w_gate3, tm, 512, D_MODEL, epilogue=gate_epi,
                    outs=[_out2d(s_len, 2 * D_MODEL, BF16, tm, 512)],
                    extras=[(row(b_gate), pl.BlockSpec((1, 512), lambda i, j, k: (0, j)))])

    attn, lse = carrying("attn_fwd", _attn_fwd, qkn, v, bias)
    yconv = carrying("conv_fwd", _conv_fwd, conv_in, conv_w, row(conv_b))
    w_ap, w_cp, w_out, w_up3, w_down = comm.late_weights()
    tw = 1024
    ya = _mm_fwd("attn_proj", attn, w_ap, tm, tw, D_MODEL, outs=[_out2d(s_len, D_MODEL, BF16, tm, tw)])

    def merge_epi(acc, e, o):
        ya_v, ga, gc = [t.astype(F32) for t in e]
        o[0][...] = acc.astype(BF16)
        o[1][...] = (ga * ya_v + gc * acc).astype(BF16)
    gate_a, gate_c = (gates, _tile_spec(tm, tw, 0)), (gates, _tile_spec(tm, tw, 1))
    yc, merged = _mm_fwd("conv_proj", yconv, w_cp, tm, tw, D_MODEL, epilogue=merge_epi,
                         outs=[_out2d(s_len, D_MODEL, BF16, tm, tw), _out2d(s_len, D_MODEL, BF16, tm, tw)],
                         extras=[(ya, _tile_spec(tm, tw)), gate_a, gate_c])

    def res_epi(acc, e, o):
        o[0][...] = e[0] + acc
    x1 = _mm_fwd("out_proj", merged, w_out, tm, tw, D_MODEL, epilogue=res_epi,
                 extras=[(x, _tile_spec(tm, tw))])
    h2 = _rms_fwd("norm2", x1, row(norm2_g))

    def up_epi(acc, e, o):
        o[0][...] = jnp.square(jnp.maximum(acc, 0.0)).astype(BF16)
    act = _mm_fwd("mlp_up", h2, w_up3, tm, tw, D_MODEL, epilogue=up_epi, outs=[_out2d(s_len, D_FF, BF16, tm, tw)])

    def loss_epi(acc, e, o):
        err = e[0] + acc - e[1]
        o[0][...] = (err * (1.0 / D_MODEL)).astype(BF16)
        sq = err * err
        part = sq[:, 0:LANE]
        for c0 in range(LANE, D_MODEL, LANE):
            part = part + sq[:, c0:c0 + LANE]
        o[1][...] = jnp.sum(part.reshape(tl // 8, 8, LANE), axis=0)
    tl = 512
    dy_b, loss_part = _mm_fwd(
        "mlp_down", act, w_down, tl, D_MODEL, D_FF, epilogue=loss_epi,
        outs=[_out2d(s_len, D_MODEL, BF16, tl, D_MODEL),
              (jax.ShapeDtypeStruct((8 * (s_len // tl), LANE), F32), pl.BlockSpec((8, LANE), lambda i, j, k: (i, 0)))],
        extras=[(x1, _tile_spec(tl, D_MODEL)), (target, _tile_spec(tl, D_MODEL))])

    def dup_epi(acc, e, o):
        o[0][...] = (acc * (2.0 * jnp.sqrt(e[0].astype(F32)))).astype(BF16)
    full = lambda cols: pl.BlockSpec((tm, cols), lambda i, j, n: (i, n))
    tokens = lambda cols: pl.BlockSpec((s_len, cols), lambda i, j, m: (m, j))
    dup = _mm_bwd_x("d_act", dy_b, full(D_MODEL), w_down, tm, tw, D_MODEL, s_len, D_MODEL, epilogue=dup_epi,
                    outs=[_out2d(s_len, D_FF, BF16, tm, tw)], extras=[(act, _tile_spec(tm, tw))])
    g_down = _mm_bwd_w("g_down", act, dy_b, tokens(D_MODEL), D_MODEL, 512, D_MODEL, s_len, False)
    g_up = _mm_bwd_w("g_up", h2, dup, tokens(512), D_FF, D_MODEL, 512, s_len, True)
    comm.grads_ready("mlp", dict(w_down=g_down, w_up=g_up))
    dh2 = carrying("d_h2", _d_input, "d_h2", [(dup.reshape(1, s_len, D_FF), w_up3)])
    dx1_b, dg2 = _rms_bwd("norm2_bwd", dh2, x1, row(norm2_g), dy_b)

    def dmerge_epi(acc, e, o):
        ya_v, yc_v, ga, gc = [t.astype(F32) for t in e]
        o[0][...] = (acc * ga).astype(BF16)
        o[1][...] = (acc * gc).astype(BF16)
        o[2][0] = (acc * ya_v * ga * (1.0 - ga)).astype(BF16)
        o[2][1] = (acc * yc_v * gc * (1.0 - gc)).astype(BF16)
    dya, dyc, dgp2 = _mm_bwd_x(
        "d_merged", dx1_b, full(D_MODEL), w_out, tm, tw, D_MODEL, s_len, D_MODEL, epilogue=dmerge_epi,
        outs=[_out2d(s_len, D_MODEL, BF16, tm, tw), _out2d(s_len, D_MODEL, BF16, tm, tw),
              (jax.ShapeDtypeStruct((2, s_len, D_MODEL), BF16), pl.BlockSpec((2, tm, tw), lambda i, j, n: (0, i, j)))],
        extras=[(ya, _tile_spec(tm, tw)), (yc, _tile_spec(tm, tw)), gate_a, gate_c])
    g_out = _mm_bwd_w("g_out", merged, dx1_b, tokens(512), D_MODEL, D_MODEL, 512, s_len, False)
    d_attn = _mm_bwd_x("d_attn", dya, full(D_MODEL), w_ap, tm, tw, D_MODEL, s_len, D_MODEL,
                       outs=[_out2d(s_len, D_MODEL, BF16, tm, tw)])
    g_ap = _mm_bwd_w("g_attn_proj", attn, dya, tokens(512), D_MODEL, D_MODEL, 512, s_len, False)
    d_yconv = _mm_bwd_x("d_yconv", dyc, full(D_MODEL), w_cp, tm, tw, D_MODEL, s_len, D_MODEL,
                        outs=[_out2d(s_len, D_MODEL, BF16, tm, tw)])
    g_cp = _mm_bwd_w("g_conv_proj", yconv, dyc, tokens(512), D_MODEL, D_MODEL, 512, s_len, False)
    comm.grads_ready("proj", dict(w_out=g_out, w_attn_proj=g_ap, w_conv_proj=g_cp))

    dproj6, dbias = carrying("attn_bwd", _attn_bwd, qkn, v, d_attn, bias, lse)
    dproj6, dgq, dgk = _qk_norm_bwd(dproj6, qk_raw, gq, gk)
    dproj6, dconv_wb = carrying("conv_bwd", _conv_bwd, dproj6, d_yconv, conv_in, conv_w, row(conv_b))

    piece = lambda width: (lambda blk: (blk * width) // D_MODEL, lambda blk: (blk * width % D_MODEL) // width)
    pc, cb = piece(512)
    pieces = pl.BlockSpec((None, s_len, 512), lambda i, j, m: (pc(j), m, cb(j)))
    g_in = carrying("g_in", _mm_bwd_w, "g_in", h, dproj6, pieces, 6 * D_MODEL, D_MODEL, 512, s_len, True)
    g_gate = carrying("g_gate", _mm_bwd_w, "g_gate", h, dgp2, pieces, 2 * D_MODEL, D_MODEL, 512, s_len, True)
    comm.grads_ready("in", dict(w_in=g_in, w_gate=g_gate))
    d_rel = carrying("bias_reduce", _bias_reduce, dbias)
    dh = carrying("d_h", _d_input, "d_h", [(dproj6, w_in3), (dgp2, w_gate3)])
    grad_x, dg1 = carrying("norm1_bwd", _rms_bwd, "norm1_bwd", dh, x, row(norm1_g), dx1_b, out_dtype=F32)

    def bsum(t, f):
        return [], [t[0].astype(F32), t[1].astype(F32)]
    ts = 512
    db_a, db_c = carrying("b_gate_sum", _ew, "b_gate_sum", bsum,
                          [(dgp2, pl.BlockSpec((None, ts, D_MODEL), lambda i: (0, i, 0))),
                           (dgp2, pl.BlockSpec((None, ts, D_MODEL), lambda i: (1, i, 0)))],
                          [], [], sums=[D_MODEL, D_MODEL], ts=ts)

    big = dict(w_in=g_in, w_attn_proj=g_ap, w_conv_proj=g_cp, w_gate=g_gate, w_out=g_out,
               w_up=g_up, w_down=g_down)
    small = dict(norm1_g=dg1, norm2_g=dg2, conv_wb=dconv_wb, b_gate=(db_a, db_c),
                 q_norm_g=dgq, k_norm_g=dgk, rel_bias=d_rel)
    return loss_part, grad_x, big, small


def _finish_loss(loss_part):
    def body(l_ref, lo_ref):
        total = jnp.sum(jnp.sum(l_ref[...], axis=0, keepdims=True), axis=1, keepdims=True)
        lo_ref[...] = jnp.broadcast_to(total * (0.5 / D_MODEL), lo_ref.shape)

    return pl.pallas_call(body, name="finish_loss", out_shape=jax.ShapeDtypeStruct((8, LANE), F32))(loss_part)


HBM_SPEC = pl.BlockSpec(memory_space=pl.ANY)


def _place():
    return lax.axis_index("x"), lax.axis_index("y"), lax.axis_index("c")


def _other_chips(x, y):
    return [(1 - x, y), (x, 1 - y), (1 - x, 1 - y)]


def _cast_into_slot(name, where, w):
    r, cols = w.shape
    ts = 256

    def body(w_ref, x_ref, o_ref):
        o_ref[...] = x_ref[...].astype(o_ref.dtype)

    return pl.pallas_call(
        body, name=name,
        grid_spec=pltpu.PrefetchScalarGridSpec(
            num_scalar_prefetch=1, grid=(r // ts,),
            in_specs=[pl.BlockSpec((ts, cols), lambda i, w: (i, 0))],
            out_specs=pl.BlockSpec((None, ts, cols), lambda i, w: (w[0], i, 0))),
        out_shape=jax.ShapeDtypeStruct((N_SHARD, r, cols), BF16),
        compiler_params=_params("parallel"),
    )(where, w)


def _remote(src, dst, send, recv, k, to):
    return pltpu.make_async_remote_copy(src_ref=src, dst_ref=dst, send_sem=send.at[k], recv_sem=recv.at[k],
                                        device_id=to, device_id_type=MESH)


def _gather_riders(slots):
    n = len(slots)

    def half(refs, w, shard, which):
        hr = slots[w].shape[1] // 2
        return refs[w].at[shard, pl.ds(which * hr, hr)]

    def each(fn):
        x, y, c = _place()
        for w in range(n):
            for j, chip in enumerate(_other_chips(x, y)):
                fn(w, 3 * w + j, 2 * x + y, 2 * chip[0] + chip[1], c, (*chip, c), (x, y, 1 - c))

    def chips_start(_, refs, send, recv):
        each(lambda w, k, me, them, c, peer, sib: _remote(half(refs, w, me, c), half(refs, w, me, c), send, recv, k, peer).start())

    def chips_finish(_, refs, send, recv):
        each(lambda w, k, me, them, c, peer, sib: _remote(half(refs, w, them, c), half(refs, w, them, c), send, recv, k, peer).wait_recv())
        each(lambda w, k, me, them, c, peer, sib: _remote(half(refs, w, me, c), half(refs, w, me, c), send, recv, k, peer).wait_send())

    def sibling_start(_, refs, send, recv):
        each(lambda w, k, me, them, c, peer, sib: _remote(half(refs, w, them, c), half(refs, w, them, c), send, recv, k, sib).start())

    def sibling_finish(_, refs, send, recv):
        each(lambda w, k, me, them, c, peer, sib: _remote(half(refs, w, them, 1 - c), half(refs, w, them, 1 - c), send, recv, k, sib).wait_recv())
        each(lambda w, k, me, them, c, peer, sib: _remote(half(refs, w, them, c), half(refs, w, them, c), send, recv, k, sib).wait_send())

    return (lambda s: _Rider([], s, 3 * n, chips_start, chips_finish),
            lambda s: _Rider([], s, 3 * n, sibling_start, sibling_finish))


def _pair_exchange_rider(grads, landing):
    n = len(grads)

    def copies(srcs, dsts, send, recv):
        x, y, c = _place()
        out = []
        for w in range(n):
            hr = grads[w].shape[1] // 2
            out.append(_remote(srcs[w].at[:, pl.ds((1 - c) * hr, hr)], dsts[w], send, recv, w, (x, y, 1 - c)))
        return out

    def start(srcs, dsts, send, recv):
        for cp in copies(srcs, dsts, send, recv):
            cp.start()

    def finish(srcs, dsts, send, recv):
        for cp in copies(srcs, dsts, send, recv):
            cp.wait()

    return _Rider(grads, landing, n, start, finish)


def _row_tile(hr):
    return min(hr, 256)


def _pair_add(name, where, grad, got):
    _, hr, cols = got.shape
    tr = _row_tile(hr)
    nblk = hr // tr

    def body(w_ref, g_ref, r_ref, o_ref):
        o_ref[...] = (g_ref[...] + r_ref[...]).astype(o_ref.dtype)

    return pl.pallas_call(
        body, name=name,
        grid_spec=pltpu.PrefetchScalarGridSpec(
            num_scalar_prefetch=1, grid=(N_SHARD, nblk),
            in_specs=[pl.BlockSpec((None, tr, cols), lambda s, i, w: (s, w[1] * nblk + i, 0)),
                      pl.BlockSpec((None, tr, cols), lambda s, i, w: (s, i, 0))],
            out_specs=pl.BlockSpec((None, tr, cols), lambda s, i, w: (s, i, 0))),
        out_shape=jax.ShapeDtypeStruct(got.shape, BF16),
        compiler_params=_params("parallel", "parallel"),
    )(where, grad, got)


def _chip_exchange_rider(partials, landing):
    n = len(partials)

    def copies(srcs, dsts, send, recv):
        x, y, c = _place()
        return [_remote(srcs[w].at[2 * chip[0] + chip[1]], dsts[w].at[j], send, recv, 3 * w + j, (*chip, c))
                for w in range(n) for j, chip in enumerate(_other_chips(x, y))]

    def start(srcs, dsts, send, recv):
        for cp in copies(srcs, dsts, send, recv):
            cp.start()

    def finish(srcs, dsts, send, recv):
        for cp in copies(srcs, dsts, send, recv):
            cp.wait()

    return _Rider(partials, landing, 3 * n, start, finish)


def _final_add(name, where, grad, got, arrived):
    _, hr, cols = got.shape
    tr = _row_tile(hr)
    nblk = hr // tr

    def body(w_ref, g_ref, r_ref, a_ref, o_ref):
        acc = g_ref[...] + r_ref[...]
        for j in range(3):
            acc = acc + a_ref[j].astype(F32)
        o_ref[...] = acc

    return pl.pallas_call(
        body, name=name,
        grid_spec=pltpu.PrefetchScalarGridSpec(
            num_scalar_prefetch=1, grid=(nblk,),
            in_specs=[pl.BlockSpec((None, tr, cols), lambda i, w: (w[0], w[1] * nblk + i, 0)),
                      pl.BlockSpec((None, tr, cols), lambda i, w: (w[0], i, 0)),
                      pl.BlockSpec((3, tr, cols), lambda i, w: (0, i, 0))],
            out_specs=pl.BlockSpec((tr, cols), lambda i, w: (w[1] * nblk + i, 0))),
        out_shape=jax.ShapeDtypeStruct((2 * hr, cols), F32),
        compiler_params=_params("parallel"),
    )(where, grad, got, arrived)


def _pair_share_rider(shards):
    n = len(shards)

    def half(refs, w, which):
        hr = shards[w].shape[0] // 2
        return refs[w].at[pl.ds(which * hr, hr)]

    def start(_, refs, send, recv):
        x, y, c = _place()
        for w in range(n):
            _remote(half(refs, w, c), half(refs, w, c), send, recv, w, (x, y, 1 - c)).start()

    def finish(_, refs, send, recv):
        x, y, c = _place()
        for w in range(n):
            _remote(half(refs, w, 1 - c), half(refs, w, 1 - c), send, recv, w, (x, y, 1 - c)).wait_recv()
        for w in range(n):
            _remote(half(refs, w, c), half(refs, w, c), send, recv, w, (x, y, 1 - c)).wait_send()

    return _Rider([], shards, n, start, finish)


class _Exchange:
    PLAN = {"bias_expand": [("early", "gather")], "norm1": [("early", "forward")],
            "attn_fwd": [("late", "gather")], "conv_fwd": [("late", "forward")],
            "d_h2": [("mlp", "pair")], "attn_bwd": [("mlp", "chips"), ("proj", "pair")], "conv_bwd": [("mlp", "share")],
            "g_in": [("proj", "chips")], "g_gate": [("proj", "share")],
            "bias_reduce": [("in", "pair")], "d_h": [("in", "chips")], "b_gate_sum": [("in", "share")]}

    def __init__(self, where, early_slots, late_slots):
        self.where = where
        self.slots = dict(early=early_slots, late=late_slots)
        self.stage = {g: dict(zip(("gather", "forward"), _gather_riders(s))) for g, s in self.slots.items()}
        self.groups, self.reduced, self.pending = {}, {}, []

    def early_weights(self):
        w_in3, w_gate3, small = self.slots["early"]
        conv_w = small[:, :3, :].transpose(1, 0, 2).reshape(3, N_SHARD * small.shape[2])
        return w_in3, w_gate3, conv_w

    def late_weights(self):
        rows = lambda a: a.reshape(a.shape[0] * a.shape[1], a.shape[2])
        w_ap, w_cp, w_out, w_up3, w_down = self.slots["late"]
        return rows(w_ap), rows(w_cp), rows(w_out), w_up3, rows(w_down)

    def grads_ready(self, group, grads):
        names = list(grads)
        g4 = [g if g.ndim == 3 else g.reshape(N_SHARD, -1, g.shape[1]) for g in grads.values()]
        self.groups[group] = dict(names=names, g4=g4)

    def riders(self, at):
        self.pending = self.PLAN.get(at, [])
        out = []
        for group, stage in self.pending:
            if group in self.slots:
                out.append(self.stage[group][stage](self.slots[group]))
                continue
            st = self.groups[group]
            if stage == "pair":
                landing = [lax.empty((N_SHARD, g.shape[1] // 2, g.shape[2]), F32) for g in st["g4"]]
                out.append(_pair_exchange_rider(st["g4"], landing))
            elif stage == "chips":
                landing = [lax.empty((3,) + p.shape[1:], p.dtype) for p in st["partial"]]
                out.append(_chip_exchange_rider(st["partial"], landing))
            else:
                out.append(_pair_share_rider(st["halves"]))
        return out

    def done(self, at, carried):
        for (group, stage), arrays in zip(self.pending, carried):
            if group in self.slots:
                self.slots[group] = arrays
                continue
            st = self.groups[group]
            tag = lambda what, n: what + "_" + n
            if stage == "pair":
                st["got"] = arrays
                st["partial"] = [_pair_add(tag("pair_add", n), self.where, g, r)
                                 for n, g, r in zip(st["names"], st["g4"], arrays)]
            elif stage == "chips":
                st["halves"] = [_final_add(tag("final_add", n), self.where, g, r, a)
                                for n, g, r, a in zip(st["names"], st["g4"], st["got"], arrays)]
            else:
                self.reduced.update(zip(st["names"], arrays))


SMALL_ROWS = 32
N_DEV = 8


def _all_reduce_small(pack):
    def body(p_ref, o_ref, buf, send, recv):
        x, y, c = _place()
        buf[4 * x + 2 * y + c] = p_ref[...]
        copies, waits = [], []
        for k in range(1, N_DEV):
            px = 1 - x if k & 4 else x
            py = 1 - y if k & 2 else y
            pc = 1 - c if k & 1 else c
            copies.append(pltpu.make_async_remote_copy(
                src_ref=p_ref, dst_ref=buf.at[4 * x + 2 * y + c], send_sem=send.at[k - 1],
                recv_sem=recv.at[k - 1], device_id=(px, py, pc), device_id_type=MESH))
            waits.append(pltpu.make_async_remote_copy(
                src_ref=p_ref, dst_ref=buf.at[4 * px + 2 * py + pc], send_sem=send.at[k - 1],
                recv_sem=recv.at[k - 1], device_id=(px, py, pc), device_id_type=MESH))
        for cp in copies:
            cp.start()
        for cp in waits:
            cp.wait_recv()
        acc = buf[0]
        for d in range(1, N_DEV):
            acc = acc + buf[d]
        o_ref[...] = acc
        for cp in copies:
            cp.wait_send()

    return pl.pallas_call(
        body, name="all_reduce_small",
        out_shape=jax.ShapeDtypeStruct(pack.shape, F32),
        scratch_shapes=[pltpu.VMEM((N_DEV,) + pack.shape, F32),
                        pltpu.SemaphoreType.DMA((N_DEV - 1,)), pltpu.SemaphoreType.DMA((N_DEV - 1,))],
    )(pack)


def _adamw(name, w, g, m, v):
    c1 = 1.0 - ADAM_B1 ** ADAM_STEP
    c2 = 1.0 - ADAM_B2 ** ADAM_STEP

    def fn(t, f):
        wv, gv, mv, vv = t
        m2 = ADAM_B1 * mv + (1.0 - ADAM_B1) * gv
        v2 = ADAM_B2 * vv + (1.0 - ADAM_B2) * (gv * gv)
        delta = -ADAM_LR * ((m2 / c1) / (jnp.sqrt(v2 / c2) + ADAM_EPS) + ADAM_WD * wv)
        return [delta, m2, v2], []

    cols = w.shape[1]
    return _ew(name, fn, [w, g, m, v], [], [(cols, F32)] * 3, ts=min(w.shape[0], 256))


LOSS_ROW = 26


def _pack_small(norm1_g, norm2_g, conv_b, b_gate, conv_w, q_norm_g, k_norm_g, rel_bias, loss=None):
    pack = jnp.zeros((SMALL_ROWS, D_MODEL), F32)
    for r0, v in ((0, norm1_g), (1, norm2_g), (2, conv_b), (3, b_gate.reshape(2, D_MODEL)), (5, conv_w),
                  (8, q_norm_g), (9, k_norm_g), (10, rel_bias)) + (((LOSS_ROW, loss),) if loss is not None else ()):
        v = v.reshape(-1, v.shape[-1]).astype(F32)
        pack = pack.at[r0:r0 + v.shape[0], :v.shape[1]].set(v)
    return pack


def _unpack_small(pack, conv_cols):
    return dict(norm1_g=pack[0], norm2_g=pack[1], conv_b=pack[2], b_gate=pack[3:5].reshape(2 * D_MODEL),
                conv_w=pack[5:8, :conv_cols], q_norm_g=pack[8, :HEAD_DIM], k_norm_g=pack[9, :HEAD_DIM],
                rel_bias=pack[10:10 + N_HEADS, :N_REL])


BIG = ["w_in", "w_attn_proj", "w_conv_proj", "w_gate", "w_out", "w_up", "w_down"]
LATE = ["w_attn_proj", "w_conv_proj", "w_out", "w_up", "w_down"]
WEIGHTS = ["norm1_g", "w_in", "q_norm_g", "k_norm_g", "rel_bias", "conv_w", "conv_b", "w_attn_proj",
           "w_conv_proj", "w_gate", "b_gate", "w_out", "norm2_g", "w_up", "w_down"]


def kernel(x, norm1_g, w_in, q_norm_g, k_norm_g, rel_bias, conv_w, conv_b, w_attn_proj, w_conv_proj, w_gate, b_gate, w_out, norm2_g, w_up, w_down, loss_target, m_norm1_g, m_w_in, m_q_norm_g, m_k_norm_g, m_rel_bias, m_conv_w, m_conv_b, m_w_attn_proj, m_w_conv_proj, m_w_gate, m_b_gate, m_w_out, m_norm2_g, m_w_up, m_w_down, v_norm1_g, v_w_in, v_q_norm_g, v_k_norm_g, v_rel_bias, v_conv_w, v_conv_b, v_w_attn_proj, v_w_conv_proj, v_w_gate, v_b_gate, v_w_out, v_norm2_g, v_w_up, v_w_down):
    given = dict(locals())
    w = {n: given[n] for n in WEIGHTS}
    m = {n: given["m_" + n] for n in WEIGHTS}
    v = {n: given["v_" + n] for n in WEIGHTS}
    s_len = x.shape[1]
    shard = 2 * lax.axis_index("x") + lax.axis_index("y")
    where = jnp.stack([shard, lax.axis_index("c")]).astype(jnp.int32)
    conv_cols = conv_w.shape[1]

    small_in = lax.dynamic_update_slice(jnp.zeros((N_SHARD, 16, conv_cols), F32), conv_w[None], (shard, 0, 0))
    slot = {n: _cast_into_slot("cast_" + n, where, w[n]) for n in BIG}
    comm = _Exchange(where, [slot["w_in"], slot["w_gate"], small_in], [slot[n] for n in LATE])

    loss_part, grad_x, _, small = _local_grads(
        x.reshape(s_len, D_MODEL), loss_target.reshape(s_len, D_MODEL), norm1_g, q_norm_g, k_norm_g,
        rel_bias, conv_b, b_gate, norm2_g, comm)
    grad = dict(comm.reduced)

    loss_local = _finish_loss(loss_part)
    pack = _pack_small(small["norm1_g"], small["norm2_g"], small["conv_wb"][3], jnp.concatenate(small["b_gate"], axis=1),
                       small["conv_wb"][0:3], small["q_norm_g"], small["k_norm_g"], small["rel_bias"],
                       loss=loss_local[0:1, :])
    total = _all_reduce_small(pack)
    g_small = _unpack_small(total, D_MODEL)
    g_small["conv_w"] = lax.dynamic_slice(g_small["conv_w"], (0, shard * conv_cols), (3, conv_cols))
    grad.update(g_small)

    delta, new_m, new_v = {}, {}, {}
    for n in BIG:
        delta[n], new_m[n], new_v[n] = _adamw("adamw_" + n, w[n], grad[n], m[n], v[n])
    small_names = [n for n in WEIGHTS if n not in BIG]
    packs = [_pack_small(**{n: src[n] for n in small_names}) for src in (w, grad, m, v)]
    for out, packed in zip((delta, new_m, new_v), _adamw("adamw_small", *packs)):
        out.update({n: a.reshape(w[n].shape) for n, a in _unpack_small(packed, conv_cols).items()})

    outs = [total[LOSS_ROW, 0], grad_x.reshape(x.shape)]
    for group in (grad, delta, new_m, new_v):
        outs += [group[n].reshape(w[n].shape) for n in WEIGHTS]
    return tuple(outs)
```

```python
import functools

import jax
import jax.numpy as jnp
from jax import lax
from jax.experimental import pallas as pl
from jax.experimental.pallas import tpu as pltpu

F32 = jnp.float32
BF16 = jnp.bfloat16

D_MODEL = 1024
N_HEADS = 16
HEAD_DIM = 64
CHUNK = 64
N_PREV_CHUNKS = 8
MAX_REL = 256
D_FF = 4096
N_REL = 2 * MAX_REL + 1
REL_PAD = 640
EPS = 1e-6
NEG_INF = -1e30
QK_SCALE = HEAD_DIM ** -0.5

SUPER = 4 * CHUNK
BAND = SUPER + N_PREV_CHUNKS * CHUNK
SKEW_W = 1024
N_SHARD = 4
LANE = 128
VMEM_LIMIT = 48 * 1024 * 1024

ADAM_LR = 0.001
ADAM_B1 = 0.9
ADAM_B2 = 0.999
ADAM_EPS = 1e-08
ADAM_WD = 0.01
ADAM_STEP = 10

MESH = pl.DeviceIdType.MESH
NN = (((1,), (0,)), ((), ()))
NT = (((1,), (1,)), ((), ()))
TN = (((0,), (0,)), ((), ()))


def _params(*sem):
    return pltpu.CompilerParams(dimension_semantics=sem or None, vmem_limit_bytes=VMEM_LIMIT)


HBM_SPEC = pl.BlockSpec(memory_space=pl.ANY)


class _Rider:
    def __init__(self, sources, arrays, n_sem, start, finish):
        self.sources, self.arrays, self.n_sem, self.start, self.finish = sources, arrays, n_sem, start, finish


def _carry(riders, body, *, name, out_shape, grid=(), in_specs=None, out_specs=None, scratch_shapes=(),
           semantics=(), input_output_aliases=None):
    aliases = dict(input_output_aliases or {})
    if not riders:
        kw = {} if in_specs is None else dict(in_specs=in_specs, out_specs=out_specs)
        return pl.pallas_call(body, name=name, grid=grid, out_shape=out_shape, scratch_shapes=scratch_shapes,
                              input_output_aliases=aliases, compiler_params=_params(*semantics), **kw)
    single = not isinstance(out_shape, (list, tuple))
    shapes = [out_shape] if single else list(out_shape)
    n_out, n_scr = len(shapes), len(scratch_shapes)
    srcs = [a for r in riders for a in r.sources]
    arrs = [a for r in riders for a in r.arrays]
    vmem = pl.BlockSpec(memory_space=pltpu.VMEM)

    def run(*args):
        n_in = len(args)

        def wrapped(*refs):
            pos = n_in
            src_refs = refs[pos:pos + len(srcs)]
            pos += len(srcs) + len(arrs)
            outs = refs[pos:pos + n_out]
            pos += n_out
            arr_refs = refs[pos:pos + len(arrs)]
            pos += len(arrs)
            scratch = refs[pos:pos + n_scr]
            sems = refs[pos + n_scr:]
            first, last = True, True
            for d, size in enumerate(grid):
                first = jnp.logical_and(first, pl.program_id(d) == 0)
                last = jnp.logical_and(last, pl.program_id(d) == size - 1)

            def each(method):
                s0 = a0 = 0
                for k, r in enumerate(riders):
                    getattr(r, method)(src_refs[s0:s0 + len(r.sources)], arr_refs[a0:a0 + len(r.arrays)],
                                       sems[2 * k], sems[2 * k + 1])
                    s0, a0 = s0 + len(r.sources), a0 + len(r.arrays)

            pl.when(first)(lambda: each("start"))
            body(*refs[:n_in], *outs, *scratch)
            pl.when(last)(lambda: each("finish"))

        ins = [vmem] * n_in if in_specs is None else list(in_specs)
        if out_specs is None:
            o_specs = [vmem] * n_out
        else:
            o_specs = [out_specs] if single else list(out_specs)
        for k in range(len(arrs)):
            aliases[n_in + len(srcs) + k] = n_out + k
        res = pl.pallas_call(
            wrapped, name=name, grid=grid,
            in_specs=ins + [HBM_SPEC] * (len(srcs) + len(arrs)),
            out_specs=o_specs + [HBM_SPEC] * len(arrs),
            out_shape=shapes + [jax.ShapeDtypeStruct(a.shape, a.dtype) for a in arrs],
            scratch_shapes=list(scratch_shapes) + [pltpu.SemaphoreType.DMA((r.n_sem,)) for r in riders for _ in range(2)],
            input_output_aliases=aliases,
            compiler_params=_params(*["arbitrary"] * len(grid)),
        )(*args, *srcs, *arrs)
        core, rest = res[:n_out], list(res[n_out:])
        carried, a0 = [], 0
        for r in riders:
            carried.append(rest[a0:a0 + len(r.arrays)])
            a0 += len(r.arrays)
        return (core[0] if single else core), carried

    return run


def _mm(name, dims, a, a_spec, b, b_spec, grid, tile, outs, epilogue=None, extras=(), riders=()):
    nk, ne, no = grid[2], len(extras), len(outs)

    def body(a_ref, b_ref, *refs):
        e_refs, o_refs = refs[:ne], refs[ne:ne + no]
        part = lax.dot_general(a_ref[...], b_ref[...], dims, preferred_element_type=F32)

        def finish(acc):
            if epilogue is None:
                o_refs[0][...] = acc.astype(o_refs[0].dtype)
            else:
                epilogue(acc, [r[...] for r in e_refs], o_refs)

        if nk == 1:
            finish(part)
        else:
            acc_ref = refs[ne + no]
            k = pl.program_id(2)

            @pl.when(k == 0)
            def _():
                acc_ref[...] = part

            @pl.when(k > 0)
            def _():
                acc_ref[...] += part

            @pl.when(k == nk - 1)
            def _():
                finish(acc_ref[...])

    res = _carry(
        riders, body, name=name, grid=grid,
        in_specs=[a_spec, b_spec] + [s for _, s in extras],
        out_specs=[s for _, s in outs],
        out_shape=[s for s, _ in outs],
        scratch_shapes=[pltpu.VMEM(tile, F32)] if nk > 1 else [],
        semantics=("parallel", "parallel", "arbitrary"),
    )(a, b, *[e for e, _ in extras])
    res, carried = res if riders else (res, None)
    res = res[0] if no == 1 else res
    return (res, carried) if riders else res


def _tile_spec(tm, tn, col0=0):
    return pl.BlockSpec((tm, tn), lambda i, j, k: (i, j + col0))


def _out2d(m, n, dtype, tm, tn):
    return (jax.ShapeDtypeStruct((m, n), dtype), _tile_spec(tm, tn))


def _mm_fwd(name, a, w, tm, tn, tk, outs=None, epilogue=None, extras=(), col0=0, ncols=None, riders=()):
    m, kdim = a.shape
    if w.ndim == 3:
        per = w.shape[2] // tn
        n = ncols or N_SHARD * w.shape[2]
        w_spec = pl.BlockSpec((None, tk, tn), lambda i, j, k: ((j + col0) // per, k, (j + col0) % per))
    else:
        n = ncols or w.shape[1]
        w_spec = pl.BlockSpec((tk, tn), lambda i, j, k: (k, j + col0))
    if outs is None:
        outs = [_out2d(m, n, F32, tm, tn)]
    return _mm(name, NN, a, pl.BlockSpec((tm, tk), lambda i, j, k: (i, k)), w, w_spec,
               (m // tm, n // tn, kdim // tk), (tm, tn), outs, epilogue, extras, riders)


def _mm_bwd_x(name, g, g_spec, w, tm, tj, tc, m, n_contract, outs=None, epilogue=None, extras=(), riders=()):
    if w.ndim == 3:
        per = w.shape[2] // tc
        kdim = w.shape[1]
        w_spec = pl.BlockSpec((None, tj, tc), lambda i, j, n: (n // per, j, n % per))
    else:
        kdim = w.shape[0]
        w_spec = pl.BlockSpec((tj, tc), lambda i, j, n: (j, n))
    if outs is None:
        outs = [_out2d(m, kdim, F32, tm, tj)]
    return _mm(name, NT, g, g_spec, w, w_spec, (m // tm, kdim // tj, n_contract // tc),
               (tm, tj), outs, epilogue, extras, riders)


def _mm_bwd_w(name, a, g, g_spec, n, tk, tn, tm, sharded, riders=()):
    m, kdim = a.shape
    if sharded:
        per = (n // N_SHARD) // tn
        out = (jax.ShapeDtypeStruct((N_SHARD, kdim, n // N_SHARD), F32),
               pl.BlockSpec((None, tk, tn), lambda i, j, mm: (j // per, i, j % per)))
    else:
        out = (jax.ShapeDtypeStruct((kdim, n), F32), pl.BlockSpec((tk, tn), lambda i, j, mm: (i, j)))
    return _mm(name, TN, a, pl.BlockSpec((tm, tk), lambda i, j, mm: (mm, i)), g, g_spec,
               (kdim // tk, n // tn, m // tm), (tk, tn), [out], riders=riders)


def _ew(name, fn, tiles, fulls, outs, sums=(), ts=512, riders=()):
    tiles = [t if isinstance(t, tuple) else (t, pl.BlockSpec((ts, t.shape[1]), lambda i: (i, 0)))
             for t in tiles]
    s_rows = tiles[0][0].shape[-2]
    nt, nf, no = len(tiles), len(fulls), len(outs)

    def body(*refs):
        t_vals = [r[...] for r in refs[:nt]]
        f_vals = [r[...] for r in refs[nt:nt + nf]]
        o_refs, s_refs = refs[nt + nf:nt + nf + no], refs[nt + nf + no:]
        o_vals, s_vals = fn(t_vals, f_vals)
        for r, v in zip(o_refs, o_vals):
            r[...] = v.astype(r.dtype)
        for r, v in zip(s_refs, s_vals):
            part = jnp.sum(v, axis=0, keepdims=True)

            @pl.when(pl.program_id(0) == 0)
            def _():
                r[...] = part

            @pl.when(pl.program_id(0) > 0)
            def _():
                r[...] += part

    full_specs = [pl.BlockSpec(f.shape, lambda i, nd=f.ndim: (0,) * nd) for f in fulls]
    return _carry(
        riders, body, name=name, grid=(s_rows // ts,),
        in_specs=[s for _, s in tiles] + full_specs,
        out_specs=[pl.BlockSpec((ts, c), lambda i: (i, 0)) for c, _ in outs]
        + [pl.BlockSpec((1, c), lambda i: (0, 0)) for c in sums],
        out_shape=[jax.ShapeDtypeStruct((s_rows, c), dt) for c, dt in outs]
        + [jax.ShapeDtypeStruct((1, c), F32) for c in sums],
        semantics=("arbitrary",),
    )(*[t for t, _ in tiles], *fulls)


def _rms_fwd(name, x, g, riders=()):
    def fn(t, f):
        xv = t[0]
        r = lax.rsqrt(jnp.mean(xv * xv, axis=-1, keepdims=True) + EPS)
        return [xv * r * f[0]], []
    out = _ew(name, fn, [x], [g], [(x.shape[1], BF16)], riders=riders)
    return (out[0][0], out[1]) if riders else out[0]


def _rms_bwd(name, dh, x, g, dres, out_dtype=BF16, riders=()):
    def fn(t, f):
        dhv, xv, dr = [v.astype(F32) for v in t]
        r = lax.rsqrt(jnp.mean(xv * xv, axis=-1, keepdims=True) + EPS)
        xhat = xv * r
        u = dhv * f[0]
        dx = r * (u - xhat * jnp.mean(u * xhat, axis=-1, keepdims=True)) + dr
        return [dx], [dhv * xhat]
    c = x.shape[1]
    return _ew(name, fn, [dh, x, dres], [g], [(c, out_dtype)], sums=[c], riders=riders)


def _split3(x):
    x1 = x.astype(BF16)
    r1 = x - x1.astype(F32)
    x2 = r1.astype(BF16)
    x3 = (r1 - x2.astype(F32)).astype(BF16)
    return x1, x2, x3


def _rel_class(cp):
    far = (cp < MAX_REL) | (cp > BAND)
    return jnp.where(far, 2 * MAX_REL, BAND - cp)


def _skew_rows(x, sign):
    row = lax.broadcasted_iota(jnp.int32, x.shape, 0)
    for b in range(CHUNK.bit_length() - 1):
        shift = (1 << b) if sign > 0 else SKEW_W - (1 << b)
        x = jnp.where((row >> b) & 1 == 1, pltpu.roll(x, shift, 1), x)
    return x


def _roll_lanes(x, shift):
    return x if shift % SKEW_W == 0 else pltpu.roll(x, shift % SKEW_W, 1)


def _bias_expand(rel_bias, riders=()):
    rel = jnp.pad(rel_bias, ((0, 0), (0, REL_PAD - N_REL))).reshape(N_HEADS, 1, REL_PAD)

    def body(rel_ref, o_ref):
        cls = lax.broadcasted_iota(jnp.int32, (REL_PAD, SKEW_W), 0)
        cp = lax.broadcasted_iota(jnp.int32, (REL_PAD, SKEW_W), 1)
        onehot = (cls == _rel_class(cp)).astype(BF16)
        rel8 = jnp.broadcast_to(rel_ref[...], (8, REL_PAD))
        trow = sum(jnp.dot(p, onehot, preferred_element_type=F32) for p in _split3(rel8))[0:1]
        first = _skew_rows(jnp.broadcast_to(trow, (CHUNK, SKEW_W)), +1)
        full = jnp.concatenate([_roll_lanes(first, CHUNK * g) for g in range(SUPER // CHUNK)], axis=0)[:, :BAND]
        qc = lax.broadcasted_iota(jnp.int32, (SUPER, BAND), 0) // CHUNK
        kc = lax.broadcasted_iota(jnp.int32, (SUPER, BAND), 1) // CHUNK
        on_band = (kc >= qc) & (kc <= qc + N_PREV_CHUNKS)
        o_ref[...] = jnp.where(on_band, full, NEG_INF).T

    return _carry(
        riders, body, name="bias_expand", grid=(N_HEADS,),
        in_specs=[pl.BlockSpec((None, 1, REL_PAD), lambda h: (h, 0, 0))],
        out_specs=pl.BlockSpec((None, BAND, SUPER), lambda h: (h, 0, 0)),
        out_shape=jax.ShapeDtypeStruct((N_HEADS, BAND, SUPER), F32),
        semantics=("arbitrary",),
    )(rel)


def _bias_reduce(dbias, riders=()):
    def body(d_ref, o_ref):
        x = jnp.concatenate([d_ref[...].T, jnp.zeros((SUPER, SKEW_W - BAND), F32)], axis=1)
        folded = sum(_roll_lanes(x[CHUNK * g:CHUNK * (g + 1)], -CHUNK * g) for g in range(SUPER // CHUNK))
        diag = jnp.sum(_skew_rows(folded, -1), axis=0, keepdims=True)
        cp = lax.broadcasted_iota(jnp.int32, (SKEW_W, REL_PAD), 0)
        cls = lax.broadcasted_iota(jnp.int32, (SKEW_W, REL_PAD), 1)
        onehot = (cls == _rel_class(cp)).astype(BF16)
        diag8 = jnp.broadcast_to(diag, (8, SKEW_W))
        o_ref[...] = sum(jnp.dot(p, onehot, preferred_element_type=F32) for p in _split3(diag8))[0:1]

    out = _carry(
        riders, body, name="bias_reduce", grid=(N_HEADS,),
        in_specs=[pl.BlockSpec((None, BAND, SUPER), lambda h: (h, 0, 0))],
        out_specs=pl.BlockSpec((None, 1, REL_PAD), lambda h: (h, 0, 0)),
        out_shape=jax.ShapeDtypeStruct((N_HEADS, 1, REL_PAD), F32),
        semantics=("arbitrary",),
    )(dbias)
    out, carried = out if riders else (out, None)
    out = out.reshape(N_HEADS, REL_PAD)[:, :N_REL]
    return (out, carried) if riders else out


HEADS_PER_STEP = 8
HEAD_COLS = HEADS_PER_STEP * HEAD_DIM
N_HEAD_GROUPS = N_HEADS // HEADS_PER_STEP


def _unit(x):
    r = lax.rsqrt(jnp.mean(x * x, axis=-1, keepdims=True) + EPS)
    return x * r, r


def _scores_t(qs, kn, bias_t, dead):
    s = jnp.concatenate([lax.dot_general(k, qs, NT, preferred_element_type=F32) for k in kn], axis=0) + bias_t
    return jnp.where(dead, NEG_INF, s)


def _band_specs(nb, col0, clamp_hi):
    def spec(d):
        def index(hg, i):
            blk = jnp.maximum(i - d, 0)
            if clamp_hi:
                blk = jnp.minimum(blk, nb - 1)
            return (blk, col0 + hg)
        return pl.BlockSpec((SUPER, HEAD_COLS), index)
    return [spec(2), spec(1), spec(0)]


def _head_sums(y):
    cols = y.shape[1]
    same_head = (lax.broadcasted_iota(jnp.int32, (cols, cols), 0) // HEAD_DIM
                 == lax.broadcasted_iota(jnp.int32, (cols, cols), 1) // HEAD_DIM).astype(BF16)
    hi = y.astype(BF16)
    lo = (y - hi.astype(F32)).astype(BF16)
    return (jnp.dot(hi, same_head, preferred_element_type=F32)
            + jnp.dot(lo, same_head, preferred_element_type=F32))


def _head_unit(x):
    r = lax.rsqrt(_head_sums(x * x) * (1.0 / HEAD_DIM) + EPS)
    return x * r, r


def _head(hh):
    return slice(HEAD_DIM * hh, HEAD_DIM * (hh + 1))


def _key_block(j):
    return slice(SUPER * j, SUPER * (j + 1))


def _dead_keys(i):
    key = lax.broadcasted_iota(jnp.int32, (BAND, SUPER), 0)
    return key < (2 - i) * SUPER


LSE_ROWS = 8


def _attn_fwd(qkn, v, bias, riders=()):
    s_len = qkn.shape[0]
    nb = s_len // SUPER

    def body(q_ref, k0, k1, k2, v0, v1, v2, b_ref, o_ref, lse_ref):
        dead = _dead_keys(pl.program_id(1))
        outs = []

        def probabilities(hh):
            sl = _head(hh)
            s = _scores_t(q_ref[:, sl], [k0[:, sl], k1[:, sl], k2[:, sl]], b_ref[hh], dead)
            m = jnp.max(s, axis=0, keepdims=True)
            e = jnp.exp(s - m)
            l = jnp.sum(e, axis=0, keepdims=True)
            lse_ref[hh:hh + 1, :] = m + jnp.log(l)
            return (e * (1.0 / l)).astype(BF16), sl

        def weighted_values(p, sl):
            outs.append(sum(lax.dot_general(p[_key_block(j), :], vj[:, sl], TN, preferred_element_type=F32)
                            for j, vj in enumerate((v0, v1, v2))))

        ready = probabilities(0)
        for hh in range(1, HEADS_PER_STEP):
            following = probabilities(hh)
            weighted_values(*ready)
            ready = following
        weighted_values(*ready)
        o_ref[...] = jnp.concatenate(outs, axis=1).astype(o_ref.dtype)

    return _carry(
        riders, body, name="attn_fwd", grid=(N_HEAD_GROUPS, nb),
        in_specs=[pl.BlockSpec((SUPER, HEAD_COLS), lambda hg, i: (i, hg))]
        + _band_specs(nb, N_HEAD_GROUPS, False) + _band_specs(nb, 0, False)
        + [pl.BlockSpec((HEADS_PER_STEP, BAND, SUPER), lambda hg, i: (hg, 0, 0))],
        out_specs=[pl.BlockSpec((SUPER, HEAD_COLS), lambda hg, i: (i, hg)),
                   pl.BlockSpec((None, LSE_ROWS, SUPER), lambda hg, i: (hg, 0, i))],
        out_shape=[jax.ShapeDtypeStruct((s_len, D_MODEL), BF16),
                   jax.ShapeDtypeStruct((N_HEAD_GROUPS, LSE_ROWS, s_len), F32)],
        semantics=("parallel", "arbitrary"),
    )(qkn, qkn, qkn, qkn, v, v, v, bias)


def _attn_bwd(qkn, v, d_out, bias, lse, riders=()):
    s_len = qkn.shape[0]
    nb = s_len // SUPER

    def body(q_ref, k0, k1, k2, v0, v1, v2, do_ref, b_ref, lse_ref, dp_ref, db_ref, aq_ref, ak_ref, av_ref):
        i = pl.program_id(1)

        @pl.when(i == 0)
        def _():
            aq_ref[...] = jnp.zeros_like(aq_ref)
            ak_ref[...] = jnp.zeros_like(ak_ref)
            av_ref[...] = jnp.zeros_like(av_ref)
            db_ref[...] = jnp.zeros_like(db_ref)

        @pl.when(i < nb)
        def _():
            dead = _dead_keys(i)
            dq, dk, dv = [], [[], [], []], [[], [], []]

            def softmax_grad(hh):
                sl = _head(hh)
                qs, do = q_ref[:, sl], do_ref[:, sl]
                kn = [k0[:, sl], k1[:, sl], k2[:, sl]]
                vv = [v0[:, sl], v1[:, sl], v2[:, sl]]
                p = jnp.exp(_scores_t(qs, kn, b_ref[hh], dead) - lse_ref[hh:hh + 1, :])
                dp = jnp.concatenate([lax.dot_general(vj, do, NT, preferred_element_type=F32) for vj in vv], axis=0)
                ds = p * (dp - jnp.sum(p * dp, axis=0, keepdims=True))
                db_ref[hh] += ds
                return p.astype(BF16), ds.astype(BF16), qs, do, kn

            def operand_grads(pb, dsb, qs, do, kn):
                dq.append(sum(lax.dot_general(dsb[_key_block(j), :], kn[j], TN, preferred_element_type=F32)
                              for j in range(3)))
                for j in range(3):
                    dv[j].append(jnp.dot(pb[_key_block(j), :], do, preferred_element_type=F32))
                    dk[j].append(jnp.dot(dsb[_key_block(j), :], qs, preferred_element_type=F32))

            ready = softmax_grad(0)
            for hh in range(1, HEADS_PER_STEP):
                following = softmax_grad(hh)
                operand_grads(*ready)
                ready = following
            operand_grads(*ready)
            aq_ref[i % 3] = jnp.concatenate(dq, axis=1)
            for j in range(3):
                slot = (i + 1 + j) % 3
                if j < 2:
                    ak_ref[slot] += jnp.concatenate(dk[j], axis=1)
                    av_ref[slot] += jnp.concatenate(dv[j], axis=1)
                else:
                    ak_ref[slot] = jnp.concatenate(dk[j], axis=1)
                    av_ref[slot] = jnp.concatenate(dv[j], axis=1)

        slot = (i + 1) % 3
        dp_ref[0] = aq_ref[slot].astype(dp_ref.dtype)
        dp_ref[1] = ak_ref[slot].astype(dp_ref.dtype)
        dp_ref[2] = av_ref[slot].astype(dp_ref.dtype)

    def qrow(hg, i):
        return (jnp.minimum(i, nb - 1), hg)

    return _carry(
        riders, body, name="attn_bwd", grid=(N_HEAD_GROUPS, nb + 2),
        in_specs=[pl.BlockSpec((SUPER, HEAD_COLS), qrow)]
        + _band_specs(nb, N_HEAD_GROUPS, True) + _band_specs(nb, 0, True)
        + [pl.BlockSpec((SUPER, HEAD_COLS), qrow),
           pl.BlockSpec((HEADS_PER_STEP, BAND, SUPER), lambda hg, i: (hg, 0, 0)),
           pl.BlockSpec((None, LSE_ROWS, SUPER), lambda hg, i: (hg, 0, jnp.minimum(i, nb - 1)))],
        out_specs=[pl.BlockSpec((3, SUPER, HEAD_COLS), lambda hg, i: (0, jnp.maximum(i - 2, 0), hg)),
                   pl.BlockSpec((HEADS_PER_STEP, BAND, SUPER), lambda hg, i: (hg, 0, 0))],
        out_shape=[jax.ShapeDtypeStruct((6, s_len, D_MODEL), BF16),
                   jax.ShapeDtypeStruct((N_HEADS, BAND, SUPER), F32)],
        scratch_shapes=[pltpu.VMEM((3, SUPER, HEAD_COLS), F32)] * 3,
        semantics=("parallel", "arbitrary"),
    )(qkn, qkn, qkn, qkn, v, v, v, d_out, bias, lse)


def _qk_norm_bwd(dproj6, qk_raw, gq, gk):
    s_len = qk_raw.shape[0]
    ts = min(1024, s_len)

    nsteps = s_len // ts
    half = D_MODEL // 2

    def body(d_ref, raw_ref, gq_ref, gk_ref, o_ref, dgq_ref, dgk_ref, acc_ref):
        step = pl.program_id(0)

        @pl.when(step == 0)
        def _():
            acc_ref[...] = jnp.zeros_like(acc_ref)

        for piece, (g_ref, scale) in enumerate(((gq_ref, QK_SCALE), (gk_ref, 1.0))):
            for c0 in (0, half):
                xhat, r = _head_unit(raw_ref[:, piece * D_MODEL + c0:piece * D_MODEL + c0 + half].astype(F32))
                dn = d_ref[piece, :, c0:c0 + half].astype(F32) * scale
                u = dn * g_ref[...]
                dx = r * (u - xhat * (_head_sums(u * xhat) * (1.0 / HEAD_DIM)))
                o_ref[piece, :, c0:c0 + half] = dx.astype(o_ref.dtype)
                acc_ref[piece:piece + 1, c0:c0 + half] += jnp.sum(dn * xhat, axis=0, keepdims=True)

        @pl.when(step == nsteps - 1)
        def _():
            lane = lax.broadcasted_iota(jnp.int32, (D_MODEL, LANE), 0) % HEAD_DIM
            fold = (lane == lax.broadcasted_iota(jnp.int32, (D_MODEL, LANE), 1)).astype(BF16)
            tot = sum(jnp.dot(p, fold, preferred_element_type=F32) for p in _split3(acc_ref[...]))
            dgq_ref[...] = tot[0:1, :HEAD_DIM]
            dgk_ref[...] = tot[1:2, :HEAD_DIM]

    gain = pl.BlockSpec((1, half), lambda i: (0, 0))
    small = pl.BlockSpec((1, HEAD_DIM), lambda i: (0, 0))
    per_head = lambda g: jnp.tile(g, (1, half // HEAD_DIM))
    return pl.pallas_call(
        body, name="qk_norm_bwd", grid=(nsteps,),
        in_specs=[pl.BlockSpec((2, ts, D_MODEL), lambda i: (0, i, 0)),
                  pl.BlockSpec((ts, 2 * D_MODEL), lambda i: (i, 0)), gain, gain],
        out_specs=[pl.BlockSpec((2, ts, D_MODEL), lambda i: (0, i, 0)), small, small],
        out_shape=[jax.ShapeDtypeStruct(dproj6.shape, dproj6.dtype),
                   jax.ShapeDtypeStruct((1, HEAD_DIM), F32), jax.ShapeDtypeStruct((1, HEAD_DIM), F32)],
        scratch_shapes=[pltpu.VMEM((8, D_MODEL), F32)],
        input_output_aliases={0: 0},
        compiler_params=_params("arbitrary"),
    )(dproj6, qk_raw, per_head(gq), per_head(gk))


CONV_ROWS = 512
HALO = 16


def _rows_with_halo(ref, r0, n, front, s_len):
    zeros = jnp.zeros((HALO, ref.shape[1]), F32)
    if front:
        return (jnp.concatenate([zeros, ref[0:n, :].astype(F32)], axis=0) if r0 == 0
                else ref[r0 - HALO:r0 + n, :].astype(F32))
    return (jnp.concatenate([ref[r0:r0 + n, :].astype(F32), zeros], axis=0) if r0 + n == s_len
            else ref[r0:r0 + n + HALO, :].astype(F32))


def _earlier(ext, k):
    return pltpu.roll(ext, k, 0)[HALO:]


def _later(ext, k):
    n = ext.shape[0]
    return pltpu.roll(ext, n - k, 0)[:n - HALO]


def _conv_cols(col0):
    return lambda s_len: pl.BlockSpec((s_len, LANE), lambda j: (0, col0 + j))


def _conv_fwd(proj, conv_w, conv_b, riders=()):
    s_len = proj.shape[0]

    def body(bg_ref, cg_ref, xc_ref, w_ref, b_ref, o_ref):
        w = [w_ref[t:t + 1, :] for t in range(3)]
        for r0 in range(0, s_len, CONV_ROWS):
            u = _rows_with_halo(cg_ref, r0, CONV_ROWS, True, s_len) * \
                _rows_with_halo(xc_ref, r0, CONV_ROWS, True, s_len)
            conv = b_ref[...] + w[0] * _earlier(u, 2) + w[1] * _earlier(u, 1) + w[2] * u[HALO:]
            o_ref[r0:r0 + CONV_ROWS, :] = (bg_ref[r0:r0 + CONV_ROWS, :].astype(F32) * conv).astype(o_ref.dtype)

    return _carry(
        riders, body, name="conv_fwd", grid=(D_MODEL // LANE,),
        in_specs=[_conv_cols(0)(s_len), _conv_cols(8)(s_len), _conv_cols(16)(s_len),
                  pl.BlockSpec((3, LANE), lambda j: (0, j)), pl.BlockSpec((1, LANE), lambda j: (0, j))],
        out_specs=pl.BlockSpec((s_len, LANE), lambda j: (0, j)),
        out_shape=jax.ShapeDtypeStruct((s_len, D_MODEL), BF16),
        semantics=("parallel",),
    )(proj, proj, proj, conv_w, conv_b)


def _conv_bwd(dproj6, dy, proj, conv_w, conv_b, riders=()):
    s_len = proj.shape[0]

    def body(dy_ref, bg_ref, cg_ref, xc_ref, w_ref, b_ref, _, dp_ref, dw_ref):
        w = [w_ref[t:t + 1, :] for t in range(3)]
        acc = [jnp.zeros((1, LANE), F32) for _ in range(4)]
        for r0 in range(0, s_len, CONV_ROWS):
            rows = slice(r0, r0 + CONV_ROWS)
            u = _rows_with_halo(cg_ref, r0, CONV_ROWS, True, s_len) * \
                _rows_with_halo(xc_ref, r0, CONV_ROWS, True, s_len)
            u2, u1, u0 = _earlier(u, 2), _earlier(u, 1), u[HALO:]
            conv = b_ref[...] + w[0] * u2 + w[1] * u1 + w[2] * u0
            dp_ref[0, rows, :] = (dy_ref[rows, :].astype(F32) * conv).astype(dp_ref.dtype)
            dconv_ext = _rows_with_halo(dy_ref, r0, CONV_ROWS, False, s_len) * \
                _rows_with_halo(bg_ref, r0, CONV_ROWS, False, s_len)
            dconv = dconv_ext[:CONV_ROWS]
            for t, term in enumerate([dconv * u2, dconv * u1, dconv * u0, dconv]):
                acc[t] = acc[t] + jnp.sum(term, axis=0, keepdims=True)
            du = w[2] * dconv + w[1] * _later(dconv_ext, 1) + w[0] * _later(dconv_ext, 2)
            dp_ref[1, rows, :] = (du * xc_ref[rows, :].astype(F32)).astype(dp_ref.dtype)
            dp_ref[2, rows, :] = (du * cg_ref[rows, :].astype(F32)).astype(dp_ref.dtype)
        dw_ref[...] = jnp.zeros_like(dw_ref)
        for t in range(4):
            dw_ref[t:t + 1, :] = acc[t]

    return _carry(
        riders, body, name="conv_bwd", grid=(D_MODEL // LANE,),
        in_specs=[pl.BlockSpec((s_len, LANE), lambda j: (0, j)),
                  _conv_cols(0)(s_len), _conv_cols(8)(s_len), _conv_cols(16)(s_len),
                  pl.BlockSpec((3, LANE), lambda j: (0, j)), pl.BlockSpec((1, LANE), lambda j: (0, j)),
                  pl.BlockSpec(memory_space=pl.ANY)],
        out_specs=[pl.BlockSpec((3, s_len, LANE), lambda j: (1, 0, j)),
                   pl.BlockSpec((8, LANE), lambda j: (0, j))],
        out_shape=[jax.ShapeDtypeStruct(dproj6.shape, dproj6.dtype),
                   jax.ShapeDtypeStruct((8, D_MODEL), F32)],
        input_output_aliases={6: 0},
        semantics=("parallel",),
    )(dy, proj, proj, proj, conv_w, conv_b, dproj6)


class _NoComm:
    def __init__(self, early, late):
        self.early, self.late = early, late

    def riders(self, at):
        return ()

    def done(self, at, carried):
        pass

    def early_weights(self):
        return self.early[0], self.early[2]

    def gate_weight(self):
        return self.early[1]

    def late_weights(self):
        return self.late

    def grads_ready(self, group, grads):
        pass


def _d_input(name, terms, riders=()):
    s_len, kdim = terms[0][0].shape[1], terms[0][1].shape[1]
    tm, chunk = 512, 512
    n = len(terms)

    def body(*refs):
        o_ref = refs[-1]
        acc = jnp.zeros(o_ref.shape, F32)
        for p_ref, w_ref in zip(refs[:n], refs[n:2 * n]):
            n_piece, _, width = p_ref.shape
            per_shard = w_ref.shape[2]
            for c0 in range(0, n_piece * width, chunk):
                acc = acc + lax.dot_general(p_ref[c0 // width, :, c0 % width:c0 % width + chunk],
                                            w_ref[c0 // per_shard, :, c0 % per_shard:c0 % per_shard + chunk],
                                            NT, preferred_element_type=F32)
        o_ref[...] = acc.astype(o_ref.dtype)

    tile = pl.BlockSpec((tm, kdim), lambda i: (i, 0))
    return _carry(
        riders, body, name=name, grid=(s_len // tm,),
        in_specs=[pl.BlockSpec((p.shape[0], tm, p.shape[2]), lambda i: (0, i, 0)) for p, _ in terms]
        + [pl.BlockSpec(w.shape, lambda i: (0, 0, 0), pipeline_mode=pl.Buffered(1)) for _, w in terms],
        out_specs=tile, out_shape=jax.ShapeDtypeStruct((s_len, kdim), BF16),
        semantics=("parallel",),
    )(*[p for p, _ in terms], *[w for _, w in terms])


def _local_grads(x, target, norm1_g, q_norm_g, k_norm_g, rel_bias, conv_b, b_gate, norm2_g, comm):
    s_len = x.shape[0]
    tm = min(1024, s_len)
    row = lambda v: v.reshape(1, -1)

    def carrying(at, fn, *args, **kw):
        riders = comm.riders(at)
        out = fn(*args, riders=riders, **kw)
        if riders:
            out, carried = out
            comm.done(at, carried)
        return out

    bias = carrying("bias_expand", _bias_expand, rel_bias)
    h = carrying("norm1", _rms_fwd, "norm1", x, row(norm1_g))
    w_in3, conv_w = comm.early_weights()
    gq, gk = row(q_norm_g), row(k_norm_g)

    def qk_epi(acc, e, o):
        o[0][...] = acc.astype(BF16)
        gain = jnp.where(pl.program_id(1) < 2, e[0] * QK_SCALE, e[1])
        o[1][...] = (_head_unit(acc)[0] * gain).astype(BF16)
    small = pl.BlockSpec((1, 512), lambda i, j, k: (0, 0))
    per_head = lambda g: jnp.tile(g, (1, 512 // HEAD_DIM))
    qk_raw, qkn = carrying("proj_qk", _mm_fwd, "proj_qk", h, w_in3, tm, 512, D_MODEL, ncols=2 * D_MODEL,
                           epilogue=qk_epi, outs=[_out2d(s_len, 2 * D_MODEL, BF16, tm, 512)] * 2,
                           extras=[(per_head(gq), small), (per_head(gk), small)])
    v = carrying("proj_v", _mm_fwd, "proj_v", h, w_in3, tm, 512, D_MODEL, col0=4, ncols=D_MODEL,
                 outs=[_out2d(s_len, D_MODEL, BF16, tm, 512)])
    w_gate3 = comm.gate_weight()
    conv_in = _mm_fwd("proj_conv", h, w_in3, tm, 1536, D_MODEL, col0=2, ncols=3 * D_MODEL,
                      outs=[_out2d(s_len, 3 * D_MODEL, BF16, tm, 1536)])

    def gate_epi(acc, e, o):
        o[0][...] = jax.nn.sigmoid(acc + e[0]).astype(BF16)
    gates = _mm_fwd("gates", h, w_gate3, tm, 512, D_MODEL, epilogue=gate_epi,
                    outs=[_out2d(s_len, 2 * D_MODEL, BF16, tm, 512)],
                    extras=[(row(b_gate), pl.BlockSpec((1, 512), lambda i, j, k: (0, j)))])

    attn, lse = carrying("attn_fwd", _attn_fwd, qkn, v, bias)
    yconv = carrying("conv_fwd", _conv_fwd, conv_in, conv_w, row(conv_b))
    w_ap, w_cp, w_out, w_up3, w_down = comm.late_weights()
    tw = 1024
    ya = _mm_fwd("attn_proj", attn, w_ap, tm, tw, D_MODEL, outs=[_out2d(s_len, D_MODEL, BF16, tm, tw)])

    def merge_epi(acc, e, o):
        ya_v, ga, gc = [t.astype(F32) for t in e]
        o[0][...] = acc.astype(BF16)
        o[1][...] = (ga * ya_v + gc * acc).astype(BF16)
    gate_a, gate_c = (gates, _tile_spec(tm, tw, 0)), (gates, _tile_spec(tm, tw, 1))
    yc, merged = _mm_fwd("conv_proj", yconv, w_cp, tm, tw, D_MODEL, epilogue=merge_epi,
                         outs=[_out2d(s_len, D_MODEL, BF16, tm, tw), _out2d(s_len, D_MODEL, BF16, tm, tw)],
                         extras=[(ya, _tile_spec(tm, tw)), gate_a, gate_c])

    def res_epi(acc, e, o):
        o[0][...] = e[0] + acc
    x1 = _mm_fwd("out_proj", merged, w_out, tm, tw, D_MODEL, epilogue=res_epi,
                 extras=[(x, _tile_spec(tm, tw))])
    h2 = _rms_fwd("norm2", x1, row(norm2_g))

    def up_epi(acc, e, o):
        o[0][...] = jnp.square(jnp.maximum(acc, 0.0)).astype(BF16)
    act = _mm_fwd("mlp_up", h2, w_up3, tm, tw, D_MODEL, epilogue=up_epi, outs=[_out2d(s_len, D_FF, BF16, tm, tw)])

    def loss_epi(acc, e, o):
        err = e[0] + acc - e[1]
        o[0][...] = (err * (1.0 / D_MODEL)).astype(BF16)
        sq = err * err
        part = sq[:, 0:LANE]
        for c0 in range(LANE, D_MODEL, LANE):
            part = part + sq[:, c0:c0 + LANE]
        o[1][...] = jnp.sum(part.reshape(tl // 8, 8, LANE), axis=0)
    tl = 512
    dy_b, loss_part = _mm_fwd(
        "mlp_down", act, w_down, tl, D_MODEL, D_FF, epilogue=loss_epi,
        outs=[_out2d(s_len, D_MODEL, BF16, tl, D_MODEL),
              (jax.ShapeDtypeStruct((8 * (s_len // tl), LANE), F32), pl.BlockSpec((8, LANE), lambda i, j, k: (i, 0)))],
        extras=[(x1, _tile_spec(tl, D_MODEL)), (target, _tile_spec(tl, D_MODEL))])

    def dup_epi(acc, e, o):
        o[0][...] = (acc * (2.0 * jnp.sqrt(e[0].astype(F32)))).astype(BF16)
    full = lambda cols: pl.BlockSpec((tm, cols), lambda i, j, n: (i, n))
    tokens = lambda cols: pl.BlockSpec((s_len, cols), lambda i, j, m: (m, j))
    dup = _mm_bwd_x("d_act", dy_b, full(D_MODEL), w_down, tm, tw, D_MODEL, s_len, D_MODEL, epilogue=dup_epi,
                    outs=[_out2d(s_len, D_FF, BF16, tm, tw)], extras=[(act, _tile_spec(tm, tw))])
    g_down = _mm_bwd_w("g_down", act, dy_b, tokens(D_MODEL), D_MODEL, 512, D_MODEL, s_len, False)
    g_up = _mm_bwd_w("g_up", h2, dup, tokens(512), D_FF, D_MODEL, 512, s_len, True)
    comm.grads_ready("mlp", dict(w_down=g_down, w_up=g_up))
    dh2 = carrying("d_h2", _d_input, "d_h2", [(dup.reshape(1, s_len, D_FF), w_up3)])
    dx1_b, dg2 = _rms_bwd("norm2_bwd", dh2, x1, row(norm2_g), dy_b)

    def dmerge_epi(acc, e, o):
        ya_v, yc_v, ga, gc = [t.astype(F32) for t in e]
        o[0][...] = (acc * ga).astype(BF16)
        o[1][...] = (acc * gc).astype(BF16)
        o[2][0] = (acc * ya_v * ga * (1.0 - ga)).astype(BF16)
        o[2][1] = (acc * yc_v * gc * (1.0 - gc)).astype(BF16)
    dya, dyc, dgp2 = _mm_bwd_x(
        "d_merged", dx1_b, full(D_MODEL), w_out, tm, tw, D_MODEL, s_len, D_MODEL, epilogue=dmerge_epi,
        outs=[_out2d(s_len, D_MODEL, BF16, tm, tw), _out2d(s_len, D_MODEL, BF16, tm, tw),
              (jax.ShapeDtypeStruct((2, s_len, D_MODEL), BF16), pl.BlockSpec((2, tm, tw), lambda i, j, n: (0, i, j)))],
        extras=[(ya, _tile_spec(tm, tw)), (yc, _tile_spec(tm, tw)), gate_a, gate_c])
    g_out = _mm_bwd_w("g_out", merged, dx1_b, tokens(512), D_MODEL, D_MODEL, 512, s_len, False)
    d_attn = _mm_bwd_x("d_attn", dya, full(D_MODEL), w_ap, tm, tw, D_MODEL, s_len, D_MODEL,
                       outs=[_out2d(s_len, D_MODEL, BF16, tm, tw)])
    g_ap = _mm_bwd_w("g_attn_proj", attn, dya, tokens(512), D_MODEL, D_MODEL, 512, s_len, False)
    d_yconv = _mm_bwd_x("d_yconv", dyc, full(D_MODEL), w_cp, tm, tw, D_MODEL, s_len, D_MODEL,
                        outs=[_out2d(s_len, D_MODEL, BF16, tm, tw)])
    g_cp = _mm_bwd_w("g_conv_proj", yconv, dyc, tokens(512), D_MODEL, D_MODEL, 512, s_len, False)
    comm.grads_ready("proj", dict(w_out=g_out, w_attn_proj=g_ap, w_conv_proj=g_cp))

    dproj6, dbias = carrying("attn_bwd", _attn_bwd, qkn, v, d_attn, bias, lse)
    dproj6, dgq, dgk = _qk_norm_bwd(dproj6, qk_raw, gq, gk)
    dproj6, dconv_wb = carrying("conv_bwd", _conv_bwd, dproj6, d_yconv, conv_in, conv_w, row(conv_b))

    piece = lambda width: (lambda blk: (blk * width) // D_MODEL, lambda blk: (blk * width % D_MODEL) // width)
    pc, cb = piece(512)
    pieces = pl.BlockSpec((None, s_len, 512), lambda i, j, m: (pc(j), m, cb(j)))
    g_in = carrying("g_in", _mm_bwd_w, "g_in", h, dproj6, pieces, 6 * D_MODEL, D_MODEL, 512, s_len, True)
    g_gate = carrying("g_gate", _mm_bwd_w, "g_gate", h, dgp2, pieces, 2 * D_MODEL, D_MODEL, 512, s_len, True)
    comm.grads_ready("in", dict(w_in=g_in, w_gate=g_gate))
    d_rel = carrying("bias_reduce", _bias_reduce, dbias)
    dh = carrying("d_h", _d_input, "d_h", [(dproj6, w_in3), (dgp2, w_gate3)])
    grad_x, dg1 = carrying("norm1_bwd", _rms_bwd, "norm1_bwd", dh, x, row(norm1_g), dx1_b, out_dtype=F32)

    def bsum(t, f):
        return [], [t[0].astype(F32), t[1].astype(F32)]
    ts = 512
    db_a, db_c = carrying("b_gate_sum", _ew, "b_gate_sum", bsum,
                          [(dgp2, pl.BlockSpec((None, ts, D_MODEL), lambda i: (0, i, 0))),
                           (dgp2, pl.BlockSpec((None, ts, D_MODEL), lambda i: (1, i, 0)))],
                          [], [], sums=[D_MODEL, D_MODEL], ts=ts)

    big = dict(w_in=g_in, w_attn_proj=g_ap, w_conv_proj=g_cp, w_gate=g_gate, w_out=g_out,
               w_up=g_up, w_down=g_down)
    small = dict(norm1_g=dg1, norm2_g=dg2, conv_wb=dconv_wb, b_gate=(db_a, db_c),
                 q_norm_g=dgq, k_norm_g=dgk, rel_bias=d_rel)
    return loss_part, grad_x, big, small


def _finish_loss(loss_part):
    def body(l_ref, lo_ref):
        total = jnp.sum(jnp.sum(l_ref[...], axis=0, keepdims=True), axis=1, keepdims=True)
        lo_ref[...] = jnp.broadcast_to(total * (0.5 / D_MODEL), lo_ref.shape)

    return pl.pallas_call(body, name="finish_loss", out_shape=jax.ShapeDtypeStruct((8, LANE), F32))(loss_part)


HBM_SPEC = pl.BlockSpec(memory_space=pl.ANY)


def _place():
    return lax.axis_index("x"), lax.axis_index("y"), lax.axis_index("c")


def _other_chips(x, y):
    return [(1 - x, y), (x, 1 - y), (1 - x, 1 - y)]


def _cast_into_slot(name, where, w):
    r, cols = w.shape
    ts = 256

    def body(w_ref, x_ref, o_ref):
        o_ref[...] = x_ref[...].astype(o_ref.dtype)

    return pl.pallas_call(
        body, name=name,
        grid_spec=pltpu.PrefetchScalarGridSpec(
            num_scalar_prefetch=1, grid=(r // ts,),
            in_specs=[pl.BlockSpec((ts, cols), lambda i, w: (i, 0))],
            out_specs=pl.BlockSpec((None, ts, cols), lambda i, w: (w[0], i, 0))),
        out_shape=jax.ShapeDtypeStruct((N_SHARD, r, cols), BF16),
        compiler_params=_params("parallel"),
    )(where, w)


def _remote(src, dst, send, recv, k, to):
    return pltpu.make_async_remote_copy(src_ref=src, dst_ref=dst, send_sem=send.at[k], recv_sem=recv.at[k],
                                        device_id=to, device_id_type=MESH)


def _gather_riders(slots):
    n = len(slots)

    def half(refs, w, shard, which):
        hr = slots[w].shape[1] // 2
        return refs[w].at[shard, pl.ds(which * hr, hr)]

    def each(fn):
        x, y, c = _place()
        for w in range(n):
            for j, chip in enumerate(_other_chips(x, y)):
                fn(w, 3 * w + j, 2 * x + y, 2 * chip[0] + chip[1], c, (*chip, c), (x, y, 1 - c))

    def chips_start(_, refs, send, recv):
        each(lambda w, k, me, them, c, peer, sib: _remote(half(refs, w, me, c), half(refs, w, me, c), send, recv, k, peer).start())

    def chips_finish(_, refs, send, recv):
        each(lambda w, k, me, them, c, peer, sib: _remote(half(refs, w, them, c), half(refs, w, them, c), send, recv, k, peer).wait_recv())
        each(lambda w, k, me, them, c, peer, sib: _remote(half(refs, w, me, c), half(refs, w, me, c), send, recv, k, peer).wait_send())

    def sibling_start(_, refs, send, recv):
        each(lambda w, k, me, them, c, peer, sib: _remote(half(refs, w, them, c), half(refs, w, them, c), send, recv, k, sib).start())

    def sibling_finish(_, refs, send, recv):
        each(lambda w, k, me, them, c, peer, sib: _remote(half(refs, w, them, 1 - c), half(refs, w, them, 1 - c), send, recv, k, sib).wait_recv())
        each(lambda w, k, me, them, c, peer, sib: _remote(half(refs, w, them, c), half(refs, w, them, c), send, recv, k, sib).wait_send())

    return (lambda s: _Rider([], s, 3 * n, chips_start, chips_finish),
            lambda s: _Rider([], s, 3 * n, sibling_start, sibling_finish))


def _pair_exchange_rider(grads, landing):
    n = len(grads)

    def copies(srcs, dsts, send, recv):
        x, y, c = _place()
        out = []
        for w in range(n):
            hr = grads[w].shape[1] // 2
            out.append(_remote(srcs[w].at[:, pl.ds((1 - c) * hr, hr)], dsts[w], send, recv, w, (x, y, 1 - c)))
        return out

    def start(srcs, dsts, send, recv):
        for cp in copies(srcs, dsts, send, recv):
            cp.start()

    def finish(srcs, dsts, send, recv):
        for cp in copies(srcs, dsts, send, recv):
            cp.wait()

    return _Rider(grads, landing, n, start, finish)


def _row_tile(hr):
    return min(hr, 256)


def _pair_add(name, where, grad, got):
    _, hr, cols = got.shape
    tr = _row_tile(hr)
    nblk = hr // tr

    def body(w_ref, g_ref, r_ref, o_ref):
        o_ref[...] = (g_ref[...] + r_ref[...]).astype(o_ref.dtype)

    return pl.pallas_call(
        body, name=name,
        grid_spec=pltpu.PrefetchScalarGridSpec(
            num_scalar_prefetch=1, grid=(N_SHARD, nblk),
            in_specs=[pl.BlockSpec((None, tr, cols), lambda s, i, w: (s, w[1] * nblk + i, 0)),
                      pl.BlockSpec((None, tr, cols), lambda s, i, w: (s, i, 0))],
            out_specs=pl.BlockSpec((None, tr, cols), lambda s, i, w: (s, i, 0))),
        out_shape=jax.ShapeDtypeStruct(got.shape, BF16),
        compiler_params=_params("parallel", "parallel"),
    )(where, grad, got)


def _chip_exchange_rider(partials, landing):
    n = len(partials)

    def copies(srcs, dsts, send, recv):
        x, y, c = _place()
        return [_remote(srcs[w].at[2 * chip[0] + chip[1]], dsts[w].at[j], send, recv, 3 * w + j, (*chip, c))
                for w in range(n) for j, chip in enumerate(_other_chips(x, y))]

    def start(srcs, dsts, send, recv):
        for cp in copies(srcs, dsts, send, recv):
            cp.start()

    def finish(srcs, dsts, send, recv):
        for cp in copies(srcs, dsts, send, recv):
            cp.wait()

    return _Rider(partials, landing, 3 * n, start, finish)


def _final_add(name, where, grad, got, arrived):
    _, hr, cols = got.shape
    tr = _row_tile(hr)
    nblk = hr // tr

    def body(w_ref, g_ref, r_ref, a_ref, o_ref):
        acc = g_ref[...] + r_ref[...]
        for j in range(3):
            acc = acc + a_ref[j].astype(F32)
        o_ref[...] = acc

    return pl.pallas_call(
        body, name=name,
        grid_spec=pltpu.PrefetchScalarGridSpec(
            num_scalar_prefetch=1, grid=(nblk,),
            in_specs=[pl.BlockSpec((None, tr, cols), lambda i, w: (w[0], w[1] * nblk + i, 0)),
                      pl.BlockSpec((None, tr, cols), lambda i, w: (w[0], i, 0)),
                      pl.BlockSpec((3, tr, cols), lambda i, w: (0, i, 0))],
            out_specs=pl.BlockSpec((tr, cols), lambda i, w: (w[1] * nblk + i, 0))),
        out_shape=jax.ShapeDtypeStruct((2 * hr, cols), F32),
        compiler_params=_params("parallel"),
    )(where, grad, got, arrived)


def _pair_share_rider(shards):
    n = len(shards)

    def half(refs, w, which):
        hr = shards[w].shape[0] // 2
        return refs[w].at[pl.ds(which * hr, hr)]

    def start(_, refs, send, recv):
        x, y, c = _place()
        for w in range(n):
            _remote(half(refs, w, c), half(refs, w, c), send, recv, w, (x, y, 1 - c)).start()

    def finish(_, refs, send, recv):
        x, y, c = _place()
        for w in range(n):
            _remote(half(refs, w, 1 - c), half(refs, w, 1 - c), send, recv, w, (x, y, 1 - c)).wait_recv()
        for w in range(n):
            _remote(half(refs, w, c), half(refs, w, c), send, recv, w, (x, y, 1 - c)).wait_send()

    return _Rider([], shards, n, start, finish)


class _Exchange:
    PLAN = {"bias_expand": [("early", "gather")], "norm1": [("early", "forward")],
            "proj_qk": [("gate", "gather")], "proj_v": [("gate", "forward")],
            "attn_fwd": [("late", "gather")], "conv_fwd": [("late", "forward")],
            "d_h2": [("mlp", "pair")], "attn_bwd": [("mlp", "chips"), ("proj", "pair")], "conv_bwd": [("mlp", "share")],
            "g_in": [("proj", "chips")], "g_gate": [("proj", "share")],
            "bias_reduce": [("in", "pair")], "d_h": [("in", "chips")], "b_gate_sum": [("in", "share")]}

    def __init__(self, where, early_slots, gate_slots, late_slots):
        self.where = where
        self.slots = dict(early=early_slots, gate=gate_slots, late=late_slots)
        self.stage = {g: dict(zip(("gather", "forward"), _gather_riders(s))) for g, s in self.slots.items()}
        self.groups, self.reduced, self.pending = {}, {}, []

    def early_weights(self):
        w_in3, small = self.slots["early"]
        conv_w = small[:, :3, :].transpose(1, 0, 2).reshape(3, N_SHARD * small.shape[2])
        return w_in3, conv_w

    def gate_weight(self):
        return self.slots["gate"][0]

    def late_weights(self):
        rows = lambda a: a.reshape(a.shape[0] * a.shape[1], a.shape[2])
        w_ap, w_cp, w_out, w_up3, w_down = self.slots["late"]
        return rows(w_ap), rows(w_cp), rows(w_out), w_up3, rows(w_down)

    def grads_ready(self, group, grads):
        names = list(grads)
        g4 = [g if g.ndim == 3 else g.reshape(N_SHARD, -1, g.shape[1]) for g in grads.values()]
        self.groups[group] = dict(names=names, g4=g4)

    def riders(self, at):
        self.pending = self.PLAN.get(at, [])
        out = []
        for group, stage in self.pending:
            if group in self.slots:
                out.append(self.stage[group][stage](self.slots[group]))
                continue
            st = self.groups[group]
            if stage == "pair":
                landing = [lax.empty((N_SHARD, g.shape[1] // 2, g.shape[2]), F32) for g in st["g4"]]
                out.append(_pair_exchange_rider(st["g4"], landing))
            elif stage == "chips":
                landing = [lax.empty((3,) + p.shape[1:], p.dtype) for p in st["partial"]]
                out.append(_chip_exchange_rider(st["partial"], landing))
            else:
                out.append(_pair_share_rider(st["halves"]))
        return out

    def done(self, at, carried):
        for (group, stage), arrays in zip(self.pending, carried):
            if group in self.slots:
                self.slots[group] = arrays
                continue
            st = self.groups[group]
            tag = lambda what, n: what + "_" + n
            if stage == "pair":
                st["got"] = arrays
                st["partial"] = [_pair_add(tag("pair_add", n), self.where, g, r)
                                 for n, g, r in zip(st["names"], st["g4"], arrays)]
            elif stage == "chips":
                st["halves"] = [_final_add(tag("final_add", n), self.where, g, r, a)
                                for n, g, r, a in zip(st["names"], st["g4"], st["got"], arrays)]
            else:
                self.reduced.update(zip(st["names"], arrays))


SMALL_ROWS = 32
N_DEV = 8


def _all_reduce_small(pack):
    def body(p_ref, o_ref, buf, send, recv):
        x, y, c = _place()
        buf[4 * x + 2 * y + c] = p_ref[...]
        copies, waits = [], []
        for k in range(1, N_DEV):
            px = 1 - x if k & 4 else x
            py = 1 - y if k & 2 else y
            pc = 1 - c if k & 1 else c
            copies.append(pltpu.make_async_remote_copy(
                src_ref=p_ref, dst_ref=buf.at[4 * x + 2 * y + c], send_sem=send.at[k - 1],
                recv_sem=recv.at[k - 1], device_id=(px, py, pc), device_id_type=MESH))
            waits.append(pltpu.make_async_remote_copy(
                src_ref=p_ref, dst_ref=buf.at[4 * px + 2 * py + pc], send_sem=send.at[k - 1],
                recv_sem=recv.at[k - 1], device_id=(px, py, pc), device_id_type=MESH))
        for cp in copies:
            cp.start()
        for cp in waits:
            cp.wait_recv()
        acc = buf[0]
        for d in range(1, N_DEV):
            acc = acc + buf[d]
        o_ref[...] = acc
        for cp in copies:
            cp.wait_send()

    return pl.pallas_call(
        body, name="all_reduce_small",
        out_shape=jax.ShapeDtypeStruct(pack.shape, F32),
        scratch_shapes=[pltpu.VMEM((N_DEV,) + pack.shape, F32),
                        pltpu.SemaphoreType.DMA((N_DEV - 1,)), pltpu.SemaphoreType.DMA((N_DEV - 1,))],
    )(pack)


def _adamw(name, w, g, m, v):
    c1 = 1.0 - ADAM_B1 ** ADAM_STEP
    c2 = 1.0 - ADAM_B2 ** ADAM_STEP

    def fn(t, f):
        wv, gv, mv, vv = t
        m2 = ADAM_B1 * mv + (1.0 - ADAM_B1) * gv
        v2 = ADAM_B2 * vv + (1.0 - ADAM_B2) * (gv * gv)
        delta = -ADAM_LR * ((m2 / c1) / (jnp.sqrt(v2 / c2) + ADAM_EPS) + ADAM_WD * wv)
        return [delta, m2, v2], []

    cols = w.shape[1]
    return _ew(name, fn, [w, g, m, v], [], [(cols, F32)] * 3, ts=min(w.shape[0], 256))


LOSS_ROW = 26


def _pack_small(norm1_g, norm2_g, conv_b, b_gate, conv_w, q_norm_g, k_norm_g, rel_bias, loss=None):
    pack = jnp.zeros((SMALL_ROWS, D_MODEL), F32)
    for r0, v in ((0, norm1_g), (1, norm2_g), (2, conv_b), (3, b_gate.reshape(2, D_MODEL)), (5, conv_w),
                  (8, q_norm_g), (9, k_norm_g), (10, rel_bias)) + (((LOSS_ROW, loss),) if loss is not None else ()):
        v = v.reshape(-1, v.shape[-1]).astype(F32)
        pack = pack.at[r0:r0 + v.shape[0], :v.shape[1]].set(v)
    return pack


def _unpack_small(pack, conv_cols):
    return dict(norm1_g=pack[0], norm2_g=pack[1], conv_b=pack[2], b_gate=pack[3:5].reshape(2 * D_MODEL),
                conv_w=pack[5:8, :conv_cols], q_norm_g=pack[8, :HEAD_DIM], k_norm_g=pack[9, :HEAD_DIM],
                rel_bias=pack[10:10 + N_HEADS, :N_REL])


BIG = ["w_in", "w_attn_proj", "w_conv_proj", "w_gate", "w_out", "w_up", "w_down"]
LATE = ["w_attn_proj", "w_conv_proj", "w_out", "w_up", "w_down"]
WEIGHTS = ["norm1_g", "w_in", "q_norm_g", "k_norm_g", "rel_bias", "conv_w", "conv_b", "w_attn_proj",
           "w_conv_proj", "w_gate", "b_gate", "w_out", "norm2_g", "w_up", "w_down"]


def kernel(x, norm1_g, w_in, q_norm_g, k_norm_g, rel_bias, conv_w, conv_b, w_attn_proj, w_conv_proj, w_gate, b_gate, w_out, norm2_g, w_up, w_down, loss_target, m_norm1_g, m_w_in, m_q_norm_g, m_k_norm_g, m_rel_bias, m_conv_w, m_conv_b, m_w_attn_proj, m_w_conv_proj, m_w_gate, m_b_gate, m_w_out, m_norm2_g, m_w_up, m_w_down, v_norm1_g, v_w_in, v_q_norm_g, v_k_norm_g, v_rel_bias, v_conv_w, v_conv_b, v_w_attn_proj, v_w_conv_proj, v_w_gate, v_b_gate, v_w_out, v_norm2_g, v_w_up, v_w_down):
    given = dict(locals())
    w = {n: given[n] for n in WEIGHTS}
    m = {n: given["m_" + n] for n in WEIGHTS}
    v = {n: given["v_" + n] for n in WEIGHTS}
    s_len = x.shape[1]
    shard = 2 * lax.axis_index("x") + lax.axis_index("y")
    where = jnp.stack([shard, lax.axis_index("c")]).astype(jnp.int32)
    conv_cols = conv_w.shape[1]

    small_in = lax.dynamic_update_slice(jnp.zeros((N_SHARD, 16, conv_cols), F32), conv_w[None], (shard, 0, 0))
    slot = {n: _cast_into_slot("cast_" + n, where, w[n]) for n in BIG}
    comm = _Exchange(where, [slot["w_in"], small_in], [slot["w_gate"]], [slot[n] for n in LATE])

    loss_part, grad_x, _, small = _local_grads(
        x.reshape(s_len, D_MODEL), loss_target.reshape(s_len, D_MODEL), norm1_g, q_norm_g, k_norm_g,
        rel_bias, conv_b, b_gate, norm2_g, comm)
    grad = dict(comm.reduced)

    loss_local = _finish_loss(loss_part)
    pack = _pack_small(small["norm1_g"], small["norm2_g"], small["conv_wb"][3], jnp.concatenate(small["b_gate"], axis=1),
                       small["conv_wb"][0:3], small["q_norm_g"], small["k_norm_g"], small["rel_bias"],
                       loss=loss_local[0:1, :])
    total = _all_reduce_small(pack)
    g_small = _unpack_small(total, D_MODEL)
    g_small["conv_w"] = lax.dynamic_slice(g_small["conv_w"], (0, shard * conv_cols), (3, conv_cols))
    grad.update(g_small)

    delta, new_m, new_v = {}, {}, {}
    for n in BIG:
        delta[n], new_m[n], new_v[n] = _adamw("adamw_" + n, w[n], grad[n], m[n], v[n])
    small_names = [n for n in WEIGHTS if n not in BIG]
    packs = [_pack_small(**{n: src[n] for n in small_names}) for src in (w, grad, m, v)]
    for out, packed in zip((delta, new_m, new_v), _adamw("adamw_small", *packs)):
        out.update({n: a.reshape(w[n].shape) for n, a in _unpack_small(packed, conv_cols).items()})

    outs = [total[LOSS_ROW, 0], grad_x.reshape(x.shape)]
    for group in (grad, delta, new_m, new_v):
        outs += [group[n].reshape(w[n].shape) for n in WEIGHTS]
    return tuple(outs)
```

```python
import functools

import jax
import jax.numpy as jnp
from jax import lax
from jax.experimental import pallas as pl
from jax.experimental.pallas import tpu as pltpu

F32 = jnp.float32
BF16 = jnp.bfloat16

D_MODEL = 1024
N_HEADS = 16
HEAD_DIM = 64
CHUNK = 64
N_PREV_CHUNKS = 8
MAX_REL = 256
D_FF = 4096
N_REL = 2 * MAX_REL + 1
REL_PAD = 640
EPS = 1e-6
NEG_INF = -1e30
QK_SCALE = HEAD_DIM ** -0.5

SUPER = 4 * CHUNK
BAND = SUPER + N_PREV_CHUNKS * CHUNK
SKEW_W = 1024
N_SHARD = 4
LANE = 128
VMEM_LIMIT = 48 * 1024 * 1024

ADAM_LR = 0.001
ADAM_B1 = 0.9
ADAM_B2 = 0.999
ADAM_EPS = 1e-08
ADAM_WD = 0.01
ADAM_STEP = 10

MESH = pl.DeviceIdType.MESH
NN = (((1,), (0,)), ((), ()))
NT = (((1,), (1,)), ((), ()))
TN = (((0,), (0,)), ((), ()))


def _params(*sem):
    return pltpu.CompilerParams(dimension_semantics=sem or None, vmem_limit_bytes=VMEM_LIMIT)


HBM_SPEC = pl.BlockSpec(memory_space=pl.ANY)


class _Rider:
    def __init__(self, sources, arrays, n_sem, start, finish):
        self.sources, self.arrays, self.n_sem, self.start, self.finish = sources, arrays, n_sem, start, finish


def _carry(riders, body, *, name, out_shape, grid=(), in_specs=None, out_specs=None, scratch_shapes=(),
           semantics=(), input_output_aliases=None):
    aliases = dict(input_output_aliases or {})
    if not riders:
        kw = {} if in_specs is None else dict(in_specs=in_specs, out_specs=out_specs)
        return pl.pallas_call(body, name=name, grid=grid, out_shape=out_shape, scratch_shapes=scratch_shapes,
                              input_output_aliases=aliases, compiler_params=_params(*semantics), **kw)
    single = not isinstance(out_shape, (list, tuple))
    shapes = [out_shape] if single else list(out_shape)
    n_out, n_scr = len(shapes), len(scratch_shapes)
    srcs = [a for r in riders for a in r.sources]
    arrs = [a for r in riders for a in r.arrays]
    vmem = pl.BlockSpec(memory_space=pltpu.VMEM)

    def run(*args):
        n_in = len(args)

        def wrapped(*refs):
            pos = n_in
            src_refs = refs[pos:pos + len(srcs)]
            pos += len(srcs) + len(arrs)
            outs = refs[pos:pos + n_out]
            pos += n_out
            arr_refs = refs[pos:pos + len(arrs)]
            pos += len(arrs)
            scratch = refs[pos:pos + n_scr]
            sems = refs[pos + n_scr:]
            first, last = True, True
            for d, size in enumerate(grid):
                first = jnp.logical_and(first, pl.program_id(d) == 0)
                last = jnp.logical_and(last, pl.program_id(d) == size - 1)

            def each(method):
                s0 = a0 = 0
                for k, r in enumerate(riders):
                    getattr(r, method)(src_refs[s0:s0 + len(r.sources)], arr_refs[a0:a0 + len(r.arrays)],
                                       sems[2 * k], sems[2 * k + 1])
                    s0, a0 = s0 + len(r.sources), a0 + len(r.arrays)

            pl.when(first)(lambda: each("start"))
            body(*refs[:n_in], *outs, *scratch)
            pl.when(last)(lambda: each("finish"))

        ins = [vmem] * n_in if in_specs is None else list(in_specs)
        if out_specs is None:
            o_specs = [vmem] * n_out
        else:
            o_specs = [out_specs] if single else list(out_specs)
        for k in range(len(arrs)):
            aliases[n_in + len(srcs) + k] = n_out + k
        res = pl.pallas_call(
            wrapped, name=name, grid=grid,
            in_specs=ins + [HBM_SPEC] * (len(srcs) + len(arrs)),
            out_specs=o_specs + [HBM_SPEC] * len(arrs),
            out_shape=shapes + [jax.ShapeDtypeStruct(a.shape, a.dtype) for a in arrs],
            scratch_shapes=list(scratch_shapes) + [pltpu.SemaphoreType.DMA((r.n_sem,)) for r in riders for _ in range(2)],
            input_output_aliases=aliases,
            compiler_params=_params(*["arbitrary"] * len(grid)),
        )(*args, *srcs, *arrs)
        core, rest = res[:n_out], list(res[n_out:])
        carried, a0 = [], 0
        for r in riders:
            carried.append(rest[a0:a0 + len(r.arrays)])
            a0 += len(r.arrays)
        return (core[0] if single else core), carried

    return run


def _mm(name, dims, a, a_spec, b, b_spec, grid, tile, outs, epilogue=None, extras=(), riders=()):
    nk, ne, no = grid[2], len(extras), len(outs)

    def body(a_ref, b_ref, *refs):
        e_refs, o_refs = refs[:ne], refs[ne:ne + no]
        part = lax.dot_general(a_ref[...], b_ref[...], dims, preferred_element_type=F32)

        def finish(acc):
            if epilogue is None:
                o_refs[0][...] = acc.astype(o_refs[0].dtype)
            else:
                epilogue(acc, [r[...] for r in e_refs], o_refs)

        if nk == 1:
            finish(part)
        else:
            acc_ref = refs[ne + no]
            k = pl.program_id(2)

            @pl.when(k == 0)
            def _():
                acc_ref[...] = part

            @pl.when(k > 0)
            def _():
                acc_ref[...] += part

            @pl.when(k == nk - 1)
            def _():
                finish(acc_ref[...])

    res = _carry(
        riders, body, name=name, grid=grid,
        in_specs=[a_spec, b_spec] + [s for _, s in extras],
        out_specs=[s for _, s in outs],
        out_shape=[s for s, _ in outs],
        scratch_shapes=[pltpu.VMEM(tile, F32)] if nk > 1 else [],
        semantics=("parallel", "parallel", "arbitrary"),
    )(a, b, *[e for e, _ in extras])
    res, carried = res if riders else (res, None)
    res = res[0] if no == 1 else res
    return (res, carried) if riders else res


def _tile_spec(tm, tn, col0=0):
    return pl.BlockSpec((tm, tn), lambda i, j, k: (i, j + col0))


def _out2d(m, n, dtype, tm, tn):
    return (jax.ShapeDtypeStruct((m, n), dtype), _tile_spec(tm, tn))


def _mm_fwd(name, a, w, tm, tn, tk, outs=None, epilogue=None, extras=(), col0=0, ncols=None, riders=()):
    m, kdim = a.shape
    if w.ndim == 3:
        per = w.shape[2] // tn
        n = ncols or N_SHARD * w.shape[2]
        w_spec = pl.BlockSpec((None, tk, tn), lambda i, j, k: ((j + col0) // per, k, (j + col0) % per))
    else:
        n = ncols or w.shape[1]
        w_spec = pl.BlockSpec((tk, tn), lambda i, j, k: (k, j + col0))
    if outs is None:
        outs = [_out2d(m, n, F32, tm, tn)]
    return _mm(name, NN, a, pl.BlockSpec((tm, tk), lambda i, j, k: (i, k)), w, w_spec,
               (m // tm, n // tn, kdim // tk), (tm, tn), outs, epilogue, extras, riders)


def _mm_bwd_x(name, g, g_spec, w, tm, tj, tc, m, n_contract, outs=None, epilogue=None, extras=(), riders=()):
    if w.ndim == 3:
        per = w.shape[2] // tc
        kdim = w.shape[1]
        w_spec = pl.BlockSpec((None, tj, tc), lambda i, j, n: (n // per, j, n % per))
    else:
        kdim = w.shape[0]
        w_spec = pl.BlockSpec((tj, tc), lambda i, j, n: (j, n))
    if outs is None:
        outs = [_out2d(m, kdim, F32, tm, tj)]
    return _mm(name, NT, g, g_spec, w, w_spec, (m // tm, kdim // tj, n_contract // tc),
               (tm, tj), outs, epilogue, extras, riders)


def _mm_bwd_w(name, a, g, g_spec, n, tk, tn, tm, sharded, riders=()):
    m, kdim = a.shape
    if sharded:
        per = (n // N_SHARD) // tn
        out = (jax.ShapeDtypeStruct((N_SHARD, kdim, n // N_SHARD), F32),
               pl.BlockSpec((None, tk, tn), lambda i, j, mm: (j // per, i, j % per)))
    else:
        out = (jax.ShapeDtypeStruct((kdim, n), F32), pl.BlockSpec((tk, tn), lambda i, j, mm: (i, j)))
    return _mm(name, TN, a, pl.BlockSpec((tm, tk), lambda i, j, mm: (mm, i)), g, g_spec,
               (kdim // tk, n // tn, m // tm), (tk, tn), [out], riders=riders)


def _ew(name, fn, tiles, fulls, outs, sums=(), ts=512, riders=()):
    tiles = [t if isinstance(t, tuple) else (t, pl.BlockSpec((ts, t.shape[1]), lambda i: (i, 0)))
             for t in tiles]
    s_rows = tiles[0][0].shape[-2]
    nt, nf, no = len(tiles), len(fulls), len(outs)

    def body(*refs):
        t_vals = [r[...] for r in refs[:nt]]
        f_vals = [r[...] for r in refs[nt:nt + nf]]
        o_refs, s_refs = refs[nt + nf:nt + nf + no], refs[nt + nf + no:]
        o_vals, s_vals = fn(t_vals, f_vals)
        for r, v in zip(o_refs, o_vals):
            r[...] = v.astype(r.dtype)
        for r, v in zip(s_refs, s_vals):
            part = jnp.sum(v, axis=0, keepdims=True)

            @pl.when(pl.program_id(0) == 0)
            def _():
                r[...] = part

            @pl.when(pl.program_id(0) > 0)
            def _():
                r[...] += part

    full_specs = [pl.BlockSpec(f.shape, lambda i, nd=f.ndim: (0,) * nd) for f in fulls]
    return _carry(
        riders, body, name=name, grid=(s_rows // ts,),
        in_specs=[s for _, s in tiles] + full_specs,
        out_specs=[pl.BlockSpec((ts, c), lambda i: (i, 0)) for c, _ in outs]
        + [pl.BlockSpec((1, c), lambda i: (0, 0)) for c in sums],
        out_shape=[jax.ShapeDtypeStruct((s_rows, c), dt) for c, dt in outs]
        + [jax.ShapeDtypeStruct((1, c), F32) for c in sums],
        semantics=("arbitrary",),
    )(*[t for t, _ in tiles], *fulls)


def _rms_fwd(name, x, g, riders=()):
    def fn(t, f):
        xv = t[0]
        r = lax.rsqrt(jnp.mean(xv * xv, axis=-1, keepdims=True) + EPS)
        return [xv * r * f[0]], []
    out = _ew(name, fn, [x], [g], [(x.shape[1], BF16)], riders=riders)
    return (out[0][0], out[1]) if riders else out[0]


def _rms_bwd(name, dh, x, g, dres, out_dtype=BF16, riders=()):
    def fn(t, f):
        dhv, xv, dr = [v.astype(F32) for v in t]
        r = lax.rsqrt(jnp.mean(xv * xv, axis=-1, keepdims=True) + EPS)
        xhat = xv * r
        u = dhv * f[0]
        dx = r * (u - xhat * jnp.mean(u * xhat, axis=-1, keepdims=True)) + dr
        return [dx], [dhv * xhat]
    c = x.shape[1]
    return _ew(name, fn, [dh, x, dres], [g], [(c, out_dtype)], sums=[c], riders=riders)


def _split3(x):
    x1 = x.astype(BF16)
    r1 = x - x1.astype(F32)
    x2 = r1.astype(BF16)
    x3 = (r1 - x2.astype(F32)).astype(BF16)
    return x1, x2, x3


def _rel_class(cp):
    far = (cp < MAX_REL) | (cp > BAND)
    return jnp.where(far, 2 * MAX_REL, BAND - cp)


def _skew_rows(x, sign):
    row = lax.broadcasted_iota(jnp.int32, x.shape, 0)
    for b in range(CHUNK.bit_length() - 1):
        shift = (1 << b) if sign > 0 else SKEW_W - (1 << b)
        x = jnp.where((row >> b) & 1 == 1, pltpu.roll(x, shift, 1), x)
    return x


def _roll_lanes(x, shift):
    return x if shift % SKEW_W == 0 else pltpu.roll(x, shift % SKEW_W, 1)


N_START = 3


def _bias_expand(rel_bias, riders=()):
    rel = jnp.pad(rel_bias, ((0, 0), (0, REL_PAD - N_REL))).reshape(N_HEADS, 1, REL_PAD)

    def body(rel_ref, o_ref):
        cls = lax.broadcasted_iota(jnp.int32, (REL_PAD, SKEW_W), 0)
        cp = lax.broadcasted_iota(jnp.int32, (REL_PAD, SKEW_W), 1)
        onehot = (cls == _rel_class(cp)).astype(BF16)
        rel8 = jnp.broadcast_to(rel_ref[...], (8, REL_PAD))
        trow = sum(jnp.dot(p, onehot, preferred_element_type=F32) for p in _split3(rel8))[0:1]
        first = _skew_rows(jnp.broadcast_to(trow, (CHUNK, SKEW_W)), +1)
        full = jnp.concatenate([_roll_lanes(first, CHUNK * g) for g in range(SUPER // CHUNK)], axis=0)[:, :BAND]
        qc = lax.broadcasted_iota(jnp.int32, (SUPER, BAND), 0) // CHUNK
        kc = lax.broadcasted_iota(jnp.int32, (SUPER, BAND), 1) // CHUNK
        on_band = (kc >= qc) & (kc <= qc + N_PREV_CHUNKS)
        table = jnp.where(on_band, full, NEG_INF).T
        key = lax.broadcasted_iota(jnp.int32, (BAND, SUPER), 0)
        for t in range(N_START):
            o_ref[t] = jnp.where(key < (N_START - 1 - t) * SUPER, NEG_INF, table)

    return _carry(
        riders, body, name="bias_expand", grid=(N_HEADS,),
        in_specs=[pl.BlockSpec((None, 1, REL_PAD), lambda h: (h, 0, 0))],
        out_specs=pl.BlockSpec((N_START, None, BAND, SUPER), lambda h: (0, h, 0, 0)),
        out_shape=jax.ShapeDtypeStruct((N_START, N_HEADS, BAND, SUPER), F32),
        semantics=("arbitrary",),
    )(rel)


def _bias_reduce(dbias, riders=()):
    def body(d_ref, o_ref):
        x = jnp.concatenate([d_ref[...].T, jnp.zeros((SUPER, SKEW_W - BAND), F32)], axis=1)
        folded = sum(_roll_lanes(x[CHUNK * g:CHUNK * (g + 1)], -CHUNK * g) for g in range(SUPER // CHUNK))
        diag = jnp.sum(_skew_rows(folded, -1), axis=0, keepdims=True)
        cp = lax.broadcasted_iota(jnp.int32, (SKEW_W, REL_PAD), 0)
        cls = lax.broadcasted_iota(jnp.int32, (SKEW_W, REL_PAD), 1)
        onehot = (cls == _rel_class(cp)).astype(BF16)
        diag8 = jnp.broadcast_to(diag, (8, SKEW_W))
        o_ref[...] = sum(jnp.dot(p, onehot, preferred_element_type=F32) for p in _split3(diag8))[0:1]

    out = _carry(
        riders, body, name="bias_reduce", grid=(N_HEADS,),
        in_specs=[pl.BlockSpec((None, BAND, SUPER), lambda h: (h, 0, 0))],
        out_specs=pl.BlockSpec((None, 1, REL_PAD), lambda h: (h, 0, 0)),
        out_shape=jax.ShapeDtypeStruct((N_HEADS, 1, REL_PAD), F32),
        semantics=("arbitrary",),
    )(dbias)
    out, carried = out if riders else (out, None)
    out = out.reshape(N_HEADS, REL_PAD)[:, :N_REL]
    return (out, carried) if riders else out


HEADS_PER_STEP = 8
HEAD_COLS = HEADS_PER_STEP * HEAD_DIM
N_HEAD_GROUPS = N_HEADS // HEADS_PER_STEP


def _unit(x):
    r = lax.rsqrt(jnp.mean(x * x, axis=-1, keepdims=True) + EPS)
    return x * r, r


def _scores_t(qs, kn, bias_t):
    return jnp.concatenate([lax.dot_general(k, qs, NT, preferred_element_type=F32) for k in kn], axis=0) + bias_t


def _bias_spec():
    return pl.BlockSpec((None, HEADS_PER_STEP, BAND, SUPER), lambda hg, i: (jnp.minimum(i, N_START - 1), hg, 0, 0))


def _band_specs(nb, col0, clamp_hi):
    def spec(d):
        def index(hg, i):
            blk = jnp.maximum(i - d, 0)
            if clamp_hi:
                blk = jnp.minimum(blk, nb - 1)
            return (blk, col0 + hg)
        return pl.BlockSpec((SUPER, HEAD_COLS), index)
    return [spec(2), spec(1), spec(0)]


def _head_sums(y):
    cols = y.shape[1]
    same_head = (lax.broadcasted_iota(jnp.int32, (cols, cols), 0) // HEAD_DIM
                 == lax.broadcasted_iota(jnp.int32, (cols, cols), 1) // HEAD_DIM).astype(BF16)
    hi = y.astype(BF16)
    lo = (y - hi.astype(F32)).astype(BF16)
    return (jnp.dot(hi, same_head, preferred_element_type=F32)
            + jnp.dot(lo, same_head, preferred_element_type=F32))


def _head_unit(x):
    r = lax.rsqrt(_head_sums(x * x) * (1.0 / HEAD_DIM) + EPS)
    return x * r, r


def _head(hh):
    return slice(HEAD_DIM * hh, HEAD_DIM * (hh + 1))


def _key_block(j):
    return slice(SUPER * j, SUPER * (j + 1))


LSE_ROWS = 8


def _attn_fwd(qkn, v, bias, riders=()):
    s_len = qkn.shape[0]
    nb = s_len // SUPER

    def body(q_ref, k0, k1, k2, v0, v1, v2, b_ref, o_ref, lse_ref):
        outs = []

        def probabilities(hh):
            sl = _head(hh)
            s = _scores_t(q_ref[:, sl], [k0[:, sl], k1[:, sl], k2[:, sl]], b_ref[hh])
            m = jnp.max(s, axis=0, keepdims=True)
            e = jnp.exp(s - m)
            l = jnp.sum(e, axis=0, keepdims=True)
            lse_ref[hh:hh + 1, :] = m + jnp.log(l)
            return (e * (1.0 / l)).astype(BF16), sl

        def weighted_values(p, sl):
            outs.append(sum(lax.dot_general(p[_key_block(j), :], vj[:, sl], TN, preferred_element_type=F32)
                            for j, vj in enumerate((v0, v1, v2))))

        ready = probabilities(0)
        for hh in range(1, HEADS_PER_STEP):
            following = probabilities(hh)
            weighted_values(*ready)
            ready = following
        weighted_values(*ready)
        o_ref[...] = jnp.concatenate(outs, axis=1).astype(o_ref.dtype)

    return _carry(
        riders, body, name="attn_fwd", grid=(N_HEAD_GROUPS, nb),
        in_specs=[pl.BlockSpec((SUPER, HEAD_COLS), lambda hg, i: (i, hg))]
        + _band_specs(nb, N_HEAD_GROUPS, False) + _band_specs(nb, 0, False) + [_bias_spec()],
        out_specs=[pl.BlockSpec((SUPER, HEAD_COLS), lambda hg, i: (i, hg)),
                   pl.BlockSpec((None, LSE_ROWS, SUPER), lambda hg, i: (hg, 0, i))],
        out_shape=[jax.ShapeDtypeStruct((s_len, D_MODEL), BF16),
                   jax.ShapeDtypeStruct((N_HEAD_GROUPS, LSE_ROWS, s_len), F32)],
        semantics=("parallel", "arbitrary"),
    )(qkn, qkn, qkn, qkn, v, v, v, bias)


def _attn_bwd(qkn, v, out, d_out, bias, lse, riders=()):
    s_len = qkn.shape[0]
    nb = s_len // SUPER

    def body(q_ref, k0, k1, k2, v0, v1, v2, o_ref, do_ref, b_ref, lse_ref, dp_ref, db_ref, aq_ref, ak_ref, av_ref):
        i = pl.program_id(1)

        @pl.when(i == 0)
        def _():
            aq_ref[...] = jnp.zeros_like(aq_ref)
            ak_ref[...] = jnp.zeros_like(ak_ref)
            av_ref[...] = jnp.zeros_like(av_ref)
            db_ref[...] = jnp.zeros_like(db_ref)

        @pl.when(i < nb)
        def _():
            dq, dk, dv = [], [[], [], []], [[], [], []]
            ones = jnp.ones((8, HEAD_DIM), BF16)

            def softmax_grad(hh):
                sl = _head(hh)
                qs, do = q_ref[:, sl], do_ref[:, sl]
                kn = [k0[:, sl], k1[:, sl], k2[:, sl]]
                prod = do.astype(F32) * o_ref[:, sl].astype(F32)
                hi = prod.astype(BF16)
                lo = (prod - hi.astype(F32)).astype(BF16)
                delta = (lax.dot_general(ones, hi, NT, preferred_element_type=F32)
                         + lax.dot_general(ones, lo, NT, preferred_element_type=F32))[0:1]
                lse_row = lse_ref[hh:hh + 1, :]
                pb, dsb = [], []
                for j, (kj, vj) in enumerate(zip(kn, (v0, v1, v2))):
                    rows = _key_block(j)
                    p = jnp.exp(lax.dot_general(kj, qs, NT, preferred_element_type=F32) + b_ref[hh, rows, :] - lse_row)
                    ds = p * (lax.dot_general(vj[:, sl], do, NT, preferred_element_type=F32) - delta)
                    db_ref[hh, rows, :] += ds
                    pb.append(p.astype(BF16))
                    dsb.append(ds.astype(BF16))
                return pb, dsb, qs, do, kn

            def operand_grads(pb, dsb, qs, do, kn):
                dq.append(sum(lax.dot_general(dsb[j], kn[j], TN, preferred_element_type=F32) for j in range(3)))
                for j in range(3):
                    dv[j].append(jnp.dot(pb[j], do, preferred_element_type=F32))
                    dk[j].append(jnp.dot(dsb[j], qs, preferred_element_type=F32))

            ready = softmax_grad(0)
            for hh in range(1, HEADS_PER_STEP):
                following = softmax_grad(hh)
                operand_grads(*ready)
                ready = following
            operand_grads(*ready)
            aq_ref[i % 3] = jnp.concatenate(dq, axis=1)
            for j in range(3):
                slot = (i + 1 + j) % 3
                if j < 2:
                    ak_ref[slot] += jnp.concatenate(dk[j], axis=1)
                    av_ref[slot] += jnp.concatenate(dv[j], axis=1)
                else:
                    ak_ref[slot] = jnp.concatenate(dk[j], axis=1)
                    av_ref[slot] = jnp.concatenate(dv[j], axis=1)

        slot = (i + 1) % 3
        dp_ref[0] = aq_ref[slot].astype(dp_ref.dtype)
        dp_ref[1] = ak_ref[slot].astype(dp_ref.dtype)
        dp_ref[2] = av_ref[slot].astype(dp_ref.dtype)

    def qrow(hg, i):
        return (jnp.minimum(i, nb - 1), hg)

    return _carry(
        riders, body, name="attn_bwd", grid=(N_HEAD_GROUPS, nb + 2),
        in_specs=[pl.BlockSpec((SUPER, HEAD_COLS), qrow)]
        + _band_specs(nb, N_HEAD_GROUPS, True) + _band_specs(nb, 0, True)
        + [pl.BlockSpec((SUPER, HEAD_COLS), qrow), pl.BlockSpec((SUPER, HEAD_COLS), qrow), _bias_spec(),
           pl.BlockSpec((None, LSE_ROWS, SUPER), lambda hg, i: (hg, 0, jnp.minimum(i, nb - 1)))],
        out_specs=[pl.BlockSpec((3, SUPER, HEAD_COLS), lambda hg, i: (0, jnp.maximum(i - 2, 0), hg)),
                   pl.BlockSpec((HEADS_PER_STEP, BAND, SUPER), lambda hg, i: (hg, 0, 0))],
        out_shape=[jax.ShapeDtypeStruct((6, s_len, D_MODEL), BF16),
                   jax.ShapeDtypeStruct((N_HEADS, BAND, SUPER), F32)],
        scratch_shapes=[pltpu.VMEM((3, SUPER, HEAD_COLS), F32)] * 3,
        semantics=("parallel", "arbitrary"),
    )(qkn, qkn, qkn, qkn, v, v, v, out, d_out, bias, lse)


def _qk_norm_bwd(dproj6, qk_raw, gq, gk):
    s_len = qk_raw.shape[0]
    ts = min(1024, s_len)

    nsteps = s_len // ts
    half = D_MODEL // 2

    def body(d_ref, raw_ref, gq_ref, gk_ref, o_ref, dgq_ref, dgk_ref, acc_ref):
        step = pl.program_id(0)

        @pl.when(step == 0)
        def _():
            acc_ref[...] = jnp.zeros_like(acc_ref)

        for piece, (g_ref, scale) in enumerate(((gq_ref, QK_SCALE), (gk_ref, 1.0))):
            for c0 in (0, half):
                xhat, r = _head_unit(raw_ref[:, piece * D_MODEL + c0:piece * D_MODEL + c0 + half].astype(F32))
                dn = d_ref[piece, :, c0:c0 + half].astype(F32) * scale
                u = dn * g_ref[...]
                dx = r * (u - xhat * (_head_sums(u * xhat) * (1.0 / HEAD_DIM)))
                o_ref[piece, :, c0:c0 + half] = dx.astype(o_ref.dtype)
                acc_ref[piece:piece + 1, c0:c0 + half] += jnp.sum(dn * xhat, axis=0, keepdims=True)

        @pl.when(step == nsteps - 1)
        def _():
            lane = lax.broadcasted_iota(jnp.int32, (D_MODEL, LANE), 0) % HEAD_DIM
            fold = (lane == lax.broadcasted_iota(jnp.int32, (D_MODEL, LANE), 1)).astype(BF16)
            tot = sum(jnp.dot(p, fold, preferred_element_type=F32) for p in _split3(acc_ref[...]))
            dgq_ref[...] = tot[0:1, :HEAD_DIM]
            dgk_ref[...] = tot[1:2, :HEAD_DIM]

    gain = pl.BlockSpec((1, half), lambda i: (0, 0))
    small = pl.BlockSpec((1, HEAD_DIM), lambda i: (0, 0))
    per_head = lambda g: jnp.tile(g, (1, half // HEAD_DIM))
    return pl.pallas_call(
        body, name="qk_norm_bwd", grid=(nsteps,),
        in_specs=[pl.BlockSpec((2, ts, D_MODEL), lambda i: (0, i, 0)),
                  pl.BlockSpec((ts, 2 * D_MODEL), lambda i: (i, 0)), gain, gain],
        out_specs=[pl.BlockSpec((2, ts, D_MODEL), lambda i: (0, i, 0)), small, small],
        out_shape=[jax.ShapeDtypeStruct(dproj6.shape, dproj6.dtype),
                   jax.ShapeDtypeStruct((1, HEAD_DIM), F32), jax.ShapeDtypeStruct((1, HEAD_DIM), F32)],
        scratch_shapes=[pltpu.VMEM((8, D_MODEL), F32)],
        input_output_aliases={0: 0},
        compiler_params=_params("arbitrary"),
    )(dproj6, qk_raw, per_head(gq), per_head(gk))


CONV_ROWS = 512
HALO = 16


def _rows_with_halo(ref, r0, n, front, s_len):
    zeros = jnp.zeros((HALO, ref.shape[1]), F32)
    if front:
        return (jnp.concatenate([zeros, ref[0:n, :].astype(F32)], axis=0) if r0 == 0
                else ref[r0 - HALO:r0 + n, :].astype(F32))
    return (jnp.concatenate([ref[r0:r0 + n, :].astype(F32), zeros], axis=0) if r0 + n == s_len
            else ref[r0:r0 + n + HALO, :].astype(F32))


def _earlier(ext, k):
    return pltpu.roll(ext, k, 0)[HALO:]


def _later(ext, k):
    n = ext.shape[0]
    return pltpu.roll(ext, n - k, 0)[:n - HALO]


def _conv_cols(col0):
    return lambda s_len: pl.BlockSpec((s_len, LANE), lambda j: (0, col0 + j))


def _conv_fwd(proj, conv_w, conv_b, riders=()):
    s_len = proj.shape[0]

    def body(bg_ref, cg_ref, xc_ref, w_ref, b_ref, o_ref):
        w = [w_ref[t:t + 1, :] for t in range(3)]
        for r0 in range(0, s_len, CONV_ROWS):
            u = _rows_with_halo(cg_ref, r0, CONV_ROWS, True, s_len) * \
                _rows_with_halo(xc_ref, r0, CONV_ROWS, True, s_len)
            conv = b_ref[...] + w[0] * _earlier(u, 2) + w[1] * _earlier(u, 1) + w[2] * u[HALO:]
            o_ref[r0:r0 + CONV_ROWS, :] = (bg_ref[r0:r0 + CONV_ROWS, :].astype(F32) * conv).astype(o_ref.dtype)

    return _carry(
        riders, body, name="conv_fwd", grid=(D_MODEL // LANE,),
        in_specs=[_conv_cols(0)(s_len), _conv_cols(8)(s_len), _conv_cols(16)(s_len),
                  pl.BlockSpec((3, LANE), lambda j: (0, j)), pl.BlockSpec((1, LANE), lambda j: (0, j))],
        out_specs=pl.BlockSpec((s_len, LANE), lambda j: (0, j)),
        out_shape=jax.ShapeDtypeStruct((s_len, D_MODEL), BF16),
        semantics=("parallel",),
    )(proj, proj, proj, conv_w, conv_b)


def _conv_bwd(dproj6, dy, proj, conv_w, conv_b, riders=()):
    s_len = proj.shape[0]

    def body(dy_ref, bg_ref, cg_ref, xc_ref, w_ref, b_ref, _, dp_ref, dw_ref):
        w = [w_ref[t:t + 1, :] for t in range(3)]
        acc = [jnp.zeros((1, LANE), F32) for _ in range(4)]
        for r0 in range(0, s_len, CONV_ROWS):
            rows = slice(r0, r0 + CONV_ROWS)
            u = _rows_with_halo(cg_ref, r0, CONV_ROWS, True, s_len) * \
                _rows_with_halo(xc_ref, r0, CONV_ROWS, True, s_len)
            u2, u1, u0 = _earlier(u, 2), _earlier(u, 1), u[HALO:]
            conv = b_ref[...] + w[0] * u2 + w[1] * u1 + w[2] * u0
            dp_ref[0, rows, :] = (dy_ref[rows, :].astype(F32) * conv).astype(dp_ref.dtype)
            dconv_ext = _rows_with_halo(dy_ref, r0, CONV_ROWS, False, s_len) * \
                _rows_with_halo(bg_ref, r0, CONV_ROWS, False, s_len)
            dconv = dconv_ext[:CONV_ROWS]
            for t, term in enumerate([dconv * u2, dconv * u1, dconv * u0, dconv]):
                acc[t] = acc[t] + jnp.sum(term, axis=0, keepdims=True)
            du = w[2] * dconv + w[1] * _later(dconv_ext, 1) + w[0] * _later(dconv_ext, 2)
            dp_ref[1, rows, :] = (du * xc_ref[rows, :].astype(F32)).astype(dp_ref.dtype)
            dp_ref[2, rows, :] = (du * cg_ref[rows, :].astype(F32)).astype(dp_ref.dtype)
        dw_ref[...] = jnp.zeros_like(dw_ref)
        for t in range(4):
            dw_ref[t:t + 1, :] = acc[t]

    return _carry(
        riders, body, name="conv_bwd", grid=(D_MODEL // LANE,),
        in_specs=[pl.BlockSpec((s_len, LANE), lambda j: (0, j)),
                  _conv_cols(0)(s_len), _conv_cols(8)(s_len), _conv_cols(16)(s_len),
                  pl.BlockSpec((3, LANE), lambda j: (0, j)), pl.BlockSpec((1, LANE), lambda j: (0, j)),
                  pl.BlockSpec(memory_space=pl.ANY)],
        out_specs=[pl.BlockSpec((3, s_len, LANE), lambda j: (1, 0, j)),
                   pl.BlockSpec((8, LANE), lambda j: (0, j))],
        out_shape=[jax.ShapeDtypeStruct(dproj6.shape, dproj6.dtype),
                   jax.ShapeDtypeStruct((8, D_MODEL), F32)],
        input_output_aliases={6: 0},
        semantics=("parallel",),
    )(dy, proj, proj, proj, conv_w, conv_b, dproj6)


class _NoComm:
    def __init__(self, early, late):
        self.early, self.late = early, late

    def riders(self, at):
        return ()

    def done(self, at, carried):
        pass

    def early_weights(self):
        return self.early[0], self.early[2]

    def gate_weight(self):
        return self.early[1]

    def late_weights(self):
        return self.late

    def grads_ready(self, group, grads):
        pass


def _d_input(name, terms, riders=()):
    s_len, kdim = terms[0][0].shape[1], terms[0][1].shape[1]
    tm, chunk = 512, 512
    n = len(terms)

    def body(*refs):
        o_ref = refs[-1]
        acc = jnp.zeros(o_ref.shape, F32)
        for p_ref, w_ref in zip(refs[:n], refs[n:2 * n]):
            n_piece, _, width = p_ref.shape
            per_shard = w_ref.shape[2]
            for c0 in range(0, n_piece * width, chunk):
                acc = acc + lax.dot_general(p_ref[c0 // width, :, c0 % width:c0 % width + chunk],
                                            w_ref[c0 // per_shard, :, c0 % per_shard:c0 % per_shard + chunk],
                                            NT, preferred_element_type=F32)
        o_ref[...] = acc.astype(o_ref.dtype)

    tile = pl.BlockSpec((tm, kdim), lambda i: (i, 0))
    return _carry(
        riders, body, name=name, grid=(s_len // tm,),
        in_specs=[pl.BlockSpec((p.shape[0], tm, p.shape[2]), lambda i: (0, i, 0)) for p, _ in terms]
        + [pl.BlockSpec(w.shape, lambda i: (0, 0, 0), pipeline_mode=pl.Buffered(1)) for _, w in terms],
        out_specs=tile, out_shape=jax.ShapeDtypeStruct((s_len, kdim), BF16),
        semantics=("parallel",),
    )(*[p for p, _ in terms], *[w for _, w in terms])


def _local_grads(x, target, norm1_g, q_norm_g, k_norm_g, rel_bias, conv_b, b_gate, norm2_g, comm):
    s_len = x.shape[0]
    tm = min(1024, s_len)
    row = lambda v: v.reshape(1, -1)

    def carrying(at, fn, *args, **kw):
        riders = comm.riders(at)
        out = fn(*args, riders=riders, **kw)
        if riders:
            out, carried = out
            comm.done(at, carried)
        return out

    bias = carrying("bias_expand", _bias_expand, rel_bias)
    h = carrying("norm1", _rms_fwd, "norm1", x, row(norm1_g))
    w_in3, conv_w = comm.early_weights()
    gq, gk = row(q_norm_g), row(k_norm_g)

    def qk_epi(acc, e, o):
        o[0][...] = acc.astype(BF16)
        gain = jnp.where(pl.program_id(1) < 2, e[0] * QK_SCALE, e[1])
        o[1][...] = (_head_unit(acc)[0] * gain).astype(BF16)
    small = pl.BlockSpec((1, 512), lambda i, j, k: (0, 0))
    per_head = lambda g: jnp.tile(g, (1, 512 // HEAD_DIM))
    qk_raw, qkn = carrying("proj_qk", _mm_fwd, "proj_qk", h, w_in3, tm, 512, D_MODEL, ncols=2 * D_MODEL,
                           epilogue=qk_epi, outs=[_out2d(s_len, 2 * D_MODEL, BF16, tm, 512)] * 2,
                           extras=[(per_head(gq), small), (per_head(gk), small)])
    v = carrying("proj_v", _mm_fwd, "proj_v", h, w_in3, tm, 512, D_MODEL, col0=4, ncols=D_MODEL,
                 outs=[_out2d(s_len, D_MODEL, BF16, tm, 512)])
    w_gate3 = comm.gate_weight()
    conv_in = _mm_fwd("proj_conv", h, w_in3, tm, 1536, D_MODEL, col0=2, ncols=3 * D_MODEL,
                      outs=[_out2d(s_len, 3 * D_MODEL, BF16, tm, 1536)])

    def gate_epi(acc, e, o):
        o[0][...] = jax.nn.sigmoid(acc + e[0]).astype(BF16)
    gates = _mm_fwd("gates", h, w_gate3, tm, 512, D_MODEL, epilogue=gate_epi,
                    outs=[_out2d(s_len, 2 * D_MODEL, BF16, tm, 512)],
                    extras=[(row(b_gate), pl.BlockSpec((1, 512), lambda i, j, k: (0, j)))])

    attn, lse = carrying("attn_fwd", _attn_fwd, qkn, v, bias)
    yconv = carrying("conv_fwd", _conv_fwd, conv_in, conv_w, row(conv_b))
    w_ap, w_cp, w_out, w_up3, w_down = comm.late_weights()
    tw = 1024
    ya = _mm_fwd("attn_proj", attn, w_ap, tm, tw, D_MODEL, outs=[_out2d(s_len, D_MODEL, BF16, tm, tw)])

    def merge_epi(acc, e, o):
        ya_v, ga, gc = [t.astype(F32) for t in e]
        o[0][...] = acc.astype(BF16)
        o[1][...] = (ga * ya_v + gc * acc).astype(BF16)
    gate_a, gate_c = (gates, _tile_spec(tm, tw, 0)), (gates, _tile_spec(tm, tw, 1))
    yc, merged = _mm_fwd("conv_proj", yconv, w_cp, tm, tw, D_MODEL, epilogue=merge_epi,
                         outs=[_out2d(s_len, D_MODEL, BF16, tm, tw), _out2d(s_len, D_MODEL, BF16, tm, tw)],
                         extras=[(ya, _tile_spec(tm, tw)), gate_a, gate_c])

    def res_epi(acc, e, o):
        o[0][...] = e[0] + acc
    x1 = _mm_fwd("out_proj", merged, w_out, tm, tw, D_MODEL, epilogue=res_epi,
                 extras=[(x, _tile_spec(tm, tw))])
    h2 = _rms_fwd("norm2", x1, row(norm2_g))

    def up_epi(acc, e, o):
        o[0][...] = jnp.square(jnp.maximum(acc, 0.0)).astype(BF16)
    act = _mm_fwd("mlp_up", h2, w_up3, tm, tw, D_MODEL, epilogue=up_epi, outs=[_out2d(s_len, D_FF, BF16, tm, tw)])

    def loss_epi(acc, e, o):
        err = e[0] + acc - e[1]
        o[0][...] = (err * (1.0 / D_MODEL)).astype(BF16)
        sq = err * err
        part = sq[:, 0:LANE]
        for c0 in range(LANE, D_MODEL, LANE):
            part = part + sq[:, c0:c0 + LANE]
        o[1][...] = jnp.sum(part.reshape(tl // 8, 8, LANE), axis=0)
    tl = 512
    dy_b, loss_part = _mm_fwd(
        "mlp_down", act, w_down, tl, D_MODEL, D_FF, epilogue=loss_epi,
        outs=[_out2d(s_len, D_MODEL, BF16, tl, D_MODEL),
              (jax.ShapeDtypeStruct((8 * (s_len // tl), LANE), F32), pl.BlockSpec((8, LANE), lambda i, j, k: (i, 0)))],
        extras=[(x1, _tile_spec(tl, D_MODEL)), (target, _tile_spec(tl, D_MODEL))])

    def dup_epi(acc, e, o):
        o[0][...] = (acc * (2.0 * jnp.sqrt(e[0].astype(F32)))).astype(BF16)
    full = lambda cols: pl.BlockSpec((tm, cols), lambda i, j, n: (i, n))
    tokens = lambda cols: pl.BlockSpec((s_len, cols), lambda i, j, m: (m, j))
    dup = _mm_bwd_x("d_act", dy_b, full(D_MODEL), w_down, tm, tw, D_MODEL, s_len, D_MODEL, epilogue=dup_epi,
                    outs=[_out2d(s_len, D_FF, BF16, tm, tw)], extras=[(act, _tile_spec(tm, tw))])
    g_down = _mm_bwd_w("g_down", act, dy_b, tokens(D_MODEL), D_MODEL, 512, D_MODEL, s_len, False)
    g_up = _mm_bwd_w("g_up", h2, dup, tokens(512), D_FF, D_MODEL, 512, s_len, True)
    comm.grads_ready("mlp", dict(w_down=g_down, w_up=g_up))
    dh2 = carrying("d_h2", _d_input, "d_h2", [(dup.reshape(1, s_len, D_FF), w_up3)])
    dx1_b, dg2 = _rms_bwd("norm2_bwd", dh2, x1, row(norm2_g), dy_b)

    def dmerge_epi(acc, e, o):
        ya_v, yc_v, ga, gc = [t.astype(F32) for t in e]
        o[0][...] = (acc * ga).astype(BF16)
        o[1][...] = (acc * gc).astype(BF16)
        o[2][0] = (acc * ya_v * ga * (1.0 - ga)).astype(BF16)
        o[2][1] = (acc * yc_v * gc * (1.0 - gc)).astype(BF16)
    dya, dyc, dgp2 = _mm_bwd_x(
        "d_merged", dx1_b, full(D_MODEL), w_out, tm, tw, D_MODEL, s_len, D_MODEL, epilogue=dmerge_epi,
        outs=[_out2d(s_len, D_MODEL, BF16, tm, tw), _out2d(s_len, D_MODEL, BF16, tm, tw),
              (jax.ShapeDtypeStruct((2, s_len, D_MODEL), BF16), pl.BlockSpec((2, tm, tw), lambda i, j, n: (0, i, j)))],
        extras=[(ya, _tile_spec(tm, tw)), (yc, _tile_spec(tm, tw)), gate_a, gate_c])
    g_out = _mm_bwd_w("g_out", merged, dx1_b, tokens(512), D_MODEL, D_MODEL, 512, s_len, False)
    d_attn = _mm_bwd_x("d_attn", dya, full(D_MODEL), w_ap, tm, tw, D_MODEL, s_len, D_MODEL,
                       outs=[_out2d(s_len, D_MODEL, BF16, tm, tw)])
    g_ap = _mm_bwd_w("g_attn_proj", attn, dya, tokens(512), D_MODEL, D_MODEL, 512, s_len, False)
    d_yconv = _mm_bwd_x("d_yconv", dyc, full(D_MODEL), w_cp, tm, tw, D_MODEL, s_len, D_MODEL,
                        outs=[_out2d(s_len, D_MODEL, BF16, tm, tw)])
    g_cp = _mm_bwd_w("g_conv_proj", yconv, dyc, tokens(512), D_MODEL, D_MODEL, 512, s_len, False)
    comm.grads_ready("proj", dict(w_out=g_out, w_attn_proj=g_ap, w_conv_proj=g_cp))

    dproj6, dbias = carrying("attn_bwd", _attn_bwd, qkn, v, attn, d_attn, bias, lse)
    dproj6, dgq, dgk = _qk_norm_bwd(dproj6, qk_raw, gq, gk)
    dproj6, dconv_wb = carrying("conv_bwd", _conv_bwd, dproj6, d_yconv, conv_in, conv_w, row(conv_b))

    piece = lambda width: (lambda blk: (blk * width) // D_MODEL, lambda blk: (blk * width % D_MODEL) // width)
    pc, cb = piece(512)
    pieces = pl.BlockSpec((None, s_len, 512), lambda i, j, m: (pc(j), m, cb(j)))
    g_in = carrying("g_in", _mm_bwd_w, "g_in", h, dproj6, pieces, 6 * D_MODEL, D_MODEL, 512, s_len, True)
    g_gate = carrying("g_gate", _mm_bwd_w, "g_gate", h, dgp2, pieces, 2 * D_MODEL, D_MODEL, 512, s_len, True)
    comm.grads_ready("in", dict(w_in=g_in, w_gate=g_gate))
    d_rel = carrying("bias_reduce", _bias_reduce, dbias)
    dh = carrying("d_h", _d_input, "d_h", [(dproj6, w_in3), (dgp2, w_gate3)])
    grad_x, dg1 = carrying("norm1_bwd", _rms_bwd, "norm1_bwd", dh, x, row(norm1_g), dx1_b, out_dtype=F32)

    def bsum(t, f):
        return [], [t[0].astype(F32), t[1].astype(F32)]
    ts = 512
    db_a, db_c = carrying("b_gate_sum", _ew, "b_gate_sum", bsum,
                          [(dgp2, pl.BlockSpec((None, ts, D_MODEL), lambda i: (0, i, 0))),
                           (dgp2, pl.BlockSpec((None, ts, D_MODEL), lambda i: (1, i, 0)))],
                          [], [], sums=[D_MODEL, D_MODEL], ts=ts)

    big = dict(w_in=g_in, w_attn_proj=g_ap, w_conv_proj=g_cp, w_gate=g_gate, w_out=g_out,
               w_up=g_up, w_down=g_down)
    small = dict(norm1_g=dg1, norm2_g=dg2, conv_wb=dconv_wb, b_gate=(db_a, db_c),
                 q_norm_g=dgq, k_norm_g=dgk, rel_bias=d_rel)
    return loss_part, grad_x, big, small


def _finish_loss(loss_part):
    def body(l_ref, lo_ref):
        total = jnp.sum(jnp.sum(l_ref[...], axis=0, keepdims=True), axis=1, keepdims=True)
        lo_ref[...] = jnp.broadcast_to(total * (0.5 / D_MODEL), lo_ref.shape)

    return pl.pallas_call(body, name="finish_loss", out_shape=jax.ShapeDtypeStruct((8, LANE), F32))(loss_part)


HBM_SPEC = pl.BlockSpec(memory_space=pl.ANY)


def _place():
    return lax.axis_index("x"), lax.axis_index("y"), lax.axis_index("c")


def _other_chips(x, y):
    return [(1 - x, y), (x, 1 - y), (1 - x, 1 - y)]


def _cast_into_slot(name, where, w):
    r, cols = w.shape
    ts = 256

    def body(w_ref, x_ref, o_ref):
        o_ref[...] = x_ref[...].astype(o_ref.dtype)

    return pl.pallas_call(
        body, name=name,
        grid_spec=pltpu.PrefetchScalarGridSpec(
            num_scalar_prefetch=1, grid=(r // ts,),
            in_specs=[pl.BlockSpec((ts, cols), lambda i, w: (i, 0))],
            out_specs=pl.BlockSpec((None, ts, cols), lambda i, w: (w[0], i, 0))),
        out_shape=jax.ShapeDtypeStruct((N_SHARD, r, cols), BF16),
        compiler_params=_params("parallel"),
    )(where, w)


def _remote(src, dst, send, recv, k, to):
    return pltpu.make_async_remote_copy(src_ref=src, dst_ref=dst, send_sem=send.at[k], recv_sem=recv.at[k],
                                        device_id=to, device_id_type=MESH)


def _gather_riders(slots):
    n = len(slots)

    def half(refs, w, shard, which):
        hr = slots[w].shape[1] // 2
        return refs[w].at[shard, pl.ds(which * hr, hr)]

    def each(fn):
        x, y, c = _place()
        for w in range(n):
            for j, chip in enumerate(_other_chips(x, y)):
                fn(w, 3 * w + j, 2 * x + y, 2 * chip[0] + chip[1], c, (*chip, c), (x, y, 1 - c))

    def chips_start(_, refs, send, recv):
        each(lambda w, k, me, them, c, peer, sib: _remote(half(refs, w, me, c), half(refs, w, me, c), send, recv, k, peer).start())

    def chips_finish(_, refs, send, recv):
        each(lambda w, k, me, them, c, peer, sib: _remote(half(refs, w, them, c), half(refs, w, them, c), send, recv, k, peer).wait_recv())
        each(lambda w, k, me, them, c, peer, sib: _remote(half(refs, w, me, c), half(refs, w, me, c), send, recv, k, peer).wait_send())

    def sibling_start(_, refs, send, recv):
        each(lambda w, k, me, them, c, peer, sib: _remote(half(refs, w, them, c), half(refs, w, them, c), send, recv, k, sib).start())

    def sibling_finish(_, refs, send, recv):
        each(lambda w, k, me, them, c, peer, sib: _remote(half(refs, w, them, 1 - c), half(refs, w, them, 1 - c), send, recv, k, sib).wait_recv())
        each(lambda w, k, me, them, c, peer, sib: _remote(half(refs, w, them, c), half(refs, w, them, c), send, recv, k, sib).wait_send())

    return (lambda s: _Rider([], s, 3 * n, chips_start, chips_finish),
            lambda s: _Rider([], s, 3 * n, sibling_start, sibling_finish))


def _pair_exchange_rider(grads, landing):
    n = len(grads)

    def copies(srcs, dsts, send, recv):
        x, y, c = _place()
        out = []
        for w in range(n):
            hr = grads[w].shape[1] // 2
            out.append(_remote(srcs[w].at[:, pl.ds((1 - c) * hr, hr)], dsts[w], send, recv, w, (x, y, 1 - c)))
        return out

    def start(srcs, dsts, send, recv):
        for cp in copies(srcs, dsts, send, recv):
            cp.start()

    def finish(srcs, dsts, send, recv):
        for cp in copies(srcs, dsts, send, recv):
            cp.wait()

    return _Rider(grads, landing, n, start, finish)


def _row_tile(hr):
    return min(hr, 256)


def _pair_add(name, where, grad, got):
    _, hr, cols = got.shape
    tr = _row_tile(hr)
    nblk = hr // tr

    def body(w_ref, g_ref, r_ref, o_ref):
        o_ref[...] = (g_ref[...] + r_ref[...]).astype(o_ref.dtype)

    return pl.pallas_call(
        body, name=name,
        grid_spec=pltpu.PrefetchScalarGridSpec(
            num_scalar_prefetch=1, grid=(N_SHARD, nblk),
            in_specs=[pl.BlockSpec((None, tr, cols), lambda s, i, w: (s, w[1] * nblk + i, 0)),
                      pl.BlockSpec((None, tr, cols), lambda s, i, w: (s, i, 0))],
            out_specs=pl.BlockSpec((None, tr, cols), lambda s, i, w: (s, i, 0))),
        out_shape=jax.ShapeDtypeStruct(got.shape, BF16),
        compiler_params=_params("parallel", "parallel"),
    )(where, grad, got)


def _chip_exchange_rider(partials, landing):
    n = len(partials)

    def copies(srcs, dsts, send, recv):
        x, y, c = _place()
        return [_remote(srcs[w].at[2 * chip[0] + chip[1]], dsts[w].at[j], send, recv, 3 * w + j, (*chip, c))
                for w in range(n) for j, chip in enumerate(_other_chips(x, y))]

    def start(srcs, dsts, send, recv):
        for cp in copies(srcs, dsts, send, recv):
            cp.start()

    def finish(srcs, dsts, send, recv):
        for cp in copies(srcs, dsts, send, recv):
            cp.wait()

    return _Rider(partials, landing, 3 * n, start, finish)


def _final_add(name, where, grad, got, arrived):
    _, hr, cols = got.shape
    tr = _row_tile(hr)
    nblk = hr // tr

    def body(w_ref, g_ref, r_ref, a_ref, o_ref):
        acc = g_ref[...] + r_ref[...]
        for j in range(3):
            acc = acc + a_ref[j].astype(F32)
        o_ref[...] = acc

    return pl.pallas_call(
        body, name=name,
        grid_spec=pltpu.PrefetchScalarGridSpec(
            num_scalar_prefetch=1, grid=(nblk,),
            in_specs=[pl.BlockSpec((None, tr, cols), lambda i, w: (w[0], w[1] * nblk + i, 0)),
                      pl.BlockSpec((None, tr, cols), lambda i, w: (w[0], i, 0)),
                      pl.BlockSpec((3, tr, cols), lambda i, w: (0, i, 0))],
            out_specs=pl.BlockSpec((tr, cols), lambda i, w: (w[1] * nblk + i, 0))),
        out_shape=jax.ShapeDtypeStruct((2 * hr, cols), F32),
        compiler_params=_params("parallel"),
    )(where, grad, got, arrived)


def _pair_share_rider(shards):
    n = len(shards)

    def half(refs, w, which):
        hr = shards[w].shape[0] // 2
        return refs[w].at[pl.ds(which * hr, hr)]

    def start(_, refs, send, recv):
        x, y, c = _place()
        for w in range(n):
            _remote(half(refs, w, c), half(refs, w, c), send, recv, w, (x, y, 1 - c)).start()

    def finish(_, refs, send, recv):
        x, y, c = _place()
        for w in range(n):
            _remote(half(refs, w, 1 - c), half(refs, w, 1 - c), send, recv, w, (x, y, 1 - c)).wait_recv()
        for w in range(n):
            _remote(half(refs, w, c), half(refs, w, c), send, recv, w, (x, y, 1 - c)).wait_send()

    return _Rider([], shards, n, start, finish)


class _Exchange:
    PLAN = {"bias_expand": [("early", "gather")], "norm1": [("early", "forward")],
            "proj_qk": [("gate", "gather")], "proj_v": [("gate", "forward")],
            "attn_fwd": [("late", "gather")], "conv_fwd": [("late", "forward")],
            "d_h2": [("mlp", "pair")], "attn_bwd": [("mlp", "chips"), ("proj", "pair")], "conv_bwd": [("mlp", "share")],
            "g_in": [("proj", "chips")], "g_gate": [("proj", "share")],
            "bias_reduce": [("in", "pair")], "d_h": [("in", "chips")], "b_gate_sum": [("in", "share")]}

    def __init__(self, where, early_slots, gate_slots, late_slots):
        self.where = where
        self.slots = dict(early=early_slots, gate=gate_slots, late=late_slots)
        self.stage = {g: dict(zip(("gather", "forward"), _gather_riders(s))) for g, s in self.slots.items()}
        self.groups, self.reduced, self.pending = {}, {}, []

    def early_weights(self):
        w_in3, small = self.slots["early"]
        conv_w = small[:, :3, :].transpose(1, 0, 2).reshape(3, N_SHARD * small.shape[2])
        return w_in3, conv_w

    def gate_weight(self):
        return self.slots["gate"][0]

    def late_weights(self):
        rows = lambda a: a.reshape(a.shape[0] * a.shape[1], a.shape[2])
        w_ap, w_cp, w_out, w_up3, w_down = self.slots["late"]
        return rows(w_ap), rows(w_cp), rows(w_out), w_up3, rows(w_down)

    def grads_ready(self, group, grads):
        names = list(grads)
        g4 = [g if g.ndim == 3 else g.reshape(N_SHARD, -1, g.shape[1]) for g in grads.values()]
        self.groups[group] = dict(names=names, g4=g4)

    def riders(self, at):
        self.pending = self.PLAN.get(at, [])
        out = []
        for group, stage in self.pending:
            if group in self.slots:
                out.append(self.stage[group][stage](self.slots[group]))
                continue
            st = self.groups[group]
            if stage == "pair":
                landing = [lax.empty((N_SHARD, g.shape[1] // 2, g.shape[2]), F32) for g in st["g4"]]
                out.append(_pair_exchange_rider(st["g4"], landing))
            elif stage == "chips":
                landing = [lax.empty((3,) + p.shape[1:], p.dtype) for p in st["partial"]]
                out.append(_chip_exchange_rider(st["partial"], landing))
            else:
                out.append(_pair_share_rider(st["halves"]))
        return out

    def done(self, at, carried):
        for (group, stage), arrays in zip(self.pending, carried):
            if group in self.slots:
                self.slots[group] = arrays
                continue
            st = self.groups[group]
            tag = lambda what, n: what + "_" + n
            if stage == "pair":
                st["got"] = arrays
                st["partial"] = [_pair_add(tag("pair_add", n), self.where, g, r)
                                 for n, g, r in zip(st["names"], st["g4"], arrays)]
            elif stage == "chips":
                st["halves"] = [_final_add(tag("final_add", n), self.where, g, r, a)
                                for n, g, r, a in zip(st["names"], st["g4"], st["got"], arrays)]
            else:
                self.reduced.update(zip(st["names"], arrays))


SMALL_ROWS = 32
N_DEV = 8


def _all_reduce_small(pack):
    def body(p_ref, o_ref, buf, send, recv):
        x, y, c = _place()
        buf[4 * x + 2 * y + c] = p_ref[...]
        copies, waits = [], []
        for k in range(1, N_DEV):
            px = 1 - x if k & 4 else x
            py = 1 - y if k & 2 else y
            pc = 1 - c if k & 1 else c
            copies.append(pltpu.make_async_remote_copy(
                src_ref=p_ref, dst_ref=buf.at[4 * x + 2 * y + c], send_sem=send.at[k - 1],
                recv_sem=recv.at[k - 1], device_id=(px, py, pc), device_id_type=MESH))
            waits.append(pltpu.make_async_remote_copy(
                src_ref=p_ref, dst_ref=buf.at[4 * px + 2 * py + pc], send_sem=send.at[k - 1],
                recv_sem=recv.at[k - 1], device_id=(px, py, pc), device_id_type=MESH))
        for cp in copies:
            cp.start()
        for cp in waits:
            cp.wait_recv()
        acc = buf[0]
        for d in range(1, N_DEV):
            acc = acc + buf[d]
        o_ref[...] = acc
        for cp in copies:
            cp.wait_send()

    return pl.pallas_call(
        body, name="all_reduce_small",
        out_shape=jax.ShapeDtypeStruct(pack.shape, F32),
        scratch_shapes=[pltpu.VMEM((N_DEV,) + pack.shape, F32),
                        pltpu.SemaphoreType.DMA((N_DEV - 1,)), pltpu.SemaphoreType.DMA((N_DEV - 1,))],
    )(pack)


def _adamw(name, w, g, m, v):
    c1 = 1.0 - ADAM_B1 ** ADAM_STEP
    c2 = 1.0 - ADAM_B2 ** ADAM_STEP

    def fn(t, f):
        wv, gv, mv, vv = t
        m2 = ADAM_B1 * mv + (1.0 - ADAM_B1) * gv
        v2 = ADAM_B2 * vv + (1.0 - ADAM_B2) * (gv * gv)
        delta = -ADAM_LR * ((m2 / c1) / (jnp.sqrt(v2 / c2) + ADAM_EPS) + ADAM_WD * wv)
        return [delta, m2, v2], []

    cols = w.shape[1]
    return _ew(name, fn, [w, g, m, v], [], [(cols, F32)] * 3, ts=min(w.shape[0], 256))


LOSS_ROW = 26


def _pack_small(norm1_g, norm2_g, conv_b, b_gate, conv_w, q_norm_g, k_norm_g, rel_bias, loss=None):
    pack = jnp.zeros((SMALL_ROWS, D_MODEL), F32)
    for r0, v in ((0, norm1_g), (1, norm2_g), (2, conv_b), (3, b_gate.reshape(2, D_MODEL)), (5, conv_w),
                  (8, q_norm_g), (9, k_norm_g), (10, rel_bias)) + (((LOSS_ROW, loss),) if loss is not None else ()):
        v = v.reshape(-1, v.shape[-1]).astype(F32)
        pack = pack.at[r0:r0 + v.shape[0], :v.shape[1]].set(v)
    return pack


def _unpack_small(pack, conv_cols):
    return dict(norm1_g=pack[0], norm2_g=pack[1], conv_b=pack[2], b_gate=pack[3:5].reshape(2 * D_MODEL),
                conv_w=pack[5:8, :conv_cols], q_norm_g=pack[8, :HEAD_DIM], k_norm_g=pack[9, :HEAD_DIM],
                rel_bias=pack[10:10 + N_HEADS, :N_REL])


BIG = ["w_in", "w_attn_proj", "w_conv_proj", "w_gate", "w_out", "w_up", "w_down"]
LATE = ["w_attn_proj", "w_conv_proj", "w_out", "w_up", "w_down"]
WEIGHTS = ["norm1_g", "w_in", "q_norm_g", "k_norm_g", "rel_bias", "conv_w", "conv_b", "w_attn_proj",
           "w_conv_proj", "w_gate", "b_gate", "w_out", "norm2_g", "w_up", "w_down"]


def kernel(x, norm1_g, w_in, q_norm_g, k_norm_g, rel_bias, conv_w, conv_b, w_attn_proj, w_conv_proj, w_gate, b_gate, w_out, norm2_g, w_up, w_down, loss_target, m_norm1_g, m_w_in, m_q_norm_g, m_k_norm_g, m_rel_bias, m_conv_w, m_conv_b, m_w_attn_proj, m_w_conv_proj, m_w_gate, m_b_gate, m_w_out, m_norm2_g, m_w_up, m_w_down, v_norm1_g, v_w_in, v_q_norm_g, v_k_norm_g, v_rel_bias, v_conv_w, v_conv_b, v_w_attn_proj, v_w_conv_proj, v_w_gate, v_b_gate, v_w_out, v_norm2_g, v_w_up, v_w_down):
    given = dict(locals())
    w = {n: given[n] for n in WEIGHTS}
    m = {n: given["m_" + n] for n in WEIGHTS}
    v = {n: given["v_" + n] for n in WEIGHTS}
    s_len = x.shape[1]
    shard = 2 * lax.axis_index("x") + lax.axis_index("y")
    where = jnp.stack([shard, lax.axis_index("c")]).astype(jnp.int32)
    conv_cols = conv_w.shape[1]

    small_in = lax.dynamic_update_slice(jnp.zeros((N_SHARD, 16, conv_cols), F32), conv_w[None], (shard, 0, 0))
    slot = {n: _cast_into_slot("cast_" + n, where, w[n]) for n in BIG}
    comm = _Exchange(where, [slot["w_in"], small_in], [slot["w_gate"]], [slot[n] for n in LATE])

    loss_part, grad_x, _, small = _local_grads(
        x.reshape(s_len, D_MODEL), loss_target.reshape(s_len, D_MODEL), norm1_g, q_norm_g, k_norm_g,
        rel_bias, conv_b, b_gate, norm2_g, comm)
    grad = dict(comm.reduced)

    loss_local = _finish_loss(loss_part)
    pack = _pack_small(small["norm1_g"], small["norm2_g"], small["conv_wb"][3], jnp.concatenate(small["b_gate"], axis=1),
                       small["conv_wb"][0:3], small["q_norm_g"], small["k_norm_g"], small["rel_bias"],
                       loss=loss_local[0:1, :])
    total = _all_reduce_small(pack)
    g_small = _unpack_small(total, D_MODEL)
    g_small["conv_w"] = lax.dynamic_slice(g_small["conv_w"], (0, shard * conv_cols), (3, conv_cols))
    grad.update(g_small)

    delta, new_m, new_v = {}, {}, {}
    for n in BIG:
        delta[n], new_m[n], new_v[n] = _adamw("adamw_" + n, w[n], grad[n], m[n], v[n])
    small_names = [n for n in WEIGHTS if n not in BIG]
    packs = [_pack_small(**{n: src[n] for n in small_names}) for src in (w, grad, m, v)]
    for out, packed in zip((delta, new_m, new_v), _adamw("adamw_small", *packs)):
        out.update({n: a.reshape(w[n].shape) for n, a in _unpack_small(packed, conv_cols).items()})

    outs = [total[LOSS_ROW, 0], grad_x.reshape(x.shape)]
    for group in (grad, delta, new_m, new_v):
        outs += [group[n].reshape(w[n].shape) for n in WEIGHTS]
    return tuple(outs)
```

```python
import functools

import jax
import jax.numpy as jnp
from jax import lax
from jax.experimental import pallas as pl
from jax.experimental.pallas import tpu as pltpu

F32 = jnp.float32
BF16 = jnp.bfloat16

D_MODEL = 1024
N_HEADS = 16
HEAD_DIM = 64
CHUNK = 64
N_PREV_CHUNKS = 8
MAX_REL = 256
D_FF = 4096
N_REL = 2 * MAX_REL + 1
REL_PAD = 640
EPS = 1e-6
NEG_INF = -1e30
QK_SCALE = HEAD_DIM ** -0.5

SUPER = 4 * CHUNK
BAND = SUPER + N_PREV_CHUNKS * CHUNK
SKEW_W = 1024
N_SHARD = 4
LANE = 128
VMEM_LIMIT = 48 * 1024 * 1024

ADAM_LR = 0.001
ADAM_B1 = 0.9
ADAM_B2 = 0.999
ADAM_EPS = 1e-08
ADAM_WD = 0.01
ADAM_STEP = 10

MESH = pl.DeviceIdType.MESH
NN = (((1,), (0,)), ((), ()))
NT = (((1,), (1,)), ((), ()))
TN = (((0,), (0,)), ((), ()))


def _params(*sem):
    return pltpu.CompilerParams(dimension_semantics=sem or None, vmem_limit_bytes=VMEM_LIMIT)


HBM_SPEC = pl.BlockSpec(memory_space=pl.ANY)


class _Rider:
    def __init__(self, sources, arrays, n_sem, start, finish):
        self.sources, self.arrays, self.n_sem, self.start, self.finish = sources, arrays, n_sem, start, finish


def _carry(riders, body, *, name, out_shape, grid=(), in_specs=None, out_specs=None, scratch_shapes=(),
           semantics=(), input_output_aliases=None):
    aliases = dict(input_output_aliases or {})
    if not riders:
        kw = {} if in_specs is None else dict(in_specs=in_specs, out_specs=out_specs)
        return pl.pallas_call(body, name=name, grid=grid, out_shape=out_shape, scratch_shapes=scratch_shapes,
                              input_output_aliases=aliases, compiler_params=_params(*semantics), **kw)
    single = not isinstance(out_shape, (list, tuple))
    shapes = [out_shape] if single else list(out_shape)
    n_out, n_scr = len(shapes), len(scratch_shapes)
    srcs = [a for r in riders for a in r.sources]
    arrs = [a for r in riders for a in r.arrays]
    vmem = pl.BlockSpec(memory_space=pltpu.VMEM)

    def run(*args):
        n_in = len(args)

        def wrapped(*refs):
            pos = n_in
            src_refs = refs[pos:pos + len(srcs)]
            pos += len(srcs) + len(arrs)
            outs = refs[pos:pos + n_out]
            pos += n_out
            arr_refs = refs[pos:pos + len(arrs)]
            pos += len(arrs)
            scratch = refs[pos:pos + n_scr]
            sems = refs[pos + n_scr:]
            first, last = True, True
            for d, size in enumerate(grid):
                first = jnp.logical_and(first, pl.program_id(d) == 0)
                last = jnp.logical_and(last, pl.program_id(d) == size - 1)

            def each(method):
                s0 = a0 = 0
                for k, r in enumerate(riders):
                    getattr(r, method)(src_refs[s0:s0 + len(r.sources)], arr_refs[a0:a0 + len(r.arrays)],
                                       sems[2 * k], sems[2 * k + 1])
                    s0, a0 = s0 + len(r.sources), a0 + len(r.arrays)

            pl.when(first)(lambda: each("start"))
            body(*refs[:n_in], *outs, *scratch)
            pl.when(last)(lambda: each("finish"))

        ins = [vmem] * n_in if in_specs is None else list(in_specs)
        if out_specs is None:
            o_specs = [vmem] * n_out
        else:
            o_specs = [out_specs] if single else list(out_specs)
        for k in range(len(arrs)):
            aliases[n_in + len(srcs) + k] = n_out + k
        res = pl.pallas_call(
            wrapped, name=name, grid=grid,
            in_specs=ins + [HBM_SPEC] * (len(srcs) + len(arrs)),
            out_specs=o_specs + [HBM_SPEC] * len(arrs),
            out_shape=shapes + [jax.ShapeDtypeStruct(a.shape, a.dtype) for a in arrs],
            scratch_shapes=list(scratch_shapes) + [pltpu.SemaphoreType.DMA((r.n_sem,)) for r in riders for _ in range(2)],
            input_output_aliases=aliases,
            compiler_params=_params(*["arbitrary"] * len(grid)),
        )(*args, *srcs, *arrs)
        core, rest = res[:n_out], list(res[n_out:])
        carried, a0 = [], 0
        for r in riders:
            carried.append(rest[a0:a0 + len(r.arrays)])
            a0 += len(r.arrays)
        return (core[0] if single else core), carried

    return run


def _mm(name, dims, a, a_spec, b, b_spec, grid, tile, outs, epilogue=None, extras=(), riders=()):
    nk, ne, no = grid[2], len(extras), len(outs)

    def body(a_ref, b_ref, *refs):
        e_refs, o_refs = refs[:ne], refs[ne:ne + no]
        part = lax.dot_general(a_ref[...], b_ref[...], dims, preferred_element_type=F32)

        def finish(acc):
            if epilogue is None:
                o_refs[0][...] = acc.astype(o_refs[0].dtype)
            else:
                epilogue(acc, [r[...] for r in e_refs], o_refs)

        if nk == 1:
            finish(part)
        else:
            acc_ref = refs[ne + no]
            k = pl.program_id(2)

            @pl.when(k == 0)
            def _():
                acc_ref[...] = part

            @pl.when(k > 0)
            def _():
                acc_ref[...] += part

            @pl.when(k == nk - 1)
            def _():
                finish(acc_ref[...])

    res = _carry(
        riders, body, name=name, grid=grid,
        in_specs=[a_spec, b_spec] + [s for _, s in extras],
        out_specs=[s for _, s in outs],
        out_shape=[s for s, _ in outs],
        scratch_shapes=[pltpu.VMEM(tile, F32)] if nk > 1 else [],
        semantics=("parallel", "parallel", "arbitrary"),
    )(a, b, *[e for e, _ in extras])
    res, carried = res if riders else (res, None)
    res = res[0] if no == 1 else res
    return (res, carried) if riders else res


def _tile_spec(tm, tn, col0=0):
    return pl.BlockSpec((tm, tn), lambda i, j, k: (i, j + col0))


def _out2d(m, n, dtype, tm, tn):
    return (jax.ShapeDtypeStruct((m, n), dtype), _tile_spec(tm, tn))


def _mm_fwd(name, a, w, tm, tn, tk, outs=None, epilogue=None, extras=(), col0=0, ncols=None, riders=()):
    m, kdim = a.shape
    if w.ndim == 3:
        per = w.shape[2] // tn
        n = ncols or N_SHARD * w.shape[2]
        w_spec = pl.BlockSpec((None, tk, tn), lambda i, j, k: ((j + col0) // per, k, (j + col0) % per))
    else:
        n = ncols or w.shape[1]
        w_spec = pl.BlockSpec((tk, tn), lambda i, j, k: (k, j + col0))
    if outs is None:
        outs = [_out2d(m, n, F32, tm, tn)]
    return _mm(name, NN, a, pl.BlockSpec((tm, tk), lambda i, j, k: (i, k)), w, w_spec,
               (m // tm, n // tn, kdim // tk), (tm, tn), outs, epilogue, extras, riders)


def _mm_bwd_x(name, g, g_spec, w, tm, tj, tc, m, n_contract, outs=None, epilogue=None, extras=(), riders=()):
    if w.ndim == 3:
        per = w.shape[2] // tc
        kdim = w.shape[1]
        w_spec = pl.BlockSpec((None, tj, tc), lambda i, j, n: (n // per, j, n % per))
    else:
        kdim = w.shape[0]
        w_spec = pl.BlockSpec((tj, tc), lambda i, j, n: (j, n))
    if outs is None:
        outs = [_out2d(m, kdim, F32, tm, tj)]
    return _mm(name, NT, g, g_spec, w, w_spec, (m // tm, kdim // tj, n_contract // tc),
               (tm, tj), outs, epilogue, extras, riders)


def _mm_bwd_w(name, a, g, g_spec, n, tk, tn, tm, sharded, riders=()):
    m, kdim = a.shape
    if sharded:
        per = (n // N_SHARD) // tn
        out = (jax.ShapeDtypeStruct((N_SHARD, kdim, n // N_SHARD), F32),
               pl.BlockSpec((None, tk, tn), lambda i, j, mm: (j // per, i, j % per)))
    else:
        out = (jax.ShapeDtypeStruct((kdim, n), F32), pl.BlockSpec((tk, tn), lambda i, j, mm: (i, j)))
    return _mm(name, TN, a, pl.BlockSpec((tm, tk), lambda i, j, mm: (mm, i)), g, g_spec,
               (kdim // tk, n // tn, m // tm), (tk, tn), [out], riders=riders)


def _ew(name, fn, tiles, fulls, outs, sums=(), ts=512, riders=()):
    tiles = [t if isinstance(t, tuple) else (t, pl.BlockSpec((ts, t.shape[1]), lambda i: (i, 0)))
             for t in tiles]
    s_rows = tiles[0][0].shape[-2]
    nt, nf, no = len(tiles), len(fulls), len(outs)

    def body(*refs):
        t_vals = [r[...] for r in refs[:nt]]
        f_vals = [r[...] for r in refs[nt:nt + nf]]
        o_refs, s_refs = refs[nt + nf:nt + nf + no], refs[nt + nf + no:]
        o_vals, s_vals = fn(t_vals, f_vals)
        for r, v in zip(o_refs, o_vals):
            r[...] = v.astype(r.dtype)
        for r, v in zip(s_refs, s_vals):
            part = jnp.sum(v, axis=0, keepdims=True)

            @pl.when(pl.program_id(0) == 0)
            def _():
                r[...] = part

            @pl.when(pl.program_id(0) > 0)
            def _():
                r[...] += part

    full_specs = [pl.BlockSpec(f.shape, lambda i, nd=f.ndim: (0,) * nd) for f in fulls]
    return _carry(
        riders, body, name=name, grid=(s_rows // ts,),
        in_specs=[s for _, s in tiles] + full_specs,
        out_specs=[pl.BlockSpec((ts, c), lambda i: (i, 0)) for c, _ in outs]
        + [pl.BlockSpec((1, c), lambda i: (0, 0)) for c in sums],
        out_shape=[jax.ShapeDtypeStruct((s_rows, c), dt) for c, dt in outs]
        + [jax.ShapeDtypeStruct((1, c), F32) for c in sums],
        semantics=("arbitrary",),
    )(*[t for t, _ in tiles], *fulls)


def _rms_fwd(name, x, g, riders=()):
    def fn(t, f):
        xv = t[0]
        r = lax.rsqrt(jnp.mean(xv * xv, axis=-1, keepdims=True) + EPS)
        return [xv * r * f[0]], []
    out = _ew(name, fn, [x], [g], [(x.shape[1], BF16)], riders=riders)
    return (out[0][0], out[1]) if riders else out[0]


def _rms_bwd(name, dh, x, g, dres, out_dtype=BF16, riders=()):
    def fn(t, f):
        dhv, xv, dr = [v.astype(F32) for v in t]
        r = lax.rsqrt(jnp.mean(xv * xv, axis=-1, keepdims=True) + EPS)
        xhat = xv * r
        u = dhv * f[0]
        dx = r * (u - xhat * jnp.mean(u * xhat, axis=-1, keepdims=True)) + dr
        return [dx], [dhv * xhat]
    c = x.shape[1]
    return _ew(name, fn, [dh, x, dres], [g], [(c, out_dtype)], sums=[c], riders=riders)


def _split3(x):
    x1 = x.astype(BF16)
    r1 = x - x1.astype(F32)
    x2 = r1.astype(BF16)
    x3 = (r1 - x2.astype(F32)).astype(BF16)
    return x1, x2, x3


def _rel_class(cp):
    far = (cp < MAX_REL) | (cp > BAND)
    return jnp.where(far, 2 * MAX_REL, BAND - cp)


def _skew_rows(x, sign):
    row = lax.broadcasted_iota(jnp.int32, x.shape, 0)
    for b in range(CHUNK.bit_length() - 1):
        shift = (1 << b) if sign > 0 else SKEW_W - (1 << b)
        x = jnp.where((row >> b) & 1 == 1, pltpu.roll(x, shift, 1), x)
    return x


def _roll_lanes(x, shift):
    return x if shift % SKEW_W == 0 else pltpu.roll(x, shift % SKEW_W, 1)


N_START = 3


def _bias_expand(rel_bias, riders=()):
    rel = jnp.pad(rel_bias, ((0, 0), (0, REL_PAD - N_REL))).reshape(N_HEADS, 1, REL_PAD)

    def body(rel_ref, o_ref):
        cls = lax.broadcasted_iota(jnp.int32, (REL_PAD, SKEW_W), 0)
        cp = lax.broadcasted_iota(jnp.int32, (REL_PAD, SKEW_W), 1)
        onehot = (cls == _rel_class(cp)).astype(BF16)
        rel8 = jnp.broadcast_to(rel_ref[...], (8, REL_PAD))
        trow = sum(jnp.dot(p, onehot, preferred_element_type=F32) for p in _split3(rel8))[0:1]
        first = _skew_rows(jnp.broadcast_to(trow, (CHUNK, SKEW_W)), +1)
        full = jnp.concatenate([_roll_lanes(first, CHUNK * g) for g in range(SUPER // CHUNK)], axis=0)[:, :BAND]
        qc = lax.broadcasted_iota(jnp.int32, (SUPER, BAND), 0) // CHUNK
        kc = lax.broadcasted_iota(jnp.int32, (SUPER, BAND), 1) // CHUNK
        on_band = (kc >= qc) & (kc <= qc + N_PREV_CHUNKS)
        table = jnp.where(on_band, full, NEG_INF).T
        key = lax.broadcasted_iota(jnp.int32, (BAND, SUPER), 0)
        for t in range(N_START):
            o_ref[t] = jnp.where(key < (N_START - 1 - t) * SUPER, NEG_INF, table)

    return _carry(
        riders, body, name="bias_expand", grid=(N_HEADS,),
        in_specs=[pl.BlockSpec((None, 1, REL_PAD), lambda h: (h, 0, 0))],
        out_specs=pl.BlockSpec((N_START, None, BAND, SUPER), lambda h: (0, h, 0, 0)),
        out_shape=jax.ShapeDtypeStruct((N_START, N_HEADS, BAND, SUPER), F32),
        semantics=("arbitrary",),
    )(rel)


def _bias_reduce(dbias, riders=()):
    def body(d_ref, o_ref):
        x = jnp.concatenate([d_ref[...].T, jnp.zeros((SUPER, SKEW_W - BAND), F32)], axis=1)
        folded = sum(_roll_lanes(x[CHUNK * g:CHUNK * (g + 1)], -CHUNK * g) for g in range(SUPER // CHUNK))
        diag = jnp.sum(_skew_rows(folded, -1), axis=0, keepdims=True)
        cp = lax.broadcasted_iota(jnp.int32, (SKEW_W, REL_PAD), 0)
        cls = lax.broadcasted_iota(jnp.int32, (SKEW_W, REL_PAD), 1)
        onehot = (cls == _rel_class(cp)).astype(BF16)
        diag8 = jnp.broadcast_to(diag, (8, SKEW_W))
        o_ref[...] = sum(jnp.dot(p, onehot, preferred_element_type=F32) for p in _split3(diag8))[0:1]

    out = _carry(
        riders, body, name="bias_reduce", grid=(N_HEADS,),
        in_specs=[pl.BlockSpec((None, BAND, SUPER), lambda h: (h, 0, 0))],
        out_specs=pl.BlockSpec((None, 1, REL_PAD), lambda h: (h, 0, 0)),
        out_shape=jax.ShapeDtypeStruct((N_HEADS, 1, REL_PAD), F32),
        semantics=("arbitrary",),
    )(dbias)
    out, carried = out if riders else (out, None)
    out = out.reshape(N_HEADS, REL_PAD)[:, :N_REL]
    return (out, carried) if riders else out


HEADS_PER_STEP = 8
HEAD_COLS = HEADS_PER_STEP * HEAD_DIM
N_HEAD_GROUPS = N_HEADS // HEADS_PER_STEP


def _unit(x):
    r = lax.rsqrt(jnp.mean(x * x, axis=-1, keepdims=True) + EPS)
    return x * r, r


def _scores_t(qs, kn, bias_t):
    return jnp.concatenate([lax.dot_general(k, qs, NT, preferred_element_type=F32) for k in kn], axis=0) + bias_t


def _bias_spec():
    return pl.BlockSpec((None, HEADS_PER_STEP, BAND, SUPER), lambda hg, i: (jnp.minimum(i, N_START - 1), hg, 0, 0))


def _band_specs(nb, col0, clamp_hi):
    def spec(d):
        def index(hg, i):
            blk = jnp.maximum(i - d, 0)
            if clamp_hi:
                blk = jnp.minimum(blk, nb - 1)
            return (blk, col0 + hg)
        return pl.BlockSpec((SUPER, HEAD_COLS), index)
    return [spec(2), spec(1), spec(0)]


def _head_sums(y):
    cols = y.shape[1]
    same_head = (lax.broadcasted_iota(jnp.int32, (cols, cols), 0) // HEAD_DIM
                 == lax.broadcasted_iota(jnp.int32, (cols, cols), 1) // HEAD_DIM).astype(BF16)
    hi = y.astype(BF16)
    lo = (y - hi.astype(F32)).astype(BF16)
    return (jnp.dot(hi, same_head, preferred_element_type=F32)
            + jnp.dot(lo, same_head, preferred_element_type=F32))


def _head_unit(x):
    r = lax.rsqrt(_head_sums(x * x) * (1.0 / HEAD_DIM) + EPS)
    return x * r, r


def _head(hh):
    return slice(HEAD_DIM * hh, HEAD_DIM * (hh + 1))


def _key_block(j):
    return slice(SUPER * j, SUPER * (j + 1))


LSE_ROWS = 8


def _attn_fwd(qkn, v, bias, riders=()):
    s_len = qkn.shape[0]
    nb = s_len // SUPER

    def body(q_ref, k0, k1, k2, v0, v1, v2, b_ref, o_ref, lse_ref):
        outs = []

        def probabilities(hh):
            sl = _head(hh)
            s = _scores_t(q_ref[:, sl], [k0[:, sl], k1[:, sl], k2[:, sl]], b_ref[hh])
            m = jnp.max(s, axis=0, keepdims=True)
            e = jnp.exp(s - m)
            l = jnp.sum(e, axis=0, keepdims=True)
            lse_ref[hh:hh + 1, :] = m + jnp.log(l)
            return (e * (1.0 / l)).astype(BF16), sl

        def weighted_values(p, sl):
            outs.append(sum(lax.dot_general(p[_key_block(j), :], vj[:, sl], TN, preferred_element_type=F32)
                            for j, vj in enumerate((v0, v1, v2))))

        ready = probabilities(0)
        for hh in range(1, HEADS_PER_STEP):
            following = probabilities(hh)
            weighted_values(*ready)
            ready = following
        weighted_values(*ready)
        o_ref[...] = jnp.concatenate(outs, axis=1).astype(o_ref.dtype)

    return _carry(
        riders, body, name="attn_fwd", grid=(N_HEAD_GROUPS, nb),
        in_specs=[pl.BlockSpec((SUPER, HEAD_COLS), lambda hg, i: (i, hg))]
        + _band_specs(nb, N_HEAD_GROUPS, False) + _band_specs(nb, 0, False) + [_bias_spec()],
        out_specs=[pl.BlockSpec((SUPER, HEAD_COLS), lambda hg, i: (i, hg)),
                   pl.BlockSpec((None, LSE_ROWS, SUPER), lambda hg, i: (hg, 0, i))],
        out_shape=[jax.ShapeDtypeStruct((s_len, D_MODEL), BF16),
                   jax.ShapeDtypeStruct((N_HEAD_GROUPS, LSE_ROWS, s_len), F32)],
        semantics=("parallel", "arbitrary"),
    )(qkn, qkn, qkn, qkn, v, v, v, bias)


def _attn_bwd(qkn, v, out, d_out, bias, lse, riders=()):
    s_len = qkn.shape[0]
    nb = s_len // SUPER

    def body(q_ref, k0, k1, k2, v0, v1, v2, o_ref, do_ref, b_ref, lse_ref, dp_ref, db_ref, aq_ref, ak_ref, av_ref):
        i = pl.program_id(1)

        @pl.when(i == 0)
        def _():
            aq_ref[...] = jnp.zeros_like(aq_ref)
            ak_ref[...] = jnp.zeros_like(ak_ref)
            av_ref[...] = jnp.zeros_like(av_ref)
            db_ref[...] = jnp.zeros_like(db_ref)

        @pl.when(i < nb)
        def _():
            dq, dk, dv = [], [[], [], []], [[], [], []]
            ones = jnp.ones((8, HEAD_DIM), BF16)

            def softmax_grad(hh):
                sl = _head(hh)
                qs, do = q_ref[:, sl], do_ref[:, sl]
                kn = [k0[:, sl], k1[:, sl], k2[:, sl]]
                prod = do.astype(F32) * o_ref[:, sl].astype(F32)
                hi = prod.astype(BF16)
                lo = (prod - hi.astype(F32)).astype(BF16)
                delta = (lax.dot_general(ones, hi, NT, preferred_element_type=F32)
                         + lax.dot_general(ones, lo, NT, preferred_element_type=F32))[0:1]
                lse_row = lse_ref[hh:hh + 1, :]
                pb, dsb = [], []
                for j, (kj, vj) in enumerate(zip(kn, (v0, v1, v2))):
                    rows = _key_block(j)
                    p = jnp.exp(lax.dot_general(kj, qs, NT, preferred_element_type=F32) + b_ref[hh, rows, :] - lse_row)
                    ds = p * (lax.dot_general(vj[:, sl], do, NT, preferred_element_type=F32) - delta)
                    db_ref[hh, rows, :] += ds
                    pb.append(p.astype(BF16))
                    dsb.append(ds.astype(BF16))
                return pb, dsb, qs, do, kn

            def operand_grads(pb, dsb, qs, do, kn):
                dq.append(sum(lax.dot_general(dsb[j], kn[j], TN, preferred_element_type=F32) for j in range(3)))
                for j in range(3):
                    dv[j].append(jnp.dot(pb[j], do, preferred_element_type=F32))
                    dk[j].append(jnp.dot(dsb[j], qs, preferred_element_type=F32))

            ready = softmax_grad(0)
            for hh in range(1, HEADS_PER_STEP):
                following = softmax_grad(hh)
                operand_grads(*ready)
                ready = following
            operand_grads(*ready)
            aq_ref[i % 3] = jnp.concatenate(dq, axis=1)
            for j in range(3):
                slot = (i + 1 + j) % 3
                if j < 2:
                    ak_ref[slot] += jnp.concatenate(dk[j], axis=1)
                    av_ref[slot] += jnp.concatenate(dv[j], axis=1)
                else:
                    ak_ref[slot] = jnp.concatenate(dk[j], axis=1)
                    av_ref[slot] = jnp.concatenate(dv[j], axis=1)

        slot = (i + 1) % 3
        dp_ref[0] = aq_ref[slot].astype(dp_ref.dtype)
        dp_ref[1] = ak_ref[slot].astype(dp_ref.dtype)
        dp_ref[2] = av_ref[slot].astype(dp_ref.dtype)

    def qrow(hg, i):
        return (jnp.minimum(i, nb - 1), hg)

    return _carry(
        riders, body, name="attn_bwd", grid=(N_HEAD_GROUPS, nb + 2),
        in_specs=[pl.BlockSpec((SUPER, HEAD_COLS), qrow)]
        + _band_specs(nb, N_HEAD_GROUPS, True) + _band_specs(nb, 0, True)
        + [pl.BlockSpec((SUPER, HEAD_COLS), qrow), pl.BlockSpec((SUPER, HEAD_COLS), qrow), _bias_spec(),
           pl.BlockSpec((None, LSE_ROWS, SUPER), lambda hg, i: (hg, 0, jnp.minimum(i, nb - 1)))],
        out_specs=[pl.BlockSpec((3, SUPER, HEAD_COLS), lambda hg, i: (0, jnp.maximum(i - 2, 0), hg)),
                   pl.BlockSpec((HEADS_PER_STEP, BAND, SUPER), lambda hg, i: (hg, 0, 0))],
        out_shape=[jax.ShapeDtypeStruct((6, s_len, D_MODEL), BF16),
                   jax.ShapeDtypeStruct((N_HEADS, BAND, SUPER), F32)],
        scratch_shapes=[pltpu.VMEM((3, SUPER, HEAD_COLS), F32)] * 3,
        semantics=("parallel", "arbitrary"),
    )(qkn, qkn, qkn, qkn, v, v, v, out, d_out, bias, lse)


def _qk_norm_bwd(dproj6, qk_raw, gq, gk):
    s_len = qk_raw.shape[0]
    ts = min(1024, s_len)

    nsteps = s_len // ts
    half = D_MODEL // 2

    def body(d_ref, raw_ref, gq_ref, gk_ref, o_ref, dgq_ref, dgk_ref, acc_ref):
        step = pl.program_id(0)

        @pl.when(step == 0)
        def _():
            acc_ref[...] = jnp.zeros_like(acc_ref)

        for piece, (g_ref, scale) in enumerate(((gq_ref, QK_SCALE), (gk_ref, 1.0))):
            for c0 in (0, half):
                xhat, r = _head_unit(raw_ref[:, piece * D_MODEL + c0:piece * D_MODEL + c0 + half].astype(F32))
                dn = d_ref[piece, :, c0:c0 + half].astype(F32) * scale
                u = dn * g_ref[...]
                dx = r * (u - xhat * (_head_sums(u * xhat) * (1.0 / HEAD_DIM)))
                o_ref[piece, :, c0:c0 + half] = dx.astype(o_ref.dtype)
                acc_ref[piece:piece + 1, c0:c0 + half] += jnp.sum(dn * xhat, axis=0, keepdims=True)

        @pl.when(step == nsteps - 1)
        def _():
            lane = lax.broadcasted_iota(jnp.int32, (D_MODEL, LANE), 0) % HEAD_DIM
            fold = (lane == lax.broadcasted_iota(jnp.int32, (D_MODEL, LANE), 1)).astype(BF16)
            tot = sum(jnp.dot(p, fold, preferred_element_type=F32) for p in _split3(acc_ref[...]))
            dgq_ref[...] = tot[0:1, :HEAD_DIM]
            dgk_ref[...] = tot[1:2, :HEAD_DIM]

    gain = pl.BlockSpec((1, half), lambda i: (0, 0))
    small = pl.BlockSpec((1, HEAD_DIM), lambda i: (0, 0))
    per_head = lambda g: jnp.tile(g, (1, half // HEAD_DIM))
    return pl.pallas_call(
        body, name="qk_norm_bwd", grid=(nsteps,),
        in_specs=[pl.BlockSpec((2, ts, D_MODEL), lambda i: (0, i, 0)),
                  pl.BlockSpec((ts, 2 * D_MODEL), lambda i: (i, 0)), gain, gain],
        out_specs=[pl.BlockSpec((2, ts, D_MODEL), lambda i: (0, i, 0)), small, small],
        out_shape=[jax.ShapeDtypeStruct(dproj6.shape, dproj6.dtype),
                   jax.ShapeDtypeStruct((1, HEAD_DIM), F32), jax.ShapeDtypeStruct((1, HEAD_DIM), F32)],
        scratch_shapes=[pltpu.VMEM((8, D_MODEL), F32)],
        input_output_aliases={0: 0},
        compiler_params=_params("arbitrary"),
    )(dproj6, qk_raw, per_head(gq), per_head(gk))


CONV_ROWS = 512
HALO = 16


def _rows_with_halo(ref, r0, n, front, s_len):
    zeros = jnp.zeros((HALO, ref.shape[1]), F32)
    if front:
        return (jnp.concatenate([zeros, ref[0:n, :].astype(F32)], axis=0) if r0 == 0
                else ref[r0 - HALO:r0 + n, :].astype(F32))
    return (jnp.concatenate([ref[r0:r0 + n, :].astype(F32), zeros], axis=0) if r0 + n == s_len
            else ref[r0:r0 + n + HALO, :].astype(F32))


def _earlier(ext, k):
    return pltpu.roll(ext, k, 0)[HALO:]


def _later(ext, k):
    n = ext.shape[0]
    return pltpu.roll(ext, n - k, 0)[:n - HALO]


def _conv_cols(col0):
    return lambda s_len: pl.BlockSpec((s_len, LANE), lambda j: (0, col0 + j))


def _conv_fwd(proj, conv_w, conv_b, riders=()):
    s_len = proj.shape[0]

    def body(bg_ref, cg_ref, xc_ref, w_ref, b_ref, o_ref):
        w = [w_ref[t:t + 1, :] for t in range(3)]
        for r0 in range(0, s_len, CONV_ROWS):
            u = _rows_with_halo(cg_ref, r0, CONV_ROWS, True, s_len) * \
                _rows_with_halo(xc_ref, r0, CONV_ROWS, True, s_len)
            conv = b_ref[...] + w[0] * _earlier(u, 2) + w[1] * _earlier(u, 1) + w[2] * u[HALO:]
            o_ref[r0:r0 + CONV_ROWS, :] = (bg_ref[r0:r0 + CONV_ROWS, :].astype(F32) * conv).astype(o_ref.dtype)

    return _carry(
        riders, body, name="conv_fwd", grid=(D_MODEL // LANE,),
        in_specs=[_conv_cols(0)(s_len), _conv_cols(8)(s_len), _conv_cols(16)(s_len),
                  pl.BlockSpec((3, LANE), lambda j: (0, j)), pl.BlockSpec((1, LANE), lambda j: (0, j))],
        out_specs=pl.BlockSpec((s_len, LANE), lambda j: (0, j)),
        out_shape=jax.ShapeDtypeStruct((s_len, D_MODEL), BF16),
        semantics=("parallel",),
    )(proj, proj, proj, conv_w, conv_b)


def _conv_bwd(dproj6, dy, proj, conv_w, conv_b, riders=()):
    s_len = proj.shape[0]

    def body(dy_ref, bg_ref, cg_ref, xc_ref, w_ref, b_ref, _, dp_ref, dw_ref):
        w = [w_ref[t:t + 1, :] for t in range(3)]
        acc = [jnp.zeros((1, LANE), F32) for _ in range(4)]
        for r0 in range(0, s_len, CONV_ROWS):
            rows = slice(r0, r0 + CONV_ROWS)
            u = _rows_with_halo(cg_ref, r0, CONV_ROWS, True, s_len) * \
                _rows_with_halo(xc_ref, r0, CONV_ROWS, True, s_len)
            u2, u1, u0 = _earlier(u, 2), _earlier(u, 1), u[HALO:]
            conv = b_ref[...] + w[0] * u2 + w[1] * u1 + w[2] * u0
            dp_ref[0, rows, :] = (dy_ref[rows, :].astype(F32) * conv).astype(dp_ref.dtype)
            dconv_ext = _rows_with_halo(dy_ref, r0, CONV_ROWS, False, s_len) * \
                _rows_with_halo(bg_ref, r0, CONV_ROWS, False, s_len)
            dconv = dconv_ext[:CONV_ROWS]
            for t, term in enumerate([dconv * u2, dconv * u1, dconv * u0, dconv]):
                acc[t] = acc[t] + jnp.sum(term, axis=0, keepdims=True)
            du = w[2] * dconv + w[1] * _later(dconv_ext, 1) + w[0] * _later(dconv_ext, 2)
            dp_ref[1, rows, :] = (du * xc_ref[rows, :].astype(F32)).astype(dp_ref.dtype)
            dp_ref[2, rows, :] = (du * cg_ref[rows, :].astype(F32)).astype(dp_ref.dtype)
        dw_ref[...] = jnp.zeros_like(dw_ref)
        for t in range(4):
            dw_ref[t:t + 1, :] = acc[t]

    return _carry(
        riders, body, name="conv_bwd", grid=(D_MODEL // LANE,),
        in_specs=[pl.BlockSpec((s_len, LANE), lambda j: (0, j)),
                  _conv_cols(0)(s_len), _conv_cols(8)(s_len), _conv_cols(16)(s_len),
                  pl.BlockSpec((3, LANE), lambda j: (0, j)), pl.BlockSpec((1, LANE), lambda j: (0, j)),
                  pl.BlockSpec(memory_space=pl.ANY)],
        out_specs=[pl.BlockSpec((3, s_len, LANE), lambda j: (1, 0, j)),
                   pl.BlockSpec((8, LANE), lambda j: (0, j))],
        out_shape=[jax.ShapeDtypeStruct(dproj6.shape, dproj6.dtype),
                   jax.ShapeDtypeStruct((8, D_MODEL), F32)],
        input_output_aliases={6: 0},
        semantics=("parallel",),
    )(dy, proj, proj, proj, conv_w, conv_b, dproj6)


class _NoComm:
    def __init__(self, early, late):
        self.early, self.late = early, late

    def riders(self, at):
        return ()

    def done(self, at, carried):
        pass

    def early_weights(self):
        return self.early[0], self.early[2]

    def gate_weight(self):
        return self.early[1]

    def late_weights(self):
        return self.late

    def grads_ready(self, group, grads):
        pass


def _d_input(name, terms, riders=()):
    s_len, kdim = terms[0][0].shape[1], terms[0][1].shape[1]
    tm, chunk = 512, 512
    n = len(terms)

    def body(*refs):
        o_ref = refs[-1]
        acc = jnp.zeros(o_ref.shape, F32)
        for p_ref, w_ref in zip(refs[:n], refs[n:2 * n]):
            n_piece, _, width = p_ref.shape
            per_shard = w_ref.shape[2]
            for c0 in range(0, n_piece * width, chunk):
                acc = acc + lax.dot_general(p_ref[c0 // width, :, c0 % width:c0 % width + chunk],
                                            w_ref[c0 // per_shard, :, c0 % per_shard:c0 % per_shard + chunk],
                                            NT, preferred_element_type=F32)
        o_ref[...] = acc.astype(o_ref.dtype)

    tile = pl.BlockSpec((tm, kdim), lambda i: (i, 0))
    return _carry(
        riders, body, name=name, grid=(s_len // tm,),
        in_specs=[pl.BlockSpec((p.shape[0], tm, p.shape[2]), lambda i: (0, i, 0)) for p, _ in terms]
        + [pl.BlockSpec(w.shape, lambda i: (0, 0, 0), pipeline_mode=pl.Buffered(1)) for _, w in terms],
        out_specs=tile, out_shape=jax.ShapeDtypeStruct((s_len, kdim), BF16),
        semantics=("parallel",),
    )(*[p for p, _ in terms], *[w for _, w in terms])


def _local_grads(x, target, norm1_g, q_norm_g, k_norm_g, rel_bias, conv_b, b_gate, norm2_g, comm):
    s_len = x.shape[0]
    tm = min(1024, s_len)
    tb = min(2048, s_len)
    row = lambda v: v.reshape(1, -1)

    def carrying(at, fn, *args, **kw):
        riders = comm.riders(at)
        out = fn(*args, riders=riders, **kw)
        if riders:
            out, carried = out
            comm.done(at, carried)
        return out

    bias = carrying("bias_expand", _bias_expand, rel_bias)
    h = carrying("norm1", _rms_fwd, "norm1", x, row(norm1_g))
    w_in3, conv_w = comm.early_weights()
    gq, gk = row(q_norm_g), row(k_norm_g)

    def qk_epi(acc, e, o):
        o[0][...] = acc.astype(BF16)
        gain = jnp.where(pl.program_id(1) < 2, e[0] * QK_SCALE, e[1])
        o[1][...] = (_head_unit(acc)[0] * gain).astype(BF16)
    small = pl.BlockSpec((1, 512), lambda i, j, k: (0, 0))
    per_head = lambda g: jnp.tile(g, (1, 512 // HEAD_DIM))
    qk_raw, qkn = carrying("proj_qk", _mm_fwd, "proj_qk", h, w_in3, tm, 512, D_MODEL, ncols=2 * D_MODEL,
                           epilogue=qk_epi, outs=[_out2d(s_len, 2 * D_MODEL, BF16, tm, 512)] * 2,
                           extras=[(per_head(gq), small), (per_head(gk), small)])
    v = carrying("proj_v", _mm_fwd, "proj_v", h, w_in3, tb, 512, D_MODEL, col0=4, ncols=D_MODEL,
                 outs=[_out2d(s_len, D_MODEL, BF16, tb, 512)])
    w_gate3 = comm.gate_weight()
    conv_in = _mm_fwd("proj_conv", h, w_in3, tb, 1536, D_MODEL, col0=2, ncols=3 * D_MODEL,
                      outs=[_out2d(s_len, 3 * D_MODEL, BF16, tb, 1536)])

    def gate_epi(acc, e, o):
        o[0][...] = jax.nn.sigmoid(acc + e[0]).astype(BF16)
    gates = _mm_fwd("gates", h, w_gate3, tb, 512, D_MODEL, epilogue=gate_epi,
                    outs=[_out2d(s_len, 2 * D_MODEL, BF16, tb, 512)],
                    extras=[(row(b_gate), pl.BlockSpec((1, 512), lambda i, j, k: (0, j)))])

    attn, lse = carrying("attn_fwd", _attn_fwd, qkn, v, bias)
    yconv = carrying("conv_fwd", _conv_fwd, conv_in, conv_w, row(conv_b))
    w_ap, w_cp, w_out, w_up3, w_down = comm.late_weights()
    tw = 1024
    ya = _mm_fwd("attn_proj", attn, w_ap, tb, tw, D_MODEL, outs=[_out2d(s_len, D_MODEL, BF16, tb, tw)])

    def merge_epi(acc, e, o):
        ya_v, ga, gc = [t.astype(F32) for t in e]
        o[0][...] = acc.astype(BF16)
        o[1][...] = (ga * ya_v + gc * acc).astype(BF16)
    gate_a, gate_c = (gates, _tile_spec(tm, tw, 0)), (gates, _tile_spec(tm, tw, 1))
    yc, merged = _mm_fwd("conv_proj", yconv, w_cp, tm, tw, D_MODEL, epilogue=merge_epi,
                         outs=[_out2d(s_len, D_MODEL, BF16, tm, tw), _out2d(s_len, D_MODEL, BF16, tm, tw)],
                         extras=[(ya, _tile_spec(tm, tw)), gate_a, gate_c])

    def res_epi(acc, e, o):
        o[0][...] = e[0] + acc
    x1 = _mm_fwd("out_proj", merged, w_out, tm, tw, D_MODEL, epilogue=res_epi,
                 extras=[(x, _tile_spec(tm, tw))])
    h2 = _rms_fwd("norm2", x1, row(norm2_g))

    def up_epi(acc, e, o):
        o[0][...] = jnp.square(jnp.maximum(acc, 0.0)).astype(BF16)
    act = _mm_fwd("mlp_up", h2, w_up3, tb, tw, D_MODEL, epilogue=up_epi, outs=[_out2d(s_len, D_FF, BF16, tb, tw)])

    def loss_epi(acc, e, o):
        err = e[0] + acc - e[1]
        o[0][...] = (err * (1.0 / D_MODEL)).astype(BF16)
        sq = err * err
        part = sq[:, 0:LANE]
        for c0 in range(LANE, D_MODEL, LANE):
            part = part + sq[:, c0:c0 + LANE]
        o[1][...] = jnp.sum(part.reshape(tl // 8, 8, LANE), axis=0)
    tl = 512
    dy_b, loss_part = _mm_fwd(
        "mlp_down", act, w_down, tl, D_MODEL, D_FF, epilogue=loss_epi,
        outs=[_out2d(s_len, D_MODEL, BF16, tl, D_MODEL),
              (jax.ShapeDtypeStruct((8 * (s_len // tl), LANE), F32), pl.BlockSpec((8, LANE), lambda i, j, k: (i, 0)))],
        extras=[(x1, _tile_spec(tl, D_MODEL)), (target, _tile_spec(tl, D_MODEL))])

    def dup_epi(acc, e, o):
        o[0][...] = (acc * (2.0 * jnp.sqrt(e[0].astype(F32)))).astype(BF16)
    full = lambda cols: pl.BlockSpec((tm, cols), lambda i, j, n: (i, n))
    tall = lambda cols: pl.BlockSpec((tb, cols), lambda i, j, n: (i, n))
    tokens = lambda cols: pl.BlockSpec((s_len, cols), lambda i, j, m: (m, j))
    dup = _mm_bwd_x("d_act", dy_b, tall(D_MODEL), w_down, tb, tw, D_MODEL, s_len, D_MODEL, epilogue=dup_epi,
                    outs=[_out2d(s_len, D_FF, BF16, tb, tw)], extras=[(act, _tile_spec(tb, tw))])
    g_down = _mm_bwd_w("g_down", act, dy_b, tokens(D_MODEL), D_MODEL, 512, D_MODEL, s_len, False)
    g_up = _mm_bwd_w("g_up", h2, dup, tokens(512), D_FF, D_MODEL, 512, s_len, True)
    comm.grads_ready("mlp", dict(w_down=g_down, w_up=g_up))
    dh2 = carrying("d_h2", _d_input, "d_h2", [(dup.reshape(1, s_len, D_FF), w_up3)])
    dx1_b, dg2 = _rms_bwd("norm2_bwd", dh2, x1, row(norm2_g), dy_b)

    def dmerge_epi(acc, e, o):
        ya_v, yc_v, ga, gc = [t.astype(F32) for t in e]
        o[0][...] = (acc * ga).astype(BF16)
        o[1][...] = (acc * gc).astype(BF16)
        o[2][0] = (acc * ya_v * ga * (1.0 - ga)).astype(BF16)
        o[2][1] = (acc * yc_v * gc * (1.0 - gc)).astype(BF16)
    dya, dyc, dgp2 = _mm_bwd_x(
        "d_merged", dx1_b, full(D_MODEL), w_out, tm, tw, D_MODEL, s_len, D_MODEL, epilogue=dmerge_epi,
        outs=[_out2d(s_len, D_MODEL, BF16, tm, tw), _out2d(s_len, D_MODEL, BF16, tm, tw),
              (jax.ShapeDtypeStruct((2, s_len, D_MODEL), BF16), pl.BlockSpec((2, tm, tw), lambda i, j, n: (0, i, j)))],
        extras=[(ya, _tile_spec(tm, tw)), (yc, _tile_spec(tm, tw)), gate_a, gate_c])
    g_out = _mm_bwd_w("g_out", merged, dx1_b, tokens(512), D_MODEL, D_MODEL, 512, s_len, False)
    d_attn = _mm_bwd_x("d_attn", dya, tall(D_MODEL), w_ap, tb, tw, D_MODEL, s_len, D_MODEL,
                       outs=[_out2d(s_len, D_MODEL, BF16, tb, tw)])
    g_ap = _mm_bwd_w("g_attn_proj", attn, dya, tokens(512), D_MODEL, D_MODEL, 512, s_len, False)
    d_yconv = _mm_bwd_x("d_yconv", dyc, tall(D_MODEL), w_cp, tb, tw, D_MODEL, s_len, D_MODEL,
                        outs=[_out2d(s_len, D_MODEL, BF16, tb, tw)])
    g_cp = _mm_bwd_w("g_conv_proj", yconv, dyc, tokens(512), D_MODEL, D_MODEL, 512, s_len, False)
    piece = lambda width: (lambda blk: (blk * width) // D_MODEL, lambda blk: (blk * width % D_MODEL) // width)
    pc, cb = piece(512)
    pieces = pl.BlockSpec((None, s_len, 512), lambda i, j, m: (pc(j), m, cb(j)))
    g_gate = _mm_bwd_w("g_gate", h, dgp2, pieces, 2 * D_MODEL, D_MODEL, 512, s_len, True)
    comm.grads_ready("proj", dict(w_out=g_out, w_attn_proj=g_ap, w_conv_proj=g_cp, w_gate=g_gate))

    dproj6, dbias = carrying("attn_bwd", _attn_bwd, qkn, v, attn, d_attn, bias, lse)
    dproj6, dgq, dgk = _qk_norm_bwd(dproj6, qk_raw, gq, gk)
    dproj6, dconv_wb = carrying("conv_bwd", _conv_bwd, dproj6, d_yconv, conv_in, conv_w, row(conv_b))

    g_in = carrying("g_in", _mm_bwd_w, "g_in", h, dproj6, pieces, 6 * D_MODEL, D_MODEL, 512, s_len, True)
    comm.grads_ready("in", dict(w_in=g_in))
    d_rel = carrying("bias_reduce", _bias_reduce, dbias)
    dh = carrying("d_h", _d_input, "d_h", [(dproj6, w_in3), (dgp2, w_gate3)])
    grad_x, dg1 = carrying("norm1_bwd", _rms_bwd, "norm1_bwd", dh, x, row(norm1_g), dx1_b, out_dtype=F32)

    def bsum(t, f):
        return [], [t[0].astype(F32), t[1].astype(F32)]
    ts = 512
    db_a, db_c = carrying("b_gate_sum", _ew, "b_gate_sum", bsum,
                          [(dgp2, pl.BlockSpec((None, ts, D_MODEL), lambda i: (0, i, 0))),
                           (dgp2, pl.BlockSpec((None, ts, D_MODEL), lambda i: (1, i, 0)))],
                          [], [], sums=[D_MODEL, D_MODEL], ts=ts)

    big = dict(w_in=g_in, w_attn_proj=g_ap, w_conv_proj=g_cp, w_gate=g_gate, w_out=g_out,
               w_up=g_up, w_down=g_down)
    small = dict(norm1_g=dg1, norm2_g=dg2, conv_wb=dconv_wb, b_gate=(db_a, db_c),
                 q_norm_g=dgq, k_norm_g=dgk, rel_bias=d_rel)
    return loss_part, grad_x, big, small


def _finish_loss(loss_part):
    def body(l_ref, lo_ref):
        total = jnp.sum(jnp.sum(l_ref[...], axis=0, keepdims=True), axis=1, keepdims=True)
        lo_ref[...] = jnp.broadcast_to(total * (0.5 / D_MODEL), lo_ref.shape)

    return pl.pallas_call(body, name="finish_loss", out_shape=jax.ShapeDtypeStruct((8, LANE), F32))(loss_part)


HBM_SPEC = pl.BlockSpec(memory_space=pl.ANY)


def _place():
    return lax.axis_index("x"), lax.axis_index("y"), lax.axis_index("c")


def _other_chips(x, y):
    return [(1 - x, y), (x, 1 - y), (1 - x, 1 - y)]


def _cast_into_slot(name, where, w):
    r, cols = w.shape
    ts = 256

    def body(w_ref, x_ref, o_ref):
        o_ref[...] = x_ref[...].astype(o_ref.dtype)

    return pl.pallas_call(
        body, name=name,
        grid_spec=pltpu.PrefetchScalarGridSpec(
            num_scalar_prefetch=1, grid=(r // ts,),
            in_specs=[pl.BlockSpec((ts, cols), lambda i, w: (i, 0))],
            out_specs=pl.BlockSpec((None, ts, cols), lambda i, w: (w[0], i, 0))),
        out_shape=jax.ShapeDtypeStruct((N_SHARD, r, cols), BF16),
        compiler_params=_params("parallel"),
    )(where, w)


def _remote(src, dst, send, recv, k, to):
    return pltpu.make_async_remote_copy(src_ref=src, dst_ref=dst, send_sem=send.at[k], recv_sem=recv.at[k],
                                        device_id=to, device_id_type=MESH)


def _gather_riders(slots):
    n = len(slots)

    def half(refs, w, shard, which):
        hr = slots[w].shape[1] // 2
        return refs[w].at[shard, pl.ds(which * hr, hr)]

    def each(fn):
        x, y, c = _place()
        for w in range(n):
            for j, chip in enumerate(_other_chips(x, y)):
                fn(w, 3 * w + j, 2 * x + y, 2 * chip[0] + chip[1], c, (*chip, c), (x, y, 1 - c))

    def chips_start(_, refs, send, recv):
        each(lambda w, k, me, them, c, peer, sib: _remote(half(refs, w, me, c), half(refs, w, me, c), send, recv, k, peer).start())

    def chips_finish(_, refs, send, recv):
        each(lambda w, k, me, them, c, peer, sib: _remote(half(refs, w, them, c), half(refs, w, them, c), send, recv, k, peer).wait_recv())
        each(lambda w, k, me, them, c, peer, sib: _remote(half(refs, w, me, c), half(refs, w, me, c), send, recv, k, peer).wait_send())

    def sibling_start(_, refs, send, recv):
        each(lambda w, k, me, them, c, peer, sib: _remote(half(refs, w, them, c), half(refs, w, them, c), send, recv, k, sib).start())

    def sibling_finish(_, refs, send, recv):
        each(lambda w, k, me, them, c, peer, sib: _remote(half(refs, w, them, 1 - c), half(refs, w, them, 1 - c), send, recv, k, sib).wait_recv())
        each(lambda w, k, me, them, c, peer, sib: _remote(half(refs, w, them, c), half(refs, w, them, c), send, recv, k, sib).wait_send())

    return (lambda s: _Rider([], s, 3 * n, chips_start, chips_finish),
            lambda s: _Rider([], s, 3 * n, sibling_start, sibling_finish))


def _pair_exchange_rider(grads, landing):
    n = len(grads)

    def copies(srcs, dsts, send, recv):
        x, y, c = _place()
        out = []
        for w in range(n):
            hr = grads[w].shape[1] // 2
            out.append(_remote(srcs[w].at[:, pl.ds((1 - c) * hr, hr)], dsts[w], send, recv, w, (x, y, 1 - c)))
        return out

    def start(srcs, dsts, send, recv):
        for cp in copies(srcs, dsts, send, recv):
            cp.start()

    def finish(srcs, dsts, send, recv):
        for cp in copies(srcs, dsts, send, recv):
            cp.wait()

    return _Rider(grads, landing, n, start, finish)


def _row_tile(hr):
    return min(hr, 256)


def _pair_add(name, where, grad, got):
    _, hr, cols = got.shape
    tr = _row_tile(hr)
    nblk = hr // tr

    def body(w_ref, g_ref, r_ref, o_ref):
        o_ref[...] = (g_ref[...] + r_ref[...]).astype(o_ref.dtype)

    return pl.pallas_call(
        body, name=name,
        grid_spec=pltpu.PrefetchScalarGridSpec(
            num_scalar_prefetch=1, grid=(N_SHARD, nblk),
            in_specs=[pl.BlockSpec((None, tr, cols), lambda s, i, w: (s, w[1] * nblk + i, 0)),
                      pl.BlockSpec((None, tr, cols), lambda s, i, w: (s, i, 0))],
            out_specs=pl.BlockSpec((None, tr, cols), lambda s, i, w: (s, i, 0))),
        out_shape=jax.ShapeDtypeStruct(got.shape, BF16),
        compiler_params=_params("parallel", "parallel"),
    )(where, grad, got)


def _chip_exchange_rider(partials, landing):
    n = len(partials)

    def copies(srcs, dsts, send, recv):
        x, y, c = _place()
        return [_remote(srcs[w].at[2 * chip[0] + chip[1]], dsts[w].at[j], send, recv, 3 * w + j, (*chip, c))
                for w in range(n) for j, chip in enumerate(_other_chips(x, y))]

    def start(srcs, dsts, send, recv):
        for cp in copies(srcs, dsts, send, recv):
            cp.start()

    def finish(srcs, dsts, send, recv):
        for cp in copies(srcs, dsts, send, recv):
            cp.wait()

    return _Rider(partials, landing, 3 * n, start, finish)


def _final_add(name, where, grad, got, arrived):
    _, hr, cols = got.shape
    tr = _row_tile(hr)
    nblk = hr // tr

    def body(w_ref, g_ref, r_ref, a_ref, o_ref):
        acc = g_ref[...] + r_ref[...]
        for j in range(3):
            acc = acc + a_ref[j].astype(F32)
        o_ref[...] = acc

    return pl.pallas_call(
        body, name=name,
        grid_spec=pltpu.PrefetchScalarGridSpec(
            num_scalar_prefetch=1, grid=(nblk,),
            in_specs=[pl.BlockSpec((None, tr, cols), lambda i, w: (w[0], w[1] * nblk + i, 0)),
                      pl.BlockSpec((None, tr, cols), lambda i, w: (w[0], i, 0)),
                      pl.BlockSpec((3, tr, cols), lambda i, w: (0, i, 0))],
            out_specs=pl.BlockSpec((tr, cols), lambda i, w: (w[1] * nblk + i, 0))),
        out_shape=jax.ShapeDtypeStruct((2 * hr, cols), F32),
        compiler_params=_params("parallel"),
    )(where, grad, got, arrived)


def _pair_share_rider(shards):
    n = len(shards)

    def half(refs, w, which):
        hr = shards[w].shape[0] // 2
        return refs[w].at[pl.ds(which * hr, hr)]

    def start(_, refs, send, recv):
        x, y, c = _place()
        for w in range(n):
            _remote(half(refs, w, c), half(refs, w, c), send, recv, w, (x, y, 1 - c)).start()

    def finish(_, refs, send, recv):
        x, y, c = _place()
        for w in range(n):
            _remote(half(refs, w, 1 - c), half(refs, w, 1 - c), send, recv, w, (x, y, 1 - c)).wait_recv()
        for w in range(n):
            _remote(half(refs, w, c), half(refs, w, c), send, recv, w, (x, y, 1 - c)).wait_send()

    return _Rider([], shards, n, start, finish)


class _Exchange:
    PLAN = {"bias_expand": [("early", "gather")], "norm1": [("early", "forward")],
            "proj_qk": [("gate", "gather")], "proj_v": [("gate", "forward")],
            "attn_fwd": [("late", "gather")], "conv_fwd": [("late", "forward")],
            "d_h2": [("mlp", "pair")], "attn_bwd": [("mlp", "chips"), ("proj", "pair")], "conv_bwd": [("mlp", "share")],
            "g_in": [("proj", "chips")], "bias_reduce": [("proj", "share"), ("in", "pair")],
            "d_h": [("in", "chips")], "b_gate_sum": [("in", "share")]}

    def __init__(self, where, early_slots, gate_slots, late_slots):
        self.where = where
        self.slots = dict(early=early_slots, gate=gate_slots, late=late_slots)
        self.stage = {g: dict(zip(("gather", "forward"), _gather_riders(s))) for g, s in self.slots.items()}
        self.groups, self.reduced, self.pending = {}, {}, []

    def early_weights(self):
        w_in3, small = self.slots["early"]
        conv_w = small[:, :3, :].transpose(1, 0, 2).reshape(3, N_SHARD * small.shape[2])
        return w_in3, conv_w

    def gate_weight(self):
        return self.slots["gate"][0]

    def late_weights(self):
        rows = lambda a: a.reshape(a.shape[0] * a.shape[1], a.shape[2])
        w_ap, w_cp, w_out, w_up3, w_down = self.slots["late"]
        return rows(w_ap), rows(w_cp), rows(w_out), w_up3, rows(w_down)

    def grads_ready(self, group, grads):
        names = list(grads)
        g4 = [g if g.ndim == 3 else g.reshape(N_SHARD, -1, g.shape[1]) for g in grads.values()]
        self.groups[group] = dict(names=names, g4=g4)

    def riders(self, at):
        self.pending = self.PLAN.get(at, [])
        out = []
        for group, stage in self.pending:
            if group in self.slots:
                out.append(self.stage[group][stage](self.slots[group]))
                continue
            st = self.groups[group]
            if stage == "pair":
                landing = [lax.empty((N_SHARD, g.shape[1] // 2, g.shape[2]), F32) for g in st["g4"]]
                out.append(_pair_exchange_rider(st["g4"], landing))
            elif stage == "chips":
                landing = [lax.empty((3,) + p.shape[1:], p.dtype) for p in st["partial"]]
                out.append(_chip_exchange_rider(st["partial"], landing))
            else:
                out.append(_pair_share_rider(st["halves"]))
        return out

    def done(self, at, carried):
        for (group, stage), arrays in zip(self.pending, carried):
            if group in self.slots:
                self.slots[group] = arrays
                continue
            st = self.groups[group]
            tag = lambda what, n: what + "_" + n
            if stage == "pair":
                st["got"] = arrays
                st["partial"] = [_pair_add(tag("pair_add", n), self.where, g, r)
                                 for n, g, r in zip(st["names"], st["g4"], arrays)]
            elif stage == "chips":
                st["halves"] = [_final_add(tag("final_add", n), self.where, g, r, a)
                                for n, g, r, a in zip(st["names"], st["g4"], st["got"], arrays)]
            else:
                self.reduced.update(zip(st["names"], arrays))


SMALL_ROWS = 32
N_DEV = 8


def _all_reduce_small(pack):
    def body(p_ref, o_ref, buf, send, recv):
        x, y, c = _place()
        buf[4 * x + 2 * y + c] = p_ref[...]
        copies, waits = [], []
        for k in range(1, N_DEV):
            px = 1 - x if k & 4 else x
            py = 1 - y if k & 2 else y
            pc = 1 - c if k & 1 else c
            copies.append(pltpu.make_async_remote_copy(
                src_ref=p_ref, dst_ref=buf.at[4 * x + 2 * y + c], send_sem=send.at[k - 1],
                recv_sem=recv.at[k - 1], device_id=(px, py, pc), device_id_type=MESH))
            waits.append(pltpu.make_async_remote_copy(
                src_ref=p_ref, dst_ref=buf.at[4 * px + 2 * py + pc], send_sem=send.at[k - 1],
                recv_sem=recv.at[k - 1], device_id=(px, py, pc), device_id_type=MESH))
        for cp in copies:
            cp.start()
        for cp in waits:
            cp.wait_recv()
        acc = buf[0]
        for d in range(1, N_DEV):
            acc = acc + buf[d]
        o_ref[...] = acc
        for cp in copies:
            cp.wait_send()

    return pl.pallas_call(
        body, name="all_reduce_small",
        out_shape=jax.ShapeDtypeStruct(pack.shape, F32),
        scratch_shapes=[pltpu.VMEM((N_DEV,) + pack.shape, F32),
                        pltpu.SemaphoreType.DMA((N_DEV - 1,)), pltpu.SemaphoreType.DMA((N_DEV - 1,))],
    )(pack)


def _adamw(name, w, g, m, v):
    c1 = 1.0 - ADAM_B1 ** ADAM_STEP
    c2 = 1.0 - ADAM_B2 ** ADAM_STEP

    def fn(t, f):
        wv, gv, mv, vv = t
        m2 = ADAM_B1 * mv + (1.0 - ADAM_B1) * gv
        v2 = ADAM_B2 * vv + (1.0 - ADAM_B2) * (gv * gv)
        delta = -ADAM_LR * ((m2 / c1) / (jnp.sqrt(v2 / c2) + ADAM_EPS) + ADAM_WD * wv)
        return [delta, m2, v2], []

    cols = w.shape[1]
    return _ew(name, fn, [w, g, m, v], [], [(cols, F32)] * 3, ts=min(w.shape[0], 256))


LOSS_ROW = 26


def _pack_small(norm1_g, norm2_g, conv_b, b_gate, conv_w, q_norm_g, k_norm_g, rel_bias, loss=None):
    pack = jnp.zeros((SMALL_ROWS, D_MODEL), F32)
    for r0, v in ((0, norm1_g), (1, norm2_g), (2, conv_b), (3, b_gate.reshape(2, D_MODEL)), (5, conv_w),
                  (8, q_norm_g), (9, k_norm_g), (10, rel_bias)) + (((LOSS_ROW, loss),) if loss is not None else ()):
        v = v.reshape(-1, v.shape[-1]).astype(F32)
        pack = pack.at[r0:r0 + v.shape[0], :v.shape[1]].set(v)
    return pack


def _unpack_small(pack, conv_cols):
    return dict(norm1_g=pack[0], norm2_g=pack[1], conv_b=pack[2], b_gate=pack[3:5].reshape(2 * D_MODEL),
                conv_w=pack[5:8, :conv_cols], q_norm_g=pack[8, :HEAD_DIM], k_norm_g=pack[9, :HEAD_DIM],
                rel_bias=pack[10:10 + N_HEADS, :N_REL])


BIG = ["w_in", "w_attn_proj", "w_conv_proj", "w_gate", "w_out", "w_up", "w_down"]
LATE = ["w_attn_proj", "w_conv_proj", "w_out", "w_up", "w_down"]
WEIGHTS = ["norm1_g", "w_in", "q_norm_g", "k_norm_g", "rel_bias", "conv_w", "conv_b", "w_attn_proj",
           "w_conv_proj", "w_gate", "b_gate", "w_out", "norm2_g", "w_up", "w_down"]


def kernel(x, norm1_g, w_in, q_norm_g, k_norm_g, rel_bias, conv_w, conv_b, w_attn_proj, w_conv_proj, w_gate, b_gate, w_out, norm2_g, w_up, w_down, loss_target, m_norm1_g, m_w_in, m_q_norm_g, m_k_norm_g, m_rel_bias, m_conv_w, m_conv_b, m_w_attn_proj, m_w_conv_proj, m_w_gate, m_b_gate, m_w_out, m_norm2_g, m_w_up, m_w_down, v_norm1_g, v_w_in, v_q_norm_g, v_k_norm_g, v_rel_bias, v_conv_w, v_conv_b, v_w_attn_proj, v_w_conv_proj, v_w_gate, v_b_gate, v_w_out, v_norm2_g, v_w_up, v_w_down):
    given = dict(locals())
    w = {n: given[n] for n in WEIGHTS}
    m = {n: given["m_" + n] for n in WEIGHTS}
    v = {n: given["v_" + n] for n in WEIGHTS}
    s_len = x.shape[1]
    shard = 2 * lax.axis_index("x") + lax.axis_index("y")
    where = jnp.stack([shard, lax.axis_index("c")]).astype(jnp.int32)
    conv_cols = conv_w.shape[1]

    small_in = lax.dynamic_update_slice(jnp.zeros((N_SHARD, 16, conv_cols), F32), conv_w[None], (shard, 0, 0))
    slot = {n: _cast_into_slot("cast_" + n, where, w[n]) for n in BIG}
    comm = _Exchange(where, [slot["w_in"], small_in], [slot["w_gate"]], [slot[n] for n in LATE])

    loss_part, grad_x, _, small = _local_grads(
        x.reshape(s_len, D_MODEL), loss_target.reshape(s_len, D_MODEL), norm1_g, q_norm_g, k_norm_g,
        rel_bias, conv_b, b_gate, norm2_g, comm)
    grad = dict(comm.reduced)

    loss_local = _finish_loss(loss_part)
    pack = _pack_small(small["norm1_g"], small["norm2_g"], small["conv_wb"][3], jnp.concatenate(small["b_gate"], axis=1),
                       small["conv_wb"][0:3], small["q_norm_g"], small["k_norm_g"], small["rel_bias"],
                       loss=loss_local[0:1, :])
    total = _all_reduce_small(pack)
    g_small = _unpack_small(total, D_MODEL)
    g_small["conv_w"] = lax.dynamic_slice(g_small["conv_w"], (0, shard * conv_cols), (3, conv_cols))
    grad.update(g_small)

    delta, new_m, new_v = {}, {}, {}
    for n in BIG:
        delta[n], new_m[n], new_v[n] = _adamw("adamw_" + n, w[n], grad[n], m[n], v[n])
    small_names = [n for n in WEIGHTS if n not in BIG]
    packs = [_pack_small(**{n: src[n] for n in small_names}) for src in (w, grad, m, v)]
    for out, packed in zip((delta, new_m, new_v), _adamw("adamw_small", *packs)):
        out.update({n: a.reshape(w[n].shape) for n, a in _unpack_small(packed, conv_cols).items()})

    outs = [total[LOSS_ROW, 0], grad_x.reshape(x.shape)]
    for group in (grad, delta, new_m, new_v):
        outs += [group[n].reshape(w[n].shape) for n in WEIGHTS]
    return tuple(outs)
```

```python
import functools

import jax
import jax.numpy as jnp
from jax import lax
from jax.experimental import pallas as pl
from jax.experimental.pallas import tpu as pltpu

F32 = jnp.float32
BF16 = jnp.bfloat16

D_MODEL = 1024
N_HEADS = 16
HEAD_DIM = 64
CHUNK = 64
N_PREV_CHUNKS = 8
MAX_REL = 256
D_FF = 4096
N_REL = 2 * MAX_REL + 1
REL_PAD = 640
EPS = 1e-6
NEG_INF = -1e30
QK_SCALE = HEAD_DIM ** -0.5

SUPER = 4 * CHUNK
BAND = SUPER + N_PREV_CHUNKS * CHUNK
SKEW_W = 1024
N_SHARD = 4
LANE = 128
VMEM_LIMIT = 48 * 1024 * 1024

ADAM_LR = 0.001
ADAM_B1 = 0.9
ADAM_B2 = 0.999
ADAM_EPS = 1e-08
ADAM_WD = 0.01
ADAM_STEP = 10

MESH = pl.DeviceIdType.MESH
NN = (((1,), (0,)), ((), ()))
NT = (((1,), (1,)), ((), ()))
TN = (((0,), (0,)), ((), ()))


def _params(*sem):
    return pltpu.CompilerParams(dimension_semantics=sem or None, vmem_limit_bytes=VMEM_LIMIT)


HBM_SPEC = pl.BlockSpec(memory_space=pl.ANY)


class _Rider:
    def __init__(self, sources, arrays, n_sem, start, finish):
        self.sources, self.arrays, self.n_sem, self.start, self.finish = sources, arrays, n_sem, start, finish


def _carry(riders, body, *, name, out_shape, grid=(), in_specs=None, out_specs=None, scratch_shapes=(),
           semantics=(), input_output_aliases=None):
    aliases = dict(input_output_aliases or {})
    if not riders:
        kw = {} if in_specs is None else dict(in_specs=in_specs, out_specs=out_specs)
        return pl.pallas_call(body, name=name, grid=grid, out_shape=out_shape, scratch_shapes=scratch_shapes,
                              input_output_aliases=aliases, compiler_params=_params(*semantics), **kw)
    single = not isinstance(out_shape, (list, tuple))
    shapes = [out_shape] if single else list(out_shape)
    n_out, n_scr = len(shapes), len(scratch_shapes)
    srcs = [a for r in riders for a in r.sources]
    arrs = [a for r in riders for a in r.arrays]
    vmem = pl.BlockSpec(memory_space=pltpu.VMEM)

    def run(*args):
        n_in = len(args)

        def wrapped(*refs):
            pos = n_in
            src_refs = refs[pos:pos + len(srcs)]
            pos += len(srcs) + len(arrs)
            outs = refs[pos:pos + n_out]
            pos += n_out
            arr_refs = refs[pos:pos + len(arrs)]
            pos += len(arrs)
            scratch = refs[pos:pos + n_scr]
            sems = refs[pos + n_scr:]
            first, last = True, True
            for d, size in enumerate(grid):
                first = jnp.logical_and(first, pl.program_id(d) == 0)
                last = jnp.logical_and(last, pl.program_id(d) == size - 1)

            def each(method):
                s0 = a0 = 0
                for k, r in enumerate(riders):
                    getattr(r, method)(src_refs[s0:s0 + len(r.sources)], arr_refs[a0:a0 + len(r.arrays)],
                                       sems[2 * k], sems[2 * k + 1])
                    s0, a0 = s0 + len(r.sources), a0 + len(r.arrays)

            pl.when(first)(lambda: each("start"))
            body(*refs[:n_in], *outs, *scratch)
            pl.when(last)(lambda: each("finish"))

        ins = [vmem] * n_in if in_specs is None else list(in_specs)
        if out_specs is None:
            o_specs = [vmem] * n_out
        else:
            o_specs = [out_specs] if single else list(out_specs)
        for k in range(len(arrs)):
            aliases[n_in + len(srcs) + k] = n_out + k
        res = pl.pallas_call(
            wrapped, name=name, grid=grid,
            in_specs=ins + [HBM_SPEC] * (len(srcs) + len(arrs)),
            out_specs=o_specs + [HBM_SPEC] * len(arrs),
            out_shape=shapes + [jax.ShapeDtypeStruct(a.shape, a.dtype) for a in arrs],
            scratch_shapes=list(scratch_shapes) + [pltpu.SemaphoreType.DMA((r.n_sem,)) for r in riders for _ in range(2)],
            input_output_aliases=aliases,
            compiler_params=_params(*["arbitrary"] * len(grid)),
        )(*args, *srcs, *arrs)
        core, rest = res[:n_out], list(res[n_out:])
        carried, a0 = [], 0
        for r in riders:
            carried.append(rest[a0:a0 + len(r.arrays)])
            a0 += len(r.arrays)
        return (core[0] if single else core), carried

    return run


def _mm(name, dims, a, a_spec, b, b_spec, grid, tile, outs, epilogue=None, extras=(), riders=()):
    nk, ne, no = grid[2], len(extras), len(outs)

    def body(a_ref, b_ref, *refs):
        e_refs, o_refs = refs[:ne], refs[ne:ne + no]
        part = lax.dot_general(a_ref[...], b_ref[...], dims, preferred_element_type=F32)

        def finish(acc):
            if epilogue is None:
                o_refs[0][...] = acc.astype(o_refs[0].dtype)
            else:
                epilogue(acc, [r[...] for r in e_refs], o_refs)

        if nk == 1:
            finish(part)
        else:
            acc_ref = refs[ne + no]
            k = pl.program_id(2)

            @pl.when(k == 0)
            def _():
                acc_ref[...] = part

            @pl.when(k > 0)
            def _():
                acc_ref[...] += part

            @pl.when(k == nk - 1)
            def _():
                finish(acc_ref[...])

    res = _carry(
        riders, body, name=name, grid=grid,
        in_specs=[a_spec, b_spec] + [s for _, s in extras],
        out_specs=[s for _, s in outs],
        out_shape=[s for s, _ in outs],
        scratch_shapes=[pltpu.VMEM(tile, F32)] if nk > 1 else [],
        semantics=("parallel", "parallel", "arbitrary"),
    )(a, b, *[e for e, _ in extras])
    res, carried = res if riders else (res, None)
    res = res[0] if no == 1 else res
    return (res, carried) if riders else res


def _tile_spec(tm, tn, col0=0):
    return pl.BlockSpec((tm, tn), lambda i, j, k: (i, j + col0))


def _out2d(m, n, dtype, tm, tn):
    return (jax.ShapeDtypeStruct((m, n), dtype), _tile_spec(tm, tn))


def _mm_fwd(name, a, w, tm, tn, tk, outs=None, epilogue=None, extras=(), col0=0, ncols=None, riders=()):
    m, kdim = a.shape
    if w.ndim == 3:
        per = w.shape[2] // tn
        n = ncols or N_SHARD * w.shape[2]
        w_spec = pl.BlockSpec((None, tk, tn), lambda i, j, k: ((j + col0) // per, k, (j + col0) % per))
    else:
        n = ncols or w.shape[1]
        w_spec = pl.BlockSpec((tk, tn), lambda i, j, k: (k, j + col0))
    if outs is None:
        outs = [_out2d(m, n, F32, tm, tn)]
    return _mm(name, NN, a, pl.BlockSpec((tm, tk), lambda i, j, k: (i, k)), w, w_spec,
               (m // tm, n // tn, kdim // tk), (tm, tn), outs, epilogue, extras, riders)


def _mm_bwd_x(name, g, g_spec, w, tm, tj, tc, m, n_contract, outs=None, epilogue=None, extras=(), riders=()):
    if w.ndim == 3:
        per = w.shape[2] // tc
        kdim = w.shape[1]
        w_spec = pl.BlockSpec((None, tj, tc), lambda i, j, n: (n // per, j, n % per))
    else:
        kdim = w.shape[0]
        w_spec = pl.BlockSpec((tj, tc), lambda i, j, n: (j, n))
    if outs is None:
        outs = [_out2d(m, kdim, F32, tm, tj)]
    return _mm(name, NT, g, g_spec, w, w_spec, (m // tm, kdim // tj, n_contract // tc),
               (tm, tj), outs, epilogue, extras, riders)


def _mm_bwd_w(name, a, g, g_spec, n, tk, tn, tm, sharded, riders=()):
    m, kdim = a.shape
    if sharded:
        per = (n // N_SHARD) // tn
        out = (jax.ShapeDtypeStruct((N_SHARD, kdim, n // N_SHARD), F32),
               pl.BlockSpec((None, tk, tn), lambda i, j, mm: (j // per, i, j % per)))
    else:
        out = (jax.ShapeDtypeStruct((kdim, n), F32), pl.BlockSpec((tk, tn), lambda i, j, mm: (i, j)))
    return _mm(name, TN, a, pl.BlockSpec((tm, tk), lambda i, j, mm: (mm, i)), g, g_spec,
               (kdim // tk, n // tn, m // tm), (tk, tn), [out], riders=riders)


def _ew(name, fn, tiles, fulls, outs, sums=(), ts=512, riders=()):
    tiles = [t if isinstance(t, tuple) else (t, pl.BlockSpec((ts, t.shape[1]), lambda i: (i, 0)))
             for t in tiles]
    s_rows = tiles[0][0].shape[-2]
    nt, nf, no = len(tiles), len(fulls), len(outs)

    def body(*refs):
        t_vals = [r[...] for r in refs[:nt]]
        f_vals = [r[...] for r in refs[nt:nt + nf]]
        o_refs, s_refs = refs[nt + nf:nt + nf + no], refs[nt + nf + no:]
        o_vals, s_vals = fn(t_vals, f_vals)
        for r, v in zip(o_refs, o_vals):
            r[...] = v.astype(r.dtype)
        for r, v in zip(s_refs, s_vals):
            part = jnp.sum(v, axis=0, keepdims=True)

            @pl.when(pl.program_id(0) == 0)
            def _():
                r[...] = part

            @pl.when(pl.program_id(0) > 0)
            def _():
                r[...] += part

    full_specs = [pl.BlockSpec(f.shape, lambda i, nd=f.ndim: (0,) * nd) for f in fulls]
    return _carry(
        riders, body, name=name, grid=(s_rows // ts,),
        in_specs=[s for _, s in tiles] + full_specs,
        out_specs=[pl.BlockSpec((ts, c), lambda i: (i, 0)) for c, _ in outs]
        + [pl.BlockSpec((1, c), lambda i: (0, 0)) for c in sums],
        out_shape=[jax.ShapeDtypeStruct((s_rows, c), dt) for c, dt in outs]
        + [jax.ShapeDtypeStruct((1, c), F32) for c in sums],
        semantics=("arbitrary",),
    )(*[t for t, _ in tiles], *fulls)


def _rms_fwd(name, x, g, riders=()):
    def fn(t, f):
        xv = t[0]
        r = lax.rsqrt(jnp.mean(xv * xv, axis=-1, keepdims=True) + EPS)
        return [xv * r * f[0]], []
    out = _ew(name, fn, [x], [g], [(x.shape[1], BF16)], riders=riders)
    return (out[0][0], out[1]) if riders else out[0]


def _rms_bwd(name, dh, x, g, dres, out_dtype=BF16, riders=()):
    def fn(t, f):
        dhv, xv, dr = [v.astype(F32) for v in t]
        r = lax.rsqrt(jnp.mean(xv * xv, axis=-1, keepdims=True) + EPS)
        xhat = xv * r
        u = dhv * f[0]
        dx = r * (u - xhat * jnp.mean(u * xhat, axis=-1, keepdims=True)) + dr
        return [dx], [dhv * xhat]
    c = x.shape[1]
    return _ew(name, fn, [dh, x, dres], [g], [(c, out_dtype)], sums=[c], riders=riders)


def _split3(x):
    x1 = x.astype(BF16)
    r1 = x - x1.astype(F32)
    x2 = r1.astype(BF16)
    x3 = (r1 - x2.astype(F32)).astype(BF16)
    return x1, x2, x3


def _rel_class(cp):
    far = (cp < MAX_REL) | (cp > BAND)
    return jnp.where(far, 2 * MAX_REL, BAND - cp)


def _skew_rows(x, sign):
    row = lax.broadcasted_iota(jnp.int32, x.shape, 0)
    for b in range(CHUNK.bit_length() - 1):
        shift = (1 << b) if sign > 0 else SKEW_W - (1 << b)
        x = jnp.where((row >> b) & 1 == 1, pltpu.roll(x, shift, 1), x)
    return x


def _roll_lanes(x, shift):
    return x if shift % SKEW_W == 0 else pltpu.roll(x, shift % SKEW_W, 1)


N_START = 3


def _bias_expand(rel_bias, riders=()):
    rel = jnp.pad(rel_bias, ((0, 0), (0, REL_PAD - N_REL))).reshape(N_HEADS, 1, REL_PAD)

    def body(rel_ref, o_ref):
        cls = lax.broadcasted_iota(jnp.int32, (REL_PAD, SKEW_W), 0)
        cp = lax.broadcasted_iota(jnp.int32, (REL_PAD, SKEW_W), 1)
        onehot = (cls == _rel_class(cp)).astype(BF16)
        rel8 = jnp.broadcast_to(rel_ref[...], (8, REL_PAD))
        trow = sum(jnp.dot(p, onehot, preferred_element_type=F32) for p in _split3(rel8))[0:1]
        first = _skew_rows(jnp.broadcast_to(trow, (CHUNK, SKEW_W)), +1)
        full = jnp.concatenate([_roll_lanes(first, CHUNK * g) for g in range(SUPER // CHUNK)], axis=0)[:, :BAND]
        qc = lax.broadcasted_iota(jnp.int32, (SUPER, BAND), 0) // CHUNK
        kc = lax.broadcasted_iota(jnp.int32, (SUPER, BAND), 1) // CHUNK
        on_band = (kc >= qc) & (kc <= qc + N_PREV_CHUNKS)
        table = jnp.where(on_band, full, NEG_INF).T
        key = lax.broadcasted_iota(jnp.int32, (BAND, SUPER), 0)
        for t in range(N_START):
            o_ref[t] = jnp.where(key < (N_START - 1 - t) * SUPER, NEG_INF, table)

    return _carry(
        riders, body, name="bias_expand", grid=(N_HEADS,),
        in_specs=[pl.BlockSpec((None, 1, REL_PAD), lambda h: (h, 0, 0))],
        out_specs=pl.BlockSpec((N_START, None, BAND, SUPER), lambda h: (0, h, 0, 0)),
        out_shape=jax.ShapeDtypeStruct((N_START, N_HEADS, BAND, SUPER), F32),
        semantics=("arbitrary",),
    )(rel)


def _bias_reduce(dbias, riders=()):
    def body(d_ref, o_ref):
        x = jnp.concatenate([d_ref[...].T, jnp.zeros((SUPER, SKEW_W - BAND), F32)], axis=1)
        folded = sum(_roll_lanes(x[CHUNK * g:CHUNK * (g + 1)], -CHUNK * g) for g in range(SUPER // CHUNK))
        diag = jnp.sum(_skew_rows(folded, -1), axis=0, keepdims=True)
        cp = lax.broadcasted_iota(jnp.int32, (SKEW_W, REL_PAD), 0)
        cls = lax.broadcasted_iota(jnp.int32, (SKEW_W, REL_PAD), 1)
        onehot = (cls == _rel_class(cp)).astype(BF16)
        diag8 = jnp.broadcast_to(diag, (8, SKEW_W))
        o_ref[...] = sum(jnp.dot(p, onehot, preferred_element_type=F32) for p in _split3(diag8))[0:1]

    out = _carry(
        riders, body, name="bias_reduce", grid=(N_HEADS,),
        in_specs=[pl.BlockSpec((None, BAND, SUPER), lambda h: (h, 0, 0))],
        out_specs=pl.BlockSpec((None, 1, REL_PAD), lambda h: (h, 0, 0)),
        out_shape=jax.ShapeDtypeStruct((N_HEADS, 1, REL_PAD), F32),
        semantics=("arbitrary",),
    )(dbias)
    out, carried = out if riders else (out, None)
    out = out.reshape(N_HEADS, REL_PAD)[:, :N_REL]
    return (out, carried) if riders else out


HEADS_PER_STEP = 8
HEAD_COLS = HEADS_PER_STEP * HEAD_DIM
N_HEAD_GROUPS = N_HEADS // HEADS_PER_STEP


def _unit(x):
    r = lax.rsqrt(jnp.mean(x * x, axis=-1, keepdims=True) + EPS)
    return x * r, r


def _scores_t(qs, kn, bias_t):
    return jnp.concatenate([lax.dot_general(k, qs, NT, preferred_element_type=F32) for k in kn], axis=0) + bias_t


def _bias_spec():
    return pl.BlockSpec((None, HEADS_PER_STEP, BAND, SUPER), lambda hg, i: (jnp.minimum(i, N_START - 1), hg, 0, 0))


def _band_specs(nb, col0, clamp_hi):
    def spec(d):
        def index(hg, i):
            blk = jnp.maximum(i - d, 0)
            if clamp_hi:
                blk = jnp.minimum(blk, nb - 1)
            return (blk, col0 + hg)
        return pl.BlockSpec((SUPER, HEAD_COLS), index)
    return [spec(2), spec(1), spec(0)]


def _head_sums(y):
    cols = y.shape[1]
    same_head = (lax.broadcasted_iota(jnp.int32, (cols, cols), 0) // HEAD_DIM
                 == lax.broadcasted_iota(jnp.int32, (cols, cols), 1) // HEAD_DIM).astype(BF16)
    hi = y.astype(BF16)
    lo = (y - hi.astype(F32)).astype(BF16)
    return (jnp.dot(hi, same_head, preferred_element_type=F32)
            + jnp.dot(lo, same_head, preferred_element_type=F32))


def _head_unit(x):
    r = lax.rsqrt(_head_sums(x * x) * (1.0 / HEAD_DIM) + EPS)
    return x * r, r


def _head(hh):
    return slice(HEAD_DIM * hh, HEAD_DIM * (hh + 1))


def _key_block(j):
    return slice(SUPER * j, SUPER * (j + 1))


LSE_ROWS = 8


def _attn_fwd(qkn, v, bias, riders=()):
    s_len = qkn.shape[0]
    nb = s_len // SUPER

    def body(q_ref, k0, k1, k2, v0, v1, v2, b_ref, o_ref, lse_ref):
        outs = []

        def probabilities(hh):
            sl = _head(hh)
            s = _scores_t(q_ref[:, sl], [k0[:, sl], k1[:, sl], k2[:, sl]], b_ref[hh])
            m = jnp.max(s, axis=0, keepdims=True)
            e = jnp.exp(s - m)
            l = jnp.sum(e, axis=0, keepdims=True)
            lse_ref[hh:hh + 1, :] = m + jnp.log(l)
            return (e * (1.0 / l)).astype(BF16), sl

        def weighted_values(p, sl):
            outs.append(sum(lax.dot_general(p[_key_block(j), :], vj[:, sl], TN, preferred_element_type=F32)
                            for j, vj in enumerate((v0, v1, v2))))

        ready = probabilities(0)
        for hh in range(1, HEADS_PER_STEP):
            following = probabilities(hh)
            weighted_values(*ready)
            ready = following
        weighted_values(*ready)
        o_ref[...] = jnp.concatenate(outs, axis=1).astype(o_ref.dtype)

    return _carry(
        riders, body, name="attn_fwd", grid=(N_HEAD_GROUPS, nb),
        in_specs=[pl.BlockSpec((SUPER, HEAD_COLS), lambda hg, i: (i, hg))]
        + _band_specs(nb, N_HEAD_GROUPS, False) + _band_specs(nb, 0, False) + [_bias_spec()],
        out_specs=[pl.BlockSpec((SUPER, HEAD_COLS), lambda hg, i: (i, hg)),
                   pl.BlockSpec((None, LSE_ROWS, SUPER), lambda hg, i: (hg, 0, i))],
        out_shape=[jax.ShapeDtypeStruct((s_len, D_MODEL), BF16),
                   jax.ShapeDtypeStruct((N_HEAD_GROUPS, LSE_ROWS, s_len), F32)],
        semantics=("parallel", "arbitrary"),
    )(qkn, qkn, qkn, qkn, v, v, v, bias)


def _attn_bwd(qkn, v, out, d_out, bias, lse, riders=()):
    s_len = qkn.shape[0]
    nb = s_len // SUPER

    def body(q_ref, k0, k1, k2, v0, v1, v2, o_ref, do_ref, b_ref, lse_ref, dp_ref, db_ref, aq_ref, ak_ref, av_ref):
        i = pl.program_id(1)

        @pl.when(i == 0)
        def _():
            aq_ref[...] = jnp.zeros_like(aq_ref)
            ak_ref[...] = jnp.zeros_like(ak_ref)
            av_ref[...] = jnp.zeros_like(av_ref)
            db_ref[...] = jnp.zeros_like(db_ref)

        @pl.when(i < nb)
        def _():
            dq, dk, dv = [], [[], [], []], [[], [], []]
            ones = jnp.ones((8, HEAD_DIM), BF16)

            def softmax_grad(hh):
                sl = _head(hh)
                qs, do = q_ref[:, sl], do_ref[:, sl]
                kn = [k0[:, sl], k1[:, sl], k2[:, sl]]
                prod = do.astype(F32) * o_ref[:, sl].astype(F32)
                hi = prod.astype(BF16)
                lo = (prod - hi.astype(F32)).astype(BF16)
                delta = (lax.dot_general(ones, hi, NT, preferred_element_type=F32)
                         + lax.dot_general(ones, lo, NT, preferred_element_type=F32))[0:1]
                lse_row = lse_ref[hh:hh + 1, :]
                pb, dsb = [], []
                for j, (kj, vj) in enumerate(zip(kn, (v0, v1, v2))):
                    rows = _key_block(j)
                    p = jnp.exp(lax.dot_general(kj, qs, NT, preferred_element_type=F32) + b_ref[hh, rows, :] - lse_row)
                    ds = p * (lax.dot_general(vj[:, sl], do, NT, preferred_element_type=F32) - delta)
                    db_ref[hh, rows, :] += ds
                    pb.append(p.astype(BF16))
                    dsb.append(ds.astype(BF16))
                return pb, dsb, qs, do, kn

            def operand_grads(pb, dsb, qs, do, kn):
                dq.append(sum(lax.dot_general(dsb[j], kn[j], TN, preferred_element_type=F32) for j in range(3)))
                for j in range(3):
                    dv[j].append(jnp.dot(pb[j], do, preferred_element_type=F32))
                    dk[j].append(jnp.dot(dsb[j], qs, preferred_element_type=F32))

            ready = softmax_grad(0)
            for hh in range(1, HEADS_PER_STEP):
                following = softmax_grad(hh)
                operand_grads(*ready)
                ready = following
            operand_grads(*ready)
            aq_ref[i % 3] = jnp.concatenate(dq, axis=1)
            for j in range(3):
                slot = (i + 1 + j) % 3
                if j < 2:
                    ak_ref[slot] += jnp.concatenate(dk[j], axis=1)
                    av_ref[slot] += jnp.concatenate(dv[j], axis=1)
                else:
                    ak_ref[slot] = jnp.concatenate(dk[j], axis=1)
                    av_ref[slot] = jnp.concatenate(dv[j], axis=1)

        slot = (i + 1) % 3
        dp_ref[0] = aq_ref[slot].astype(dp_ref.dtype)
        dp_ref[1] = ak_ref[slot].astype(dp_ref.dtype)
        dp_ref[2] = av_ref[slot].astype(dp_ref.dtype)

    def qrow(hg, i):
        return (jnp.minimum(i, nb - 1), hg)

    return _carry(
        riders, body, name="attn_bwd", grid=(N_HEAD_GROUPS, nb + 2),
        in_specs=[pl.BlockSpec((SUPER, HEAD_COLS), qrow)]
        + _band_specs(nb, N_HEAD_GROUPS, True) + _band_specs(nb, 0, True)
        + [pl.BlockSpec((SUPER, HEAD_COLS), qrow), pl.BlockSpec((SUPER, HEAD_COLS), qrow), _bias_spec(),
           pl.BlockSpec((None, LSE_ROWS, SUPER), lambda hg, i: (hg, 0, jnp.minimum(i, nb - 1)))],
        out_specs=[pl.BlockSpec((3, SUPER, HEAD_COLS), lambda hg, i: (0, jnp.maximum(i - 2, 0), hg)),
                   pl.BlockSpec((HEADS_PER_STEP, BAND, SUPER), lambda hg, i: (hg, 0, 0))],
        out_shape=[jax.ShapeDtypeStruct((6, s_len, D_MODEL), BF16),
                   jax.ShapeDtypeStruct((N_HEADS, BAND, SUPER), F32)],
        scratch_shapes=[pltpu.VMEM((3, SUPER, HEAD_COLS), F32)] * 3,
        semantics=("parallel", "arbitrary"),
    )(qkn, qkn, qkn, qkn, v, v, v, out, d_out, bias, lse)


def _qk_norm_bwd(dproj6, qk_raw, gq, gk):
    s_len = qk_raw.shape[0]
    ts = min(1024, s_len)

    nsteps = s_len // ts
    half = D_MODEL // 2

    def body(d_ref, raw_ref, gq_ref, gk_ref, o_ref, dgq_ref, dgk_ref, acc_ref):
        step = pl.program_id(0)

        @pl.when(step == 0)
        def _():
            acc_ref[...] = jnp.zeros_like(acc_ref)

        for piece, (g_ref, scale) in enumerate(((gq_ref, QK_SCALE), (gk_ref, 1.0))):
            for c0 in (0, half):
                xhat, r = _head_unit(raw_ref[:, piece * D_MODEL + c0:piece * D_MODEL + c0 + half].astype(F32))
                dn = d_ref[piece, :, c0:c0 + half].astype(F32) * scale
                u = dn * g_ref[...]
                dx = r * (u - xhat * (_head_sums(u * xhat) * (1.0 / HEAD_DIM)))
                o_ref[piece, :, c0:c0 + half] = dx.astype(o_ref.dtype)
                acc_ref[piece:piece + 1, c0:c0 + half] += jnp.sum(dn * xhat, axis=0, keepdims=True)

        @pl.when(step == nsteps - 1)
        def _():
            lane = lax.broadcasted_iota(jnp.int32, (D_MODEL, LANE), 0) % HEAD_DIM
            fold = (lane == lax.broadcasted_iota(jnp.int32, (D_MODEL, LANE), 1)).astype(BF16)
            tot = sum(jnp.dot(p, fold, preferred_element_type=F32) for p in _split3(acc_ref[...]))
            dgq_ref[...] = tot[0:1, :HEAD_DIM]
            dgk_ref[...] = tot[1:2, :HEAD_DIM]

    gain = pl.BlockSpec((1, half), lambda i: (0, 0))
    small = pl.BlockSpec((1, HEAD_DIM), lambda i: (0, 0))
    per_head = lambda g: jnp.tile(g, (1, half // HEAD_DIM))
    return pl.pallas_call(
        body, name="qk_norm_bwd", grid=(nsteps,),
        in_specs=[pl.BlockSpec((2, ts, D_MODEL), lambda i: (0, i, 0)),
                  pl.BlockSpec((ts, 2 * D_MODEL), lambda i: (i, 0)), gain, gain],
        out_specs=[pl.BlockSpec((2, ts, D_MODEL), lambda i: (0, i, 0)), small, small],
        out_shape=[jax.ShapeDtypeStruct(dproj6.shape, dproj6.dtype),
                   jax.ShapeDtypeStruct((1, HEAD_DIM), F32), jax.ShapeDtypeStruct((1, HEAD_DIM), F32)],
        scratch_shapes=[pltpu.VMEM((8, D_MODEL), F32)],
        input_output_aliases={0: 0},
        compiler_params=_params("arbitrary"),
    )(dproj6, qk_raw, per_head(gq), per_head(gk))


CONV_ROWS = 512
HALO = 16


def _rows_with_halo(ref, r0, n, front, s_len):
    zeros = jnp.zeros((HALO, ref.shape[1]), F32)
    if front:
        return (jnp.concatenate([zeros, ref[0:n, :].astype(F32)], axis=0) if r0 == 0
                else ref[r0 - HALO:r0 + n, :].astype(F32))
    return (jnp.concatenate([ref[r0:r0 + n, :].astype(F32), zeros], axis=0) if r0 + n == s_len
            else ref[r0:r0 + n + HALO, :].astype(F32))


def _earlier(ext, k):
    return pltpu.roll(ext, k, 0)[HALO:]


def _later(ext, k):
    n = ext.shape[0]
    return pltpu.roll(ext, n - k, 0)[:n - HALO]


def _conv_cols(col0):
    return lambda s_len: pl.BlockSpec((s_len, LANE), lambda j: (0, col0 + j))


def _conv_fwd(proj, conv_w, conv_b, riders=()):
    s_len = proj.shape[0]

    def body(bg_ref, cg_ref, xc_ref, w_ref, b_ref, o_ref):
        w = [w_ref[t:t + 1, :] for t in range(3)]
        for r0 in range(0, s_len, CONV_ROWS):
            u = _rows_with_halo(cg_ref, r0, CONV_ROWS, True, s_len) * \
                _rows_with_halo(xc_ref, r0, CONV_ROWS, True, s_len)
            conv = b_ref[...] + w[0] * _earlier(u, 2) + w[1] * _earlier(u, 1) + w[2] * u[HALO:]
            o_ref[r0:r0 + CONV_ROWS, :] = (bg_ref[r0:r0 + CONV_ROWS, :].astype(F32) * conv).astype(o_ref.dtype)

    return _carry(
        riders, body, name="conv_fwd", grid=(D_MODEL // LANE,),
        in_specs=[_conv_cols(0)(s_len), _conv_cols(8)(s_len), _conv_cols(16)(s_len),
                  pl.BlockSpec((3, LANE), lambda j: (0, j)), pl.BlockSpec((1, LANE), lambda j: (0, j))],
        out_specs=pl.BlockSpec((s_len, LANE), lambda j: (0, j)),
        out_shape=jax.ShapeDtypeStruct((s_len, D_MODEL), BF16),
        semantics=("parallel",),
    )(proj, proj, proj, conv_w, conv_b)


def _conv_bwd(dproj6, dy, proj, conv_w, conv_b, riders=()):
    s_len = proj.shape[0]

    def body(dy_ref, bg_ref, cg_ref, xc_ref, w_ref, b_ref, _, dp_ref, dw_ref):
        w = [w_ref[t:t + 1, :] for t in range(3)]
        acc = [jnp.zeros((1, LANE), F32) for _ in range(4)]
        for r0 in range(0, s_len, CONV_ROWS):
            rows = slice(r0, r0 + CONV_ROWS)
            u = _rows_with_halo(cg_ref, r0, CONV_ROWS, True, s_len) * \
                _rows_with_halo(xc_ref, r0, CONV_ROWS, True, s_len)
            u2, u1, u0 = _earlier(u, 2), _earlier(u, 1), u[HALO:]
            conv = b_ref[...] + w[0] * u2 + w[1] * u1 + w[2] * u0
            dp_ref[0, rows, :] = (dy_ref[rows, :].astype(F32) * conv).astype(dp_ref.dtype)
            dconv_ext = _rows_with_halo(dy_ref, r0, CONV_ROWS, False, s_len) * \
                _rows_with_halo(bg_ref, r0, CONV_ROWS, False, s_len)
            dconv = dconv_ext[:CONV_ROWS]
            for t, term in enumerate([dconv * u2, dconv * u1, dconv * u0, dconv]):
                acc[t] = acc[t] + jnp.sum(term, axis=0, keepdims=True)
            du = w[2] * dconv + w[1] * _later(dconv_ext, 1) + w[0] * _later(dconv_ext, 2)
            dp_ref[1, rows, :] = (du * xc_ref[rows, :].astype(F32)).astype(dp_ref.dtype)
            dp_ref[2, rows, :] = (du * cg_ref[rows, :].astype(F32)).astype(dp_ref.dtype)
        dw_ref[...] = jnp.zeros_like(dw_ref)
        for t in range(4):
            dw_ref[t:t + 1, :] = acc[t]

    return _carry(
        riders, body, name="conv_bwd", grid=(D_MODEL // LANE,),
        in_specs=[pl.BlockSpec((s_len, LANE), lambda j: (0, j)),
                  _conv_cols(0)(s_len), _conv_cols(8)(s_len), _conv_cols(16)(s_len),
                  pl.BlockSpec((3, LANE), lambda j: (0, j)), pl.BlockSpec((1, LANE), lambda j: (0, j)),
                  pl.BlockSpec(memory_space=pl.ANY)],
        out_specs=[pl.BlockSpec((3, s_len, LANE), lambda j: (1, 0, j)),
                   pl.BlockSpec((8, LANE), lambda j: (0, j))],
        out_shape=[jax.ShapeDtypeStruct(dproj6.shape, dproj6.dtype),
                   jax.ShapeDtypeStruct((8, D_MODEL), F32)],
        input_output_aliases={6: 0},
        semantics=("parallel",),
    )(dy, proj, proj, proj, conv_w, conv_b, dproj6)


class _NoComm:
    def __init__(self, early, late):
        self.early, self.late = early, late

    def riders(self, at):
        return ()

    def done(self, at, carried):
        pass

    def early_weights(self):
        return self.early[0], self.early[2]

    def gate_weight(self):
        return self.early[1]

    def late_weights(self):
        return self.late

    def grads_ready(self, group, grads):
        pass


def _d_input(name, terms, riders=()):
    s_len, kdim = terms[0][0].shape[1], terms[0][1].shape[1]
    tm, chunk = 512, 512
    n = len(terms)

    def body(*refs):
        o_ref = refs[-1]
        acc = jnp.zeros(o_ref.shape, F32)
        for p_ref, w_ref in zip(refs[:n], refs[n:2 * n]):
            n_piece, _, width = p_ref.shape
            per_shard = w_ref.shape[2]
            for c0 in range(0, n_piece * width, chunk):
                acc = acc + lax.dot_general(p_ref[c0 // width, :, c0 % width:c0 % width + chunk],
                                            w_ref[c0 // per_shard, :, c0 % per_shard:c0 % per_shard + chunk],
                                            NT, preferred_element_type=F32)
        o_ref[...] = acc.astype(o_ref.dtype)

    tile = pl.BlockSpec((tm, kdim), lambda i: (i, 0))
    return _carry(
        riders, body, name=name, grid=(s_len // tm,),
        in_specs=[pl.BlockSpec((p.shape[0], tm, p.shape[2]), lambda i: (0, i, 0)) for p, _ in terms]
        + [pl.BlockSpec(w.shape, lambda i: (0, 0, 0), pipeline_mode=pl.Buffered(1)) for _, w in terms],
        out_specs=tile, out_shape=jax.ShapeDtypeStruct((s_len, kdim), BF16),
        semantics=("parallel",),
    )(*[p for p, _ in terms], *[w for _, w in terms])


def _local_grads(x, target, norm1_g, q_norm_g, k_norm_g, rel_bias, conv_b, b_gate, norm2_g, comm):
    s_len = x.shape[0]
    tm = min(1024, s_len)
    tb = min(2048, s_len)
    row = lambda v: v.reshape(1, -1)

    def carrying(at, fn, *args, **kw):
        riders = comm.riders(at)
        out = fn(*args, riders=riders, **kw)
        if riders:
            out, carried = out
            comm.done(at, carried)
        return out

    bias = carrying("bias_expand", _bias_expand, rel_bias)
    h = carrying("norm1", _rms_fwd, "norm1", x, row(norm1_g))
    w_in3, conv_w = comm.early_weights()
    gq, gk = row(q_norm_g), row(k_norm_g)

    def qk_epi(acc, e, o):
        o[0][...] = acc.astype(BF16)
        gain = jnp.where(pl.program_id(1) < 2, e[0] * QK_SCALE, e[1])
        o[1][...] = (_head_unit(acc)[0] * gain).astype(BF16)
    small = pl.BlockSpec((1, 512), lambda i, j, k: (0, 0))
    per_head = lambda g: jnp.tile(g, (1, 512 // HEAD_DIM))
    qk_raw, qkn = carrying("proj_qk", _mm_fwd, "proj_qk", h, w_in3, tm, 512, D_MODEL, ncols=2 * D_MODEL,
                           epilogue=qk_epi, outs=[_out2d(s_len, 2 * D_MODEL, BF16, tm, 512)] * 2,
                           extras=[(per_head(gq), small), (per_head(gk), small)])
    v = carrying("proj_v", _mm_fwd, "proj_v", h, w_in3, tb, 512, D_MODEL, col0=4, ncols=D_MODEL,
                 outs=[_out2d(s_len, D_MODEL, BF16, tb, 512)])
    w_gate3 = comm.gate_weight()
    conv_in = _mm_fwd("proj_conv", h, w_in3, tb, 1536, D_MODEL, col0=2, ncols=3 * D_MODEL,
                      outs=[_out2d(s_len, 3 * D_MODEL, BF16, tb, 1536)])

    def gate_epi(acc, e, o):
        o[0][...] = jax.nn.sigmoid(acc + e[0]).astype(BF16)
    gates = _mm_fwd("gates", h, w_gate3, tb, 512, D_MODEL, epilogue=gate_epi,
                    outs=[_out2d(s_len, 2 * D_MODEL, BF16, tb, 512)],
                    extras=[(row(b_gate), pl.BlockSpec((1, 512), lambda i, j, k: (0, j)))])

    attn, lse = carrying("attn_fwd", _attn_fwd, qkn, v, bias)
    yconv = carrying("conv_fwd", _conv_fwd, conv_in, conv_w, row(conv_b))
    w_ap, w_cp, w_out, w_up3, w_down = comm.late_weights()
    tw = 1024
    ya = _mm_fwd("attn_proj", attn, w_ap, tm, tw, D_MODEL, outs=[_out2d(s_len, D_MODEL, BF16, tm, tw)])

    def merge_epi(acc, e, o):
        ya_v, ga, gc = [t.astype(F32) for t in e]
        o[0][...] = acc.astype(BF16)
        o[1][...] = (ga * ya_v + gc * acc).astype(BF16)
    gate_a, gate_c = (gates, _tile_spec(tm, tw, 0)), (gates, _tile_spec(tm, tw, 1))
    yc, merged = _mm_fwd("conv_proj", yconv, w_cp, tm, tw, D_MODEL, epilogue=merge_epi,
                         outs=[_out2d(s_len, D_MODEL, BF16, tm, tw), _out2d(s_len, D_MODEL, BF16, tm, tw)],
                         extras=[(ya, _tile_spec(tm, tw)), gate_a, gate_c])

    def res_epi(acc, e, o):
        o[0][...] = e[0] + acc
    x1 = _mm_fwd("out_proj", merged, w_out, tm, tw, D_MODEL, epilogue=res_epi,
                 extras=[(x, _tile_spec(tm, tw))])
    h2 = _rms_fwd("norm2", x1, row(norm2_g))

    def up_epi(acc, e, o):
        o[0][...] = jnp.square(jnp.maximum(acc, 0.0)).astype(BF16)
    act = _mm_fwd("mlp_up", h2, w_up3, tb, tw, D_MODEL, epilogue=up_epi, outs=[_out2d(s_len, D_FF, BF16, tb, tw)])

    def loss_epi(acc, e, o):
        err = e[0] + acc - e[1]
        o[0][...] = (err * (1.0 / D_MODEL)).astype(BF16)
        sq = err * err
        part = sq[:, 0:LANE]
        for c0 in range(LANE, D_MODEL, LANE):
            part = part + sq[:, c0:c0 + LANE]
        o[1][...] = jnp.sum(part.reshape(tl // 8, 8, LANE), axis=0)
    tl = 512
    dy_b, loss_part = _mm_fwd(
        "mlp_down", act, w_down, tl, D_MODEL, D_FF, epilogue=loss_epi,
        outs=[_out2d(s_len, D_MODEL, BF16, tl, D_MODEL),
              (jax.ShapeDtypeStruct((8 * (s_len // tl), LANE), F32), pl.BlockSpec((8, LANE), lambda i, j, k: (i, 0)))],
        extras=[(x1, _tile_spec(tl, D_MODEL)), (target, _tile_spec(tl, D_MODEL))])

    def dup_epi(acc, e, o):
        o[0][...] = (acc * (2.0 * jnp.sqrt(e[0].astype(F32)))).astype(BF16)
    full = lambda cols: pl.BlockSpec((tm, cols), lambda i, j, n: (i, n))
    tokens = lambda cols: pl.BlockSpec((s_len, cols), lambda i, j, m: (m, j))
    dup = _mm_bwd_x("d_act", dy_b, full(D_MODEL), w_down, tm, tw, D_MODEL, s_len, D_MODEL, epilogue=dup_epi,
                    outs=[_out2d(s_len, D_FF, BF16, tm, tw)], extras=[(act, _tile_spec(tm, tw))])
    g_down = _mm_bwd_w("g_down", act, dy_b, tokens(D_MODEL), D_MODEL, 512, D_MODEL, s_len, False)
    g_up = _mm_bwd_w("g_up", h2, dup, tokens(512), D_FF, D_MODEL, 512, s_len, True)
    comm.grads_ready("mlp", dict(w_down=g_down, w_up=g_up))
    dh2 = carrying("d_h2", _d_input, "d_h2", [(dup.reshape(1, s_len, D_FF), w_up3)])
    dx1_b, dg2 = _rms_bwd("norm2_bwd", dh2, x1, row(norm2_g), dy_b)

    def dmerge_epi(acc, e, o):
        ya_v, yc_v, ga, gc = [t.astype(F32) for t in e]
        o[0][...] = (acc * ga).astype(BF16)
        o[1][...] = (acc * gc).astype(BF16)
        o[2][0] = (acc * ya_v * ga * (1.0 - ga)).astype(BF16)
        o[2][1] = (acc * yc_v * gc * (1.0 - gc)).astype(BF16)
    dya, dyc, dgp2 = _mm_bwd_x(
        "d_merged", dx1_b, full(D_MODEL), w_out, tm, tw, D_MODEL, s_len, D_MODEL, epilogue=dmerge_epi,
        outs=[_out2d(s_len, D_MODEL, BF16, tm, tw), _out2d(s_len, D_MODEL, BF16, tm, tw),
              (jax.ShapeDtypeStruct((2, s_len, D_MODEL), BF16), pl.BlockSpec((2, tm, tw), lambda i, j, n: (0, i, j)))],
        extras=[(ya, _tile_spec(tm, tw)), (yc, _tile_spec(tm, tw)), gate_a, gate_c])
    g_out = _mm_bwd_w("g_out", merged, dx1_b, tokens(512), D_MODEL, D_MODEL, 512, s_len, False)
    d_attn = _mm_bwd_x("d_attn", dya, full(D_MODEL), w_ap, tm, tw, D_MODEL, s_len, D_MODEL,
                       outs=[_out2d(s_len, D_MODEL, BF16, tm, tw)])
    g_ap = _mm_bwd_w("g_attn_proj", attn, dya, tokens(512), D_MODEL, D_MODEL, 512, s_len, False)
    d_yconv = _mm_bwd_x("d_yconv", dyc, full(D_MODEL), w_cp, tm, tw, D_MODEL, s_len, D_MODEL,
                        outs=[_out2d(s_len, D_MODEL, BF16, tm, tw)])
    g_cp = _mm_bwd_w("g_conv_proj", yconv, dyc, tokens(512), D_MODEL, D_MODEL, 512, s_len, False)
    piece = lambda width: (lambda blk: (blk * width) // D_MODEL, lambda blk: (blk * width % D_MODEL) // width)
    pc, cb = piece(512)
    pieces = pl.BlockSpec((None, s_len, 512), lambda i, j, m: (pc(j), m, cb(j)))
    g_gate = _mm_bwd_w("g_gate", h, dgp2, pieces, 2 * D_MODEL, D_MODEL, 512, s_len, True)
    comm.grads_ready("proj", dict(w_out=g_out, w_attn_proj=g_ap, w_conv_proj=g_cp, w_gate=g_gate))

    dproj6, dbias = carrying("attn_bwd", _attn_bwd, qkn, v, attn, d_attn, bias, lse)
    dproj6, dgq, dgk = _qk_norm_bwd(dproj6, qk_raw, gq, gk)
    dproj6, dconv_wb = carrying("conv_bwd", _conv_bwd, dproj6, d_yconv, conv_in, conv_w, row(conv_b))

    g_in = carrying("g_in", _mm_bwd_w, "g_in", h, dproj6, pieces, 6 * D_MODEL, D_MODEL, 512, s_len, True)
    comm.grads_ready("in", dict(w_in=g_in))
    d_rel = carrying("bias_reduce", _bias_reduce, dbias)
    dh = carrying("d_h", _d_input, "d_h", [(dproj6, w_in3), (dgp2, w_gate3)])
    grad_x, dg1 = carrying("norm1_bwd", _rms_bwd, "norm1_bwd", dh, x, row(norm1_g), dx1_b, out_dtype=F32)

    def bsum(t, f):
        return [], [t[0].astype(F32), t[1].astype(F32)]
    ts = 512
    db_a, db_c = carrying("b_gate_sum", _ew, "b_gate_sum", bsum,
                          [(dgp2, pl.BlockSpec((None, ts, D_MODEL), lambda i: (0, i, 0))),
                           (dgp2, pl.BlockSpec((None, ts, D_MODEL), lambda i: (1, i, 0)))],
                          [], [], sums=[D_MODEL, D_MODEL], ts=ts)

    big = dict(w_in=g_in, w_attn_proj=g_ap, w_conv_proj=g_cp, w_gate=g_gate, w_out=g_out,
               w_up=g_up, w_down=g_down)
    small = dict(norm1_g=dg1, norm2_g=dg2, conv_wb=dconv_wb, b_gate=(db_a, db_c),
                 q_norm_g=dgq, k_norm_g=dgk, rel_bias=d_rel)
    return loss_part, grad_x, big, small


def _finish_loss(loss_part):
    def body(l_ref, lo_ref):
        total = jnp.sum(jnp.sum(l_ref[...], axis=0, keepdims=True), axis=1, keepdims=True)
        lo_ref[...] = jnp.broadcast_to(total * (0.5 / D_MODEL), lo_ref.shape)

    return pl.pallas_call(body, name="finish_loss", out_shape=jax.ShapeDtypeStruct((8, LANE), F32))(loss_part)


HBM_SPEC = pl.BlockSpec(memory_space=pl.ANY)


def _place():
    return lax.axis_index("x"), lax.axis_index("y"), lax.axis_index("c")


def _other_chips(x, y):
    return [(1 - x, y), (x, 1 - y), (1 - x, 1 - y)]


def _cast_into_slot(name, where, w):
    r, cols = w.shape
    ts = 256

    def body(w_ref, x_ref, o_ref):
        o_ref[...] = x_ref[...].astype(o_ref.dtype)

    return pl.pallas_call(
        body, name=name,
        grid_spec=pltpu.PrefetchScalarGridSpec(
            num_scalar_prefetch=1, grid=(r // ts,),
            in_specs=[pl.BlockSpec((ts, cols), lambda i, w: (i, 0))],
            out_specs=pl.BlockSpec((None, ts, cols), lambda i, w: (w[0], i, 0))),
        out_shape=jax.ShapeDtypeStruct((N_SHARD, r, cols), BF16),
        compiler_params=_params("parallel"),
    )(where, w)


def _remote(src, dst, send, recv, k, to):
    return pltpu.make_async_remote_copy(src_ref=src, dst_ref=dst, send_sem=send.at[k], recv_sem=recv.at[k],
                                        device_id=to, device_id_type=MESH)


def _gather_riders(slots):
    n = len(slots)

    def half(refs, w, shard, which):
        hr = slots[w].shape[1] // 2
        return refs[w].at[shard, pl.ds(which * hr, hr)]

    def each(fn):
        x, y, c = _place()
        for w in range(n):
            for j, chip in enumerate(_other_chips(x, y)):
                fn(w, 3 * w + j, 2 * x + y, 2 * chip[0] + chip[1], c, (*chip, c), (x, y, 1 - c))

    def chips_start(_, refs, send, recv):
        each(lambda w, k, me, them, c, peer, sib: _remote(half(refs, w, me, c), half(refs, w, me, c), send, recv, k, peer).start())

    def chips_finish(_, refs, send, recv):
        each(lambda w, k, me, them, c, peer, sib: _remote(half(refs, w, them, c), half(refs, w, them, c), send, recv, k, peer).wait_recv())
        each(lambda w, k, me, them, c, peer, sib: _remote(half(refs, w, me, c), half(refs, w, me, c), send, recv, k, peer).wait_send())

    def sibling_start(_, refs, send, recv):
        each(lambda w, k, me, them, c, peer, sib: _remote(half(refs, w, them, c), half(refs, w, them, c), send, recv, k, sib).start())

    def sibling_finish(_, refs, send, recv):
        each(lambda w, k, me, them, c, peer, sib: _remote(half(refs, w, them, 1 - c), half(refs, w, them, 1 - c), send, recv, k, sib).wait_recv())
        each(lambda w, k, me, them, c, peer, sib: _remote(half(refs, w, them, c), half(refs, w, them, c), send, recv, k, sib).wait_send())

    return (lambda s: _Rider([], s, 3 * n, chips_start, chips_finish),
            lambda s: _Rider([], s, 3 * n, sibling_start, sibling_finish))


def _pair_exchange_rider(grads, landing):
    n = len(grads)

    def copies(srcs, dsts, send, recv):
        x, y, c = _place()
        out = []
        for w in range(n):
            hr = grads[w].shape[1] // 2
            out.append(_remote(srcs[w].at[:, pl.ds((1 - c) * hr, hr)], dsts[w], send, recv, w, (x, y, 1 - c)))
        return out

    def start(srcs, dsts, send, recv):
        for cp in copies(srcs, dsts, send, recv):
            cp.start()

    def finish(srcs, dsts, send, recv):
        for cp in copies(srcs, dsts, send, recv):
            cp.wait()

    return _Rider(grads, landing, n, start, finish)


def _row_tile(hr):
    return min(hr, 256)


def _pair_add(name, where, grad, got):
    _, hr, cols = got.shape
    tr = _row_tile(hr)
    nblk = hr // tr

    def body(w_ref, g_ref, r_ref, o_ref):
        o_ref[...] = (g_ref[...] + r_ref[...]).astype(o_ref.dtype)

    other = lambda k, w: (w[0] + 1 + k) % N_SHARD
    return pl.pallas_call(
        body, name=name,
        grid_spec=pltpu.PrefetchScalarGridSpec(
            num_scalar_prefetch=1, grid=(N_SHARD - 1, nblk),
            in_specs=[pl.BlockSpec((None, tr, cols), lambda k, i, w: (other(k, w), w[1] * nblk + i, 0)),
                      pl.BlockSpec((None, tr, cols), lambda k, i, w: (other(k, w), i, 0))],
            out_specs=pl.BlockSpec((None, tr, cols), lambda k, i, w: (other(k, w), i, 0))),
        out_shape=jax.ShapeDtypeStruct(got.shape, BF16),
        compiler_params=_params("parallel", "parallel"),
    )(where, grad, got)


def _chip_exchange_rider(partials, landing):
    n = len(partials)

    def copies(srcs, dsts, send, recv):
        x, y, c = _place()
        return [_remote(srcs[w].at[2 * chip[0] + chip[1]], dsts[w].at[j], send, recv, 3 * w + j, (*chip, c))
                for w in range(n) for j, chip in enumerate(_other_chips(x, y))]

    def start(srcs, dsts, send, recv):
        for cp in copies(srcs, dsts, send, recv):
            cp.start()

    def finish(srcs, dsts, send, recv):
        for cp in copies(srcs, dsts, send, recv):
            cp.wait()

    return _Rider(partials, landing, 3 * n, start, finish)


def _final_add(name, where, grad, got, arrived):
    _, hr, cols = got.shape
    tr = _row_tile(hr)
    nblk = hr // tr

    def body(w_ref, g_ref, r_ref, a_ref, o_ref):
        acc = g_ref[...] + r_ref[...]
        for j in range(3):
            acc = acc + a_ref[j].astype(F32)
        o_ref[...] = acc

    return pl.pallas_call(
        body, name=name,
        grid_spec=pltpu.PrefetchScalarGridSpec(
            num_scalar_prefetch=1, grid=(nblk,),
            in_specs=[pl.BlockSpec((None, tr, cols), lambda i, w: (w[0], w[1] * nblk + i, 0)),
                      pl.BlockSpec((None, tr, cols), lambda i, w: (w[0], i, 0)),
                      pl.BlockSpec((3, tr, cols), lambda i, w: (0, i, 0))],
            out_specs=pl.BlockSpec((tr, cols), lambda i, w: (w[1] * nblk + i, 0))),
        out_shape=jax.ShapeDtypeStruct((2 * hr, cols), F32),
        compiler_params=_params("parallel"),
    )(where, grad, got, arrived)


def _pair_share_rider(shards):
    n = len(shards)

    def half(refs, w, which):
        hr = shards[w].shape[0] // 2
        return refs[w].at[pl.ds(which * hr, hr)]

    def start(_, refs, send, recv):
        x, y, c = _place()
        for w in range(n):
            _remote(half(refs, w, c), half(refs, w, c), send, recv, w, (x, y, 1 - c)).start()

    def finish(_, refs, send, recv):
        x, y, c = _place()
        for w in range(n):
            _remote(half(refs, w, 1 - c), half(refs, w, 1 - c), send, recv, w, (x, y, 1 - c)).wait_recv()
        for w in range(n):
            _remote(half(refs, w, c), half(refs, w, c), send, recv, w, (x, y, 1 - c)).wait_send()

    return _Rider([], shards, n, start, finish)


class _Exchange:
    PLAN = {"bias_expand": [("early", "gather")], "norm1": [("early", "forward")],
            "proj_qk": [("gate", "gather")], "proj_v": [("gate", "forward")],
            "attn_fwd": [("late", "gather")], "conv_fwd": [("late", "forward")],
            "d_h2": [("mlp", "pair")], "attn_bwd": [("mlp", "chips"), ("proj", "pair")], "conv_bwd": [("mlp", "share")],
            "g_in": [("proj", "chips")], "bias_reduce": [("proj", "share"), ("in", "pair")],
            "d_h": [("in", "chips")], "b_gate_sum": [("in", "share")]}

    def __init__(self, where, early_slots, gate_slots, late_slots):
        self.where = where
        self.slots = dict(early=early_slots, gate=gate_slots, late=late_slots)
        self.stage = {g: dict(zip(("gather", "forward"), _gather_riders(s))) for g, s in self.slots.items()}
        self.groups, self.reduced, self.pending = {}, {}, []

    def early_weights(self):
        w_in3, small = self.slots["early"]
        conv_w = small[:, :3, :].transpose(1, 0, 2).reshape(3, N_SHARD * small.shape[2])
        return w_in3, conv_w

    def gate_weight(self):
        return self.slots["gate"][0]

    def late_weights(self):
        rows = lambda a: a.reshape(a.shape[0] * a.shape[1], a.shape[2])
        w_ap, w_cp, w_out, w_up3, w_down = self.slots["late"]
        return rows(w_ap), rows(w_cp), rows(w_out), w_up3, rows(w_down)

    def grads_ready(self, group, grads):
        names = list(grads)
        g4 = [g if g.ndim == 3 else g.reshape(N_SHARD, -1, g.shape[1]) for g in grads.values()]
        self.groups[group] = dict(names=names, g4=g4)

    def riders(self, at):
        self.pending = self.PLAN.get(at, [])
        out = []
        for group, stage in self.pending:
            if group in self.slots:
                out.append(self.stage[group][stage](self.slots[group]))
                continue
            st = self.groups[group]
            if stage == "pair":
                landing = [lax.empty((N_SHARD, g.shape[1] // 2, g.shape[2]), F32) for g in st["g4"]]
                out.append(_pair_exchange_rider(st["g4"], landing))
            elif stage == "chips":
                landing = [lax.empty((3,) + p.shape[1:], p.dtype) for p in st["partial"]]
                out.append(_chip_exchange_rider(st["partial"], landing))
            else:
                out.append(_pair_share_rider(st["halves"]))
        return out

    def done(self, at, carried):
        for (group, stage), arrays in zip(self.pending, carried):
            if group in self.slots:
                self.slots[group] = arrays
                continue
            st = self.groups[group]
            tag = lambda what, n: what + "_" + n
            if stage == "pair":
                st["got"] = arrays
                st["partial"] = [_pair_add(tag("pair_add", n), self.where, g, r)
                                 for n, g, r in zip(st["names"], st["g4"], arrays)]
            elif stage == "chips":
                st["halves"] = [_final_add(tag("final_add", n), self.where, g, r, a)
                                for n, g, r, a in zip(st["names"], st["g4"], st["got"], arrays)]
            else:
                self.reduced.update(zip(st["names"], arrays))


SMALL_ROWS = 32
N_DEV = 8


def _all_gather_small_rider(slots):
    def peers():
        x, y, c = _place()
        for k in range(1, N_DEV):
            yield k - 1, (1 - x if k & 4 else x, 1 - y if k & 2 else y, 1 - c if k & 1 else c)

    def start(_, refs, send, recv):
        x, y, c = _place()
        mine = refs[0].at[4 * x + 2 * y + c]
        for k, peer in peers():
            _remote(mine, mine, send, recv, k, peer).start()

    def finish(_, refs, send, recv):
        x, y, c = _place()
        mine = refs[0].at[4 * x + 2 * y + c]
        for k, (px, py, pc) in peers():
            theirs = refs[0].at[4 * px + 2 * py + pc]
            _remote(theirs, theirs, send, recv, k, (px, py, pc)).wait_recv()
        for k, peer in peers():
            _remote(mine, mine, send, recv, k, peer).wait_send()

    return _Rider([], [slots], N_DEV - 1, start, finish)


def _sum_slots(slots):
    def body(s_ref, o_ref):
        acc = s_ref[0]
        for d in range(1, N_DEV):
            acc = acc + s_ref[d]
        o_ref[...] = acc

    return pl.pallas_call(body, name="sum_small", out_shape=jax.ShapeDtypeStruct(slots.shape[1:], F32))(slots)


def _adamw(name, w, g, m, v, riders=()):
    c1 = 1.0 - ADAM_B1 ** ADAM_STEP
    c2 = 1.0 - ADAM_B2 ** ADAM_STEP

    def fn(t, f):
        wv, gv, mv, vv = t
        m2 = ADAM_B1 * mv + (1.0 - ADAM_B1) * gv
        v2 = ADAM_B2 * vv + (1.0 - ADAM_B2) * (gv * gv)
        delta = -ADAM_LR * ((m2 / c1) / (jnp.sqrt(v2 / c2) + ADAM_EPS) + ADAM_WD * wv)
        return [delta, m2, v2], []

    cols = w.shape[1]
    return _ew(name, fn, [w, g, m, v], [], [(cols, F32)] * 3, ts=min(w.shape[0], 256), riders=riders)


LOSS_ROW = 26


def _pack_small(norm1_g, norm2_g, conv_b, b_gate, conv_w, q_norm_g, k_norm_g, rel_bias, loss=None):
    pack = jnp.zeros((SMALL_ROWS, D_MODEL), F32)
    for r0, v in ((0, norm1_g), (1, norm2_g), (2, conv_b), (3, b_gate.reshape(2, D_MODEL)), (5, conv_w),
                  (8, q_norm_g), (9, k_norm_g), (10, rel_bias)) + (((LOSS_ROW, loss),) if loss is not None else ()):
        v = v.reshape(-1, v.shape[-1]).astype(F32)
        pack = pack.at[r0:r0 + v.shape[0], :v.shape[1]].set(v)
    return pack


def _unpack_small(pack, conv_cols):
    return dict(norm1_g=pack[0], norm2_g=pack[1], conv_b=pack[2], b_gate=pack[3:5].reshape(2 * D_MODEL),
                conv_w=pack[5:8, :conv_cols], q_norm_g=pack[8, :HEAD_DIM], k_norm_g=pack[9, :HEAD_DIM],
                rel_bias=pack[10:10 + N_HEADS, :N_REL])


BIG = ["w_in", "w_attn_proj", "w_conv_proj", "w_gate", "w_out", "w_up", "w_down"]
LATE = ["w_attn_proj", "w_conv_proj", "w_out", "w_up", "w_down"]
WEIGHTS = ["norm1_g", "w_in", "q_norm_g", "k_norm_g", "rel_bias", "conv_w", "conv_b", "w_attn_proj",
           "w_conv_proj", "w_gate", "b_gate", "w_out", "norm2_g", "w_up", "w_down"]


def kernel(x, norm1_g, w_in, q_norm_g, k_norm_g, rel_bias, conv_w, conv_b, w_attn_proj, w_conv_proj, w_gate, b_gate, w_out, norm2_g, w_up, w_down, loss_target, m_norm1_g, m_w_in, m_q_norm_g, m_k_norm_g, m_rel_bias, m_conv_w, m_conv_b, m_w_attn_proj, m_w_conv_proj, m_w_gate, m_b_gate, m_w_out, m_norm2_g, m_w_up, m_w_down, v_norm1_g, v_w_in, v_q_norm_g, v_k_norm_g, v_rel_bias, v_conv_w, v_conv_b, v_w_attn_proj, v_w_conv_proj, v_w_gate, v_b_gate, v_w_out, v_norm2_g, v_w_up, v_w_down):
    given = dict(locals())
    w = {n: given[n] for n in WEIGHTS}
    m = {n: given["m_" + n] for n in WEIGHTS}
    v = {n: given["v_" + n] for n in WEIGHTS}
    s_len = x.shape[1]
    shard = 2 * lax.axis_index("x") + lax.axis_index("y")
    where = jnp.stack([shard, lax.axis_index("c")]).astype(jnp.int32)
    conv_cols = conv_w.shape[1]

    small_in = lax.dynamic_update_slice(jnp.zeros((N_SHARD, 16, conv_cols), F32), conv_w[None], (shard, 0, 0))
    slot = {n: _cast_into_slot("cast_" + n, where, w[n]) for n in BIG}
    comm = _Exchange(where, [slot["w_in"], small_in], [slot["w_gate"]], [slot[n] for n in LATE])

    loss_part, grad_x, _, small = _local_grads(
        x.reshape(s_len, D_MODEL), loss_target.reshape(s_len, D_MODEL), norm1_g, q_norm_g, k_norm_g,
        rel_bias, conv_b, b_gate, norm2_g, comm)
    grad = dict(comm.reduced)

    loss_local = _finish_loss(loss_part)
    pack = _pack_small(small["norm1_g"], small["norm2_g"], small["conv_wb"][3], jnp.concatenate(small["b_gate"], axis=1),
                       small["conv_wb"][0:3], small["q_norm_g"], small["k_norm_g"], small["rel_bias"],
                       loss=loss_local[0:1, :])
    me = 2 * shard + lax.axis_index("c")
    slots = lax.dynamic_update_slice(jnp.zeros((N_DEV,) + pack.shape, F32), pack[None], (me, 0, 0))

    delta, new_m, new_v = {}, {}, {}
    for n in BIG:
        riders = [_all_gather_small_rider(slots)] if n == "w_up" else ()
        out = _adamw("adamw_" + n, w[n], grad[n], m[n], v[n], riders=riders)
        if riders:
            out, ((slots,),) = out
        delta[n], new_m[n], new_v[n] = out
    total = _sum_slots(slots)
    g_small = _unpack_small(total, D_MODEL)
    g_small["conv_w"] = lax.dynamic_slice(g_small["conv_w"], (0, shard * conv_cols), (3, conv_cols))
    grad.update(g_small)
    small_names = [n for n in WEIGHTS if n not in BIG]
    packs = [_pack_small(**{n: src[n] for n in small_names}) for src in (w, grad, m, v)]
    for out, packed in zip((delta, new_m, new_v), _adamw("adamw_small", *packs)):
        out.update({n: a.reshape(w[n].shape) for n, a in _unpack_small(packed, conv_cols).items()})

    outs = [total[LOSS_ROW, 0], grad_x.reshape(x.shape)]
    for group in (grad, delta, new_m, new_v):
        outs += [group[n].reshape(w[n].shape) for n in WEIGHTS]
    return tuple(outs)
```

```python
import functools

import jax
import jax.numpy as jnp
from jax import lax
from jax.experimental import pallas as pl
from jax.experimental.pallas import tpu as pltpu

F32 = jnp.float32
BF16 = jnp.bfloat16

D_MODEL = 1024
N_HEADS = 16
HEAD_DIM = 64
CHUNK = 64
N_PREV_CHUNKS = 8
MAX_REL = 256
D_FF = 4096
N_REL = 2 * MAX_REL + 1
REL_PAD = 640
EPS = 1e-6
NEG_INF = -1e30
QK_SCALE = HEAD_DIM ** -0.5

SUPER = 4 * CHUNK
BAND = SUPER + N_PREV_CHUNKS * CHUNK
SKEW_W = 1024
N_SHARD = 4
LANE = 128
MXU_DIM = 256
VMEM_LIMIT = 48 * 1024 * 1024

ADAM_LR = 0.001
ADAM_B1 = 0.9
ADAM_B2 = 0.999
ADAM_EPS = 1e-08
ADAM_WD = 0.01
ADAM_STEP = 10

MESH = pl.DeviceIdType.MESH
NN = (((1,), (0,)), ((), ()))
NT = (((1,), (1,)), ((), ()))
TN = (((0,), (0,)), ((), ()))


def _params(*sem):
    return pltpu.CompilerParams(dimension_semantics=sem or None, vmem_limit_bytes=VMEM_LIMIT)


HBM_SPEC = pl.BlockSpec(memory_space=pl.ANY)


class _Rider:
    def __init__(self, sources, arrays, n_sem, start, finish):
        self.sources, self.arrays, self.n_sem, self.start, self.finish = sources, arrays, n_sem, start, finish


def _carry(riders, body, *, name, out_shape, grid=(), in_specs=None, out_specs=None, scratch_shapes=(),
           semantics=(), input_output_aliases=None):
    aliases = dict(input_output_aliases or {})
    if not riders:
        kw = {} if in_specs is None else dict(in_specs=in_specs, out_specs=out_specs)
        return pl.pallas_call(body, name=name, grid=grid, out_shape=out_shape, scratch_shapes=scratch_shapes,
                              input_output_aliases=aliases, compiler_params=_params(*semantics), **kw)
    single = not isinstance(out_shape, (list, tuple))
    shapes = [out_shape] if single else list(out_shape)
    n_out, n_scr = len(shapes), len(scratch_shapes)
    srcs = [a for r in riders for a in r.sources]
    arrs = [a for r in riders for a in r.arrays]
    vmem = pl.BlockSpec(memory_space=pltpu.VMEM)

    def run(*args):
        n_in = len(args)

        def wrapped(*refs):
            pos = n_in
            src_refs = refs[pos:pos + len(srcs)]
            pos += len(srcs) + len(arrs)
            outs = refs[pos:pos + n_out]
            pos += n_out
            arr_refs = refs[pos:pos + len(arrs)]
            pos += len(arrs)
            scratch = refs[pos:pos + n_scr]
            sems = refs[pos + n_scr:]
            first, last = True, True
            for d, size in enumerate(grid):
                first = jnp.logical_and(first, pl.program_id(d) == 0)
                last = jnp.logical_and(last, pl.program_id(d) == size - 1)

            def each(method):
                s0 = a0 = 0
                for k, r in enumerate(riders):
                    getattr(r, method)(src_refs[s0:s0 + len(r.sources)], arr_refs[a0:a0 + len(r.arrays)],
                                       sems[2 * k], sems[2 * k + 1])
                    s0, a0 = s0 + len(r.sources), a0 + len(r.arrays)

            pl.when(first)(lambda: each("start"))
            body(*refs[:n_in], *outs, *scratch)
            pl.when(last)(lambda: each("finish"))

        ins = [vmem] * n_in if in_specs is None else list(in_specs)
        if out_specs is None:
            o_specs = [vmem] * n_out
        else:
            o_specs = [out_specs] if single else list(out_specs)
        for k in range(len(arrs)):
            aliases[n_in + len(srcs) + k] = n_out + k
        res = pl.pallas_call(
            wrapped, name=name, grid=grid,
            in_specs=ins + [HBM_SPEC] * (len(srcs) + len(arrs)),
            out_specs=o_specs + [HBM_SPEC] * len(arrs),
            out_shape=shapes + [jax.ShapeDtypeStruct(a.shape, a.dtype) for a in arrs],
            scratch_shapes=list(scratch_shapes) + [pltpu.SemaphoreType.DMA((r.n_sem,)) for r in riders for _ in range(2)],
            input_output_aliases=aliases,
            compiler_params=_params(*["arbitrary"] * len(grid)),
        )(*args, *srcs, *arrs)
        core, rest = res[:n_out], list(res[n_out:])
        carried, a0 = [], 0
        for r in riders:
            carried.append(rest[a0:a0 + len(r.arrays)])
            a0 += len(r.arrays)
        return (core[0] if single else core), carried

    return run


def _mm(name, dims, a, a_spec, b, b_spec, grid, tile, outs, epilogue=None, extras=(), riders=()):
    nk, ne, no = grid[2], len(extras), len(outs)

    def body(a_ref, b_ref, *refs):
        e_refs, o_refs = refs[:ne], refs[ne:ne + no]
        part = lax.dot_general(a_ref[...], b_ref[...], dims, preferred_element_type=F32)

        def finish(acc):
            if epilogue is None:
                o_refs[0][...] = acc.astype(o_refs[0].dtype)
            else:
                epilogue(acc, [r[...] for r in e_refs], o_refs)

        if nk == 1:
            finish(part)
        else:
            acc_ref = refs[ne + no]
            k = pl.program_id(2)

            @pl.when(k == 0)
            def _():
                acc_ref[...] = part

            @pl.when(k > 0)
            def _():
                acc_ref[...] += part

            @pl.when(k == nk - 1)
            def _():
                finish(acc_ref[...])

    res = _carry(
        riders, body, name=name, grid=grid,
        in_specs=[a_spec, b_spec] + [s for _, s in extras],
        out_specs=[s for _, s in outs],
        out_shape=[s for s, _ in outs],
        scratch_shapes=[pltpu.VMEM(tile, F32)] if nk > 1 else [],
        semantics=("parallel", "parallel", "arbitrary"),
    )(a, b, *[e for e, _ in extras])
    res, carried = res if riders else (res, None)
    res = res[0] if no == 1 else res
    return (res, carried) if riders else res


def _tile_spec(tm, tn, col0=0):
    return pl.BlockSpec((tm, tn), lambda i, j, k: (i, j + col0))


def _out2d(m, n, dtype, tm, tn):
    return (jax.ShapeDtypeStruct((m, n), dtype), _tile_spec(tm, tn))


def _mm_fwd(name, a, w, tm, tn, tk, outs=None, epilogue=None, extras=(), col0=0, ncols=None, riders=()):
    m, kdim = a.shape
    if w.ndim == 3:
        per = w.shape[2] // tn
        n = ncols or N_SHARD * w.shape[2]
        w_spec = pl.BlockSpec((None, tk, tn), lambda i, j, k: ((j + col0) // per, k, (j + col0) % per))
    else:
        n = ncols or w.shape[1]
        w_spec = pl.BlockSpec((tk, tn), lambda i, j, k: (k, j + col0))
    if outs is None:
        outs = [_out2d(m, n, F32, tm, tn)]
    return _mm(name, NN, a, pl.BlockSpec((tm, tk), lambda i, j, k: (i, k)), w, w_spec,
               (m // tm, n // tn, kdim // tk), (tm, tn), outs, epilogue, extras, riders)


def _mm_bwd_x(name, g, g_spec, w, tm, tj, tc, m, n_contract, outs=None, epilogue=None, extras=(), riders=()):
    if w.ndim == 3:
        per = w.shape[2] // tc
        kdim = w.shape[1]
        w_spec = pl.BlockSpec((None, tj, tc), lambda i, j, n: (n // per, j, n % per))
    else:
        kdim = w.shape[0]
        w_spec = pl.BlockSpec((tj, tc), lambda i, j, n: (j, n))
    if outs is None:
        outs = [_out2d(m, kdim, F32, tm, tj)]
    return _mm(name, NT, g, g_spec, w, w_spec, (m // tm, kdim // tj, n_contract // tc),
               (tm, tj), outs, epilogue, extras, riders)


def _mm_bwd_w(name, a, g, g_spec, n, tk, tn, tm, sharded, riders=()):
    m, kdim = a.shape
    if sharded:
        per = (n // N_SHARD) // tn
        out = (jax.ShapeDtypeStruct((N_SHARD, kdim, n // N_SHARD), F32),
               pl.BlockSpec((None, tk, tn), lambda i, j, mm: (j // per, i, j % per)))
    else:
        out = (jax.ShapeDtypeStruct((kdim, n), F32), pl.BlockSpec((tk, tn), lambda i, j, mm: (i, j)))
    return _mm(name, TN, a, pl.BlockSpec((tm, tk), lambda i, j, mm: (mm, i)), g, g_spec,
               (kdim // tk, n // tn, m // tm), (tk, tn), [out], riders=riders)


def _ew(name, fn, tiles, fulls, outs, sums=(), ts=512, riders=()):
    tiles = [t if isinstance(t, tuple) else (t, pl.BlockSpec((ts, t.shape[1]), lambda i: (i, 0)))
             for t in tiles]
    s_rows = tiles[0][0].shape[-2]
    nt, nf, no = len(tiles), len(fulls), len(outs)

    def body(*refs):
        t_vals = [r[...] for r in refs[:nt]]
        f_vals = [r[...] for r in refs[nt:nt + nf]]
        o_refs, s_refs = refs[nt + nf:nt + nf + no], refs[nt + nf + no:]
        o_vals, s_vals = fn(t_vals, f_vals)
        for r, v in zip(o_refs, o_vals):
            r[...] = v.astype(r.dtype)
        for r, v in zip(s_refs, s_vals):
            part = jnp.sum(v, axis=0, keepdims=True)

            @pl.when(pl.program_id(0) == 0)
            def _():
                r[...] = part

            @pl.when(pl.program_id(0) > 0)
            def _():
                r[...] += part

    full_specs = [pl.BlockSpec(f.shape, lambda i, nd=f.ndim: (0,) * nd) for f in fulls]
    return _carry(
        riders, body, name=name, grid=(s_rows // ts,),
        in_specs=[s for _, s in tiles] + full_specs,
        out_specs=[pl.BlockSpec((ts, c), lambda i: (i, 0)) for c, _ in outs]
        + [pl.BlockSpec((1, c), lambda i: (0, 0)) for c in sums],
        out_shape=[jax.ShapeDtypeStruct((s_rows, c), dt) for c, dt in outs]
        + [jax.ShapeDtypeStruct((1, c), F32) for c in sums],
        semantics=("arbitrary",),
    )(*[t for t, _ in tiles], *fulls)


def _rms_fwd(name, x, g, riders=()):
    def fn(t, f):
        xv = t[0]
        r = lax.rsqrt(jnp.mean(xv * xv, axis=-1, keepdims=True) + EPS)
        return [xv * r * f[0]], []
    out = _ew(name, fn, [x], [g], [(x.shape[1], BF16)], riders=riders)
    return (out[0][0], out[1]) if riders else out[0]


def _rms_bwd(name, dh, x, g, dres, out_dtype=BF16, riders=()):
    def fn(t, f):
        dhv, xv, dr = [v.astype(F32) for v in t]
        r = lax.rsqrt(jnp.mean(xv * xv, axis=-1, keepdims=True) + EPS)
        xhat = xv * r
        u = dhv * f[0]
        dx = r * (u - xhat * jnp.mean(u * xhat, axis=-1, keepdims=True)) + dr
        return [dx], [dhv * xhat]
    c = x.shape[1]
    return _ew(name, fn, [dh, x, dres], [g], [(c, out_dtype)], sums=[c], riders=riders)


def _split3(x):
    x1 = x.astype(BF16)
    r1 = x - x1.astype(F32)
    x2 = r1.astype(BF16)
    x3 = (r1 - x2.astype(F32)).astype(BF16)
    return x1, x2, x3


def _rel_class(cp):
    far = (cp < MAX_REL) | (cp > BAND)
    return jnp.where(far, 2 * MAX_REL, BAND - cp)


def _skew_rows(x, sign):
    row = lax.broadcasted_iota(jnp.int32, x.shape, 0)
    for b in range(CHUNK.bit_length() - 1):
        shift = (1 << b) if sign > 0 else SKEW_W - (1 << b)
        x = jnp.where((row >> b) & 1 == 1, pltpu.roll(x, shift, 1), x)
    return x


def _roll_lanes(x, shift):
    return x if shift % SKEW_W == 0 else pltpu.roll(x, shift % SKEW_W, 1)


N_START = 3


def _bias_expand(rel_bias, riders=()):
    rel = jnp.pad(rel_bias, ((0, 0), (0, REL_PAD - N_REL))).reshape(N_HEADS, 1, REL_PAD)

    def body(rel_ref, o_ref):
        cls = lax.broadcasted_iota(jnp.int32, (REL_PAD, SKEW_W), 0)
        cp = lax.broadcasted_iota(jnp.int32, (REL_PAD, SKEW_W), 1)
        onehot = (cls == _rel_class(cp)).astype(BF16)
        rel8 = jnp.broadcast_to(rel_ref[...], (8, REL_PAD))
        trow = sum(jnp.dot(p, onehot, preferred_element_type=F32) for p in _split3(rel8))[0:1]
        first = _skew_rows(jnp.broadcast_to(trow, (CHUNK, SKEW_W)), +1)
        full = jnp.concatenate([_roll_lanes(first, CHUNK * g) for g in range(SUPER // CHUNK)], axis=0)[:, :BAND]
        qc = lax.broadcasted_iota(jnp.int32, (SUPER, BAND), 0) // CHUNK
        kc = lax.broadcasted_iota(jnp.int32, (SUPER, BAND), 1) // CHUNK
        on_band = (kc >= qc) & (kc <= qc + N_PREV_CHUNKS)
        table = jnp.where(on_band, full, NEG_INF).T
        key = lax.broadcasted_iota(jnp.int32, (BAND, SUPER), 0)
        for t in range(N_START):
            o_ref[t] = jnp.where(key < (N_START - 1 - t) * SUPER, NEG_INF, table)

    return _carry(
        riders, body, name="bias_expand", grid=(N_HEADS,),
        in_specs=[pl.BlockSpec((None, 1, REL_PAD), lambda h: (h, 0, 0))],
        out_specs=pl.BlockSpec((N_START, None, BAND, SUPER), lambda h: (0, h, 0, 0)),
        out_shape=jax.ShapeDtypeStruct((N_START, N_HEADS, BAND, SUPER), F32),
        semantics=("arbitrary",),
    )(rel)


def _bias_reduce(dbias, riders=()):
    def body(d_ref, o_ref):
        x = jnp.concatenate([d_ref[...].T, jnp.zeros((SUPER, SKEW_W - BAND), F32)], axis=1)
        folded = sum(_roll_lanes(x[CHUNK * g:CHUNK * (g + 1)], -CHUNK * g) for g in range(SUPER // CHUNK))
        diag = jnp.sum(_skew_rows(folded, -1), axis=0, keepdims=True)
        cp = lax.broadcasted_iota(jnp.int32, (SKEW_W, REL_PAD), 0)
        cls = lax.broadcasted_iota(jnp.int32, (SKEW_W, REL_PAD), 1)
        onehot = (cls == _rel_class(cp)).astype(BF16)
        diag8 = jnp.broadcast_to(diag, (8, SKEW_W))
        o_ref[...] = sum(jnp.dot(p, onehot, preferred_element_type=F32) for p in _split3(diag8))[0:1]

    out = _carry(
        riders, body, name="bias_reduce", grid=(N_HEADS,),
        in_specs=[pl.BlockSpec((None, BAND, SUPER), lambda h: (h, 0, 0))],
        out_specs=pl.BlockSpec((None, 1, REL_PAD), lambda h: (h, 0, 0)),
        out_shape=jax.ShapeDtypeStruct((N_HEADS, 1, REL_PAD), F32),
        semantics=("arbitrary",),
    )(dbias)
    out, carried = out if riders else (out, None)
    out = out.reshape(N_HEADS, REL_PAD)[:, :N_REL]
    return (out, carried) if riders else out


HEADS_PER_STEP = 8
HEAD_COLS = HEADS_PER_STEP * HEAD_DIM
N_HEAD_GROUPS = N_HEADS // HEADS_PER_STEP


def _unit(x):
    r = lax.rsqrt(jnp.mean(x * x, axis=-1, keepdims=True) + EPS)
    return x * r, r


def _scores_t(qs, kn, bias_t):
    return jnp.concatenate([lax.dot_general(k, qs, NT, preferred_element_type=F32) for k in kn], axis=0) + bias_t


def _bias_spec():
    return pl.BlockSpec((None, HEADS_PER_STEP, BAND, SUPER), lambda hg, i: (jnp.minimum(i, N_START - 1), hg, 0, 0))


def _band_specs(nb, col0, clamp_hi):
    def spec(d):
        def index(hg, i):
            blk = jnp.maximum(i - d, 0)
            if clamp_hi:
                blk = jnp.minimum(blk, nb - 1)
            return (blk, col0 + hg)
        return pl.BlockSpec((SUPER, HEAD_COLS), index)
    return [spec(2), spec(1), spec(0)]


def _head_sums(y):
    same_head = (lax.broadcasted_iota(jnp.int32, (MXU_DIM, MXU_DIM), 0) // HEAD_DIM
                 == lax.broadcasted_iota(jnp.int32, (MXU_DIM, MXU_DIM), 1) // HEAD_DIM).astype(BF16)
    sums = []
    for c0 in range(0, y.shape[1], MXU_DIM):
        chunk = y[:, c0:c0 + MXU_DIM]
        hi = chunk.astype(BF16)
        lo = (chunk - hi.astype(F32)).astype(BF16)
        sums.append(jnp.dot(hi, same_head, preferred_element_type=F32)
                    + jnp.dot(lo, same_head, preferred_element_type=F32))
    return jnp.concatenate(sums, axis=1)


def _head_unit(x):
    r = lax.rsqrt(_head_sums(x * x) * (1.0 / HEAD_DIM) + EPS)
    return x * r, r


def _head(hh):
    return slice(HEAD_DIM * hh, HEAD_DIM * (hh + 1))


def _key_block(j):
    return slice(SUPER * j, SUPER * (j + 1))


LSE_ROWS = 8


def _attn_fwd(qkn, v, bias, riders=()):
    s_len = qkn.shape[0]
    nb = s_len // SUPER

    def body(q_ref, k0, k1, k2, v0, v1, v2, b_ref, o_ref, lse_ref):
        outs = []

        def probabilities(hh):
            sl = _head(hh)
            s = _scores_t(q_ref[:, sl], [k0[:, sl], k1[:, sl], k2[:, sl]], b_ref[hh])
            m = jnp.max(s, axis=0, keepdims=True)
            e = jnp.exp(s - m)
            l = jnp.sum(e, axis=0, keepdims=True)
            lse_ref[hh:hh + 1, :] = m + jnp.log(l)
            return (e * (1.0 / l)).astype(BF16), sl

        def weighted_values(p, sl):
            outs.append(sum(lax.dot_general(p[_key_block(j), :], vj[:, sl], TN, preferred_element_type=F32)
                            for j, vj in enumerate((v0, v1, v2))))

        ready = probabilities(0)
        for hh in range(1, HEADS_PER_STEP):
            following = probabilities(hh)
            weighted_values(*ready)
            ready = following
        weighted_values(*ready)
        o_ref[...] = jnp.concatenate(outs, axis=1).astype(o_ref.dtype)

    return _carry(
        riders, body, name="attn_fwd", grid=(N_HEAD_GROUPS, nb),
        in_specs=[pl.BlockSpec((SUPER, HEAD_COLS), lambda hg, i: (i, hg))]
        + _band_specs(nb, N_HEAD_GROUPS, False) + _band_specs(nb, 0, False) + [_bias_spec()],
        out_specs=[pl.BlockSpec((SUPER, HEAD_COLS), lambda hg, i: (i, hg)),
                   pl.BlockSpec((None, LSE_ROWS, SUPER), lambda hg, i: (hg, 0, i))],
        out_shape=[jax.ShapeDtypeStruct((s_len, D_MODEL), BF16),
                   jax.ShapeDtypeStruct((N_HEAD_GROUPS, LSE_ROWS, s_len), F32)],
        semantics=("parallel", "arbitrary"),
    )(qkn, qkn, qkn, qkn, v, v, v, bias)


def _attn_bwd(qkn, v, out, d_out, bias, lse, riders=()):
    s_len = qkn.shape[0]
    nb = s_len // SUPER

    def body(q_ref, k0, k1, k2, v0, v1, v2, o_ref, do_ref, b_ref, lse_ref, dp_ref, db_ref, aq_ref, ak_ref, av_ref):
        i = pl.program_id(1)

        @pl.when(i == 0)
        def _():
            aq_ref[...] = jnp.zeros_like(aq_ref)
            ak_ref[...] = jnp.zeros_like(ak_ref)
            av_ref[...] = jnp.zeros_like(av_ref)
            db_ref[...] = jnp.zeros_like(db_ref)

        @pl.when(i < nb)
        def _():
            dq, dk, dv = [], [[], [], []], [[], [], []]
            ones = jnp.ones((8, HEAD_DIM), BF16)

            def softmax_grad(hh):
                sl = _head(hh)
                qs, do = q_ref[:, sl], do_ref[:, sl]
                kn = [k0[:, sl], k1[:, sl], k2[:, sl]]
                prod = do.astype(F32) * o_ref[:, sl].astype(F32)
                hi = prod.astype(BF16)
                lo = (prod - hi.astype(F32)).astype(BF16)
                delta = (lax.dot_general(ones, hi, NT, preferred_element_type=F32)
                         + lax.dot_general(ones, lo, NT, preferred_element_type=F32))[0:1]
                lse_row = lse_ref[hh:hh + 1, :]
                pb, dsb = [], []
                for j, (kj, vj) in enumerate(zip(kn, (v0, v1, v2))):
                    rows = _key_block(j)
                    p = jnp.exp(lax.dot_general(kj, qs, NT, preferred_element_type=F32) + b_ref[hh, rows, :] - lse_row)
                    ds = p * (lax.dot_general(vj[:, sl], do, NT, preferred_element_type=F32) - delta)
                    db_ref[hh, rows, :] += ds
                    pb.append(p.astype(BF16))
                    dsb.append(ds.astype(BF16))
                return pb, dsb, qs, do, kn

            def operand_grads(pb, dsb, qs, do, kn):
                dq.append(sum(lax.dot_general(dsb[j], kn[j], TN, preferred_element_type=F32) for j in range(3)))
                for j in range(3):
                    dv[j].append(jnp.dot(pb[j], do, preferred_element_type=F32))
                    dk[j].append(jnp.dot(dsb[j], qs, preferred_element_type=F32))

            ready = softmax_grad(0)
            for hh in range(1, HEADS_PER_STEP):
                following = softmax_grad(hh)
                operand_grads(*ready)
                ready = following
            operand_grads(*ready)
            aq_ref[i % 3] = jnp.concatenate(dq, axis=1)
            for j in range(3):
                slot = (i + 1 + j) % 3
                if j < 2:
                    ak_ref[slot] += jnp.concatenate(dk[j], axis=1)
                    av_ref[slot] += jnp.concatenate(dv[j], axis=1)
                else:
                    ak_ref[slot] = jnp.concatenate(dk[j], axis=1)
                    av_ref[slot] = jnp.concatenate(dv[j], axis=1)

        slot = (i + 1) % 3
        dp_ref[0] = aq_ref[slot].astype(dp_ref.dtype)
        dp_ref[1] = ak_ref[slot].astype(dp_ref.dtype)
        dp_ref[2] = av_ref[slot].astype(dp_ref.dtype)

    def qrow(hg, i):
        return (jnp.minimum(i, nb - 1), hg)

    return _carry(
        riders, body, name="attn_bwd", grid=(N_HEAD_GROUPS, nb + 2),
        in_specs=[pl.BlockSpec((SUPER, HEAD_COLS), qrow)]
        + _band_specs(nb, N_HEAD_GROUPS, True) + _band_specs(nb, 0, True)
        + [pl.BlockSpec((SUPER, HEAD_COLS), qrow), pl.BlockSpec((SUPER, HEAD_COLS), qrow), _bias_spec(),
           pl.BlockSpec((None, LSE_ROWS, SUPER), lambda hg, i: (hg, 0, jnp.minimum(i, nb - 1)))],
        out_specs=[pl.BlockSpec((3, SUPER, HEAD_COLS), lambda hg, i: (0, jnp.maximum(i - 2, 0), hg)),
                   pl.BlockSpec((HEADS_PER_STEP, BAND, SUPER), lambda hg, i: (hg, 0, 0))],
        out_shape=[jax.ShapeDtypeStruct((6, s_len, D_MODEL), BF16),
                   jax.ShapeDtypeStruct((N_HEADS, BAND, SUPER), F32)],
        scratch_shapes=[pltpu.VMEM((3, SUPER, HEAD_COLS), F32)] * 3,
        semantics=("parallel", "arbitrary"),
    )(qkn, qkn, qkn, qkn, v, v, v, out, d_out, bias, lse)


def _qk_norm_bwd(dproj6, qk_raw, gq, gk):
    s_len = qk_raw.shape[0]
    ts = min(1024, s_len)

    nsteps = s_len // ts
    half = D_MODEL // 2

    def body(d_ref, raw_ref, gq_ref, gk_ref, o_ref, dgq_ref, dgk_ref, acc_ref):
        step = pl.program_id(0)

        @pl.when(step == 0)
        def _():
            acc_ref[...] = jnp.zeros_like(acc_ref)

        for piece, (g_ref, scale) in enumerate(((gq_ref, QK_SCALE), (gk_ref, 1.0))):
            for c0 in (0, half):
                xhat, r = _head_unit(raw_ref[:, piece * D_MODEL + c0:piece * D_MODEL + c0 + half].astype(F32))
                dn = d_ref[piece, :, c0:c0 + half].astype(F32) * scale
                u = dn * g_ref[...]
                dx = r * (u - xhat * (_head_sums(u * xhat) * (1.0 / HEAD_DIM)))
                o_ref[piece, :, c0:c0 + half] = dx.astype(o_ref.dtype)
                acc_ref[piece:piece + 1, c0:c0 + half] += jnp.sum(dn * xhat, axis=0, keepdims=True)

        @pl.when(step == nsteps - 1)
        def _():
            lane = lax.broadcasted_iota(jnp.int32, (D_MODEL, LANE), 0) % HEAD_DIM
            fold = (lane == lax.broadcasted_iota(jnp.int32, (D_MODEL, LANE), 1)).astype(BF16)
            tot = sum(jnp.dot(p, fold, preferred_element_type=F32) for p in _split3(acc_ref[...]))
            dgq_ref[...] = tot[0:1, :HEAD_DIM]
            dgk_ref[...] = tot[1:2, :HEAD_DIM]

    gain = pl.BlockSpec((1, half), lambda i: (0, 0))
    small = pl.BlockSpec((1, HEAD_DIM), lambda i: (0, 0))
    per_head = lambda g: jnp.tile(g, (1, half // HEAD_DIM))
    return pl.pallas_call(
        body, name="qk_norm_bwd", grid=(nsteps,),
        in_specs=[pl.BlockSpec((2, ts, D_MODEL), lambda i: (0, i, 0)),
                  pl.BlockSpec((ts, 2 * D_MODEL), lambda i: (i, 0)), gain, gain],
        out_specs=[pl.BlockSpec((2, ts, D_MODEL), lambda i: (0, i, 0)), small, small],
        out_shape=[jax.ShapeDtypeStruct(dproj6.shape, dproj6.dtype),
                   jax.ShapeDtypeStruct((1, HEAD_DIM), F32), jax.ShapeDtypeStruct((1, HEAD_DIM), F32)],
        scratch_shapes=[pltpu.VMEM((8, D_MODEL), F32)],
        input_output_aliases={0: 0},
        compiler_params=_params("arbitrary"),
    )(dproj6, qk_raw, per_head(gq), per_head(gk))


CONV_ROWS = 512
HALO = 16


def _rows_with_halo(ref, r0, n, front, s_len):
    zeros = jnp.zeros((HALO, ref.shape[1]), F32)
    if front:
        return (jnp.concatenate([zeros, ref[0:n, :].astype(F32)], axis=0) if r0 == 0
                else ref[r0 - HALO:r0 + n, :].astype(F32))
    return (jnp.concatenate([ref[r0:r0 + n, :].astype(F32), zeros], axis=0) if r0 + n == s_len
            else ref[r0:r0 + n + HALO, :].astype(F32))


def _earlier(ext, k):
    return pltpu.roll(ext, k, 0)[HALO:]


def _later(ext, k):
    n = ext.shape[0]
    return pltpu.roll(ext, n - k, 0)[:n - HALO]


def _conv_cols(col0):
    return lambda s_len: pl.BlockSpec((s_len, LANE), lambda j: (0, col0 + j))


def _conv_fwd(proj, conv_w, conv_b, riders=()):
    s_len = proj.shape[0]

    def body(bg_ref, cg_ref, xc_ref, w_ref, b_ref, o_ref):
        w = [w_ref[t:t + 1, :] for t in range(3)]
        for r0 in range(0, s_len, CONV_ROWS):
            u = _rows_with_halo(cg_ref, r0, CONV_ROWS, True, s_len) * \
                _rows_with_halo(xc_ref, r0, CONV_ROWS, True, s_len)
            conv = b_ref[...] + w[0] * _earlier(u, 2) + w[1] * _earlier(u, 1) + w[2] * u[HALO:]
            o_ref[r0:r0 + CONV_ROWS, :] = (bg_ref[r0:r0 + CONV_ROWS, :].astype(F32) * conv).astype(o_ref.dtype)

    return _carry(
        riders, body, name="conv_fwd", grid=(D_MODEL // LANE,),
        in_specs=[_conv_cols(0)(s_len), _conv_cols(8)(s_len), _conv_cols(16)(s_len),
                  pl.BlockSpec((3, LANE), lambda j: (0, j)), pl.BlockSpec((1, LANE), lambda j: (0, j))],
        out_specs=pl.BlockSpec((s_len, LANE), lambda j: (0, j)),
        out_shape=jax.ShapeDtypeStruct((s_len, D_MODEL), BF16),
        semantics=("parallel",),
    )(proj, proj, proj, conv_w, conv_b)


def _conv_bwd(dproj6, dy, proj, conv_w, conv_b, riders=()):
    s_len = proj.shape[0]

    def body(dy_ref, bg_ref, cg_ref, xc_ref, w_ref, b_ref, _, dp_ref, dw_ref):
        w = [w_ref[t:t + 1, :] for t in range(3)]
        acc = [jnp.zeros((1, LANE), F32) for _ in range(4)]
        for r0 in range(0, s_len, CONV_ROWS):
            rows = slice(r0, r0 + CONV_ROWS)
            u = _rows_with_halo(cg_ref, r0, CONV_ROWS, True, s_len) * \
                _rows_with_halo(xc_ref, r0, CONV_ROWS, True, s_len)
            u2, u1, u0 = _earlier(u, 2), _earlier(u, 1), u[HALO:]
            conv = b_ref[...] + w[0] * u2 + w[1] * u1 + w[2] * u0
            dp_ref[0, rows, :] = (dy_ref[rows, :].astype(F32) * conv).astype(dp_ref.dtype)
            dconv_ext = _rows_with_halo(dy_ref, r0, CONV_ROWS, False, s_len) * \
                _rows_with_halo(bg_ref, r0, CONV_ROWS, False, s_len)
            dconv = dconv_ext[:CONV_ROWS]
            for t, term in enumerate([dconv * u2, dconv * u1, dconv * u0, dconv]):
                acc[t] = acc[t] + jnp.sum(term, axis=0, keepdims=True)
            du = w[2] * dconv + w[1] * _later(dconv_ext, 1) + w[0] * _later(dconv_ext, 2)
            dp_ref[1, rows, :] = (du * xc_ref[rows, :].astype(F32)).astype(dp_ref.dtype)
            dp_ref[2, rows, :] = (du * cg_ref[rows, :].astype(F32)).astype(dp_ref.dtype)
        dw_ref[...] = jnp.zeros_like(dw_ref)
        for t in range(4):
            dw_ref[t:t + 1, :] = acc[t]

    return _carry(
        riders, body, name="conv_bwd", grid=(D_MODEL // LANE,),
        in_specs=[pl.BlockSpec((s_len, LANE), lambda j: (0, j)),
                  _conv_cols(0)(s_len), _conv_cols(8)(s_len), _conv_cols(16)(s_len),
                  pl.BlockSpec((3, LANE), lambda j: (0, j)), pl.BlockSpec((1, LANE), lambda j: (0, j)),
                  pl.BlockSpec(memory_space=pl.ANY)],
        out_specs=[pl.BlockSpec((3, s_len, LANE), lambda j: (1, 0, j)),
                   pl.BlockSpec((8, LANE), lambda j: (0, j))],
        out_shape=[jax.ShapeDtypeStruct(dproj6.shape, dproj6.dtype),
                   jax.ShapeDtypeStruct((8, D_MODEL), F32)],
        input_output_aliases={6: 0},
        semantics=("parallel",),
    )(dy, proj, proj, proj, conv_w, conv_b, dproj6)


class _NoComm:
    def __init__(self, early, late):
        self.early, self.late = early, late

    def riders(self, at):
        return ()

    def done(self, at, carried):
        pass

    def early_weights(self):
        return self.early[0], self.early[2]

    def gate_weight(self):
        return self.early[1]

    def late_weights(self):
        return self.late

    def grads_ready(self, group, grads):
        pass


def _d_input(name, terms, riders=()):
    s_len, kdim = terms[0][0].shape[1], terms[0][1].shape[1]
    tm, chunk = 512, 512
    n = len(terms)

    def body(*refs):
        o_ref = refs[-1]
        acc = jnp.zeros(o_ref.shape, F32)
        for p_ref, w_ref in zip(refs[:n], refs[n:2 * n]):
            n_piece, _, width = p_ref.shape
            per_shard = w_ref.shape[2]
            for c0 in range(0, n_piece * width, chunk):
                acc = acc + lax.dot_general(p_ref[c0 // width, :, c0 % width:c0 % width + chunk],
                                            w_ref[c0 // per_shard, :, c0 % per_shard:c0 % per_shard + chunk],
                                            NT, preferred_element_type=F32)
        o_ref[...] = acc.astype(o_ref.dtype)

    tile = pl.BlockSpec((tm, kdim), lambda i: (i, 0))
    return _carry(
        riders, body, name=name, grid=(s_len // tm,),
        in_specs=[pl.BlockSpec((p.shape[0], tm, p.shape[2]), lambda i: (0, i, 0)) for p, _ in terms]
        + [pl.BlockSpec(w.shape, lambda i: (0, 0, 0), pipeline_mode=pl.Buffered(1)) for _, w in terms],
        out_specs=tile, out_shape=jax.ShapeDtypeStruct((s_len, kdim), BF16),
        semantics=("parallel",),
    )(*[p for p, _ in terms], *[w for _, w in terms])


def _local_grads(x, target, norm1_g, q_norm_g, k_norm_g, rel_bias, conv_b, b_gate, norm2_g, comm):
    s_len = x.shape[0]
    tm = min(1024, s_len)
    tb = min(2048, s_len)
    row = lambda v: v.reshape(1, -1)

    def carrying(at, fn, *args, **kw):
        riders = comm.riders(at)
        out = fn(*args, riders=riders, **kw)
        if riders:
            out, carried = out
            comm.done(at, carried)
        return out

    bias = carrying("bias_expand", _bias_expand, rel_bias)
    h = carrying("norm1", _rms_fwd, "norm1", x, row(norm1_g))
    w_in3, conv_w = comm.early_weights()
    gq, gk = row(q_norm_g), row(k_norm_g)

    def qk_epi(acc, e, o):
        o[0][...] = acc.astype(BF16)
        gain = jnp.where(pl.program_id(1) < 2, e[0] * QK_SCALE, e[1])
        o[1][...] = (_head_unit(acc)[0] * gain).astype(BF16)
    small = pl.BlockSpec((1, 512), lambda i, j, k: (0, 0))
    per_head = lambda g: jnp.tile(g, (1, 512 // HEAD_DIM))
    qk_raw, qkn = carrying("proj_qk", _mm_fwd, "proj_qk", h, w_in3, tm, 512, D_MODEL, ncols=2 * D_MODEL,
                           epilogue=qk_epi, outs=[_out2d(s_len, 2 * D_MODEL, BF16, tm, 512)] * 2,
                           extras=[(per_head(gq), small), (per_head(gk), small)])
    v = carrying("proj_v", _mm_fwd, "proj_v", h, w_in3, tb, 512, D_MODEL, col0=4, ncols=D_MODEL,
                 outs=[_out2d(s_len, D_MODEL, BF16, tb, 512)])
    w_gate3 = comm.gate_weight()
    conv_in = _mm_fwd("proj_conv", h, w_in3, tb, 1536, D_MODEL, col0=2, ncols=3 * D_MODEL,
                      outs=[_out2d(s_len, 3 * D_MODEL, BF16, tb, 1536)])

    def gate_epi(acc, e, o):
        o[0][...] = jax.nn.sigmoid(acc + e[0]).astype(BF16)
    gates = _mm_fwd("gates", h, w_gate3, tb, 512, D_MODEL, epilogue=gate_epi,
                    outs=[_out2d(s_len, 2 * D_MODEL, BF16, tb, 512)],
                    extras=[(row(b_gate), pl.BlockSpec((1, 512), lambda i, j, k: (0, j)))])

    attn, lse = carrying("attn_fwd", _attn_fwd, qkn, v, bias)
    yconv = carrying("conv_fwd", _conv_fwd, conv_in, conv_w, row(conv_b))
    w_ap, w_cp, w_out, w_up3, w_down = comm.late_weights()
    tw = 1024
    ya = _mm_fwd("attn_proj", attn, w_ap, tm, tw, D_MODEL, outs=[_out2d(s_len, D_MODEL, BF16, tm, tw)])

    def merge_epi(acc, e, o):
        ya_v, ga, gc = [t.astype(F32) for t in e]
        o[0][...] = acc.astype(BF16)
        o[1][...] = (ga * ya_v + gc * acc).astype(BF16)
    gate_a, gate_c = (gates, _tile_spec(tm, tw, 0)), (gates, _tile_spec(tm, tw, 1))
    yc, merged = _mm_fwd("conv_proj", yconv, w_cp, tm, tw, D_MODEL, epilogue=merge_epi,
                         outs=[_out2d(s_len, D_MODEL, BF16, tm, tw), _out2d(s_len, D_MODEL, BF16, tm, tw)],
                         extras=[(ya, _tile_spec(tm, tw)), gate_a, gate_c])

    def res_epi(acc, e, o):
        o[0][...] = e[0] + acc
    x1 = _mm_fwd("out_proj", merged, w_out, tm, tw, D_MODEL, epilogue=res_epi,
                 extras=[(x, _tile_spec(tm, tw))])
    h2 = _rms_fwd("norm2", x1, row(norm2_g))

    def up_epi(acc, e, o):
        o[0][...] = jnp.square(jnp.maximum(acc, 0.0)).astype(BF16)
    act = _mm_fwd("mlp_up", h2, w_up3, tb, tw, D_MODEL, epilogue=up_epi, outs=[_out2d(s_len, D_FF, BF16, tb, tw)])

    def loss_epi(acc, e, o):
        err = e[0] + acc - e[1]
        o[0][...] = (err * (1.0 / D_MODEL)).astype(BF16)
        sq = err * err
        part = sq[:, 0:LANE]
        for c0 in range(LANE, D_MODEL, LANE):
            part = part + sq[:, c0:c0 + LANE]
        o[1][...] = jnp.sum(part.reshape(tl // 8, 8, LANE), axis=0)
    tl = 512
    dy_b, loss_part = _mm_fwd(
        "mlp_down", act, w_down, tl, D_MODEL, D_FF, epilogue=loss_epi,
        outs=[_out2d(s_len, D_MODEL, BF16, tl, D_MODEL),
              (jax.ShapeDtypeStruct((8 * (s_len // tl), LANE), F32), pl.BlockSpec((8, LANE), lambda i, j, k: (i, 0)))],
        extras=[(x1, _tile_spec(tl, D_MODEL)), (target, _tile_spec(tl, D_MODEL))])

    def dup_epi(acc, e, o):
        o[0][...] = (acc * (2.0 * jnp.sqrt(e[0].astype(F32)))).astype(BF16)
    full = lambda cols: pl.BlockSpec((tm, cols), lambda i, j, n: (i, n))
    tokens = lambda cols: pl.BlockSpec((s_len, cols), lambda i, j, m: (m, j))
    dup = _mm_bwd_x("d_act", dy_b, full(D_MODEL), w_down, tm, tw, D_MODEL, s_len, D_MODEL, epilogue=dup_epi,
                    outs=[_out2d(s_len, D_FF, BF16, tm, tw)], extras=[(act, _tile_spec(tm, tw))])
    g_down = _mm_bwd_w("g_down", act, dy_b, tokens(D_MODEL), D_MODEL, 512, D_MODEL, s_len, False)
    g_up = _mm_bwd_w("g_up", h2, dup, tokens(512), D_FF, D_MODEL, 512, s_len, True)
    comm.grads_ready("mlp", dict(w_down=g_down, w_up=g_up))
    dh2 = carrying("d_h2", _d_input, "d_h2", [(dup.reshape(1, s_len, D_FF), w_up3)])
    dx1_b, dg2 = _rms_bwd("norm2_bwd", dh2, x1, row(norm2_g), dy_b)

    def dmerge_epi(acc, e, o):
        ya_v, yc_v, ga, gc = [t.astype(F32) for t in e]
        o[0][...] = (acc * ga).astype(BF16)
        o[1][...] = (acc * gc).astype(BF16)
        o[2][0] = (acc * ya_v * ga * (1.0 - ga)).astype(BF16)
        o[2][1] = (acc * yc_v * gc * (1.0 - gc)).astype(BF16)
    dya, dyc, dgp2 = _mm_bwd_x(
        "d_merged", dx1_b, full(D_MODEL), w_out, tm, tw, D_MODEL, s_len, D_MODEL, epilogue=dmerge_epi,
        outs=[_out2d(s_len, D_MODEL, BF16, tm, tw), _out2d(s_len, D_MODEL, BF16, tm, tw),
              (jax.ShapeDtypeStruct((2, s_len, D_MODEL), BF16), pl.BlockSpec((2, tm, tw), lambda i, j, n: (0, i, j)))],
        extras=[(ya, _tile_spec(tm, tw)), (yc, _tile_spec(tm, tw)), gate_a, gate_c])
    g_out = _mm_bwd_w("g_out", merged, dx1_b, tokens(512), D_MODEL, D_MODEL, 512, s_len, False)
    d_attn = _mm_bwd_x("d_attn", dya, full(D_MODEL), w_ap, tm, tw, D_MODEL, s_len, D_MODEL,
                       outs=[_out2d(s_len, D_MODEL, BF16, tm, tw)])
    g_ap = _mm_bwd_w("g_attn_proj", attn, dya, tokens(512), D_MODEL, D_MODEL, 512, s_len, False)
    d_yconv = _mm_bwd_x("d_yconv", dyc, full(D_MODEL), w_cp, tm, tw, D_MODEL, s_len, D_MODEL,
                        outs=[_out2d(s_len, D_MODEL, BF16, tm, tw)])
    g_cp = _mm_bwd_w("g_conv_proj", yconv, dyc, tokens(512), D_MODEL, D_MODEL, 512, s_len, False)
    piece = lambda width: (lambda blk: (blk * width) // D_MODEL, lambda blk: (blk * width % D_MODEL) // width)
    pc, cb = piece(512)
    pieces = pl.BlockSpec((None, s_len, 512), lambda i, j, m: (pc(j), m, cb(j)))
    g_gate = _mm_bwd_w("g_gate", h, dgp2, pieces, 2 * D_MODEL, D_MODEL, 512, s_len, True)
    comm.grads_ready("proj", dict(w_out=g_out, w_attn_proj=g_ap, w_conv_proj=g_cp, w_gate=g_gate))

    dproj6, dbias = carrying("attn_bwd", _attn_bwd, qkn, v, attn, d_attn, bias, lse)
    dproj6, dgq, dgk = _qk_norm_bwd(dproj6, qk_raw, gq, gk)
    dproj6, dconv_wb = carrying("conv_bwd", _conv_bwd, dproj6, d_yconv, conv_in, conv_w, row(conv_b))

    g_in = carrying("g_in", _mm_bwd_w, "g_in", h, dproj6, pieces, 6 * D_MODEL, D_MODEL, 512, s_len, True)
    comm.grads_ready("in", dict(w_in=g_in))
    d_rel = carrying("bias_reduce", _bias_reduce, dbias)
    dh = carrying("d_h", _d_input, "d_h", [(dproj6, w_in3), (dgp2, w_gate3)])
    grad_x, dg1 = carrying("norm1_bwd", _rms_bwd, "norm1_bwd", dh, x, row(norm1_g), dx1_b, out_dtype=F32)

    def bsum(t, f):
        return [], [t[0].astype(F32), t[1].astype(F32)]
    ts = 512
    db_a, db_c = carrying("b_gate_sum", _ew, "b_gate_sum", bsum,
                          [(dgp2, pl.BlockSpec((None, ts, D_MODEL), lambda i: (0, i, 0))),
                           (dgp2, pl.BlockSpec((None, ts, D_MODEL), lambda i: (1, i, 0)))],
                          [], [], sums=[D_MODEL, D_MODEL], ts=ts)

    big = dict(w_in=g_in, w_attn_proj=g_ap, w_conv_proj=g_cp, w_gate=g_gate, w_out=g_out,
               w_up=g_up, w_down=g_down)
    small = dict(norm1_g=dg1, norm2_g=dg2, conv_wb=dconv_wb, b_gate=(db_a, db_c),
                 q_norm_g=dgq, k_norm_g=dgk, rel_bias=d_rel)
    return loss_part, grad_x, big, small


def _finish_loss(loss_part):
    def body(l_ref, lo_ref):
        total = jnp.sum(jnp.sum(l_ref[...], axis=0, keepdims=True), axis=1, keepdims=True)
        lo_ref[...] = jnp.broadcast_to(total * (0.5 / D_MODEL), lo_ref.shape)

    return pl.pallas_call(body, name="finish_loss", out_shape=jax.ShapeDtypeStruct((8, LANE), F32))(loss_part)


HBM_SPEC = pl.BlockSpec(memory_space=pl.ANY)


def _place():
    return lax.axis_index("x"), lax.axis_index("y"), lax.axis_index("c")


def _other_chips(x, y):
    return [(1 - x, y), (x, 1 - y), (1 - x, 1 - y)]


def _cast_into_slot(name, where, w):
    r, cols = w.shape
    ts = 256

    def body(w_ref, x_ref, o_ref):
        o_ref[...] = x_ref[...].astype(o_ref.dtype)

    return pl.pallas_call(
        body, name=name,
        grid_spec=pltpu.PrefetchScalarGridSpec(
            num_scalar_prefetch=1, grid=(r // ts,),
            in_specs=[pl.BlockSpec((ts, cols), lambda i, w: (i, 0))],
            out_specs=pl.BlockSpec((None, ts, cols), lambda i, w: (w[0], i, 0))),
        out_shape=jax.ShapeDtypeStruct((N_SHARD, r, cols), BF16),
        compiler_params=_params("parallel"),
    )(where, w)


def _remote(src, dst, send, recv, k, to):
    return pltpu.make_async_remote_copy(src_ref=src, dst_ref=dst, send_sem=send.at[k], recv_sem=recv.at[k],
                                        device_id=to, device_id_type=MESH)


def _gather_riders(slots):
    n = len(slots)

    def half(refs, w, shard, which):
        hr = slots[w].shape[1] // 2
        return refs[w].at[shard, pl.ds(which * hr, hr)]

    def each(fn):
        x, y, c = _place()
        for w in range(n):
            for j, chip in enumerate(_other_chips(x, y)):
                fn(w, 3 * w + j, 2 * x + y, 2 * chip[0] + chip[1], c, (*chip, c), (x, y, 1 - c))

    def chips_start(_, refs, send, recv):
        each(lambda w, k, me, them, c, peer, sib: _remote(half(refs, w, me, c), half(refs, w, me, c), send, recv, k, peer).start())

    def chips_finish(_, refs, send, recv):
        each(lambda w, k, me, them, c, peer, sib: _remote(half(refs, w, them, c), half(refs, w, them, c), send, recv, k, peer).wait_recv())
        each(lambda w, k, me, them, c, peer, sib: _remote(half(refs, w, me, c), half(refs, w, me, c), send, recv, k, peer).wait_send())

    def sibling_start(_, refs, send, recv):
        each(lambda w, k, me, them, c, peer, sib: _remote(half(refs, w, them, c), half(refs, w, them, c), send, recv, k, sib).start())

    def sibling_finish(_, refs, send, recv):
        each(lambda w, k, me, them, c, peer, sib: _remote(half(refs, w, them, 1 - c), half(refs, w, them, 1 - c), send, recv, k, sib).wait_recv())
        each(lambda w, k, me, them, c, peer, sib: _remote(half(refs, w, them, c), half(refs, w, them, c), send, recv, k, sib).wait_send())

    return (lambda s: _Rider([], s, 3 * n, chips_start, chips_finish),
            lambda s: _Rider([], s, 3 * n, sibling_start, sibling_finish))


def _pair_exchange_rider(grads, landing):
    n = len(grads)

    def copies(srcs, dsts, send, recv):
        x, y, c = _place()
        out = []
        for w in range(n):
            hr = grads[w].shape[1] // 2
            out.append(_remote(srcs[w].at[:, pl.ds((1 - c) * hr, hr)], dsts[w], send, recv, w, (x, y, 1 - c)))
        return out

    def start(srcs, dsts, send, recv):
        for cp in copies(srcs, dsts, send, recv):
            cp.start()

    def finish(srcs, dsts, send, recv):
        for cp in copies(srcs, dsts, send, recv):
            cp.wait()

    return _Rider(grads, landing, n, start, finish)


def _row_tile(hr):
    return min(hr, 256)


def _pair_add(name, where, grad, got):
    _, hr, cols = got.shape
    tr = _row_tile(hr)
    nblk = hr // tr

    def body(w_ref, g_ref, r_ref, o_ref):
        o_ref[...] = (g_ref[...] + r_ref[...]).astype(o_ref.dtype)

    other = lambda k, w: (w[0] + 1 + k) % N_SHARD
    return pl.pallas_call(
        body, name=name,
        grid_spec=pltpu.PrefetchScalarGridSpec(
            num_scalar_prefetch=1, grid=(N_SHARD - 1, nblk),
            in_specs=[pl.BlockSpec((None, tr, cols), lambda k, i, w: (other(k, w), w[1] * nblk + i, 0)),
                      pl.BlockSpec((None, tr, cols), lambda k, i, w: (other(k, w), i, 0))],
            out_specs=pl.BlockSpec((None, tr, cols), lambda k, i, w: (other(k, w), i, 0))),
        out_shape=jax.ShapeDtypeStruct(got.shape, BF16),
        compiler_params=_params("parallel", "parallel"),
    )(where, grad, got)


def _chip_exchange_rider(partials, landing):
    n = len(partials)

    def copies(srcs, dsts, send, recv):
        x, y, c = _place()
        return [_remote(srcs[w].at[2 * chip[0] + chip[1]], dsts[w].at[j], send, recv, 3 * w + j, (*chip, c))
                for w in range(n) for j, chip in enumerate(_other_chips(x, y))]

    def start(srcs, dsts, send, recv):
        for cp in copies(srcs, dsts, send, recv):
            cp.start()

    def finish(srcs, dsts, send, recv):
        for cp in copies(srcs, dsts, send, recv):
            cp.wait()

    return _Rider(partials, landing, 3 * n, start, finish)


def _final_add(name, where, grad, got, arrived):
    _, hr, cols = got.shape
    tr = _row_tile(hr)
    nblk = hr // tr

    def body(w_ref, g_ref, r_ref, a_ref, o_ref):
        acc = g_ref[...] + r_ref[...]
        for j in range(3):
            acc = acc + a_ref[j].astype(F32)
        o_ref[...] = acc

    return pl.pallas_call(
        body, name=name,
        grid_spec=pltpu.PrefetchScalarGridSpec(
            num_scalar_prefetch=1, grid=(nblk,),
            in_specs=[pl.BlockSpec((None, tr, cols), lambda i, w: (w[0], w[1] * nblk + i, 0)),
                      pl.BlockSpec((None, tr, cols), lambda i, w: (w[0], i, 0)),
                      pl.BlockSpec((3, tr, cols), lambda i, w: (0, i, 0))],
            out_specs=pl.BlockSpec((tr, cols), lambda i, w: (w[1] * nblk + i, 0))),
        out_shape=jax.ShapeDtypeStruct((2 * hr, cols), F32),
        compiler_params=_params("parallel"),
    )(where, grad, got, arrived)


def _pair_share_rider(shards):
    n = len(shards)

    def half(refs, w, which):
        hr = shards[w].shape[0] // 2
        return refs[w].at[pl.ds(which * hr, hr)]

    def start(_, refs, send, recv):
        x, y, c = _place()
        for w in range(n):
            _remote(half(refs, w, c), half(refs, w, c), send, recv, w, (x, y, 1 - c)).start()

    def finish(_, refs, send, recv):
        x, y, c = _place()
        for w in range(n):
            _remote(half(refs, w, 1 - c), half(refs, w, 1 - c), send, recv, w, (x, y, 1 - c)).wait_recv()
        for w in range(n):
            _remote(half(refs, w, c), half(refs, w, c), send, recv, w, (x, y, 1 - c)).wait_send()

    return _Rider([], shards, n, start, finish)


class _Exchange:
    PLAN = {"bias_expand": [("early", "gather")], "norm1": [("early", "forward")],
            "proj_qk": [("gate", "gather")], "proj_v": [("gate", "forward")],
            "attn_fwd": [("late", "gather")], "conv_fwd": [("late", "forward")],
            "d_h2": [("mlp", "pair")], "attn_bwd": [("mlp", "chips"), ("proj", "pair")], "conv_bwd": [("mlp", "share")],
            "g_in": [("proj", "chips")], "bias_reduce": [("proj", "share"), ("in", "pair")],
            "d_h": [("in", "chips")], "b_gate_sum": [("in", "share")]}

    def __init__(self, where, early_slots, gate_slots, late_slots):
        self.where = where
        self.slots = dict(early=early_slots, gate=gate_slots, late=late_slots)
        self.stage = {g: dict(zip(("gather", "forward"), _gather_riders(s))) for g, s in self.slots.items()}
        self.groups, self.reduced, self.pending = {}, {}, []

    def early_weights(self):
        w_in3, small = self.slots["early"]
        conv_w = small[:, :3, :].transpose(1, 0, 2).reshape(3, N_SHARD * small.shape[2])
        return w_in3, conv_w

    def gate_weight(self):
        return self.slots["gate"][0]

    def late_weights(self):
        rows = lambda a: a.reshape(a.shape[0] * a.shape[1], a.shape[2])
        w_ap, w_cp, w_out, w_up3, w_down = self.slots["late"]
        return rows(w_ap), rows(w_cp), rows(w_out), w_up3, rows(w_down)

    def grads_ready(self, group, grads):
        names = list(grads)
        g4 = [g if g.ndim == 3 else g.reshape(N_SHARD, -1, g.shape[1]) for g in grads.values()]
        self.groups[group] = dict(names=names, g4=g4)

    def riders(self, at):
        self.pending = self.PLAN.get(at, [])
        out = []
        for group, stage in self.pending:
            if group in self.slots:
                out.append(self.stage[group][stage](self.slots[group]))
                continue
            st = self.groups[group]
            if stage == "pair":
                landing = [lax.empty((N_SHARD, g.shape[1] // 2, g.shape[2]), F32) for g in st["g4"]]
                out.append(_pair_exchange_rider(st["g4"], landing))
            elif stage == "chips":
                landing = [lax.empty((3,) + p.shape[1:], p.dtype) for p in st["partial"]]
                out.append(_chip_exchange_rider(st["partial"], landing))
            else:
                out.append(_pair_share_rider(st["halves"]))
        return out

    def done(self, at, carried):
        for (group, stage), arrays in zip(self.pending, carried):
            if group in self.slots:
                self.slots[group] = arrays
                continue
            st = self.groups[group]
            tag = lambda what, n: what + "_" + n
            if stage == "pair":
                st["got"] = arrays
                st["partial"] = [_pair_add(tag("pair_add", n), self.where, g, r)
                                 for n, g, r in zip(st["names"], st["g4"], arrays)]
            elif stage == "chips":
                st["halves"] = [_final_add(tag("final_add", n), self.where, g, r, a)
                                for n, g, r, a in zip(st["names"], st["g4"], st["got"], arrays)]
            else:
                self.reduced.update(zip(st["names"], arrays))


SMALL_ROWS = 32
N_DEV = 8


def _all_reduce_small(pack):
    def body(p_ref, o_ref, buf, send, recv):
        x, y, c = _place()
        buf[4 * x + 2 * y + c] = p_ref[...]
        copies, waits = [], []
        for k in range(1, N_DEV):
            px = 1 - x if k & 4 else x
            py = 1 - y if k & 2 else y
            pc = 1 - c if k & 1 else c
            copies.append(_remote(p_ref, buf.at[4 * x + 2 * y + c], send, recv, k - 1, (px, py, pc)))
            waits.append(_remote(p_ref, buf.at[4 * px + 2 * py + pc], send, recv, k - 1, (px, py, pc)))
        for cp in copies:
            cp.start()
        for cp in waits:
            cp.wait_recv()
        acc = buf[0]
        for d in range(1, N_DEV):
            acc = acc + buf[d]
        o_ref[...] = acc
        for cp in copies:
            cp.wait_send()

    return pl.pallas_call(
        body, name="all_reduce_small",
        out_shape=jax.ShapeDtypeStruct(pack.shape, F32),
        scratch_shapes=[pltpu.VMEM((N_DEV,) + pack.shape, F32),
                        pltpu.SemaphoreType.DMA((N_DEV - 1,)), pltpu.SemaphoreType.DMA((N_DEV - 1,))],
    )(pack)


def _adamw(name, w, g, m, v, riders=()):
    c1 = 1.0 - ADAM_B1 ** ADAM_STEP
    c2 = 1.0 - ADAM_B2 ** ADAM_STEP

    def fn(t, f):
        wv, gv, mv, vv = t
        m2 = ADAM_B1 * mv + (1.0 - ADAM_B1) * gv
        v2 = ADAM_B2 * vv + (1.0 - ADAM_B2) * (gv * gv)
        delta = -ADAM_LR * ((m2 / c1) / (jnp.sqrt(v2 / c2) + ADAM_EPS) + ADAM_WD * wv)
        return [delta, m2, v2], []

    cols = w.shape[1]
    return _ew(name, fn, [w, g, m, v], [], [(cols, F32)] * 3, ts=min(w.shape[0], 256), riders=riders)


LOSS_ROW = 26


def _pack_small(norm1_g, norm2_g, conv_b, b_gate, conv_w, q_norm_g, k_norm_g, rel_bias, loss=None):
    pack = jnp.zeros((SMALL_ROWS, D_MODEL), F32)
    for r0, v in ((0, norm1_g), (1, norm2_g), (2, conv_b), (3, b_gate.reshape(2, D_MODEL)), (5, conv_w),
                  (8, q_norm_g), (9, k_norm_g), (10, rel_bias)) + (((LOSS_ROW, loss),) if loss is not None else ()):
        v = v.reshape(-1, v.shape[-1]).astype(F32)
        pack = pack.at[r0:r0 + v.shape[0], :v.shape[1]].set(v)
    return pack


def _unpack_small(pack, conv_cols):
    return dict(norm1_g=pack[0], norm2_g=pack[1], conv_b=pack[2], b_gate=pack[3:5].reshape(2 * D_MODEL),
                conv_w=pack[5:8, :conv_cols], q_norm_g=pack[8, :HEAD_DIM], k_norm_g=pack[9, :HEAD_DIM],
                rel_bias=pack[10:10 + N_HEADS, :N_REL])


BIG = ["w_in", "w_attn_proj", "w_conv_proj", "w_gate", "w_out", "w_up", "w_down"]
LATE = ["w_attn_proj", "w_conv_proj", "w_out", "w_up", "w_down"]
WEIGHTS = ["norm1_g", "w_in", "q_norm_g", "k_norm_g", "rel_bias", "conv_w", "conv_b", "w_attn_proj",
           "w_conv_proj", "w_gate", "b_gate", "w_out", "norm2_g", "w_up", "w_down"]


def kernel(x, norm1_g, w_in, q_norm_g, k_norm_g, rel_bias, conv_w, conv_b, w_attn_proj, w_conv_proj, w_gate, b_gate, w_out, norm2_g, w_up, w_down, loss_target, m_norm1_g, m_w_in, m_q_norm_g, m_k_norm_g, m_rel_bias, m_conv_w, m_conv_b, m_w_attn_proj, m_w_conv_proj, m_w_gate, m_b_gate, m_w_out, m_norm2_g, m_w_up, m_w_down, v_norm1_g, v_w_in, v_q_norm_g, v_k_norm_g, v_rel_bias, v_conv_w, v_conv_b, v_w_attn_proj, v_w_conv_proj, v_w_gate, v_b_gate, v_w_out, v_norm2_g, v_w_up, v_w_down):
    given = dict(locals())
    w = {n: given[n] for n in WEIGHTS}
    m = {n: given["m_" + n] for n in WEIGHTS}
    v = {n: given["v_" + n] for n in WEIGHTS}
    s_len = x.shape[1]
    shard = 2 * lax.axis_index("x") + lax.axis_index("y")
    where = jnp.stack([shard, lax.axis_index("c")]).astype(jnp.int32)
    conv_cols = conv_w.shape[1]

    small_in = lax.dynamic_update_slice(jnp.zeros((N_SHARD, 16, conv_cols), F32), conv_w[None], (shard, 0, 0))
    slot = {n: _cast_into_slot("cast_" + n, where, w[n]) for n in BIG}
    comm = _Exchange(where, [slot["w_in"], small_in], [slot["w_gate"]], [slot[n] for n in LATE])

    loss_part, grad_x, _, small = _local_grads(
        x.reshape(s_len, D_MODEL), loss_target.reshape(s_len, D_MODEL), norm1_g, q_norm_g, k_norm_g,
        rel_bias, conv_b, b_gate, norm2_g, comm)
    grad = dict(comm.reduced)

    loss_local = _finish_loss(loss_part)
    pack = _pack_small(small["norm1_g"], small["norm2_g"], small["conv_wb"][3], jnp.concatenate(small["b_gate"], axis=1),
                       small["conv_wb"][0:3], small["q_norm_g"], small["k_norm_g"], small["rel_bias"],
                       loss=loss_local[0:1, :])
    total = _all_reduce_small(pack)
    g_small = _unpack_small(total, D_MODEL)
    g_small["conv_w"] = lax.dynamic_slice(g_small["conv_w"], (0, shard * conv_cols), (3, conv_cols))
    grad.update(g_small)

    delta, new_m, new_v = {}, {}, {}
    for n in BIG:
        delta[n], new_m[n], new_v[n] = _adamw("adamw_" + n, w[n], grad[n], m[n], v[n])
    small_names = [n for n in WEIGHTS if n not in BIG]
    packs = [_pack_small(**{n: src[n] for n in small_names}) for src in (w, grad, m, v)]
    for out, packed in zip((delta, new_m, new_v), _adamw("adamw_small", *packs)):
        out.update({n: a.reshape(w[n].shape) for n, a in _unpack_small(packed, conv_cols).items()})

    outs = [total[LOSS_ROW, 0], grad_x.reshape(x.shape)]
    for group in (grad, delta, new_m, new_v):
        outs += [group[n].reshape(w[n].shape) for n in WEIGHTS]
    return tuple(outs)
```

```python
import functools

import jax
import jax.numpy as jnp
from jax import lax
from jax.experimental import pallas as pl
from jax.experimental.pallas import tpu as pltpu

F32 = jnp.float32
BF16 = jnp.bfloat16

D_MODEL = 1024
N_HEADS = 16
HEAD_DIM = 64
CHUNK = 64
N_PREV_CHUNKS = 8
MAX_REL = 256
D_FF = 4096
N_REL = 2 * MAX_REL + 1
REL_PAD = 640
EPS = 1e-6
NEG_INF = -1e30
QK_SCALE = HEAD_DIM ** -0.5

SUPER = 4 * CHUNK
BAND = SUPER + N_PREV_CHUNKS * CHUNK
SKEW_W = 1024
N_SHARD = 4
LANE = 128
MXU_DIM = 256
VMEM_LIMIT = 48 * 1024 * 1024

ADAM_LR = 0.001
ADAM_B1 = 0.9
ADAM_B2 = 0.999
ADAM_EPS = 1e-08
ADAM_WD = 0.01
ADAM_STEP = 10

MESH = pl.DeviceIdType.MESH
NN = (((1,), (0,)), ((), ()))
NT = (((1,), (1,)), ((), ()))
TN = (((0,), (0,)), ((), ()))


def _params(*sem):
    return pltpu.CompilerParams(dimension_semantics=sem or None, vmem_limit_bytes=VMEM_LIMIT)


HBM_SPEC = pl.BlockSpec(memory_space=pl.ANY)


class _Rider:
    def __init__(self, sources, arrays, n_sem, start, finish):
        self.sources, self.arrays, self.n_sem, self.start, self.finish = sources, arrays, n_sem, start, finish


def _carry(riders, body, *, name, out_shape, grid=(), in_specs=None, out_specs=None, scratch_shapes=(),
           semantics=(), input_output_aliases=None):
    aliases = dict(input_output_aliases or {})
    if not riders:
        kw = {} if in_specs is None else dict(in_specs=in_specs, out_specs=out_specs)
        return pl.pallas_call(body, name=name, grid=grid, out_shape=out_shape, scratch_shapes=scratch_shapes,
                              input_output_aliases=aliases, compiler_params=_params(*semantics), **kw)
    single = not isinstance(out_shape, (list, tuple))
    shapes = [out_shape] if single else list(out_shape)
    n_out, n_scr = len(shapes), len(scratch_shapes)
    srcs = [a for r in riders for a in r.sources]
    arrs = [a for r in riders for a in r.arrays]
    vmem = pl.BlockSpec(memory_space=pltpu.VMEM)

    def run(*args):
        n_in = len(args)

        def wrapped(*refs):
            pos = n_in
            src_refs = refs[pos:pos + len(srcs)]
            pos += len(srcs) + len(arrs)
            outs = refs[pos:pos + n_out]
            pos += n_out
            arr_refs = refs[pos:pos + len(arrs)]
            pos += len(arrs)
            scratch = refs[pos:pos + n_scr]
            sems = refs[pos + n_scr:]
            first, last = True, True
            for d, size in enumerate(grid):
                first = jnp.logical_and(first, pl.program_id(d) == 0)
                last = jnp.logical_and(last, pl.program_id(d) == size - 1)

            def each(method):
                s0 = a0 = 0
                for k, r in enumerate(riders):
                    getattr(r, method)(src_refs[s0:s0 + len(r.sources)], arr_refs[a0:a0 + len(r.arrays)],
                                       sems[2 * k], sems[2 * k + 1])
                    s0, a0 = s0 + len(r.sources), a0 + len(r.arrays)

            pl.when(first)(lambda: each("start"))
            body(*refs[:n_in], *outs, *scratch)
            pl.when(last)(lambda: each("finish"))

        ins = [vmem] * n_in if in_specs is None else list(in_specs)
        if out_specs is None:
            o_specs = [vmem] * n_out
        else:
            o_specs = [out_specs] if single else list(out_specs)
        for k in range(len(arrs)):
            aliases[n_in + len(srcs) + k] = n_out + k
        res = pl.pallas_call(
            wrapped, name=name, grid=grid,
            in_specs=ins + [HBM_SPEC] * (len(srcs) + len(arrs)),
            out_specs=o_specs + [HBM_SPEC] * len(arrs),
            out_shape=shapes + [jax.ShapeDtypeStruct(a.shape, a.dtype) for a in arrs],
            scratch_shapes=list(scratch_shapes) + [pltpu.SemaphoreType.DMA((r.n_sem,)) for r in riders for _ in range(2)],
            input_output_aliases=aliases,
            compiler_params=_params(*["arbitrary"] * len(grid)),
        )(*args, *srcs, *arrs)
        core, rest = res[:n_out], list(res[n_out:])
        carried, a0 = [], 0
        for r in riders:
            carried.append(rest[a0:a0 + len(r.arrays)])
            a0 += len(r.arrays)
        return (core[0] if single else core), carried

    return run


def _mm(name, dims, a, a_spec, b, b_spec, grid, tile, outs, epilogue=None, extras=(), riders=()):
    nk, ne, no = grid[2], len(extras), len(outs)

    def body(a_ref, b_ref, *refs):
        e_refs, o_refs = refs[:ne], refs[ne:ne + no]
        part = lax.dot_general(a_ref[...], b_ref[...], dims, preferred_element_type=F32)

        def finish(acc):
            if epilogue is None:
                o_refs[0][...] = acc.astype(o_refs[0].dtype)
            else:
                epilogue(acc, [r[...] for r in e_refs], o_refs)

        if nk == 1:
            finish(part)
        else:
            acc_ref = refs[ne + no]
            k = pl.program_id(2)

            @pl.when(k == 0)
            def _():
                acc_ref[...] = part

            @pl.when(k > 0)
            def _():
                acc_ref[...] += part

            @pl.when(k == nk - 1)
            def _():
                finish(acc_ref[...])

    res = _carry(
        riders, body, name=name, grid=grid,
        in_specs=[a_spec, b_spec] + [s for _, s in extras],
        out_specs=[s for _, s in outs],
        out_shape=[s for s, _ in outs],
        scratch_shapes=[pltpu.VMEM(tile, F32)] if nk > 1 else [],
        semantics=("parallel", "parallel", "arbitrary"),
    )(a, b, *[e for e, _ in extras])
    res, carried = res if riders else (res, None)
    res = res[0] if no == 1 else res
    return (res, carried) if riders else res


def _tile_spec(tm, tn, col0=0):
    return pl.BlockSpec((tm, tn), lambda i, j, k: (i, j + col0))


def _out2d(m, n, dtype, tm, tn):
    return (jax.ShapeDtypeStruct((m, n), dtype), _tile_spec(tm, tn))


def _mm_fwd(name, a, w, tm, tn, tk, outs=None, epilogue=None, extras=(), col0=0, ncols=None, riders=()):
    m, kdim = a.shape
    if w.ndim == 3:
        per = w.shape[2] // tn
        n = ncols or N_SHARD * w.shape[2]
        w_spec = pl.BlockSpec((None, tk, tn), lambda i, j, k: ((j + col0) // per, k, (j + col0) % per))
    else:
        n = ncols or w.shape[1]
        w_spec = pl.BlockSpec((tk, tn), lambda i, j, k: (k, j + col0))
    if outs is None:
        outs = [_out2d(m, n, F32, tm, tn)]
    return _mm(name, NN, a, pl.BlockSpec((tm, tk), lambda i, j, k: (i, k)), w, w_spec,
               (m // tm, n // tn, kdim // tk), (tm, tn), outs, epilogue, extras, riders)


def _mm_bwd_x(name, g, g_spec, w, tm, tj, tc, m, n_contract, outs=None, epilogue=None, extras=(), riders=()):
    if w.ndim == 3:
        per = w.shape[2] // tc
        kdim = w.shape[1]
        w_spec = pl.BlockSpec((None, tj, tc), lambda i, j, n: (n // per, j, n % per))
    else:
        kdim = w.shape[0]
        w_spec = pl.BlockSpec((tj, tc), lambda i, j, n: (j, n))
    if outs is None:
        outs = [_out2d(m, kdim, F32, tm, tj)]
    return _mm(name, NT, g, g_spec, w, w_spec, (m // tm, kdim // tj, n_contract // tc),
               (tm, tj), outs, epilogue, extras, riders)


def _mm_bwd_w(name, a, g, g_spec, n, tk, tn, tm, sharded, riders=()):
    m, kdim = a.shape
    if sharded:
        per = (n // N_SHARD) // tn
        out = (jax.ShapeDtypeStruct((N_SHARD, kdim, n // N_SHARD), F32),
               pl.BlockSpec((None, tk, tn), lambda i, j, mm: (j // per, i, j % per)))
    else:
        out = (jax.ShapeDtypeStruct((kdim, n), F32), pl.BlockSpec((tk, tn), lambda i, j, mm: (i, j)))
    return _mm(name, TN, a, pl.BlockSpec((tm, tk), lambda i, j, mm: (mm, i)), g, g_spec,
               (kdim // tk, n // tn, m // tm), (tk, tn), [out], riders=riders)


def _ew(name, fn, tiles, fulls, outs, sums=(), ts=512, riders=()):
    tiles = [t if isinstance(t, tuple) else (t, pl.BlockSpec((ts, t.shape[1]), lambda i: (i, 0)))
             for t in tiles]
    s_rows = tiles[0][0].shape[-2]
    nt, nf, no = len(tiles), len(fulls), len(outs)

    def body(*refs):
        t_vals = [r[...] for r in refs[:nt]]
        f_vals = [r[...] for r in refs[nt:nt + nf]]
        o_refs, s_refs = refs[nt + nf:nt + nf + no], refs[nt + nf + no:]
        o_vals, s_vals = fn(t_vals, f_vals)
        for r, v in zip(o_refs, o_vals):
            r[...] = v.astype(r.dtype)
        for r, v in zip(s_refs, s_vals):
            part = jnp.sum(v, axis=0, keepdims=True)

            @pl.when(pl.program_id(0) == 0)
            def _():
                r[...] = part

            @pl.when(pl.program_id(0) > 0)
            def _():
                r[...] += part

    full_specs = [pl.BlockSpec(f.shape, lambda i, nd=f.ndim: (0,) * nd) for f in fulls]
    return _carry(
        riders, body, name=name, grid=(s_rows // ts,),
        in_specs=[s for _, s in tiles] + full_specs,
        out_specs=[pl.BlockSpec((ts, c), lambda i: (i, 0)) for c, _ in outs]
        + [pl.BlockSpec((1, c), lambda i: (0, 0)) for c in sums],
        out_shape=[jax.ShapeDtypeStruct((s_rows, c), dt) for c, dt in outs]
        + [jax.ShapeDtypeStruct((1, c), F32) for c in sums],
        semantics=("arbitrary",),
    )(*[t for t, _ in tiles], *fulls)


def _rms_fwd(name, x, g, riders=()):
    def fn(t, f):
        xv = t[0]
        r = lax.rsqrt(jnp.mean(xv * xv, axis=-1, keepdims=True) + EPS)
        return [xv * r * f[0]], []
    out = _ew(name, fn, [x], [g], [(x.shape[1], BF16)], riders=riders)
    return (out[0][0], out[1]) if riders else out[0]


def _rms_bwd(name, dh, x, g, dres, out_dtype=BF16, riders=()):
    def fn(t, f):
        dhv, xv, dr = [v.astype(F32) for v in t]
        r = lax.rsqrt(jnp.mean(xv * xv, axis=-1, keepdims=True) + EPS)
        xhat = xv * r
        u = dhv * f[0]
        dx = r * (u - xhat * jnp.mean(u * xhat, axis=-1, keepdims=True)) + dr
        return [dx], [dhv * xhat]
    c = x.shape[1]
    return _ew(name, fn, [dh, x, dres], [g], [(c, out_dtype)], sums=[c], riders=riders)


def _split3(x):
    x1 = x.astype(BF16)
    r1 = x - x1.astype(F32)
    x2 = r1.astype(BF16)
    x3 = (r1 - x2.astype(F32)).astype(BF16)
    return x1, x2, x3


def _rel_class(cp):
    far = (cp < MAX_REL) | (cp > BAND)
    return jnp.where(far, 2 * MAX_REL, BAND - cp)


def _skew_rows(x, sign):
    row = lax.broadcasted_iota(jnp.int32, x.shape, 0)
    for b in range(CHUNK.bit_length() - 1):
        shift = (1 << b) if sign > 0 else SKEW_W - (1 << b)
        x = jnp.where((row >> b) & 1 == 1, pltpu.roll(x, shift, 1), x)
    return x


def _roll_lanes(x, shift):
    return x if shift % SKEW_W == 0 else pltpu.roll(x, shift % SKEW_W, 1)


N_START = 3


def _bias_expand(rel_bias, riders=()):
    rel = jnp.pad(rel_bias, ((0, 0), (0, REL_PAD - N_REL))).reshape(N_HEADS, 1, REL_PAD)

    def body(rel_ref, o_ref):
        cls = lax.broadcasted_iota(jnp.int32, (REL_PAD, SKEW_W), 0)
        cp = lax.broadcasted_iota(jnp.int32, (REL_PAD, SKEW_W), 1)
        onehot = (cls == _rel_class(cp)).astype(BF16)
        rel8 = jnp.broadcast_to(rel_ref[...], (8, REL_PAD))
        trow = sum(jnp.dot(p, onehot, preferred_element_type=F32) for p in _split3(rel8))[0:1]
        first = _skew_rows(jnp.broadcast_to(trow, (CHUNK, SKEW_W)), +1)
        full = jnp.concatenate([_roll_lanes(first, CHUNK * g) for g in range(SUPER // CHUNK)], axis=0)[:, :BAND]
        qc = lax.broadcasted_iota(jnp.int32, (SUPER, BAND), 0) // CHUNK
        kc = lax.broadcasted_iota(jnp.int32, (SUPER, BAND), 1) // CHUNK
        on_band = (kc >= qc) & (kc <= qc + N_PREV_CHUNKS)
        table = jnp.where(on_band, full, NEG_INF).T
        key = lax.broadcasted_iota(jnp.int32, (BAND, SUPER), 0)
        for t in range(N_START):
            o_ref[t] = jnp.where(key < (N_START - 1 - t) * SUPER, NEG_INF, table)

    return _carry(
        riders, body, name="bias_expand", grid=(N_HEADS,),
        in_specs=[pl.BlockSpec((None, 1, REL_PAD), lambda h: (h, 0, 0))],
        out_specs=pl.BlockSpec((N_START, None, BAND, SUPER), lambda h: (0, h, 0, 0)),
        out_shape=jax.ShapeDtypeStruct((N_START, N_HEADS, BAND, SUPER), F32),
        semantics=("arbitrary",),
    )(rel)


def _bias_reduce(dbias, riders=()):
    def body(d_ref, o_ref):
        x = jnp.concatenate([d_ref[...].T, jnp.zeros((SUPER, SKEW_W - BAND), F32)], axis=1)
        folded = sum(_roll_lanes(x[CHUNK * g:CHUNK * (g + 1)], -CHUNK * g) for g in range(SUPER // CHUNK))
        diag = jnp.sum(_skew_rows(folded, -1), axis=0, keepdims=True)
        cp = lax.broadcasted_iota(jnp.int32, (SKEW_W, REL_PAD), 0)
        cls = lax.broadcasted_iota(jnp.int32, (SKEW_W, REL_PAD), 1)
        onehot = (cls == _rel_class(cp)).astype(BF16)
        diag8 = jnp.broadcast_to(diag, (8, SKEW_W))
        o_ref[...] = sum(jnp.dot(p, onehot, preferred_element_type=F32) for p in _split3(diag8))[0:1]

    out = _carry(
        riders, body, name="bias_reduce", grid=(N_HEADS,),
        in_specs=[pl.BlockSpec((None, BAND, SUPER), lambda h: (h, 0, 0))],
        out_specs=pl.BlockSpec((None, 1, REL_PAD), lambda h: (h, 0, 0)),
        out_shape=jax.ShapeDtypeStruct((N_HEADS, 1, REL_PAD), F32),
        semantics=("arbitrary",),
    )(dbias)
    out, carried = out if riders else (out, None)
    out = out.reshape(N_HEADS, REL_PAD)[:, :N_REL]
    return (out, carried) if riders else out


HEADS_PER_STEP = 8
HEAD_COLS = HEADS_PER_STEP * HEAD_DIM
N_HEAD_GROUPS = N_HEADS // HEADS_PER_STEP


def _unit(x):
    r = lax.rsqrt(jnp.mean(x * x, axis=-1, keepdims=True) + EPS)
    return x * r, r


def _scores_t(qs, kn, bias_t):
    return jnp.concatenate([lax.dot_general(k, qs, NT, preferred_element_type=F32) for k in kn], axis=0) + bias_t


def _bias_spec():
    return pl.BlockSpec((None, HEADS_PER_STEP, BAND, SUPER), lambda hg, i: (jnp.minimum(i, N_START - 1), hg, 0, 0))


def _band_specs(nb, col0, clamp_hi):
    def spec(d):
        def index(hg, i):
            blk = jnp.maximum(i - d, 0)
            if clamp_hi:
                blk = jnp.minimum(blk, nb - 1)
            return (blk, col0 + hg)
        return pl.BlockSpec((SUPER, HEAD_COLS), index)
    return [spec(2), spec(1), spec(0)]


def _head_sums(y):
    same_head = (lax.broadcasted_iota(jnp.int32, (MXU_DIM, MXU_DIM), 0) // HEAD_DIM
                 == lax.broadcasted_iota(jnp.int32, (MXU_DIM, MXU_DIM), 1) // HEAD_DIM).astype(BF16)
    sums = []
    for c0 in range(0, y.shape[1], MXU_DIM):
        chunk = y[:, c0:c0 + MXU_DIM]
        hi = chunk.astype(BF16)
        lo = (chunk - hi.astype(F32)).astype(BF16)
        sums.append(jnp.dot(hi, same_head, preferred_element_type=F32)
                    + jnp.dot(lo, same_head, preferred_element_type=F32))
    return jnp.concatenate(sums, axis=1)


def _head_unit(x):
    r = lax.rsqrt(_head_sums(x * x) * (1.0 / HEAD_DIM) + EPS)
    return x * r, r


def _head(hh):
    return slice(HEAD_DIM * hh, HEAD_DIM * (hh + 1))


def _key_block(j):
    return slice(SUPER * j, SUPER * (j + 1))


LSE_ROWS = 8


def _attn_fwd(qkn, v, bias, riders=()):
    s_len = qkn.shape[0]
    nb = s_len // SUPER

    def body(q_ref, k0, k1, k2, v0, v1, v2, b_ref, o_ref, lse_ref):
        outs = []
        v_t = [v0[...].T, v1[...].T, v2[...].T]

        def probabilities(hh):
            sl = _head(hh)
            s = _scores_t(q_ref[:, sl], [k0[:, sl], k1[:, sl], k2[:, sl]], b_ref[hh])
            m = jnp.max(s, axis=0, keepdims=True)
            e = jnp.exp(s - m)
            l = jnp.sum(e, axis=0, keepdims=True)
            lse_ref[hh:hh + 1, :] = m + jnp.log(l)
            return (e * (1.0 / l)).astype(BF16), sl

        def weighted_values(p, sl):
            outs.append(sum(jnp.dot(v_t[j][sl, :], p[_key_block(j), :], preferred_element_type=F32)
                            for j in range(3)))

        ready = probabilities(0)
        for hh in range(1, HEADS_PER_STEP):
            following = probabilities(hh)
            weighted_values(*ready)
            ready = following
        weighted_values(*ready)
        o_ref[...] = jnp.concatenate(outs, axis=0).T.astype(o_ref.dtype)

    return _carry(
        riders, body, name="attn_fwd", grid=(N_HEAD_GROUPS, nb),
        in_specs=[pl.BlockSpec((SUPER, HEAD_COLS), lambda hg, i: (i, hg))]
        + _band_specs(nb, N_HEAD_GROUPS, False) + _band_specs(nb, 0, False) + [_bias_spec()],
        out_specs=[pl.BlockSpec((SUPER, HEAD_COLS), lambda hg, i: (i, hg)),
                   pl.BlockSpec((None, LSE_ROWS, SUPER), lambda hg, i: (hg, 0, i))],
        out_shape=[jax.ShapeDtypeStruct((s_len, D_MODEL), BF16),
                   jax.ShapeDtypeStruct((N_HEAD_GROUPS, LSE_ROWS, s_len), F32)],
        semantics=("parallel", "arbitrary"),
    )(qkn, qkn, qkn, qkn, v, v, v, bias)


def _attn_bwd(qkn, v, out, d_out, bias, lse, riders=()):
    s_len = qkn.shape[0]
    nb = s_len // SUPER

    def body(q_ref, k0, k1, k2, v0, v1, v2, o_ref, do_ref, b_ref, lse_ref, dp_ref, db_ref, aq_ref, ak_ref, av_ref):
        i = pl.program_id(1)

        @pl.when(i == 0)
        def _():
            aq_ref[...] = jnp.zeros_like(aq_ref)
            ak_ref[...] = jnp.zeros_like(ak_ref)
            av_ref[...] = jnp.zeros_like(av_ref)
            db_ref[...] = jnp.zeros_like(db_ref)

        @pl.when(i < nb)
        def _():
            dq, dk, dv = [], [[], [], []], [[], [], []]
            ones = jnp.ones((8, HEAD_DIM), BF16)
            k_t = [k0[...].T, k1[...].T, k2[...].T]

            def softmax_grad(hh):
                sl = _head(hh)
                qs, do = q_ref[:, sl], do_ref[:, sl]
                kn = [k0[:, sl], k1[:, sl], k2[:, sl]]
                prod = do.astype(F32) * o_ref[:, sl].astype(F32)
                hi = prod.astype(BF16)
                lo = (prod - hi.astype(F32)).astype(BF16)
                delta = (lax.dot_general(ones, hi, NT, preferred_element_type=F32)
                         + lax.dot_general(ones, lo, NT, preferred_element_type=F32))[0:1]
                lse_row = lse_ref[hh:hh + 1, :]
                pb, dsb = [], []
                for j, (kj, vj) in enumerate(zip(kn, (v0, v1, v2))):
                    rows = _key_block(j)
                    p = jnp.exp(lax.dot_general(kj, qs, NT, preferred_element_type=F32) + b_ref[hh, rows, :] - lse_row)
                    ds = p * (lax.dot_general(vj[:, sl], do, NT, preferred_element_type=F32) - delta)
                    db_ref[hh, rows, :] += ds
                    pb.append(p.astype(BF16))
                    dsb.append(ds.astype(BF16))
                return pb, dsb, qs, do, sl

            def operand_grads(pb, dsb, qs, do, sl):
                dq.append(sum(jnp.dot(k_t[j][sl, :], dsb[j], preferred_element_type=F32) for j in range(3)))
                for j in range(3):
                    dv[j].append(jnp.dot(pb[j], do, preferred_element_type=F32))
                    dk[j].append(jnp.dot(dsb[j], qs, preferred_element_type=F32))

            ready = softmax_grad(0)
            for hh in range(1, HEADS_PER_STEP):
                following = softmax_grad(hh)
                operand_grads(*ready)
                ready = following
            operand_grads(*ready)
            aq_ref[i % 3] = jnp.concatenate(dq, axis=0).T
            for j in range(3):
                slot = (i + 1 + j) % 3
                if j < 2:
                    ak_ref[slot] += jnp.concatenate(dk[j], axis=1)
                    av_ref[slot] += jnp.concatenate(dv[j], axis=1)
                else:
                    ak_ref[slot] = jnp.concatenate(dk[j], axis=1)
                    av_ref[slot] = jnp.concatenate(dv[j], axis=1)

        slot = (i + 1) % 3
        dp_ref[0] = aq_ref[slot].astype(dp_ref.dtype)
        dp_ref[1] = ak_ref[slot].astype(dp_ref.dtype)
        dp_ref[2] = av_ref[slot].astype(dp_ref.dtype)

    def qrow(hg, i):
        return (jnp.minimum(i, nb - 1), hg)

    return _carry(
        riders, body, name="attn_bwd", grid=(N_HEAD_GROUPS, nb + 2),
        in_specs=[pl.BlockSpec((SUPER, HEAD_COLS), qrow)]
        + _band_specs(nb, N_HEAD_GROUPS, True) + _band_specs(nb, 0, True)
        + [pl.BlockSpec((SUPER, HEAD_COLS), qrow), pl.BlockSpec((SUPER, HEAD_COLS), qrow), _bias_spec(),
           pl.BlockSpec((None, LSE_ROWS, SUPER), lambda hg, i: (hg, 0, jnp.minimum(i, nb - 1)))],
        out_specs=[pl.BlockSpec((3, SUPER, HEAD_COLS), lambda hg, i: (0, jnp.maximum(i - 2, 0), hg)),
                   pl.BlockSpec((HEADS_PER_STEP, BAND, SUPER), lambda hg, i: (hg, 0, 0))],
        out_shape=[jax.ShapeDtypeStruct((6, s_len, D_MODEL), BF16),
                   jax.ShapeDtypeStruct((N_HEADS, BAND, SUPER), F32)],
        scratch_shapes=[pltpu.VMEM((3, SUPER, HEAD_COLS), F32)] * 3,
        semantics=("parallel", "arbitrary"),
    )(qkn, qkn, qkn, qkn, v, v, v, out, d_out, bias, lse)


def _qk_norm_bwd(dproj6, qk_raw, gq, gk):
    s_len = qk_raw.shape[0]
    ts = min(1024, s_len)

    nsteps = s_len // ts
    half = D_MODEL // 2

    def body(d_ref, raw_ref, gq_ref, gk_ref, o_ref, dgq_ref, dgk_ref, acc_ref):
        step = pl.program_id(0)

        @pl.when(step == 0)
        def _():
            acc_ref[...] = jnp.zeros_like(acc_ref)

        for piece, (g_ref, scale) in enumerate(((gq_ref, QK_SCALE), (gk_ref, 1.0))):
            for c0 in (0, half):
                xhat, r = _head_unit(raw_ref[:, piece * D_MODEL + c0:piece * D_MODEL + c0 + half].astype(F32))
                dn = d_ref[piece, :, c0:c0 + half].astype(F32) * scale
                u = dn * g_ref[...]
                dx = r * (u - xhat * (_head_sums(u * xhat) * (1.0 / HEAD_DIM)))
                o_ref[piece, :, c0:c0 + half] = dx.astype(o_ref.dtype)
                acc_ref[piece:piece + 1, c0:c0 + half] += jnp.sum(dn * xhat, axis=0, keepdims=True)

        @pl.when(step == nsteps - 1)
        def _():
            lane = lax.broadcasted_iota(jnp.int32, (D_MODEL, LANE), 0) % HEAD_DIM
            fold = (lane == lax.broadcasted_iota(jnp.int32, (D_MODEL, LANE), 1)).astype(BF16)
            tot = sum(jnp.dot(p, fold, preferred_element_type=F32) for p in _split3(acc_ref[...]))
            dgq_ref[...] = tot[0:1, :HEAD_DIM]
            dgk_ref[...] = tot[1:2, :HEAD_DIM]

    gain = pl.BlockSpec((1, half), lambda i: (0, 0))
    small = pl.BlockSpec((1, HEAD_DIM), lambda i: (0, 0))
    per_head = lambda g: jnp.tile(g, (1, half // HEAD_DIM))
    return pl.pallas_call(
        body, name="qk_norm_bwd", grid=(nsteps,),
        in_specs=[pl.BlockSpec((2, ts, D_MODEL), lambda i: (0, i, 0)),
                  pl.BlockSpec((ts, 2 * D_MODEL), lambda i: (i, 0)), gain, gain],
        out_specs=[pl.BlockSpec((2, ts, D_MODEL), lambda i: (0, i, 0)), small, small],
        out_shape=[jax.ShapeDtypeStruct(dproj6.shape, dproj6.dtype),
                   jax.ShapeDtypeStruct((1, HEAD_DIM), F32), jax.ShapeDtypeStruct((1, HEAD_DIM), F32)],
        scratch_shapes=[pltpu.VMEM((8, D_MODEL), F32)],
        input_output_aliases={0: 0},
        compiler_params=_params("arbitrary"),
    )(dproj6, qk_raw, per_head(gq), per_head(gk))


CONV_ROWS = 512
HALO = 16


def _rows_with_halo(ref, r0, n, front, s_len):
    zeros = jnp.zeros((HALO, ref.shape[1]), F32)
    if front:
        return (jnp.concatenate([zeros, ref[0:n, :].astype(F32)], axis=0) if r0 == 0
                else ref[r0 - HALO:r0 + n, :].astype(F32))
    return (jnp.concatenate([ref[r0:r0 + n, :].astype(F32), zeros], axis=0) if r0 + n == s_len
            else ref[r0:r0 + n + HALO, :].astype(F32))


def _earlier(ext, k):
    return pltpu.roll(ext, k, 0)[HALO:]


def _later(ext, k):
    n = ext.shape[0]
    return pltpu.roll(ext, n - k, 0)[:n - HALO]


def _conv_cols(col0):
    return lambda s_len: pl.BlockSpec((s_len, LANE), lambda j: (0, col0 + j))


def _conv_fwd(proj, conv_w, conv_b, riders=()):
    s_len = proj.shape[0]

    def body(bg_ref, cg_ref, xc_ref, w_ref, b_ref, o_ref):
        w = [w_ref[t:t + 1, :] for t in range(3)]
        for r0 in range(0, s_len, CONV_ROWS):
            u = _rows_with_halo(cg_ref, r0, CONV_ROWS, True, s_len) * \
                _rows_with_halo(xc_ref, r0, CONV_ROWS, True, s_len)
            conv = b_ref[...] + w[0] * _earlier(u, 2) + w[1] * _earlier(u, 1) + w[2] * u[HALO:]
            o_ref[r0:r0 + CONV_ROWS, :] = (bg_ref[r0:r0 + CONV_ROWS, :].astype(F32) * conv).astype(o_ref.dtype)

    return _carry(
        riders, body, name="conv_fwd", grid=(D_MODEL // LANE,),
        in_specs=[_conv_cols(0)(s_len), _conv_cols(8)(s_len), _conv_cols(16)(s_len),
                  pl.BlockSpec((3, LANE), lambda j: (0, j)), pl.BlockSpec((1, LANE), lambda j: (0, j))],
        out_specs=pl.BlockSpec((s_len, LANE), lambda j: (0, j)),
        out_shape=jax.ShapeDtypeStruct((s_len, D_MODEL), BF16),
        semantics=("parallel",),
    )(proj, proj, proj, conv_w, conv_b)


def _conv_bwd(dproj6, dy, proj, conv_w, conv_b, riders=()):
    s_len = proj.shape[0]

    def body(dy_ref, bg_ref, cg_ref, xc_ref, w_ref, b_ref, _, dp_ref, dw_ref):
        w = [w_ref[t:t + 1, :] for t in range(3)]
        acc = [jnp.zeros((1, LANE), F32) for _ in range(4)]
        for r0 in range(0, s_len, CONV_ROWS):
            rows = slice(r0, r0 + CONV_ROWS)
            u = _rows_with_halo(cg_ref, r0, CONV_ROWS, True, s_len) * \
                _rows_with_halo(xc_ref, r0, CONV_ROWS, True, s_len)
            u2, u1, u0 = _earlier(u, 2), _earlier(u, 1), u[HALO:]
            conv = b_ref[...] + w[0] * u2 + w[1] * u1 + w[2] * u0
            dp_ref[0, rows, :] = (dy_ref[rows, :].astype(F32) * conv).astype(dp_ref.dtype)
            dconv_ext = _rows_with_halo(dy_ref, r0, CONV_ROWS, False, s_len) * \
                _rows_with_halo(bg_ref, r0, CONV_ROWS, False, s_len)
            dconv = dconv_ext[:CONV_ROWS]
            for t, term in enumerate([dconv * u2, dconv * u1, dconv * u0, dconv]):
                acc[t] = acc[t] + jnp.sum(term, axis=0, keepdims=True)
            du = w[2] * dconv + w[1] * _later(dconv_ext, 1) + w[0] * _later(dconv_ext, 2)
            dp_ref[1, rows, :] = (du * xc_ref[rows, :].astype(F32)).astype(dp_ref.dtype)
            dp_ref[2, rows, :] = (du * cg_ref[rows, :].astype(F32)).astype(dp_ref.dtype)
        dw_ref[...] = jnp.zeros_like(dw_ref)
        for t in range(4):
            dw_ref[t:t + 1, :] = acc[t]

    return _carry(
        riders, body, name="conv_bwd", grid=(D_MODEL // LANE,),
        in_specs=[pl.BlockSpec((s_len, LANE), lambda j: (0, j)),
                  _conv_cols(0)(s_len), _conv_cols(8)(s_len), _conv_cols(16)(s_len),
                  pl.BlockSpec((3, LANE), lambda j: (0, j)), pl.BlockSpec((1, LANE), lambda j: (0, j)),
                  pl.BlockSpec(memory_space=pl.ANY)],
        out_specs=[pl.BlockSpec((3, s_len, LANE), lambda j: (1, 0, j)),
                   pl.BlockSpec((8, LANE), lambda j: (0, j))],
        out_shape=[jax.ShapeDtypeStruct(dproj6.shape, dproj6.dtype),
                   jax.ShapeDtypeStruct((8, D_MODEL), F32)],
        input_output_aliases={6: 0},
        semantics=("parallel",),
    )(dy, proj, proj, proj, conv_w, conv_b, dproj6)


class _NoComm:
    def __init__(self, early, late):
        self.early, self.late = early, late

    def riders(self, at):
        return ()

    def done(self, at, carried):
        pass

    def early_weights(self):
        return self.early[0], self.early[2]

    def gate_weight(self):
        return self.early[1]

    def late_weights(self):
        return self.late

    def grads_ready(self, group, grads):
        pass


def _d_input(name, terms, riders=()):
    s_len, kdim = terms[0][0].shape[1], terms[0][1].shape[1]
    tm, chunk = 512, 512
    n = len(terms)

    def body(*refs):
        o_ref = refs[-1]
        acc = jnp.zeros(o_ref.shape, F32)
        for p_ref, w_ref in zip(refs[:n], refs[n:2 * n]):
            n_piece, _, width = p_ref.shape
            per_shard = w_ref.shape[2]
            for c0 in range(0, n_piece * width, chunk):
                acc = acc + lax.dot_general(p_ref[c0 // width, :, c0 % width:c0 % width + chunk],
                                            w_ref[c0 // per_shard, :, c0 % per_shard:c0 % per_shard + chunk],
                                            NT, preferred_element_type=F32)
        o_ref[...] = acc.astype(o_ref.dtype)

    tile = pl.BlockSpec((tm, kdim), lambda i: (i, 0))
    return _carry(
        riders, body, name=name, grid=(s_len // tm,),
        in_specs=[pl.BlockSpec((p.shape[0], tm, p.shape[2]), lambda i: (0, i, 0)) for p, _ in terms]
        + [pl.BlockSpec(w.shape, lambda i: (0, 0, 0), pipeline_mode=pl.Buffered(1)) for _, w in terms],
        out_specs=tile, out_shape=jax.ShapeDtypeStruct((s_len, kdim), BF16),
        semantics=("parallel",),
    )(*[p for p, _ in terms], *[w for _, w in terms])


def _local_grads(x, target, norm1_g, q_norm_g, k_norm_g, rel_bias, conv_b, b_gate, norm2_g, comm):
    s_len = x.shape[0]
    tm = min(1024, s_len)
    tb = min(2048, s_len)
    row = lambda v: v.reshape(1, -1)

    def carrying(at, fn, *args, **kw):
        riders = comm.riders(at)
        out = fn(*args, riders=riders, **kw)
        if riders:
            out, carried = out
            comm.done(at, carried)
        return out

    bias = carrying("bias_expand", _bias_expand, rel_bias)
    h = carrying("norm1", _rms_fwd, "norm1", x, row(norm1_g))
    w_in3, conv_w = comm.early_weights()
    gq, gk = row(q_norm_g), row(k_norm_g)

    def qk_epi(acc, e, o):
        o[0][...] = acc.astype(BF16)
        gain = jnp.where(pl.program_id(1) < 2, e[0] * QK_SCALE, e[1])
        o[1][...] = (_head_unit(acc)[0] * gain).astype(BF16)
    small = pl.BlockSpec((1, 512), lambda i, j, k: (0, 0))
    per_head = lambda g: jnp.tile(g, (1, 512 // HEAD_DIM))
    qk_raw, qkn = carrying("proj_qk", _mm_fwd, "proj_qk", h, w_in3, tm, 512, D_MODEL, ncols=2 * D_MODEL,
                           epilogue=qk_epi, outs=[_out2d(s_len, 2 * D_MODEL, BF16, tm, 512)] * 2,
                           extras=[(per_head(gq), small), (per_head(gk), small)])
    v = carrying("proj_v", _mm_fwd, "proj_v", h, w_in3, tb, 512, D_MODEL, col0=4, ncols=D_MODEL,
                 outs=[_out2d(s_len, D_MODEL, BF16, tb, 512)])
    w_gate3 = comm.gate_weight()
    conv_in = _mm_fwd("proj_conv", h, w_in3, tb, 1536, D_MODEL, col0=2, ncols=3 * D_MODEL,
                      outs=[_out2d(s_len, 3 * D_MODEL, BF16, tb, 1536)])

    def gate_epi(acc, e, o):
        o[0][...] = jax.nn.sigmoid(acc + e[0]).astype(BF16)
    gates = _mm_fwd("gates", h, w_gate3, tb, 512, D_MODEL, epilogue=gate_epi,
                    outs=[_out2d(s_len, 2 * D_MODEL, BF16, tb, 512)],
                    extras=[(row(b_gate), pl.BlockSpec((1, 512), lambda i, j, k: (0, j)))])

    attn, lse = carrying("attn_fwd", _attn_fwd, qkn, v, bias)
    yconv = carrying("conv_fwd", _conv_fwd, conv_in, conv_w, row(conv_b))
    w_ap, w_cp, w_out, w_up3, w_down = comm.late_weights()
    tw = 1024
    ya = _mm_fwd("attn_proj", attn, w_ap, tm, tw, D_MODEL, outs=[_out2d(s_len, D_MODEL, BF16, tm, tw)])

    def merge_epi(acc, e, o):
        ya_v, ga, gc = [t.astype(F32) for t in e]
        o[0][...] = acc.astype(BF16)
        o[1][...] = (ga * ya_v + gc * acc).astype(BF16)
    gate_a, gate_c = (gates, _tile_spec(tm, tw, 0)), (gates, _tile_spec(tm, tw, 1))
    yc, merged = _mm_fwd("conv_proj", yconv, w_cp, tm, tw, D_MODEL, epilogue=merge_epi,
                         outs=[_out2d(s_len, D_MODEL, BF16, tm, tw), _out2d(s_len, D_MODEL, BF16, tm, tw)],
                         extras=[(ya, _tile_spec(tm, tw)), gate_a, gate_c])

    def res_epi(acc, e, o):
        o[0][...] = e[0] + acc
    x1 = _mm_fwd("out_proj", merged, w_out, tm, tw, D_MODEL, epilogue=res_epi,
                 extras=[(x, _tile_spec(tm, tw))])
    h2 = _rms_fwd("norm2", x1, row(norm2_g))

    def up_epi(acc, e, o):
        o[0][...] = jnp.square(jnp.maximum(acc, 0.0)).astype(BF16)
    act = _mm_fwd("mlp_up", h2, w_up3, tb, tw, D_MODEL, epilogue=up_epi, outs=[_out2d(s_len, D_FF, BF16, tb, tw)])

    def loss_epi(acc, e, o):
        err = e[0] + acc - e[1]
        o[0][...] = (err * (1.0 / D_MODEL)).astype(BF16)
        sq = err * err
        part = sq[:, 0:LANE]
        for c0 in range(LANE, D_MODEL, LANE):
            part = part + sq[:, c0:c0 + LANE]
        o[1][...] = jnp.sum(part.reshape(tl // 8, 8, LANE), axis=0)
    tl = 512
    dy_b, loss_part = _mm_fwd(
        "mlp_down", act, w_down, tl, D_MODEL, D_FF, epilogue=loss_epi,
        outs=[_out2d(s_len, D_MODEL, BF16, tl, D_MODEL),
              (jax.ShapeDtypeStruct((8 * (s_len // tl), LANE), F32), pl.BlockSpec((8, LANE), lambda i, j, k: (i, 0)))],
        extras=[(x1, _tile_spec(tl, D_MODEL)), (target, _tile_spec(tl, D_MODEL))])

    def dup_epi(acc, e, o):
        o[0][...] = (acc * (2.0 * jnp.sqrt(e[0].astype(F32)))).astype(BF16)
    full = lambda cols: pl.BlockSpec((tm, cols), lambda i, j, n: (i, n))
    tokens = lambda cols: pl.BlockSpec((s_len, cols), lambda i, j, m: (m, j))
    dup = _mm_bwd_x("d_act", dy_b, full(D_MODEL), w_down, tm, tw, D_MODEL, s_len, D_MODEL, epilogue=dup_epi,
                    outs=[_out2d(s_len, D_FF, BF16, tm, tw)], extras=[(act, _tile_spec(tm, tw))])
    g_down = _mm_bwd_w("g_down", act, dy_b, tokens(D_MODEL), D_MODEL, 512, D_MODEL, s_len, False)
    g_up = _mm_bwd_w("g_up", h2, dup, tokens(512), D_FF, D_MODEL, 512, s_len, True)
    comm.grads_ready("mlp", dict(w_down=g_down, w_up=g_up))
    dh2 = carrying("d_h2", _d_input, "d_h2", [(dup.reshape(1, s_len, D_FF), w_up3)])
    dx1_b, dg2 = _rms_bwd("norm2_bwd", dh2, x1, row(norm2_g), dy_b)

    def dmerge_epi(acc, e, o):
        ya_v, yc_v, ga, gc = [t.astype(F32) for t in e]
        o[0][...] = (acc * ga).astype(BF16)
        o[1][...] = (acc * gc).astype(BF16)
        o[2][0] = (acc * ya_v * ga * (1.0 - ga)).astype(BF16)
        o[2][1] = (acc * yc_v * gc * (1.0 - gc)).astype(BF16)
    dya, dyc, dgp2 = _mm_bwd_x(
        "d_merged", dx1_b, full(D_MODEL), w_out, tm, tw, D_MODEL, s_len, D_MODEL, epilogue=dmerge_epi,
        outs=[_out2d(s_len, D_MODEL, BF16, tm, tw), _out2d(s_len, D_MODEL, BF16, tm, tw),
              (jax.ShapeDtypeStruct((2, s_len, D_MODEL), BF16), pl.BlockSpec((2, tm, tw), lambda i, j, n: (0, i, j)))],
        extras=[(ya, _tile_spec(tm, tw)), (yc, _tile_spec(tm, tw)), gate_a, gate_c])
    g_out = _mm_bwd_w("g_out", merged, dx1_b, tokens(512), D_MODEL, D_MODEL, 512, s_len, False)
    d_attn = _mm_bwd_x("d_attn", dya, full(D_MODEL), w_ap, tm, tw, D_MODEL, s_len, D_MODEL,
                       outs=[_out2d(s_len, D_MODEL, BF16, tm, tw)])
    g_ap = _mm_bwd_w("g_attn_proj", attn, dya, tokens(512), D_MODEL, D_MODEL, 512, s_len, False)
    d_yconv = _mm_bwd_x("d_yconv", dyc, full(D_MODEL), w_cp, tm, tw, D_MODEL, s_len, D_MODEL,
                        outs=[_out2d(s_len, D_MODEL, BF16, tm, tw)])
    g_cp = _mm_bwd_w("g_conv_proj", yconv, dyc, tokens(512), D_MODEL, D_MODEL, 512, s_len, False)
    piece = lambda width: (lambda blk: (blk * width) // D_MODEL, lambda blk: (blk * width % D_MODEL) // width)
    pc, cb = piece(512)
    pieces = pl.BlockSpec((None, s_len, 512), lambda i, j, m: (pc(j), m, cb(j)))
    g_gate = _mm_bwd_w("g_gate", h, dgp2, pieces, 2 * D_MODEL, D_MODEL, 512, s_len, True)
    comm.grads_ready("proj", dict(w_out=g_out, w_attn_proj=g_ap, w_conv_proj=g_cp, w_gate=g_gate))

    dproj6, dbias = carrying("attn_bwd", _attn_bwd, qkn, v, attn, d_attn, bias, lse)
    dproj6, dgq, dgk = _qk_norm_bwd(dproj6, qk_raw, gq, gk)
    dproj6, dconv_wb = carrying("conv_bwd", _conv_bwd, dproj6, d_yconv, conv_in, conv_w, row(conv_b))

    g_in = carrying("g_in", _mm_bwd_w, "g_in", h, dproj6, pieces, 6 * D_MODEL, D_MODEL, 512, s_len, True)
    comm.grads_ready("in", dict(w_in=g_in))
    d_rel = carrying("bias_reduce", _bias_reduce, dbias)
    dh = carrying("d_h", _d_input, "d_h", [(dproj6, w_in3), (dgp2, w_gate3)])
    grad_x, dg1 = carrying("norm1_bwd", _rms_bwd, "norm1_bwd", dh, x, row(norm1_g), dx1_b, out_dtype=F32)

    def bsum(t, f):
        return [], [t[0].astype(F32), t[1].astype(F32)]
    ts = 512
    db_a, db_c = carrying("b_gate_sum", _ew, "b_gate_sum", bsum,
                          [(dgp2, pl.BlockSpec((None, ts, D_MODEL), lambda i: (0, i, 0))),
                           (dgp2, pl.BlockSpec((None, ts, D_MODEL), lambda i: (1, i, 0)))],
                          [], [], sums=[D_MODEL, D_MODEL], ts=ts)

    big = dict(w_in=g_in, w_attn_proj=g_ap, w_conv_proj=g_cp, w_gate=g_gate, w_out=g_out,
               w_up=g_up, w_down=g_down)
    small = dict(norm1_g=dg1, norm2_g=dg2, conv_wb=dconv_wb, b_gate=(db_a, db_c),
                 q_norm_g=dgq, k_norm_g=dgk, rel_bias=d_rel)
    return loss_part, grad_x, big, small


def _finish_loss(loss_part):
    def body(l_ref, lo_ref):
        total = jnp.sum(jnp.sum(l_ref[...], axis=0, keepdims=True), axis=1, keepdims=True)
        lo_ref[...] = jnp.broadcast_to(total * (0.5 / D_MODEL), lo_ref.shape)

    return pl.pallas_call(body, name="finish_loss", out_shape=jax.ShapeDtypeStruct((8, LANE), F32))(loss_part)


HBM_SPEC = pl.BlockSpec(memory_space=pl.ANY)


def _place():
    return lax.axis_index("x"), lax.axis_index("y"), lax.axis_index("c")


def _other_chips(x, y):
    return [(1 - x, y), (x, 1 - y), (1 - x, 1 - y)]


def _cast_into_slot(name, where, w):
    r, cols = w.shape
    ts = 256

    def body(w_ref, x_ref, o_ref):
        o_ref[...] = x_ref[...].astype(o_ref.dtype)

    return pl.pallas_call(
        body, name=name,
        grid_spec=pltpu.PrefetchScalarGridSpec(
            num_scalar_prefetch=1, grid=(r // ts,),
            in_specs=[pl.BlockSpec((ts, cols), lambda i, w: (i, 0))],
            out_specs=pl.BlockSpec((None, ts, cols), lambda i, w: (w[0], i, 0))),
        out_shape=jax.ShapeDtypeStruct((N_SHARD, r, cols), BF16),
        compiler_params=_params("parallel"),
    )(where, w)


def _remote(src, dst, send, recv, k, to):
    return pltpu.make_async_remote_copy(src_ref=src, dst_ref=dst, send_sem=send.at[k], recv_sem=recv.at[k],
                                        device_id=to, device_id_type=MESH)


def _gather_riders(slots):
    n = len(slots)

    def half(refs, w, shard, which):
        hr = slots[w].shape[1] // 2
        return refs[w].at[shard, pl.ds(which * hr, hr)]

    def each(fn):
        x, y, c = _place()
        for w in range(n):
            for j, chip in enumerate(_other_chips(x, y)):
                fn(w, 3 * w + j, 2 * x + y, 2 * chip[0] + chip[1], c, (*chip, c), (x, y, 1 - c))

    def chips_start(_, refs, send, recv):
        each(lambda w, k, me, them, c, peer, sib: _remote(half(refs, w, me, c), half(refs, w, me, c), send, recv, k, peer).start())

    def chips_finish(_, refs, send, recv):
        each(lambda w, k, me, them, c, peer, sib: _remote(half(refs, w, them, c), half(refs, w, them, c), send, recv, k, peer).wait_recv())
        each(lambda w, k, me, them, c, peer, sib: _remote(half(refs, w, me, c), half(refs, w, me, c), send, recv, k, peer).wait_send())

    def sibling_start(_, refs, send, recv):
        each(lambda w, k, me, them, c, peer, sib: _remote(half(refs, w, them, c), half(refs, w, them, c), send, recv, k, sib).start())

    def sibling_finish(_, refs, send, recv):
        each(lambda w, k, me, them, c, peer, sib: _remote(half(refs, w, them, 1 - c), half(refs, w, them, 1 - c), send, recv, k, sib).wait_recv())
        each(lambda w, k, me, them, c, peer, sib: _remote(half(refs, w, them, c), half(refs, w, them, c), send, recv, k, sib).wait_send())

    return (lambda s: _Rider([], s, 3 * n, chips_start, chips_finish),
            lambda s: _Rider([], s, 3 * n, sibling_start, sibling_finish))


def _pair_exchange_rider(grads, landing):
    n = len(grads)

    def copies(srcs, dsts, send, recv):
        x, y, c = _place()
        out = []
        for w in range(n):
            hr = grads[w].shape[1] // 2
            out.append(_remote(srcs[w].at[:, pl.ds((1 - c) * hr, hr)], dsts[w], send, recv, w, (x, y, 1 - c)))
        return out

    def start(srcs, dsts, send, recv):
        for cp in copies(srcs, dsts, send, recv):
            cp.start()

    def finish(srcs, dsts, send, recv):
        for cp in copies(srcs, dsts, send, recv):
            cp.wait()

    return _Rider(grads, landing, n, start, finish)


def _row_tile(hr):
    return min(hr, 256)


def _pair_add(name, where, grad, got):
    _, hr, cols = got.shape
    tr = _row_tile(hr)
    nblk = hr // tr

    def body(w_ref, g_ref, r_ref, o_ref):
        o_ref[...] = (g_ref[...] + r_ref[...]).astype(o_ref.dtype)

    other = lambda k, w: (w[0] + 1 + k) % N_SHARD
    return pl.pallas_call(
        body, name=name,
        grid_spec=pltpu.PrefetchScalarGridSpec(
            num_scalar_prefetch=1, grid=(N_SHARD - 1, nblk),
            in_specs=[pl.BlockSpec((None, tr, cols), lambda k, i, w: (other(k, w), w[1] * nblk + i, 0)),
                      pl.BlockSpec((None, tr, cols), lambda k, i, w: (other(k, w), i, 0))],
            out_specs=pl.BlockSpec((None, tr, cols), lambda k, i, w: (other(k, w), i, 0))),
        out_shape=jax.ShapeDtypeStruct(got.shape, BF16),
        compiler_params=_params("parallel", "parallel"),
    )(where, grad, got)


def _chip_exchange_rider(partials, landing):
    n = len(partials)

    def copies(srcs, dsts, send, recv):
        x, y, c = _place()
        return [_remote(srcs[w].at[2 * chip[0] + chip[1]], dsts[w].at[j], send, recv, 3 * w + j, (*chip, c))
                for w in range(n) for j, chip in enumerate(_other_chips(x, y))]

    def start(srcs, dsts, send, recv):
        for cp in copies(srcs, dsts, send, recv):
            cp.start()

    def finish(srcs, dsts, send, recv):
        for cp in copies(srcs, dsts, send, recv):
            cp.wait()

    return _Rider(partials, landing, 3 * n, start, finish)


def _final_add(name, where, grad, got, arrived):
    _, hr, cols = got.shape
    tr = _row_tile(hr)
    nblk = hr // tr

    def body(w_ref, g_ref, r_ref, a_ref, o_ref):
        acc = g_ref[...] + r_ref[...]
        for j in range(3):
            acc = acc + a_ref[j].astype(F32)
        o_ref[...] = acc

    return pl.pallas_call(
        body, name=name,
        grid_spec=pltpu.PrefetchScalarGridSpec(
            num_scalar_prefetch=1, grid=(nblk,),
            in_specs=[pl.BlockSpec((None, tr, cols), lambda i, w: (w[0], w[1] * nblk + i, 0)),
                      pl.BlockSpec((None, tr, cols), lambda i, w: (w[0], i, 0)),
                      pl.BlockSpec((3, tr, cols), lambda i, w: (0, i, 0))],
            out_specs=pl.BlockSpec((tr, cols), lambda i, w: (w[1] * nblk + i, 0))),
        out_shape=jax.ShapeDtypeStruct((2 * hr, cols), F32),
        compiler_params=_params("parallel"),
    )(where, grad, got, arrived)


def _pair_share_rider(shards):
    n = len(shards)

    def half(refs, w, which):
        hr = shards[w].shape[0] // 2
        return refs[w].at[pl.ds(which * hr, hr)]

    def start(_, refs, send, recv):
        x, y, c = _place()
        for w in range(n):
            _remote(half(refs, w, c), half(refs, w, c), send, recv, w, (x, y, 1 - c)).start()

    def finish(_, refs, send, recv):
        x, y, c = _place()
        for w in range(n):
            _remote(half(refs, w, 1 - c), half(refs, w, 1 - c), send, recv, w, (x, y, 1 - c)).wait_recv()
        for w in range(n):
            _remote(half(refs, w, c), half(refs, w, c), send, recv, w, (x, y, 1 - c)).wait_send()

    return _Rider([], shards, n, start, finish)


class _Exchange:
    PLAN = {"bias_expand": [("early", "gather")], "norm1": [("early", "forward")],
            "proj_qk": [("gate", "gather")], "proj_v": [("gate", "forward")],
            "attn_fwd": [("late", "gather")], "conv_fwd": [("late", "forward")],
            "d_h2": [("mlp", "pair")], "attn_bwd": [("mlp", "chips"), ("proj", "pair")], "conv_bwd": [("mlp", "share")],
            "g_in": [("proj", "chips")], "bias_reduce": [("proj", "share"), ("in", "pair")],
            "d_h": [("in", "chips")], "b_gate_sum": [("in", "share")]}

    def __init__(self, where, early_slots, gate_slots, late_slots):
        self.where = where
        self.slots = dict(early=early_slots, gate=gate_slots, late=late_slots)
        self.stage = {g: dict(zip(("gather", "forward"), _gather_riders(s))) for g, s in self.slots.items()}
        self.groups, self.reduced, self.pending = {}, {}, []

    def early_weights(self):
        w_in3, small = self.slots["early"]
        conv_w = small[:, :3, :].transpose(1, 0, 2).reshape(3, N_SHARD * small.shape[2])
        return w_in3, conv_w

    def gate_weight(self):
        return self.slots["gate"][0]

    def late_weights(self):
        rows = lambda a: a.reshape(a.shape[0] * a.shape[1], a.shape[2])
        w_ap, w_cp, w_out, w_up3, w_down = self.slots["late"]
        return rows(w_ap), rows(w_cp), rows(w_out), w_up3, rows(w_down)

    def grads_ready(self, group, grads):
        names = list(grads)
        g4 = [g if g.ndim == 3 else g.reshape(N_SHARD, -1, g.shape[1]) for g in grads.values()]
        self.groups[group] = dict(names=names, g4=g4)

    def riders(self, at):
        self.pending = self.PLAN.get(at, [])
        out = []
        for group, stage in self.pending:
            if group in self.slots:
                out.append(self.stage[group][stage](self.slots[group]))
                continue
            st = self.groups[group]
            if stage == "pair":
                landing = [lax.empty((N_SHARD, g.shape[1] // 2, g.shape[2]), F32) for g in st["g4"]]
                out.append(_pair_exchange_rider(st["g4"], landing))
            elif stage == "chips":
                landing = [lax.empty((3,) + p.shape[1:], p.dtype) for p in st["partial"]]
                out.append(_chip_exchange_rider(st["partial"], landing))
            else:
                out.append(_pair_share_rider(st["halves"]))
        return out

    def done(self, at, carried):
        for (group, stage), arrays in zip(self.pending, carried):
            if group in self.slots:
                self.slots[group] = arrays
                continue
            st = self.groups[group]
            tag = lambda what, n: what + "_" + n
            if stage == "pair":
                st["got"] = arrays
                st["partial"] = [_pair_add(tag("pair_add", n), self.where, g, r)
                                 for n, g, r in zip(st["names"], st["g4"], arrays)]
            elif stage == "chips":
                st["halves"] = [_final_add(tag("final_add", n), self.where, g, r, a)
                                for n, g, r, a in zip(st["names"], st["g4"], st["got"], arrays)]
            else:
                self.reduced.update(zip(st["names"], arrays))


SMALL_ROWS = 32
N_DEV = 8


def _all_reduce_small(pack):
    def body(p_ref, o_ref, buf, send, recv):
        x, y, c = _place()
        buf[4 * x + 2 * y + c] = p_ref[...]
        copies, waits = [], []
        for k in range(1, N_DEV):
            px = 1 - x if k & 4 else x
            py = 1 - y if k & 2 else y
            pc = 1 - c if k & 1 else c
            copies.append(_remote(p_ref, buf.at[4 * x + 2 * y + c], send, recv, k - 1, (px, py, pc)))
            waits.append(_remote(p_ref, buf.at[4 * px + 2 * py + pc], send, recv, k - 1, (px, py, pc)))
        for cp in copies:
            cp.start()
        for cp in waits:
            cp.wait_recv()
        acc = buf[0]
        for d in range(1, N_DEV):
            acc = acc + buf[d]
        o_ref[...] = acc
        for cp in copies:
            cp.wait_send()

    return pl.pallas_call(
        body, name="all_reduce_small",
        out_shape=jax.ShapeDtypeStruct(pack.shape, F32),
        scratch_shapes=[pltpu.VMEM((N_DEV,) + pack.shape, F32),
                        pltpu.SemaphoreType.DMA((N_DEV - 1,)), pltpu.SemaphoreType.DMA((N_DEV - 1,))],
    )(pack)


def _adamw(name, w, g, m, v, riders=()):
    c1 = 1.0 - ADAM_B1 ** ADAM_STEP
    c2 = 1.0 - ADAM_B2 ** ADAM_STEP

    def fn(t, f):
        wv, gv, mv, vv = t
        m2 = ADAM_B1 * mv + (1.0 - ADAM_B1) * gv
        v2 = ADAM_B2 * vv + (1.0 - ADAM_B2) * (gv * gv)
        delta = -ADAM_LR * ((m2 / c1) / (jnp.sqrt(v2 / c2) + ADAM_EPS) + ADAM_WD * wv)
        return [delta, m2, v2], []

    cols = w.shape[1]
    return _ew(name, fn, [w, g, m, v], [], [(cols, F32)] * 3, ts=min(w.shape[0], 256), riders=riders)


LOSS_ROW = 26


def _pack_small(norm1_g, norm2_g, conv_b, b_gate, conv_w, q_norm_g, k_norm_g, rel_bias, loss=None):
    pack = jnp.zeros((SMALL_ROWS, D_MODEL), F32)
    for r0, v in ((0, norm1_g), (1, norm2_g), (2, conv_b), (3, b_gate.reshape(2, D_MODEL)), (5, conv_w),
                  (8, q_norm_g), (9, k_norm_g), (10, rel_bias)) + (((LOSS_ROW, loss),) if loss is not None else ()):
        v = v.reshape(-1, v.shape[-1]).astype(F32)
        pack = pack.at[r0:r0 + v.shape[0], :v.shape[1]].set(v)
    return pack


def _unpack_small(pack, conv_cols):
    return dict(norm1_g=pack[0], norm2_g=pack[1], conv_b=pack[2], b_gate=pack[3:5].reshape(2 * D_MODEL),
                conv_w=pack[5:8, :conv_cols], q_norm_g=pack[8, :HEAD_DIM], k_norm_g=pack[9, :HEAD_DIM],
                rel_bias=pack[10:10 + N_HEADS, :N_REL])


BIG = ["w_in", "w_attn_proj", "w_conv_proj", "w_gate", "w_out", "w_up", "w_down"]
LATE = ["w_attn_proj", "w_conv_proj", "w_out", "w_up", "w_down"]
WEIGHTS = ["norm1_g", "w_in", "q_norm_g", "k_norm_g", "rel_bias", "conv_w", "conv_b", "w_attn_proj",
           "w_conv_proj", "w_gate", "b_gate", "w_out", "norm2_g", "w_up", "w_down"]


def kernel(x, norm1_g, w_in, q_norm_g, k_norm_g, rel_bias, conv_w, conv_b, w_attn_proj, w_conv_proj, w_gate, b_gate, w_out, norm2_g, w_up, w_down, loss_target, m_norm1_g, m_w_in, m_q_norm_g, m_k_norm_g, m_rel_bias, m_conv_w, m_conv_b, m_w_attn_proj, m_w_conv_proj, m_w_gate, m_b_gate, m_w_out, m_norm2_g, m_w_up, m_w_down, v_norm1_g, v_w_in, v_q_norm_g, v_k_norm_g, v_rel_bias, v_conv_w, v_conv_b, v_w_attn_proj, v_w_conv_proj, v_w_gate, v_b_gate, v_w_out, v_norm2_g, v_w_up, v_w_down):
    given = dict(locals())
    w = {n: given[n] for n in WEIGHTS}
    m = {n: given["m_" + n] for n in WEIGHTS}
    v = {n: given["v_" + n] for n in WEIGHTS}
    s_len = x.shape[1]
    shard = 2 * lax.axis_index("x") + lax.axis_index("y")
    where = jnp.stack([shard, lax.axis_index("c")]).astype(jnp.int32)
    conv_cols = conv_w.shape[1]

    small_in = lax.dynamic_update_slice(jnp.zeros((N_SHARD, 16, conv_cols), F32), conv_w[None], (shard, 0, 0))
    slot = {n: _cast_into_slot("cast_" + n, where, w[n]) for n in BIG}
    comm = _Exchange(where, [slot["w_in"], small_in], [slot["w_gate"]], [slot[n] for n in LATE])

    loss_part, grad_x, _, small = _local_grads(
        x.reshape(s_len, D_MODEL), loss_target.reshape(s_len, D_MODEL), norm1_g, q_norm_g, k_norm_g,
        rel_bias, conv_b, b_gate, norm2_g, comm)
    grad = dict(comm.reduced)

    loss_local = _finish_loss(loss_part)
    pack = _pack_small(small["norm1_g"], small["norm2_g"], small["conv_wb"][3], jnp.concatenate(small["b_gate"], axis=1),
                       small["conv_wb"][0:3], small["q_norm_g"], small["k_norm_g"], small["rel_bias"],
                       loss=loss_local[0:1, :])
    total = _all_reduce_small(pack)
    g_small = _unpack_small(total, D_MODEL)
    g_small["conv_w"] = lax.dynamic_slice(g_small["conv_w"], (0, shard * conv_cols), (3, conv_cols))
    grad.update(g_small)

    delta, new_m, new_v = {}, {}, {}
    for n in BIG:
        delta[n], new_m[n], new_v[n] = _adamw("adamw_" + n, w[n], grad[n], m[n], v[n])
    small_names = [n for n in WEIGHTS if n not in BIG]
    packs = [_pack_small(**{n: src[n] for n in small_names}) for src in (w, grad, m, v)]
    for out, packed in zip((delta, new_m, new_v), _adamw("adamw_small", *packs)):
        out.update({n: a.reshape(w[n].shape) for n, a in _unpack_small(packed, conv_cols).items()})

    outs = [total[LOSS_ROW, 0], grad_x.reshape(x.shape)]
    for group in (grad, delta, new_m, new_v):
        outs += [group[n].reshape(w[n].shape) for n in WEIGHTS]
    return tuple(outs)
```

```python
import functools

import jax
import jax.numpy as jnp
from jax import lax
from jax.experimental import pallas as pl
from jax.experimental.pallas import tpu as pltpu

F32 = jnp.float32
BF16 = jnp.bfloat16

D_MODEL = 1024
N_HEADS = 16
HEAD_DIM = 64
CHUNK = 64
N_PREV_CHUNKS = 8
MAX_REL = 256
D_FF = 4096
N_REL = 2 * MAX_REL + 1
REL_PAD = 640
EPS = 1e-6
NEG_INF = -1e30
QK_SCALE = HEAD_DIM ** -0.5

SUPER = 4 * CHUNK
BAND = SUPER + N_PREV_CHUNKS * CHUNK
SKEW_W = 1024
N_SHARD = 4
LANE = 128
MXU_DIM = 256
VMEM_LIMIT = 48 * 1024 * 1024

ADAM_LR = 0.001
ADAM_B1 = 0.9
ADAM_B2 = 0.999
ADAM_EPS = 1e-08
ADAM_WD = 0.01
ADAM_STEP = 10

MESH = pl.DeviceIdType.MESH
NN = (((1,), (0,)), ((), ()))
NT = (((1,), (1,)), ((), ()))
TN = (((0,), (0,)), ((), ()))


def _params(*sem):
    return pltpu.CompilerParams(dimension_semantics=sem or None, vmem_limit_bytes=VMEM_LIMIT)


HBM_SPEC = pl.BlockSpec(memory_space=pl.ANY)


class _Rider:
    def __init__(self, sources, arrays, n_sem, start, finish):
        self.sources, self.arrays, self.n_sem, self.start, self.finish = sources, arrays, n_sem, start, finish


def _carry(riders, body, *, name, out_shape, grid=(), in_specs=None, out_specs=None, scratch_shapes=(),
           semantics=(), input_output_aliases=None):
    aliases = dict(input_output_aliases or {})
    if not riders:
        kw = {} if in_specs is None else dict(in_specs=in_specs, out_specs=out_specs)
        return pl.pallas_call(body, name=name, grid=grid, out_shape=out_shape, scratch_shapes=scratch_shapes,
                              input_output_aliases=aliases, compiler_params=_params(*semantics), **kw)
    single = not isinstance(out_shape, (list, tuple))
    shapes = [out_shape] if single else list(out_shape)
    n_out, n_scr = len(shapes), len(scratch_shapes)
    srcs = [a for r in riders for a in r.sources]
    arrs = [a for r in riders for a in r.arrays]
    vmem = pl.BlockSpec(memory_space=pltpu.VMEM)

    def run(*args):
        n_in = len(args)

        def wrapped(*refs):
            pos = n_in
            src_refs = refs[pos:pos + len(srcs)]
            pos += len(srcs) + len(arrs)
            outs = refs[pos:pos + n_out]
            pos += n_out
            arr_refs = refs[pos:pos + len(arrs)]
            pos += len(arrs)
            scratch = refs[pos:pos + n_scr]
            sems = refs[pos + n_scr:]
            first, last = True, True
            for d, size in enumerate(grid):
                first = jnp.logical_and(first, pl.program_id(d) == 0)
                last = jnp.logical_and(last, pl.program_id(d) == size - 1)

            def each(method):
                s0 = a0 = 0
                for k, r in enumerate(riders):
                    getattr(r, method)(src_refs[s0:s0 + len(r.sources)], arr_refs[a0:a0 + len(r.arrays)],
                                       sems[2 * k], sems[2 * k + 1])
                    s0, a0 = s0 + len(r.sources), a0 + len(r.arrays)

            pl.when(first)(lambda: each("start"))
            body(*refs[:n_in], *outs, *scratch)
            pl.when(last)(lambda: each("finish"))

        ins = [vmem] * n_in if in_specs is None else list(in_specs)
        if out_specs is None:
            o_specs = [vmem] * n_out
        else:
            o_specs = [out_specs] if single else list(out_specs)
        for k in range(len(arrs)):
            aliases[n_in + len(srcs) + k] = n_out + k
        res = pl.pallas_call(
            wrapped, name=name, grid=grid,
            in_specs=ins + [HBM_SPEC] * (len(srcs) + len(arrs)),
            out_specs=o_specs + [HBM_SPEC] * len(arrs),
            out_shape=shapes + [jax.ShapeDtypeStruct(a.shape, a.dtype) for a in arrs],
            scratch_shapes=list(scratch_shapes) + [pltpu.SemaphoreType.DMA((r.n_sem,)) for r in riders for _ in range(2)],
            input_output_aliases=aliases,
            compiler_params=_params(*["arbitrary"] * len(grid)),
        )(*args, *srcs, *arrs)
        core, rest = res[:n_out], list(res[n_out:])
        carried, a0 = [], 0
        for r in riders:
            carried.append(rest[a0:a0 + len(r.arrays)])
            a0 += len(r.arrays)
        return (core[0] if single else core), carried

    return run


def _mm(name, dims, a, a_spec, b, b_spec, grid, tile, outs, epilogue=None, extras=(), riders=()):
    nk, ne, no = grid[2], len(extras), len(outs)

    def body(a_ref, b_ref, *refs):
        e_refs, o_refs = refs[:ne], refs[ne:ne + no]
        part = lax.dot_general(a_ref[...], b_ref[...], dims, preferred_element_type=F32)

        def finish(acc):
            if epilogue is None:
                o_refs[0][...] = acc.astype(o_refs[0].dtype)
            else:
                epilogue(acc, [r[...] for r in e_refs], o_refs)

        if nk == 1:
            finish(part)
        else:
            acc_ref = refs[ne + no]
            k = pl.program_id(2)

            @pl.when(k == 0)
            def _():
                acc_ref[...] = part

            @pl.when(k > 0)
            def _():
                acc_ref[...] += part

            @pl.when(k == nk - 1)
            def _():
                finish(acc_ref[...])

    res = _carry(
        riders, body, name=name, grid=grid,
        in_specs=[a_spec, b_spec] + [s for _, s in extras],
        out_specs=[s for _, s in outs],
        out_shape=[s for s, _ in outs],
        scratch_shapes=[pltpu.VMEM(tile, F32)] if nk > 1 else [],
        semantics=("parallel", "parallel", "arbitrary"),
    )(a, b, *[e for e, _ in extras])
    res, carried = res if riders else (res, None)
    res = res[0] if no == 1 else res
    return (res, carried) if riders else res


def _tile_spec(tm, tn, col0=0):
    return pl.BlockSpec((tm, tn), lambda i, j, k: (i, j + col0))


def _out2d(m, n, dtype, tm, tn):
    return (jax.ShapeDtypeStruct((m, n), dtype), _tile_spec(tm, tn))


def _mm_fwd(name, a, w, tm, tn, tk, outs=None, epilogue=None, extras=(), col0=0, ncols=None, riders=()):
    m, kdim = a.shape
    if w.ndim == 3:
        per = w.shape[2] // tn
        n = ncols or N_SHARD * w.shape[2]
        w_spec = pl.BlockSpec((None, tk, tn), lambda i, j, k: ((j + col0) // per, k, (j + col0) % per))
    else:
        n = ncols or w.shape[1]
        w_spec = pl.BlockSpec((tk, tn), lambda i, j, k: (k, j + col0))
    if outs is None:
        outs = [_out2d(m, n, F32, tm, tn)]
    return _mm(name, NN, a, pl.BlockSpec((tm, tk), lambda i, j, k: (i, k)), w, w_spec,
               (m // tm, n // tn, kdim // tk), (tm, tn), outs, epilogue, extras, riders)


def _mm_bwd_x(name, g, g_spec, w, tm, tj, tc, m, n_contract, outs=None, epilogue=None, extras=(), riders=()):
    if w.ndim == 3:
        per = w.shape[2] // tc
        kdim = w.shape[1]
        w_spec = pl.BlockSpec((None, tj, tc), lambda i, j, n: (n // per, j, n % per))
    else:
        kdim = w.shape[0]
        w_spec = pl.BlockSpec((tj, tc), lambda i, j, n: (j, n))
    if outs is None:
        outs = [_out2d(m, kdim, F32, tm, tj)]
    return _mm(name, NT, g, g_spec, w, w_spec, (m // tm, kdim // tj, n_contract // tc),
               (tm, tj), outs, epilogue, extras, riders)


def _mm_bwd_w(name, a, g, g_spec, n, tk, tn, tm, sharded, riders=()):
    m, kdim = a.shape
    if sharded:
        per = (n // N_SHARD) // tn
        out = (jax.ShapeDtypeStruct((N_SHARD, kdim, n // N_SHARD), F32),
               pl.BlockSpec((None, tk, tn), lambda i, j, mm: (j // per, i, j % per)))
    else:
        out = (jax.ShapeDtypeStruct((kdim, n), F32), pl.BlockSpec((tk, tn), lambda i, j, mm: (i, j)))
    return _mm(name, TN, a, pl.BlockSpec((tm, tk), lambda i, j, mm: (mm, i)), g, g_spec,
               (kdim // tk, n // tn, m // tm), (tk, tn), [out], riders=riders)


def _ew(name, fn, tiles, fulls, outs, sums=(), ts=512, riders=()):
    tiles = [t if isinstance(t, tuple) else (t, pl.BlockSpec((ts, t.shape[1]), lambda i: (i, 0)))
             for t in tiles]
    s_rows = tiles[0][0].shape[-2]
    nt, nf, no = len(tiles), len(fulls), len(outs)

    def body(*refs):
        t_vals = [r[...] for r in refs[:nt]]
        f_vals = [r[...] for r in refs[nt:nt + nf]]
        o_refs, s_refs = refs[nt + nf:nt + nf + no], refs[nt + nf + no:]
        o_vals, s_vals = fn(t_vals, f_vals)
        for r, v in zip(o_refs, o_vals):
            r[...] = v.astype(r.dtype)
        for r, v in zip(s_refs, s_vals):
            part = jnp.sum(v, axis=0, keepdims=True)

            @pl.when(pl.program_id(0) == 0)
            def _():
                r[...] = part

            @pl.when(pl.program_id(0) > 0)
            def _():
                r[...] += part

    full_specs = [pl.BlockSpec(f.shape, lambda i, nd=f.ndim: (0,) * nd) for f in fulls]
    return _carry(
        riders, body, name=name, grid=(s_rows // ts,),
        in_specs=[s for _, s in tiles] + full_specs,
        out_specs=[pl.BlockSpec((ts, c), lambda i: (i, 0)) for c, _ in outs]
        + [pl.BlockSpec((1, c), lambda i: (0, 0)) for c in sums],
        out_shape=[jax.ShapeDtypeStruct((s_rows, c), dt) for c, dt in outs]
        + [jax.ShapeDtypeStruct((1, c), F32) for c in sums],
        semantics=("arbitrary",),
    )(*[t for t, _ in tiles], *fulls)


def _rms_fwd(name, x, g, riders=()):
    def fn(t, f):
        xv = t[0]
        r = lax.rsqrt(jnp.mean(xv * xv, axis=-1, keepdims=True) + EPS)
        return [xv * r * f[0]], []
    out = _ew(name, fn, [x], [g], [(x.shape[1], BF16)], riders=riders)
    return (out[0][0], out[1]) if riders else out[0]


def _rms_bwd(name, dh, x, g, dres, out_dtype=BF16, riders=()):
    def fn(t, f):
        dhv, xv, dr = [v.astype(F32) for v in t]
        r = lax.rsqrt(jnp.mean(xv * xv, axis=-1, keepdims=True) + EPS)
        xhat = xv * r
        u = dhv * f[0]
        dx = r * (u - xhat * jnp.mean(u * xhat, axis=-1, keepdims=True)) + dr
        return [dx], [dhv * xhat]
    c = x.shape[1]
    return _ew(name, fn, [dh, x, dres], [g], [(c, out_dtype)], sums=[c], riders=riders)


def _split3(x):
    x1 = x.astype(BF16)
    r1 = x - x1.astype(F32)
    x2 = r1.astype(BF16)
    x3 = (r1 - x2.astype(F32)).astype(BF16)
    return x1, x2, x3


def _rel_class(cp):
    far = (cp < MAX_REL) | (cp > BAND)
    return jnp.where(far, 2 * MAX_REL, BAND - cp)


def _skew_rows(x, sign):
    row = lax.broadcasted_iota(jnp.int32, x.shape, 0)
    for b in range(CHUNK.bit_length() - 1):
        shift = (1 << b) if sign > 0 else SKEW_W - (1 << b)
        x = jnp.where((row >> b) & 1 == 1, pltpu.roll(x, shift, 1), x)
    return x


def _roll_lanes(x, shift):
    return x if shift % SKEW_W == 0 else pltpu.roll(x, shift % SKEW_W, 1)


N_START = 3


def _bias_expand(rel_bias, riders=()):
    rel = jnp.pad(rel_bias, ((0, 0), (0, REL_PAD - N_REL))).reshape(N_HEADS, 1, REL_PAD)

    def body(rel_ref, o_ref):
        cls = lax.broadcasted_iota(jnp.int32, (REL_PAD, SKEW_W), 0)
        cp = lax.broadcasted_iota(jnp.int32, (REL_PAD, SKEW_W), 1)
        onehot = (cls == _rel_class(cp)).astype(BF16)
        rel8 = jnp.broadcast_to(rel_ref[...], (8, REL_PAD))
        trow = sum(jnp.dot(p, onehot, preferred_element_type=F32) for p in _split3(rel8))[0:1]
        first = _skew_rows(jnp.broadcast_to(trow, (CHUNK, SKEW_W)), +1)
        full = jnp.concatenate([_roll_lanes(first, CHUNK * g) for g in range(SUPER // CHUNK)], axis=0)[:, :BAND]
        qc = lax.broadcasted_iota(jnp.int32, (SUPER, BAND), 0) // CHUNK
        kc = lax.broadcasted_iota(jnp.int32, (SUPER, BAND), 1) // CHUNK
        on_band = (kc >= qc) & (kc <= qc + N_PREV_CHUNKS)
        table = jnp.where(on_band, full, NEG_INF).T
        key = lax.broadcasted_iota(jnp.int32, (BAND, SUPER), 0)
        for t in range(N_START):
            o_ref[t] = jnp.where(key < (N_START - 1 - t) * SUPER, NEG_INF, table)

    return _carry(
        riders, body, name="bias_expand", grid=(N_HEADS,),
        in_specs=[pl.BlockSpec((None, 1, REL_PAD), lambda h: (h, 0, 0))],
        out_specs=pl.BlockSpec((N_START, None, BAND, SUPER), lambda h: (0, h, 0, 0)),
        out_shape=jax.ShapeDtypeStruct((N_START, N_HEADS, BAND, SUPER), F32),
        semantics=("arbitrary",),
    )(rel)


def _bias_reduce(dbias, riders=()):
    def body(d_ref, o_ref):
        x = jnp.concatenate([d_ref[...].T, jnp.zeros((SUPER, SKEW_W - BAND), F32)], axis=1)
        folded = sum(_roll_lanes(x[CHUNK * g:CHUNK * (g + 1)], -CHUNK * g) for g in range(SUPER // CHUNK))
        diag = jnp.sum(_skew_rows(folded, -1), axis=0, keepdims=True)
        cp = lax.broadcasted_iota(jnp.int32, (SKEW_W, REL_PAD), 0)
        cls = lax.broadcasted_iota(jnp.int32, (SKEW_W, REL_PAD), 1)
        onehot = (cls == _rel_class(cp)).astype(BF16)
        diag8 = jnp.broadcast_to(diag, (8, SKEW_W))
        o_ref[...] = sum(jnp.dot(p, onehot, preferred_element_type=F32) for p in _split3(diag8))[0:1]

    out = _carry(
        riders, body, name="bias_reduce", grid=(N_HEADS,),
        in_specs=[pl.BlockSpec((None, BAND, SUPER), lambda h: (h, 0, 0))],
        out_specs=pl.BlockSpec((None, 1, REL_PAD), lambda h: (h, 0, 0)),
        out_shape=jax.ShapeDtypeStruct((N_HEADS, 1, REL_PAD), F32),
        semantics=("arbitrary",),
    )(dbias)
    out, carried = out if riders else (out, None)
    out = out.reshape(N_HEADS, REL_PAD)[:, :N_REL]
    return (out, carried) if riders else out


HEADS_PER_STEP = 8
HEAD_COLS = HEADS_PER_STEP * HEAD_DIM
N_HEAD_GROUPS = N_HEADS // HEADS_PER_STEP


def _unit(x):
    r = lax.rsqrt(jnp.mean(x * x, axis=-1, keepdims=True) + EPS)
    return x * r, r


def _scores_t(qs, kn, bias_t):
    return jnp.concatenate([lax.dot_general(k, qs, NT, preferred_element_type=F32) for k in kn], axis=0) + bias_t


def _bias_spec():
    return pl.BlockSpec((None, HEADS_PER_STEP, BAND, SUPER), lambda hg, i: (jnp.minimum(i, N_START - 1), hg, 0, 0))


def _band_specs(nb, col0, clamp_hi):
    def spec(d):
        def index(hg, i):
            blk = jnp.maximum(i - d, 0)
            if clamp_hi:
                blk = jnp.minimum(blk, nb - 1)
            return (blk, col0 + hg)
        return pl.BlockSpec((SUPER, HEAD_COLS), index)
    return [spec(2), spec(1), spec(0)]


def _head_sums(y):
    same_head = (lax.broadcasted_iota(jnp.int32, (MXU_DIM, MXU_DIM), 0) // HEAD_DIM
                 == lax.broadcasted_iota(jnp.int32, (MXU_DIM, MXU_DIM), 1) // HEAD_DIM).astype(BF16)
    sums = []
    for c0 in range(0, y.shape[1], MXU_DIM):
        chunk = y[:, c0:c0 + MXU_DIM]
        hi = chunk.astype(BF16)
        lo = (chunk - hi.astype(F32)).astype(BF16)
        sums.append(jnp.dot(hi, same_head, preferred_element_type=F32)
                    + jnp.dot(lo, same_head, preferred_element_type=F32))
    return jnp.concatenate(sums, axis=1)


def _head_unit(x):
    r = lax.rsqrt(_head_sums(x * x) * (1.0 / HEAD_DIM) + EPS)
    return x * r, r


def _head(hh):
    return slice(HEAD_DIM * hh, HEAD_DIM * (hh + 1))


def _key_block(j):
    return slice(SUPER * j, SUPER * (j + 1))


LSE_ROWS = 8


def _attn_fwd(qkn, v, bias, riders=()):
    s_len = qkn.shape[0]
    nb = s_len // SUPER

    def body(q_ref, k0, k1, k2, v0, v1, v2, b_ref, o_ref, lse_ref):
        outs = []
        v_t = [v0[...].T, v1[...].T, v2[...].T]

        def probabilities(hh):
            sl = _head(hh)
            s = _scores_t(q_ref[:, sl], [k0[:, sl], k1[:, sl], k2[:, sl]], b_ref[hh])
            m = jnp.max(s, axis=0, keepdims=True)
            e = jnp.exp(s - m)
            l = jnp.sum(e, axis=0, keepdims=True)
            lse_ref[hh:hh + 1, :] = m + jnp.log(l)
            return (e * (1.0 / l)).astype(BF16), sl

        def weighted_values(p, sl):
            outs.append(sum(jnp.dot(v_t[j][sl, :], p[_key_block(j), :], preferred_element_type=F32)
                            for j in range(3)))

        ready = probabilities(0)
        for hh in range(1, HEADS_PER_STEP):
            following = probabilities(hh)
            weighted_values(*ready)
            ready = following
        weighted_values(*ready)
        o_ref[...] = jnp.concatenate(outs, axis=0).T.astype(o_ref.dtype)

    return _carry(
        riders, body, name="attn_fwd", grid=(N_HEAD_GROUPS, nb),
        in_specs=[pl.BlockSpec((SUPER, HEAD_COLS), lambda hg, i: (i, hg))]
        + _band_specs(nb, N_HEAD_GROUPS, False) + _band_specs(nb, 0, False) + [_bias_spec()],
        out_specs=[pl.BlockSpec((SUPER, HEAD_COLS), lambda hg, i: (i, hg)),
                   pl.BlockSpec((None, LSE_ROWS, SUPER), lambda hg, i: (hg, 0, i))],
        out_shape=[jax.ShapeDtypeStruct((s_len, D_MODEL), BF16),
                   jax.ShapeDtypeStruct((N_HEAD_GROUPS, LSE_ROWS, s_len), F32)],
        semantics=("parallel", "arbitrary"),
    )(qkn, qkn, qkn, qkn, v, v, v, bias)


def _attn_bwd(qkn, v, out, d_out, bias, lse, riders=()):
    s_len = qkn.shape[0]
    nb = s_len // SUPER

    def body(q_ref, k0, k1, k2, v0, v1, v2, o_ref, do_ref, b_ref, lse_ref, dp_ref, db_ref, aq_ref, ak_ref, av_ref):
        i = pl.program_id(1)

        @pl.when(i == 0)
        def _():
            aq_ref[...] = jnp.zeros_like(aq_ref)
            ak_ref[...] = jnp.zeros_like(ak_ref)
            av_ref[...] = jnp.zeros_like(av_ref)
            db_ref[...] = jnp.zeros_like(db_ref)

        @pl.when(i < nb)
        def _():
            dq, dk, dv = [], [[], [], []], [[], [], []]
            ones = jnp.ones((8, HEAD_DIM), BF16)
            k_t = [k0[...].T, k1[...].T, k2[...].T]

            def softmax_grad(hh):
                sl = _head(hh)
                qs, do = q_ref[:, sl], do_ref[:, sl]
                kn = [k0[:, sl], k1[:, sl], k2[:, sl]]
                prod = do.astype(F32) * o_ref[:, sl].astype(F32)
                hi = prod.astype(BF16)
                lo = (prod - hi.astype(F32)).astype(BF16)
                delta = (lax.dot_general(ones, hi, NT, preferred_element_type=F32)
                         + lax.dot_general(ones, lo, NT, preferred_element_type=F32))[0:1]
                lse_row = lse_ref[hh:hh + 1, :]
                pb, dsb = [], []
                for j, (kj, vj) in enumerate(zip(kn, (v0, v1, v2))):
                    rows = _key_block(j)
                    p = jnp.exp(lax.dot_general(kj, qs, NT, preferred_element_type=F32) + b_ref[hh, rows, :] - lse_row)
                    ds = p * (lax.dot_general(vj[:, sl], do, NT, preferred_element_type=F32) - delta)
                    db_ref[hh, rows, :] += ds
                    pb.append(p.astype(BF16))
                    dsb.append(ds.astype(BF16))
                return pb, dsb, qs, do, sl

            def operand_grads(pb, dsb, qs, do, sl):
                dq.append(sum(jnp.dot(k_t[j][sl, :], dsb[j], preferred_element_type=F32) for j in range(3)))
                for j in range(3):
                    dv[j].append(jnp.dot(pb[j], do, preferred_element_type=F32))
                    dk[j].append(jnp.dot(dsb[j], qs, preferred_element_type=F32))

            ready = softmax_grad(0)
            for hh in range(1, HEADS_PER_STEP):
                following = softmax_grad(hh)
                operand_grads(*ready)
                ready = following
            operand_grads(*ready)
            aq_ref[i % 3] = jnp.concatenate(dq, axis=0).T
            for j in range(3):
                slot = (i + 1 + j) % 3
                if j < 2:
                    ak_ref[slot] += jnp.concatenate(dk[j], axis=1)
                    av_ref[slot] += jnp.concatenate(dv[j], axis=1)
                else:
                    ak_ref[slot] = jnp.concatenate(dk[j], axis=1)
                    av_ref[slot] = jnp.concatenate(dv[j], axis=1)

        slot = (i + 1) % 3
        dp_ref[0] = aq_ref[slot].astype(dp_ref.dtype)
        dp_ref[1] = ak_ref[slot].astype(dp_ref.dtype)
        dp_ref[2] = av_ref[slot].astype(dp_ref.dtype)

    def qrow(hg, i):
        return (jnp.minimum(i, nb - 1), hg)

    return _carry(
        riders, body, name="attn_bwd", grid=(N_HEAD_GROUPS, nb + 2),
        in_specs=[pl.BlockSpec((SUPER, HEAD_COLS), qrow)]
        + _band_specs(nb, N_HEAD_GROUPS, True) + _band_specs(nb, 0, True)
        + [pl.BlockSpec((SUPER, HEAD_COLS), qrow), pl.BlockSpec((SUPER, HEAD_COLS), qrow), _bias_spec(),
           pl.BlockSpec((None, LSE_ROWS, SUPER), lambda hg, i: (hg, 0, jnp.minimum(i, nb - 1)))],
        out_specs=[pl.BlockSpec((3, SUPER, HEAD_COLS), lambda hg, i: (0, jnp.maximum(i - 2, 0), hg)),
                   pl.BlockSpec((HEADS_PER_STEP, BAND, SUPER), lambda hg, i: (hg, 0, 0))],
        out_shape=[jax.ShapeDtypeStruct((6, s_len, D_MODEL), BF16),
                   jax.ShapeDtypeStruct((N_HEADS, BAND, SUPER), F32)],
        scratch_shapes=[pltpu.VMEM((3, SUPER, HEAD_COLS), F32)] * 3,
        semantics=("parallel", "arbitrary"),
    )(qkn, qkn, qkn, qkn, v, v, v, out, d_out, bias, lse)


def _qk_norm_bwd(dproj6, qk_raw, gq, gk):
    s_len = qk_raw.shape[0]
    ts = min(1024, s_len)

    nsteps = s_len // ts
    half = D_MODEL // 2

    def body(d_ref, raw_ref, gq_ref, gk_ref, o_ref, dgq_ref, dgk_ref, acc_ref):
        step = pl.program_id(0)

        @pl.when(step == 0)
        def _():
            acc_ref[...] = jnp.zeros_like(acc_ref)

        for piece, (g_ref, scale) in enumerate(((gq_ref, QK_SCALE), (gk_ref, 1.0))):
            for c0 in (0, half):
                xhat, r = _head_unit(raw_ref[:, piece * D_MODEL + c0:piece * D_MODEL + c0 + half].astype(F32))
                dn = d_ref[piece, :, c0:c0 + half].astype(F32) * scale
                u = dn * g_ref[...]
                dx = r * (u - xhat * (_head_sums(u * xhat) * (1.0 / HEAD_DIM)))
                o_ref[piece, :, c0:c0 + half] = dx.astype(o_ref.dtype)
                acc_ref[piece:piece + 1, c0:c0 + half] += jnp.sum(dn * xhat, axis=0, keepdims=True)

        @pl.when(step == nsteps - 1)
        def _():
            lane = lax.broadcasted_iota(jnp.int32, (D_MODEL, LANE), 0) % HEAD_DIM
            fold = (lane == lax.broadcasted_iota(jnp.int32, (D_MODEL, LANE), 1)).astype(BF16)
            tot = sum(jnp.dot(p, fold, preferred_element_type=F32) for p in _split3(acc_ref[...]))
            dgq_ref[...] = tot[0:1, :HEAD_DIM]
            dgk_ref[...] = tot[1:2, :HEAD_DIM]

    gain = pl.BlockSpec((1, half), lambda i: (0, 0))
    small = pl.BlockSpec((1, HEAD_DIM), lambda i: (0, 0))
    per_head = lambda g: jnp.tile(g, (1, half // HEAD_DIM))
    return pl.pallas_call(
        body, name="qk_norm_bwd", grid=(nsteps,),
        in_specs=[pl.BlockSpec((2, ts, D_MODEL), lambda i: (0, i, 0)),
                  pl.BlockSpec((ts, 2 * D_MODEL), lambda i: (i, 0)), gain, gain],
        out_specs=[pl.BlockSpec((2, ts, D_MODEL), lambda i: (0, i, 0)), small, small],
        out_shape=[jax.ShapeDtypeStruct(dproj6.shape, dproj6.dtype),
                   jax.ShapeDtypeStruct((1, HEAD_DIM), F32), jax.ShapeDtypeStruct((1, HEAD_DIM), F32)],
        scratch_shapes=[pltpu.VMEM((8, D_MODEL), F32)],
        input_output_aliases={0: 0},
        compiler_params=_params("arbitrary"),
    )(dproj6, qk_raw, per_head(gq), per_head(gk))


CONV_ROWS = 512
HALO = 16


def _rows_with_halo(ref, r0, n, front, s_len):
    zeros = jnp.zeros((HALO, ref.shape[1]), F32)
    if front:
        return (jnp.concatenate([zeros, ref[0:n, :].astype(F32)], axis=0) if r0 == 0
                else ref[r0 - HALO:r0 + n, :].astype(F32))
    return (jnp.concatenate([ref[r0:r0 + n, :].astype(F32), zeros], axis=0) if r0 + n == s_len
            else ref[r0:r0 + n + HALO, :].astype(F32))


def _earlier(ext, k):
    return pltpu.roll(ext, k, 0)[HALO:]


def _later(ext, k):
    n = ext.shape[0]
    return pltpu.roll(ext, n - k, 0)[:n - HALO]


def _conv_cols(col0):
    return lambda s_len: pl.BlockSpec((s_len, LANE), lambda j: (0, col0 + j))


def _conv_fwd(proj, conv_w, conv_b, riders=()):
    s_len = proj.shape[0]

    def body(bg_ref, cg_ref, xc_ref, w_ref, b_ref, o_ref):
        w = [w_ref[t:t + 1, :] for t in range(3)]
        for r0 in range(0, s_len, CONV_ROWS):
            u = _rows_with_halo(cg_ref, r0, CONV_ROWS, True, s_len) * \
                _rows_with_halo(xc_ref, r0, CONV_ROWS, True, s_len)
            conv = b_ref[...] + w[0] * _earlier(u, 2) + w[1] * _earlier(u, 1) + w[2] * u[HALO:]
            o_ref[r0:r0 + CONV_ROWS, :] = (bg_ref[r0:r0 + CONV_ROWS, :].astype(F32) * conv).astype(o_ref.dtype)

    return _carry(
        riders, body, name="conv_fwd", grid=(D_MODEL // LANE,),
        in_specs=[_conv_cols(0)(s_len), _conv_cols(8)(s_len), _conv_cols(16)(s_len),
                  pl.BlockSpec((3, LANE), lambda j: (0, j)), pl.BlockSpec((1, LANE), lambda j: (0, j))],
        out_specs=pl.BlockSpec((s_len, LANE), lambda j: (0, j)),
        out_shape=jax.ShapeDtypeStruct((s_len, D_MODEL), BF16),
        semantics=("parallel",),
    )(proj, proj, proj, conv_w, conv_b)


def _conv_bwd(dproj6, dy, proj, conv_w, conv_b, riders=()):
    s_len = proj.shape[0]

    def body(dy_ref, bg_ref, cg_ref, xc_ref, w_ref, b_ref, _, dp_ref, dw_ref):
        w = [w_ref[t:t + 1, :] for t in range(3)]
        acc = [jnp.zeros((1, LANE), F32) for _ in range(4)]
        for r0 in range(0, s_len, CONV_ROWS):
            rows = slice(r0, r0 + CONV_ROWS)
            u = _rows_with_halo(cg_ref, r0, CONV_ROWS, True, s_len) * \
                _rows_with_halo(xc_ref, r0, CONV_ROWS, True, s_len)
            u2, u1, u0 = _earlier(u, 2), _earlier(u, 1), u[HALO:]
            conv = b_ref[...] + w[0] * u2 + w[1] * u1 + w[2] * u0
            dp_ref[0, rows, :] = (dy_ref[rows, :].astype(F32) * conv).astype(dp_ref.dtype)
            dconv_ext = _rows_with_halo(dy_ref, r0, CONV_ROWS, False, s_len) * \
                _rows_with_halo(bg_ref, r0, CONV_ROWS, False, s_len)
            dconv = dconv_ext[:CONV_ROWS]
            for t, term in enumerate([dconv * u2, dconv * u1, dconv * u0, dconv]):
                acc[t] = acc[t] + jnp.sum(term, axis=0, keepdims=True)
            du = w[2] * dconv + w[1] * _later(dconv_ext, 1) + w[0] * _later(dconv_ext, 2)
            dp_ref[1, rows, :] = (du * xc_ref[rows, :].astype(F32)).astype(dp_ref.dtype)
            dp_ref[2, rows, :] = (du * cg_ref[rows, :].astype(F32)).astype(dp_ref.dtype)
        dw_ref[...] = jnp.zeros_like(dw_ref)
        for t in range(4):
            dw_ref[t:t + 1, :] = acc[t]

    return _carry(
        riders, body, name="conv_bwd", grid=(D_MODEL // LANE,),
        in_specs=[pl.BlockSpec((s_len, LANE), lambda j: (0, j)),
                  _conv_cols(0)(s_len), _conv_cols(8)(s_len), _conv_cols(16)(s_len),
                  pl.BlockSpec((3, LANE), lambda j: (0, j)), pl.BlockSpec((1, LANE), lambda j: (0, j)),
                  pl.BlockSpec(memory_space=pl.ANY)],
        out_specs=[pl.BlockSpec((3, s_len, LANE), lambda j: (1, 0, j)),
                   pl.BlockSpec((8, LANE), lambda j: (0, j))],
        out_shape=[jax.ShapeDtypeStruct(dproj6.shape, dproj6.dtype),
                   jax.ShapeDtypeStruct((8, D_MODEL), F32)],
        input_output_aliases={6: 0},
        semantics=("parallel",),
    )(dy, proj, proj, proj, conv_w, conv_b, dproj6)


class _NoComm:
    def __init__(self, early, late):
        self.early, self.late = early, late

    def riders(self, at):
        return ()

    def done(self, at, carried):
        pass

    def early_weights(self):
        return self.early[0], self.early[2]

    def gate_weight(self):
        return self.early[1]

    def late_weights(self):
        return self.late

    def grads_ready(self, group, grads):
        pass


def _d_input(name, terms, riders=()):
    s_len, kdim = terms[0][0].shape[1], terms[0][1].shape[1]
    tm, chunk = 512, 512
    n = len(terms)

    def body(*refs):
        o_ref = refs[-1]
        acc = jnp.zeros(o_ref.shape, F32)
        for p_ref, w_ref in zip(refs[:n], refs[n:2 * n]):
            n_piece, _, width = p_ref.shape
            per_shard = w_ref.shape[2]
            for c0 in range(0, n_piece * width, chunk):
                acc = acc + lax.dot_general(p_ref[c0 // width, :, c0 % width:c0 % width + chunk],
                                            w_ref[c0 // per_shard, :, c0 % per_shard:c0 % per_shard + chunk],
                                            NT, preferred_element_type=F32)
        o_ref[...] = acc.astype(o_ref.dtype)

    tile = pl.BlockSpec((tm, kdim), lambda i: (i, 0))
    return _carry(
        riders, body, name=name, grid=(s_len // tm,),
        in_specs=[pl.BlockSpec((p.shape[0], tm, p.shape[2]), lambda i: (0, i, 0)) for p, _ in terms]
        + [pl.BlockSpec(w.shape, lambda i: (0, 0, 0), pipeline_mode=pl.Buffered(1)) for _, w in terms],
        out_specs=tile, out_shape=jax.ShapeDtypeStruct((s_len, kdim), BF16),
        semantics=("parallel",),
    )(*[p for p, _ in terms], *[w for _, w in terms])


def _local_grads(x, target, norm1_g, q_norm_g, k_norm_g, rel_bias, conv_b, b_gate, norm2_g, comm):
    s_len = x.shape[0]
    tm = min(1024, s_len)
    tb = min(2048, s_len)
    row = lambda v: v.reshape(1, -1)

    def carrying(at, fn, *args, **kw):
        riders = comm.riders(at)
        out = fn(*args, riders=riders, **kw)
        if riders:
            out, carried = out
            comm.done(at, carried)
        return out

    bias = carrying("bias_expand", _bias_expand, rel_bias)
    h = carrying("norm1", _rms_fwd, "norm1", x, row(norm1_g))
    w_in3, conv_w = comm.early_weights()
    gq, gk = row(q_norm_g), row(k_norm_g)

    def qk_epi(acc, e, o):
        o[0][...] = acc.astype(BF16)
        gain = jnp.where(pl.program_id(1) < 2, e[0] * QK_SCALE, e[1])
        o[1][...] = (_head_unit(acc)[0] * gain).astype(BF16)
    small = pl.BlockSpec((1, 512), lambda i, j, k: (0, 0))
    per_head = lambda g: jnp.tile(g, (1, 512 // HEAD_DIM))
    qk_raw, qkn = carrying("proj_qk", _mm_fwd, "proj_qk", h, w_in3, tm, 512, D_MODEL, ncols=2 * D_MODEL,
                           epilogue=qk_epi, outs=[_out2d(s_len, 2 * D_MODEL, BF16, tm, 512)] * 2,
                           extras=[(per_head(gq), small), (per_head(gk), small)])
    v = carrying("proj_v", _mm_fwd, "proj_v", h, w_in3, tb, 512, D_MODEL, col0=4, ncols=D_MODEL,
                 outs=[_out2d(s_len, D_MODEL, BF16, tb, 512)])
    w_gate3 = comm.gate_weight()
    conv_in = carrying("proj_conv", _mm_fwd, "proj_conv", h, w_in3, tb, 1536, D_MODEL, col0=2, ncols=3 * D_MODEL,
                       outs=[_out2d(s_len, 3 * D_MODEL, BF16, tb, 1536)])

    def gate_epi(acc, e, o):
        o[0][...] = jax.nn.sigmoid(acc + e[0]).astype(BF16)
    gates = carrying("gates", _mm_fwd, "gates", h, w_gate3, tb, 512, D_MODEL, epilogue=gate_epi,
                     outs=[_out2d(s_len, 2 * D_MODEL, BF16, tb, 512)],
                     extras=[(row(b_gate), pl.BlockSpec((1, 512), lambda i, j, k: (0, j)))])

    attn, lse = carrying("attn_fwd", _attn_fwd, qkn, v, bias)
    yconv = carrying("conv_fwd", _conv_fwd, conv_in, conv_w, row(conv_b))
    w_ap, w_cp, w_out, w_up3, w_down = comm.late_weights()
    tw = 1024
    ya = _mm_fwd("attn_proj", attn, w_ap, tm, tw, D_MODEL, outs=[_out2d(s_len, D_MODEL, BF16, tm, tw)])

    def merge_epi(acc, e, o):
        ya_v, ga, gc = [t.astype(F32) for t in e]
        o[0][...] = acc.astype(BF16)
        o[1][...] = (ga * ya_v + gc * acc).astype(BF16)
    gate_a, gate_c = (gates, _tile_spec(tm, tw, 0)), (gates, _tile_spec(tm, tw, 1))
    yc, merged = _mm_fwd("conv_proj", yconv, w_cp, tm, tw, D_MODEL, epilogue=merge_epi,
                         outs=[_out2d(s_len, D_MODEL, BF16, tm, tw), _out2d(s_len, D_MODEL, BF16, tm, tw)],
                         extras=[(ya, _tile_spec(tm, tw)), gate_a, gate_c])

    def res_epi(acc, e, o):
        o[0][...] = e[0] + acc
    x1 = _mm_fwd("out_proj", merged, w_out, tm, tw, D_MODEL, epilogue=res_epi,
                 extras=[(x, _tile_spec(tm, tw))])
    h2 = _rms_fwd("norm2", x1, row(norm2_g))

    def up_epi(acc, e, o):
        o[0][...] = jnp.square(jnp.maximum(acc, 0.0)).astype(BF16)
    act = _mm_fwd("mlp_up", h2, w_up3, tb, tw, D_MODEL, epilogue=up_epi, outs=[_out2d(s_len, D_FF, BF16, tb, tw)])

    def loss_epi(acc, e, o):
        err = e[0] + acc - e[1]
        o[0][...] = (err * (1.0 / D_MODEL)).astype(BF16)
        sq = err * err
        part = sq[:, 0:LANE]
        for c0 in range(LANE, D_MODEL, LANE):
            part = part + sq[:, c0:c0 + LANE]
        o[1][...] = jnp.sum(part.reshape(tl // 8, 8, LANE), axis=0)
    tl = 512
    dy_b, loss_part = _mm_fwd(
        "mlp_down", act, w_down, tl, D_MODEL, D_FF, epilogue=loss_epi,
        outs=[_out2d(s_len, D_MODEL, BF16, tl, D_MODEL),
              (jax.ShapeDtypeStruct((8 * (s_len // tl), LANE), F32), pl.BlockSpec((8, LANE), lambda i, j, k: (i, 0)))],
        extras=[(x1, _tile_spec(tl, D_MODEL)), (target, _tile_spec(tl, D_MODEL))])

    def dup_epi(acc, e, o):
        o[0][...] = (acc * (2.0 * jnp.sqrt(e[0].astype(F32)))).astype(BF16)
    full = lambda cols: pl.BlockSpec((tm, cols), lambda i, j, n: (i, n))
    tokens = lambda cols: pl.BlockSpec((s_len, cols), lambda i, j, m: (m, j))
    dup = _mm_bwd_x("d_act", dy_b, full(D_MODEL), w_down, tm, tw, D_MODEL, s_len, D_MODEL, epilogue=dup_epi,
                    outs=[_out2d(s_len, D_FF, BF16, tm, tw)], extras=[(act, _tile_spec(tm, tw))])
    g_down = _mm_bwd_w("g_down", act, dy_b, tokens(D_MODEL), D_MODEL, 512, D_MODEL, s_len, False)
    g_up = _mm_bwd_w("g_up", h2, dup, tokens(512), D_FF, D_MODEL, 512, s_len, True)
    comm.grads_ready("mlp", dict(w_down=g_down, w_up=g_up))
    dh2 = carrying("d_h2", _d_input, "d_h2", [(dup.reshape(1, s_len, D_FF), w_up3)])
    dx1_b, dg2 = _rms_bwd("norm2_bwd", dh2, x1, row(norm2_g), dy_b)

    def dmerge_epi(acc, e, o):
        ya_v, yc_v, ga, gc = [t.astype(F32) for t in e]
        o[0][...] = (acc * ga).astype(BF16)
        o[1][...] = (acc * gc).astype(BF16)
        o[2][0] = (acc * ya_v * ga * (1.0 - ga)).astype(BF16)
        o[2][1] = (acc * yc_v * gc * (1.0 - gc)).astype(BF16)
    dya, dyc, dgp2 = _mm_bwd_x(
        "d_merged", dx1_b, full(D_MODEL), w_out, tm, tw, D_MODEL, s_len, D_MODEL, epilogue=dmerge_epi,
        outs=[_out2d(s_len, D_MODEL, BF16, tm, tw), _out2d(s_len, D_MODEL, BF16, tm, tw),
              (jax.ShapeDtypeStruct((2, s_len, D_MODEL), BF16), pl.BlockSpec((2, tm, tw), lambda i, j, n: (0, i, j)))],
        extras=[(ya, _tile_spec(tm, tw)), (yc, _tile_spec(tm, tw)), gate_a, gate_c])
    g_out = _mm_bwd_w("g_out", merged, dx1_b, tokens(512), D_MODEL, D_MODEL, 512, s_len, False)
    d_attn = _mm_bwd_x("d_attn", dya, full(D_MODEL), w_ap, tm, tw, D_MODEL, s_len, D_MODEL,
                       outs=[_out2d(s_len, D_MODEL, BF16, tm, tw)])
    g_ap = _mm_bwd_w("g_attn_proj", attn, dya, tokens(512), D_MODEL, D_MODEL, 512, s_len, False)
    d_yconv = _mm_bwd_x("d_yconv", dyc, full(D_MODEL), w_cp, tm, tw, D_MODEL, s_len, D_MODEL,
                        outs=[_out2d(s_len, D_MODEL, BF16, tm, tw)])
    g_cp = _mm_bwd_w("g_conv_proj", yconv, dyc, tokens(512), D_MODEL, D_MODEL, 512, s_len, False)
    piece = lambda width: (lambda blk: (blk * width) // D_MODEL, lambda blk: (blk * width % D_MODEL) // width)
    pc, cb = piece(512)
    pieces = pl.BlockSpec((None, s_len, 512), lambda i, j, m: (pc(j), m, cb(j)))
    g_gate = _mm_bwd_w("g_gate", h, dgp2, pieces, 2 * D_MODEL, D_MODEL, 512, s_len, True)
    comm.grads_ready("proj", dict(w_out=g_out, w_attn_proj=g_ap, w_conv_proj=g_cp, w_gate=g_gate))

    dproj6, dbias = carrying("attn_bwd", _attn_bwd, qkn, v, attn, d_attn, bias, lse)
    dproj6, dgq, dgk = _qk_norm_bwd(dproj6, qk_raw, gq, gk)
    dproj6, dconv_wb = carrying("conv_bwd", _conv_bwd, dproj6, d_yconv, conv_in, conv_w, row(conv_b))

    g_in = carrying("g_in", _mm_bwd_w, "g_in", h, dproj6, pieces, 6 * D_MODEL, D_MODEL, 512, s_len, True)
    comm.grads_ready("in", dict(w_in=g_in))
    d_rel = carrying("bias_reduce", _bias_reduce, dbias)
    dh = carrying("d_h", _d_input, "d_h", [(dproj6, w_in3), (dgp2, w_gate3)])
    grad_x, dg1 = carrying("norm1_bwd", _rms_bwd, "norm1_bwd", dh, x, row(norm1_g), dx1_b, out_dtype=F32)

    def bsum(t, f):
        return [], [t[0].astype(F32), t[1].astype(F32)]
    ts = 512
    db_a, db_c = carrying("b_gate_sum", _ew, "b_gate_sum", bsum,
                          [(dgp2, pl.BlockSpec((None, ts, D_MODEL), lambda i: (0, i, 0))),
                           (dgp2, pl.BlockSpec((None, ts, D_MODEL), lambda i: (1, i, 0)))],
                          [], [], sums=[D_MODEL, D_MODEL], ts=ts)

    big = dict(w_in=g_in, w_attn_proj=g_ap, w_conv_proj=g_cp, w_gate=g_gate, w_out=g_out,
               w_up=g_up, w_down=g_down)
    small = dict(norm1_g=dg1, norm2_g=dg2, conv_wb=dconv_wb, b_gate=(db_a, db_c),
                 q_norm_g=dgq, k_norm_g=dgk, rel_bias=d_rel)
    return loss_part, grad_x, big, small


def _finish_loss(loss_part):
    def body(l_ref, lo_ref):
        total = jnp.sum(jnp.sum(l_ref[...], axis=0, keepdims=True), axis=1, keepdims=True)
        lo_ref[...] = jnp.broadcast_to(total * (0.5 / D_MODEL), lo_ref.shape)

    return pl.pallas_call(body, name="finish_loss", out_shape=jax.ShapeDtypeStruct((8, LANE), F32))(loss_part)


HBM_SPEC = pl.BlockSpec(memory_space=pl.ANY)


def _place():
    return lax.axis_index("x"), lax.axis_index("y"), lax.axis_index("c")


def _other_chips(x, y):
    return [(1 - x, y), (x, 1 - y), (1 - x, 1 - y)]


def _cast_into_slot(name, where, w):
    r, cols = w.shape
    ts = 256

    def body(w_ref, x_ref, o_ref):
        o_ref[...] = x_ref[...].astype(o_ref.dtype)

    return pl.pallas_call(
        body, name=name,
        grid_spec=pltpu.PrefetchScalarGridSpec(
            num_scalar_prefetch=1, grid=(r // ts,),
            in_specs=[pl.BlockSpec((ts, cols), lambda i, w: (i, 0))],
            out_specs=pl.BlockSpec((None, ts, cols), lambda i, w: (w[0], i, 0))),
        out_shape=jax.ShapeDtypeStruct((N_SHARD, r, cols), BF16),
        compiler_params=_params("parallel"),
    )(where, w)


def _remote(src, dst, send, recv, k, to):
    return pltpu.make_async_remote_copy(src_ref=src, dst_ref=dst, send_sem=send.at[k], recv_sem=recv.at[k],
                                        device_id=to, device_id_type=MESH)


def _gather_riders(slots):
    n = len(slots)

    def half(refs, w, shard, which):
        hr = slots[w].shape[1] // 2
        return refs[w].at[shard, pl.ds(which * hr, hr)]

    def each(fn):
        x, y, c = _place()
        for w in range(n):
            for j, chip in enumerate(_other_chips(x, y)):
                fn(w, 3 * w + j, 2 * x + y, 2 * chip[0] + chip[1], c, (*chip, c), (x, y, 1 - c))

    def chips_start(_, refs, send, recv):
        each(lambda w, k, me, them, c, peer, sib: _remote(half(refs, w, me, c), half(refs, w, me, c), send, recv, k, peer).start())

    def chips_finish(_, refs, send, recv):
        each(lambda w, k, me, them, c, peer, sib: _remote(half(refs, w, them, c), half(refs, w, them, c), send, recv, k, peer).wait_recv())
        each(lambda w, k, me, them, c, peer, sib: _remote(half(refs, w, me, c), half(refs, w, me, c), send, recv, k, peer).wait_send())

    def sibling_start(_, refs, send, recv):
        each(lambda w, k, me, them, c, peer, sib: _remote(half(refs, w, them, c), half(refs, w, them, c), send, recv, k, sib).start())

    def sibling_finish(_, refs, send, recv):
        each(lambda w, k, me, them, c, peer, sib: _remote(half(refs, w, them, 1 - c), half(refs, w, them, 1 - c), send, recv, k, sib).wait_recv())
        each(lambda w, k, me, them, c, peer, sib: _remote(half(refs, w, them, c), half(refs, w, them, c), send, recv, k, sib).wait_send())

    return (lambda s: _Rider([], s, 3 * n, chips_start, chips_finish),
            lambda s: _Rider([], s, 3 * n, sibling_start, sibling_finish))


def _pair_exchange_rider(grads, landing):
    n = len(grads)

    def copies(srcs, dsts, send, recv):
        x, y, c = _place()
        out = []
        for w in range(n):
            hr = grads[w].shape[1] // 2
            out.append(_remote(srcs[w].at[:, pl.ds((1 - c) * hr, hr)], dsts[w], send, recv, w, (x, y, 1 - c)))
        return out

    def start(srcs, dsts, send, recv):
        for cp in copies(srcs, dsts, send, recv):
            cp.start()

    def finish(srcs, dsts, send, recv):
        for cp in copies(srcs, dsts, send, recv):
            cp.wait()

    return _Rider(grads, landing, n, start, finish)


def _row_tile(hr):
    return min(hr, 256)


def _pair_add(name, where, grad, got):
    _, hr, cols = got.shape
    tr = _row_tile(hr)
    nblk = hr // tr

    def body(w_ref, g_ref, r_ref, o_ref):
        o_ref[...] = (g_ref[...] + r_ref[...]).astype(o_ref.dtype)

    other = lambda k, w: (w[0] + 1 + k) % N_SHARD
    return pl.pallas_call(
        body, name=name,
        grid_spec=pltpu.PrefetchScalarGridSpec(
            num_scalar_prefetch=1, grid=(N_SHARD - 1, nblk),
            in_specs=[pl.BlockSpec((None, tr, cols), lambda k, i, w: (other(k, w), w[1] * nblk + i, 0)),
                      pl.BlockSpec((None, tr, cols), lambda k, i, w: (other(k, w), i, 0))],
            out_specs=pl.BlockSpec((None, tr, cols), lambda k, i, w: (other(k, w), i, 0))),
        out_shape=jax.ShapeDtypeStruct(got.shape, BF16),
        compiler_params=_params("parallel", "parallel"),
    )(where, grad, got)


def _chip_exchange_rider(partials, landing):
    n = len(partials)

    def copies(srcs, dsts, send, recv):
        x, y, c = _place()
        return [_remote(srcs[w].at[2 * chip[0] + chip[1]], dsts[w].at[j], send, recv, 3 * w + j, (*chip, c))
                for w in range(n) for j, chip in enumerate(_other_chips(x, y))]

    def start(srcs, dsts, send, recv):
        for cp in copies(srcs, dsts, send, recv):
            cp.start()

    def finish(srcs, dsts, send, recv):
        for cp in copies(srcs, dsts, send, recv):
            cp.wait()

    return _Rider(partials, landing, 3 * n, start, finish)


def _final_add(name, where, grad, got, arrived):
    _, hr, cols = got.shape
    tr = _row_tile(hr)
    nblk = hr // tr

    def body(w_ref, g_ref, r_ref, a_ref, o_ref):
        acc = g_ref[...] + r_ref[...]
        for j in range(3):
            acc = acc + a_ref[j].astype(F32)
        o_ref[...] = acc

    return pl.pallas_call(
        body, name=name,
        grid_spec=pltpu.PrefetchScalarGridSpec(
            num_scalar_prefetch=1, grid=(nblk,),
            in_specs=[pl.BlockSpec((None, tr, cols), lambda i, w: (w[0], w[1] * nblk + i, 0)),
                      pl.BlockSpec((None, tr, cols), lambda i, w: (w[0], i, 0)),
                      pl.BlockSpec((3, tr, cols), lambda i, w: (0, i, 0))],
            out_specs=pl.BlockSpec((tr, cols), lambda i, w: (w[1] * nblk + i, 0))),
        out_shape=jax.ShapeDtypeStruct((2 * hr, cols), F32),
        compiler_params=_params("parallel"),
    )(where, grad, got, arrived)


def _pair_share_rider(shards):
    n = len(shards)

    def half(refs, w, which):
        hr = shards[w].shape[0] // 2
        return refs[w].at[pl.ds(which * hr, hr)]

    def start(_, refs, send, recv):
        x, y, c = _place()
        for w in range(n):
            _remote(half(refs, w, c), half(refs, w, c), send, recv, w, (x, y, 1 - c)).start()

    def finish(_, refs, send, recv):
        x, y, c = _place()
        for w in range(n):
            _remote(half(refs, w, 1 - c), half(refs, w, 1 - c), send, recv, w, (x, y, 1 - c)).wait_recv()
        for w in range(n):
            _remote(half(refs, w, c), half(refs, w, c), send, recv, w, (x, y, 1 - c)).wait_send()

    return _Rider([], shards, n, start, finish)


class _Exchange:
    PLAN = {"bias_expand": [("early", "gather")], "norm1": [("early", "forward")],
            "proj_qk": [("gate", "gather")], "proj_v": [("gate", "forward")],
            "proj_conv": [("late", "gather")], "gates": [("late", "forward")],
            "attn_fwd": [("mlp_w", "gather")], "conv_fwd": [("mlp_w", "forward")],
            "d_h2": [("mlp", "pair")], "attn_bwd": [("mlp", "chips"), ("proj", "pair")], "conv_bwd": [("mlp", "share")],
            "g_in": [("proj", "chips")], "bias_reduce": [("proj", "share"), ("in", "pair")],
            "d_h": [("in", "chips")], "b_gate_sum": [("in", "share")]}

    def __init__(self, where, early_slots, gate_slots, late_slots):
        self.where = where
        self.slots = dict(early=early_slots, gate=gate_slots, late=late_slots[:3], mlp_w=late_slots[3:])
        self.stage = {g: dict(zip(("gather", "forward"), _gather_riders(s))) for g, s in self.slots.items()}
        self.groups, self.reduced, self.pending = {}, {}, []

    def early_weights(self):
        w_in3, small = self.slots["early"]
        conv_w = small[:, :3, :].transpose(1, 0, 2).reshape(3, N_SHARD * small.shape[2])
        return w_in3, conv_w

    def gate_weight(self):
        return self.slots["gate"][0]

    def late_weights(self):
        rows = lambda a: a.reshape(a.shape[0] * a.shape[1], a.shape[2])
        w_ap, w_cp, w_out = self.slots["late"]
        w_up3, w_down = self.slots["mlp_w"]
        return rows(w_ap), rows(w_cp), rows(w_out), w_up3, rows(w_down)

    def grads_ready(self, group, grads):
        names = list(grads)
        g4 = [g if g.ndim == 3 else g.reshape(N_SHARD, -1, g.shape[1]) for g in grads.values()]
        self.groups[group] = dict(names=names, g4=g4)

    def riders(self, at):
        self.pending = self.PLAN.get(at, [])
        out = []
        for group, stage in self.pending:
            if group in self.slots:
                out.append(self.stage[group][stage](self.slots[group]))
                continue
            st = self.groups[group]
            if stage == "pair":
                landing = [lax.empty((N_SHARD, g.shape[1] // 2, g.shape[2]), F32) for g in st["g4"]]
                out.append(_pair_exchange_rider(st["g4"], landing))
            elif stage == "chips":
                landing = [lax.empty((3,) + p.shape[1:], p.dtype) for p in st["partial"]]
                out.append(_chip_exchange_rider(st["partial"], landing))
            else:
                out.append(_pair_share_rider(st["halves"]))
        return out

    def done(self, at, carried):
        for (group, stage), arrays in zip(self.pending, carried):
            if group in self.slots:
                self.slots[group] = arrays
                continue
            st = self.groups[group]
            tag = lambda what, n: what + "_" + n
            if stage == "pair":
                st["got"] = arrays
                st["partial"] = [_pair_add(tag("pair_add", n), self.where, g, r)
                                 for n, g, r in zip(st["names"], st["g4"], arrays)]
            elif stage == "chips":
                st["halves"] = [_final_add(tag("final_add", n), self.where, g, r, a)
                                for n, g, r, a in zip(st["names"], st["g4"], st["got"], arrays)]
            else:
                self.reduced.update(zip(st["names"], arrays))


SMALL_ROWS = 32
N_DEV = 8


def _all_reduce_small(pack):
    def body(p_ref, o_ref, buf, send, recv):
        x, y, c = _place()
        buf[4 * x + 2 * y + c] = p_ref[...]
        copies, waits = [], []
        for k in range(1, N_DEV):
            px = 1 - x if k & 4 else x
            py = 1 - y if k & 2 else y
            pc = 1 - c if k & 1 else c
            copies.append(_remote(p_ref, buf.at[4 * x + 2 * y + c], send, recv, k - 1, (px, py, pc)))
            waits.append(_remote(p_ref, buf.at[4 * px + 2 * py + pc], send, recv, k - 1, (px, py, pc)))
        for cp in copies:
            cp.start()
        for cp in waits:
            cp.wait_recv()
        acc = buf[0]
        for d in range(1, N_DEV):
            acc = acc + buf[d]
        o_ref[...] = acc
        for cp in copies:
            cp.wait_send()

    return pl.pallas_call(
        body, name="all_reduce_small",
        out_shape=jax.ShapeDtypeStruct(pack.shape, F32),
        scratch_shapes=[pltpu.VMEM((N_DEV,) + pack.shape, F32),
                        pltpu.SemaphoreType.DMA((N_DEV - 1,)), pltpu.SemaphoreType.DMA((N_DEV - 1,))],
    )(pack)


def _adamw(name, w, g, m, v, riders=()):
    c1 = 1.0 - ADAM_B1 ** ADAM_STEP
    c2 = 1.0 - ADAM_B2 ** ADAM_STEP

    def fn(t, f):
        wv, gv, mv, vv = t
        m2 = ADAM_B1 * mv + (1.0 - ADAM_B1) * gv
        v2 = ADAM_B2 * vv + (1.0 - ADAM_B2) * (gv * gv)
        delta = -ADAM_LR * ((m2 / c1) / (jnp.sqrt(v2 / c2) + ADAM_EPS) + ADAM_WD * wv)
        return [delta, m2, v2], []

    cols = w.shape[1]
    return _ew(name, fn, [w, g, m, v], [], [(cols, F32)] * 3, ts=min(w.shape[0], 256), riders=riders)


LOSS_ROW = 26


def _pack_small(norm1_g, norm2_g, conv_b, b_gate, conv_w, q_norm_g, k_norm_g, rel_bias, loss=None):
    pack = jnp.zeros((SMALL_ROWS, D_MODEL), F32)
    for r0, v in ((0, norm1_g), (1, norm2_g), (2, conv_b), (3, b_gate.reshape(2, D_MODEL)), (5, conv_w),
                  (8, q_norm_g), (9, k_norm_g), (10, rel_bias)) + (((LOSS_ROW, loss),) if loss is not None else ()):
        v = v.reshape(-1, v.shape[-1]).astype(F32)
        pack = pack.at[r0:r0 + v.shape[0], :v.shape[1]].set(v)
    return pack


def _unpack_small(pack, conv_cols):
    return dict(norm1_g=pack[0], norm2_g=pack[1], conv_b=pack[2], b_gate=pack[3:5].reshape(2 * D_MODEL),
                conv_w=pack[5:8, :conv_cols], q_norm_g=pack[8, :HEAD_DIM], k_norm_g=pack[9, :HEAD_DIM],
                rel_bias=pack[10:10 + N_HEADS, :N_REL])


BIG = ["w_in", "w_attn_proj", "w_conv_proj", "w_gate", "w_out", "w_up", "w_down"]
LATE = ["w_attn_proj", "w_conv_proj", "w_out", "w_up", "w_down"]
WEIGHTS = ["norm1_g", "w_in", "q_norm_g", "k_norm_g", "rel_bias", "conv_w", "conv_b", "w_attn_proj",
           "w_conv_proj", "w_gate", "b_gate", "w_out", "norm2_g", "w_up", "w_down"]


def kernel(x, norm1_g, w_in, q_norm_g, k_norm_g, rel_bias, conv_w, conv_b, w_attn_proj, w_conv_proj, w_gate, b_gate, w_out, norm2_g, w_up, w_down, loss_target, m_norm1_g, m_w_in, m_q_norm_g, m_k_norm_g, m_rel_bias, m_conv_w, m_conv_b, m_w_attn_proj, m_w_conv_proj, m_w_gate, m_b_gate, m_w_out, m_norm2_g, m_w_up, m_w_down, v_norm1_g, v_w_in, v_q_norm_g, v_k_norm_g, v_rel_bias, v_conv_w, v_conv_b, v_w_attn_proj, v_w_conv_proj, v_w_gate, v_b_gate, v_w_out, v_norm2_g, v_w_up, v_w_down):
    given = dict(locals())
    w = {n: given[n] for n in WEIGHTS}
    m = {n: given["m_" + n] for n in WEIGHTS}
    v = {n: given["v_" + n] for n in WEIGHTS}
    s_len = x.shape[1]
    shard = 2 * lax.axis_index("x") + lax.axis_index("y")
    where = jnp.stack([shard, lax.axis_index("c")]).astype(jnp.int32)
    conv_cols = conv_w.shape[1]

    small_in = lax.dynamic_update_slice(jnp.zeros((N_SHARD, 16, conv_cols), F32), conv_w[None], (shard, 0, 0))
    slot = {n: _cast_into_slot("cast_" + n, where, w[n]) for n in BIG}
    comm = _Exchange(where, [slot["w_in"], small_in], [slot["w_gate"]], [slot[n] for n in LATE])

    loss_part, grad_x, _, small = _local_grads(
        x.reshape(s_len, D_MODEL), loss_target.reshape(s_len, D_MODEL), norm1_g, q_norm_g, k_norm_g,
        rel_bias, conv_b, b_gate, norm2_g, comm)
    grad = dict(comm.reduced)

    loss_local = _finish_loss(loss_part)
    pack = _pack_small(small["norm1_g"], small["norm2_g"], small["conv_wb"][3], jnp.concatenate(small["b_gate"], axis=1),
                       small["conv_wb"][0:3], small["q_norm_g"], small["k_norm_g"], small["rel_bias"],
                       loss=loss_local[0:1, :])
    total = _all_reduce_small(pack)
    g_small = _unpack_small(total, D_MODEL)
    g_small["conv_w"] = lax.dynamic_slice(g_small["conv_w"], (0, shard * conv_cols), (3, conv_cols))
    grad.update(g_small)

    delta, new_m, new_v = {}, {}, {}
    for n in BIG:
        delta[n], new_m[n], new_v[n] = _adamw("adamw_" + n, w[n], grad[n], m[n], v[n])
    small_names = [n for n in WEIGHTS if n not in BIG]
    packs = [_pack_small(**{n: src[n] for n in small_names}) for src in (w, grad, m, v)]
    for out, packed in zip((delta, new_m, new_v), _adamw("adamw_small", *packs)):
        out.update({n: a.reshape(w[n].shape) for n, a in _unpack_small(packed, conv_cols).items()})

    outs = [total[LOSS_ROW, 0], grad_x.reshape(x.shape)]
    for group in (grad, delta, new_m, new_v):
        outs += [group[n].reshape(w[n].shape) for n in WEIGHTS]
    return tuple(outs)
```

```python
import jax
import jax.numpy as jnp
from jax import lax
from jax.experimental import pallas as pl
from jax.experimental.pallas import tpu as pltpu

F32 = jnp.float32
BF16 = jnp.bfloat16

D_MODEL = 1024
N_HEADS = 16
HEAD_DIM = 64
CHUNK = 64
N_PREV_CHUNKS = 8
MAX_REL = 256
D_FF = 4096
N_REL = 2 * MAX_REL + 1
REL_PAD = 640
EPS = 1e-6
NEG_INF = -1e30
QK_SCALE = HEAD_DIM ** -0.5

SUPER = 4 * CHUNK
BAND = SUPER + N_PREV_CHUNKS * CHUNK
SKEW_W = 1024
N_SHARD = 4
LANE = 128
MXU_DIM = 256
VMEM_LIMIT = 48 * 1024 * 1024

ADAM_LR = 0.001
ADAM_B1 = 0.9
ADAM_B2 = 0.999
ADAM_EPS = 1e-08
ADAM_WD = 0.01
ADAM_STEP = 10

MESH = pl.DeviceIdType.MESH
NN = (((1,), (0,)), ((), ()))
NT = (((1,), (1,)), ((), ()))
TN = (((0,), (0,)), ((), ()))


def _params(*sem):
    return pltpu.CompilerParams(dimension_semantics=sem or None, vmem_limit_bytes=VMEM_LIMIT)


HBM_SPEC = pl.BlockSpec(memory_space=pl.ANY)


class _Rider:
    def __init__(self, sources, arrays, n_sem, start, finish):
        self.sources, self.arrays, self.n_sem, self.start, self.finish = sources, arrays, n_sem, start, finish


def _carry(riders, body, *, name, out_shape, grid=(), in_specs=None, out_specs=None, scratch_shapes=(),
           semantics=(), input_output_aliases=None):
    aliases = dict(input_output_aliases or {})
    if not riders:
        kw = {} if in_specs is None else dict(in_specs=in_specs, out_specs=out_specs)
        return pl.pallas_call(body, name=name, grid=grid, out_shape=out_shape, scratch_shapes=scratch_shapes,
                              input_output_aliases=aliases, compiler_params=_params(*semantics), **kw)
    single = not isinstance(out_shape, (list, tuple))
    shapes = [out_shape] if single else list(out_shape)
    n_out, n_scr = len(shapes), len(scratch_shapes)
    in_hbm = lambda a: pltpu.with_memory_space_constraint(a, pltpu.HBM)
    srcs = [in_hbm(a) for r in riders for a in r.sources]
    arrs = [in_hbm(a) for r in riders for a in r.arrays]
    vmem = pl.BlockSpec(memory_space=pltpu.VMEM)

    def run(*args):
        n_in = len(args)

        def wrapped(*refs):
            pos = n_in
            src_refs = refs[pos:pos + len(srcs)]
            pos += len(srcs) + len(arrs)
            outs = refs[pos:pos + n_out]
            pos += n_out
            arr_refs = refs[pos:pos + len(arrs)]
            pos += len(arrs)
            scratch = refs[pos:pos + n_scr]
            sems = refs[pos + n_scr:]
            first, last = True, True
            for d, size in enumerate(grid):
                first = jnp.logical_and(first, pl.program_id(d) == 0)
                last = jnp.logical_and(last, pl.program_id(d) == size - 1)

            def each(method):
                s0 = a0 = 0
                for k, r in enumerate(riders):
                    getattr(r, method)(src_refs[s0:s0 + len(r.sources)], arr_refs[a0:a0 + len(r.arrays)],
                                       sems[2 * k], sems[2 * k + 1])
                    s0, a0 = s0 + len(r.sources), a0 + len(r.arrays)

            pl.when(first)(lambda: each("start"))
            body(*refs[:n_in], *outs, *scratch)
            pl.when(last)(lambda: each("finish"))

        ins = [vmem] * n_in if in_specs is None else list(in_specs)
        if out_specs is None:
            o_specs = [vmem] * n_out
        else:
            o_specs = [out_specs] if single else list(out_specs)
        for k in range(len(arrs)):
            aliases[n_in + len(srcs) + k] = n_out + k
        res = pl.pallas_call(
            wrapped, name=name, grid=grid,
            in_specs=ins + [HBM_SPEC] * (len(srcs) + len(arrs)),
            out_specs=o_specs + [HBM_SPEC] * len(arrs),
            out_shape=shapes + [jax.ShapeDtypeStruct(a.shape, a.dtype) for a in arrs],
            scratch_shapes=list(scratch_shapes) + [pltpu.SemaphoreType.DMA((r.n_sem,)) for r in riders for _ in range(2)],
            input_output_aliases=aliases,
            compiler_params=_params(*["arbitrary"] * len(grid)),
        )(*args, *srcs, *arrs)
        core, rest = res[:n_out], list(res[n_out:])
        carried, a0 = [], 0
        for r in riders:
            carried.append(rest[a0:a0 + len(r.arrays)])
            a0 += len(r.arrays)
        return (core[0] if single else core), carried

    return run


def _mm(name, dims, a, a_spec, b, b_spec, grid, tile, outs, epilogue=None, extras=(), riders=()):
    nk, ne, no = grid[2], len(extras), len(outs)

    def body(a_ref, b_ref, *refs):
        e_refs, o_refs = refs[:ne], refs[ne:ne + no]
        part = lax.dot_general(a_ref[...], b_ref[...], dims, preferred_element_type=F32)

        def finish(acc):
            if epilogue is None:
                o_refs[0][...] = acc.astype(o_refs[0].dtype)
            else:
                epilogue(acc, [r[...] for r in e_refs], o_refs)

        if nk == 1:
            finish(part)
        else:
            acc_ref = refs[ne + no]
            k = pl.program_id(2)

            @pl.when(k == 0)
            def _():
                acc_ref[...] = part

            @pl.when(k > 0)
            def _():
                acc_ref[...] += part

            @pl.when(k == nk - 1)
            def _():
                finish(acc_ref[...])

    res = _carry(
        riders, body, name=name, grid=grid,
        in_specs=[a_spec, b_spec] + [s for _, s in extras],
        out_specs=[s for _, s in outs],
        out_shape=[s for s, _ in outs],
        scratch_shapes=[pltpu.VMEM(tile, F32)] if nk > 1 else [],
        semantics=("parallel", "parallel", "arbitrary"),
    )(a, b, *[e for e, _ in extras])
    res, carried = res if riders else (res, None)
    res = res[0] if no == 1 else res
    return (res, carried) if riders else res


def _tile_spec(tm, tn, col0=0):
    return pl.BlockSpec((tm, tn), lambda i, j, k: (i, j + col0))


def _out2d(m, n, dtype, tm, tn):
    return (jax.ShapeDtypeStruct((m, n), dtype), _tile_spec(tm, tn))


def _mm_fwd(name, a, w, tm, tn, tk, outs=None, epilogue=None, extras=(), col0=0, ncols=None, riders=()):
    m, kdim = a.shape
    if w.ndim == 3:
        per = w.shape[2] // tn
        n = ncols or N_SHARD * w.shape[2]
        w_spec = pl.BlockSpec((None, tk, tn), lambda i, j, k: ((j + col0) // per, k, (j + col0) % per))
    else:
        n = ncols or w.shape[1]
        w_spec = pl.BlockSpec((tk, tn), lambda i, j, k: (k, j + col0))
    if outs is None:
        outs = [_out2d(m, n, F32, tm, tn)]
    return _mm(name, NN, a, pl.BlockSpec((tm, tk), lambda i, j, k: (i, k)), w, w_spec,
               (m // tm, n // tn, kdim // tk), (tm, tn), outs, epilogue, extras, riders)


def _mm_bwd_x(name, g, g_spec, w, tm, tj, tc, m, n_contract, outs=None, epilogue=None, extras=(), riders=()):
    if w.ndim == 3:
        per = w.shape[2] // tc
        kdim = w.shape[1]
        w_spec = pl.BlockSpec((None, tj, tc), lambda i, j, n: (n // per, j, n % per))
    else:
        kdim = w.shape[0]
        w_spec = pl.BlockSpec((tj, tc), lambda i, j, n: (j, n))
    if outs is None:
        outs = [_out2d(m, kdim, F32, tm, tj)]
    return _mm(name, NT, g, g_spec, w, w_spec, (m // tm, kdim // tj, n_contract // tc),
               (tm, tj), outs, epilogue, extras, riders)


def _mm_bwd_w(name, a, g, g_spec, n, tk, tn, tm, sharded, riders=()):
    m, kdim = a.shape
    if sharded:
        per = (n // N_SHARD) // tn
        out = (jax.ShapeDtypeStruct((N_SHARD, kdim, n // N_SHARD), F32),
               pl.BlockSpec((None, tk, tn), lambda i, j, mm: (j // per, i, j % per)))
    else:
        out = (jax.ShapeDtypeStruct((kdim, n), F32), pl.BlockSpec((tk, tn), lambda i, j, mm: (i, j)))
    return _mm(name, TN, a, pl.BlockSpec((tm, tk), lambda i, j, mm: (mm, i)), g, g_spec,
               (kdim // tk, n // tn, m // tm), (tk, tn), [out], riders=riders)


def _ew(name, fn, tiles, fulls, outs, sums=(), ts=512, riders=()):
    tiles = [t if isinstance(t, tuple) else (t, pl.BlockSpec((ts, t.shape[1]), lambda i: (i, 0)))
             for t in tiles]
    s_rows = tiles[0][0].shape[-2]
    nt, nf, no = len(tiles), len(fulls), len(outs)

    def body(*refs):
        t_vals = [r[...] for r in refs[:nt]]
        f_vals = [r[...] for r in refs[nt:nt + nf]]
        o_refs, s_refs = refs[nt + nf:nt + nf + no], refs[nt + nf + no:]
        o_vals, s_vals = fn(t_vals, f_vals)
        for r, v in zip(o_refs, o_vals):
            r[...] = v.astype(r.dtype)
        for r, v in zip(s_refs, s_vals):
            part = jnp.sum(v, axis=0, keepdims=True)

            @pl.when(pl.program_id(0) == 0)
            def _():
                r[...] = part

            @pl.when(pl.program_id(0) > 0)
            def _():
                r[...] += part

    full_specs = [pl.BlockSpec(f.shape, lambda i, nd=f.ndim: (0,) * nd) for f in fulls]
    return _carry(
        riders, body, name=name, grid=(s_rows // ts,),
        in_specs=[s for _, s in tiles] + full_specs,
        out_specs=[pl.BlockSpec((ts, c), lambda i: (i, 0)) for c, _ in outs]
        + [pl.BlockSpec((1, c), lambda i: (0, 0)) for c in sums],
        out_shape=[jax.ShapeDtypeStruct((s_rows, c), dt) for c, dt in outs]
        + [jax.ShapeDtypeStruct((1, c), F32) for c in sums],
        semantics=("arbitrary",),
    )(*[t for t, _ in tiles], *fulls)


def _rms_fwd(name, x, g, riders=()):
    def fn(t, f):
        xv = t[0]
        r = lax.rsqrt(jnp.mean(xv * xv, axis=-1, keepdims=True) + EPS)
        return [xv * r * f[0]], []
    out = _ew(name, fn, [x], [g], [(x.shape[1], BF16)], riders=riders)
    return (out[0][0], out[1]) if riders else out[0]


def _rms_bwd(name, dh, x, g, dres, out_dtype=BF16, riders=()):
    def fn(t, f):
        dhv, xv, dr = [v.astype(F32) for v in t]
        r = lax.rsqrt(jnp.mean(xv * xv, axis=-1, keepdims=True) + EPS)
        xhat = xv * r
        u = dhv * f[0]
        dx = r * (u - xhat * jnp.mean(u * xhat, axis=-1, keepdims=True)) + dr
        return [dx], [dhv * xhat]
    c = x.shape[1]
    return _ew(name, fn, [dh, x, dres], [g], [(c, out_dtype)], sums=[c], riders=riders)


def _split3(x):
    x1 = x.astype(BF16)
    r1 = x - x1.astype(F32)
    x2 = r1.astype(BF16)
    x3 = (r1 - x2.astype(F32)).astype(BF16)
    return x1, x2, x3


def _rel_class(cp):
    far = (cp < MAX_REL) | (cp > BAND)
    return jnp.where(far, 2 * MAX_REL, BAND - cp)


def _skew_rows(x, sign):
    row = lax.broadcasted_iota(jnp.int32, x.shape, 0)
    for b in range(CHUNK.bit_length() - 1):
        shift = (1 << b) if sign > 0 else SKEW_W - (1 << b)
        x = jnp.where((row >> b) & 1 == 1, pltpu.roll(x, shift, 1), x)
    return x


def _roll_lanes(x, shift):
    return x if shift % SKEW_W == 0 else pltpu.roll(x, shift % SKEW_W, 1)


N_START = 3


def _bias_expand(rel_bias, riders=()):
    rel = jnp.pad(rel_bias, ((0, 0), (0, REL_PAD - N_REL))).reshape(N_HEADS, 1, REL_PAD)

    def body(rel_ref, o_ref):
        cls = lax.broadcasted_iota(jnp.int32, (REL_PAD, SKEW_W), 0)
        cp = lax.broadcasted_iota(jnp.int32, (REL_PAD, SKEW_W), 1)
        onehot = (cls == _rel_class(cp)).astype(BF16)
        rel8 = jnp.broadcast_to(rel_ref[...], (8, REL_PAD))
        trow = sum(jnp.dot(p, onehot, preferred_element_type=F32) for p in _split3(rel8))[0:1]
        first = _skew_rows(jnp.broadcast_to(trow, (CHUNK, SKEW_W)), +1)
        full = jnp.concatenate([_roll_lanes(first, CHUNK * g) for g in range(SUPER // CHUNK)], axis=0)[:, :BAND]
        qc = lax.broadcasted_iota(jnp.int32, (SUPER, BAND), 0) // CHUNK
        kc = lax.broadcasted_iota(jnp.int32, (SUPER, BAND), 1) // CHUNK
        on_band = (kc >= qc) & (kc <= qc + N_PREV_CHUNKS)
        table = jnp.where(on_band, full, NEG_INF).T
        key = lax.broadcasted_iota(jnp.int32, (BAND, SUPER), 0)
        for t in range(N_START):
            o_ref[t] = jnp.where(key < (N_START - 1 - t) * SUPER, NEG_INF, table)

    return _carry(
        riders, body, name="bias_expand", grid=(N_HEADS,),
        in_specs=[pl.BlockSpec((None, 1, REL_PAD), lambda h: (h, 0, 0))],
        out_specs=pl.BlockSpec((N_START, None, BAND, SUPER), lambda h: (0, h, 0, 0)),
        out_shape=jax.ShapeDtypeStruct((N_START, N_HEADS, BAND, SUPER), F32),
        semantics=("arbitrary",),
    )(rel)


def _bias_reduce(dbias, riders=()):
    def body(d_ref, o_ref):
        x = jnp.concatenate([d_ref[...].T, jnp.zeros((SUPER, SKEW_W - BAND), F32)], axis=1)
        folded = sum(_roll_lanes(x[CHUNK * g:CHUNK * (g + 1)], -CHUNK * g) for g in range(SUPER // CHUNK))
        diag = jnp.sum(_skew_rows(folded, -1), axis=0, keepdims=True)
        cp = lax.broadcasted_iota(jnp.int32, (SKEW_W, REL_PAD), 0)
        cls = lax.broadcasted_iota(jnp.int32, (SKEW_W, REL_PAD), 1)
        onehot = (cls == _rel_class(cp)).astype(BF16)
        diag8 = jnp.broadcast_to(diag, (8, SKEW_W))
        o_ref[...] = sum(jnp.dot(p, onehot, preferred_element_type=F32) for p in _split3(diag8))[0:1]

    out = _carry(
        riders, body, name="bias_reduce", grid=(N_HEADS,),
        in_specs=[pl.BlockSpec((None, BAND, SUPER), lambda h: (h, 0, 0))],
        out_specs=pl.BlockSpec((None, 1, REL_PAD), lambda h: (h, 0, 0)),
        out_shape=jax.ShapeDtypeStruct((N_HEADS, 1, REL_PAD), F32),
        semantics=("arbitrary",),
    )(dbias)
    out, carried = out if riders else (out, None)
    out = out.reshape(N_HEADS, REL_PAD)[:, :N_REL]
    return (out, carried) if riders else out


HEADS_PER_STEP = 8
HEAD_COLS = HEADS_PER_STEP * HEAD_DIM
N_HEAD_GROUPS = N_HEADS // HEADS_PER_STEP


def _scores_t(qs, kn, bias_t):
    return jnp.concatenate([lax.dot_general(k, qs, NT, preferred_element_type=F32) for k in kn], axis=0) + bias_t


def _bias_spec():
    return pl.BlockSpec((None, HEADS_PER_STEP, BAND, SUPER), lambda hg, i: (jnp.minimum(i, N_START - 1), hg, 0, 0))


def _band_specs(nb, col0, clamp_hi):
    def spec(d):
        def index(hg, i):
            blk = jnp.maximum(i - d, 0)
            if clamp_hi:
                blk = jnp.minimum(blk, nb - 1)
            return (blk, col0 + hg)
        return pl.BlockSpec((SUPER, HEAD_COLS), index)
    return [spec(2), spec(1), spec(0)]


def _head_sums(y):
    same_head = (lax.broadcasted_iota(jnp.int32, (MXU_DIM, MXU_DIM), 0) // HEAD_DIM
                 == lax.broadcasted_iota(jnp.int32, (MXU_DIM, MXU_DIM), 1) // HEAD_DIM).astype(BF16)
    sums = []
    for c0 in range(0, y.shape[1], MXU_DIM):
        chunk = y[:, c0:c0 + MXU_DIM]
        hi = chunk.astype(BF16)
        lo = (chunk - hi.astype(F32)).astype(BF16)
        sums.append(jnp.dot(hi, same_head, preferred_element_type=F32)
                    + jnp.dot(lo, same_head, preferred_element_type=F32))
    return jnp.concatenate(sums, axis=1)


def _head_unit(x):
    r = lax.rsqrt(_head_sums(x * x) * (1.0 / HEAD_DIM) + EPS)
    return x * r, r


def _head(hh):
    return slice(HEAD_DIM * hh, HEAD_DIM * (hh + 1))


def _key_block(j):
    return slice(SUPER * j, SUPER * (j + 1))


LSE_ROWS = 8


def _attn_fwd(qkn, v, bias, riders=()):
    s_len = qkn.shape[0]
    nb = s_len // SUPER

    def body(q_ref, k0, k1, k2, v0, v1, v2, b_ref, o_ref, lse_ref):
        outs = []
        v_t = [v0[...].T, v1[...].T, v2[...].T]

        def probabilities(hh):
            sl = _head(hh)
            s = _scores_t(q_ref[:, sl], [k0[:, sl], k1[:, sl], k2[:, sl]], b_ref[hh])
            m = jnp.max(s, axis=0, keepdims=True)
            e = jnp.exp(s - m)
            l = jnp.sum(e, axis=0, keepdims=True)
            lse_ref[hh:hh + 1, :] = m + jnp.log(l)
            return (e * (1.0 / l)).astype(BF16), sl

        def weighted_values(p, sl):
            outs.append(sum(jnp.dot(v_t[j][sl, :], p[_key_block(j), :], preferred_element_type=F32)
                            for j in range(3)))

        ready = probabilities(0)
        for hh in range(1, HEADS_PER_STEP):
            following = probabilities(hh)
            weighted_values(*ready)
            ready = following
        weighted_values(*ready)
        o_ref[...] = jnp.concatenate(outs, axis=0).T.astype(o_ref.dtype)

    return _carry(
        riders, body, name="attn_fwd", grid=(N_HEAD_GROUPS, nb),
        in_specs=[pl.BlockSpec((SUPER, HEAD_COLS), lambda hg, i: (i, hg))]
        + _band_specs(nb, N_HEAD_GROUPS, False) + _band_specs(nb, 0, False) + [_bias_spec()],
        out_specs=[pl.BlockSpec((SUPER, HEAD_COLS), lambda hg, i: (i, hg)),
                   pl.BlockSpec((None, LSE_ROWS, SUPER), lambda hg, i: (hg, 0, i))],
        out_shape=[jax.ShapeDtypeStruct((s_len, D_MODEL), BF16),
                   jax.ShapeDtypeStruct((N_HEAD_GROUPS, LSE_ROWS, s_len), F32)],
        semantics=("parallel", "arbitrary"),
    )(qkn, qkn, qkn, qkn, v, v, v, bias)


def _attn_bwd(qkn, v, out, d_out, bias, lse, riders=()):
    s_len = qkn.shape[0]
    nb = s_len // SUPER

    def body(q_ref, k0, k1, k2, v0, v1, v2, o_ref, do_ref, b_ref, lse_ref, dp_ref, db_ref, aq_ref, ak_ref, av_ref):
        i = pl.program_id(1)

        @pl.when(i == 0)
        def _():
            aq_ref[...] = jnp.zeros_like(aq_ref)
            ak_ref[...] = jnp.zeros_like(ak_ref)
            av_ref[...] = jnp.zeros_like(av_ref)
            db_ref[...] = jnp.zeros_like(db_ref)

        @pl.when(i < nb)
        def _():
            dq, dk, dv = [], [[], [], []], [[], [], []]
            ones = jnp.ones((8, HEAD_DIM), BF16)
            k_t = [k0[...].T, k1[...].T, k2[...].T]

            def softmax_grad(hh):
                sl = _head(hh)
                qs, do = q_ref[:, sl], do_ref[:, sl]
                kn = [k0[:, sl], k1[:, sl], k2[:, sl]]
                prod = do.astype(F32) * o_ref[:, sl].astype(F32)
                hi = prod.astype(BF16)
                lo = (prod - hi.astype(F32)).astype(BF16)
                delta = (lax.dot_general(ones, hi, NT, preferred_element_type=F32)
                         + lax.dot_general(ones, lo, NT, preferred_element_type=F32))[0:1]
                lse_row = lse_ref[hh:hh + 1, :]
                pb, dsb = [], []
                for j, (kj, vj) in enumerate(zip(kn, (v0, v1, v2))):
                    rows = _key_block(j)
                    p = jnp.exp(lax.dot_general(kj, qs, NT, preferred_element_type=F32) + b_ref[hh, rows, :] - lse_row)
                    ds = p * (lax.dot_general(vj[:, sl], do, NT, preferred_element_type=F32) - delta)
                    db_ref[hh, rows, :] += ds
                    pb.append(p.astype(BF16))
                    dsb.append(ds.astype(BF16))
                return pb, dsb, qs, do, sl

            def operand_grads(pb, dsb, qs, do, sl):
                dq.append(sum(jnp.dot(k_t[j][sl, :], dsb[j], preferred_element_type=F32) for j in range(3)))
                for j in range(3):
                    dv[j].append(jnp.dot(pb[j], do, preferred_element_type=F32))
                    dk[j].append(jnp.dot(dsb[j], qs, preferred_element_type=F32))

            ready = softmax_grad(0)
            for hh in range(1, HEADS_PER_STEP):
                following = softmax_grad(hh)
                operand_grads(*ready)
                ready = following
            operand_grads(*ready)
            aq_ref[i % 3] = jnp.concatenate(dq, axis=0).T
            for j in range(3):
                slot = (i + 1 + j) % 3
                if j < 2:
                    ak_ref[slot] += jnp.concatenate(dk[j], axis=1)
                    av_ref[slot] += jnp.concatenate(dv[j], axis=1)
                else:
                    ak_ref[slot] = jnp.concatenate(dk[j], axis=1)
                    av_ref[slot] = jnp.concatenate(dv[j], axis=1)

        slot = (i + 1) % 3
        dp_ref[0] = aq_ref[slot].astype(dp_ref.dtype)
        dp_ref[1] = ak_ref[slot].astype(dp_ref.dtype)
        dp_ref[2] = av_ref[slot].astype(dp_ref.dtype)

    def qrow(hg, i):
        return (jnp.minimum(i, nb - 1), hg)

    return _carry(
        riders, body, name="attn_bwd", grid=(N_HEAD_GROUPS, nb + 2),
        in_specs=[pl.BlockSpec((SUPER, HEAD_COLS), qrow)]
        + _band_specs(nb, N_HEAD_GROUPS, True) + _band_specs(nb, 0, True)
        + [pl.BlockSpec((SUPER, HEAD_COLS), qrow), pl.BlockSpec((SUPER, HEAD_COLS), qrow), _bias_spec(),
           pl.BlockSpec((None, LSE_ROWS, SUPER), lambda hg, i: (hg, 0, jnp.minimum(i, nb - 1)))],
        out_specs=[pl.BlockSpec((3, SUPER, HEAD_COLS), lambda hg, i: (0, jnp.maximum(i - 2, 0), hg)),
                   pl.BlockSpec((HEADS_PER_STEP, BAND, SUPER), lambda hg, i: (hg, 0, 0))],
        out_shape=[jax.ShapeDtypeStruct((6, s_len, D_MODEL), BF16),
                   jax.ShapeDtypeStruct((N_HEADS, BAND, SUPER), F32)],
        scratch_shapes=[pltpu.VMEM((3, SUPER, HEAD_COLS), F32)] * 3,
        semantics=("parallel", "arbitrary"),
    )(qkn, qkn, qkn, qkn, v, v, v, out, d_out, bias, lse)


def _qk_norm_bwd(dproj6, qk_raw, gq, gk):
    s_len = qk_raw.shape[0]
    ts = min(1024, s_len)

    nsteps = s_len // ts
    half = D_MODEL // 2

    def body(d_ref, raw_ref, gq_ref, gk_ref, o_ref, dgq_ref, dgk_ref, acc_ref):
        step = pl.program_id(0)

        @pl.when(step == 0)
        def _():
            acc_ref[...] = jnp.zeros_like(acc_ref)

        for piece, (g_ref, scale) in enumerate(((gq_ref, QK_SCALE), (gk_ref, 1.0))):
            for c0 in (0, half):
                xhat, r = _head_unit(raw_ref[:, piece * D_MODEL + c0:piece * D_MODEL + c0 + half].astype(F32))
                dn = d_ref[piece, :, c0:c0 + half].astype(F32) * scale
                u = dn * g_ref[...]
                dx = r * (u - xhat * (_head_sums(u * xhat) * (1.0 / HEAD_DIM)))
                o_ref[piece, :, c0:c0 + half] = dx.astype(o_ref.dtype)
                acc_ref[piece:piece + 1, c0:c0 + half] += jnp.sum(dn * xhat, axis=0, keepdims=True)

        @pl.when(step == nsteps - 1)
        def _():
            lane = lax.broadcasted_iota(jnp.int32, (D_MODEL, LANE), 0) % HEAD_DIM
            fold = (lane == lax.broadcasted_iota(jnp.int32, (D_MODEL, LANE), 1)).astype(BF16)
            tot = sum(jnp.dot(p, fold, preferred_element_type=F32) for p in _split3(acc_ref[...]))
            dgq_ref[...] = tot[0:1, :HEAD_DIM]
            dgk_ref[...] = tot[1:2, :HEAD_DIM]

    gain = pl.BlockSpec((1, half), lambda i: (0, 0))
    small = pl.BlockSpec((1, HEAD_DIM), lambda i: (0, 0))
    per_head = lambda g: jnp.tile(g, (1, half // HEAD_DIM))
    return pl.pallas_call(
        body, name="qk_norm_bwd", grid=(nsteps,),
        in_specs=[pl.BlockSpec((2, ts, D_MODEL), lambda i: (0, i, 0)),
                  pl.BlockSpec((ts, 2 * D_MODEL), lambda i: (i, 0)), gain, gain],
        out_specs=[pl.BlockSpec((2, ts, D_MODEL), lambda i: (0, i, 0)), small, small],
        out_shape=[jax.ShapeDtypeStruct(dproj6.shape, dproj6.dtype),
                   jax.ShapeDtypeStruct((1, HEAD_DIM), F32), jax.ShapeDtypeStruct((1, HEAD_DIM), F32)],
        scratch_shapes=[pltpu.VMEM((8, D_MODEL), F32)],
        input_output_aliases={0: 0},
        compiler_params=_params("arbitrary"),
    )(dproj6, qk_raw, per_head(gq), per_head(gk))


CONV_ROWS = 512
HALO = 16


def _rows_with_halo(ref, r0, n, front, s_len):
    zeros = jnp.zeros((HALO, ref.shape[1]), F32)
    if front:
        return (jnp.concatenate([zeros, ref[0:n, :].astype(F32)], axis=0) if r0 == 0
                else ref[r0 - HALO:r0 + n, :].astype(F32))
    return (jnp.concatenate([ref[r0:r0 + n, :].astype(F32), zeros], axis=0) if r0 + n == s_len
            else ref[r0:r0 + n + HALO, :].astype(F32))


def _earlier(ext, k):
    return pltpu.roll(ext, k, 0)[HALO:]


def _later(ext, k):
    n = ext.shape[0]
    return pltpu.roll(ext, n - k, 0)[:n - HALO]


def _conv_cols(col0):
    return lambda s_len: pl.BlockSpec((s_len, LANE), lambda j: (0, col0 + j))


def _conv_fwd(proj, conv_w, conv_b, riders=()):
    s_len = proj.shape[0]

    def body(bg_ref, cg_ref, xc_ref, w_ref, b_ref, o_ref):
        w = [w_ref[t:t + 1, :] for t in range(3)]
        for r0 in range(0, s_len, CONV_ROWS):
            u = _rows_with_halo(cg_ref, r0, CONV_ROWS, True, s_len) * \
                _rows_with_halo(xc_ref, r0, CONV_ROWS, True, s_len)
            conv = b_ref[...] + w[0] * _earlier(u, 2) + w[1] * _earlier(u, 1) + w[2] * u[HALO:]
            o_ref[r0:r0 + CONV_ROWS, :] = (bg_ref[r0:r0 + CONV_ROWS, :].astype(F32) * conv).astype(o_ref.dtype)

    return _carry(
        riders, body, name="conv_fwd", grid=(D_MODEL // LANE,),
        in_specs=[_conv_cols(0)(s_len), _conv_cols(8)(s_len), _conv_cols(16)(s_len),
                  pl.BlockSpec((3, LANE), lambda j: (0, j)), pl.BlockSpec((1, LANE), lambda j: (0, j))],
        out_specs=pl.BlockSpec((s_len, LANE), lambda j: (0, j)),
        out_shape=jax.ShapeDtypeStruct((s_len, D_MODEL), BF16),
        semantics=("parallel",),
    )(proj, proj, proj, conv_w, conv_b)


def _conv_bwd(dproj6, dy, proj, conv_w, conv_b, riders=()):
    s_len = proj.shape[0]

    def body(dy_ref, bg_ref, cg_ref, xc_ref, w_ref, b_ref, _, dp_ref, dw_ref):
        w = [w_ref[t:t + 1, :] for t in range(3)]
        acc = [jnp.zeros((1, LANE), F32) for _ in range(4)]
        for r0 in range(0, s_len, CONV_ROWS):
            rows = slice(r0, r0 + CONV_ROWS)
            u = _rows_with_halo(cg_ref, r0, CONV_ROWS, True, s_len) * \
                _rows_with_halo(xc_ref, r0, CONV_ROWS, True, s_len)
            u2, u1, u0 = _earlier(u, 2), _earlier(u, 1), u[HALO:]
            conv = b_ref[...] + w[0] * u2 + w[1] * u1 + w[2] * u0
            dp_ref[0, rows, :] = (dy_ref[rows, :].astype(F32) * conv).astype(dp_ref.dtype)
            dconv_ext = _rows_with_halo(dy_ref, r0, CONV_ROWS, False, s_len) * \
                _rows_with_halo(bg_ref, r0, CONV_ROWS, False, s_len)
            dconv = dconv_ext[:CONV_ROWS]
            for t, term in enumerate([dconv * u2, dconv * u1, dconv * u0, dconv]):
                acc[t] = acc[t] + jnp.sum(term, axis=0, keepdims=True)
            du = w[2] * dconv + w[1] * _later(dconv_ext, 1) + w[0] * _later(dconv_ext, 2)
            dp_ref[1, rows, :] = (du * xc_ref[rows, :].astype(F32)).astype(dp_ref.dtype)
            dp_ref[2, rows, :] = (du * cg_ref[rows, :].astype(F32)).astype(dp_ref.dtype)
        dw_ref[...] = jnp.zeros_like(dw_ref)
        for t in range(4):
            dw_ref[t:t + 1, :] = acc[t]

    return _carry(
        riders, body, name="conv_bwd", grid=(D_MODEL // LANE,),
        in_specs=[pl.BlockSpec((s_len, LANE), lambda j: (0, j)),
                  _conv_cols(0)(s_len), _conv_cols(8)(s_len), _conv_cols(16)(s_len),
                  pl.BlockSpec((3, LANE), lambda j: (0, j)), pl.BlockSpec((1, LANE), lambda j: (0, j)),
                  pl.BlockSpec(memory_space=pl.ANY)],
        out_specs=[pl.BlockSpec((3, s_len, LANE), lambda j: (1, 0, j)),
                   pl.BlockSpec((8, LANE), lambda j: (0, j))],
        out_shape=[jax.ShapeDtypeStruct(dproj6.shape, dproj6.dtype),
                   jax.ShapeDtypeStruct((8, D_MODEL), F32)],
        input_output_aliases={6: 0},
        semantics=("parallel",),
    )(dy, proj, proj, proj, conv_w, conv_b, dproj6)


def _d_input(name, terms, riders=()):
    s_len, kdim = terms[0][0].shape[1], terms[0][1].shape[1]
    tm, chunk = 512, 512
    n = len(terms)

    def body(*refs):
        o_ref = refs[-1]
        acc = jnp.zeros(o_ref.shape, F32)
        for p_ref, w_ref in zip(refs[:n], refs[n:2 * n]):
            n_piece, _, width = p_ref.shape
            per_shard = w_ref.shape[2]
            for c0 in range(0, n_piece * width, chunk):
                acc = acc + lax.dot_general(p_ref[c0 // width, :, c0 % width:c0 % width + chunk],
                                            w_ref[c0 // per_shard, :, c0 % per_shard:c0 % per_shard + chunk],
                                            NT, preferred_element_type=F32)
        o_ref[...] = acc.astype(o_ref.dtype)

    tile = pl.BlockSpec((tm, kdim), lambda i: (i, 0))
    return _carry(
        riders, body, name=name, grid=(s_len // tm,),
        in_specs=[pl.BlockSpec((p.shape[0], tm, p.shape[2]), lambda i: (0, i, 0)) for p, _ in terms]
        + [pl.BlockSpec(w.shape, lambda i: (0, 0, 0), pipeline_mode=pl.Buffered(1)) for _, w in terms],
        out_specs=tile, out_shape=jax.ShapeDtypeStruct((s_len, kdim), BF16),
        semantics=("parallel",),
    )(*[p for p, _ in terms], *[w for _, w in terms])


def _local_grads(x, target, norm1_g, q_norm_g, k_norm_g, rel_bias, conv_b, b_gate, norm2_g, comm):
    s_len = x.shape[0]
    tm = min(1024, s_len)
    tb = min(2048, s_len)
    row = lambda v: v.reshape(1, -1)

    def carrying(at, fn, *args, **kw):
        riders = comm.riders(at)
        out = fn(*args, riders=riders, **kw)
        if riders:
            out, carried = out
            comm.done(at, carried)
        return out

    bias = carrying("bias_expand", _bias_expand, rel_bias)
    h = carrying("norm1", _rms_fwd, "norm1", x, row(norm1_g))
    w_in3, conv_w = comm.early_weights()
    gq, gk = row(q_norm_g), row(k_norm_g)

    def qk_epi(acc, e, o):
        o[0][...] = acc.astype(BF16)
        gain = jnp.where(pl.program_id(1) < 2, e[0] * QK_SCALE, e[1])
        o[1][...] = (_head_unit(acc)[0] * gain).astype(BF16)
    small = pl.BlockSpec((1, 512), lambda i, j, k: (0, 0))
    per_head = lambda g: jnp.tile(g, (1, 512 // HEAD_DIM))
    qk_raw, qkn = carrying("proj_qk", _mm_fwd, "proj_qk", h, w_in3, tm, 512, D_MODEL, ncols=2 * D_MODEL,
                           epilogue=qk_epi, outs=[_out2d(s_len, 2 * D_MODEL, BF16, tm, 512)] * 2,
                           extras=[(per_head(gq), small), (per_head(gk), small)])
    v = carrying("proj_v", _mm_fwd, "proj_v", h, w_in3, tb, 512, D_MODEL, col0=4, ncols=D_MODEL,
                 outs=[_out2d(s_len, D_MODEL, BF16, tb, 512)])
    w_gate3 = comm.gate_weight()
    conv_in = carrying("proj_conv", _mm_fwd, "proj_conv", h, w_in3, tb, 1536, D_MODEL, col0=2, ncols=3 * D_MODEL,
                       outs=[_out2d(s_len, 3 * D_MODEL, BF16, tb, 1536)])

    def gate_epi(acc, e, o):
        o[0][...] = jax.nn.sigmoid(acc + e[0]).astype(BF16)
    gates = carrying("gates", _mm_fwd, "gates", h, w_gate3, tb, 512, D_MODEL, epilogue=gate_epi,
                     outs=[_out2d(s_len, 2 * D_MODEL, BF16, tb, 512)],
                     extras=[(row(b_gate), pl.BlockSpec((1, 512), lambda i, j, k: (0, j)))])

    attn, lse = carrying("attn_fwd", _attn_fwd, qkn, v, bias)
    yconv = carrying("conv_fwd", _conv_fwd, conv_in, conv_w, row(conv_b))
    w_ap, w_cp, w_out, w_up3, w_down = comm.late_weights()
    tw = 1024
    ya = _mm_fwd("attn_proj", attn, w_ap, tm, tw, D_MODEL, outs=[_out2d(s_len, D_MODEL, BF16, tm, tw)])

    def merge_epi(acc, e, o):
        ya_v, ga, gc = [t.astype(F32) for t in e]
        o[0][...] = acc.astype(BF16)
        o[1][...] = (ga * ya_v + gc * acc).astype(BF16)
    gate_a, gate_c = (gates, _tile_spec(tm, tw, 0)), (gates, _tile_spec(tm, tw, 1))
    yc, merged = _mm_fwd("conv_proj", yconv, w_cp, tm, tw, D_MODEL, epilogue=merge_epi,
                         outs=[_out2d(s_len, D_MODEL, BF16, tm, tw), _out2d(s_len, D_MODEL, BF16, tm, tw)],
                         extras=[(ya, _tile_spec(tm, tw)), gate_a, gate_c])

    def res_epi(acc, e, o):
        o[0][...] = e[0] + acc
    x1 = _mm_fwd("out_proj", merged, w_out, tm, tw, D_MODEL, epilogue=res_epi,
                 extras=[(x, _tile_spec(tm, tw))])
    h2 = _rms_fwd("norm2", x1, row(norm2_g))

    def up_epi(acc, e, o):
        o[0][...] = jnp.square(jnp.maximum(acc, 0.0)).astype(BF16)
    act = _mm_fwd("mlp_up", h2, w_up3, tb, tw, D_MODEL, epilogue=up_epi, outs=[_out2d(s_len, D_FF, BF16, tb, tw)])

    def loss_epi(acc, e, o):
        err = e[0] + acc - e[1]
        o[0][...] = (err * (1.0 / D_MODEL)).astype(BF16)
        sq = err * err
        part = sq[:, 0:LANE]
        for c0 in range(LANE, D_MODEL, LANE):
            part = part + sq[:, c0:c0 + LANE]
        o[1][...] = jnp.sum(part.reshape(tl // 8, 8, LANE), axis=0)
    tl = 512
    dy_b, loss_part = _mm_fwd(
        "mlp_down", act, w_down, tl, D_MODEL, D_FF, epilogue=loss_epi,
        outs=[_out2d(s_len, D_MODEL, BF16, tl, D_MODEL),
              (jax.ShapeDtypeStruct((8 * (s_len // tl), LANE), F32), pl.BlockSpec((8, LANE), lambda i, j, k: (i, 0)))],
        extras=[(x1, _tile_spec(tl, D_MODEL)), (target, _tile_spec(tl, D_MODEL))])

    def dup_epi(acc, e, o):
        o[0][...] = (acc * (2.0 * jnp.sqrt(e[0].astype(F32)))).astype(BF16)
    full = lambda cols: pl.BlockSpec((tm, cols), lambda i, j, n: (i, n))
    tokens = lambda cols: pl.BlockSpec((s_len, cols), lambda i, j, m: (m, j))
    dup = _mm_bwd_x("d_act", dy_b, full(D_MODEL), w_down, tm, tw, D_MODEL, s_len, D_MODEL, epilogue=dup_epi,
                    outs=[_out2d(s_len, D_FF, BF16, tm, tw)], extras=[(act, _tile_spec(tm, tw))])
    g_down = _mm_bwd_w("g_down", act, dy_b, tokens(D_MODEL), D_MODEL, 512, D_MODEL, s_len, False)
    g_up = _mm_bwd_w("g_up", h2, dup, tokens(512), D_FF, D_MODEL, 512, s_len, True)
    comm.grads_ready("mlp", dict(w_down=g_down, w_up=g_up))
    dh2 = carrying("d_h2", _d_input, "d_h2", [(dup.reshape(1, s_len, D_FF), w_up3)])
    dx1_b, dg2 = _rms_bwd("norm2_bwd", dh2, x1, row(norm2_g), dy_b)

    def dmerge_epi(acc, e, o):
        ya_v, yc_v, ga, gc = [t.astype(F32) for t in e]
        o[0][...] = (acc * ga).astype(BF16)
        o[1][...] = (acc * gc).astype(BF16)
        o[2][0] = (acc * ya_v * ga * (1.0 - ga)).astype(BF16)
        o[2][1] = (acc * yc_v * gc * (1.0 - gc)).astype(BF16)
    dya, dyc, dgp2 = _mm_bwd_x(
        "d_merged", dx1_b, full(D_MODEL), w_out, tm, tw, D_MODEL, s_len, D_MODEL, epilogue=dmerge_epi,
        outs=[_out2d(s_len, D_MODEL, BF16, tm, tw), _out2d(s_len, D_MODEL, BF16, tm, tw),
              (jax.ShapeDtypeStruct((2, s_len, D_MODEL), BF16), pl.BlockSpec((2, tm, tw), lambda i, j, n: (0, i, j)))],
        extras=[(ya, _tile_spec(tm, tw)), (yc, _tile_spec(tm, tw)), gate_a, gate_c])
    g_out = _mm_bwd_w("g_out", merged, dx1_b, tokens(512), D_MODEL, D_MODEL, 512, s_len, False)
    d_attn = _mm_bwd_x("d_attn", dya, full(D_MODEL), w_ap, tm, tw, D_MODEL, s_len, D_MODEL,
                       outs=[_out2d(s_len, D_MODEL, BF16, tm, tw)])
    g_ap = _mm_bwd_w("g_attn_proj", attn, dya, tokens(512), D_MODEL, D_MODEL, 512, s_len, False)
    d_yconv = _mm_bwd_x("d_yconv", dyc, full(D_MODEL), w_cp, tm, tw, D_MODEL, s_len, D_MODEL,
                        outs=[_out2d(s_len, D_MODEL, BF16, tm, tw)])
    g_cp = _mm_bwd_w("g_conv_proj", yconv, dyc, tokens(512), D_MODEL, D_MODEL, 512, s_len, False)
    piece = lambda width: (lambda blk: (blk * width) // D_MODEL, lambda blk: (blk * width % D_MODEL) // width)
    pc, cb = piece(512)
    pieces = pl.BlockSpec((None, s_len, 512), lambda i, j, m: (pc(j), m, cb(j)))
    g_gate = _mm_bwd_w("g_gate", h, dgp2, pieces, 2 * D_MODEL, D_MODEL, 512, s_len, True)
    comm.grads_ready("proj", dict(w_out=g_out, w_attn_proj=g_ap, w_conv_proj=g_cp, w_gate=g_gate))

    dproj6, dbias = carrying("attn_bwd", _attn_bwd, qkn, v, attn, d_attn, bias, lse)
    dproj6, dgq, dgk = _qk_norm_bwd(dproj6, qk_raw, gq, gk)
    dproj6, dconv_wb = carrying("conv_bwd", _conv_bwd, dproj6, d_yconv, conv_in, conv_w, row(conv_b))

    g_in = carrying("g_in", _mm_bwd_w, "g_in", h, dproj6, pieces, 6 * D_MODEL, D_MODEL, 512, s_len, True)
    comm.grads_ready("in", dict(w_in=g_in))
    d_rel = carrying("bias_reduce", _bias_reduce, dbias)
    dh = carrying("d_h", _d_input, "d_h", [(dproj6, w_in3), (dgp2, w_gate3)])
    grad_x, dg1 = carrying("norm1_bwd", _rms_bwd, "norm1_bwd", dh, x, row(norm1_g), dx1_b, out_dtype=F32)

    def bsum(t, f):
        return [], [t[0].astype(F32), t[1].astype(F32)]
    ts = 512
    db_a, db_c = carrying("b_gate_sum", _ew, "b_gate_sum", bsum,
                          [(dgp2, pl.BlockSpec((None, ts, D_MODEL), lambda i: (0, i, 0))),
                           (dgp2, pl.BlockSpec((None, ts, D_MODEL), lambda i: (1, i, 0)))],
                          [], [], sums=[D_MODEL, D_MODEL], ts=ts)

    big = dict(w_in=g_in, w_attn_proj=g_ap, w_conv_proj=g_cp, w_gate=g_gate, w_out=g_out,
               w_up=g_up, w_down=g_down)
    small = dict(norm1_g=dg1, norm2_g=dg2, conv_wb=dconv_wb, b_gate=(db_a, db_c),
                 q_norm_g=dgq, k_norm_g=dgk, rel_bias=d_rel)
    return loss_part, grad_x, big, small


def _finish_loss(loss_part):
    def body(l_ref, lo_ref):
        total = jnp.sum(jnp.sum(l_ref[...], axis=0, keepdims=True), axis=1, keepdims=True)
        lo_ref[...] = jnp.broadcast_to(total * (0.5 / D_MODEL), lo_ref.shape)

    return pl.pallas_call(body, name="finish_loss", out_shape=jax.ShapeDtypeStruct((8, LANE), F32))(loss_part)


def _place():
    return lax.axis_index("x"), lax.axis_index("y"), lax.axis_index("c")


def _other_chips(x, y):
    return [(1 - x, y), (x, 1 - y), (1 - x, 1 - y)]


def _cast_into_slot(name, where, w):
    r, cols = w.shape
    ts = 256

    def body(w_ref, x_ref, o_ref):
        o_ref[...] = x_ref[...].astype(o_ref.dtype)

    return pl.pallas_call(
        body, name=name,
        grid_spec=pltpu.PrefetchScalarGridSpec(
            num_scalar_prefetch=1, grid=(r // ts,),
            in_specs=[pl.BlockSpec((ts, cols), lambda i, w: (i, 0))],
            out_specs=pl.BlockSpec((None, ts, cols), lambda i, w: (w[0], i, 0))),
        out_shape=jax.ShapeDtypeStruct((N_SHARD, r, cols), BF16),
        compiler_params=_params("parallel"),
    )(where, w)


def _remote(src, dst, send, recv, k, to):
    return pltpu.make_async_remote_copy(src_ref=src, dst_ref=dst, send_sem=send.at[k], recv_sem=recv.at[k],
                                        device_id=to, device_id_type=MESH)


def _gather_riders(slots):
    n = len(slots)

    def copy(refs, w, j, shard, which, send, recv, to):
        hr = slots[w].shape[1] // 2
        ref = refs[w].at[shard, pl.ds(which * hr, hr)]
        return _remote(ref, ref, send, recv, 3 * w + j, to)

    def for_each_peer(fn):
        x, y, c = _place()
        for w in range(n):
            for j, chip in enumerate(_other_chips(x, y)):
                fn(w, j, 2 * x + y, 2 * chip[0] + chip[1], c, (*chip, c), (x, y, 1 - c))

    def chips_start(_, refs, send, recv):
        for_each_peer(lambda w, j, mine, theirs, c, peer, sib: copy(refs, w, j, mine, c, send, recv, peer).start())

    def chips_finish(_, refs, send, recv):
        for_each_peer(lambda w, j, mine, theirs, c, peer, sib: copy(refs, w, j, theirs, c, send, recv, peer).wait_recv())
        for_each_peer(lambda w, j, mine, theirs, c, peer, sib: copy(refs, w, j, mine, c, send, recv, peer).wait_send())

    def sibling_start(_, refs, send, recv):
        for_each_peer(lambda w, j, mine, theirs, c, peer, sib: copy(refs, w, j, theirs, c, send, recv, sib).start())

    def sibling_finish(_, refs, send, recv):
        for_each_peer(lambda w, j, mine, theirs, c, peer, sib: copy(refs, w, j, theirs, 1 - c, send, recv, sib).wait_recv())
        for_each_peer(lambda w, j, mine, theirs, c, peer, sib: copy(refs, w, j, theirs, c, send, recv, sib).wait_send())

    return (lambda arrays: _Rider([], arrays, 3 * n, chips_start, chips_finish),
            lambda arrays: _Rider([], arrays, 3 * n, sibling_start, sibling_finish))


def _pair_exchange_rider(grads, landing):
    n = len(grads)

    def copies(srcs, dsts, send, recv):
        x, y, c = _place()
        out = []
        for w in range(n):
            hr = grads[w].shape[1] // 2
            out.append(_remote(srcs[w].at[:, pl.ds((1 - c) * hr, hr)], dsts[w], send, recv, w, (x, y, 1 - c)))
        return out

    def start(srcs, dsts, send, recv):
        for cp in copies(srcs, dsts, send, recv):
            cp.start()

    def finish(srcs, dsts, send, recv):
        for cp in copies(srcs, dsts, send, recv):
            cp.wait()

    return _Rider(grads, landing, n, start, finish)


def _row_tile(hr):
    return min(hr, 256)


def _pair_add(name, where, grad, got):
    _, hr, cols = got.shape
    tr = _row_tile(hr)
    nblk = hr // tr

    def body(w_ref, g_ref, r_ref, o_ref):
        o_ref[...] = (g_ref[...] + r_ref[...]).astype(o_ref.dtype)

    other = lambda k, w: (w[0] + 1 + k) % N_SHARD
    return pl.pallas_call(
        body, name=name,
        grid_spec=pltpu.PrefetchScalarGridSpec(
            num_scalar_prefetch=1, grid=(N_SHARD - 1, nblk),
            in_specs=[pl.BlockSpec((None, tr, cols), lambda k, i, w: (other(k, w), w[1] * nblk + i, 0)),
                      pl.BlockSpec((None, tr, cols), lambda k, i, w: (other(k, w), i, 0))],
            out_specs=pl.BlockSpec((None, tr, cols), lambda k, i, w: (other(k, w), i, 0))),
        out_shape=jax.ShapeDtypeStruct(got.shape, BF16),
        compiler_params=_params("parallel", "parallel"),
    )(where, grad, got)


def _chip_exchange_rider(partials, landing):
    n = len(partials)

    def copies(srcs, dsts, send, recv):
        x, y, c = _place()
        return [_remote(srcs[w].at[2 * chip[0] + chip[1]], dsts[w].at[j], send, recv, 3 * w + j, (*chip, c))
                for w in range(n) for j, chip in enumerate(_other_chips(x, y))]

    def start(srcs, dsts, send, recv):
        for cp in copies(srcs, dsts, send, recv):
            cp.start()

    def finish(srcs, dsts, send, recv):
        for cp in copies(srcs, dsts, send, recv):
            cp.wait()

    return _Rider(partials, landing, 3 * n, start, finish)


def _final_add(name, where, grad, got, arrived):
    _, hr, cols = got.shape
    tr = _row_tile(hr)
    nblk = hr // tr

    def body(w_ref, g_ref, r_ref, a_ref, o_ref):
        acc = g_ref[...] + r_ref[...]
        for j in range(3):
            acc = acc + a_ref[j].astype(F32)
        o_ref[...] = acc

    return pl.pallas_call(
        body, name=name,
        grid_spec=pltpu.PrefetchScalarGridSpec(
            num_scalar_prefetch=1, grid=(nblk,),
            in_specs=[pl.BlockSpec((None, tr, cols), lambda i, w: (w[0], w[1] * nblk + i, 0)),
                      pl.BlockSpec((None, tr, cols), lambda i, w: (w[0], i, 0)),
                      pl.BlockSpec((3, tr, cols), lambda i, w: (0, i, 0))],
            out_specs=pl.BlockSpec((tr, cols), lambda i, w: (w[1] * nblk + i, 0))),
        out_shape=jax.ShapeDtypeStruct((2 * hr, cols), F32),
        compiler_params=_params("parallel"),
    )(where, grad, got, arrived)


def _pair_share_rider(shards):
    n = len(shards)

    def half(refs, w, which):
        hr = shards[w].shape[0] // 2
        return refs[w].at[pl.ds(which * hr, hr)]

    def start(_, refs, send, recv):
        x, y, c = _place()
        for w in range(n):
            _remote(half(refs, w, c), half(refs, w, c), send, recv, w, (x, y, 1 - c)).start()

    def finish(_, refs, send, recv):
        x, y, c = _place()
        for w in range(n):
            _remote(half(refs, w, 1 - c), half(refs, w, 1 - c), send, recv, w, (x, y, 1 - c)).wait_recv()
        for w in range(n):
            _remote(half(refs, w, c), half(refs, w, c), send, recv, w, (x, y, 1 - c)).wait_send()

    return _Rider([], shards, n, start, finish)


class _Exchange:
    PLAN = {"bias_expand": [("early", "gather")], "norm1": [("early", "forward")],
            "proj_qk": [("gate", "gather")], "proj_v": [("gate", "forward")],
            "proj_conv": [("late", "gather")], "gates": [("late", "forward")],
            "attn_fwd": [("mlp_w", "gather")], "conv_fwd": [("mlp_w", "forward")],
            "d_h2": [("mlp", "pair")], "attn_bwd": [("mlp", "chips"), ("proj", "pair")], "conv_bwd": [("mlp", "share")],
            "g_in": [("proj", "chips")], "bias_reduce": [("proj", "share"), ("in", "pair")],
            "d_h": [("in", "chips")], "b_gate_sum": [("in", "share")]}

    def __init__(self, where, early_slots, gate_slots, late_slots):
        self.where = where
        self.slots = dict(early=early_slots, gate=gate_slots, late=late_slots[:3], mlp_w=late_slots[3:])
        self.stage = {g: dict(zip(("gather", "forward"), _gather_riders(s))) for g, s in self.slots.items()}
        self.groups, self.reduced, self.pending = {}, {}, []

    def early_weights(self):
        w_in3, small = self.slots["early"]
        conv_w = small[:, :3, :].transpose(1, 0, 2).reshape(3, N_SHARD * small.shape[2])
        return w_in3, conv_w

    def gate_weight(self):
        return self.slots["gate"][0]

    def late_weights(self):
        rows = lambda a: a.reshape(a.shape[0] * a.shape[1], a.shape[2])
        w_ap, w_cp, w_out = self.slots["late"]
        w_up3, w_down = self.slots["mlp_w"]
        return rows(w_ap), rows(w_cp), rows(w_out), w_up3, rows(w_down)

    def grads_ready(self, group, grads):
        names = list(grads)
        g4 = [g if g.ndim == 3 else g.reshape(N_SHARD, -1, g.shape[1]) for g in grads.values()]
        self.groups[group] = dict(names=names, g4=g4)

    def riders(self, at):
        self.pending = self.PLAN.get(at, [])
        out = []
        for group, stage in self.pending:
            if group in self.slots:
                out.append(self.stage[group][stage](self.slots[group]))
                continue
            st = self.groups[group]
            if stage == "pair":
                landing = [lax.empty((N_SHARD, g.shape[1] // 2, g.shape[2]), F32) for g in st["g4"]]
                out.append(_pair_exchange_rider(st["g4"], landing))
            elif stage == "chips":
                landing = [lax.empty((3,) + p.shape[1:], p.dtype) for p in st["partial"]]
                out.append(_chip_exchange_rider(st["partial"], landing))
            else:
                out.append(_pair_share_rider(st["halves"]))
        return out

    def done(self, at, carried):
        for (group, stage), arrays in zip(self.pending, carried):
            if group in self.slots:
                self.slots[group] = arrays
                continue
            st = self.groups[group]
            tag = lambda what, n: what + "_" + n
            if stage == "pair":
                st["got"] = arrays
                st["partial"] = [_pair_add(tag("pair_add", n), self.where, g, r)
                                 for n, g, r in zip(st["names"], st["g4"], arrays)]
            elif stage == "chips":
                st["halves"] = [_final_add(tag("final_add", n), self.where, g, r, a)
                                for n, g, r, a in zip(st["names"], st["g4"], st["got"], arrays)]
            else:
                self.reduced.update(zip(st["names"], arrays))


SMALL_ROWS = 32
N_DEV = 8


def _all_reduce_small(pack):
    def body(p_ref, o_ref, buf, send, recv):
        x, y, c = _place()
        buf[4 * x + 2 * y + c] = p_ref[...]
        copies, waits = [], []
        for k in range(1, N_DEV):
            px = 1 - x if k & 4 else x
            py = 1 - y if k & 2 else y
            pc = 1 - c if k & 1 else c
            copies.append(_remote(p_ref, buf.at[4 * x + 2 * y + c], send, recv, k - 1, (px, py, pc)))
            waits.append(_remote(p_ref, buf.at[4 * px + 2 * py + pc], send, recv, k - 1, (px, py, pc)))
        for cp in copies:
            cp.start()
        for cp in waits:
            cp.wait_recv()
        acc = buf[0]
        for d in range(1, N_DEV):
            acc = acc + buf[d]
        o_ref[...] = acc
        for cp in copies:
            cp.wait_send()

    return pl.pallas_call(
        body, name="all_reduce_small",
        out_shape=jax.ShapeDtypeStruct(pack.shape, F32),
        scratch_shapes=[pltpu.VMEM((N_DEV,) + pack.shape, F32),
                        pltpu.SemaphoreType.DMA((N_DEV - 1,)), pltpu.SemaphoreType.DMA((N_DEV - 1,))],
    )(pack)


def _adamw(name, w, g, m, v, riders=()):
    c1 = 1.0 - ADAM_B1 ** ADAM_STEP
    c2 = 1.0 - ADAM_B2 ** ADAM_STEP

    def fn(t, f):
        wv, gv, mv, vv = t
        m2 = ADAM_B1 * mv + (1.0 - ADAM_B1) * gv
        v2 = ADAM_B2 * vv + (1.0 - ADAM_B2) * (gv * gv)
        delta = -ADAM_LR * ((m2 / c1) / (jnp.sqrt(v2 / c2) + ADAM_EPS) + ADAM_WD * wv)
        return [delta, m2, v2], []

    cols = w.shape[1]
    return _ew(name, fn, [w, g, m, v], [], [(cols, F32)] * 3, ts=min(w.shape[0], 256), riders=riders)


LOSS_ROW = 26


def _pack_small(norm1_g, norm2_g, conv_b, b_gate, conv_w, q_norm_g, k_norm_g, rel_bias, loss=None):
    pack = jnp.zeros((SMALL_ROWS, D_MODEL), F32)
    for r0, v in ((0, norm1_g), (1, norm2_g), (2, conv_b), (3, b_gate.reshape(2, D_MODEL)), (5, conv_w),
                  (8, q_norm_g), (9, k_norm_g), (10, rel_bias)) + (((LOSS_ROW, loss),) if loss is not None else ()):
        v = v.reshape(-1, v.shape[-1]).astype(F32)
        pack = pack.at[r0:r0 + v.shape[0], :v.shape[1]].set(v)
    return pack


def _unpack_small(pack, conv_cols):
    return dict(norm1_g=pack[0], norm2_g=pack[1], conv_b=pack[2], b_gate=pack[3:5].reshape(2 * D_MODEL),
                conv_w=pack[5:8, :conv_cols], q_norm_g=pack[8, :HEAD_DIM], k_norm_g=pack[9, :HEAD_DIM],
                rel_bias=pack[10:10 + N_HEADS, :N_REL])


BIG = ["w_in", "w_attn_proj", "w_conv_proj", "w_gate", "w_out", "w_up", "w_down"]
LATE = ["w_attn_proj", "w_conv_proj", "w_out", "w_up", "w_down"]
WEIGHTS = ["norm1_g", "w_in", "q_norm_g", "k_norm_g", "rel_bias", "conv_w", "conv_b", "w_attn_proj",
           "w_conv_proj", "w_gate", "b_gate", "w_out", "norm2_g", "w_up", "w_down"]


def kernel(x, norm1_g, w_in, q_norm_g, k_norm_g, rel_bias, conv_w, conv_b, w_attn_proj, w_conv_proj, w_gate, b_gate, w_out, norm2_g, w_up, w_down, loss_target, m_norm1_g, m_w_in, m_q_norm_g, m_k_norm_g, m_rel_bias, m_conv_w, m_conv_b, m_w_attn_proj, m_w_conv_proj, m_w_gate, m_b_gate, m_w_out, m_norm2_g, m_w_up, m_w_down, v_norm1_g, v_w_in, v_q_norm_g, v_k_norm_g, v_rel_bias, v_conv_w, v_conv_b, v_w_attn_proj, v_w_conv_proj, v_w_gate, v_b_gate, v_w_out, v_norm2_g, v_w_up, v_w_down):
    given = dict(locals())
    w = {n: given[n] for n in WEIGHTS}
    m = {n: given["m_" + n] for n in WEIGHTS}
    v = {n: given["v_" + n] for n in WEIGHTS}
    s_len = x.shape[1]
    shard = 2 * lax.axis_index("x") + lax.axis_index("y")
    where = jnp.stack([shard, lax.axis_index("c")]).astype(jnp.int32)
    conv_cols = conv_w.shape[1]

    small_in = lax.dynamic_update_slice(jnp.zeros((N_SHARD, 16, conv_cols), F32), conv_w[None], (shard, 0, 0))
    slot = {n: _cast_into_slot("cast_" + n, where, w[n]) for n in BIG}
    comm = _Exchange(where, [slot["w_in"], small_in], [slot["w_gate"]], [slot[n] for n in LATE])

    loss_part, grad_x, _, small = _local_grads(
        x.reshape(s_len, D_MODEL), loss_target.reshape(s_len, D_MODEL), norm1_g, q_norm_g, k_norm_g,
        rel_bias, conv_b, b_gate, norm2_g, comm)
    grad = dict(comm.reduced)

    loss_local = _finish_loss(loss_part)
    pack = _pack_small(small["norm1_g"], small["norm2_g"], small["conv_wb"][3], jnp.concatenate(small["b_gate"], axis=1),
                       small["conv_wb"][0:3], small["q_norm_g"], small["k_norm_g"], small["rel_bias"],
                       loss=loss_local[0:1, :])
    total = _all_reduce_small(pack)
    g_small = _unpack_small(total, D_MODEL)
    g_small["conv_w"] = lax.dynamic_slice(g_small["conv_w"], (0, shard * conv_cols), (3, conv_cols))
    grad.update(g_small)

    delta, new_m, new_v = {}, {}, {}
    for n in BIG:
        delta[n], new_m[n], new_v[n] = _adamw("adamw_" + n, w[n], grad[n], m[n], v[n])
    small_names = [n for n in WEIGHTS if n not in BIG]
    packs = [_pack_small(**{n: src[n] for n in small_names}) for src in (w, grad, m, v)]
    for out, packed in zip((delta, new_m, new_v), _adamw("adamw_small", *packs)):
        out.update({n: a.reshape(w[n].shape) for n, a in _unpack_small(packed, conv_cols).items()})

    outs = [total[LOSS_ROW, 0], grad_x.reshape(x.shape)]
    for group in (grad, delta, new_m, new_v):
        outs += [group[n].reshape(w[n].shape) for n in WEIGHTS]
    return tuple(outs)
```

```python
import jax
import jax.numpy as jnp
from jax import lax
from jax.experimental import pallas as pl
from jax.experimental.pallas import tpu as pltpu

F32 = jnp.float32
BF16 = jnp.bfloat16

D_MODEL = 1024
N_HEADS = 16
HEAD_DIM = 64
CHUNK = 64
N_PREV_CHUNKS = 8
MAX_REL = 256
D_FF = 4096
N_REL = 2 * MAX_REL + 1
REL_PAD = 640
EPS = 1e-6
NEG_INF = -1e30
QK_SCALE = HEAD_DIM ** -0.5

SUPER = 4 * CHUNK
BAND = SUPER + N_PREV_CHUNKS * CHUNK
SKEW_W = 1024
N_SHARD = 4
LANE = 128
MXU_DIM = 256
VMEM_LIMIT = 48 * 1024 * 1024

ADAM_LR = 0.001
ADAM_B1 = 0.9
ADAM_B2 = 0.999
ADAM_EPS = 1e-08
ADAM_WD = 0.01
ADAM_STEP = 10

MESH = pl.DeviceIdType.MESH
NN = (((1,), (0,)), ((), ()))
NT = (((1,), (1,)), ((), ()))
TN = (((0,), (0,)), ((), ()))


def _params(*sem):
    return pltpu.CompilerParams(dimension_semantics=sem or None, vmem_limit_bytes=VMEM_LIMIT)


HBM_SPEC = pl.BlockSpec(memory_space=pl.ANY)


class _Rider:
    def __init__(self, sources, arrays, n_sem, start, finish):
        self.sources, self.arrays, self.n_sem, self.start, self.finish = sources, arrays, n_sem, start, finish


def _carry(riders, body, *, name, out_shape, grid=(), in_specs=None, out_specs=None, scratch_shapes=(),
           semantics=(), input_output_aliases=None):
    aliases = dict(input_output_aliases or {})
    if not riders:
        kw = {} if in_specs is None else dict(in_specs=in_specs, out_specs=out_specs)
        return pl.pallas_call(body, name=name, grid=grid, out_shape=out_shape, scratch_shapes=scratch_shapes,
                              input_output_aliases=aliases, compiler_params=_params(*semantics), **kw)
    single = not isinstance(out_shape, (list, tuple))
    shapes = [out_shape] if single else list(out_shape)
    n_out, n_scr = len(shapes), len(scratch_shapes)
    in_hbm = lambda a: pltpu.with_memory_space_constraint(a, pltpu.HBM)
    srcs = [in_hbm(a) for r in riders for a in r.sources]
    arrs = [in_hbm(a) for r in riders for a in r.arrays]
    vmem = pl.BlockSpec(memory_space=pltpu.VMEM)

    def run(*args):
        n_in = len(args)

        def wrapped(*refs):
            pos = n_in
            src_refs = refs[pos:pos + len(srcs)]
            pos += len(srcs) + len(arrs)
            outs = refs[pos:pos + n_out]
            pos += n_out
            arr_refs = refs[pos:pos + len(arrs)]
            pos += len(arrs)
            scratch = refs[pos:pos + n_scr]
            sems = refs[pos + n_scr:]
            first, last = True, True
            for d, size in enumerate(grid):
                first = jnp.logical_and(first, pl.program_id(d) == 0)
                last = jnp.logical_and(last, pl.program_id(d) == size - 1)

            def each(method):
                s0 = a0 = 0
                for k, r in enumerate(riders):
                    getattr(r, method)(src_refs[s0:s0 + len(r.sources)], arr_refs[a0:a0 + len(r.arrays)],
                                       sems[2 * k], sems[2 * k + 1])
                    s0, a0 = s0 + len(r.sources), a0 + len(r.arrays)

            pl.when(first)(lambda: each("start"))
            body(*refs[:n_in], *outs, *scratch)
            pl.when(last)(lambda: each("finish"))

        ins = [vmem] * n_in if in_specs is None else list(in_specs)
        if out_specs is None:
            o_specs = [vmem] * n_out
        else:
            o_specs = [out_specs] if single else list(out_specs)
        for k in range(len(arrs)):
            aliases[n_in + len(srcs) + k] = n_out + k
        res = pl.pallas_call(
            wrapped, name=name, grid=grid,
            in_specs=ins + [HBM_SPEC] * (len(srcs) + len(arrs)),
            out_specs=o_specs + [HBM_SPEC] * len(arrs),
            out_shape=shapes + [jax.ShapeDtypeStruct(a.shape, a.dtype) for a in arrs],
            scratch_shapes=list(scratch_shapes) + [pltpu.SemaphoreType.DMA((r.n_sem,)) for r in riders for _ in range(2)],
            input_output_aliases=aliases,
            compiler_params=_params(*["arbitrary"] * len(grid)),
        )(*args, *srcs, *arrs)
        core, rest = res[:n_out], list(res[n_out:])
        carried, a0 = [], 0
        for r in riders:
            carried.append(rest[a0:a0 + len(r.arrays)])
            a0 += len(r.arrays)
        return (core[0] if single else core), carried

    return run


def _mm(name, dims, a, a_spec, b, b_spec, grid, tile, outs, epilogue=None, extras=(), riders=()):
    nk, ne, no = grid[2], len(extras), len(outs)

    def body(a_ref, b_ref, *refs):
        e_refs, o_refs = refs[:ne], refs[ne:ne + no]
        part = lax.dot_general(a_ref[...], b_ref[...], dims, preferred_element_type=F32)

        def finish(acc):
            if epilogue is None:
                o_refs[0][...] = acc.astype(o_refs[0].dtype)
            else:
                epilogue(acc, [r[...] for r in e_refs], o_refs)

        if nk == 1:
            finish(part)
        else:
            acc_ref = refs[ne + no]
            k = pl.program_id(2)

            @pl.when(k == 0)
            def _():
                acc_ref[...] = part

            @pl.when(k > 0)
            def _():
                acc_ref[...] += part

            @pl.when(k == nk - 1)
            def _():
                finish(acc_ref[...])

    res = _carry(
        riders, body, name=name, grid=grid,
        in_specs=[a_spec, b_spec] + [s for _, s in extras],
        out_specs=[s for _, s in outs],
        out_shape=[s for s, _ in outs],
        scratch_shapes=[pltpu.VMEM(tile, F32)] if nk > 1 else [],
        semantics=("parallel", "parallel", "arbitrary"),
    )(a, b, *[e for e, _ in extras])
    res, carried = res if riders else (res, None)
    res = res[0] if no == 1 else res
    return (res, carried) if riders else res


def _tile_spec(tm, tn, col0=0):
    return pl.BlockSpec((tm, tn), lambda i, j, k: (i, j + col0))


def _out2d(m, n, dtype, tm, tn):
    return (jax.ShapeDtypeStruct((m, n), dtype), _tile_spec(tm, tn))


def _mm_fwd(name, a, w, tm, tn, tk, outs=None, epilogue=None, extras=(), col0=0, ncols=None, riders=()):
    m, kdim = a.shape
    if w.ndim == 3:
        per = w.shape[2] // tn
        n = ncols or N_SHARD * w.shape[2]
        w_spec = pl.BlockSpec((None, tk, tn), lambda i, j, k: ((j + col0) // per, k, (j + col0) % per))
    else:
        n = ncols or w.shape[1]
        w_spec = pl.BlockSpec((tk, tn), lambda i, j, k: (k, j + col0))
    if outs is None:
        outs = [_out2d(m, n, F32, tm, tn)]
    return _mm(name, NN, a, pl.BlockSpec((tm, tk), lambda i, j, k: (i, k)), w, w_spec,
               (m // tm, n // tn, kdim // tk), (tm, tn), outs, epilogue, extras, riders)


def _mm_bwd_x(name, g, g_spec, w, tm, tj, tc, m, n_contract, outs=None, epilogue=None, extras=(), riders=()):
    if w.ndim == 3:
        per = w.shape[2] // tc
        kdim = w.shape[1]
        w_spec = pl.BlockSpec((None, tj, tc), lambda i, j, n: (n // per, j, n % per))
    else:
        kdim = w.shape[0]
        w_spec = pl.BlockSpec((tj, tc), lambda i, j, n: (j, n))
    if outs is None:
        outs = [_out2d(m, kdim, F32, tm, tj)]
    return _mm(name, NT, g, g_spec, w, w_spec, (m // tm, kdim // tj, n_contract // tc),
               (tm, tj), outs, epilogue, extras, riders)


def _mm_bwd_w(name, a, g, g_spec, n, tk, tn, tm, sharded, riders=()):
    m, kdim = a.shape
    if sharded:
        per = (n // N_SHARD) // tn
        out = (jax.ShapeDtypeStruct((N_SHARD, kdim, n // N_SHARD), F32),
               pl.BlockSpec((None, tk, tn), lambda i, j, mm: (j // per, i, j % per)))
    else:
        out = (jax.ShapeDtypeStruct((kdim, n), F32), pl.BlockSpec((tk, tn), lambda i, j, mm: (i, j)))
    return _mm(name, TN, a, pl.BlockSpec((tm, tk), lambda i, j, mm: (mm, i)), g, g_spec,
               (kdim // tk, n // tn, m // tm), (tk, tn), [out], riders=riders)


def _ew(name, fn, tiles, fulls, outs, sums=(), ts=512, riders=()):
    tiles = [t if isinstance(t, tuple) else (t, pl.BlockSpec((ts, t.shape[1]), lambda i: (i, 0)))
             for t in tiles]
    s_rows = tiles[0][0].shape[-2]
    nt, nf, no = len(tiles), len(fulls), len(outs)

    def body(*refs):
        t_vals = [r[...] for r in refs[:nt]]
        f_vals = [r[...] for r in refs[nt:nt + nf]]
        o_refs, s_refs = refs[nt + nf:nt + nf + no], refs[nt + nf + no:]
        o_vals, s_vals = fn(t_vals, f_vals)
        for r, v in zip(o_refs, o_vals):
            r[...] = v.astype(r.dtype)
        for r, v in zip(s_refs, s_vals):
            part = jnp.sum(v, axis=0, keepdims=True)

            @pl.when(pl.program_id(0) == 0)
            def _():
                r[...] = part

            @pl.when(pl.program_id(0) > 0)
            def _():
                r[...] += part

    full_specs = [pl.BlockSpec(f.shape, lambda i, nd=f.ndim: (0,) * nd) for f in fulls]
    return _carry(
        riders, body, name=name, grid=(s_rows // ts,),
        in_specs=[s for _, s in tiles] + full_specs,
        out_specs=[pl.BlockSpec((ts, c), lambda i: (i, 0)) for c, _ in outs]
        + [pl.BlockSpec((1, c), lambda i: (0, 0)) for c in sums],
        out_shape=[jax.ShapeDtypeStruct((s_rows, c), dt) for c, dt in outs]
        + [jax.ShapeDtypeStruct((1, c), F32) for c in sums],
        semantics=("arbitrary",),
    )(*[t for t, _ in tiles], *fulls)


def _rms_fwd(name, x, g, riders=()):
    def fn(t, f):
        xv = t[0]
        r = lax.rsqrt(jnp.mean(xv * xv, axis=-1, keepdims=True) + EPS)
        return [xv * r * f[0]], []
    out = _ew(name, fn, [x], [g], [(x.shape[1], BF16)], riders=riders)
    return (out[0][0], out[1]) if riders else out[0]


def _rms_bwd(name, dh, x, g, dres, out_dtype=BF16, riders=()):
    def fn(t, f):
        dhv, xv, dr = [v.astype(F32) for v in t]
        r = lax.rsqrt(jnp.mean(xv * xv, axis=-1, keepdims=True) + EPS)
        xhat = xv * r
        u = dhv * f[0]
        dx = r * (u - xhat * jnp.mean(u * xhat, axis=-1, keepdims=True)) + dr
        return [dx], [dhv * xhat]
    c = x.shape[1]
    return _ew(name, fn, [dh, x, dres], [g], [(c, out_dtype)], sums=[c], riders=riders)


def _split3(x):
    x1 = x.astype(BF16)
    r1 = x - x1.astype(F32)
    x2 = r1.astype(BF16)
    x3 = (r1 - x2.astype(F32)).astype(BF16)
    return x1, x2, x3


def _rel_class(cp):
    far = (cp < MAX_REL) | (cp > BAND)
    return jnp.where(far, 2 * MAX_REL, BAND - cp)


def _skew_rows(x, sign):
    row = lax.broadcasted_iota(jnp.int32, x.shape, 0)
    for b in range(CHUNK.bit_length() - 1):
        shift = (1 << b) if sign > 0 else SKEW_W - (1 << b)
        x = jnp.where((row >> b) & 1 == 1, pltpu.roll(x, shift, 1), x)
    return x


def _roll_lanes(x, shift):
    return x if shift % SKEW_W == 0 else pltpu.roll(x, shift % SKEW_W, 1)


N_START = 3


def _bias_expand(rel_bias, riders=()):
    rel = jnp.pad(rel_bias, ((0, 0), (0, REL_PAD - N_REL))).reshape(N_HEADS, 1, REL_PAD)

    def body(rel_ref, o_ref):
        cls = lax.broadcasted_iota(jnp.int32, (REL_PAD, SKEW_W), 0)
        cp = lax.broadcasted_iota(jnp.int32, (REL_PAD, SKEW_W), 1)
        onehot = (cls == _rel_class(cp)).astype(BF16)
        rel8 = jnp.broadcast_to(rel_ref[...], (8, REL_PAD))
        trow = sum(jnp.dot(p, onehot, preferred_element_type=F32) for p in _split3(rel8))[0:1]
        first = _skew_rows(jnp.broadcast_to(trow, (CHUNK, SKEW_W)), +1)
        full = jnp.concatenate([_roll_lanes(first, CHUNK * g) for g in range(SUPER // CHUNK)], axis=0)[:, :BAND]
        qc = lax.broadcasted_iota(jnp.int32, (SUPER, BAND), 0) // CHUNK
        kc = lax.broadcasted_iota(jnp.int32, (SUPER, BAND), 1) // CHUNK
        on_band = (kc >= qc) & (kc <= qc + N_PREV_CHUNKS)
        table = jnp.where(on_band, full, NEG_INF).T
        key = lax.broadcasted_iota(jnp.int32, (BAND, SUPER), 0)
        for t in range(N_START):
            o_ref[t] = jnp.where(key < (N_START - 1 - t) * SUPER, NEG_INF, table)

    return _carry(
        riders, body, name="bias_expand", grid=(N_HEADS,),
        in_specs=[pl.BlockSpec((None, 1, REL_PAD), lambda h: (h, 0, 0))],
        out_specs=pl.BlockSpec((N_START, None, BAND, SUPER), lambda h: (0, h, 0, 0)),
        out_shape=jax.ShapeDtypeStruct((N_START, N_HEADS, BAND, SUPER), F32),
        semantics=("arbitrary",),
    )(rel)


def _bias_reduce(dbias, riders=()):
    def body(d_ref, o_ref):
        x = jnp.concatenate([d_ref[...].T, jnp.zeros((SUPER, SKEW_W - BAND), F32)], axis=1)
        folded = sum(_roll_lanes(x[CHUNK * g:CHUNK * (g + 1)], -CHUNK * g) for g in range(SUPER // CHUNK))
        diag = jnp.sum(_skew_rows(folded, -1), axis=0, keepdims=True)
        cp = lax.broadcasted_iota(jnp.int32, (SKEW_W, REL_PAD), 0)
        cls = lax.broadcasted_iota(jnp.int32, (SKEW_W, REL_PAD), 1)
        onehot = (cls == _rel_class(cp)).astype(BF16)
        diag8 = jnp.broadcast_to(diag, (8, SKEW_W))
        o_ref[...] = sum(jnp.dot(p, onehot, preferred_element_type=F32) for p in _split3(diag8))[0:1]

    out = _carry(
        riders, body, name="bias_reduce", grid=(N_HEADS,),
        in_specs=[pl.BlockSpec((None, BAND, SUPER), lambda h: (h, 0, 0))],
        out_specs=pl.BlockSpec((None, 1, REL_PAD), lambda h: (h, 0, 0)),
        out_shape=jax.ShapeDtypeStruct((N_HEADS, 1, REL_PAD), F32),
        semantics=("arbitrary",),
    )(dbias)
    out, carried = out if riders else (out, None)
    out = out.reshape(N_HEADS, REL_PAD)[:, :N_REL]
    return (out, carried) if riders else out


HEADS_PER_STEP = 8
HEAD_COLS = HEADS_PER_STEP * HEAD_DIM
N_HEAD_GROUPS = N_HEADS // HEADS_PER_STEP


def _scores_t(qs, kn, bias_t):
    return jnp.concatenate([lax.dot_general(k, qs, NT, preferred_element_type=F32) for k in kn], axis=0) + bias_t


def _bias_spec():
    return pl.BlockSpec((None, HEADS_PER_STEP, BAND, SUPER), lambda hg, i: (jnp.minimum(i, N_START - 1), hg, 0, 0))


def _band_specs(nb, col0, clamp_hi):
    def spec(d):
        def index(hg, i):
            blk = jnp.maximum(i - d, 0)
            if clamp_hi:
                blk = jnp.minimum(blk, nb - 1)
            return (blk, col0 + hg)
        return pl.BlockSpec((SUPER, HEAD_COLS), index)
    return [spec(2), spec(1), spec(0)]


def _head_sums(y):
    same_head = (lax.broadcasted_iota(jnp.int32, (MXU_DIM, MXU_DIM), 0) // HEAD_DIM
                 == lax.broadcasted_iota(jnp.int32, (MXU_DIM, MXU_DIM), 1) // HEAD_DIM).astype(BF16)
    sums = []
    for c0 in range(0, y.shape[1], MXU_DIM):
        chunk = y[:, c0:c0 + MXU_DIM]
        hi = chunk.astype(BF16)
        lo = (chunk - hi.astype(F32)).astype(BF16)
        sums.append(jnp.dot(hi, same_head, preferred_element_type=F32)
                    + jnp.dot(lo, same_head, preferred_element_type=F32))
    return jnp.concatenate(sums, axis=1)


def _head_unit(x):
    r = lax.rsqrt(_head_sums(x * x) * (1.0 / HEAD_DIM) + EPS)
    return x * r, r


def _head(hh):
    return slice(HEAD_DIM * hh, HEAD_DIM * (hh + 1))


def _key_block(j):
    return slice(SUPER * j, SUPER * (j + 1))


LSE_ROWS = 8


def _attn_fwd(qkn, v, bias, riders=()):
    s_len = qkn.shape[0]
    nb = s_len // SUPER

    def body(q_ref, k0, k1, k2, v0, v1, v2, b_ref, o_ref, lse_ref):
        outs = []
        v_t = [v0[...].T, v1[...].T, v2[...].T]

        def probabilities(hh):
            sl = _head(hh)
            s = _scores_t(q_ref[:, sl], [k0[:, sl], k1[:, sl], k2[:, sl]], b_ref[hh])
            m = jnp.max(s, axis=0, keepdims=True)
            e = jnp.exp(s - m)
            l = jnp.sum(e, axis=0, keepdims=True)
            lse_ref[hh:hh + 1, :] = m + jnp.log(l)
            return (e * (1.0 / l)).astype(BF16), sl

        def weighted_values(p, sl):
            outs.append(sum(jnp.dot(v_t[j][sl, :], p[_key_block(j), :], preferred_element_type=F32)
                            for j in range(3)))

        ready = probabilities(0)
        for hh in range(1, HEADS_PER_STEP):
            following = probabilities(hh)
            weighted_values(*ready)
            ready = following
        weighted_values(*ready)
        o_ref[...] = jnp.concatenate(outs, axis=0).T.astype(o_ref.dtype)

    return _carry(
        riders, body, name="attn_fwd", grid=(N_HEAD_GROUPS, nb),
        in_specs=[pl.BlockSpec((SUPER, HEAD_COLS), lambda hg, i: (i, hg))]
        + _band_specs(nb, N_HEAD_GROUPS, False) + _band_specs(nb, 0, False) + [_bias_spec()],
        out_specs=[pl.BlockSpec((SUPER, HEAD_COLS), lambda hg, i: (i, hg)),
                   pl.BlockSpec((None, LSE_ROWS, SUPER), lambda hg, i: (hg, 0, i))],
        out_shape=[jax.ShapeDtypeStruct((s_len, D_MODEL), BF16),
                   jax.ShapeDtypeStruct((N_HEAD_GROUPS, LSE_ROWS, s_len), F32)],
        semantics=("parallel", "arbitrary"),
    )(qkn, qkn, qkn, qkn, v, v, v, bias)


def _attn_bwd(qkn, v, out, d_out, bias, lse, riders=()):
    s_len = qkn.shape[0]
    nb = s_len // SUPER

    def body(q_ref, k0, k1, k2, v0, v1, v2, o_ref, do_ref, b_ref, lse_ref, dp_ref, db_ref, aq_ref, ak_ref, av_ref):
        i = pl.program_id(1)

        @pl.when(i == 0)
        def _():
            aq_ref[...] = jnp.zeros_like(aq_ref)
            ak_ref[...] = jnp.zeros_like(ak_ref)
            av_ref[...] = jnp.zeros_like(av_ref)
            db_ref[...] = jnp.zeros_like(db_ref)

        @pl.when(i < nb)
        def _():
            dq, dk, dv = [], [[], [], []], [[], [], []]
            ones = jnp.ones((8, HEAD_DIM), BF16)
            k_t = [k0[...].T, k1[...].T, k2[...].T]

            def softmax_grad(hh):
                sl = _head(hh)
                qs, do = q_ref[:, sl], do_ref[:, sl]
                kn = [k0[:, sl], k1[:, sl], k2[:, sl]]
                prod = do.astype(F32) * o_ref[:, sl].astype(F32)
                hi = prod.astype(BF16)
                lo = (prod - hi.astype(F32)).astype(BF16)
                delta = (lax.dot_general(ones, hi, NT, preferred_element_type=F32)
                         + lax.dot_general(ones, lo, NT, preferred_element_type=F32))[0:1]
                lse_row = lse_ref[hh:hh + 1, :]
                pb, dsb = [], []
                for j, (kj, vj) in enumerate(zip(kn, (v0, v1, v2))):
                    rows = _key_block(j)
                    p = jnp.exp(lax.dot_general(kj, qs, NT, preferred_element_type=F32) + b_ref[hh, rows, :] - lse_row)
                    ds = p * (lax.dot_general(vj[:, sl], do, NT, preferred_element_type=F32) - delta)
                    db_ref[hh, rows, :] += ds
                    pb.append(p.astype(BF16))
                    dsb.append(ds.astype(BF16))
                return pb, dsb, qs, do, sl

            def operand_grads(pb, dsb, qs, do, sl):
                dq.append(sum(jnp.dot(k_t[j][sl, :], dsb[j], preferred_element_type=F32) for j in range(3)))
                for j in range(3):
                    dv[j].append(jnp.dot(pb[j], do, preferred_element_type=F32))
                    dk[j].append(jnp.dot(dsb[j], qs, preferred_element_type=F32))

            ready = softmax_grad(0)
            for hh in range(1, HEADS_PER_STEP):
                following = softmax_grad(hh)
                operand_grads(*ready)
                ready = following
            operand_grads(*ready)
            aq_ref[i % 3] = jnp.concatenate(dq, axis=0).T
            for j in range(3):
                slot = (i + 1 + j) % 3
                if j < 2:
                    ak_ref[slot] += jnp.concatenate(dk[j], axis=1)
                    av_ref[slot] += jnp.concatenate(dv[j], axis=1)
                else:
                    ak_ref[slot] = jnp.concatenate(dk[j], axis=1)
                    av_ref[slot] = jnp.concatenate(dv[j], axis=1)

        slot = (i + 1) % 3
        dp_ref[0] = aq_ref[slot].astype(dp_ref.dtype)
        dp_ref[1] = ak_ref[slot].astype(dp_ref.dtype)
        dp_ref[2] = av_ref[slot].astype(dp_ref.dtype)

    def qrow(hg, i):
        return (jnp.minimum(i, nb - 1), hg)

    return _carry(
        riders, body, name="attn_bwd", grid=(N_HEAD_GROUPS, nb + 2),
        in_specs=[pl.BlockSpec((SUPER, HEAD_COLS), qrow)]
        + _band_specs(nb, N_HEAD_GROUPS, True) + _band_specs(nb, 0, True)
        + [pl.BlockSpec((SUPER, HEAD_COLS), qrow), pl.BlockSpec((SUPER, HEAD_COLS), qrow), _bias_spec(),
           pl.BlockSpec((None, LSE_ROWS, SUPER), lambda hg, i: (hg, 0, jnp.minimum(i, nb - 1)))],
        out_specs=[pl.BlockSpec((3, SUPER, HEAD_COLS), lambda hg, i: (0, jnp.maximum(i - 2, 0), hg)),
                   pl.BlockSpec((HEADS_PER_STEP, BAND, SUPER), lambda hg, i: (hg, 0, 0))],
        out_shape=[jax.ShapeDtypeStruct((6, s_len, D_MODEL), BF16),
                   jax.ShapeDtypeStruct((N_HEADS, BAND, SUPER), F32)],
        scratch_shapes=[pltpu.VMEM((3, SUPER, HEAD_COLS), F32)] * 3,
        semantics=("parallel", "arbitrary"),
    )(qkn, qkn, qkn, qkn, v, v, v, out, d_out, bias, lse)


def _qk_norm_bwd(dproj6, qk_raw, gq, gk):
    s_len = qk_raw.shape[0]
    ts = min(1024, s_len)

    nsteps = s_len // ts
    half = D_MODEL // 2

    def body(d_ref, raw_ref, gq_ref, gk_ref, o_ref, dgq_ref, dgk_ref, acc_ref):
        step = pl.program_id(0)

        @pl.when(step == 0)
        def _():
            acc_ref[...] = jnp.zeros_like(acc_ref)

        for piece, (g_ref, scale) in enumerate(((gq_ref, QK_SCALE), (gk_ref, 1.0))):
            for c0 in (0, half):
                xhat, r = _head_unit(raw_ref[:, piece * D_MODEL + c0:piece * D_MODEL + c0 + half].astype(F32))
                dn = d_ref[piece, :, c0:c0 + half].astype(F32) * scale
                u = dn * g_ref[...]
                dx = r * (u - xhat * (_head_sums(u * xhat) * (1.0 / HEAD_DIM)))
                o_ref[piece, :, c0:c0 + half] = dx.astype(o_ref.dtype)
                acc_ref[piece:piece + 1, c0:c0 + half] += jnp.sum(dn * xhat, axis=0, keepdims=True)

        @pl.when(step == nsteps - 1)
        def _():
            lane = lax.broadcasted_iota(jnp.int32, (D_MODEL, LANE), 0) % HEAD_DIM
            fold = (lane == lax.broadcasted_iota(jnp.int32, (D_MODEL, LANE), 1)).astype(BF16)
            tot = sum(jnp.dot(p, fold, preferred_element_type=F32) for p in _split3(acc_ref[...]))
            dgq_ref[...] = tot[0:1, :HEAD_DIM]
            dgk_ref[...] = tot[1:2, :HEAD_DIM]

    gain = pl.BlockSpec((1, half), lambda i: (0, 0))
    small = pl.BlockSpec((1, HEAD_DIM), lambda i: (0, 0))
    per_head = lambda g: jnp.tile(g, (1, half // HEAD_DIM))
    return pl.pallas_call(
        body, name="qk_norm_bwd", grid=(nsteps,),
        in_specs=[pl.BlockSpec((2, ts, D_MODEL), lambda i: (0, i, 0)),
                  pl.BlockSpec((ts, 2 * D_MODEL), lambda i: (i, 0)), gain, gain],
        out_specs=[pl.BlockSpec((2, ts, D_MODEL), lambda i: (0, i, 0)), small, small],
        out_shape=[jax.ShapeDtypeStruct(dproj6.shape, dproj6.dtype),
                   jax.ShapeDtypeStruct((1, HEAD_DIM), F32), jax.ShapeDtypeStruct((1, HEAD_DIM), F32)],
        scratch_shapes=[pltpu.VMEM((8, D_MODEL), F32)],
        input_output_aliases={0: 0},
        compiler_params=_params("arbitrary"),
    )(dproj6, qk_raw, per_head(gq), per_head(gk))


CONV_ROWS = 512
HALO = 16


def _rows_with_halo(ref, r0, n, front, s_len):
    zeros = jnp.zeros((HALO, ref.shape[1]), F32)
    if front:
        return (jnp.concatenate([zeros, ref[0:n, :].astype(F32)], axis=0) if r0 == 0
                else ref[r0 - HALO:r0 + n, :].astype(F32))
    return (jnp.concatenate([ref[r0:r0 + n, :].astype(F32), zeros], axis=0) if r0 + n == s_len
            else ref[r0:r0 + n + HALO, :].astype(F32))


def _earlier(ext, k):
    return pltpu.roll(ext, k, 0)[HALO:]


def _later(ext, k):
    n = ext.shape[0]
    return pltpu.roll(ext, n - k, 0)[:n - HALO]


def _conv_cols(col0):
    return lambda s_len: pl.BlockSpec((s_len, LANE), lambda j: (0, col0 + j))


def _conv_fwd(proj, conv_w, conv_b, riders=()):
    s_len = proj.shape[0]

    def body(bg_ref, cg_ref, xc_ref, w_ref, b_ref, o_ref):
        w = [w_ref[t:t + 1, :] for t in range(3)]
        for r0 in range(0, s_len, CONV_ROWS):
            u = _rows_with_halo(cg_ref, r0, CONV_ROWS, True, s_len) * \
                _rows_with_halo(xc_ref, r0, CONV_ROWS, True, s_len)
            conv = b_ref[...] + w[0] * _earlier(u, 2) + w[1] * _earlier(u, 1) + w[2] * u[HALO:]
            o_ref[r0:r0 + CONV_ROWS, :] = (bg_ref[r0:r0 + CONV_ROWS, :].astype(F32) * conv).astype(o_ref.dtype)

    return _carry(
        riders, body, name="conv_fwd", grid=(D_MODEL // LANE,),
        in_specs=[_conv_cols(0)(s_len), _conv_cols(8)(s_len), _conv_cols(16)(s_len),
                  pl.BlockSpec((3, LANE), lambda j: (0, j)), pl.BlockSpec((1, LANE), lambda j: (0, j))],
        out_specs=pl.BlockSpec((s_len, LANE), lambda j: (0, j)),
        out_shape=jax.ShapeDtypeStruct((s_len, D_MODEL), BF16),
        semantics=("parallel",),
    )(proj, proj, proj, conv_w, conv_b)


def _conv_bwd(dproj6, dy, proj, conv_w, conv_b, riders=()):
    s_len = proj.shape[0]

    def body(dy_ref, bg_ref, cg_ref, xc_ref, w_ref, b_ref, _, dp_ref, dw_ref):
        w = [w_ref[t:t + 1, :] for t in range(3)]
        acc = [jnp.zeros((1, LANE), F32) for _ in range(4)]
        for r0 in range(0, s_len, CONV_ROWS):
            rows = slice(r0, r0 + CONV_ROWS)
            u = _rows_with_halo(cg_ref, r0, CONV_ROWS, True, s_len) * \
                _rows_with_halo(xc_ref, r0, CONV_ROWS, True, s_len)
            u2, u1, u0 = _earlier(u, 2), _earlier(u, 1), u[HALO:]
            conv = b_ref[...] + w[0] * u2 + w[1] * u1 + w[2] * u0
            dp_ref[0, rows, :] = (dy_ref[rows, :].astype(F32) * conv).astype(dp_ref.dtype)
            dconv_ext = _rows_with_halo(dy_ref, r0, CONV_ROWS, False, s_len) * \
                _rows_with_halo(bg_ref, r0, CONV_ROWS, False, s_len)
            dconv = dconv_ext[:CONV_ROWS]
            for t, term in enumerate([dconv * u2, dconv * u1, dconv * u0, dconv]):
                acc[t] = acc[t] + jnp.sum(term, axis=0, keepdims=True)
            du = w[2] * dconv + w[1] * _later(dconv_ext, 1) + w[0] * _later(dconv_ext, 2)
            dp_ref[1, rows, :] = (du * xc_ref[rows, :].astype(F32)).astype(dp_ref.dtype)
            dp_ref[2, rows, :] = (du * cg_ref[rows, :].astype(F32)).astype(dp_ref.dtype)
        dw_ref[...] = jnp.zeros_like(dw_ref)
        for t in range(4):
            dw_ref[t:t + 1, :] = acc[t]

    return _carry(
        riders, body, name="conv_bwd", grid=(D_MODEL // LANE,),
        in_specs=[pl.BlockSpec((s_len, LANE), lambda j: (0, j)),
                  _conv_cols(0)(s_len), _conv_cols(8)(s_len), _conv_cols(16)(s_len),
                  pl.BlockSpec((3, LANE), lambda j: (0, j)), pl.BlockSpec((1, LANE), lambda j: (0, j)),
                  pl.BlockSpec(memory_space=pl.ANY)],
        out_specs=[pl.BlockSpec((3, s_len, LANE), lambda j: (1, 0, j)),
                   pl.BlockSpec((8, LANE), lambda j: (0, j))],
        out_shape=[jax.ShapeDtypeStruct(dproj6.shape, dproj6.dtype),
                   jax.ShapeDtypeStruct((8, D_MODEL), F32)],
        input_output_aliases={6: 0},
        semantics=("parallel",),
    )(dy, proj, proj, proj, conv_w, conv_b, dproj6)


def _d_input(name, terms, riders=()):
    s_len, kdim = terms[0][0].shape[1], terms[0][1].shape[1]
    tm, chunk = 512, 512
    n = len(terms)

    def body(*refs):
        o_ref = refs[-1]
        acc = jnp.zeros(o_ref.shape, F32)
        for p_ref, w_ref in zip(refs[:n], refs[n:2 * n]):
            n_piece, _, width = p_ref.shape
            per_shard = w_ref.shape[2]
            for c0 in range(0, n_piece * width, chunk):
                acc = acc + lax.dot_general(p_ref[c0 // width, :, c0 % width:c0 % width + chunk],
                                            w_ref[c0 // per_shard, :, c0 % per_shard:c0 % per_shard + chunk],
                                            NT, preferred_element_type=F32)
        o_ref[...] = acc.astype(o_ref.dtype)

    tile = pl.BlockSpec((tm, kdim), lambda i: (i, 0))
    return _carry(
        riders, body, name=name, grid=(s_len // tm,),
        in_specs=[pl.BlockSpec((p.shape[0], tm, p.shape[2]), lambda i: (0, i, 0)) for p, _ in terms]
        + [pl.BlockSpec(w.shape, lambda i: (0, 0, 0), pipeline_mode=pl.Buffered(1)) for _, w in terms],
        out_specs=tile, out_shape=jax.ShapeDtypeStruct((s_len, kdim), BF16),
        semantics=("parallel",),
    )(*[p for p, _ in terms], *[w for _, w in terms])


def _local_grads(x, target, norm1_g, q_norm_g, k_norm_g, rel_bias, conv_b, b_gate, norm2_g, comm):
    s_len = x.shape[0]
    tm = min(1024, s_len)
    tb = min(2048, s_len)
    row = lambda v: v.reshape(1, -1)

    def carrying(at, fn, *args, **kw):
        riders = comm.riders(at)
        out = fn(*args, riders=riders, **kw)
        if riders:
            out, carried = out
            comm.done(at, carried)
        return out

    bias = carrying("bias_expand", _bias_expand, rel_bias)
    h = carrying("norm1", _rms_fwd, "norm1", x, row(norm1_g))
    w_in3, conv_w = comm.early_weights()
    gq, gk = row(q_norm_g), row(k_norm_g)

    def qk_epi(acc, e, o):
        o[0][...] = acc.astype(BF16)
        gain = jnp.where(pl.program_id(1) < 2, e[0] * QK_SCALE, e[1])
        o[1][...] = (_head_unit(acc)[0] * gain).astype(BF16)
    conv_in = carrying("proj_conv", _mm_fwd, "proj_conv", h, w_in3, tb, 1536, D_MODEL, col0=2, ncols=3 * D_MODEL,
                       outs=[_out2d(s_len, 3 * D_MODEL, BF16, tb, 1536)])
    small = pl.BlockSpec((1, 512), lambda i, j, k: (0, 0))
    per_head = lambda g: jnp.tile(g, (1, 512 // HEAD_DIM))
    qk_raw, qkn = carrying("proj_qk", _mm_fwd, "proj_qk", h, w_in3, tm, 512, D_MODEL, ncols=2 * D_MODEL,
                           epilogue=qk_epi, outs=[_out2d(s_len, 2 * D_MODEL, BF16, tm, 512)] * 2,
                           extras=[(per_head(gq), small), (per_head(gk), small)])
    v = carrying("proj_v", _mm_fwd, "proj_v", h, w_in3, tb, 512, D_MODEL, col0=4, ncols=D_MODEL,
                 outs=[_out2d(s_len, D_MODEL, BF16, tb, 512)])
    w_gate3 = comm.gate_weight()

    def gate_epi(acc, e, o):
        o[0][...] = jax.nn.sigmoid(acc + e[0]).astype(BF16)
    gates = carrying("gates", _mm_fwd, "gates", h, w_gate3, tb, 512, D_MODEL, epilogue=gate_epi,
                     outs=[_out2d(s_len, 2 * D_MODEL, BF16, tb, 512)],
                     extras=[(row(b_gate), pl.BlockSpec((1, 512), lambda i, j, k: (0, j)))])

    attn, lse = carrying("attn_fwd", _attn_fwd, qkn, v, bias)
    yconv = carrying("conv_fwd", _conv_fwd, conv_in, conv_w, row(conv_b))
    w_ap, w_cp, w_out, w_up3, w_down = comm.late_weights()
    tw = 1024
    ya = _mm_fwd("attn_proj", attn, w_ap, tm, tw, D_MODEL, outs=[_out2d(s_len, D_MODEL, BF16, tm, tw)])

    def merge_epi(acc, e, o):
        ya_v, ga, gc = [t.astype(F32) for t in e]
        o[0][...] = acc.astype(BF16)
        o[1][...] = (ga * ya_v + gc * acc).astype(BF16)
    gate_a, gate_c = (gates, _tile_spec(tm, tw, 0)), (gates, _tile_spec(tm, tw, 1))
    yc, merged = _mm_fwd("conv_proj", yconv, w_cp, tm, tw, D_MODEL, epilogue=merge_epi,
                         outs=[_out2d(s_len, D_MODEL, BF16, tm, tw), _out2d(s_len, D_MODEL, BF16, tm, tw)],
                         extras=[(ya, _tile_spec(tm, tw)), gate_a, gate_c])

    def res_epi(acc, e, o):
        o[0][...] = e[0] + acc
    x1 = _mm_fwd("out_proj", merged, w_out, tm, tw, D_MODEL, epilogue=res_epi,
                 extras=[(x, _tile_spec(tm, tw))])
    h2 = _rms_fwd("norm2", x1, row(norm2_g))

    def up_epi(acc, e, o):
        o[0][...] = jnp.square(jnp.maximum(acc, 0.0)).astype(BF16)
    act = _mm_fwd("mlp_up", h2, w_up3, tb, tw, D_MODEL, epilogue=up_epi, outs=[_out2d(s_len, D_FF, BF16, tb, tw)])

    def loss_epi(acc, e, o):
        err = e[0] + acc - e[1]
        o[0][...] = (err * (1.0 / D_MODEL)).astype(BF16)
        sq = err * err
        part = sq[:, 0:LANE]
        for c0 in range(LANE, D_MODEL, LANE):
            part = part + sq[:, c0:c0 + LANE]
        o[1][...] = jnp.sum(part.reshape(tl // 8, 8, LANE), axis=0)
    tl = 512
    dy_b, loss_part = _mm_fwd(
        "mlp_down", act, w_down, tl, D_MODEL, D_FF, epilogue=loss_epi,
        outs=[_out2d(s_len, D_MODEL, BF16, tl, D_MODEL),
              (jax.ShapeDtypeStruct((8 * (s_len // tl), LANE), F32), pl.BlockSpec((8, LANE), lambda i, j, k: (i, 0)))],
        extras=[(x1, _tile_spec(tl, D_MODEL)), (target, _tile_spec(tl, D_MODEL))])

    def dup_epi(acc, e, o):
        o[0][...] = (acc * (2.0 * jnp.sqrt(e[0].astype(F32)))).astype(BF16)
    full = lambda cols: pl.BlockSpec((tm, cols), lambda i, j, n: (i, n))
    tokens = lambda cols: pl.BlockSpec((s_len, cols), lambda i, j, m: (m, j))
    dup = _mm_bwd_x("d_act", dy_b, full(D_MODEL), w_down, tm, tw, D_MODEL, s_len, D_MODEL, epilogue=dup_epi,
                    outs=[_out2d(s_len, D_FF, BF16, tm, tw)], extras=[(act, _tile_spec(tm, tw))])
    g_down = _mm_bwd_w("g_down", act, dy_b, tokens(D_MODEL), D_MODEL, 512, D_MODEL, s_len, False)
    g_up = _mm_bwd_w("g_up", h2, dup, tokens(512), D_FF, D_MODEL, 512, s_len, True)
    comm.grads_ready("mlp", dict(w_down=g_down, w_up=g_up))
    dh2 = carrying("d_h2", _d_input, "d_h2", [(dup.reshape(1, s_len, D_FF), w_up3)])
    dx1_b, dg2 = _rms_bwd("norm2_bwd", dh2, x1, row(norm2_g), dy_b)

    def dmerge_epi(acc, e, o):
        ya_v, yc_v, ga, gc = [t.astype(F32) for t in e]
        o[0][...] = (acc * ga).astype(BF16)
        o[1][...] = (acc * gc).astype(BF16)
        o[2][0] = (acc * ya_v * ga * (1.0 - ga)).astype(BF16)
        o[2][1] = (acc * yc_v * gc * (1.0 - gc)).astype(BF16)
    dya, dyc, dgp2 = _mm_bwd_x(
        "d_merged", dx1_b, full(D_MODEL), w_out, tm, tw, D_MODEL, s_len, D_MODEL, epilogue=dmerge_epi,
        outs=[_out2d(s_len, D_MODEL, BF16, tm, tw), _out2d(s_len, D_MODEL, BF16, tm, tw),
              (jax.ShapeDtypeStruct((2, s_len, D_MODEL), BF16), pl.BlockSpec((2, tm, tw), lambda i, j, n: (0, i, j)))],
        extras=[(ya, _tile_spec(tm, tw)), (yc, _tile_spec(tm, tw)), gate_a, gate_c])
    g_out = _mm_bwd_w("g_out", merged, dx1_b, tokens(512), D_MODEL, D_MODEL, 512, s_len, False)
    d_attn = _mm_bwd_x("d_attn", dya, full(D_MODEL), w_ap, tm, tw, D_MODEL, s_len, D_MODEL,
                       outs=[_out2d(s_len, D_MODEL, BF16, tm, tw)])
    g_ap = _mm_bwd_w("g_attn_proj", attn, dya, tokens(512), D_MODEL, D_MODEL, 512, s_len, False)
    d_yconv = _mm_bwd_x("d_yconv", dyc, full(D_MODEL), w_cp, tm, tw, D_MODEL, s_len, D_MODEL,
                        outs=[_out2d(s_len, D_MODEL, BF16, tm, tw)])
    g_cp = _mm_bwd_w("g_conv_proj", yconv, dyc, tokens(512), D_MODEL, D_MODEL, 512, s_len, False)
    piece = lambda width: (lambda blk: (blk * width) // D_MODEL, lambda blk: (blk * width % D_MODEL) // width)
    pc, cb = piece(512)
    pieces = pl.BlockSpec((None, s_len, 512), lambda i, j, m: (pc(j), m, cb(j)))
    g_gate = _mm_bwd_w("g_gate", h, dgp2, pieces, 2 * D_MODEL, D_MODEL, 512, s_len, True)
    comm.grads_ready("proj", dict(w_out=g_out, w_attn_proj=g_ap, w_conv_proj=g_cp, w_gate=g_gate))

    dproj6, dbias = carrying("attn_bwd", _attn_bwd, qkn, v, attn, d_attn, bias, lse)
    dproj6, dgq, dgk = _qk_norm_bwd(dproj6, qk_raw, gq, gk)
    dproj6, dconv_wb = carrying("conv_bwd", _conv_bwd, dproj6, d_yconv, conv_in, conv_w, row(conv_b))

    g_in = carrying("g_in", _mm_bwd_w, "g_in", h, dproj6, pieces, 6 * D_MODEL, D_MODEL, 512, s_len, True)
    comm.grads_ready("in", dict(w_in=g_in))
    d_rel = carrying("bias_reduce", _bias_reduce, dbias)
    dh = carrying("d_h", _d_input, "d_h", [(dproj6, w_in3), (dgp2, w_gate3)])
    grad_x, dg1 = carrying("norm1_bwd", _rms_bwd, "norm1_bwd", dh, x, row(norm1_g), dx1_b, out_dtype=F32)

    def bsum(t, f):
        return [], [t[0].astype(F32), t[1].astype(F32)]
    ts = 512
    db_a, db_c = carrying("b_gate_sum", _ew, "b_gate_sum", bsum,
                          [(dgp2, pl.BlockSpec((None, ts, D_MODEL), lambda i: (0, i, 0))),
                           (dgp2, pl.BlockSpec((None, ts, D_MODEL), lambda i: (1, i, 0)))],
                          [], [], sums=[D_MODEL, D_MODEL], ts=ts)

    big = dict(w_in=g_in, w_attn_proj=g_ap, w_conv_proj=g_cp, w_gate=g_gate, w_out=g_out,
               w_up=g_up, w_down=g_down)
    small = dict(norm1_g=dg1, norm2_g=dg2, conv_wb=dconv_wb, b_gate=(db_a, db_c),
                 q_norm_g=dgq, k_norm_g=dgk, rel_bias=d_rel)
    return loss_part, grad_x, big, small


def _finish_loss(loss_part):
    def body(l_ref, lo_ref):
        total = jnp.sum(jnp.sum(l_ref[...], axis=0, keepdims=True), axis=1, keepdims=True)
        lo_ref[...] = jnp.broadcast_to(total * (0.5 / D_MODEL), lo_ref.shape)

    return pl.pallas_call(body, name="finish_loss", out_shape=jax.ShapeDtypeStruct((8, LANE), F32))(loss_part)


def _place():
    return lax.axis_index("x"), lax.axis_index("y"), lax.axis_index("c")


def _other_chips(x, y):
    return [(1 - x, y), (x, 1 - y), (1 - x, 1 - y)]


def _cast_into_slot(name, where, w):
    r, cols = w.shape
    ts = 256

    def body(w_ref, x_ref, o_ref):
        o_ref[...] = x_ref[...].astype(o_ref.dtype)

    return pl.pallas_call(
        body, name=name,
        grid_spec=pltpu.PrefetchScalarGridSpec(
            num_scalar_prefetch=1, grid=(r // ts,),
            in_specs=[pl.BlockSpec((ts, cols), lambda i, w: (i, 0))],
            out_specs=pl.BlockSpec((None, ts, cols), lambda i, w: (w[0], i, 0))),
        out_shape=jax.ShapeDtypeStruct((N_SHARD, r, cols), BF16),
        compiler_params=_params("parallel"),
    )(where, w)


def _remote(src, dst, send, recv, k, to):
    return pltpu.make_async_remote_copy(src_ref=src, dst_ref=dst, send_sem=send.at[k], recv_sem=recv.at[k],
                                        device_id=to, device_id_type=MESH)


def _gather_riders(slots):
    n = len(slots)

    def copy(refs, w, j, shard, which, send, recv, to):
        hr = slots[w].shape[1] // 2
        ref = refs[w].at[shard, pl.ds(which * hr, hr)]
        return _remote(ref, ref, send, recv, 3 * w + j, to)

    def for_each_peer(fn):
        x, y, c = _place()
        for w in range(n):
            for j, chip in enumerate(_other_chips(x, y)):
                fn(w, j, 2 * x + y, 2 * chip[0] + chip[1], c, (*chip, c), (x, y, 1 - c))

    def chips_start(_, refs, send, recv):
        for_each_peer(lambda w, j, mine, theirs, c, peer, sib: copy(refs, w, j, mine, c, send, recv, peer).start())

    def chips_finish(_, refs, send, recv):
        for_each_peer(lambda w, j, mine, theirs, c, peer, sib: copy(refs, w, j, theirs, c, send, recv, peer).wait_recv())
        for_each_peer(lambda w, j, mine, theirs, c, peer, sib: copy(refs, w, j, mine, c, send, recv, peer).wait_send())

    def sibling_start(_, refs, send, recv):
        for_each_peer(lambda w, j, mine, theirs, c, peer, sib: copy(refs, w, j, theirs, c, send, recv, sib).start())

    def sibling_finish(_, refs, send, recv):
        for_each_peer(lambda w, j, mine, theirs, c, peer, sib: copy(refs, w, j, theirs, 1 - c, send, recv, sib).wait_recv())
        for_each_peer(lambda w, j, mine, theirs, c, peer, sib: copy(refs, w, j, theirs, c, send, recv, sib).wait_send())

    return (lambda arrays: _Rider([], arrays, 3 * n, chips_start, chips_finish),
            lambda arrays: _Rider([], arrays, 3 * n, sibling_start, sibling_finish))


def _pair_exchange_rider(grads, landing):
    n = len(grads)

    def copies(srcs, dsts, send, recv):
        x, y, c = _place()
        out = []
        for w in range(n):
            hr = grads[w].shape[1] // 2
            out.append(_remote(srcs[w].at[:, pl.ds((1 - c) * hr, hr)], dsts[w], send, recv, w, (x, y, 1 - c)))
        return out

    def start(srcs, dsts, send, recv):
        for cp in copies(srcs, dsts, send, recv):
            cp.start()

    def finish(srcs, dsts, send, recv):
        for cp in copies(srcs, dsts, send, recv):
            cp.wait()

    return _Rider(grads, landing, n, start, finish)


def _row_tile(hr):
    return min(hr, 256)


def _pair_add(name, where, grad, got):
    _, hr, cols = got.shape
    tr = _row_tile(hr)
    nblk = hr // tr

    def body(w_ref, g_ref, r_ref, o_ref):
        o_ref[...] = (g_ref[...] + r_ref[...]).astype(o_ref.dtype)

    other = lambda k, w: (w[0] + 1 + k) % N_SHARD
    return pl.pallas_call(
        body, name=name,
        grid_spec=pltpu.PrefetchScalarGridSpec(
            num_scalar_prefetch=1, grid=(N_SHARD - 1, nblk),
            in_specs=[pl.BlockSpec((None, tr, cols), lambda k, i, w: (other(k, w), w[1] * nblk + i, 0)),
                      pl.BlockSpec((None, tr, cols), lambda k, i, w: (other(k, w), i, 0))],
            out_specs=pl.BlockSpec((None, tr, cols), lambda k, i, w: (other(k, w), i, 0))),
        out_shape=jax.ShapeDtypeStruct(got.shape, BF16),
        compiler_params=_params("parallel", "parallel"),
    )(where, grad, got)


def _chip_exchange_rider(partials, landing):
    n = len(partials)

    def copies(srcs, dsts, send, recv):
        x, y, c = _place()
        return [_remote(srcs[w].at[2 * chip[0] + chip[1]], dsts[w].at[j], send, recv, 3 * w + j, (*chip, c))
                for w in range(n) for j, chip in enumerate(_other_chips(x, y))]

    def start(srcs, dsts, send, recv):
        for cp in copies(srcs, dsts, send, recv):
            cp.start()

    def finish(srcs, dsts, send, recv):
        for cp in copies(srcs, dsts, send, recv):
            cp.wait()

    return _Rider(partials, landing, 3 * n, start, finish)


def _final_add(name, where, grad, got, arrived):
    _, hr, cols = got.shape
    tr = _row_tile(hr)
    nblk = hr // tr

    def body(w_ref, g_ref, r_ref, a_ref, o_ref):
        acc = g_ref[...] + r_ref[...]
        for j in range(3):
            acc = acc + a_ref[j].astype(F32)
        o_ref[...] = acc

    return pl.pallas_call(
        body, name=name,
        grid_spec=pltpu.PrefetchScalarGridSpec(
            num_scalar_prefetch=1, grid=(nblk,),
            in_specs=[pl.BlockSpec((None, tr, cols), lambda i, w: (w[0], w[1] * nblk + i, 0)),
                      pl.BlockSpec((None, tr, cols), lambda i, w: (w[0], i, 0)),
                      pl.BlockSpec((3, tr, cols), lambda i, w: (0, i, 0))],
            out_specs=pl.BlockSpec((tr, cols), lambda i, w: (w[1] * nblk + i, 0))),
        out_shape=jax.ShapeDtypeStruct((2 * hr, cols), F32),
        compiler_params=_params("parallel"),
    )(where, grad, got, arrived)


def _pair_share_rider(shards):
    n = len(shards)

    def half(refs, w, which):
        hr = shards[w].shape[0] // 2
        return refs[w].at[pl.ds(which * hr, hr)]

    def start(_, refs, send, recv):
        x, y, c = _place()
        for w in range(n):
            _remote(half(refs, w, c), half(refs, w, c), send, recv, w, (x, y, 1 - c)).start()

    def finish(_, refs, send, recv):
        x, y, c = _place()
        for w in range(n):
            _remote(half(refs, w, 1 - c), half(refs, w, 1 - c), send, recv, w, (x, y, 1 - c)).wait_recv()
        for w in range(n):
            _remote(half(refs, w, c), half(refs, w, c), send, recv, w, (x, y, 1 - c)).wait_send()

    return _Rider([], shards, n, start, finish)


class _Exchange:
    PLAN = {"bias_expand": [("early", "gather")], "norm1": [("early", "forward")],
            "proj_conv": [("gate", "gather")], "proj_v": [("gate", "forward")],
            "proj_qk": [("late", "gather")], "gates": [("late", "forward")],
            "attn_fwd": [("mlp_w", "gather")], "conv_fwd": [("mlp_w", "forward")],
            "d_h2": [("mlp", "pair")], "attn_bwd": [("mlp", "chips"), ("proj", "pair")], "conv_bwd": [("mlp", "share")],
            "g_in": [("proj", "chips")], "bias_reduce": [("proj", "share"), ("in", "pair")],
            "d_h": [("in", "chips")], "b_gate_sum": [("in", "share")]}

    def __init__(self, where, early_slots, gate_slots, late_slots):
        self.where = where
        self.slots = dict(early=early_slots, gate=gate_slots, late=late_slots[:3], mlp_w=late_slots[3:])
        self.stage = {g: dict(zip(("gather", "forward"), _gather_riders(s))) for g, s in self.slots.items()}
        self.groups, self.reduced, self.pending = {}, {}, []

    def early_weights(self):
        w_in3, small = self.slots["early"]
        conv_w = small[:, :3, :].transpose(1, 0, 2).reshape(3, N_SHARD * small.shape[2])
        return w_in3, conv_w

    def gate_weight(self):
        return self.slots["gate"][0]

    def late_weights(self):
        rows = lambda a: a.reshape(a.shape[0] * a.shape[1], a.shape[2])
        w_ap, w_cp, w_out = self.slots["late"]
        w_up3, w_down = self.slots["mlp_w"]
        return rows(w_ap), rows(w_cp), rows(w_out), w_up3, rows(w_down)

    def grads_ready(self, group, grads):
        names = list(grads)
        g4 = [g if g.ndim == 3 else g.reshape(N_SHARD, -1, g.shape[1]) for g in grads.values()]
        self.groups[group] = dict(names=names, g4=g4)

    def riders(self, at):
        self.pending = self.PLAN.get(at, [])
        out = []
        for group, stage in self.pending:
            if group in self.slots:
                out.append(self.stage[group][stage](self.slots[group]))
                continue
            st = self.groups[group]
            if stage == "pair":
                landing = [lax.empty((N_SHARD, g.shape[1] // 2, g.shape[2]), F32) for g in st["g4"]]
                out.append(_pair_exchange_rider(st["g4"], landing))
            elif stage == "chips":
                landing = [lax.empty((3,) + p.shape[1:], p.dtype) for p in st["partial"]]
                out.append(_chip_exchange_rider(st["partial"], landing))
            else:
                out.append(_pair_share_rider(st["halves"]))
        return out

    def done(self, at, carried):
        for (group, stage), arrays in zip(self.pending, carried):
            if group in self.slots:
                self.slots[group] = arrays
                continue
            st = self.groups[group]
            tag = lambda what, n: what + "_" + n
            if stage == "pair":
                st["got"] = arrays
                st["partial"] = [_pair_add(tag("pair_add", n), self.where, g, r)
                                 for n, g, r in zip(st["names"], st["g4"], arrays)]
            elif stage == "chips":
                st["halves"] = [_final_add(tag("final_add", n), self.where, g, r, a)
                                for n, g, r, a in zip(st["names"], st["g4"], st["got"], arrays)]
            else:
                self.reduced.update(zip(st["names"], arrays))


SMALL_ROWS = 32
N_DEV = 8


def _all_reduce_small(pack):
    def body(p_ref, o_ref, buf, send, recv):
        x, y, c = _place()
        buf[4 * x + 2 * y + c] = p_ref[...]
        copies, waits = [], []
        for k in range(1, N_DEV):
            px = 1 - x if k & 4 else x
            py = 1 - y if k & 2 else y
            pc = 1 - c if k & 1 else c
            copies.append(_remote(p_ref, buf.at[4 * x + 2 * y + c], send, recv, k - 1, (px, py, pc)))
            waits.append(_remote(p_ref, buf.at[4 * px + 2 * py + pc], send, recv, k - 1, (px, py, pc)))
        for cp in copies:
            cp.start()
        for cp in waits:
            cp.wait_recv()
        acc = buf[0]
        for d in range(1, N_DEV):
            acc = acc + buf[d]
        o_ref[...] = acc
        for cp in copies:
            cp.wait_send()

    return pl.pallas_call(
        body, name="all_reduce_small",
        out_shape=jax.ShapeDtypeStruct(pack.shape, F32),
        scratch_shapes=[pltpu.VMEM((N_DEV,) + pack.shape, F32),
                        pltpu.SemaphoreType.DMA((N_DEV - 1,)), pltpu.SemaphoreType.DMA((N_DEV - 1,))],
    )(pack)


def _adamw(name, w, g, m, v, riders=()):
    c1 = 1.0 - ADAM_B1 ** ADAM_STEP
    c2 = 1.0 - ADAM_B2 ** ADAM_STEP

    def fn(t, f):
        wv, gv, mv, vv = t
        m2 = ADAM_B1 * mv + (1.0 - ADAM_B1) * gv
        v2 = ADAM_B2 * vv + (1.0 - ADAM_B2) * (gv * gv)
        delta = -ADAM_LR * ((m2 / c1) / (jnp.sqrt(v2 / c2) + ADAM_EPS) + ADAM_WD * wv)
        return [delta, m2, v2], []

    cols = w.shape[1]
    return _ew(name, fn, [w, g, m, v], [], [(cols, F32)] * 3, ts=min(w.shape[0], 256), riders=riders)


LOSS_ROW = 26


def _pack_small(norm1_g, norm2_g, conv_b, b_gate, conv_w, q_norm_g, k_norm_g, rel_bias, loss=None):
    pack = jnp.zeros((SMALL_ROWS, D_MODEL), F32)
    for r0, v in ((0, norm1_g), (1, norm2_g), (2, conv_b), (3, b_gate.reshape(2, D_MODEL)), (5, conv_w),
                  (8, q_norm_g), (9, k_norm_g), (10, rel_bias)) + (((LOSS_ROW, loss),) if loss is not None else ()):
        v = v.reshape(-1, v.shape[-1]).astype(F32)
        pack = pack.at[r0:r0 + v.shape[0], :v.shape[1]].set(v)
    return pack


def _unpack_small(pack, conv_cols):
    return dict(norm1_g=pack[0], norm2_g=pack[1], conv_b=pack[2], b_gate=pack[3:5].reshape(2 * D_MODEL),
                conv_w=pack[5:8, :conv_cols], q_norm_g=pack[8, :HEAD_DIM], k_norm_g=pack[9, :HEAD_DIM],
                rel_bias=pack[10:10 + N_HEADS, :N_REL])


BIG = ["w_in", "w_attn_proj", "w_conv_proj", "w_gate", "w_out", "w_up", "w_down"]
LATE = ["w_attn_proj", "w_conv_proj", "w_out", "w_up", "w_down"]
WEIGHTS = ["norm1_g", "w_in", "q_norm_g", "k_norm_g", "rel_bias", "conv_w", "conv_b", "w_attn_proj",
           "w_conv_proj", "w_gate", "b_gate", "w_out", "norm2_g", "w_up", "w_down"]


def kernel(x, norm1_g, w_in, q_norm_g, k_norm_g, rel_bias, conv_w, conv_b, w_attn_proj, w_conv_proj, w_gate, b_gate, w_out, norm2_g, w_up, w_down, loss_target, m_norm1_g, m_w_in, m_q_norm_g, m_k_norm_g, m_rel_bias, m_conv_w, m_conv_b, m_w_attn_proj, m_w_conv_proj, m_w_gate, m_b_gate, m_w_out, m_norm2_g, m_w_up, m_w_down, v_norm1_g, v_w_in, v_q_norm_g, v_k_norm_g, v_rel_bias, v_conv_w, v_conv_b, v_w_attn_proj, v_w_conv_proj, v_w_gate, v_b_gate, v_w_out, v_norm2_g, v_w_up, v_w_down):
    given = dict(locals())
    w = {n: given[n] for n in WEIGHTS}
    m = {n: given["m_" + n] for n in WEIGHTS}
    v = {n: given["v_" + n] for n in WEIGHTS}
    s_len = x.shape[1]
    shard = 2 * lax.axis_index("x") + lax.axis_index("y")
    where = jnp.stack([shard, lax.axis_index("c")]).astype(jnp.int32)
    conv_cols = conv_w.shape[1]

    small_in = lax.dynamic_update_slice(jnp.zeros((N_SHARD, 16, conv_cols), F32), conv_w[None], (shard, 0, 0))
    slot = {n: _cast_into_slot("cast_" + n, where, w[n]) for n in BIG}
    comm = _Exchange(where, [slot["w_in"], small_in], [slot["w_gate"]], [slot[n] for n in LATE])

    loss_part, grad_x, _, small = _local_grads(
        x.reshape(s_len, D_MODEL), loss_target.reshape(s_len, D_MODEL), norm1_g, q_norm_g, k_norm_g,
        rel_bias, conv_b, b_gate, norm2_g, comm)
    grad = dict(comm.reduced)

    loss_local = _finish_loss(loss_part)
    pack = _pack_small(small["norm1_g"], small["norm2_g"], small["conv_wb"][3], jnp.concatenate(small["b_gate"], axis=1),
                       small["conv_wb"][0:3], small["q_norm_g"], small["k_norm_g"], small["rel_bias"],
                       loss=loss_local[0:1, :])
    total = _all_reduce_small(pack)
    g_small = _unpack_small(total, D_MODEL)
    g_small["conv_w"] = lax.dynamic_slice(g_small["conv_w"], (0, shard * conv_cols), (3, conv_cols))
    grad.update(g_small)

    delta, new_m, new_v = {}, {}, {}
    for n in BIG:
        delta[n], new_m[n], new_v[n] = _adamw("adamw_" + n, w[n], grad[n], m[n], v[n])
    small_names = [n for n in WEIGHTS if n not in BIG]
    packs = [_pack_small(**{n: src[n] for n in small_names}) for src in (w, grad, m, v)]
    for out, packed in zip((delta, new_m, new_v), _adamw("adamw_small", *packs)):
        out.update({n: a.reshape(w[n].shape) for n, a in _unpack_small(packed, conv_cols).items()})

    outs = [total[LOSS_ROW, 0], grad_x.reshape(x.shape)]
    for group in (grad, delta, new_m, new_v):
        outs += [group[n].reshape(w[n].shape) for n in WEIGHTS]
    return tuple(outs)
```

```python
import jax
import jax.numpy as jnp
from jax import lax
from jax.experimental import pallas as pl
from jax.experimental.pallas import tpu as pltpu

F32 = jnp.float32
BF16 = jnp.bfloat16

D_MODEL = 1024
N_HEADS = 16
HEAD_DIM = 64
CHUNK = 64
N_PREV_CHUNKS = 8
MAX_REL = 256
D_FF = 4096
N_REL = 2 * MAX_REL + 1
REL_PAD = 640
EPS = 1e-6
NEG_INF = -1e30
QK_SCALE = HEAD_DIM ** -0.5

SUPER = 4 * CHUNK
BAND = SUPER + N_PREV_CHUNKS * CHUNK
SKEW_W = 1024
N_SHARD = 4
LANE = 128
MXU_DIM = 256
VMEM_LIMIT = 48 * 1024 * 1024

ADAM_LR = 0.001
ADAM_B1 = 0.9
ADAM_B2 = 0.999
ADAM_EPS = 1e-08
ADAM_WD = 0.01
ADAM_STEP = 10

MESH = pl.DeviceIdType.MESH
NN = (((1,), (0,)), ((), ()))
NT = (((1,), (1,)), ((), ()))
TN = (((0,), (0,)), ((), ()))


def _params(*sem):
    return pltpu.CompilerParams(dimension_semantics=sem or None, vmem_limit_bytes=VMEM_LIMIT)


HBM_SPEC = pl.BlockSpec(memory_space=pl.ANY)


class _Rider:
    def __init__(self, sources, arrays, n_sem, start, finish):
        self.sources, self.arrays, self.n_sem, self.start, self.finish = sources, arrays, n_sem, start, finish


def _carry(riders, body, *, name, out_shape, grid=(), in_specs=None, out_specs=None, scratch_shapes=(),
           semantics=(), input_output_aliases=None):
    aliases = dict(input_output_aliases or {})
    if not riders:
        kw = {} if in_specs is None else dict(in_specs=in_specs, out_specs=out_specs)
        return pl.pallas_call(body, name=name, grid=grid, out_shape=out_shape, scratch_shapes=scratch_shapes,
                              input_output_aliases=aliases, compiler_params=_params(*semantics), **kw)
    single = not isinstance(out_shape, (list, tuple))
    shapes = [out_shape] if single else list(out_shape)
    n_out, n_scr = len(shapes), len(scratch_shapes)
    in_hbm = lambda a: pltpu.with_memory_space_constraint(a, pltpu.HBM)
    srcs = [in_hbm(a) for r in riders for a in r.sources]
    arrs = [in_hbm(a) for r in riders for a in r.arrays]
    vmem = pl.BlockSpec(memory_space=pltpu.VMEM)

    def run(*args):
        n_in = len(args)

        def wrapped(*refs):
            pos = n_in
            src_refs = refs[pos:pos + len(srcs)]
            pos += len(srcs) + len(arrs)
            outs = refs[pos:pos + n_out]
            pos += n_out
            arr_refs = refs[pos:pos + len(arrs)]
            pos += len(arrs)
            scratch = refs[pos:pos + n_scr]
            sems = refs[pos + n_scr:]
            first, last = True, True
            for d, size in enumerate(grid):
                first = jnp.logical_and(first, pl.program_id(d) == 0)
                last = jnp.logical_and(last, pl.program_id(d) == size - 1)

            def each(method):
                s0 = a0 = 0
                for k, r in enumerate(riders):
                    getattr(r, method)(src_refs[s0:s0 + len(r.sources)], arr_refs[a0:a0 + len(r.arrays)],
                                       sems[2 * k], sems[2 * k + 1])
                    s0, a0 = s0 + len(r.sources), a0 + len(r.arrays)

            pl.when(first)(lambda: each("start"))
            body(*refs[:n_in], *outs, *scratch)
            pl.when(last)(lambda: each("finish"))

        ins = [vmem] * n_in if in_specs is None else list(in_specs)
        if out_specs is None:
            o_specs = [vmem] * n_out
        else:
            o_specs = [out_specs] if single else list(out_specs)
        for k in range(len(arrs)):
            aliases[n_in + len(srcs) + k] = n_out + k
        res = pl.pallas_call(
            wrapped, name=name, grid=grid,
            in_specs=ins + [HBM_SPEC] * (len(srcs) + len(arrs)),
            out_specs=o_specs + [HBM_SPEC] * len(arrs),
            out_shape=shapes + [jax.ShapeDtypeStruct(a.shape, a.dtype) for a in arrs],
            scratch_shapes=list(scratch_shapes) + [pltpu.SemaphoreType.DMA((r.n_sem,)) for r in riders for _ in range(2)],
            input_output_aliases=aliases,
            compiler_params=_params(*["arbitrary"] * len(grid)),
        )(*args, *srcs, *arrs)
        core, rest = res[:n_out], list(res[n_out:])
        carried, a0 = [], 0
        for r in riders:
            carried.append(rest[a0:a0 + len(r.arrays)])
            a0 += len(r.arrays)
        return (core[0] if single else core), carried

    return run


def _mm(name, dims, a, a_spec, b, b_spec, grid, tile, outs, epilogue=None, extras=(), riders=()):
    nk, ne, no = grid[2], len(extras), len(outs)

    def body(a_ref, b_ref, *refs):
        e_refs, o_refs = refs[:ne], refs[ne:ne + no]
        part = lax.dot_general(a_ref[...], b_ref[...], dims, preferred_element_type=F32)

        def finish(acc):
            if epilogue is None:
                o_refs[0][...] = acc.astype(o_refs[0].dtype)
            else:
                epilogue(acc, [r[...] for r in e_refs], o_refs)

        if nk == 1:
            finish(part)
        else:
            acc_ref = refs[ne + no]
            k = pl.program_id(2)

            @pl.when(k == 0)
            def _():
                acc_ref[...] = part

            @pl.when(k > 0)
            def _():
                acc_ref[...] += part

            @pl.when(k == nk - 1)
            def _():
                finish(acc_ref[...])

    res = _carry(
        riders, body, name=name, grid=grid,
        in_specs=[a_spec, b_spec] + [s for _, s in extras],
        out_specs=[s for _, s in outs],
        out_shape=[s for s, _ in outs],
        scratch_shapes=[pltpu.VMEM(tile, F32)] if nk > 1 else [],
        semantics=("parallel", "parallel", "arbitrary"),
    )(a, b, *[e for e, _ in extras])
    res, carried = res if riders else (res, None)
    res = res[0] if no == 1 else res
    return (res, carried) if riders else res


def _tile_spec(tm, tn, col0=0):
    return pl.BlockSpec((tm, tn), lambda i, j, k: (i, j + col0))


def _out2d(m, n, dtype, tm, tn):
    return (jax.ShapeDtypeStruct((m, n), dtype), _tile_spec(tm, tn))


def _mm_fwd(name, a, w, tm, tn, tk, outs=None, epilogue=None, extras=(), col0=0, ncols=None, riders=()):
    m, kdim = a.shape
    if w.ndim == 3:
        per = w.shape[2] // tn
        n = ncols or N_SHARD * w.shape[2]
        w_spec = pl.BlockSpec((None, tk, tn), lambda i, j, k: ((j + col0) // per, k, (j + col0) % per))
    else:
        n = ncols or w.shape[1]
        w_spec = pl.BlockSpec((tk, tn), lambda i, j, k: (k, j + col0))
    if outs is None:
        outs = [_out2d(m, n, F32, tm, tn)]
    return _mm(name, NN, a, pl.BlockSpec((tm, tk), lambda i, j, k: (i, k)), w, w_spec,
               (m // tm, n // tn, kdim // tk), (tm, tn), outs, epilogue, extras, riders)


def _mm_bwd_x(name, g, g_spec, w, tm, tj, tc, m, n_contract, outs=None, epilogue=None, extras=(), riders=()):
    if w.ndim == 3:
        per = w.shape[2] // tc
        kdim = w.shape[1]
        w_spec = pl.BlockSpec((None, tj, tc), lambda i, j, n: (n // per, j, n % per))
    else:
        kdim = w.shape[0]
        w_spec = pl.BlockSpec((tj, tc), lambda i, j, n: (j, n))
    if outs is None:
        outs = [_out2d(m, kdim, F32, tm, tj)]
    return _mm(name, NT, g, g_spec, w, w_spec, (m // tm, kdim // tj, n_contract // tc),
               (tm, tj), outs, epilogue, extras, riders)


def _mm_bwd_w(name, a, g, g_spec, n, tk, tn, tm, sharded, riders=()):
    m, kdim = a.shape
    if sharded:
        per = (n // N_SHARD) // tn
        out = (jax.ShapeDtypeStruct((N_SHARD, kdim, n // N_SHARD), F32),
               pl.BlockSpec((None, tk, tn), lambda i, j, mm: (j // per, i, j % per)))
    else:
        out = (jax.ShapeDtypeStruct((kdim, n), F32), pl.BlockSpec((tk, tn), lambda i, j, mm: (i, j)))
    return _mm(name, TN, a, pl.BlockSpec((tm, tk), lambda i, j, mm: (mm, i)), g, g_spec,
               (kdim // tk, n // tn, m // tm), (tk, tn), [out], riders=riders)


def _ew(name, fn, tiles, fulls, outs, sums=(), ts=512, riders=()):
    tiles = [t if isinstance(t, tuple) else (t, pl.BlockSpec((ts, t.shape[1]), lambda i: (i, 0)))
             for t in tiles]
    s_rows = tiles[0][0].shape[-2]
    nt, nf, no = len(tiles), len(fulls), len(outs)

    def body(*refs):
        t_vals = [r[...] for r in refs[:nt]]
        f_vals = [r[...] for r in refs[nt:nt + nf]]
        o_refs, s_refs = refs[nt + nf:nt + nf + no], refs[nt + nf + no:]
        o_vals, s_vals = fn(t_vals, f_vals)
        for r, v in zip(o_refs, o_vals):
            r[...] = v.astype(r.dtype)
        for r, v in zip(s_refs, s_vals):
            part = jnp.sum(v, axis=0, keepdims=True)

            @pl.when(pl.program_id(0) == 0)
            def _():
                r[...] = part

            @pl.when(pl.program_id(0) > 0)
            def _():
                r[...] += part

    full_specs = [pl.BlockSpec(f.shape, lambda i, nd=f.ndim: (0,) * nd) for f in fulls]
    return _carry(
        riders, body, name=name, grid=(s_rows // ts,),
        in_specs=[s for _, s in tiles] + full_specs,
        out_specs=[pl.BlockSpec((ts, c), lambda i: (i, 0)) for c, _ in outs]
        + [pl.BlockSpec((1, c), lambda i: (0, 0)) for c in sums],
        out_shape=[jax.ShapeDtypeStruct((s_rows, c), dt) for c, dt in outs]
        + [jax.ShapeDtypeStruct((1, c), F32) for c in sums],
        semantics=("arbitrary",),
    )(*[t for t, _ in tiles], *fulls)


def _rms_fwd(name, x, g, riders=()):
    def fn(t, f):
        xv = t[0]
        r = lax.rsqrt(jnp.mean(xv * xv, axis=-1, keepdims=True) + EPS)
        return [xv * r * f[0]], []
    out = _ew(name, fn, [x], [g], [(x.shape[1], BF16)], riders=riders)
    return (out[0][0], out[1]) if riders else out[0]


def _split3(x):
    x1 = x.astype(BF16)
    r1 = x - x1.astype(F32)
    x2 = r1.astype(BF16)
    x3 = (r1 - x2.astype(F32)).astype(BF16)
    return x1, x2, x3


def _rel_class(cp):
    far = (cp < MAX_REL) | (cp > BAND)
    return jnp.where(far, 2 * MAX_REL, BAND - cp)


def _skew_rows(x, sign):
    row = lax.broadcasted_iota(jnp.int32, x.shape, 0)
    for b in range(CHUNK.bit_length() - 1):
        shift = (1 << b) if sign > 0 else SKEW_W - (1 << b)
        x = jnp.where((row >> b) & 1 == 1, pltpu.roll(x, shift, 1), x)
    return x


def _roll_lanes(x, shift):
    return x if shift % SKEW_W == 0 else pltpu.roll(x, shift % SKEW_W, 1)


N_START = 3


def _bias_expand(rel_bias, riders=()):
    rel = jnp.pad(rel_bias, ((0, 0), (0, REL_PAD - N_REL))).reshape(N_HEADS, 1, REL_PAD)

    def body(rel_ref, o_ref):
        cls = lax.broadcasted_iota(jnp.int32, (REL_PAD, SKEW_W), 0)
        cp = lax.broadcasted_iota(jnp.int32, (REL_PAD, SKEW_W), 1)
        onehot = (cls == _rel_class(cp)).astype(BF16)
        rel8 = jnp.broadcast_to(rel_ref[...], (8, REL_PAD))
        trow = sum(jnp.dot(p, onehot, preferred_element_type=F32) for p in _split3(rel8))[0:1]
        first = _skew_rows(jnp.broadcast_to(trow, (CHUNK, SKEW_W)), +1)
        full = jnp.concatenate([_roll_lanes(first, CHUNK * g) for g in range(SUPER // CHUNK)], axis=0)[:, :BAND]
        qc = lax.broadcasted_iota(jnp.int32, (SUPER, BAND), 0) // CHUNK
        kc = lax.broadcasted_iota(jnp.int32, (SUPER, BAND), 1) // CHUNK
        on_band = (kc >= qc) & (kc <= qc + N_PREV_CHUNKS)
        table = jnp.where(on_band, full, NEG_INF).T
        key = lax.broadcasted_iota(jnp.int32, (BAND, SUPER), 0)
        for t in range(N_START):
            o_ref[t] = jnp.where(key < (N_START - 1 - t) * SUPER, NEG_INF, table)

    return _carry(
        riders, body, name="bias_expand", grid=(N_HEADS,),
        in_specs=[pl.BlockSpec((None, 1, REL_PAD), lambda h: (h, 0, 0))],
        out_specs=pl.BlockSpec((N_START, None, BAND, SUPER), lambda h: (0, h, 0, 0)),
        out_shape=jax.ShapeDtypeStruct((N_START, N_HEADS, BAND, SUPER), F32),
        semantics=("arbitrary",),
    )(rel)


def _bias_reduce(dbias, riders=()):
    def body(d_ref, o_ref):
        x = jnp.concatenate([d_ref[...].T, jnp.zeros((SUPER, SKEW_W - BAND), F32)], axis=1)
        folded = sum(_roll_lanes(x[CHUNK * g:CHUNK * (g + 1)], -CHUNK * g) for g in range(SUPER // CHUNK))
        diag = jnp.sum(_skew_rows(folded, -1), axis=0, keepdims=True)
        cp = lax.broadcasted_iota(jnp.int32, (SKEW_W, REL_PAD), 0)
        cls = lax.broadcasted_iota(jnp.int32, (SKEW_W, REL_PAD), 1)
        onehot = (cls == _rel_class(cp)).astype(BF16)
        diag8 = jnp.broadcast_to(diag, (8, SKEW_W))
        o_ref[...] = sum(jnp.dot(p, onehot, preferred_element_type=F32) for p in _split3(diag8))[0:1]

    out = _carry(
        riders, body, name="bias_reduce", grid=(N_HEADS,),
        in_specs=[pl.BlockSpec((None, BAND, SUPER), lambda h: (h, 0, 0))],
        out_specs=pl.BlockSpec((None, 1, REL_PAD), lambda h: (h, 0, 0)),
        out_shape=jax.ShapeDtypeStruct((N_HEADS, 1, REL_PAD), F32),
        semantics=("arbitrary",),
    )(dbias)
    out, carried = out if riders else (out, None)
    out = out.reshape(N_HEADS, REL_PAD)[:, :N_REL]
    return (out, carried) if riders else out


HEADS_PER_STEP = 8
HEAD_COLS = HEADS_PER_STEP * HEAD_DIM
N_HEAD_GROUPS = N_HEADS // HEADS_PER_STEP


def _scores_t(qs, kn, bias_t):
    return jnp.concatenate([lax.dot_general(k, qs, NT, preferred_element_type=F32) for k in kn], axis=0) + bias_t


def _bias_spec():
    return pl.BlockSpec((None, HEADS_PER_STEP, BAND, SUPER), lambda hg, i: (jnp.minimum(i, N_START - 1), hg, 0, 0))


def _band_specs(nb, col0, clamp_hi):
    def spec(d):
        def index(hg, i):
            blk = jnp.maximum(i - d, 0)
            if clamp_hi:
                blk = jnp.minimum(blk, nb - 1)
            return (blk, col0 + hg)
        return pl.BlockSpec((SUPER, HEAD_COLS), index)
    return [spec(2), spec(1), spec(0)]


def _head_sums(y):
    same_head = (lax.broadcasted_iota(jnp.int32, (MXU_DIM, MXU_DIM), 0) // HEAD_DIM
                 == lax.broadcasted_iota(jnp.int32, (MXU_DIM, MXU_DIM), 1) // HEAD_DIM).astype(BF16)
    sums = []
    for c0 in range(0, y.shape[1], MXU_DIM):
        chunk = y[:, c0:c0 + MXU_DIM]
        hi = chunk.astype(BF16)
        lo = (chunk - hi.astype(F32)).astype(BF16)
        sums.append(jnp.dot(hi, same_head, preferred_element_type=F32)
                    + jnp.dot(lo, same_head, preferred_element_type=F32))
    return jnp.concatenate(sums, axis=1)


def _head_unit(x):
    r = lax.rsqrt(_head_sums(x * x) * (1.0 / HEAD_DIM) + EPS)
    return x * r, r


def _head(hh):
    return slice(HEAD_DIM * hh, HEAD_DIM * (hh + 1))


def _key_block(j):
    return slice(SUPER * j, SUPER * (j + 1))


LSE_ROWS = 8


def _attn_fwd(qkn, v, bias, riders=()):
    s_len = qkn.shape[0]
    nb = s_len // SUPER

    def body(q_ref, k0, k1, k2, v0, v1, v2, b_ref, o_ref, lse_ref):
        outs = []
        v_t = [v0[...].T, v1[...].T, v2[...].T]

        def probabilities(hh):
            sl = _head(hh)
            s = _scores_t(q_ref[:, sl], [k0[:, sl], k1[:, sl], k2[:, sl]], b_ref[hh])
            m = jnp.max(s, axis=0, keepdims=True)
            e = jnp.exp(s - m)
            l = jnp.sum(e, axis=0, keepdims=True)
            lse_ref[hh:hh + 1, :] = m + jnp.log(l)
            return (e * (1.0 / l)).astype(BF16), sl

        def weighted_values(p, sl):
            outs.append(sum(jnp.dot(v_t[j][sl, :], p[_key_block(j), :], preferred_element_type=F32)
                            for j in range(3)))

        ready = probabilities(0)
        for hh in range(1, HEADS_PER_STEP):
            following = probabilities(hh)
            weighted_values(*ready)
            ready = following
        weighted_values(*ready)
        o_ref[...] = jnp.concatenate(outs, axis=0).T.astype(o_ref.dtype)

    return _carry(
        riders, body, name="attn_fwd", grid=(N_HEAD_GROUPS, nb),
        in_specs=[pl.BlockSpec((SUPER, HEAD_COLS), lambda hg, i: (i, hg))]
        + _band_specs(nb, N_HEAD_GROUPS, False) + _band_specs(nb, 0, False) + [_bias_spec()],
        out_specs=[pl.BlockSpec((SUPER, HEAD_COLS), lambda hg, i: (i, hg)),
                   pl.BlockSpec((None, LSE_ROWS, SUPER), lambda hg, i: (hg, 0, i))],
        out_shape=[jax.ShapeDtypeStruct((s_len, D_MODEL), BF16),
                   jax.ShapeDtypeStruct((N_HEAD_GROUPS, LSE_ROWS, s_len), F32)],
        semantics=("parallel", "arbitrary"),
    )(qkn, qkn, qkn, qkn, v, v, v, bias)


def _attn_bwd(qkn, v, out, d_out, bias, lse, riders=()):
    s_len = qkn.shape[0]
    nb = s_len // SUPER

    def body(q_ref, k0, k1, k2, v0, v1, v2, o_ref, do_ref, b_ref, lse_ref, dp_ref, db_ref, aq_ref, ak_ref, av_ref):
        i = pl.program_id(1)

        @pl.when(i == 0)
        def _():
            aq_ref[...] = jnp.zeros_like(aq_ref)
            ak_ref[...] = jnp.zeros_like(ak_ref)
            av_ref[...] = jnp.zeros_like(av_ref)
            db_ref[...] = jnp.zeros_like(db_ref)

        @pl.when(i < nb)
        def _():
            dq, dk, dv = [], [[], [], []], [[], [], []]
            ones = jnp.ones((8, HEAD_DIM), BF16)
            k_t = [k0[...].T, k1[...].T, k2[...].T]

            def softmax_grad(hh):
                sl = _head(hh)
                qs, do = q_ref[:, sl], do_ref[:, sl]
                kn = [k0[:, sl], k1[:, sl], k2[:, sl]]
                prod = do.astype(F32) * o_ref[:, sl].astype(F32)
                hi = prod.astype(BF16)
                lo = (prod - hi.astype(F32)).astype(BF16)
                delta = (lax.dot_general(ones, hi, NT, preferred_element_type=F32)
                         + lax.dot_general(ones, lo, NT, preferred_element_type=F32))[0:1]
                lse_row = lse_ref[hh:hh + 1, :]
                pb, dsb = [], []
                for j, (kj, vj) in enumerate(zip(kn, (v0, v1, v2))):
                    rows = _key_block(j)
                    p = jnp.exp(lax.dot_general(kj, qs, NT, preferred_element_type=F32) + b_ref[hh, rows, :] - lse_row)
                    ds = p * (lax.dot_general(vj[:, sl], do, NT, preferred_element_type=F32) - delta)
                    db_ref[hh, rows, :] += ds
                    pb.append(p.astype(BF16))
                    dsb.append(ds.astype(BF16))
                return pb, dsb, qs, do, sl

            def operand_grads(pb, dsb, qs, do, sl):
                dq.append(sum(jnp.dot(k_t[j][sl, :], dsb[j], preferred_element_type=F32) for j in range(3)))
                for j in range(3):
                    dv[j].append(jnp.dot(pb[j], do, preferred_element_type=F32))
                    dk[j].append(jnp.dot(dsb[j], qs, preferred_element_type=F32))

            ready = softmax_grad(0)
            for hh in range(1, HEADS_PER_STEP):
                following = softmax_grad(hh)
                operand_grads(*ready)
                ready = following
            operand_grads(*ready)
            aq_ref[i % 3] = jnp.concatenate(dq, axis=0).T
            for j in range(3):
                slot = (i + 1 + j) % 3
                if j < 2:
                    ak_ref[slot] += jnp.concatenate(dk[j], axis=1)
                    av_ref[slot] += jnp.concatenate(dv[j], axis=1)
                else:
                    ak_ref[slot] = jnp.concatenate(dk[j], axis=1)
                    av_ref[slot] = jnp.concatenate(dv[j], axis=1)

        slot = (i + 1) % 3
        dp_ref[0] = aq_ref[slot].astype(dp_ref.dtype)
        dp_ref[1] = ak_ref[slot].astype(dp_ref.dtype)
        dp_ref[2] = av_ref[slot].astype(dp_ref.dtype)

    def qrow(hg, i):
        return (jnp.minimum(i, nb - 1), hg)

    return _carry(
        riders, body, name="attn_bwd", grid=(N_HEAD_GROUPS, nb + 2),
        in_specs=[pl.BlockSpec((SUPER, HEAD_COLS), qrow)]
        + _band_specs(nb, N_HEAD_GROUPS, True) + _band_specs(nb, 0, True)
        + [pl.BlockSpec((SUPER, HEAD_COLS), qrow), pl.BlockSpec((SUPER, HEAD_COLS), qrow), _bias_spec(),
           pl.BlockSpec((None, LSE_ROWS, SUPER), lambda hg, i: (hg, 0, jnp.minimum(i, nb - 1)))],
        out_specs=[pl.BlockSpec((3, SUPER, HEAD_COLS), lambda hg, i: (0, jnp.maximum(i - 2, 0), hg)),
                   pl.BlockSpec((HEADS_PER_STEP, BAND, SUPER), lambda hg, i: (hg, 0, 0))],
        out_shape=[jax.ShapeDtypeStruct((6, s_len, D_MODEL), BF16),
                   jax.ShapeDtypeStruct((N_HEADS, BAND, SUPER), F32)],
        scratch_shapes=[pltpu.VMEM((3, SUPER, HEAD_COLS), F32)] * 3,
        semantics=("parallel", "arbitrary"),
    )(qkn, qkn, qkn, qkn, v, v, v, out, d_out, bias, lse)


def _qk_norm_bwd(dproj6, qk_raw, gq, gk):
    s_len = qk_raw.shape[0]
    ts = min(1024, s_len)

    nsteps = s_len // ts
    half = D_MODEL // 2

    def body(d_ref, raw_ref, gq_ref, gk_ref, o_ref, dgq_ref, dgk_ref, acc_ref):
        step = pl.program_id(0)

        @pl.when(step == 0)
        def _():
            acc_ref[...] = jnp.zeros_like(acc_ref)

        for piece, (g_ref, scale) in enumerate(((gq_ref, QK_SCALE), (gk_ref, 1.0))):
            for c0 in (0, half):
                xhat, r = _head_unit(raw_ref[:, piece * D_MODEL + c0:piece * D_MODEL + c0 + half].astype(F32))
                dn = d_ref[piece, :, c0:c0 + half].astype(F32) * scale
                u = dn * g_ref[...]
                dx = r * (u - xhat * (_head_sums(u * xhat) * (1.0 / HEAD_DIM)))
                o_ref[piece, :, c0:c0 + half] = dx.astype(o_ref.dtype)
                acc_ref[piece:piece + 1, c0:c0 + half] += jnp.sum(dn * xhat, axis=0, keepdims=True)

        @pl.when(step == nsteps - 1)
        def _():
            lane = lax.broadcasted_iota(jnp.int32, (D_MODEL, LANE), 0) % HEAD_DIM
            fold = (lane == lax.broadcasted_iota(jnp.int32, (D_MODEL, LANE), 1)).astype(BF16)
            tot = sum(jnp.dot(p, fold, preferred_element_type=F32) for p in _split3(acc_ref[...]))
            dgq_ref[...] = tot[0:1, :HEAD_DIM]
            dgk_ref[...] = tot[1:2, :HEAD_DIM]

    gain = pl.BlockSpec((1, half), lambda i: (0, 0))
    small = pl.BlockSpec((1, HEAD_DIM), lambda i: (0, 0))
    per_head = lambda g: jnp.tile(g, (1, half // HEAD_DIM))
    return pl.pallas_call(
        body, name="qk_norm_bwd", grid=(nsteps,),
        in_specs=[pl.BlockSpec((2, ts, D_MODEL), lambda i: (0, i, 0)),
                  pl.BlockSpec((ts, 2 * D_MODEL), lambda i: (i, 0)), gain, gain],
        out_specs=[pl.BlockSpec((2, ts, D_MODEL), lambda i: (0, i, 0)), small, small],
        out_shape=[jax.ShapeDtypeStruct(dproj6.shape, dproj6.dtype),
                   jax.ShapeDtypeStruct((1, HEAD_DIM), F32), jax.ShapeDtypeStruct((1, HEAD_DIM), F32)],
        scratch_shapes=[pltpu.VMEM((8, D_MODEL), F32)],
        input_output_aliases={0: 0},
        compiler_params=_params("arbitrary"),
    )(dproj6, qk_raw, per_head(gq), per_head(gk))


CONV_ROWS = 512
HALO = 16


def _rows_with_halo(ref, r0, n, front, s_len):
    zeros = jnp.zeros((HALO, ref.shape[1]), F32)
    if front:
        return (jnp.concatenate([zeros, ref[0:n, :].astype(F32)], axis=0) if r0 == 0
                else ref[r0 - HALO:r0 + n, :].astype(F32))
    return (jnp.concatenate([ref[r0:r0 + n, :].astype(F32), zeros], axis=0) if r0 + n == s_len
            else ref[r0:r0 + n + HALO, :].astype(F32))


def _earlier(ext, k):
    return pltpu.roll(ext, k, 0)[HALO:]


def _later(ext, k):
    n = ext.shape[0]
    return pltpu.roll(ext, n - k, 0)[:n - HALO]


def _conv_cols(col0):
    return lambda s_len: pl.BlockSpec((s_len, LANE), lambda j: (0, col0 + j))


def _conv_fwd(proj, conv_w, conv_b, riders=()):
    s_len = proj.shape[0]

    def body(bg_ref, cg_ref, xc_ref, w_ref, b_ref, o_ref):
        w = [w_ref[t:t + 1, :] for t in range(3)]
        for r0 in range(0, s_len, CONV_ROWS):
            u = _rows_with_halo(cg_ref, r0, CONV_ROWS, True, s_len) * \
                _rows_with_halo(xc_ref, r0, CONV_ROWS, True, s_len)
            conv = b_ref[...] + w[0] * _earlier(u, 2) + w[1] * _earlier(u, 1) + w[2] * u[HALO:]
            o_ref[r0:r0 + CONV_ROWS, :] = (bg_ref[r0:r0 + CONV_ROWS, :].astype(F32) * conv).astype(o_ref.dtype)

    return _carry(
        riders, body, name="conv_fwd", grid=(D_MODEL // LANE,),
        in_specs=[_conv_cols(0)(s_len), _conv_cols(8)(s_len), _conv_cols(16)(s_len),
                  pl.BlockSpec((3, LANE), lambda j: (0, j)), pl.BlockSpec((1, LANE), lambda j: (0, j))],
        out_specs=pl.BlockSpec((s_len, LANE), lambda j: (0, j)),
        out_shape=jax.ShapeDtypeStruct((s_len, D_MODEL), BF16),
        semantics=("parallel",),
    )(proj, proj, proj, conv_w, conv_b)


def _conv_bwd(dproj6, dy, proj, conv_w, conv_b, riders=()):
    s_len = proj.shape[0]

    def body(dy_ref, bg_ref, cg_ref, xc_ref, w_ref, b_ref, _, dp_ref, dw_ref):
        w = [w_ref[t:t + 1, :] for t in range(3)]
        acc = [jnp.zeros((1, LANE), F32) for _ in range(4)]
        for r0 in range(0, s_len, CONV_ROWS):
            rows = slice(r0, r0 + CONV_ROWS)
            u = _rows_with_halo(cg_ref, r0, CONV_ROWS, True, s_len) * \
                _rows_with_halo(xc_ref, r0, CONV_ROWS, True, s_len)
            u2, u1, u0 = _earlier(u, 2), _earlier(u, 1), u[HALO:]
            conv = b_ref[...] + w[0] * u2 + w[1] * u1 + w[2] * u0
            dp_ref[0, rows, :] = (dy_ref[rows, :].astype(F32) * conv).astype(dp_ref.dtype)
            dconv_ext = _rows_with_halo(dy_ref, r0, CONV_ROWS, False, s_len) * \
                _rows_with_halo(bg_ref, r0, CONV_ROWS, False, s_len)
            dconv = dconv_ext[:CONV_ROWS]
            for t, term in enumerate([dconv * u2, dconv * u1, dconv * u0, dconv]):
                acc[t] = acc[t] + jnp.sum(term, axis=0, keepdims=True)
            du = w[2] * dconv + w[1] * _later(dconv_ext, 1) + w[0] * _later(dconv_ext, 2)
            dp_ref[1, rows, :] = (du * xc_ref[rows, :].astype(F32)).astype(dp_ref.dtype)
            dp_ref[2, rows, :] = (du * cg_ref[rows, :].astype(F32)).astype(dp_ref.dtype)
        dw_ref[...] = jnp.zeros_like(dw_ref)
        for t in range(4):
            dw_ref[t:t + 1, :] = acc[t]

    return _carry(
        riders, body, name="conv_bwd", grid=(D_MODEL // LANE,),
        in_specs=[pl.BlockSpec((s_len, LANE), lambda j: (0, j)),
                  _conv_cols(0)(s_len), _conv_cols(8)(s_len), _conv_cols(16)(s_len),
                  pl.BlockSpec((3, LANE), lambda j: (0, j)), pl.BlockSpec((1, LANE), lambda j: (0, j)),
                  pl.BlockSpec(memory_space=pl.ANY)],
        out_specs=[pl.BlockSpec((3, s_len, LANE), lambda j: (1, 0, j)),
                   pl.BlockSpec((8, LANE), lambda j: (0, j))],
        out_shape=[jax.ShapeDtypeStruct(dproj6.shape, dproj6.dtype),
                   jax.ShapeDtypeStruct((8, D_MODEL), F32)],
        input_output_aliases={6: 0},
        semantics=("parallel",),
    )(dy, proj, proj, proj, conv_w, conv_b, dproj6)


def _d_norm_input(name, terms, x, g, dres, out_dtype, riders=()):
    s_len, kdim = terms[0][0].shape[1], terms[0][1].shape[1]
    tm, chunk = 512, 512
    n = len(terms)

    def body(*refs):
        x_ref, g_ref, r_ref, o_ref, dg_ref = refs[2 * n:]
        dh = jnp.zeros((tm, kdim), F32)
        for p_ref, w_ref in zip(refs[:n], refs[n:2 * n]):
            n_piece, _, width = p_ref.shape
            per_shard = w_ref.shape[2]
            for c0 in range(0, n_piece * width, chunk):
                dh = dh + lax.dot_general(p_ref[c0 // width, :, c0 % width:c0 % width + chunk],
                                          w_ref[c0 // per_shard, :, c0 % per_shard:c0 % per_shard + chunk],
                                          NT, preferred_element_type=F32)
        xv = x_ref[...]
        r = lax.rsqrt(jnp.mean(xv * xv, axis=-1, keepdims=True) + EPS)
        xhat = xv * r
        u = dh * g_ref[...]
        dx = r * (u - xhat * jnp.mean(u * xhat, axis=-1, keepdims=True)) + r_ref[...].astype(F32)
        o_ref[...] = dx.astype(o_ref.dtype)
        part = jnp.sum(dh * xhat, axis=0, keepdims=True)

        @pl.when(pl.program_id(0) == 0)
        def _():
            dg_ref[...] = part

        @pl.when(pl.program_id(0) > 0)
        def _():
            dg_ref[...] += part

    tile = pl.BlockSpec((tm, kdim), lambda i: (i, 0))
    vec = pl.BlockSpec((1, kdim), lambda i: (0, 0))
    return _carry(
        riders, body, name=name, grid=(s_len // tm,),
        in_specs=[pl.BlockSpec((p.shape[0], tm, p.shape[2]), lambda i: (0, i, 0)) for p, _ in terms]
        + [pl.BlockSpec(w.shape, lambda i: (0, 0, 0), pipeline_mode=pl.Buffered(1)) for _, w in terms]
        + [tile, vec, tile],
        out_specs=[tile, vec],
        out_shape=[jax.ShapeDtypeStruct((s_len, kdim), out_dtype), jax.ShapeDtypeStruct((1, kdim), F32)],
        semantics=("arbitrary",),
    )(*[p for p, _ in terms], *[w for _, w in terms], x, g, dres)


def _local_grads(x, target, norm1_g, q_norm_g, k_norm_g, rel_bias, conv_b, b_gate, norm2_g, comm):
    s_len = x.shape[0]
    tm = min(1024, s_len)
    tb = min(2048, s_len)
    row = lambda v: v.reshape(1, -1)

    def carrying(at, fn, *args, **kw):
        riders = comm.riders(at)
        out = fn(*args, riders=riders, **kw)
        if riders:
            out, carried = out
            comm.done(at, carried)
        return out

    bias = carrying("bias_expand", _bias_expand, rel_bias)
    h = carrying("norm1", _rms_fwd, "norm1", x, row(norm1_g))
    w_in3, conv_w = comm.early_weights()
    gq, gk = row(q_norm_g), row(k_norm_g)

    def qk_epi(acc, e, o):
        o[0][...] = acc.astype(BF16)
        gain = jnp.where(pl.program_id(1) < 2, e[0] * QK_SCALE, e[1])
        o[1][...] = (_head_unit(acc)[0] * gain).astype(BF16)
    conv_in = carrying("proj_conv", _mm_fwd, "proj_conv", h, w_in3, tb, 1536, D_MODEL, col0=2, ncols=3 * D_MODEL,
                       outs=[_out2d(s_len, 3 * D_MODEL, BF16, tb, 1536)])
    small = pl.BlockSpec((1, 512), lambda i, j, k: (0, 0))
    per_head = lambda g: jnp.tile(g, (1, 512 // HEAD_DIM))
    qk_raw, qkn = carrying("proj_qk", _mm_fwd, "proj_qk", h, w_in3, tm, 512, D_MODEL, ncols=2 * D_MODEL,
                           epilogue=qk_epi, outs=[_out2d(s_len, 2 * D_MODEL, BF16, tm, 512)] * 2,
                           extras=[(per_head(gq), small), (per_head(gk), small)])
    v = carrying("proj_v", _mm_fwd, "proj_v", h, w_in3, tb, 512, D_MODEL, col0=4, ncols=D_MODEL,
                 outs=[_out2d(s_len, D_MODEL, BF16, tb, 512)])
    w_gate3 = comm.gate_weight()

    def gate_epi(acc, e, o):
        o[0][...] = jax.nn.sigmoid(acc + e[0]).astype(BF16)
    gates = carrying("gates", _mm_fwd, "gates", h, w_gate3, tb, 512, D_MODEL, epilogue=gate_epi,
                     outs=[_out2d(s_len, 2 * D_MODEL, BF16, tb, 512)],
                     extras=[(row(b_gate), pl.BlockSpec((1, 512), lambda i, j, k: (0, j)))])

    attn, lse = carrying("attn_fwd", _attn_fwd, qkn, v, bias)
    yconv = carrying("conv_fwd", _conv_fwd, conv_in, conv_w, row(conv_b))
    w_ap, w_cp, w_out, w_up3, w_down = comm.late_weights()
    tw = 1024
    ya = _mm_fwd("attn_proj", attn, w_ap, tm, tw, D_MODEL, outs=[_out2d(s_len, D_MODEL, BF16, tm, tw)])

    def merge_epi(acc, e, o):
        ya_v, ga, gc = [t.astype(F32) for t in e]
        o[0][...] = acc.astype(BF16)
        o[1][...] = (ga * ya_v + gc * acc).astype(BF16)
    gate_a, gate_c = (gates, _tile_spec(tm, tw, 0)), (gates, _tile_spec(tm, tw, 1))
    yc, merged = _mm_fwd("conv_proj", yconv, w_cp, tm, tw, D_MODEL, epilogue=merge_epi,
                         outs=[_out2d(s_len, D_MODEL, BF16, tm, tw), _out2d(s_len, D_MODEL, BF16, tm, tw)],
                         extras=[(ya, _tile_spec(tm, tw)), gate_a, gate_c])

    def res_epi(acc, e, o):
        o[0][...] = e[0] + acc
    x1 = _mm_fwd("out_proj", merged, w_out, tm, tw, D_MODEL, epilogue=res_epi,
                 extras=[(x, _tile_spec(tm, tw))])
    h2 = _rms_fwd("norm2", x1, row(norm2_g))

    def up_epi(acc, e, o):
        o[0][...] = jnp.square(jnp.maximum(acc, 0.0)).astype(BF16)
    act = _mm_fwd("mlp_up", h2, w_up3, tb, tw, D_MODEL, epilogue=up_epi, outs=[_out2d(s_len, D_FF, BF16, tb, tw)])

    def loss_epi(acc, e, o):
        err = e[0] + acc - e[1]
        o[0][...] = (err * (1.0 / D_MODEL)).astype(BF16)
        sq = err * err
        part = sq[:, 0:LANE]
        for c0 in range(LANE, D_MODEL, LANE):
            part = part + sq[:, c0:c0 + LANE]
        o[1][...] = jnp.sum(part.reshape(tl // 8, 8, LANE), axis=0)
    tl = 512
    dy_b, loss_part = _mm_fwd(
        "mlp_down", act, w_down, tl, D_MODEL, D_FF, epilogue=loss_epi,
        outs=[_out2d(s_len, D_MODEL, BF16, tl, D_MODEL),
              (jax.ShapeDtypeStruct((8 * (s_len // tl), LANE), F32), pl.BlockSpec((8, LANE), lambda i, j, k: (i, 0)))],
        extras=[(x1, _tile_spec(tl, D_MODEL)), (target, _tile_spec(tl, D_MODEL))])

    def dup_epi(acc, e, o):
        o[0][...] = (acc * (2.0 * jnp.sqrt(e[0].astype(F32)))).astype(BF16)
    full = lambda cols: pl.BlockSpec((tm, cols), lambda i, j, n: (i, n))
    tokens = lambda cols: pl.BlockSpec((s_len, cols), lambda i, j, m: (m, j))
    dup = _mm_bwd_x("d_act", dy_b, full(D_MODEL), w_down, tm, tw, D_MODEL, s_len, D_MODEL, epilogue=dup_epi,
                    outs=[_out2d(s_len, D_FF, BF16, tm, tw)], extras=[(act, _tile_spec(tm, tw))])
    g_down = _mm_bwd_w("g_down", act, dy_b, tokens(D_MODEL), D_MODEL, 512, D_MODEL, s_len, False)
    g_up = _mm_bwd_w("g_up", h2, dup, tokens(512), D_FF, D_MODEL, 512, s_len, True)
    comm.grads_ready("mlp", dict(w_down=g_down, w_up=g_up))
    dx1_b, dg2 = carrying("d_h2", _d_norm_input, "d_h2", [(dup.reshape(1, s_len, D_FF), w_up3)],
                          x1, row(norm2_g), dy_b, BF16)

    def dmerge_epi(acc, e, o):
        ya_v, yc_v, ga, gc = [t.astype(F32) for t in e]
        o[0][...] = (acc * ga).astype(BF16)
        o[1][...] = (acc * gc).astype(BF16)
        o[2][0] = (acc * ya_v * ga * (1.0 - ga)).astype(BF16)
        o[2][1] = (acc * yc_v * gc * (1.0 - gc)).astype(BF16)
    dya, dyc, dgp2 = _mm_bwd_x(
        "d_merged", dx1_b, full(D_MODEL), w_out, tm, tw, D_MODEL, s_len, D_MODEL, epilogue=dmerge_epi,
        outs=[_out2d(s_len, D_MODEL, BF16, tm, tw), _out2d(s_len, D_MODEL, BF16, tm, tw),
              (jax.ShapeDtypeStruct((2, s_len, D_MODEL), BF16), pl.BlockSpec((2, tm, tw), lambda i, j, n: (0, i, j)))],
        extras=[(ya, _tile_spec(tm, tw)), (yc, _tile_spec(tm, tw)), gate_a, gate_c])
    g_out = _mm_bwd_w("g_out", merged, dx1_b, tokens(512), D_MODEL, D_MODEL, 512, s_len, False)
    d_attn = _mm_bwd_x("d_attn", dya, full(D_MODEL), w_ap, tm, tw, D_MODEL, s_len, D_MODEL,
                       outs=[_out2d(s_len, D_MODEL, BF16, tm, tw)])
    g_ap = _mm_bwd_w("g_attn_proj", attn, dya, tokens(512), D_MODEL, D_MODEL, 512, s_len, False)
    d_yconv = _mm_bwd_x("d_yconv", dyc, full(D_MODEL), w_cp, tm, tw, D_MODEL, s_len, D_MODEL,
                        outs=[_out2d(s_len, D_MODEL, BF16, tm, tw)])
    g_cp = _mm_bwd_w("g_conv_proj", yconv, dyc, tokens(512), D_MODEL, D_MODEL, 512, s_len, False)
    piece = lambda width: (lambda blk: (blk * width) // D_MODEL, lambda blk: (blk * width % D_MODEL) // width)
    pc, cb = piece(512)
    pieces = pl.BlockSpec((None, s_len, 512), lambda i, j, m: (pc(j), m, cb(j)))
    g_gate = _mm_bwd_w("g_gate", h, dgp2, pieces, 2 * D_MODEL, D_MODEL, 512, s_len, True)
    comm.grads_ready("proj", dict(w_out=g_out, w_attn_proj=g_ap, w_conv_proj=g_cp, w_gate=g_gate))

    dproj6, dbias = carrying("attn_bwd", _attn_bwd, qkn, v, attn, d_attn, bias, lse)
    dproj6, dgq, dgk = _qk_norm_bwd(dproj6, qk_raw, gq, gk)
    dproj6, dconv_wb = carrying("conv_bwd", _conv_bwd, dproj6, d_yconv, conv_in, conv_w, row(conv_b))

    g_in = carrying("g_in", _mm_bwd_w, "g_in", h, dproj6, pieces, 6 * D_MODEL, D_MODEL, 512, s_len, True)
    comm.grads_ready("in", dict(w_in=g_in))
    d_rel = carrying("bias_reduce", _bias_reduce, dbias)
    grad_x, dg1 = carrying("d_h", _d_norm_input, "d_h", [(dproj6, w_in3), (dgp2, w_gate3)],
                           x, row(norm1_g), dx1_b, F32)

    def bsum(t, f):
        return [], [t[0].astype(F32), t[1].astype(F32)]
    ts = 512
    db_a, db_c = carrying("b_gate_sum", _ew, "b_gate_sum", bsum,
                          [(dgp2, pl.BlockSpec((None, ts, D_MODEL), lambda i: (0, i, 0))),
                           (dgp2, pl.BlockSpec((None, ts, D_MODEL), lambda i: (1, i, 0)))],
                          [], [], sums=[D_MODEL, D_MODEL], ts=ts)

    big = dict(w_in=g_in, w_attn_proj=g_ap, w_conv_proj=g_cp, w_gate=g_gate, w_out=g_out,
               w_up=g_up, w_down=g_down)
    small = dict(norm1_g=dg1, norm2_g=dg2, conv_wb=dconv_wb, b_gate=(db_a, db_c),
                 q_norm_g=dgq, k_norm_g=dgk, rel_bias=d_rel)
    return loss_part, grad_x, big, small


def _finish_loss(loss_part):
    def body(l_ref, lo_ref):
        total = jnp.sum(jnp.sum(l_ref[...], axis=0, keepdims=True), axis=1, keepdims=True)
        lo_ref[...] = jnp.broadcast_to(total * (0.5 / D_MODEL), lo_ref.shape)

    return pl.pallas_call(body, name="finish_loss", out_shape=jax.ShapeDtypeStruct((8, LANE), F32))(loss_part)


def _place():
    return lax.axis_index("x"), lax.axis_index("y"), lax.axis_index("c")


def _other_chips(x, y):
    return [(1 - x, y), (x, 1 - y), (1 - x, 1 - y)]


def _cast_into_slot(name, where, w):
    r, cols = w.shape
    ts = 256

    def body(w_ref, x_ref, o_ref):
        o_ref[...] = x_ref[...].astype(o_ref.dtype)

    return pl.pallas_call(
        body, name=name,
        grid_spec=pltpu.PrefetchScalarGridSpec(
            num_scalar_prefetch=1, grid=(r // ts,),
            in_specs=[pl.BlockSpec((ts, cols), lambda i, w: (i, 0))],
            out_specs=pl.BlockSpec((None, ts, cols), lambda i, w: (w[0], i, 0))),
        out_shape=jax.ShapeDtypeStruct((N_SHARD, r, cols), BF16),
        compiler_params=_params("parallel"),
    )(where, w)


def _remote(src, dst, send, recv, k, to):
    return pltpu.make_async_remote_copy(src_ref=src, dst_ref=dst, send_sem=send.at[k], recv_sem=recv.at[k],
                                        device_id=to, device_id_type=MESH)


def _gather_riders(slots):
    n = len(slots)

    def copy(refs, w, j, shard, which, send, recv, to):
        hr = slots[w].shape[1] // 2
        ref = refs[w].at[shard, pl.ds(which * hr, hr)]
        return _remote(ref, ref, send, recv, 3 * w + j, to)

    def for_each_peer(fn):
        x, y, c = _place()
        for w in range(n):
            for j, chip in enumerate(_other_chips(x, y)):
                fn(w, j, 2 * x + y, 2 * chip[0] + chip[1], c, (*chip, c), (x, y, 1 - c))

    def chips_start(_, refs, send, recv):
        for_each_peer(lambda w, j, mine, theirs, c, peer, sib: copy(refs, w, j, mine, c, send, recv, peer).start())

    def chips_finish(_, refs, send, recv):
        for_each_peer(lambda w, j, mine, theirs, c, peer, sib: copy(refs, w, j, theirs, c, send, recv, peer).wait_recv())
        for_each_peer(lambda w, j, mine, theirs, c, peer, sib: copy(refs, w, j, mine, c, send, recv, peer).wait_send())

    def sibling_start(_, refs, send, recv):
        for_each_peer(lambda w, j, mine, theirs, c, peer, sib: copy(refs, w, j, theirs, c, send, recv, sib).start())

    def sibling_finish(_, refs, send, recv):
        for_each_peer(lambda w, j, mine, theirs, c, peer, sib: copy(refs, w, j, theirs, 1 - c, send, recv, sib).wait_recv())
        for_each_peer(lambda w, j, mine, theirs, c, peer, sib: copy(refs, w, j, theirs, c, send, recv, sib).wait_send())

    return (lambda arrays: _Rider([], arrays, 3 * n, chips_start, chips_finish),
            lambda arrays: _Rider([], arrays, 3 * n, sibling_start, sibling_finish))


def _pair_exchange_rider(grads, landing):
    n = len(grads)

    def copies(srcs, dsts, send, recv):
        x, y, c = _place()
        out = []
        for w in range(n):
            hr = grads[w].shape[1] // 2
            out.append(_remote(srcs[w].at[:, pl.ds((1 - c) * hr, hr)], dsts[w], send, recv, w, (x, y, 1 - c)))
        return out

    def start(srcs, dsts, send, recv):
        for cp in copies(srcs, dsts, send, recv):
            cp.start()

    def finish(srcs, dsts, send, recv):
        for cp in copies(srcs, dsts, send, recv):
            cp.wait()

    return _Rider(grads, landing, n, start, finish)


def _row_tile(hr):
    return min(hr, 256)


def _pair_add(name, where, grad, got):
    _, hr, cols = got.shape
    tr = _row_tile(hr)
    nblk = hr // tr

    def body(w_ref, g_ref, r_ref, o_ref):
        o_ref[...] = (g_ref[...] + r_ref[...]).astype(o_ref.dtype)

    other = lambda k, w: (w[0] + 1 + k) % N_SHARD
    return pl.pallas_call(
        body, name=name,
        grid_spec=pltpu.PrefetchScalarGridSpec(
            num_scalar_prefetch=1, grid=(N_SHARD - 1, nblk),
            in_specs=[pl.BlockSpec((None, tr, cols), lambda k, i, w: (other(k, w), w[1] * nblk + i, 0)),
                      pl.BlockSpec((None, tr, cols), lambda k, i, w: (other(k, w), i, 0))],
            out_specs=pl.BlockSpec((None, tr, cols), lambda k, i, w: (other(k, w), i, 0))),
        out_shape=jax.ShapeDtypeStruct(got.shape, BF16),
        compiler_params=_params("parallel", "parallel"),
    )(where, grad, got)


def _chip_exchange_rider(partials, landing):
    n = len(partials)

    def copies(srcs, dsts, send, recv):
        x, y, c = _place()
        return [_remote(srcs[w].at[2 * chip[0] + chip[1]], dsts[w].at[j], send, recv, 3 * w + j, (*chip, c))
                for w in range(n) for j, chip in enumerate(_other_chips(x, y))]

    def start(srcs, dsts, send, recv):
        for cp in copies(srcs, dsts, send, recv):
            cp.start()

    def finish(srcs, dsts, send, recv):
        for cp in copies(srcs, dsts, send, recv):
            cp.wait()

    return _Rider(partials, landing, 3 * n, start, finish)


def _final_add(name, where, grad, got, arrived):
    _, hr, cols = got.shape
    tr = _row_tile(hr)
    nblk = hr // tr

    def body(w_ref, g_ref, r_ref, a_ref, o_ref):
        acc = g_ref[...] + r_ref[...]
        for j in range(3):
            acc = acc + a_ref[j].astype(F32)
        o_ref[...] = acc

    return pl.pallas_call(
        body, name=name,
        grid_spec=pltpu.PrefetchScalarGridSpec(
            num_scalar_prefetch=1, grid=(nblk,),
            in_specs=[pl.BlockSpec((None, tr, cols), lambda i, w: (w[0], w[1] * nblk + i, 0)),
                      pl.BlockSpec((None, tr, cols), lambda i, w: (w[0], i, 0)),
                      pl.BlockSpec((3, tr, cols), lambda i, w: (0, i, 0))],
            out_specs=pl.BlockSpec((tr, cols), lambda i, w: (w[1] * nblk + i, 0))),
        out_shape=jax.ShapeDtypeStruct((2 * hr, cols), F32),
        compiler_params=_params("parallel"),
    )(where, grad, got, arrived)


def _pair_share_rider(shards):
    n = len(shards)

    def half(refs, w, which):
        hr = shards[w].shape[0] // 2
        return refs[w].at[pl.ds(which * hr, hr)]

    def start(_, refs, send, recv):
        x, y, c = _place()
        for w in range(n):
            _remote(half(refs, w, c), half(refs, w, c), send, recv, w, (x, y, 1 - c)).start()

    def finish(_, refs, send, recv):
        x, y, c = _place()
        for w in range(n):
            _remote(half(refs, w, 1 - c), half(refs, w, 1 - c), send, recv, w, (x, y, 1 - c)).wait_recv()
        for w in range(n):
            _remote(half(refs, w, c), half(refs, w, c), send, recv, w, (x, y, 1 - c)).wait_send()

    return _Rider([], shards, n, start, finish)


class _Exchange:
    PLAN = {"bias_expand": [("early", "gather")], "norm1": [("early", "forward")],
            "proj_conv": [("gate", "gather")], "proj_v": [("gate", "forward")],
            "proj_qk": [("late", "gather")], "gates": [("late", "forward")],
            "attn_fwd": [("mlp_w", "gather")], "conv_fwd": [("mlp_w", "forward")],
            "d_h2": [("mlp", "pair")], "attn_bwd": [("mlp", "chips"), ("proj", "pair")], "conv_bwd": [("mlp", "share")],
            "g_in": [("proj", "chips")], "bias_reduce": [("proj", "share"), ("in", "pair")],
            "d_h": [("in", "chips")], "b_gate_sum": [("in", "share")]}

    def __init__(self, where, early_slots, gate_slots, late_slots):
        self.where = where
        self.slots = dict(early=early_slots, gate=gate_slots, late=late_slots[:3], mlp_w=late_slots[3:])
        self.stage = {g: dict(zip(("gather", "forward"), _gather_riders(s))) for g, s in self.slots.items()}
        self.groups, self.reduced, self.pending = {}, {}, []

    def early_weights(self):
        w_in3, small = self.slots["early"]
        conv_w = small[:, :3, :].transpose(1, 0, 2).reshape(3, N_SHARD * small.shape[2])
        return w_in3, conv_w

    def gate_weight(self):
        return self.slots["gate"][0]

    def late_weights(self):
        rows = lambda a: a.reshape(a.shape[0] * a.shape[1], a.shape[2])
        w_ap, w_cp, w_out = self.slots["late"]
        w_up3, w_down = self.slots["mlp_w"]
        return rows(w_ap), rows(w_cp), rows(w_out), w_up3, rows(w_down)

    def grads_ready(self, group, grads):
        names = list(grads)
        g4 = [g if g.ndim == 3 else g.reshape(N_SHARD, -1, g.shape[1]) for g in grads.values()]
        self.groups[group] = dict(names=names, g4=g4)

    def riders(self, at):
        self.pending = self.PLAN.get(at, [])
        out = []
        for group, stage in self.pending:
            if group in self.slots:
                out.append(self.stage[group][stage](self.slots[group]))
                continue
            st = self.groups[group]
            if stage == "pair":
                landing = [lax.empty((N_SHARD, g.shape[1] // 2, g.shape[2]), F32) for g in st["g4"]]
                out.append(_pair_exchange_rider(st["g4"], landing))
            elif stage == "chips":
                landing = [lax.empty((3,) + p.shape[1:], p.dtype) for p in st["partial"]]
                out.append(_chip_exchange_rider(st["partial"], landing))
            else:
                out.append(_pair_share_rider(st["halves"]))
        return out

    def done(self, at, carried):
        for (group, stage), arrays in zip(self.pending, carried):
            if group in self.slots:
                self.slots[group] = arrays
                continue
            st = self.groups[group]
            tag = lambda what, n: what + "_" + n
            if stage == "pair":
                st["got"] = arrays
                st["partial"] = [_pair_add(tag("pair_add", n), self.where, g, r)
                                 for n, g, r in zip(st["names"], st["g4"], arrays)]
            elif stage == "chips":
                st["halves"] = [_final_add(tag("final_add", n), self.where, g, r, a)
                                for n, g, r, a in zip(st["names"], st["g4"], st["got"], arrays)]
            else:
                self.reduced.update(zip(st["names"], arrays))


SMALL_ROWS = 32
N_DEV = 8


def _all_reduce_small(pack):
    def body(p_ref, o_ref, buf, send, recv):
        x, y, c = _place()
        buf[4 * x + 2 * y + c] = p_ref[...]
        copies, waits = [], []
        for k in range(1, N_DEV):
            px = 1 - x if k & 4 else x
            py = 1 - y if k & 2 else y
            pc = 1 - c if k & 1 else c
            copies.append(_remote(p_ref, buf.at[4 * x + 2 * y + c], send, recv, k - 1, (px, py, pc)))
            waits.append(_remote(p_ref, buf.at[4 * px + 2 * py + pc], send, recv, k - 1, (px, py, pc)))
        for cp in copies:
            cp.start()
        for cp in waits:
            cp.wait_recv()
        acc = buf[0]
        for d in range(1, N_DEV):
            acc = acc + buf[d]
        o_ref[...] = acc
        for cp in copies:
            cp.wait_send()

    return pl.pallas_call(
        body, name="all_reduce_small",
        out_shape=jax.ShapeDtypeStruct(pack.shape, F32),
        scratch_shapes=[pltpu.VMEM((N_DEV,) + pack.shape, F32),
                        pltpu.SemaphoreType.DMA((N_DEV - 1,)), pltpu.SemaphoreType.DMA((N_DEV - 1,))],
    )(pack)


def _adamw(name, w, g, m, v, riders=()):
    c1 = 1.0 - ADAM_B1 ** ADAM_STEP
    c2 = 1.0 - ADAM_B2 ** ADAM_STEP

    def fn(t, f):
        wv, gv, mv, vv = t
        m2 = ADAM_B1 * mv + (1.0 - ADAM_B1) * gv
        v2 = ADAM_B2 * vv + (1.0 - ADAM_B2) * (gv * gv)
        delta = -ADAM_LR * ((m2 / c1) / (jnp.sqrt(v2 / c2) + ADAM_EPS) + ADAM_WD * wv)
        return [delta, m2, v2], []

    cols = w.shape[1]
    return _ew(name, fn, [w, g, m, v], [], [(cols, F32)] * 3, ts=min(w.shape[0], 256), riders=riders)


LOSS_ROW = 26


def _pack_small(norm1_g, norm2_g, conv_b, b_gate, conv_w, q_norm_g, k_norm_g, rel_bias, loss=None):
    pack = jnp.zeros((SMALL_ROWS, D_MODEL), F32)
    for r0, v in ((0, norm1_g), (1, norm2_g), (2, conv_b), (3, b_gate.reshape(2, D_MODEL)), (5, conv_w),
                  (8, q_norm_g), (9, k_norm_g), (10, rel_bias)) + (((LOSS_ROW, loss),) if loss is not None else ()):
        v = v.reshape(-1, v.shape[-1]).astype(F32)
        pack = pack.at[r0:r0 + v.shape[0], :v.shape[1]].set(v)
    return pack


def _unpack_small(pack, conv_cols):
    return dict(norm1_g=pack[0], norm2_g=pack[1], conv_b=pack[2], b_gate=pack[3:5].reshape(2 * D_MODEL),
                conv_w=pack[5:8, :conv_cols], q_norm_g=pack[8, :HEAD_DIM], k_norm_g=pack[9, :HEAD_DIM],
                rel_bias=pack[10:10 + N_HEADS, :N_REL])


BIG = ["w_in", "w_attn_proj", "w_conv_proj", "w_gate", "w_out", "w_up", "w_down"]
LATE = ["w_attn_proj", "w_conv_proj", "w_out", "w_up", "w_down"]
WEIGHTS = ["norm1_g", "w_in", "q_norm_g", "k_norm_g", "rel_bias", "conv_w", "conv_b", "w_attn_proj",
           "w_conv_proj", "w_gate", "b_gate", "w_out", "norm2_g", "w_up", "w_down"]


def kernel(x, norm1_g, w_in, q_norm_g, k_norm_g, rel_bias, conv_w, conv_b, w_attn_proj, w_conv_proj, w_gate, b_gate, w_out, norm2_g, w_up, w_down, loss_target, m_norm1_g, m_w_in, m_q_norm_g, m_k_norm_g, m_rel_bias, m_conv_w, m_conv_b, m_w_attn_proj, m_w_conv_proj, m_w_gate, m_b_gate, m_w_out, m_norm2_g, m_w_up, m_w_down, v_norm1_g, v_w_in, v_q_norm_g, v_k_norm_g, v_rel_bias, v_conv_w, v_conv_b, v_w_attn_proj, v_w_conv_proj, v_w_gate, v_b_gate, v_w_out, v_norm2_g, v_w_up, v_w_down):
    given = dict(locals())
    w = {n: given[n] for n in WEIGHTS}
    m = {n: given["m_" + n] for n in WEIGHTS}
    v = {n: given["v_" + n] for n in WEIGHTS}
    s_len = x.shape[1]
    shard = 2 * lax.axis_index("x") + lax.axis_index("y")
    where = jnp.stack([shard, lax.axis_index("c")]).astype(jnp.int32)
    conv_cols = conv_w.shape[1]

    small_in = lax.dynamic_update_slice(jnp.zeros((N_SHARD, 16, conv_cols), F32), conv_w[None], (shard, 0, 0))
    slot = {n: _cast_into_slot("cast_" + n, where, w[n]) for n in BIG}
    comm = _Exchange(where, [slot["w_in"], small_in], [slot["w_gate"]], [slot[n] for n in LATE])

    loss_part, grad_x, _, small = _local_grads(
        x.reshape(s_len, D_MODEL), loss_target.reshape(s_len, D_MODEL), norm1_g, q_norm_g, k_norm_g,
        rel_bias, conv_b, b_gate, norm2_g, comm)
    grad = dict(comm.reduced)

    loss_local = _finish_loss(loss_part)
    pack = _pack_small(small["norm1_g"], small["norm2_g"], small["conv_wb"][3], jnp.concatenate(small["b_gate"], axis=1),
                       small["conv_wb"][0:3], small["q_norm_g"], small["k_norm_g"], small["rel_bias"],
                       loss=loss_local[0:1, :])
    total = _all_reduce_small(pack)
    g_small = _unpack_small(total, D_MODEL)
    g_small["conv_w"] = lax.dynamic_slice(g_small["conv_w"], (0, shard * conv_cols), (3, conv_cols))
    grad.update(g_small)

    delta, new_m, new_v = {}, {}, {}
    for n in BIG:
        delta[n], new_m[n], new_v[n] = _adamw("adamw_" + n, w[n], grad[n], m[n], v[n])
    small_names = [n for n in WEIGHTS if n not in BIG]
    packs = [_pack_small(**{n: src[n] for n in small_names}) for src in (w, grad, m, v)]
    for out, packed in zip((delta, new_m, new_v), _adamw("adamw_small", *packs)):
        out.update({n: a.reshape(w[n].shape) for n, a in _unpack_small(packed, conv_cols).items()})

    outs = [total[LOSS_ROW, 0], grad_x.reshape(x.shape)]
    for group in (grad, delta, new_m, new_v):
        outs += [group[n].reshape(w[n].shape) for n in WEIGHTS]
    return tuple(outs)
```

```python
import jax
import jax.numpy as jnp
from jax import lax
from jax.experimental import pallas as pl
from jax.experimental.pallas import tpu as pltpu

F32 = jnp.float32
BF16 = jnp.bfloat16

D_MODEL = 1024
N_HEADS = 16
HEAD_DIM = 64
CHUNK = 64
N_PREV_CHUNKS = 8
MAX_REL = 256
D_FF = 4096
N_REL = 2 * MAX_REL + 1
REL_PAD = 640
EPS = 1e-6
NEG_INF = -1e30
QK_SCALE = HEAD_DIM ** -0.5

SUPER = 4 * CHUNK
BAND = SUPER + N_PREV_CHUNKS * CHUNK
SKEW_W = 1024
N_SHARD = 4
LANE = 128
MXU_DIM = 256
VMEM_LIMIT = 48 * 1024 * 1024

ADAM_LR = 0.001
ADAM_B1 = 0.9
ADAM_B2 = 0.999
ADAM_EPS = 1e-08
ADAM_WD = 0.01
ADAM_STEP = 10

MESH = pl.DeviceIdType.MESH
NN = (((1,), (0,)), ((), ()))
NT = (((1,), (1,)), ((), ()))
TN = (((0,), (0,)), ((), ()))


def _params(*sem):
    return pltpu.CompilerParams(dimension_semantics=sem or None, vmem_limit_bytes=VMEM_LIMIT)


HBM_SPEC = pl.BlockSpec(memory_space=pl.ANY)


class _Rider:
    def __init__(self, sources, arrays, n_sem, start, finish):
        self.sources, self.arrays, self.n_sem, self.start, self.finish = sources, arrays, n_sem, start, finish


def _carry(riders, body, *, name, out_shape, grid=(), in_specs=None, out_specs=None, scratch_shapes=(),
           semantics=(), input_output_aliases=None):
    aliases = dict(input_output_aliases or {})
    if not riders:
        kw = {} if in_specs is None else dict(in_specs=in_specs, out_specs=out_specs)
        return pl.pallas_call(body, name=name, grid=grid, out_shape=out_shape, scratch_shapes=scratch_shapes,
                              input_output_aliases=aliases, compiler_params=_params(*semantics), **kw)
    single = not isinstance(out_shape, (list, tuple))
    shapes = [out_shape] if single else list(out_shape)
    n_out, n_scr = len(shapes), len(scratch_shapes)
    in_hbm = lambda a: pltpu.with_memory_space_constraint(a, pltpu.HBM)
    srcs = [in_hbm(a) for r in riders for a in r.sources]
    arrs = [in_hbm(a) for r in riders for a in r.arrays]
    vmem = pl.BlockSpec(memory_space=pltpu.VMEM)

    def run(*args):
        n_in = len(args)

        def wrapped(*refs):
            pos = n_in
            src_refs = refs[pos:pos + len(srcs)]
            pos += len(srcs) + len(arrs)
            outs = refs[pos:pos + n_out]
            pos += n_out
            arr_refs = refs[pos:pos + len(arrs)]
            pos += len(arrs)
            scratch = refs[pos:pos + n_scr]
            sems = refs[pos + n_scr:]
            first, last = True, True
            for d, size in enumerate(grid):
                first = jnp.logical_and(first, pl.program_id(d) == 0)
                last = jnp.logical_and(last, pl.program_id(d) == size - 1)

            def each(method):
                s0 = a0 = 0
                for k, r in enumerate(riders):
                    getattr(r, method)(src_refs[s0:s0 + len(r.sources)], arr_refs[a0:a0 + len(r.arrays)],
                                       sems[2 * k], sems[2 * k + 1])
                    s0, a0 = s0 + len(r.sources), a0 + len(r.arrays)

            pl.when(first)(lambda: each("start"))
            body(*refs[:n_in], *outs, *scratch)
            pl.when(last)(lambda: each("finish"))

        ins = [vmem] * n_in if in_specs is None else list(in_specs)
        if out_specs is None:
            o_specs = [vmem] * n_out
        else:
            o_specs = [out_specs] if single else list(out_specs)
        for k in range(len(arrs)):
            aliases[n_in + len(srcs) + k] = n_out + k
        res = pl.pallas_call(
            wrapped, name=name, grid=grid,
            in_specs=ins + [HBM_SPEC] * (len(srcs) + len(arrs)),
            out_specs=o_specs + [HBM_SPEC] * len(arrs),
            out_shape=shapes + [jax.ShapeDtypeStruct(a.shape, a.dtype) for a in arrs],
            scratch_shapes=list(scratch_shapes) + [pltpu.SemaphoreType.DMA((r.n_sem,)) for r in riders for _ in range(2)],
            input_output_aliases=aliases,
            compiler_params=_params(*["arbitrary"] * len(grid)),
        )(*args, *srcs, *arrs)
        core, rest = res[:n_out], list(res[n_out:])
        carried, a0 = [], 0
        for r in riders:
            carried.append(rest[a0:a0 + len(r.arrays)])
            a0 += len(r.arrays)
        return (core[0] if single else core), carried

    return run


def _mm(name, dims, a, a_spec, b, b_spec, grid, tile, outs, epilogue=None, extras=(), riders=()):
    nk, ne, no = grid[2], len(extras), len(outs)

    def body(a_ref, b_ref, *refs):
        e_refs, o_refs = refs[:ne], refs[ne:ne + no]
        part = lax.dot_general(a_ref[...], b_ref[...], dims, preferred_element_type=F32)

        def finish(acc):
            if epilogue is None:
                o_refs[0][...] = acc.astype(o_refs[0].dtype)
            else:
                epilogue(acc, [r[...] for r in e_refs], o_refs)

        if nk == 1:
            finish(part)
        else:
            acc_ref = refs[ne + no]
            k = pl.program_id(2)

            @pl.when(k == 0)
            def _():
                acc_ref[...] = part

            @pl.when(k > 0)
            def _():
                acc_ref[...] += part

            @pl.when(k == nk - 1)
            def _():
                finish(acc_ref[...])

    res = _carry(
        riders, body, name=name, grid=grid,
        in_specs=[a_spec, b_spec] + [s for _, s in extras],
        out_specs=[s for _, s in outs],
        out_shape=[s for s, _ in outs],
        scratch_shapes=[pltpu.VMEM(tile, F32)] if nk > 1 else [],
        semantics=("parallel", "parallel", "arbitrary"),
    )(a, b, *[e for e, _ in extras])
    res, carried = res if riders else (res, None)
    res = res[0] if no == 1 else res
    return (res, carried) if riders else res


def _tile_spec(tm, tn, col0=0):
    return pl.BlockSpec((tm, tn), lambda i, j, k: (i, j + col0))


def _out2d(m, n, dtype, tm, tn):
    return (jax.ShapeDtypeStruct((m, n), dtype), _tile_spec(tm, tn))


def _mm_fwd(name, a, w, tm, tn, tk, outs=None, epilogue=None, extras=(), col0=0, ncols=None, riders=()):
    m, kdim = a.shape
    if w.ndim == 3:
        per = w.shape[2] // tn
        n = ncols or N_SHARD * w.shape[2]
        w_spec = pl.BlockSpec((None, tk, tn), lambda i, j, k: ((j + col0) // per, k, (j + col0) % per))
    else:
        n = ncols or w.shape[1]
        w_spec = pl.BlockSpec((tk, tn), lambda i, j, k: (k, j + col0))
    if outs is None:
        outs = [_out2d(m, n, F32, tm, tn)]
    return _mm(name, NN, a, pl.BlockSpec((tm, tk), lambda i, j, k: (i, k)), w, w_spec,
               (m // tm, n // tn, kdim // tk), (tm, tn), outs, epilogue, extras, riders)


def _mm_bwd_x(name, g, g_spec, w, tm, tj, tc, m, n_contract, outs=None, epilogue=None, extras=(), riders=()):
    if w.ndim == 3:
        per = w.shape[2] // tc
        kdim = w.shape[1]
        w_spec = pl.BlockSpec((None, tj, tc), lambda i, j, n: (n // per, j, n % per))
    else:
        kdim = w.shape[0]
        w_spec = pl.BlockSpec((tj, tc), lambda i, j, n: (j, n))
    if outs is None:
        outs = [_out2d(m, kdim, F32, tm, tj)]
    return _mm(name, NT, g, g_spec, w, w_spec, (m // tm, kdim // tj, n_contract // tc),
               (tm, tj), outs, epilogue, extras, riders)


def _mm_bwd_w(name, a, g, g_spec, n, tk, tn, tm, sharded, riders=()):
    m, kdim = a.shape
    if sharded:
        per = (n // N_SHARD) // tn
        out = (jax.ShapeDtypeStruct((N_SHARD, kdim, n // N_SHARD), F32),
               pl.BlockSpec((None, tk, tn), lambda i, j, mm: (j // per, i, j % per)))
    else:
        out = (jax.ShapeDtypeStruct((kdim, n), F32), pl.BlockSpec((tk, tn), lambda i, j, mm: (i, j)))
    return _mm(name, TN, a, pl.BlockSpec((tm, tk), lambda i, j, mm: (mm, i)), g, g_spec,
               (kdim // tk, n // tn, m // tm), (tk, tn), [out], riders=riders)


def _ew(name, fn, tiles, fulls, outs, sums=(), ts=512, riders=()):
    tiles = [t if isinstance(t, tuple) else (t, pl.BlockSpec((ts, t.shape[1]), lambda i: (i, 0)))
             for t in tiles]
    s_rows = tiles[0][0].shape[-2]
    nt, nf, no = len(tiles), len(fulls), len(outs)

    def body(*refs):
        t_vals = [r[...] for r in refs[:nt]]
        f_vals = [r[...] for r in refs[nt:nt + nf]]
        o_refs, s_refs = refs[nt + nf:nt + nf + no], refs[nt + nf + no:]
        o_vals, s_vals = fn(t_vals, f_vals)
        for r, v in zip(o_refs, o_vals):
            r[...] = v.astype(r.dtype)
        for r, v in zip(s_refs, s_vals):
            part = jnp.sum(v, axis=0, keepdims=True)

            @pl.when(pl.program_id(0) == 0)
            def _():
                r[...] = part

            @pl.when(pl.program_id(0) > 0)
            def _():
                r[...] += part

    full_specs = [pl.BlockSpec(f.shape, lambda i, nd=f.ndim: (0,) * nd) for f in fulls]
    return _carry(
        riders, body, name=name, grid=(s_rows // ts,),
        in_specs=[s for _, s in tiles] + full_specs,
        out_specs=[pl.BlockSpec((ts, c), lambda i: (i, 0)) for c, _ in outs]
        + [pl.BlockSpec((1, c), lambda i: (0, 0)) for c in sums],
        out_shape=[jax.ShapeDtypeStruct((s_rows, c), dt) for c, dt in outs]
        + [jax.ShapeDtypeStruct((1, c), F32) for c in sums],
        semantics=("arbitrary",),
    )(*[t for t, _ in tiles], *fulls)


def _rms_fwd(name, x, g, riders=()):
    def fn(t, f):
        xv = t[0]
        r = lax.rsqrt(jnp.mean(xv * xv, axis=-1, keepdims=True) + EPS)
        return [xv * r * f[0]], []
    out = _ew(name, fn, [x], [g], [(x.shape[1], BF16)], riders=riders)
    return (out[0][0], out[1]) if riders else out[0]


def _split3(x):
    x1 = x.astype(BF16)
    r1 = x - x1.astype(F32)
    x2 = r1.astype(BF16)
    x3 = (r1 - x2.astype(F32)).astype(BF16)
    return x1, x2, x3


def _rel_class(cp):
    far = (cp < MAX_REL) | (cp > BAND)
    return jnp.where(far, 2 * MAX_REL, BAND - cp)


def _skew_rows(x, sign):
    row = lax.broadcasted_iota(jnp.int32, x.shape, 0)
    for b in range(CHUNK.bit_length() - 1):
        shift = (1 << b) if sign > 0 else SKEW_W - (1 << b)
        x = jnp.where((row >> b) & 1 == 1, pltpu.roll(x, shift, 1), x)
    return x


def _roll_lanes(x, shift):
    return x if shift % SKEW_W == 0 else pltpu.roll(x, shift % SKEW_W, 1)


N_START = 3


def _bias_expand(rel_bias, riders=()):
    rel = jnp.pad(rel_bias, ((0, 0), (0, REL_PAD - N_REL))).reshape(N_HEADS, 1, REL_PAD)

    def body(rel_ref, o_ref):
        cls = lax.broadcasted_iota(jnp.int32, (REL_PAD, SKEW_W), 0)
        cp = lax.broadcasted_iota(jnp.int32, (REL_PAD, SKEW_W), 1)
        onehot = (cls == _rel_class(cp)).astype(BF16)
        rel8 = jnp.broadcast_to(rel_ref[...], (8, REL_PAD))
        trow = sum(jnp.dot(p, onehot, preferred_element_type=F32) for p in _split3(rel8))[0:1]
        first = _skew_rows(jnp.broadcast_to(trow, (CHUNK, SKEW_W)), +1)
        full = jnp.concatenate([_roll_lanes(first, CHUNK * g) for g in range(SUPER // CHUNK)], axis=0)[:, :BAND]
        qc = lax.broadcasted_iota(jnp.int32, (SUPER, BAND), 0) // CHUNK
        kc = lax.broadcasted_iota(jnp.int32, (SUPER, BAND), 1) // CHUNK
        on_band = (kc >= qc) & (kc <= qc + N_PREV_CHUNKS)
        table = jnp.where(on_band, full, NEG_INF).T
        key = lax.broadcasted_iota(jnp.int32, (BAND, SUPER), 0)
        for t in range(N_START):
            o_ref[t] = jnp.where(key < (N_START - 1 - t) * SUPER, NEG_INF, table)

    return _carry(
        riders, body, name="bias_expand", grid=(N_HEADS,),
        in_specs=[pl.BlockSpec((None, 1, REL_PAD), lambda h: (h, 0, 0))],
        out_specs=pl.BlockSpec((N_START, None, BAND, SUPER), lambda h: (0, h, 0, 0)),
        out_shape=jax.ShapeDtypeStruct((N_START, N_HEADS, BAND, SUPER), F32),
        semantics=("arbitrary",),
    )(rel)


def _bias_reduce(dbias, riders=()):
    def body(d_ref, o_ref):
        x = jnp.concatenate([d_ref[...].T, jnp.zeros((SUPER, SKEW_W - BAND), F32)], axis=1)
        folded = sum(_roll_lanes(x[CHUNK * g:CHUNK * (g + 1)], -CHUNK * g) for g in range(SUPER // CHUNK))
        diag = jnp.sum(_skew_rows(folded, -1), axis=0, keepdims=True)
        cp = lax.broadcasted_iota(jnp.int32, (SKEW_W, REL_PAD), 0)
        cls = lax.broadcasted_iota(jnp.int32, (SKEW_W, REL_PAD), 1)
        onehot = (cls == _rel_class(cp)).astype(BF16)
        diag8 = jnp.broadcast_to(diag, (8, SKEW_W))
        o_ref[...] = sum(jnp.dot(p, onehot, preferred_element_type=F32) for p in _split3(diag8))[0:1]

    out = _carry(
        riders, body, name="bias_reduce", grid=(N_HEADS,),
        in_specs=[pl.BlockSpec((None, BAND, SUPER), lambda h: (h, 0, 0))],
        out_specs=pl.BlockSpec((None, 1, REL_PAD), lambda h: (h, 0, 0)),
        out_shape=jax.ShapeDtypeStruct((N_HEADS, 1, REL_PAD), F32),
        semantics=("arbitrary",),
    )(dbias)
    out, carried = out if riders else (out, None)
    out = out.reshape(N_HEADS, REL_PAD)[:, :N_REL]
    return (out, carried) if riders else out


HEADS_PER_STEP = 8
HEAD_COLS = HEADS_PER_STEP * HEAD_DIM
N_HEAD_GROUPS = N_HEADS // HEADS_PER_STEP


def _scores_t(qs, kn, bias_t):
    return jnp.concatenate([lax.dot_general(k, qs, NT, preferred_element_type=F32) for k in kn], axis=0) + bias_t


def _bias_spec():
    return pl.BlockSpec((None, HEADS_PER_STEP, BAND, SUPER), lambda hg, i: (jnp.minimum(i, N_START - 1), hg, 0, 0))


def _band_specs(nb, col0, clamp_hi):
    def spec(d):
        def index(hg, i):
            blk = jnp.maximum(i - d, 0)
            if clamp_hi:
                blk = jnp.minimum(blk, nb - 1)
            return (blk, col0 + hg)
        return pl.BlockSpec((SUPER, HEAD_COLS), index)
    return [spec(2), spec(1), spec(0)]


def _head_sums(y):
    same_head = (lax.broadcasted_iota(jnp.int32, (MXU_DIM, MXU_DIM), 0) // HEAD_DIM
                 == lax.broadcasted_iota(jnp.int32, (MXU_DIM, MXU_DIM), 1) // HEAD_DIM).astype(BF16)
    sums = []
    for c0 in range(0, y.shape[1], MXU_DIM):
        chunk = y[:, c0:c0 + MXU_DIM]
        hi = chunk.astype(BF16)
        lo = (chunk - hi.astype(F32)).astype(BF16)
        sums.append(jnp.dot(hi, same_head, preferred_element_type=F32)
                    + jnp.dot(lo, same_head, preferred_element_type=F32))
    return jnp.concatenate(sums, axis=1)


def _head_unit(x):
    r = lax.rsqrt(_head_sums(x * x) * (1.0 / HEAD_DIM) + EPS)
    return x * r, r


def _head(hh):
    return slice(HEAD_DIM * hh, HEAD_DIM * (hh + 1))


def _key_block(j):
    return slice(SUPER * j, SUPER * (j + 1))


LSE_ROWS = 8


def _attn_fwd(qkn, v, bias, riders=()):
    s_len = qkn.shape[0]
    nb = s_len // SUPER

    def body(q_ref, k0, k1, k2, v0, v1, v2, b_ref, o_ref, lse_ref):
        outs = []
        v_t = [v0[...].T, v1[...].T, v2[...].T]

        def probabilities(hh):
            sl = _head(hh)
            s = _scores_t(q_ref[:, sl], [k0[:, sl], k1[:, sl], k2[:, sl]], b_ref[hh])
            m = jnp.max(s, axis=0, keepdims=True)
            e = jnp.exp(s - m)
            l = jnp.sum(e, axis=0, keepdims=True)
            lse_ref[hh:hh + 1, :] = m + jnp.log(l)
            return (e * (1.0 / l)).astype(BF16), sl

        def weighted_values(p, sl):
            outs.append(sum(jnp.dot(v_t[j][sl, :], p[_key_block(j), :], preferred_element_type=F32)
                            for j in range(3)))

        ready = probabilities(0)
        for hh in range(1, HEADS_PER_STEP):
            following = probabilities(hh)
            weighted_values(*ready)
            ready = following
        weighted_values(*ready)
        o_ref[...] = jnp.concatenate(outs, axis=0).T.astype(o_ref.dtype)

    return _carry(
        riders, body, name="attn_fwd", grid=(N_HEAD_GROUPS, nb),
        in_specs=[pl.BlockSpec((SUPER, HEAD_COLS), lambda hg, i: (i, hg))]
        + _band_specs(nb, N_HEAD_GROUPS, False) + _band_specs(nb, 0, False) + [_bias_spec()],
        out_specs=[pl.BlockSpec((SUPER, HEAD_COLS), lambda hg, i: (i, hg)),
                   pl.BlockSpec((None, LSE_ROWS, SUPER), lambda hg, i: (hg, 0, i))],
        out_shape=[jax.ShapeDtypeStruct((s_len, D_MODEL), BF16),
                   jax.ShapeDtypeStruct((N_HEAD_GROUPS, LSE_ROWS, s_len), F32)],
        semantics=("parallel", "arbitrary"),
    )(qkn, qkn, qkn, qkn, v, v, v, bias)


def _attn_bwd(qkn, v, out, d_out, bias, lse, riders=()):
    s_len = qkn.shape[0]
    nb = s_len // SUPER

    def body(q_ref, k0, k1, k2, v0, v1, v2, o_ref, do_ref, b_ref, lse_ref, dp_ref, db_ref, aq_ref, ak_ref, av_ref):
        i = pl.program_id(1)

        @pl.when(i == 0)
        def _():
            aq_ref[...] = jnp.zeros_like(aq_ref)
            ak_ref[...] = jnp.zeros_like(ak_ref)
            av_ref[...] = jnp.zeros_like(av_ref)
            db_ref[...] = jnp.zeros_like(db_ref)

        @pl.when(i < nb)
        def _():
            dq, dk, dv = [], [[], [], []], [[], [], []]
            ones = jnp.ones((8, HEAD_DIM), BF16)
            k_t = [k0[...].T, k1[...].T, k2[...].T]

            def softmax_grad(hh):
                sl = _head(hh)
                qs, do = q_ref[:, sl], do_ref[:, sl]
                kn = [k0[:, sl], k1[:, sl], k2[:, sl]]
                prod = do.astype(F32) * o_ref[:, sl].astype(F32)
                hi = prod.astype(BF16)
                lo = (prod - hi.astype(F32)).astype(BF16)
                delta = (lax.dot_general(ones, hi, NT, preferred_element_type=F32)
                         + lax.dot_general(ones, lo, NT, preferred_element_type=F32))[0:1]
                lse_row = lse_ref[hh:hh + 1, :]
                pb, dsb = [], []
                for j, (kj, vj) in enumerate(zip(kn, (v0, v1, v2))):
                    rows = _key_block(j)
                    p = jnp.exp(lax.dot_general(kj, qs, NT, preferred_element_type=F32) + b_ref[hh, rows, :] - lse_row)
                    ds = p * (lax.dot_general(vj[:, sl], do, NT, preferred_element_type=F32) - delta)
                    db_ref[hh, rows, :] += ds
                    pb.append(p.astype(BF16))
                    dsb.append(ds.astype(BF16))
                return pb, dsb, qs, do, sl

            def operand_grads(pb, dsb, qs, do, sl):
                dq.append(sum(jnp.dot(k_t[j][sl, :], dsb[j], preferred_element_type=F32) for j in range(3)))
                for j in range(3):
                    dv[j].append(jnp.dot(pb[j], do, preferred_element_type=F32))
                    dk[j].append(jnp.dot(dsb[j], qs, preferred_element_type=F32))

            ready = softmax_grad(0)
            for hh in range(1, HEADS_PER_STEP):
                following = softmax_grad(hh)
                operand_grads(*ready)
                ready = following
            operand_grads(*ready)
            aq_ref[i % 3] = jnp.concatenate(dq, axis=0).T
            for j in range(3):
                slot = (i + 1 + j) % 3
                if j < 2:
                    ak_ref[slot] += jnp.concatenate(dk[j], axis=1)
                    av_ref[slot] += jnp.concatenate(dv[j], axis=1)
                else:
                    ak_ref[slot] = jnp.concatenate(dk[j], axis=1)
                    av_ref[slot] = jnp.concatenate(dv[j], axis=1)

        slot = (i + 1) % 3
        dp_ref[0] = aq_ref[slot].astype(dp_ref.dtype)
        dp_ref[1] = ak_ref[slot].astype(dp_ref.dtype)
        dp_ref[2] = av_ref[slot].astype(dp_ref.dtype)

    def qrow(hg, i):
        return (jnp.minimum(i, nb - 1), hg)

    return _carry(
        riders, body, name="attn_bwd", grid=(N_HEAD_GROUPS, nb + 2),
        in_specs=[pl.BlockSpec((SUPER, HEAD_COLS), qrow)]
        + _band_specs(nb, N_HEAD_GROUPS, True) + _band_specs(nb, 0, True)
        + [pl.BlockSpec((SUPER, HEAD_COLS), qrow), pl.BlockSpec((SUPER, HEAD_COLS), qrow), _bias_spec(),
           pl.BlockSpec((None, LSE_ROWS, SUPER), lambda hg, i: (hg, 0, jnp.minimum(i, nb - 1)))],
        out_specs=[pl.BlockSpec((3, SUPER, HEAD_COLS), lambda hg, i: (0, jnp.maximum(i - 2, 0), hg)),
                   pl.BlockSpec((HEADS_PER_STEP, BAND, SUPER), lambda hg, i: (hg, 0, 0))],
        out_shape=[jax.ShapeDtypeStruct((6, s_len, D_MODEL), BF16),
                   jax.ShapeDtypeStruct((N_HEADS, BAND, SUPER), F32)],
        scratch_shapes=[pltpu.VMEM((3, SUPER, HEAD_COLS), F32)] * 3,
        semantics=("parallel", "arbitrary"),
    )(qkn, qkn, qkn, qkn, v, v, v, out, d_out, bias, lse)


def _qk_norm_bwd(dproj6, qk_raw, gq, gk):
    s_len = qk_raw.shape[0]
    ts = min(1024, s_len)

    nsteps = s_len // ts
    half = D_MODEL // 2

    def body(d_ref, raw_ref, gq_ref, gk_ref, o_ref, dgq_ref, dgk_ref, acc_ref):
        step = pl.program_id(0)

        @pl.when(step == 0)
        def _():
            acc_ref[...] = jnp.zeros_like(acc_ref)

        for piece, (g_ref, scale) in enumerate(((gq_ref, QK_SCALE), (gk_ref, 1.0))):
            for c0 in (0, half):
                xhat, r = _head_unit(raw_ref[:, piece * D_MODEL + c0:piece * D_MODEL + c0 + half].astype(F32))
                dn = d_ref[piece, :, c0:c0 + half].astype(F32) * scale
                u = dn * g_ref[...]
                dx = r * (u - xhat * (_head_sums(u * xhat) * (1.0 / HEAD_DIM)))
                o_ref[piece, :, c0:c0 + half] = dx.astype(o_ref.dtype)
                acc_ref[piece:piece + 1, c0:c0 + half] += jnp.sum(dn * xhat, axis=0, keepdims=True)

        @pl.when(step == nsteps - 1)
        def _():
            lane = lax.broadcasted_iota(jnp.int32, (D_MODEL, LANE), 0) % HEAD_DIM
            fold = (lane == lax.broadcasted_iota(jnp.int32, (D_MODEL, LANE), 1)).astype(BF16)
            tot = sum(jnp.dot(p, fold, preferred_element_type=F32) for p in _split3(acc_ref[...]))
            dgq_ref[...] = tot[0:1, :HEAD_DIM]
            dgk_ref[...] = tot[1:2, :HEAD_DIM]

    gain = pl.BlockSpec((1, half), lambda i: (0, 0))
    small = pl.BlockSpec((1, HEAD_DIM), lambda i: (0, 0))
    per_head = lambda g: jnp.tile(g, (1, half // HEAD_DIM))
    return pl.pallas_call(
        body, name="qk_norm_bwd", grid=(nsteps,),
        in_specs=[pl.BlockSpec((2, ts, D_MODEL), lambda i: (0, i, 0)),
                  pl.BlockSpec((ts, 2 * D_MODEL), lambda i: (i, 0)), gain, gain],
        out_specs=[pl.BlockSpec((2, ts, D_MODEL), lambda i: (0, i, 0)), small, small],
        out_shape=[jax.ShapeDtypeStruct(dproj6.shape, dproj6.dtype),
                   jax.ShapeDtypeStruct((1, HEAD_DIM), F32), jax.ShapeDtypeStruct((1, HEAD_DIM), F32)],
        scratch_shapes=[pltpu.VMEM((8, D_MODEL), F32)],
        input_output_aliases={0: 0},
        compiler_params=_params("arbitrary"),
    )(dproj6, qk_raw, per_head(gq), per_head(gk))


CONV_ROWS = 512
HALO = 16


def _rows_with_halo(ref, r0, n, front, s_len):
    zeros = jnp.zeros((HALO, ref.shape[1]), F32)
    if front:
        return (jnp.concatenate([zeros, ref[0:n, :].astype(F32)], axis=0) if r0 == 0
                else ref[r0 - HALO:r0 + n, :].astype(F32))
    return (jnp.concatenate([ref[r0:r0 + n, :].astype(F32), zeros], axis=0) if r0 + n == s_len
            else ref[r0:r0 + n + HALO, :].astype(F32))


def _earlier(ext, k):
    return pltpu.roll(ext, k, 0)[HALO:]


def _later(ext, k):
    n = ext.shape[0]
    return pltpu.roll(ext, n - k, 0)[:n - HALO]


def _conv_cols(col0):
    return lambda s_len: pl.BlockSpec((s_len, LANE), lambda j: (0, col0 + j))


def _conv_fwd(proj, conv_w, conv_b, riders=()):
    s_len = proj.shape[0]

    def body(bg_ref, cg_ref, xc_ref, w_ref, b_ref, o_ref):
        w = [w_ref[t:t + 1, :] for t in range(3)]
        for r0 in range(0, s_len, CONV_ROWS):
            u = _rows_with_halo(cg_ref, r0, CONV_ROWS, True, s_len) * \
                _rows_with_halo(xc_ref, r0, CONV_ROWS, True, s_len)
            conv = b_ref[...] + w[0] * _earlier(u, 2) + w[1] * _earlier(u, 1) + w[2] * u[HALO:]
            o_ref[r0:r0 + CONV_ROWS, :] = (bg_ref[r0:r0 + CONV_ROWS, :].astype(F32) * conv).astype(o_ref.dtype)

    return _carry(
        riders, body, name="conv_fwd", grid=(D_MODEL // LANE,),
        in_specs=[_conv_cols(0)(s_len), _conv_cols(8)(s_len), _conv_cols(16)(s_len),
                  pl.BlockSpec((3, LANE), lambda j: (0, j)), pl.BlockSpec((1, LANE), lambda j: (0, j))],
        out_specs=pl.BlockSpec((s_len, LANE), lambda j: (0, j)),
        out_shape=jax.ShapeDtypeStruct((s_len, D_MODEL), BF16),
        semantics=("parallel",),
    )(proj, proj, proj, conv_w, conv_b)


def _conv_bwd(dproj6, dy, proj, conv_w, conv_b, riders=()):
    s_len = proj.shape[0]

    def body(dy_ref, bg_ref, cg_ref, xc_ref, w_ref, b_ref, _, dp_ref, dw_ref):
        w = [w_ref[t:t + 1, :] for t in range(3)]
        acc = [jnp.zeros((1, LANE), F32) for _ in range(4)]
        for r0 in range(0, s_len, CONV_ROWS):
            rows = slice(r0, r0 + CONV_ROWS)
            u = _rows_with_halo(cg_ref, r0, CONV_ROWS, True, s_len) * \
                _rows_with_halo(xc_ref, r0, CONV_ROWS, True, s_len)
            u2, u1, u0 = _earlier(u, 2), _earlier(u, 1), u[HALO:]
            conv = b_ref[...] + w[0] * u2 + w[1] * u1 + w[2] * u0
            dp_ref[0, rows, :] = (dy_ref[rows, :].astype(F32) * conv).astype(dp_ref.dtype)
            dconv_ext = _rows_with_halo(dy_ref, r0, CONV_ROWS, False, s_len) * \
                _rows_with_halo(bg_ref, r0, CONV_ROWS, False, s_len)
            dconv = dconv_ext[:CONV_ROWS]
            for t, term in enumerate([dconv * u2, dconv * u1, dconv * u0, dconv]):
                acc[t] = acc[t] + jnp.sum(term, axis=0, keepdims=True)
            du = w[2] * dconv + w[1] * _later(dconv_ext, 1) + w[0] * _later(dconv_ext, 2)
            dp_ref[1, rows, :] = (du * xc_ref[rows, :].astype(F32)).astype(dp_ref.dtype)
            dp_ref[2, rows, :] = (du * cg_ref[rows, :].astype(F32)).astype(dp_ref.dtype)
        dw_ref[...] = jnp.zeros_like(dw_ref)
        for t in range(4):
            dw_ref[t:t + 1, :] = acc[t]

    return _carry(
        riders, body, name="conv_bwd", grid=(D_MODEL // LANE,),
        in_specs=[pl.BlockSpec((s_len, LANE), lambda j: (0, j)),
                  _conv_cols(0)(s_len), _conv_cols(8)(s_len), _conv_cols(16)(s_len),
                  pl.BlockSpec((3, LANE), lambda j: (0, j)), pl.BlockSpec((1, LANE), lambda j: (0, j)),
                  pl.BlockSpec(memory_space=pl.ANY)],
        out_specs=[pl.BlockSpec((3, s_len, LANE), lambda j: (1, 0, j)),
                   pl.BlockSpec((8, LANE), lambda j: (0, j))],
        out_shape=[jax.ShapeDtypeStruct(dproj6.shape, dproj6.dtype),
                   jax.ShapeDtypeStruct((8, D_MODEL), F32)],
        input_output_aliases={6: 0},
        semantics=("parallel",),
    )(dy, proj, proj, proj, conv_w, conv_b, dproj6)


def _d_norm_input(name, terms, x, g, dres, out_dtype, riders=()):
    s_len, kdim = terms[0][0].shape[1], terms[0][1].shape[1]
    tm, chunk = 512, 512
    n = len(terms)

    def body(*refs):
        x_ref, g_ref, r_ref, o_ref, dg_ref = refs[2 * n:]
        dh = jnp.zeros((tm, kdim), F32)
        for p_ref, w_ref in zip(refs[:n], refs[n:2 * n]):
            n_piece, _, width = p_ref.shape
            per_shard = w_ref.shape[2]
            for c0 in range(0, n_piece * width, chunk):
                dh = dh + lax.dot_general(p_ref[c0 // width, :, c0 % width:c0 % width + chunk],
                                          w_ref[c0 // per_shard, :, c0 % per_shard:c0 % per_shard + chunk],
                                          NT, preferred_element_type=F32)
        xv = x_ref[...]
        r = lax.rsqrt(jnp.mean(xv * xv, axis=-1, keepdims=True) + EPS)
        xhat = xv * r
        u = dh * g_ref[...]
        dx = r * (u - xhat * jnp.mean(u * xhat, axis=-1, keepdims=True)) + r_ref[...].astype(F32)
        o_ref[...] = dx.astype(o_ref.dtype)
        part = jnp.sum(dh * xhat, axis=0, keepdims=True)

        @pl.when(pl.program_id(0) == 0)
        def _():
            dg_ref[...] = part

        @pl.when(pl.program_id(0) > 0)
        def _():
            dg_ref[...] += part

    tile = pl.BlockSpec((tm, kdim), lambda i: (i, 0))
    vec = pl.BlockSpec((1, kdim), lambda i: (0, 0))
    return _carry(
        riders, body, name=name, grid=(s_len // tm,),
        in_specs=[pl.BlockSpec((p.shape[0], tm, p.shape[2]), lambda i: (0, i, 0)) for p, _ in terms]
        + [pl.BlockSpec(w.shape, lambda i: (0, 0, 0), pipeline_mode=pl.Buffered(1)) for _, w in terms]
        + [tile, vec, tile],
        out_specs=[tile, vec],
        out_shape=[jax.ShapeDtypeStruct((s_len, kdim), out_dtype), jax.ShapeDtypeStruct((1, kdim), F32)],
        semantics=("arbitrary",),
    )(*[p for p, _ in terms], *[w for _, w in terms], x, g, dres)


def _local_grads(x, target, norm1_g, q_norm_g, k_norm_g, rel_bias, conv_b, b_gate, norm2_g, comm):
    s_len = x.shape[0]
    tm = min(1024, s_len)
    tb = min(2048, s_len)
    row = lambda v: v.reshape(1, -1)

    def carrying(at, fn, *args, **kw):
        riders = comm.riders(at)
        out = fn(*args, riders=riders, **kw)
        if riders:
            out, carried = out
            comm.done(at, carried)
        return out

    bias = carrying("bias_expand", _bias_expand, rel_bias)
    h = carrying("norm1", _rms_fwd, "norm1", x, row(norm1_g))
    w_in3, conv_w = comm.early_weights()
    gq, gk = row(q_norm_g), row(k_norm_g)

    def qk_epi(acc, e, o):
        o[0][...] = acc.astype(BF16)
        gain = jnp.where(pl.program_id(1) < 2, e[0] * QK_SCALE, e[1])
        o[1][...] = (_head_unit(acc)[0] * gain).astype(BF16)
    conv_in = carrying("proj_conv", _mm_fwd, "proj_conv", h, w_in3, tb, 1536, D_MODEL, col0=2, ncols=3 * D_MODEL,
                       outs=[_out2d(s_len, 3 * D_MODEL, BF16, tb, 1536)])
    small = pl.BlockSpec((1, 512), lambda i, j, k: (0, 0))
    per_head = lambda g: jnp.tile(g, (1, 512 // HEAD_DIM))
    qk_raw, qkn = carrying("proj_qk", _mm_fwd, "proj_qk", h, w_in3, tm, 512, D_MODEL, ncols=2 * D_MODEL,
                           epilogue=qk_epi, outs=[_out2d(s_len, 2 * D_MODEL, BF16, tm, 512)] * 2,
                           extras=[(per_head(gq), small), (per_head(gk), small)])
    v = carrying("proj_v", _mm_fwd, "proj_v", h, w_in3, tb, 512, D_MODEL, col0=4, ncols=D_MODEL,
                 outs=[_out2d(s_len, D_MODEL, BF16, tb, 512)])
    w_gate3 = comm.gate_weight()

    def gate_epi(acc, e, o):
        o[0][...] = jax.nn.sigmoid(acc + e[0]).astype(BF16)
    gates = carrying("gates", _mm_fwd, "gates", h, w_gate3, tb, 512, D_MODEL, epilogue=gate_epi,
                     outs=[_out2d(s_len, 2 * D_MODEL, BF16, tb, 512)],
                     extras=[(row(b_gate), pl.BlockSpec((1, 512), lambda i, j, k: (0, j)))])

    attn, lse = carrying("attn_fwd", _attn_fwd, qkn, v, bias)
    yconv = carrying("conv_fwd", _conv_fwd, conv_in, conv_w, row(conv_b))
    w_ap, w_cp, w_out, w_up3, w_down = comm.late_weights()
    tw = 1024
    ya = _mm_fwd("attn_proj", attn, w_ap, tm, tw, D_MODEL, outs=[_out2d(s_len, D_MODEL, BF16, tm, tw)])

    def merge_epi(acc, e, o):
        ya_v, ga, gc = [t.astype(F32) for t in e]
        o[0][...] = acc.astype(BF16)
        o[1][...] = (ga * ya_v + gc * acc).astype(BF16)
    gate_a, gate_c = (gates, _tile_spec(tm, tw, 0)), (gates, _tile_spec(tm, tw, 1))
    yc, merged = _mm_fwd("conv_proj", yconv, w_cp, tm, tw, D_MODEL, epilogue=merge_epi,
                         outs=[_out2d(s_len, D_MODEL, BF16, tm, tw), _out2d(s_len, D_MODEL, BF16, tm, tw)],
                         extras=[(ya, _tile_spec(tm, tw)), gate_a, gate_c])

    def res_epi(acc, e, o):
        x1_v = e[0] + acc
        o[0][...] = x1_v
        r = lax.rsqrt(jnp.mean(x1_v * x1_v, axis=-1, keepdims=True) + EPS)
        o[1][...] = (x1_v * r * e[1]).astype(BF16)
    assert tw == D_MODEL
    x1, h2 = _mm_fwd("out_proj", merged, w_out, tm, tw, D_MODEL, epilogue=res_epi,
                     outs=[_out2d(s_len, D_MODEL, F32, tm, tw), _out2d(s_len, D_MODEL, BF16, tm, tw)],
                     extras=[(x, _tile_spec(tm, tw)), (row(norm2_g), pl.BlockSpec((1, tw), lambda i, j, k: (0, 0)))])

    def up_epi(acc, e, o):
        o[0][...] = jnp.square(jnp.maximum(acc, 0.0)).astype(BF16)
    act = _mm_fwd("mlp_up", h2, w_up3, tb, tw, D_MODEL, epilogue=up_epi, outs=[_out2d(s_len, D_FF, BF16, tb, tw)])

    def loss_epi(acc, e, o):
        err = e[0] + acc - e[1]
        o[0][...] = (err * (1.0 / D_MODEL)).astype(BF16)
        sq = err * err
        part = sq[:, 0:LANE]
        for c0 in range(LANE, D_MODEL, LANE):
            part = part + sq[:, c0:c0 + LANE]
        o[1][...] = jnp.sum(part.reshape(tl // 8, 8, LANE), axis=0)
    tl = 512
    dy_b, loss_part = _mm_fwd(
        "mlp_down", act, w_down, tl, D_MODEL, D_FF, epilogue=loss_epi,
        outs=[_out2d(s_len, D_MODEL, BF16, tl, D_MODEL),
              (jax.ShapeDtypeStruct((8 * (s_len // tl), LANE), F32), pl.BlockSpec((8, LANE), lambda i, j, k: (i, 0)))],
        extras=[(x1, _tile_spec(tl, D_MODEL)), (target, _tile_spec(tl, D_MODEL))])

    def dup_epi(acc, e, o):
        o[0][...] = (acc * (2.0 * jnp.sqrt(e[0].astype(F32)))).astype(BF16)
    full = lambda cols: pl.BlockSpec((tm, cols), lambda i, j, n: (i, n))
    tokens = lambda cols: pl.BlockSpec((s_len, cols), lambda i, j, m: (m, j))
    dup = _mm_bwd_x("d_act", dy_b, full(D_MODEL), w_down, tm, tw, D_MODEL, s_len, D_MODEL, epilogue=dup_epi,
                    outs=[_out2d(s_len, D_FF, BF16, tm, tw)], extras=[(act, _tile_spec(tm, tw))])
    g_down = _mm_bwd_w("g_down", act, dy_b, tokens(D_MODEL), D_MODEL, 512, D_MODEL, s_len, False)
    g_up = _mm_bwd_w("g_up", h2, dup, tokens(512), D_FF, D_MODEL, 512, s_len, True)
    comm.grads_ready("mlp", dict(w_down=g_down, w_up=g_up))
    dx1_b, dg2 = carrying("d_h2", _d_norm_input, "d_h2", [(dup.reshape(1, s_len, D_FF), w_up3)],
                          x1, row(norm2_g), dy_b, BF16)

    def dmerge_epi(acc, e, o):
        ya_v, yc_v, ga, gc = [t.astype(F32) for t in e]
        o[0][...] = (acc * ga).astype(BF16)
        o[1][...] = (acc * gc).astype(BF16)
        o[2][0] = (acc * ya_v * ga * (1.0 - ga)).astype(BF16)
        o[2][1] = (acc * yc_v * gc * (1.0 - gc)).astype(BF16)
    dya, dyc, dgp2 = _mm_bwd_x(
        "d_merged", dx1_b, full(D_MODEL), w_out, tm, tw, D_MODEL, s_len, D_MODEL, epilogue=dmerge_epi,
        outs=[_out2d(s_len, D_MODEL, BF16, tm, tw), _out2d(s_len, D_MODEL, BF16, tm, tw),
              (jax.ShapeDtypeStruct((2, s_len, D_MODEL), BF16), pl.BlockSpec((2, tm, tw), lambda i, j, n: (0, i, j)))],
        extras=[(ya, _tile_spec(tm, tw)), (yc, _tile_spec(tm, tw)), gate_a, gate_c])
    g_out = _mm_bwd_w("g_out", merged, dx1_b, tokens(512), D_MODEL, D_MODEL, 512, s_len, False)
    d_attn = _mm_bwd_x("d_attn", dya, full(D_MODEL), w_ap, tm, tw, D_MODEL, s_len, D_MODEL,
                       outs=[_out2d(s_len, D_MODEL, BF16, tm, tw)])
    g_ap = _mm_bwd_w("g_attn_proj", attn, dya, tokens(512), D_MODEL, D_MODEL, 512, s_len, False)
    d_yconv = _mm_bwd_x("d_yconv", dyc, full(D_MODEL), w_cp, tm, tw, D_MODEL, s_len, D_MODEL,
                        outs=[_out2d(s_len, D_MODEL, BF16, tm, tw)])
    g_cp = _mm_bwd_w("g_conv_proj", yconv, dyc, tokens(512), D_MODEL, D_MODEL, 512, s_len, False)
    piece = lambda width: (lambda blk: (blk * width) // D_MODEL, lambda blk: (blk * width % D_MODEL) // width)
    pc, cb = piece(512)
    pieces = pl.BlockSpec((None, s_len, 512), lambda i, j, m: (pc(j), m, cb(j)))
    g_gate = _mm_bwd_w("g_gate", h, dgp2, pieces, 2 * D_MODEL, D_MODEL, 512, s_len, True)
    comm.grads_ready("proj", dict(w_out=g_out, w_attn_proj=g_ap, w_conv_proj=g_cp, w_gate=g_gate))

    dproj6, dbias = carrying("attn_bwd", _attn_bwd, qkn, v, attn, d_attn, bias, lse)
    dproj6, dgq, dgk = _qk_norm_bwd(dproj6, qk_raw, gq, gk)
    dproj6, dconv_wb = carrying("conv_bwd", _conv_bwd, dproj6, d_yconv, conv_in, conv_w, row(conv_b))

    g_in = carrying("g_in", _mm_bwd_w, "g_in", h, dproj6, pieces, 6 * D_MODEL, D_MODEL, 512, s_len, True)
    comm.grads_ready("in", dict(w_in=g_in))
    d_rel = carrying("bias_reduce", _bias_reduce, dbias)
    grad_x, dg1 = carrying("d_h", _d_norm_input, "d_h", [(dproj6, w_in3), (dgp2, w_gate3)],
                           x, row(norm1_g), dx1_b, F32)

    def bsum(t, f):
        return [], [t[0].astype(F32), t[1].astype(F32)]
    ts = 512
    db_a, db_c = carrying("b_gate_sum", _ew, "b_gate_sum", bsum,
                          [(dgp2, pl.BlockSpec((None, ts, D_MODEL), lambda i: (0, i, 0))),
                           (dgp2, pl.BlockSpec((None, ts, D_MODEL), lambda i: (1, i, 0)))],
                          [], [], sums=[D_MODEL, D_MODEL], ts=ts)

    big = dict(w_in=g_in, w_attn_proj=g_ap, w_conv_proj=g_cp, w_gate=g_gate, w_out=g_out,
               w_up=g_up, w_down=g_down)
    small = dict(norm1_g=dg1, norm2_g=dg2, conv_wb=dconv_wb, b_gate=(db_a, db_c),
                 q_norm_g=dgq, k_norm_g=dgk, rel_bias=d_rel)
    return loss_part, grad_x, big, small


def _finish_loss(loss_part):
    def body(l_ref, lo_ref):
        total = jnp.sum(jnp.sum(l_ref[...], axis=0, keepdims=True), axis=1, keepdims=True)
        lo_ref[...] = jnp.broadcast_to(total * (0.5 / D_MODEL), lo_ref.shape)

    return pl.pallas_call(body, name="finish_loss", out_shape=jax.ShapeDtypeStruct((8, LANE), F32))(loss_part)


def _place():
    return lax.axis_index("x"), lax.axis_index("y"), lax.axis_index("c")


def _other_chips(x, y):
    return [(1 - x, y), (x, 1 - y), (1 - x, 1 - y)]


def _cast_into_slot(name, where, w):
    r, cols = w.shape
    ts = 256

    def body(w_ref, x_ref, o_ref):
        o_ref[...] = x_ref[...].astype(o_ref.dtype)

    return pl.pallas_call(
        body, name=name,
        grid_spec=pltpu.PrefetchScalarGridSpec(
            num_scalar_prefetch=1, grid=(r // ts,),
            in_specs=[pl.BlockSpec((ts, cols), lambda i, w: (i, 0))],
            out_specs=pl.BlockSpec((None, ts, cols), lambda i, w: (w[0], i, 0))),
        out_shape=jax.ShapeDtypeStruct((N_SHARD, r, cols), BF16),
        compiler_params=_params("parallel"),
    )(where, w)


def _remote(src, dst, send, recv, k, to):
    return pltpu.make_async_remote_copy(src_ref=src, dst_ref=dst, send_sem=send.at[k], recv_sem=recv.at[k],
                                        device_id=to, device_id_type=MESH)


def _gather_riders(slots):
    n = len(slots)

    def copy(refs, w, j, shard, which, send, recv, to):
        hr = slots[w].shape[1] // 2
        ref = refs[w].at[shard, pl.ds(which * hr, hr)]
        return _remote(ref, ref, send, recv, 3 * w + j, to)

    def for_each_peer(fn):
        x, y, c = _place()
        for w in range(n):
            for j, chip in enumerate(_other_chips(x, y)):
                fn(w, j, 2 * x + y, 2 * chip[0] + chip[1], c, (*chip, c), (x, y, 1 - c))

    def chips_start(_, refs, send, recv):
        for_each_peer(lambda w, j, mine, theirs, c, peer, sib: copy(refs, w, j, mine, c, send, recv, peer).start())

    def chips_finish(_, refs, send, recv):
        for_each_peer(lambda w, j, mine, theirs, c, peer, sib: copy(refs, w, j, theirs, c, send, recv, peer).wait_recv())
        for_each_peer(lambda w, j, mine, theirs, c, peer, sib: copy(refs, w, j, mine, c, send, recv, peer).wait_send())

    def sibling_start(_, refs, send, recv):
        for_each_peer(lambda w, j, mine, theirs, c, peer, sib: copy(refs, w, j, theirs, c, send, recv, sib).start())

    def sibling_finish(_, refs, send, recv):
        for_each_peer(lambda w, j, mine, theirs, c, peer, sib: copy(refs, w, j, theirs, 1 - c, send, recv, sib).wait_recv())
        for_each_peer(lambda w, j, mine, theirs, c, peer, sib: copy(refs, w, j, theirs, c, send, recv, sib).wait_send())

    return (lambda arrays: _Rider([], arrays, 3 * n, chips_start, chips_finish),
            lambda arrays: _Rider([], arrays, 3 * n, sibling_start, sibling_finish))


def _pair_exchange_rider(grads, landing):
    n = len(grads)

    def copies(srcs, dsts, send, recv):
        x, y, c = _place()
        out = []
        for w in range(n):
            hr = grads[w].shape[1] // 2
            out.append(_remote(srcs[w].at[:, pl.ds((1 - c) * hr, hr)], dsts[w], send, recv, w, (x, y, 1 - c)))
        return out

    def start(srcs, dsts, send, recv):
        for cp in copies(srcs, dsts, send, recv):
            cp.start()

    def finish(srcs, dsts, send, recv):
        for cp in copies(srcs, dsts, send, recv):
            cp.wait()

    return _Rider(grads, landing, n, start, finish)


def _row_tile(hr):
    return min(hr, 256)


def _pair_add(name, where, grad, got):
    _, hr, cols = got.shape
    tr = _row_tile(hr)
    nblk = hr // tr

    def body(w_ref, g_ref, r_ref, o_ref):
        o_ref[...] = (g_ref[...] + r_ref[...]).astype(o_ref.dtype)

    other = lambda k, w: (w[0] + 1 + k) % N_SHARD
    return pl.pallas_call(
        body, name=name,
        grid_spec=pltpu.PrefetchScalarGridSpec(
            num_scalar_prefetch=1, grid=(N_SHARD - 1, nblk),
            in_specs=[pl.BlockSpec((None, tr, cols), lambda k, i, w: (other(k, w), w[1] * nblk + i, 0)),
                      pl.BlockSpec((None, tr, cols), lambda k, i, w: (other(k, w), i, 0))],
            out_specs=pl.BlockSpec((None, tr, cols), lambda k, i, w: (other(k, w), i, 0))),
        out_shape=jax.ShapeDtypeStruct(got.shape, BF16),
        compiler_params=_params("parallel", "parallel"),
    )(where, grad, got)


def _chip_exchange_rider(partials, landing):
    n = len(partials)

    def copies(srcs, dsts, send, recv):
        x, y, c = _place()
        return [_remote(srcs[w].at[2 * chip[0] + chip[1]], dsts[w].at[j], send, recv, 3 * w + j, (*chip, c))
                for w in range(n) for j, chip in enumerate(_other_chips(x, y))]

    def start(srcs, dsts, send, recv):
        for cp in copies(srcs, dsts, send, recv):
            cp.start()

    def finish(srcs, dsts, send, recv):
        for cp in copies(srcs, dsts, send, recv):
            cp.wait()

    return _Rider(partials, landing, 3 * n, start, finish)


def _final_add(name, where, grad, got, arrived):
    _, hr, cols = got.shape
    tr = _row_tile(hr)
    nblk = hr // tr

    def body(w_ref, g_ref, r_ref, a_ref, o_ref):
        acc = g_ref[...] + r_ref[...]
        for j in range(3):
            acc = acc + a_ref[j].astype(F32)
        o_ref[...] = acc

    return pl.pallas_call(
        body, name=name,
        grid_spec=pltpu.PrefetchScalarGridSpec(
            num_scalar_prefetch=1, grid=(nblk,),
            in_specs=[pl.BlockSpec((None, tr, cols), lambda i, w: (w[0], w[1] * nblk + i, 0)),
                      pl.BlockSpec((None, tr, cols), lambda i, w: (w[0], i, 0)),
                      pl.BlockSpec((3, tr, cols), lambda i, w: (0, i, 0))],
            out_specs=pl.BlockSpec((tr, cols), lambda i, w: (w[1] * nblk + i, 0))),
        out_shape=jax.ShapeDtypeStruct((2 * hr, cols), F32),
        compiler_params=_params("parallel"),
    )(where, grad, got, arrived)


def _pair_share_rider(shards):
    n = len(shards)

    def half(refs, w, which):
        hr = shards[w].shape[0] // 2
        return refs[w].at[pl.ds(which * hr, hr)]

    def start(_, refs, send, recv):
        x, y, c = _place()
        for w in range(n):
            _remote(half(refs, w, c), half(refs, w, c), send, recv, w, (x, y, 1 - c)).start()

    def finish(_, refs, send, recv):
        x, y, c = _place()
        for w in range(n):
            _remote(half(refs, w, 1 - c), half(refs, w, 1 - c), send, recv, w, (x, y, 1 - c)).wait_recv()
        for w in range(n):
            _remote(half(refs, w, c), half(refs, w, c), send, recv, w, (x, y, 1 - c)).wait_send()

    return _Rider([], shards, n, start, finish)


class _Exchange:
    PLAN = {"bias_expand": [("early", "gather")], "norm1": [("early", "forward")],
            "proj_conv": [("gate", "gather")], "proj_v": [("gate", "forward")],
            "proj_qk": [("late", "gather")], "gates": [("late", "forward")],
            "attn_fwd": [("mlp_w", "gather")], "conv_fwd": [("mlp_w", "forward")],
            "d_h2": [("mlp", "pair")], "attn_bwd": [("mlp", "chips"), ("proj", "pair")], "conv_bwd": [("mlp", "share")],
            "g_in": [("proj", "chips")], "bias_reduce": [("proj", "share"), ("in", "pair")],
            "d_h": [("in", "chips")], "b_gate_sum": [("in", "share")]}

    def __init__(self, where, early_slots, gate_slots, late_slots):
        self.where = where
        self.slots = dict(early=early_slots, gate=gate_slots, late=late_slots[:3], mlp_w=late_slots[3:])
        self.stage = {g: dict(zip(("gather", "forward"), _gather_riders(s))) for g, s in self.slots.items()}
        self.groups, self.reduced, self.pending = {}, {}, []

    def early_weights(self):
        w_in3, small = self.slots["early"]
        conv_w = small[:, :3, :].transpose(1, 0, 2).reshape(3, N_SHARD * small.shape[2])
        return w_in3, conv_w

    def gate_weight(self):
        return self.slots["gate"][0]

    def late_weights(self):
        rows = lambda a: a.reshape(a.shape[0] * a.shape[1], a.shape[2])
        w_ap, w_cp, w_out = self.slots["late"]
        w_up3, w_down = self.slots["mlp_w"]
        return rows(w_ap), rows(w_cp), rows(w_out), w_up3, rows(w_down)

    def grads_ready(self, group, grads):
        names = list(grads)
        g4 = [g if g.ndim == 3 else g.reshape(N_SHARD, -1, g.shape[1]) for g in grads.values()]
        self.groups[group] = dict(names=names, g4=g4)

    def riders(self, at):
        self.pending = self.PLAN.get(at, [])
        out = []
        for group, stage in self.pending:
            if group in self.slots:
                out.append(self.stage[group][stage](self.slots[group]))
                continue
            st = self.groups[group]
            if stage == "pair":
                landing = [lax.empty((N_SHARD, g.shape[1] // 2, g.shape[2]), F32) for g in st["g4"]]
                out.append(_pair_exchange_rider(st["g4"], landing))
            elif stage == "chips":
                landing = [lax.empty((3,) + p.shape[1:], p.dtype) for p in st["partial"]]
                out.append(_chip_exchange_rider(st["partial"], landing))
            else:
                out.append(_pair_share_rider(st["halves"]))
        return out

    def done(self, at, carried):
        for (group, stage), arrays in zip(self.pending, carried):
            if group in self.slots:
                self.slots[group] = arrays
                continue
            st = self.groups[group]
            tag = lambda what, n: what + "_" + n
            if stage == "pair":
                st["got"] = arrays
                st["partial"] = [_pair_add(tag("pair_add", n), self.where, g, r)
                                 for n, g, r in zip(st["names"], st["g4"], arrays)]
            elif stage == "chips":
                st["halves"] = [_final_add(tag("final_add", n), self.where, g, r, a)
                                for n, g, r, a in zip(st["names"], st["g4"], st["got"], arrays)]
            else:
                self.reduced.update(zip(st["names"], arrays))


SMALL_ROWS = 32
N_DEV = 8


def _all_reduce_small(pack):
    def body(p_ref, o_ref, buf, send, recv):
        x, y, c = _place()
        buf[4 * x + 2 * y + c] = p_ref[...]
        copies, waits = [], []
        for k in range(1, N_DEV):
            px = 1 - x if k & 4 else x
            py = 1 - y if k & 2 else y
            pc = 1 - c if k & 1 else c
            copies.append(_remote(p_ref, buf.at[4 * x + 2 * y + c], send, recv, k - 1, (px, py, pc)))
            waits.append(_remote(p_ref, buf.at[4 * px + 2 * py + pc], send, recv, k - 1, (px, py, pc)))
        for cp in copies:
            cp.start()
        for cp in waits:
            cp.wait_recv()
        acc = buf[0]
        for d in range(1, N_DEV):
            acc = acc + buf[d]
        o_ref[...] = acc
        for cp in copies:
            cp.wait_send()

    return pl.pallas_call(
        body, name="all_reduce_small",
        out_shape=jax.ShapeDtypeStruct(pack.shape, F32),
        scratch_shapes=[pltpu.VMEM((N_DEV,) + pack.shape, F32),
                        pltpu.SemaphoreType.DMA((N_DEV - 1,)), pltpu.SemaphoreType.DMA((N_DEV - 1,))],
    )(pack)


def _adamw(name, w, g, m, v):
    c1 = 1.0 - ADAM_B1 ** ADAM_STEP
    c2 = 1.0 - ADAM_B2 ** ADAM_STEP

    def fn(t, f):
        wv, gv, mv, vv = t
        m2 = ADAM_B1 * mv + (1.0 - ADAM_B1) * gv
        v2 = ADAM_B2 * vv + (1.0 - ADAM_B2) * (gv * gv)
        delta = -ADAM_LR * ((m2 / c1) / (jnp.sqrt(v2 / c2) + ADAM_EPS) + ADAM_WD * wv)
        return [delta, m2, v2], []

    cols = w.shape[1]
    return _ew(name, fn, [w, g, m, v], [], [(cols, F32)] * 3, ts=min(w.shape[0], 256))


LOSS_ROW = 26


def _pack_small(norm1_g, norm2_g, conv_b, b_gate, conv_w, q_norm_g, k_norm_g, rel_bias, loss=None):
    pack = jnp.zeros((SMALL_ROWS, D_MODEL), F32)
    for r0, v in ((0, norm1_g), (1, norm2_g), (2, conv_b), (3, b_gate.reshape(2, D_MODEL)), (5, conv_w),
                  (8, q_norm_g), (9, k_norm_g), (10, rel_bias)) + (((LOSS_ROW, loss),) if loss is not None else ()):
        v = v.reshape(-1, v.shape[-1]).astype(F32)
        pack = pack.at[r0:r0 + v.shape[0], :v.shape[1]].set(v)
    return pack


def _unpack_small(pack, conv_cols):
    return dict(norm1_g=pack[0], norm2_g=pack[1], conv_b=pack[2], b_gate=pack[3:5].reshape(2 * D_MODEL),
                conv_w=pack[5:8, :conv_cols], q_norm_g=pack[8, :HEAD_DIM], k_norm_g=pack[9, :HEAD_DIM],
                rel_bias=pack[10:10 + N_HEADS, :N_REL])


BIG = ["w_in", "w_attn_proj", "w_conv_proj", "w_gate", "w_out", "w_up", "w_down"]
LATE = ["w_attn_proj", "w_conv_proj", "w_out", "w_up", "w_down"]
WEIGHTS = ["norm1_g", "w_in", "q_norm_g", "k_norm_g", "rel_bias", "conv_w", "conv_b", "w_attn_proj",
           "w_conv_proj", "w_gate", "b_gate", "w_out", "norm2_g", "w_up", "w_down"]


def kernel(x, norm1_g, w_in, q_norm_g, k_norm_g, rel_bias, conv_w, conv_b, w_attn_proj, w_conv_proj, w_gate, b_gate, w_out, norm2_g, w_up, w_down, loss_target, m_norm1_g, m_w_in, m_q_norm_g, m_k_norm_g, m_rel_bias, m_conv_w, m_conv_b, m_w_attn_proj, m_w_conv_proj, m_w_gate, m_b_gate, m_w_out, m_norm2_g, m_w_up, m_w_down, v_norm1_g, v_w_in, v_q_norm_g, v_k_norm_g, v_rel_bias, v_conv_w, v_conv_b, v_w_attn_proj, v_w_conv_proj, v_w_gate, v_b_gate, v_w_out, v_norm2_g, v_w_up, v_w_down):
    given = dict(locals())
    w = {n: given[n] for n in WEIGHTS}
    m = {n: given["m_" + n] for n in WEIGHTS}
    v = {n: given["v_" + n] for n in WEIGHTS}
    s_len = x.shape[1]
    shard = 2 * lax.axis_index("x") + lax.axis_index("y")
    where = jnp.stack([shard, lax.axis_index("c")]).astype(jnp.int32)
    conv_cols = conv_w.shape[1]

    small_in = lax.dynamic_update_slice(jnp.zeros((N_SHARD, 16, conv_cols), F32), conv_w[None], (shard, 0, 0))
    slot = {n: _cast_into_slot("cast_" + n, where, w[n]) for n in BIG}
    comm = _Exchange(where, [slot["w_in"], small_in], [slot["w_gate"]], [slot[n] for n in LATE])

    loss_part, grad_x, _, small = _local_grads(
        x.reshape(s_len, D_MODEL), loss_target.reshape(s_len, D_MODEL), norm1_g, q_norm_g, k_norm_g,
        rel_bias, conv_b, b_gate, norm2_g, comm)
    grad = dict(comm.reduced)

    loss_local = _finish_loss(loss_part)
    pack = _pack_small(small["norm1_g"], small["norm2_g"], small["conv_wb"][3], jnp.concatenate(small["b_gate"], axis=1),
                       small["conv_wb"][0:3], small["q_norm_g"], small["k_norm_g"], small["rel_bias"],
                       loss=loss_local[0:1, :])
    total = _all_reduce_small(pack)
    g_small = _unpack_small(total, D_MODEL)
    g_small["conv_w"] = lax.dynamic_slice(g_small["conv_w"], (0, shard * conv_cols), (3, conv_cols))
    grad.update(g_small)

    delta, new_m, new_v = {}, {}, {}
    for n in BIG:
        delta[n], new_m[n], new_v[n] = _adamw("adamw_" + n, w[n], grad[n], m[n], v[n])
    small_names = [n for n in WEIGHTS if n not in BIG]
    packs = [_pack_small(**{n: src[n] for n in small_names}) for src in (w, grad, m, v)]
    for out, packed in zip((delta, new_m, new_v), _adamw("adamw_small", *packs)):
        out.update({n: a.reshape(w[n].shape) for n, a in _unpack_small(packed, conv_cols).items()})

    outs = [total[LOSS_ROW, 0], grad_x.reshape(x.shape)]
    for group in (grad, delta, new_m, new_v):
        outs += [group[n].reshape(w[n].shape) for n in WEIGHTS]
    return tuple(outs)
```

```python
import jax
import jax.numpy as jnp
from jax import lax
from jax.experimental import pallas as pl
from jax.experimental.pallas import tpu as pltpu

F32 = jnp.float32
BF16 = jnp.bfloat16

D_MODEL = 1024
N_HEADS = 16
HEAD_DIM = 64
CHUNK = 64
N_PREV_CHUNKS = 8
MAX_REL = 256
D_FF = 4096
N_REL = 2 * MAX_REL + 1
REL_PAD = 640
EPS = 1e-6
NEG_INF = -1e30
QK_SCALE = HEAD_DIM ** -0.5

SUPER = 4 * CHUNK
BAND = SUPER + N_PREV_CHUNKS * CHUNK
SKEW_W = 1024
N_SHARD = 4
LANE = 128
MXU_DIM = 256
VMEM_LIMIT = 48 * 1024 * 1024

ADAM_LR = 0.001
ADAM_B1 = 0.9
ADAM_B2 = 0.999
ADAM_EPS = 1e-08
ADAM_WD = 0.01
ADAM_STEP = 10

MESH = pl.DeviceIdType.MESH
NN = (((1,), (0,)), ((), ()))
NT = (((1,), (1,)), ((), ()))
TN = (((0,), (0,)), ((), ()))


def _params(*sem):
    return pltpu.CompilerParams(dimension_semantics=sem or None, vmem_limit_bytes=VMEM_LIMIT)


HBM_SPEC = pl.BlockSpec(memory_space=pl.ANY)


class _Rider:
    def __init__(self, sources, arrays, n_sem, start, finish):
        self.sources, self.arrays, self.n_sem, self.start, self.finish = sources, arrays, n_sem, start, finish


def _carry(riders, body, *, name, out_shape, grid=(), in_specs=None, out_specs=None, scratch_shapes=(),
           semantics=(), input_output_aliases=None):
    aliases = dict(input_output_aliases or {})
    if not riders:
        kw = {} if in_specs is None else dict(in_specs=in_specs, out_specs=out_specs)
        return pl.pallas_call(body, name=name, grid=grid, out_shape=out_shape, scratch_shapes=scratch_shapes,
                              input_output_aliases=aliases, compiler_params=_params(*semantics), **kw)
    single = not isinstance(out_shape, (list, tuple))
    shapes = [out_shape] if single else list(out_shape)
    n_out, n_scr = len(shapes), len(scratch_shapes)
    in_hbm = lambda a: pltpu.with_memory_space_constraint(a, pltpu.HBM)
    srcs = [in_hbm(a) for r in riders for a in r.sources]
    arrs = [in_hbm(a) for r in riders for a in r.arrays]
    vmem = pl.BlockSpec(memory_space=pltpu.VMEM)

    def run(*args):
        n_in = len(args)

        def wrapped(*refs):
            pos = n_in
            src_refs = refs[pos:pos + len(srcs)]
            pos += len(srcs) + len(arrs)
            outs = refs[pos:pos + n_out]
            pos += n_out
            arr_refs = refs[pos:pos + len(arrs)]
            pos += len(arrs)
            scratch = refs[pos:pos + n_scr]
            sems = refs[pos + n_scr:]
            first, last = True, True
            for d, size in enumerate(grid):
                first = jnp.logical_and(first, pl.program_id(d) == 0)
                last = jnp.logical_and(last, pl.program_id(d) == size - 1)

            def each(method):
                s0 = a0 = 0
                for k, r in enumerate(riders):
                    getattr(r, method)(src_refs[s0:s0 + len(r.sources)], arr_refs[a0:a0 + len(r.arrays)],
                                       sems[2 * k], sems[2 * k + 1])
                    s0, a0 = s0 + len(r.sources), a0 + len(r.arrays)

            pl.when(first)(lambda: each("start"))
            body(*refs[:n_in], *outs, *scratch)
            pl.when(last)(lambda: each("finish"))

        ins = [vmem] * n_in if in_specs is None else list(in_specs)
        if out_specs is None:
            o_specs = [vmem] * n_out
        else:
            o_specs = [out_specs] if single else list(out_specs)
        for k in range(len(arrs)):
            aliases[n_in + len(srcs) + k] = n_out + k
        res = pl.pallas_call(
            wrapped, name=name, grid=grid,
            in_specs=ins + [HBM_SPEC] * (len(srcs) + len(arrs)),
            out_specs=o_specs + [HBM_SPEC] * len(arrs),
            out_shape=shapes + [jax.ShapeDtypeStruct(a.shape, a.dtype) for a in arrs],
            scratch_shapes=list(scratch_shapes) + [pltpu.SemaphoreType.DMA((r.n_sem,)) for r in riders for _ in range(2)],
            input_output_aliases=aliases,
            compiler_params=_params(*["arbitrary"] * len(grid)),
        )(*args, *srcs, *arrs)
        core, rest = res[:n_out], list(res[n_out:])
        carried, a0 = [], 0
        for r in riders:
            carried.append(rest[a0:a0 + len(r.arrays)])
            a0 += len(r.arrays)
        return (core[0] if single else core), carried

    return run


def _mm(name, dims, a, a_spec, b, b_spec, grid, tile, outs, epilogue=None, extras=(), riders=()):
    nk, ne, no = grid[2], len(extras), len(outs)

    def body(a_ref, b_ref, *refs):
        e_refs, o_refs = refs[:ne], refs[ne:ne + no]
        part = lax.dot_general(a_ref[...], b_ref[...], dims, preferred_element_type=F32)

        def finish(acc):
            if epilogue is None:
                o_refs[0][...] = acc.astype(o_refs[0].dtype)
            else:
                epilogue(acc, [r[...] for r in e_refs], o_refs)

        if nk == 1:
            finish(part)
        else:
            acc_ref = refs[ne + no]
            k = pl.program_id(2)

            @pl.when(k == 0)
            def _():
                acc_ref[...] = part

            @pl.when(k > 0)
            def _():
                acc_ref[...] += part

            @pl.when(k == nk - 1)
            def _():
                finish(acc_ref[...])

    res = _carry(
        riders, body, name=name, grid=grid,
        in_specs=[a_spec, b_spec] + [s for _, s in extras],
        out_specs=[s for _, s in outs],
        out_shape=[s for s, _ in outs],
        scratch_shapes=[pltpu.VMEM(tile, F32)] if nk > 1 else [],
        semantics=("parallel", "parallel", "arbitrary"),
    )(a, b, *[e for e, _ in extras])
    res, carried = res if riders else (res, None)
    res = res[0] if no == 1 else res
    return (res, carried) if riders else res


def _tile_spec(tm, tn, col0=0):
    return pl.BlockSpec((tm, tn), lambda i, j, k: (i, j + col0))


def _out2d(m, n, dtype, tm, tn):
    return (jax.ShapeDtypeStruct((m, n), dtype), _tile_spec(tm, tn))


def _mm_fwd(name, a, w, tm, tn, tk, outs=None, epilogue=None, extras=(), col0=0, ncols=None, riders=()):
    m, kdim = a.shape
    if w.ndim == 3:
        per = w.shape[2] // tn
        n = ncols or N_SHARD * w.shape[2]
        w_spec = pl.BlockSpec((None, tk, tn), lambda i, j, k: ((j + col0) // per, k, (j + col0) % per))
    else:
        n = ncols or w.shape[1]
        w_spec = pl.BlockSpec((tk, tn), lambda i, j, k: (k, j + col0))
    if outs is None:
        outs = [_out2d(m, n, F32, tm, tn)]
    return _mm(name, NN, a, pl.BlockSpec((tm, tk), lambda i, j, k: (i, k)), w, w_spec,
               (m // tm, n // tn, kdim // tk), (tm, tn), outs, epilogue, extras, riders)


def _mm_bwd_x(name, g, g_spec, w, tm, tj, tc, m, n_contract, outs=None, epilogue=None, extras=(), riders=()):
    if w.ndim == 3:
        per = w.shape[2] // tc
        kdim = w.shape[1]
        w_spec = pl.BlockSpec((None, tj, tc), lambda i, j, n: (n // per, j, n % per))
    else:
        kdim = w.shape[0]
        w_spec = pl.BlockSpec((tj, tc), lambda i, j, n: (j, n))
    if outs is None:
        outs = [_out2d(m, kdim, F32, tm, tj)]
    return _mm(name, NT, g, g_spec, w, w_spec, (m // tm, kdim // tj, n_contract // tc),
               (tm, tj), outs, epilogue, extras, riders)


def _mm_bwd_w(name, a, g, g_spec, n, tk, tn, tm, sharded, riders=()):
    m, kdim = a.shape
    if sharded:
        per = (n // N_SHARD) // tn
        out = (jax.ShapeDtypeStruct((N_SHARD, kdim, n // N_SHARD), F32),
               pl.BlockSpec((None, tk, tn), lambda i, j, mm: (j // per, i, j % per)))
    else:
        out = (jax.ShapeDtypeStruct((kdim, n), F32), pl.BlockSpec((tk, tn), lambda i, j, mm: (i, j)))
    return _mm(name, TN, a, pl.BlockSpec((tm, tk), lambda i, j, mm: (mm, i)), g, g_spec,
               (kdim // tk, n // tn, m // tm), (tk, tn), [out], riders=riders)


def _ew(name, fn, tiles, fulls, outs, sums=(), ts=512, riders=()):
    tiles = [t if isinstance(t, tuple) else (t, pl.BlockSpec((ts, t.shape[1]), lambda i: (i, 0)))
             for t in tiles]
    s_rows = tiles[0][0].shape[-2]
    nt, nf, no = len(tiles), len(fulls), len(outs)

    def body(*refs):
        t_vals = [r[...] for r in refs[:nt]]
        f_vals = [r[...] for r in refs[nt:nt + nf]]
        o_refs, s_refs = refs[nt + nf:nt + nf + no], refs[nt + nf + no:]
        o_vals, s_vals = fn(t_vals, f_vals)
        for r, v in zip(o_refs, o_vals):
            r[...] = v.astype(r.dtype)
        for r, v in zip(s_refs, s_vals):
            part = jnp.sum(v, axis=0, keepdims=True)

            @pl.when(pl.program_id(0) == 0)
            def _():
                r[...] = part

            @pl.when(pl.program_id(0) > 0)
            def _():
                r[...] += part

    full_specs = [pl.BlockSpec(f.shape, lambda i, nd=f.ndim: (0,) * nd) for f in fulls]
    return _carry(
        riders, body, name=name, grid=(s_rows // ts,),
        in_specs=[s for _, s in tiles] + full_specs,
        out_specs=[pl.BlockSpec((ts, c), lambda i: (i, 0)) for c, _ in outs]
        + [pl.BlockSpec((1, c), lambda i: (0, 0)) for c in sums],
        out_shape=[jax.ShapeDtypeStruct((s_rows, c), dt) for c, dt in outs]
        + [jax.ShapeDtypeStruct((1, c), F32) for c in sums],
        semantics=("arbitrary",),
    )(*[t for t, _ in tiles], *fulls)


def _rms_fwd(name, x, g, riders=()):
    def fn(t, f):
        xv = t[0]
        r = lax.rsqrt(jnp.mean(xv * xv, axis=-1, keepdims=True) + EPS)
        return [xv * r * f[0]], []
    out = _ew(name, fn, [x], [g], [(x.shape[1], BF16)], riders=riders)
    return (out[0][0], out[1]) if riders else out[0]


def _split3(x):
    x1 = x.astype(BF16)
    r1 = x - x1.astype(F32)
    x2 = r1.astype(BF16)
    x3 = (r1 - x2.astype(F32)).astype(BF16)
    return x1, x2, x3


def _rel_class(cp):
    far = (cp < MAX_REL) | (cp > BAND)
    return jnp.where(far, 2 * MAX_REL, BAND - cp)


def _skew_rows(x, sign):
    row = lax.broadcasted_iota(jnp.int32, x.shape, 0)
    for b in range(CHUNK.bit_length() - 1):
        shift = (1 << b) if sign > 0 else SKEW_W - (1 << b)
        x = jnp.where((row >> b) & 1 == 1, pltpu.roll(x, shift, 1), x)
    return x


def _roll_lanes(x, shift):
    return x if shift % SKEW_W == 0 else pltpu.roll(x, shift % SKEW_W, 1)


N_START = 3


def _bias_expand(rel_bias, riders=()):
    rel = jnp.pad(rel_bias, ((0, 0), (0, REL_PAD - N_REL))).reshape(N_HEADS, 1, REL_PAD)

    def body(rel_ref, o_ref):
        cls = lax.broadcasted_iota(jnp.int32, (REL_PAD, SKEW_W), 0)
        cp = lax.broadcasted_iota(jnp.int32, (REL_PAD, SKEW_W), 1)
        onehot = (cls == _rel_class(cp)).astype(BF16)
        rel8 = jnp.broadcast_to(rel_ref[...], (8, REL_PAD))
        trow = sum(jnp.dot(p, onehot, preferred_element_type=F32) for p in _split3(rel8))[0:1]
        first = _skew_rows(jnp.broadcast_to(trow, (CHUNK, SKEW_W)), +1)
        full = jnp.concatenate([_roll_lanes(first, CHUNK * g) for g in range(SUPER // CHUNK)], axis=0)[:, :BAND]
        qc = lax.broadcasted_iota(jnp.int32, (SUPER, BAND), 0) // CHUNK
        kc = lax.broadcasted_iota(jnp.int32, (SUPER, BAND), 1) // CHUNK
        on_band = (kc >= qc) & (kc <= qc + N_PREV_CHUNKS)
        table = jnp.where(on_band, full, NEG_INF).T
        key = lax.broadcasted_iota(jnp.int32, (BAND, SUPER), 0)
        for t in range(N_START):
            o_ref[t] = jnp.where(key < (N_START - 1 - t) * SUPER, NEG_INF, table)

    return _carry(
        riders, body, name="bias_expand", grid=(N_HEADS,),
        in_specs=[pl.BlockSpec((None, 1, REL_PAD), lambda h: (h, 0, 0))],
        out_specs=pl.BlockSpec((N_START, None, BAND, SUPER), lambda h: (0, h, 0, 0)),
        out_shape=jax.ShapeDtypeStruct((N_START, N_HEADS, BAND, SUPER), F32),
        semantics=("arbitrary",),
    )(rel)


def _bias_reduce(dbias, riders=()):
    def body(d_ref, o_ref):
        x = jnp.concatenate([d_ref[...].T, jnp.zeros((SUPER, SKEW_W - BAND), F32)], axis=1)
        folded = sum(_roll_lanes(x[CHUNK * g:CHUNK * (g + 1)], -CHUNK * g) for g in range(SUPER // CHUNK))
        diag = jnp.sum(_skew_rows(folded, -1), axis=0, keepdims=True)
        cp = lax.broadcasted_iota(jnp.int32, (SKEW_W, REL_PAD), 0)
        cls = lax.broadcasted_iota(jnp.int32, (SKEW_W, REL_PAD), 1)
        onehot = (cls == _rel_class(cp)).astype(BF16)
        diag8 = jnp.broadcast_to(diag, (8, SKEW_W))
        o_ref[...] = sum(jnp.dot(p, onehot, preferred_element_type=F32) for p in _split3(diag8))[0:1]

    out = _carry(
        riders, body, name="bias_reduce", grid=(N_HEADS,),
        in_specs=[pl.BlockSpec((None, BAND, SUPER), lambda h: (h, 0, 0))],
        out_specs=pl.BlockSpec((None, 1, REL_PAD), lambda h: (h, 0, 0)),
        out_shape=jax.ShapeDtypeStruct((N_HEADS, 1, REL_PAD), F32),
        semantics=("arbitrary",),
    )(dbias)
    out, carried = out if riders else (out, None)
    out = out.reshape(N_HEADS, REL_PAD)[:, :N_REL]
    return (out, carried) if riders else out


HEADS_PER_STEP = 8
HEAD_COLS = HEADS_PER_STEP * HEAD_DIM
N_HEAD_GROUPS = N_HEADS // HEADS_PER_STEP


def _scores_t(qs, kn, bias_t):
    return jnp.concatenate([lax.dot_general(k, qs, NT, preferred_element_type=F32) for k in kn], axis=0) + bias_t


def _bias_spec():
    return pl.BlockSpec((None, HEADS_PER_STEP, BAND, SUPER), lambda hg, i: (jnp.minimum(i, N_START - 1), hg, 0, 0))


def _band_specs(nb, col0, clamp_hi):
    def spec(d):
        def index(hg, i):
            blk = jnp.maximum(i - d, 0)
            if clamp_hi:
                blk = jnp.minimum(blk, nb - 1)
            return (blk, col0 + hg)
        return pl.BlockSpec((SUPER, HEAD_COLS), index)
    return [spec(2), spec(1), spec(0)]


def _head_sums(y):
    same_head = (lax.broadcasted_iota(jnp.int32, (MXU_DIM, MXU_DIM), 0) // HEAD_DIM
                 == lax.broadcasted_iota(jnp.int32, (MXU_DIM, MXU_DIM), 1) // HEAD_DIM).astype(BF16)
    sums = []
    for c0 in range(0, y.shape[1], MXU_DIM):
        chunk = y[:, c0:c0 + MXU_DIM]
        hi = chunk.astype(BF16)
        lo = (chunk - hi.astype(F32)).astype(BF16)
        sums.append(jnp.dot(hi, same_head, preferred_element_type=F32)
                    + jnp.dot(lo, same_head, preferred_element_type=F32))
    return jnp.concatenate(sums, axis=1)


def _head_unit(x):
    r = lax.rsqrt(_head_sums(x * x) * (1.0 / HEAD_DIM) + EPS)
    return x * r, r


def _head(hh):
    return slice(HEAD_DIM * hh, HEAD_DIM * (hh + 1))


def _key_block(j):
    return slice(SUPER * j, SUPER * (j + 1))


LSE_ROWS = 8


def _attn_fwd(qkn, v, bias, riders=()):
    s_len = qkn.shape[0]
    nb = s_len // SUPER

    def body(q_ref, k0, k1, k2, v0, v1, v2, b_ref, o_ref, lse_ref):
        outs = []
        v_t = [v0[...].T, v1[...].T, v2[...].T]

        def probabilities(hh):
            sl = _head(hh)
            s = _scores_t(q_ref[:, sl], [k0[:, sl], k1[:, sl], k2[:, sl]], b_ref[hh])
            m = jnp.max(s, axis=0, keepdims=True)
            e = jnp.exp(s - m)
            l = jnp.sum(e, axis=0, keepdims=True)
            lse_ref[hh:hh + 1, :] = m + jnp.log(l)
            return e.astype(BF16), 1.0 / l, sl

        def weighted_values(e, inv_l, sl):
            outs.append(sum(jnp.dot(v_t[j][sl, :], e[_key_block(j), :], preferred_element_type=F32)
                            for j in range(3)) * inv_l)

        ready = probabilities(0)
        for hh in range(1, HEADS_PER_STEP):
            following = probabilities(hh)
            weighted_values(*ready)
            ready = following
        weighted_values(*ready)
        o_ref[...] = jnp.concatenate(outs, axis=0).T.astype(o_ref.dtype)

    return _carry(
        riders, body, name="attn_fwd", grid=(N_HEAD_GROUPS, nb),
        in_specs=[pl.BlockSpec((SUPER, HEAD_COLS), lambda hg, i: (i, hg))]
        + _band_specs(nb, N_HEAD_GROUPS, False) + _band_specs(nb, 0, False) + [_bias_spec()],
        out_specs=[pl.BlockSpec((SUPER, HEAD_COLS), lambda hg, i: (i, hg)),
                   pl.BlockSpec((None, LSE_ROWS, SUPER), lambda hg, i: (hg, 0, i))],
        out_shape=[jax.ShapeDtypeStruct((s_len, D_MODEL), BF16),
                   jax.ShapeDtypeStruct((N_HEAD_GROUPS, LSE_ROWS, s_len), F32)],
        semantics=("parallel", "arbitrary"),
    )(qkn, qkn, qkn, qkn, v, v, v, bias)


def _attn_bwd(qkn, v, out, d_out, bias, lse, riders=()):
    s_len = qkn.shape[0]
    nb = s_len // SUPER

    def body(q_ref, k0, k1, k2, v0, v1, v2, o_ref, do_ref, b_ref, lse_ref, dp_ref, db_ref, aq_ref, ak_ref, av_ref):
        i = pl.program_id(1)

        @pl.when(i == 0)
        def _():
            aq_ref[...] = jnp.zeros_like(aq_ref)
            ak_ref[...] = jnp.zeros_like(ak_ref)
            av_ref[...] = jnp.zeros_like(av_ref)
            db_ref[...] = jnp.zeros_like(db_ref)

        @pl.when(i < nb)
        def _():
            dq, dk, dv = [], [[], [], []], [[], [], []]
            ones = jnp.ones((8, HEAD_DIM), BF16)
            k_t = [k0[...].T, k1[...].T, k2[...].T]

            def softmax_grad(hh):
                sl = _head(hh)
                qs, do = q_ref[:, sl], do_ref[:, sl]
                kn = [k0[:, sl], k1[:, sl], k2[:, sl]]
                prod = do.astype(F32) * o_ref[:, sl].astype(F32)
                hi = prod.astype(BF16)
                lo = (prod - hi.astype(F32)).astype(BF16)
                delta = (lax.dot_general(ones, hi, NT, preferred_element_type=F32)
                         + lax.dot_general(ones, lo, NT, preferred_element_type=F32))[0:1]
                lse_row = lse_ref[hh:hh + 1, :]
                pb, dsb = [], []
                for j, (kj, vj) in enumerate(zip(kn, (v0, v1, v2))):
                    rows = _key_block(j)
                    p = jnp.exp(lax.dot_general(kj, qs, NT, preferred_element_type=F32) + b_ref[hh, rows, :] - lse_row)
                    ds = p * (lax.dot_general(vj[:, sl], do, NT, preferred_element_type=F32) - delta)
                    db_ref[hh, rows, :] += ds
                    pb.append(p.astype(BF16))
                    dsb.append(ds.astype(BF16))
                return pb, dsb, qs, do, sl

            def operand_grads(pb, dsb, qs, do, sl):
                dq.append(sum(jnp.dot(k_t[j][sl, :], dsb[j], preferred_element_type=F32) for j in range(3)))
                for j in range(3):
                    dv[j].append(jnp.dot(pb[j], do, preferred_element_type=F32))
                    dk[j].append(jnp.dot(dsb[j], qs, preferred_element_type=F32))

            ready = softmax_grad(0)
            for hh in range(1, HEADS_PER_STEP):
                following = softmax_grad(hh)
                operand_grads(*ready)
                ready = following
            operand_grads(*ready)
            aq_ref[i % 3] = jnp.concatenate(dq, axis=0).T
            for j in range(3):
                slot = (i + 1 + j) % 3
                if j < 2:
                    ak_ref[slot] += jnp.concatenate(dk[j], axis=1)
                    av_ref[slot] += jnp.concatenate(dv[j], axis=1)
                else:
                    ak_ref[slot] = jnp.concatenate(dk[j], axis=1)
                    av_ref[slot] = jnp.concatenate(dv[j], axis=1)

        slot = (i + 1) % 3
        dp_ref[0] = aq_ref[slot].astype(dp_ref.dtype)
        dp_ref[1] = ak_ref[slot].astype(dp_ref.dtype)
        dp_ref[2] = av_ref[slot].astype(dp_ref.dtype)

    def qrow(hg, i):
        return (jnp.minimum(i, nb - 1), hg)

    return _carry(
        riders, body, name="attn_bwd", grid=(N_HEAD_GROUPS, nb + 2),
        in_specs=[pl.BlockSpec((SUPER, HEAD_COLS), qrow)]
        + _band_specs(nb, N_HEAD_GROUPS, True) + _band_specs(nb, 0, True)
        + [pl.BlockSpec((SUPER, HEAD_COLS), qrow), pl.BlockSpec((SUPER, HEAD_COLS), qrow), _bias_spec(),
           pl.BlockSpec((None, LSE_ROWS, SUPER), lambda hg, i: (hg, 0, jnp.minimum(i, nb - 1)))],
        out_specs=[pl.BlockSpec((3, SUPER, HEAD_COLS), lambda hg, i: (0, jnp.maximum(i - 2, 0), hg)),
                   pl.BlockSpec((HEADS_PER_STEP, BAND, SUPER), lambda hg, i: (hg, 0, 0))],
        out_shape=[jax.ShapeDtypeStruct((6, s_len, D_MODEL), BF16),
                   jax.ShapeDtypeStruct((N_HEADS, BAND, SUPER), F32)],
        scratch_shapes=[pltpu.VMEM((3, SUPER, HEAD_COLS), F32)] * 3,
        semantics=("parallel", "arbitrary"),
    )(qkn, qkn, qkn, qkn, v, v, v, out, d_out, bias, lse)


def _qk_norm_bwd(dproj6, qk_raw, gq, gk):
    s_len = qk_raw.shape[0]
    ts = min(1024, s_len)

    nsteps = s_len // ts
    half = D_MODEL // 2

    def body(d_ref, raw_ref, gq_ref, gk_ref, o_ref, dgq_ref, dgk_ref, acc_ref):
        step = pl.program_id(0)

        @pl.when(step == 0)
        def _():
            acc_ref[...] = jnp.zeros_like(acc_ref)

        for piece, (g_ref, scale) in enumerate(((gq_ref, QK_SCALE), (gk_ref, 1.0))):
            for c0 in (0, half):
                xhat, r = _head_unit(raw_ref[:, piece * D_MODEL + c0:piece * D_MODEL + c0 + half].astype(F32))
                dn = d_ref[piece, :, c0:c0 + half].astype(F32) * scale
                u = dn * g_ref[...]
                dx = r * (u - xhat * (_head_sums(u * xhat) * (1.0 / HEAD_DIM)))
                o_ref[piece, :, c0:c0 + half] = dx.astype(o_ref.dtype)
                acc_ref[piece:piece + 1, c0:c0 + half] += jnp.sum(dn * xhat, axis=0, keepdims=True)

        @pl.when(step == nsteps - 1)
        def _():
            lane = lax.broadcasted_iota(jnp.int32, (D_MODEL, LANE), 0) % HEAD_DIM
            fold = (lane == lax.broadcasted_iota(jnp.int32, (D_MODEL, LANE), 1)).astype(BF16)
            tot = sum(jnp.dot(p, fold, preferred_element_type=F32) for p in _split3(acc_ref[...]))
            dgq_ref[...] = tot[0:1, :HEAD_DIM]
            dgk_ref[...] = tot[1:2, :HEAD_DIM]

    gain = pl.BlockSpec((1, half), lambda i: (0, 0))
    small = pl.BlockSpec((1, HEAD_DIM), lambda i: (0, 0))
    per_head = lambda g: jnp.tile(g, (1, half // HEAD_DIM))
    return pl.pallas_call(
        body, name="qk_norm_bwd", grid=(nsteps,),
        in_specs=[pl.BlockSpec((2, ts, D_MODEL), lambda i: (0, i, 0)),
                  pl.BlockSpec((ts, 2 * D_MODEL), lambda i: (i, 0)), gain, gain],
        out_specs=[pl.BlockSpec((2, ts, D_MODEL), lambda i: (0, i, 0)), small, small],
        out_shape=[jax.ShapeDtypeStruct(dproj6.shape, dproj6.dtype),
                   jax.ShapeDtypeStruct((1, HEAD_DIM), F32), jax.ShapeDtypeStruct((1, HEAD_DIM), F32)],
        scratch_shapes=[pltpu.VMEM((8, D_MODEL), F32)],
        input_output_aliases={0: 0},
        compiler_params=_params("arbitrary"),
    )(dproj6, qk_raw, per_head(gq), per_head(gk))


CONV_ROWS = 512
HALO = 16


def _rows_with_halo(ref, r0, n, front, s_len):
    zeros = jnp.zeros((HALO, ref.shape[1]), F32)
    if front:
        return (jnp.concatenate([zeros, ref[0:n, :].astype(F32)], axis=0) if r0 == 0
                else ref[r0 - HALO:r0 + n, :].astype(F32))
    return (jnp.concatenate([ref[r0:r0 + n, :].astype(F32), zeros], axis=0) if r0 + n == s_len
            else ref[r0:r0 + n + HALO, :].astype(F32))


def _earlier(ext, k):
    return pltpu.roll(ext, k, 0)[HALO:]


def _later(ext, k):
    n = ext.shape[0]
    return pltpu.roll(ext, n - k, 0)[:n - HALO]


def _conv_cols(col0):
    return lambda s_len: pl.BlockSpec((s_len, LANE), lambda j: (0, col0 + j))


def _conv_fwd(proj, conv_w, conv_b, riders=()):
    s_len = proj.shape[0]

    def body(bg_ref, cg_ref, xc_ref, w_ref, b_ref, o_ref):
        w = [w_ref[t:t + 1, :] for t in range(3)]
        for r0 in range(0, s_len, CONV_ROWS):
            u = _rows_with_halo(cg_ref, r0, CONV_ROWS, True, s_len) * \
                _rows_with_halo(xc_ref, r0, CONV_ROWS, True, s_len)
            conv = b_ref[...] + w[0] * _earlier(u, 2) + w[1] * _earlier(u, 1) + w[2] * u[HALO:]
            o_ref[r0:r0 + CONV_ROWS, :] = (bg_ref[r0:r0 + CONV_ROWS, :].astype(F32) * conv).astype(o_ref.dtype)

    return _carry(
        riders, body, name="conv_fwd", grid=(D_MODEL // LANE,),
        in_specs=[_conv_cols(0)(s_len), _conv_cols(8)(s_len), _conv_cols(16)(s_len),
                  pl.BlockSpec((3, LANE), lambda j: (0, j)), pl.BlockSpec((1, LANE), lambda j: (0, j))],
        out_specs=pl.BlockSpec((s_len, LANE), lambda j: (0, j)),
        out_shape=jax.ShapeDtypeStruct((s_len, D_MODEL), BF16),
        semantics=("parallel",),
    )(proj, proj, proj, conv_w, conv_b)


def _conv_bwd(dproj6, dy, proj, conv_w, conv_b, riders=()):
    s_len = proj.shape[0]

    def body(dy_ref, bg_ref, cg_ref, xc_ref, w_ref, b_ref, _, dp_ref, dw_ref):
        w = [w_ref[t:t + 1, :] for t in range(3)]
        acc = [jnp.zeros((1, LANE), F32) for _ in range(4)]
        for r0 in range(0, s_len, CONV_ROWS):
            rows = slice(r0, r0 + CONV_ROWS)
            u = _rows_with_halo(cg_ref, r0, CONV_ROWS, True, s_len) * \
                _rows_with_halo(xc_ref, r0, CONV_ROWS, True, s_len)
            u2, u1, u0 = _earlier(u, 2), _earlier(u, 1), u[HALO:]
            conv = b_ref[...] + w[0] * u2 + w[1] * u1 + w[2] * u0
            dp_ref[0, rows, :] = (dy_ref[rows, :].astype(F32) * conv).astype(dp_ref.dtype)
            dconv_ext = _rows_with_halo(dy_ref, r0, CONV_ROWS, False, s_len) * \
                _rows_with_halo(bg_ref, r0, CONV_ROWS, False, s_len)
            dconv = dconv_ext[:CONV_ROWS]
            for t, term in enumerate([dconv * u2, dconv * u1, dconv * u0, dconv]):
                acc[t] = acc[t] + jnp.sum(term, axis=0, keepdims=True)
            du = w[2] * dconv + w[1] * _later(dconv_ext, 1) + w[0] * _later(dconv_ext, 2)
            dp_ref[1, rows, :] = (du * xc_ref[rows, :].astype(F32)).astype(dp_ref.dtype)
            dp_ref[2, rows, :] = (du * cg_ref[rows, :].astype(F32)).astype(dp_ref.dtype)
        dw_ref[...] = jnp.zeros_like(dw_ref)
        for t in range(4):
            dw_ref[t:t + 1, :] = acc[t]

    return _carry(
        riders, body, name="conv_bwd", grid=(D_MODEL // LANE,),
        in_specs=[pl.BlockSpec((s_len, LANE), lambda j: (0, j)),
                  _conv_cols(0)(s_len), _conv_cols(8)(s_len), _conv_cols(16)(s_len),
                  pl.BlockSpec((3, LANE), lambda j: (0, j)), pl.BlockSpec((1, LANE), lambda j: (0, j)),
                  pl.BlockSpec(memory_space=pl.ANY)],
        out_specs=[pl.BlockSpec((3, s_len, LANE), lambda j: (1, 0, j)),
                   pl.BlockSpec((8, LANE), lambda j: (0, j))],
        out_shape=[jax.ShapeDtypeStruct(dproj6.shape, dproj6.dtype),
                   jax.ShapeDtypeStruct((8, D_MODEL), F32)],
        input_output_aliases={6: 0},
        semantics=("parallel",),
    )(dy, proj, proj, proj, conv_w, conv_b, dproj6)


def _d_norm_input(name, terms, x, g, dres, out_dtype, riders=()):
    s_len, kdim = terms[0][0].shape[1], terms[0][1].shape[1]
    tm, chunk = 512, 512
    n = len(terms)

    def body(*refs):
        x_ref, g_ref, r_ref, o_ref, dg_ref = refs[2 * n:]
        dh = jnp.zeros((tm, kdim), F32)
        for p_ref, w_ref in zip(refs[:n], refs[n:2 * n]):
            n_piece, _, width = p_ref.shape
            per_shard = w_ref.shape[2]
            for c0 in range(0, n_piece * width, chunk):
                dh = dh + lax.dot_general(p_ref[c0 // width, :, c0 % width:c0 % width + chunk],
                                          w_ref[c0 // per_shard, :, c0 % per_shard:c0 % per_shard + chunk],
                                          NT, preferred_element_type=F32)
        xv = x_ref[...]
        r = lax.rsqrt(jnp.mean(xv * xv, axis=-1, keepdims=True) + EPS)
        xhat = xv * r
        u = dh * g_ref[...]
        dx = r * (u - xhat * jnp.mean(u * xhat, axis=-1, keepdims=True)) + r_ref[...].astype(F32)
        o_ref[...] = dx.astype(o_ref.dtype)
        part = jnp.sum(dh * xhat, axis=0, keepdims=True)

        @pl.when(pl.program_id(0) == 0)
        def _():
            dg_ref[...] = part

        @pl.when(pl.program_id(0) > 0)
        def _():
            dg_ref[...] += part

    tile = pl.BlockSpec((tm, kdim), lambda i: (i, 0))
    vec = pl.BlockSpec((1, kdim), lambda i: (0, 0))
    return _carry(
        riders, body, name=name, grid=(s_len // tm,),
        in_specs=[pl.BlockSpec((p.shape[0], tm, p.shape[2]), lambda i: (0, i, 0)) for p, _ in terms]
        + [pl.BlockSpec(w.shape, lambda i: (0, 0, 0), pipeline_mode=pl.Buffered(1)) for _, w in terms]
        + [tile, vec, tile],
        out_specs=[tile, vec],
        out_shape=[jax.ShapeDtypeStruct((s_len, kdim), out_dtype), jax.ShapeDtypeStruct((1, kdim), F32)],
        semantics=("arbitrary",),
    )(*[p for p, _ in terms], *[w for _, w in terms], x, g, dres)


def _local_grads(x, target, norm1_g, q_norm_g, k_norm_g, rel_bias, conv_b, b_gate, norm2_g, comm):
    s_len = x.shape[0]
    tm = min(1024, s_len)
    tb = min(2048, s_len)
    row = lambda v: v.reshape(1, -1)

    def carrying(at, fn, *args, **kw):
        riders = comm.riders(at)
        out = fn(*args, riders=riders, **kw)
        if riders:
            out, carried = out
            comm.done(at, carried)
        return out

    bias = carrying("bias_expand", _bias_expand, rel_bias)
    h = carrying("norm1", _rms_fwd, "norm1", x, row(norm1_g))
    w_in3, conv_w = comm.early_weights()
    gq, gk = row(q_norm_g), row(k_norm_g)

    def qk_epi(acc, e, o):
        o[0][...] = acc.astype(BF16)
        gain = jnp.where(pl.program_id(1) < 2, e[0] * QK_SCALE, e[1])
        o[1][...] = (_head_unit(acc)[0] * gain).astype(BF16)
    conv_in = carrying("proj_conv", _mm_fwd, "proj_conv", h, w_in3, tb, 1536, D_MODEL, col0=2, ncols=3 * D_MODEL,
                       outs=[_out2d(s_len, 3 * D_MODEL, BF16, tb, 1536)])
    small = pl.BlockSpec((1, 512), lambda i, j, k: (0, 0))
    per_head = lambda g: jnp.tile(g, (1, 512 // HEAD_DIM))
    qk_raw, qkn = carrying("proj_qk", _mm_fwd, "proj_qk", h, w_in3, tm, 512, D_MODEL, ncols=2 * D_MODEL,
                           epilogue=qk_epi, outs=[_out2d(s_len, 2 * D_MODEL, BF16, tm, 512)] * 2,
                           extras=[(per_head(gq), small), (per_head(gk), small)])
    v = carrying("proj_v", _mm_fwd, "proj_v", h, w_in3, tb, 512, D_MODEL, col0=4, ncols=D_MODEL,
                 outs=[_out2d(s_len, D_MODEL, BF16, tb, 512)])
    w_gate3 = comm.gate_weight()

    def gate_epi(acc, e, o):
        o[0][...] = jax.nn.sigmoid(acc + e[0]).astype(BF16)
    gates = carrying("gates", _mm_fwd, "gates", h, w_gate3, tb, 512, D_MODEL, epilogue=gate_epi,
                     outs=[_out2d(s_len, 2 * D_MODEL, BF16, tb, 512)],
                     extras=[(row(b_gate), pl.BlockSpec((1, 512), lambda i, j, k: (0, j)))])

    attn, lse = carrying("attn_fwd", _attn_fwd, qkn, v, bias)
    yconv = carrying("conv_fwd", _conv_fwd, conv_in, conv_w, row(conv_b))
    w_ap, w_cp, w_out, w_up3, w_down = comm.late_weights()
    tw = 1024
    ya = _mm_fwd("attn_proj", attn, w_ap, tm, tw, D_MODEL, outs=[_out2d(s_len, D_MODEL, BF16, tm, tw)])

    def merge_epi(acc, e, o):
        ya_v, ga, gc = [t.astype(F32) for t in e]
        o[0][...] = acc.astype(BF16)
        o[1][...] = (ga * ya_v + gc * acc).astype(BF16)
    gate_a, gate_c = (gates, _tile_spec(tm, tw, 0)), (gates, _tile_spec(tm, tw, 1))
    yc, merged = _mm_fwd("conv_proj", yconv, w_cp, tm, tw, D_MODEL, epilogue=merge_epi,
                         outs=[_out2d(s_len, D_MODEL, BF16, tm, tw), _out2d(s_len, D_MODEL, BF16, tm, tw)],
                         extras=[(ya, _tile_spec(tm, tw)), gate_a, gate_c])

    def res_epi(acc, e, o):
        x1_v = e[0] + acc
        o[0][...] = x1_v
        r = lax.rsqrt(jnp.mean(x1_v * x1_v, axis=-1, keepdims=True) + EPS)
        o[1][...] = (x1_v * r * e[1]).astype(BF16)
    assert tw == D_MODEL
    x1, h2 = _mm_fwd("out_proj", merged, w_out, tm, tw, D_MODEL, epilogue=res_epi,
                     outs=[_out2d(s_len, D_MODEL, F32, tm, tw), _out2d(s_len, D_MODEL, BF16, tm, tw)],
                     extras=[(x, _tile_spec(tm, tw)), (row(norm2_g), pl.BlockSpec((1, tw), lambda i, j, k: (0, 0)))])

    def up_epi(acc, e, o):
        o[0][...] = jnp.square(jnp.maximum(acc, 0.0)).astype(BF16)
    act = _mm_fwd("mlp_up", h2, w_up3, tb, tw, D_MODEL, epilogue=up_epi, outs=[_out2d(s_len, D_FF, BF16, tb, tw)])

    def loss_epi(acc, e, o):
        err = e[0] + acc - e[1]
        o[0][...] = (err * (1.0 / D_MODEL)).astype(BF16)
        sq = err * err
        part = sq[:, 0:LANE]
        for c0 in range(LANE, D_MODEL, LANE):
            part = part + sq[:, c0:c0 + LANE]
        o[1][...] = jnp.sum(part.reshape(tl // 8, 8, LANE), axis=0)
    tl = 512
    dy_b, loss_part = _mm_fwd(
        "mlp_down", act, w_down, tl, D_MODEL, D_FF, epilogue=loss_epi,
        outs=[_out2d(s_len, D_MODEL, BF16, tl, D_MODEL),
              (jax.ShapeDtypeStruct((8 * (s_len // tl), LANE), F32), pl.BlockSpec((8, LANE), lambda i, j, k: (i, 0)))],
        extras=[(x1, _tile_spec(tl, D_MODEL)), (target, _tile_spec(tl, D_MODEL))])

    def dup_epi(acc, e, o):
        o[0][...] = (acc * (2.0 * jnp.sqrt(e[0].astype(F32)))).astype(BF16)
    full = lambda cols: pl.BlockSpec((tm, cols), lambda i, j, n: (i, n))
    tokens = lambda cols: pl.BlockSpec((s_len, cols), lambda i, j, m: (m, j))
    dup = _mm_bwd_x("d_act", dy_b, full(D_MODEL), w_down, tm, tw, D_MODEL, s_len, D_MODEL, epilogue=dup_epi,
                    outs=[_out2d(s_len, D_FF, BF16, tm, tw)], extras=[(act, _tile_spec(tm, tw))])
    g_down = _mm_bwd_w("g_down", act, dy_b, tokens(D_MODEL), D_MODEL, 512, D_MODEL, s_len, False)
    g_up = _mm_bwd_w("g_up", h2, dup, tokens(512), D_FF, D_MODEL, 512, s_len, True)
    comm.grads_ready("mlp", dict(w_down=g_down, w_up=g_up))
    dx1_b, dg2 = carrying("d_h2", _d_norm_input, "d_h2", [(dup.reshape(1, s_len, D_FF), w_up3)],
                          x1, row(norm2_g), dy_b, BF16)

    def dmerge_epi(acc, e, o):
        ya_v, yc_v, ga, gc = [t.astype(F32) for t in e]
        o[0][...] = (acc * ga).astype(BF16)
        o[1][...] = (acc * gc).astype(BF16)
        o[2][0] = (acc * ya_v * ga * (1.0 - ga)).astype(BF16)
        o[2][1] = (acc * yc_v * gc * (1.0 - gc)).astype(BF16)
    dya, dyc, dgp2 = _mm_bwd_x(
        "d_merged", dx1_b, full(D_MODEL), w_out, tm, tw, D_MODEL, s_len, D_MODEL, epilogue=dmerge_epi,
        outs=[_out2d(s_len, D_MODEL, BF16, tm, tw), _out2d(s_len, D_MODEL, BF16, tm, tw),
              (jax.ShapeDtypeStruct((2, s_len, D_MODEL), BF16), pl.BlockSpec((2, tm, tw), lambda i, j, n: (0, i, j)))],
        extras=[(ya, _tile_spec(tm, tw)), (yc, _tile_spec(tm, tw)), gate_a, gate_c])
    g_out = _mm_bwd_w("g_out", merged, dx1_b, tokens(512), D_MODEL, D_MODEL, 512, s_len, False)
    d_attn = _mm_bwd_x("d_attn", dya, full(D_MODEL), w_ap, tm, tw, D_MODEL, s_len, D_MODEL,
                       outs=[_out2d(s_len, D_MODEL, BF16, tm, tw)])
    g_ap = _mm_bwd_w("g_attn_proj", attn, dya, tokens(512), D_MODEL, D_MODEL, 512, s_len, False)
    d_yconv = _mm_bwd_x("d_yconv", dyc, full(D_MODEL), w_cp, tm, tw, D_MODEL, s_len, D_MODEL,
                        outs=[_out2d(s_len, D_MODEL, BF16, tm, tw)])
    g_cp = _mm_bwd_w("g_conv_proj", yconv, dyc, tokens(512), D_MODEL, D_MODEL, 512, s_len, False)
    piece = lambda width: (lambda blk: (blk * width) // D_MODEL, lambda blk: (blk * width % D_MODEL) // width)
    pc, cb = piece(512)
    pieces = pl.BlockSpec((None, s_len, 512), lambda i, j, m: (pc(j), m, cb(j)))
    g_gate = _mm_bwd_w("g_gate", h, dgp2, pieces, 2 * D_MODEL, D_MODEL, 512, s_len, True)
    comm.grads_ready("proj", dict(w_out=g_out, w_attn_proj=g_ap, w_conv_proj=g_cp, w_gate=g_gate))

    dproj6, dbias = carrying("attn_bwd", _attn_bwd, qkn, v, attn, d_attn, bias, lse)
    dproj6, dgq, dgk = _qk_norm_bwd(dproj6, qk_raw, gq, gk)
    dproj6, dconv_wb = carrying("conv_bwd", _conv_bwd, dproj6, d_yconv, conv_in, conv_w, row(conv_b))

    g_in = carrying("g_in", _mm_bwd_w, "g_in", h, dproj6, pieces, 6 * D_MODEL, D_MODEL, 512, s_len, True)
    comm.grads_ready("in", dict(w_in=g_in))
    d_rel = carrying("bias_reduce", _bias_reduce, dbias)
    grad_x, dg1 = carrying("d_h", _d_norm_input, "d_h", [(dproj6, w_in3), (dgp2, w_gate3)],
                           x, row(norm1_g), dx1_b, F32)

    def bsum(t, f):
        return [], [t[0].astype(F32), t[1].astype(F32)]
    ts = 512
    db_a, db_c = carrying("b_gate_sum", _ew, "b_gate_sum", bsum,
                          [(dgp2, pl.BlockSpec((None, ts, D_MODEL), lambda i: (0, i, 0))),
                           (dgp2, pl.BlockSpec((None, ts, D_MODEL), lambda i: (1, i, 0)))],
                          [], [], sums=[D_MODEL, D_MODEL], ts=ts)

    big = dict(w_in=g_in, w_attn_proj=g_ap, w_conv_proj=g_cp, w_gate=g_gate, w_out=g_out,
               w_up=g_up, w_down=g_down)
    small = dict(norm1_g=dg1, norm2_g=dg2, conv_wb=dconv_wb, b_gate=(db_a, db_c),
                 q_norm_g=dgq, k_norm_g=dgk, rel_bias=d_rel)
    return loss_part, grad_x, big, small


def _finish_loss(loss_part):
    def body(l_ref, lo_ref):
        total = jnp.sum(jnp.sum(l_ref[...], axis=0, keepdims=True), axis=1, keepdims=True)
        lo_ref[...] = jnp.broadcast_to(total * (0.5 / D_MODEL), lo_ref.shape)

    return pl.pallas_call(body, name="finish_loss", out_shape=jax.ShapeDtypeStruct((8, LANE), F32))(loss_part)


def _place():
    return lax.axis_index("x"), lax.axis_index("y"), lax.axis_index("c")


def _other_chips(x, y):
    return [(1 - x, y), (x, 1 - y), (1 - x, 1 - y)]


def _cast_into_slot(name, where, w):
    r, cols = w.shape
    ts = 256

    def body(w_ref, x_ref, o_ref):
        o_ref[...] = x_ref[...].astype(o_ref.dtype)

    return pl.pallas_call(
        body, name=name,
        grid_spec=pltpu.PrefetchScalarGridSpec(
            num_scalar_prefetch=1, grid=(r // ts,),
            in_specs=[pl.BlockSpec((ts, cols), lambda i, w: (i, 0))],
            out_specs=pl.BlockSpec((None, ts, cols), lambda i, w: (w[0], i, 0))),
        out_shape=jax.ShapeDtypeStruct((N_SHARD, r, cols), BF16),
        compiler_params=_params("parallel"),
    )(where, w)


def _remote(src, dst, send, recv, k, to):
    return pltpu.make_async_remote_copy(src_ref=src, dst_ref=dst, send_sem=send.at[k], recv_sem=recv.at[k],
                                        device_id=to, device_id_type=MESH)


def _gather_riders(slots):
    n = len(slots)

    def copy(refs, w, j, shard, which, send, recv, to):
        hr = slots[w].shape[1] // 2
        ref = refs[w].at[shard, pl.ds(which * hr, hr)]
        return _remote(ref, ref, send, recv, 3 * w + j, to)

    def for_each_peer(fn):
        x, y, c = _place()
        for w in range(n):
            for j, chip in enumerate(_other_chips(x, y)):
                fn(w, j, 2 * x + y, 2 * chip[0] + chip[1], c, (*chip, c), (x, y, 1 - c))

    def chips_start(_, refs, send, recv):
        for_each_peer(lambda w, j, mine, theirs, c, peer, sib: copy(refs, w, j, mine, c, send, recv, peer).start())

    def chips_finish(_, refs, send, recv):
        for_each_peer(lambda w, j, mine, theirs, c, peer, sib: copy(refs, w, j, theirs, c, send, recv, peer).wait_recv())
        for_each_peer(lambda w, j, mine, theirs, c, peer, sib: copy(refs, w, j, mine, c, send, recv, peer).wait_send())

    def sibling_start(_, refs, send, recv):
        for_each_peer(lambda w, j, mine, theirs, c, peer, sib: copy(refs, w, j, theirs, c, send, recv, sib).start())

    def sibling_finish(_, refs, send, recv):
        for_each_peer(lambda w, j, mine, theirs, c, peer, sib: copy(refs, w, j, theirs, 1 - c, send, recv, sib).wait_recv())
        for_each_peer(lambda w, j, mine, theirs, c, peer, sib: copy(refs, w, j, theirs, c, send, recv, sib).wait_send())

    return (lambda arrays: _Rider([], arrays, 3 * n, chips_start, chips_finish),
            lambda arrays: _Rider([], arrays, 3 * n, sibling_start, sibling_finish))


def _pair_exchange_rider(grads, landing):
    n = len(grads)

    def copies(srcs, dsts, send, recv):
        x, y, c = _place()
        out = []
        for w in range(n):
            hr = grads[w].shape[1] // 2
            out.append(_remote(srcs[w].at[:, pl.ds((1 - c) * hr, hr)], dsts[w], send, recv, w, (x, y, 1 - c)))
        return out

    def start(srcs, dsts, send, recv):
        for cp in copies(srcs, dsts, send, recv):
            cp.start()

    def finish(srcs, dsts, send, recv):
        for cp in copies(srcs, dsts, send, recv):
            cp.wait()

    return _Rider(grads, landing, n, start, finish)


def _row_tile(hr):
    return min(hr, 256)


def _pair_add(name, where, grad, got):
    _, hr, cols = got.shape
    tr = _row_tile(hr)
    nblk = hr // tr

    def body(w_ref, g_ref, r_ref, o_ref):
        o_ref[...] = (g_ref[...] + r_ref[...]).astype(o_ref.dtype)

    other = lambda k, w: (w[0] + 1 + k) % N_SHARD
    return pl.pallas_call(
        body, name=name,
        grid_spec=pltpu.PrefetchScalarGridSpec(
            num_scalar_prefetch=1, grid=(N_SHARD - 1, nblk),
            in_specs=[pl.BlockSpec((None, tr, cols), lambda k, i, w: (other(k, w), w[1] * nblk + i, 0)),
                      pl.BlockSpec((None, tr, cols), lambda k, i, w: (other(k, w), i, 0))],
            out_specs=pl.BlockSpec((None, tr, cols), lambda k, i, w: (other(k, w), i, 0))),
        out_shape=jax.ShapeDtypeStruct(got.shape, BF16),
        compiler_params=_params("parallel", "parallel"),
    )(where, grad, got)


def _chip_exchange_rider(partials, landing):
    n = len(partials)

    def copies(srcs, dsts, send, recv):
        x, y, c = _place()
        return [_remote(srcs[w].at[2 * chip[0] + chip[1]], dsts[w].at[j], send, recv, 3 * w + j, (*chip, c))
                for w in range(n) for j, chip in enumerate(_other_chips(x, y))]

    def start(srcs, dsts, send, recv):
        for cp in copies(srcs, dsts, send, recv):
            cp.start()

    def finish(srcs, dsts, send, recv):
        for cp in copies(srcs, dsts, send, recv):
            cp.wait()

    return _Rider(partials, landing, 3 * n, start, finish)


def _final_add(name, where, grad, got, arrived):
    _, hr, cols = got.shape
    tr = _row_tile(hr)
    nblk = hr // tr

    def body(w_ref, g_ref, r_ref, a_ref, o_ref):
        acc = g_ref[...] + r_ref[...]
        for j in range(3):
            acc = acc + a_ref[j].astype(F32)
        o_ref[...] = acc

    return pl.pallas_call(
        body, name=name,
        grid_spec=pltpu.PrefetchScalarGridSpec(
            num_scalar_prefetch=1, grid=(nblk,),
            in_specs=[pl.BlockSpec((None, tr, cols), lambda i, w: (w[0], w[1] * nblk + i, 0)),
                      pl.BlockSpec((None, tr, cols), lambda i, w: (w[0], i, 0)),
                      pl.BlockSpec((3, tr, cols), lambda i, w: (0, i, 0))],
            out_specs=pl.BlockSpec((tr, cols), lambda i, w: (w[1] * nblk + i, 0))),
        out_shape=jax.ShapeDtypeStruct((2 * hr, cols), F32),
        compiler_params=_params("parallel"),
    )(where, grad, got, arrived)


def _pair_share_rider(shards):
    n = len(shards)

    def half(refs, w, which):
        hr = shards[w].shape[0] // 2
        return refs[w].at[pl.ds(which * hr, hr)]

    def start(_, refs, send, recv):
        x, y, c = _place()
        for w in range(n):
            _remote(half(refs, w, c), half(refs, w, c), send, recv, w, (x, y, 1 - c)).start()

    def finish(_, refs, send, recv):
        x, y, c = _place()
        for w in range(n):
            _remote(half(refs, w, 1 - c), half(refs, w, 1 - c), send, recv, w, (x, y, 1 - c)).wait_recv()
        for w in range(n):
            _remote(half(refs, w, c), half(refs, w, c), send, recv, w, (x, y, 1 - c)).wait_send()

    return _Rider([], shards, n, start, finish)


class _Exchange:
    PLAN = {"bias_expand": [("early", "gather")], "norm1": [("early", "forward")],
            "proj_conv": [("gate", "gather")], "proj_v": [("gate", "forward")],
            "proj_qk": [("late", "gather")], "gates": [("late", "forward")],
            "attn_fwd": [("mlp_w", "gather")], "conv_fwd": [("mlp_w", "forward")],
            "d_h2": [("mlp", "pair")], "attn_bwd": [("mlp", "chips"), ("proj", "pair")], "conv_bwd": [("mlp", "share")],
            "g_in": [("proj", "chips")], "bias_reduce": [("proj", "share"), ("in", "pair")],
            "d_h": [("in", "chips")], "b_gate_sum": [("in", "share")]}

    def __init__(self, where, early_slots, gate_slots, late_slots):
        self.where = where
        self.slots = dict(early=early_slots, gate=gate_slots, late=late_slots[:3], mlp_w=late_slots[3:])
        self.stage = {g: dict(zip(("gather", "forward"), _gather_riders(s))) for g, s in self.slots.items()}
        self.groups, self.reduced, self.pending = {}, {}, []

    def early_weights(self):
        w_in3, small = self.slots["early"]
        conv_w = small[:, :3, :].transpose(1, 0, 2).reshape(3, N_SHARD * small.shape[2])
        return w_in3, conv_w

    def gate_weight(self):
        return self.slots["gate"][0]

    def late_weights(self):
        rows = lambda a: a.reshape(a.shape[0] * a.shape[1], a.shape[2])
        w_ap, w_cp, w_out = self.slots["late"]
        w_up3, w_down = self.slots["mlp_w"]
        return rows(w_ap), rows(w_cp), rows(w_out), w_up3, rows(w_down)

    def grads_ready(self, group, grads):
        names = list(grads)
        g4 = [g if g.ndim == 3 else g.reshape(N_SHARD, -1, g.shape[1]) for g in grads.values()]
        self.groups[group] = dict(names=names, g4=g4)

    def riders(self, at):
        self.pending = self.PLAN.get(at, [])
        out = []
        for group, stage in self.pending:
            if group in self.slots:
                out.append(self.stage[group][stage](self.slots[group]))
                continue
            st = self.groups[group]
            if stage == "pair":
                landing = [lax.empty((N_SHARD, g.shape[1] // 2, g.shape[2]), F32) for g in st["g4"]]
                out.append(_pair_exchange_rider(st["g4"], landing))
            elif stage == "chips":
                landing = [lax.empty((3,) + p.shape[1:], p.dtype) for p in st["partial"]]
                out.append(_chip_exchange_rider(st["partial"], landing))
            else:
                out.append(_pair_share_rider(st["halves"]))
        return out

    def done(self, at, carried):
        for (group, stage), arrays in zip(self.pending, carried):
            if group in self.slots:
                self.slots[group] = arrays
                continue
            st = self.groups[group]
            tag = lambda what, n: what + "_" + n
            if stage == "pair":
                st["got"] = arrays
                st["partial"] = [_pair_add(tag("pair_add", n), self.where, g, r)
                                 for n, g, r in zip(st["names"], st["g4"], arrays)]
            elif stage == "chips":
                st["halves"] = [_final_add(tag("final_add", n), self.where, g, r, a)
                                for n, g, r, a in zip(st["names"], st["g4"], st["got"], arrays)]
            else:
                self.reduced.update(zip(st["names"], arrays))


SMALL_ROWS = 32
N_DEV = 8


def _all_reduce_small(pack):
    def body(p_ref, o_ref, buf, send, recv):
        x, y, c = _place()
        buf[4 * x + 2 * y + c] = p_ref[...]
        copies, waits = [], []
        for k in range(1, N_DEV):
            px = 1 - x if k & 4 else x
            py = 1 - y if k & 2 else y
            pc = 1 - c if k & 1 else c
            copies.append(_remote(p_ref, buf.at[4 * x + 2 * y + c], send, recv, k - 1, (px, py, pc)))
            waits.append(_remote(p_ref, buf.at[4 * px + 2 * py + pc], send, recv, k - 1, (px, py, pc)))
        for cp in copies:
            cp.start()
        for cp in waits:
            cp.wait_recv()
        acc = buf[0]
        for d in range(1, N_DEV):
            acc = acc + buf[d]
        o_ref[...] = acc
        for cp in copies:
            cp.wait_send()

    return pl.pallas_call(
        body, name="all_reduce_small",
        out_shape=jax.ShapeDtypeStruct(pack.shape, F32),
        scratch_shapes=[pltpu.VMEM((N_DEV,) + pack.shape, F32),
                        pltpu.SemaphoreType.DMA((N_DEV - 1,)), pltpu.SemaphoreType.DMA((N_DEV - 1,))],
    )(pack)


def _adamw(name, w, g, m, v):
    c1 = 1.0 - ADAM_B1 ** ADAM_STEP
    c2 = 1.0 - ADAM_B2 ** ADAM_STEP

    def fn(t, f):
        wv, gv, mv, vv = t
        m2 = ADAM_B1 * mv + (1.0 - ADAM_B1) * gv
        v2 = ADAM_B2 * vv + (1.0 - ADAM_B2) * (gv * gv)
        delta = -ADAM_LR * ((m2 / c1) / (jnp.sqrt(v2 / c2) + ADAM_EPS) + ADAM_WD * wv)
        return [delta, m2, v2], []

    cols = w.shape[1]
    return _ew(name, fn, [w, g, m, v], [], [(cols, F32)] * 3, ts=min(w.shape[0], 256))


LOSS_ROW = 26


def _pack_small(norm1_g, norm2_g, conv_b, b_gate, conv_w, q_norm_g, k_norm_g, rel_bias, loss=None):
    pack = jnp.zeros((SMALL_ROWS, D_MODEL), F32)
    for r0, v in ((0, norm1_g), (1, norm2_g), (2, conv_b), (3, b_gate.reshape(2, D_MODEL)), (5, conv_w),
                  (8, q_norm_g), (9, k_norm_g), (10, rel_bias)) + (((LOSS_ROW, loss),) if loss is not None else ()):
        v = v.reshape(-1, v.shape[-1]).astype(F32)
        pack = pack.at[r0:r0 + v.shape[0], :v.shape[1]].set(v)
    return pack


def _unpack_small(pack, conv_cols):
    return dict(norm1_g=pack[0], norm2_g=pack[1], conv_b=pack[2], b_gate=pack[3:5].reshape(2 * D_MODEL),
                conv_w=pack[5:8, :conv_cols], q_norm_g=pack[8, :HEAD_DIM], k_norm_g=pack[9, :HEAD_DIM],
                rel_bias=pack[10:10 + N_HEADS, :N_REL])


BIG = ["w_in", "w_attn_proj", "w_conv_proj", "w_gate", "w_out", "w_up", "w_down"]
LATE = ["w_attn_proj", "w_conv_proj", "w_out", "w_up", "w_down"]
WEIGHTS = ["norm1_g", "w_in", "q_norm_g", "k_norm_g", "rel_bias", "conv_w", "conv_b", "w_attn_proj",
           "w_conv_proj", "w_gate", "b_gate", "w_out", "norm2_g", "w_up", "w_down"]


def kernel(x, norm1_g, w_in, q_norm_g, k_norm_g, rel_bias, conv_w, conv_b, w_attn_proj, w_conv_proj, w_gate, b_gate, w_out, norm2_g, w_up, w_down, loss_target, m_norm1_g, m_w_in, m_q_norm_g, m_k_norm_g, m_rel_bias, m_conv_w, m_conv_b, m_w_attn_proj, m_w_conv_proj, m_w_gate, m_b_gate, m_w_out, m_norm2_g, m_w_up, m_w_down, v_norm1_g, v_w_in, v_q_norm_g, v_k_norm_g, v_rel_bias, v_conv_w, v_conv_b, v_w_attn_proj, v_w_conv_proj, v_w_gate, v_b_gate, v_w_out, v_norm2_g, v_w_up, v_w_down):
    given = dict(locals())
    w = {n: given[n] for n in WEIGHTS}
    m = {n: given["m_" + n] for n in WEIGHTS}
    v = {n: given["v_" + n] for n in WEIGHTS}
    s_len = x.shape[1]
    shard = 2 * lax.axis_index("x") + lax.axis_index("y")
    where = jnp.stack([shard, lax.axis_index("c")]).astype(jnp.int32)
    conv_cols = conv_w.shape[1]

    small_in = lax.dynamic_update_slice(jnp.zeros((N_SHARD, 16, conv_cols), F32), conv_w[None], (shard, 0, 0))
    slot = {n: _cast_into_slot("cast_" + n, where, w[n]) for n in BIG}
    comm = _Exchange(where, [slot["w_in"], small_in], [slot["w_gate"]], [slot[n] for n in LATE])

    loss_part, grad_x, _, small = _local_grads(
        x.reshape(s_len, D_MODEL), loss_target.reshape(s_len, D_MODEL), norm1_g, q_norm_g, k_norm_g,
        rel_bias, conv_b, b_gate, norm2_g, comm)
    grad = dict(comm.reduced)

    loss_local = _finish_loss(loss_part)
    pack = _pack_small(small["norm1_g"], small["norm2_g"], small["conv_wb"][3], jnp.concatenate(small["b_gate"], axis=1),
                       small["conv_wb"][0:3], small["q_norm_g"], small["k_norm_g"], small["rel_bias"],
                       loss=loss_local[0:1, :])
    total = _all_reduce_small(pack)
    g_small = _unpack_small(total, D_MODEL)
    g_small["conv_w"] = lax.dynamic_slice(g_small["conv_w"], (0, shard * conv_cols), (3, conv_cols))
    grad.update(g_small)

    delta, new_m, new_v = {}, {}, {}
    for n in BIG:
        delta[n], new_m[n], new_v[n] = _adamw("adamw_" + n, w[n], grad[n], m[n], v[n])
    small_names = [n for n in WEIGHTS if n not in BIG]
    packs = [_pack_small(**{n: src[n] for n in small_names}) for src in (w, grad, m, v)]
    for out, packed in zip((delta, new_m, new_v), _adamw("adamw_small", *packs)):
        out.update({n: a.reshape(w[n].shape) for n, a in _unpack_small(packed, conv_cols).items()})

    outs = [total[LOSS_ROW, 0], grad_x.reshape(x.shape)]
    for group in (grad, delta, new_m, new_v):
        outs += [group[n].reshape(w[n].shape) for n in WEIGHTS]
    return tuple(outs)
```

```python
import jax
import jax.numpy as jnp
from jax import lax
from jax.experimental import pallas as pl
from jax.experimental.pallas import tpu as pltpu

F32 = jnp.float32
BF16 = jnp.bfloat16

D_MODEL = 1024
N_HEADS = 16
HEAD_DIM = 64
CHUNK = 64
N_PREV_CHUNKS = 8
MAX_REL = 256
D_FF = 4096
N_REL = 2 * MAX_REL + 1
REL_PAD = 640
EPS = 1e-6
NEG_INF = -1e30
QK_SCALE = HEAD_DIM ** -0.5

SUPER = 4 * CHUNK
BAND = SUPER + N_PREV_CHUNKS * CHUNK
SKEW_W = 1024
N_SHARD = 4
LANE = 128
MXU_DIM = 256
VMEM_LIMIT = 48 * 1024 * 1024

ADAM_LR = 0.001
ADAM_B1 = 0.9
ADAM_B2 = 0.999
ADAM_EPS = 1e-08
ADAM_WD = 0.01
ADAM_STEP = 10

MESH = pl.DeviceIdType.MESH
NN = (((1,), (0,)), ((), ()))
NT = (((1,), (1,)), ((), ()))
TN = (((0,), (0,)), ((), ()))


def _params(*sem):
    return pltpu.CompilerParams(dimension_semantics=sem or None, vmem_limit_bytes=VMEM_LIMIT)


HBM_SPEC = pl.BlockSpec(memory_space=pl.ANY)


class _Rider:
    def __init__(self, sources, arrays, n_sem, start, finish):
        self.sources, self.arrays, self.n_sem, self.start, self.finish = sources, arrays, n_sem, start, finish


def _carry(riders, body, *, name, out_shape, grid=(), in_specs=None, out_specs=None, scratch_shapes=(),
           semantics=(), input_output_aliases=None):
    aliases = dict(input_output_aliases or {})
    if not riders:
        kw = {} if in_specs is None else dict(in_specs=in_specs, out_specs=out_specs)
        return pl.pallas_call(body, name=name, grid=grid, out_shape=out_shape, scratch_shapes=scratch_shapes,
                              input_output_aliases=aliases, compiler_params=_params(*semantics), **kw)
    single = not isinstance(out_shape, (list, tuple))
    shapes = [out_shape] if single else list(out_shape)
    n_out, n_scr = len(shapes), len(scratch_shapes)
    in_hbm = lambda a: pltpu.with_memory_space_constraint(a, pltpu.HBM)
    srcs = [in_hbm(a) for r in riders for a in r.sources]
    arrs = [in_hbm(a) for r in riders for a in r.arrays]
    vmem = pl.BlockSpec(memory_space=pltpu.VMEM)

    def run(*args):
        n_in = len(args)

        def wrapped(*refs):
            pos = n_in
            src_refs = refs[pos:pos + len(srcs)]
            pos += len(srcs) + len(arrs)
            outs = refs[pos:pos + n_out]
            pos += n_out
            arr_refs = refs[pos:pos + len(arrs)]
            pos += len(arrs)
            scratch = refs[pos:pos + n_scr]
            sems = refs[pos + n_scr:]
            first, last = True, True
            for d, size in enumerate(grid):
                first = jnp.logical_and(first, pl.program_id(d) == 0)
                last = jnp.logical_and(last, pl.program_id(d) == size - 1)

            def each(method):
                s0 = a0 = 0
                for k, r in enumerate(riders):
                    getattr(r, method)(src_refs[s0:s0 + len(r.sources)], arr_refs[a0:a0 + len(r.arrays)],
                                       sems[2 * k], sems[2 * k + 1])
                    s0, a0 = s0 + len(r.sources), a0 + len(r.arrays)

            pl.when(first)(lambda: each("start"))
            body(*refs[:n_in], *outs, *scratch)
            pl.when(last)(lambda: each("finish"))

        ins = [vmem] * n_in if in_specs is None else list(in_specs)
        if out_specs is None:
            o_specs = [vmem] * n_out
        else:
            o_specs = [out_specs] if single else list(out_specs)
        for k in range(len(arrs)):
            aliases[n_in + len(srcs) + k] = n_out + k
        res = pl.pallas_call(
            wrapped, name=name, grid=grid,
            in_specs=ins + [HBM_SPEC] * (len(srcs) + len(arrs)),
            out_specs=o_specs + [HBM_SPEC] * len(arrs),
            out_shape=shapes + [jax.ShapeDtypeStruct(a.shape, a.dtype) for a in arrs],
            scratch_shapes=list(scratch_shapes) + [pltpu.SemaphoreType.DMA((r.n_sem,)) for r in riders for _ in range(2)],
            input_output_aliases=aliases,
            compiler_params=_params(*["arbitrary"] * len(grid)),
        )(*args, *srcs, *arrs)
        core, rest = res[:n_out], list(res[n_out:])
        carried, a0 = [], 0
        for r in riders:
            carried.append(rest[a0:a0 + len(r.arrays)])
            a0 += len(r.arrays)
        return (core[0] if single else core), carried

    return run


def _mm(name, dims, a, a_spec, b, b_spec, grid, tile, outs, epilogue=None, extras=(), riders=()):
    nk, ne, no = grid[2], len(extras), len(outs)

    def body(a_ref, b_ref, *refs):
        e_refs, o_refs = refs[:ne], refs[ne:ne + no]
        part = lax.dot_general(a_ref[...], b_ref[...], dims, preferred_element_type=F32)

        def finish(acc):
            if epilogue is None:
                o_refs[0][...] = acc.astype(o_refs[0].dtype)
            else:
                epilogue(acc, [r[...] for r in e_refs], o_refs)

        if nk == 1:
            finish(part)
        else:
            acc_ref = refs[ne + no]
            k = pl.program_id(2)

            @pl.when(k == 0)
            def _():
                acc_ref[...] = part

            @pl.when(k > 0)
            def _():
                acc_ref[...] += part

            @pl.when(k == nk - 1)
            def _():
                finish(acc_ref[...])

    res = _carry(
        riders, body, name=name, grid=grid,
        in_specs=[a_spec, b_spec] + [s for _, s in extras],
        out_specs=[s for _, s in outs],
        out_shape=[s for s, _ in outs],
        scratch_shapes=[pltpu.VMEM(tile, F32)] if nk > 1 else [],
        semantics=("parallel", "parallel", "arbitrary"),
    )(a, b, *[e for e, _ in extras])
    res, carried = res if riders else (res, None)
    res = res[0] if no == 1 else res
    return (res, carried) if riders else res


def _tile_spec(tm, tn, col0=0):
    return pl.BlockSpec((tm, tn), lambda i, j, k: (i, j + col0))


def _out2d(m, n, dtype, tm, tn):
    return (jax.ShapeDtypeStruct((m, n), dtype), _tile_spec(tm, tn))


def _mm_fwd(name, a, w, tm, tn, tk, outs=None, epilogue=None, extras=(), col0=0, ncols=None, riders=()):
    m, kdim = a.shape
    if w.ndim == 3:
        per = w.shape[2] // tn
        n = ncols or N_SHARD * w.shape[2]
        w_spec = pl.BlockSpec((None, tk, tn), lambda i, j, k: ((j + col0) // per, k, (j + col0) % per))
    else:
        n = ncols or w.shape[1]
        w_spec = pl.BlockSpec((tk, tn), lambda i, j, k: (k, j + col0))
    if outs is None:
        outs = [_out2d(m, n, F32, tm, tn)]
    return _mm(name, NN, a, pl.BlockSpec((tm, tk), lambda i, j, k: (i, k)), w, w_spec,
               (m // tm, n // tn, kdim // tk), (tm, tn), outs, epilogue, extras, riders)


def _mm_bwd_x(name, g, g_spec, w, tm, tj, tc, m, n_contract, outs=None, epilogue=None, extras=(), riders=()):
    if w.ndim == 3:
        per = w.shape[2] // tc
        kdim = w.shape[1]
        w_spec = pl.BlockSpec((None, tj, tc), lambda i, j, n: (n // per, j, n % per))
    else:
        kdim = w.shape[0]
        w_spec = pl.BlockSpec((tj, tc), lambda i, j, n: (j, n))
    if outs is None:
        outs = [_out2d(m, kdim, F32, tm, tj)]
    return _mm(name, NT, g, g_spec, w, w_spec, (m // tm, kdim // tj, n_contract // tc),
               (tm, tj), outs, epilogue, extras, riders)


def _mm_bwd_w(name, a, g, g_spec, n, tk, tn, tm, sharded, riders=()):
    m, kdim = a.shape
    if sharded:
        per = (n // N_SHARD) // tn
        out = (jax.ShapeDtypeStruct((N_SHARD, kdim, n // N_SHARD), F32),
               pl.BlockSpec((None, tk, tn), lambda i, j, mm: (j // per, i, j % per)))
    else:
        out = (jax.ShapeDtypeStruct((kdim, n), F32), pl.BlockSpec((tk, tn), lambda i, j, mm: (i, j)))
    return _mm(name, TN, a, pl.BlockSpec((tm, tk), lambda i, j, mm: (mm, i)), g, g_spec,
               (kdim // tk, n // tn, m // tm), (tk, tn), [out], riders=riders)


def _ew(name, fn, tiles, fulls, outs, sums=(), ts=512, riders=()):
    tiles = [t if isinstance(t, tuple) else (t, pl.BlockSpec((ts, t.shape[1]), lambda i: (i, 0)))
             for t in tiles]
    s_rows = tiles[0][0].shape[-2]
    nt, nf, no = len(tiles), len(fulls), len(outs)

    def body(*refs):
        t_vals = [r[...] for r in refs[:nt]]
        f_vals = [r[...] for r in refs[nt:nt + nf]]
        o_refs, s_refs = refs[nt + nf:nt + nf + no], refs[nt + nf + no:]
        o_vals, s_vals = fn(t_vals, f_vals)
        for r, v in zip(o_refs, o_vals):
            r[...] = v.astype(r.dtype)
        for r, v in zip(s_refs, s_vals):
            part = jnp.sum(v, axis=0, keepdims=True)

            @pl.when(pl.program_id(0) == 0)
            def _():
                r[...] = part

            @pl.when(pl.program_id(0) > 0)
            def _():
                r[...] += part

    full_specs = [pl.BlockSpec(f.shape, lambda i, nd=f.ndim: (0,) * nd) for f in fulls]
    return _carry(
        riders, body, name=name, grid=(s_rows // ts,),
        in_specs=[s for _, s in tiles] + full_specs,
        out_specs=[pl.BlockSpec((ts, c), lambda i: (i, 0)) for c, _ in outs]
        + [pl.BlockSpec((1, c), lambda i: (0, 0)) for c in sums],
        out_shape=[jax.ShapeDtypeStruct((s_rows, c), dt) for c, dt in outs]
        + [jax.ShapeDtypeStruct((1, c), F32) for c in sums],
        semantics=("arbitrary",),
    )(*[t for t, _ in tiles], *fulls)


def _rms_fwd(name, x, g, riders=()):
    def fn(t, f):
        xv = t[0]
        r = lax.rsqrt(jnp.mean(xv * xv, axis=-1, keepdims=True) + EPS)
        return [xv * r * f[0]], []
    out = _ew(name, fn, [x], [g], [(x.shape[1], BF16)], riders=riders)
    return (out[0][0], out[1]) if riders else out[0]


def _split3(x):
    x1 = x.astype(BF16)
    r1 = x - x1.astype(F32)
    x2 = r1.astype(BF16)
    x3 = (r1 - x2.astype(F32)).astype(BF16)
    return x1, x2, x3


def _rel_class(cp):
    far = (cp < MAX_REL) | (cp > BAND)
    return jnp.where(far, 2 * MAX_REL, BAND - cp)


def _skew_rows(x, sign):
    row = lax.broadcasted_iota(jnp.int32, x.shape, 0)
    for b in range(CHUNK.bit_length() - 1):
        shift = (1 << b) if sign > 0 else SKEW_W - (1 << b)
        x = jnp.where((row >> b) & 1 == 1, pltpu.roll(x, shift, 1), x)
    return x


def _roll_lanes(x, shift):
    return x if shift % SKEW_W == 0 else pltpu.roll(x, shift % SKEW_W, 1)


N_START = 3


def _bias_expand(rel_bias, riders=()):
    rel = jnp.pad(rel_bias, ((0, 0), (0, REL_PAD - N_REL))).reshape(N_HEADS, 1, REL_PAD)

    def body(rel_ref, o_ref):
        cls = lax.broadcasted_iota(jnp.int32, (REL_PAD, SKEW_W), 0)
        cp = lax.broadcasted_iota(jnp.int32, (REL_PAD, SKEW_W), 1)
        onehot = (cls == _rel_class(cp)).astype(BF16)
        rel8 = jnp.broadcast_to(rel_ref[...], (8, REL_PAD))
        trow = sum(jnp.dot(p, onehot, preferred_element_type=F32) for p in _split3(rel8))[0:1]
        first = _skew_rows(jnp.broadcast_to(trow, (CHUNK, SKEW_W)), +1)
        full = jnp.concatenate([_roll_lanes(first, CHUNK * g) for g in range(SUPER // CHUNK)], axis=0)[:, :BAND]
        qc = lax.broadcasted_iota(jnp.int32, (SUPER, BAND), 0) // CHUNK
        kc = lax.broadcasted_iota(jnp.int32, (SUPER, BAND), 1) // CHUNK
        on_band = (kc >= qc) & (kc <= qc + N_PREV_CHUNKS)
        table = jnp.where(on_band, full, NEG_INF).T
        key = lax.broadcasted_iota(jnp.int32, (BAND, SUPER), 0)
        for t in range(N_START):
            o_ref[t] = jnp.where(key < (N_START - 1 - t) * SUPER, NEG_INF, table)

    return _carry(
        riders, body, name="bias_expand", grid=(N_HEADS,),
        in_specs=[pl.BlockSpec((None, 1, REL_PAD), lambda h: (h, 0, 0))],
        out_specs=pl.BlockSpec((N_START, None, BAND, SUPER), lambda h: (0, h, 0, 0)),
        out_shape=jax.ShapeDtypeStruct((N_START, N_HEADS, BAND, SUPER), F32),
        semantics=("arbitrary",),
    )(rel)


def _bias_reduce(dbias, riders=()):
    def body(d_ref, o_ref):
        x = jnp.concatenate([d_ref[...].T, jnp.zeros((SUPER, SKEW_W - BAND), F32)], axis=1)
        folded = sum(_roll_lanes(x[CHUNK * g:CHUNK * (g + 1)], -CHUNK * g) for g in range(SUPER // CHUNK))
        diag = jnp.sum(_skew_rows(folded, -1), axis=0, keepdims=True)
        cp = lax.broadcasted_iota(jnp.int32, (SKEW_W, REL_PAD), 0)
        cls = lax.broadcasted_iota(jnp.int32, (SKEW_W, REL_PAD), 1)
        onehot = (cls == _rel_class(cp)).astype(BF16)
        diag8 = jnp.broadcast_to(diag, (8, SKEW_W))
        o_ref[...] = sum(jnp.dot(p, onehot, preferred_element_type=F32) for p in _split3(diag8))[0:1]

    out = _carry(
        riders, body, name="bias_reduce", grid=(N_HEADS,),
        in_specs=[pl.BlockSpec((None, BAND, SUPER), lambda h: (h, 0, 0))],
        out_specs=pl.BlockSpec((None, 1, REL_PAD), lambda h: (h, 0, 0)),
        out_shape=jax.ShapeDtypeStruct((N_HEADS, 1, REL_PAD), F32),
        semantics=("arbitrary",),
    )(dbias)
    out, carried = out if riders else (out, None)
    out = out.reshape(N_HEADS, REL_PAD)[:, :N_REL]
    return (out, carried) if riders else out


HEADS_PER_STEP = 8
HEAD_COLS = HEADS_PER_STEP * HEAD_DIM
N_HEAD_GROUPS = N_HEADS // HEADS_PER_STEP


def _scores_t(qs, kn, bias_t):
    return jnp.concatenate([lax.dot_general(k, qs, NT, preferred_element_type=F32) for k in kn], axis=0) + bias_t


def _bias_spec():
    return pl.BlockSpec((None, HEADS_PER_STEP, BAND, SUPER), lambda hg, i: (jnp.minimum(i, N_START - 1), hg, 0, 0))


def _band_specs(nb, col0, clamp_hi):
    def spec(d):
        def index(hg, i):
            blk = jnp.maximum(i - d, 0)
            if clamp_hi:
                blk = jnp.minimum(blk, nb - 1)
            return (blk, col0 + hg)
        return pl.BlockSpec((SUPER, HEAD_COLS), index)
    return [spec(2), spec(1), spec(0)]


def _head_sums(y):
    same_head = (lax.broadcasted_iota(jnp.int32, (MXU_DIM, MXU_DIM), 0) // HEAD_DIM
                 == lax.broadcasted_iota(jnp.int32, (MXU_DIM, MXU_DIM), 1) // HEAD_DIM).astype(BF16)
    sums = []
    for c0 in range(0, y.shape[1], MXU_DIM):
        chunk = y[:, c0:c0 + MXU_DIM]
        hi = chunk.astype(BF16)
        lo = (chunk - hi.astype(F32)).astype(BF16)
        sums.append(jnp.dot(hi, same_head, preferred_element_type=F32)
                    + jnp.dot(lo, same_head, preferred_element_type=F32))
    return jnp.concatenate(sums, axis=1)


def _head_unit(x):
    r = lax.rsqrt(_head_sums(x * x) * (1.0 / HEAD_DIM) + EPS)
    return x * r, r


def _head(hh):
    return slice(HEAD_DIM * hh, HEAD_DIM * (hh + 1))


def _key_block(j):
    return slice(SUPER * j, SUPER * (j + 1))


LSE_ROWS = 8


def _attn_fwd(qkn, v, bias, riders=()):
    s_len = qkn.shape[0]
    nb = s_len // SUPER

    def body(q_ref, k0, k1, k2, v0, v1, v2, b_ref, o_ref, lse_ref):
        outs = []
        v_t = [v0[...].T, v1[...].T, v2[...].T]

        def probabilities(hh):
            sl = _head(hh)
            s = _scores_t(q_ref[:, sl], [k0[:, sl], k1[:, sl], k2[:, sl]], b_ref[hh])
            m = jnp.max(s, axis=0, keepdims=True)
            e = jnp.exp(s - m)
            l = jnp.sum(e, axis=0, keepdims=True)
            lse_ref[hh:hh + 1, :] = m + jnp.log(l)
            return e.astype(BF16), 1.0 / l, sl

        def weighted_values(e, inv_l, sl):
            outs.append(sum(jnp.dot(v_t[j][sl, :], e[_key_block(j), :], preferred_element_type=F32)
                            for j in range(3)) * inv_l)

        ready = probabilities(0)
        for hh in range(1, HEADS_PER_STEP):
            following = probabilities(hh)
            weighted_values(*ready)
            ready = following
        weighted_values(*ready)
        o_ref[...] = jnp.concatenate(outs, axis=0).T.astype(o_ref.dtype)

    return _carry(
        riders, body, name="attn_fwd", grid=(N_HEAD_GROUPS, nb),
        in_specs=[pl.BlockSpec((SUPER, HEAD_COLS), lambda hg, i: (i, hg))]
        + _band_specs(nb, N_HEAD_GROUPS, False) + _band_specs(nb, 0, False) + [_bias_spec()],
        out_specs=[pl.BlockSpec((SUPER, HEAD_COLS), lambda hg, i: (i, hg)),
                   pl.BlockSpec((None, LSE_ROWS, SUPER), lambda hg, i: (hg, 0, i))],
        out_shape=[jax.ShapeDtypeStruct((s_len, D_MODEL), BF16),
                   jax.ShapeDtypeStruct((N_HEAD_GROUPS, LSE_ROWS, s_len), F32)],
        semantics=("parallel", "arbitrary"),
    )(qkn, qkn, qkn, qkn, v, v, v, bias)


def _attn_bwd(qkn, v, out, d_out, bias, lse, riders=()):
    s_len = qkn.shape[0]
    nb = s_len // SUPER

    def body(q_ref, k0, k1, k2, v0, v1, v2, o_ref, do_ref, b_ref, lse_ref, dp_ref, db_ref, aq_ref, ak_ref, av_ref):
        i = pl.program_id(1)

        @pl.when(i == 0)
        def _():
            aq_ref[...] = jnp.zeros_like(aq_ref)
            ak_ref[...] = jnp.zeros_like(ak_ref)
            av_ref[...] = jnp.zeros_like(av_ref)
            db_ref[...] = jnp.zeros_like(db_ref)

        @pl.when(i < nb)
        def _():
            dq, dk, dv = [], [[], [], []], [[], [], []]
            ones = jnp.ones((8, HEAD_DIM), BF16)
            k_t = [k0[...].T, k1[...].T, k2[...].T]

            def softmax_grad(hh):
                sl = _head(hh)
                qs, do = q_ref[:, sl], do_ref[:, sl]
                kn = [k0[:, sl], k1[:, sl], k2[:, sl]]
                prod = do.astype(F32) * o_ref[:, sl].astype(F32)
                hi = prod.astype(BF16)
                lo = (prod - hi.astype(F32)).astype(BF16)
                delta = (lax.dot_general(ones, hi, NT, preferred_element_type=F32)
                         + lax.dot_general(ones, lo, NT, preferred_element_type=F32))[0:1]
                lse_row = lse_ref[hh:hh + 1, :]
                pb, dsb = [], []
                for j, (kj, vj) in enumerate(zip(kn, (v0, v1, v2))):
                    rows = _key_block(j)
                    p = jnp.exp(lax.dot_general(kj, qs, NT, preferred_element_type=F32) + b_ref[hh, rows, :] - lse_row)
                    ds = p * (lax.dot_general(vj[:, sl], do, NT, preferred_element_type=F32) - delta)
                    db_ref[hh, rows, :] += ds
                    pb.append(p.astype(BF16))
                    dsb.append(ds.astype(BF16))
                return pb, dsb, qs, do, sl

            def operand_grads(pb, dsb, qs, do, sl):
                dq.append(sum(jnp.dot(k_t[j][sl, :], dsb[j], preferred_element_type=F32) for j in range(3)))
                for j in range(3):
                    dv[j].append(jnp.dot(pb[j], do, preferred_element_type=F32))
                    dk[j].append(jnp.dot(dsb[j], qs, preferred_element_type=F32))

            ready = softmax_grad(0)
            for hh in range(1, HEADS_PER_STEP):
                following = softmax_grad(hh)
                operand_grads(*ready)
                ready = following
            operand_grads(*ready)
            aq_ref[i % 3] = jnp.concatenate(dq, axis=0).T
            for j in range(3):
                slot = (i + 1 + j) % 3
                if j < 2:
                    ak_ref[slot] += jnp.concatenate(dk[j], axis=1)
                    av_ref[slot] += jnp.concatenate(dv[j], axis=1)
                else:
                    ak_ref[slot] = jnp.concatenate(dk[j], axis=1)
                    av_ref[slot] = jnp.concatenate(dv[j], axis=1)

        slot = (i + 1) % 3
        dp_ref[0] = aq_ref[slot].astype(dp_ref.dtype)
        dp_ref[1] = ak_ref[slot].astype(dp_ref.dtype)
        dp_ref[2] = av_ref[slot].astype(dp_ref.dtype)

    def qrow(hg, i):
        return (jnp.minimum(i, nb - 1), hg)

    return _carry(
        riders, body, name="attn_bwd", grid=(N_HEAD_GROUPS, nb + 2),
        in_specs=[pl.BlockSpec((SUPER, HEAD_COLS), qrow)]
        + _band_specs(nb, N_HEAD_GROUPS, True) + _band_specs(nb, 0, True)
        + [pl.BlockSpec((SUPER, HEAD_COLS), qrow), pl.BlockSpec((SUPER, HEAD_COLS), qrow), _bias_spec(),
           pl.BlockSpec((None, LSE_ROWS, SUPER), lambda hg, i: (hg, 0, jnp.minimum(i, nb - 1)))],
        out_specs=[pl.BlockSpec((3, SUPER, HEAD_COLS), lambda hg, i: (0, jnp.maximum(i - 2, 0), hg)),
                   pl.BlockSpec((HEADS_PER_STEP, BAND, SUPER), lambda hg, i: (hg, 0, 0))],
        out_shape=[jax.ShapeDtypeStruct((6, s_len, D_MODEL), BF16),
                   jax.ShapeDtypeStruct((N_HEADS, BAND, SUPER), F32)],
        scratch_shapes=[pltpu.VMEM((3, SUPER, HEAD_COLS), F32)] * 3,
        semantics=("parallel", "arbitrary"),
    )(qkn, qkn, qkn, qkn, v, v, v, out, d_out, bias, lse)


def _qk_norm_bwd(dproj6, qk_raw, gq, gk):
    s_len = qk_raw.shape[0]
    ts = min(1024, s_len)

    nsteps = s_len // ts
    half = D_MODEL // 2

    def body(d_ref, raw_ref, gq_ref, gk_ref, o_ref, dgq_ref, dgk_ref, acc_ref):
        step = pl.program_id(0)

        @pl.when(step == 0)
        def _():
            acc_ref[...] = jnp.zeros_like(acc_ref)

        for piece, (g_ref, scale) in enumerate(((gq_ref, QK_SCALE), (gk_ref, 1.0))):
            for c0 in (0, half):
                xhat, r = _head_unit(raw_ref[:, piece * D_MODEL + c0:piece * D_MODEL + c0 + half].astype(F32))
                dn = d_ref[piece, :, c0:c0 + half].astype(F32) * scale
                u = dn * g_ref[...]
                dx = r * (u - xhat * (_head_sums(u * xhat) * (1.0 / HEAD_DIM)))
                o_ref[piece, :, c0:c0 + half] = dx.astype(o_ref.dtype)
                acc_ref[piece:piece + 1, c0:c0 + half] += jnp.sum(dn * xhat, axis=0, keepdims=True)

        @pl.when(step == nsteps - 1)
        def _():
            lane = lax.broadcasted_iota(jnp.int32, (D_MODEL, LANE), 0) % HEAD_DIM
            fold = (lane == lax.broadcasted_iota(jnp.int32, (D_MODEL, LANE), 1)).astype(BF16)
            tot = sum(jnp.dot(p, fold, preferred_element_type=F32) for p in _split3(acc_ref[...]))
            dgq_ref[...] = tot[0:1, :HEAD_DIM]
            dgk_ref[...] = tot[1:2, :HEAD_DIM]

    gain = pl.BlockSpec((1, half), lambda i: (0, 0))
    small = pl.BlockSpec((1, HEAD_DIM), lambda i: (0, 0))
    per_head = lambda g: jnp.tile(g, (1, half // HEAD_DIM))
    return pl.pallas_call(
        body, name="qk_norm_bwd", grid=(nsteps,),
        in_specs=[pl.BlockSpec((2, ts, D_MODEL), lambda i: (0, i, 0)),
                  pl.BlockSpec((ts, 2 * D_MODEL), lambda i: (i, 0)), gain, gain],
        out_specs=[pl.BlockSpec((2, ts, D_MODEL), lambda i: (0, i, 0)), small, small],
        out_shape=[jax.ShapeDtypeStruct(dproj6.shape, dproj6.dtype),
                   jax.ShapeDtypeStruct((1, HEAD_DIM), F32), jax.ShapeDtypeStruct((1, HEAD_DIM), F32)],
        scratch_shapes=[pltpu.VMEM((8, D_MODEL), F32)],
        input_output_aliases={0: 0},
        compiler_params=_params("arbitrary"),
    )(dproj6, qk_raw, per_head(gq), per_head(gk))


CONV_ROWS = 512
HALO = 16


def _rows_with_halo(ref, r0, n, front, s_len):
    zeros = jnp.zeros((HALO, ref.shape[1]), F32)
    if front:
        return (jnp.concatenate([zeros, ref[0:n, :].astype(F32)], axis=0) if r0 == 0
                else ref[r0 - HALO:r0 + n, :].astype(F32))
    return (jnp.concatenate([ref[r0:r0 + n, :].astype(F32), zeros], axis=0) if r0 + n == s_len
            else ref[r0:r0 + n + HALO, :].astype(F32))


def _earlier(ext, k):
    return pltpu.roll(ext, k, 0)[HALO:]


def _later(ext, k):
    n = ext.shape[0]
    return pltpu.roll(ext, n - k, 0)[:n - HALO]


def _conv_cols(col0):
    return lambda s_len: pl.BlockSpec((s_len, LANE), lambda j: (0, col0 + j))


def _conv_fwd(proj, conv_w, conv_b, riders=()):
    s_len = proj.shape[0]

    def body(bg_ref, cg_ref, xc_ref, w_ref, b_ref, o_ref):
        w = [w_ref[t:t + 1, :] for t in range(3)]
        for r0 in range(0, s_len, CONV_ROWS):
            u = _rows_with_halo(cg_ref, r0, CONV_ROWS, True, s_len) * \
                _rows_with_halo(xc_ref, r0, CONV_ROWS, True, s_len)
            conv = b_ref[...] + w[0] * _earlier(u, 2) + w[1] * _earlier(u, 1) + w[2] * u[HALO:]
            o_ref[r0:r0 + CONV_ROWS, :] = (bg_ref[r0:r0 + CONV_ROWS, :].astype(F32) * conv).astype(o_ref.dtype)

    return _carry(
        riders, body, name="conv_fwd", grid=(D_MODEL // LANE,),
        in_specs=[_conv_cols(0)(s_len), _conv_cols(8)(s_len), _conv_cols(16)(s_len),
                  pl.BlockSpec((3, LANE), lambda j: (0, j)), pl.BlockSpec((1, LANE), lambda j: (0, j))],
        out_specs=pl.BlockSpec((s_len, LANE), lambda j: (0, j)),
        out_shape=jax.ShapeDtypeStruct((s_len, D_MODEL), BF16),
        semantics=("parallel",),
    )(proj, proj, proj, conv_w, conv_b)


def _conv_bwd(dproj6, dy, proj, conv_w, conv_b, riders=()):
    s_len = proj.shape[0]

    def body(dy_ref, bg_ref, cg_ref, xc_ref, w_ref, b_ref, _, dp_ref, dw_ref):
        w = [w_ref[t:t + 1, :] for t in range(3)]
        acc = [jnp.zeros((1, LANE), F32) for _ in range(4)]
        for r0 in range(0, s_len, CONV_ROWS):
            rows = slice(r0, r0 + CONV_ROWS)
            u = _rows_with_halo(cg_ref, r0, CONV_ROWS, True, s_len) * \
                _rows_with_halo(xc_ref, r0, CONV_ROWS, True, s_len)
            u2, u1, u0 = _earlier(u, 2), _earlier(u, 1), u[HALO:]
            conv = b_ref[...] + w[0] * u2 + w[1] * u1 + w[2] * u0
            dp_ref[0, rows, :] = (dy_ref[rows, :].astype(F32) * conv).astype(dp_ref.dtype)
            dconv_ext = _rows_with_halo(dy_ref, r0, CONV_ROWS, False, s_len) * \
                _rows_with_halo(bg_ref, r0, CONV_ROWS, False, s_len)
            dconv = dconv_ext[:CONV_ROWS]
            for t, term in enumerate([dconv * u2, dconv * u1, dconv * u0, dconv]):
                acc[t] = acc[t] + jnp.sum(term, axis=0, keepdims=True)
            du = w[2] * dconv + w[1] * _later(dconv_ext, 1) + w[0] * _later(dconv_ext, 2)
            dp_ref[1, rows, :] = (du * xc_ref[rows, :].astype(F32)).astype(dp_ref.dtype)
            dp_ref[2, rows, :] = (du * cg_ref[rows, :].astype(F32)).astype(dp_ref.dtype)
        dw_ref[...] = jnp.zeros_like(dw_ref)
        for t in range(4):
            dw_ref[t:t + 1, :] = acc[t]

    return _carry(
        riders, body, name="conv_bwd", grid=(D_MODEL // LANE,),
        in_specs=[pl.BlockSpec((s_len, LANE), lambda j: (0, j)),
                  _conv_cols(0)(s_len), _conv_cols(8)(s_len), _conv_cols(16)(s_len),
                  pl.BlockSpec((3, LANE), lambda j: (0, j)), pl.BlockSpec((1, LANE), lambda j: (0, j)),
                  pl.BlockSpec(memory_space=pl.ANY)],
        out_specs=[pl.BlockSpec((3, s_len, LANE), lambda j: (1, 0, j)),
                   pl.BlockSpec((8, LANE), lambda j: (0, j))],
        out_shape=[jax.ShapeDtypeStruct(dproj6.shape, dproj6.dtype),
                   jax.ShapeDtypeStruct((8, D_MODEL), F32)],
        input_output_aliases={6: 0},
        semantics=("parallel",),
    )(dy, proj, proj, proj, conv_w, conv_b, dproj6)


def _d_norm_input(name, terms, x, g, dres, out_dtype, riders=()):
    s_len, kdim = terms[0][0].shape[1], terms[0][1].shape[1]
    tm, chunk = 512, 512
    n = len(terms)

    def body(*refs):
        x_ref, g_ref, r_ref, o_ref, dg_ref = refs[2 * n:]
        dh = jnp.zeros((tm, kdim), F32)
        for p_ref, w_ref in zip(refs[:n], refs[n:2 * n]):
            n_piece, _, width = p_ref.shape
            per_shard = w_ref.shape[2]
            for c0 in range(0, n_piece * width, chunk):
                dh = dh + lax.dot_general(p_ref[c0 // width, :, c0 % width:c0 % width + chunk],
                                          w_ref[c0 // per_shard, :, c0 % per_shard:c0 % per_shard + chunk],
                                          NT, preferred_element_type=F32)
        xv = x_ref[...]
        r = lax.rsqrt(jnp.mean(xv * xv, axis=-1, keepdims=True) + EPS)
        xhat = xv * r
        u = dh * g_ref[...]
        dx = r * (u - xhat * jnp.mean(u * xhat, axis=-1, keepdims=True)) + r_ref[...].astype(F32)
        o_ref[...] = dx.astype(o_ref.dtype)
        part = jnp.sum(dh * xhat, axis=0, keepdims=True)

        @pl.when(pl.program_id(0) == 0)
        def _():
            dg_ref[...] = part

        @pl.when(pl.program_id(0) > 0)
        def _():
            dg_ref[...] += part

    tile = pl.BlockSpec((tm, kdim), lambda i: (i, 0))
    vec = pl.BlockSpec((1, kdim), lambda i: (0, 0))
    return _carry(
        riders, body, name=name, grid=(s_len // tm,),
        in_specs=[pl.BlockSpec((p.shape[0], tm, p.shape[2]), lambda i: (0, i, 0)) for p, _ in terms]
        + [pl.BlockSpec(w.shape, lambda i: (0, 0, 0), pipeline_mode=pl.Buffered(1)) for _, w in terms]
        + [tile, vec, tile],
        out_specs=[tile, vec],
        out_shape=[jax.ShapeDtypeStruct((s_len, kdim), out_dtype), jax.ShapeDtypeStruct((1, kdim), F32)],
        semantics=("arbitrary",),
    )(*[p for p, _ in terms], *[w for _, w in terms], x, g, dres)


def _local_grads(x, target, norm1_g, q_norm_g, k_norm_g, rel_bias, conv_b, b_gate, norm2_g, comm):
    s_len = x.shape[0]
    tm = min(1024, s_len)
    tb = min(2048, s_len)
    row = lambda v: v.reshape(1, -1)

    def carrying(at, fn, *args, **kw):
        riders = comm.riders(at)
        out = fn(*args, riders=riders, **kw)
        if riders:
            out, carried = out
            comm.done(at, carried)
        return out

    bias = carrying("bias_expand", _bias_expand, rel_bias)
    h = carrying("norm1", _rms_fwd, "norm1", x, row(norm1_g))
    w_in3, conv_w = comm.early_weights()
    gq, gk = row(q_norm_g), row(k_norm_g)

    def qk_epi(acc, e, o):
        o[0][...] = acc.astype(BF16)
        gain = jnp.where(pl.program_id(1) < 2, e[0] * QK_SCALE, e[1])
        o[1][...] = (_head_unit(acc)[0] * gain).astype(BF16)
    conv_in = carrying("proj_conv", _mm_fwd, "proj_conv", h, w_in3, tb, 1536, D_MODEL, col0=2, ncols=3 * D_MODEL,
                       outs=[_out2d(s_len, 3 * D_MODEL, BF16, tb, 1536)])
    small = pl.BlockSpec((1, 512), lambda i, j, k: (0, 0))
    per_head = lambda g: jnp.tile(g, (1, 512 // HEAD_DIM))
    qk_raw, qkn = carrying("proj_qk", _mm_fwd, "proj_qk", h, w_in3, tm, 512, D_MODEL, ncols=2 * D_MODEL,
                           epilogue=qk_epi, outs=[_out2d(s_len, 2 * D_MODEL, BF16, tm, 512)] * 2,
                           extras=[(per_head(gq), small), (per_head(gk), small)])
    v = carrying("proj_v", _mm_fwd, "proj_v", h, w_in3, tb, 512, D_MODEL, col0=4, ncols=D_MODEL,
                 outs=[_out2d(s_len, D_MODEL, BF16, tb, 512)])
    w_gate3 = comm.gate_weight()

    def gate_epi(acc, e, o):
        o[0][...] = jax.nn.sigmoid(acc + e[0]).astype(BF16)
    gates = carrying("gates", _mm_fwd, "gates", h, w_gate3, tb, 512, D_MODEL, epilogue=gate_epi,
                     outs=[_out2d(s_len, 2 * D_MODEL, BF16, tb, 512)],
                     extras=[(row(b_gate), pl.BlockSpec((1, 512), lambda i, j, k: (0, j)))])

    attn, lse = carrying("attn_fwd", _attn_fwd, qkn, v, bias)
    yconv = carrying("conv_fwd", _conv_fwd, conv_in, conv_w, row(conv_b))
    w_ap, w_cp, w_out, w_up3, w_down = comm.late_weights()
    tw = 1024
    ya = _mm_fwd("attn_proj", attn, w_ap, tm, tw, D_MODEL, outs=[_out2d(s_len, D_MODEL, BF16, tm, tw)])

    def merge_epi(acc, e, o):
        ya_v, ga, gc = [t.astype(F32) for t in e]
        o[0][...] = acc.astype(BF16)
        o[1][...] = (ga * ya_v + gc * acc).astype(BF16)
    gate_a, gate_c = (gates, _tile_spec(tm, tw, 0)), (gates, _tile_spec(tm, tw, 1))
    yc, merged = _mm_fwd("conv_proj", yconv, w_cp, tm, tw, D_MODEL, epilogue=merge_epi,
                         outs=[_out2d(s_len, D_MODEL, BF16, tm, tw), _out2d(s_len, D_MODEL, BF16, tm, tw)],
                         extras=[(ya, _tile_spec(tm, tw)), gate_a, gate_c])

    def res_epi(acc, e, o):
        x1_v = e[0] + acc
        o[0][...] = x1_v
        r = lax.rsqrt(jnp.mean(x1_v * x1_v, axis=-1, keepdims=True) + EPS)
        o[1][...] = (x1_v * r * e[1]).astype(BF16)
    assert tw == D_MODEL
    x1, h2 = _mm_fwd("out_proj", merged, w_out, tm, tw, D_MODEL, epilogue=res_epi,
                     outs=[_out2d(s_len, D_MODEL, F32, tm, tw), _out2d(s_len, D_MODEL, BF16, tm, tw)],
                     extras=[(x, _tile_spec(tm, tw)), (row(norm2_g), pl.BlockSpec((1, tw), lambda i, j, k: (0, 0)))])

    def up_epi(acc, e, o):
        o[0][...] = jnp.square(jnp.maximum(acc, 0.0)).astype(BF16)
    act = _mm_fwd("mlp_up", h2, w_up3, tb, tw, D_MODEL, epilogue=up_epi, outs=[_out2d(s_len, D_FF, BF16, tb, tw)])

    def loss_epi(acc, e, o):
        err = e[0] + acc - e[1]
        o[0][...] = (err * (1.0 / D_MODEL)).astype(BF16)
        sq = err * err
        part = sq[:, 0:LANE]
        for c0 in range(LANE, D_MODEL, LANE):
            part = part + sq[:, c0:c0 + LANE]
        o[1][...] = jnp.sum(part.reshape(tl // 8, 8, LANE), axis=0)
    tl = 512
    dy_b, loss_part = _mm_fwd(
        "mlp_down", act, w_down, tl, D_MODEL, D_FF, epilogue=loss_epi,
        outs=[_out2d(s_len, D_MODEL, BF16, tl, D_MODEL),
              (jax.ShapeDtypeStruct((8 * (s_len // tl), LANE), F32), pl.BlockSpec((8, LANE), lambda i, j, k: (i, 0)))],
        extras=[(x1, _tile_spec(tl, D_MODEL)), (target, _tile_spec(tl, D_MODEL))])

    def dup_epi(acc, e, o):
        o[0][...] = (acc * (2.0 * jnp.sqrt(e[0].astype(F32)))).astype(BF16)
    full = lambda cols: pl.BlockSpec((tm, cols), lambda i, j, n: (i, n))
    tokens = lambda cols: pl.BlockSpec((s_len, cols), lambda i, j, m: (m, j))
    dup = _mm_bwd_x("d_act", dy_b, full(D_MODEL), w_down, tm, tw, D_MODEL, s_len, D_MODEL, epilogue=dup_epi,
                    outs=[_out2d(s_len, D_FF, BF16, tm, tw)], extras=[(act, _tile_spec(tm, tw))])
    g_down = _mm_bwd_w("g_down", act, dy_b, tokens(D_MODEL), D_MODEL, 512, D_MODEL, s_len, False)
    g_up = _mm_bwd_w("g_up", h2, dup, tokens(512), D_FF, D_MODEL, 512, s_len, True)
    comm.grads_ready("mlp", dict(w_down=g_down, w_up=g_up))
    dx1_b, dg2 = carrying("d_h2", _d_norm_input, "d_h2", [(dup.reshape(1, s_len, D_FF), w_up3)],
                          x1, row(norm2_g), dy_b, BF16)

    def dmerge_epi(acc, e, o):
        ya_v, yc_v, ga, gc = [t.astype(F32) for t in e]
        o[0][...] = (acc * ga).astype(BF16)
        o[1][...] = (acc * gc).astype(BF16)
        o[2][0] = (acc * ya_v * ga * (1.0 - ga)).astype(BF16)
        o[2][1] = (acc * yc_v * gc * (1.0 - gc)).astype(BF16)
    dya, dyc, dgp2 = _mm_bwd_x(
        "d_merged", dx1_b, full(D_MODEL), w_out, tm, tw, D_MODEL, s_len, D_MODEL, epilogue=dmerge_epi,
        outs=[_out2d(s_len, D_MODEL, BF16, tm, tw), _out2d(s_len, D_MODEL, BF16, tm, tw),
              (jax.ShapeDtypeStruct((2, s_len, D_MODEL), BF16), pl.BlockSpec((2, tm, tw), lambda i, j, n: (0, i, j)))],
        extras=[(ya, _tile_spec(tm, tw)), (yc, _tile_spec(tm, tw)), gate_a, gate_c])
    g_out = _mm_bwd_w("g_out", merged, dx1_b, tokens(512), D_MODEL, D_MODEL, 512, s_len, False)
    d_attn = _mm_bwd_x("d_attn", dya, full(D_MODEL), w_ap, tm, tw, D_MODEL, s_len, D_MODEL,
                       outs=[_out2d(s_len, D_MODEL, BF16, tm, tw)])
    g_ap = _mm_bwd_w("g_attn_proj", attn, dya, tokens(512), D_MODEL, D_MODEL, 512, s_len, False)
    d_yconv = _mm_bwd_x("d_yconv", dyc, full(D_MODEL), w_cp, tm, tw, D_MODEL, s_len, D_MODEL,
                        outs=[_out2d(s_len, D_MODEL, BF16, tm, tw)])
    g_cp = _mm_bwd_w("g_conv_proj", yconv, dyc, tokens(512), D_MODEL, D_MODEL, 512, s_len, False)
    piece = lambda width: (lambda blk: (blk * width) // D_MODEL, lambda blk: (blk * width % D_MODEL) // width)
    pc, cb = piece(512)
    pieces = pl.BlockSpec((None, s_len, 512), lambda i, j, m: (pc(j), m, cb(j)))
    g_gate = _mm_bwd_w("g_gate", h, dgp2, pieces, 2 * D_MODEL, D_MODEL, 512, s_len, True)
    comm.grads_ready("proj", dict(w_out=g_out, w_attn_proj=g_ap, w_conv_proj=g_cp, w_gate=g_gate))

    dproj6, dbias = carrying("attn_bwd", _attn_bwd, qkn, v, attn, d_attn, bias, lse)
    dproj6, dgq, dgk = _qk_norm_bwd(dproj6, qk_raw, gq, gk)
    dproj6, dconv_wb = carrying("conv_bwd", _conv_bwd, dproj6, d_yconv, conv_in, conv_w, row(conv_b))

    g_in = carrying("g_in", _mm_bwd_w, "g_in", h, dproj6, pieces, 6 * D_MODEL, D_MODEL, 512, s_len, True)
    comm.grads_ready("in", dict(w_in=g_in))
    d_rel = carrying("bias_reduce", _bias_reduce, dbias)
    grad_x, dg1 = carrying("d_h", _d_norm_input, "d_h", [(dproj6, w_in3), (dgp2, w_gate3)],
                           x, row(norm1_g), dx1_b, F32)

    def bsum(t, f):
        return [], [t[0].astype(F32), t[1].astype(F32)]
    ts = 512
    db_a, db_c = carrying("b_gate_sum", _ew, "b_gate_sum", bsum,
                          [(dgp2, pl.BlockSpec((None, ts, D_MODEL), lambda i: (0, i, 0))),
                           (dgp2, pl.BlockSpec((None, ts, D_MODEL), lambda i: (1, i, 0)))],
                          [], [], sums=[D_MODEL, D_MODEL], ts=ts)

    big = dict(w_in=g_in, w_attn_proj=g_ap, w_conv_proj=g_cp, w_gate=g_gate, w_out=g_out,
               w_up=g_up, w_down=g_down)
    small = dict(norm1_g=dg1, norm2_g=dg2, conv_wb=dconv_wb, b_gate=(db_a, db_c),
                 q_norm_g=dgq, k_norm_g=dgk, rel_bias=d_rel)
    return loss_part, grad_x, big, small


def _finish_loss(loss_part):
    def body(l_ref, lo_ref):
        total = jnp.sum(jnp.sum(l_ref[...], axis=0, keepdims=True), axis=1, keepdims=True)
        lo_ref[...] = jnp.broadcast_to(total * (0.5 / D_MODEL), lo_ref.shape)

    return pl.pallas_call(body, name="finish_loss", out_shape=jax.ShapeDtypeStruct((8, LANE), F32))(loss_part)


def _place():
    return lax.axis_index("x"), lax.axis_index("y"), lax.axis_index("c")


def _other_chips(x, y):
    return [(1 - x, y), (x, 1 - y), (1 - x, 1 - y)]


def _cast_into_slot(name, where, w):
    r, cols = w.shape
    ts = 256

    def body(w_ref, x_ref, o_ref):
        o_ref[...] = x_ref[...].astype(o_ref.dtype)

    return pl.pallas_call(
        body, name=name,
        grid_spec=pltpu.PrefetchScalarGridSpec(
            num_scalar_prefetch=1, grid=(r // ts,),
            in_specs=[pl.BlockSpec((ts, cols), lambda i, w: (i, 0))],
            out_specs=pl.BlockSpec((None, ts, cols), lambda i, w: (w[0], i, 0))),
        out_shape=jax.ShapeDtypeStruct((N_SHARD, r, cols), BF16),
        compiler_params=_params("parallel"),
    )(where, w)


def _remote(src, dst, send, recv, k, to):
    return pltpu.make_async_remote_copy(src_ref=src, dst_ref=dst, send_sem=send.at[k], recv_sem=recv.at[k],
                                        device_id=to, device_id_type=MESH)


def _gather_riders(slots):
    n = len(slots)

    def copy(refs, w, j, shard, which, send, recv, to):
        hr = slots[w].shape[1] // 2
        ref = refs[w].at[shard, pl.ds(which * hr, hr)]
        return _remote(ref, ref, send, recv, 3 * w + j, to)

    def for_each_peer(fn):
        x, y, c = _place()
        for w in range(n):
            for j, chip in enumerate(_other_chips(x, y)):
                fn(w, j, 2 * x + y, 2 * chip[0] + chip[1], c, (*chip, c), (x, y, 1 - c))

    def chips_start(_, refs, send, recv):
        for_each_peer(lambda w, j, mine, theirs, c, peer, sib: copy(refs, w, j, mine, c, send, recv, peer).start())

    def chips_finish(_, refs, send, recv):
        for_each_peer(lambda w, j, mine, theirs, c, peer, sib: copy(refs, w, j, theirs, c, send, recv, peer).wait_recv())
        for_each_peer(lambda w, j, mine, theirs, c, peer, sib: copy(refs, w, j, mine, c, send, recv, peer).wait_send())

    def sibling_start(_, refs, send, recv):
        for_each_peer(lambda w, j, mine, theirs, c, peer, sib: copy(refs, w, j, theirs, c, send, recv, sib).start())

    def sibling_finish(_, refs, send, recv):
        for_each_peer(lambda w, j, mine, theirs, c, peer, sib: copy(refs, w, j, theirs, 1 - c, send, recv, sib).wait_recv())
        for_each_peer(lambda w, j, mine, theirs, c, peer, sib: copy(refs, w, j, theirs, c, send, recv, sib).wait_send())

    return (lambda arrays: _Rider([], arrays, 3 * n, chips_start, chips_finish),
            lambda arrays: _Rider([], arrays, 3 * n, sibling_start, sibling_finish))


def _pair_exchange_rider(grads, landing):
    n = len(grads)

    def copies(srcs, dsts, send, recv):
        x, y, c = _place()
        out = []
        for w in range(n):
            hr = grads[w].shape[1] // 2
            out.append(_remote(srcs[w].at[:, pl.ds((1 - c) * hr, hr)], dsts[w], send, recv, w, (x, y, 1 - c)))
        return out

    def start(srcs, dsts, send, recv):
        for cp in copies(srcs, dsts, send, recv):
            cp.start()

    def finish(srcs, dsts, send, recv):
        for cp in copies(srcs, dsts, send, recv):
            cp.wait()

    return _Rider(grads, landing, n, start, finish)


def _row_tile(hr):
    return min(hr, 256)


def _pair_add(name, where, grad, got):
    _, hr, cols = got.shape
    tr = _row_tile(hr)
    nblk = hr // tr

    def body(w_ref, g_ref, r_ref, o_ref):
        o_ref[...] = (g_ref[...] + r_ref[...]).astype(o_ref.dtype)

    other = lambda k, w: (w[0] + 1 + k) % N_SHARD
    return pl.pallas_call(
        body, name=name,
        grid_spec=pltpu.PrefetchScalarGridSpec(
            num_scalar_prefetch=1, grid=(N_SHARD - 1, nblk),
            in_specs=[pl.BlockSpec((None, tr, cols), lambda k, i, w: (other(k, w), w[1] * nblk + i, 0)),
                      pl.BlockSpec((None, tr, cols), lambda k, i, w: (other(k, w), i, 0))],
            out_specs=pl.BlockSpec((None, tr, cols), lambda k, i, w: (other(k, w), i, 0))),
        out_shape=jax.ShapeDtypeStruct(got.shape, BF16),
        compiler_params=_params("parallel", "parallel"),
    )(where, grad, got)


def _chip_exchange_rider(partials, landing):
    n = len(partials)

    def copies(srcs, dsts, send, recv):
        x, y, c = _place()
        return [_remote(srcs[w].at[2 * chip[0] + chip[1]], dsts[w].at[j], send, recv, 3 * w + j, (*chip, c))
                for w in range(n) for j, chip in enumerate(_other_chips(x, y))]

    def start(srcs, dsts, send, recv):
        for cp in copies(srcs, dsts, send, recv):
            cp.start()

    def finish(srcs, dsts, send, recv):
        for cp in copies(srcs, dsts, send, recv):
            cp.wait()

    return _Rider(partials, landing, 3 * n, start, finish)


def _final_add(name, where, grad, got, arrived):
    _, hr, cols = got.shape
    tr = _row_tile(hr)
    nblk = hr // tr

    def body(w_ref, g_ref, r_ref, a_ref, o_ref):
        acc = g_ref[...] + r_ref[...]
        for j in range(3):
            acc = acc + a_ref[j].astype(F32)
        o_ref[...] = acc

    return pl.pallas_call(
        body, name=name,
        grid_spec=pltpu.PrefetchScalarGridSpec(
            num_scalar_prefetch=1, grid=(nblk,),
            in_specs=[pl.BlockSpec((None, tr, cols), lambda i, w: (w[0], w[1] * nblk + i, 0)),
                      pl.BlockSpec((None, tr, cols), lambda i, w: (w[0], i, 0)),
                      pl.BlockSpec((3, tr, cols), lambda i, w: (0, i, 0))],
            out_specs=pl.BlockSpec((tr, cols), lambda i, w: (w[1] * nblk + i, 0))),
        out_shape=jax.ShapeDtypeStruct((2 * hr, cols), F32),
        compiler_params=_params("parallel"),
    )(where, grad, got, arrived)


def _pair_share_rider(shards):
    n = len(shards)

    def half(refs, w, which):
        hr = shards[w].shape[0] // 2
        return refs[w].at[pl.ds(which * hr, hr)]

    def start(_, refs, send, recv):
        x, y, c = _place()
        for w in range(n):
            _remote(half(refs, w, c), half(refs, w, c), send, recv, w, (x, y, 1 - c)).start()

    def finish(_, refs, send, recv):
        x, y, c = _place()
        for w in range(n):
            _remote(half(refs, w, 1 - c), half(refs, w, 1 - c), send, recv, w, (x, y, 1 - c)).wait_recv()
        for w in range(n):
            _remote(half(refs, w, c), half(refs, w, c), send, recv, w, (x, y, 1 - c)).wait_send()

    return _Rider([], shards, n, start, finish)


class _Exchange:
    PLAN = {"bias_expand": [("early", "gather")], "norm1": [("early", "forward")],
            "proj_conv": [("gate", "gather")], "proj_v": [("gate", "forward")],
            "proj_qk": [("late", "gather")], "gates": [("late", "forward")],
            "attn_fwd": [("mlp_w", "gather")], "conv_fwd": [("mlp_w", "forward")],
            "d_h2": [("mlp", "pair")], "attn_bwd": [("mlp", "chips"), ("proj", "pair")], "conv_bwd": [("mlp", "share")],
            "g_in": [("proj", "chips")], "bias_reduce": [("proj", "share"), ("in", "pair")],
            "d_h": [("in", "chips")], "b_gate_sum": [("in", "share")]}

    def __init__(self, where, early_slots, gate_slots, late_slots):
        self.where = where
        self.slots = dict(early=early_slots, gate=gate_slots, late=late_slots[4:], mlp_w=late_slots[:4])
        self.stage = {g: dict(zip(("gather", "forward"), _gather_riders(s))) for g, s in self.slots.items()}
        self.groups, self.reduced, self.pending = {}, {}, []

    def early_weights(self):
        w_in3, small = self.slots["early"]
        conv_w = small[:, :3, :].transpose(1, 0, 2).reshape(3, N_SHARD * small.shape[2])
        return w_in3, conv_w

    def gate_weight(self):
        return self.slots["gate"][0]

    def late_weights(self):
        rows = lambda a: a.reshape(a.shape[0] * a.shape[1], a.shape[2])
        w_ap, w_cp, w_out, w_up3 = self.slots["mlp_w"]
        (w_down,) = self.slots["late"]
        return rows(w_ap), rows(w_cp), rows(w_out), w_up3, rows(w_down)

    def grads_ready(self, group, grads):
        names = list(grads)
        g4 = [g if g.ndim == 3 else g.reshape(N_SHARD, -1, g.shape[1]) for g in grads.values()]
        self.groups[group] = dict(names=names, g4=g4)

    def riders(self, at):
        self.pending = self.PLAN.get(at, [])
        out = []
        for group, stage in self.pending:
            if group in self.slots:
                out.append(self.stage[group][stage](self.slots[group]))
                continue
            st = self.groups[group]
            if stage == "pair":
                landing = [lax.empty((N_SHARD, g.shape[1] // 2, g.shape[2]), F32) for g in st["g4"]]
                out.append(_pair_exchange_rider(st["g4"], landing))
            elif stage == "chips":
                landing = [lax.empty((3,) + p.shape[1:], p.dtype) for p in st["partial"]]
                out.append(_chip_exchange_rider(st["partial"], landing))
            else:
                out.append(_pair_share_rider(st["halves"]))
        return out

    def done(self, at, carried):
        for (group, stage), arrays in zip(self.pending, carried):
            if group in self.slots:
                self.slots[group] = arrays
                continue
            st = self.groups[group]
            tag = lambda what, n: what + "_" + n
            if stage == "pair":
                st["got"] = arrays
                st["partial"] = [_pair_add(tag("pair_add", n), self.where, g, r)
                                 for n, g, r in zip(st["names"], st["g4"], arrays)]
            elif stage == "chips":
                st["halves"] = [_final_add(tag("final_add", n), self.where, g, r, a)
                                for n, g, r, a in zip(st["names"], st["g4"], st["got"], arrays)]
            else:
                self.reduced.update(zip(st["names"], arrays))


SMALL_ROWS = 32
N_DEV = 8


def _all_reduce_small(pack):
    def body(p_ref, o_ref, buf, send, recv):
        x, y, c = _place()
        buf[4 * x + 2 * y + c] = p_ref[...]
        copies, waits = [], []
        for k in range(1, N_DEV):
            px = 1 - x if k & 4 else x
            py = 1 - y if k & 2 else y
            pc = 1 - c if k & 1 else c
            copies.append(_remote(p_ref, buf.at[4 * x + 2 * y + c], send, recv, k - 1, (px, py, pc)))
            waits.append(_remote(p_ref, buf.at[4 * px + 2 * py + pc], send, recv, k - 1, (px, py, pc)))
        for cp in copies:
            cp.start()
        for cp in waits:
            cp.wait_recv()
        acc = buf[0]
        for d in range(1, N_DEV):
            acc = acc + buf[d]
        o_ref[...] = acc
        for cp in copies:
            cp.wait_send()

    return pl.pallas_call(
        body, name="all_reduce_small",
        out_shape=jax.ShapeDtypeStruct(pack.shape, F32),
        scratch_shapes=[pltpu.VMEM((N_DEV,) + pack.shape, F32),
                        pltpu.SemaphoreType.DMA((N_DEV - 1,)), pltpu.SemaphoreType.DMA((N_DEV - 1,))],
    )(pack)


def _adamw(name, w, g, m, v):
    c1 = 1.0 - ADAM_B1 ** ADAM_STEP
    c2 = 1.0 - ADAM_B2 ** ADAM_STEP

    def fn(t, f):
        wv, gv, mv, vv = t
        m2 = ADAM_B1 * mv + (1.0 - ADAM_B1) * gv
        v2 = ADAM_B2 * vv + (1.0 - ADAM_B2) * (gv * gv)
        delta = -ADAM_LR * ((m2 / c1) / (jnp.sqrt(v2 / c2) + ADAM_EPS) + ADAM_WD * wv)
        return [delta, m2, v2], []

    cols = w.shape[1]
    return _ew(name, fn, [w, g, m, v], [], [(cols, F32)] * 3, ts=min(w.shape[0], 256))


LOSS_ROW = 26


def _pack_small(norm1_g, norm2_g, conv_b, b_gate, conv_w, q_norm_g, k_norm_g, rel_bias, loss=None):
    pack = jnp.zeros((SMALL_ROWS, D_MODEL), F32)
    for r0, v in ((0, norm1_g), (1, norm2_g), (2, conv_b), (3, b_gate.reshape(2, D_MODEL)), (5, conv_w),
                  (8, q_norm_g), (9, k_norm_g), (10, rel_bias)) + (((LOSS_ROW, loss),) if loss is not None else ()):
        v = v.reshape(-1, v.shape[-1]).astype(F32)
        pack = pack.at[r0:r0 + v.shape[0], :v.shape[1]].set(v)
    return pack


def _unpack_small(pack, conv_cols):
    return dict(norm1_g=pack[0], norm2_g=pack[1], conv_b=pack[2], b_gate=pack[3:5].reshape(2 * D_MODEL),
                conv_w=pack[5:8, :conv_cols], q_norm_g=pack[8, :HEAD_DIM], k_norm_g=pack[9, :HEAD_DIM],
                rel_bias=pack[10:10 + N_HEADS, :N_REL])


BIG = ["w_in", "w_attn_proj", "w_conv_proj", "w_gate", "w_out", "w_up", "w_down"]
LATE = ["w_attn_proj", "w_conv_proj", "w_out", "w_up", "w_down"]
WEIGHTS = ["norm1_g", "w_in", "q_norm_g", "k_norm_g", "rel_bias", "conv_w", "conv_b", "w_attn_proj",
           "w_conv_proj", "w_gate", "b_gate", "w_out", "norm2_g", "w_up", "w_down"]


def kernel(x, norm1_g, w_in, q_norm_g, k_norm_g, rel_bias, conv_w, conv_b, w_attn_proj, w_conv_proj, w_gate, b_gate, w_out, norm2_g, w_up, w_down, loss_target, m_norm1_g, m_w_in, m_q_norm_g, m_k_norm_g, m_rel_bias, m_conv_w, m_conv_b, m_w_attn_proj, m_w_conv_proj, m_w_gate, m_b_gate, m_w_out, m_norm2_g, m_w_up, m_w_down, v_norm1_g, v_w_in, v_q_norm_g, v_k_norm_g, v_rel_bias, v_conv_w, v_conv_b, v_w_attn_proj, v_w_conv_proj, v_w_gate, v_b_gate, v_w_out, v_norm2_g, v_w_up, v_w_down):
    given = dict(locals())
    w = {n: given[n] for n in WEIGHTS}
    m = {n: given["m_" + n] for n in WEIGHTS}
    v = {n: given["v_" + n] for n in WEIGHTS}
    s_len = x.shape[1]
    shard = 2 * lax.axis_index("x") + lax.axis_index("y")
    where = jnp.stack([shard, lax.axis_index("c")]).astype(jnp.int32)
    conv_cols = conv_w.shape[1]

    small_in = lax.dynamic_update_slice(jnp.zeros((N_SHARD, 16, conv_cols), F32), conv_w[None], (shard, 0, 0))
    slot = {n: _cast_into_slot("cast_" + n, where, w[n]) for n in BIG}
    comm = _Exchange(where, [slot["w_in"], small_in], [slot["w_gate"]], [slot[n] for n in LATE])

    loss_part, grad_x, _, small = _local_grads(
        x.reshape(s_len, D_MODEL), loss_target.reshape(s_len, D_MODEL), norm1_g, q_norm_g, k_norm_g,
        rel_bias, conv_b, b_gate, norm2_g, comm)
    grad = dict(comm.reduced)

    loss_local = _finish_loss(loss_part)
    pack = _pack_small(small["norm1_g"], small["norm2_g"], small["conv_wb"][3], jnp.concatenate(small["b_gate"], axis=1),
                       small["conv_wb"][0:3], small["q_norm_g"], small["k_norm_g"], small["rel_bias"],
                       loss=loss_local[0:1, :])
    total = _all_reduce_small(pack)
    g_small = _unpack_small(total, D_MODEL)
    g_small["conv_w"] = lax.dynamic_slice(g_small["conv_w"], (0, shard * conv_cols), (3, conv_cols))
    grad.update(g_small)

    delta, new_m, new_v = {}, {}, {}
    for n in BIG:
        delta[n], new_m[n], new_v[n] = _adamw("adamw_" + n, w[n], grad[n], m[n], v[n])
    small_names = [n for n in WEIGHTS if n not in BIG]
    packs = [_pack_small(**{n: src[n] for n in small_names}) for src in (w, grad, m, v)]
    for out, packed in zip((delta, new_m, new_v), _adamw("adamw_small", *packs)):
        out.update({n: a.reshape(w[n].shape) for n, a in _unpack_small(packed, conv_cols).items()})

    outs = [total[LOSS_ROW, 0], grad_x.reshape(x.shape)]
    for group in (grad, delta, new_m, new_v):
        outs += [group[n].reshape(w[n].shape) for n in WEIGHTS]
    return tuple(outs)
```

```python
import jax
import jax.numpy as jnp
from jax import lax
from jax.experimental import pallas as pl
from jax.experimental.pallas import tpu as pltpu

F32 = jnp.float32
BF16 = jnp.bfloat16

D_MODEL = 1024
N_HEADS = 16
HEAD_DIM = 64
CHUNK = 64
N_PREV_CHUNKS = 8
MAX_REL = 256
D_FF = 4096
N_REL = 2 * MAX_REL + 1
REL_PAD = 640
EPS = 1e-6
NEG_INF = -1e30
QK_SCALE = HEAD_DIM ** -0.5

SUPER = 4 * CHUNK
BAND = SUPER + N_PREV_CHUNKS * CHUNK
SKEW_W = 1024
N_SHARD = 4
LANE = 128
MXU_DIM = 256
VMEM_LIMIT = 48 * 1024 * 1024

ADAM_LR = 0.001
ADAM_B1 = 0.9
ADAM_B2 = 0.999
ADAM_EPS = 1e-08
ADAM_WD = 0.01
ADAM_STEP = 10

MESH = pl.DeviceIdType.MESH
NN = (((1,), (0,)), ((), ()))
NT = (((1,), (1,)), ((), ()))
TN = (((0,), (0,)), ((), ()))


def _params(*sem):
    return pltpu.CompilerParams(dimension_semantics=sem or None, vmem_limit_bytes=VMEM_LIMIT)


HBM_SPEC = pl.BlockSpec(memory_space=pl.ANY)


class _Rider:
    def __init__(self, sources, arrays, n_sem, start, finish):
        self.sources, self.arrays, self.n_sem, self.start, self.finish = sources, arrays, n_sem, start, finish


def _carry(riders, body, *, name, out_shape, grid=(), in_specs=None, out_specs=None, scratch_shapes=(),
           semantics=(), input_output_aliases=None):
    aliases = dict(input_output_aliases or {})
    if not riders:
        kw = {} if in_specs is None else dict(in_specs=in_specs, out_specs=out_specs)
        return pl.pallas_call(body, name=name, grid=grid, out_shape=out_shape, scratch_shapes=scratch_shapes,
                              input_output_aliases=aliases, compiler_params=_params(*semantics), **kw)
    single = not isinstance(out_shape, (list, tuple))
    shapes = [out_shape] if single else list(out_shape)
    n_out, n_scr = len(shapes), len(scratch_shapes)
    in_hbm = lambda a: pltpu.with_memory_space_constraint(a, pltpu.HBM)
    srcs = [in_hbm(a) for r in riders for a in r.sources]
    arrs = [in_hbm(a) for r in riders for a in r.arrays]
    vmem = pl.BlockSpec(memory_space=pltpu.VMEM)

    def run(*args):
        n_in = len(args)

        def wrapped(*refs):
            pos = n_in
            src_refs = refs[pos:pos + len(srcs)]
            pos += len(srcs) + len(arrs)
            outs = refs[pos:pos + n_out]
            pos += n_out
            arr_refs = refs[pos:pos + len(arrs)]
            pos += len(arrs)
            scratch = refs[pos:pos + n_scr]
            sems = refs[pos + n_scr:]
            first, last = True, True
            for d, size in enumerate(grid):
                first = jnp.logical_and(first, pl.program_id(d) == 0)
                last = jnp.logical_and(last, pl.program_id(d) == size - 1)

            def each(method):
                s0 = a0 = 0
                for k, r in enumerate(riders):
                    getattr(r, method)(src_refs[s0:s0 + len(r.sources)], arr_refs[a0:a0 + len(r.arrays)],
                                       sems[2 * k], sems[2 * k + 1])
                    s0, a0 = s0 + len(r.sources), a0 + len(r.arrays)

            pl.when(first)(lambda: each("start"))
            body(*refs[:n_in], *outs, *scratch)
            pl.when(last)(lambda: each("finish"))

        ins = [vmem] * n_in if in_specs is None else list(in_specs)
        if out_specs is None:
            o_specs = [vmem] * n_out
        else:
            o_specs = [out_specs] if single else list(out_specs)
        for k in range(len(arrs)):
            aliases[n_in + len(srcs) + k] = n_out + k
        res = pl.pallas_call(
            wrapped, name=name, grid=grid,
            in_specs=ins + [HBM_SPEC] * (len(srcs) + len(arrs)),
            out_specs=o_specs + [HBM_SPEC] * len(arrs),
            out_shape=shapes + [jax.ShapeDtypeStruct(a.shape, a.dtype) for a in arrs],
            scratch_shapes=list(scratch_shapes) + [pltpu.SemaphoreType.DMA((r.n_sem,)) for r in riders for _ in range(2)],
            input_output_aliases=aliases,
            compiler_params=_params(*["arbitrary"] * len(grid)),
        )(*args, *srcs, *arrs)
        core, rest = res[:n_out], list(res[n_out:])
        carried, a0 = [], 0
        for r in riders:
            carried.append(rest[a0:a0 + len(r.arrays)])
            a0 += len(r.arrays)
        return (core[0] if single else core), carried

    return run


def _mm(name, dims, a, a_spec, b, b_spec, grid, tile, outs, epilogue=None, extras=(), riders=()):
    nk, ne, no = grid[2], len(extras), len(outs)

    def body(a_ref, b_ref, *refs):
        e_refs, o_refs = refs[:ne], refs[ne:ne + no]
        part = lax.dot_general(a_ref[...], b_ref[...], dims, preferred_element_type=F32)

        def finish(acc):
            if epilogue is None:
                o_refs[0][...] = acc.astype(o_refs[0].dtype)
            else:
                epilogue(acc, [r[...] for r in e_refs], o_refs)

        if nk == 1:
            finish(part)
        else:
            acc_ref = refs[ne + no]
            k = pl.program_id(2)

            @pl.when(k == 0)
            def _():
                acc_ref[...] = part

            @pl.when(k > 0)
            def _():
                acc_ref[...] += part

            @pl.when(k == nk - 1)
            def _():
                finish(acc_ref[...])

    res = _carry(
        riders, body, name=name, grid=grid,
        in_specs=[a_spec, b_spec] + [s for _, s in extras],
        out_specs=[s for _, s in outs],
        out_shape=[s for s, _ in outs],
        scratch_shapes=[pltpu.VMEM(tile, F32)] if nk > 1 else [],
        semantics=("parallel", "parallel", "arbitrary"),
    )(a, b, *[e for e, _ in extras])
    res, carried = res if riders else (res, None)
    res = res[0] if no == 1 else res
    return (res, carried) if riders else res


def _tile_spec(tm, tn, col0=0):
    return pl.BlockSpec((tm, tn), lambda i, j, k: (i, j + col0))


def _out2d(m, n, dtype, tm, tn):
    return (jax.ShapeDtypeStruct((m, n), dtype), _tile_spec(tm, tn))


def _mm_fwd(name, a, w, tm, tn, tk, outs=None, epilogue=None, extras=(), col0=0, ncols=None, riders=()):
    m, kdim = a.shape
    if w.ndim == 3:
        per = w.shape[2] // tn
        n = ncols or N_SHARD * w.shape[2]
        w_spec = pl.BlockSpec((None, tk, tn), lambda i, j, k: ((j + col0) // per, k, (j + col0) % per))
    else:
        n = ncols or w.shape[1]
        w_spec = pl.BlockSpec((tk, tn), lambda i, j, k: (k, j + col0))
    if outs is None:
        outs = [_out2d(m, n, F32, tm, tn)]
    return _mm(name, NN, a, pl.BlockSpec((tm, tk), lambda i, j, k: (i, k)), w, w_spec,
               (m // tm, n // tn, kdim // tk), (tm, tn), outs, epilogue, extras, riders)


def _mm_bwd_x(name, g, g_spec, w, tm, tj, tc, m, n_contract, outs=None, epilogue=None, extras=(), riders=()):
    if w.ndim == 3:
        per = w.shape[2] // tc
        kdim = w.shape[1]
        w_spec = pl.BlockSpec((None, tj, tc), lambda i, j, n: (n // per, j, n % per))
    else:
        kdim = w.shape[0]
        w_spec = pl.BlockSpec((tj, tc), lambda i, j, n: (j, n))
    if outs is None:
        outs = [_out2d(m, kdim, F32, tm, tj)]
    return _mm(name, NT, g, g_spec, w, w_spec, (m // tm, kdim // tj, n_contract // tc),
               (tm, tj), outs, epilogue, extras, riders)


def _mm_bwd_w(name, a, g, g_spec, n, tk, tn, tm, sharded, riders=()):
    m, kdim = a.shape
    if sharded:
        per = (n // N_SHARD) // tn
        out = (jax.ShapeDtypeStruct((N_SHARD, kdim, n // N_SHARD), F32),
               pl.BlockSpec((None, tk, tn), lambda i, j, mm: (j // per, i, j % per)))
    else:
        out = (jax.ShapeDtypeStruct((kdim, n), F32), pl.BlockSpec((tk, tn), lambda i, j, mm: (i, j)))
    return _mm(name, TN, a, pl.BlockSpec((tm, tk), lambda i, j, mm: (mm, i)), g, g_spec,
               (kdim // tk, n // tn, m // tm), (tk, tn), [out], riders=riders)


def _ew(name, fn, tiles, fulls, outs, sums=(), ts=512, riders=()):
    tiles = [t if isinstance(t, tuple) else (t, pl.BlockSpec((ts, t.shape[1]), lambda i: (i, 0)))
             for t in tiles]
    s_rows = tiles[0][0].shape[-2]
    nt, nf, no = len(tiles), len(fulls), len(outs)

    def body(*refs):
        t_vals = [r[...] for r in refs[:nt]]
        f_vals = [r[...] for r in refs[nt:nt + nf]]
        o_refs, s_refs = refs[nt + nf:nt + nf + no], refs[nt + nf + no:]
        o_vals, s_vals = fn(t_vals, f_vals)
        for r, v in zip(o_refs, o_vals):
            r[...] = v.astype(r.dtype)
        for r, v in zip(s_refs, s_vals):
            part = jnp.sum(v, axis=0, keepdims=True)

            @pl.when(pl.program_id(0) == 0)
            def _():
                r[...] = part

            @pl.when(pl.program_id(0) > 0)
            def _():
                r[...] += part

    full_specs = [pl.BlockSpec(f.shape, lambda i, nd=f.ndim: (0,) * nd) for f in fulls]
    return _carry(
        riders, body, name=name, grid=(s_rows // ts,),
        in_specs=[s for _, s in tiles] + full_specs,
        out_specs=[pl.BlockSpec((ts, c), lambda i: (i, 0)) for c, _ in outs]
        + [pl.BlockSpec((1, c), lambda i: (0, 0)) for c in sums],
        out_shape=[jax.ShapeDtypeStruct((s_rows, c), dt) for c, dt in outs]
        + [jax.ShapeDtypeStruct((1, c), F32) for c in sums],
        semantics=("arbitrary",),
    )(*[t for t, _ in tiles], *fulls)


def _rms_fwd(name, x, g, riders=()):
    def fn(t, f):
        xv = t[0]
        r = lax.rsqrt(jnp.mean(xv * xv, axis=-1, keepdims=True) + EPS)
        return [xv * r * f[0]], []
    out = _ew(name, fn, [x], [g], [(x.shape[1], BF16)], riders=riders)
    return (out[0][0], out[1]) if riders else out[0]


def _split3(x):
    x1 = x.astype(BF16)
    r1 = x - x1.astype(F32)
    x2 = r1.astype(BF16)
    x3 = (r1 - x2.astype(F32)).astype(BF16)
    return x1, x2, x3


def _rel_class(cp):
    far = (cp < MAX_REL) | (cp > BAND)
    return jnp.where(far, 2 * MAX_REL, BAND - cp)


def _skew_rows(x, sign):
    row = lax.broadcasted_iota(jnp.int32, x.shape, 0)
    for b in range(CHUNK.bit_length() - 1):
        shift = (1 << b) if sign > 0 else SKEW_W - (1 << b)
        x = jnp.where((row >> b) & 1 == 1, pltpu.roll(x, shift, 1), x)
    return x


def _roll_lanes(x, shift):
    return x if shift % SKEW_W == 0 else pltpu.roll(x, shift % SKEW_W, 1)


N_START = 3


def _bias_expand(rel_bias, riders=()):
    rel = jnp.pad(rel_bias, ((0, 0), (0, REL_PAD - N_REL))).reshape(N_HEADS, 1, REL_PAD)

    def body(rel_ref, o_ref):
        cls = lax.broadcasted_iota(jnp.int32, (REL_PAD, SKEW_W), 0)
        cp = lax.broadcasted_iota(jnp.int32, (REL_PAD, SKEW_W), 1)
        onehot = (cls == _rel_class(cp)).astype(BF16)
        rel8 = jnp.broadcast_to(rel_ref[...], (8, REL_PAD))
        trow = sum(jnp.dot(p, onehot, preferred_element_type=F32) for p in _split3(rel8))[0:1]
        first = _skew_rows(jnp.broadcast_to(trow, (CHUNK, SKEW_W)), +1)
        full = jnp.concatenate([_roll_lanes(first, CHUNK * g) for g in range(SUPER // CHUNK)], axis=0)[:, :BAND]
        qc = lax.broadcasted_iota(jnp.int32, (SUPER, BAND), 0) // CHUNK
        kc = lax.broadcasted_iota(jnp.int32, (SUPER, BAND), 1) // CHUNK
        on_band = (kc >= qc) & (kc <= qc + N_PREV_CHUNKS)
        table = jnp.where(on_band, full, NEG_INF).T
        key = lax.broadcasted_iota(jnp.int32, (BAND, SUPER), 0)
        for t in range(N_START):
            o_ref[t] = jnp.where(key < (N_START - 1 - t) * SUPER, NEG_INF, table)

    return _carry(
        riders, body, name="bias_expand", grid=(N_HEADS,),
        in_specs=[pl.BlockSpec((None, 1, REL_PAD), lambda h: (h, 0, 0))],
        out_specs=pl.BlockSpec((N_START, None, BAND, SUPER), lambda h: (0, h, 0, 0)),
        out_shape=jax.ShapeDtypeStruct((N_START, N_HEADS, BAND, SUPER), F32),
        semantics=("arbitrary",),
    )(rel)


def _bias_reduce(dbias, riders=()):
    def body(d_ref, o_ref):
        x = jnp.concatenate([d_ref[...].T, jnp.zeros((SUPER, SKEW_W - BAND), F32)], axis=1)
        folded = sum(_roll_lanes(x[CHUNK * g:CHUNK * (g + 1)], -CHUNK * g) for g in range(SUPER // CHUNK))
        diag = jnp.sum(_skew_rows(folded, -1), axis=0, keepdims=True)
        cp = lax.broadcasted_iota(jnp.int32, (SKEW_W, REL_PAD), 0)
        cls = lax.broadcasted_iota(jnp.int32, (SKEW_W, REL_PAD), 1)
        onehot = (cls == _rel_class(cp)).astype(BF16)
        diag8 = jnp.broadcast_to(diag, (8, SKEW_W))
        o_ref[...] = sum(jnp.dot(p, onehot, preferred_element_type=F32) for p in _split3(diag8))[0:1]

    out = _carry(
        riders, body, name="bias_reduce", grid=(N_HEADS,),
        in_specs=[pl.BlockSpec((None, BAND, SUPER), lambda h: (h, 0, 0))],
        out_specs=pl.BlockSpec((None, 1, REL_PAD), lambda h: (h, 0, 0)),
        out_shape=jax.ShapeDtypeStruct((N_HEADS, 1, REL_PAD), F32),
        semantics=("arbitrary",),
    )(dbias)
    out, carried = out if riders else (out, None)
    out = out.reshape(N_HEADS, REL_PAD)[:, :N_REL]
    return (out, carried) if riders else out


HEADS_PER_STEP = 8
HEAD_COLS = HEADS_PER_STEP * HEAD_DIM
N_HEAD_GROUPS = N_HEADS // HEADS_PER_STEP


def _scores_t(qs, kn, bias_t):
    return jnp.concatenate([lax.dot_general(k, qs, NT, preferred_element_type=F32) for k in kn], axis=0) + bias_t


def _bias_spec():
    return pl.BlockSpec((None, HEADS_PER_STEP, BAND, SUPER), lambda hg, i: (jnp.minimum(i, N_START - 1), hg, 0, 0))


def _band_specs(nb, col0, clamp_hi):
    def spec(d):
        def index(hg, i):
            blk = jnp.maximum(i - d, 0)
            if clamp_hi:
                blk = jnp.minimum(blk, nb - 1)
            return (blk, col0 + hg)
        return pl.BlockSpec((SUPER, HEAD_COLS), index)
    return [spec(2), spec(1), spec(0)]


def _head_sums(y):
    same_head = (lax.broadcasted_iota(jnp.int32, (MXU_DIM, MXU_DIM), 0) // HEAD_DIM
                 == lax.broadcasted_iota(jnp.int32, (MXU_DIM, MXU_DIM), 1) // HEAD_DIM).astype(BF16)
    sums = []
    for c0 in range(0, y.shape[1], MXU_DIM):
        chunk = y[:, c0:c0 + MXU_DIM]
        hi = chunk.astype(BF16)
        lo = (chunk - hi.astype(F32)).astype(BF16)
        sums.append(jnp.dot(hi, same_head, preferred_element_type=F32)
                    + jnp.dot(lo, same_head, preferred_element_type=F32))
    return jnp.concatenate(sums, axis=1)


def _head_unit(x):
    r = lax.rsqrt(_head_sums(x * x) * (1.0 / HEAD_DIM) + EPS)
    return x * r, r


def _head(hh):
    return slice(HEAD_DIM * hh, HEAD_DIM * (hh + 1))


def _key_block(j):
    return slice(SUPER * j, SUPER * (j + 1))


LSE_ROWS = 8


def _attn_fwd(qkn, v, bias, riders=()):
    s_len = qkn.shape[0]
    nb = s_len // SUPER

    def body(q_ref, k0, k1, k2, v0, v1, v2, b_ref, o_ref, lse_ref):
        outs = []
        v_t = [v0[...].T, v1[...].T, v2[...].T]

        def probabilities(hh):
            sl = _head(hh)
            s = _scores_t(q_ref[:, sl], [k0[:, sl], k1[:, sl], k2[:, sl]], b_ref[hh])
            m = jnp.max(s, axis=0, keepdims=True)
            e = jnp.exp(s - m)
            l = jnp.sum(e, axis=0, keepdims=True)
            lse_ref[hh:hh + 1, :] = m + jnp.log(l)
            return e.astype(BF16), 1.0 / l, sl

        def weighted_values(e, inv_l, sl):
            outs.append(sum(jnp.dot(v_t[j][sl, :], e[_key_block(j), :], preferred_element_type=F32)
                            for j in range(3)) * inv_l)

        ready = probabilities(0)
        for hh in range(1, HEADS_PER_STEP):
            following = probabilities(hh)
            weighted_values(*ready)
            ready = following
        weighted_values(*ready)
        o_ref[...] = jnp.concatenate(outs, axis=0).T.astype(o_ref.dtype)

    return _carry(
        riders, body, name="attn_fwd", grid=(N_HEAD_GROUPS, nb),
        in_specs=[pl.BlockSpec((SUPER, HEAD_COLS), lambda hg, i: (i, hg))]
        + _band_specs(nb, N_HEAD_GROUPS, False) + _band_specs(nb, 0, False) + [_bias_spec()],
        out_specs=[pl.BlockSpec((SUPER, HEAD_COLS), lambda hg, i: (i, hg)),
                   pl.BlockSpec((None, LSE_ROWS, SUPER), lambda hg, i: (hg, 0, i))],
        out_shape=[jax.ShapeDtypeStruct((s_len, D_MODEL), BF16),
                   jax.ShapeDtypeStruct((N_HEAD_GROUPS, LSE_ROWS, s_len), F32)],
        semantics=("parallel", "arbitrary"),
    )(qkn, qkn, qkn, qkn, v, v, v, bias)


def _attn_bwd(qkn, v, out, d_out, bias, lse, riders=()):
    s_len = qkn.shape[0]
    nb = s_len // SUPER

    def body(q_ref, k0, k1, k2, v0, v1, v2, o_ref, do_ref, b_ref, lse_ref, dp_ref, db_ref, aq_ref, ak_ref, av_ref):
        i = pl.program_id(1)

        @pl.when(i == 0)
        def _():
            aq_ref[...] = jnp.zeros_like(aq_ref)
            ak_ref[...] = jnp.zeros_like(ak_ref)
            av_ref[...] = jnp.zeros_like(av_ref)
            db_ref[...] = jnp.zeros_like(db_ref)

        @pl.when(i < nb)
        def _():
            dq, dk, dv = [], [[], [], []], [[], [], []]
            ones = jnp.ones((8, HEAD_DIM), BF16)
            k_t = [k0[...].T, k1[...].T, k2[...].T]
            q_t, do_t = q_ref[...].T, do_ref[...].T

            def softmax_grad(hh):
                sl = _head(hh)
                qs, do = q_ref[:, sl], do_ref[:, sl]
                kn = [k0[:, sl], k1[:, sl], k2[:, sl]]
                prod = do.astype(F32) * o_ref[:, sl].astype(F32)
                hi = prod.astype(BF16)
                lo = (prod - hi.astype(F32)).astype(BF16)
                delta = (lax.dot_general(ones, hi, NT, preferred_element_type=F32)
                         + lax.dot_general(ones, lo, NT, preferred_element_type=F32))[0:1]
                lse_row = lse_ref[hh:hh + 1, :]
                pb, dsb = [], []
                for j, (kj, vj) in enumerate(zip(kn, (v0, v1, v2))):
                    rows = _key_block(j)
                    p = jnp.exp(lax.dot_general(kj, qs, NT, preferred_element_type=F32) + b_ref[hh, rows, :] - lse_row)
                    ds = p * (lax.dot_general(vj[:, sl], do, NT, preferred_element_type=F32) - delta)
                    db_ref[hh, rows, :] += ds
                    pb.append(p.astype(BF16))
                    dsb.append(ds.astype(BF16))
                return pb, dsb, qs, do, sl

            def operand_grads(pb, dsb, qs, do, sl):
                dq.append(sum(jnp.dot(k_t[j][sl, :], dsb[j], preferred_element_type=F32) for j in range(3)))
                for j in range(3):
                    dv[j].append(lax.dot_general(do_t[sl, :], pb[j], NT, preferred_element_type=F32))
                    dk[j].append(lax.dot_general(q_t[sl, :], dsb[j], NT, preferred_element_type=F32))

            ready = softmax_grad(0)
            for hh in range(1, HEADS_PER_STEP):
                following = softmax_grad(hh)
                operand_grads(*ready)
                ready = following
            operand_grads(*ready)
            aq_ref[i % 3] = jnp.concatenate(dq, axis=0)
            for j in range(3):
                slot = (i + 1 + j) % 3
                if j < 2:
                    ak_ref[slot] += jnp.concatenate(dk[j], axis=0)
                    av_ref[slot] += jnp.concatenate(dv[j], axis=0)
                else:
                    ak_ref[slot] = jnp.concatenate(dk[j], axis=0)
                    av_ref[slot] = jnp.concatenate(dv[j], axis=0)

        slot = (i + 1) % 3
        dp_ref[0] = aq_ref[slot].T.astype(dp_ref.dtype)
        dp_ref[1] = ak_ref[slot].T.astype(dp_ref.dtype)
        dp_ref[2] = av_ref[slot].T.astype(dp_ref.dtype)

    def qrow(hg, i):
        return (jnp.minimum(i, nb - 1), hg)

    return _carry(
        riders, body, name="attn_bwd", grid=(N_HEAD_GROUPS, nb + 2),
        in_specs=[pl.BlockSpec((SUPER, HEAD_COLS), qrow)]
        + _band_specs(nb, N_HEAD_GROUPS, True) + _band_specs(nb, 0, True)
        + [pl.BlockSpec((SUPER, HEAD_COLS), qrow), pl.BlockSpec((SUPER, HEAD_COLS), qrow), _bias_spec(),
           pl.BlockSpec((None, LSE_ROWS, SUPER), lambda hg, i: (hg, 0, jnp.minimum(i, nb - 1)))],
        out_specs=[pl.BlockSpec((3, SUPER, HEAD_COLS), lambda hg, i: (0, jnp.maximum(i - 2, 0), hg)),
                   pl.BlockSpec((HEADS_PER_STEP, BAND, SUPER), lambda hg, i: (hg, 0, 0))],
        out_shape=[jax.ShapeDtypeStruct((6, s_len, D_MODEL), BF16),
                   jax.ShapeDtypeStruct((N_HEADS, BAND, SUPER), F32)],
        scratch_shapes=[pltpu.VMEM((3, HEAD_COLS, SUPER), F32)] * 3,
        semantics=("parallel", "arbitrary"),
    )(qkn, qkn, qkn, qkn, v, v, v, out, d_out, bias, lse)


def _qk_norm_bwd(dproj6, qk_raw, gq, gk):
    s_len = qk_raw.shape[0]
    ts = min(1024, s_len)

    nsteps = s_len // ts
    half = D_MODEL // 2

    def body(d_ref, raw_ref, gq_ref, gk_ref, o_ref, dgq_ref, dgk_ref, acc_ref):
        step = pl.program_id(0)

        @pl.when(step == 0)
        def _():
            acc_ref[...] = jnp.zeros_like(acc_ref)

        for piece, (g_ref, scale) in enumerate(((gq_ref, QK_SCALE), (gk_ref, 1.0))):
            for c0 in (0, half):
                xhat, r = _head_unit(raw_ref[:, piece * D_MODEL + c0:piece * D_MODEL + c0 + half].astype(F32))
                dn = d_ref[piece, :, c0:c0 + half].astype(F32) * scale
                u = dn * g_ref[...]
                dx = r * (u - xhat * (_head_sums(u * xhat) * (1.0 / HEAD_DIM)))
                o_ref[piece, :, c0:c0 + half] = dx.astype(o_ref.dtype)
                acc_ref[piece:piece + 1, c0:c0 + half] += jnp.sum(dn * xhat, axis=0, keepdims=True)

        @pl.when(step == nsteps - 1)
        def _():
            lane = lax.broadcasted_iota(jnp.int32, (D_MODEL, LANE), 0) % HEAD_DIM
            fold = (lane == lax.broadcasted_iota(jnp.int32, (D_MODEL, LANE), 1)).astype(BF16)
            tot = sum(jnp.dot(p, fold, preferred_element_type=F32) for p in _split3(acc_ref[...]))
            dgq_ref[...] = tot[0:1, :HEAD_DIM]
            dgk_ref[...] = tot[1:2, :HEAD_DIM]

    gain = pl.BlockSpec((1, half), lambda i: (0, 0))
    small = pl.BlockSpec((1, HEAD_DIM), lambda i: (0, 0))
    per_head = lambda g: jnp.tile(g, (1, half // HEAD_DIM))
    return pl.pallas_call(
        body, name="qk_norm_bwd", grid=(nsteps,),
        in_specs=[pl.BlockSpec((2, ts, D_MODEL), lambda i: (0, i, 0)),
                  pl.BlockSpec((ts, 2 * D_MODEL), lambda i: (i, 0)), gain, gain],
        out_specs=[pl.BlockSpec((2, ts, D_MODEL), lambda i: (0, i, 0)), small, small],
        out_shape=[jax.ShapeDtypeStruct(dproj6.shape, dproj6.dtype),
                   jax.ShapeDtypeStruct((1, HEAD_DIM), F32), jax.ShapeDtypeStruct((1, HEAD_DIM), F32)],
        scratch_shapes=[pltpu.VMEM((8, D_MODEL), F32)],
        input_output_aliases={0: 0},
        compiler_params=_params("arbitrary"),
    )(dproj6, qk_raw, per_head(gq), per_head(gk))


CONV_ROWS = 512
HALO = 16


def _rows_with_halo(ref, r0, n, front, s_len):
    zeros = jnp.zeros((HALO, ref.shape[1]), F32)
    if front:
        return (jnp.concatenate([zeros, ref[0:n, :].astype(F32)], axis=0) if r0 == 0
                else ref[r0 - HALO:r0 + n, :].astype(F32))
    return (jnp.concatenate([ref[r0:r0 + n, :].astype(F32), zeros], axis=0) if r0 + n == s_len
            else ref[r0:r0 + n + HALO, :].astype(F32))


def _earlier(ext, k):
    return pltpu.roll(ext, k, 0)[HALO:]


def _later(ext, k):
    n = ext.shape[0]
    return pltpu.roll(ext, n - k, 0)[:n - HALO]


def _conv_cols(col0):
    return lambda s_len: pl.BlockSpec((s_len, LANE), lambda j: (0, col0 + j))


def _conv_fwd(proj, conv_w, conv_b, riders=()):
    s_len = proj.shape[0]

    def body(bg_ref, cg_ref, xc_ref, w_ref, b_ref, o_ref):
        w = [w_ref[t:t + 1, :] for t in range(3)]
        for r0 in range(0, s_len, CONV_ROWS):
            u = _rows_with_halo(cg_ref, r0, CONV_ROWS, True, s_len) * \
                _rows_with_halo(xc_ref, r0, CONV_ROWS, True, s_len)
            conv = b_ref[...] + w[0] * _earlier(u, 2) + w[1] * _earlier(u, 1) + w[2] * u[HALO:]
            o_ref[r0:r0 + CONV_ROWS, :] = (bg_ref[r0:r0 + CONV_ROWS, :].astype(F32) * conv).astype(o_ref.dtype)

    return _carry(
        riders, body, name="conv_fwd", grid=(D_MODEL // LANE,),
        in_specs=[_conv_cols(0)(s_len), _conv_cols(8)(s_len), _conv_cols(16)(s_len),
                  pl.BlockSpec((3, LANE), lambda j: (0, j)), pl.BlockSpec((1, LANE), lambda j: (0, j))],
        out_specs=pl.BlockSpec((s_len, LANE), lambda j: (0, j)),
        out_shape=jax.ShapeDtypeStruct((s_len, D_MODEL), BF16),
        semantics=("parallel",),
    )(proj, proj, proj, conv_w, conv_b)


def _conv_bwd(dproj6, dy, proj, conv_w, conv_b, riders=()):
    s_len = proj.shape[0]

    def body(dy_ref, bg_ref, cg_ref, xc_ref, w_ref, b_ref, _, dp_ref, dw_ref):
        w = [w_ref[t:t + 1, :] for t in range(3)]
        acc = [jnp.zeros((1, LANE), F32) for _ in range(4)]
        for r0 in range(0, s_len, CONV_ROWS):
            rows = slice(r0, r0 + CONV_ROWS)
            u = _rows_with_halo(cg_ref, r0, CONV_ROWS, True, s_len) * \
                _rows_with_halo(xc_ref, r0, CONV_ROWS, True, s_len)
            u2, u1, u0 = _earlier(u, 2), _earlier(u, 1), u[HALO:]
            conv = b_ref[...] + w[0] * u2 + w[1] * u1 + w[2] * u0
            dp_ref[0, rows, :] = (dy_ref[rows, :].astype(F32) * conv).astype(dp_ref.dtype)
            dconv_ext = _rows_with_halo(dy_ref, r0, CONV_ROWS, False, s_len) * \
                _rows_with_halo(bg_ref, r0, CONV_ROWS, False, s_len)
            dconv = dconv_ext[:CONV_ROWS]
            for t, term in enumerate([dconv * u2, dconv * u1, dconv * u0, dconv]):
                acc[t] = acc[t] + jnp.sum(term, axis=0, keepdims=True)
            du = w[2] * dconv + w[1] * _later(dconv_ext, 1) + w[0] * _later(dconv_ext, 2)
            dp_ref[1, rows, :] = (du * xc_ref[rows, :].astype(F32)).astype(dp_ref.dtype)
            dp_ref[2, rows, :] = (du * cg_ref[rows, :].astype(F32)).astype(dp_ref.dtype)
        dw_ref[...] = jnp.zeros_like(dw_ref)
        for t in range(4):
            dw_ref[t:t + 1, :] = acc[t]

    return _carry(
        riders, body, name="conv_bwd", grid=(D_MODEL // LANE,),
        in_specs=[pl.BlockSpec((s_len, LANE), lambda j: (0, j)),
                  _conv_cols(0)(s_len), _conv_cols(8)(s_len), _conv_cols(16)(s_len),
                  pl.BlockSpec((3, LANE), lambda j: (0, j)), pl.BlockSpec((1, LANE), lambda j: (0, j)),
                  pl.BlockSpec(memory_space=pl.ANY)],
        out_specs=[pl.BlockSpec((3, s_len, LANE), lambda j: (1, 0, j)),
                   pl.BlockSpec((8, LANE), lambda j: (0, j))],
        out_shape=[jax.ShapeDtypeStruct(dproj6.shape, dproj6.dtype),
                   jax.ShapeDtypeStruct((8, D_MODEL), F32)],
        input_output_aliases={6: 0},
        semantics=("parallel",),
    )(dy, proj, proj, proj, conv_w, conv_b, dproj6)


def _d_norm_input(name, terms, x, g, dres, out_dtype, riders=()):
    s_len, kdim = terms[0][0].shape[1], terms[0][1].shape[1]
    tm, chunk = 512, 512
    n = len(terms)

    def body(*refs):
        x_ref, g_ref, r_ref, o_ref, dg_ref = refs[2 * n:]
        dh = jnp.zeros((tm, kdim), F32)
        for p_ref, w_ref in zip(refs[:n], refs[n:2 * n]):
            n_piece, _, width = p_ref.shape
            per_shard = w_ref.shape[2]
            for c0 in range(0, n_piece * width, chunk):
                dh = dh + lax.dot_general(p_ref[c0 // width, :, c0 % width:c0 % width + chunk],
                                          w_ref[c0 // per_shard, :, c0 % per_shard:c0 % per_shard + chunk],
                                          NT, preferred_element_type=F32)
        xv = x_ref[...]
        r = lax.rsqrt(jnp.mean(xv * xv, axis=-1, keepdims=True) + EPS)
        xhat = xv * r
        u = dh * g_ref[...]
        dx = r * (u - xhat * jnp.mean(u * xhat, axis=-1, keepdims=True)) + r_ref[...].astype(F32)
        o_ref[...] = dx.astype(o_ref.dtype)
        part = jnp.sum(dh * xhat, axis=0, keepdims=True)

        @pl.when(pl.program_id(0) == 0)
        def _():
            dg_ref[...] = part

        @pl.when(pl.program_id(0) > 0)
        def _():
            dg_ref[...] += part

    tile = pl.BlockSpec((tm, kdim), lambda i: (i, 0))
    vec = pl.BlockSpec((1, kdim), lambda i: (0, 0))
    return _carry(
        riders, body, name=name, grid=(s_len // tm,),
        in_specs=[pl.BlockSpec((p.shape[0], tm, p.shape[2]), lambda i: (0, i, 0)) for p, _ in terms]
        + [pl.BlockSpec(w.shape, lambda i: (0, 0, 0), pipeline_mode=pl.Buffered(1)) for _, w in terms]
        + [tile, vec, tile],
        out_specs=[tile, vec],
        out_shape=[jax.ShapeDtypeStruct((s_len, kdim), out_dtype), jax.ShapeDtypeStruct((1, kdim), F32)],
        semantics=("arbitrary",),
    )(*[p for p, _ in terms], *[w for _, w in terms], x, g, dres)


def _local_grads(x, target, norm1_g, q_norm_g, k_norm_g, rel_bias, conv_b, b_gate, norm2_g, comm):
    s_len = x.shape[0]
    tm = min(1024, s_len)
    tb = min(2048, s_len)
    row = lambda v: v.reshape(1, -1)

    def carrying(at, fn, *args, **kw):
        riders = comm.riders(at)
        out = fn(*args, riders=riders, **kw)
        if riders:
            out, carried = out
            comm.done(at, carried)
        return out

    bias = carrying("bias_expand", _bias_expand, rel_bias)
    h = carrying("norm1", _rms_fwd, "norm1", x, row(norm1_g))
    w_in3, conv_w = comm.early_weights()
    gq, gk = row(q_norm_g), row(k_norm_g)

    def qk_epi(acc, e, o):
        o[0][...] = acc.astype(BF16)
        gain = jnp.where(pl.program_id(1) < 2, e[0] * QK_SCALE, e[1])
        o[1][...] = (_head_unit(acc)[0] * gain).astype(BF16)
    conv_in = carrying("proj_conv", _mm_fwd, "proj_conv", h, w_in3, tb, 1536, D_MODEL, col0=2, ncols=3 * D_MODEL,
                       outs=[_out2d(s_len, 3 * D_MODEL, BF16, tb, 1536)])
    small = pl.BlockSpec((1, 512), lambda i, j, k: (0, 0))
    per_head = lambda g: jnp.tile(g, (1, 512 // HEAD_DIM))
    qk_raw, qkn = carrying("proj_qk", _mm_fwd, "proj_qk", h, w_in3, tm, 512, D_MODEL, ncols=2 * D_MODEL,
                           epilogue=qk_epi, outs=[_out2d(s_len, 2 * D_MODEL, BF16, tm, 512)] * 2,
                           extras=[(per_head(gq), small), (per_head(gk), small)])
    v = carrying("proj_v", _mm_fwd, "proj_v", h, w_in3, tb, 512, D_MODEL, col0=4, ncols=D_MODEL,
                 outs=[_out2d(s_len, D_MODEL, BF16, tb, 512)])
    w_gate3 = comm.gate_weight()

    def gate_epi(acc, e, o):
        o[0][...] = jax.nn.sigmoid(acc + e[0]).astype(BF16)
    gates = carrying("gates", _mm_fwd, "gates", h, w_gate3, tb, 512, D_MODEL, epilogue=gate_epi,
                     outs=[_out2d(s_len, 2 * D_MODEL, BF16, tb, 512)],
                     extras=[(row(b_gate), pl.BlockSpec((1, 512), lambda i, j, k: (0, j)))])

    attn, lse = carrying("attn_fwd", _attn_fwd, qkn, v, bias)
    yconv = carrying("conv_fwd", _conv_fwd, conv_in, conv_w, row(conv_b))
    w_ap, w_cp, w_out, w_up3, w_down = comm.late_weights()
    tw = 1024
    ya = _mm_fwd("attn_proj", attn, w_ap, tm, tw, D_MODEL, outs=[_out2d(s_len, D_MODEL, BF16, tm, tw)])

    def merge_epi(acc, e, o):
        ya_v, ga, gc = [t.astype(F32) for t in e]
        o[0][...] = acc.astype(BF16)
        o[1][...] = (ga * ya_v + gc * acc).astype(BF16)
    gate_a, gate_c = (gates, _tile_spec(tm, tw, 0)), (gates, _tile_spec(tm, tw, 1))
    yc, merged = _mm_fwd("conv_proj", yconv, w_cp, tm, tw, D_MODEL, epilogue=merge_epi,
                         outs=[_out2d(s_len, D_MODEL, BF16, tm, tw), _out2d(s_len, D_MODEL, BF16, tm, tw)],
                         extras=[(ya, _tile_spec(tm, tw)), gate_a, gate_c])

    def res_epi(acc, e, o):
        x1_v = e[0] + acc
        o[0][...] = x1_v
        r = lax.rsqrt(jnp.mean(x1_v * x1_v, axis=-1, keepdims=True) + EPS)
        o[1][...] = (x1_v * r * e[1]).astype(BF16)
    assert tw == D_MODEL
    x1, h2 = _mm_fwd("out_proj", merged, w_out, tm, tw, D_MODEL, epilogue=res_epi,
                     outs=[_out2d(s_len, D_MODEL, F32, tm, tw), _out2d(s_len, D_MODEL, BF16, tm, tw)],
                     extras=[(x, _tile_spec(tm, tw)), (row(norm2_g), pl.BlockSpec((1, tw), lambda i, j, k: (0, 0)))])

    def up_epi(acc, e, o):
        o[0][...] = jnp.square(jnp.maximum(acc, 0.0)).astype(BF16)
    act = _mm_fwd("mlp_up", h2, w_up3, tb, tw, D_MODEL, epilogue=up_epi, outs=[_out2d(s_len, D_FF, BF16, tb, tw)])

    def loss_epi(acc, e, o):
        err = e[0] + acc - e[1]
        o[0][...] = (err * (1.0 / D_MODEL)).astype(BF16)
        sq = err * err
        part = sq[:, 0:LANE]
        for c0 in range(LANE, D_MODEL, LANE):
            part = part + sq[:, c0:c0 + LANE]
        o[1][...] = jnp.sum(part.reshape(tl // 8, 8, LANE), axis=0)
    tl = 512
    dy_b, loss_part = _mm_fwd(
        "mlp_down", act, w_down, tl, D_MODEL, D_FF, epilogue=loss_epi,
        outs=[_out2d(s_len, D_MODEL, BF16, tl, D_MODEL),
              (jax.ShapeDtypeStruct((8 * (s_len // tl), LANE), F32), pl.BlockSpec((8, LANE), lambda i, j, k: (i, 0)))],
        extras=[(x1, _tile_spec(tl, D_MODEL)), (target, _tile_spec(tl, D_MODEL))])

    def dup_epi(acc, e, o):
        o[0][...] = (acc * (2.0 * jnp.sqrt(e[0].astype(F32)))).astype(BF16)
    full = lambda cols: pl.BlockSpec((tm, cols), lambda i, j, n: (i, n))
    tokens = lambda cols: pl.BlockSpec((s_len, cols), lambda i, j, m: (m, j))
    dup = _mm_bwd_x("d_act", dy_b, full(D_MODEL), w_down, tm, tw, D_MODEL, s_len, D_MODEL, epilogue=dup_epi,
                    outs=[_out2d(s_len, D_FF, BF16, tm, tw)], extras=[(act, _tile_spec(tm, tw))])
    g_down = _mm_bwd_w("g_down", act, dy_b, tokens(D_MODEL), D_MODEL, 512, D_MODEL, s_len, False)
    g_up = _mm_bwd_w("g_up", h2, dup, tokens(512), D_FF, D_MODEL, 512, s_len, True)
    comm.grads_ready("mlp", dict(w_down=g_down, w_up=g_up))
    dx1_b, dg2 = carrying("d_h2", _d_norm_input, "d_h2", [(dup.reshape(1, s_len, D_FF), w_up3)],
                          x1, row(norm2_g), dy_b, BF16)

    def dmerge_epi(acc, e, o):
        ya_v, yc_v, ga, gc = [t.astype(F32) for t in e]
        o[0][...] = (acc * ga).astype(BF16)
        o[1][...] = (acc * gc).astype(BF16)
        o[2][0] = (acc * ya_v * ga * (1.0 - ga)).astype(BF16)
        o[2][1] = (acc * yc_v * gc * (1.0 - gc)).astype(BF16)
    dya, dyc, dgp2 = _mm_bwd_x(
        "d_merged", dx1_b, full(D_MODEL), w_out, tm, tw, D_MODEL, s_len, D_MODEL, epilogue=dmerge_epi,
        outs=[_out2d(s_len, D_MODEL, BF16, tm, tw), _out2d(s_len, D_MODEL, BF16, tm, tw),
              (jax.ShapeDtypeStruct((2, s_len, D_MODEL), BF16), pl.BlockSpec((2, tm, tw), lambda i, j, n: (0, i, j)))],
        extras=[(ya, _tile_spec(tm, tw)), (yc, _tile_spec(tm, tw)), gate_a, gate_c])
    g_out = _mm_bwd_w("g_out", merged, dx1_b, tokens(512), D_MODEL, D_MODEL, 512, s_len, False)
    d_attn = _mm_bwd_x("d_attn", dya, full(D_MODEL), w_ap, tm, tw, D_MODEL, s_len, D_MODEL,
                       outs=[_out2d(s_len, D_MODEL, BF16, tm, tw)])
    g_ap = _mm_bwd_w("g_attn_proj", attn, dya, tokens(512), D_MODEL, D_MODEL, 512, s_len, False)
    d_yconv = _mm_bwd_x("d_yconv", dyc, full(D_MODEL), w_cp, tm, tw, D_MODEL, s_len, D_MODEL,
                        outs=[_out2d(s_len, D_MODEL, BF16, tm, tw)])
    g_cp = _mm_bwd_w("g_conv_proj", yconv, dyc, tokens(512), D_MODEL, D_MODEL, 512, s_len, False)
    piece = lambda width: (lambda blk: (blk * width) // D_MODEL, lambda blk: (blk * width % D_MODEL) // width)
    pc, cb = piece(512)
    pieces = pl.BlockSpec((None, s_len, 512), lambda i, j, m: (pc(j), m, cb(j)))
    g_gate = _mm_bwd_w("g_gate", h, dgp2, pieces, 2 * D_MODEL, D_MODEL, 512, s_len, True)
    comm.grads_ready("proj", dict(w_out=g_out, w_attn_proj=g_ap, w_conv_proj=g_cp, w_gate=g_gate))

    dproj6, dbias = carrying("attn_bwd", _attn_bwd, qkn, v, attn, d_attn, bias, lse)
    dproj6, dgq, dgk = _qk_norm_bwd(dproj6, qk_raw, gq, gk)
    dproj6, dconv_wb = carrying("conv_bwd", _conv_bwd, dproj6, d_yconv, conv_in, conv_w, row(conv_b))

    g_in = carrying("g_in", _mm_bwd_w, "g_in", h, dproj6, pieces, 6 * D_MODEL, D_MODEL, 512, s_len, True)
    comm.grads_ready("in", dict(w_in=g_in))
    d_rel = carrying("bias_reduce", _bias_reduce, dbias)
    grad_x, dg1 = carrying("d_h", _d_norm_input, "d_h", [(dproj6, w_in3), (dgp2, w_gate3)],
                           x, row(norm1_g), dx1_b, F32)

    def bsum(t, f):
        return [], [t[0].astype(F32), t[1].astype(F32)]
    ts = 512
    db_a, db_c = carrying("b_gate_sum", _ew, "b_gate_sum", bsum,
                          [(dgp2, pl.BlockSpec((None, ts, D_MODEL), lambda i: (0, i, 0))),
                           (dgp2, pl.BlockSpec((None, ts, D_MODEL), lambda i: (1, i, 0)))],
                          [], [], sums=[D_MODEL, D_MODEL], ts=ts)

    big = dict(w_in=g_in, w_attn_proj=g_ap, w_conv_proj=g_cp, w_gate=g_gate, w_out=g_out,
               w_up=g_up, w_down=g_down)
    small = dict(norm1_g=dg1, norm2_g=dg2, conv_wb=dconv_wb, b_gate=(db_a, db_c),
                 q_norm_g=dgq, k_norm_g=dgk, rel_bias=d_rel)
    return loss_part, grad_x, big, small


def _finish_loss(loss_part):
    def body(l_ref, lo_ref):
        total = jnp.sum(jnp.sum(l_ref[...], axis=0, keepdims=True), axis=1, keepdims=True)
        lo_ref[...] = jnp.broadcast_to(total * (0.5 / D_MODEL), lo_ref.shape)

    return pl.pallas_call(body, name="finish_loss", out_shape=jax.ShapeDtypeStruct((8, LANE), F32))(loss_part)


def _place():
    return lax.axis_index("x"), lax.axis_index("y"), lax.axis_index("c")


def _other_chips(x, y):
    return [(1 - x, y), (x, 1 - y), (1 - x, 1 - y)]


def _cast_into_slot(name, where, w):
    r, cols = w.shape
    ts = 256

    def body(w_ref, x_ref, o_ref):
        o_ref[...] = x_ref[...].astype(o_ref.dtype)

    return pl.pallas_call(
        body, name=name,
        grid_spec=pltpu.PrefetchScalarGridSpec(
            num_scalar_prefetch=1, grid=(r // ts,),
            in_specs=[pl.BlockSpec((ts, cols), lambda i, w: (i, 0))],
            out_specs=pl.BlockSpec((None, ts, cols), lambda i, w: (w[0], i, 0))),
        out_shape=jax.ShapeDtypeStruct((N_SHARD, r, cols), BF16),
        compiler_params=_params("parallel"),
    )(where, w)


def _remote(src, dst, send, recv, k, to):
    return pltpu.make_async_remote_copy(src_ref=src, dst_ref=dst, send_sem=send.at[k], recv_sem=recv.at[k],
                                        device_id=to, device_id_type=MESH)


def _gather_riders(slots):
    n = len(slots)

    def copy(refs, w, j, shard, which, send, recv, to):
        hr = slots[w].shape[1] // 2
        ref = refs[w].at[shard, pl.ds(which * hr, hr)]
        return _remote(ref, ref, send, recv, 3 * w + j, to)

    def for_each_peer(fn):
        x, y, c = _place()
        for w in range(n):
            for j, chip in enumerate(_other_chips(x, y)):
                fn(w, j, 2 * x + y, 2 * chip[0] + chip[1], c, (*chip, c), (x, y, 1 - c))

    def chips_start(_, refs, send, recv):
        for_each_peer(lambda w, j, mine, theirs, c, peer, sib: copy(refs, w, j, mine, c, send, recv, peer).start())

    def chips_finish(_, refs, send, recv):
        for_each_peer(lambda w, j, mine, theirs, c, peer, sib: copy(refs, w, j, theirs, c, send, recv, peer).wait_recv())
        for_each_peer(lambda w, j, mine, theirs, c, peer, sib: copy(refs, w, j, mine, c, send, recv, peer).wait_send())

    def sibling_start(_, refs, send, recv):
        for_each_peer(lambda w, j, mine, theirs, c, peer, sib: copy(refs, w, j, theirs, c, send, recv, sib).start())

    def sibling_finish(_, refs, send, recv):
        for_each_peer(lambda w, j, mine, theirs, c, peer, sib: copy(refs, w, j, theirs, 1 - c, send, recv, sib).wait_recv())
        for_each_peer(lambda w, j, mine, theirs, c, peer, sib: copy(refs, w, j, theirs, c, send, recv, sib).wait_send())

    return (lambda arrays: _Rider([], arrays, 3 * n, chips_start, chips_finish),
            lambda arrays: _Rider([], arrays, 3 * n, sibling_start, sibling_finish))


def _pair_exchange_rider(grads, landing):
    n = len(grads)

    def copies(srcs, dsts, send, recv):
        x, y, c = _place()
        out = []
        for w in range(n):
            hr = grads[w].shape[1] // 2
            out.append(_remote(srcs[w].at[:, pl.ds((1 - c) * hr, hr)], dsts[w], send, recv, w, (x, y, 1 - c)))
        return out

    def start(srcs, dsts, send, recv):
        for cp in copies(srcs, dsts, send, recv):
            cp.start()

    def finish(srcs, dsts, send, recv):
        for cp in copies(srcs, dsts, send, recv):
            cp.wait()

    return _Rider(grads, landing, n, start, finish)


def _row_tile(hr):
    return min(hr, 256)


def _pair_add(name, where, grad, got):
    _, hr, cols = got.shape
    tr = _row_tile(hr)
    nblk = hr // tr

    def body(w_ref, g_ref, r_ref, o_ref):
        o_ref[...] = (g_ref[...] + r_ref[...]).astype(o_ref.dtype)

    other = lambda k, w: (w[0] + 1 + k) % N_SHARD
    return pl.pallas_call(
        body, name=name,
        grid_spec=pltpu.PrefetchScalarGridSpec(
            num_scalar_prefetch=1, grid=(N_SHARD - 1, nblk),
            in_specs=[pl.BlockSpec((None, tr, cols), lambda k, i, w: (other(k, w), w[1] * nblk + i, 0)),
                      pl.BlockSpec((None, tr, cols), lambda k, i, w: (other(k, w), i, 0))],
            out_specs=pl.BlockSpec((None, tr, cols), lambda k, i, w: (other(k, w), i, 0))),
        out_shape=jax.ShapeDtypeStruct(got.shape, BF16),
        compiler_params=_params("parallel", "parallel"),
    )(where, grad, got)


def _chip_exchange_rider(partials, landing):
    n = len(partials)

    def copies(srcs, dsts, send, recv):
        x, y, c = _place()
        return [_remote(srcs[w].at[2 * chip[0] + chip[1]], dsts[w].at[j], send, recv, 3 * w + j, (*chip, c))
                for w in range(n) for j, chip in enumerate(_other_chips(x, y))]

    def start(srcs, dsts, send, recv):
        for cp in copies(srcs, dsts, send, recv):
            cp.start()

    def finish(srcs, dsts, send, recv):
        for cp in copies(srcs, dsts, send, recv):
            cp.wait()

    return _Rider(partials, landing, 3 * n, start, finish)


def _final_add(name, where, grad, got, arrived):
    _, hr, cols = got.shape
    tr = _row_tile(hr)
    nblk = hr // tr

    def body(w_ref, g_ref, r_ref, a_ref, o_ref):
        acc = g_ref[...] + r_ref[...]
        for j in range(3):
            acc = acc + a_ref[j].astype(F32)
        o_ref[...] = acc

    return pl.pallas_call(
        body, name=name,
        grid_spec=pltpu.PrefetchScalarGridSpec(
            num_scalar_prefetch=1, grid=(nblk,),
            in_specs=[pl.BlockSpec((None, tr, cols), lambda i, w: (w[0], w[1] * nblk + i, 0)),
                      pl.BlockSpec((None, tr, cols), lambda i, w: (w[0], i, 0)),
                      pl.BlockSpec((3, tr, cols), lambda i, w: (0, i, 0))],
            out_specs=pl.BlockSpec((tr, cols), lambda i, w: (w[1] * nblk + i, 0))),
        out_shape=jax.ShapeDtypeStruct((2 * hr, cols), F32),
        compiler_params=_params("parallel"),
    )(where, grad, got, arrived)


def _pair_share_rider(shards):
    n = len(shards)

    def half(refs, w, which):
        hr = shards[w].shape[0] // 2
        return refs[w].at[pl.ds(which * hr, hr)]

    def start(_, refs, send, recv):
        x, y, c = _place()
        for w in range(n):
            _remote(half(refs, w, c), half(refs, w, c), send, recv, w, (x, y, 1 - c)).start()

    def finish(_, refs, send, recv):
        x, y, c = _place()
        for w in range(n):
            _remote(half(refs, w, 1 - c), half(refs, w, 1 - c), send, recv, w, (x, y, 1 - c)).wait_recv()
        for w in range(n):
            _remote(half(refs, w, c), half(refs, w, c), send, recv, w, (x, y, 1 - c)).wait_send()

    return _Rider([], shards, n, start, finish)


class _Exchange:
    PLAN = {"bias_expand": [("early", "gather")], "norm1": [("early", "forward")],
            "proj_conv": [("gate", "gather")], "proj_v": [("gate", "forward")],
            "proj_qk": [("late", "gather")], "gates": [("late", "forward")],
            "attn_fwd": [("mlp_w", "gather")], "conv_fwd": [("mlp_w", "forward")],
            "d_h2": [("mlp", "pair")], "attn_bwd": [("mlp", "chips"), ("proj", "pair")], "conv_bwd": [("mlp", "share")],
            "g_in": [("proj", "chips")], "bias_reduce": [("proj", "share"), ("in", "pair")],
            "d_h": [("in", "chips")], "b_gate_sum": [("in", "share")]}

    def __init__(self, where, early_slots, gate_slots, late_slots):
        self.where = where
        self.slots = dict(early=early_slots, gate=gate_slots, late=late_slots[:3], mlp_w=late_slots[3:])
        self.stage = {g: dict(zip(("gather", "forward"), _gather_riders(s))) for g, s in self.slots.items()}
        self.groups, self.reduced, self.pending = {}, {}, []

    def early_weights(self):
        w_in3, small = self.slots["early"]
        conv_w = small[:, :3, :].transpose(1, 0, 2).reshape(3, N_SHARD * small.shape[2])
        return w_in3, conv_w

    def gate_weight(self):
        return self.slots["gate"][0]

    def late_weights(self):
        rows = lambda a: a.reshape(a.shape[0] * a.shape[1], a.shape[2])
        w_ap, w_cp, w_out = self.slots["late"]
        w_up3, w_down = self.slots["mlp_w"]
        return rows(w_ap), rows(w_cp), rows(w_out), w_up3, rows(w_down)

    def grads_ready(self, group, grads):
        names = list(grads)
        g4 = [g if g.ndim == 3 else g.reshape(N_SHARD, -1, g.shape[1]) for g in grads.values()]
        self.groups[group] = dict(names=names, g4=g4)

    def riders(self, at):
        self.pending = self.PLAN.get(at, [])
        out = []
        for group, stage in self.pending:
            if group in self.slots:
                out.append(self.stage[group][stage](self.slots[group]))
                continue
            st = self.groups[group]
            if stage == "pair":
                landing = [lax.empty((N_SHARD, g.shape[1] // 2, g.shape[2]), F32) for g in st["g4"]]
                out.append(_pair_exchange_rider(st["g4"], landing))
            elif stage == "chips":
                landing = [lax.empty((3,) + p.shape[1:], p.dtype) for p in st["partial"]]
                out.append(_chip_exchange_rider(st["partial"], landing))
            else:
                out.append(_pair_share_rider(st["halves"]))
        return out

    def done(self, at, carried):
        for (group, stage), arrays in zip(self.pending, carried):
            if group in self.slots:
                self.slots[group] = arrays
                continue
            st = self.groups[group]
            tag = lambda what, n: what + "_" + n
            if stage == "pair":
                st["got"] = arrays
                st["partial"] = [_pair_add(tag("pair_add", n), self.where, g, r)
                                 for n, g, r in zip(st["names"], st["g4"], arrays)]
            elif stage == "chips":
                st["halves"] = [_final_add(tag("final_add", n), self.where, g, r, a)
                                for n, g, r, a in zip(st["names"], st["g4"], st["got"], arrays)]
            else:
                self.reduced.update(zip(st["names"], arrays))


SMALL_ROWS = 32
N_DEV = 8


def _all_reduce_small(pack):
    def body(p_ref, o_ref, buf, send, recv):
        x, y, c = _place()
        buf[4 * x + 2 * y + c] = p_ref[...]
        copies, waits = [], []
        for k in range(1, N_DEV):
            px = 1 - x if k & 4 else x
            py = 1 - y if k & 2 else y
            pc = 1 - c if k & 1 else c
            copies.append(_remote(p_ref, buf.at[4 * x + 2 * y + c], send, recv, k - 1, (px, py, pc)))
            waits.append(_remote(p_ref, buf.at[4 * px + 2 * py + pc], send, recv, k - 1, (px, py, pc)))
        for cp in copies:
            cp.start()
        for cp in waits:
            cp.wait_recv()
        acc = buf[0]
        for d in range(1, N_DEV):
            acc = acc + buf[d]
        o_ref[...] = acc
        for cp in copies:
            cp.wait_send()

    return pl.pallas_call(
        body, name="all_reduce_small",
        out_shape=jax.ShapeDtypeStruct(pack.shape, F32),
        scratch_shapes=[pltpu.VMEM((N_DEV,) + pack.shape, F32),
                        pltpu.SemaphoreType.DMA((N_DEV - 1,)), pltpu.SemaphoreType.DMA((N_DEV - 1,))],
    )(pack)


def _adamw(name, w, g, m, v):
    c1 = 1.0 - ADAM_B1 ** ADAM_STEP
    c2 = 1.0 - ADAM_B2 ** ADAM_STEP

    def fn(t, f):
        wv, gv, mv, vv = t
        m2 = ADAM_B1 * mv + (1.0 - ADAM_B1) * gv
        v2 = ADAM_B2 * vv + (1.0 - ADAM_B2) * (gv * gv)
        delta = -ADAM_LR * ((m2 / c1) / (jnp.sqrt(v2 / c2) + ADAM_EPS) + ADAM_WD * wv)
        return [delta, m2, v2], []

    cols = w.shape[1]
    return _ew(name, fn, [w, g, m, v], [], [(cols, F32)] * 3, ts=min(w.shape[0], 256))


LOSS_ROW = 26


def _pack_small(norm1_g, norm2_g, conv_b, b_gate, conv_w, q_norm_g, k_norm_g, rel_bias, loss=None):
    pack = jnp.zeros((SMALL_ROWS, D_MODEL), F32)
    for r0, v in ((0, norm1_g), (1, norm2_g), (2, conv_b), (3, b_gate.reshape(2, D_MODEL)), (5, conv_w),
                  (8, q_norm_g), (9, k_norm_g), (10, rel_bias)) + (((LOSS_ROW, loss),) if loss is not None else ()):
        v = v.reshape(-1, v.shape[-1]).astype(F32)
        pack = pack.at[r0:r0 + v.shape[0], :v.shape[1]].set(v)
    return pack


def _unpack_small(pack, conv_cols):
    return dict(norm1_g=pack[0], norm2_g=pack[1], conv_b=pack[2], b_gate=pack[3:5].reshape(2 * D_MODEL),
                conv_w=pack[5:8, :conv_cols], q_norm_g=pack[8, :HEAD_DIM], k_norm_g=pack[9, :HEAD_DIM],
                rel_bias=pack[10:10 + N_HEADS, :N_REL])


BIG = ["w_in", "w_attn_proj", "w_conv_proj", "w_gate", "w_out", "w_up", "w_down"]
LATE = ["w_attn_proj", "w_conv_proj", "w_out", "w_up", "w_down"]
WEIGHTS = ["norm1_g", "w_in", "q_norm_g", "k_norm_g", "rel_bias", "conv_w", "conv_b", "w_attn_proj",
           "w_conv_proj", "w_gate", "b_gate", "w_out", "norm2_g", "w_up", "w_down"]


def kernel(x, norm1_g, w_in, q_norm_g, k_norm_g, rel_bias, conv_w, conv_b, w_attn_proj, w_conv_proj, w_gate, b_gate, w_out, norm2_g, w_up, w_down, loss_target, m_norm1_g, m_w_in, m_q_norm_g, m_k_norm_g, m_rel_bias, m_conv_w, m_conv_b, m_w_attn_proj, m_w_conv_proj, m_w_gate, m_b_gate, m_w_out, m_norm2_g, m_w_up, m_w_down, v_norm1_g, v_w_in, v_q_norm_g, v_k_norm_g, v_rel_bias, v_conv_w, v_conv_b, v_w_attn_proj, v_w_conv_proj, v_w_gate, v_b_gate, v_w_out, v_norm2_g, v_w_up, v_w_down):
    given = dict(locals())
    w = {n: given[n] for n in WEIGHTS}
    m = {n: given["m_" + n] for n in WEIGHTS}
    v = {n: given["v_" + n] for n in WEIGHTS}
    s_len = x.shape[1]
    shard = 2 * lax.axis_index("x") + lax.axis_index("y")
    where = jnp.stack([shard, lax.axis_index("c")]).astype(jnp.int32)
    conv_cols = conv_w.shape[1]

    small_in = lax.dynamic_update_slice(jnp.zeros((N_SHARD, 16, conv_cols), F32), conv_w[None], (shard, 0, 0))
    slot = {n: _cast_into_slot("cast_" + n, where, w[n]) for n in BIG}
    comm = _Exchange(where, [slot["w_in"], small_in], [slot["w_gate"]], [slot[n] for n in LATE])

    loss_part, grad_x, _, small = _local_grads(
        x.reshape(s_len, D_MODEL), loss_target.reshape(s_len, D_MODEL), norm1_g, q_norm_g, k_norm_g,
        rel_bias, conv_b, b_gate, norm2_g, comm)
    grad = dict(comm.reduced)

    loss_local = _finish_loss(loss_part)
    pack = _pack_small(small["norm1_g"], small["norm2_g"], small["conv_wb"][3], jnp.concatenate(small["b_gate"], axis=1),
                       small["conv_wb"][0:3], small["q_norm_g"], small["k_norm_g"], small["rel_bias"],
                       loss=loss_local[0:1, :])
    total = _all_reduce_small(pack)
    g_small = _unpack_small(total, D_MODEL)
    g_small["conv_w"] = lax.dynamic_slice(g_small["conv_w"], (0, shard * conv_cols), (3, conv_cols))
    grad.update(g_small)

    delta, new_m, new_v = {}, {}, {}
    for n in BIG:
        delta[n], new_m[n], new_v[n] = _adamw("adamw_" + n, w[n], grad[n], m[n], v[n])
    small_names = [n for n in WEIGHTS if n not in BIG]
    packs = [_pack_small(**{n: src[n] for n in small_names}) for src in (w, grad, m, v)]
    for out, packed in zip((delta, new_m, new_v), _adamw("adamw_small", *packs)):
        out.update({n: a.reshape(w[n].shape) for n, a in _unpack_small(packed, conv_cols).items()})

    outs = [total[LOSS_ROW, 0], grad_x.reshape(x.shape)]
    for group in (grad, delta, new_m, new_v):
        outs += [group[n].reshape(w[n].shape) for n in WEIGHTS]
    return tuple(outs)
```

```python
import jax
import jax.numpy as jnp
from jax import lax
from jax.experimental import pallas as pl
from jax.experimental.pallas import tpu as pltpu

F32 = jnp.float32
BF16 = jnp.bfloat16

D_MODEL = 1024
N_HEADS = 16
HEAD_DIM = 64
CHUNK = 64
N_PREV_CHUNKS = 8
MAX_REL = 256
D_FF = 4096
N_REL = 2 * MAX_REL + 1
REL_PAD = 640
EPS = 1e-6
NEG_INF = -1e30
QK_SCALE = HEAD_DIM ** -0.5

SUPER = 4 * CHUNK
BAND = SUPER + N_PREV_CHUNKS * CHUNK
SKEW_W = 1024
N_SHARD = 4
LANE = 128
MXU_DIM = 256
VMEM_LIMIT = 48 * 1024 * 1024

ADAM_LR = 0.001
ADAM_B1 = 0.9
ADAM_B2 = 0.999
ADAM_EPS = 1e-08
ADAM_WD = 0.01
ADAM_STEP = 10

MESH = pl.DeviceIdType.MESH
NN = (((1,), (0,)), ((), ()))
NT = (((1,), (1,)), ((), ()))
TN = (((0,), (0,)), ((), ()))


def _params(*sem):
    return pltpu.CompilerParams(dimension_semantics=sem or None, vmem_limit_bytes=VMEM_LIMIT)


HBM_SPEC = pl.BlockSpec(memory_space=pl.ANY)


def _in_hbm(a):
    return pltpu.with_memory_space_constraint(a, pltpu.HBM)


class _Rider:
    def __init__(self, sources, arrays, n_sem, start, finish):
        self.sources, self.arrays, self.n_sem, self.start, self.finish = sources, arrays, n_sem, start, finish


def _carry(riders, body, *, name, out_shape, grid=(), in_specs=None, out_specs=None, scratch_shapes=(),
           semantics=(), input_output_aliases=None):
    aliases = dict(input_output_aliases or {})
    if not riders:
        kw = {} if in_specs is None else dict(in_specs=in_specs, out_specs=out_specs)
        call = pl.pallas_call(body, name=name, grid=grid, out_shape=out_shape, scratch_shapes=scratch_shapes,
                              input_output_aliases=aliases, compiler_params=_params(*semantics), **kw)
        return call if in_specs is None else (lambda *args: call(*[_in_hbm(a) for a in args]))
    single = not isinstance(out_shape, (list, tuple))
    shapes = [out_shape] if single else list(out_shape)
    n_out, n_scr = len(shapes), len(scratch_shapes)
    srcs = [_in_hbm(a) for r in riders for a in r.sources]
    arrs = [_in_hbm(a) for r in riders for a in r.arrays]
    vmem = pl.BlockSpec(memory_space=pltpu.VMEM)

    def run(*args):
        n_in = len(args)

        def wrapped(*refs):
            pos = n_in
            src_refs = refs[pos:pos + len(srcs)]
            pos += len(srcs) + len(arrs)
            outs = refs[pos:pos + n_out]
            pos += n_out
            arr_refs = refs[pos:pos + len(arrs)]
            pos += len(arrs)
            scratch = refs[pos:pos + n_scr]
            sems = refs[pos + n_scr:]
            first, last = True, True
            for d, size in enumerate(grid):
                first = jnp.logical_and(first, pl.program_id(d) == 0)
                last = jnp.logical_and(last, pl.program_id(d) == size - 1)

            def each(method):
                s0 = a0 = 0
                for k, r in enumerate(riders):
                    getattr(r, method)(src_refs[s0:s0 + len(r.sources)], arr_refs[a0:a0 + len(r.arrays)],
                                       sems[2 * k], sems[2 * k + 1])
                    s0, a0 = s0 + len(r.sources), a0 + len(r.arrays)

            pl.when(first)(lambda: each("start"))
            body(*refs[:n_in], *outs, *scratch)
            pl.when(last)(lambda: each("finish"))

        ins = [vmem] * n_in if in_specs is None else list(in_specs)
        if out_specs is None:
            o_specs = [vmem] * n_out
        else:
            o_specs = [out_specs] if single else list(out_specs)
        for k in range(len(arrs)):
            aliases[n_in + len(srcs) + k] = n_out + k
        res = pl.pallas_call(
            wrapped, name=name, grid=grid,
            in_specs=ins + [HBM_SPEC] * (len(srcs) + len(arrs)),
            out_specs=o_specs + [HBM_SPEC] * len(arrs),
            out_shape=shapes + [jax.ShapeDtypeStruct(a.shape, a.dtype) for a in arrs],
            scratch_shapes=list(scratch_shapes) + [pltpu.SemaphoreType.DMA((r.n_sem,)) for r in riders for _ in range(2)],
            input_output_aliases=aliases,
            compiler_params=_params(*["arbitrary"] * len(grid)),
        )(*(args if in_specs is None else [_in_hbm(a) for a in args]), *srcs, *arrs)
        core, rest = res[:n_out], list(res[n_out:])
        carried, a0 = [], 0
        for r in riders:
            carried.append(rest[a0:a0 + len(r.arrays)])
            a0 += len(r.arrays)
        return (core[0] if single else core), carried

    return run


def _mm(name, dims, a, a_spec, b, b_spec, grid, tile, outs, epilogue=None, extras=(), riders=()):
    nk, ne, no = grid[2], len(extras), len(outs)

    def body(a_ref, b_ref, *refs):
        e_refs, o_refs = refs[:ne], refs[ne:ne + no]
        part = lax.dot_general(a_ref[...], b_ref[...], dims, preferred_element_type=F32)

        def finish(acc):
            if epilogue is None:
                o_refs[0][...] = acc.astype(o_refs[0].dtype)
            else:
                epilogue(acc, [r[...] for r in e_refs], o_refs)

        if nk == 1:
            finish(part)
        else:
            acc_ref = refs[ne + no]
            k = pl.program_id(2)

            @pl.when(k == 0)
            def _():
                acc_ref[...] = part

            @pl.when(k > 0)
            def _():
                acc_ref[...] += part

            @pl.when(k == nk - 1)
            def _():
                finish(acc_ref[...])

    res = _carry(
        riders, body, name=name, grid=grid,
        in_specs=[a_spec, b_spec] + [s for _, s in extras],
        out_specs=[s for _, s in outs],
        out_shape=[s for s, _ in outs],
        scratch_shapes=[pltpu.VMEM(tile, F32)] if nk > 1 else [],
        semantics=("parallel", "parallel", "arbitrary"),
    )(a, b, *[e for e, _ in extras])
    res, carried = res if riders else (res, None)
    res = res[0] if no == 1 else res
    return (res, carried) if riders else res


def _tile_spec(tm, tn, col0=0):
    return pl.BlockSpec((tm, tn), lambda i, j, k: (i, j + col0))


def _out2d(m, n, dtype, tm, tn):
    return (jax.ShapeDtypeStruct((m, n), dtype), _tile_spec(tm, tn))


def _mm_fwd(name, a, w, tm, tn, tk, outs=None, epilogue=None, extras=(), col0=0, ncols=None, riders=()):
    m, kdim = a.shape
    if w.ndim == 3:
        per = w.shape[2] // tn
        n = ncols or N_SHARD * w.shape[2]
        w_spec = pl.BlockSpec((None, tk, tn), lambda i, j, k: ((j + col0) // per, k, (j + col0) % per))
    else:
        n = ncols or w.shape[1]
        w_spec = pl.BlockSpec((tk, tn), lambda i, j, k: (k, j + col0))
    if outs is None:
        outs = [_out2d(m, n, F32, tm, tn)]
    return _mm(name, NN, a, pl.BlockSpec((tm, tk), lambda i, j, k: (i, k)), w, w_spec,
               (m // tm, n // tn, kdim // tk), (tm, tn), outs, epilogue, extras, riders)


def _mm_bwd_x(name, g, g_spec, w, tm, tj, tc, m, n_contract, outs=None, epilogue=None, extras=(), riders=()):
    if w.ndim == 3:
        per = w.shape[2] // tc
        kdim = w.shape[1]
        w_spec = pl.BlockSpec((None, tj, tc), lambda i, j, n: (n // per, j, n % per))
    else:
        kdim = w.shape[0]
        w_spec = pl.BlockSpec((tj, tc), lambda i, j, n: (j, n))
    if outs is None:
        outs = [_out2d(m, kdim, F32, tm, tj)]
    return _mm(name, NT, g, g_spec, w, w_spec, (m // tm, kdim // tj, n_contract // tc),
               (tm, tj), outs, epilogue, extras, riders)


def _mm_bwd_w(name, a, g, g_spec, n, tk, tn, tm, sharded, riders=()):
    m, kdim = a.shape
    if sharded:
        per = (n // N_SHARD) // tn
        out = (jax.ShapeDtypeStruct((N_SHARD, kdim, n // N_SHARD), F32),
               pl.BlockSpec((None, tk, tn), lambda i, j, mm: (j // per, i, j % per)))
    else:
        out = (jax.ShapeDtypeStruct((kdim, n), F32), pl.BlockSpec((tk, tn), lambda i, j, mm: (i, j)))
    return _mm(name, TN, a, pl.BlockSpec((tm, tk), lambda i, j, mm: (mm, i)), g, g_spec,
               (kdim // tk, n // tn, m // tm), (tk, tn), [out], riders=riders)


def _ew(name, fn, tiles, fulls, outs, sums=(), ts=512, riders=()):
    tiles = [t if isinstance(t, tuple) else (t, pl.BlockSpec((ts, t.shape[1]), lambda i: (i, 0)))
             for t in tiles]
    s_rows = tiles[0][0].shape[-2]
    nt, nf, no = len(tiles), len(fulls), len(outs)

    def body(*refs):
        t_vals = [r[...] for r in refs[:nt]]
        f_vals = [r[...] for r in refs[nt:nt + nf]]
        o_refs, s_refs = refs[nt + nf:nt + nf + no], refs[nt + nf + no:]
        o_vals, s_vals = fn(t_vals, f_vals)
        for r, v in zip(o_refs, o_vals):
            r[...] = v.astype(r.dtype)
        for r, v in zip(s_refs, s_vals):
            part = jnp.sum(v, axis=0, keepdims=True)

            @pl.when(pl.program_id(0) == 0)
            def _():
                r[...] = part

            @pl.when(pl.program_id(0) > 0)
            def _():
                r[...] += part

    full_specs = [pl.BlockSpec(f.shape, lambda i, nd=f.ndim: (0,) * nd) for f in fulls]
    return _carry(
        riders, body, name=name, grid=(s_rows // ts,),
        in_specs=[s for _, s in tiles] + full_specs,
        out_specs=[pl.BlockSpec((ts, c), lambda i: (i, 0)) for c, _ in outs]
        + [pl.BlockSpec((1, c), lambda i: (0, 0)) for c in sums],
        out_shape=[jax.ShapeDtypeStruct((s_rows, c), dt) for c, dt in outs]
        + [jax.ShapeDtypeStruct((1, c), F32) for c in sums],
        semantics=("arbitrary",),
    )(*[t for t, _ in tiles], *fulls)


def _rms_fwd(name, x, g, riders=()):
    def fn(t, f):
        xv = t[0]
        r = lax.rsqrt(jnp.mean(xv * xv, axis=-1, keepdims=True) + EPS)
        return [xv * r * f[0]], []
    out = _ew(name, fn, [x], [g], [(x.shape[1], BF16)], riders=riders)
    return (out[0][0], out[1]) if riders else out[0]


def _split3(x):
    x1 = x.astype(BF16)
    r1 = x - x1.astype(F32)
    x2 = r1.astype(BF16)
    x3 = (r1 - x2.astype(F32)).astype(BF16)
    return x1, x2, x3


def _rel_class(cp):
    far = (cp < MAX_REL) | (cp > BAND)
    return jnp.where(far, 2 * MAX_REL, BAND - cp)


def _skew_rows(x, sign):
    row = lax.broadcasted_iota(jnp.int32, x.shape, 0)
    for b in range(CHUNK.bit_length() - 1):
        shift = (1 << b) if sign > 0 else SKEW_W - (1 << b)
        x = jnp.where((row >> b) & 1 == 1, pltpu.roll(x, shift, 1), x)
    return x


def _roll_lanes(x, shift):
    return x if shift % SKEW_W == 0 else pltpu.roll(x, shift % SKEW_W, 1)


N_START = 3


def _bias_expand(rel_bias, riders=()):
    rel = jnp.pad(rel_bias, ((0, 0), (0, REL_PAD - N_REL))).reshape(N_HEADS, 1, REL_PAD)

    def body(rel_ref, o_ref):
        cls = lax.broadcasted_iota(jnp.int32, (REL_PAD, SKEW_W), 0)
        cp = lax.broadcasted_iota(jnp.int32, (REL_PAD, SKEW_W), 1)
        onehot = (cls == _rel_class(cp)).astype(BF16)
        rel8 = jnp.broadcast_to(rel_ref[...], (8, REL_PAD))
        trow = sum(jnp.dot(p, onehot, preferred_element_type=F32) for p in _split3(rel8))[0:1]
        first = _skew_rows(jnp.broadcast_to(trow, (CHUNK, SKEW_W)), +1)
        full = jnp.concatenate([_roll_lanes(first, CHUNK * g) for g in range(SUPER // CHUNK)], axis=0)[:, :BAND]
        qc = lax.broadcasted_iota(jnp.int32, (SUPER, BAND), 0) // CHUNK
        kc = lax.broadcasted_iota(jnp.int32, (SUPER, BAND), 1) // CHUNK
        on_band = (kc >= qc) & (kc <= qc + N_PREV_CHUNKS)
        table = jnp.where(on_band, full, NEG_INF).T
        key = lax.broadcasted_iota(jnp.int32, (BAND, SUPER), 0)
        for t in range(N_START):
            o_ref[t] = jnp.where(key < (N_START - 1 - t) * SUPER, NEG_INF, table)

    return _carry(
        riders, body, name="bias_expand", grid=(N_HEADS,),
        in_specs=[pl.BlockSpec((None, 1, REL_PAD), lambda h: (h, 0, 0))],
        out_specs=pl.BlockSpec((N_START, None, BAND, SUPER), lambda h: (0, h, 0, 0)),
        out_shape=jax.ShapeDtypeStruct((N_START, N_HEADS, BAND, SUPER), F32),
        semantics=("arbitrary",),
    )(rel)


def _bias_reduce(dbias, riders=()):
    def body(d_ref, o_ref):
        x = jnp.concatenate([d_ref[...].T, jnp.zeros((SUPER, SKEW_W - BAND), F32)], axis=1)
        folded = sum(_roll_lanes(x[CHUNK * g:CHUNK * (g + 1)], -CHUNK * g) for g in range(SUPER // CHUNK))
        diag = jnp.sum(_skew_rows(folded, -1), axis=0, keepdims=True)
        cp = lax.broadcasted_iota(jnp.int32, (SKEW_W, REL_PAD), 0)
        cls = lax.broadcasted_iota(jnp.int32, (SKEW_W, REL_PAD), 1)
        onehot = (cls == _rel_class(cp)).astype(BF16)
        diag8 = jnp.broadcast_to(diag, (8, SKEW_W))
        o_ref[...] = sum(jnp.dot(p, onehot, preferred_element_type=F32) for p in _split3(diag8))[0:1]

    out = _carry(
        riders, body, name="bias_reduce", grid=(N_HEADS,),
        in_specs=[pl.BlockSpec((None, BAND, SUPER), lambda h: (h, 0, 0))],
        out_specs=pl.BlockSpec((None, 1, REL_PAD), lambda h: (h, 0, 0)),
        out_shape=jax.ShapeDtypeStruct((N_HEADS, 1, REL_PAD), F32),
        semantics=("arbitrary",),
    )(dbias)
    out, carried = out if riders else (out, None)
    out = out.reshape(N_HEADS, REL_PAD)[:, :N_REL]
    return (out, carried) if riders else out


HEADS_PER_STEP = 8
HEAD_COLS = HEADS_PER_STEP * HEAD_DIM
N_HEAD_GROUPS = N_HEADS // HEADS_PER_STEP


def _scores_t(qs, kn, bias_t):
    return jnp.concatenate([lax.dot_general(k, qs, NT, preferred_element_type=F32) for k in kn], axis=0) + bias_t


def _bias_spec():
    return pl.BlockSpec((None, HEADS_PER_STEP, BAND, SUPER), lambda hg, i: (jnp.minimum(i, N_START - 1), hg, 0, 0))


def _band_specs(nb, col0, clamp_hi):
    def spec(d):
        def index(hg, i):
            blk = jnp.maximum(i - d, 0)
            if clamp_hi:
                blk = jnp.minimum(blk, nb - 1)
            return (blk, col0 + hg)
        return pl.BlockSpec((SUPER, HEAD_COLS), index)
    return [spec(2), spec(1), spec(0)]


def _head_sums(y):
    same_head = (lax.broadcasted_iota(jnp.int32, (MXU_DIM, MXU_DIM), 0) // HEAD_DIM
                 == lax.broadcasted_iota(jnp.int32, (MXU_DIM, MXU_DIM), 1) // HEAD_DIM).astype(BF16)
    sums = []
    for c0 in range(0, y.shape[1], MXU_DIM):
        chunk = y[:, c0:c0 + MXU_DIM]
        hi = chunk.astype(BF16)
        lo = (chunk - hi.astype(F32)).astype(BF16)
        sums.append(jnp.dot(hi, same_head, preferred_element_type=F32)
                    + jnp.dot(lo, same_head, preferred_element_type=F32))
    return jnp.concatenate(sums, axis=1)


def _head_unit(x):
    r = lax.rsqrt(_head_sums(x * x) * (1.0 / HEAD_DIM) + EPS)
    return x * r, r


def _head(hh):
    return slice(HEAD_DIM * hh, HEAD_DIM * (hh + 1))


def _key_block(j):
    return slice(SUPER * j, SUPER * (j + 1))


LSE_ROWS = 8


def _attn_fwd(qkn, v, bias, riders=()):
    s_len = qkn.shape[0]
    nb = s_len // SUPER

    def body(q_ref, k0, k1, k2, v0, v1, v2, b_ref, o_ref, lse_ref):
        outs = []
        v_t = [v0[...].T, v1[...].T, v2[...].T]

        def probabilities(hh):
            sl = _head(hh)
            s = _scores_t(q_ref[:, sl], [k0[:, sl], k1[:, sl], k2[:, sl]], b_ref[hh])
            m = jnp.max(s, axis=0, keepdims=True)
            e = jnp.exp(s - m)
            l = jnp.sum(e, axis=0, keepdims=True)
            lse_ref[hh:hh + 1, :] = m + jnp.log(l)
            return e.astype(BF16), 1.0 / l, sl

        def weighted_values(e, inv_l, sl):
            outs.append(sum(jnp.dot(v_t[j][sl, :], e[_key_block(j), :], preferred_element_type=F32)
                            for j in range(3)) * inv_l)

        ready = probabilities(0)
        for hh in range(1, HEADS_PER_STEP):
            following = probabilities(hh)
            weighted_values(*ready)
            ready = following
        weighted_values(*ready)
        o_ref[...] = jnp.concatenate(outs, axis=0).T.astype(o_ref.dtype)

    return _carry(
        riders, body, name="attn_fwd", grid=(N_HEAD_GROUPS, nb),
        in_specs=[pl.BlockSpec((SUPER, HEAD_COLS), lambda hg, i: (i, hg))]
        + _band_specs(nb, N_HEAD_GROUPS, False) + _band_specs(nb, 0, False) + [_bias_spec()],
        out_specs=[pl.BlockSpec((SUPER, HEAD_COLS), lambda hg, i: (i, hg)),
                   pl.BlockSpec((None, LSE_ROWS, SUPER), lambda hg, i: (hg, 0, i))],
        out_shape=[jax.ShapeDtypeStruct((s_len, D_MODEL), BF16),
                   jax.ShapeDtypeStruct((N_HEAD_GROUPS, LSE_ROWS, s_len), F32)],
        semantics=("parallel", "arbitrary"),
    )(qkn, qkn, qkn, qkn, v, v, v, bias)


def _attn_bwd(qkn, v, out, d_out, bias, lse, riders=()):
    s_len = qkn.shape[0]
    nb = s_len // SUPER

    def body(q_ref, k0, k1, k2, v0, v1, v2, o_ref, do_ref, b_ref, lse_ref, dp_ref, db_ref, aq_ref, ak_ref, av_ref):
        i = pl.program_id(1)

        @pl.when(i == 0)
        def _():
            aq_ref[...] = jnp.zeros_like(aq_ref)
            ak_ref[...] = jnp.zeros_like(ak_ref)
            av_ref[...] = jnp.zeros_like(av_ref)
            db_ref[...] = jnp.zeros_like(db_ref)

        @pl.when(i < nb)
        def _():
            dq, dk, dv = [], [[], [], []], [[], [], []]
            ones = jnp.ones((8, HEAD_DIM), BF16)
            k_t = [k0[...].T, k1[...].T, k2[...].T]

            def softmax_grad(hh):
                sl = _head(hh)
                qs, do = q_ref[:, sl], do_ref[:, sl]
                kn = [k0[:, sl], k1[:, sl], k2[:, sl]]
                prod = do.astype(F32) * o_ref[:, sl].astype(F32)
                hi = prod.astype(BF16)
                lo = (prod - hi.astype(F32)).astype(BF16)
                delta = (lax.dot_general(ones, hi, NT, preferred_element_type=F32)
                         + lax.dot_general(ones, lo, NT, preferred_element_type=F32))[0:1]
                lse_row = lse_ref[hh:hh + 1, :]
                pb, dsb = [], []
                for j, (kj, vj) in enumerate(zip(kn, (v0, v1, v2))):
                    rows = _key_block(j)
                    p = jnp.exp(lax.dot_general(kj, qs, NT, preferred_element_type=F32) + b_ref[hh, rows, :] - lse_row)
                    ds = p * (lax.dot_general(vj[:, sl], do, NT, preferred_element_type=F32) - delta)
                    db_ref[hh, rows, :] += ds
                    pb.append(p.astype(BF16))
                    dsb.append(ds.astype(BF16))
                return pb, dsb, qs, do, sl

            def operand_grads(pb, dsb, qs, do, sl):
                dq.append(sum(jnp.dot(k_t[j][sl, :], dsb[j], preferred_element_type=F32) for j in range(3)))
                for j in range(3):
                    dv[j].append(jnp.dot(pb[j], do, preferred_element_type=F32))
                    dk[j].append(jnp.dot(dsb[j], qs, preferred_element_type=F32))

            ready = softmax_grad(0)
            for hh in range(1, HEADS_PER_STEP):
                following = softmax_grad(hh)
                operand_grads(*ready)
                ready = following
            operand_grads(*ready)
            aq_ref[i % 3] = jnp.concatenate(dq, axis=0).T
            for j in range(3):
                slot = (i + 1 + j) % 3
                if j < 2:
                    ak_ref[slot] += jnp.concatenate(dk[j], axis=1)
                    av_ref[slot] += jnp.concatenate(dv[j], axis=1)
                else:
                    ak_ref[slot] = jnp.concatenate(dk[j], axis=1)
                    av_ref[slot] = jnp.concatenate(dv[j], axis=1)

        slot = (i + 1) % 3
        dp_ref[0] = aq_ref[slot].astype(dp_ref.dtype)
        dp_ref[1] = ak_ref[slot].astype(dp_ref.dtype)
        dp_ref[2] = av_ref[slot].astype(dp_ref.dtype)

    def qrow(hg, i):
        return (jnp.minimum(i, nb - 1), hg)

    return _carry(
        riders, body, name="attn_bwd", grid=(N_HEAD_GROUPS, nb + 2),
        in_specs=[pl.BlockSpec((SUPER, HEAD_COLS), qrow)]
        + _band_specs(nb, N_HEAD_GROUPS, True) + _band_specs(nb, 0, True)
        + [pl.BlockSpec((SUPER, HEAD_COLS), qrow), pl.BlockSpec((SUPER, HEAD_COLS), qrow), _bias_spec(),
           pl.BlockSpec((None, LSE_ROWS, SUPER), lambda hg, i: (hg, 0, jnp.minimum(i, nb - 1)))],
        out_specs=[pl.BlockSpec((3, SUPER, HEAD_COLS), lambda hg, i: (0, jnp.maximum(i - 2, 0), hg)),
                   pl.BlockSpec((HEADS_PER_STEP, BAND, SUPER), lambda hg, i: (hg, 0, 0))],
        out_shape=[jax.ShapeDtypeStruct((6, s_len, D_MODEL), BF16),
                   jax.ShapeDtypeStruct((N_HEADS, BAND, SUPER), F32)],
        scratch_shapes=[pltpu.VMEM((3, SUPER, HEAD_COLS), F32)] * 3,
        semantics=("parallel", "arbitrary"),
    )(qkn, qkn, qkn, qkn, v, v, v, out, d_out, bias, lse)


def _qk_norm_bwd(dproj6, qk_raw, gq, gk):
    s_len = qk_raw.shape[0]
    ts = min(1024, s_len)

    nsteps = s_len // ts
    half = D_MODEL // 2

    def body(d_ref, raw_ref, gq_ref, gk_ref, o_ref, dgq_ref, dgk_ref, acc_ref):
        step = pl.program_id(0)

        @pl.when(step == 0)
        def _():
            acc_ref[...] = jnp.zeros_like(acc_ref)

        for piece, (g_ref, scale) in enumerate(((gq_ref, QK_SCALE), (gk_ref, 1.0))):
            for c0 in (0, half):
                xhat, r = _head_unit(raw_ref[:, piece * D_MODEL + c0:piece * D_MODEL + c0 + half].astype(F32))
                dn = d_ref[piece, :, c0:c0 + half].astype(F32) * scale
                u = dn * g_ref[...]
                dx = r * (u - xhat * (_head_sums(u * xhat) * (1.0 / HEAD_DIM)))
                o_ref[piece, :, c0:c0 + half] = dx.astype(o_ref.dtype)
                acc_ref[piece:piece + 1, c0:c0 + half] += jnp.sum(dn * xhat, axis=0, keepdims=True)

        @pl.when(step == nsteps - 1)
        def _():
            lane = lax.broadcasted_iota(jnp.int32, (D_MODEL, LANE), 0) % HEAD_DIM
            fold = (lane == lax.broadcasted_iota(jnp.int32, (D_MODEL, LANE), 1)).astype(BF16)
            tot = sum(jnp.dot(p, fold, preferred_element_type=F32) for p in _split3(acc_ref[...]))
            dgq_ref[...] = tot[0:1, :HEAD_DIM]
            dgk_ref[...] = tot[1:2, :HEAD_DIM]

    gain = pl.BlockSpec((1, half), lambda i: (0, 0))
    small = pl.BlockSpec((1, HEAD_DIM), lambda i: (0, 0))
    per_head = lambda g: jnp.tile(g, (1, half // HEAD_DIM))
    return pl.pallas_call(
        body, name="qk_norm_bwd", grid=(nsteps,),
        in_specs=[pl.BlockSpec((2, ts, D_MODEL), lambda i: (0, i, 0)),
                  pl.BlockSpec((ts, 2 * D_MODEL), lambda i: (i, 0)), gain, gain],
        out_specs=[pl.BlockSpec((2, ts, D_MODEL), lambda i: (0, i, 0)), small, small],
        out_shape=[jax.ShapeDtypeStruct(dproj6.shape, dproj6.dtype),
                   jax.ShapeDtypeStruct((1, HEAD_DIM), F32), jax.ShapeDtypeStruct((1, HEAD_DIM), F32)],
        scratch_shapes=[pltpu.VMEM((8, D_MODEL), F32)],
        input_output_aliases={0: 0},
        compiler_params=_params("arbitrary"),
    )(_in_hbm(dproj6), _in_hbm(qk_raw), per_head(gq), per_head(gk))


CONV_ROWS = 512
HALO = 16


def _rows_with_halo(ref, r0, n, front, s_len):
    zeros = jnp.zeros((HALO, ref.shape[1]), F32)
    if front:
        return (jnp.concatenate([zeros, ref[0:n, :].astype(F32)], axis=0) if r0 == 0
                else ref[r0 - HALO:r0 + n, :].astype(F32))
    return (jnp.concatenate([ref[r0:r0 + n, :].astype(F32), zeros], axis=0) if r0 + n == s_len
            else ref[r0:r0 + n + HALO, :].astype(F32))


def _earlier(ext, k):
    return pltpu.roll(ext, k, 0)[HALO:]


def _later(ext, k):
    n = ext.shape[0]
    return pltpu.roll(ext, n - k, 0)[:n - HALO]


def _conv_cols(col0):
    return lambda s_len: pl.BlockSpec((s_len, LANE), lambda j: (0, col0 + j))


def _conv_fwd(proj, conv_w, conv_b, riders=()):
    s_len = proj.shape[0]

    def body(bg_ref, cg_ref, xc_ref, w_ref, b_ref, o_ref):
        w = [w_ref[t:t + 1, :] for t in range(3)]
        for r0 in range(0, s_len, CONV_ROWS):
            u = _rows_with_halo(cg_ref, r0, CONV_ROWS, True, s_len) * \
                _rows_with_halo(xc_ref, r0, CONV_ROWS, True, s_len)
            conv = b_ref[...] + w[0] * _earlier(u, 2) + w[1] * _earlier(u, 1) + w[2] * u[HALO:]
            o_ref[r0:r0 + CONV_ROWS, :] = (bg_ref[r0:r0 + CONV_ROWS, :].astype(F32) * conv).astype(o_ref.dtype)

    return _carry(
        riders, body, name="conv_fwd", grid=(D_MODEL // LANE,),
        in_specs=[_conv_cols(0)(s_len), _conv_cols(8)(s_len), _conv_cols(16)(s_len),
                  pl.BlockSpec((3, LANE), lambda j: (0, j)), pl.BlockSpec((1, LANE), lambda j: (0, j))],
        out_specs=pl.BlockSpec((s_len, LANE), lambda j: (0, j)),
        out_shape=jax.ShapeDtypeStruct((s_len, D_MODEL), BF16),
        semantics=("parallel",),
    )(proj, proj, proj, conv_w, conv_b)


def _conv_bwd(dproj6, dy, proj, conv_w, conv_b, riders=()):
    s_len = proj.shape[0]

    def body(dy_ref, bg_ref, cg_ref, xc_ref, w_ref, b_ref, _, dp_ref, dw_ref):
        w = [w_ref[t:t + 1, :] for t in range(3)]
        acc = [jnp.zeros((1, LANE), F32) for _ in range(4)]
        for r0 in range(0, s_len, CONV_ROWS):
            rows = slice(r0, r0 + CONV_ROWS)
            u = _rows_with_halo(cg_ref, r0, CONV_ROWS, True, s_len) * \
                _rows_with_halo(xc_ref, r0, CONV_ROWS, True, s_len)
            u2, u1, u0 = _earlier(u, 2), _earlier(u, 1), u[HALO:]
            conv = b_ref[...] + w[0] * u2 + w[1] * u1 + w[2] * u0
            dp_ref[0, rows, :] = (dy_ref[rows, :].astype(F32) * conv).astype(dp_ref.dtype)
            dconv_ext = _rows_with_halo(dy_ref, r0, CONV_ROWS, False, s_len) * \
                _rows_with_halo(bg_ref, r0, CONV_ROWS, False, s_len)
            dconv = dconv_ext[:CONV_ROWS]
            for t, term in enumerate([dconv * u2, dconv * u1, dconv * u0, dconv]):
                acc[t] = acc[t] + jnp.sum(term, axis=0, keepdims=True)
            du = w[2] * dconv + w[1] * _later(dconv_ext, 1) + w[0] * _later(dconv_ext, 2)
            dp_ref[1, rows, :] = (du * xc_ref[rows, :].astype(F32)).astype(dp_ref.dtype)
            dp_ref[2, rows, :] = (du * cg_ref[rows, :].astype(F32)).astype(dp_ref.dtype)
        dw_ref[...] = jnp.zeros_like(dw_ref)
        for t in range(4):
            dw_ref[t:t + 1, :] = acc[t]

    return _carry(
        riders, body, name="conv_bwd", grid=(D_MODEL // LANE,),
        in_specs=[pl.BlockSpec((s_len, LANE), lambda j: (0, j)),
                  _conv_cols(0)(s_len), _conv_cols(8)(s_len), _conv_cols(16)(s_len),
                  pl.BlockSpec((3, LANE), lambda j: (0, j)), pl.BlockSpec((1, LANE), lambda j: (0, j)),
                  pl.BlockSpec(memory_space=pl.ANY)],
        out_specs=[pl.BlockSpec((3, s_len, LANE), lambda j: (1, 0, j)),
                   pl.BlockSpec((8, LANE), lambda j: (0, j))],
        out_shape=[jax.ShapeDtypeStruct(dproj6.shape, dproj6.dtype),
                   jax.ShapeDtypeStruct((8, D_MODEL), F32)],
        input_output_aliases={6: 0},
        semantics=("parallel",),
    )(dy, proj, proj, proj, conv_w, conv_b, dproj6)


def _d_norm_input(name, terms, x, g, dres, out_dtype, riders=()):
    s_len, kdim = terms[0][0].shape[1], terms[0][1].shape[1]
    tm, chunk = 512, 512
    n = len(terms)

    def body(*refs):
        x_ref, g_ref, r_ref, o_ref, dg_ref = refs[2 * n:]
        dh = jnp.zeros((tm, kdim), F32)
        for p_ref, w_ref in zip(refs[:n], refs[n:2 * n]):
            n_piece, _, width = p_ref.shape
            per_shard = w_ref.shape[2]
            for c0 in range(0, n_piece * width, chunk):
                dh = dh + lax.dot_general(p_ref[c0 // width, :, c0 % width:c0 % width + chunk],
                                          w_ref[c0 // per_shard, :, c0 % per_shard:c0 % per_shard + chunk],
                                          NT, preferred_element_type=F32)
        xv = x_ref[...]
        r = lax.rsqrt(jnp.mean(xv * xv, axis=-1, keepdims=True) + EPS)
        xhat = xv * r
        u = dh * g_ref[...]
        dx = r * (u - xhat * jnp.mean(u * xhat, axis=-1, keepdims=True)) + r_ref[...].astype(F32)
        o_ref[...] = dx.astype(o_ref.dtype)
        part = jnp.sum(dh * xhat, axis=0, keepdims=True)

        @pl.when(pl.program_id(0) == 0)
        def _():
            dg_ref[...] = part

        @pl.when(pl.program_id(0) > 0)
        def _():
            dg_ref[...] += part

    tile = pl.BlockSpec((tm, kdim), lambda i: (i, 0))
    vec = pl.BlockSpec((1, kdim), lambda i: (0, 0))
    return _carry(
        riders, body, name=name, grid=(s_len // tm,),
        in_specs=[pl.BlockSpec((p.shape[0], tm, p.shape[2]), lambda i: (0, i, 0)) for p, _ in terms]
        + [pl.BlockSpec(w.shape, lambda i: (0, 0, 0), pipeline_mode=pl.Buffered(1)) for _, w in terms]
        + [tile, vec, tile],
        out_specs=[tile, vec],
        out_shape=[jax.ShapeDtypeStruct((s_len, kdim), out_dtype), jax.ShapeDtypeStruct((1, kdim), F32)],
        semantics=("arbitrary",),
    )(*[p for p, _ in terms], *[w for _, w in terms], x, g, dres)


def _local_grads(x, target, norm1_g, q_norm_g, k_norm_g, rel_bias, conv_b, b_gate, norm2_g, comm):
    s_len = x.shape[0]
    tm = min(1024, s_len)
    tb = min(2048, s_len)
    row = lambda v: v.reshape(1, -1)

    def carrying(at, fn, *args, **kw):
        riders = comm.riders(at)
        out = fn(*args, riders=riders, **kw)
        if riders:
            out, carried = out
            comm.done(at, carried)
        return out

    bias = carrying("bias_expand", _bias_expand, rel_bias)
    h = carrying("norm1", _rms_fwd, "norm1", x, row(norm1_g))
    w_in3, conv_w = comm.early_weights()
    gq, gk = row(q_norm_g), row(k_norm_g)

    def qk_epi(acc, e, o):
        o[0][...] = acc.astype(BF16)
        gain = jnp.where(pl.program_id(1) < 2, e[0] * QK_SCALE, e[1])
        o[1][...] = (_head_unit(acc)[0] * gain).astype(BF16)
    conv_in = carrying("proj_conv", _mm_fwd, "proj_conv", h, w_in3, tb, 1536, D_MODEL, col0=2, ncols=3 * D_MODEL,
                       outs=[_out2d(s_len, 3 * D_MODEL, BF16, tb, 1536)])
    small = pl.BlockSpec((1, 512), lambda i, j, k: (0, 0))
    per_head = lambda g: jnp.tile(g, (1, 512 // HEAD_DIM))
    qk_raw, qkn = carrying("proj_qk", _mm_fwd, "proj_qk", h, w_in3, tm, 512, D_MODEL, ncols=2 * D_MODEL,
                           epilogue=qk_epi, outs=[_out2d(s_len, 2 * D_MODEL, BF16, tm, 512)] * 2,
                           extras=[(per_head(gq), small), (per_head(gk), small)])
    v = carrying("proj_v", _mm_fwd, "proj_v", h, w_in3, tb, 512, D_MODEL, col0=4, ncols=D_MODEL,
                 outs=[_out2d(s_len, D_MODEL, BF16, tb, 512)])
    w_gate3 = comm.gate_weight()

    def gate_epi(acc, e, o):
        o[0][...] = jax.nn.sigmoid(acc + e[0]).astype(BF16)
    gates = carrying("gates", _mm_fwd, "gates", h, w_gate3, tb, 512, D_MODEL, epilogue=gate_epi,
                     outs=[_out2d(s_len, 2 * D_MODEL, BF16, tb, 512)],
                     extras=[(row(b_gate), pl.BlockSpec((1, 512), lambda i, j, k: (0, j)))])

    attn, lse = carrying("attn_fwd", _attn_fwd, qkn, v, bias)
    yconv = carrying("conv_fwd", _conv_fwd, conv_in, conv_w, row(conv_b))
    w_ap, w_cp, w_out, w_up3, w_down = comm.late_weights()
    tw = 1024
    ya = _mm_fwd("attn_proj", attn, w_ap, tm, tw, D_MODEL, outs=[_out2d(s_len, D_MODEL, BF16, tm, tw)])

    def merge_epi(acc, e, o):
        ya_v, ga, gc = [t.astype(F32) for t in e]
        o[0][...] = acc.astype(BF16)
        o[1][...] = (ga * ya_v + gc * acc).astype(BF16)
    gate_a, gate_c = (gates, _tile_spec(tm, tw, 0)), (gates, _tile_spec(tm, tw, 1))
    yc, merged = _mm_fwd("conv_proj", yconv, w_cp, tm, tw, D_MODEL, epilogue=merge_epi,
                         outs=[_out2d(s_len, D_MODEL, BF16, tm, tw), _out2d(s_len, D_MODEL, BF16, tm, tw)],
                         extras=[(ya, _tile_spec(tm, tw)), gate_a, gate_c])

    def res_epi(acc, e, o):
        x1_v = e[0] + acc
        o[0][...] = x1_v
        r = lax.rsqrt(jnp.mean(x1_v * x1_v, axis=-1, keepdims=True) + EPS)
        o[1][...] = (x1_v * r * e[1]).astype(BF16)
    assert tw == D_MODEL
    x1, h2 = _mm_fwd("out_proj", merged, w_out, tm, tw, D_MODEL, epilogue=res_epi,
                     outs=[_out2d(s_len, D_MODEL, F32, tm, tw), _out2d(s_len, D_MODEL, BF16, tm, tw)],
                     extras=[(x, _tile_spec(tm, tw)), (row(norm2_g), pl.BlockSpec((1, tw), lambda i, j, k: (0, 0)))])

    def up_epi(acc, e, o):
        o[0][...] = jnp.square(jnp.maximum(acc, 0.0)).astype(BF16)
    act = _mm_fwd("mlp_up", h2, w_up3, tb, tw, D_MODEL, epilogue=up_epi, outs=[_out2d(s_len, D_FF, BF16, tb, tw)])

    def loss_epi(acc, e, o):
        err = e[0] + acc - e[1]
        o[0][...] = (err * (1.0 / D_MODEL)).astype(BF16)
        sq = err * err
        part = sq[:, 0:LANE]
        for c0 in range(LANE, D_MODEL, LANE):
            part = part + sq[:, c0:c0 + LANE]
        o[1][...] = jnp.sum(part.reshape(tl // 8, 8, LANE), axis=0)
    tl = 512
    dy_b, loss_part = _mm_fwd(
        "mlp_down", act, w_down, tl, D_MODEL, D_FF, epilogue=loss_epi,
        outs=[_out2d(s_len, D_MODEL, BF16, tl, D_MODEL),
              (jax.ShapeDtypeStruct((8 * (s_len // tl), LANE), F32), pl.BlockSpec((8, LANE), lambda i, j, k: (i, 0)))],
        extras=[(x1, _tile_spec(tl, D_MODEL)), (target, _tile_spec(tl, D_MODEL))])

    def dup_epi(acc, e, o):
        o[0][...] = (acc * (2.0 * jnp.sqrt(e[0].astype(F32)))).astype(BF16)
    full = lambda cols: pl.BlockSpec((tm, cols), lambda i, j, n: (i, n))
    tokens = lambda cols: pl.BlockSpec((s_len, cols), lambda i, j, m: (m, j))
    dup = _mm_bwd_x("d_act", dy_b, full(D_MODEL), w_down, tm, tw, D_MODEL, s_len, D_MODEL, epilogue=dup_epi,
                    outs=[_out2d(s_len, D_FF, BF16, tm, tw)], extras=[(act, _tile_spec(tm, tw))])
    g_down = _mm_bwd_w("g_down", act, dy_b, tokens(D_MODEL), D_MODEL, 512, D_MODEL, s_len, False)
    g_up = _mm_bwd_w("g_up", h2, dup, tokens(512), D_FF, D_MODEL, 512, s_len, True)
    comm.grads_ready("mlp", dict(w_down=g_down, w_up=g_up))
    dx1_b, dg2 = carrying("d_h2", _d_norm_input, "d_h2", [(dup.reshape(1, s_len, D_FF), w_up3)],
                          x1, row(norm2_g), dy_b, BF16)

    def dmerge_epi(acc, e, o):
        ya_v, yc_v, ga, gc = [t.astype(F32) for t in e]
        o[0][...] = (acc * ga).astype(BF16)
        o[1][...] = (acc * gc).astype(BF16)
        o[2][0] = (acc * ya_v * ga * (1.0 - ga)).astype(BF16)
        o[2][1] = (acc * yc_v * gc * (1.0 - gc)).astype(BF16)
    dya, dyc, dgp2 = _mm_bwd_x(
        "d_merged", dx1_b, full(D_MODEL), w_out, tm, tw, D_MODEL, s_len, D_MODEL, epilogue=dmerge_epi,
        outs=[_out2d(s_len, D_MODEL, BF16, tm, tw), _out2d(s_len, D_MODEL, BF16, tm, tw),
              (jax.ShapeDtypeStruct((2, s_len, D_MODEL), BF16), pl.BlockSpec((2, tm, tw), lambda i, j, n: (0, i, j)))],
        extras=[(ya, _tile_spec(tm, tw)), (yc, _tile_spec(tm, tw)), gate_a, gate_c])
    g_out = _mm_bwd_w("g_out", merged, dx1_b, tokens(512), D_MODEL, D_MODEL, 512, s_len, False)
    d_attn = _mm_bwd_x("d_attn", dya, full(D_MODEL), w_ap, tm, tw, D_MODEL, s_len, D_MODEL,
                       outs=[_out2d(s_len, D_MODEL, BF16, tm, tw)])
    g_ap = _mm_bwd_w("g_attn_proj", attn, dya, tokens(512), D_MODEL, D_MODEL, 512, s_len, False)
    d_yconv = _mm_bwd_x("d_yconv", dyc, full(D_MODEL), w_cp, tm, tw, D_MODEL, s_len, D_MODEL,
                        outs=[_out2d(s_len, D_MODEL, BF16, tm, tw)])
    g_cp = _mm_bwd_w("g_conv_proj", yconv, dyc, tokens(512), D_MODEL, D_MODEL, 512, s_len, False)
    piece = lambda width: (lambda blk: (blk * width) // D_MODEL, lambda blk: (blk * width % D_MODEL) // width)
    pc, cb = piece(512)
    pieces = pl.BlockSpec((None, s_len, 512), lambda i, j, m: (pc(j), m, cb(j)))
    g_gate = _mm_bwd_w("g_gate", h, dgp2, pieces, 2 * D_MODEL, D_MODEL, 512, s_len, True)
    comm.grads_ready("proj", dict(w_out=g_out, w_attn_proj=g_ap, w_conv_proj=g_cp, w_gate=g_gate))

    dproj6, dbias = carrying("attn_bwd", _attn_bwd, qkn, v, attn, d_attn, bias, lse)
    dproj6, dgq, dgk = _qk_norm_bwd(dproj6, qk_raw, gq, gk)
    dproj6, dconv_wb = carrying("conv_bwd", _conv_bwd, dproj6, d_yconv, conv_in, conv_w, row(conv_b))

    g_in = carrying("g_in", _mm_bwd_w, "g_in", h, dproj6, pieces, 6 * D_MODEL, D_MODEL, 512, s_len, True)
    comm.grads_ready("in", dict(w_in=g_in))
    d_rel = carrying("bias_reduce", _bias_reduce, dbias)
    grad_x, dg1 = carrying("d_h", _d_norm_input, "d_h", [(dproj6, w_in3), (dgp2, w_gate3)],
                           x, row(norm1_g), dx1_b, F32)

    def bsum(t, f):
        return [], [t[0].astype(F32), t[1].astype(F32)]
    ts = 512
    db_a, db_c = carrying("b_gate_sum", _ew, "b_gate_sum", bsum,
                          [(dgp2, pl.BlockSpec((None, ts, D_MODEL), lambda i: (0, i, 0))),
                           (dgp2, pl.BlockSpec((None, ts, D_MODEL), lambda i: (1, i, 0)))],
                          [], [], sums=[D_MODEL, D_MODEL], ts=ts)

    big = dict(w_in=g_in, w_attn_proj=g_ap, w_conv_proj=g_cp, w_gate=g_gate, w_out=g_out,
               w_up=g_up, w_down=g_down)
    small = dict(norm1_g=dg1, norm2_g=dg2, conv_wb=dconv_wb, b_gate=(db_a, db_c),
                 q_norm_g=dgq, k_norm_g=dgk, rel_bias=d_rel)
    return loss_part, grad_x, big, small


def _finish_loss(loss_part):
    def body(l_ref, lo_ref):
        total = jnp.sum(jnp.sum(l_ref[...], axis=0, keepdims=True), axis=1, keepdims=True)
        lo_ref[...] = jnp.broadcast_to(total * (0.5 / D_MODEL), lo_ref.shape)

    return pl.pallas_call(body, name="finish_loss", out_shape=jax.ShapeDtypeStruct((8, LANE), F32))(loss_part)


def _place():
    return lax.axis_index("x"), lax.axis_index("y"), lax.axis_index("c")


def _other_chips(x, y):
    return [(1 - x, y), (x, 1 - y), (1 - x, 1 - y)]


def _cast_into_slot(name, where, w):
    r, cols = w.shape
    ts = 256

    def body(w_ref, x_ref, o_ref):
        o_ref[...] = x_ref[...].astype(o_ref.dtype)

    return pl.pallas_call(
        body, name=name,
        grid_spec=pltpu.PrefetchScalarGridSpec(
            num_scalar_prefetch=1, grid=(r // ts,),
            in_specs=[pl.BlockSpec((ts, cols), lambda i, w: (i, 0))],
            out_specs=pl.BlockSpec((None, ts, cols), lambda i, w: (w[0], i, 0))),
        out_shape=jax.ShapeDtypeStruct((N_SHARD, r, cols), BF16),
        compiler_params=_params("parallel"),
    )(where, _in_hbm(w))


def _remote(src, dst, send, recv, k, to):
    return pltpu.make_async_remote_copy(src_ref=src, dst_ref=dst, send_sem=send.at[k], recv_sem=recv.at[k],
                                        device_id=to, device_id_type=MESH)


def _gather_riders(slots):
    n = len(slots)

    def copy(refs, w, j, shard, which, send, recv, to):
        hr = slots[w].shape[1] // 2
        ref = refs[w].at[shard, pl.ds(which * hr, hr)]
        return _remote(ref, ref, send, recv, 3 * w + j, to)

    def for_each_peer(fn):
        x, y, c = _place()
        for w in range(n):
            for j, chip in enumerate(_other_chips(x, y)):
                fn(w, j, 2 * x + y, 2 * chip[0] + chip[1], c, (*chip, c), (x, y, 1 - c))

    def chips_start(_, refs, send, recv):
        for_each_peer(lambda w, j, mine, theirs, c, peer, sib: copy(refs, w, j, mine, c, send, recv, peer).start())

    def chips_finish(_, refs, send, recv):
        for_each_peer(lambda w, j, mine, theirs, c, peer, sib: copy(refs, w, j, theirs, c, send, recv, peer).wait_recv())
        for_each_peer(lambda w, j, mine, theirs, c, peer, sib: copy(refs, w, j, mine, c, send, recv, peer).wait_send())

    def sibling_start(_, refs, send, recv):
        for_each_peer(lambda w, j, mine, theirs, c, peer, sib: copy(refs, w, j, theirs, c, send, recv, sib).start())

    def sibling_finish(_, refs, send, recv):
        for_each_peer(lambda w, j, mine, theirs, c, peer, sib: copy(refs, w, j, theirs, 1 - c, send, recv, sib).wait_recv())
        for_each_peer(lambda w, j, mine, theirs, c, peer, sib: copy(refs, w, j, theirs, c, send, recv, sib).wait_send())

    return (lambda arrays: _Rider([], arrays, 3 * n, chips_start, chips_finish),
            lambda arrays: _Rider([], arrays, 3 * n, sibling_start, sibling_finish))


def _pair_exchange_rider(grads, landing):
    n = len(grads)

    def copies(srcs, dsts, send, recv):
        x, y, c = _place()
        out = []
        for w in range(n):
            hr = grads[w].shape[1] // 2
            out.append(_remote(srcs[w].at[:, pl.ds((1 - c) * hr, hr)], dsts[w], send, recv, w, (x, y, 1 - c)))
        return out

    def start(srcs, dsts, send, recv):
        for cp in copies(srcs, dsts, send, recv):
            cp.start()

    def finish(srcs, dsts, send, recv):
        for cp in copies(srcs, dsts, send, recv):
            cp.wait()

    return _Rider(grads, landing, n, start, finish)


def _row_tile(hr):
    return min(hr, 256)


def _pair_add(name, where, grad, got):
    _, hr, cols = got.shape
    tr = _row_tile(hr)
    nblk = hr // tr

    def body(w_ref, g_ref, r_ref, o_ref):
        o_ref[...] = (g_ref[...] + r_ref[...]).astype(o_ref.dtype)

    other = lambda k, w: (w[0] + 1 + k) % N_SHARD
    return pl.pallas_call(
        body, name=name,
        grid_spec=pltpu.PrefetchScalarGridSpec(
            num_scalar_prefetch=1, grid=(N_SHARD - 1, nblk),
            in_specs=[pl.BlockSpec((None, tr, cols), lambda k, i, w: (other(k, w), w[1] * nblk + i, 0)),
                      pl.BlockSpec((None, tr, cols), lambda k, i, w: (other(k, w), i, 0))],
            out_specs=pl.BlockSpec((None, tr, cols), lambda k, i, w: (other(k, w), i, 0))),
        out_shape=jax.ShapeDtypeStruct(got.shape, BF16),
        compiler_params=_params("parallel", "parallel"),
    )(where, _in_hbm(grad), _in_hbm(got))


def _chip_exchange_rider(partials, landing):
    n = len(partials)

    def copies(srcs, dsts, send, recv):
        x, y, c = _place()
        return [_remote(srcs[w].at[2 * chip[0] + chip[1]], dsts[w].at[j], send, recv, 3 * w + j, (*chip, c))
                for w in range(n) for j, chip in enumerate(_other_chips(x, y))]

    def start(srcs, dsts, send, recv):
        for cp in copies(srcs, dsts, send, recv):
            cp.start()

    def finish(srcs, dsts, send, recv):
        for cp in copies(srcs, dsts, send, recv):
            cp.wait()

    return _Rider(partials, landing, 3 * n, start, finish)


def _final_add(name, where, grad, got, arrived):
    _, hr, cols = got.shape
    tr = _row_tile(hr)
    nblk = hr // tr

    def body(w_ref, g_ref, r_ref, a_ref, o_ref):
        acc = g_ref[...] + r_ref[...]
        for j in range(3):
            acc = acc + a_ref[j].astype(F32)
        o_ref[...] = acc

    return pl.pallas_call(
        body, name=name,
        grid_spec=pltpu.PrefetchScalarGridSpec(
            num_scalar_prefetch=1, grid=(nblk,),
            in_specs=[pl.BlockSpec((None, tr, cols), lambda i, w: (w[0], w[1] * nblk + i, 0)),
                      pl.BlockSpec((None, tr, cols), lambda i, w: (w[0], i, 0)),
                      pl.BlockSpec((3, tr, cols), lambda i, w: (0, i, 0))],
            out_specs=pl.BlockSpec((tr, cols), lambda i, w: (w[1] * nblk + i, 0))),
        out_shape=jax.ShapeDtypeStruct((2 * hr, cols), F32),
        compiler_params=_params("parallel"),
    )(where, _in_hbm(grad), _in_hbm(got), _in_hbm(arrived))


def _pair_share_rider(shards):
    n = len(shards)

    def half(refs, w, which):
        hr = shards[w].shape[0] // 2
        return refs[w].at[pl.ds(which * hr, hr)]

    def start(_, refs, send, recv):
        x, y, c = _place()
        for w in range(n):
            _remote(half(refs, w, c), half(refs, w, c), send, recv, w, (x, y, 1 - c)).start()

    def finish(_, refs, send, recv):
        x, y, c = _place()
        for w in range(n):
            _remote(half(refs, w, 1 - c), half(refs, w, 1 - c), send, recv, w, (x, y, 1 - c)).wait_recv()
        for w in range(n):
            _remote(half(refs, w, c), half(refs, w, c), send, recv, w, (x, y, 1 - c)).wait_send()

    return _Rider([], shards, n, start, finish)


class _Exchange:
    PLAN = {"bias_expand": [("early", "gather")], "norm1": [("early", "forward")],
            "proj_conv": [("gate", "gather")], "proj_v": [("gate", "forward")],
            "proj_qk": [("late", "gather")], "gates": [("late", "forward")],
            "attn_fwd": [("mlp_w", "gather")], "conv_fwd": [("mlp_w", "forward")],
            "d_h2": [("mlp", "pair")], "attn_bwd": [("mlp", "chips"), ("proj", "pair")], "conv_bwd": [("mlp", "share")],
            "g_in": [("proj", "chips")], "bias_reduce": [("proj", "share"), ("in", "pair")],
            "d_h": [("in", "chips")], "b_gate_sum": [("in", "share")]}

    def __init__(self, where, early_slots, gate_slots, late_slots):
        self.where = where
        self.slots = dict(early=early_slots, gate=gate_slots, late=late_slots[:3], mlp_w=late_slots[3:])
        self.stage = {g: dict(zip(("gather", "forward"), _gather_riders(s))) for g, s in self.slots.items()}
        self.groups, self.reduced, self.pending = {}, {}, []

    def early_weights(self):
        w_in3, small = self.slots["early"]
        conv_w = small[:, :3, :].transpose(1, 0, 2).reshape(3, N_SHARD * small.shape[2])
        return w_in3, conv_w

    def gate_weight(self):
        return self.slots["gate"][0]

    def late_weights(self):
        rows = lambda a: a.reshape(a.shape[0] * a.shape[1], a.shape[2])
        w_ap, w_cp, w_out = self.slots["late"]
        w_up3, w_down = self.slots["mlp_w"]
        return rows(w_ap), rows(w_cp), rows(w_out), w_up3, rows(w_down)

    def grads_ready(self, group, grads):
        names = list(grads)
        g4 = [g if g.ndim == 3 else g.reshape(N_SHARD, -1, g.shape[1]) for g in grads.values()]
        self.groups[group] = dict(names=names, g4=g4)

    def riders(self, at):
        self.pending = self.PLAN.get(at, [])
        out = []
        for group, stage in self.pending:
            if group in self.slots:
                out.append(self.stage[group][stage](self.slots[group]))
                continue
            st = self.groups[group]
            if stage == "pair":
                landing = [lax.empty((N_SHARD, g.shape[1] // 2, g.shape[2]), F32) for g in st["g4"]]
                out.append(_pair_exchange_rider(st["g4"], landing))
            elif stage == "chips":
                landing = [lax.empty((3,) + p.shape[1:], p.dtype) for p in st["partial"]]
                out.append(_chip_exchange_rider(st["partial"], landing))
            else:
                out.append(_pair_share_rider(st["halves"]))
        return out

    def done(self, at, carried):
        for (group, stage), arrays in zip(self.pending, carried):
            if group in self.slots:
                self.slots[group] = arrays
                continue
            st = self.groups[group]
            tag = lambda what, n: what + "_" + n
            if stage == "pair":
                st["got"] = arrays
                st["partial"] = [_pair_add(tag("pair_add", n), self.where, g, r)
                                 for n, g, r in zip(st["names"], st["g4"], arrays)]
            elif stage == "chips":
                st["halves"] = [_final_add(tag("final_add", n), self.where, g, r, a)
                                for n, g, r, a in zip(st["names"], st["g4"], st["got"], arrays)]
            else:
                self.reduced.update(zip(st["names"], arrays))


SMALL_ROWS = 32
N_DEV = 8


def _all_reduce_small(pack):
    def body(p_ref, o_ref, buf, send, recv):
        x, y, c = _place()
        buf[4 * x + 2 * y + c] = p_ref[...]
        copies, waits = [], []
        for k in range(1, N_DEV):
            px = 1 - x if k & 4 else x
            py = 1 - y if k & 2 else y
            pc = 1 - c if k & 1 else c
            copies.append(_remote(p_ref, buf.at[4 * x + 2 * y + c], send, recv, k - 1, (px, py, pc)))
            waits.append(_remote(p_ref, buf.at[4 * px + 2 * py + pc], send, recv, k - 1, (px, py, pc)))
        for cp in copies:
            cp.start()
        for cp in waits:
            cp.wait_recv()
        acc = buf[0]
        for d in range(1, N_DEV):
            acc = acc + buf[d]
        o_ref[...] = acc
        for cp in copies:
            cp.wait_send()

    return pl.pallas_call(
        body, name="all_reduce_small",
        out_shape=jax.ShapeDtypeStruct(pack.shape, F32),
        scratch_shapes=[pltpu.VMEM((N_DEV,) + pack.shape, F32),
                        pltpu.SemaphoreType.DMA((N_DEV - 1,)), pltpu.SemaphoreType.DMA((N_DEV - 1,))],
    )(pack)


def _adamw(name, w, g, m, v):
    c1 = 1.0 - ADAM_B1 ** ADAM_STEP
    c2 = 1.0 - ADAM_B2 ** ADAM_STEP

    def fn(t, f):
        wv, gv, mv, vv = t
        m2 = ADAM_B1 * mv + (1.0 - ADAM_B1) * gv
        v2 = ADAM_B2 * vv + (1.0 - ADAM_B2) * (gv * gv)
        delta = -ADAM_LR * ((m2 / c1) / (jnp.sqrt(v2 / c2) + ADAM_EPS) + ADAM_WD * wv)
        return [delta, m2, v2], []

    cols = w.shape[1]
    return _ew(name, fn, [w, g, m, v], [], [(cols, F32)] * 3, ts=min(w.shape[0], 256))


LOSS_ROW = 26


def _pack_small(norm1_g, norm2_g, conv_b, b_gate, conv_w, q_norm_g, k_norm_g, rel_bias, loss=None):
    pack = jnp.zeros((SMALL_ROWS, D_MODEL), F32)
    for r0, v in ((0, norm1_g), (1, norm2_g), (2, conv_b), (3, b_gate.reshape(2, D_MODEL)), (5, conv_w),
                  (8, q_norm_g), (9, k_norm_g), (10, rel_bias)) + (((LOSS_ROW, loss),) if loss is not None else ()):
        v = v.reshape(-1, v.shape[-1]).astype(F32)
        pack = pack.at[r0:r0 + v.shape[0], :v.shape[1]].set(v)
    return pack


def _unpack_small(pack, conv_cols):
    return dict(norm1_g=pack[0], norm2_g=pack[1], conv_b=pack[2], b_gate=pack[3:5].reshape(2 * D_MODEL),
                conv_w=pack[5:8, :conv_cols], q_norm_g=pack[8, :HEAD_DIM], k_norm_g=pack[9, :HEAD_DIM],
                rel_bias=pack[10:10 + N_HEADS, :N_REL])


BIG = ["w_in", "w_attn_proj", "w_conv_proj", "w_gate", "w_out", "w_up", "w_down"]
LATE = ["w_attn_proj", "w_conv_proj", "w_out", "w_up", "w_down"]
WEIGHTS = ["norm1_g", "w_in", "q_norm_g", "k_norm_g", "rel_bias", "conv_w", "conv_b", "w_attn_proj",
           "w_conv_proj", "w_gate", "b_gate", "w_out", "norm2_g", "w_up", "w_down"]


def kernel(x, norm1_g, w_in, q_norm_g, k_norm_g, rel_bias, conv_w, conv_b, w_attn_proj, w_conv_proj, w_gate, b_gate, w_out, norm2_g, w_up, w_down, loss_target, m_norm1_g, m_w_in, m_q_norm_g, m_k_norm_g, m_rel_bias, m_conv_w, m_conv_b, m_w_attn_proj, m_w_conv_proj, m_w_gate, m_b_gate, m_w_out, m_norm2_g, m_w_up, m_w_down, v_norm1_g, v_w_in, v_q_norm_g, v_k_norm_g, v_rel_bias, v_conv_w, v_conv_b, v_w_attn_proj, v_w_conv_proj, v_w_gate, v_b_gate, v_w_out, v_norm2_g, v_w_up, v_w_down):
    given = dict(locals())
    w = {n: given[n] for n in WEIGHTS}
    m = {n: given["m_" + n] for n in WEIGHTS}
    v = {n: given["v_" + n] for n in WEIGHTS}
    s_len = x.shape[1]
    shard = 2 * lax.axis_index("x") + lax.axis_index("y")
    where = jnp.stack([shard, lax.axis_index("c")]).astype(jnp.int32)
    conv_cols = conv_w.shape[1]

    small_in = lax.dynamic_update_slice(jnp.zeros((N_SHARD, 16, conv_cols), F32), conv_w[None], (shard, 0, 0))
    slot = {n: _cast_into_slot("cast_" + n, where, w[n]) for n in BIG}
    comm = _Exchange(where, [slot["w_in"], small_in], [slot["w_gate"]], [slot[n] for n in LATE])

    loss_part, grad_x, _, small = _local_grads(
        x.reshape(s_len, D_MODEL), loss_target.reshape(s_len, D_MODEL), norm1_g, q_norm_g, k_norm_g,
        rel_bias, conv_b, b_gate, norm2_g, comm)
    grad = dict(comm.reduced)

    loss_local = _finish_loss(loss_part)
    pack = _pack_small(small["norm1_g"], small["norm2_g"], small["conv_wb"][3], jnp.concatenate(small["b_gate"], axis=1),
                       small["conv_wb"][0:3], small["q_norm_g"], small["k_norm_g"], small["rel_bias"],
                       loss=loss_local[0:1, :])
    total = _all_reduce_small(pack)
    g_small = _unpack_small(total, D_MODEL)
    g_small["conv_w"] = lax.dynamic_slice(g_small["conv_w"], (0, shard * conv_cols), (3, conv_cols))
    grad.update(g_small)

    delta, new_m, new_v = {}, {}, {}
    for n in BIG:
        delta[n], new_m[n], new_v[n] = _adamw("adamw_" + n, w[n], grad[n], m[n], v[n])
    small_names = [n for n in WEIGHTS if n not in BIG]
    packs = [_pack_small(**{n: src[n] for n in small_names}) for src in (w, grad, m, v)]
    for out, packed in zip((delta, new_m, new_v), _adamw("adamw_small", *packs)):
        out.update({n: a.reshape(w[n].shape) for n, a in _unpack_small(packed, conv_cols).items()})

    outs = [total[LOSS_ROW, 0], grad_x.reshape(x.shape)]
    for group in (grad, delta, new_m, new_v):
        outs += [group[n].reshape(w[n].shape) for n in WEIGHTS]
    return tuple(outs)
```

```python
import jax
import jax.numpy as jnp
from jax import lax
from jax.experimental import pallas as pl
from jax.experimental.pallas import tpu as pltpu

F32 = jnp.float32
BF16 = jnp.bfloat16

D_MODEL = 1024
N_HEADS = 16
HEAD_DIM = 64
CHUNK = 64
N_PREV_CHUNKS = 8
MAX_REL = 256
D_FF = 4096
N_REL = 2 * MAX_REL + 1
REL_PAD = 640
EPS = 1e-6
NEG_INF = -1e30
QK_SCALE = HEAD_DIM ** -0.5

SUPER = 4 * CHUNK
BAND = SUPER + N_PREV_CHUNKS * CHUNK
SKEW_W = 1024
N_SHARD = 4
LANE = 128
MXU_DIM = 256
VMEM_LIMIT = 48 * 1024 * 1024

ADAM_LR = 0.001
ADAM_B1 = 0.9
ADAM_B2 = 0.999
ADAM_EPS = 1e-08
ADAM_WD = 0.01
ADAM_STEP = 10

MESH = pl.DeviceIdType.MESH
NN = (((1,), (0,)), ((), ()))
NT = (((1,), (1,)), ((), ()))
TN = (((0,), (0,)), ((), ()))


def _params(*sem):
    return pltpu.CompilerParams(dimension_semantics=sem or None, vmem_limit_bytes=VMEM_LIMIT)


HBM_SPEC = pl.BlockSpec(memory_space=pl.ANY)


class _Rider:
    def __init__(self, sources, arrays, n_sem, start, finish):
        self.sources, self.arrays, self.n_sem, self.start, self.finish = sources, arrays, n_sem, start, finish


def _carry(riders, body, *, name, out_shape, grid=(), in_specs=None, out_specs=None, scratch_shapes=(),
           semantics=(), input_output_aliases=None):
    aliases = dict(input_output_aliases or {})
    if not riders:
        kw = {} if in_specs is None else dict(in_specs=in_specs, out_specs=out_specs)
        return pl.pallas_call(body, name=name, grid=grid, out_shape=out_shape, scratch_shapes=scratch_shapes,
                              input_output_aliases=aliases, compiler_params=_params(*semantics), **kw)
    single = not isinstance(out_shape, (list, tuple))
    shapes = [out_shape] if single else list(out_shape)
    n_out, n_scr = len(shapes), len(scratch_shapes)
    in_hbm = lambda a: pltpu.with_memory_space_constraint(a, pltpu.HBM)
    srcs = [in_hbm(a) for r in riders for a in r.sources]
    arrs = [in_hbm(a) for r in riders for a in r.arrays]
    vmem = pl.BlockSpec(memory_space=pltpu.VMEM)

    def run(*args):
        n_in = len(args)

        def wrapped(*refs):
            pos = n_in
            src_refs = refs[pos:pos + len(srcs)]
            pos += len(srcs) + len(arrs)
            outs = refs[pos:pos + n_out]
            pos += n_out
            arr_refs = refs[pos:pos + len(arrs)]
            pos += len(arrs)
            scratch = refs[pos:pos + n_scr]
            sems = refs[pos + n_scr:]
            first, last = True, True
            for d, size in enumerate(grid):
                first = jnp.logical_and(first, pl.program_id(d) == 0)
                last = jnp.logical_and(last, pl.program_id(d) == size - 1)

            def each(method):
                s0 = a0 = 0
                for k, r in enumerate(riders):
                    getattr(r, method)(src_refs[s0:s0 + len(r.sources)], arr_refs[a0:a0 + len(r.arrays)],
                                       sems[2 * k], sems[2 * k + 1])
                    s0, a0 = s0 + len(r.sources), a0 + len(r.arrays)

            pl.when(first)(lambda: each("start"))
            body(*refs[:n_in], *outs, *scratch)
            pl.when(last)(lambda: each("finish"))

        ins = [vmem] * n_in if in_specs is None else list(in_specs)
        if out_specs is None:
            o_specs = [vmem] * n_out
        else:
            o_specs = [out_specs] if single else list(out_specs)
        for k in range(len(arrs)):
            aliases[n_in + len(srcs) + k] = n_out + k
        res = pl.pallas_call(
            wrapped, name=name, grid=grid,
            in_specs=ins + [HBM_SPEC] * (len(srcs) + len(arrs)),
            out_specs=o_specs + [HBM_SPEC] * len(arrs),
            out_shape=shapes + [jax.ShapeDtypeStruct(a.shape, a.dtype) for a in arrs],
            scratch_shapes=list(scratch_shapes) + [pltpu.SemaphoreType.DMA((r.n_sem,)) for r in riders for _ in range(2)],
            input_output_aliases=aliases,
            compiler_params=_params(*["arbitrary"] * len(grid)),
        )(*args, *srcs, *arrs)
        core, rest = res[:n_out], list(res[n_out:])
        carried, a0 = [], 0
        for r in riders:
            carried.append(rest[a0:a0 + len(r.arrays)])
            a0 += len(r.arrays)
        return (core[0] if single else core), carried

    return run


def _mm(name, dims, a, a_spec, b, b_spec, grid, tile, outs, epilogue=None, extras=(), riders=()):
    nk, ne, no = grid[2], len(extras), len(outs)

    def body(a_ref, b_ref, *refs):
        e_refs, o_refs = refs[:ne], refs[ne:ne + no]
        part = lax.dot_general(a_ref[...], b_ref[...], dims, preferred_element_type=F32)

        def finish(acc):
            if epilogue is None:
                o_refs[0][...] = acc.astype(o_refs[0].dtype)
            else:
                epilogue(acc, [r[...] for r in e_refs], o_refs)

        if nk == 1:
            finish(part)
        else:
            acc_ref = refs[ne + no]
            k = pl.program_id(2)

            @pl.when(k == 0)
            def _():
                acc_ref[...] = part

            @pl.when(k > 0)
            def _():
                acc_ref[...] += part

            @pl.when(k == nk - 1)
            def _():
                finish(acc_ref[...])

    res = _carry(
        riders, body, name=name, grid=grid,
        in_specs=[a_spec, b_spec] + [s for _, s in extras],
        out_specs=[s for _, s in outs],
        out_shape=[s for s, _ in outs],
        scratch_shapes=[pltpu.VMEM(tile, F32)] if nk > 1 else [],
        semantics=("parallel", "parallel", "arbitrary"),
    )(a, b, *[e for e, _ in extras])
    res, carried = res if riders else (res, None)
    res = res[0] if no == 1 else res
    return (res, carried) if riders else res


def _tile_spec(tm, tn, col0=0):
    return pl.BlockSpec((tm, tn), lambda i, j, k: (i, j + col0))


def _out2d(m, n, dtype, tm, tn):
    return (jax.ShapeDtypeStruct((m, n), dtype), _tile_spec(tm, tn))


def _mm_fwd(name, a, w, tm, tn, tk, outs=None, epilogue=None, extras=(), col0=0, ncols=None, riders=()):
    m, kdim = a.shape
    if w.ndim == 3:
        per = w.shape[2] // tn
        n = ncols or N_SHARD * w.shape[2]
        w_spec = pl.BlockSpec((None, tk, tn), lambda i, j, k: ((j + col0) // per, k, (j + col0) % per))
    else:
        n = ncols or w.shape[1]
        w_spec = pl.BlockSpec((tk, tn), lambda i, j, k: (k, j + col0))
    if outs is None:
        outs = [_out2d(m, n, F32, tm, tn)]
    return _mm(name, NN, a, pl.BlockSpec((tm, tk), lambda i, j, k: (i, k)), w, w_spec,
               (m // tm, n // tn, kdim // tk), (tm, tn), outs, epilogue, extras, riders)


def _mm_bwd_x(name, g, g_spec, w, tm, tj, tc, m, n_contract, outs=None, epilogue=None, extras=(), riders=()):
    if w.ndim == 3:
        per = w.shape[2] // tc
        kdim = w.shape[1]
        w_spec = pl.BlockSpec((None, tj, tc), lambda i, j, n: (n // per, j, n % per))
    else:
        kdim = w.shape[0]
        w_spec = pl.BlockSpec((tj, tc), lambda i, j, n: (j, n))
    if outs is None:
        outs = [_out2d(m, kdim, F32, tm, tj)]
    return _mm(name, NT, g, g_spec, w, w_spec, (m // tm, kdim // tj, n_contract // tc),
               (tm, tj), outs, epilogue, extras, riders)


def _mm_bwd_w(name, a, g, g_spec, n, tk, tn, tm, sharded, riders=()):
    m, kdim = a.shape
    if sharded:
        per = (n // N_SHARD) // tn
        out = (jax.ShapeDtypeStruct((N_SHARD, kdim, n // N_SHARD), F32),
               pl.BlockSpec((None, tk, tn), lambda i, j, mm: (j // per, i, j % per)))
    else:
        out = (jax.ShapeDtypeStruct((kdim, n), F32), pl.BlockSpec((tk, tn), lambda i, j, mm: (i, j)))
    return _mm(name, TN, a, pl.BlockSpec((tm, tk), lambda i, j, mm: (mm, i)), g, g_spec,
               (kdim // tk, n // tn, m // tm), (tk, tn), [out], riders=riders)


def _ew(name, fn, tiles, fulls, outs, sums=(), ts=512, riders=()):
    tiles = [t if isinstance(t, tuple) else (t, pl.BlockSpec((ts, t.shape[1]), lambda i: (i, 0)))
             for t in tiles]
    s_rows = tiles[0][0].shape[-2]
    nt, nf, no = len(tiles), len(fulls), len(outs)

    def body(*refs):
        t_vals = [r[...] for r in refs[:nt]]
        f_vals = [r[...] for r in refs[nt:nt + nf]]
        o_refs, s_refs = refs[nt + nf:nt + nf + no], refs[nt + nf + no:]
        o_vals, s_vals = fn(t_vals, f_vals)
        for r, v in zip(o_refs, o_vals):
            r[...] = v.astype(r.dtype)
        for r, v in zip(s_refs, s_vals):
            part = jnp.sum(v, axis=0, keepdims=True)

            @pl.when(pl.program_id(0) == 0)
            def _():
                r[...] = part

            @pl.when(pl.program_id(0) > 0)
            def _():
                r[...] += part

    full_specs = [pl.BlockSpec(f.shape, lambda i, nd=f.ndim: (0,) * nd) for f in fulls]
    return _carry(
        riders, body, name=name, grid=(s_rows // ts,),
        in_specs=[s for _, s in tiles] + full_specs,
        out_specs=[pl.BlockSpec((ts, c), lambda i: (i, 0)) for c, _ in outs]
        + [pl.BlockSpec((1, c), lambda i: (0, 0)) for c in sums],
        out_shape=[jax.ShapeDtypeStruct((s_rows, c), dt) for c, dt in outs]
        + [jax.ShapeDtypeStruct((1, c), F32) for c in sums],
        semantics=("arbitrary",),
    )(*[t for t, _ in tiles], *fulls)


def _rms_fwd(name, x, g, riders=()):
    def fn(t, f):
        xv = t[0]
        r = lax.rsqrt(jnp.mean(xv * xv, axis=-1, keepdims=True) + EPS)
        return [xv * r * f[0]], []
    out = _ew(name, fn, [x], [g], [(x.shape[1], BF16)], riders=riders)
    return (out[0][0], out[1]) if riders else out[0]


def _split3(x):
    x1 = x.astype(BF16)
    r1 = x - x1.astype(F32)
    x2 = r1.astype(BF16)
    x3 = (r1 - x2.astype(F32)).astype(BF16)
    return x1, x2, x3


def _rel_class(cp):
    far = (cp < MAX_REL) | (cp > BAND)
    return jnp.where(far, 2 * MAX_REL, BAND - cp)


def _skew_rows(x, sign):
    row = lax.broadcasted_iota(jnp.int32, x.shape, 0)
    for b in range(CHUNK.bit_length() - 1):
        shift = (1 << b) if sign > 0 else SKEW_W - (1 << b)
        x = jnp.where((row >> b) & 1 == 1, pltpu.roll(x, shift, 1), x)
    return x


def _roll_lanes(x, shift):
    return x if shift % SKEW_W == 0 else pltpu.roll(x, shift % SKEW_W, 1)


N_START = 3


def _bias_expand(rel_bias, riders=()):
    rel = jnp.pad(rel_bias, ((0, 0), (0, REL_PAD - N_REL))).reshape(N_HEADS, 1, REL_PAD)

    def body(rel_ref, o_ref):
        cls = lax.broadcasted_iota(jnp.int32, (REL_PAD, SKEW_W), 0)
        cp = lax.broadcasted_iota(jnp.int32, (REL_PAD, SKEW_W), 1)
        onehot = (cls == _rel_class(cp)).astype(BF16)
        rel8 = jnp.broadcast_to(rel_ref[...], (8, REL_PAD))
        trow = sum(jnp.dot(p, onehot, preferred_element_type=F32) for p in _split3(rel8))[0:1]
        first = _skew_rows(jnp.broadcast_to(trow, (CHUNK, SKEW_W)), +1)
        full = jnp.concatenate([_roll_lanes(first, CHUNK * g) for g in range(SUPER // CHUNK)], axis=0)[:, :BAND]
        qc = lax.broadcasted_iota(jnp.int32, (SUPER, BAND), 0) // CHUNK
        kc = lax.broadcasted_iota(jnp.int32, (SUPER, BAND), 1) // CHUNK
        on_band = (kc >= qc) & (kc <= qc + N_PREV_CHUNKS)
        table = jnp.where(on_band, full, NEG_INF).T
        key = lax.broadcasted_iota(jnp.int32, (BAND, SUPER), 0)
        for t in range(N_START):
            o_ref[t] = jnp.where(key < (N_START - 1 - t) * SUPER, NEG_INF, table)

    return _carry(
        riders, body, name="bias_expand", grid=(N_HEADS,),
        in_specs=[pl.BlockSpec((None, 1, REL_PAD), lambda h: (h, 0, 0))],
        out_specs=pl.BlockSpec((N_START, None, BAND, SUPER), lambda h: (0, h, 0, 0)),
        out_shape=jax.ShapeDtypeStruct((N_START, N_HEADS, BAND, SUPER), F32),
        semantics=("arbitrary",),
    )(rel)


def _bias_reduce(dbias, riders=()):
    def body(d_ref, o_ref):
        x = jnp.concatenate([d_ref[...].T, jnp.zeros((SUPER, SKEW_W - BAND), F32)], axis=1)
        folded = sum(_roll_lanes(x[CHUNK * g:CHUNK * (g + 1)], -CHUNK * g) for g in range(SUPER // CHUNK))
        diag = jnp.sum(_skew_rows(folded, -1), axis=0, keepdims=True)
        cp = lax.broadcasted_iota(jnp.int32, (SKEW_W, REL_PAD), 0)
        cls = lax.broadcasted_iota(jnp.int32, (SKEW_W, REL_PAD), 1)
        onehot = (cls == _rel_class(cp)).astype(BF16)
        diag8 = jnp.broadcast_to(diag, (8, SKEW_W))
        o_ref[...] = sum(jnp.dot(p, onehot, preferred_element_type=F32) for p in _split3(diag8))[0:1]

    out = _carry(
        riders, body, name="bias_reduce", grid=(N_HEADS,),
        in_specs=[pl.BlockSpec((None, BAND, SUPER), lambda h: (h, 0, 0))],
        out_specs=pl.BlockSpec((None, 1, REL_PAD), lambda h: (h, 0, 0)),
        out_shape=jax.ShapeDtypeStruct((N_HEADS, 1, REL_PAD), F32),
        semantics=("arbitrary",),
    )(dbias)
    out, carried = out if riders else (out, None)
    out = out.reshape(N_HEADS, REL_PAD)[:, :N_REL]
    return (out, carried) if riders else out


HEADS_PER_STEP = 8
HEAD_COLS = HEADS_PER_STEP * HEAD_DIM
N_HEAD_GROUPS = N_HEADS // HEADS_PER_STEP


def _scores_t(qs, kn, bias_t):
    return jnp.concatenate([lax.dot_general(k, qs, NT, preferred_element_type=F32) for k in kn], axis=0) + bias_t


def _bias_spec():
    return pl.BlockSpec((None, HEADS_PER_STEP, BAND, SUPER), lambda hg, i: (jnp.minimum(i, N_START - 1), hg, 0, 0))


def _band_specs(nb, col0, clamp_hi):
    def spec(d):
        def index(hg, i):
            blk = jnp.maximum(i - d, 0)
            if clamp_hi:
                blk = jnp.minimum(blk, nb - 1)
            return (blk, col0 + hg)
        return pl.BlockSpec((SUPER, HEAD_COLS), index)
    return [spec(2), spec(1), spec(0)]


def _head_sums(y):
    same_head = (lax.broadcasted_iota(jnp.int32, (MXU_DIM, MXU_DIM), 0) // HEAD_DIM
                 == lax.broadcasted_iota(jnp.int32, (MXU_DIM, MXU_DIM), 1) // HEAD_DIM).astype(BF16)
    sums = []
    for c0 in range(0, y.shape[1], MXU_DIM):
        chunk = y[:, c0:c0 + MXU_DIM]
        hi = chunk.astype(BF16)
        lo = (chunk - hi.astype(F32)).astype(BF16)
        sums.append(jnp.dot(hi, same_head, preferred_element_type=F32)
                    + jnp.dot(lo, same_head, preferred_element_type=F32))
    return jnp.concatenate(sums, axis=1)


def _head_unit(x):
    r = lax.rsqrt(_head_sums(x * x) * (1.0 / HEAD_DIM) + EPS)
    return x * r, r


def _head(hh):
    return slice(HEAD_DIM * hh, HEAD_DIM * (hh + 1))


def _key_block(j):
    return slice(SUPER * j, SUPER * (j + 1))


LSE_ROWS = 8


def _attn_fwd(qkn, v, bias, riders=()):
    s_len = qkn.shape[0]
    nb = s_len // SUPER

    def body(q_ref, k0, k1, k2, v0, v1, v2, b_ref, o_ref, lse_ref):
        outs = []
        v_t = [v0[...].T, v1[...].T, v2[...].T]

        def probabilities(hh):
            sl = _head(hh)
            s = _scores_t(q_ref[:, sl], [k0[:, sl], k1[:, sl], k2[:, sl]], b_ref[hh])
            m = jnp.max(s, axis=0, keepdims=True)
            e = jnp.exp(s - m)
            l = jnp.sum(e, axis=0, keepdims=True)
            lse_ref[hh:hh + 1, :] = m + jnp.log(l)
            return e.astype(BF16), 1.0 / l, sl

        def weighted_values(e, inv_l, sl):
            outs.append(sum(jnp.dot(v_t[j][sl, :], e[_key_block(j), :], preferred_element_type=F32)
                            for j in range(3)) * inv_l)

        ready = probabilities(0)
        for hh in range(1, HEADS_PER_STEP):
            following = probabilities(hh)
            weighted_values(*ready)
            ready = following
        weighted_values(*ready)
        o_ref[...] = jnp.concatenate(outs, axis=0).T.astype(o_ref.dtype)

    return _carry(
        riders, body, name="attn_fwd", grid=(N_HEAD_GROUPS, nb),
        in_specs=[pl.BlockSpec((SUPER, HEAD_COLS), lambda hg, i: (i, hg))]
        + _band_specs(nb, N_HEAD_GROUPS, False) + _band_specs(nb, 0, False) + [_bias_spec()],
        out_specs=[pl.BlockSpec((SUPER, HEAD_COLS), lambda hg, i: (i, hg)),
                   pl.BlockSpec((None, LSE_ROWS, SUPER), lambda hg, i: (hg, 0, i))],
        out_shape=[jax.ShapeDtypeStruct((s_len, D_MODEL), BF16),
                   jax.ShapeDtypeStruct((N_HEAD_GROUPS, LSE_ROWS, s_len), F32)],
        semantics=("parallel", "arbitrary"),
    )(qkn, qkn, qkn, qkn, v, v, v, bias)


def _attn_bwd(qkn, v, out, d_out, bias, lse, riders=()):
    s_len = qkn.shape[0]
    nb = s_len // SUPER

    def body(q_ref, k0, k1, k2, v0, v1, v2, o_ref, do_ref, b_ref, lse_ref, dp_ref, db_ref, aq_ref, ak_ref, av_ref):
        i = pl.program_id(1)

        @pl.when(i == 0)
        def _():
            aq_ref[...] = jnp.zeros_like(aq_ref)
            ak_ref[...] = jnp.zeros_like(ak_ref)
            av_ref[...] = jnp.zeros_like(av_ref)
            db_ref[...] = jnp.zeros_like(db_ref)

        @pl.when(i < nb)
        def _():
            dq, dk, dv = [], [[], [], []], [[], [], []]
            ones = jnp.ones((8, HEAD_DIM), BF16)
            k_t = [k0[...].T, k1[...].T, k2[...].T]

            def softmax_grad(hh):
                sl = _head(hh)
                qs, do = q_ref[:, sl], do_ref[:, sl]
                kn = [k0[:, sl], k1[:, sl], k2[:, sl]]
                prod = do.astype(F32) * o_ref[:, sl].astype(F32)
                hi = prod.astype(BF16)
                lo = (prod - hi.astype(F32)).astype(BF16)
                delta = (lax.dot_general(ones, hi, NT, preferred_element_type=F32)
                         + lax.dot_general(ones, lo, NT, preferred_element_type=F32))[0:1]
                lse_row = lse_ref[hh:hh + 1, :]
                pb, dsb = [], []
                for j, (kj, vj) in enumerate(zip(kn, (v0, v1, v2))):
                    rows = _key_block(j)
                    p = jnp.exp(lax.dot_general(kj, qs, NT, preferred_element_type=F32) + b_ref[hh, rows, :] - lse_row)
                    ds = p * (lax.dot_general(vj[:, sl], do, NT, preferred_element_type=F32) - delta)
                    db_ref[hh, rows, :] += ds
                    pb.append(p.astype(BF16))
                    dsb.append(ds.astype(BF16))
                return pb, dsb, qs, do, sl

            def operand_grads(pb, dsb, qs, do, sl):
                dq.append(sum(jnp.dot(k_t[j][sl, :], dsb[j], preferred_element_type=F32) for j in range(3)))
                for j in range(3):
                    dv[j].append(jnp.dot(pb[j], do, preferred_element_type=F32))
                    dk[j].append(jnp.dot(dsb[j], qs, preferred_element_type=F32))

            ready = softmax_grad(0)
            for hh in range(1, HEADS_PER_STEP):
                following = softmax_grad(hh)
                operand_grads(*ready)
                ready = following
            operand_grads(*ready)
            aq_ref[i % 3] = jnp.concatenate(dq, axis=0).T
            for j in range(3):
                slot = (i + 1 + j) % 3
                if j < 2:
                    ak_ref[slot] += jnp.concatenate(dk[j], axis=1)
                    av_ref[slot] += jnp.concatenate(dv[j], axis=1)
                else:
                    ak_ref[slot] = jnp.concatenate(dk[j], axis=1)
                    av_ref[slot] = jnp.concatenate(dv[j], axis=1)

        slot = (i + 1) % 3
        dp_ref[0] = aq_ref[slot].astype(dp_ref.dtype)
        dp_ref[1] = ak_ref[slot].astype(dp_ref.dtype)
        dp_ref[2] = av_ref[slot].astype(dp_ref.dtype)

    def qrow(hg, i):
        return (jnp.minimum(i, nb - 1), hg)

    return _carry(
        riders, body, name="attn_bwd", grid=(N_HEAD_GROUPS, nb + 2),
        in_specs=[pl.BlockSpec((SUPER, HEAD_COLS), qrow)]
        + _band_specs(nb, N_HEAD_GROUPS, True) + _band_specs(nb, 0, True)
        + [pl.BlockSpec((SUPER, HEAD_COLS), qrow), pl.BlockSpec((SUPER, HEAD_COLS), qrow), _bias_spec(),
           pl.BlockSpec((None, LSE_ROWS, SUPER), lambda hg, i: (hg, 0, jnp.minimum(i, nb - 1)))],
        out_specs=[pl.BlockSpec((3, SUPER, HEAD_COLS), lambda hg, i: (0, jnp.maximum(i - 2, 0), hg)),
                   pl.BlockSpec((HEADS_PER_STEP, BAND, SUPER), lambda hg, i: (hg, 0, 0))],
        out_shape=[jax.ShapeDtypeStruct((6, s_len, D_MODEL), BF16),
                   jax.ShapeDtypeStruct((N_HEADS, BAND, SUPER), F32)],
        scratch_shapes=[pltpu.VMEM((3, SUPER, HEAD_COLS), F32)] * 3,
        semantics=("parallel", "arbitrary"),
    )(qkn, qkn, qkn, qkn, v, v, v, out, d_out, bias, lse)


def _qk_norm_bwd(dproj6, qk_raw, gq, gk, riders=()):
    s_len = qk_raw.shape[0]
    ts = min(1024, s_len)

    nsteps = s_len // ts
    half = D_MODEL // 2

    def body(d_ref, raw_ref, gq_ref, gk_ref, o_ref, dgq_ref, dgk_ref, acc_ref):
        step = pl.program_id(0)

        @pl.when(step == 0)
        def _():
            acc_ref[...] = jnp.zeros_like(acc_ref)

        for piece, (g_ref, scale) in enumerate(((gq_ref, QK_SCALE), (gk_ref, 1.0))):
            for c0 in (0, half):
                xhat, r = _head_unit(raw_ref[:, piece * D_MODEL + c0:piece * D_MODEL + c0 + half].astype(F32))
                dn = d_ref[piece, :, c0:c0 + half].astype(F32) * scale
                u = dn * g_ref[...]
                dx = r * (u - xhat * (_head_sums(u * xhat) * (1.0 / HEAD_DIM)))
                o_ref[piece, :, c0:c0 + half] = dx.astype(o_ref.dtype)
                acc_ref[piece:piece + 1, c0:c0 + half] += jnp.sum(dn * xhat, axis=0, keepdims=True)

        @pl.when(step == nsteps - 1)
        def _():
            lane = lax.broadcasted_iota(jnp.int32, (D_MODEL, LANE), 0) % HEAD_DIM
            fold = (lane == lax.broadcasted_iota(jnp.int32, (D_MODEL, LANE), 1)).astype(BF16)
            tot = sum(jnp.dot(p, fold, preferred_element_type=F32) for p in _split3(acc_ref[...]))
            dgq_ref[...] = tot[0:1, :HEAD_DIM]
            dgk_ref[...] = tot[1:2, :HEAD_DIM]

    gain = pl.BlockSpec((1, half), lambda i: (0, 0))
    small = pl.BlockSpec((1, HEAD_DIM), lambda i: (0, 0))
    per_head = lambda g: jnp.tile(g, (1, half // HEAD_DIM))
    return _carry(
        riders, body, name="qk_norm_bwd", grid=(nsteps,),
        in_specs=[pl.BlockSpec((2, ts, D_MODEL), lambda i: (0, i, 0)),
                  pl.BlockSpec((ts, 2 * D_MODEL), lambda i: (i, 0)), gain, gain],
        out_specs=[pl.BlockSpec((2, ts, D_MODEL), lambda i: (0, i, 0)), small, small],
        out_shape=[jax.ShapeDtypeStruct(dproj6.shape, dproj6.dtype),
                   jax.ShapeDtypeStruct((1, HEAD_DIM), F32), jax.ShapeDtypeStruct((1, HEAD_DIM), F32)],
        scratch_shapes=[pltpu.VMEM((8, D_MODEL), F32)],
        input_output_aliases={0: 0},
        semantics=("arbitrary",),
    )(dproj6, qk_raw, per_head(gq), per_head(gk))


CONV_ROWS = 512
HALO = 16


def _rows_with_halo(ref, r0, n, front, s_len):
    zeros = jnp.zeros((HALO, ref.shape[1]), F32)
    if front:
        return (jnp.concatenate([zeros, ref[0:n, :].astype(F32)], axis=0) if r0 == 0
                else ref[r0 - HALO:r0 + n, :].astype(F32))
    return (jnp.concatenate([ref[r0:r0 + n, :].astype(F32), zeros], axis=0) if r0 + n == s_len
            else ref[r0:r0 + n + HALO, :].astype(F32))


def _earlier(ext, k):
    return pltpu.roll(ext, k, 0)[HALO:]


def _later(ext, k):
    n = ext.shape[0]
    return pltpu.roll(ext, n - k, 0)[:n - HALO]


def _conv_cols(col0):
    return lambda s_len: pl.BlockSpec((s_len, LANE), lambda j: (0, col0 + j))


def _conv_fwd(proj, conv_w, conv_b, riders=()):
    s_len = proj.shape[0]

    def body(bg_ref, cg_ref, xc_ref, w_ref, b_ref, o_ref):
        w = [w_ref[t:t + 1, :] for t in range(3)]
        for r0 in range(0, s_len, CONV_ROWS):
            u = _rows_with_halo(cg_ref, r0, CONV_ROWS, True, s_len) * \
                _rows_with_halo(xc_ref, r0, CONV_ROWS, True, s_len)
            conv = b_ref[...] + w[0] * _earlier(u, 2) + w[1] * _earlier(u, 1) + w[2] * u[HALO:]
            o_ref[r0:r0 + CONV_ROWS, :] = (bg_ref[r0:r0 + CONV_ROWS, :].astype(F32) * conv).astype(o_ref.dtype)

    return _carry(
        riders, body, name="conv_fwd", grid=(D_MODEL // LANE,),
        in_specs=[_conv_cols(0)(s_len), _conv_cols(8)(s_len), _conv_cols(16)(s_len),
                  pl.BlockSpec((3, LANE), lambda j: (0, j)), pl.BlockSpec((1, LANE), lambda j: (0, j))],
        out_specs=pl.BlockSpec((s_len, LANE), lambda j: (0, j)),
        out_shape=jax.ShapeDtypeStruct((s_len, D_MODEL), BF16),
        semantics=("parallel",),
    )(proj, proj, proj, conv_w, conv_b)


def _conv_bwd(dproj6, dy, proj, conv_w, conv_b, riders=()):
    s_len = proj.shape[0]

    def body(dy_ref, bg_ref, cg_ref, xc_ref, w_ref, b_ref, _, dp_ref, dw_ref):
        w = [w_ref[t:t + 1, :] for t in range(3)]
        acc = [jnp.zeros((1, LANE), F32) for _ in range(4)]
        for r0 in range(0, s_len, CONV_ROWS):
            rows = slice(r0, r0 + CONV_ROWS)
            u = _rows_with_halo(cg_ref, r0, CONV_ROWS, True, s_len) * \
                _rows_with_halo(xc_ref, r0, CONV_ROWS, True, s_len)
            u2, u1, u0 = _earlier(u, 2), _earlier(u, 1), u[HALO:]
            conv = b_ref[...] + w[0] * u2 + w[1] * u1 + w[2] * u0
            dp_ref[0, rows, :] = (dy_ref[rows, :].astype(F32) * conv).astype(dp_ref.dtype)
            dconv_ext = _rows_with_halo(dy_ref, r0, CONV_ROWS, False, s_len) * \
                _rows_with_halo(bg_ref, r0, CONV_ROWS, False, s_len)
            dconv = dconv_ext[:CONV_ROWS]
            for t, term in enumerate([dconv * u2, dconv * u1, dconv * u0, dconv]):
                acc[t] = acc[t] + jnp.sum(term, axis=0, keepdims=True)
            du = w[2] * dconv + w[1] * _later(dconv_ext, 1) + w[0] * _later(dconv_ext, 2)
            dp_ref[1, rows, :] = (du * xc_ref[rows, :].astype(F32)).astype(dp_ref.dtype)
            dp_ref[2, rows, :] = (du * cg_ref[rows, :].astype(F32)).astype(dp_ref.dtype)
        dw_ref[...] = jnp.zeros_like(dw_ref)
        for t in range(4):
            dw_ref[t:t + 1, :] = acc[t]

    return _carry(
        riders, body, name="conv_bwd", grid=(D_MODEL // LANE,),
        in_specs=[pl.BlockSpec((s_len, LANE), lambda j: (0, j)),
                  _conv_cols(0)(s_len), _conv_cols(8)(s_len), _conv_cols(16)(s_len),
                  pl.BlockSpec((3, LANE), lambda j: (0, j)), pl.BlockSpec((1, LANE), lambda j: (0, j)),
                  pl.BlockSpec(memory_space=pl.ANY)],
        out_specs=[pl.BlockSpec((3, s_len, LANE), lambda j: (1, 0, j)),
                   pl.BlockSpec((8, LANE), lambda j: (0, j))],
        out_shape=[jax.ShapeDtypeStruct(dproj6.shape, dproj6.dtype),
                   jax.ShapeDtypeStruct((8, D_MODEL), F32)],
        input_output_aliases={6: 0},
        semantics=("parallel",),
    )(dy, proj, proj, proj, conv_w, conv_b, dproj6)


def _d_norm_input(name, terms, x, g, dres, out_dtype, riders=()):
    s_len, kdim = terms[0][0].shape[1], terms[0][1].shape[1]
    tm, chunk = 512, 512
    n = len(terms)

    def body(*refs):
        x_ref, g_ref, r_ref, o_ref, dg_ref = refs[2 * n:]
        dh = jnp.zeros((tm, kdim), F32)
        for p_ref, w_ref in zip(refs[:n], refs[n:2 * n]):
            n_piece, _, width = p_ref.shape
            per_shard = w_ref.shape[2]
            for c0 in range(0, n_piece * width, chunk):
                dh = dh + lax.dot_general(p_ref[c0 // width, :, c0 % width:c0 % width + chunk],
                                          w_ref[c0 // per_shard, :, c0 % per_shard:c0 % per_shard + chunk],
                                          NT, preferred_element_type=F32)
        xv = x_ref[...]
        r = lax.rsqrt(jnp.mean(xv * xv, axis=-1, keepdims=True) + EPS)
        xhat = xv * r
        u = dh * g_ref[...]
        dx = r * (u - xhat * jnp.mean(u * xhat, axis=-1, keepdims=True)) + r_ref[...].astype(F32)
        o_ref[...] = dx.astype(o_ref.dtype)
        part = jnp.sum(dh * xhat, axis=0, keepdims=True)

        @pl.when(pl.program_id(0) == 0)
        def _():
            dg_ref[...] = part

        @pl.when(pl.program_id(0) > 0)
        def _():
            dg_ref[...] += part

    tile = pl.BlockSpec((tm, kdim), lambda i: (i, 0))
    vec = pl.BlockSpec((1, kdim), lambda i: (0, 0))
    return _carry(
        riders, body, name=name, grid=(s_len // tm,),
        in_specs=[pl.BlockSpec((p.shape[0], tm, p.shape[2]), lambda i: (0, i, 0)) for p, _ in terms]
        + [pl.BlockSpec(w.shape, lambda i: (0, 0, 0), pipeline_mode=pl.Buffered(1)) for _, w in terms]
        + [tile, vec, tile],
        out_specs=[tile, vec],
        out_shape=[jax.ShapeDtypeStruct((s_len, kdim), out_dtype), jax.ShapeDtypeStruct((1, kdim), F32)],
        semantics=("arbitrary",),
    )(*[p for p, _ in terms], *[w for _, w in terms], x, g, dres)


def _local_grads(x, target, norm1_g, q_norm_g, k_norm_g, rel_bias, conv_b, b_gate, norm2_g, comm):
    s_len = x.shape[0]
    tm = min(1024, s_len)
    tb = min(2048, s_len)
    row = lambda v: v.reshape(1, -1)

    def carrying(at, fn, *args, **kw):
        riders = comm.riders(at)
        out = fn(*args, riders=riders, **kw)
        if riders:
            out, carried = out
            comm.done(at, carried)
        return out

    bias = carrying("bias_expand", _bias_expand, rel_bias)
    h = carrying("norm1", _rms_fwd, "norm1", x, row(norm1_g))
    w_in3, conv_w = comm.early_weights()
    gq, gk = row(q_norm_g), row(k_norm_g)

    def qk_epi(acc, e, o):
        o[0][...] = acc.astype(BF16)
        gain = jnp.where(pl.program_id(1) < 2, e[0] * QK_SCALE, e[1])
        o[1][...] = (_head_unit(acc)[0] * gain).astype(BF16)
    conv_in = carrying("proj_conv", _mm_fwd, "proj_conv", h, w_in3, tb, 1536, D_MODEL, col0=2, ncols=3 * D_MODEL,
                       outs=[_out2d(s_len, 3 * D_MODEL, BF16, tb, 1536)])
    small = pl.BlockSpec((1, 512), lambda i, j, k: (0, 0))
    per_head = lambda g: jnp.tile(g, (1, 512 // HEAD_DIM))
    qk_raw, qkn = carrying("proj_qk", _mm_fwd, "proj_qk", h, w_in3, tm, 512, D_MODEL, ncols=2 * D_MODEL,
                           epilogue=qk_epi, outs=[_out2d(s_len, 2 * D_MODEL, BF16, tm, 512)] * 2,
                           extras=[(per_head(gq), small), (per_head(gk), small)])
    v = carrying("proj_v", _mm_fwd, "proj_v", h, w_in3, tb, 512, D_MODEL, col0=4, ncols=D_MODEL,
                 outs=[_out2d(s_len, D_MODEL, BF16, tb, 512)])
    w_gate3 = comm.gate_weight()

    def gate_epi(acc, e, o):
        o[0][...] = jax.nn.sigmoid(acc + e[0]).astype(BF16)
    gates = carrying("gates", _mm_fwd, "gates", h, w_gate3, tb, 512, D_MODEL, epilogue=gate_epi,
                     outs=[_out2d(s_len, 2 * D_MODEL, BF16, tb, 512)],
                     extras=[(row(b_gate), pl.BlockSpec((1, 512), lambda i, j, k: (0, j)))])

    attn, lse = carrying("attn_fwd", _attn_fwd, qkn, v, bias)
    yconv = carrying("conv_fwd", _conv_fwd, conv_in, conv_w, row(conv_b))
    w_ap, w_cp, w_out, w_up3, w_down = comm.late_weights()
    tw = 1024
    ya = _mm_fwd("attn_proj", attn, w_ap, tm, tw, D_MODEL, outs=[_out2d(s_len, D_MODEL, BF16, tm, tw)])

    def merge_epi(acc, e, o):
        ya_v, ga, gc = [t.astype(F32) for t in e]
        o[0][...] = acc.astype(BF16)
        o[1][...] = (ga * ya_v + gc * acc).astype(BF16)
    gate_a, gate_c = (gates, _tile_spec(tm, tw, 0)), (gates, _tile_spec(tm, tw, 1))
    yc, merged = _mm_fwd("conv_proj", yconv, w_cp, tm, tw, D_MODEL, epilogue=merge_epi,
                         outs=[_out2d(s_len, D_MODEL, BF16, tm, tw), _out2d(s_len, D_MODEL, BF16, tm, tw)],
                         extras=[(ya, _tile_spec(tm, tw)), gate_a, gate_c])

    def res_epi(acc, e, o):
        x1_v = e[0] + acc
        o[0][...] = x1_v
        r = lax.rsqrt(jnp.mean(x1_v * x1_v, axis=-1, keepdims=True) + EPS)
        o[1][...] = (x1_v * r * e[1]).astype(BF16)
    assert tw == D_MODEL
    x1, h2 = _mm_fwd("out_proj", merged, w_out, tm, tw, D_MODEL, epilogue=res_epi,
                     outs=[_out2d(s_len, D_MODEL, F32, tm, tw), _out2d(s_len, D_MODEL, BF16, tm, tw)],
                     extras=[(x, _tile_spec(tm, tw)), (row(norm2_g), pl.BlockSpec((1, tw), lambda i, j, k: (0, 0)))])

    def up_epi(acc, e, o):
        o[0][...] = jnp.square(jnp.maximum(acc, 0.0)).astype(BF16)
    act = _mm_fwd("mlp_up", h2, w_up3, tb, tw, D_MODEL, epilogue=up_epi, outs=[_out2d(s_len, D_FF, BF16, tb, tw)])

    def loss_epi(acc, e, o):
        err = e[0] + acc - e[1]
        o[0][...] = (err * (1.0 / D_MODEL)).astype(BF16)
        sq = err * err
        part = sq[:, 0:LANE]
        for c0 in range(LANE, D_MODEL, LANE):
            part = part + sq[:, c0:c0 + LANE]
        o[1][...] = jnp.sum(part.reshape(tl // 8, 8, LANE), axis=0)
    tl = 512
    dy_b, loss_part = _mm_fwd(
        "mlp_down", act, w_down, tl, D_MODEL, D_FF, epilogue=loss_epi,
        outs=[_out2d(s_len, D_MODEL, BF16, tl, D_MODEL),
              (jax.ShapeDtypeStruct((8 * (s_len // tl), LANE), F32), pl.BlockSpec((8, LANE), lambda i, j, k: (i, 0)))],
        extras=[(x1, _tile_spec(tl, D_MODEL)), (target, _tile_spec(tl, D_MODEL))])

    def dup_epi(acc, e, o):
        o[0][...] = (acc * (2.0 * jnp.sqrt(e[0].astype(F32)))).astype(BF16)
    full = lambda cols: pl.BlockSpec((tm, cols), lambda i, j, n: (i, n))
    tokens = lambda cols: pl.BlockSpec((s_len, cols), lambda i, j, m: (m, j))
    dup = _mm_bwd_x("d_act", dy_b, full(D_MODEL), w_down, tm, tw, D_MODEL, s_len, D_MODEL, epilogue=dup_epi,
                    outs=[_out2d(s_len, D_FF, BF16, tm, tw)], extras=[(act, _tile_spec(tm, tw))])
    g_down = _mm_bwd_w("g_down", act, dy_b, tokens(D_MODEL), D_MODEL, 512, D_MODEL, s_len, False)
    g_up = _mm_bwd_w("g_up", h2, dup, tokens(512), D_FF, D_MODEL, 512, s_len, True)
    comm.grads_ready("mlp", dict(w_down=g_down, w_up=g_up))
    dx1_b, dg2 = carrying("d_h2", _d_norm_input, "d_h2", [(dup.reshape(1, s_len, D_FF), w_up3)],
                          x1, row(norm2_g), dy_b, BF16)

    def dmerge_epi(acc, e, o):
        ya_v, yc_v, ga, gc = [t.astype(F32) for t in e]
        o[0][...] = (acc * ga).astype(BF16)
        o[1][...] = (acc * gc).astype(BF16)
        o[2][0] = (acc * ya_v * ga * (1.0 - ga)).astype(BF16)
        o[2][1] = (acc * yc_v * gc * (1.0 - gc)).astype(BF16)
    dya, dyc, dgp2 = _mm_bwd_x(
        "d_merged", dx1_b, full(D_MODEL), w_out, tm, tw, D_MODEL, s_len, D_MODEL, epilogue=dmerge_epi,
        outs=[_out2d(s_len, D_MODEL, BF16, tm, tw), _out2d(s_len, D_MODEL, BF16, tm, tw),
              (jax.ShapeDtypeStruct((2, s_len, D_MODEL), BF16), pl.BlockSpec((2, tm, tw), lambda i, j, n: (0, i, j)))],
        extras=[(ya, _tile_spec(tm, tw)), (yc, _tile_spec(tm, tw)), gate_a, gate_c])
    g_out = _mm_bwd_w("g_out", merged, dx1_b, tokens(512), D_MODEL, D_MODEL, 512, s_len, False)
    d_attn = _mm_bwd_x("d_attn", dya, full(D_MODEL), w_ap, tm, tw, D_MODEL, s_len, D_MODEL,
                       outs=[_out2d(s_len, D_MODEL, BF16, tm, tw)])
    g_ap = _mm_bwd_w("g_attn_proj", attn, dya, tokens(512), D_MODEL, D_MODEL, 512, s_len, False)
    d_yconv = _mm_bwd_x("d_yconv", dyc, full(D_MODEL), w_cp, tm, tw, D_MODEL, s_len, D_MODEL,
                        outs=[_out2d(s_len, D_MODEL, BF16, tm, tw)])
    g_cp = _mm_bwd_w("g_conv_proj", yconv, dyc, tokens(512), D_MODEL, D_MODEL, 512, s_len, False)
    piece = lambda width: (lambda blk: (blk * width) // D_MODEL, lambda blk: (blk * width % D_MODEL) // width)
    pc, cb = piece(512)
    pieces = pl.BlockSpec((None, s_len, 512), lambda i, j, m: (pc(j), m, cb(j)))
    g_gate = _mm_bwd_w("g_gate", h, dgp2, pieces, 2 * D_MODEL, D_MODEL, 512, s_len, True)
    comm.grads_ready("proj", dict(w_out=g_out, w_attn_proj=g_ap, w_conv_proj=g_cp))
    comm.grads_ready("gate_g", dict(w_gate=g_gate))

    dproj6, dbias = carrying("attn_bwd", _attn_bwd, qkn, v, attn, d_attn, bias, lse)
    dproj6, dgq, dgk = carrying("qk_norm_bwd", _qk_norm_bwd, dproj6, qk_raw, gq, gk)
    dproj6, dconv_wb = carrying("conv_bwd", _conv_bwd, dproj6, d_yconv, conv_in, conv_w, row(conv_b))

    g_in = carrying("g_in", _mm_bwd_w, "g_in", h, dproj6, pieces, 6 * D_MODEL, D_MODEL, 512, s_len, True)
    comm.grads_ready("in", dict(w_in=g_in))
    d_rel = carrying("bias_reduce", _bias_reduce, dbias)
    grad_x, dg1 = carrying("d_h", _d_norm_input, "d_h", [(dproj6, w_in3), (dgp2, w_gate3)],
                           x, row(norm1_g), dx1_b, F32)

    def bsum(t, f):
        return [], [t[0].astype(F32), t[1].astype(F32)]
    ts = 512
    db_a, db_c = carrying("b_gate_sum", _ew, "b_gate_sum", bsum,
                          [(dgp2, pl.BlockSpec((None, ts, D_MODEL), lambda i: (0, i, 0))),
                           (dgp2, pl.BlockSpec((None, ts, D_MODEL), lambda i: (1, i, 0)))],
                          [], [], sums=[D_MODEL, D_MODEL], ts=ts)

    big = dict(w_in=g_in, w_attn_proj=g_ap, w_conv_proj=g_cp, w_gate=g_gate, w_out=g_out,
               w_up=g_up, w_down=g_down)
    small = dict(norm1_g=dg1, norm2_g=dg2, conv_wb=dconv_wb, b_gate=(db_a, db_c),
                 q_norm_g=dgq, k_norm_g=dgk, rel_bias=d_rel)
    return loss_part, grad_x, big, small


def _finish_loss(loss_part):
    def body(l_ref, lo_ref):
        total = jnp.sum(jnp.sum(l_ref[...], axis=0, keepdims=True), axis=1, keepdims=True)
        lo_ref[...] = jnp.broadcast_to(total * (0.5 / D_MODEL), lo_ref.shape)

    return pl.pallas_call(body, name="finish_loss", out_shape=jax.ShapeDtypeStruct((8, LANE), F32))(loss_part)


def _place():
    return lax.axis_index("x"), lax.axis_index("y"), lax.axis_index("c")


def _other_chips(x, y):
    return [(1 - x, y), (x, 1 - y), (1 - x, 1 - y)]


def _cast_into_slot(name, where, w):
    r, cols = w.shape
    ts = 256

    def body(w_ref, x_ref, o_ref):
        o_ref[...] = x_ref[...].astype(o_ref.dtype)

    return pl.pallas_call(
        body, name=name,
        grid_spec=pltpu.PrefetchScalarGridSpec(
            num_scalar_prefetch=1, grid=(r // ts,),
            in_specs=[pl.BlockSpec((ts, cols), lambda i, w: (i, 0))],
            out_specs=pl.BlockSpec((None, ts, cols), lambda i, w: (w[0], i, 0))),
        out_shape=jax.ShapeDtypeStruct((N_SHARD, r, cols), BF16),
        compiler_params=_params("parallel"),
    )(where, w)


def _remote(src, dst, send, recv, k, to):
    return pltpu.make_async_remote_copy(src_ref=src, dst_ref=dst, send_sem=send.at[k], recv_sem=recv.at[k],
                                        device_id=to, device_id_type=MESH)


def _gather_riders(slots):
    n = len(slots)

    def copy(refs, w, j, shard, which, send, recv, to):
        hr = slots[w].shape[1] // 2
        ref = refs[w].at[shard, pl.ds(which * hr, hr)]
        return _remote(ref, ref, send, recv, 3 * w + j, to)

    def for_each_peer(fn):
        x, y, c = _place()
        for w in range(n):
            for j, chip in enumerate(_other_chips(x, y)):
                fn(w, j, 2 * x + y, 2 * chip[0] + chip[1], c, (*chip, c), (x, y, 1 - c))

    def chips_start(_, refs, send, recv):
        for_each_peer(lambda w, j, mine, theirs, c, peer, sib: copy(refs, w, j, mine, c, send, recv, peer).start())

    def chips_finish(_, refs, send, recv):
        for_each_peer(lambda w, j, mine, theirs, c, peer, sib: copy(refs, w, j, theirs, c, send, recv, peer).wait_recv())
        for_each_peer(lambda w, j, mine, theirs, c, peer, sib: copy(refs, w, j, mine, c, send, recv, peer).wait_send())

    def sibling_start(_, refs, send, recv):
        for_each_peer(lambda w, j, mine, theirs, c, peer, sib: copy(refs, w, j, theirs, c, send, recv, sib).start())

    def sibling_finish(_, refs, send, recv):
        for_each_peer(lambda w, j, mine, theirs, c, peer, sib: copy(refs, w, j, theirs, 1 - c, send, recv, sib).wait_recv())
        for_each_peer(lambda w, j, mine, theirs, c, peer, sib: copy(refs, w, j, theirs, c, send, recv, sib).wait_send())

    return (lambda arrays: _Rider([], arrays, 3 * n, chips_start, chips_finish),
            lambda arrays: _Rider([], arrays, 3 * n, sibling_start, sibling_finish))


def _pair_exchange_rider(grads, landing):
    n = len(grads)

    def copies(srcs, dsts, send, recv):
        x, y, c = _place()
        out = []
        for w in range(n):
            hr = grads[w].shape[1] // 2
            out.append(_remote(srcs[w].at[:, pl.ds((1 - c) * hr, hr)], dsts[w], send, recv, w, (x, y, 1 - c)))
        return out

    def start(srcs, dsts, send, recv):
        for cp in copies(srcs, dsts, send, recv):
            cp.start()

    def finish(srcs, dsts, send, recv):
        for cp in copies(srcs, dsts, send, recv):
            cp.wait()

    return _Rider(grads, landing, n, start, finish)


def _row_tile(hr):
    return min(hr, 256)


def _pair_add(name, where, grad, got):
    _, hr, cols = got.shape
    tr = _row_tile(hr)
    nblk = hr // tr

    def body(w_ref, g_ref, r_ref, o_ref):
        o_ref[...] = (g_ref[...] + r_ref[...]).astype(o_ref.dtype)

    other = lambda k, w: (w[0] + 1 + k) % N_SHARD
    return pl.pallas_call(
        body, name=name,
        grid_spec=pltpu.PrefetchScalarGridSpec(
            num_scalar_prefetch=1, grid=(N_SHARD - 1, nblk),
            in_specs=[pl.BlockSpec((None, tr, cols), lambda k, i, w: (other(k, w), w[1] * nblk + i, 0)),
                      pl.BlockSpec((None, tr, cols), lambda k, i, w: (other(k, w), i, 0))],
            out_specs=pl.BlockSpec((None, tr, cols), lambda k, i, w: (other(k, w), i, 0))),
        out_shape=jax.ShapeDtypeStruct(got.shape, BF16),
        compiler_params=_params("parallel", "parallel"),
    )(where, grad, got)


def _chip_exchange_rider(partials, landing):
    n = len(partials)

    def copies(srcs, dsts, send, recv):
        x, y, c = _place()
        return [_remote(srcs[w].at[2 * chip[0] + chip[1]], dsts[w].at[j], send, recv, 3 * w + j, (*chip, c))
                for w in range(n) for j, chip in enumerate(_other_chips(x, y))]

    def start(srcs, dsts, send, recv):
        for cp in copies(srcs, dsts, send, recv):
            cp.start()

    def finish(srcs, dsts, send, recv):
        for cp in copies(srcs, dsts, send, recv):
            cp.wait()

    return _Rider(partials, landing, 3 * n, start, finish)


def _final_add(name, where, grad, got, arrived):
    _, hr, cols = got.shape
    tr = _row_tile(hr)
    nblk = hr // tr

    def body(w_ref, g_ref, r_ref, a_ref, o_ref):
        acc = g_ref[...] + r_ref[...]
        for j in range(3):
            acc = acc + a_ref[j].astype(F32)
        o_ref[...] = acc

    return pl.pallas_call(
        body, name=name,
        grid_spec=pltpu.PrefetchScalarGridSpec(
            num_scalar_prefetch=1, grid=(nblk,),
            in_specs=[pl.BlockSpec((None, tr, cols), lambda i, w: (w[0], w[1] * nblk + i, 0)),
                      pl.BlockSpec((None, tr, cols), lambda i, w: (w[0], i, 0)),
                      pl.BlockSpec((3, tr, cols), lambda i, w: (0, i, 0))],
            out_specs=pl.BlockSpec((tr, cols), lambda i, w: (w[1] * nblk + i, 0))),
        out_shape=jax.ShapeDtypeStruct((2 * hr, cols), F32),
        compiler_params=_params("parallel"),
    )(where, grad, got, arrived)


def _pair_share_rider(shards):
    n = len(shards)

    def half(refs, w, which):
        hr = shards[w].shape[0] // 2
        return refs[w].at[pl.ds(which * hr, hr)]

    def start(_, refs, send, recv):
        x, y, c = _place()
        for w in range(n):
            _remote(half(refs, w, c), half(refs, w, c), send, recv, w, (x, y, 1 - c)).start()

    def finish(_, refs, send, recv):
        x, y, c = _place()
        for w in range(n):
            _remote(half(refs, w, 1 - c), half(refs, w, 1 - c), send, recv, w, (x, y, 1 - c)).wait_recv()
        for w in range(n):
            _remote(half(refs, w, c), half(refs, w, c), send, recv, w, (x, y, 1 - c)).wait_send()

    return _Rider([], shards, n, start, finish)


class _Exchange:
    PLAN = {"bias_expand": [("early", "gather")], "norm1": [("early", "forward")],
            "proj_conv": [("gate", "gather")], "proj_v": [("gate", "forward")],
            "proj_qk": [("late", "gather")], "gates": [("late", "forward")],
            "attn_fwd": [("mlp_w", "gather")], "conv_fwd": [("mlp_w", "forward")],
            "d_h2": [("mlp", "pair")], "attn_bwd": [("mlp", "chips"), ("proj", "pair"), ("gate_g", "pair")],
            "qk_norm_bwd": [("gate_g", "chips")], "conv_bwd": [("mlp", "share")],
            "g_in": [("proj", "chips")], "bias_reduce": [("proj", "share"), ("gate_g", "share"), ("in", "pair")],
            "d_h": [("in", "chips")], "b_gate_sum": [("in", "share")]}

    def __init__(self, where, early_slots, gate_slots, late_slots):
        self.where = where
        self.slots = dict(early=early_slots, gate=gate_slots, late=late_slots[:3], mlp_w=late_slots[3:])
        self.stage = {g: dict(zip(("gather", "forward"), _gather_riders(s))) for g, s in self.slots.items()}
        self.groups, self.reduced, self.pending = {}, {}, []

    def early_weights(self):
        w_in3, small = self.slots["early"]
        conv_w = small[:, :3, :].transpose(1, 0, 2).reshape(3, N_SHARD * small.shape[2])
        return w_in3, conv_w

    def gate_weight(self):
        return self.slots["gate"][0]

    def late_weights(self):
        rows = lambda a: a.reshape(a.shape[0] * a.shape[1], a.shape[2])
        w_ap, w_cp, w_out = self.slots["late"]
        w_up3, w_down = self.slots["mlp_w"]
        return rows(w_ap), rows(w_cp), rows(w_out), w_up3, rows(w_down)

    def grads_ready(self, group, grads):
        names = list(grads)
        g4 = [g if g.ndim == 3 else g.reshape(N_SHARD, -1, g.shape[1]) for g in grads.values()]
        self.groups[group] = dict(names=names, g4=g4)

    def riders(self, at):
        self.pending = self.PLAN.get(at, [])
        out = []
        for group, stage in self.pending:
            if group in self.slots:
                out.append(self.stage[group][stage](self.slots[group]))
                continue
            st = self.groups[group]
            if stage == "pair":
                landing = [lax.empty((N_SHARD, g.shape[1] // 2, g.shape[2]), F32) for g in st["g4"]]
                out.append(_pair_exchange_rider(st["g4"], landing))
            elif stage == "chips":
                landing = [lax.empty((3,) + p.shape[1:], p.dtype) for p in st["partial"]]
                out.append(_chip_exchange_rider(st["partial"], landing))
            else:
                out.append(_pair_share_rider(st["halves"]))
        return out

    def done(self, at, carried):
        for (group, stage), arrays in zip(self.pending, carried):
            if group in self.slots:
                self.slots[group] = arrays
                continue
            st = self.groups[group]
            tag = lambda what, n: what + "_" + n
            if stage == "pair":
                st["got"] = arrays
                st["partial"] = [_pair_add(tag("pair_add", n), self.where, g, r)
                                 for n, g, r in zip(st["names"], st["g4"], arrays)]
            elif stage == "chips":
                st["halves"] = [_final_add(tag("final_add", n), self.where, g, r, a)
                                for n, g, r, a in zip(st["names"], st["g4"], st["got"], arrays)]
            else:
                self.reduced.update(zip(st["names"], arrays))


SMALL_ROWS = 32
N_DEV = 8


def _all_reduce_small(pack):
    def body(p_ref, o_ref, buf, send, recv):
        x, y, c = _place()
        buf[4 * x + 2 * y + c] = p_ref[...]
        copies, waits = [], []
        for k in range(1, N_DEV):
            px = 1 - x if k & 4 else x
            py = 1 - y if k & 2 else y
            pc = 1 - c if k & 1 else c
            copies.append(_remote(p_ref, buf.at[4 * x + 2 * y + c], send, recv, k - 1, (px, py, pc)))
            waits.append(_remote(p_ref, buf.at[4 * px + 2 * py + pc], send, recv, k - 1, (px, py, pc)))
        for cp in copies:
            cp.start()
        for cp in waits:
            cp.wait_recv()
        acc = buf[0]
        for d in range(1, N_DEV):
            acc = acc + buf[d]
        o_ref[...] = acc
        for cp in copies:
            cp.wait_send()

    return pl.pallas_call(
        body, name="all_reduce_small",
        out_shape=jax.ShapeDtypeStruct(pack.shape, F32),
        scratch_shapes=[pltpu.VMEM((N_DEV,) + pack.shape, F32),
                        pltpu.SemaphoreType.DMA((N_DEV - 1,)), pltpu.SemaphoreType.DMA((N_DEV - 1,))],
    )(pack)


def _adamw(name, w, g, m, v):
    c1 = 1.0 - ADAM_B1 ** ADAM_STEP
    c2 = 1.0 - ADAM_B2 ** ADAM_STEP

    def fn(t, f):
        wv, gv, mv, vv = t
        m2 = ADAM_B1 * mv + (1.0 - ADAM_B1) * gv
        v2 = ADAM_B2 * vv + (1.0 - ADAM_B2) * (gv * gv)
        delta = -ADAM_LR * ((m2 / c1) / (jnp.sqrt(v2 / c2) + ADAM_EPS) + ADAM_WD * wv)
        return [delta, m2, v2], []

    cols = w.shape[1]
    return _ew(name, fn, [w, g, m, v], [], [(cols, F32)] * 3, ts=min(w.shape[0], 256))


LOSS_ROW = 26


def _pack_small(norm1_g, norm2_g, conv_b, b_gate, conv_w, q_norm_g, k_norm_g, rel_bias, loss=None):
    pack = jnp.zeros((SMALL_ROWS, D_MODEL), F32)
    for r0, v in ((0, norm1_g), (1, norm2_g), (2, conv_b), (3, b_gate.reshape(2, D_MODEL)), (5, conv_w),
                  (8, q_norm_g), (9, k_norm_g), (10, rel_bias)) + (((LOSS_ROW, loss),) if loss is not None else ()):
        v = v.reshape(-1, v.shape[-1]).astype(F32)
        pack = pack.at[r0:r0 + v.shape[0], :v.shape[1]].set(v)
    return pack


def _unpack_small(pack, conv_cols):
    return dict(norm1_g=pack[0], norm2_g=pack[1], conv_b=pack[2], b_gate=pack[3:5].reshape(2 * D_MODEL),
                conv_w=pack[5:8, :conv_cols], q_norm_g=pack[8, :HEAD_DIM], k_norm_g=pack[9, :HEAD_DIM],
                rel_bias=pack[10:10 + N_HEADS, :N_REL])


BIG = ["w_in", "w_attn_proj", "w_conv_proj", "w_gate", "w_out", "w_up", "w_down"]
LATE = ["w_attn_proj", "w_conv_proj", "w_out", "w_up", "w_down"]
WEIGHTS = ["norm1_g", "w_in", "q_norm_g", "k_norm_g", "rel_bias", "conv_w", "conv_b", "w_attn_proj",
           "w_conv_proj", "w_gate", "b_gate", "w_out", "norm2_g", "w_up", "w_down"]


def kernel(x, norm1_g, w_in, q_norm_g, k_norm_g, rel_bias, conv_w, conv_b, w_attn_proj, w_conv_proj, w_gate, b_gate, w_out, norm2_g, w_up, w_down, loss_target, m_norm1_g, m_w_in, m_q_norm_g, m_k_norm_g, m_rel_bias, m_conv_w, m_conv_b, m_w_attn_proj, m_w_conv_proj, m_w_gate, m_b_gate, m_w_out, m_norm2_g, m_w_up, m_w_down, v_norm1_g, v_w_in, v_q_norm_g, v_k_norm_g, v_rel_bias, v_conv_w, v_conv_b, v_w_attn_proj, v_w_conv_proj, v_w_gate, v_b_gate, v_w_out, v_norm2_g, v_w_up, v_w_down):
    given = dict(locals())
    w = {n: given[n] for n in WEIGHTS}
    m = {n: given["m_" + n] for n in WEIGHTS}
    v = {n: given["v_" + n] for n in WEIGHTS}
    s_len = x.shape[1]
    shard = 2 * lax.axis_index("x") + lax.axis_index("y")
    where = jnp.stack([shard, lax.axis_index("c")]).astype(jnp.int32)
    conv_cols = conv_w.shape[1]

    small_in = lax.dynamic_update_slice(jnp.zeros((N_SHARD, 16, conv_cols), F32), conv_w[None], (shard, 0, 0))
    slot = {n: _cast_into_slot("cast_" + n, where, w[n]) for n in BIG}
    comm = _Exchange(where, [slot["w_in"], small_in], [slot["w_gate"]], [slot[n] for n in LATE])

    loss_part, grad_x, _, small = _local_grads(
        x.reshape(s_len, D_MODEL), loss_target.reshape(s_len, D_MODEL), norm1_g, q_norm_g, k_norm_g,
        rel_bias, conv_b, b_gate, norm2_g, comm)
    grad = dict(comm.reduced)

    loss_local = _finish_loss(loss_part)
    pack = _pack_small(small["norm1_g"], small["norm2_g"], small["conv_wb"][3], jnp.concatenate(small["b_gate"], axis=1),
                       small["conv_wb"][0:3], small["q_norm_g"], small["k_norm_g"], small["rel_bias"],
                       loss=loss_local[0:1, :])
    total = _all_reduce_small(pack)
    g_small = _unpack_small(total, D_MODEL)
    g_small["conv_w"] = lax.dynamic_slice(g_small["conv_w"], (0, shard * conv_cols), (3, conv_cols))
    grad.update(g_small)

    delta, new_m, new_v = {}, {}, {}
    for n in BIG:
        delta[n], new_m[n], new_v[n] = _adamw("adamw_" + n, w[n], grad[n], m[n], v[n])
    small_names = [n for n in WEIGHTS if n not in BIG]
    packs = [_pack_small(**{n: src[n] for n in small_names}) for src in (w, grad, m, v)]
    for out, packed in zip((delta, new_m, new_v), _adamw("adamw_small", *packs)):
        out.update({n: a.reshape(w[n].shape) for n, a in _unpack_small(packed, conv_cols).items()})

    outs = [total[LOSS_ROW, 0], grad_x.reshape(x.shape)]
    for group in (grad, delta, new_m, new_v):
        outs += [group[n].reshape(w[n].shape) for n in WEIGHTS]
    return tuple(outs)
```

```python
import jax
import jax.numpy as jnp
from jax import lax
from jax.experimental import pallas as pl
from jax.experimental.pallas import tpu as pltpu

F32 = jnp.float32
BF16 = jnp.bfloat16

D_MODEL = 1024
N_HEADS = 16
HEAD_DIM = 64
CHUNK = 64
N_PREV_CHUNKS = 8
MAX_REL = 256
D_FF = 4096
N_REL = 2 * MAX_REL + 1
REL_PAD = 640
EPS = 1e-6
NEG_INF = -1e30
QK_SCALE = HEAD_DIM ** -0.5

SUPER = 4 * CHUNK
BAND = SUPER + N_PREV_CHUNKS * CHUNK
SKEW_W = 1024
N_SHARD = 4
LANE = 128
MXU_DIM = 256
VMEM_LIMIT = 48 * 1024 * 1024

ADAM_LR = 0.001
ADAM_B1 = 0.9
ADAM_B2 = 0.999
ADAM_EPS = 1e-08
ADAM_WD = 0.01
ADAM_STEP = 10

MESH = pl.DeviceIdType.MESH
NN = (((1,), (0,)), ((), ()))
NT = (((1,), (1,)), ((), ()))
TN = (((0,), (0,)), ((), ()))


def _params(*sem):
    return pltpu.CompilerParams(dimension_semantics=sem or None, vmem_limit_bytes=VMEM_LIMIT)


HBM_SPEC = pl.BlockSpec(memory_space=pl.ANY)


class _Rider:
    def __init__(self, sources, arrays, n_sem, start, finish):
        self.sources, self.arrays, self.n_sem, self.start, self.finish = sources, arrays, n_sem, start, finish


def _carry(riders, body, *, name, out_shape, grid=(), in_specs=None, out_specs=None, scratch_shapes=(),
           semantics=(), input_output_aliases=None):
    aliases = dict(input_output_aliases or {})
    if not riders:
        kw = {} if in_specs is None else dict(in_specs=in_specs, out_specs=out_specs)
        return pl.pallas_call(body, name=name, grid=grid, out_shape=out_shape, scratch_shapes=scratch_shapes,
                              input_output_aliases=aliases, compiler_params=_params(*semantics), **kw)
    single = not isinstance(out_shape, (list, tuple))
    shapes = [out_shape] if single else list(out_shape)
    n_out, n_scr = len(shapes), len(scratch_shapes)
    in_hbm = lambda a: pltpu.with_memory_space_constraint(a, pltpu.HBM)
    srcs = [in_hbm(a) for r in riders for a in r.sources]
    arrs = [in_hbm(a) for r in riders for a in r.arrays]
    vmem = pl.BlockSpec(memory_space=pltpu.VMEM)

    def run(*args):
        n_in = len(args)

        def wrapped(*refs):
            pos = n_in
            src_refs = refs[pos:pos + len(srcs)]
            pos += len(srcs) + len(arrs)
            outs = refs[pos:pos + n_out]
            pos += n_out
            arr_refs = refs[pos:pos + len(arrs)]
            pos += len(arrs)
            scratch = refs[pos:pos + n_scr]
            sems = refs[pos + n_scr:]
            first, last = True, True
            for d, size in enumerate(grid):
                first = jnp.logical_and(first, pl.program_id(d) == 0)
                last = jnp.logical_and(last, pl.program_id(d) == size - 1)

            def each(method):
                s0 = a0 = 0
                for k, r in enumerate(riders):
                    getattr(r, method)(src_refs[s0:s0 + len(r.sources)], arr_refs[a0:a0 + len(r.arrays)],
                                       sems[2 * k], sems[2 * k + 1])
                    s0, a0 = s0 + len(r.sources), a0 + len(r.arrays)

            pl.when(first)(lambda: each("start"))
            body(*refs[:n_in], *outs, *scratch)
            pl.when(last)(lambda: each("finish"))

        ins = [vmem] * n_in if in_specs is None else list(in_specs)
        if out_specs is None:
            o_specs = [vmem] * n_out
        else:
            o_specs = [out_specs] if single else list(out_specs)
        for k in range(len(arrs)):
            aliases[n_in + len(srcs) + k] = n_out + k
        res = pl.pallas_call(
            wrapped, name=name, grid=grid,
            in_specs=ins + [HBM_SPEC] * (len(srcs) + len(arrs)),
            out_specs=o_specs + [HBM_SPEC] * len(arrs),
            out_shape=shapes + [jax.ShapeDtypeStruct(a.shape, a.dtype) for a in arrs],
            scratch_shapes=list(scratch_shapes) + [pltpu.SemaphoreType.DMA((r.n_sem,)) for r in riders for _ in range(2)],
            input_output_aliases=aliases,
            compiler_params=_params(*["arbitrary"] * len(grid)),
        )(*args, *srcs, *arrs)
        core, rest = res[:n_out], list(res[n_out:])
        carried, a0 = [], 0
        for r in riders:
            carried.append(rest[a0:a0 + len(r.arrays)])
            a0 += len(r.arrays)
        return (core[0] if single else core), carried

    return run


def _mm(name, dims, a, a_spec, b, b_spec, grid, tile, outs, epilogue=None, extras=(), riders=()):
    nk, ne, no = grid[2], len(extras), len(outs)

    def body(a_ref, b_ref, *refs):
        e_refs, o_refs = refs[:ne], refs[ne:ne + no]
        part = lax.dot_general(a_ref[...], b_ref[...], dims, preferred_element_type=F32)

        def finish(acc):
            if epilogue is None:
                o_refs[0][...] = acc.astype(o_refs[0].dtype)
            else:
                epilogue(acc, [r[...] for r in e_refs], o_refs)

        if nk == 1:
            finish(part)
        else:
            acc_ref = refs[ne + no]
            k = pl.program_id(2)

            @pl.when(k == 0)
            def _():
                acc_ref[...] = part

            @pl.when(k > 0)
            def _():
                acc_ref[...] += part

            @pl.when(k == nk - 1)
            def _():
                finish(acc_ref[...])

    res = _carry(
        riders, body, name=name, grid=grid,
        in_specs=[a_spec, b_spec] + [s for _, s in extras],
        out_specs=[s for _, s in outs],
        out_shape=[s for s, _ in outs],
        scratch_shapes=[pltpu.VMEM(tile, F32)] if nk > 1 else [],
        semantics=("parallel", "parallel", "arbitrary"),
    )(a, b, *[e for e, _ in extras])
    res, carried = res if riders else (res, None)
    res = res[0] if no == 1 else res
    return (res, carried) if riders else res


def _tile_spec(tm, tn, col0=0):
    return pl.BlockSpec((tm, tn), lambda i, j, k: (i, j + col0))


def _out2d(m, n, dtype, tm, tn):
    return (jax.ShapeDtypeStruct((m, n), dtype), _tile_spec(tm, tn))


def _mm_fwd(name, a, w, tm, tn, tk, outs=None, epilogue=None, extras=(), col0=0, ncols=None, riders=()):
    m, kdim = a.shape
    if w.ndim == 3:
        per = w.shape[2] // tn
        n = ncols or N_SHARD * w.shape[2]
        w_spec = pl.BlockSpec((None, tk, tn), lambda i, j, k: ((j + col0) // per, k, (j + col0) % per))
    else:
        n = ncols or w.shape[1]
        w_spec = pl.BlockSpec((tk, tn), lambda i, j, k: (k, j + col0))
    if outs is None:
        outs = [_out2d(m, n, F32, tm, tn)]
    return _mm(name, NN, a, pl.BlockSpec((tm, tk), lambda i, j, k: (i, k)), w, w_spec,
               (m // tm, n // tn, kdim // tk), (tm, tn), outs, epilogue, extras, riders)


def _mm_bwd_x(name, g, g_spec, w, tm, tj, tc, m, n_contract, outs=None, epilogue=None, extras=(), riders=()):
    if w.ndim == 3:
        per = w.shape[2] // tc
        kdim = w.shape[1]
        w_spec = pl.BlockSpec((None, tj, tc), lambda i, j, n: (n // per, j, n % per))
    else:
        kdim = w.shape[0]
        w_spec = pl.BlockSpec((tj, tc), lambda i, j, n: (j, n))
    if outs is None:
        outs = [_out2d(m, kdim, F32, tm, tj)]
    return _mm(name, NT, g, g_spec, w, w_spec, (m // tm, kdim // tj, n_contract // tc),
               (tm, tj), outs, epilogue, extras, riders)


def _mm_bwd_w(name, a, g, g_spec, n, tk, tn, tm, sharded, riders=()):
    m, kdim = a.shape
    if sharded:
        per = (n // N_SHARD) // tn
        out = (jax.ShapeDtypeStruct((N_SHARD, kdim, n // N_SHARD), F32),
               pl.BlockSpec((None, tk, tn), lambda i, j, mm: (j // per, i, j % per)))
    else:
        out = (jax.ShapeDtypeStruct((kdim, n), F32), pl.BlockSpec((tk, tn), lambda i, j, mm: (i, j)))
    return _mm(name, TN, a, pl.BlockSpec((tm, tk), lambda i, j, mm: (mm, i)), g, g_spec,
               (kdim // tk, n // tn, m // tm), (tk, tn), [out], riders=riders)


def _ew(name, fn, tiles, fulls, outs, sums=(), ts=512, riders=()):
    tiles = [t if isinstance(t, tuple) else (t, pl.BlockSpec((ts, t.shape[1]), lambda i: (i, 0)))
             for t in tiles]
    s_rows = tiles[0][0].shape[-2]
    nt, nf, no = len(tiles), len(fulls), len(outs)

    def body(*refs):
        t_vals = [r[...] for r in refs[:nt]]
        f_vals = [r[...] for r in refs[nt:nt + nf]]
        o_refs, s_refs = refs[nt + nf:nt + nf + no], refs[nt + nf + no:]
        o_vals, s_vals = fn(t_vals, f_vals)
        for r, v in zip(o_refs, o_vals):
            r[...] = v.astype(r.dtype)
        for r, v in zip(s_refs, s_vals):
            part = jnp.sum(v, axis=0, keepdims=True)

            @pl.when(pl.program_id(0) == 0)
            def _():
                r[...] = part

            @pl.when(pl.program_id(0) > 0)
            def _():
                r[...] += part

    full_specs = [pl.BlockSpec(f.shape, lambda i, nd=f.ndim: (0,) * nd) for f in fulls]
    return _carry(
        riders, body, name=name, grid=(s_rows // ts,),
        in_specs=[s for _, s in tiles] + full_specs,
        out_specs=[pl.BlockSpec((ts, c), lambda i: (i, 0)) for c, _ in outs]
        + [pl.BlockSpec((1, c), lambda i: (0, 0)) for c in sums],
        out_shape=[jax.ShapeDtypeStruct((s_rows, c), dt) for c, dt in outs]
        + [jax.ShapeDtypeStruct((1, c), F32) for c in sums],
        semantics=("arbitrary",),
    )(*[t for t, _ in tiles], *fulls)


def _rms_fwd(name, x, g, riders=()):
    def fn(t, f):
        xv = t[0]
        r = lax.rsqrt(jnp.mean(xv * xv, axis=-1, keepdims=True) + EPS)
        return [xv * r * f[0]], []
    out = _ew(name, fn, [x], [g], [(x.shape[1], BF16)], riders=riders)
    return (out[0][0], out[1]) if riders else out[0]


def _split3(x):
    x1 = x.astype(BF16)
    r1 = x - x1.astype(F32)
    x2 = r1.astype(BF16)
    x3 = (r1 - x2.astype(F32)).astype(BF16)
    return x1, x2, x3


def _rel_class(cp):
    far = (cp < MAX_REL) | (cp > BAND)
    return jnp.where(far, 2 * MAX_REL, BAND - cp)


def _skew_rows(x, sign):
    row = lax.broadcasted_iota(jnp.int32, x.shape, 0)
    for b in range(CHUNK.bit_length() - 1):
        shift = (1 << b) if sign > 0 else SKEW_W - (1 << b)
        x = jnp.where((row >> b) & 1 == 1, pltpu.roll(x, shift, 1), x)
    return x


def _roll_lanes(x, shift):
    return x if shift % SKEW_W == 0 else pltpu.roll(x, shift % SKEW_W, 1)


N_START = 3


def _bias_expand(rel_bias, riders=()):
    rel = jnp.pad(rel_bias, ((0, 0), (0, REL_PAD - N_REL))).reshape(N_HEADS, 1, REL_PAD)

    def body(rel_ref, o_ref):
        cls = lax.broadcasted_iota(jnp.int32, (REL_PAD, SKEW_W), 0)
        cp = lax.broadcasted_iota(jnp.int32, (REL_PAD, SKEW_W), 1)
        onehot = (cls == _rel_class(cp)).astype(BF16)
        rel8 = jnp.broadcast_to(rel_ref[...], (8, REL_PAD))
        trow = sum(jnp.dot(p, onehot, preferred_element_type=F32) for p in _split3(rel8))[0:1]
        first = _skew_rows(jnp.broadcast_to(trow, (CHUNK, SKEW_W)), +1)
        full = jnp.concatenate([_roll_lanes(first, CHUNK * g) for g in range(SUPER // CHUNK)], axis=0)[:, :BAND]
        qc = lax.broadcasted_iota(jnp.int32, (SUPER, BAND), 0) // CHUNK
        kc = lax.broadcasted_iota(jnp.int32, (SUPER, BAND), 1) // CHUNK
        on_band = (kc >= qc) & (kc <= qc + N_PREV_CHUNKS)
        table = jnp.where(on_band, full, NEG_INF).T
        key = lax.broadcasted_iota(jnp.int32, (BAND, SUPER), 0)
        for t in range(N_START):
            o_ref[t] = jnp.where(key < (N_START - 1 - t) * SUPER, NEG_INF, table)

    return _carry(
        riders, body, name="bias_expand", grid=(N_HEADS,),
        in_specs=[pl.BlockSpec((None, 1, REL_PAD), lambda h: (h, 0, 0))],
        out_specs=pl.BlockSpec((N_START, None, BAND, SUPER), lambda h: (0, h, 0, 0)),
        out_shape=jax.ShapeDtypeStruct((N_START, N_HEADS, BAND, SUPER), F32),
        semantics=("arbitrary",),
    )(rel)


def _bias_reduce(dbias, riders=()):
    def body(d_ref, o_ref):
        x = jnp.concatenate([d_ref[...].T, jnp.zeros((SUPER, SKEW_W - BAND), F32)], axis=1)
        folded = sum(_roll_lanes(x[CHUNK * g:CHUNK * (g + 1)], -CHUNK * g) for g in range(SUPER // CHUNK))
        diag = jnp.sum(_skew_rows(folded, -1), axis=0, keepdims=True)
        cp = lax.broadcasted_iota(jnp.int32, (SKEW_W, REL_PAD), 0)
        cls = lax.broadcasted_iota(jnp.int32, (SKEW_W, REL_PAD), 1)
        onehot = (cls == _rel_class(cp)).astype(BF16)
        diag8 = jnp.broadcast_to(diag, (8, SKEW_W))
        o_ref[...] = sum(jnp.dot(p, onehot, preferred_element_type=F32) for p in _split3(diag8))[0:1]

    out = _carry(
        riders, body, name="bias_reduce", grid=(N_HEADS,),
        in_specs=[pl.BlockSpec((None, BAND, SUPER), lambda h: (h, 0, 0))],
        out_specs=pl.BlockSpec((None, 1, REL_PAD), lambda h: (h, 0, 0)),
        out_shape=jax.ShapeDtypeStruct((N_HEADS, 1, REL_PAD), F32),
        semantics=("arbitrary",),
    )(dbias)
    out, carried = out if riders else (out, None)
    out = out.reshape(N_HEADS, REL_PAD)[:, :N_REL]
    return (out, carried) if riders else out


HEADS_PER_STEP = 8
HEAD_COLS = HEADS_PER_STEP * HEAD_DIM
N_HEAD_GROUPS = N_HEADS // HEADS_PER_STEP


def _scores_t(qs, kn, bias_t):
    return jnp.concatenate([lax.dot_general(k, qs, NT, preferred_element_type=F32) for k in kn], axis=0) + bias_t


def _bias_spec():
    return pl.BlockSpec((None, HEADS_PER_STEP, BAND, SUPER), lambda hg, i: (jnp.minimum(i, N_START - 1), hg, 0, 0))


def _band_specs(nb, col0, clamp_hi):
    def spec(d):
        def index(hg, i):
            blk = jnp.maximum(i - d, 0)
            if clamp_hi:
                blk = jnp.minimum(blk, nb - 1)
            return (blk, col0 + hg)
        return pl.BlockSpec((SUPER, HEAD_COLS), index)
    return [spec(2), spec(1), spec(0)]


def _head_sums(y):
    same_head = (lax.broadcasted_iota(jnp.int32, (MXU_DIM, MXU_DIM), 0) // HEAD_DIM
                 == lax.broadcasted_iota(jnp.int32, (MXU_DIM, MXU_DIM), 1) // HEAD_DIM).astype(BF16)
    sums = []
    for c0 in range(0, y.shape[1], MXU_DIM):
        chunk = y[:, c0:c0 + MXU_DIM]
        hi = chunk.astype(BF16)
        lo = (chunk - hi.astype(F32)).astype(BF16)
        sums.append(jnp.dot(hi, same_head, preferred_element_type=F32)
                    + jnp.dot(lo, same_head, preferred_element_type=F32))
    return jnp.concatenate(sums, axis=1)


def _head_unit(x):
    r = lax.rsqrt(_head_sums(x * x) * (1.0 / HEAD_DIM) + EPS)
    return x * r, r


def _head(hh):
    return slice(HEAD_DIM * hh, HEAD_DIM * (hh + 1))


def _key_block(j):
    return slice(SUPER * j, SUPER * (j + 1))


LSE_ROWS = 8


def _attn_fwd(qkn, v, bias, riders=()):
    s_len = qkn.shape[0]
    nb = s_len // SUPER

    def body(q_ref, k0, k1, k2, v0, v1, v2, b_ref, o_ref, lse_ref):
        outs = []
        v_t = [v0[...].T, v1[...].T, v2[...].T]

        def probabilities(hh):
            sl = _head(hh)
            s = _scores_t(q_ref[:, sl], [k0[:, sl], k1[:, sl], k2[:, sl]], b_ref[hh])
            m = jnp.max(s, axis=0, keepdims=True)
            e = jnp.exp(s - m)
            l = jnp.sum(e, axis=0, keepdims=True)
            lse_ref[hh:hh + 1, :] = m + jnp.log(l)
            return e.astype(BF16), 1.0 / l, sl

        def weighted_values(e, inv_l, sl):
            outs.append(sum(jnp.dot(v_t[j][sl, :], e[_key_block(j), :], preferred_element_type=F32)
                            for j in range(3)) * inv_l)

        ready = probabilities(0)
        for hh in range(1, HEADS_PER_STEP):
            following = probabilities(hh)
            weighted_values(*ready)
            ready = following
        weighted_values(*ready)
        o_ref[...] = jnp.concatenate(outs, axis=0).T.astype(o_ref.dtype)

    return _carry(
        riders, body, name="attn_fwd", grid=(N_HEAD_GROUPS, nb),
        in_specs=[pl.BlockSpec((SUPER, HEAD_COLS), lambda hg, i: (i, hg))]
        + _band_specs(nb, N_HEAD_GROUPS, False) + _band_specs(nb, 0, False) + [_bias_spec()],
        out_specs=[pl.BlockSpec((SUPER, HEAD_COLS), lambda hg, i: (i, hg)),
                   pl.BlockSpec((None, LSE_ROWS, SUPER), lambda hg, i: (hg, 0, i))],
        out_shape=[jax.ShapeDtypeStruct((s_len, D_MODEL), BF16),
                   jax.ShapeDtypeStruct((N_HEAD_GROUPS, LSE_ROWS, s_len), F32)],
        semantics=("parallel", "arbitrary"),
    )(qkn, qkn, qkn, qkn, v, v, v, bias)


def _attn_bwd(qkn, v, out, d_out, bias, lse, riders=()):
    s_len = qkn.shape[0]
    nb = s_len // SUPER

    def body(q_ref, k0, k1, k2, v0, v1, v2, o_ref, do_ref, b_ref, lse_ref, dp_ref, db_ref, aq_ref, ak_ref, av_ref):
        i = pl.program_id(1)

        @pl.when(i == 0)
        def _():
            aq_ref[...] = jnp.zeros_like(aq_ref)
            ak_ref[...] = jnp.zeros_like(ak_ref)
            av_ref[...] = jnp.zeros_like(av_ref)
            db_ref[...] = jnp.zeros_like(db_ref)

        @pl.when(i < nb)
        def _():
            dq, dk, dv = [], [[], [], []], [[], [], []]
            ones = jnp.ones((8, HEAD_DIM), BF16)
            k_t = [k0[...].T, k1[...].T, k2[...].T]

            def softmax_grad(hh):
                sl = _head(hh)
                qs, do = q_ref[:, sl], do_ref[:, sl]
                kn = [k0[:, sl], k1[:, sl], k2[:, sl]]
                prod = do.astype(F32) * o_ref[:, sl].astype(F32)
                hi = prod.astype(BF16)
                lo = (prod - hi.astype(F32)).astype(BF16)
                delta = (lax.dot_general(ones, hi, NT, preferred_element_type=F32)
                         + lax.dot_general(ones, lo, NT, preferred_element_type=F32))[0:1]
                lse_row = lse_ref[hh:hh + 1, :]
                pb, dsb = [], []
                for j, (kj, vj) in enumerate(zip(kn, (v0, v1, v2))):
                    rows = _key_block(j)
                    p = jnp.exp(lax.dot_general(kj, qs, NT, preferred_element_type=F32) + b_ref[hh, rows, :] - lse_row)
                    ds = p * (lax.dot_general(vj[:, sl], do, NT, preferred_element_type=F32) - delta)
                    db_ref[hh, rows, :] += ds
                    pb.append(p.astype(BF16))
                    dsb.append(ds.astype(BF16))
                return pb, dsb, qs, do, sl

            def operand_grads(pb, dsb, qs, do, sl):
                dq.append(sum(jnp.dot(k_t[j][sl, :], dsb[j], preferred_element_type=F32) for j in range(3)))
                for j in range(3):
                    dv[j].append(jnp.dot(pb[j], do, preferred_element_type=F32))
                    dk[j].append(jnp.dot(dsb[j], qs, preferred_element_type=F32))

            ready = softmax_grad(0)
            for hh in range(1, HEADS_PER_STEP):
                following = softmax_grad(hh)
                operand_grads(*ready)
                ready = following
            operand_grads(*ready)
            aq_ref[i % 3] = jnp.concatenate(dq, axis=0).T
            for j in range(3):
                slot = (i + 1 + j) % 3
                if j < 2:
                    ak_ref[slot] += jnp.concatenate(dk[j], axis=1)
                    av_ref[slot] += jnp.concatenate(dv[j], axis=1)
                else:
                    ak_ref[slot] = jnp.concatenate(dk[j], axis=1)
                    av_ref[slot] = jnp.concatenate(dv[j], axis=1)

        slot = (i + 1) % 3
        dp_ref[0] = aq_ref[slot].astype(dp_ref.dtype)
        dp_ref[1] = ak_ref[slot].astype(dp_ref.dtype)
        dp_ref[2] = av_ref[slot].astype(dp_ref.dtype)

    def qrow(hg, i):
        return (jnp.minimum(i, nb - 1), hg)

    return _carry(
        riders, body, name="attn_bwd", grid=(N_HEAD_GROUPS, nb + 2),
        in_specs=[pl.BlockSpec((SUPER, HEAD_COLS), qrow)]
        + _band_specs(nb, N_HEAD_GROUPS, True) + _band_specs(nb, 0, True)
        + [pl.BlockSpec((SUPER, HEAD_COLS), qrow), pl.BlockSpec((SUPER, HEAD_COLS), qrow), _bias_spec(),
           pl.BlockSpec((None, LSE_ROWS, SUPER), lambda hg, i: (hg, 0, jnp.minimum(i, nb - 1)))],
        out_specs=[pl.BlockSpec((3, SUPER, HEAD_COLS), lambda hg, i: (0, jnp.maximum(i - 2, 0), hg)),
                   pl.BlockSpec((HEADS_PER_STEP, BAND, SUPER), lambda hg, i: (hg, 0, 0))],
        out_shape=[jax.ShapeDtypeStruct((6, s_len, D_MODEL), BF16),
                   jax.ShapeDtypeStruct((N_HEADS, BAND, SUPER), F32)],
        scratch_shapes=[pltpu.VMEM((3, SUPER, HEAD_COLS), F32)] * 3,
        semantics=("parallel", "arbitrary"),
    )(qkn, qkn, qkn, qkn, v, v, v, out, d_out, bias, lse)


def _qk_norm_bwd(dproj6, qk_raw, gq, gk):
    s_len = qk_raw.shape[0]
    ts = min(1024, s_len)

    nsteps = s_len // ts
    half = D_MODEL // 2

    def body(d_ref, raw_ref, gq_ref, gk_ref, o_ref, dgq_ref, dgk_ref, acc_ref):
        step = pl.program_id(0)

        @pl.when(step == 0)
        def _():
            acc_ref[...] = jnp.zeros_like(acc_ref)

        for piece, (g_ref, scale) in enumerate(((gq_ref, QK_SCALE), (gk_ref, 1.0))):
            for c0 in (0, half):
                xhat, r = _head_unit(raw_ref[:, piece * D_MODEL + c0:piece * D_MODEL + c0 + half].astype(F32))
                dn = d_ref[piece, :, c0:c0 + half].astype(F32) * scale
                u = dn * g_ref[...]
                dx = r * (u - xhat * (_head_sums(u * xhat) * (1.0 / HEAD_DIM)))
                o_ref[piece, :, c0:c0 + half] = dx.astype(o_ref.dtype)
                acc_ref[piece:piece + 1, c0:c0 + half] += jnp.sum(dn * xhat, axis=0, keepdims=True)

        @pl.when(step == nsteps - 1)
        def _():
            lane = lax.broadcasted_iota(jnp.int32, (D_MODEL, LANE), 0) % HEAD_DIM
            fold = (lane == lax.broadcasted_iota(jnp.int32, (D_MODEL, LANE), 1)).astype(BF16)
            tot = sum(jnp.dot(p, fold, preferred_element_type=F32) for p in _split3(acc_ref[...]))
            dgq_ref[...] = tot[0:1, :HEAD_DIM]
            dgk_ref[...] = tot[1:2, :HEAD_DIM]

    gain = pl.BlockSpec((1, half), lambda i: (0, 0))
    small = pl.BlockSpec((1, HEAD_DIM), lambda i: (0, 0))
    per_head = lambda g: jnp.tile(g, (1, half // HEAD_DIM))
    return pl.pallas_call(
        body, name="qk_norm_bwd", grid=(nsteps,),
        in_specs=[pl.BlockSpec((2, ts, D_MODEL), lambda i: (0, i, 0)),
                  pl.BlockSpec((ts, 2 * D_MODEL), lambda i: (i, 0)), gain, gain],
        out_specs=[pl.BlockSpec((2, ts, D_MODEL), lambda i: (0, i, 0)), small, small],
        out_shape=[jax.ShapeDtypeStruct(dproj6.shape, dproj6.dtype),
                   jax.ShapeDtypeStruct((1, HEAD_DIM), F32), jax.ShapeDtypeStruct((1, HEAD_DIM), F32)],
        scratch_shapes=[pltpu.VMEM((8, D_MODEL), F32)],
        input_output_aliases={0: 0},
        compiler_params=_params("arbitrary"),
    )(dproj6, qk_raw, per_head(gq), per_head(gk))


CONV_ROWS = 512
HALO = 16


def _rows_with_halo(ref, r0, n, front, s_len):
    zeros = jnp.zeros((HALO, ref.shape[1]), F32)
    if front:
        return (jnp.concatenate([zeros, ref[0:n, :].astype(F32)], axis=0) if r0 == 0
                else ref[r0 - HALO:r0 + n, :].astype(F32))
    return (jnp.concatenate([ref[r0:r0 + n, :].astype(F32), zeros], axis=0) if r0 + n == s_len
            else ref[r0:r0 + n + HALO, :].astype(F32))


def _earlier(ext, k):
    return pltpu.roll(ext, k, 0)[HALO:]


def _later(ext, k):
    n = ext.shape[0]
    return pltpu.roll(ext, n - k, 0)[:n - HALO]


def _conv_cols(col0):
    return lambda s_len: pl.BlockSpec((s_len, LANE), lambda j: (0, col0 + j))


def _conv_fwd(proj, conv_w, conv_b, riders=()):
    s_len = proj.shape[0]

    def body(bg_ref, cg_ref, xc_ref, w_ref, b_ref, o_ref):
        w = [w_ref[t:t + 1, :] for t in range(3)]
        for r0 in range(0, s_len, CONV_ROWS):
            u = _rows_with_halo(cg_ref, r0, CONV_ROWS, True, s_len) * \
                _rows_with_halo(xc_ref, r0, CONV_ROWS, True, s_len)
            conv = b_ref[...] + w[0] * _earlier(u, 2) + w[1] * _earlier(u, 1) + w[2] * u[HALO:]
            o_ref[r0:r0 + CONV_ROWS, :] = (bg_ref[r0:r0 + CONV_ROWS, :].astype(F32) * conv).astype(o_ref.dtype)

    return _carry(
        riders, body, name="conv_fwd", grid=(D_MODEL // LANE,),
        in_specs=[_conv_cols(0)(s_len), _conv_cols(8)(s_len), _conv_cols(16)(s_len),
                  pl.BlockSpec((3, LANE), lambda j: (0, j)), pl.BlockSpec((1, LANE), lambda j: (0, j))],
        out_specs=pl.BlockSpec((s_len, LANE), lambda j: (0, j)),
        out_shape=jax.ShapeDtypeStruct((s_len, D_MODEL), BF16),
        semantics=("parallel",),
    )(proj, proj, proj, conv_w, conv_b)


def _conv_bwd(dproj6, dy, proj, conv_w, conv_b, riders=()):
    s_len = proj.shape[0]

    def body(dy_ref, bg_ref, cg_ref, xc_ref, w_ref, b_ref, _, dp_ref, dw_ref):
        w = [w_ref[t:t + 1, :] for t in range(3)]
        acc = [jnp.zeros((1, LANE), F32) for _ in range(4)]
        for r0 in range(0, s_len, CONV_ROWS):
            rows = slice(r0, r0 + CONV_ROWS)
            u = _rows_with_halo(cg_ref, r0, CONV_ROWS, True, s_len) * \
                _rows_with_halo(xc_ref, r0, CONV_ROWS, True, s_len)
            u2, u1, u0 = _earlier(u, 2), _earlier(u, 1), u[HALO:]
            conv = b_ref[...] + w[0] * u2 + w[1] * u1 + w[2] * u0
            dp_ref[0, rows, :] = (dy_ref[rows, :].astype(F32) * conv).astype(dp_ref.dtype)
            dconv_ext = _rows_with_halo(dy_ref, r0, CONV_ROWS, False, s_len) * \
                _rows_with_halo(bg_ref, r0, CONV_ROWS, False, s_len)
            dconv = dconv_ext[:CONV_ROWS]
            for t, term in enumerate([dconv * u2, dconv * u1, dconv * u0, dconv]):
                acc[t] = acc[t] + jnp.sum(term, axis=0, keepdims=True)
            du = w[2] * dconv + w[1] * _later(dconv_ext, 1) + w[0] * _later(dconv_ext, 2)
            dp_ref[1, rows, :] = (du * xc_ref[rows, :].astype(F32)).astype(dp_ref.dtype)
            dp_ref[2, rows, :] = (du * cg_ref[rows, :].astype(F32)).astype(dp_ref.dtype)
        dw_ref[...] = jnp.zeros_like(dw_ref)
        for t in range(4):
            dw_ref[t:t + 1, :] = acc[t]

    return _carry(
        riders, body, name="conv_bwd", grid=(D_MODEL // LANE,),
        in_specs=[pl.BlockSpec((s_len, LANE), lambda j: (0, j)),
                  _conv_cols(0)(s_len), _conv_cols(8)(s_len), _conv_cols(16)(s_len),
                  pl.BlockSpec((3, LANE), lambda j: (0, j)), pl.BlockSpec((1, LANE), lambda j: (0, j)),
                  pl.BlockSpec(memory_space=pl.ANY)],
        out_specs=[pl.BlockSpec((3, s_len, LANE), lambda j: (1, 0, j)),
                   pl.BlockSpec((8, LANE), lambda j: (0, j))],
        out_shape=[jax.ShapeDtypeStruct(dproj6.shape, dproj6.dtype),
                   jax.ShapeDtypeStruct((8, D_MODEL), F32)],
        input_output_aliases={6: 0},
        semantics=("parallel",),
    )(dy, proj, proj, proj, conv_w, conv_b, dproj6)


def _d_norm_input(name, terms, x, g, dres, out_dtype, riders=()):
    s_len, kdim = terms[0][0].shape[1], terms[0][1].shape[1]
    tm, chunk = 512, 512
    n = len(terms)

    def body(*refs):
        x_ref, g_ref, r_ref, o_ref, dg_ref = refs[2 * n:]
        dh = jnp.zeros((tm, kdim), F32)
        for p_ref, w_ref in zip(refs[:n], refs[n:2 * n]):
            n_piece, _, width = p_ref.shape
            per_shard = w_ref.shape[2]
            for c0 in range(0, n_piece * width, chunk):
                dh = dh + lax.dot_general(p_ref[c0 // width, :, c0 % width:c0 % width + chunk],
                                          w_ref[c0 // per_shard, :, c0 % per_shard:c0 % per_shard + chunk],
                                          NT, preferred_element_type=F32)
        xv = x_ref[...]
        r = lax.rsqrt(jnp.mean(xv * xv, axis=-1, keepdims=True) + EPS)
        xhat = xv * r
        u = dh * g_ref[...]
        dx = r * (u - xhat * jnp.mean(u * xhat, axis=-1, keepdims=True)) + r_ref[...].astype(F32)
        o_ref[...] = dx.astype(o_ref.dtype)
        part = jnp.sum(dh * xhat, axis=0, keepdims=True)

        @pl.when(pl.program_id(0) == 0)
        def _():
            dg_ref[...] = part

        @pl.when(pl.program_id(0) > 0)
        def _():
            dg_ref[...] += part

    tile = pl.BlockSpec((tm, kdim), lambda i: (i, 0))
    vec = pl.BlockSpec((1, kdim), lambda i: (0, 0))
    return _carry(
        riders, body, name=name, grid=(s_len // tm,),
        in_specs=[pl.BlockSpec((p.shape[0], tm, p.shape[2]), lambda i: (0, i, 0)) for p, _ in terms]
        + [pl.BlockSpec(w.shape, lambda i: (0, 0, 0), pipeline_mode=pl.Buffered(1)) for _, w in terms]
        + [tile, vec, tile],
        out_specs=[tile, vec],
        out_shape=[jax.ShapeDtypeStruct((s_len, kdim), out_dtype), jax.ShapeDtypeStruct((1, kdim), F32)],
        semantics=("arbitrary",),
    )(*[p for p, _ in terms], *[w for _, w in terms], x, g, dres)


def _local_grads(x, target, norm1_g, q_norm_g, k_norm_g, rel_bias, conv_b, b_gate, norm2_g, comm):
    s_len = x.shape[0]
    tm = min(1024, s_len)
    tb = min(2048, s_len)
    row = lambda v: v.reshape(1, -1)

    def carrying(at, fn, *args, **kw):
        riders = comm.riders(at)
        out = fn(*args, riders=riders, **kw)
        if riders:
            out, carried = out
            comm.done(at, carried)
        return out

    bias = carrying("bias_expand", _bias_expand, rel_bias)
    h = carrying("norm1", _rms_fwd, "norm1", x, row(norm1_g))
    w_in3, conv_w = comm.early_weights()
    gq, gk = row(q_norm_g), row(k_norm_g)

    def qk_epi(acc, e, o):
        o[0][...] = acc.astype(BF16)
        gain = jnp.where(pl.program_id(1) < 2, e[0] * QK_SCALE, e[1])
        o[1][...] = (_head_unit(acc)[0] * gain).astype(BF16)
    conv_in = carrying("proj_conv", _mm_fwd, "proj_conv", h, w_in3, tb, 1536, D_MODEL, col0=2, ncols=3 * D_MODEL,
                       outs=[_out2d(s_len, 3 * D_MODEL, BF16, tb, 1536)])
    small = pl.BlockSpec((1, 512), lambda i, j, k: (0, 0))
    per_head = lambda g: jnp.tile(g, (1, 512 // HEAD_DIM))
    qk_raw, qkn = carrying("proj_qk", _mm_fwd, "proj_qk", h, w_in3, tm, 512, D_MODEL, ncols=2 * D_MODEL,
                           epilogue=qk_epi, outs=[_out2d(s_len, 2 * D_MODEL, BF16, tm, 512)] * 2,
                           extras=[(per_head(gq), small), (per_head(gk), small)])
    v = carrying("proj_v", _mm_fwd, "proj_v", h, w_in3, tb, 512, D_MODEL, col0=4, ncols=D_MODEL,
                 outs=[_out2d(s_len, D_MODEL, BF16, tb, 512)])
    w_gate3 = comm.gate_weight()

    def gate_epi(acc, e, o):
        o[0][...] = jax.nn.sigmoid(acc + e[0]).astype(BF16)
    gates = carrying("gates", _mm_fwd, "gates", h, w_gate3, tb, 512, D_MODEL, epilogue=gate_epi,
                     outs=[_out2d(s_len, 2 * D_MODEL, BF16, tb, 512)],
                     extras=[(row(b_gate), pl.BlockSpec((1, 512), lambda i, j, k: (0, j)))])

    attn, lse = carrying("attn_fwd", _attn_fwd, qkn, v, bias)
    yconv = carrying("conv_fwd", _conv_fwd, conv_in, conv_w, row(conv_b))
    w_ap, w_cp, w_out, w_up3, w_down = comm.late_weights()
    tw = 1024
    ya = _mm_fwd("attn_proj", attn, w_ap, tm, tw, D_MODEL, outs=[_out2d(s_len, D_MODEL, BF16, tm, tw)])

    def merge_epi(acc, e, o):
        ya_v, ga, gc = [t.astype(F32) for t in e]
        o[0][...] = acc.astype(BF16)
        o[1][...] = (ga * ya_v + gc * acc).astype(BF16)
    gate_a, gate_c = (gates, _tile_spec(tm, tw, 0)), (gates, _tile_spec(tm, tw, 1))
    yc, merged = _mm_fwd("conv_proj", yconv, w_cp, tm, tw, D_MODEL, epilogue=merge_epi,
                         outs=[_out2d(s_len, D_MODEL, BF16, tm, tw), _out2d(s_len, D_MODEL, BF16, tm, tw)],
                         extras=[(ya, _tile_spec(tm, tw)), gate_a, gate_c])

    def res_epi(acc, e, o):
        x1_v = e[0] + acc
        o[0][...] = x1_v
        r = lax.rsqrt(jnp.mean(x1_v * x1_v, axis=-1, keepdims=True) + EPS)
        o[1][...] = (x1_v * r * e[1]).astype(BF16)
    assert tw == D_MODEL
    x1, h2 = _mm_fwd("out_proj", merged, w_out, tm, tw, D_MODEL, epilogue=res_epi,
                     outs=[_out2d(s_len, D_MODEL, F32, tm, tw), _out2d(s_len, D_MODEL, BF16, tm, tw)],
                     extras=[(x, _tile_spec(tm, tw)), (row(norm2_g), pl.BlockSpec((1, tw), lambda i, j, k: (0, 0)))])

    def up_epi(acc, e, o):
        o[0][...] = jnp.square(jnp.maximum(acc, 0.0)).astype(BF16)
    act = _mm_fwd("mlp_up", h2, w_up3, tb, tw, D_MODEL, epilogue=up_epi, outs=[_out2d(s_len, D_FF, BF16, tb, tw)])

    def loss_epi(acc, e, o):
        err = e[0] + acc - e[1]
        o[0][...] = (err * (1.0 / D_MODEL)).astype(BF16)
        sq = err * err
        part = sq[:, 0:LANE]
        for c0 in range(LANE, D_MODEL, LANE):
            part = part + sq[:, c0:c0 + LANE]
        o[1][...] = jnp.sum(part.reshape(tl // 8, 8, LANE), axis=0)
    tl = 512
    dy_b, loss_part = _mm_fwd(
        "mlp_down", act, w_down, tl, D_MODEL, D_FF, epilogue=loss_epi,
        outs=[_out2d(s_len, D_MODEL, BF16, tl, D_MODEL),
              (jax.ShapeDtypeStruct((8 * (s_len // tl), LANE), F32), pl.BlockSpec((8, LANE), lambda i, j, k: (i, 0)))],
        extras=[(x1, _tile_spec(tl, D_MODEL)), (target, _tile_spec(tl, D_MODEL))])

    def dup_epi(acc, e, o):
        o[0][...] = (acc * (2.0 * jnp.sqrt(e[0].astype(F32)))).astype(BF16)
    full = lambda cols: pl.BlockSpec((tm, cols), lambda i, j, n: (i, n))
    tokens = lambda cols: pl.BlockSpec((s_len, cols), lambda i, j, m: (m, j))
    dup = _mm_bwd_x("d_act", dy_b, full(D_MODEL), w_down, tm, tw, D_MODEL, s_len, D_MODEL, epilogue=dup_epi,
                    outs=[_out2d(s_len, D_FF, BF16, tm, tw)], extras=[(act, _tile_spec(tm, tw))])
    g_down = _mm_bwd_w("g_down", act, dy_b, tokens(D_MODEL), D_MODEL, 512, D_MODEL, s_len, False)
    g_up = _mm_bwd_w("g_up", h2, dup, tokens(512), D_FF, D_MODEL, 512, s_len, True)
    comm.grads_ready("mlp", dict(w_down=g_down, w_up=g_up))
    dx1_b, dg2 = carrying("d_h2", _d_norm_input, "d_h2", [(dup.reshape(1, s_len, D_FF), w_up3)],
                          x1, row(norm2_g), dy_b, BF16)

    def dmerge_epi(acc, e, o):
        ya_v, yc_v, ga, gc = [t.astype(F32) for t in e]
        o[0][...] = (acc * ga).astype(BF16)
        o[1][...] = (acc * gc).astype(BF16)
        o[2][0] = (acc * ya_v * ga * (1.0 - ga)).astype(BF16)
        o[2][1] = (acc * yc_v * gc * (1.0 - gc)).astype(BF16)
    dya, dyc, dgp2 = _mm_bwd_x(
        "d_merged", dx1_b, full(D_MODEL), w_out, tm, tw, D_MODEL, s_len, D_MODEL, epilogue=dmerge_epi,
        outs=[_out2d(s_len, D_MODEL, BF16, tm, tw), _out2d(s_len, D_MODEL, BF16, tm, tw),
              (jax.ShapeDtypeStruct((2, s_len, D_MODEL), BF16), pl.BlockSpec((2, tm, tw), lambda i, j, n: (0, i, j)))],
        extras=[(ya, _tile_spec(tm, tw)), (yc, _tile_spec(tm, tw)), gate_a, gate_c])
    g_out = _mm_bwd_w("g_out", merged, dx1_b, tokens(512), D_MODEL, D_MODEL, 512, s_len, False)
    d_attn = _mm_bwd_x("d_attn", dya, full(D_MODEL), w_ap, tm, tw, D_MODEL, s_len, D_MODEL,
                       outs=[_out2d(s_len, D_MODEL, BF16, tm, tw)])
    g_ap = _mm_bwd_w("g_attn_proj", attn, dya, tokens(512), D_MODEL, D_MODEL, 512, s_len, False)
    d_yconv = _mm_bwd_x("d_yconv", dyc, full(D_MODEL), w_cp, tm, tw, D_MODEL, s_len, D_MODEL,
                        outs=[_out2d(s_len, D_MODEL, BF16, tm, tw)])
    g_cp = _mm_bwd_w("g_conv_proj", yconv, dyc, tokens(512), D_MODEL, D_MODEL, 512, s_len, False)
    piece = lambda width: (lambda blk: (blk * width) // D_MODEL, lambda blk: (blk * width % D_MODEL) // width)
    pc, cb = piece(512)
    pieces = pl.BlockSpec((None, s_len, 512), lambda i, j, m: (pc(j), m, cb(j)))
    g_gate = _mm_bwd_w("g_gate", h, dgp2, pieces, 2 * D_MODEL, D_MODEL, 512, s_len, True)
    comm.grads_ready("proj", dict(w_out=g_out, w_attn_proj=g_ap, w_conv_proj=g_cp, w_gate=g_gate))

    dproj6, dbias = carrying("attn_bwd", _attn_bwd, qkn, v, attn, d_attn, bias, lse)
    dproj6, dgq, dgk = _qk_norm_bwd(dproj6, qk_raw, gq, gk)
    dproj6, dconv_wb = carrying("conv_bwd", _conv_bwd, dproj6, d_yconv, conv_in, conv_w, row(conv_b))

    g_in = carrying("g_in", _mm_bwd_w, "g_in", h, dproj6, pieces, 6 * D_MODEL, D_MODEL, 512, s_len, True)
    comm.grads_ready("in", dict(w_in=g_in))
    d_rel = carrying("bias_reduce", _bias_reduce, dbias)
    grad_x, dg1 = carrying("d_h", _d_norm_input, "d_h", [(dproj6, w_in3), (dgp2, w_gate3)],
                           x, row(norm1_g), dx1_b, F32)

    def bsum(t, f):
        return [], [t[0].astype(F32), t[1].astype(F32)]
    ts = 512
    db_a, db_c = carrying("b_gate_sum", _ew, "b_gate_sum", bsum,
                          [(dgp2, pl.BlockSpec((None, ts, D_MODEL), lambda i: (0, i, 0))),
                           (dgp2, pl.BlockSpec((None, ts, D_MODEL), lambda i: (1, i, 0)))],
                          [], [], sums=[D_MODEL, D_MODEL], ts=ts)

    big = dict(w_in=g_in, w_attn_proj=g_ap, w_conv_proj=g_cp, w_gate=g_gate, w_out=g_out,
               w_up=g_up, w_down=g_down)
    small = dict(norm1_g=dg1, norm2_g=dg2, conv_wb=dconv_wb, b_gate=(db_a, db_c),
                 q_norm_g=dgq, k_norm_g=dgk, rel_bias=d_rel)
    return loss_part, grad_x, big, small


def _finish_loss(loss_part):
    def body(l_ref, lo_ref):
        total = jnp.sum(jnp.sum(l_ref[...], axis=0, keepdims=True), axis=1, keepdims=True)
        lo_ref[...] = jnp.broadcast_to(total * (0.5 / D_MODEL), lo_ref.shape)

    return pl.pallas_call(body, name="finish_loss", out_shape=jax.ShapeDtypeStruct((8, LANE), F32))(loss_part)


def _place():
    return lax.axis_index("x"), lax.axis_index("y"), lax.axis_index("c")


def _other_chips(x, y):
    return [(1 - x, y), (x, 1 - y), (1 - x, 1 - y)]


def _cast_into_slot(name, where, w):
    r, cols = w.shape
    ts = 256

    def body(w_ref, x_ref, o_ref):
        o_ref[...] = x_ref[...].astype(o_ref.dtype)

    return pl.pallas_call(
        body, name=name,
        grid_spec=pltpu.PrefetchScalarGridSpec(
            num_scalar_prefetch=1, grid=(r // ts,),
            in_specs=[pl.BlockSpec((ts, cols), lambda i, w: (i, 0))],
            out_specs=pl.BlockSpec((None, ts, cols), lambda i, w: (w[0], i, 0))),
        out_shape=jax.ShapeDtypeStruct((N_SHARD, r, cols), BF16),
        compiler_params=_params("parallel"),
    )(where, w)


def _remote(src, dst, send, recv, k, to):
    return pltpu.make_async_remote_copy(src_ref=src, dst_ref=dst, send_sem=send.at[k], recv_sem=recv.at[k],
                                        device_id=to, device_id_type=MESH)


def _gather_riders(slots):
    n = len(slots)

    def copy(refs, w, j, shard, which, send, recv, to):
        hr = slots[w].shape[1] // 2
        ref = refs[w].at[shard, pl.ds(which * hr, hr)]
        return _remote(ref, ref, send, recv, 3 * w + j, to)

    def for_each_peer(fn):
        x, y, c = _place()
        for w in range(n):
            for j, chip in enumerate(_other_chips(x, y)):
                fn(w, j, 2 * x + y, 2 * chip[0] + chip[1], c, (*chip, c), (x, y, 1 - c))

    def chips_start(_, refs, send, recv):
        for_each_peer(lambda w, j, mine, theirs, c, peer, sib: copy(refs, w, j, mine, c, send, recv, peer).start())

    def chips_finish(_, refs, send, recv):
        for_each_peer(lambda w, j, mine, theirs, c, peer, sib: copy(refs, w, j, theirs, c, send, recv, peer).wait_recv())
        for_each_peer(lambda w, j, mine, theirs, c, peer, sib: copy(refs, w, j, mine, c, send, recv, peer).wait_send())

    def sibling_start(_, refs, send, recv):
        for_each_peer(lambda w, j, mine, theirs, c, peer, sib: copy(refs, w, j, theirs, c, send, recv, sib).start())

    def sibling_finish(_, refs, send, recv):
        for_each_peer(lambda w, j, mine, theirs, c, peer, sib: copy(refs, w, j, theirs, 1 - c, send, recv, sib).wait_recv())
        for_each_peer(lambda w, j, mine, theirs, c, peer, sib: copy(refs, w, j, theirs, c, send, recv, sib).wait_send())

    return (lambda arrays: _Rider([], arrays, 3 * n, chips_start, chips_finish),
            lambda arrays: _Rider([], arrays, 3 * n, sibling_start, sibling_finish))


def _pair_exchange_rider(grads, landing):
    n = len(grads)

    def copies(srcs, dsts, send, recv):
        x, y, c = _place()
        out = []
        for w in range(n):
            hr = grads[w].shape[1] // 2
            out.append(_remote(srcs[w].at[:, pl.ds((1 - c) * hr, hr)], dsts[w], send, recv, w, (x, y, 1 - c)))
        return out

    def start(srcs, dsts, send, recv):
        for cp in copies(srcs, dsts, send, recv):
            cp.start()

    def finish(srcs, dsts, send, recv):
        for cp in copies(srcs, dsts, send, recv):
            cp.wait()

    return _Rider(grads, landing, n, start, finish)


def _row_tile(hr):
    return min(hr, 256)


def _pair_add(name, where, grad, got):
    _, hr, cols = got.shape
    tr = _row_tile(hr)
    nblk = hr // tr

    def body(w_ref, g_ref, r_ref, o_ref):
        o_ref[...] = (g_ref[...] + r_ref[...]).astype(o_ref.dtype)

    other = lambda k, w: (w[0] + 1 + k) % N_SHARD
    return pl.pallas_call(
        body, name=name,
        grid_spec=pltpu.PrefetchScalarGridSpec(
            num_scalar_prefetch=1, grid=(N_SHARD - 1, nblk),
            in_specs=[pl.BlockSpec((None, tr, cols), lambda k, i, w: (other(k, w), w[1] * nblk + i, 0)),
                      pl.BlockSpec((None, tr, cols), lambda k, i, w: (other(k, w), i, 0))],
            out_specs=pl.BlockSpec((None, tr, cols), lambda k, i, w: (other(k, w), i, 0))),
        out_shape=jax.ShapeDtypeStruct(got.shape, BF16),
        compiler_params=_params("parallel", "parallel"),
    )(where, grad, got)


def _chip_exchange_rider(partials, landing):
    n = len(partials)

    def copies(srcs, dsts, send, recv):
        x, y, c = _place()
        return [_remote(srcs[w].at[2 * chip[0] + chip[1]], dsts[w].at[j], send, recv, 3 * w + j, (*chip, c))
                for w in range(n) for j, chip in enumerate(_other_chips(x, y))]

    def start(srcs, dsts, send, recv):
        for cp in copies(srcs, dsts, send, recv):
            cp.start()

    def finish(srcs, dsts, send, recv):
        for cp in copies(srcs, dsts, send, recv):
            cp.wait()

    return _Rider(partials, landing, 3 * n, start, finish)


def _final_add(name, where, grad, got, arrived):
    _, hr, cols = got.shape
    tr = _row_tile(hr)
    nblk = hr // tr

    def body(w_ref, g_ref, r_ref, a_ref, o_ref):
        acc = g_ref[...] + r_ref[...]
        for j in range(3):
            acc = acc + a_ref[j].astype(F32)
        o_ref[...] = acc

    return pl.pallas_call(
        body, name=name,
        grid_spec=pltpu.PrefetchScalarGridSpec(
            num_scalar_prefetch=1, grid=(nblk,),
            in_specs=[pl.BlockSpec((None, tr, cols), lambda i, w: (w[0], w[1] * nblk + i, 0)),
                      pl.BlockSpec((None, tr, cols), lambda i, w: (w[0], i, 0)),
                      pl.BlockSpec((3, tr, cols), lambda i, w: (0, i, 0))],
            out_specs=pl.BlockSpec((tr, cols), lambda i, w: (w[1] * nblk + i, 0))),
        out_shape=jax.ShapeDtypeStruct((2 * hr, cols), F32),
        compiler_params=_params("parallel"),
    )(where, grad, got, arrived)


def _pair_share_rider(shards):
    n = len(shards)

    def half(refs, w, which):
        hr = shards[w].shape[0] // 2
        return refs[w].at[pl.ds(which * hr, hr)]

    def start(_, refs, send, recv):
        x, y, c = _place()
        for w in range(n):
            _remote(half(refs, w, c), half(refs, w, c), send, recv, w, (x, y, 1 - c)).start()

    def finish(_, refs, send, recv):
        x, y, c = _place()
        for w in range(n):
            _remote(half(refs, w, 1 - c), half(refs, w, 1 - c), send, recv, w, (x, y, 1 - c)).wait_recv()
        for w in range(n):
            _remote(half(refs, w, c), half(refs, w, c), send, recv, w, (x, y, 1 - c)).wait_send()

    return _Rider([], shards, n, start, finish)


class _Exchange:
    PLAN = {"bias_expand": [("early", "gather")], "norm1": [("early", "forward")],
            "proj_conv": [("gate", "gather")], "proj_v": [("gate", "forward")],
            "proj_qk": [("late", "gather")], "gates": [("late", "forward")],
            "attn_fwd": [("mlp_w", "gather")], "conv_fwd": [("mlp_w", "forward")],
            "d_h2": [("mlp", "pair")], "attn_bwd": [("mlp", "chips"), ("proj", "pair")], "conv_bwd": [("mlp", "share")],
            "g_in": [("proj", "chips")], "bias_reduce": [("proj", "share"), ("in", "pair")],
            "d_h": [("in", "chips")], "b_gate_sum": [("in", "share")]}

    def __init__(self, where, early_slots, gate_slots, late_slots):
        self.where = where
        self.slots = dict(early=early_slots, gate=gate_slots, late=late_slots[:3], mlp_w=late_slots[3:])
        self.stage = {g: dict(zip(("gather", "forward"), _gather_riders(s))) for g, s in self.slots.items()}
        self.groups, self.reduced, self.pending = {}, {}, []

    def early_weights(self):
        w_in3, small = self.slots["early"]
        conv_w = small[:, :3, :].transpose(1, 0, 2).reshape(3, N_SHARD * small.shape[2])
        return w_in3, conv_w

    def gate_weight(self):
        return self.slots["gate"][0]

    def late_weights(self):
        rows = lambda a: a.reshape(a.shape[0] * a.shape[1], a.shape[2])
        w_ap, w_cp, w_out = self.slots["late"]
        w_up3, w_down = self.slots["mlp_w"]
        return rows(w_ap), rows(w_cp), rows(w_out), w_up3, rows(w_down)

    def grads_ready(self, group, grads):
        names = list(grads)
        g4 = [g if g.ndim == 3 else g.reshape(N_SHARD, -1, g.shape[1]) for g in grads.values()]
        self.groups[group] = dict(names=names, g4=g4)

    def riders(self, at):
        self.pending = self.PLAN.get(at, [])
        out = []
        for group, stage in self.pending:
            if group in self.slots:
                out.append(self.stage[group][stage](self.slots[group]))
                continue
            st = self.groups[group]
            if stage == "pair":
                landing = [lax.empty((N_SHARD, g.shape[1] // 2, g.shape[2]), F32) for g in st["g4"]]
                out.append(_pair_exchange_rider(st["g4"], landing))
            elif stage == "chips":
                landing = [lax.empty((3,) + p.shape[1:], p.dtype) for p in st["partial"]]
                out.append(_chip_exchange_rider(st["partial"], landing))
            else:
                out.append(_pair_share_rider(st["halves"]))
        return out

    def done(self, at, carried):
        for (group, stage), arrays in zip(self.pending, carried):
            if group in self.slots:
                self.slots[group] = arrays
                continue
            st = self.groups[group]
            tag = lambda what, n: what + "_" + n
            if stage == "pair":
                st["got"] = arrays
                st["partial"] = [_pair_add(tag("pair_add", n), self.where, g, r)
                                 for n, g, r in zip(st["names"], st["g4"], arrays)]
            elif stage == "chips":
                st["halves"] = [_final_add(tag("final_add", n), self.where, g, r, a)
                                for n, g, r, a in zip(st["names"], st["g4"], st["got"], arrays)]
            else:
                self.reduced.update(zip(st["names"], arrays))


SMALL_ROWS = 32
N_DEV = 8


def _all_reduce_small(pack):
    def body(p_ref, o_ref, buf, send, recv):
        x, y, c = _place()
        buf[4 * x + 2 * y + c] = p_ref[...]
        copies, waits = [], []
        for k in range(1, N_DEV):
            px = 1 - x if k & 4 else x
            py = 1 - y if k & 2 else y
            pc = 1 - c if k & 1 else c
            copies.append(_remote(p_ref, buf.at[4 * x + 2 * y + c], send, recv, k - 1, (px, py, pc)))
            waits.append(_remote(p_ref, buf.at[4 * px + 2 * py + pc], send, recv, k - 1, (px, py, pc)))
        for cp in copies:
            cp.start()
        for cp in waits:
            cp.wait_recv()
        acc = buf[0]
        for d in range(1, N_DEV):
            acc = acc + buf[d]
        o_ref[...] = acc
        for cp in copies:
            cp.wait_send()

    return pl.pallas_call(
        body, name="all_reduce_small",
        out_shape=jax.ShapeDtypeStruct(pack.shape, F32),
        scratch_shapes=[pltpu.VMEM((N_DEV,) + pack.shape, F32),
                        pltpu.SemaphoreType.DMA((N_DEV - 1,)), pltpu.SemaphoreType.DMA((N_DEV - 1,))],
    )(pack)


def _adamw(name, w, g, m, v):
    c1 = 1.0 - ADAM_B1 ** ADAM_STEP
    c2 = 1.0 - ADAM_B2 ** ADAM_STEP

    def fn(t, f):
        wv, gv, mv, vv = t
        m2 = ADAM_B1 * mv + (1.0 - ADAM_B1) * gv
        v2 = ADAM_B2 * vv + (1.0 - ADAM_B2) * (gv * gv)
        delta = -ADAM_LR * ((m2 / c1) / (jnp.sqrt(v2 / c2) + ADAM_EPS) + ADAM_WD * wv)
        return [delta, m2, v2], []

    cols = w.shape[1]
    return _ew(name, fn, [w, g, m, v], [], [(cols, F32)] * 3, ts=min(w.shape[0], 256))


def _adamw_all(ws, gs, ms, vs):
    n = len(ws)
    c1 = 1.0 - ADAM_B1 ** ADAM_STEP
    c2 = 1.0 - ADAM_B2 ** ADAM_STEP

    def body(*refs):
        ins, outs = refs[:4 * n], refs[4 * n:]
        for k in range(n):
            wv, gv, mv, vv = [ins[4 * k + t][...] for t in range(4)]
            m2 = ADAM_B1 * mv + (1.0 - ADAM_B1) * gv
            v2 = ADAM_B2 * vv + (1.0 - ADAM_B2) * (gv * gv)
            outs[3 * k][...] = -ADAM_LR * ((m2 / c1) / (jnp.sqrt(v2 / c2) + ADAM_EPS) + ADAM_WD * wv)
            outs[3 * k + 1][...] = m2
            outs[3 * k + 2][...] = v2

    spec = lambda a: pl.BlockSpec((a.shape[0] // ADAM_STEPS, a.shape[1]), lambda i: (i, 0))
    res = pl.pallas_call(
        body, name="adamw", grid=(ADAM_STEPS,),
        in_specs=[spec(w) for w in ws for _ in range(4)],
        out_specs=[spec(w) for w in ws for _ in range(3)],
        out_shape=[jax.ShapeDtypeStruct(w.shape, F32) for w in ws for _ in range(3)],
        compiler_params=_params("parallel"),
    )(*[a for quad in zip(ws, gs, ms, vs) for a in quad])
    return [tuple(res[3 * k:3 * k + 3]) for k in range(n)]


ADAM_STEPS = 16
LOSS_ROW = 26


def _pack_small(norm1_g, norm2_g, conv_b, b_gate, conv_w, q_norm_g, k_norm_g, rel_bias, loss=None):
    pack = jnp.zeros((SMALL_ROWS, D_MODEL), F32)
    for r0, v in ((0, norm1_g), (1, norm2_g), (2, conv_b), (3, b_gate.reshape(2, D_MODEL)), (5, conv_w),
                  (8, q_norm_g), (9, k_norm_g), (10, rel_bias)) + (((LOSS_ROW, loss),) if loss is not None else ()):
        v = v.reshape(-1, v.shape[-1]).astype(F32)
        pack = pack.at[r0:r0 + v.shape[0], :v.shape[1]].set(v)
    return pack


def _unpack_small(pack, conv_cols):
    return dict(norm1_g=pack[0], norm2_g=pack[1], conv_b=pack[2], b_gate=pack[3:5].reshape(2 * D_MODEL),
                conv_w=pack[5:8, :conv_cols], q_norm_g=pack[8, :HEAD_DIM], k_norm_g=pack[9, :HEAD_DIM],
                rel_bias=pack[10:10 + N_HEADS, :N_REL])


BIG = ["w_in", "w_attn_proj", "w_conv_proj", "w_gate", "w_out", "w_up", "w_down"]
LATE = ["w_attn_proj", "w_conv_proj", "w_out", "w_up", "w_down"]
WEIGHTS = ["norm1_g", "w_in", "q_norm_g", "k_norm_g", "rel_bias", "conv_w", "conv_b", "w_attn_proj",
           "w_conv_proj", "w_gate", "b_gate", "w_out", "norm2_g", "w_up", "w_down"]


def kernel(x, norm1_g, w_in, q_norm_g, k_norm_g, rel_bias, conv_w, conv_b, w_attn_proj, w_conv_proj, w_gate, b_gate, w_out, norm2_g, w_up, w_down, loss_target, m_norm1_g, m_w_in, m_q_norm_g, m_k_norm_g, m_rel_bias, m_conv_w, m_conv_b, m_w_attn_proj, m_w_conv_proj, m_w_gate, m_b_gate, m_w_out, m_norm2_g, m_w_up, m_w_down, v_norm1_g, v_w_in, v_q_norm_g, v_k_norm_g, v_rel_bias, v_conv_w, v_conv_b, v_w_attn_proj, v_w_conv_proj, v_w_gate, v_b_gate, v_w_out, v_norm2_g, v_w_up, v_w_down):
    given = dict(locals())
    w = {n: given[n] for n in WEIGHTS}
    m = {n: given["m_" + n] for n in WEIGHTS}
    v = {n: given["v_" + n] for n in WEIGHTS}
    s_len = x.shape[1]
    shard = 2 * lax.axis_index("x") + lax.axis_index("y")
    where = jnp.stack([shard, lax.axis_index("c")]).astype(jnp.int32)
    conv_cols = conv_w.shape[1]

    small_in = lax.dynamic_update_slice(jnp.zeros((N_SHARD, 16, conv_cols), F32), conv_w[None], (shard, 0, 0))
    slot = {n: _cast_into_slot("cast_" + n, where, w[n]) for n in BIG}
    comm = _Exchange(where, [slot["w_in"], small_in], [slot["w_gate"]], [slot[n] for n in LATE])

    loss_part, grad_x, _, small = _local_grads(
        x.reshape(s_len, D_MODEL), loss_target.reshape(s_len, D_MODEL), norm1_g, q_norm_g, k_norm_g,
        rel_bias, conv_b, b_gate, norm2_g, comm)
    grad = dict(comm.reduced)

    loss_local = _finish_loss(loss_part)
    pack = _pack_small(small["norm1_g"], small["norm2_g"], small["conv_wb"][3], jnp.concatenate(small["b_gate"], axis=1),
                       small["conv_wb"][0:3], small["q_norm_g"], small["k_norm_g"], small["rel_bias"],
                       loss=loss_local[0:1, :])
    total = _all_reduce_small(pack)
    g_small = _unpack_small(total, D_MODEL)
    g_small["conv_w"] = lax.dynamic_slice(g_small["conv_w"], (0, shard * conv_cols), (3, conv_cols))
    grad.update(g_small)

    delta, new_m, new_v = {}, {}, {}
    for n, out in zip(BIG, _adamw_all(*[[src[n] for n in BIG] for src in (w, grad, m, v)])):
        delta[n], new_m[n], new_v[n] = out
    small_names = [n for n in WEIGHTS if n not in BIG]
    packs = [_pack_small(**{n: src[n] for n in small_names}) for src in (w, grad, m, v)]
    for out, packed in zip((delta, new_m, new_v), _adamw("adamw_small", *packs)):
        out.update({n: a.reshape(w[n].shape) for n, a in _unpack_small(packed, conv_cols).items()})

    outs = [total[LOSS_ROW, 0], grad_x.reshape(x.shape)]
    for group in (grad, delta, new_m, new_v):
        outs += [group[n].reshape(w[n].shape) for n in WEIGHTS]
    return tuple(outs)
```

```python
import jax
import jax.numpy as jnp
from jax import lax
from jax.experimental import pallas as pl
from jax.experimental.pallas import tpu as pltpu

F32 = jnp.float32
BF16 = jnp.bfloat16

D_MODEL = 1024
N_HEADS = 16
HEAD_DIM = 64
CHUNK = 64
N_PREV_CHUNKS = 8
MAX_REL = 256
D_FF = 4096
N_REL = 2 * MAX_REL + 1
REL_PAD = 640
EPS = 1e-6
NEG_INF = -1e30
QK_SCALE = HEAD_DIM ** -0.5

SUPER = 4 * CHUNK
BAND = SUPER + N_PREV_CHUNKS * CHUNK
SKEW_W = 1024
N_SHARD = 4
LANE = 128
MXU_DIM = 256
VMEM_LIMIT = 48 * 1024 * 1024

ADAM_LR = 0.001
ADAM_B1 = 0.9
ADAM_B2 = 0.999
ADAM_EPS = 1e-08
ADAM_WD = 0.01
ADAM_STEP = 10

MESH = pl.DeviceIdType.MESH
NN = (((1,), (0,)), ((), ()))
NT = (((1,), (1,)), ((), ()))
TN = (((0,), (0,)), ((), ()))


def _params(*sem):
    return pltpu.CompilerParams(dimension_semantics=sem or None, vmem_limit_bytes=VMEM_LIMIT)


HBM_SPEC = pl.BlockSpec(memory_space=pl.ANY)


class _Rider:
    def __init__(self, sources, arrays, n_sem, start, finish):
        self.sources, self.arrays, self.n_sem, self.start, self.finish = sources, arrays, n_sem, start, finish


def _carry(riders, body, *, name, out_shape, grid=(), in_specs=None, out_specs=None, scratch_shapes=(),
           semantics=(), input_output_aliases=None):
    aliases = dict(input_output_aliases or {})
    if not riders:
        kw = {} if in_specs is None else dict(in_specs=in_specs, out_specs=out_specs)
        return pl.pallas_call(body, name=name, grid=grid, out_shape=out_shape, scratch_shapes=scratch_shapes,
                              input_output_aliases=aliases, compiler_params=_params(*semantics), **kw)
    single = not isinstance(out_shape, (list, tuple))
    shapes = [out_shape] if single else list(out_shape)
    n_out, n_scr = len(shapes), len(scratch_shapes)
    in_hbm = lambda a: pltpu.with_memory_space_constraint(a, pltpu.HBM)
    srcs = [in_hbm(a) for r in riders for a in r.sources]
    arrs = [in_hbm(a) for r in riders for a in r.arrays]
    vmem = pl.BlockSpec(memory_space=pltpu.VMEM)

    def run(*args):
        n_in = len(args)

        def wrapped(*refs):
            pos = n_in
            src_refs = refs[pos:pos + len(srcs)]
            pos += len(srcs) + len(arrs)
            outs = refs[pos:pos + n_out]
            pos += n_out
            arr_refs = refs[pos:pos + len(arrs)]
            pos += len(arrs)
            scratch = refs[pos:pos + n_scr]
            sems = refs[pos + n_scr:]
            first, last = True, True
            for d, size in enumerate(grid):
                first = jnp.logical_and(first, pl.program_id(d) == 0)
                last = jnp.logical_and(last, pl.program_id(d) == size - 1)

            def each(method):
                s0 = a0 = 0
                for k, r in enumerate(riders):
                    getattr(r, method)(src_refs[s0:s0 + len(r.sources)], arr_refs[a0:a0 + len(r.arrays)],
                                       sems[2 * k], sems[2 * k + 1])
                    s0, a0 = s0 + len(r.sources), a0 + len(r.arrays)

            pl.when(first)(lambda: each("start"))
            body(*refs[:n_in], *outs, *scratch)
            pl.when(last)(lambda: each("finish"))

        ins = [vmem] * n_in if in_specs is None else list(in_specs)
        if out_specs is None:
            o_specs = [vmem] * n_out
        else:
            o_specs = [out_specs] if single else list(out_specs)
        for k in range(len(arrs)):
            aliases[n_in + len(srcs) + k] = n_out + k
        res = pl.pallas_call(
            wrapped, name=name, grid=grid,
            in_specs=ins + [HBM_SPEC] * (len(srcs) + len(arrs)),
            out_specs=o_specs + [HBM_SPEC] * len(arrs),
            out_shape=shapes + [jax.ShapeDtypeStruct(a.shape, a.dtype) for a in arrs],
            scratch_shapes=list(scratch_shapes) + [pltpu.SemaphoreType.DMA((r.n_sem,)) for r in riders for _ in range(2)],
            input_output_aliases=aliases,
            compiler_params=_params(*["arbitrary"] * len(grid)),
        )(*args, *srcs, *arrs)
        core, rest = res[:n_out], list(res[n_out:])
        carried, a0 = [], 0
        for r in riders:
            carried.append(rest[a0:a0 + len(r.arrays)])
            a0 += len(r.arrays)
        return (core[0] if single else core), carried

    return run


def _mm(name, dims, a, a_spec, b, b_spec, grid, tile, outs, epilogue=None, extras=(), riders=()):
    nk, ne, no = grid[2], len(extras), len(outs)

    def body(a_ref, b_ref, *refs):
        e_refs, o_refs = refs[:ne], refs[ne:ne + no]
        part = lax.dot_general(a_ref[...], b_ref[...], dims, preferred_element_type=F32)

        def finish(acc):
            if epilogue is None:
                o_refs[0][...] = acc.astype(o_refs[0].dtype)
            else:
                epilogue(acc, [r[...] for r in e_refs], o_refs)

        if nk == 1:
            finish(part)
        else:
            acc_ref = refs[ne + no]
            k = pl.program_id(2)

            @pl.when(k == 0)
            def _():
                acc_ref[...] = part

            @pl.when(k > 0)
            def _():
                acc_ref[...] += part

            @pl.when(k == nk - 1)
            def _():
                finish(acc_ref[...])

    res = _carry(
        riders, body, name=name, grid=grid,
        in_specs=[a_spec, b_spec] + [s for _, s in extras],
        out_specs=[s for _, s in outs],
        out_shape=[s for s, _ in outs],
        scratch_shapes=[pltpu.VMEM(tile, F32)] if nk > 1 else [],
        semantics=("parallel", "parallel", "arbitrary"),
    )(a, b, *[e for e, _ in extras])
    res, carried = res if riders else (res, None)
    res = res[0] if no == 1 else res
    return (res, carried) if riders else res


def _tile_spec(tm, tn, col0=0):
    return pl.BlockSpec((tm, tn), lambda i, j, k: (i, j + col0))


def _out2d(m, n, dtype, tm, tn):
    return (jax.ShapeDtypeStruct((m, n), dtype), _tile_spec(tm, tn))


def _mm_fwd(name, a, w, tm, tn, tk, outs=None, epilogue=None, extras=(), col0=0, ncols=None, riders=()):
    m, kdim = a.shape
    if w.ndim == 3:
        per = w.shape[2] // tn
        n = ncols or N_SHARD * w.shape[2]
        w_spec = pl.BlockSpec((None, tk, tn), lambda i, j, k: ((j + col0) // per, k, (j + col0) % per))
    else:
        n = ncols or w.shape[1]
        w_spec = pl.BlockSpec((tk, tn), lambda i, j, k: (k, j + col0))
    if outs is None:
        outs = [_out2d(m, n, F32, tm, tn)]
    return _mm(name, NN, a, pl.BlockSpec((tm, tk), lambda i, j, k: (i, k)), w, w_spec,
               (m // tm, n // tn, kdim // tk), (tm, tn), outs, epilogue, extras, riders)


def _mm_bwd_x(name, g, g_spec, w, tm, tj, tc, m, n_contract, outs=None, epilogue=None, extras=(), riders=()):
    if w.ndim == 3:
        per = w.shape[2] // tc
        kdim = w.shape[1]
        w_spec = pl.BlockSpec((None, tj, tc), lambda i, j, n: (n // per, j, n % per))
    else:
        kdim = w.shape[0]
        w_spec = pl.BlockSpec((tj, tc), lambda i, j, n: (j, n))
    if outs is None:
        outs = [_out2d(m, kdim, F32, tm, tj)]
    return _mm(name, NT, g, g_spec, w, w_spec, (m // tm, kdim // tj, n_contract // tc),
               (tm, tj), outs, epilogue, extras, riders)


def _mm_bwd_w(name, a, g, g_spec, n, tk, tn, tm, sharded, riders=()):
    m, kdim = a.shape
    if sharded:
        per = (n // N_SHARD) // tn
        out = (jax.ShapeDtypeStruct((N_SHARD, kdim, n // N_SHARD), F32),
               pl.BlockSpec((None, tk, tn), lambda i, j, mm: (j // per, i, j % per)))
    else:
        out = (jax.ShapeDtypeStruct((kdim, n), F32), pl.BlockSpec((tk, tn), lambda i, j, mm: (i, j)))
    return _mm(name, TN, a, pl.BlockSpec((tm, tk), lambda i, j, mm: (mm, i)), g, g_spec,
               (kdim // tk, n // tn, m // tm), (tk, tn), [out], riders=riders)


def _ew(name, fn, tiles, fulls, outs, sums=(), ts=512, riders=()):
    tiles = [t if isinstance(t, tuple) else (t, pl.BlockSpec((ts, t.shape[1]), lambda i: (i, 0)))
             for t in tiles]
    s_rows = tiles[0][0].shape[-2]
    nt, nf, no = len(tiles), len(fulls), len(outs)

    def body(*refs):
        t_vals = [r[...] for r in refs[:nt]]
        f_vals = [r[...] for r in refs[nt:nt + nf]]
        o_refs, s_refs = refs[nt + nf:nt + nf + no], refs[nt + nf + no:]
        o_vals, s_vals = fn(t_vals, f_vals)
        for r, v in zip(o_refs, o_vals):
            r[...] = v.astype(r.dtype)
        for r, v in zip(s_refs, s_vals):
            part = jnp.sum(v, axis=0, keepdims=True)

            @pl.when(pl.program_id(0) == 0)
            def _():
                r[...] = part

            @pl.when(pl.program_id(0) > 0)
            def _():
                r[...] += part

    full_specs = [pl.BlockSpec(f.shape, lambda i, nd=f.ndim: (0,) * nd) for f in fulls]
    return _carry(
        riders, body, name=name, grid=(s_rows // ts,),
        in_specs=[s for _, s in tiles] + full_specs,
        out_specs=[pl.BlockSpec((ts, c), lambda i: (i, 0)) for c, _ in outs]
        + [pl.BlockSpec((1, c), lambda i: (0, 0)) for c in sums],
        out_shape=[jax.ShapeDtypeStruct((s_rows, c), dt) for c, dt in outs]
        + [jax.ShapeDtypeStruct((1, c), F32) for c in sums],
        semantics=("arbitrary",),
    )(*[t for t, _ in tiles], *fulls)


def _rms_fwd(name, x, g, riders=()):
    def fn(t, f):
        xv = t[0]
        r = lax.rsqrt(jnp.mean(xv * xv, axis=-1, keepdims=True) + EPS)
        return [xv * r * f[0]], []
    out = _ew(name, fn, [x], [g], [(x.shape[1], BF16)], riders=riders)
    return (out[0][0], out[1]) if riders else out[0]


def _split3(x):
    x1 = x.astype(BF16)
    r1 = x - x1.astype(F32)
    x2 = r1.astype(BF16)
    x3 = (r1 - x2.astype(F32)).astype(BF16)
    return x1, x2, x3


def _rel_class(cp):
    far = (cp < MAX_REL) | (cp > BAND)
    return jnp.where(far, 2 * MAX_REL, BAND - cp)


def _skew_rows(x, sign):
    row = lax.broadcasted_iota(jnp.int32, x.shape, 0)
    for b in range(CHUNK.bit_length() - 1):
        shift = (1 << b) if sign > 0 else SKEW_W - (1 << b)
        x = jnp.where((row >> b) & 1 == 1, pltpu.roll(x, shift, 1), x)
    return x


def _roll_lanes(x, shift):
    return x if shift % SKEW_W == 0 else pltpu.roll(x, shift % SKEW_W, 1)


N_START = 3


def _bias_expand(rel_bias, riders=()):
    rel = jnp.pad(rel_bias, ((0, 0), (0, REL_PAD - N_REL))).reshape(N_HEADS, 1, REL_PAD)

    def body(rel_ref, o_ref):
        cls = lax.broadcasted_iota(jnp.int32, (REL_PAD, SKEW_W), 0)
        cp = lax.broadcasted_iota(jnp.int32, (REL_PAD, SKEW_W), 1)
        onehot = (cls == _rel_class(cp)).astype(BF16)
        rel8 = jnp.broadcast_to(rel_ref[...], (8, REL_PAD))
        trow = sum(jnp.dot(p, onehot, preferred_element_type=F32) for p in _split3(rel8))[0:1]
        first = _skew_rows(jnp.broadcast_to(trow, (CHUNK, SKEW_W)), +1)
        full = jnp.concatenate([_roll_lanes(first, CHUNK * g) for g in range(SUPER // CHUNK)], axis=0)[:, :BAND]
        qc = lax.broadcasted_iota(jnp.int32, (SUPER, BAND), 0) // CHUNK
        kc = lax.broadcasted_iota(jnp.int32, (SUPER, BAND), 1) // CHUNK
        on_band = (kc >= qc) & (kc <= qc + N_PREV_CHUNKS)
        table = jnp.where(on_band, full, NEG_INF).T
        key = lax.broadcasted_iota(jnp.int32, (BAND, SUPER), 0)
        for t in range(N_START):
            o_ref[t] = jnp.where(key < (N_START - 1 - t) * SUPER, NEG_INF, table)

    return _carry(
        riders, body, name="bias_expand", grid=(N_HEADS,),
        in_specs=[pl.BlockSpec((None, 1, REL_PAD), lambda h: (h, 0, 0))],
        out_specs=pl.BlockSpec((N_START, None, BAND, SUPER), lambda h: (0, h, 0, 0)),
        out_shape=jax.ShapeDtypeStruct((N_START, N_HEADS, BAND, SUPER), F32),
        semantics=("arbitrary",),
    )(rel)


def _bias_reduce(dbias, riders=()):
    def body(d_ref, o_ref):
        x = jnp.concatenate([d_ref[...].T, jnp.zeros((SUPER, SKEW_W - BAND), F32)], axis=1)
        folded = sum(_roll_lanes(x[CHUNK * g:CHUNK * (g + 1)], -CHUNK * g) for g in range(SUPER // CHUNK))
        diag = jnp.sum(_skew_rows(folded, -1), axis=0, keepdims=True)
        cp = lax.broadcasted_iota(jnp.int32, (SKEW_W, REL_PAD), 0)
        cls = lax.broadcasted_iota(jnp.int32, (SKEW_W, REL_PAD), 1)
        onehot = (cls == _rel_class(cp)).astype(BF16)
        diag8 = jnp.broadcast_to(diag, (8, SKEW_W))
        o_ref[...] = sum(jnp.dot(p, onehot, preferred_element_type=F32) for p in _split3(diag8))[0:1]

    out = _carry(
        riders, body, name="bias_reduce", grid=(N_HEADS,),
        in_specs=[pl.BlockSpec((None, BAND, SUPER), lambda h: (h, 0, 0))],
        out_specs=pl.BlockSpec((None, 1, REL_PAD), lambda h: (h, 0, 0)),
        out_shape=jax.ShapeDtypeStruct((N_HEADS, 1, REL_PAD), F32),
        semantics=("arbitrary",),
    )(dbias)
    out, carried = out if riders else (out, None)
    out = out.reshape(N_HEADS, REL_PAD)[:, :N_REL]
    return (out, carried) if riders else out


HEADS_PER_STEP = 8
HEAD_COLS = HEADS_PER_STEP * HEAD_DIM
N_HEAD_GROUPS = N_HEADS // HEADS_PER_STEP


def _scores_t(qs, kn, bias_t):
    return jnp.concatenate([lax.dot_general(k, qs, NT, preferred_element_type=F32) for k in kn], axis=0) + bias_t


def _bias_spec():
    return pl.BlockSpec((None, HEADS_PER_STEP, BAND, SUPER), lambda hg, i: (jnp.minimum(i, N_START - 1), hg, 0, 0))


def _band_specs(nb, col0, clamp_hi):
    def spec(d):
        def index(hg, i):
            blk = jnp.maximum(i - d, 0)
            if clamp_hi:
                blk = jnp.minimum(blk, nb - 1)
            return (blk, col0 + hg)
        return pl.BlockSpec((SUPER, HEAD_COLS), index)
    return [spec(2), spec(1), spec(0)]


def _head_sums(y):
    same_head = (lax.broadcasted_iota(jnp.int32, (MXU_DIM, MXU_DIM), 0) // HEAD_DIM
                 == lax.broadcasted_iota(jnp.int32, (MXU_DIM, MXU_DIM), 1) // HEAD_DIM).astype(BF16)
    sums = []
    for c0 in range(0, y.shape[1], MXU_DIM):
        chunk = y[:, c0:c0 + MXU_DIM]
        hi = chunk.astype(BF16)
        lo = (chunk - hi.astype(F32)).astype(BF16)
        sums.append(jnp.dot(hi, same_head, preferred_element_type=F32)
                    + jnp.dot(lo, same_head, preferred_element_type=F32))
    return jnp.concatenate(sums, axis=1)


def _head_unit(x):
    r = lax.rsqrt(_head_sums(x * x) * (1.0 / HEAD_DIM) + EPS)
    return x * r, r


def _head(hh):
    return slice(HEAD_DIM * hh, HEAD_DIM * (hh + 1))


def _key_block(j):
    return slice(SUPER * j, SUPER * (j + 1))


LSE_ROWS = 8


def _attn_fwd(qkn, v, bias, riders=()):
    s_len = qkn.shape[0]
    nb = s_len // SUPER

    def body(q_ref, k0, k1, k2, v0, v1, v2, b_ref, o_ref, lse_ref):
        outs = []
        v_t = [v0[...].T, v1[...].T, v2[...].T]

        def probabilities(hh):
            sl = _head(hh)
            s = _scores_t(q_ref[:, sl], [k0[:, sl], k1[:, sl], k2[:, sl]], b_ref[hh])
            m = jnp.max(s, axis=0, keepdims=True)
            e = jnp.exp(s - m)
            l = jnp.sum(e, axis=0, keepdims=True)
            lse_ref[hh:hh + 1, :] = m + jnp.log(l)
            return e.astype(BF16), 1.0 / l, sl

        def weighted_values(e, inv_l, sl):
            outs.append(sum(jnp.dot(v_t[j][sl, :], e[_key_block(j), :], preferred_element_type=F32)
                            for j in range(3)) * inv_l)

        ready = probabilities(0)
        for hh in range(1, HEADS_PER_STEP):
            following = probabilities(hh)
            weighted_values(*ready)
            ready = following
        weighted_values(*ready)
        o_ref[...] = jnp.concatenate(outs, axis=0).T.astype(o_ref.dtype)

    return _carry(
        riders, body, name="attn_fwd", grid=(N_HEAD_GROUPS, nb),
        in_specs=[pl.BlockSpec((SUPER, HEAD_COLS), lambda hg, i: (i, hg))]
        + _band_specs(nb, N_HEAD_GROUPS, False) + _band_specs(nb, 0, False) + [_bias_spec()],
        out_specs=[pl.BlockSpec((SUPER, HEAD_COLS), lambda hg, i: (i, hg)),
                   pl.BlockSpec((None, LSE_ROWS, SUPER), lambda hg, i: (hg, 0, i))],
        out_shape=[jax.ShapeDtypeStruct((s_len, D_MODEL), BF16),
                   jax.ShapeDtypeStruct((N_HEAD_GROUPS, LSE_ROWS, s_len), F32)],
        semantics=("parallel", "arbitrary"),
    )(qkn, qkn, qkn, qkn, v, v, v, bias)


def _attn_bwd(qkn, v, out, d_out, bias, lse, riders=()):
    s_len = qkn.shape[0]
    nb = s_len // SUPER

    def body(q_ref, k0, k1, k2, v0, v1, v2, o_ref, do_ref, b_ref, lse_ref, dp_ref, db_ref, aq_ref, ak_ref, av_ref):
        i = pl.program_id(1)

        @pl.when(i == 0)
        def _():
            aq_ref[...] = jnp.zeros_like(aq_ref)
            ak_ref[...] = jnp.zeros_like(ak_ref)
            av_ref[...] = jnp.zeros_like(av_ref)
            db_ref[...] = jnp.zeros_like(db_ref)

        @pl.when(i < nb)
        def _():
            dq, dk, dv = [], [[], [], []], [[], [], []]
            ones = jnp.ones((8, HEAD_DIM), BF16)
            k_t = [k0[...].T, k1[...].T, k2[...].T]

            def softmax_grad(hh):
                sl = _head(hh)
                qs, do = q_ref[:, sl], do_ref[:, sl]
                kn = [k0[:, sl], k1[:, sl], k2[:, sl]]
                prod = do.astype(F32) * o_ref[:, sl].astype(F32)
                hi = prod.astype(BF16)
                lo = (prod - hi.astype(F32)).astype(BF16)
                delta = (lax.dot_general(ones, hi, NT, preferred_element_type=F32)
                         + lax.dot_general(ones, lo, NT, preferred_element_type=F32))[0:1]
                lse_row = lse_ref[hh:hh + 1, :]
                pb, dsb = [], []
                for j, (kj, vj) in enumerate(zip(kn, (v0, v1, v2))):
                    rows = _key_block(j)
                    p = jnp.exp(lax.dot_general(kj, qs, NT, preferred_element_type=F32) + b_ref[hh, rows, :] - lse_row)
                    ds = p * (lax.dot_general(vj[:, sl], do, NT, preferred_element_type=F32) - delta)
                    db_ref[hh, rows, :] += ds
                    pb.append(p.astype(BF16))
                    dsb.append(ds.astype(BF16))
                return pb, dsb, qs, do, sl

            def operand_grads(pb, dsb, qs, do, sl):
                dq.append(sum(jnp.dot(k_t[j][sl, :], dsb[j], preferred_element_type=F32) for j in range(3)))
                for j in range(3):
                    dv[j].append(jnp.dot(pb[j], do, preferred_element_type=F32))
                    dk[j].append(jnp.dot(dsb[j], qs, preferred_element_type=F32))

            ready = softmax_grad(0)
            for hh in range(1, HEADS_PER_STEP):
                following = softmax_grad(hh)
                operand_grads(*ready)
                ready = following
            operand_grads(*ready)
            aq_ref[i % 3] = jnp.concatenate(dq, axis=0).T
            for j in range(3):
                slot = (i + 1 + j) % 3
                if j < 2:
                    ak_ref[slot] += jnp.concatenate(dk[j], axis=1)
                    av_ref[slot] += jnp.concatenate(dv[j], axis=1)
                else:
                    ak_ref[slot] = jnp.concatenate(dk[j], axis=1)
                    av_ref[slot] = jnp.concatenate(dv[j], axis=1)

        slot = (i + 1) % 3
        dp_ref[0] = aq_ref[slot].astype(dp_ref.dtype)
        dp_ref[1] = ak_ref[slot].astype(dp_ref.dtype)
        dp_ref[2] = av_ref[slot].astype(dp_ref.dtype)

    def qrow(hg, i):
        return (jnp.minimum(i, nb - 1), hg)

    return _carry(
        riders, body, name="attn_bwd", grid=(N_HEAD_GROUPS, nb + 2),
        in_specs=[pl.BlockSpec((SUPER, HEAD_COLS), qrow)]
        + _band_specs(nb, N_HEAD_GROUPS, True) + _band_specs(nb, 0, True)
        + [pl.BlockSpec((SUPER, HEAD_COLS), qrow), pl.BlockSpec((SUPER, HEAD_COLS), qrow), _bias_spec(),
           pl.BlockSpec((None, LSE_ROWS, SUPER), lambda hg, i: (hg, 0, jnp.minimum(i, nb - 1)))],
        out_specs=[pl.BlockSpec((3, SUPER, HEAD_COLS), lambda hg, i: (0, jnp.maximum(i - 2, 0), hg)),
                   pl.BlockSpec((HEADS_PER_STEP, BAND, SUPER), lambda hg, i: (hg, 0, 0))],
        out_shape=[jax.ShapeDtypeStruct((6, s_len, D_MODEL), BF16),
                   jax.ShapeDtypeStruct((N_HEADS, BAND, SUPER), F32)],
        scratch_shapes=[pltpu.VMEM((3, SUPER, HEAD_COLS), F32)] * 3,
        semantics=("parallel", "arbitrary"),
    )(qkn, qkn, qkn, qkn, v, v, v, out, d_out, bias, lse)


def _qk_norm_bwd(dproj6, qk_raw, gq, gk):
    s_len = qk_raw.shape[0]
    ts = min(1024, s_len)

    nsteps = s_len // ts
    half = D_MODEL // 2

    def body(d_ref, raw_ref, gq_ref, gk_ref, o_ref, dgq_ref, dgk_ref, acc_ref):
        step = pl.program_id(0)

        @pl.when(step == 0)
        def _():
            acc_ref[...] = jnp.zeros_like(acc_ref)

        for piece, (g_ref, scale) in enumerate(((gq_ref, QK_SCALE), (gk_ref, 1.0))):
            for c0 in (0, half):
                xhat, r = _head_unit(raw_ref[:, piece * D_MODEL + c0:piece * D_MODEL + c0 + half].astype(F32))
                dn = d_ref[piece, :, c0:c0 + half].astype(F32) * scale
                u = dn * g_ref[...]
                dx = r * (u - xhat * (_head_sums(u * xhat) * (1.0 / HEAD_DIM)))
                o_ref[piece, :, c0:c0 + half] = dx.astype(o_ref.dtype)
                acc_ref[piece:piece + 1, c0:c0 + half] += jnp.sum(dn * xhat, axis=0, keepdims=True)

        @pl.when(step == nsteps - 1)
        def _():
            lane = lax.broadcasted_iota(jnp.int32, (D_MODEL, LANE), 0) % HEAD_DIM
            fold = (lane == lax.broadcasted_iota(jnp.int32, (D_MODEL, LANE), 1)).astype(BF16)
            tot = sum(jnp.dot(p, fold, preferred_element_type=F32) for p in _split3(acc_ref[...]))
            dgq_ref[...] = tot[0:1, :HEAD_DIM]
            dgk_ref[...] = tot[1:2, :HEAD_DIM]

    gain = pl.BlockSpec((1, half), lambda i: (0, 0))
    small = pl.BlockSpec((1, HEAD_DIM), lambda i: (0, 0))
    per_head = lambda g: jnp.tile(g, (1, half // HEAD_DIM))
    return pl.pallas_call(
        body, name="qk_norm_bwd", grid=(nsteps,),
        in_specs=[pl.BlockSpec((2, ts, D_MODEL), lambda i: (0, i, 0)),
                  pl.BlockSpec((ts, 2 * D_MODEL), lambda i: (i, 0)), gain, gain],
        out_specs=[pl.BlockSpec((2, ts, D_MODEL), lambda i: (0, i, 0)), small, small],
        out_shape=[jax.ShapeDtypeStruct(dproj6.shape, dproj6.dtype),
                   jax.ShapeDtypeStruct((1, HEAD_DIM), F32), jax.ShapeDtypeStruct((1, HEAD_DIM), F32)],
        scratch_shapes=[pltpu.VMEM((8, D_MODEL), F32)],
        input_output_aliases={0: 0},
        compiler_params=_params("arbitrary"),
    )(dproj6, qk_raw, per_head(gq), per_head(gk))


CONV_ROWS = 512
HALO = 16


def _rows_with_halo(ref, r0, n, front, s_len):
    zeros = jnp.zeros((HALO, ref.shape[1]), F32)
    if front:
        return (jnp.concatenate([zeros, ref[0:n, :].astype(F32)], axis=0) if r0 == 0
                else ref[r0 - HALO:r0 + n, :].astype(F32))
    return (jnp.concatenate([ref[r0:r0 + n, :].astype(F32), zeros], axis=0) if r0 + n == s_len
            else ref[r0:r0 + n + HALO, :].astype(F32))


def _earlier(ext, k):
    return pltpu.roll(ext, k, 0)[HALO:]


def _later(ext, k):
    n = ext.shape[0]
    return pltpu.roll(ext, n - k, 0)[:n - HALO]


def _conv_cols(col0):
    return lambda s_len: pl.BlockSpec((s_len, LANE), lambda j: (0, col0 + j))


def _conv_fwd(proj, conv_w, conv_b, riders=()):
    s_len = proj.shape[0]

    def body(bg_ref, cg_ref, xc_ref, w_ref, b_ref, o_ref):
        w = [w_ref[t:t + 1, :] for t in range(3)]
        for r0 in range(0, s_len, CONV_ROWS):
            u = _rows_with_halo(cg_ref, r0, CONV_ROWS, True, s_len) * \
                _rows_with_halo(xc_ref, r0, CONV_ROWS, True, s_len)
            conv = b_ref[...] + w[0] * _earlier(u, 2) + w[1] * _earlier(u, 1) + w[2] * u[HALO:]
            o_ref[r0:r0 + CONV_ROWS, :] = (bg_ref[r0:r0 + CONV_ROWS, :].astype(F32) * conv).astype(o_ref.dtype)

    return _carry(
        riders, body, name="conv_fwd", grid=(D_MODEL // LANE,),
        in_specs=[_conv_cols(0)(s_len), _conv_cols(8)(s_len), _conv_cols(16)(s_len),
                  pl.BlockSpec((3, LANE), lambda j: (0, j)), pl.BlockSpec((1, LANE), lambda j: (0, j))],
        out_specs=pl.BlockSpec((s_len, LANE), lambda j: (0, j)),
        out_shape=jax.ShapeDtypeStruct((s_len, D_MODEL), BF16),
        semantics=("parallel",),
    )(proj, proj, proj, conv_w, conv_b)


def _conv_bwd(dproj6, dy, proj, conv_w, conv_b, riders=()):
    s_len = proj.shape[0]

    def body(dy_ref, bg_ref, cg_ref, xc_ref, w_ref, b_ref, _, dp_ref, dw_ref):
        w = [w_ref[t:t + 1, :] for t in range(3)]
        acc = [jnp.zeros((1, LANE), F32) for _ in range(4)]
        for r0 in range(0, s_len, CONV_ROWS):
            rows = slice(r0, r0 + CONV_ROWS)
            u = _rows_with_halo(cg_ref, r0, CONV_ROWS, True, s_len) * \
                _rows_with_halo(xc_ref, r0, CONV_ROWS, True, s_len)
            u2, u1, u0 = _earlier(u, 2), _earlier(u, 1), u[HALO:]
            conv = b_ref[...] + w[0] * u2 + w[1] * u1 + w[2] * u0
            dp_ref[0, rows, :] = (dy_ref[rows, :].astype(F32) * conv).astype(dp_ref.dtype)
            dconv_ext = _rows_with_halo(dy_ref, r0, CONV_ROWS, False, s_len) * \
                _rows_with_halo(bg_ref, r0, CONV_ROWS, False, s_len)
            dconv = dconv_ext[:CONV_ROWS]
            for t, term in enumerate([dconv * u2, dconv * u1, dconv * u0, dconv]):
                acc[t] = acc[t] + jnp.sum(term, axis=0, keepdims=True)
            du = w[2] * dconv + w[1] * _later(dconv_ext, 1) + w[0] * _later(dconv_ext, 2)
            dp_ref[1, rows, :] = (du * xc_ref[rows, :].astype(F32)).astype(dp_ref.dtype)
            dp_ref[2, rows, :] = (du * cg_ref[rows, :].astype(F32)).astype(dp_ref.dtype)
        dw_ref[...] = jnp.zeros_like(dw_ref)
        for t in range(4):
            dw_ref[t:t + 1, :] = acc[t]

    return _carry(
        riders, body, name="conv_bwd", grid=(D_MODEL // LANE,),
        in_specs=[pl.BlockSpec((s_len, LANE), lambda j: (0, j)),
                  _conv_cols(0)(s_len), _conv_cols(8)(s_len), _conv_cols(16)(s_len),
                  pl.BlockSpec((3, LANE), lambda j: (0, j)), pl.BlockSpec((1, LANE), lambda j: (0, j)),
                  pl.BlockSpec(memory_space=pl.ANY)],
        out_specs=[pl.BlockSpec((3, s_len, LANE), lambda j: (1, 0, j)),
                   pl.BlockSpec((8, LANE), lambda j: (0, j))],
        out_shape=[jax.ShapeDtypeStruct(dproj6.shape, dproj6.dtype),
                   jax.ShapeDtypeStruct((8, D_MODEL), F32)],
        input_output_aliases={6: 0},
        semantics=("parallel",),
    )(dy, proj, proj, proj, conv_w, conv_b, dproj6)


def _d_norm_input(name, terms, x, g, dres, out_dtype, riders=()):
    s_len, kdim = terms[0][0].shape[1], terms[0][1].shape[1]
    tm, chunk = 512, 512
    n = len(terms)

    def body(*refs):
        x_ref, g_ref, r_ref, o_ref, dg_ref = refs[2 * n:]
        dh = jnp.zeros((tm, kdim), F32)
        for p_ref, w_ref in zip(refs[:n], refs[n:2 * n]):
            n_piece, _, width = p_ref.shape
            per_shard = w_ref.shape[2]
            for c0 in range(0, n_piece * width, chunk):
                dh = dh + lax.dot_general(p_ref[c0 // width, :, c0 % width:c0 % width + chunk],
                                          w_ref[c0 // per_shard, :, c0 % per_shard:c0 % per_shard + chunk],
                                          NT, preferred_element_type=F32)
        xv = x_ref[...]
        r = lax.rsqrt(jnp.mean(xv * xv, axis=-1, keepdims=True) + EPS)
        xhat = xv * r
        u = dh * g_ref[...]
        dx = r * (u - xhat * jnp.mean(u * xhat, axis=-1, keepdims=True)) + r_ref[...].astype(F32)
        o_ref[...] = dx.astype(o_ref.dtype)
        part = jnp.sum(dh * xhat, axis=0, keepdims=True)

        @pl.when(pl.program_id(0) == 0)
        def _():
            dg_ref[...] = part

        @pl.when(pl.program_id(0) > 0)
        def _():
            dg_ref[...] += part

    tile = pl.BlockSpec((tm, kdim), lambda i: (i, 0))
    vec = pl.BlockSpec((1, kdim), lambda i: (0, 0))
    return _carry(
        riders, body, name=name, grid=(s_len // tm,),
        in_specs=[pl.BlockSpec((p.shape[0], tm, p.shape[2]), lambda i: (0, i, 0)) for p, _ in terms]
        + [pl.BlockSpec(w.shape, lambda i: (0, 0, 0), pipeline_mode=pl.Buffered(1)) for _, w in terms]
        + [tile, vec, tile],
        out_specs=[tile, vec],
        out_shape=[jax.ShapeDtypeStruct((s_len, kdim), out_dtype), jax.ShapeDtypeStruct((1, kdim), F32)],
        semantics=("arbitrary",),
    )(*[p for p, _ in terms], *[w for _, w in terms], x, g, dres)


def _local_grads(x, target, norm1_g, q_norm_g, k_norm_g, rel_bias, conv_b, b_gate, norm2_g, comm):
    s_len = x.shape[0]
    tm = min(1024, s_len)
    tb = min(2048, s_len)
    row = lambda v: v.reshape(1, -1)

    def carrying(at, fn, *args, **kw):
        riders = comm.riders(at)
        out = fn(*args, riders=riders, **kw)
        if riders:
            out, carried = out
            comm.done(at, carried)
        return out

    bias = carrying("bias_expand", _bias_expand, rel_bias)
    h = carrying("norm1", _rms_fwd, "norm1", x, row(norm1_g))
    w_in3, conv_w = comm.early_weights()
    gq, gk = row(q_norm_g), row(k_norm_g)

    def qk_epi(acc, e, o):
        o[0][...] = acc.astype(BF16)
        gain = jnp.where(pl.program_id(1) < 2, e[0] * QK_SCALE, e[1])
        o[1][...] = (_head_unit(acc)[0] * gain).astype(BF16)
    conv_in = carrying("proj_conv", _mm_fwd, "proj_conv", h, w_in3, tb, 1536, D_MODEL, col0=2, ncols=3 * D_MODEL,
                       outs=[_out2d(s_len, 3 * D_MODEL, BF16, tb, 1536)])
    small = pl.BlockSpec((1, 512), lambda i, j, k: (0, 0))
    per_head = lambda g: jnp.tile(g, (1, 512 // HEAD_DIM))
    qk_raw, qkn = carrying("proj_qk", _mm_fwd, "proj_qk", h, w_in3, tm, 512, D_MODEL, ncols=2 * D_MODEL,
                           epilogue=qk_epi, outs=[_out2d(s_len, 2 * D_MODEL, BF16, tm, 512)] * 2,
                           extras=[(per_head(gq), small), (per_head(gk), small)])
    v = carrying("proj_v", _mm_fwd, "proj_v", h, w_in3, tb, 512, D_MODEL, col0=4, ncols=D_MODEL,
                 outs=[_out2d(s_len, D_MODEL, BF16, tb, 512)])
    w_gate3 = comm.gate_weight()

    def gate_epi(acc, e, o):
        o[0][...] = jax.nn.sigmoid(acc + e[0]).astype(BF16)
    gates = carrying("gates", _mm_fwd, "gates", h, w_gate3, tb, 512, D_MODEL, epilogue=gate_epi,
                     outs=[_out2d(s_len, 2 * D_MODEL, BF16, tb, 512)],
                     extras=[(row(b_gate), pl.BlockSpec((1, 512), lambda i, j, k: (0, j)))])

    attn, lse = carrying("attn_fwd", _attn_fwd, qkn, v, bias)
    yconv = carrying("conv_fwd", _conv_fwd, conv_in, conv_w, row(conv_b))
    w_ap, w_cp, w_out, w_up3, w_down = comm.late_weights()
    tw = 1024
    ya = _mm_fwd("attn_proj", attn, w_ap, tm, tw, D_MODEL, outs=[_out2d(s_len, D_MODEL, BF16, tm, tw)])

    def merge_epi(acc, e, o):
        ya_v, ga, gc = [t.astype(F32) for t in e]
        o[0][...] = acc.astype(BF16)
        o[1][...] = (ga * ya_v + gc * acc).astype(BF16)
    gate_a, gate_c = (gates, _tile_spec(tm, tw, 0)), (gates, _tile_spec(tm, tw, 1))
    yc, merged = _mm_fwd("conv_proj", yconv, w_cp, tm, tw, D_MODEL, epilogue=merge_epi,
                         outs=[_out2d(s_len, D_MODEL, BF16, tm, tw), _out2d(s_len, D_MODEL, BF16, tm, tw)],
                         extras=[(ya, _tile_spec(tm, tw)), gate_a, gate_c])

    def res_epi(acc, e, o):
        x1_v = e[0] + acc
        o[0][...] = x1_v
        r = lax.rsqrt(jnp.mean(x1_v * x1_v, axis=-1, keepdims=True) + EPS)
        o[1][...] = (x1_v * r * e[1]).astype(BF16)
    assert tw == D_MODEL
    x1, h2 = _mm_fwd("out_proj", merged, w_out, tm, tw, D_MODEL, epilogue=res_epi,
                     outs=[_out2d(s_len, D_MODEL, F32, tm, tw), _out2d(s_len, D_MODEL, BF16, tm, tw)],
                     extras=[(x, _tile_spec(tm, tw)), (row(norm2_g), pl.BlockSpec((1, tw), lambda i, j, k: (0, 0)))])

    def up_epi(acc, e, o):
        o[0][...] = jnp.square(jnp.maximum(acc, 0.0)).astype(BF16)
    act = _mm_fwd("mlp_up", h2, w_up3, tb, tw, D_MODEL, epilogue=up_epi, outs=[_out2d(s_len, D_FF, BF16, tb, tw)])

    def loss_epi(acc, e, o):
        err = e[0] + acc - e[1]
        o[0][...] = (err * (1.0 / D_MODEL)).astype(BF16)
        sq = err * err
        part = sq[:, 0:LANE]
        for c0 in range(LANE, D_MODEL, LANE):
            part = part + sq[:, c0:c0 + LANE]
        o[1][...] = jnp.sum(part.reshape(tl // 8, 8, LANE), axis=0)
    tl = 512
    dy_b, loss_part = _mm_fwd(
        "mlp_down", act, w_down, tl, D_MODEL, D_FF, epilogue=loss_epi,
        outs=[_out2d(s_len, D_MODEL, BF16, tl, D_MODEL),
              (jax.ShapeDtypeStruct((8 * (s_len // tl), LANE), F32), pl.BlockSpec((8, LANE), lambda i, j, k: (i, 0)))],
        extras=[(x1, _tile_spec(tl, D_MODEL)), (target, _tile_spec(tl, D_MODEL))])

    def dup_epi(acc, e, o):
        o[0][...] = (acc * (2.0 * jnp.sqrt(e[0].astype(F32)))).astype(BF16)
    full = lambda cols: pl.BlockSpec((tm, cols), lambda i, j, n: (i, n))
    tokens = lambda cols: pl.BlockSpec((s_len, cols), lambda i, j, m: (m, j))
    dup = _mm_bwd_x("d_act", dy_b, full(D_MODEL), w_down, tm, tw, D_MODEL, s_len, D_MODEL, epilogue=dup_epi,
                    outs=[_out2d(s_len, D_FF, BF16, tm, tw)], extras=[(act, _tile_spec(tm, tw))])
    g_down = _mm_bwd_w("g_down", act, dy_b, tokens(D_MODEL), D_MODEL, 512, D_MODEL, s_len, False)
    g_up = _mm_bwd_w("g_up", h2, dup, tokens(512), D_FF, D_MODEL, 512, s_len, True)
    comm.grads_ready("mlp", dict(w_down=g_down, w_up=g_up))
    dx1_b, dg2 = carrying("d_h2", _d_norm_input, "d_h2", [(dup.reshape(1, s_len, D_FF), w_up3)],
                          x1, row(norm2_g), dy_b, BF16)

    def dmerge_epi(acc, e, o):
        ya_v, yc_v, ga, gc = [t.astype(F32) for t in e]
        o[0][...] = (acc * ga).astype(BF16)
        o[1][...] = (acc * gc).astype(BF16)
        o[2][0] = (acc * ya_v * ga * (1.0 - ga)).astype(BF16)
        o[2][1] = (acc * yc_v * gc * (1.0 - gc)).astype(BF16)
    dya, dyc, dgp2 = _mm_bwd_x(
        "d_merged", dx1_b, full(D_MODEL), w_out, tm, tw, D_MODEL, s_len, D_MODEL, epilogue=dmerge_epi,
        outs=[_out2d(s_len, D_MODEL, BF16, tm, tw), _out2d(s_len, D_MODEL, BF16, tm, tw),
              (jax.ShapeDtypeStruct((2, s_len, D_MODEL), BF16), pl.BlockSpec((2, tm, tw), lambda i, j, n: (0, i, j)))],
        extras=[(ya, _tile_spec(tm, tw)), (yc, _tile_spec(tm, tw)), gate_a, gate_c])
    g_out = _mm_bwd_w("g_out", merged, dx1_b, tokens(512), D_MODEL, D_MODEL, 512, s_len, False)
    d_attn = _mm_bwd_x("d_attn", dya, full(D_MODEL), w_ap, tm, tw, D_MODEL, s_len, D_MODEL,
                       outs=[_out2d(s_len, D_MODEL, BF16, tm, tw)])
    g_ap = _mm_bwd_w("g_attn_proj", attn, dya, tokens(512), D_MODEL, D_MODEL, 512, s_len, False)
    d_yconv = _mm_bwd_x("d_yconv", dyc, full(D_MODEL), w_cp, tm, tw, D_MODEL, s_len, D_MODEL,
                        outs=[_out2d(s_len, D_MODEL, BF16, tm, tw)])
    g_cp = _mm_bwd_w("g_conv_proj", yconv, dyc, tokens(512), D_MODEL, D_MODEL, 512, s_len, False)
    piece = lambda width: (lambda blk: (blk * width) // D_MODEL, lambda blk: (blk * width % D_MODEL) // width)
    pc, cb = piece(512)
    pieces = pl.BlockSpec((None, s_len, 512), lambda i, j, m: (pc(j), m, cb(j)))
    g_gate = _mm_bwd_w("g_gate", h, dgp2, pieces, 2 * D_MODEL, D_MODEL, 512, s_len, True)
    comm.grads_ready("proj", dict(w_out=g_out, w_attn_proj=g_ap, w_conv_proj=g_cp, w_gate=g_gate))

    dproj6, dbias = carrying("attn_bwd", _attn_bwd, qkn, v, attn, d_attn, bias, lse)
    dproj6, dgq, dgk = _qk_norm_bwd(dproj6, qk_raw, gq, gk)
    dproj6, dconv_wb = carrying("conv_bwd", _conv_bwd, dproj6, d_yconv, conv_in, conv_w, row(conv_b))

    g_in = carrying("g_in", _mm_bwd_w, "g_in", h, dproj6, pieces, 6 * D_MODEL, D_MODEL, 512, s_len, True)
    comm.grads_ready("in", dict(w_in=g_in))
    d_rel = carrying("bias_reduce", _bias_reduce, dbias)
    grad_x, dg1 = carrying("d_h", _d_norm_input, "d_h", [(dproj6, w_in3), (dgp2, w_gate3)],
                           x, row(norm1_g), dx1_b, F32)

    def bsum(t, f):
        return [], [t[0].astype(F32), t[1].astype(F32)]
    ts = 512
    db_a, db_c = carrying("b_gate_sum", _ew, "b_gate_sum", bsum,
                          [(dgp2, pl.BlockSpec((None, ts, D_MODEL), lambda i: (0, i, 0))),
                           (dgp2, pl.BlockSpec((None, ts, D_MODEL), lambda i: (1, i, 0)))],
                          [], [], sums=[D_MODEL, D_MODEL], ts=ts)

    big = dict(w_in=g_in, w_attn_proj=g_ap, w_conv_proj=g_cp, w_gate=g_gate, w_out=g_out,
               w_up=g_up, w_down=g_down)
    small = dict(norm1_g=dg1, norm2_g=dg2, conv_wb=dconv_wb, b_gate=(db_a, db_c),
                 q_norm_g=dgq, k_norm_g=dgk, rel_bias=d_rel)
    return loss_part, grad_x, big, small


def _finish_loss(loss_part):
    def body(l_ref, lo_ref):
        total = jnp.sum(jnp.sum(l_ref[...], axis=0, keepdims=True), axis=1, keepdims=True)
        lo_ref[...] = jnp.broadcast_to(total * (0.5 / D_MODEL), lo_ref.shape)

    return pl.pallas_call(body, name="finish_loss", out_shape=jax.ShapeDtypeStruct((8, LANE), F32))(loss_part)


def _place():
    return lax.axis_index("x"), lax.axis_index("y"), lax.axis_index("c")


def _other_chips(x, y):
    return [(1 - x, y), (x, 1 - y), (1 - x, 1 - y)]


def _cast_into_slot(name, where, w):
    r, cols = w.shape
    ts = 256

    def body(w_ref, x_ref, o_ref):
        o_ref[...] = x_ref[...].astype(o_ref.dtype)

    return pl.pallas_call(
        body, name=name,
        grid_spec=pltpu.PrefetchScalarGridSpec(
            num_scalar_prefetch=1, grid=(r // ts,),
            in_specs=[pl.BlockSpec((ts, cols), lambda i, w: (i, 0))],
            out_specs=pl.BlockSpec((None, ts, cols), lambda i, w: (w[0], i, 0))),
        out_shape=jax.ShapeDtypeStruct((N_SHARD, r, cols), BF16),
        compiler_params=_params("parallel"),
    )(where, w)


def _remote(src, dst, send, recv, k, to):
    return pltpu.make_async_remote_copy(src_ref=src, dst_ref=dst, send_sem=send.at[k], recv_sem=recv.at[k],
                                        device_id=to, device_id_type=MESH)


def _gather_riders(slots):
    n = len(slots)

    def rows(w, which, part=None):
        hr = slots[w].shape[1] // 2
        if part is None:
            return pl.ds(which * hr, hr)
        if hr % 32:
            return pl.ds(which * hr, hr) if part == 0 else None
        return pl.ds(which * hr + part * (hr // 2), hr // 2)

    def copy(refs, w, k, shard, span, send, recv, to):
        ref = refs[w].at[shard, span]
        return _remote(ref, ref, send, recv, k, to)

    def places():
        x, y, c = _place()
        chips = _other_chips(x, y)
        return c, 2 * x + y, [2 * ch[0] + ch[1] for ch in chips], [(*ch, c) for ch in chips], (x, y, 1 - c)

    def chips_copies(refs, send, recv, sending):
        c, mine, theirs, peers, sib = places()
        return [copy(refs, w, 2 * w + j, mine if sending else theirs[j], rows(w, c), send, recv, peers[j])
                for w in range(n) for j in range(2)]

    def pass_copies(refs, send, recv, sending):
        c, mine, theirs, peers, sib = places()
        out = []
        for w in range(n):
            for j in range(2):
                out.append(copy(refs, w, 5 * w + j, theirs[j], rows(w, c if sending else 1 - c), send, recv, sib))
            for link in range(2):
                span = rows(w, c, part=1 - link)
                if span is not None:
                    out.append(copy(refs, w, 5 * w + 2 + link, theirs[1 - link] if sending else theirs[2], span,
                                    send, recv, peers[link]))
        return out

    def last_copies(refs, send, recv, sending):
        c, mine, theirs, peers, sib = places()
        return [copy(refs, w, 5 * w + 4, theirs[2], rows(w, c if sending else 1 - c), send, recv, sib)
                for w in range(n)]

    def chips_start(_, refs, send, recv):
        for d in chips_copies(refs, send, recv, True):
            d.start()

    def chips_finish(_, refs, send, recv):
        for d in chips_copies(refs, send, recv, False):
            d.wait_recv()
        for d in chips_copies(refs, send, recv, True):
            d.wait_send()

    def sibling_start(_, refs, send, recv):
        for d in pass_copies(refs, send, recv, True):
            d.start()

    def sibling_finish(_, refs, send, recv):
        for d in pass_copies(refs, send, recv, False):
            d.wait_recv()
        for d in last_copies(refs, send, recv, True):
            d.start()
        for d in pass_copies(refs, send, recv, True):
            d.wait_send()
        for d in last_copies(refs, send, recv, False):
            d.wait_recv()
        for d in last_copies(refs, send, recv, True):
            d.wait_send()

    return (lambda arrays: _Rider([], arrays, 2 * n, chips_start, chips_finish),
            lambda arrays: _Rider([], arrays, 5 * n, sibling_start, sibling_finish))


def _pair_exchange_rider(grads, landing):
    n = len(grads)

    def copies(srcs, dsts, send, recv):
        x, y, c = _place()
        out = []
        for w in range(n):
            hr = grads[w].shape[1] // 2
            out.append(_remote(srcs[w].at[:, pl.ds((1 - c) * hr, hr)], dsts[w], send, recv, w, (x, y, 1 - c)))
        return out

    def start(srcs, dsts, send, recv):
        for cp in copies(srcs, dsts, send, recv):
            cp.start()

    def finish(srcs, dsts, send, recv):
        for cp in copies(srcs, dsts, send, recv):
            cp.wait()

    return _Rider(grads, landing, n, start, finish)


def _row_tile(hr):
    return min(hr, 256)


def _pair_add(name, where, grad, got):
    _, hr, cols = got.shape
    tr = _row_tile(hr)
    nblk = hr // tr

    def body(w_ref, g_ref, r_ref, o_ref):
        o_ref[...] = (g_ref[...] + r_ref[...]).astype(o_ref.dtype)

    other = lambda k, w: (w[0] + 1 + k) % N_SHARD
    return pl.pallas_call(
        body, name=name,
        grid_spec=pltpu.PrefetchScalarGridSpec(
            num_scalar_prefetch=1, grid=(N_SHARD - 1, nblk),
            in_specs=[pl.BlockSpec((None, tr, cols), lambda k, i, w: (other(k, w), w[1] * nblk + i, 0)),
                      pl.BlockSpec((None, tr, cols), lambda k, i, w: (other(k, w), i, 0))],
            out_specs=pl.BlockSpec((None, tr, cols), lambda k, i, w: (other(k, w), i, 0))),
        out_shape=jax.ShapeDtypeStruct(got.shape, BF16),
        compiler_params=_params("parallel", "parallel"),
    )(where, grad, got)


def _chip_exchange_rider(partials, landing):
    n = len(partials)

    def copies(srcs, dsts, send, recv):
        x, y, c = _place()
        return [_remote(srcs[w].at[2 * chip[0] + chip[1]], dsts[w].at[j], send, recv, 3 * w + j, (*chip, c))
                for w in range(n) for j, chip in enumerate(_other_chips(x, y))]

    def start(srcs, dsts, send, recv):
        for cp in copies(srcs, dsts, send, recv):
            cp.start()

    def finish(srcs, dsts, send, recv):
        for cp in copies(srcs, dsts, send, recv):
            cp.wait()

    return _Rider(partials, landing, 3 * n, start, finish)


def _final_add(name, where, grad, got, arrived):
    _, hr, cols = got.shape
    tr = _row_tile(hr)
    nblk = hr // tr

    def body(w_ref, g_ref, r_ref, a_ref, o_ref):
        acc = g_ref[...] + r_ref[...]
        for j in range(3):
            acc = acc + a_ref[j].astype(F32)
        o_ref[...] = acc

    return pl.pallas_call(
        body, name=name,
        grid_spec=pltpu.PrefetchScalarGridSpec(
            num_scalar_prefetch=1, grid=(nblk,),
            in_specs=[pl.BlockSpec((None, tr, cols), lambda i, w: (w[0], w[1] * nblk + i, 0)),
                      pl.BlockSpec((None, tr, cols), lambda i, w: (w[0], i, 0)),
                      pl.BlockSpec((3, tr, cols), lambda i, w: (0, i, 0))],
            out_specs=pl.BlockSpec((tr, cols), lambda i, w: (w[1] * nblk + i, 0))),
        out_shape=jax.ShapeDtypeStruct((2 * hr, cols), F32),
        compiler_params=_params("parallel"),
    )(where, grad, got, arrived)


def _pair_share_rider(shards):
    n = len(shards)

    def half(refs, w, which):
        hr = shards[w].shape[0] // 2
        return refs[w].at[pl.ds(which * hr, hr)]

    def start(_, refs, send, recv):
        x, y, c = _place()
        for w in range(n):
            _remote(half(refs, w, c), half(refs, w, c), send, recv, w, (x, y, 1 - c)).start()

    def finish(_, refs, send, recv):
        x, y, c = _place()
        for w in range(n):
            _remote(half(refs, w, 1 - c), half(refs, w, 1 - c), send, recv, w, (x, y, 1 - c)).wait_recv()
        for w in range(n):
            _remote(half(refs, w, c), half(refs, w, c), send, recv, w, (x, y, 1 - c)).wait_send()

    return _Rider([], shards, n, start, finish)


class _Exchange:
    PLAN = {"bias_expand": [("early", "gather")], "norm1": [("early", "forward")],
            "proj_conv": [("gate", "gather")], "proj_v": [("gate", "forward")],
            "proj_qk": [("late", "gather")], "gates": [("late", "forward")],
            "attn_fwd": [("mlp_w", "gather")], "conv_fwd": [("mlp_w", "forward")],
            "d_h2": [("mlp", "pair")], "attn_bwd": [("mlp", "chips"), ("proj", "pair")], "conv_bwd": [("mlp", "share")],
            "g_in": [("proj", "chips")], "bias_reduce": [("proj", "share"), ("in", "pair")],
            "d_h": [("in", "chips")], "b_gate_sum": [("in", "share")]}

    def __init__(self, where, early_slots, gate_slots, late_slots):
        self.where = where
        self.slots = dict(early=early_slots, gate=gate_slots, late=late_slots[:3], mlp_w=late_slots[3:])
        self.stage = {g: dict(zip(("gather", "forward"), _gather_riders(s))) for g, s in self.slots.items()}
        self.groups, self.reduced, self.pending = {}, {}, []

    def early_weights(self):
        w_in3, small = self.slots["early"]
        conv_w = small[:, :3, :].transpose(1, 0, 2).reshape(3, N_SHARD * small.shape[2])
        return w_in3, conv_w

    def gate_weight(self):
        return self.slots["gate"][0]

    def late_weights(self):
        rows = lambda a: a.reshape(a.shape[0] * a.shape[1], a.shape[2])
        w_ap, w_cp, w_out = self.slots["late"]
        w_up3, w_down = self.slots["mlp_w"]
        return rows(w_ap), rows(w_cp), rows(w_out), w_up3, rows(w_down)

    def grads_ready(self, group, grads):
        names = list(grads)
        g4 = [g if g.ndim == 3 else g.reshape(N_SHARD, -1, g.shape[1]) for g in grads.values()]
        self.groups[group] = dict(names=names, g4=g4)

    def riders(self, at):
        self.pending = self.PLAN.get(at, [])
        out = []
        for group, stage in self.pending:
            if group in self.slots:
                out.append(self.stage[group][stage](self.slots[group]))
                continue
            st = self.groups[group]
            if stage == "pair":
                landing = [lax.empty((N_SHARD, g.shape[1] // 2, g.shape[2]), F32) for g in st["g4"]]
                out.append(_pair_exchange_rider(st["g4"], landing))
            elif stage == "chips":
                landing = [lax.empty((3,) + p.shape[1:], p.dtype) for p in st["partial"]]
                out.append(_chip_exchange_rider(st["partial"], landing))
            else:
                out.append(_pair_share_rider(st["halves"]))
        return out

    def done(self, at, carried):
        for (group, stage), arrays in zip(self.pending, carried):
            if group in self.slots:
                self.slots[group] = arrays
                continue
            st = self.groups[group]
            tag = lambda what, n: what + "_" + n
            if stage == "pair":
                st["got"] = arrays
                st["partial"] = [_pair_add(tag("pair_add", n), self.where, g, r)
                                 for n, g, r in zip(st["names"], st["g4"], arrays)]
            elif stage == "chips":
                st["halves"] = [_final_add(tag("final_add", n), self.where, g, r, a)
                                for n, g, r, a in zip(st["names"], st["g4"], st["got"], arrays)]
            else:
                self.reduced.update(zip(st["names"], arrays))


SMALL_ROWS = 32
N_DEV = 8


def _all_reduce_small(pack):
    def body(p_ref, o_ref, buf, send, recv):
        x, y, c = _place()
        buf[4 * x + 2 * y + c] = p_ref[...]
        copies, waits = [], []
        for k in range(1, N_DEV):
            px = 1 - x if k & 4 else x
            py = 1 - y if k & 2 else y
            pc = 1 - c if k & 1 else c
            copies.append(_remote(p_ref, buf.at[4 * x + 2 * y + c], send, recv, k - 1, (px, py, pc)))
            waits.append(_remote(p_ref, buf.at[4 * px + 2 * py + pc], send, recv, k - 1, (px, py, pc)))
        for cp in copies:
            cp.start()
        for cp in waits:
            cp.wait_recv()
        acc = buf[0]
        for d in range(1, N_DEV):
            acc = acc + buf[d]
        o_ref[...] = acc
        for cp in copies:
            cp.wait_send()

    return pl.pallas_call(
        body, name="all_reduce_small",
        out_shape=jax.ShapeDtypeStruct(pack.shape, F32),
        scratch_shapes=[pltpu.VMEM((N_DEV,) + pack.shape, F32),
                        pltpu.SemaphoreType.DMA((N_DEV - 1,)), pltpu.SemaphoreType.DMA((N_DEV - 1,))],
    )(pack)


def _adamw(name, w, g, m, v):
    c1 = 1.0 - ADAM_B1 ** ADAM_STEP
    c2 = 1.0 - ADAM_B2 ** ADAM_STEP

    def fn(t, f):
        wv, gv, mv, vv = t
        m2 = ADAM_B1 * mv + (1.0 - ADAM_B1) * gv
        v2 = ADAM_B2 * vv + (1.0 - ADAM_B2) * (gv * gv)
        delta = -ADAM_LR * ((m2 / c1) / (jnp.sqrt(v2 / c2) + ADAM_EPS) + ADAM_WD * wv)
        return [delta, m2, v2], []

    cols = w.shape[1]
    return _ew(name, fn, [w, g, m, v], [], [(cols, F32)] * 3, ts=min(w.shape[0], 256))


LOSS_ROW = 26


def _pack_small(norm1_g, norm2_g, conv_b, b_gate, conv_w, q_norm_g, k_norm_g, rel_bias, loss=None):
    pack = jnp.zeros((SMALL_ROWS, D_MODEL), F32)
    for r0, v in ((0, norm1_g), (1, norm2_g), (2, conv_b), (3, b_gate.reshape(2, D_MODEL)), (5, conv_w),
                  (8, q_norm_g), (9, k_norm_g), (10, rel_bias)) + (((LOSS_ROW, loss),) if loss is not None else ()):
        v = v.reshape(-1, v.shape[-1]).astype(F32)
        pack = pack.at[r0:r0 + v.shape[0], :v.shape[1]].set(v)
    return pack


def _unpack_small(pack, conv_cols):
    return dict(norm1_g=pack[0], norm2_g=pack[1], conv_b=pack[2], b_gate=pack[3:5].reshape(2 * D_MODEL),
                conv_w=pack[5:8, :conv_cols], q_norm_g=pack[8, :HEAD_DIM], k_norm_g=pack[9, :HEAD_DIM],
                rel_bias=pack[10:10 + N_HEADS, :N_REL])


BIG = ["w_in", "w_attn_proj", "w_conv_proj", "w_gate", "w_out", "w_up", "w_down"]
LATE = ["w_attn_proj", "w_conv_proj", "w_out", "w_up", "w_down"]
WEIGHTS = ["norm1_g", "w_in", "q_norm_g", "k_norm_g", "rel_bias", "conv_w", "conv_b", "w_attn_proj",
           "w_conv_proj", "w_gate", "b_gate", "w_out", "norm2_g", "w_up", "w_down"]


def kernel(x, norm1_g, w_in, q_norm_g, k_norm_g, rel_bias, conv_w, conv_b, w_attn_proj, w_conv_proj, w_gate, b_gate, w_out, norm2_g, w_up, w_down, loss_target, m_norm1_g, m_w_in, m_q_norm_g, m_k_norm_g, m_rel_bias, m_conv_w, m_conv_b, m_w_attn_proj, m_w_conv_proj, m_w_gate, m_b_gate, m_w_out, m_norm2_g, m_w_up, m_w_down, v_norm1_g, v_w_in, v_q_norm_g, v_k_norm_g, v_rel_bias, v_conv_w, v_conv_b, v_w_attn_proj, v_w_conv_proj, v_w_gate, v_b_gate, v_w_out, v_norm2_g, v_w_up, v_w_down):
    given = dict(locals())
    w = {n: given[n] for n in WEIGHTS}
    m = {n: given["m_" + n] for n in WEIGHTS}
    v = {n: given["v_" + n] for n in WEIGHTS}
    s_len = x.shape[1]
    shard = 2 * lax.axis_index("x") + lax.axis_index("y")
    where = jnp.stack([shard, lax.axis_index("c")]).astype(jnp.int32)
    conv_cols = conv_w.shape[1]

    small_in = lax.dynamic_update_slice(jnp.zeros((N_SHARD, 16, conv_cols), F32), conv_w[None], (shard, 0, 0))
    slot = {n: _cast_into_slot("cast_" + n, where, w[n]) for n in BIG}
    comm = _Exchange(where, [slot["w_in"], small_in], [slot["w_gate"]], [slot[n] for n in LATE])

    loss_part, grad_x, _, small = _local_grads(
        x.reshape(s_len, D_MODEL), loss_target.reshape(s_len, D_MODEL), norm1_g, q_norm_g, k_norm_g,
        rel_bias, conv_b, b_gate, norm2_g, comm)
    grad = dict(comm.reduced)

    loss_local = _finish_loss(loss_part)
    pack = _pack_small(small["norm1_g"], small["norm2_g"], small["conv_wb"][3], jnp.concatenate(small["b_gate"], axis=1),
                       small["conv_wb"][0:3], small["q_norm_g"], small["k_norm_g"], small["rel_bias"],
                       loss=loss_local[0:1, :])
    total = _all_reduce_small(pack)
    g_small = _unpack_small(total, D_MODEL)
    g_small["conv_w"] = lax.dynamic_slice(g_small["conv_w"], (0, shard * conv_cols), (3, conv_cols))
    grad.update(g_small)

    delta, new_m, new_v = {}, {}, {}
    for n in BIG:
        delta[n], new_m[n], new_v[n] = _adamw("adamw_" + n, w[n], grad[n], m[n], v[n])
    small_names = [n for n in WEIGHTS if n not in BIG]
    packs = [_pack_small(**{n: src[n] for n in small_names}) for src in (w, grad, m, v)]
    for out, packed in zip((delta, new_m, new_v), _adamw("adamw_small", *packs)):
        out.update({n: a.reshape(w[n].shape) for n, a in _unpack_small(packed, conv_cols).items()})

    outs = [total[LOSS_ROW, 0], grad_x.reshape(x.shape)]
    for group in (grad, delta, new_m, new_v):
        outs += [group[n].reshape(w[n].shape) for n in WEIGHTS]
    return tuple(outs)
```

```python
import jax
import jax.numpy as jnp
from jax import lax
from jax.experimental import pallas as pl
from jax.experimental.pallas import tpu as pltpu

F32 = jnp.float32
BF16 = jnp.bfloat16

D_MODEL = 1024
N_HEADS = 16
HEAD_DIM = 64
CHUNK = 64
N_PREV_CHUNKS = 8
MAX_REL = 256
D_FF = 4096
N_REL = 2 * MAX_REL + 1
REL_PAD = 640
EPS = 1e-6
NEG_INF = -1e30
QK_SCALE = HEAD_DIM ** -0.5

SUPER = 4 * CHUNK
BAND = SUPER + N_PREV_CHUNKS * CHUNK
SKEW_W = 1024
N_SHARD = 4
LANE = 128
MXU_DIM = 256
VMEM_LIMIT = 48 * 1024 * 1024

ADAM_LR = 0.001
ADAM_B1 = 0.9
ADAM_B2 = 0.999
ADAM_EPS = 1e-08
ADAM_WD = 0.01
ADAM_STEP = 10

MESH = pl.DeviceIdType.MESH
NN = (((1,), (0,)), ((), ()))
NT = (((1,), (1,)), ((), ()))
TN = (((0,), (0,)), ((), ()))


def _params(*sem):
    return pltpu.CompilerParams(dimension_semantics=sem or None, vmem_limit_bytes=VMEM_LIMIT)


HBM_SPEC = pl.BlockSpec(memory_space=pl.ANY)


class _Rider:
    def __init__(self, sources, arrays, n_sem, start, finish):
        self.sources, self.arrays, self.n_sem, self.start, self.finish = sources, arrays, n_sem, start, finish


def _carry(riders, body, *, name, out_shape, grid=(), in_specs=None, out_specs=None, scratch_shapes=(),
           semantics=(), input_output_aliases=None):
    aliases = dict(input_output_aliases or {})
    if not riders:
        kw = {} if in_specs is None else dict(in_specs=in_specs, out_specs=out_specs)
        return pl.pallas_call(body, name=name, grid=grid, out_shape=out_shape, scratch_shapes=scratch_shapes,
                              input_output_aliases=aliases, compiler_params=_params(*semantics), **kw)
    single = not isinstance(out_shape, (list, tuple))
    shapes = [out_shape] if single else list(out_shape)
    n_out, n_scr = len(shapes), len(scratch_shapes)
    in_hbm = lambda a: pltpu.with_memory_space_constraint(a, pltpu.HBM)
    srcs = [in_hbm(a) for r in riders for a in r.sources]
    arrs = [in_hbm(a) for r in riders for a in r.arrays]
    vmem = pl.BlockSpec(memory_space=pltpu.VMEM)

    def run(*args):
        n_in = len(args)

        def wrapped(*refs):
            pos = n_in
            src_refs = refs[pos:pos + len(srcs)]
            pos += len(srcs) + len(arrs)
            outs = refs[pos:pos + n_out]
            pos += n_out
            arr_refs = refs[pos:pos + len(arrs)]
            pos += len(arrs)
            scratch = refs[pos:pos + n_scr]
            sems = refs[pos + n_scr:]
            first, last = True, True
            for d, size in enumerate(grid):
                first = jnp.logical_and(first, pl.program_id(d) == 0)
                last = jnp.logical_and(last, pl.program_id(d) == size - 1)

            def each(method):
                s0 = a0 = 0
                for k, r in enumerate(riders):
                    getattr(r, method)(src_refs[s0:s0 + len(r.sources)], arr_refs[a0:a0 + len(r.arrays)],
                                       sems[2 * k], sems[2 * k + 1])
                    s0, a0 = s0 + len(r.sources), a0 + len(r.arrays)

            pl.when(first)(lambda: each("start"))
            body(*refs[:n_in], *outs, *scratch)
            pl.when(last)(lambda: each("finish"))

        ins = [vmem] * n_in if in_specs is None else list(in_specs)
        if out_specs is None:
            o_specs = [vmem] * n_out
        else:
            o_specs = [out_specs] if single else list(out_specs)
        for k in range(len(arrs)):
            aliases[n_in + len(srcs) + k] = n_out + k
        res = pl.pallas_call(
            wrapped, name=name, grid=grid,
            in_specs=ins + [HBM_SPEC] * (len(srcs) + len(arrs)),
            out_specs=o_specs + [HBM_SPEC] * len(arrs),
            out_shape=shapes + [jax.ShapeDtypeStruct(a.shape, a.dtype) for a in arrs],
            scratch_shapes=list(scratch_shapes) + [pltpu.SemaphoreType.DMA((r.n_sem,)) for r in riders for _ in range(2)],
            input_output_aliases=aliases,
            compiler_params=_params(*["arbitrary"] * len(grid)),
        )(*args, *srcs, *arrs)
        core, rest = res[:n_out], list(res[n_out:])
        carried, a0 = [], 0
        for r in riders:
            carried.append(rest[a0:a0 + len(r.arrays)])
            a0 += len(r.arrays)
        return (core[0] if single else core), carried

    return run


def _mm(name, dims, a, a_spec, b, b_spec, grid, tile, outs, epilogue=None, extras=(), riders=()):
    nk, ne, no = grid[2], len(extras), len(outs)

    def body(a_ref, b_ref, *refs):
        e_refs, o_refs = refs[:ne], refs[ne:ne + no]
        part = lax.dot_general(a_ref[...], b_ref[...], dims, preferred_element_type=F32)

        def finish(acc):
            if epilogue is None:
                o_refs[0][...] = acc.astype(o_refs[0].dtype)
            else:
                epilogue(acc, [r[...] for r in e_refs], o_refs)

        if nk == 1:
            finish(part)
        else:
            acc_ref = refs[ne + no]
            k = pl.program_id(2)

            @pl.when(k == 0)
            def _():
                acc_ref[...] = part

            @pl.when(k > 0)
            def _():
                acc_ref[...] += part

            @pl.when(k == nk - 1)
            def _():
                finish(acc_ref[...])

    res = _carry(
        riders, body, name=name, grid=grid,
        in_specs=[a_spec, b_spec] + [s for _, s in extras],
        out_specs=[s for _, s in outs],
        out_shape=[s for s, _ in outs],
        scratch_shapes=[pltpu.VMEM(tile, F32)] if nk > 1 else [],
        semantics=("parallel", "parallel", "arbitrary"),
    )(a, b, *[e for e, _ in extras])
    res, carried = res if riders else (res, None)
    res = res[0] if no == 1 else res
    return (res, carried) if riders else res


def _tile_spec(tm, tn, col0=0):
    return pl.BlockSpec((tm, tn), lambda i, j, k: (i, j + col0))


def _out2d(m, n, dtype, tm, tn):
    return (jax.ShapeDtypeStruct((m, n), dtype), _tile_spec(tm, tn))


def _mm_fwd(name, a, w, tm, tn, tk, outs=None, epilogue=None, extras=(), col0=0, ncols=None, riders=()):
    m, kdim = a.shape
    if w.ndim == 3:
        per = w.shape[2] // tn
        n = ncols or N_SHARD * w.shape[2]
        w_spec = pl.BlockSpec((None, tk, tn), lambda i, j, k: ((j + col0) // per, k, (j + col0) % per))
    else:
        n = ncols or w.shape[1]
        w_spec = pl.BlockSpec((tk, tn), lambda i, j, k: (k, j + col0))
    if outs is None:
        outs = [_out2d(m, n, F32, tm, tn)]
    return _mm(name, NN, a, pl.BlockSpec((tm, tk), lambda i, j, k: (i, k)), w, w_spec,
               (m // tm, n // tn, kdim // tk), (tm, tn), outs, epilogue, extras, riders)


def _mm_bwd_x(name, g, g_spec, w, tm, tj, tc, m, n_contract, outs=None, epilogue=None, extras=(), riders=()):
    if w.ndim == 3:
        per = w.shape[2] // tc
        kdim = w.shape[1]
        w_spec = pl.BlockSpec((None, tj, tc), lambda i, j, n: (n // per, j, n % per))
    else:
        kdim = w.shape[0]
        w_spec = pl.BlockSpec((tj, tc), lambda i, j, n: (j, n))
    if outs is None:
        outs = [_out2d(m, kdim, F32, tm, tj)]
    return _mm(name, NT, g, g_spec, w, w_spec, (m // tm, kdim // tj, n_contract // tc),
               (tm, tj), outs, epilogue, extras, riders)


def _mm_bwd_w(name, a, g, g_spec, n, tk, tn, tm, sharded, riders=()):
    m, kdim = a.shape
    if sharded:
        per = (n // N_SHARD) // tn
        out = (jax.ShapeDtypeStruct((N_SHARD, kdim, n // N_SHARD), F32),
               pl.BlockSpec((None, tk, tn), lambda i, j, mm: (j // per, i, j % per)))
    else:
        out = (jax.ShapeDtypeStruct((kdim, n), F32), pl.BlockSpec((tk, tn), lambda i, j, mm: (i, j)))
    return _mm(name, TN, a, pl.BlockSpec((tm, tk), lambda i, j, mm: (mm, i)), g, g_spec,
               (kdim // tk, n // tn, m // tm), (tk, tn), [out], riders=riders)


def _ew(name, fn, tiles, fulls, outs, sums=(), ts=512, riders=()):
    tiles = [t if isinstance(t, tuple) else (t, pl.BlockSpec((ts, t.shape[1]), lambda i: (i, 0)))
             for t in tiles]
    s_rows = tiles[0][0].shape[-2]
    nt, nf, no = len(tiles), len(fulls), len(outs)

    def body(*refs):
        t_vals = [r[...] for r in refs[:nt]]
        f_vals = [r[...] for r in refs[nt:nt + nf]]
        o_refs, s_refs = refs[nt + nf:nt + nf + no], refs[nt + nf + no:]
        o_vals, s_vals = fn(t_vals, f_vals)
        for r, v in zip(o_refs, o_vals):
            r[...] = v.astype(r.dtype)
        for r, v in zip(s_refs, s_vals):
            part = jnp.sum(v, axis=0, keepdims=True)

            @pl.when(pl.program_id(0) == 0)
            def _():
                r[...] = part

            @pl.when(pl.program_id(0) > 0)
            def _():
                r[...] += part

    full_specs = [pl.BlockSpec(f.shape, lambda i, nd=f.ndim: (0,) * nd) for f in fulls]
    return _carry(
        riders, body, name=name, grid=(s_rows // ts,),
        in_specs=[s for _, s in tiles] + full_specs,
        out_specs=[pl.BlockSpec((ts, c), lambda i: (i, 0)) for c, _ in outs]
        + [pl.BlockSpec((1, c), lambda i: (0, 0)) for c in sums],
        out_shape=[jax.ShapeDtypeStruct((s_rows, c), dt) for c, dt in outs]
        + [jax.ShapeDtypeStruct((1, c), F32) for c in sums],
        semantics=("arbitrary",),
    )(*[t for t, _ in tiles], *fulls)


def _rms_fwd(name, x, g, riders=()):
    def fn(t, f):
        xv = t[0]
        r = lax.rsqrt(jnp.mean(xv * xv, axis=-1, keepdims=True) + EPS)
        return [xv * r * f[0]], []
    out = _ew(name, fn, [x], [g], [(x.shape[1], BF16)], riders=riders)
    return (out[0][0], out[1]) if riders else out[0]


def _split3(x):
    x1 = x.astype(BF16)
    r1 = x - x1.astype(F32)
    x2 = r1.astype(BF16)
    x3 = (r1 - x2.astype(F32)).astype(BF16)
    return x1, x2, x3


def _rel_class(cp):
    far = (cp < MAX_REL) | (cp > BAND)
    return jnp.where(far, 2 * MAX_REL, BAND - cp)


def _skew_rows(x, sign):
    row = lax.broadcasted_iota(jnp.int32, x.shape, 0)
    for b in range(CHUNK.bit_length() - 1):
        shift = (1 << b) if sign > 0 else SKEW_W - (1 << b)
        x = jnp.where((row >> b) & 1 == 1, pltpu.roll(x, shift, 1), x)
    return x


def _roll_lanes(x, shift):
    return x if shift % SKEW_W == 0 else pltpu.roll(x, shift % SKEW_W, 1)


N_START = 3


def _bias_expand(rel_bias, riders=()):
    rel = jnp.pad(rel_bias, ((0, 0), (0, REL_PAD - N_REL))).reshape(N_HEADS, 1, REL_PAD)

    def body(rel_ref, o_ref):
        cls = lax.broadcasted_iota(jnp.int32, (REL_PAD, SKEW_W), 0)
        cp = lax.broadcasted_iota(jnp.int32, (REL_PAD, SKEW_W), 1)
        onehot = (cls == _rel_class(cp)).astype(BF16)
        rel8 = jnp.broadcast_to(rel_ref[...], (8, REL_PAD))
        trow = sum(jnp.dot(p, onehot, preferred_element_type=F32) for p in _split3(rel8))[0:1]
        first = _skew_rows(jnp.broadcast_to(trow, (CHUNK, SKEW_W)), +1)
        full = jnp.concatenate([_roll_lanes(first, CHUNK * g) for g in range(SUPER // CHUNK)], axis=0)[:, :BAND]
        qc = lax.broadcasted_iota(jnp.int32, (SUPER, BAND), 0) // CHUNK
        kc = lax.broadcasted_iota(jnp.int32, (SUPER, BAND), 1) // CHUNK
        on_band = (kc >= qc) & (kc <= qc + N_PREV_CHUNKS)
        table = jnp.where(on_band, full, NEG_INF).T
        key = lax.broadcasted_iota(jnp.int32, (BAND, SUPER), 0)
        for t in range(N_START):
            o_ref[t] = jnp.where(key < (N_START - 1 - t) * SUPER, NEG_INF, table)

    return _carry(
        riders, body, name="bias_expand", grid=(N_HEADS,),
        in_specs=[pl.BlockSpec((None, 1, REL_PAD), lambda h: (h, 0, 0))],
        out_specs=pl.BlockSpec((N_START, None, BAND, SUPER), lambda h: (0, h, 0, 0)),
        out_shape=jax.ShapeDtypeStruct((N_START, N_HEADS, BAND, SUPER), F32),
        semantics=("arbitrary",),
    )(rel)


def _bias_reduce(dbias, riders=()):
    def body(d_ref, o_ref):
        x = jnp.concatenate([d_ref[...].T, jnp.zeros((SUPER, SKEW_W - BAND), F32)], axis=1)
        folded = sum(_roll_lanes(x[CHUNK * g:CHUNK * (g + 1)], -CHUNK * g) for g in range(SUPER // CHUNK))
        diag = jnp.sum(_skew_rows(folded, -1), axis=0, keepdims=True)
        cp = lax.broadcasted_iota(jnp.int32, (SKEW_W, REL_PAD), 0)
        cls = lax.broadcasted_iota(jnp.int32, (SKEW_W, REL_PAD), 1)
        onehot = (cls == _rel_class(cp)).astype(BF16)
        diag8 = jnp.broadcast_to(diag, (8, SKEW_W))
        o_ref[...] = sum(jnp.dot(p, onehot, preferred_element_type=F32) for p in _split3(diag8))[0:1]

    out = _carry(
        riders, body, name="bias_reduce", grid=(N_HEADS,),
        in_specs=[pl.BlockSpec((None, BAND, SUPER), lambda h: (h, 0, 0))],
        out_specs=pl.BlockSpec((None, 1, REL_PAD), lambda h: (h, 0, 0)),
        out_shape=jax.ShapeDtypeStruct((N_HEADS, 1, REL_PAD), F32),
        semantics=("arbitrary",),
    )(dbias)
    out, carried = out if riders else (out, None)
    out = out.reshape(N_HEADS, REL_PAD)[:, :N_REL]
    return (out, carried) if riders else out


HEADS_PER_STEP = 8
HEAD_COLS = HEADS_PER_STEP * HEAD_DIM
N_HEAD_GROUPS = N_HEADS // HEADS_PER_STEP


def _scores_t(qs, kn, bias_t):
    return jnp.concatenate([lax.dot_general(k, qs, NT, preferred_element_type=F32) for k in kn], axis=0) + bias_t


def _bias_spec():
    return pl.BlockSpec((None, HEADS_PER_STEP, BAND, SUPER), lambda hg, i: (jnp.minimum(i, N_START - 1), hg, 0, 0))


def _band_specs(nb, col0, clamp_hi):
    def spec(d):
        def index(hg, i):
            blk = jnp.maximum(i - d, 0)
            if clamp_hi:
                blk = jnp.minimum(blk, nb - 1)
            return (blk, col0 + hg)
        return pl.BlockSpec((SUPER, HEAD_COLS), index)
    return [spec(2), spec(1), spec(0)]


def _head_sums(y):
    same_head = (lax.broadcasted_iota(jnp.int32, (MXU_DIM, MXU_DIM), 0) // HEAD_DIM
                 == lax.broadcasted_iota(jnp.int32, (MXU_DIM, MXU_DIM), 1) // HEAD_DIM).astype(BF16)
    sums = []
    for c0 in range(0, y.shape[1], MXU_DIM):
        chunk = y[:, c0:c0 + MXU_DIM]
        hi = chunk.astype(BF16)
        lo = (chunk - hi.astype(F32)).astype(BF16)
        sums.append(jnp.dot(hi, same_head, preferred_element_type=F32)
                    + jnp.dot(lo, same_head, preferred_element_type=F32))
    return jnp.concatenate(sums, axis=1)


def _head_unit(x):
    r = lax.rsqrt(_head_sums(x * x) * (1.0 / HEAD_DIM) + EPS)
    return x * r, r


def _head(hh):
    return slice(HEAD_DIM * hh, HEAD_DIM * (hh + 1))


def _key_block(j):
    return slice(SUPER * j, SUPER * (j + 1))


LSE_ROWS = 8


def _attn_fwd(qkn, v, bias, riders=()):
    s_len = qkn.shape[0]
    nb = s_len // SUPER

    def body(q_ref, k0, k1, k2, v0, v1, v2, b_ref, o_ref, lse_ref):
        outs = []
        v_t = [v0[...].T, v1[...].T, v2[...].T]

        def probabilities(hh):
            sl = _head(hh)
            s = _scores_t(q_ref[:, sl], [k0[:, sl], k1[:, sl], k2[:, sl]], b_ref[hh])
            m = jnp.max(s, axis=0, keepdims=True)
            e = jnp.exp(s - m)
            l = jnp.sum(e, axis=0, keepdims=True)
            lse_ref[hh:hh + 1, :] = m + jnp.log(l)
            return e.astype(BF16), 1.0 / l, sl

        def weighted_values(e, inv_l, sl):
            outs.append(sum(jnp.dot(v_t[j][sl, :], e[_key_block(j), :], preferred_element_type=F32)
                            for j in range(3)) * inv_l)

        ready = probabilities(0)
        for hh in range(1, HEADS_PER_STEP):
            following = probabilities(hh)
            weighted_values(*ready)
            ready = following
        weighted_values(*ready)
        o_ref[...] = jnp.concatenate(outs, axis=0).T.astype(o_ref.dtype)

    return _carry(
        riders, body, name="attn_fwd", grid=(N_HEAD_GROUPS, nb),
        in_specs=[pl.BlockSpec((SUPER, HEAD_COLS), lambda hg, i: (i, hg))]
        + _band_specs(nb, N_HEAD_GROUPS, False) + _band_specs(nb, 0, False) + [_bias_spec()],
        out_specs=[pl.BlockSpec((SUPER, HEAD_COLS), lambda hg, i: (i, hg)),
                   pl.BlockSpec((None, LSE_ROWS, SUPER), lambda hg, i: (hg, 0, i))],
        out_shape=[jax.ShapeDtypeStruct((s_len, D_MODEL), BF16),
                   jax.ShapeDtypeStruct((N_HEAD_GROUPS, LSE_ROWS, s_len), F32)],
        semantics=("parallel", "arbitrary"),
    )(qkn, qkn, qkn, qkn, v, v, v, bias)


def _attn_bwd(qkn, v, out, d_out, bias, lse, riders=()):
    s_len = qkn.shape[0]
    nb = s_len // SUPER

    def body(q_ref, k0, k1, k2, v0, v1, v2, o_ref, do_ref, b_ref, lse_ref, dp_ref, db_ref, aq_ref, ak_ref, av_ref):
        i = pl.program_id(1)

        @pl.when(i == 0)
        def _():
            aq_ref[...] = jnp.zeros_like(aq_ref)
            ak_ref[...] = jnp.zeros_like(ak_ref)
            av_ref[...] = jnp.zeros_like(av_ref)
            db_ref[...] = jnp.zeros_like(db_ref)

        @pl.when(i < nb)
        def _():
            dq, dk, dv = [], [[], [], []], [[], [], []]
            ones = jnp.ones((8, HEAD_DIM), BF16)
            k_t = [k0[...].T, k1[...].T, k2[...].T]

            def softmax_grad(hh):
                sl = _head(hh)
                qs, do = q_ref[:, sl], do_ref[:, sl]
                kn = [k0[:, sl], k1[:, sl], k2[:, sl]]
                prod = do.astype(F32) * o_ref[:, sl].astype(F32)
                hi = prod.astype(BF16)
                lo = (prod - hi.astype(F32)).astype(BF16)
                delta = (lax.dot_general(ones, hi, NT, preferred_element_type=F32)
                         + lax.dot_general(ones, lo, NT, preferred_element_type=F32))[0:1]
                lse_row = lse_ref[hh:hh + 1, :]
                pb, dsb = [], []
                for j, (kj, vj) in enumerate(zip(kn, (v0, v1, v2))):
                    rows = _key_block(j)
                    p = jnp.exp(lax.dot_general(kj, qs, NT, preferred_element_type=F32) + b_ref[hh, rows, :] - lse_row)
                    ds = p * (lax.dot_general(vj[:, sl], do, NT, preferred_element_type=F32) - delta)
                    db_ref[hh, rows, :] += ds
                    pb.append(p.astype(BF16))
                    dsb.append(ds.astype(BF16))
                return pb, dsb, qs, do, sl

            def operand_grads(pb, dsb, qs, do, sl):
                dq.append(sum(jnp.dot(k_t[j][sl, :], dsb[j], preferred_element_type=F32) for j in range(3)))
                for j in range(3):
                    dv[j].append(jnp.dot(pb[j], do, preferred_element_type=F32))
                    dk[j].append(jnp.dot(dsb[j], qs, preferred_element_type=F32))

            ready = softmax_grad(0)
            for hh in range(1, HEADS_PER_STEP):
                following = softmax_grad(hh)
                operand_grads(*ready)
                ready = following
            operand_grads(*ready)
            aq_ref[i % 3] = jnp.concatenate(dq, axis=0).T
            for j in range(3):
                slot = (i + 1 + j) % 3
                if j < 2:
                    ak_ref[slot] += jnp.concatenate(dk[j], axis=1)
                    av_ref[slot] += jnp.concatenate(dv[j], axis=1)
                else:
                    ak_ref[slot] = jnp.concatenate(dk[j], axis=1)
                    av_ref[slot] = jnp.concatenate(dv[j], axis=1)

        slot = (i + 1) % 3
        dp_ref[0] = aq_ref[slot].astype(dp_ref.dtype)
        dp_ref[1] = ak_ref[slot].astype(dp_ref.dtype)
        dp_ref[2] = av_ref[slot].astype(dp_ref.dtype)

    def qrow(hg, i):
        return (jnp.minimum(i, nb - 1), hg)

    return _carry(
        riders, body, name="attn_bwd", grid=(N_HEAD_GROUPS, nb + 2),
        in_specs=[pl.BlockSpec((SUPER, HEAD_COLS), qrow)]
        + _band_specs(nb, N_HEAD_GROUPS, True) + _band_specs(nb, 0, True)
        + [pl.BlockSpec((SUPER, HEAD_COLS), qrow), pl.BlockSpec((SUPER, HEAD_COLS), qrow), _bias_spec(),
           pl.BlockSpec((None, LSE_ROWS, SUPER), lambda hg, i: (hg, 0, jnp.minimum(i, nb - 1)))],
        out_specs=[pl.BlockSpec((3, SUPER, HEAD_COLS), lambda hg, i: (0, jnp.maximum(i - 2, 0), hg)),
                   pl.BlockSpec((HEADS_PER_STEP, BAND, SUPER), lambda hg, i: (hg, 0, 0))],
        out_shape=[jax.ShapeDtypeStruct((6, s_len, D_MODEL), BF16),
                   jax.ShapeDtypeStruct((N_HEADS, BAND, SUPER), F32)],
        scratch_shapes=[pltpu.VMEM((3, SUPER, HEAD_COLS), F32)] * 3,
        semantics=("parallel", "arbitrary"),
    )(qkn, qkn, qkn, qkn, v, v, v, out, d_out, bias, lse)


def _qk_norm_bwd(dproj6, qk_raw, gq, gk):
    s_len = qk_raw.shape[0]
    ts = min(1024, s_len)

    nsteps = s_len // ts
    half = D_MODEL // 2

    def body(d_ref, raw_ref, gq_ref, gk_ref, o_ref, dgq_ref, dgk_ref, acc_ref):
        step = pl.program_id(0)

        @pl.when(step == 0)
        def _():
            acc_ref[...] = jnp.zeros_like(acc_ref)

        for piece, (g_ref, scale) in enumerate(((gq_ref, QK_SCALE), (gk_ref, 1.0))):
            for c0 in (0, half):
                xhat, r = _head_unit(raw_ref[:, piece * D_MODEL + c0:piece * D_MODEL + c0 + half].astype(F32))
                dn = d_ref[piece, :, c0:c0 + half].astype(F32) * scale
                u = dn * g_ref[...]
                dx = r * (u - xhat * (_head_sums(u * xhat) * (1.0 / HEAD_DIM)))
                o_ref[piece, :, c0:c0 + half] = dx.astype(o_ref.dtype)
                acc_ref[piece:piece + 1, c0:c0 + half] += jnp.sum(dn * xhat, axis=0, keepdims=True)

        @pl.when(step == nsteps - 1)
        def _():
            lane = lax.broadcasted_iota(jnp.int32, (D_MODEL, LANE), 0) % HEAD_DIM
            fold = (lane == lax.broadcasted_iota(jnp.int32, (D_MODEL, LANE), 1)).astype(BF16)
            tot = sum(jnp.dot(p, fold, preferred_element_type=F32) for p in _split3(acc_ref[...]))
            dgq_ref[...] = tot[0:1, :HEAD_DIM]
            dgk_ref[...] = tot[1:2, :HEAD_DIM]

    gain = pl.BlockSpec((1, half), lambda i: (0, 0))
    small = pl.BlockSpec((1, HEAD_DIM), lambda i: (0, 0))
    per_head = lambda g: jnp.tile(g, (1, half // HEAD_DIM))
    return pl.pallas_call(
        body, name="qk_norm_bwd", grid=(nsteps,),
        in_specs=[pl.BlockSpec((2, ts, D_MODEL), lambda i: (0, i, 0)),
                  pl.BlockSpec((ts, 2 * D_MODEL), lambda i: (i, 0)), gain, gain],
        out_specs=[pl.BlockSpec((2, ts, D_MODEL), lambda i: (0, i, 0)), small, small],
        out_shape=[jax.ShapeDtypeStruct(dproj6.shape, dproj6.dtype),
                   jax.ShapeDtypeStruct((1, HEAD_DIM), F32), jax.ShapeDtypeStruct((1, HEAD_DIM), F32)],
        scratch_shapes=[pltpu.VMEM((8, D_MODEL), F32)],
        input_output_aliases={0: 0},
        compiler_params=_params("arbitrary"),
    )(dproj6, qk_raw, per_head(gq), per_head(gk))


CONV_ROWS = 512
HALO = 16


def _rows_with_halo(ref, r0, n, front, s_len):
    zeros = jnp.zeros((HALO, ref.shape[1]), F32)
    if front:
        return (jnp.concatenate([zeros, ref[0:n, :].astype(F32)], axis=0) if r0 == 0
                else ref[r0 - HALO:r0 + n, :].astype(F32))
    return (jnp.concatenate([ref[r0:r0 + n, :].astype(F32), zeros], axis=0) if r0 + n == s_len
            else ref[r0:r0 + n + HALO, :].astype(F32))


def _earlier(ext, k):
    return pltpu.roll(ext, k, 0)[HALO:]


def _later(ext, k):
    n = ext.shape[0]
    return pltpu.roll(ext, n - k, 0)[:n - HALO]


def _conv_cols(col0):
    return lambda s_len: pl.BlockSpec((s_len, LANE), lambda j: (0, col0 + j))


def _conv_fwd(proj, conv_w, conv_b, riders=()):
    s_len = proj.shape[0]

    def body(bg_ref, cg_ref, xc_ref, w_ref, b_ref, o_ref):
        w = [w_ref[t:t + 1, :] for t in range(3)]
        for r0 in range(0, s_len, CONV_ROWS):
            u = _rows_with_halo(cg_ref, r0, CONV_ROWS, True, s_len) * \
                _rows_with_halo(xc_ref, r0, CONV_ROWS, True, s_len)
            conv = b_ref[...] + w[0] * _earlier(u, 2) + w[1] * _earlier(u, 1) + w[2] * u[HALO:]
            o_ref[r0:r0 + CONV_ROWS, :] = (bg_ref[r0:r0 + CONV_ROWS, :].astype(F32) * conv).astype(o_ref.dtype)

    return _carry(
        riders, body, name="conv_fwd", grid=(D_MODEL // LANE,),
        in_specs=[_conv_cols(0)(s_len), _conv_cols(8)(s_len), _conv_cols(16)(s_len),
                  pl.BlockSpec((3, LANE), lambda j: (0, j)), pl.BlockSpec((1, LANE), lambda j: (0, j))],
        out_specs=pl.BlockSpec((s_len, LANE), lambda j: (0, j)),
        out_shape=jax.ShapeDtypeStruct((s_len, D_MODEL), BF16),
        semantics=("parallel",),
    )(proj, proj, proj, conv_w, conv_b)


def _conv_bwd(dproj6, dy, proj, conv_w, conv_b, riders=()):
    s_len = proj.shape[0]

    def body(dy_ref, bg_ref, cg_ref, xc_ref, w_ref, b_ref, _, dp_ref, dw_ref):
        w = [w_ref[t:t + 1, :] for t in range(3)]
        acc = [jnp.zeros((1, LANE), F32) for _ in range(4)]
        for r0 in range(0, s_len, CONV_ROWS):
            rows = slice(r0, r0 + CONV_ROWS)
            u = _rows_with_halo(cg_ref, r0, CONV_ROWS, True, s_len) * \
                _rows_with_halo(xc_ref, r0, CONV_ROWS, True, s_len)
            u2, u1, u0 = _earlier(u, 2), _earlier(u, 1), u[HALO:]
            conv = b_ref[...] + w[0] * u2 + w[1] * u1 + w[2] * u0
            dp_ref[0, rows, :] = (dy_ref[rows, :].astype(F32) * conv).astype(dp_ref.dtype)
            dconv_ext = _rows_with_halo(dy_ref, r0, CONV_ROWS, False, s_len) * \
                _rows_with_halo(bg_ref, r0, CONV_ROWS, False, s_len)
            dconv = dconv_ext[:CONV_ROWS]
            for t, term in enumerate([dconv * u2, dconv * u1, dconv * u0, dconv]):
                acc[t] = acc[t] + jnp.sum(term, axis=0, keepdims=True)
            du = w[2] * dconv + w[1] * _later(dconv_ext, 1) + w[0] * _later(dconv_ext, 2)
            dp_ref[1, rows, :] = (du * xc_ref[rows, :].astype(F32)).astype(dp_ref.dtype)
            dp_ref[2, rows, :] = (du * cg_ref[rows, :].astype(F32)).astype(dp_ref.dtype)
        dw_ref[...] = jnp.zeros_like(dw_ref)
        for t in range(4):
            dw_ref[t:t + 1, :] = acc[t]

    return _carry(
        riders, body, name="conv_bwd", grid=(D_MODEL // LANE,),
        in_specs=[pl.BlockSpec((s_len, LANE), lambda j: (0, j)),
                  _conv_cols(0)(s_len), _conv_cols(8)(s_len), _conv_cols(16)(s_len),
                  pl.BlockSpec((3, LANE), lambda j: (0, j)), pl.BlockSpec((1, LANE), lambda j: (0, j)),
                  pl.BlockSpec(memory_space=pl.ANY)],
        out_specs=[pl.BlockSpec((3, s_len, LANE), lambda j: (1, 0, j)),
                   pl.BlockSpec((8, LANE), lambda j: (0, j))],
        out_shape=[jax.ShapeDtypeStruct(dproj6.shape, dproj6.dtype),
                   jax.ShapeDtypeStruct((8, D_MODEL), F32)],
        input_output_aliases={6: 0},
        semantics=("parallel",),
    )(dy, proj, proj, proj, conv_w, conv_b, dproj6)


def _d_norm_input(name, terms, x, g, dres, out_dtype, riders=()):
    s_len, kdim = terms[0][0].shape[1], terms[0][1].shape[1]
    tm, chunk = 512, 512
    n = len(terms)

    def body(*refs):
        x_ref, g_ref, r_ref, o_ref, dg_ref = refs[2 * n:]
        dh = jnp.zeros((tm, kdim), F32)
        for p_ref, w_ref in zip(refs[:n], refs[n:2 * n]):
            n_piece, _, width = p_ref.shape
            per_shard = w_ref.shape[2]
            for c0 in range(0, n_piece * width, chunk):
                dh = dh + lax.dot_general(p_ref[c0 // width, :, c0 % width:c0 % width + chunk],
                                          w_ref[c0 // per_shard, :, c0 % per_shard:c0 % per_shard + chunk],
                                          NT, preferred_element_type=F32)
        xv = x_ref[...]
        r = lax.rsqrt(jnp.mean(xv * xv, axis=-1, keepdims=True) + EPS)
        xhat = xv * r
        u = dh * g_ref[...]
        dx = r * (u - xhat * jnp.mean(u * xhat, axis=-1, keepdims=True)) + r_ref[...].astype(F32)
        o_ref[...] = dx.astype(o_ref.dtype)
        part = jnp.sum(dh * xhat, axis=0, keepdims=True)

        @pl.when(pl.program_id(0) == 0)
        def _():
            dg_ref[...] = part

        @pl.when(pl.program_id(0) > 0)
        def _():
            dg_ref[...] += part

    tile = pl.BlockSpec((tm, kdim), lambda i: (i, 0))
    vec = pl.BlockSpec((1, kdim), lambda i: (0, 0))
    return _carry(
        riders, body, name=name, grid=(s_len // tm,),
        in_specs=[pl.BlockSpec((p.shape[0], tm, p.shape[2]), lambda i: (0, i, 0)) for p, _ in terms]
        + [pl.BlockSpec(w.shape, lambda i: (0, 0, 0), pipeline_mode=pl.Buffered(1)) for _, w in terms]
        + [tile, vec, tile],
        out_specs=[tile, vec],
        out_shape=[jax.ShapeDtypeStruct((s_len, kdim), out_dtype), jax.ShapeDtypeStruct((1, kdim), F32)],
        semantics=("arbitrary",),
    )(*[p for p, _ in terms], *[w for _, w in terms], x, g, dres)


def _local_grads(x, target, norm1_g, q_norm_g, k_norm_g, rel_bias, conv_b, b_gate, norm2_g, comm):
    s_len = x.shape[0]
    tm = min(1024, s_len)
    tb = min(2048, s_len)
    row = lambda v: v.reshape(1, -1)

    def carrying(at, fn, *args, **kw):
        riders = comm.riders(at)
        out = fn(*args, riders=riders, **kw)
        if riders:
            out, carried = out
            comm.done(at, carried)
        return out

    bias = carrying("bias_expand", _bias_expand, rel_bias)
    h = carrying("norm1", _rms_fwd, "norm1", x, row(norm1_g))
    w_in3, conv_w = comm.early_weights()
    gq, gk = row(q_norm_g), row(k_norm_g)

    def qk_epi(acc, e, o):
        o[0][...] = acc.astype(BF16)
        gain = jnp.where(pl.program_id(1) < 2, e[0] * QK_SCALE, e[1])
        o[1][...] = (_head_unit(acc)[0] * gain).astype(BF16)
    conv_in = carrying("proj_conv", _mm_fwd, "proj_conv", h, w_in3, tb, 1536, D_MODEL, col0=2, ncols=3 * D_MODEL,
                       outs=[_out2d(s_len, 3 * D_MODEL, BF16, tb, 1536)])
    small = pl.BlockSpec((1, 512), lambda i, j, k: (0, 0))
    per_head = lambda g: jnp.tile(g, (1, 512 // HEAD_DIM))
    qk_raw, qkn = carrying("proj_qk", _mm_fwd, "proj_qk", h, w_in3, tm, 512, D_MODEL, ncols=2 * D_MODEL,
                           epilogue=qk_epi, outs=[_out2d(s_len, 2 * D_MODEL, BF16, tm, 512)] * 2,
                           extras=[(per_head(gq), small), (per_head(gk), small)])
    v = carrying("proj_v", _mm_fwd, "proj_v", h, w_in3, tb, 512, D_MODEL, col0=4, ncols=D_MODEL,
                 outs=[_out2d(s_len, D_MODEL, BF16, tb, 512)])
    w_gate3 = comm.gate_weight()

    def gate_epi(acc, e, o):
        o[0][...] = jax.nn.sigmoid(acc + e[0]).astype(BF16)
    gates = carrying("gates", _mm_fwd, "gates", h, w_gate3, tb, 512, D_MODEL, epilogue=gate_epi,
                     outs=[_out2d(s_len, 2 * D_MODEL, BF16, tb, 512)],
                     extras=[(row(b_gate), pl.BlockSpec((1, 512), lambda i, j, k: (0, j)))])

    attn, lse = carrying("attn_fwd", _attn_fwd, qkn, v, bias)
    yconv = carrying("conv_fwd", _conv_fwd, conv_in, conv_w, row(conv_b))
    w_ap, w_cp, w_out, w_up3, w_down = comm.late_weights()
    tw = 1024
    ya = _mm_fwd("attn_proj", attn, w_ap, tm, tw, D_MODEL, outs=[_out2d(s_len, D_MODEL, BF16, tm, tw)])

    def merge_epi(acc, e, o):
        ya_v, ga, gc = [t.astype(F32) for t in e]
        o[0][...] = acc.astype(BF16)
        o[1][...] = (ga * ya_v + gc * acc).astype(BF16)
    gate_a, gate_c = (gates, _tile_spec(tm, tw, 0)), (gates, _tile_spec(tm, tw, 1))
    yc, merged = _mm_fwd("conv_proj", yconv, w_cp, tm, tw, D_MODEL, epilogue=merge_epi,
                         outs=[_out2d(s_len, D_MODEL, BF16, tm, tw), _out2d(s_len, D_MODEL, BF16, tm, tw)],
                         extras=[(ya, _tile_spec(tm, tw)), gate_a, gate_c])

    def res_epi(acc, e, o):
        x1_v = e[0] + acc
        o[0][...] = x1_v
        r = lax.rsqrt(jnp.mean(x1_v * x1_v, axis=-1, keepdims=True) + EPS)
        o[1][...] = (x1_v * r * e[1]).astype(BF16)
    assert tw == D_MODEL
    x1, h2 = _mm_fwd("out_proj", merged, w_out, tm, tw, D_MODEL, epilogue=res_epi,
                     outs=[_out2d(s_len, D_MODEL, F32, tm, tw), _out2d(s_len, D_MODEL, BF16, tm, tw)],
                     extras=[(x, _tile_spec(tm, tw)), (row(norm2_g), pl.BlockSpec((1, tw), lambda i, j, k: (0, 0)))])

    def up_epi(acc, e, o):
        o[0][...] = jnp.square(jnp.maximum(acc, 0.0)).astype(BF16)
    act = _mm_fwd("mlp_up", h2, w_up3, tb, tw, D_MODEL, epilogue=up_epi, outs=[_out2d(s_len, D_FF, BF16, tb, tw)])

    def loss_epi(acc, e, o):
        err = e[0] + acc - e[1]
        o[0][...] = (err * (1.0 / D_MODEL)).astype(BF16)
        sq = err * err
        part = sq[:, 0:LANE]
        for c0 in range(LANE, D_MODEL, LANE):
            part = part + sq[:, c0:c0 + LANE]
        o[1][...] = jnp.sum(part.reshape(tl // 8, 8, LANE), axis=0)
    tl = 512
    dy_b, loss_part = _mm_fwd(
        "mlp_down", act, w_down, tl, D_MODEL, D_FF, epilogue=loss_epi,
        outs=[_out2d(s_len, D_MODEL, BF16, tl, D_MODEL),
              (jax.ShapeDtypeStruct((8 * (s_len // tl), LANE), F32), pl.BlockSpec((8, LANE), lambda i, j, k: (i, 0)))],
        extras=[(x1, _tile_spec(tl, D_MODEL)), (target, _tile_spec(tl, D_MODEL))])

    def dup_epi(acc, e, o):
        o[0][...] = (acc * (2.0 * jnp.sqrt(e[0].astype(F32)))).astype(BF16)
    full = lambda cols: pl.BlockSpec((tm, cols), lambda i, j, n: (i, n))
    tokens = lambda cols: pl.BlockSpec((s_len, cols), lambda i, j, m: (m, j))
    dup = _mm_bwd_x("d_act", dy_b, full(D_MODEL), w_down, tm, tw, D_MODEL, s_len, D_MODEL, epilogue=dup_epi,
                    outs=[_out2d(s_len, D_FF, BF16, tm, tw)], extras=[(act, _tile_spec(tm, tw))])
    g_down = _mm_bwd_w("g_down", act, dy_b, tokens(D_MODEL), D_MODEL, 512, D_MODEL, s_len, False)
    g_up = _mm_bwd_w("g_up", h2, dup, tokens(512), D_FF, D_MODEL, 512, s_len, True)
    comm.grads_ready("mlp", dict(w_down=g_down, w_up=g_up))
    dx1_b, dg2 = carrying("d_h2", _d_norm_input, "d_h2", [(dup.reshape(1, s_len, D_FF), w_up3)],
                          x1, row(norm2_g), dy_b, BF16)

    def dmerge_epi(acc, e, o):
        ya_v, yc_v, ga, gc = [t.astype(F32) for t in e]
        o[0][...] = (acc * ga).astype(BF16)
        o[1][...] = (acc * gc).astype(BF16)
        o[2][0] = (acc * ya_v * ga * (1.0 - ga)).astype(BF16)
        o[2][1] = (acc * yc_v * gc * (1.0 - gc)).astype(BF16)
    dya, dyc, dgp2 = _mm_bwd_x(
        "d_merged", dx1_b, full(D_MODEL), w_out, tm, tw, D_MODEL, s_len, D_MODEL, epilogue=dmerge_epi,
        outs=[_out2d(s_len, D_MODEL, BF16, tm, tw), _out2d(s_len, D_MODEL, BF16, tm, tw),
              (jax.ShapeDtypeStruct((2, s_len, D_MODEL), BF16), pl.BlockSpec((2, tm, tw), lambda i, j, n: (0, i, j)))],
        extras=[(ya, _tile_spec(tm, tw)), (yc, _tile_spec(tm, tw)), gate_a, gate_c])
    g_out = _mm_bwd_w("g_out", merged, dx1_b, tokens(512), D_MODEL, D_MODEL, 512, s_len, False)
    d_attn = _mm_bwd_x("d_attn", dya, full(D_MODEL), w_ap, tm, tw, D_MODEL, s_len, D_MODEL,
                       outs=[_out2d(s_len, D_MODEL, BF16, tm, tw)])
    g_ap = _mm_bwd_w("g_attn_proj", attn, dya, tokens(512), D_MODEL, D_MODEL, 512, s_len, False)
    d_yconv = _mm_bwd_x("d_yconv", dyc, full(D_MODEL), w_cp, tm, tw, D_MODEL, s_len, D_MODEL,
                        outs=[_out2d(s_len, D_MODEL, BF16, tm, tw)])
    g_cp = _mm_bwd_w("g_conv_proj", yconv, dyc, tokens(512), D_MODEL, D_MODEL, 512, s_len, False)
    piece = lambda width: (lambda blk: (blk * width) // D_MODEL, lambda blk: (blk * width % D_MODEL) // width)
    pc, cb = piece(512)
    pieces = pl.BlockSpec((None, s_len, 512), lambda i, j, m: (pc(j), m, cb(j)))
    g_gate = _mm_bwd_w("g_gate", h, dgp2, pieces, 2 * D_MODEL, D_MODEL, 512, s_len, True)
    comm.grads_ready("proj", dict(w_out=g_out, w_attn_proj=g_ap, w_conv_proj=g_cp, w_gate=g_gate))

    dproj6, dbias = carrying("attn_bwd", _attn_bwd, qkn, v, attn, d_attn, bias, lse)
    dproj6, dgq, dgk = _qk_norm_bwd(dproj6, qk_raw, gq, gk)
    dproj6, dconv_wb = carrying("conv_bwd", _conv_bwd, dproj6, d_yconv, conv_in, conv_w, row(conv_b))

    g_in = carrying("g_in", _mm_bwd_w, "g_in", h, dproj6, pieces, 6 * D_MODEL, D_MODEL, 512, s_len, True)
    comm.grads_ready("in", dict(w_in=g_in))
    d_rel = carrying("bias_reduce", _bias_reduce, dbias)
    grad_x, dg1 = carrying("d_h", _d_norm_input, "d_h", [(dproj6, w_in3), (dgp2, w_gate3)],
                           x, row(norm1_g), dx1_b, F32)

    def bsum(t, f):
        return [], [t[0].astype(F32), t[1].astype(F32)]
    ts = 512
    db_a, db_c = carrying("b_gate_sum", _ew, "b_gate_sum", bsum,
                          [(dgp2, pl.BlockSpec((None, ts, D_MODEL), lambda i: (0, i, 0))),
                           (dgp2, pl.BlockSpec((None, ts, D_MODEL), lambda i: (1, i, 0)))],
                          [], [], sums=[D_MODEL, D_MODEL], ts=ts)

    big = dict(w_in=g_in, w_attn_proj=g_ap, w_conv_proj=g_cp, w_gate=g_gate, w_out=g_out,
               w_up=g_up, w_down=g_down)
    small = dict(norm1_g=dg1, norm2_g=dg2, conv_wb=dconv_wb, b_gate=(db_a, db_c),
                 q_norm_g=dgq, k_norm_g=dgk, rel_bias=d_rel)
    return loss_part, grad_x, big, small


def _finish_loss(loss_part):
    def body(l_ref, lo_ref):
        total = jnp.sum(jnp.sum(l_ref[...], axis=0, keepdims=True), axis=1, keepdims=True)
        lo_ref[...] = jnp.broadcast_to(total * (0.5 / D_MODEL), lo_ref.shape)

    return pl.pallas_call(body, name="finish_loss", out_shape=jax.ShapeDtypeStruct((8, LANE), F32))(loss_part)


def _place():
    return lax.axis_index("x"), lax.axis_index("y"), lax.axis_index("c")


def _other_chips(x, y):
    return [(1 - x, y), (x, 1 - y), (1 - x, 1 - y)]


def _cast_into_slot(name, where, w):
    r, cols = w.shape
    ts = 256

    def body(w_ref, x_ref, o_ref):
        o_ref[...] = x_ref[...].astype(o_ref.dtype)

    return pl.pallas_call(
        body, name=name,
        grid_spec=pltpu.PrefetchScalarGridSpec(
            num_scalar_prefetch=1, grid=(r // ts,),
            in_specs=[pl.BlockSpec((ts, cols), lambda i, w: (i, 0))],
            out_specs=pl.BlockSpec((None, ts, cols), lambda i, w: (w[0], i, 0))),
        out_shape=jax.ShapeDtypeStruct((N_SHARD, r, cols), BF16),
        compiler_params=_params("parallel"),
    )(where, w)


def _remote(src, dst, send, recv, k, to):
    return pltpu.make_async_remote_copy(src_ref=src, dst_ref=dst, send_sem=send.at[k], recv_sem=recv.at[k],
                                        device_id=to, device_id_type=MESH)


def _gather_riders(slots, relay):
    n = len(slots)
    near, per = (2, 5) if relay else (3, 3)

    def rows(w, which, part=None):
        hr = slots[w].shape[1] // 2
        if part is None:
            return pl.ds(which * hr, hr)
        if hr % 32:
            return pl.ds(which * hr, hr) if part == 0 else None
        return pl.ds(which * hr + part * (hr // 2), hr // 2)

    def copy(refs, w, k, shard, span, send, recv, to):
        ref = refs[w].at[shard, span]
        return _remote(ref, ref, send, recv, k, to)

    def places():
        x, y, c = _place()
        chips = _other_chips(x, y)
        return c, 2 * x + y, [2 * ch[0] + ch[1] for ch in chips], [(*ch, c) for ch in chips], (x, y, 1 - c)

    def chips_copies(refs, send, recv, sending):
        c, mine, theirs, peers, sib = places()
        return [copy(refs, w, near * w + j, mine if sending else theirs[j], rows(w, c), send, recv, peers[j])
                for w in range(n) for j in range(near)]

    def pass_copies(refs, send, recv, sending):
        c, mine, theirs, peers, sib = places()
        out = []
        for w in range(n):
            for j in range(near):
                out.append(copy(refs, w, per * w + j, theirs[j], rows(w, c if sending else 1 - c), send, recv, sib))
            for link in range(2 if relay else 0):
                span = rows(w, c, part=1 - link)
                if span is not None:
                    out.append(copy(refs, w, 5 * w + 2 + link, theirs[1 - link] if sending else theirs[2], span,
                                    send, recv, peers[link]))
        return out

    def last_copies(refs, send, recv, sending):
        c, mine, theirs, peers, sib = places()
        return [copy(refs, w, 5 * w + 4, theirs[2], rows(w, c if sending else 1 - c), send, recv, sib)
                for w in range(n if relay else 0)]

    def chips_start(_, refs, send, recv):
        for d in chips_copies(refs, send, recv, True):
            d.start()

    def chips_finish(_, refs, send, recv):
        for d in chips_copies(refs, send, recv, False):
            d.wait_recv()
        for d in chips_copies(refs, send, recv, True):
            d.wait_send()

    def sibling_start(_, refs, send, recv):
        for d in pass_copies(refs, send, recv, True):
            d.start()

    def sibling_finish(_, refs, send, recv):
        for d in pass_copies(refs, send, recv, False):
            d.wait_recv()
        for d in last_copies(refs, send, recv, True):
            d.start()
        for d in pass_copies(refs, send, recv, True):
            d.wait_send()
        for d in last_copies(refs, send, recv, False):
            d.wait_recv()
        for d in last_copies(refs, send, recv, True):
            d.wait_send()

    return (lambda arrays: _Rider([], arrays, near * n, chips_start, chips_finish),
            lambda arrays: _Rider([], arrays, per * n, sibling_start, sibling_finish))


def _pair_exchange_rider(grads, landing):
    n = len(grads)

    def copies(srcs, dsts, send, recv):
        x, y, c = _place()
        out = []
        for w in range(n):
            hr = grads[w].shape[1] // 2
            out.append(_remote(srcs[w].at[:, pl.ds((1 - c) * hr, hr)], dsts[w], send, recv, w, (x, y, 1 - c)))
        return out

    def start(srcs, dsts, send, recv):
        for cp in copies(srcs, dsts, send, recv):
            cp.start()

    def finish(srcs, dsts, send, recv):
        for cp in copies(srcs, dsts, send, recv):
            cp.wait()

    return _Rider(grads, landing, n, start, finish)


def _row_tile(hr):
    return min(hr, 256)


def _pair_add(name, where, grad, got):
    _, hr, cols = got.shape
    tr = _row_tile(hr)
    nblk = hr // tr

    def body(w_ref, g_ref, r_ref, o_ref):
        o_ref[...] = (g_ref[...] + r_ref[...]).astype(o_ref.dtype)

    other = lambda k, w: (w[0] + 1 + k) % N_SHARD
    return pl.pallas_call(
        body, name=name,
        grid_spec=pltpu.PrefetchScalarGridSpec(
            num_scalar_prefetch=1, grid=(N_SHARD - 1, nblk),
            in_specs=[pl.BlockSpec((None, tr, cols), lambda k, i, w: (other(k, w), w[1] * nblk + i, 0)),
                      pl.BlockSpec((None, tr, cols), lambda k, i, w: (other(k, w), i, 0))],
            out_specs=pl.BlockSpec((None, tr, cols), lambda k, i, w: (other(k, w), i, 0))),
        out_shape=jax.ShapeDtypeStruct(got.shape, BF16),
        compiler_params=_params("parallel", "parallel"),
    )(where, grad, got)


def _chip_exchange_rider(partials, landing):
    n = len(partials)

    def copies(srcs, dsts, send, recv):
        x, y, c = _place()
        return [_remote(srcs[w].at[2 * chip[0] + chip[1]], dsts[w].at[j], send, recv, 3 * w + j, (*chip, c))
                for w in range(n) for j, chip in enumerate(_other_chips(x, y))]

    def start(srcs, dsts, send, recv):
        for cp in copies(srcs, dsts, send, recv):
            cp.start()

    def finish(srcs, dsts, send, recv):
        for cp in copies(srcs, dsts, send, recv):
            cp.wait()

    return _Rider(partials, landing, 3 * n, start, finish)


def _final_add(name, where, grad, got, arrived):
    _, hr, cols = got.shape
    tr = _row_tile(hr)
    nblk = hr // tr

    def body(w_ref, g_ref, r_ref, a_ref, o_ref):
        acc = g_ref[...] + r_ref[...]
        for j in range(3):
            acc = acc + a_ref[j].astype(F32)
        o_ref[...] = acc

    return pl.pallas_call(
        body, name=name,
        grid_spec=pltpu.PrefetchScalarGridSpec(
            num_scalar_prefetch=1, grid=(nblk,),
            in_specs=[pl.BlockSpec((None, tr, cols), lambda i, w: (w[0], w[1] * nblk + i, 0)),
                      pl.BlockSpec((None, tr, cols), lambda i, w: (w[0], i, 0)),
                      pl.BlockSpec((3, tr, cols), lambda i, w: (0, i, 0))],
            out_specs=pl.BlockSpec((tr, cols), lambda i, w: (w[1] * nblk + i, 0))),
        out_shape=jax.ShapeDtypeStruct((2 * hr, cols), F32),
        compiler_params=_params("parallel"),
    )(where, grad, got, arrived)


def _pair_share_rider(shards):
    n = len(shards)

    def half(refs, w, which):
        hr = shards[w].shape[0] // 2
        return refs[w].at[pl.ds(which * hr, hr)]

    def start(_, refs, send, recv):
        x, y, c = _place()
        for w in range(n):
            _remote(half(refs, w, c), half(refs, w, c), send, recv, w, (x, y, 1 - c)).start()

    def finish(_, refs, send, recv):
        x, y, c = _place()
        for w in range(n):
            _remote(half(refs, w, 1 - c), half(refs, w, 1 - c), send, recv, w, (x, y, 1 - c)).wait_recv()
        for w in range(n):
            _remote(half(refs, w, c), half(refs, w, c), send, recv, w, (x, y, 1 - c)).wait_send()

    return _Rider([], shards, n, start, finish)


class _Exchange:
    PLAN = {"bias_expand": [("early", "gather")], "norm1": [("early", "forward")],
            "proj_conv": [("gate", "gather")], "proj_v": [("gate", "forward")],
            "proj_qk": [("late", "gather")], "gates": [("late", "forward")],
            "attn_fwd": [("mlp_w", "gather")], "conv_fwd": [("mlp_w", "forward")],
            "d_h2": [("mlp", "pair")], "attn_bwd": [("mlp", "chips"), ("proj", "pair")], "conv_bwd": [("mlp", "share")],
            "g_in": [("proj", "chips")], "bias_reduce": [("proj", "share"), ("in", "pair")],
            "d_h": [("in", "chips")], "b_gate_sum": [("in", "share")]}

    def __init__(self, where, early_slots, gate_slots, late_slots):
        self.where = where
        self.slots = dict(early=early_slots, gate=gate_slots, late=late_slots[:3], mlp_w=late_slots[3:])
        self.stage = {g: dict(zip(("gather", "forward"), _gather_riders(s, relay=g == "early"))) for g, s in self.slots.items()}
        self.groups, self.reduced, self.pending = {}, {}, []

    def early_weights(self):
        w_in3, small = self.slots["early"]
        conv_w = small[:, :3, :].transpose(1, 0, 2).reshape(3, N_SHARD * small.shape[2])
        return w_in3, conv_w

    def gate_weight(self):
        return self.slots["gate"][0]

    def late_weights(self):
        rows = lambda a: a.reshape(a.shape[0] * a.shape[1], a.shape[2])
        w_ap, w_cp, w_out = self.slots["late"]
        w_up3, w_down = self.slots["mlp_w"]
        return rows(w_ap), rows(w_cp), rows(w_out), w_up3, rows(w_down)

    def grads_ready(self, group, grads):
        names = list(grads)
        g4 = [g if g.ndim == 3 else g.reshape(N_SHARD, -1, g.shape[1]) for g in grads.values()]
        self.groups[group] = dict(names=names, g4=g4)

    def riders(self, at):
        self.pending = self.PLAN.get(at, [])
        out = []
        for group, stage in self.pending:
            if group in self.slots:
                out.append(self.stage[group][stage](self.slots[group]))
                continue
            st = self.groups[group]
            if stage == "pair":
                landing = [lax.empty((N_SHARD, g.shape[1] // 2, g.shape[2]), F32) for g in st["g4"]]
                out.append(_pair_exchange_rider(st["g4"], landing))
            elif stage == "chips":
                landing = [lax.empty((3,) + p.shape[1:], p.dtype) for p in st["partial"]]
                out.append(_chip_exchange_rider(st["partial"], landing))
            else:
                out.append(_pair_share_rider(st["halves"]))
        return out

    def done(self, at, carried):
        for (group, stage), arrays in zip(self.pending, carried):
            if group in self.slots:
                self.slots[group] = arrays
                continue
            st = self.groups[group]
            tag = lambda what, n: what + "_" + n
            if stage == "pair":
                st["got"] = arrays
                st["partial"] = [_pair_add(tag("pair_add", n), self.where, g, r)
                                 for n, g, r in zip(st["names"], st["g4"], arrays)]
            elif stage == "chips":
                st["halves"] = [_final_add(tag("final_add", n), self.where, g, r, a)
                                for n, g, r, a in zip(st["names"], st["g4"], st["got"], arrays)]
            else:
                self.reduced.update(zip(st["names"], arrays))


SMALL_ROWS = 32
N_DEV = 8


def _all_reduce_small(pack):
    def body(p_ref, o_ref, buf, send, recv):
        x, y, c = _place()
        buf[4 * x + 2 * y + c] = p_ref[...]
        copies, waits = [], []
        for k in range(1, N_DEV):
            px = 1 - x if k & 4 else x
            py = 1 - y if k & 2 else y
            pc = 1 - c if k & 1 else c
            copies.append(_remote(p_ref, buf.at[4 * x + 2 * y + c], send, recv, k - 1, (px, py, pc)))
            waits.append(_remote(p_ref, buf.at[4 * px + 2 * py + pc], send, recv, k - 1, (px, py, pc)))
        for cp in copies:
            cp.start()
        for cp in waits:
            cp.wait_recv()
        acc = buf[0]
        for d in range(1, N_DEV):
            acc = acc + buf[d]
        o_ref[...] = acc
        for cp in copies:
            cp.wait_send()

    return pl.pallas_call(
        body, name="all_reduce_small",
        out_shape=jax.ShapeDtypeStruct(pack.shape, F32),
        scratch_shapes=[pltpu.VMEM((N_DEV,) + pack.shape, F32),
                        pltpu.SemaphoreType.DMA((N_DEV - 1,)), pltpu.SemaphoreType.DMA((N_DEV - 1,))],
    )(pack)


def _adamw(name, w, g, m, v):
    c1 = 1.0 - ADAM_B1 ** ADAM_STEP
    c2 = 1.0 - ADAM_B2 ** ADAM_STEP

    def fn(t, f):
        wv, gv, mv, vv = t
        m2 = ADAM_B1 * mv + (1.0 - ADAM_B1) * gv
        v2 = ADAM_B2 * vv + (1.0 - ADAM_B2) * (gv * gv)
        delta = -ADAM_LR * ((m2 / c1) / (jnp.sqrt(v2 / c2) + ADAM_EPS) + ADAM_WD * wv)
        return [delta, m2, v2], []

    cols = w.shape[1]
    return _ew(name, fn, [w, g, m, v], [], [(cols, F32)] * 3, ts=min(w.shape[0], 256))


LOSS_ROW = 26


def _pack_small(norm1_g, norm2_g, conv_b, b_gate, conv_w, q_norm_g, k_norm_g, rel_bias, loss=None):
    pack = jnp.zeros((SMALL_ROWS, D_MODEL), F32)
    for r0, v in ((0, norm1_g), (1, norm2_g), (2, conv_b), (3, b_gate.reshape(2, D_MODEL)), (5, conv_w),
                  (8, q_norm_g), (9, k_norm_g), (10, rel_bias)) + (((LOSS_ROW, loss),) if loss is not None else ()):
        v = v.reshape(-1, v.shape[-1]).astype(F32)
        pack = pack.at[r0:r0 + v.shape[0], :v.shape[1]].set(v)
    return pack


def _unpack_small(pack, conv_cols):
    return dict(norm1_g=pack[0], norm2_g=pack[1], conv_b=pack[2], b_gate=pack[3:5].reshape(2 * D_MODEL),
                conv_w=pack[5:8, :conv_cols], q_norm_g=pack[8, :HEAD_DIM], k_norm_g=pack[9, :HEAD_DIM],
                rel_bias=pack[10:10 + N_HEADS, :N_REL])


BIG = ["w_in", "w_attn_proj", "w_conv_proj", "w_gate", "w_out", "w_up", "w_down"]
LATE = ["w_attn_proj", "w_conv_proj", "w_out", "w_up", "w_down"]
WEIGHTS = ["norm1_g", "w_in", "q_norm_g", "k_norm_g", "rel_bias", "conv_w", "conv_b", "w_attn_proj",
           "w_conv_proj", "w_gate", "b_gate", "w_out", "norm2_g", "w_up", "w_down"]


def kernel(x, norm1_g, w_in, q_norm_g, k_norm_g, rel_bias, conv_w, conv_b, w_attn_proj, w_conv_proj, w_gate, b_gate, w_out, norm2_g, w_up, w_down, loss_target, m_norm1_g, m_w_in, m_q_norm_g, m_k_norm_g, m_rel_bias, m_conv_w, m_conv_b, m_w_attn_proj, m_w_conv_proj, m_w_gate, m_b_gate, m_w_out, m_norm2_g, m_w_up, m_w_down, v_norm1_g, v_w_in, v_q_norm_g, v_k_norm_g, v_rel_bias, v_conv_w, v_conv_b, v_w_attn_proj, v_w_conv_proj, v_w_gate, v_b_gate, v_w_out, v_norm2_g, v_w_up, v_w_down):
    given = dict(locals())
    w = {n: given[n] for n in WEIGHTS}
    m = {n: given["m_" + n] for n in WEIGHTS}
    v = {n: given["v_" + n] for n in WEIGHTS}
    s_len = x.shape[1]
    shard = 2 * lax.axis_index("x") + lax.axis_index("y")
    where = jnp.stack([shard, lax.axis_index("c")]).astype(jnp.int32)
    conv_cols = conv_w.shape[1]

    small_in = lax.dynamic_update_slice(jnp.zeros((N_SHARD, 16, conv_cols), F32), conv_w[None], (shard, 0, 0))
    slot = {n: _cast_into_slot("cast_" + n, where, w[n]) for n in BIG}
    comm = _Exchange(where, [slot["w_in"], small_in], [slot["w_gate"]], [slot[n] for n in LATE])

    loss_part, grad_x, _, small = _local_grads(
        x.reshape(s_len, D_MODEL), loss_target.reshape(s_len, D_MODEL), norm1_g, q_norm_g, k_norm_g,
        rel_bias, conv_b, b_gate, norm2_g, comm)
    grad = dict(comm.reduced)

    loss_local = _finish_loss(loss_part)
    pack = _pack_small(small["norm1_g"], small["norm2_g"], small["conv_wb"][3], jnp.concatenate(small["b_gate"], axis=1),
                       small["conv_wb"][0:3], small["q_norm_g"], small["k_norm_g"], small["rel_bias"],
                       loss=loss_local[0:1, :])
    total = _all_reduce_small(pack)
    g_small = _unpack_small(total, D_MODEL)
    g_small["conv_w"] = lax.dynamic_slice(g_small["conv_w"], (0, shard * conv_cols), (3, conv_cols))
    grad.update(g_small)

    delta, new_m, new_v = {}, {}, {}
    for n in BIG:
        delta[n], new_m[n], new_v[n] = _adamw("adamw_" + n, w[n], grad[n], m[n], v[n])
    small_names = [n for n in WEIGHTS if n not in BIG]
    packs = [_pack_small(**{n: src[n] for n in small_names}) for src in (w, grad, m, v)]
    for out, packed in zip((delta, new_m, new_v), _adamw("adamw_small", *packs)):
        out.update({n: a.reshape(w[n].shape) for n, a in _unpack_small(packed, conv_cols).items()})

    outs = [total[LOSS_ROW, 0], grad_x.reshape(x.shape)]
    for group in (grad, delta, new_m, new_v):
        outs += [group[n].reshape(w[n].shape) for n in WEIGHTS]
    return tuple(outs)
```

```python
import jax
import jax.numpy as jnp
from jax import lax
from jax.experimental import pallas as pl
from jax.experimental.pallas import tpu as pltpu

F32 = jnp.float32
BF16 = jnp.bfloat16

D_MODEL = 1024
N_HEADS = 16
HEAD_DIM = 64
CHUNK = 64
N_PREV_CHUNKS = 8
MAX_REL = 256
D_FF = 4096
N_REL = 2 * MAX_REL + 1
REL_PAD = 640
EPS = 1e-6
NEG_INF = -1e30
QK_SCALE = HEAD_DIM ** -0.5

SUPER = 4 * CHUNK
BAND = SUPER + N_PREV_CHUNKS * CHUNK
SKEW_W = 1024
N_SHARD = 4
LANE = 128
MXU_DIM = 256
VMEM_LIMIT = 48 * 1024 * 1024

ADAM_LR = 0.001
ADAM_B1 = 0.9
ADAM_B2 = 0.999
ADAM_EPS = 1e-08
ADAM_WD = 0.01
ADAM_STEP = 10

MESH = pl.DeviceIdType.MESH
NN = (((1,), (0,)), ((), ()))
NT = (((1,), (1,)), ((), ()))
TN = (((0,), (0,)), ((), ()))


def _params(*sem):
    return pltpu.CompilerParams(dimension_semantics=sem or None, vmem_limit_bytes=VMEM_LIMIT)


HBM_SPEC = pl.BlockSpec(memory_space=pl.ANY)


class _Rider:
    def __init__(self, sources, arrays, n_sem, start, finish):
        self.sources, self.arrays, self.n_sem, self.start, self.finish = sources, arrays, n_sem, start, finish


def _carry(riders, body, *, name, out_shape, grid=(), in_specs=None, out_specs=None, scratch_shapes=(),
           semantics=(), input_output_aliases=None):
    aliases = dict(input_output_aliases or {})
    if not riders:
        kw = {} if in_specs is None else dict(in_specs=in_specs, out_specs=out_specs)
        return pl.pallas_call(body, name=name, grid=grid, out_shape=out_shape, scratch_shapes=scratch_shapes,
                              input_output_aliases=aliases, compiler_params=_params(*semantics), **kw)
    single = not isinstance(out_shape, (list, tuple))
    shapes = [out_shape] if single else list(out_shape)
    n_out, n_scr = len(shapes), len(scratch_shapes)
    in_hbm = lambda a: pltpu.with_memory_space_constraint(a, pltpu.HBM)
    srcs = [in_hbm(a) for r in riders for a in r.sources]
    arrs = [in_hbm(a) for r in riders for a in r.arrays]
    vmem = pl.BlockSpec(memory_space=pltpu.VMEM)

    def run(*args):
        n_in = len(args)

        def wrapped(*refs):
            pos = n_in
            src_refs = refs[pos:pos + len(srcs)]
            pos += len(srcs) + len(arrs)
            outs = refs[pos:pos + n_out]
            pos += n_out
            arr_refs = refs[pos:pos + len(arrs)]
            pos += len(arrs)
            scratch = refs[pos:pos + n_scr]
            sems = refs[pos + n_scr:]
            first, last = True, True
            for d, size in enumerate(grid):
                first = jnp.logical_and(first, pl.program_id(d) == 0)
                last = jnp.logical_and(last, pl.program_id(d) == size - 1)

            def each(method):
                s0 = a0 = 0
                for k, r in enumerate(riders):
                    getattr(r, method)(src_refs[s0:s0 + len(r.sources)], arr_refs[a0:a0 + len(r.arrays)],
                                       sems[2 * k], sems[2 * k + 1])
                    s0, a0 = s0 + len(r.sources), a0 + len(r.arrays)

            pl.when(first)(lambda: each("start"))
            body(*refs[:n_in], *outs, *scratch)
            pl.when(last)(lambda: each("finish"))

        ins = [vmem] * n_in if in_specs is None else list(in_specs)
        if out_specs is None:
            o_specs = [vmem] * n_out
        else:
            o_specs = [out_specs] if single else list(out_specs)
        for k in range(len(arrs)):
            aliases[n_in + len(srcs) + k] = n_out + k
        res = pl.pallas_call(
            wrapped, name=name, grid=grid,
            in_specs=ins + [HBM_SPEC] * (len(srcs) + len(arrs)),
            out_specs=o_specs + [HBM_SPEC] * len(arrs),
            out_shape=shapes + [jax.ShapeDtypeStruct(a.shape, a.dtype) for a in arrs],
            scratch_shapes=list(scratch_shapes) + [pltpu.SemaphoreType.DMA((r.n_sem,)) for r in riders for _ in range(2)],
            input_output_aliases=aliases,
            compiler_params=_params(*["arbitrary"] * len(grid)),
        )(*args, *srcs, *arrs)
        core, rest = res[:n_out], list(res[n_out:])
        carried, a0 = [], 0
        for r in riders:
            carried.append(rest[a0:a0 + len(r.arrays)])
            a0 += len(r.arrays)
        return (core[0] if single else core), carried

    return run


def _mm(name, dims, a, a_spec, b, b_spec, grid, tile, outs, epilogue=None, extras=(), riders=()):
    nk, ne, no = grid[2], len(extras), len(outs)

    def body(a_ref, b_ref, *refs):
        e_refs, o_refs = refs[:ne], refs[ne:ne + no]
        part = lax.dot_general(a_ref[...], b_ref[...], dims, preferred_element_type=F32)

        def finish(acc):
            if epilogue is None:
                o_refs[0][...] = acc.astype(o_refs[0].dtype)
            else:
                epilogue(acc, [r[...] for r in e_refs], o_refs)

        if nk == 1:
            finish(part)
        else:
            acc_ref = refs[ne + no]
            k = pl.program_id(2)

            @pl.when(k == 0)
            def _():
                acc_ref[...] = part

            @pl.when(k > 0)
            def _():
                acc_ref[...] += part

            @pl.when(k == nk - 1)
            def _():
                finish(acc_ref[...])

    res = _carry(
        riders, body, name=name, grid=grid,
        in_specs=[a_spec, b_spec] + [s for _, s in extras],
        out_specs=[s for _, s in outs],
        out_shape=[s for s, _ in outs],
        scratch_shapes=[pltpu.VMEM(tile, F32)] if nk > 1 else [],
        semantics=("parallel", "parallel", "arbitrary"),
    )(a, b, *[e for e, _ in extras])
    res, carried = res if riders else (res, None)
    res = res[0] if no == 1 else res
    return (res, carried) if riders else res


def _tile_spec(tm, tn, col0=0):
    return pl.BlockSpec((tm, tn), lambda i, j, k: (i, j + col0))


def _out2d(m, n, dtype, tm, tn):
    return (jax.ShapeDtypeStruct((m, n), dtype), _tile_spec(tm, tn))


def _mm_fwd(name, a, w, tm, tn, tk, outs=None, epilogue=None, extras=(), col0=0, ncols=None, riders=()):
    m, kdim = a.shape
    if w.ndim == 3:
        per = w.shape[2] // tn
        n = ncols or N_SHARD * w.shape[2]
        w_spec = pl.BlockSpec((None, tk, tn), lambda i, j, k: ((j + col0) // per, k, (j + col0) % per))
    else:
        n = ncols or w.shape[1]
        w_spec = pl.BlockSpec((tk, tn), lambda i, j, k: (k, j + col0))
    if outs is None:
        outs = [_out2d(m, n, F32, tm, tn)]
    return _mm(name, NN, a, pl.BlockSpec((tm, tk), lambda i, j, k: (i, k)), w, w_spec,
               (m // tm, n // tn, kdim // tk), (tm, tn), outs, epilogue, extras, riders)


def _mm_bwd_x(name, g, g_spec, w, tm, tj, tc, m, n_contract, outs=None, epilogue=None, extras=(), riders=()):
    if w.ndim == 3:
        per = w.shape[2] // tc
        kdim = w.shape[1]
        w_spec = pl.BlockSpec((None, tj, tc), lambda i, j, n: (n // per, j, n % per))
    else:
        kdim = w.shape[0]
        w_spec = pl.BlockSpec((tj, tc), lambda i, j, n: (j, n))
    if outs is None:
        outs = [_out2d(m, kdim, F32, tm, tj)]
    return _mm(name, NT, g, g_spec, w, w_spec, (m // tm, kdim // tj, n_contract // tc),
               (tm, tj), outs, epilogue, extras, riders)


def _mm_bwd_w(name, a, g, g_spec, n, tk, tn, tm, sharded, riders=()):
    m, kdim = a.shape
    if sharded:
        per = (n // N_SHARD) // tn
        out = (jax.ShapeDtypeStruct((N_SHARD, kdim, n // N_SHARD), F32),
               pl.BlockSpec((None, tk, tn), lambda i, j, mm: (j // per, i, j % per)))
    else:
        out = (jax.ShapeDtypeStruct((kdim, n), F32), pl.BlockSpec((tk, tn), lambda i, j, mm: (i, j)))
    return _mm(name, TN, a, pl.BlockSpec((tm, tk), lambda i, j, mm: (mm, i)), g, g_spec,
               (kdim // tk, n // tn, m // tm), (tk, tn), [out], riders=riders)


def _ew(name, fn, tiles, fulls, outs, sums=(), ts=512, riders=()):
    tiles = [t if isinstance(t, tuple) else (t, pl.BlockSpec((ts, t.shape[1]), lambda i: (i, 0)))
             for t in tiles]
    s_rows = tiles[0][0].shape[-2]
    nt, nf, no = len(tiles), len(fulls), len(outs)

    def body(*refs):
        t_vals = [r[...] for r in refs[:nt]]
        f_vals = [r[...] for r in refs[nt:nt + nf]]
        o_refs, s_refs = refs[nt + nf:nt + nf + no], refs[nt + nf + no:]
        o_vals, s_vals = fn(t_vals, f_vals)
        for r, v in zip(o_refs, o_vals):
            r[...] = v.astype(r.dtype)
        for r, v in zip(s_refs, s_vals):
            part = jnp.sum(v, axis=0, keepdims=True)

            @pl.when(pl.program_id(0) == 0)
            def _():
                r[...] = part

            @pl.when(pl.program_id(0) > 0)
            def _():
                r[...] += part

    full_specs = [pl.BlockSpec(f.shape, lambda i, nd=f.ndim: (0,) * nd) for f in fulls]
    return _carry(
        riders, body, name=name, grid=(s_rows // ts,),
        in_specs=[s for _, s in tiles] + full_specs,
        out_specs=[pl.BlockSpec((ts, c), lambda i: (i, 0)) for c, _ in outs]
        + [pl.BlockSpec((1, c), lambda i: (0, 0)) for c in sums],
        out_shape=[jax.ShapeDtypeStruct((s_rows, c), dt) for c, dt in outs]
        + [jax.ShapeDtypeStruct((1, c), F32) for c in sums],
        semantics=("arbitrary",),
    )(*[t for t, _ in tiles], *fulls)


def _rms_fwd(name, x, g, riders=()):
    def fn(t, f):
        xv = t[0]
        r = lax.rsqrt(jnp.mean(xv * xv, axis=-1, keepdims=True) + EPS)
        return [xv * r * f[0]], []
    out = _ew(name, fn, [x], [g], [(x.shape[1], BF16)], riders=riders)
    return (out[0][0], out[1]) if riders else out[0]


def _split3(x):
    x1 = x.astype(BF16)
    r1 = x - x1.astype(F32)
    x2 = r1.astype(BF16)
    x3 = (r1 - x2.astype(F32)).astype(BF16)
    return x1, x2, x3


def _rel_class(cp):
    far = (cp < MAX_REL) | (cp > BAND)
    return jnp.where(far, 2 * MAX_REL, BAND - cp)


def _skew_rows(x, sign):
    row = lax.broadcasted_iota(jnp.int32, x.shape, 0)
    for b in range(CHUNK.bit_length() - 1):
        shift = (1 << b) if sign > 0 else SKEW_W - (1 << b)
        x = jnp.where((row >> b) & 1 == 1, pltpu.roll(x, shift, 1), x)
    return x


def _roll_lanes(x, shift):
    return x if shift % SKEW_W == 0 else pltpu.roll(x, shift % SKEW_W, 1)


N_START = 3


def _bias_expand(rel_bias, riders=()):
    rel = jnp.pad(rel_bias, ((0, 0), (0, REL_PAD - N_REL))).reshape(N_HEADS, 1, REL_PAD)

    def body(rel_ref, o_ref):
        cls = lax.broadcasted_iota(jnp.int32, (REL_PAD, SKEW_W), 0)
        cp = lax.broadcasted_iota(jnp.int32, (REL_PAD, SKEW_W), 1)
        onehot = (cls == _rel_class(cp)).astype(BF16)
        rel8 = jnp.broadcast_to(rel_ref[...], (8, REL_PAD))
        trow = sum(jnp.dot(p, onehot, preferred_element_type=F32) for p in _split3(rel8))[0:1]
        first = _skew_rows(jnp.broadcast_to(trow, (CHUNK, SKEW_W)), +1)
        full = jnp.concatenate([_roll_lanes(first, CHUNK * g) for g in range(SUPER // CHUNK)], axis=0)[:, :BAND]
        qc = lax.broadcasted_iota(jnp.int32, (SUPER, BAND), 0) // CHUNK
        kc = lax.broadcasted_iota(jnp.int32, (SUPER, BAND), 1) // CHUNK
        on_band = (kc >= qc) & (kc <= qc + N_PREV_CHUNKS)
        table = jnp.where(on_band, full, NEG_INF).T
        key = lax.broadcasted_iota(jnp.int32, (BAND, SUPER), 0)
        for t in range(N_START):
            o_ref[t] = jnp.where(key < (N_START - 1 - t) * SUPER, NEG_INF, table)

    return _carry(
        riders, body, name="bias_expand", grid=(N_HEADS,),
        in_specs=[pl.BlockSpec((None, 1, REL_PAD), lambda h: (h, 0, 0))],
        out_specs=pl.BlockSpec((N_START, None, BAND, SUPER), lambda h: (0, h, 0, 0)),
        out_shape=jax.ShapeDtypeStruct((N_START, N_HEADS, BAND, SUPER), F32),
        semantics=("arbitrary",),
    )(rel)


def _bias_reduce(dbias, riders=()):
    def body(d_ref, o_ref):
        x = jnp.concatenate([d_ref[...].T, jnp.zeros((SUPER, SKEW_W - BAND), F32)], axis=1)
        folded = sum(_roll_lanes(x[CHUNK * g:CHUNK * (g + 1)], -CHUNK * g) for g in range(SUPER // CHUNK))
        diag = jnp.sum(_skew_rows(folded, -1), axis=0, keepdims=True)
        cp = lax.broadcasted_iota(jnp.int32, (SKEW_W, REL_PAD), 0)
        cls = lax.broadcasted_iota(jnp.int32, (SKEW_W, REL_PAD), 1)
        onehot = (cls == _rel_class(cp)).astype(BF16)
        diag8 = jnp.broadcast_to(diag, (8, SKEW_W))
        o_ref[...] = sum(jnp.dot(p, onehot, preferred_element_type=F32) for p in _split3(diag8))[0:1]

    out = _carry(
        riders, body, name="bias_reduce", grid=(N_HEADS,),
        in_specs=[pl.BlockSpec((None, BAND, SUPER), lambda h: (h, 0, 0))],
        out_specs=pl.BlockSpec((None, 1, REL_PAD), lambda h: (h, 0, 0)),
        out_shape=jax.ShapeDtypeStruct((N_HEADS, 1, REL_PAD), F32),
        semantics=("arbitrary",),
    )(dbias)
    out, carried = out if riders else (out, None)
    out = out.reshape(N_HEADS, REL_PAD)[:, :N_REL]
    return (out, carried) if riders else out


HEADS_PER_STEP = 8
HEAD_COLS = HEADS_PER_STEP * HEAD_DIM
N_HEAD_GROUPS = N_HEADS // HEADS_PER_STEP


def _scores_t(qs, kn, bias_t):
    return jnp.concatenate([lax.dot_general(k, qs, NT, preferred_element_type=F32) for k in kn], axis=0) + bias_t


def _bias_spec():
    return pl.BlockSpec((None, HEADS_PER_STEP, BAND, SUPER), lambda hg, i: (jnp.minimum(i, N_START - 1), hg, 0, 0))


def _band_specs(nb, col0, clamp_hi):
    def spec(d):
        def index(hg, i):
            blk = jnp.maximum(i - d, 0)
            if clamp_hi:
                blk = jnp.minimum(blk, nb - 1)
            return (blk, col0 + hg)
        return pl.BlockSpec((SUPER, HEAD_COLS), index)
    return [spec(2), spec(1), spec(0)]


def _head_sums(y):
    same_head = (lax.broadcasted_iota(jnp.int32, (MXU_DIM, MXU_DIM), 0) // HEAD_DIM
                 == lax.broadcasted_iota(jnp.int32, (MXU_DIM, MXU_DIM), 1) // HEAD_DIM).astype(BF16)
    sums = []
    for c0 in range(0, y.shape[1], MXU_DIM):
        chunk = y[:, c0:c0 + MXU_DIM]
        hi = chunk.astype(BF16)
        lo = (chunk - hi.astype(F32)).astype(BF16)
        sums.append(jnp.dot(hi, same_head, preferred_element_type=F32)
                    + jnp.dot(lo, same_head, preferred_element_type=F32))
    return jnp.concatenate(sums, axis=1)


def _head_unit(x):
    r = lax.rsqrt(_head_sums(x * x) * (1.0 / HEAD_DIM) + EPS)
    return x * r, r


def _head(hh):
    return slice(HEAD_DIM * hh, HEAD_DIM * (hh + 1))


def _key_block(j):
    return slice(SUPER * j, SUPER * (j + 1))


LSE_ROWS = 8


def _attn_fwd(qkn, v, bias, riders=()):
    s_len = qkn.shape[0]
    nb = s_len // SUPER

    def body(q_ref, k0, k1, k2, v0, v1, v2, b_ref, o_ref, lse_ref):
        outs = []
        v_t = [v0[...].T, v1[...].T, v2[...].T]

        def probabilities(hh):
            sl = _head(hh)
            s = _scores_t(q_ref[:, sl], [k0[:, sl], k1[:, sl], k2[:, sl]], b_ref[hh])
            m = jnp.max(s, axis=0, keepdims=True)
            e = jnp.exp(s - m)
            l = jnp.sum(e, axis=0, keepdims=True)
            lse_ref[hh:hh + 1, :] = m + jnp.log(l)
            return e.astype(BF16), 1.0 / l, sl

        def weighted_values(e, inv_l, sl):
            outs.append(sum(jnp.dot(v_t[j][sl, :], e[_key_block(j), :], preferred_element_type=F32)
                            for j in range(3)) * inv_l)

        ready = probabilities(0)
        for hh in range(1, HEADS_PER_STEP):
            following = probabilities(hh)
            weighted_values(*ready)
            ready = following
        weighted_values(*ready)
        o_ref[...] = jnp.concatenate(outs, axis=0).T.astype(o_ref.dtype)

    return _carry(
        riders, body, name="attn_fwd", grid=(N_HEAD_GROUPS, nb),
        in_specs=[pl.BlockSpec((SUPER, HEAD_COLS), lambda hg, i: (i, hg))]
        + _band_specs(nb, N_HEAD_GROUPS, False) + _band_specs(nb, 0, False) + [_bias_spec()],
        out_specs=[pl.BlockSpec((SUPER, HEAD_COLS), lambda hg, i: (i, hg)),
                   pl.BlockSpec((None, LSE_ROWS, SUPER), lambda hg, i: (hg, 0, i))],
        out_shape=[jax.ShapeDtypeStruct((s_len, D_MODEL), BF16),
                   jax.ShapeDtypeStruct((N_HEAD_GROUPS, LSE_ROWS, s_len), F32)],
        semantics=("parallel", "arbitrary"),
    )(qkn, qkn, qkn, qkn, v, v, v, bias)


def _attn_bwd(qkn, v, out, d_out, bias, lse, riders=()):
    s_len = qkn.shape[0]
    nb = s_len // SUPER

    def body(q_ref, k0, k1, k2, v0, v1, v2, o_ref, do_ref, b_ref, lse_ref, dp_ref, db_ref, aq_ref, ak_ref, av_ref):
        i = pl.program_id(1)

        @pl.when(i == 0)
        def _():
            aq_ref[...] = jnp.zeros_like(aq_ref)
            ak_ref[...] = jnp.zeros_like(ak_ref)
            av_ref[...] = jnp.zeros_like(av_ref)
            db_ref[...] = jnp.zeros_like(db_ref)

        @pl.when(i < nb)
        def _():
            dq, dk, dv = [], [[], [], []], [[], [], []]
            ones = jnp.ones((8, HEAD_DIM), BF16)
            k_t = [k0[...].T, k1[...].T, k2[...].T]

            def softmax_grad(hh):
                sl = _head(hh)
                qs, do = q_ref[:, sl], do_ref[:, sl]
                kn = [k0[:, sl], k1[:, sl], k2[:, sl]]
                prod = do.astype(F32) * o_ref[:, sl].astype(F32)
                hi = prod.astype(BF16)
                lo = (prod - hi.astype(F32)).astype(BF16)
                delta = (lax.dot_general(ones, hi, NT, preferred_element_type=F32)
                         + lax.dot_general(ones, lo, NT, preferred_element_type=F32))[0:1]
                lse_row = lse_ref[hh:hh + 1, :]
                pb, dsb = [], []
                for j, (kj, vj) in enumerate(zip(kn, (v0, v1, v2))):
                    rows = _key_block(j)
                    p = jnp.exp(lax.dot_general(kj, qs, NT, preferred_element_type=F32) + b_ref[hh, rows, :] - lse_row)
                    ds = p * (lax.dot_general(vj[:, sl], do, NT, preferred_element_type=F32) - delta)
                    db_ref[hh, rows, :] += ds
                    pb.append(p.astype(BF16))
                    dsb.append(ds.astype(BF16))
                return pb, dsb, qs, do, sl

            def operand_grads(pb, dsb, qs, do, sl):
                dq.append(sum(jnp.dot(k_t[j][sl, :], dsb[j], preferred_element_type=F32) for j in range(3)))
                for j in range(3):
                    dv[j].append(jnp.dot(pb[j], do, preferred_element_type=F32))
                    dk[j].append(jnp.dot(dsb[j], qs, preferred_element_type=F32))

            ready = softmax_grad(0)
            for hh in range(1, HEADS_PER_STEP):
                following = softmax_grad(hh)
                operand_grads(*ready)
                ready = following
            operand_grads(*ready)
            aq_ref[i % 3] = jnp.concatenate(dq, axis=0).T
            for j in range(3):
                slot = (i + 1 + j) % 3
                if j < 2:
                    ak_ref[slot] += jnp.concatenate(dk[j], axis=1)
                    av_ref[slot] += jnp.concatenate(dv[j], axis=1)
                else:
                    ak_ref[slot] = jnp.concatenate(dk[j], axis=1)
                    av_ref[slot] = jnp.concatenate(dv[j], axis=1)

        slot = (i + 1) % 3
        dp_ref[0] = aq_ref[slot].astype(dp_ref.dtype)
        dp_ref[1] = ak_ref[slot].astype(dp_ref.dtype)
        dp_ref[2] = av_ref[slot].astype(dp_ref.dtype)

    def qrow(hg, i):
        return (jnp.minimum(i, nb - 1), hg)

    return _carry(
        riders, body, name="attn_bwd", grid=(N_HEAD_GROUPS, nb + 2),
        in_specs=[pl.BlockSpec((SUPER, HEAD_COLS), qrow)]
        + _band_specs(nb, N_HEAD_GROUPS, True) + _band_specs(nb, 0, True)
        + [pl.BlockSpec((SUPER, HEAD_COLS), qrow), pl.BlockSpec((SUPER, HEAD_COLS), qrow), _bias_spec(),
           pl.BlockSpec((None, LSE_ROWS, SUPER), lambda hg, i: (hg, 0, jnp.minimum(i, nb - 1)))],
        out_specs=[pl.BlockSpec((3, SUPER, HEAD_COLS), lambda hg, i: (0, jnp.maximum(i - 2, 0), hg)),
                   pl.BlockSpec((HEADS_PER_STEP, BAND, SUPER), lambda hg, i: (hg, 0, 0))],
        out_shape=[jax.ShapeDtypeStruct((6, s_len, D_MODEL), BF16),
                   jax.ShapeDtypeStruct((N_HEADS, BAND, SUPER), F32)],
        scratch_shapes=[pltpu.VMEM((3, SUPER, HEAD_COLS), F32)] * 3,
        semantics=("parallel", "arbitrary"),
    )(qkn, qkn, qkn, qkn, v, v, v, out, d_out, bias, lse)


def _qk_norm_bwd(dproj6, qk_raw, gq, gk):
    s_len = qk_raw.shape[0]
    ts = min(1024, s_len)

    nsteps = s_len // ts
    half = D_MODEL // 2

    def body(d_ref, raw_ref, gq_ref, gk_ref, o_ref, dgq_ref, dgk_ref, acc_ref):
        step = pl.program_id(0)

        @pl.when(step == 0)
        def _():
            acc_ref[...] = jnp.zeros_like(acc_ref)

        for piece, (g_ref, scale) in enumerate(((gq_ref, QK_SCALE), (gk_ref, 1.0))):
            for c0 in (0, half):
                xhat, r = _head_unit(raw_ref[:, piece * D_MODEL + c0:piece * D_MODEL + c0 + half].astype(F32))
                dn = d_ref[piece, :, c0:c0 + half].astype(F32) * scale
                u = dn * g_ref[...]
                dx = r * (u - xhat * (_head_sums(u * xhat) * (1.0 / HEAD_DIM)))
                o_ref[piece, :, c0:c0 + half] = dx.astype(o_ref.dtype)
                acc_ref[piece:piece + 1, c0:c0 + half] += jnp.sum(dn * xhat, axis=0, keepdims=True)

        @pl.when(step == nsteps - 1)
        def _():
            lane = lax.broadcasted_iota(jnp.int32, (D_MODEL, LANE), 0) % HEAD_DIM
            fold = (lane == lax.broadcasted_iota(jnp.int32, (D_MODEL, LANE), 1)).astype(BF16)
            tot = sum(jnp.dot(p, fold, preferred_element_type=F32) for p in _split3(acc_ref[...]))
            dgq_ref[...] = tot[0:1, :HEAD_DIM]
            dgk_ref[...] = tot[1:2, :HEAD_DIM]

    gain = pl.BlockSpec((1, half), lambda i: (0, 0))
    small = pl.BlockSpec((1, HEAD_DIM), lambda i: (0, 0))
    per_head = lambda g: jnp.tile(g, (1, half // HEAD_DIM))
    return pl.pallas_call(
        body, name="qk_norm_bwd", grid=(nsteps,),
        in_specs=[pl.BlockSpec((2, ts, D_MODEL), lambda i: (0, i, 0)),
                  pl.BlockSpec((ts, 2 * D_MODEL), lambda i: (i, 0)), gain, gain],
        out_specs=[pl.BlockSpec((2, ts, D_MODEL), lambda i: (0, i, 0)), small, small],
        out_shape=[jax.ShapeDtypeStruct(dproj6.shape, dproj6.dtype),
                   jax.ShapeDtypeStruct((1, HEAD_DIM), F32), jax.ShapeDtypeStruct((1, HEAD_DIM), F32)],
        scratch_shapes=[pltpu.VMEM((8, D_MODEL), F32)],
        input_output_aliases={0: 0},
        compiler_params=_params("arbitrary"),
    )(dproj6, qk_raw, per_head(gq), per_head(gk))


CONV_ROWS = 512
HALO = 16


def _rows_with_halo(ref, r0, n, front, s_len):
    zeros = jnp.zeros((HALO, ref.shape[1]), F32)
    if front:
        return (jnp.concatenate([zeros, ref[0:n, :].astype(F32)], axis=0) if r0 == 0
                else ref[r0 - HALO:r0 + n, :].astype(F32))
    return (jnp.concatenate([ref[r0:r0 + n, :].astype(F32), zeros], axis=0) if r0 + n == s_len
            else ref[r0:r0 + n + HALO, :].astype(F32))


def _earlier(ext, k):
    return pltpu.roll(ext, k, 0)[HALO:]


def _later(ext, k):
    n = ext.shape[0]
    return pltpu.roll(ext, n - k, 0)[:n - HALO]


def _conv_cols(col0):
    return lambda s_len: pl.BlockSpec((s_len, LANE), lambda j: (0, col0 + j))


def _conv_fwd(proj, conv_w, conv_b, riders=()):
    s_len = proj.shape[0]

    def body(bg_ref, cg_ref, xc_ref, w_ref, b_ref, o_ref):
        w = [w_ref[t:t + 1, :] for t in range(3)]
        for r0 in range(0, s_len, CONV_ROWS):
            u = _rows_with_halo(cg_ref, r0, CONV_ROWS, True, s_len) * \
                _rows_with_halo(xc_ref, r0, CONV_ROWS, True, s_len)
            conv = b_ref[...] + w[0] * _earlier(u, 2) + w[1] * _earlier(u, 1) + w[2] * u[HALO:]
            o_ref[r0:r0 + CONV_ROWS, :] = (bg_ref[r0:r0 + CONV_ROWS, :].astype(F32) * conv).astype(o_ref.dtype)

    return _carry(
        riders, body, name="conv_fwd", grid=(D_MODEL // LANE,),
        in_specs=[_conv_cols(0)(s_len), _conv_cols(8)(s_len), _conv_cols(16)(s_len),
                  pl.BlockSpec((3, LANE), lambda j: (0, j)), pl.BlockSpec((1, LANE), lambda j: (0, j))],
        out_specs=pl.BlockSpec((s_len, LANE), lambda j: (0, j)),
        out_shape=jax.ShapeDtypeStruct((s_len, D_MODEL), BF16),
        semantics=("parallel",),
    )(proj, proj, proj, conv_w, conv_b)


def _conv_bwd(dproj6, dy, proj, conv_w, conv_b, riders=()):
    s_len = proj.shape[0]

    def body(dy_ref, bg_ref, cg_ref, xc_ref, w_ref, b_ref, _, dp_ref, dw_ref):
        w = [w_ref[t:t + 1, :] for t in range(3)]
        acc = [jnp.zeros((1, LANE), F32) for _ in range(4)]
        for r0 in range(0, s_len, CONV_ROWS):
            rows = slice(r0, r0 + CONV_ROWS)
            u = _rows_with_halo(cg_ref, r0, CONV_ROWS, True, s_len) * \
                _rows_with_halo(xc_ref, r0, CONV_ROWS, True, s_len)
            u2, u1, u0 = _earlier(u, 2), _earlier(u, 1), u[HALO:]
            conv = b_ref[...] + w[0] * u2 + w[1] * u1 + w[2] * u0
            dp_ref[0, rows, :] = (dy_ref[rows, :].astype(F32) * conv).astype(dp_ref.dtype)
            dconv_ext = _rows_with_halo(dy_ref, r0, CONV_ROWS, False, s_len) * \
                _rows_with_halo(bg_ref, r0, CONV_ROWS, False, s_len)
            dconv = dconv_ext[:CONV_ROWS]
            for t, term in enumerate([dconv * u2, dconv * u1, dconv * u0, dconv]):
                acc[t] = acc[t] + jnp.sum(term, axis=0, keepdims=True)
            du = w[2] * dconv + w[1] * _later(dconv_ext, 1) + w[0] * _later(dconv_ext, 2)
            dp_ref[1, rows, :] = (du * xc_ref[rows, :].astype(F32)).astype(dp_ref.dtype)
            dp_ref[2, rows, :] = (du * cg_ref[rows, :].astype(F32)).astype(dp_ref.dtype)
        dw_ref[...] = jnp.zeros_like(dw_ref)
        for t in range(4):
            dw_ref[t:t + 1, :] = acc[t]

    return _carry(
        riders, body, name="conv_bwd", grid=(D_MODEL // LANE,),
        in_specs=[pl.BlockSpec((s_len, LANE), lambda j: (0, j)),
                  _conv_cols(0)(s_len), _conv_cols(8)(s_len), _conv_cols(16)(s_len),
                  pl.BlockSpec((3, LANE), lambda j: (0, j)), pl.BlockSpec((1, LANE), lambda j: (0, j)),
                  pl.BlockSpec(memory_space=pl.ANY)],
        out_specs=[pl.BlockSpec((3, s_len, LANE), lambda j: (1, 0, j)),
                   pl.BlockSpec((8, LANE), lambda j: (0, j))],
        out_shape=[jax.ShapeDtypeStruct(dproj6.shape, dproj6.dtype),
                   jax.ShapeDtypeStruct((8, D_MODEL), F32)],
        input_output_aliases={6: 0},
        semantics=("parallel",),
    )(dy, proj, proj, proj, conv_w, conv_b, dproj6)


def _d_norm_input(name, terms, x, g, dres, out_dtype, riders=()):
    s_len, kdim = terms[0][0].shape[1], terms[0][1].shape[1]
    tm, chunk = 512, 512
    n = len(terms)

    def body(*refs):
        x_ref, g_ref, r_ref, o_ref, dg_ref = refs[2 * n:]
        dh = jnp.zeros((tm, kdim), F32)
        for p_ref, w_ref in zip(refs[:n], refs[n:2 * n]):
            n_piece, _, width = p_ref.shape
            per_shard = w_ref.shape[2]
            for c0 in range(0, n_piece * width, chunk):
                dh = dh + lax.dot_general(p_ref[c0 // width, :, c0 % width:c0 % width + chunk],
                                          w_ref[c0 // per_shard, :, c0 % per_shard:c0 % per_shard + chunk],
                                          NT, preferred_element_type=F32)
        xv = x_ref[...]
        r = lax.rsqrt(jnp.mean(xv * xv, axis=-1, keepdims=True) + EPS)
        xhat = xv * r
        u = dh * g_ref[...]
        dx = r * (u - xhat * jnp.mean(u * xhat, axis=-1, keepdims=True)) + r_ref[...].astype(F32)
        o_ref[...] = dx.astype(o_ref.dtype)
        part = jnp.sum(dh * xhat, axis=0, keepdims=True)

        @pl.when(pl.program_id(0) == 0)
        def _():
            dg_ref[...] = part

        @pl.when(pl.program_id(0) > 0)
        def _():
            dg_ref[...] += part

    tile = pl.BlockSpec((tm, kdim), lambda i: (i, 0))
    vec = pl.BlockSpec((1, kdim), lambda i: (0, 0))
    return _carry(
        riders, body, name=name, grid=(s_len // tm,),
        in_specs=[pl.BlockSpec((p.shape[0], tm, p.shape[2]), lambda i: (0, i, 0)) for p, _ in terms]
        + [pl.BlockSpec(w.shape, lambda i: (0, 0, 0), pipeline_mode=pl.Buffered(1)) for _, w in terms]
        + [tile, vec, tile],
        out_specs=[tile, vec],
        out_shape=[jax.ShapeDtypeStruct((s_len, kdim), out_dtype), jax.ShapeDtypeStruct((1, kdim), F32)],
        semantics=("arbitrary",),
    )(*[p for p, _ in terms], *[w for _, w in terms], x, g, dres)


def _local_grads(x, target, norm1_g, q_norm_g, k_norm_g, rel_bias, conv_b, b_gate, norm2_g, comm):
    s_len = x.shape[0]
    tm = min(1024, s_len)
    tb = min(2048, s_len)
    row = lambda v: v.reshape(1, -1)

    def carrying(at, fn, *args, **kw):
        riders = comm.riders(at)
        out = fn(*args, riders=riders, **kw)
        if riders:
            out, carried = out
            comm.done(at, carried)
        return out

    bias = carrying("bias_expand", _bias_expand, rel_bias)
    h = carrying("norm1", _rms_fwd, "norm1", x, row(norm1_g))
    w_gate3 = comm.gate_weight()

    def gate_epi(acc, e, o):
        o[0][...] = jax.nn.sigmoid(acc + e[0]).astype(BF16)
    gates = carrying("gates", _mm_fwd, "gates", h, w_gate3, tb, 512, D_MODEL, epilogue=gate_epi,
                     outs=[_out2d(s_len, 2 * D_MODEL, BF16, tb, 512)],
                     extras=[(row(b_gate), pl.BlockSpec((1, 512), lambda i, j, k: (0, j)))])
    w_in3, conv_w = comm.early_weights()
    gq, gk = row(q_norm_g), row(k_norm_g)

    def qk_epi(acc, e, o):
        o[0][...] = acc.astype(BF16)
        gain = jnp.where(pl.program_id(1) < 2, e[0] * QK_SCALE, e[1])
        o[1][...] = (_head_unit(acc)[0] * gain).astype(BF16)
    conv_in = carrying("proj_conv", _mm_fwd, "proj_conv", h, w_in3, tb, 1536, D_MODEL, col0=2, ncols=3 * D_MODEL,
                       outs=[_out2d(s_len, 3 * D_MODEL, BF16, tb, 1536)])
    small = pl.BlockSpec((1, 512), lambda i, j, k: (0, 0))
    per_head = lambda g: jnp.tile(g, (1, 512 // HEAD_DIM))
    qk_raw, qkn = carrying("proj_qk", _mm_fwd, "proj_qk", h, w_in3, tm, 512, D_MODEL, ncols=2 * D_MODEL,
                           epilogue=qk_epi, outs=[_out2d(s_len, 2 * D_MODEL, BF16, tm, 512)] * 2,
                           extras=[(per_head(gq), small), (per_head(gk), small)])
    v = carrying("proj_v", _mm_fwd, "proj_v", h, w_in3, tb, 512, D_MODEL, col0=4, ncols=D_MODEL,
                 outs=[_out2d(s_len, D_MODEL, BF16, tb, 512)])

    attn, lse = carrying("attn_fwd", _attn_fwd, qkn, v, bias)
    yconv = carrying("conv_fwd", _conv_fwd, conv_in, conv_w, row(conv_b))
    w_ap, w_cp, w_out, w_up3, w_down = comm.late_weights()
    tw = 1024
    ya = _mm_fwd("attn_proj", attn, w_ap, tm, tw, D_MODEL, outs=[_out2d(s_len, D_MODEL, BF16, tm, tw)])

    def merge_epi(acc, e, o):
        ya_v, ga, gc = [t.astype(F32) for t in e]
        o[0][...] = acc.astype(BF16)
        o[1][...] = (ga * ya_v + gc * acc).astype(BF16)
    gate_a, gate_c = (gates, _tile_spec(tm, tw, 0)), (gates, _tile_spec(tm, tw, 1))
    yc, merged = _mm_fwd("conv_proj", yconv, w_cp, tm, tw, D_MODEL, epilogue=merge_epi,
                         outs=[_out2d(s_len, D_MODEL, BF16, tm, tw), _out2d(s_len, D_MODEL, BF16, tm, tw)],
                         extras=[(ya, _tile_spec(tm, tw)), gate_a, gate_c])

    def res_epi(acc, e, o):
        x1_v = e[0] + acc
        o[0][...] = x1_v
        r = lax.rsqrt(jnp.mean(x1_v * x1_v, axis=-1, keepdims=True) + EPS)
        o[1][...] = (x1_v * r * e[1]).astype(BF16)
    assert tw == D_MODEL
    x1, h2 = _mm_fwd("out_proj", merged, w_out, tm, tw, D_MODEL, epilogue=res_epi,
                     outs=[_out2d(s_len, D_MODEL, F32, tm, tw), _out2d(s_len, D_MODEL, BF16, tm, tw)],
                     extras=[(x, _tile_spec(tm, tw)), (row(norm2_g), pl.BlockSpec((1, tw), lambda i, j, k: (0, 0)))])

    def up_epi(acc, e, o):
        o[0][...] = jnp.square(jnp.maximum(acc, 0.0)).astype(BF16)
    act = _mm_fwd("mlp_up", h2, w_up3, tb, tw, D_MODEL, epilogue=up_epi, outs=[_out2d(s_len, D_FF, BF16, tb, tw)])

    def loss_epi(acc, e, o):
        err = e[0] + acc - e[1]
        o[0][...] = (err * (1.0 / D_MODEL)).astype(BF16)
        sq = err * err
        part = sq[:, 0:LANE]
        for c0 in range(LANE, D_MODEL, LANE):
            part = part + sq[:, c0:c0 + LANE]
        o[1][...] = jnp.sum(part.reshape(tl // 8, 8, LANE), axis=0)
    tl = 512
    dy_b, loss_part = _mm_fwd(
        "mlp_down", act, w_down, tl, D_MODEL, D_FF, epilogue=loss_epi,
        outs=[_out2d(s_len, D_MODEL, BF16, tl, D_MODEL),
              (jax.ShapeDtypeStruct((8 * (s_len // tl), LANE), F32), pl.BlockSpec((8, LANE), lambda i, j, k: (i, 0)))],
        extras=[(x1, _tile_spec(tl, D_MODEL)), (target, _tile_spec(tl, D_MODEL))])

    def dup_epi(acc, e, o):
        o[0][...] = (acc * (2.0 * jnp.sqrt(e[0].astype(F32)))).astype(BF16)
    full = lambda cols: pl.BlockSpec((tm, cols), lambda i, j, n: (i, n))
    tokens = lambda cols: pl.BlockSpec((s_len, cols), lambda i, j, m: (m, j))
    dup = _mm_bwd_x("d_act", dy_b, full(D_MODEL), w_down, tm, tw, D_MODEL, s_len, D_MODEL, epilogue=dup_epi,
                    outs=[_out2d(s_len, D_FF, BF16, tm, tw)], extras=[(act, _tile_spec(tm, tw))])
    g_down = _mm_bwd_w("g_down", act, dy_b, tokens(D_MODEL), D_MODEL, 512, D_MODEL, s_len, False)
    g_up = _mm_bwd_w("g_up", h2, dup, tokens(512), D_FF, D_MODEL, 512, s_len, True)
    comm.grads_ready("mlp", dict(w_down=g_down, w_up=g_up))
    dx1_b, dg2 = carrying("d_h2", _d_norm_input, "d_h2", [(dup.reshape(1, s_len, D_FF), w_up3)],
                          x1, row(norm2_g), dy_b, BF16)

    def dmerge_epi(acc, e, o):
        ya_v, yc_v, ga, gc = [t.astype(F32) for t in e]
        o[0][...] = (acc * ga).astype(BF16)
        o[1][...] = (acc * gc).astype(BF16)
        o[2][0] = (acc * ya_v * ga * (1.0 - ga)).astype(BF16)
        o[2][1] = (acc * yc_v * gc * (1.0 - gc)).astype(BF16)
    dya, dyc, dgp2 = _mm_bwd_x(
        "d_merged", dx1_b, full(D_MODEL), w_out, tm, tw, D_MODEL, s_len, D_MODEL, epilogue=dmerge_epi,
        outs=[_out2d(s_len, D_MODEL, BF16, tm, tw), _out2d(s_len, D_MODEL, BF16, tm, tw),
              (jax.ShapeDtypeStruct((2, s_len, D_MODEL), BF16), pl.BlockSpec((2, tm, tw), lambda i, j, n: (0, i, j)))],
        extras=[(ya, _tile_spec(tm, tw)), (yc, _tile_spec(tm, tw)), gate_a, gate_c])
    g_out = _mm_bwd_w("g_out", merged, dx1_b, tokens(512), D_MODEL, D_MODEL, 512, s_len, False)
    d_attn = _mm_bwd_x("d_attn", dya, full(D_MODEL), w_ap, tm, tw, D_MODEL, s_len, D_MODEL,
                       outs=[_out2d(s_len, D_MODEL, BF16, tm, tw)])
    g_ap = _mm_bwd_w("g_attn_proj", attn, dya, tokens(512), D_MODEL, D_MODEL, 512, s_len, False)
    d_yconv = _mm_bwd_x("d_yconv", dyc, full(D_MODEL), w_cp, tm, tw, D_MODEL, s_len, D_MODEL,
                        outs=[_out2d(s_len, D_MODEL, BF16, tm, tw)])
    g_cp = _mm_bwd_w("g_conv_proj", yconv, dyc, tokens(512), D_MODEL, D_MODEL, 512, s_len, False)
    piece = lambda width: (lambda blk: (blk * width) // D_MODEL, lambda blk: (blk * width % D_MODEL) // width)
    pc, cb = piece(512)
    pieces = pl.BlockSpec((None, s_len, 512), lambda i, j, m: (pc(j), m, cb(j)))
    g_gate = _mm_bwd_w("g_gate", h, dgp2, pieces, 2 * D_MODEL, D_MODEL, 512, s_len, True)
    comm.grads_ready("proj", dict(w_out=g_out, w_attn_proj=g_ap, w_conv_proj=g_cp, w_gate=g_gate))

    dproj6, dbias = carrying("attn_bwd", _attn_bwd, qkn, v, attn, d_attn, bias, lse)
    dproj6, dgq, dgk = _qk_norm_bwd(dproj6, qk_raw, gq, gk)
    dproj6, dconv_wb = carrying("conv_bwd", _conv_bwd, dproj6, d_yconv, conv_in, conv_w, row(conv_b))

    g_in = carrying("g_in", _mm_bwd_w, "g_in", h, dproj6, pieces, 6 * D_MODEL, D_MODEL, 512, s_len, True)
    comm.grads_ready("in", dict(w_in=g_in))
    d_rel = carrying("bias_reduce", _bias_reduce, dbias)
    grad_x, dg1 = carrying("d_h", _d_norm_input, "d_h", [(dproj6, w_in3), (dgp2, w_gate3)],
                           x, row(norm1_g), dx1_b, F32)

    def bsum(t, f):
        return [], [t[0].astype(F32), t[1].astype(F32)]
    ts = 512
    db_a, db_c = carrying("b_gate_sum", _ew, "b_gate_sum", bsum,
                          [(dgp2, pl.BlockSpec((None, ts, D_MODEL), lambda i: (0, i, 0))),
                           (dgp2, pl.BlockSpec((None, ts, D_MODEL), lambda i: (1, i, 0)))],
                          [], [], sums=[D_MODEL, D_MODEL], ts=ts)

    big = dict(w_in=g_in, w_attn_proj=g_ap, w_conv_proj=g_cp, w_gate=g_gate, w_out=g_out,
               w_up=g_up, w_down=g_down)
    small = dict(norm1_g=dg1, norm2_g=dg2, conv_wb=dconv_wb, b_gate=(db_a, db_c),
                 q_norm_g=dgq, k_norm_g=dgk, rel_bias=d_rel)
    return loss_part, grad_x, big, small


def _finish_loss(loss_part):
    def body(l_ref, lo_ref):
        total = jnp.sum(jnp.sum(l_ref[...], axis=0, keepdims=True), axis=1, keepdims=True)
        lo_ref[...] = jnp.broadcast_to(total * (0.5 / D_MODEL), lo_ref.shape)

    return pl.pallas_call(body, name="finish_loss", out_shape=jax.ShapeDtypeStruct((8, LANE), F32))(loss_part)


def _place():
    return lax.axis_index("x"), lax.axis_index("y"), lax.axis_index("c")


def _other_chips(x, y):
    return [(1 - x, y), (x, 1 - y), (1 - x, 1 - y)]


def _cast_into_slot(name, where, w):
    r, cols = w.shape
    ts = 256

    def body(w_ref, x_ref, o_ref):
        o_ref[...] = x_ref[...].astype(o_ref.dtype)

    return pl.pallas_call(
        body, name=name,
        grid_spec=pltpu.PrefetchScalarGridSpec(
            num_scalar_prefetch=1, grid=(r // ts,),
            in_specs=[pl.BlockSpec((ts, cols), lambda i, w: (i, 0))],
            out_specs=pl.BlockSpec((None, ts, cols), lambda i, w: (w[0], i, 0))),
        out_shape=jax.ShapeDtypeStruct((N_SHARD, r, cols), BF16),
        compiler_params=_params("parallel"),
    )(where, w)


def _remote(src, dst, send, recv, k, to):
    return pltpu.make_async_remote_copy(src_ref=src, dst_ref=dst, send_sem=send.at[k], recv_sem=recv.at[k],
                                        device_id=to, device_id_type=MESH)


def _gather_riders(slots, relay):
    n = len(slots)
    near, per = (2, 5) if relay else (3, 3)

    def rows(w, which, part=None):
        hr = slots[w].shape[1] // 2
        if part is None:
            return pl.ds(which * hr, hr)
        if hr % 32:
            return pl.ds(which * hr, hr) if part == 0 else None
        return pl.ds(which * hr + part * (hr // 2), hr // 2)

    def copy(refs, w, k, shard, span, send, recv, to):
        ref = refs[w].at[shard, span]
        return _remote(ref, ref, send, recv, k, to)

    def places():
        x, y, c = _place()
        chips = _other_chips(x, y)
        return c, 2 * x + y, [2 * ch[0] + ch[1] for ch in chips], [(*ch, c) for ch in chips], (x, y, 1 - c)

    def chips_copies(refs, send, recv, sending):
        c, mine, theirs, peers, sib = places()
        return [copy(refs, w, near * w + j, mine if sending else theirs[j], rows(w, c), send, recv, peers[j])
                for w in range(n) for j in range(near)]

    def pass_copies(refs, send, recv, sending):
        c, mine, theirs, peers, sib = places()
        out = []
        for w in range(n):
            for j in range(near):
                out.append(copy(refs, w, per * w + j, theirs[j], rows(w, c if sending else 1 - c), send, recv, sib))
            for link in range(2 if relay else 0):
                span = rows(w, c, part=1 - link)
                if span is not None:
                    out.append(copy(refs, w, 5 * w + 2 + link, theirs[1 - link] if sending else theirs[2], span,
                                    send, recv, peers[link]))
        return out

    def last_copies(refs, send, recv, sending):
        c, mine, theirs, peers, sib = places()
        return [copy(refs, w, 5 * w + 4, theirs[2], rows(w, c if sending else 1 - c), send, recv, sib)
                for w in range(n if relay else 0)]

    def chips_start(_, refs, send, recv):
        for d in chips_copies(refs, send, recv, True):
            d.start()

    def chips_finish(_, refs, send, recv):
        for d in chips_copies(refs, send, recv, False):
            d.wait_recv()
        for d in chips_copies(refs, send, recv, True):
            d.wait_send()

    def sibling_start(_, refs, send, recv):
        for d in pass_copies(refs, send, recv, True):
            d.start()

    def sibling_finish(_, refs, send, recv):
        for d in pass_copies(refs, send, recv, False):
            d.wait_recv()
        for d in last_copies(refs, send, recv, True):
            d.start()
        for d in pass_copies(refs, send, recv, True):
            d.wait_send()
        for d in last_copies(refs, send, recv, False):
            d.wait_recv()
        for d in last_copies(refs, send, recv, True):
            d.wait_send()

    return (lambda arrays: _Rider([], arrays, near * n, chips_start, chips_finish),
            lambda arrays: _Rider([], arrays, per * n, sibling_start, sibling_finish))


def _pair_exchange_rider(grads, landing):
    n = len(grads)

    def copies(srcs, dsts, send, recv):
        x, y, c = _place()
        out = []
        for w in range(n):
            hr = grads[w].shape[1] // 2
            out.append(_remote(srcs[w].at[:, pl.ds((1 - c) * hr, hr)], dsts[w], send, recv, w, (x, y, 1 - c)))
        return out

    def start(srcs, dsts, send, recv):
        for cp in copies(srcs, dsts, send, recv):
            cp.start()

    def finish(srcs, dsts, send, recv):
        for cp in copies(srcs, dsts, send, recv):
            cp.wait()

    return _Rider(grads, landing, n, start, finish)


def _row_tile(hr):
    return min(hr, 256)


def _pair_add(name, where, grad, got):
    _, hr, cols = got.shape
    tr = _row_tile(hr)
    nblk = hr // tr

    def body(w_ref, g_ref, r_ref, o_ref):
        o_ref[...] = (g_ref[...] + r_ref[...]).astype(o_ref.dtype)

    other = lambda k, w: (w[0] + 1 + k) % N_SHARD
    return pl.pallas_call(
        body, name=name,
        grid_spec=pltpu.PrefetchScalarGridSpec(
            num_scalar_prefetch=1, grid=(N_SHARD - 1, nblk),
            in_specs=[pl.BlockSpec((None, tr, cols), lambda k, i, w: (other(k, w), w[1] * nblk + i, 0)),
                      pl.BlockSpec((None, tr, cols), lambda k, i, w: (other(k, w), i, 0))],
            out_specs=pl.BlockSpec((None, tr, cols), lambda k, i, w: (other(k, w), i, 0))),
        out_shape=jax.ShapeDtypeStruct(got.shape, BF16),
        compiler_params=_params("parallel", "parallel"),
    )(where, grad, got)


def _chip_exchange_rider(partials, landing):
    n = len(partials)

    def copies(srcs, dsts, send, recv):
        x, y, c = _place()
        return [_remote(srcs[w].at[2 * chip[0] + chip[1]], dsts[w].at[j], send, recv, 3 * w + j, (*chip, c))
                for w in range(n) for j, chip in enumerate(_other_chips(x, y))]

    def start(srcs, dsts, send, recv):
        for cp in copies(srcs, dsts, send, recv):
            cp.start()

    def finish(srcs, dsts, send, recv):
        for cp in copies(srcs, dsts, send, recv):
            cp.wait()

    return _Rider(partials, landing, 3 * n, start, finish)


def _final_add(name, where, grad, got, arrived):
    _, hr, cols = got.shape
    tr = _row_tile(hr)
    nblk = hr // tr

    def body(w_ref, g_ref, r_ref, a_ref, o_ref):
        acc = g_ref[...] + r_ref[...]
        for j in range(3):
            acc = acc + a_ref[j].astype(F32)
        o_ref[...] = acc

    return pl.pallas_call(
        body, name=name,
        grid_spec=pltpu.PrefetchScalarGridSpec(
            num_scalar_prefetch=1, grid=(nblk,),
            in_specs=[pl.BlockSpec((None, tr, cols), lambda i, w: (w[0], w[1] * nblk + i, 0)),
                      pl.BlockSpec((None, tr, cols), lambda i, w: (w[0], i, 0)),
                      pl.BlockSpec((3, tr, cols), lambda i, w: (0, i, 0))],
            out_specs=pl.BlockSpec((tr, cols), lambda i, w: (w[1] * nblk + i, 0))),
        out_shape=jax.ShapeDtypeStruct((2 * hr, cols), F32),
        compiler_params=_params("parallel"),
    )(where, grad, got, arrived)


def _pair_share_rider(shards):
    n = len(shards)

    def half(refs, w, which):
        hr = shards[w].shape[0] // 2
        return refs[w].at[pl.ds(which * hr, hr)]

    def start(_, refs, send, recv):
        x, y, c = _place()
        for w in range(n):
            _remote(half(refs, w, c), half(refs, w, c), send, recv, w, (x, y, 1 - c)).start()

    def finish(_, refs, send, recv):
        x, y, c = _place()
        for w in range(n):
            _remote(half(refs, w, 1 - c), half(refs, w, 1 - c), send, recv, w, (x, y, 1 - c)).wait_recv()
        for w in range(n):
            _remote(half(refs, w, c), half(refs, w, c), send, recv, w, (x, y, 1 - c)).wait_send()

    return _Rider([], shards, n, start, finish)


class _Exchange:
    PLAN = {"bias_expand": [("gate", "gather"), ("early", "gather")], "norm1": [("gate", "forward")],
            "gates": [("early", "forward")],
            "proj_conv": [("late", "gather")], "proj_qk": [("late", "forward")],
            "attn_fwd": [("mlp_w", "gather")], "conv_fwd": [("mlp_w", "forward")],
            "d_h2": [("mlp", "pair")], "attn_bwd": [("mlp", "chips"), ("proj", "pair")], "conv_bwd": [("mlp", "share")],
            "g_in": [("proj", "chips")], "bias_reduce": [("proj", "share"), ("in", "pair")],
            "d_h": [("in", "chips")], "b_gate_sum": [("in", "share")]}

    def __init__(self, where, early_slots, gate_slots, late_slots):
        self.where = where
        self.slots = dict(early=early_slots, gate=gate_slots, late=late_slots[:3], mlp_w=late_slots[3:])
        self.stage = {g: dict(zip(("gather", "forward"), _gather_riders(s, relay=g in ("early", "gate")))) for g, s in self.slots.items()}
        self.groups, self.reduced, self.pending = {}, {}, []

    def early_weights(self):
        w_in3, small = self.slots["early"]
        conv_w = small[:, :3, :].transpose(1, 0, 2).reshape(3, N_SHARD * small.shape[2])
        return w_in3, conv_w

    def gate_weight(self):
        return self.slots["gate"][0]

    def late_weights(self):
        rows = lambda a: a.reshape(a.shape[0] * a.shape[1], a.shape[2])
        w_ap, w_cp, w_out = self.slots["late"]
        w_up3, w_down = self.slots["mlp_w"]
        return rows(w_ap), rows(w_cp), rows(w_out), w_up3, rows(w_down)

    def grads_ready(self, group, grads):
        names = list(grads)
        g4 = [g if g.ndim == 3 else g.reshape(N_SHARD, -1, g.shape[1]) for g in grads.values()]
        self.groups[group] = dict(names=names, g4=g4)

    def riders(self, at):
        self.pending = self.PLAN.get(at, [])
        out = []
        for group, stage in self.pending:
            if group in self.slots:
                out.append(self.stage[group][stage](self.slots[group]))
                continue
            st = self.groups[group]
            if stage == "pair":
                landing = [lax.empty((N_SHARD, g.shape[1] // 2, g.shape[2]), F32) for g in st["g4"]]
                out.append(_pair_exchange_rider(st["g4"], landing))
            elif stage == "chips":
                landing = [lax.empty((3,) + p.shape[1:], p.dtype) for p in st["partial"]]
                out.append(_chip_exchange_rider(st["partial"], landing))
            else:
                out.append(_pair_share_rider(st["halves"]))
        return out

    def done(self, at, carried):
        for (group, stage), arrays in zip(self.pending, carried):
            if group in self.slots:
                self.slots[group] = arrays
                continue
            st = self.groups[group]
            tag = lambda what, n: what + "_" + n
            if stage == "pair":
                st["got"] = arrays
                st["partial"] = [_pair_add(tag("pair_add", n), self.where, g, r)
                                 for n, g, r in zip(st["names"], st["g4"], arrays)]
            elif stage == "chips":
                st["halves"] = [_final_add(tag("final_add", n), self.where, g, r, a)
                                for n, g, r, a in zip(st["names"], st["g4"], st["got"], arrays)]
            else:
                self.reduced.update(zip(st["names"], arrays))


SMALL_ROWS = 32
N_DEV = 8


def _all_reduce_small(pack):
    def body(p_ref, o_ref, buf, send, recv):
        x, y, c = _place()
        buf[4 * x + 2 * y + c] = p_ref[...]
        copies, waits = [], []
        for k in range(1, N_DEV):
            px = 1 - x if k & 4 else x
            py = 1 - y if k & 2 else y
            pc = 1 - c if k & 1 else c
            copies.append(_remote(p_ref, buf.at[4 * x + 2 * y + c], send, recv, k - 1, (px, py, pc)))
            waits.append(_remote(p_ref, buf.at[4 * px + 2 * py + pc], send, recv, k - 1, (px, py, pc)))
        for cp in copies:
            cp.start()
        for cp in waits:
            cp.wait_recv()
        acc = buf[0]
        for d in range(1, N_DEV):
            acc = acc + buf[d]
        o_ref[...] = acc
        for cp in copies:
            cp.wait_send()

    return pl.pallas_call(
        body, name="all_reduce_small",
        out_shape=jax.ShapeDtypeStruct(pack.shape, F32),
        scratch_shapes=[pltpu.VMEM((N_DEV,) + pack.shape, F32),
                        pltpu.SemaphoreType.DMA((N_DEV - 1,)), pltpu.SemaphoreType.DMA((N_DEV - 1,))],
    )(pack)


def _adamw(name, w, g, m, v):
    c1 = 1.0 - ADAM_B1 ** ADAM_STEP
    c2 = 1.0 - ADAM_B2 ** ADAM_STEP

    def fn(t, f):
        wv, gv, mv, vv = t
        m2 = ADAM_B1 * mv + (1.0 - ADAM_B1) * gv
        v2 = ADAM_B2 * vv + (1.0 - ADAM_B2) * (gv * gv)
        delta = -ADAM_LR * ((m2 / c1) / (jnp.sqrt(v2 / c2) + ADAM_EPS) + ADAM_WD * wv)
        return [delta, m2, v2], []

    cols = w.shape[1]
    return _ew(name, fn, [w, g, m, v], [], [(cols, F32)] * 3, ts=min(w.shape[0], 256))


LOSS_ROW = 26


def _pack_small(norm1_g, norm2_g, conv_b, b_gate, conv_w, q_norm_g, k_norm_g, rel_bias, loss=None):
    pack = jnp.zeros((SMALL_ROWS, D_MODEL), F32)
    for r0, v in ((0, norm1_g), (1, norm2_g), (2, conv_b), (3, b_gate.reshape(2, D_MODEL)), (5, conv_w),
                  (8, q_norm_g), (9, k_norm_g), (10, rel_bias)) + (((LOSS_ROW, loss),) if loss is not None else ()):
        v = v.reshape(-1, v.shape[-1]).astype(F32)
        pack = pack.at[r0:r0 + v.shape[0], :v.shape[1]].set(v)
    return pack


def _unpack_small(pack, conv_cols):
    return dict(norm1_g=pack[0], norm2_g=pack[1], conv_b=pack[2], b_gate=pack[3:5].reshape(2 * D_MODEL),
                conv_w=pack[5:8, :conv_cols], q_norm_g=pack[8, :HEAD_DIM], k_norm_g=pack[9, :HEAD_DIM],
                rel_bias=pack[10:10 + N_HEADS, :N_REL])


BIG = ["w_in", "w_attn_proj", "w_conv_proj", "w_gate", "w_out", "w_up", "w_down"]
LATE = ["w_attn_proj", "w_conv_proj", "w_out", "w_up", "w_down"]
WEIGHTS = ["norm1_g", "w_in", "q_norm_g", "k_norm_g", "rel_bias", "conv_w", "conv_b", "w_attn_proj",
           "w_conv_proj", "w_gate", "b_gate", "w_out", "norm2_g", "w_up", "w_down"]


def kernel(x, norm1_g, w_in, q_norm_g, k_norm_g, rel_bias, conv_w, conv_b, w_attn_proj, w_conv_proj, w_gate, b_gate, w_out, norm2_g, w_up, w_down, loss_target, m_norm1_g, m_w_in, m_q_norm_g, m_k_norm_g, m_rel_bias, m_conv_w, m_conv_b, m_w_attn_proj, m_w_conv_proj, m_w_gate, m_b_gate, m_w_out, m_norm2_g, m_w_up, m_w_down, v_norm1_g, v_w_in, v_q_norm_g, v_k_norm_g, v_rel_bias, v_conv_w, v_conv_b, v_w_attn_proj, v_w_conv_proj, v_w_gate, v_b_gate, v_w_out, v_norm2_g, v_w_up, v_w_down):
    given = dict(locals())
    w = {n: given[n] for n in WEIGHTS}
    m = {n: given["m_" + n] for n in WEIGHTS}
    v = {n: given["v_" + n] for n in WEIGHTS}
    s_len = x.shape[1]
    shard = 2 * lax.axis_index("x") + lax.axis_index("y")
    where = jnp.stack([shard, lax.axis_index("c")]).astype(jnp.int32)
    conv_cols = conv_w.shape[1]

    small_in = lax.dynamic_update_slice(jnp.zeros((N_SHARD, 16, conv_cols), F32), conv_w[None], (shard, 0, 0))
    slot = {n: _cast_into_slot("cast_" + n, where, w[n]) for n in BIG}
    comm = _Exchange(where, [slot["w_in"], small_in], [slot["w_gate"]], [slot[n] for n in LATE])

    loss_part, grad_x, _, small = _local_grads(
        x.reshape(s_len, D_MODEL), loss_target.reshape(s_len, D_MODEL), norm1_g, q_norm_g, k_norm_g,
        rel_bias, conv_b, b_gate, norm2_g, comm)
    grad = dict(comm.reduced)

    loss_local = _finish_loss(loss_part)
    pack = _pack_small(small["norm1_g"], small["norm2_g"], small["conv_wb"][3], jnp.concatenate(small["b_gate"], axis=1),
                       small["conv_wb"][0:3], small["q_norm_g"], small["k_norm_g"], small["rel_bias"],
                       loss=loss_local[0:1, :])
    total = _all_reduce_small(pack)
    g_small = _unpack_small(total, D_MODEL)
    g_small["conv_w"] = lax.dynamic_slice(g_small["conv_w"], (0, shard * conv_cols), (3, conv_cols))
    grad.update(g_small)

    delta, new_m, new_v = {}, {}, {}
    for n in BIG:
        delta[n], new_m[n], new_v[n] = _adamw("adamw_" + n, w[n], grad[n], m[n], v[n])
    small_names = [n for n in WEIGHTS if n not in BIG]
    packs = [_pack_small(**{n: src[n] for n in small_names}) for src in (w, grad, m, v)]
    for out, packed in zip((delta, new_m, new_v), _adamw("adamw_small", *packs)):
        out.update({n: a.reshape(w[n].shape) for n, a in _unpack_small(packed, conv_cols).items()})

    outs = [total[LOSS_ROW, 0], grad_x.reshape(x.shape)]
    for group in (grad, delta, new_m, new_v):
        outs += [group[n].reshape(w[n].shape) for n in WEIGHTS]
    return tuple(outs)
```
